```python
import jax, jax.numpy as jnp
from jax import lax
import numpy as np

D_MODEL = 1024
BATCH = 8
SEQ = 4096
DEPTH = 1

MEM_LEN = 256
EPS = 1e-6
CONV_WIDTH = 3
A_WIDTH = D_MODEL // 2
HEAD_DIM = 64
ATTN_WIDTH = D_MODEL // 2
N_Q_HEADS = ATTN_WIDTH // HEAD_DIM
N_KV_HEADS = N_Q_HEADS // 4
KV_WIDTH = N_KV_HEADS * HEAD_DIM
WINDOW = 128
BLOCK = 128
ROPE_THETA = 500000.0
ROT_DIM = HEAD_DIM // 4
MEM_HEADS = 4
MEM_HEAD_DIM = D_MODEL // 8
MEM_WIDTH = MEM_HEADS * MEM_HEAD_DIM
N_BRANCHES = 3
IN_SIZES = (A_WIDTH, A_WIDTH, A_WIDTH, A_WIDTH,
            ATTN_WIDTH, KV_WIDTH, KV_WIDTH, ATTN_WIDTH,
            MEM_WIDTH, MEM_WIDTH,
            N_BRANCHES * D_MODEL)
IN_WIDTH = sum(IN_SIZES)

kernel_name = "hybrid_gated_conv_swa_memxattn_block"


def rms_norm(x, g):
    xf = x.astype(jnp.float32)
    y = xf * lax.rsqrt(jnp.mean(xf * xf, axis=-1, keepdims=True) + EPS)
    return (y * g.astype(jnp.float32)).astype(x.dtype)


def partial_rope(t, pos):
    half = ROT_DIM // 2
    inv_freq = jnp.power(jnp.float32(ROPE_THETA), -jnp.arange(half, dtype=jnp.float32) * (2.0 / ROT_DIM))
    ang = pos.astype(jnp.float32)[:, None] * inv_freq[None, :]
    cos = jnp.cos(ang)[None, :, None, :]
    sin = jnp.sin(ang)[None, :, None, :]
    tr = t[..., :ROT_DIM].astype(jnp.float32)
    t1, t2 = tr[..., :half], tr[..., half:]
    rot = jnp.concatenate([t1 * cos - t2 * sin, t2 * cos + t1 * sin], axis=-1).astype(t.dtype)
    return jnp.concatenate([rot, t[..., ROT_DIM:]], axis=-1)


def short_gated_conv(b, c, u, w_conv):
    s = u.shape[1]
    cu = jnp.pad(c * u, ((0, 0), (1, 1), (0, 0)))
    y = cu[:, :s] * w_conv[0] + cu[:, 1:s + 1] * w_conv[1] + cu[:, 2:] * w_conv[2]
    return b * y


def window_attention_with_sink(q, k, v, sink):
    bsz, s, hq, dh = q.shape
    hkv = k.shape[2]
    grp = hq // hkv
    nb = s // BLOCK
    qb = q.reshape(bsz, nb, BLOCK, hkv, grp, dh)
    pad = ((0, 0), (BLOCK, BLOCK), (0, 0), (0, 0))
    kp = jnp.pad(k, pad).reshape(bsz, nb + 2, BLOCK, hkv, dh)
    vp = jnp.pad(v, pad).reshape(bsz, nb + 2, BLOCK, hkv, dh)
    kw = jnp.concatenate([kp[:, :nb], kp[:, 1:nb + 1], kp[:, 2:]], axis=2)
    vw = jnp.concatenate([vp[:, :nb], vp[:, 1:nb + 1], vp[:, 2:]], axis=2)
    qpos = jnp.arange(nb)[:, None] * BLOCK + jnp.arange(BLOCK)[None, :]
    kpos = (jnp.arange(nb)[:, None] - 1) * BLOCK + jnp.arange(3 * BLOCK)[None, :]
    valid = ((jnp.abs(qpos[:, :, None] - kpos[:, None, :]) <= WINDOW)
             & (kpos >= 0)[:, None, :] & (kpos < s)[:, None, :])
    scores = jnp.einsum('bnqhgd,bnkhd->bnhgqk', qb, kw,
                        preferred_element_type=jnp.float32) * (dh ** -0.5)
    scores = jnp.where(valid[None, :, None, None], scores, -jnp.inf)
    sink_l = sink.astype(jnp.float32).reshape(hkv, grp)[None, None, :, :, None, None]
    m = jnp.maximum(jnp.max(scores, axis=-1, keepdims=True), sink_l)
    p = jnp.exp(scores - m)
    p = p / (jnp.sum(p, axis=-1, keepdims=True) + jnp.exp(sink_l - m))
    out = jnp.einsum('bnhgqk,bnkhd->bnqhgd', p.astype(v.dtype), vw)
    return out.reshape(bsz, s, hq * dh)


def memory_cross_attention(q, mk, mv):
    dm = q.shape[-1]
    scores = jnp.einsum('bshd,bmhd->bhsm', q, mk,
                        preferred_element_type=jnp.float32) * (dm ** -0.5)
    p = jax.nn.softmax(scores, axis=-1)
    out = jnp.einsum('bhsm,bmhd->bshd', p.astype(mv.dtype), mv)
    return out.reshape(q.shape[0], q.shape[1], -1)


def hybrid_layer(x, mem, g_pre, w_in, w_conv, attn_sink, g_mem, w_mem_kv,
                 w_up_a, w_up_b, w_up_m, w_out, g_post):
    bsz, s, d = x.shape
    h = rms_norm(x, g_pre)
    proj = h @ w_in
    idx = list(np.cumsum(IN_SIZES)[:-1])
    (a_b, a_c, a_x, a_z, b_q, b_k, b_v, b_z, m_q, m_z, gate_logits) = jnp.split(proj, idx, axis=-1)

    ya = short_gated_conv(a_b, a_c, a_x, w_conv) * jax.nn.silu(a_z)
    ua = ya @ w_up_a

    pos = jnp.arange(s)
    q = partial_rope(b_q.reshape(bsz, s, N_Q_HEADS, HEAD_DIM), pos)
    k = partial_rope(b_k.reshape(bsz, s, N_KV_HEADS, HEAD_DIM), pos)
    v = b_v.reshape(bsz, s, N_KV_HEADS, HEAD_DIM)
    yb = window_attention_with_sink(q, k, v, attn_sink) * jax.nn.silu(b_z)
    ub = yb @ w_up_b

    mn = rms_norm(mem, g_mem)
    mkv = mn @ w_mem_kv
    mk, mv = jnp.split(mkv, 2, axis=-1)
    mlen = mem.shape[1]
    ym = memory_cross_attention(m_q.reshape(bsz, s, MEM_HEADS, MEM_HEAD_DIM),
                                mk.reshape(bsz, mlen, MEM_HEADS, MEM_HEAD_DIM),
                                mv.reshape(bsz, mlen, MEM_HEADS, MEM_HEAD_DIM)) * jax.nn.silu(m_z)
    um = ym @ w_up_m

    gates = jax.nn.sigmoid(gate_logits.astype(jnp.float32)).astype(x.dtype).reshape(bsz, s, N_BRANCHES, d)
    merged = gates[:, :, 0] * ua + gates[:, :, 1] * ub + gates[:, :, 2] * um
    out = merged @ w_out
    return x + rms_norm(out, g_post)


def _fwd_setup_inputs(seed: int = 0) -> dict:
    key = jax.random.key(seed)
    ks = jax.random.split(key, 14)
    f32 = jnp.float32
    nrm = lambda k, shape, fan_in: jax.random.normal(k, shape, f32) * (fan_in ** -0.5)
    gain = lambda k, shape: 1.0 + 0.05 * jax.random.normal(k, shape, f32)
    return {
        "x": jax.random.normal(ks[0], (BATCH, SEQ, D_MODEL), f32),
        "mem": jax.random.normal(ks[1], (BATCH, MEM_LEN, D_MODEL), f32),
        "g_pre": gain(ks[2], (DEPTH, D_MODEL)),
        "w_in": nrm(ks[3], (DEPTH, D_MODEL, IN_WIDTH), D_MODEL),
        "w_conv": nrm(ks[4], (DEPTH, CONV_WIDTH, A_WIDTH), CONV_WIDTH),
        "attn_sink": 0.5 * jax.random.normal(ks[5], (DEPTH, N_Q_HEADS), f32),
        "g_mem": gain(ks[6], (DEPTH, D_MODEL)),
        "w_mem_kv": nrm(ks[7], (DEPTH, D_MODEL, 2 * MEM_WIDTH), D_MODEL),
        "w_up_a": nrm(ks[8], (DEPTH, A_WIDTH, D_MODEL), A_WIDTH),
        "w_up_b": nrm(ks[9], (DEPTH, ATTN_WIDTH, D_MODEL), ATTN_WIDTH),
        "w_up_m": nrm(ks[10], (DEPTH, MEM_WIDTH, D_MODEL), MEM_WIDTH),
        "w_out": nrm(ks[11], (DEPTH, D_MODEL, D_MODEL), D_MODEL),
        "g_post": gain(ks[12], (DEPTH, D_MODEL)),
    }


def _fwd_reference(x, mem, g_pre, w_in, w_conv, attn_sink, g_mem, w_mem_kv,
              w_up_a, w_up_b, w_up_m, w_out, g_post):
    for l in range(DEPTH):
        x = hybrid_layer(x, mem, g_pre[l], w_in[l], w_conv[l], attn_sink[l], g_mem[l],
                         w_mem_kv[l], w_up_a[l], w_up_b[l], w_up_m[l], w_out[l], g_post[l])
    return x


import jax as _jax
import jax.numpy as _jnp

TWIN_FORMAT = 'train_step'
FWD_PARAMS = ['x', 'mem', 'g_pre', 'w_in', 'w_conv', 'attn_sink', 'g_mem', 'w_mem_kv', 'w_up_a', 'w_up_b', 'w_up_m', 'w_out', 'g_post']
TWIN_WEIGHTS = ['g_pre', 'w_in', 'w_conv', 'attn_sink', 'g_mem', 'w_mem_kv', 'w_up_a', 'w_up_b', 'w_up_m', 'w_out', 'g_post']
TWIN_DIFF_INPUT = 'x'
TWIN_INPUTS = ['x', 'mem', 'g_pre', 'w_in', 'w_conv', 'attn_sink', 'g_mem', 'w_mem_kv', 'w_up_a', 'w_up_b', 'w_up_m', 'w_out', 'g_post', 'loss_target', 'm_g_pre', 'm_w_in', 'm_w_conv', 'm_attn_sink', 'm_g_mem', 'm_w_mem_kv', 'm_w_up_a', 'm_w_up_b', 'm_w_up_m', 'm_w_out', 'm_g_post', 'v_g_pre', 'v_w_in', 'v_w_conv', 'v_attn_sink', 'v_g_mem', 'v_w_mem_kv', 'v_w_up_a', 'v_w_up_b', 'v_w_up_m', 'v_w_out', 'v_g_post']
TWIN_OUTPUTS = ['loss', 'grad_x', 'grad_g_pre', 'grad_w_in', 'grad_w_conv', 'grad_attn_sink', 'grad_g_mem', 'grad_w_mem_kv', 'grad_w_up_a', 'grad_w_up_b', 'grad_w_up_m', 'grad_w_out', 'grad_g_post', 'delta_g_pre', 'delta_w_in', 'delta_w_conv', 'delta_attn_sink', 'delta_g_mem', 'delta_w_mem_kv', 'delta_w_up_a', 'delta_w_up_b', 'delta_w_up_m', 'delta_w_out', 'delta_g_post', 'new_m_g_pre', 'new_m_w_in', 'new_m_w_conv', 'new_m_attn_sink', 'new_m_g_mem', 'new_m_w_mem_kv', 'new_m_w_up_a', 'new_m_w_up_b', 'new_m_w_up_m', 'new_m_w_out', 'new_m_g_post', 'new_v_g_pre', 'new_v_w_in', 'new_v_w_conv', 'new_v_attn_sink', 'new_v_g_mem', 'new_v_w_mem_kv', 'new_v_w_up_a', 'new_v_w_up_b', 'new_v_w_up_m', 'new_v_w_out', 'new_v_g_post']
TWIN_LEAF_KINDS = {'loss': 'loss', 'grad_x': 'grad_x', 'grad_g_pre': 'grad_w', 'grad_w_in': 'grad_w', 'grad_w_conv': 'grad_w', 'grad_attn_sink': 'grad_w', 'grad_g_mem': 'grad_w', 'grad_w_mem_kv': 'grad_w', 'grad_w_up_a': 'grad_w', 'grad_w_up_b': 'grad_w', 'grad_w_up_m': 'grad_w', 'grad_w_out': 'grad_w', 'grad_g_post': 'grad_w', 'delta_g_pre': 'delta_w', 'delta_w_in': 'delta_w', 'delta_w_conv': 'delta_w', 'delta_attn_sink': 'delta_w', 'delta_g_mem': 'delta_w', 'delta_w_mem_kv': 'delta_w', 'delta_w_up_a': 'delta_w', 'delta_w_up_b': 'delta_w', 'delta_w_up_m': 'delta_w', 'delta_w_out': 'delta_w', 'delta_g_post': 'delta_w', 'new_m_g_pre': 'new_m', 'new_m_w_in': 'new_m', 'new_m_w_conv': 'new_m', 'new_m_attn_sink': 'new_m', 'new_m_g_mem': 'new_m', 'new_m_w_mem_kv': 'new_m', 'new_m_w_up_a': 'new_m', 'new_m_w_up_b': 'new_m', 'new_m_w_up_m': 'new_m', 'new_m_w_out': 'new_m', 'new_m_g_post': 'new_m', 'new_v_g_pre': 'new_v', 'new_v_w_in': 'new_v', 'new_v_w_conv': 'new_v', 'new_v_attn_sink': 'new_v', 'new_v_g_mem': 'new_v', 'new_v_w_mem_kv': 'new_v', 'new_v_w_up_a': 'new_v', 'new_v_w_up_b': 'new_v', 'new_v_w_up_m': 'new_v', 'new_v_w_out': 'new_v', 'new_v_g_post': 'new_v'}


def _forward(args):
    return _fwd_reference(*[args[k] for k in FWD_PARAMS])


def _output_shape():
    def fwd():
        inp = _fwd_setup_inputs(0)
        return _fwd_reference(*[inp[k] for k in FWD_PARAMS])
    out = _jax.eval_shape(fwd)
    return out.shape, out.dtype

N_MICROBATCH = 1
ADAM_LR = 0.001
ADAM_B1 = 0.9
ADAM_B2 = 0.999
ADAM_EPS = 1e-08
ADAM_WD = 0.01
ADAM_STEP = 10
PER_EXAMPLE_BATCH_AXIS = {'x': 0, 'mem': 0, 'loss_target': 0}
SHARED_INPUTS = []
_WEIGHT_DTYPES = {'g_pre': _jnp.float32, 'w_in': _jnp.float32, 'w_conv': _jnp.float32, 'attn_sink': _jnp.float32, 'g_mem': _jnp.float32, 'w_mem_kv': _jnp.float32, 'w_up_a': _jnp.float32, 'w_up_b': _jnp.float32, 'w_up_m': _jnp.float32, 'w_out': _jnp.float32, 'g_post': _jnp.float32}
MOMENT_SCALE = {'g_pre': 5.347579e-01, 'w_in': 1.981615e-01, 'w_conv': 3.581648e-01, 'attn_sink': 2.373334e-03, 'g_mem': 4.750032e-02, 'w_mem_kv': 4.615805e-02, 'w_up_a': 2.677195e-01, 'w_up_b': 3.537883e-02, 'w_up_m': 3.431516e-02, 'w_out': 2.780056e-01, 'g_post': 3.221153e+01}


def _to_microbatches(a, axis):
    t = _jnp.moveaxis(a, axis, 0)
    t = t.reshape((N_MICROBATCH, t.shape[0] // N_MICROBATCH) + t.shape[1:])
    return _jnp.moveaxis(t, 1, axis + 1)


def setup_inputs(seed: int = 0) -> dict:
    inp = _fwd_setup_inputs(seed)
    key = _jax.random.fold_in(_jax.random.key(seed), 7919)
    shape, _ = _output_shape()
    out = dict(inp)
    out["loss_target"] = _jax.random.normal(_jax.random.fold_in(key, 0), shape, _jnp.float32)
    for i, name in enumerate(TWIN_WEIGHTS):
        w = inp[name].astype(_jnp.float32)
        if MOMENT_SCALE is None:
            s = _jnp.sqrt(_jnp.mean(_jnp.square(w)) + 1e-30)
        else:
            s = MOMENT_SCALE[name]
        km, kv = _jax.random.split(_jax.random.fold_in(key, i + 1))
        out[name] = w
        out["m_" + name] = s * _jax.random.normal(km, w.shape, _jnp.float32)
        out["v_" + name] = (s * s) * _jax.random.uniform(kv, w.shape, _jnp.float32, 0.5, 1.5)
    if N_MICROBATCH > 1:
        for name, axis in PER_EXAMPLE_BATCH_AXIS.items():
            out[name] = _to_microbatches(out[name], axis)
    return {'x': out['x'], 'mem': out['mem'], 'g_pre': out['g_pre'], 'w_in': out['w_in'], 'w_conv': out['w_conv'], 'attn_sink': out['attn_sink'], 'g_mem': out['g_mem'], 'w_mem_kv': out['w_mem_kv'], 'w_up_a': out['w_up_a'], 'w_up_b': out['w_up_b'], 'w_up_m': out['w_up_m'], 'w_out': out['w_out'], 'g_post': out['g_post'], 'loss_target': out['loss_target'], 'm_g_pre': out['m_g_pre'], 'm_w_in': out['m_w_in'], 'm_w_conv': out['m_w_conv'], 'm_attn_sink': out['m_attn_sink'], 'm_g_mem': out['m_g_mem'], 'm_w_mem_kv': out['m_w_mem_kv'], 'm_w_up_a': out['m_w_up_a'], 'm_w_up_b': out['m_w_up_b'], 'm_w_up_m': out['m_w_up_m'], 'm_w_out': out['m_w_out'], 'm_g_post': out['m_g_post'], 'v_g_pre': out['v_g_pre'], 'v_w_in': out['v_w_in'], 'v_w_conv': out['v_w_conv'], 'v_attn_sink': out['v_attn_sink'], 'v_g_mem': out['v_g_mem'], 'v_w_mem_kv': out['v_w_mem_kv'], 'v_w_up_a': out['v_w_up_a'], 'v_w_up_b': out['v_w_up_b'], 'v_w_up_m': out['v_w_up_m'], 'v_w_out': out['v_w_out'], 'v_g_post': out['v_g_post']}


def _loss(weights, diff, rest, loss_target):
    with _jax.named_scope("forward"):
        args = {**rest, TWIN_DIFF_INPUT: diff, **{k: w.astype(_WEIGHT_DTYPES[k]) for k, w in weights.items()}}
        y = _forward(args)
    with _jax.named_scope("loss_head"):
        err = _jnp.square(y.astype(_jnp.float32) - loss_target)
        return 0.5 * _jnp.sum(_jnp.mean(err, axis=-1)) if err.ndim else 0.5 * err


def _adamw(w, g, m, v):
    m = ADAM_B1 * m + (1.0 - ADAM_B1) * g
    v = ADAM_B2 * v + (1.0 - ADAM_B2) * _jnp.square(g)
    m_hat = m / (1.0 - ADAM_B1 ** ADAM_STEP)
    v_hat = v / (1.0 - ADAM_B2 ** ADAM_STEP)
    delta = -ADAM_LR * (m_hat / (_jnp.sqrt(v_hat) + ADAM_EPS) + ADAM_WD * w)
    return delta, m, v


def reference(x, mem, g_pre, w_in, w_conv, attn_sink, g_mem, w_mem_kv, w_up_a, w_up_b, w_up_m, w_out, g_post, loss_target, m_g_pre, m_w_in, m_w_conv, m_attn_sink, m_g_mem, m_w_mem_kv, m_w_up_a, m_w_up_b, m_w_up_m, m_w_out, m_g_post, v_g_pre, v_w_in, v_w_conv, v_attn_sink, v_g_mem, v_w_mem_kv, v_w_up_a, v_w_up_b, v_w_up_m, v_w_out, v_g_post):
    given = dict(x=x, mem=mem, g_pre=g_pre, w_in=w_in, w_conv=w_conv, attn_sink=attn_sink, g_mem=g_mem, w_mem_kv=w_mem_kv, w_up_a=w_up_a, w_up_b=w_up_b, w_up_m=w_up_m, w_out=w_out, g_post=g_post, loss_target=loss_target, m_g_pre=m_g_pre, m_w_in=m_w_in, m_w_conv=m_w_conv, m_attn_sink=m_attn_sink, m_g_mem=m_g_mem, m_w_mem_kv=m_w_mem_kv, m_w_up_a=m_w_up_a, m_w_up_b=m_w_up_b, m_w_up_m=m_w_up_m, m_w_out=m_w_out, m_g_post=m_g_post, v_g_pre=v_g_pre, v_w_in=v_w_in, v_w_conv=v_w_conv, v_attn_sink=v_attn_sink, v_g_mem=v_g_mem, v_w_mem_kv=v_w_mem_kv, v_w_up_a=v_w_up_a, v_w_up_b=v_w_up_b, v_w_up_m=v_w_up_m, v_w_out=v_w_out, v_g_post=v_g_post)
    weights = {n: given[n] for n in TWIN_WEIGHTS}
    shared = {n: given[n] for n in SHARED_INPUTS}
    per_example = {n: given[n] for n in ['x', 'mem']}
    grad_fn = _jax.value_and_grad(_loss, argnums=(0, 1))

    def one_microbatch(ex, loss_target):
        ex = dict(ex)
        diff = ex.pop(TWIN_DIFF_INPUT)
        return grad_fn(weights, diff, {**shared, **ex}, loss_target)

    if N_MICROBATCH == 1:
        loss, (grad_w, grad_x) = one_microbatch(per_example, given["loss_target"])
    else:
        def body(carry, xs):
            loss_sum, grad_sum = carry
            l_k, (gw_k, gx_k) = one_microbatch(xs[0], xs[1])
            with _jax.named_scope("update"):
                return (loss_sum + l_k, _jax.tree.map(_jnp.add, grad_sum, gw_k)), gx_k

        init = (_jnp.zeros((), _jnp.float32), _jax.tree.map(_jnp.zeros_like, weights))
        (loss, grad_w), grad_x = _jax.lax.scan(body, init, (per_example, given["loss_target"]))
    with _jax.named_scope("update"):
        delta_w, new_m, new_v = {}, {}, {}
        for n in TWIN_WEIGHTS:
            delta_w[n], new_m[n], new_v[n] = _adamw(weights[n], grad_w[n], given["m_" + n], given["v_" + n])
    return (loss, grad_x, *[grad_w[n] for n in TWIN_WEIGHTS], *[delta_w[n] for n in TWIN_WEIGHTS],
            *[new_m[n] for n in TWIN_WEIGHTS], *[new_v[n] for n in TWIN_WEIGHTS])
```

```python
import functools

import jax
import jax.numpy as jnp
from jax import lax
from jax.experimental import pallas as pl
from jax.experimental.pallas import tpu as pltpu

F32 = jnp.float32
BF16 = jnp.bfloat16
MESH = pl.DeviceIdType.MESH

N_DEV = 8
D_MODEL = 1024
EPS = 1e-6
ROPE_THETA = 500000.0
ROT_DIM = 16
HEAD_DIM = 64
ATTN_BLOCK = 128
MEM_HEADS = 4
MEM_HEAD_DIM = 128
ATTN_SCALE = HEAD_DIM ** -0.5
MEM_SCALE = MEM_HEAD_DIM ** -0.5

ADAM_LR = 0.001
ADAM_B1 = 0.9
ADAM_B2 = 0.999
ADAM_EPS = 1e-08
ADAM_WD = 0.01
ADAM_STEP = 10

SEG_A = (0, 2048)
SEG_BQ = (2048, 512)
SEG_BKV = (2560, 256)
SEG_BZ = (2816, 512)
SEG_MQ = (3328, 512)
SEG_MZ = (3840, 512)
SEG_G = (4352, 3072)
SEGS = (SEG_A, SEG_BQ, SEG_BKV, SEG_BZ, SEG_MQ, SEG_MZ, SEG_G)
IN_WIDTH = 7424
SHARD_IN = IN_WIDTH // N_DEV

V7X_VMEM_BYTES = 64 * 1024 * 1024
ANY = pl.BlockSpec(memory_space=pl.ANY)


def _params(vmem_mb):
    assert vmem_mb * 1024 * 1024 < V7X_VMEM_BYTES
    return pltpu.CompilerParams(dimension_semantics=("arbitrary",), vmem_limit_bytes=vmem_mb * 1024 * 1024)


def _full(shape):
    zeros = (0,) * len(shape)
    return pl.BlockSpec(shape, lambda i: zeros)


def _rows(tm, width):
    return pl.BlockSpec((tm, width), lambda i: (i, 0))


def _dot(a, b):
    return jnp.dot(a, b, preferred_element_type=F32)


def _dot_nt(a, b):
    return lax.dot_general(a, b, (((1,), (1,)), ((), ())), preferred_element_type=F32)


def _dot_tn(a, b):
    return lax.dot_general(a, b, (((0,), (0,)), ((), ())), preferred_element_type=F32)


def _sigmoid(z):
    return 1.0 / (1.0 + jnp.exp(-z))


def _rope(t, cs, s1, s2):
    return t * cs + pltpu.roll(t, 120, 1) * s1 + pltpu.roll(t, 8, 1) * s2


def _rope_t(d, cs, s1, s2):
    return d * cs + pltpu.roll(d * s1, 8, 1) + pltpu.roll(d * s2, 120, 1)


def _rope_tables(s):
    half = ROT_DIM // 2
    inv_freq = jnp.power(jnp.float32(ROPE_THETA), -jnp.arange(half, dtype=F32) * (2.0 / ROT_DIM))
    ang = jnp.arange(s).astype(F32)[:, None] * inv_freq[None, :]
    cos, sin = jnp.cos(ang), jnp.sin(ang)
    one, zero = jnp.ones((s, HEAD_DIM - ROT_DIM), F32), jnp.zeros((s, HEAD_DIM - ROT_DIM), F32)
    z8 = jnp.zeros((s, half), F32)
    cs = jnp.concatenate([cos, cos, one], axis=1)
    s1 = jnp.concatenate([-sin, z8, zero], axis=1)
    s2 = jnp.concatenate([z8, sin, zero], axis=1)
    return tuple(jnp.concatenate([t, t], axis=1) for t in (cs, s1, s2))


def _load_once(pairs, sems):
    @pl.when(pl.program_id(0) == 0)
    def _():
        cps = [pltpu.make_async_copy(src, dst, sems.at[k]) for k, (src, dst) in enumerate(pairs)]
        for cp in cps:
            cp.start()
        for cp in cps:
            cp.wait()


def _my_place():
    x, y, c = lax.axis_index("x"), lax.axis_index("y"), lax.axis_index("c")
    return x, y, c


def _all_gather(arrs, name):
    n = len(arrs)

    def body(*refs):
        ins, outs = refs[:n], refs[n:2 * n]
        send_sems, recv_sems, local_sems = refs[2 * n:]
        x, y, c = _my_place()
        me, sibling = (x, y, c), (x, y, 1 - c)
        chips = [(1 - x, y), (x, 1 - y), (1 - x, 1 - y)]

        def idx(px, py, pc):
            return 4 * px + 2 * py + pc

        def copy(a, k, block, to, src=None):
            dst = outs[a].at[idx(*block)]
            return pltpu.make_async_remote_copy(
                src_ref=dst if src is None else src, dst_ref=dst,
                send_sem=send_sems.at[a * 7 + k], recv_sem=recv_sems.at[a * 7 + k],
                device_id=to, device_id_type=MESH)

        mine = [pltpu.make_async_copy(ins[a], outs[a].at[idx(*me)], local_sems.at[a]) for a in range(n)]
        for cp in mine:
            cp.start()
        first = []
        for a in range(n):
            first.append(copy(a, 0, me, sibling, src=ins[a]))
        for j, chip in enumerate(chips):
            for a in range(n):
                first.append(copy(a, 1 + j, me, (*chip, c), src=ins[a]))
        for cp in first:
            cp.start()
        passed = []
        for j, chip in enumerate(chips):
            for a in range(n):
                copy(a, 1 + j, (*chip, c), me).wait_recv()
                cp = copy(a, 4 + j, (*chip, c), sibling)
                cp.start()
                passed.append(cp)
        for a in range(n):
            copy(a, 0, sibling, me).wait_recv()
        for j, chip in enumerate(chips):
            for a in range(n):
                copy(a, 4 + j, (*chip, 1 - c), me).wait_recv()
        for cp in first + passed:
            cp.wait_send()
        for cp in mine:
            cp.wait()

    return pl.pallas_call(
        body, name=name,
        out_shape=[jax.ShapeDtypeStruct((N_DEV,) + a.shape, a.dtype) for a in arrs],
        in_specs=[ANY] * n, out_specs=[ANY] * n,
        scratch_shapes=[pltpu.SemaphoreType.DMA((7 * n,)), pltpu.SemaphoreType.DMA((7 * n,)),
                        pltpu.SemaphoreType.DMA((n,))],
    )(*arrs)


def _all_to_all(arrs, name):
    n = len(arrs)

    def body(*refs):
        ins, outs = refs[:n], refs[n:2 * n]
        send_sems, recv_sems, local_sems = refs[2 * n:]
        x, y, c = _my_place()
        me_idx = 4 * x + 2 * y + c
        flips = [(0, 0, 1), (0, 1, 0), (1, 0, 0), (0, 1, 1), (1, 0, 1), (1, 1, 0), (1, 1, 1)]

        def peer(f):
            fx, fy, fc = f
            return ((1 - x) if fx else x, (1 - y) if fy else y, (1 - c) if fc else c)

        mine = [pltpu.make_async_copy(ins[a].at[me_idx], outs[a].at[me_idx], local_sems.at[a]) for a in range(n)]
        for cp in mine:
            cp.start()
        sent = []
        for k, f in enumerate(flips):
            px, py, pc = peer(f)
            p_idx = 4 * px + 2 * py + pc
            for a in range(n):
                cp = pltpu.make_async_remote_copy(
                    src_ref=ins[a].at[p_idx], dst_ref=outs[a].at[me_idx],
                    send_sem=send_sems.at[a * 7 + k], recv_sem=recv_sems.at[a * 7 + k],
                    device_id=(px, py, pc), device_id_type=MESH)
                cp.start()
                sent.append(cp)
        for k, f in enumerate(flips):
            px, py, pc = peer(f)
            p_idx = 4 * px + 2 * py + pc
            for a in range(n):
                pltpu.make_async_remote_copy(
                    src_ref=ins[a].at[p_idx], dst_ref=outs[a].at[p_idx],
                    send_sem=send_sems.at[a * 7 + k], recv_sem=recv_sems.at[a * 7 + k],
                    device_id=(px, py, pc), device_id_type=MESH).wait_recv()
        for cp in sent:
            cp.wait_send()
        for cp in mine:
            cp.wait()

    return pl.pallas_call(
        body, name=name,
        out_shape=[jax.ShapeDtypeStruct(a.shape, a.dtype) for a in arrs],
        in_specs=[ANY] * n, out_specs=[ANY] * n,
        scratch_shapes=[pltpu.SemaphoreType.DMA((7 * n,)), pltpu.SemaphoreType.DMA((7 * n,)),
                        pltpu.SemaphoreType.DMA((n,))],
    )(*arrs)


def _proj_fwd(x, g_pre, w_int, tabs):
    s = x.shape[0]
    tm = min(512, s)

    def body(x_ref, g_ref, cs_ref, s1_ref, s2_ref, w_hbm,
             h_ref, pa_ref, pq_ref, pkv_ref, pbz_ref, pmq_ref, pmz_ref, pg_ref, w_vm, sems):
        _load_once([(w_hbm, w_vm)], sems)
        xf = x_ref[...]
        r = lax.rsqrt(jnp.mean(xf * xf, axis=-1, keepdims=True) + EPS)
        h = ((xf * r) * g_ref[...]).astype(BF16)
        h_ref[...] = h
        cs, s1, s2 = cs_ref[...], s1_ref[...], s2_ref[...]

        def mm(seg, c0, width):
            return _dot_nt(h, w_vm[seg[0] + c0:seg[0] + c0 + width, :])

        for c0 in range(0, SEG_A[1], 512):
            pa_ref[:, c0:c0 + 512] = mm(SEG_A, c0, 512).astype(BF16)
        q = mm(SEG_BQ, 0, 512)
        for b in range(4):
            pq_ref[:, 128 * b:128 * b + 128] = _rope(q[:, 128 * b:128 * b + 128], cs, s1, s2).astype(BF16)
        kv = mm(SEG_BKV, 0, 256)
        pkv_ref[:, 0:128] = _rope(kv[:, 0:128], cs, s1, s2).astype(BF16)
        pkv_ref[:, 128:256] = kv[:, 128:256].astype(BF16)
        pbz_ref[...] = mm(SEG_BZ, 0, 512).astype(BF16)
        pmq_ref[...] = mm(SEG_MQ, 0, 512).astype(BF16)
        pmz_ref[...] = mm(SEG_MZ, 0, 512).astype(BF16)
        for c0 in range(0, SEG_G[1], 512):
            pg_ref[:, c0:c0 + 512] = mm(SEG_G, c0, 512).astype(BF16)

    widths = (D_MODEL, 2048, 512, 256, 512, 512, 512, 3072)
    return pl.pallas_call(
        body, name="proj_fwd", grid=(s // tm,),
        out_shape=[jax.ShapeDtypeStruct((s, w), BF16) for w in widths],
        in_specs=[_rows(tm, D_MODEL), _full((1, D_MODEL)), _rows(tm, 128), _rows(tm, 128), _rows(tm, 128), ANY],
        out_specs=[_rows(tm, w) for w in widths],
        scratch_shapes=[pltpu.VMEM((IN_WIDTH, D_MODEL), BF16), pltpu.SemaphoreType.DMA((1,))],
        compiler_params=_params(52),
    )(x, g_pre, *tabs, w_int)


def _mem_kv_fwd(mem, g_mem, w_mkv):
    m = mem.shape[0]

    def body(mem_ref, g_ref, w_ref, mn_ref, mkv_ref):
        xf = mem_ref[...]
        r = lax.rsqrt(jnp.mean(xf * xf, axis=-1, keepdims=True) + EPS)
        mn = ((xf * r) * g_ref[...]).astype(BF16)
        mn_ref[...] = mn
        mkv_ref[...] = _dot(mn, w_ref[...]).astype(BF16)

    return pl.pallas_call(
        body, name="mem_kv_fwd", grid=(1,),
        out_shape=[jax.ShapeDtypeStruct((m, D_MODEL), BF16)] * 2,
        in_specs=[_full((m, D_MODEL)), _full((1, D_MODEL)), _full((D_MODEL, D_MODEL))],
        out_specs=[_full((m, D_MODEL))] * 2,
        compiler_params=_params(32),
    )(mem, g_mem, w_mkv)


def _halo_specs(s, tm, rows, width):
    nblk = s // rows
    prev = pl.BlockSpec((rows, width), lambda i: (jnp.maximum(i * (tm // rows) - 1, 0), 0))
    nxt = pl.BlockSpec((rows, width), lambda i: (jnp.minimum((i + 1) * (tm // rows), nblk - 1), 0))
    return prev, nxt


def _conv_common(pa, prev_row, next_row, w, first, last, tm):
    b, c, u, z = (pa[:, 512 * k:512 * k + 512] for k in range(4))
    cu = c * u
    cu_prev = jnp.where(first, 0.0, prev_row[:, 512:1024] * prev_row[:, 1024:1536])
    cu_next = jnp.where(last, 0.0, next_row[:, 512:1024] * next_row[:, 1024:1536])
    row = lax.broadcasted_iota(jnp.int32, (tm, 512), 0)
    cu_m1 = jnp.where(row == 0, cu_prev, pltpu.roll(cu, 1, 0))
    cu_p1 = jnp.where(row == tm - 1, cu_next, pltpu.roll(cu, tm - 1, 0))
    y = cu_m1 * w[0:1] + cu * w[1:2] + cu_p1 * w[2:3]
    sig = _sigmoid(z)
    return b, c, u, z, cu, cu_m1, cu_p1, y, sig, row


def _conv_fwd(pa, w_conv):
    s = pa.shape[0]
    tm = min(512, s)
    nt = s // tm

    def body(pa_ref, pp_ref, pn_ref, w_ref, ya_ref):
        i = pl.program_id(0)
        prev_row = pp_ref[...].astype(F32)[15:16, :]
        next_row = pn_ref[...].astype(F32)[0:1, :]
        b, _, _, z, _, _, _, y, sig, _ = _conv_common(
            pa_ref[...].astype(F32), prev_row, next_row, w_ref[...], i == 0, i == nt - 1, tm)
        ya_ref[...] = (b * y * (z * sig)).astype(BF16)

    prev, nxt = _halo_specs(s, tm, 16, 2048)
    return pl.pallas_call(
        body, name="conv_fwd", grid=(nt,),
        out_shape=jax.ShapeDtypeStruct((s, 512), BF16),
        in_specs=[_rows(tm, 2048), prev, nxt, _full((3, 512))],
        out_specs=_rows(tm, 512),
        compiler_params=_params(48),
    )(pa, pa, pa, w_conv)


def _stack_heads(a, g, lane):
    in_g = (lane >= 64 * g) & (lane < 64 * g + 64)
    parts = []
    for j in range(4):
        h = 4 * g + j
        blk = a[:, 128 * (h // 2):128 * (h // 2) + 128]
        if h % 2 != g:
            blk = pltpu.roll(blk, 64, 1)
        parts.append(jnp.where(in_g, blk, 0.0))
    return jnp.concatenate(parts, axis=0)


def _unstack_heads(o0, o1, lane):
    blocks = []
    for b in range(4):
        g = b // 2
        og = (o0, o1)[g]
        je, jo = (2 * b) % 4, (2 * b + 1) % 4
        even, odd = og[128 * je:128 * je + 128], og[128 * jo:128 * jo + 128]
        if g == 0:
            odd = pltpu.roll(odd, 64, 1)
        else:
            even = pltpu.roll(even, 64, 1)
        blocks.append(jnp.where(lane < 64, even, odd))
    return jnp.concatenate(blocks, axis=1)


def _window_mask(n, s):
    r = lax.broadcasted_iota(jnp.int32, (512, 384), 0) & 127
    c = lax.broadcasted_iota(jnp.int32, (512, 384), 1)
    kpos = c + (n - 1) * ATTN_BLOCK
    return (c >= r) & (c <= r + 2 * ATTN_BLOCK) & (kpos >= 0) & (kpos < s)


def _sink_col(sink_ref, g):
    return jnp.concatenate([jnp.full((ATTN_BLOCK, 1), sink_ref[4 * g + j], F32) for j in range(4)], axis=0)


def _attn_group(qg, kw, vw, sink, valid):
    sc = jnp.where(valid, _dot_nt(qg, kw) * ATTN_SCALE, -jnp.inf)
    m = jnp.maximum(jnp.max(sc, axis=-1, keepdims=True), sink)
    e = jnp.exp(sc - m)
    es = jnp.exp(sink - m)
    denom = jnp.sum(e, axis=-1, keepdims=True) + es
    p = e / denom
    o = _dot(p.astype(BF16), vw)
    return p, o, es / denom


def _fill_padded(kv_ref, kpad, vpad, s):
    zero = jnp.zeros((ATTN_BLOCK, 128), BF16)
    kpad[0:ATTN_BLOCK, :] = zero
    vpad[0:ATTN_BLOCK, :] = zero
    kpad[ATTN_BLOCK + s:2 * ATTN_BLOCK + s, :] = zero
    vpad[ATTN_BLOCK + s:2 * ATTN_BLOCK + s, :] = zero
    kpad[ATTN_BLOCK:ATTN_BLOCK + s, :] = kv_ref[:, 0:128]
    vpad[ATTN_BLOCK:ATTN_BLOCK + s, :] = kv_ref[:, 128:256]


def _attn_fwd(pq, pkv, pbz, sink):
    s = pq.shape[0]
    nb = s // ATTN_BLOCK

    def body(sink_ref, q_ref, z_ref, kv_ref, yb_ref, kpad, vpad):
        n = pl.program_id(0)

        @pl.when(n == 0)
        def _():
            _fill_padded(kv_ref, kpad, vpad, s)

        lane = lax.broadcasted_iota(jnp.int32, (ATTN_BLOCK, 128), 1)
        start = pl.multiple_of(n * ATTN_BLOCK, ATTN_BLOCK)
        kw, vw = kpad[pl.ds(start, 384), :], vpad[pl.ds(start, 384), :]
        qf = q_ref[...].astype(F32)
        valid = _window_mask(n, s)
        outs = []
        for g in range(2):
            qg = _stack_heads(qf, g, lane).astype(BF16)
            _, o, _ = _attn_group(qg, kw, vw, _sink_col(sink_ref, g), valid)
            outs.append(o)
        attn = _unstack_heads(outs[0], outs[1], lane)
        z = z_ref[...].astype(F32)
        yb_ref[...] = (attn * (z * _sigmoid(z))).astype(BF16)

    return pl.pallas_call(
        body, name="attn_fwd", grid=(nb,),
        out_shape=jax.ShapeDtypeStruct((s, 512), BF16),
        in_specs=[pl.BlockSpec(memory_space=pltpu.SMEM), _rows(ATTN_BLOCK, 512), _rows(ATTN_BLOCK, 512),
                  _full((s, 256))],
        out_specs=_rows(ATTN_BLOCK, 512),
        scratch_shapes=[pltpu.VMEM((s + 2 * ATTN_BLOCK, 128), BF16)] * 2,
        compiler_params=_params(32),
    )(sink, pq, pbz, pkv)


def _mem_softmax(q, mk):
    sc = _dot_nt(q, mk) * MEM_SCALE
    e = jnp.exp(sc - jnp.max(sc, axis=-1, keepdims=True))
    return e / jnp.sum(e, axis=-1, keepdims=True)


def _mem_attn_fwd(pmq, pmz, mkv):
    s = pmq.shape[0]
    m = mkv.shape[0]
    tm = min(512, s)

    def body(q_ref, z_ref, mk_ref, mv_ref, ym_ref):
        z = z_ref[...].astype(F32)
        sz = z * _sigmoid(z)
        for h in range(MEM_HEADS):
            cols = slice(128 * h, 128 * h + 128)
            p = _mem_softmax(q_ref[:, cols], mk_ref[:, cols])
            o = _dot(p.astype(BF16), mv_ref[:, cols])
            ym_ref[:, cols] = (o * sz[:, cols]).astype(BF16)

    return pl.pallas_call(
        body, name="mem_attn_fwd", grid=(s // tm,),
        out_shape=jax.ShapeDtypeStruct((s, 512), BF16),
        in_specs=[_rows(tm, 512), _rows(tm, 512), pl.BlockSpec((m, 512), lambda i: (0, 0)),
                  pl.BlockSpec((m, 512), lambda i: (0, 1))],
        out_specs=_rows(tm, 512),
        compiler_params=_params(32),
    )(pmq, pmz, mkv, mkv)


def _mid(ya, yb, ym, pg, x, target, g_post, w_up, w_out):
    s = x.shape[0]
    tm = min(256, s)
    nt = s // tm

    def body(ya_ref, yb_ref, ym_ref, pg_ref, x_ref, t_ref, gp_ref, wup_hbm, wout_hbm,
             dg_ref, dya_ref, dyb_ref, dym_ref, dy_ref, loss_ref, ggp_ref, gwout_hbm, gwup_hbm,
             wup_vm, wout_vm, acc_out, acc_up, st_out, st_up, sems):
        i = pl.program_id(0)
        _load_once([(wup_hbm.at[d], wup_vm.at[:, pl.ds(128 * d, 128)]) for d in range(N_DEV)]
                   + [(wout_hbm, wout_vm)], sems)

        @pl.when(i == 0)
        def _():
            acc_out[...] = jnp.zeros_like(acc_out)
            acc_up[...] = jnp.zeros_like(acc_up)
            loss_ref[...] = jnp.zeros_like(loss_ref)
            ggp_ref[...] = jnp.zeros_like(ggp_ref)

        ys = (ya_ref[...], yb_ref[...], ym_ref[...])
        us = [_dot(ys[k], wup_vm[512 * k:512 * k + 512, :]) for k in range(3)]
        gates = [_sigmoid(pg_ref[:, 1024 * k:1024 * k + 1024].astype(F32)) for k in range(3)]
        merged = gates[0] * us[0] + gates[1] * us[1] + gates[2] * us[2]
        mb = merged.astype(BF16)
        out = _dot(mb, wout_vm[...])
        r = lax.rsqrt(jnp.mean(out * out, axis=-1, keepdims=True) + EPS)
        on = out * r
        gp = gp_ref[...]
        err = (x_ref[...] + on * gp) - t_ref[...]
        loss_ref[...] += 0.5 * jnp.sum(err * err) * (1.0 / D_MODEL)
        dy = err * (1.0 / D_MODEL)
        dy_ref[...] = dy
        ggp_ref[...] += jnp.sum(dy * on, axis=0, keepdims=True)
        a = dy * gp
        d_out = r * (a - on * jnp.mean(a * on, axis=-1, keepdims=True))
        dob = d_out.astype(BF16)
        acc_out[...] += _dot_tn(mb, dob)
        d_merged = _dot_nt(dob, wout_vm[...])
        d_refs = (dya_ref, dyb_ref, dym_ref)
        for k in range(3):
            g = gates[k]
            dg_ref[:, 1024 * k:1024 * k + 1024] = (d_merged * us[k] * g * (1.0 - g)).astype(BF16)
            du = (d_merged * g).astype(BF16)
            d_refs[k][...] = _dot_nt(du, wup_vm[512 * k:512 * k + 512, :])
            acc_up[512 * k:512 * k + 512, :] += _dot_tn(ys[k], du)

        @pl.when(i == nt - 1)
        def _():
            st_out[...] = acc_out[...].astype(BF16)
            for d in range(N_DEV):
                st_up[d] = acc_up[:, 128 * d:128 * d + 128].astype(BF16)
            cps = [pltpu.make_async_copy(st_out, gwout_hbm, sems.at[0]),
                   pltpu.make_async_copy(st_up, gwup_hbm, sems.at[1])]
            for cp in cps:
                cp.start()
            for cp in cps:
                cp.wait()

    return pl.pallas_call(
        body, name="mid", grid=(nt,),
        out_shape=[jax.ShapeDtypeStruct((s, 3072), BF16)] + [jax.ShapeDtypeStruct((s, 512), F32)] * 3
        + [jax.ShapeDtypeStruct((s, D_MODEL), F32), jax.ShapeDtypeStruct((8, 128), F32),
           jax.ShapeDtypeStruct((1, D_MODEL), F32), jax.ShapeDtypeStruct((D_MODEL, D_MODEL), BF16),
           jax.ShapeDtypeStruct((N_DEV, 1536, 128), BF16)],
        in_specs=[_rows(tm, 512)] * 3 + [_rows(tm, 3072), _rows(tm, D_MODEL), _rows(tm, D_MODEL),
                                         _full((1, D_MODEL)), ANY, ANY],
        out_specs=[_rows(tm, 3072)] + [_rows(tm, 512)] * 3 + [_rows(tm, D_MODEL), _full((8, 128)),
                                                               _full((1, D_MODEL)), ANY, ANY],
        scratch_shapes=[pltpu.VMEM((1536, D_MODEL), BF16), pltpu.VMEM((D_MODEL, D_MODEL), BF16),
                        pltpu.VMEM((D_MODEL, D_MODEL), F32), pltpu.VMEM((1536, D_MODEL), F32),
                        pltpu.VMEM((D_MODEL, D_MODEL), BF16), pltpu.VMEM((N_DEV, 1536, 128), BF16),
                        pltpu.SemaphoreType.DMA((N_DEV + 1,))],
        compiler_params=_params(60),
    )(ya, yb, ym, pg, x, target, g_post, w_up, w_out)


def _conv_bwd(pa, dya, w_conv):
    s = pa.shape[0]
    tm = min(512, s)
    nt = s // tm

    def body(pa_ref, pp_ref, pn_ref, d_ref, dp_ref, dn_ref, w_ref, da_ref, gw_ref):
        i = pl.program_id(0)
        first, last = i == 0, i == nt - 1

        @pl.when(first)
        def _():
            gw_ref[...] = jnp.zeros_like(gw_ref)

        w = w_ref[...]
        prev_row = pp_ref[...].astype(F32)[15:16, :]
        next_row = pn_ref[...].astype(F32)[0:1, :]
        b, c, u, z, cu, cu_m1, cu_p1, y, sig, row = _conv_common(
            pa_ref[...].astype(F32), prev_row, next_row, w, first, last, tm)
        sz = z * sig
        dya_t = d_ref[...]
        d_y = dya_t * b * sz

        def halo_dy(p_row, d_row):
            zz = p_row[:, 1536:2048]
            return d_row * p_row[:, 0:512] * (zz * _sigmoid(zz))

        dy_prev = jnp.where(first, 0.0, halo_dy(prev_row, dp_ref[7:8, :]))
        dy_next = jnp.where(last, 0.0, halo_dy(next_row, dn_ref[0:1, :]))
        dy_m1 = jnp.where(row == 0, dy_prev, pltpu.roll(d_y, 1, 0))
        dy_p1 = jnp.where(row == tm - 1, dy_next, pltpu.roll(d_y, tm - 1, 0))
        d_cu = dy_p1 * w[0:1] + d_y * w[1:2] + dy_m1 * w[2:3]
        da_ref[:, 0:512] = (dya_t * y * sz).astype(BF16)
        da_ref[:, 512:1024] = (d_cu * u).astype(BF16)
        da_ref[:, 1024:1536] = (d_cu * c).astype(BF16)
        da_ref[:, 1536:2048] = (dya_t * b * y * (sig * (1.0 + z * (1.0 - sig)))).astype(BF16)
        gw_ref[0:1, :] += jnp.sum(d_y * cu_m1, axis=0, keepdims=True)
        gw_ref[1:2, :] += jnp.sum(d_y * cu, axis=0, keepdims=True)
        gw_ref[2:3, :] += jnp.sum(d_y * cu_p1, axis=0, keepdims=True)

    prev, nxt = _halo_specs(s, tm, 16, 2048)
    dprev, dnxt = _halo_specs(s, tm, 8, 512)
    return pl.pallas_call(
        body, name="conv_bwd", grid=(nt,),
        out_shape=[jax.ShapeDtypeStruct((s, 2048), BF16), jax.ShapeDtypeStruct((8, 512), F32)],
        in_specs=[_rows(tm, 2048), prev, nxt, _rows(tm, 512), dprev, dnxt, _full((3, 512))],
        out_specs=[_rows(tm, 2048), _full((8, 512))],
        compiler_params=_params(48),
    )(pa, pa, pa, dya, dya, dya, w_conv)


def _attn_bwd(pq, pkv, pbz, dyb, sink, tabs):
    s = pq.shape[0]
    nb = s // ATTN_BLOCK

    def body(sink_ref, q_ref, z_ref, d_ref, cs_ref, s1_ref, s2_ref, kv_ref, csf_ref, s1f_ref, s2f_ref,
             dq_ref, dz_ref, dkv_ref, gs_ref, kpad, vpad, dk_acc, dv_acc):
        n = pl.program_id(0)

        @pl.when(n == 0)
        def _():
            _fill_padded(kv_ref, kpad, vpad, s)
            dk_acc[...] = jnp.zeros_like(dk_acc)
            dv_acc[...] = jnp.zeros_like(dv_acc)
            gs_ref[...] = jnp.zeros_like(gs_ref)

        lane = lax.broadcasted_iota(jnp.int32, (ATTN_BLOCK, 128), 1)
        start = pl.multiple_of(n * ATTN_BLOCK, ATTN_BLOCK)
        kw, vw = kpad[pl.ds(start, 384), :], vpad[pl.ds(start, 384), :]
        qf = q_ref[...].astype(F32)
        valid = _window_mask(n, s)
        z = z_ref[...].astype(F32)
        sig = _sigmoid(z)
        dyb_t = d_ref[...]
        d_attn = dyb_t * (z * sig)
        outs, dqs = [], []
        dk_w = jnp.zeros((384, 128), F32)
        dv_w = jnp.zeros((384, 128), F32)
        for g in range(2):
            qg = _stack_heads(qf, g, lane).astype(BF16)
            p, o, p_sink = _attn_group(qg, kw, vw, _sink_col(sink_ref, g), valid)
            outs.append(o)
            do = _stack_heads(d_attn, g, lane)
            dob = do.astype(BF16)
            dp = _dot_nt(dob, vw)
            delta = jnp.sum(do * o, axis=-1, keepdims=True)
            ds = (p * (dp - delta)).astype(BF16)
            sink_part = p_sink * delta
            for j in range(4):
                h = 4 * g + j
                gs_ref[h:h + 1, :] -= jnp.sum(sink_part[128 * j:128 * j + 128, :])
            dqs.append(_dot(ds, kw) * ATTN_SCALE)
            dk_w += _dot_tn(ds, qg) * ATTN_SCALE
            dv_w += _dot_tn(p.astype(BF16), dob)
        dk_acc[pl.ds(start, 384), :] += dk_w
        dv_acc[pl.ds(start, 384), :] += dv_w
        attn = _unstack_heads(outs[0], outs[1], lane)
        dz_ref[...] = (dyb_t * attn * (sig * (1.0 + z * (1.0 - sig)))).astype(BF16)
        dq = _unstack_heads(dqs[0], dqs[1], lane)
        cs, s1, s2 = cs_ref[...], s1_ref[...], s2_ref[...]
        for b in range(4):
            dq_ref[:, 128 * b:128 * b + 128] = _rope_t(dq[:, 128 * b:128 * b + 128], cs, s1, s2).astype(BF16)

        @pl.when(n == nb - 1)
        def _():
            dk = dk_acc[ATTN_BLOCK:ATTN_BLOCK + s, :]
            dkv_ref[:, 0:128] = _rope_t(dk, csf_ref[...], s1f_ref[...], s2f_ref[...]).astype(BF16)
            dkv_ref[:, 128:256] = dv_acc[ATTN_BLOCK:ATTN_BLOCK + s, :].astype(BF16)

    tile = _rows(ATTN_BLOCK, 512)
    tab = _rows(ATTN_BLOCK, 128)
    return pl.pallas_call(
        body, name="attn_bwd", grid=(nb,),
        out_shape=[jax.ShapeDtypeStruct((s, 512), BF16), jax.ShapeDtypeStruct((s, 512), BF16),
                   jax.ShapeDtypeStruct((s, 256), BF16), jax.ShapeDtypeStruct((8, 128), F32)],
        in_specs=[pl.BlockSpec(memory_space=pltpu.SMEM), tile, tile, tile, tab, tab, tab,
                  _full((s, 256)), _full((s, 128)), _full((s, 128)), _full((s, 128))],
        out_specs=[tile, tile, _full((s, 256)), _full((8, 128))],
        scratch_shapes=[pltpu.VMEM((s + 2 * ATTN_BLOCK, 128), BF16)] * 2
        + [pltpu.VMEM((s + 2 * ATTN_BLOCK, 128), F32)] * 2,
        compiler_params=_params(48),
    )(sink, pq, pbz, dyb, *tabs, pkv, *tabs)


def _mem_attn_bwd(pmq, pmz, mkv, dym):
    s = pmq.shape[0]
    m = mkv.shape[0]
    tm = min(512, s)

    def body(q_ref, z_ref, d_ref, mk_ref, mv_ref, dq_ref, dz_ref, dmkv_ref):
        @pl.when(pl.program_id(0) == 0)
        def _():
            dmkv_ref[...] = jnp.zeros_like(dmkv_ref)

        z = z_ref[...].astype(F32)
        sig = _sigmoid(z)
        dym_t = d_ref[...]
        d_attn = dym_t * (z * sig)
        dsilu = sig * (1.0 + z * (1.0 - sig))
        for h in range(MEM_HEADS):
            cols = slice(128 * h, 128 * h + 128)
            q, mk, mv = q_ref[:, cols], mk_ref[:, cols], mv_ref[:, cols]
            p = _mem_softmax(q, mk)
            pb = p.astype(BF16)
            o = _dot(pb, mv)
            do = d_attn[:, cols]
            dob = do.astype(BF16)
            dp = _dot_nt(dob, mv)
            ds = (p * (dp - jnp.sum(do * o, axis=-1, keepdims=True))).astype(BF16)
            dq_ref[:, cols] = (_dot(ds, mk) * MEM_SCALE).astype(BF16)
            dz_ref[:, cols] = (dym_t[:, cols] * o * dsilu[:, cols]).astype(BF16)
            dmkv_ref[:, cols] += _dot_tn(ds, q) * MEM_SCALE
            dmkv_ref[:, 512 + 128 * h:512 + 128 * h + 128] += _dot_tn(pb, dob)

    return pl.pallas_call(
        body, name="mem_attn_bwd", grid=(s // tm,),
        out_shape=[jax.ShapeDtypeStruct((s, 512), BF16), jax.ShapeDtypeStruct((s, 512), BF16),
                   jax.ShapeDtypeStruct((m, D_MODEL), F32)],
        in_specs=[_rows(tm, 512), _rows(tm, 512), _rows(tm, 512), pl.BlockSpec((m, 512), lambda i: (0, 0)),
                  pl.BlockSpec((m, 512), lambda i: (0, 1))],
        out_specs=[_rows(tm, 512), _rows(tm, 512), _full((m, D_MODEL))],
        compiler_params=_params(32),
    )(pmq, pmz, dym, mkv, mkv)


def _mem_kv_bwd(mem, g_mem, mn, dmkv, w_mkv):
    m = mem.shape[0]

    def body(mem_ref, g_ref, mn_ref, d_ref, w_ref, gw_ref, gg_ref):
        db = d_ref[...].astype(BF16)
        gw_ref[...] = _dot_tn(mn_ref[...], db).astype(BF16)
        d_mn = _dot_nt(db, w_ref[...])
        xf = mem_ref[...]
        r = lax.rsqrt(jnp.mean(xf * xf, axis=-1, keepdims=True) + EPS)
        gg_ref[...] = jnp.sum(d_mn * (xf * r), axis=0, keepdims=True)

    return pl.pallas_call(
        body, name="mem_kv_bwd", grid=(1,),
        out_shape=[jax.ShapeDtypeStruct((D_MODEL, D_MODEL), BF16), jax.ShapeDtypeStruct((1, D_MODEL), F32)],
        in_specs=[_full((m, D_MODEL)), _full((1, D_MODEL)), _full((m, D_MODEL)), _full((m, D_MODEL)),
                  _full((D_MODEL, D_MODEL))],
        out_specs=[_full((D_MODEL, D_MODEL)), _full((1, D_MODEL))],
        compiler_params=_params(32),
    )(mem, g_mem, mn, dmkv, w_mkv)


def _dh_bwd(dparts, x, dy, g_pre, w_int):
    s = x.shape[0]
    tm = min(256, s)

    def body(*refs):
        d_refs = refs[:7]
        x_ref, dy_ref, g_ref, w_hbm, gx_ref, gg_ref, w_vm, sems = refs[7:]
        _load_once([(w_hbm, w_vm)], sems)

        @pl.when(pl.program_id(0) == 0)
        def _():
            gg_ref[...] = jnp.zeros_like(gg_ref)

        d_h = jnp.zeros((tm, D_MODEL), F32)
        for d_ref, (r0, width) in zip(d_refs, SEGS):
            for c0 in range(0, width, 512):
                cw = min(512, width - c0)
                d_h += _dot(d_ref[:, c0:c0 + cw], w_vm[r0 + c0:r0 + c0 + cw, :])
        xf = x_ref[...]
        r = lax.rsqrt(jnp.mean(xf * xf, axis=-1, keepdims=True) + EPS)
        xn = xf * r
        a = d_h * g_ref[...]
        gx_ref[...] = r * (a - xn * jnp.mean(a * xn, axis=-1, keepdims=True)) + dy_ref[...]
        gg_ref[...] += jnp.sum(d_h * xn, axis=0, keepdims=True)

    return pl.pallas_call(
        body, name="dh_bwd", grid=(s // tm,),
        out_shape=[jax.ShapeDtypeStruct((s, D_MODEL), F32), jax.ShapeDtypeStruct((1, D_MODEL), F32)],
        in_specs=[_rows(tm, w) for _, w in SEGS] + [_rows(tm, D_MODEL), _rows(tm, D_MODEL), _full((1, D_MODEL)), ANY],
        out_specs=[_rows(tm, D_MODEL), _full((1, D_MODEL))],
        scratch_shapes=[pltpu.VMEM((IN_WIDTH, D_MODEL), BF16), pltpu.SemaphoreType.DMA((1,))],
        compiler_params=_params(52),
    )(*dparts, x, dy, g_pre, w_int)


def _gw_in(dparts, h):
    s = h.shape[0]
    tn = 256
    starts, counts = [], []
    for r0, width in SEGS:
        starts.append(r0 // tn)
        counts.append(width // tn)

    def body(*refs):
        d_refs = refs[:7]
        h_hbm, o_ref, h_vm, sems = refs[7:]
        _load_once([(h_hbm, h_vm)], sems)
        j = pl.program_id(0)
        for d_ref, st, cnt in zip(d_refs, starts, counts):
            @pl.when((j >= st) & (j < st + cnt))
            def _(d_ref=d_ref):
                o_ref[...] = _dot_tn(d_ref[...], h_vm[...]).astype(BF16)

    def seg_spec(st, cnt):
        return pl.BlockSpec((s, tn), lambda j: (0, jnp.clip(j - st, 0, cnt - 1)))

    return pl.pallas_call(
        body, name="gw_in", grid=(IN_WIDTH // tn,),
        out_shape=jax.ShapeDtypeStruct((IN_WIDTH, D_MODEL), BF16),
        in_specs=[seg_spec(st, cnt) for st, cnt in zip(starts, counts)] + [ANY],
        out_specs=pl.BlockSpec((tn, D_MODEL), lambda j: (j, 0)),
        scratch_shapes=[pltpu.VMEM((s, D_MODEL), BF16), pltpu.SemaphoreType.DMA((1,))],
        compiler_params=_params(52),
    )(*dparts, h)


def _adamw_math(w, g, m, v):
    m2 = ADAM_B1 * m + (1.0 - ADAM_B1) * g
    v2 = ADAM_B2 * v + (1.0 - ADAM_B2) * (g * g)
    m_hat = m2 / (1.0 - ADAM_B1 ** ADAM_STEP)
    v_hat = v2 / (1.0 - ADAM_B2 ** ADAM_STEP)
    delta = -ADAM_LR * (m_hat / (jnp.sqrt(v_hat) + ADAM_EPS) + ADAM_WD * w)
    return delta, m2, v2


def _sum_sources(parts, name, rows_tile):
    _, r, c = parts.shape

    def body(p_ref, o_ref):
        acc = p_ref[0].astype(F32)
        for k in range(1, N_DEV):
            acc += p_ref[k].astype(F32)
        o_ref[...] = acc

    return pl.pallas_call(
        body, name=name, grid=(r // rows_tile,),
        out_shape=jax.ShapeDtypeStruct((r, c), F32),
        in_specs=[pl.BlockSpec((N_DEV, rows_tile, c), lambda i: (0, i, 0))],
        out_specs=pl.BlockSpec((rows_tile, c), lambda i: (i, 0)),
        compiler_params=_params(48),
    )(parts)


def _adamw(g, w, m, v, name, rows_tile):
    r, c = w.shape

    def body(g_ref, w_ref, m_ref, v_ref, d_ref, m2_ref, v2_ref):
        d_ref[...], m2_ref[...], v2_ref[...] = _adamw_math(w_ref[...], g_ref[...], m_ref[...], v_ref[...])

    spec = pl.BlockSpec((rows_tile, c), lambda i: (i, 0))
    return pl.pallas_call(
        body, name=name, grid=(r // rows_tile,),
        out_shape=[jax.ShapeDtypeStruct((r, c), F32)] * 3,
        in_specs=[spec] * 4, out_specs=[spec] * 3,
        compiler_params=_params(48),
    )(g, w, m, v)


def _sum_adamw(parts, block, w, m, v, name):
    r, c = w.shape

    def body(p_ref, w_ref, m_ref, v_ref, g_ref, d_ref, m2_ref, v2_ref):
        g = p_ref[0].astype(F32)
        for k in range(1, N_DEV):
            g += p_ref[k].astype(F32)
        g_ref[...] = g
        d_ref[...], m2_ref[...], v2_ref[...] = _adamw_math(w_ref[...], g, m_ref[...], v_ref[...])

    return pl.pallas_call(
        body, name=name, grid=(1,),
        out_shape=[jax.ShapeDtypeStruct((r, c), F32)] * 4,
        in_specs=[pl.BlockSpec((N_DEV, r, c), lambda i: (0, block, 0))] + [_full((r, c))] * 3,
        out_specs=[_full((r, c))] * 4,
        compiler_params=_params(32),
    )(parts, w, m, v)


def _pack_sum(packs):
    def body(p_ref, o_ref):
        acc = p_ref[0]
        for k in range(1, N_DEV):
            acc += p_ref[k]
        o_ref[...] = acc

    return pl.pallas_call(
        body, name="pack_sum", grid=(1,),
        out_shape=jax.ShapeDtypeStruct((8, D_MODEL), F32),
        in_specs=[_full((N_DEV, 8, D_MODEL))], out_specs=_full((8, D_MODEL)),
    )(packs)


def _small_adamw(ws, gs, ms, vs):
    k = len(ws)

    def body(*refs):
        w_refs, g_refs, m_refs, v_refs = (refs[j * k:(j + 1) * k] for j in range(4))
        outs = refs[4 * k:]
        for j in range(k):
            outs[j][...], outs[k + j][...], outs[2 * k + j][...] = _adamw_math(
                w_refs[j][...], g_refs[j][...], m_refs[j][...], v_refs[j][...])

    specs = [_full(w.shape) for w in ws]
    res = pl.pallas_call(
        body, name="small_adamw", grid=(1,),
        out_shape=[jax.ShapeDtypeStruct(w.shape, F32) for w in ws] * 3,
        in_specs=specs * 4, out_specs=specs * 3,
    )(*ws, *gs, *ms, *vs)
    return res[:k], res[k:2 * k], res[2 * k:]


def kernel(x, mem, g_pre, w_in, w_conv, attn_sink, g_mem, w_mem_kv, w_up_a, w_up_b, w_up_m, w_out, g_post, loss_target, m_g_pre, m_w_in, m_w_conv, m_attn_sink, m_g_mem, m_w_mem_kv, m_w_up_a, m_w_up_b, m_w_up_m, m_w_out, m_g_post, v_g_pre, v_w_in, v_w_conv, v_attn_sink, v_g_mem, v_w_mem_kv, v_w_up_a, v_w_up_b, v_w_up_m, v_w_out, v_g_post):
    s = x.shape[1]
    x2, mem2, tgt2 = x[0], mem[0], loss_target[0]
    me = 4 * lax.axis_index("x") + 2 * lax.axis_index("y") + lax.axis_index("c")

    w_up_loc = jnp.concatenate([w_up_a[0], w_up_b[0], w_up_m[0]], axis=0).astype(BF16)
    w_conv_loc = jnp.zeros((8, 128), F32).at[:3, :64].set(w_conv[0])
    w_int_g, w_mkv_g, w_out_g, w_up_g, w_conv_g = _all_gather(
        [w_in[0].T.astype(BF16), w_mem_kv[0].astype(BF16), w_out[0].astype(BF16), w_up_loc, w_conv_loc],
        "gather_weights")
    w_int = w_int_g.reshape(IN_WIDTH, D_MODEL)
    w_mkv = w_mkv_g.reshape(D_MODEL, D_MODEL)
    w_out_f = w_out_g.reshape(D_MODEL, D_MODEL)
    w_conv_f = w_conv_g[:, :3, :64].transpose(1, 0, 2).reshape(3, 512)
    sink = attn_sink[0]
    tabs = _rope_tables(s)

    h, pa, pq, pkv, pbz, pmq, pmz, pg = _proj_fwd(x2, g_pre, w_int, tabs)
    mn, mkv = _mem_kv_fwd(mem2, g_mem, w_mkv)
    ya = _conv_fwd(pa, w_conv_f)
    yb = _attn_fwd(pq, pkv, pbz, sink)
    ym = _mem_attn_fwd(pmq, pmz, mkv)
    dg, dya, dyb, dym, dy, loss_p, gg_post, gw_out, gw_up = _mid(ya, yb, ym, pg, x2, tgt2, g_post, w_up_g, w_out_f)

    da, gw_conv = _conv_bwd(pa, dya, w_conv_f)
    dq, dbz, dkv, g_sink = _attn_bwd(pq, pkv, pbz, dyb, sink, tabs)
    dmq, dmz, dmkv = _mem_attn_bwd(pmq, pmz, mkv, dym)
    gw_mkv, gg_mem = _mem_kv_bwd(mem2, g_mem, mn, dmkv, w_mkv)
    dparts = (da, dq, dkv, dbz, dmq, dmz, dg)
    grad_x, gg_pre = _dh_bwd(dparts, x2, dy, g_pre, w_int)
    gw_int = _gw_in(dparts, h)

    r_int, r_mkv, r_out, r_up = _all_to_all(
        [gw_int.reshape(N_DEV, SHARD_IN, D_MODEL), gw_mkv.reshape(N_DEV, 128, D_MODEL),
         gw_out.reshape(N_DEV, 128, D_MODEL), gw_up], "scatter_grads")
    row3 = jnp.concatenate([gw_conv[0:1], gw_conv[1:2]], axis=1)
    row4 = jnp.concatenate([gw_conv[2:3], g_sink[:, 0].reshape(1, 8), loss_p[0:1, 0:1],
                            jnp.zeros((1, 512 - 9), F32)], axis=1)
    pack = jnp.concatenate([gg_pre, gg_mem, gg_post, row3, row4, jnp.zeros((3, D_MODEL), F32)], axis=0)
    (packs,) = _all_gather([pack], "gather_small")
    tot = _pack_sum(packs)

    g_w_in = _sum_sources(r_int, "sum_w_in", SHARD_IN // 2).T
    d_w_in, nm_w_in, nv_w_in = _adamw(g_w_in, w_in[0], m_w_in[0], v_w_in[0], "adamw_w_in", 256)
    g_mkv, d_mkv, nm_mkv, nv_mkv = _sum_adamw(r_mkv, 0, w_mem_kv[0], m_w_mem_kv[0], v_w_mem_kv[0], "adamw_w_mem_kv")
    g_out, d_out, nm_out, nv_out = _sum_adamw(r_out, 0, w_out[0], m_w_out[0], v_w_out[0], "adamw_w_out")
    up = [_sum_adamw(r_up, k, w[0], m[0], v[0], "adamw_w_up_" + "abm"[k])
          for k, (w, m, v) in enumerate([(w_up_a, m_w_up_a, v_w_up_a), (w_up_b, m_w_up_b, v_w_up_b),
                                         (w_up_m, m_w_up_m, v_w_up_m)])]

    g_g_pre, g_g_mem, g_g_post = tot[0:1], tot[1:2], tot[2:3]
    g_conv_full = jnp.concatenate([tot[3:4, 0:512], tot[3:4, 512:1024], tot[4:5, 0:512]], axis=0)
    g_conv = lax.dynamic_slice(g_conv_full, (0, 64 * me), (3, 64))
    g_sink_tot = tot[4:5, 512:520]
    loss = tot[4, 520]
    small_w = [g_pre, w_conv[0], attn_sink, g_mem, g_post]
    small_g = [g_g_pre, g_conv, g_sink_tot, g_g_mem, g_g_post]
    small_m = [m_g_pre, m_w_conv[0], m_attn_sink, m_g_mem, m_g_post]
    small_v = [v_g_pre, v_w_conv[0], v_attn_sink, v_g_mem, v_g_post]
    sd, sm, sv = _small_adamw(small_w, small_g, small_m, small_v)

    def lead(a):
        return a[None]

    grads = [g_g_pre, lead(g_w_in), lead(g_conv), g_sink_tot, g_g_mem, lead(g_mkv), lead(up[0][0]),
             lead(up[1][0]), lead(up[2][0]), lead(g_out), g_g_post]

    def assemble(small, big_in, big_mkv, big_up, big_out):
        return [small[0], lead(big_in), lead(small[1]), small[2], small[3], lead(big_mkv), lead(big_up[0]),
                lead(big_up[1]), lead(big_up[2]), lead(big_out), small[4]]

    deltas = assemble(sd, d_w_in, d_mkv, [u[1] for u in up], d_out)
    new_m = assemble(sm, nm_w_in, nm_mkv, [u[2] for u in up], nm_out)
    new_v = assemble(sv, nv_w_in, nv_mkv, [u[3] for u in up], nv_out)
    return (loss, grad_x[None], *grads, *deltas, *new_m, *new_v)
```

```python
import functools

import jax
import jax.numpy as jnp
from jax import lax
from jax.experimental import pallas as pl
from jax.experimental.pallas import tpu as pltpu

F32 = jnp.float32
BF16 = jnp.bfloat16
MESH = pl.DeviceIdType.MESH

N_DEV = 8
D_MODEL = 1024
EPS = 1e-6
ROPE_THETA = 500000.0
ROT_DIM = 16
HEAD_DIM = 64
ATTN_BLOCK = 128
MEM_HEADS = 4
MEM_HEAD_DIM = 128
ATTN_SCALE = HEAD_DIM ** -0.5
MEM_SCALE = MEM_HEAD_DIM ** -0.5

ADAM_LR = 0.001
ADAM_B1 = 0.9
ADAM_B2 = 0.999
ADAM_EPS = 1e-08
ADAM_WD = 0.01
ADAM_STEP = 10

SEG_A = (0, 2048)
SEG_BQ = (2048, 512)
SEG_BKV = (2560, 256)
SEG_BZ = (2816, 512)
SEG_MQ = (3328, 512)
SEG_MZ = (3840, 512)
SEG_G = (4352, 3072)
SEGS = (SEG_A, SEG_BQ, SEG_BKV, SEG_BZ, SEG_MQ, SEG_MZ, SEG_G)
IN_WIDTH = 7424
SHARD_IN = IN_WIDTH // N_DEV

V7X_VMEM_BYTES = 64 * 1024 * 1024
ANY = pl.BlockSpec(memory_space=pl.ANY)


def _params(vmem_mb):
    assert vmem_mb * 1024 * 1024 < V7X_VMEM_BYTES
    return pltpu.CompilerParams(dimension_semantics=("arbitrary",), vmem_limit_bytes=vmem_mb * 1024 * 1024)


def _full(shape):
    zeros = (0,) * len(shape)
    return pl.BlockSpec(shape, lambda i: zeros)


def _rows(tm, width):
    return pl.BlockSpec((tm, width), lambda i: (i, 0))


def _dot(a, b):
    return jnp.dot(a, b, preferred_element_type=F32)


def _dot_nt(a, b):
    return lax.dot_general(a, b, (((1,), (1,)), ((), ())), preferred_element_type=F32)


def _dot_tn(a, b):
    return lax.dot_general(a, b, (((0,), (0,)), ((), ())), preferred_element_type=F32)


def _sigmoid(z):
    return 1.0 / (1.0 + jnp.exp(-z))


def _rope(t, cs, s1, s2):
    return t * cs + pltpu.roll(t, 120, 1) * s1 + pltpu.roll(t, 8, 1) * s2


def _rope_t(d, cs, s1, s2):
    return d * cs + pltpu.roll(d * s1, 8, 1) + pltpu.roll(d * s2, 120, 1)


def _rope_tables(s):
    half = ROT_DIM // 2
    inv_freq = jnp.power(jnp.float32(ROPE_THETA), -jnp.arange(half, dtype=F32) * (2.0 / ROT_DIM))
    d = jnp.arange(128) % HEAD_DIM
    ang = jnp.arange(s).astype(F32)[:, None] * inv_freq[d % half][None, :]
    cos, sin = jnp.cos(ang), jnp.sin(ang)
    lo, hi = (d < half)[None, :], ((d >= half) & (d < ROT_DIM))[None, :]
    return jnp.where(lo | hi, cos, 1.0), jnp.where(lo, -sin, 0.0), jnp.where(hi, sin, 0.0)


def _load_once(pairs, sems):
    @pl.when(pl.program_id(0) == 0)
    def _():
        cps = [pltpu.make_async_copy(src, dst, sems.at[k]) for k, (src, dst) in enumerate(pairs)]
        for cp in cps:
            cp.start()
        for cp in cps:
            cp.wait()


def _my_place():
    x, y, c = lax.axis_index("x"), lax.axis_index("y"), lax.axis_index("c")
    return x, y, c


def _all_gather(arrs, name):
    n = len(arrs)

    def body(*refs):
        ins, outs = refs[:n], refs[n:2 * n]
        send_sems, recv_sems, local_sems = refs[2 * n:]
        x, y, c = _my_place()
        me, sibling = (x, y, c), (x, y, 1 - c)
        chips = [(1 - x, y), (x, 1 - y), (1 - x, 1 - y)]

        def idx(px, py, pc):
            return 4 * px + 2 * py + pc

        def copy(a, k, block, to, src=None):
            dst = outs[a].at[idx(*block)]
            return pltpu.make_async_remote_copy(
                src_ref=dst if src is None else src, dst_ref=dst,
                send_sem=send_sems.at[a * 7 + k], recv_sem=recv_sems.at[a * 7 + k],
                device_id=to, device_id_type=MESH)

        mine = [pltpu.make_async_copy(ins[a], outs[a].at[idx(*me)], local_sems.at[a]) for a in range(n)]
        for cp in mine:
            cp.start()
        first = []
        for a in range(n):
            first.append(copy(a, 0, me, sibling, src=ins[a]))
        for j, chip in enumerate(chips):
            for a in range(n):
                first.append(copy(a, 1 + j, me, (*chip, c), src=ins[a]))
        for cp in first:
            cp.start()
        passed = []
        for j, chip in enumerate(chips):
            for a in range(n):
                copy(a, 1 + j, (*chip, c), me).wait_recv()
                cp = copy(a, 4 + j, (*chip, c), sibling)
                cp.start()
                passed.append(cp)
        for a in range(n):
            copy(a, 0, sibling, me).wait_recv()
        for j, chip in enumerate(chips):
            for a in range(n):
                copy(a, 4 + j, (*chip, 1 - c), me).wait_recv()
        for cp in first + passed:
            cp.wait_send()
        for cp in mine:
            cp.wait()

    return pl.pallas_call(
        body, name=name,
        out_shape=[jax.ShapeDtypeStruct((N_DEV,) + a.shape, a.dtype) for a in arrs],
        in_specs=[ANY] * n, out_specs=[ANY] * n,
        scratch_shapes=[pltpu.SemaphoreType.DMA((7 * n,)), pltpu.SemaphoreType.DMA((7 * n,)),
                        pltpu.SemaphoreType.DMA((n,))],
    )(*arrs)


N_CHIPS = 4


def _sibling_exchange(arrs, name):
    n = len(arrs)

    def body(*refs):
        ins, outs = refs[:n], refs[n:2 * n]
        send_sems, recv_sems = refs[2 * n:]
        x, y, c = _my_place()
        sibling = (x, y, 1 - c)

        def copy(a, j):
            return pltpu.make_async_remote_copy(
                src_ref=ins[a].at[2 * j + (1 - c)], dst_ref=outs[a].at[j],
                send_sem=send_sems.at[a * N_CHIPS + j], recv_sem=recv_sems.at[a * N_CHIPS + j],
                device_id=sibling, device_id_type=MESH)

        cps = [copy(a, j) for j in range(N_CHIPS) for a in range(n)]
        for cp in cps:
            cp.start()
        for cp in cps:
            cp.wait_recv()
        for cp in cps:
            cp.wait_send()

    return pl.pallas_call(
        body, name=name,
        out_shape=[jax.ShapeDtypeStruct((N_CHIPS,) + a.shape[1:], a.dtype) for a in arrs],
        in_specs=[ANY] * n, out_specs=[ANY] * n,
        scratch_shapes=[pltpu.SemaphoreType.DMA((N_CHIPS * n,)), pltpu.SemaphoreType.DMA((N_CHIPS * n,))],
    )(*arrs)


def _pair_add(mine, recv, core, name):
    n = len(mine)

    def body(c_ref, *refs):
        for a in range(n):
            refs[2 * n + a][...] = (refs[a][...].astype(F32) + refs[n + a][...].astype(F32)).astype(BF16)

    def blk(a):
        return (None,) + a.shape[1:]

    grid_spec = pltpu.PrefetchScalarGridSpec(
        num_scalar_prefetch=1, grid=(N_CHIPS,),
        in_specs=[pl.BlockSpec(blk(a), lambda j, c_ref: (2 * j + c_ref[0], 0, 0)) for a in mine]
        + [pl.BlockSpec(blk(a), lambda j, c_ref: (j, 0, 0)) for a in recv],
        out_specs=[pl.BlockSpec(blk(a), lambda j, c_ref: (j, 0, 0)) for a in recv])
    return pl.pallas_call(
        body, name=name, grid_spec=grid_spec,
        out_shape=[jax.ShapeDtypeStruct(a.shape, BF16) for a in recv],
        compiler_params=_params(32),
    )(core, *mine, *recv)


def _chip_exchange(arrs, name):
    n = len(arrs)

    def body(*refs):
        ins, outs = refs[:n], refs[n:2 * n]
        send_sems, recv_sems, local_sems = refs[2 * n:]
        x, y, c = _my_place()
        my_chip = 2 * x + y
        peers = [(x, 1 - y), (1 - x, y), (1 - x, 1 - y)]

        mine = [pltpu.make_async_copy(ins[a].at[my_chip], outs[a].at[my_chip], local_sems.at[a]) for a in range(n)]
        for cp in mine:
            cp.start()

        def copy(a, k, dst_slot):
            px, py = peers[k]
            return pltpu.make_async_remote_copy(
                src_ref=ins[a].at[2 * px + py], dst_ref=outs[a].at[dst_slot],
                send_sem=send_sems.at[a * 3 + k], recv_sem=recv_sems.at[a * 3 + k],
                device_id=(px, py, c), device_id_type=MESH)

        sent = [copy(a, k, my_chip) for k in range(3) for a in range(n)]
        for cp in sent:
            cp.start()
        for k in range(3):
            px, py = peers[k]
            for a in range(n):
                copy(a, k, 2 * px + py).wait_recv()
        for cp in sent:
            cp.wait_send()
        for cp in mine:
            cp.wait()

    return pl.pallas_call(
        body, name=name,
        out_shape=[jax.ShapeDtypeStruct(a.shape, a.dtype) for a in arrs],
        in_specs=[ANY] * n, out_specs=[ANY] * n,
        scratch_shapes=[pltpu.SemaphoreType.DMA((3 * n,)), pltpu.SemaphoreType.DMA((3 * n,)),
                        pltpu.SemaphoreType.DMA((n,))],
    )(*arrs)


def _proj_fwd(x, g_pre, w_int, tabs):
    s = x.shape[0]
    tm = min(512, s)

    def body(x_ref, g_ref, cs_ref, s1_ref, s2_ref, w_hbm,
             h_ref, pa_ref, pq_ref, pkv_ref, pbz_ref, pmq_ref, pmz_ref, pg_ref, w_vm, sems):
        _load_once([(w_hbm, w_vm)], sems)
        xf = x_ref[...]
        r = lax.rsqrt(jnp.mean(xf * xf, axis=-1, keepdims=True) + EPS)
        h = ((xf * r) * g_ref[...]).astype(BF16)
        h_ref[...] = h
        cs, s1, s2 = cs_ref[...], s1_ref[...], s2_ref[...]

        def mm(seg, c0, width):
            return _dot_nt(h, w_vm[seg[0] + c0:seg[0] + c0 + width, :])

        for c0 in range(0, SEG_A[1], 512):
            pa_ref[:, c0:c0 + 512] = mm(SEG_A, c0, 512).astype(BF16)
        q = mm(SEG_BQ, 0, 512)
        for b in range(4):
            pq_ref[:, 128 * b:128 * b + 128] = _rope(q[:, 128 * b:128 * b + 128], cs, s1, s2).astype(BF16)
        kv = mm(SEG_BKV, 0, 256)
        pkv_ref[:, 0:128] = _rope(kv[:, 0:128], cs, s1, s2).astype(BF16)
        pkv_ref[:, 128:256] = kv[:, 128:256].astype(BF16)
        pbz_ref[...] = mm(SEG_BZ, 0, 512).astype(BF16)
        pmq_ref[...] = mm(SEG_MQ, 0, 512).astype(BF16)
        pmz_ref[...] = mm(SEG_MZ, 0, 512).astype(BF16)
        for c0 in range(0, SEG_G[1], 512):
            pg_ref[:, c0:c0 + 512] = mm(SEG_G, c0, 512).astype(BF16)

    widths = (D_MODEL, 2048, 512, 256, 512, 512, 512, 3072)
    return pl.pallas_call(
        body, name="proj_fwd", grid=(s // tm,),
        out_shape=[jax.ShapeDtypeStruct((s, w), BF16) for w in widths],
        in_specs=[_rows(tm, D_MODEL), _full((1, D_MODEL)), _rows(tm, 128), _rows(tm, 128), _rows(tm, 128), ANY],
        out_specs=[_rows(tm, w) for w in widths],
        scratch_shapes=[pltpu.VMEM((IN_WIDTH, D_MODEL), BF16), pltpu.SemaphoreType.DMA((1,))],
        compiler_params=_params(52),
    )(x, g_pre, *tabs, w_int)


def _mem_kv_fwd(mem, g_mem, w_mkv):
    m = mem.shape[0]

    def body(mem_ref, g_ref, w_ref, mn_ref, mkv_ref):
        xf = mem_ref[...]
        r = lax.rsqrt(jnp.mean(xf * xf, axis=-1, keepdims=True) + EPS)
        mn = ((xf * r) * g_ref[...]).astype(BF16)
        mn_ref[...] = mn
        mkv_ref[...] = _dot(mn, w_ref[...]).astype(BF16)

    return pl.pallas_call(
        body, name="mem_kv_fwd", grid=(1,),
        out_shape=[jax.ShapeDtypeStruct((m, D_MODEL), BF16)] * 2,
        in_specs=[_full((m, D_MODEL)), _full((1, D_MODEL)), _full((D_MODEL, D_MODEL))],
        out_specs=[_full((m, D_MODEL))] * 2,
        compiler_params=_params(32),
    )(mem, g_mem, w_mkv)


def _halo_specs(s, tm, rows, width):
    nblk = s // rows
    prev = pl.BlockSpec((rows, width), lambda i: (jnp.maximum(i * (tm // rows) - 1, 0), 0))
    nxt = pl.BlockSpec((rows, width), lambda i: (jnp.minimum((i + 1) * (tm // rows), nblk - 1), 0))
    return prev, nxt


def _conv_common(pa, prev_row, next_row, w, first, last, tm):
    b, c, u, z = (pa[:, 512 * k:512 * k + 512] for k in range(4))
    cu = c * u
    cu_prev = jnp.where(first, 0.0, prev_row[:, 512:1024] * prev_row[:, 1024:1536])
    cu_next = jnp.where(last, 0.0, next_row[:, 512:1024] * next_row[:, 1024:1536])
    row = lax.broadcasted_iota(jnp.int32, (tm, 512), 0)
    cu_m1 = jnp.where(row == 0, cu_prev, pltpu.roll(cu, 1, 0))
    cu_p1 = jnp.where(row == tm - 1, cu_next, pltpu.roll(cu, tm - 1, 0))
    y = cu_m1 * w[0:1] + cu * w[1:2] + cu_p1 * w[2:3]
    sig = _sigmoid(z)
    return b, c, u, z, cu, cu_m1, cu_p1, y, sig, row


def _conv_fwd(pa, w_conv):
    s = pa.shape[0]
    tm = min(512, s)
    nt = s // tm

    def body(pa_ref, pp_ref, pn_ref, w_ref, ya_ref):
        i = pl.program_id(0)
        prev_row = pp_ref[...].astype(F32)[15:16, :]
        next_row = pn_ref[...].astype(F32)[0:1, :]
        b, _, _, z, _, _, _, y, sig, _ = _conv_common(
            pa_ref[...].astype(F32), prev_row, next_row, w_ref[...], i == 0, i == nt - 1, tm)
        ya_ref[...] = (b * y * (z * sig)).astype(BF16)

    prev, nxt = _halo_specs(s, tm, 16, 2048)
    return pl.pallas_call(
        body, name="conv_fwd", grid=(nt,),
        out_shape=jax.ShapeDtypeStruct((s, 512), BF16),
        in_specs=[_rows(tm, 2048), prev, nxt, _full((3, 512))],
        out_specs=_rows(tm, 512),
        compiler_params=_params(48),
    )(pa, pa, pa, w_conv)


def _stack_heads(a, g, lane):
    in_g = (lane >= 64 * g) & (lane < 64 * g + 64)
    parts = []
    for j in range(4):
        h = 4 * g + j
        blk = a[:, 128 * (h // 2):128 * (h // 2) + 128]
        if h % 2 != g:
            blk = pltpu.roll(blk, 64, 1)
        parts.append(jnp.where(in_g, blk, 0.0))
    return jnp.concatenate(parts, axis=0)


def _unstack_heads(o0, o1, lane):
    blocks = []
    for b in range(4):
        g = b // 2
        og = (o0, o1)[g]
        je, jo = (2 * b) % 4, (2 * b + 1) % 4
        even, odd = og[128 * je:128 * je + 128], og[128 * jo:128 * jo + 128]
        if g == 0:
            odd = pltpu.roll(odd, 64, 1)
        else:
            even = pltpu.roll(even, 64, 1)
        blocks.append(jnp.where(lane < 64, even, odd))
    return jnp.concatenate(blocks, axis=1)


def _window_mask(n, s):
    r = lax.broadcasted_iota(jnp.int32, (512, 384), 0) & 127
    c = lax.broadcasted_iota(jnp.int32, (512, 384), 1)
    kpos = c + (n - 1) * ATTN_BLOCK
    return (c >= r) & (c <= r + 2 * ATTN_BLOCK) & (kpos >= 0) & (kpos < s)


def _sink_col(sink_ref, g):
    return jnp.concatenate([jnp.full((ATTN_BLOCK, 1), sink_ref[4 * g + j], F32) for j in range(4)], axis=0)


def _attn_group(qg, kw, vw, sink, valid):
    sc = jnp.where(valid, _dot_nt(qg, kw) * ATTN_SCALE, -jnp.inf)
    m = jnp.maximum(jnp.max(sc, axis=-1, keepdims=True), sink)
    e = jnp.exp(sc - m)
    es = jnp.exp(sink - m)
    denom = jnp.sum(e, axis=-1, keepdims=True) + es
    p = e / denom
    o = _dot(p.astype(BF16), vw)
    return p, o, es / denom


def _fill_padded(kv_ref, kpad, vpad, s):
    zero = jnp.zeros((ATTN_BLOCK, 128), BF16)
    kpad[0:ATTN_BLOCK, :] = zero
    vpad[0:ATTN_BLOCK, :] = zero
    kpad[ATTN_BLOCK + s:2 * ATTN_BLOCK + s, :] = zero
    vpad[ATTN_BLOCK + s:2 * ATTN_BLOCK + s, :] = zero
    kpad[ATTN_BLOCK:ATTN_BLOCK + s, :] = kv_ref[:, 0:128]
    vpad[ATTN_BLOCK:ATTN_BLOCK + s, :] = kv_ref[:, 128:256]


def _attn_fwd(pq, pkv, pbz, sink):
    s = pq.shape[0]
    nb = s // ATTN_BLOCK

    def body(sink_ref, q_ref, z_ref, kv_ref, yb_ref, kpad, vpad):
        n = pl.program_id(0)

        @pl.when(n == 0)
        def _():
            _fill_padded(kv_ref, kpad, vpad, s)

        lane = lax.broadcasted_iota(jnp.int32, (ATTN_BLOCK, 128), 1)
        start = pl.multiple_of(n * ATTN_BLOCK, ATTN_BLOCK)
        kw, vw = kpad[pl.ds(start, 384), :], vpad[pl.ds(start, 384), :]
        qf = q_ref[...].astype(F32)
        valid = _window_mask(n, s)
        outs = []
        for g in range(2):
            qg = _stack_heads(qf, g, lane).astype(BF16)
            _, o, _ = _attn_group(qg, kw, vw, _sink_col(sink_ref, g), valid)
            outs.append(o)
        attn = _unstack_heads(outs[0], outs[1], lane)
        z = z_ref[...].astype(F32)
        yb_ref[...] = (attn * (z * _sigmoid(z))).astype(BF16)

    return pl.pallas_call(
        body, name="attn_fwd", grid=(nb,),
        out_shape=jax.ShapeDtypeStruct((s, 512), BF16),
        in_specs=[pl.BlockSpec(memory_space=pltpu.SMEM), _rows(ATTN_BLOCK, 512), _rows(ATTN_BLOCK, 512),
                  _full((s, 256))],
        out_specs=_rows(ATTN_BLOCK, 512),
        scratch_shapes=[pltpu.VMEM((s + 2 * ATTN_BLOCK, 128), BF16)] * 2,
        compiler_params=_params(32),
    )(sink, pq, pbz, pkv)


def _mem_softmax(q, mk):
    sc = _dot_nt(q, mk) * MEM_SCALE
    e = jnp.exp(sc - jnp.max(sc, axis=-1, keepdims=True))
    return e / jnp.sum(e, axis=-1, keepdims=True)


def _mem_attn_fwd(pmq, pmz, mkv):
    s = pmq.shape[0]
    m = mkv.shape[0]
    tm = min(512, s)

    def body(q_ref, z_ref, mk_ref, mv_ref, ym_ref):
        z = z_ref[...].astype(F32)
        sz = z * _sigmoid(z)
        for h in range(MEM_HEADS):
            cols = slice(128 * h, 128 * h + 128)
            p = _mem_softmax(q_ref[:, cols], mk_ref[:, cols])
            o = _dot(p.astype(BF16), mv_ref[:, cols])
            ym_ref[:, cols] = (o * sz[:, cols]).astype(BF16)

    return pl.pallas_call(
        body, name="mem_attn_fwd", grid=(s // tm,),
        out_shape=jax.ShapeDtypeStruct((s, 512), BF16),
        in_specs=[_rows(tm, 512), _rows(tm, 512), pl.BlockSpec((m, 512), lambda i: (0, 0)),
                  pl.BlockSpec((m, 512), lambda i: (0, 1))],
        out_specs=_rows(tm, 512),
        compiler_params=_params(32),
    )(pmq, pmz, mkv, mkv)


def _mid(ya, yb, ym, pg, x, target, g_post, w_up, w_out):
    s = x.shape[0]
    tm = min(256, s)
    nt = s // tm

    def body(ya_ref, yb_ref, ym_ref, pg_ref, x_ref, t_ref, gp_ref, wup_hbm, wout_hbm,
             dg_ref, dya_ref, dyb_ref, dym_ref, dy_ref, loss_ref, ggp_ref, gwout_hbm, gwup_hbm,
             wup_vm, wout_vm, acc_out, acc_up, st_out, st_up, sems):
        i = pl.program_id(0)
        _load_once([(wup_hbm.at[d], wup_vm.at[:, pl.ds(128 * d, 128)]) for d in range(N_DEV)]
                   + [(wout_hbm, wout_vm)], sems)

        @pl.when(i == 0)
        def _():
            acc_out[...] = jnp.zeros_like(acc_out)
            acc_up[...] = jnp.zeros_like(acc_up)
            loss_ref[...] = jnp.zeros_like(loss_ref)
            ggp_ref[...] = jnp.zeros_like(ggp_ref)

        ys = (ya_ref[...], yb_ref[...], ym_ref[...])
        us = [_dot(ys[k], wup_vm[512 * k:512 * k + 512, :]) for k in range(3)]
        gates = [_sigmoid(pg_ref[:, 1024 * k:1024 * k + 1024].astype(F32)) for k in range(3)]
        merged = gates[0] * us[0] + gates[1] * us[1] + gates[2] * us[2]
        mb = merged.astype(BF16)
        out = _dot(mb, wout_vm[...])
        r = lax.rsqrt(jnp.mean(out * out, axis=-1, keepdims=True) + EPS)
        on = out * r
        gp = gp_ref[...]
        err = (x_ref[...] + on * gp) - t_ref[...]
        loss_ref[...] += 0.5 * jnp.sum(err * err) * (1.0 / D_MODEL)
        dy = err * (1.0 / D_MODEL)
        dy_ref[...] = dy
        ggp_ref[...] += jnp.sum(dy * on, axis=0, keepdims=True)
        a = dy * gp
        d_out = r * (a - on * jnp.mean(a * on, axis=-1, keepdims=True))
        dob = d_out.astype(BF16)
        acc_out[...] += _dot_tn(mb, dob)
        d_merged = _dot_nt(dob, wout_vm[...])
        d_refs = (dya_ref, dyb_ref, dym_ref)
        for k in range(3):
            g = gates[k]
            dg_ref[:, 1024 * k:1024 * k + 1024] = (d_merged * us[k] * g * (1.0 - g)).astype(BF16)
            du = (d_merged * g).astype(BF16)
            d_refs[k][...] = _dot_nt(du, wup_vm[512 * k:512 * k + 512, :])
            acc_up[512 * k:512 * k + 512, :] += _dot_tn(ys[k], du)

        @pl.when(i == nt - 1)
        def _():
            st_out[...] = acc_out[...].astype(BF16)
            for d in range(N_DEV):
                st_up[d] = acc_up[:, 128 * d:128 * d + 128].astype(BF16)
            cps = [pltpu.make_async_copy(st_out, gwout_hbm, sems.at[0]),
                   pltpu.make_async_copy(st_up, gwup_hbm, sems.at[1])]
            for cp in cps:
                cp.start()
            for cp in cps:
                cp.wait()

    return pl.pallas_call(
        body, name="mid", grid=(nt,),
        out_shape=[jax.ShapeDtypeStruct((s, 3072), BF16)] + [jax.ShapeDtypeStruct((s, 512), F32)] * 3
        + [jax.ShapeDtypeStruct((s, D_MODEL), F32), jax.ShapeDtypeStruct((8, 128), F32),
           jax.ShapeDtypeStruct((1, D_MODEL), F32), jax.ShapeDtypeStruct((D_MODEL, D_MODEL), BF16),
           jax.ShapeDtypeStruct((N_DEV, 1536, 128), BF16)],
        in_specs=[_rows(tm, 512)] * 3 + [_rows(tm, 3072), _rows(tm, D_MODEL), _rows(tm, D_MODEL),
                                         _full((1, D_MODEL)), ANY, ANY],
        out_specs=[_rows(tm, 3072)] + [_rows(tm, 512)] * 3 + [_rows(tm, D_MODEL), _full((8, 128)),
                                                               _full((1, D_MODEL)), ANY, ANY],
        scratch_shapes=[pltpu.VMEM((1536, D_MODEL), BF16), pltpu.VMEM((D_MODEL, D_MODEL), BF16),
                        pltpu.VMEM((D_MODEL, D_MODEL), F32), pltpu.VMEM((1536, D_MODEL), F32),
                        pltpu.VMEM((D_MODEL, D_MODEL), BF16), pltpu.VMEM((N_DEV, 1536, 128), BF16),
                        pltpu.SemaphoreType.DMA((N_DEV + 1,))],
        compiler_params=_params(60),
    )(ya, yb, ym, pg, x, target, g_post, w_up, w_out)


def _conv_bwd(pa, dya, w_conv):
    s = pa.shape[0]
    tm = min(512, s)
    nt = s // tm

    def body(pa_ref, pp_ref, pn_ref, d_ref, dp_ref, dn_ref, w_ref, da_ref, gw_ref):
        i = pl.program_id(0)
        first, last = i == 0, i == nt - 1

        @pl.when(first)
        def _():
            gw_ref[...] = jnp.zeros_like(gw_ref)

        w = w_ref[...]
        prev_row = pp_ref[...].astype(F32)[15:16, :]
        next_row = pn_ref[...].astype(F32)[0:1, :]
        b, c, u, z, cu, cu_m1, cu_p1, y, sig, row = _conv_common(
            pa_ref[...].astype(F32), prev_row, next_row, w, first, last, tm)
        sz = z * sig
        dya_t = d_ref[...]
        d_y = dya_t * b * sz

        def halo_dy(p_row, d_row):
            zz = p_row[:, 1536:2048]
            return d_row * p_row[:, 0:512] * (zz * _sigmoid(zz))

        dy_prev = jnp.where(first, 0.0, halo_dy(prev_row, dp_ref[7:8, :]))
        dy_next = jnp.where(last, 0.0, halo_dy(next_row, dn_ref[0:1, :]))
        dy_m1 = jnp.where(row == 0, dy_prev, pltpu.roll(d_y, 1, 0))
        dy_p1 = jnp.where(row == tm - 1, dy_next, pltpu.roll(d_y, tm - 1, 0))
        d_cu = dy_p1 * w[0:1] + d_y * w[1:2] + dy_m1 * w[2:3]
        da_ref[:, 0:512] = (dya_t * y * sz).astype(BF16)
        da_ref[:, 512:1024] = (d_cu * u).astype(BF16)
        da_ref[:, 1024:1536] = (d_cu * c).astype(BF16)
        da_ref[:, 1536:2048] = (dya_t * b * y * (sig * (1.0 + z * (1.0 - sig)))).astype(BF16)
        gw_ref[0:1, :] += jnp.sum(d_y * cu_m1, axis=0, keepdims=True)
        gw_ref[1:2, :] += jnp.sum(d_y * cu, axis=0, keepdims=True)
        gw_ref[2:3, :] += jnp.sum(d_y * cu_p1, axis=0, keepdims=True)

    prev, nxt = _halo_specs(s, tm, 16, 2048)
    dprev, dnxt = _halo_specs(s, tm, 8, 512)
    return pl.pallas_call(
        body, name="conv_bwd", grid=(nt,),
        out_shape=[jax.ShapeDtypeStruct((s, 2048), BF16), jax.ShapeDtypeStruct((8, 512), F32)],
        in_specs=[_rows(tm, 2048), prev, nxt, _rows(tm, 512), dprev, dnxt, _full((3, 512))],
        out_specs=[_rows(tm, 2048), _full((8, 512))],
        compiler_params=_params(48),
    )(pa, pa, pa, dya, dya, dya, w_conv)


def _attn_bwd(pq, pkv, pbz, dyb, sink, tabs):
    s = pq.shape[0]
    nb = s // ATTN_BLOCK

    def body(sink_ref, q_ref, z_ref, d_ref, cs_ref, s1_ref, s2_ref, kv_ref, csf_ref, s1f_ref, s2f_ref,
             dq_ref, dz_ref, dkv_ref, gs_ref, kpad, vpad, dk_acc, dv_acc):
        n = pl.program_id(0)

        @pl.when(n == 0)
        def _():
            _fill_padded(kv_ref, kpad, vpad, s)
            dk_acc[...] = jnp.zeros_like(dk_acc)
            dv_acc[...] = jnp.zeros_like(dv_acc)
            gs_ref[...] = jnp.zeros_like(gs_ref)

        lane = lax.broadcasted_iota(jnp.int32, (ATTN_BLOCK, 128), 1)
        start = pl.multiple_of(n * ATTN_BLOCK, ATTN_BLOCK)
        kw, vw = kpad[pl.ds(start, 384), :], vpad[pl.ds(start, 384), :]
        qf = q_ref[...].astype(F32)
        valid = _window_mask(n, s)
        z = z_ref[...].astype(F32)
        sig = _sigmoid(z)
        dyb_t = d_ref[...]
        d_attn = dyb_t * (z * sig)
        outs, dqs = [], []
        dk_w = jnp.zeros((384, 128), F32)
        dv_w = jnp.zeros((384, 128), F32)
        for g in range(2):
            qg = _stack_heads(qf, g, lane).astype(BF16)
            p, o, p_sink = _attn_group(qg, kw, vw, _sink_col(sink_ref, g), valid)
            outs.append(o)
            do = _stack_heads(d_attn, g, lane)
            dob = do.astype(BF16)
            dp = _dot_nt(dob, vw)
            delta = jnp.sum(do * o, axis=-1, keepdims=True)
            ds = (p * (dp - delta)).astype(BF16)
            sink_part = p_sink * delta
            for j in range(4):
                h = 4 * g + j
                gs_ref[h:h + 1, :] -= jnp.sum(sink_part[128 * j:128 * j + 128, :])
            dqs.append(_dot(ds, kw) * ATTN_SCALE)
            dk_w += _dot_tn(ds, qg) * ATTN_SCALE
            dv_w += _dot_tn(p.astype(BF16), dob)
        dk_acc[pl.ds(start, 384), :] += dk_w
        dv_acc[pl.ds(start, 384), :] += dv_w
        attn = _unstack_heads(outs[0], outs[1], lane)
        dz_ref[...] = (dyb_t * attn * (sig * (1.0 + z * (1.0 - sig)))).astype(BF16)
        dq = _unstack_heads(dqs[0], dqs[1], lane)
        cs, s1, s2 = cs_ref[...], s1_ref[...], s2_ref[...]
        for b in range(4):
            dq_ref[:, 128 * b:128 * b + 128] = _rope_t(dq[:, 128 * b:128 * b + 128], cs, s1, s2).astype(BF16)

        @pl.when(n == nb - 1)
        def _():
            dk = dk_acc[ATTN_BLOCK:ATTN_BLOCK + s, :]
            dkv_ref[:, 0:128] = _rope_t(dk, csf_ref[...], s1f_ref[...], s2f_ref[...]).astype(BF16)
            dkv_ref[:, 128:256] = dv_acc[ATTN_BLOCK:ATTN_BLOCK + s, :].astype(BF16)

    tile = _rows(ATTN_BLOCK, 512)
    tab = _rows(ATTN_BLOCK, 128)
    return pl.pallas_call(
        body, name="attn_bwd", grid=(nb,),
        out_shape=[jax.ShapeDtypeStruct((s, 512), BF16), jax.ShapeDtypeStruct((s, 512), BF16),
                   jax.ShapeDtypeStruct((s, 256), BF16), jax.ShapeDtypeStruct((8, 128), F32)],
        in_specs=[pl.BlockSpec(memory_space=pltpu.SMEM), tile, tile, tile, tab, tab, tab,
                  _full((s, 256)), _full((s, 128)), _full((s, 128)), _full((s, 128))],
        out_specs=[tile, tile, _full((s, 256)), _full((8, 128))],
        scratch_shapes=[pltpu.VMEM((s + 2 * ATTN_BLOCK, 128), BF16)] * 2
        + [pltpu.VMEM((s + 2 * ATTN_BLOCK, 128), F32)] * 2,
        compiler_params=_params(48),
    )(sink, pq, pbz, dyb, *tabs, pkv, *tabs)


def _mem_attn_bwd(pmq, pmz, mkv, dym):
    s = pmq.shape[0]
    m = mkv.shape[0]
    tm = min(512, s)

    def body(q_ref, z_ref, d_ref, mk_ref, mv_ref, dq_ref, dz_ref, dmkv_ref):
        @pl.when(pl.program_id(0) == 0)
        def _():
            dmkv_ref[...] = jnp.zeros_like(dmkv_ref)

        z = z_ref[...].astype(F32)
        sig = _sigmoid(z)
        dym_t = d_ref[...]
        d_attn = dym_t * (z * sig)
        dsilu = sig * (1.0 + z * (1.0 - sig))
        for h in range(MEM_HEADS):
            cols = slice(128 * h, 128 * h + 128)
            q, mk, mv = q_ref[:, cols], mk_ref[:, cols], mv_ref[:, cols]
            p = _mem_softmax(q, mk)
            pb = p.astype(BF16)
            o = _dot(pb, mv)
            do = d_attn[:, cols]
            dob = do.astype(BF16)
            dp = _dot_nt(dob, mv)
            ds = (p * (dp - jnp.sum(do * o, axis=-1, keepdims=True))).astype(BF16)
            dq_ref[:, cols] = (_dot(ds, mk) * MEM_SCALE).astype(BF16)
            dz_ref[:, cols] = (dym_t[:, cols] * o * dsilu[:, cols]).astype(BF16)
            dmkv_ref[:, cols] += _dot_tn(ds, q) * MEM_SCALE
            dmkv_ref[:, 512 + 128 * h:512 + 128 * h + 128] += _dot_tn(pb, dob)

    return pl.pallas_call(
        body, name="mem_attn_bwd", grid=(s // tm,),
        out_shape=[jax.ShapeDtypeStruct((s, 512), BF16), jax.ShapeDtypeStruct((s, 512), BF16),
                   jax.ShapeDtypeStruct((m, D_MODEL), F32)],
        in_specs=[_rows(tm, 512), _rows(tm, 512), _rows(tm, 512), pl.BlockSpec((m, 512), lambda i: (0, 0)),
                  pl.BlockSpec((m, 512), lambda i: (0, 1))],
        out_specs=[_rows(tm, 512), _rows(tm, 512), _full((m, D_MODEL))],
        compiler_params=_params(32),
    )(pmq, pmz, dym, mkv, mkv)


def _mem_kv_bwd(mem, g_mem, mn, dmkv, w_mkv):
    m = mem.shape[0]

    def body(mem_ref, g_ref, mn_ref, d_ref, w_ref, gw_ref, gg_ref):
        db = d_ref[...].astype(BF16)
        gw_ref[...] = _dot_tn(mn_ref[...], db).astype(BF16)
        d_mn = _dot_nt(db, w_ref[...])
        xf = mem_ref[...]
        r = lax.rsqrt(jnp.mean(xf * xf, axis=-1, keepdims=True) + EPS)
        gg_ref[...] = jnp.sum(d_mn * (xf * r), axis=0, keepdims=True)

    return pl.pallas_call(
        body, name="mem_kv_bwd", grid=(1,),
        out_shape=[jax.ShapeDtypeStruct((D_MODEL, D_MODEL), BF16), jax.ShapeDtypeStruct((1, D_MODEL), F32)],
        in_specs=[_full((m, D_MODEL)), _full((1, D_MODEL)), _full((m, D_MODEL)), _full((m, D_MODEL)),
                  _full((D_MODEL, D_MODEL))],
        out_specs=[_full((D_MODEL, D_MODEL)), _full((1, D_MODEL))],
        compiler_params=_params(32),
    )(mem, g_mem, mn, dmkv, w_mkv)


def _dh_bwd(dparts, x, dy, g_pre, w_int):
    s = x.shape[0]
    tm = min(256, s)

    def body(*refs):
        d_refs = refs[:7]
        x_ref, dy_ref, g_ref, w_hbm, gx_ref, gg_ref, w_vm, sems = refs[7:]
        _load_once([(w_hbm, w_vm)], sems)

        @pl.when(pl.program_id(0) == 0)
        def _():
            gg_ref[...] = jnp.zeros_like(gg_ref)

        d_h = jnp.zeros((tm, D_MODEL), F32)
        for d_ref, (r0, width) in zip(d_refs, SEGS):
            for c0 in range(0, width, 512):
                cw = min(512, width - c0)
                d_h += _dot(d_ref[:, c0:c0 + cw], w_vm[r0 + c0:r0 + c0 + cw, :])
        xf = x_ref[...]
        r = lax.rsqrt(jnp.mean(xf * xf, axis=-1, keepdims=True) + EPS)
        xn = xf * r
        a = d_h * g_ref[...]
        gx_ref[...] = r * (a - xn * jnp.mean(a * xn, axis=-1, keepdims=True)) + dy_ref[...]
        gg_ref[...] += jnp.sum(d_h * xn, axis=0, keepdims=True)

    return pl.pallas_call(
        body, name="dh_bwd", grid=(s // tm,),
        out_shape=[jax.ShapeDtypeStruct((s, D_MODEL), F32), jax.ShapeDtypeStruct((1, D_MODEL), F32)],
        in_specs=[_rows(tm, w) for _, w in SEGS] + [_rows(tm, D_MODEL), _rows(tm, D_MODEL), _full((1, D_MODEL)), ANY],
        out_specs=[_rows(tm, D_MODEL), _full((1, D_MODEL))],
        scratch_shapes=[pltpu.VMEM((IN_WIDTH, D_MODEL), BF16), pltpu.SemaphoreType.DMA((1,))],
        compiler_params=_params(52),
    )(*dparts, x, dy, g_pre, w_int)


def _gw_in(dparts, h):
    s = h.shape[0]
    tn = 256
    starts, counts = [], []
    for r0, width in SEGS:
        starts.append(r0 // tn)
        counts.append(width // tn)

    def body(*refs):
        d_refs = refs[:7]
        h_hbm, o_ref, h_vm, sems = refs[7:]
        _load_once([(h_hbm, h_vm)], sems)
        j = pl.program_id(0)
        for d_ref, st, cnt in zip(d_refs, starts, counts):
            @pl.when((j >= st) & (j < st + cnt))
            def _(d_ref=d_ref):
                o_ref[...] = _dot_tn(d_ref[...], h_vm[...]).astype(BF16)

    def seg_spec(st, cnt):
        return pl.BlockSpec((s, tn), lambda j: (0, jnp.clip(j - st, 0, cnt - 1)))

    return pl.pallas_call(
        body, name="gw_in", grid=(IN_WIDTH // tn,),
        out_shape=jax.ShapeDtypeStruct((IN_WIDTH, D_MODEL), BF16),
        in_specs=[seg_spec(st, cnt) for st, cnt in zip(starts, counts)] + [ANY],
        out_specs=pl.BlockSpec((tn, D_MODEL), lambda j: (j, 0)),
        scratch_shapes=[pltpu.VMEM((s, D_MODEL), BF16), pltpu.SemaphoreType.DMA((1,))],
        compiler_params=_params(52),
    )(*dparts, h)


def _adamw_math(w, g, m, v):
    m2 = ADAM_B1 * m + (1.0 - ADAM_B1) * g
    v2 = ADAM_B2 * v + (1.0 - ADAM_B2) * (g * g)
    m_hat = m2 / (1.0 - ADAM_B1 ** ADAM_STEP)
    v_hat = v2 / (1.0 - ADAM_B2 ** ADAM_STEP)
    delta = -ADAM_LR * (m_hat / (jnp.sqrt(v_hat) + ADAM_EPS) + ADAM_WD * w)
    return delta, m2, v2


def _sum_adamw(parts, block, w, m, v, name, tiles=1):
    r, c = w.shape
    rt = r // tiles

    def body(p_ref, w_ref, m_ref, v_ref, g_ref, d_ref, m2_ref, v2_ref):
        g = p_ref[0].astype(F32)
        for k in range(1, N_CHIPS):
            g += p_ref[k].astype(F32)
        g_ref[...] = g
        d_ref[...], m2_ref[...], v2_ref[...] = _adamw_math(w_ref[...], g, m_ref[...], v_ref[...])

    spec = pl.BlockSpec((rt, c), lambda i: (i, 0))
    return pl.pallas_call(
        body, name=name, grid=(tiles,),
        out_shape=[jax.ShapeDtypeStruct((r, c), F32)] * 4,
        in_specs=[pl.BlockSpec((N_CHIPS, rt, c), lambda i: (0, block * tiles + i, 0))] + [spec] * 3,
        out_specs=[spec] * 4,
        compiler_params=_params(48),
    )(parts, w, m, v)


def _pack_sum(packs):
    def body(p_ref, o_ref):
        acc = p_ref[0]
        for k in range(1, N_DEV):
            acc += p_ref[k]
        o_ref[...] = acc

    return pl.pallas_call(
        body, name="pack_sum", grid=(1,),
        out_shape=jax.ShapeDtypeStruct((8, D_MODEL), F32),
        in_specs=[_full((N_DEV, 8, D_MODEL))], out_specs=_full((8, D_MODEL)),
    )(packs)


def _small_adamw(ws, gs, ms, vs):
    k = len(ws)

    def body(*refs):
        w_refs, g_refs, m_refs, v_refs = (refs[j * k:(j + 1) * k] for j in range(4))
        outs = refs[4 * k:]
        for j in range(k):
            outs[j][...], outs[k + j][...], outs[2 * k + j][...] = _adamw_math(
                w_refs[j][...], g_refs[j][...], m_refs[j][...], v_refs[j][...])

    specs = [_full(w.shape) for w in ws]
    res = pl.pallas_call(
        body, name="small_adamw", grid=(1,),
        out_shape=[jax.ShapeDtypeStruct(w.shape, F32) for w in ws] * 3,
        in_specs=specs * 4, out_specs=specs * 3,
    )(*ws, *gs, *ms, *vs)
    return res[:k], res[k:2 * k], res[2 * k:]


def kernel(x, mem, g_pre, w_in, w_conv, attn_sink, g_mem, w_mem_kv, w_up_a, w_up_b, w_up_m, w_out, g_post, loss_target, m_g_pre, m_w_in, m_w_conv, m_attn_sink, m_g_mem, m_w_mem_kv, m_w_up_a, m_w_up_b, m_w_up_m, m_w_out, m_g_post, v_g_pre, v_w_in, v_w_conv, v_attn_sink, v_g_mem, v_w_mem_kv, v_w_up_a, v_w_up_b, v_w_up_m, v_w_out, v_g_post):
    s = x.shape[1]
    x2, mem2, tgt2 = x[0], mem[0], loss_target[0]
    me = 4 * lax.axis_index("x") + 2 * lax.axis_index("y") + lax.axis_index("c")

    w_up_loc = jnp.concatenate([w_up_a[0], w_up_b[0], w_up_m[0]], axis=0).astype(BF16)
    w_conv_loc = jnp.zeros((8, 128), F32).at[:3, :64].set(w_conv[0])
    w_int_g, w_mkv_g, w_out_g, w_up_g, w_conv_g = _all_gather(
        [w_in[0].T.astype(BF16), w_mem_kv[0].astype(BF16), w_out[0].astype(BF16), w_up_loc, w_conv_loc],
        "gather_weights")
    w_int = w_int_g.reshape(IN_WIDTH, D_MODEL)
    w_mkv = w_mkv_g.reshape(D_MODEL, D_MODEL)
    w_out_f = w_out_g.reshape(D_MODEL, D_MODEL)
    w_conv_f = w_conv_g[:, :3, :64].transpose(1, 0, 2).reshape(3, 512)
    sink = attn_sink[0]
    tabs = _rope_tables(s)

    h, pa, pq, pkv, pbz, pmq, pmz, pg = _proj_fwd(x2, g_pre, w_int, tabs)
    mn, mkv = _mem_kv_fwd(mem2, g_mem, w_mkv)
    ya = _conv_fwd(pa, w_conv_f)
    yb = _attn_fwd(pq, pkv, pbz, sink)
    ym = _mem_attn_fwd(pmq, pmz, mkv)
    dg, dya, dyb, dym, dy, loss_p, gg_post, gw_out, gw_up = _mid(ya, yb, ym, pg, x2, tgt2, g_post, w_up_g, w_out_f)

    da, gw_conv = _conv_bwd(pa, dya, w_conv_f)
    dq, dbz, dkv, g_sink = _attn_bwd(pq, pkv, pbz, dyb, sink, tabs)
    dmq, dmz, dmkv = _mem_attn_bwd(pmq, pmz, mkv, dym)
    gw_mkv, gg_mem = _mem_kv_bwd(mem2, g_mem, mn, dmkv, w_mkv)
    dparts = (da, dq, dkv, dbz, dmq, dmz, dg)
    grad_x, gg_pre = _dh_bwd(dparts, x2, dy, g_pre, w_int)
    gw_int = _gw_in(dparts, h)

    shares = [gw_int.reshape(N_DEV, SHARD_IN, D_MODEL), gw_mkv.reshape(N_DEV, 128, D_MODEL),
              gw_out.reshape(N_DEV, 128, D_MODEL), gw_up]
    core = lax.axis_index("c").astype(jnp.int32).reshape(1)
    from_sibling = _sibling_exchange(shares, "grads_to_sibling")
    chip_shares = _pair_add(shares, from_sibling, core, "grads_pair_add")
    r_int, r_mkv, r_out, r_up = _chip_exchange(chip_shares, "grads_to_chips")
    row3 = jnp.concatenate([gw_conv[0:1], gw_conv[1:2]], axis=1)
    row4 = jnp.concatenate([gw_conv[2:3], g_sink[:, 0].reshape(1, 8), loss_p[0:1, 0:1],
                            jnp.zeros((1, 512 - 9), F32)], axis=1)
    pack = jnp.concatenate([gg_pre, gg_mem, gg_post, row3, row4, jnp.zeros((3, D_MODEL), F32)], axis=0)
    (packs,) = _all_gather([pack], "gather_small")
    tot = _pack_sum(packs)

    g_w_in, d_w_in, nm_w_in, nv_w_in = (t.T for t in _sum_adamw(
        r_int, 0, w_in[0].T, m_w_in[0].T, v_w_in[0].T, "adamw_w_in", tiles=2))
    g_mkv, d_mkv, nm_mkv, nv_mkv = _sum_adamw(r_mkv, 0, w_mem_kv[0], m_w_mem_kv[0], v_w_mem_kv[0], "adamw_w_mem_kv")
    g_out, d_out, nm_out, nv_out = _sum_adamw(r_out, 0, w_out[0], m_w_out[0], v_w_out[0], "adamw_w_out")
    up = [_sum_adamw(r_up, k, w[0], m[0], v[0], "adamw_w_up_" + "abm"[k])
          for k, (w, m, v) in enumerate([(w_up_a, m_w_up_a, v_w_up_a), (w_up_b, m_w_up_b, v_w_up_b),
                                         (w_up_m, m_w_up_m, v_w_up_m)])]

    g_g_pre, g_g_mem, g_g_post = tot[0:1], tot[1:2], tot[2:3]
    g_conv_full = jnp.concatenate([tot[3:4, 0:512], tot[3:4, 512:1024], tot[4:5, 0:512]], axis=0)
    g_conv = lax.dynamic_slice(g_conv_full, (0, 64 * me), (3, 64))
    g_sink_tot = tot[4:5, 512:520]
    loss = tot[4, 520]
    small_w = [g_pre, w_conv[0], attn_sink, g_mem, g_post]
    small_g = [g_g_pre, g_conv, g_sink_tot, g_g_mem, g_g_post]
    small_m = [m_g_pre, m_w_conv[0], m_attn_sink, m_g_mem, m_g_post]
    small_v = [v_g_pre, v_w_conv[0], v_attn_sink, v_g_mem, v_g_post]
    sd, sm, sv = _small_adamw(small_w, small_g, small_m, small_v)

    def lead(a):
        return a[None]

    grads = [g_g_pre, lead(g_w_in), lead(g_conv), g_sink_tot, g_g_mem, lead(g_mkv), lead(up[0][0]),
             lead(up[1][0]), lead(up[2][0]), lead(g_out), g_g_post]

    def assemble(small, big_in, big_mkv, big_up, big_out):
        return [small[0], lead(big_in), lead(small[1]), small[2], small[3], lead(big_mkv), lead(big_up[0]),
                lead(big_up[1]), lead(big_up[2]), lead(big_out), small[4]]

    deltas = assemble(sd, d_w_in, d_mkv, [u[1] for u in up], d_out)
    new_m = assemble(sm, nm_w_in, nm_mkv, [u[2] for u in up], nm_out)
    new_v = assemble(sv, nv_w_in, nv_mkv, [u[3] for u in up], nv_out)
    return (loss, grad_x[None], *grads, *deltas, *new_m, *new_v)
```

```python
import functools

import jax
import jax.numpy as jnp
from jax import lax
from jax.experimental import pallas as pl
from jax.experimental.pallas import tpu as pltpu

F32 = jnp.float32
BF16 = jnp.bfloat16
MESH = pl.DeviceIdType.MESH

N_DEV = 8
D_MODEL = 1024
EPS = 1e-6
ROPE_THETA = 500000.0
ROT_DIM = 16
HEAD_DIM = 64
ATTN_BLOCK = 128
MEM_HEADS = 4
MEM_HEAD_DIM = 128
ATTN_SCALE = HEAD_DIM ** -0.5
MEM_SCALE = MEM_HEAD_DIM ** -0.5

ADAM_LR = 0.001
ADAM_B1 = 0.9
ADAM_B2 = 0.999
ADAM_EPS = 1e-08
ADAM_WD = 0.01
ADAM_STEP = 10

SEG_A = (0, 2048)
SEG_BQ = (2048, 512)
SEG_BKV = (2560, 256)
SEG_BZ = (2816, 512)
SEG_MQ = (3328, 512)
SEG_MZ = (3840, 512)
SEG_G = (4352, 3072)
SEGS = (SEG_A, SEG_BQ, SEG_BKV, SEG_BZ, SEG_MQ, SEG_MZ, SEG_G)
IN_WIDTH = 7424
SHARD_IN = IN_WIDTH // N_DEV

V7X_VMEM_BYTES = 64 * 1024 * 1024
ANY = pl.BlockSpec(memory_space=pl.ANY)


def _params(vmem_mb):
    assert vmem_mb * 1024 * 1024 < V7X_VMEM_BYTES
    return pltpu.CompilerParams(dimension_semantics=("arbitrary",), vmem_limit_bytes=vmem_mb * 1024 * 1024)


def _full(shape):
    zeros = (0,) * len(shape)
    return pl.BlockSpec(shape, lambda i: zeros)


def _rows(tm, width):
    return pl.BlockSpec((tm, width), lambda i: (i, 0))


def _dot(a, b):
    return jnp.dot(a, b, preferred_element_type=F32)


def _dot_nt(a, b):
    return lax.dot_general(a, b, (((1,), (1,)), ((), ())), preferred_element_type=F32)


def _dot_tn(a, b):
    return lax.dot_general(a, b, (((0,), (0,)), ((), ())), preferred_element_type=F32)


def _sigmoid(z):
    return 1.0 / (1.0 + jnp.exp(-z))


def _rope(t, cs, s1, s2):
    return t * cs + pltpu.roll(t, 120, 1) * s1 + pltpu.roll(t, 8, 1) * s2


def _rope_t(d, cs, s1, s2):
    return d * cs + pltpu.roll(d * s1, 8, 1) + pltpu.roll(d * s2, 120, 1)


def _rope_tables(s):
    half = ROT_DIM // 2
    inv_freq = jnp.power(jnp.float32(ROPE_THETA), -jnp.arange(half, dtype=F32) * (2.0 / ROT_DIM))
    d = jnp.arange(128) % HEAD_DIM
    ang = jnp.arange(s).astype(F32)[:, None] * inv_freq[d % half][None, :]
    cos, sin = jnp.cos(ang), jnp.sin(ang)
    lo, hi = (d < half)[None, :], ((d >= half) & (d < ROT_DIM))[None, :]
    return jnp.where(lo | hi, cos, 1.0), jnp.where(lo, -sin, 0.0), jnp.where(hi, sin, 0.0)


def _load_once(pairs, sems):
    @pl.when(pl.program_id(0) == 0)
    def _():
        cps = [pltpu.make_async_copy(src, dst, sems.at[k]) for k, (src, dst) in enumerate(pairs)]
        for cp in cps:
            cp.start()
        for cp in cps:
            cp.wait()


def _my_place():
    x, y, c = lax.axis_index("x"), lax.axis_index("y"), lax.axis_index("c")
    return x, y, c


def _all_gather(arrs, name):
    n = len(arrs)

    def body(*refs):
        ins, outs = refs[:n], refs[n:2 * n]
        send_sems, recv_sems, local_sems = refs[2 * n:]
        x, y, c = _my_place()
        me, sibling = (x, y, c), (x, y, 1 - c)
        chips = [(1 - x, y), (x, 1 - y), (1 - x, 1 - y)]

        def idx(px, py, pc):
            return 4 * px + 2 * py + pc

        def copy(a, k, block, to, src=None):
            dst = outs[a].at[idx(*block)]
            return pltpu.make_async_remote_copy(
                src_ref=dst if src is None else src, dst_ref=dst,
                send_sem=send_sems.at[a * 7 + k], recv_sem=recv_sems.at[a * 7 + k],
                device_id=to, device_id_type=MESH)

        mine = [pltpu.make_async_copy(ins[a], outs[a].at[idx(*me)], local_sems.at[a]) for a in range(n)]
        for cp in mine:
            cp.start()
        first = []
        for a in range(n):
            first.append(copy(a, 0, me, sibling, src=ins[a]))
        for j, chip in enumerate(chips):
            for a in range(n):
                first.append(copy(a, 1 + j, me, (*chip, c), src=ins[a]))
        for cp in first:
            cp.start()
        passed = []
        for j, chip in enumerate(chips):
            for a in range(n):
                copy(a, 1 + j, (*chip, c), me).wait_recv()
                cp = copy(a, 4 + j, (*chip, c), sibling)
                cp.start()
                passed.append(cp)
        for a in range(n):
            copy(a, 0, sibling, me).wait_recv()
        for j, chip in enumerate(chips):
            for a in range(n):
                copy(a, 4 + j, (*chip, 1 - c), me).wait_recv()
        for cp in first + passed:
            cp.wait_send()
        for cp in mine:
            cp.wait()

    return pl.pallas_call(
        body, name=name,
        out_shape=[jax.ShapeDtypeStruct((N_DEV,) + a.shape, a.dtype) for a in arrs],
        in_specs=[ANY] * n, out_specs=[ANY] * n,
        scratch_shapes=[pltpu.SemaphoreType.DMA((7 * n,)), pltpu.SemaphoreType.DMA((7 * n,)),
                        pltpu.SemaphoreType.DMA((n,))],
    )(*arrs)


N_CHIPS = 4


def _sibling_exchange(arrs, name):
    n = len(arrs)

    def body(*refs):
        ins, outs = refs[:n], refs[n:2 * n]
        send_sems, recv_sems = refs[2 * n:]
        x, y, c = _my_place()
        sibling = (x, y, 1 - c)

        def copy(a, j):
            return pltpu.make_async_remote_copy(
                src_ref=ins[a].at[2 * j + (1 - c)], dst_ref=outs[a].at[j],
                send_sem=send_sems.at[a * N_CHIPS + j], recv_sem=recv_sems.at[a * N_CHIPS + j],
                device_id=sibling, device_id_type=MESH)

        cps = [copy(a, j) for j in range(N_CHIPS) for a in range(n)]
        for cp in cps:
            cp.start()
        for cp in cps:
            cp.wait_recv()
        for cp in cps:
            cp.wait_send()

    return pl.pallas_call(
        body, name=name,
        out_shape=[jax.ShapeDtypeStruct((N_CHIPS,) + a.shape[1:], a.dtype) for a in arrs],
        in_specs=[ANY] * n, out_specs=[ANY] * n,
        scratch_shapes=[pltpu.SemaphoreType.DMA((N_CHIPS * n,)), pltpu.SemaphoreType.DMA((N_CHIPS * n,))],
    )(*arrs)


def _pair_add(mine, recv, core, name):
    n = len(mine)

    def body(c_ref, *refs):
        for a in range(n):
            refs[2 * n + a][...] = (refs[a][...].astype(F32) + refs[n + a][...].astype(F32)).astype(BF16)

    def blk(a):
        return (None,) + a.shape[1:]

    grid_spec = pltpu.PrefetchScalarGridSpec(
        num_scalar_prefetch=1, grid=(N_CHIPS,),
        in_specs=[pl.BlockSpec(blk(a), lambda j, c_ref: (2 * j + c_ref[0], 0, 0)) for a in mine]
        + [pl.BlockSpec(blk(a), lambda j, c_ref: (j, 0, 0)) for a in recv],
        out_specs=[pl.BlockSpec(blk(a), lambda j, c_ref: (j, 0, 0)) for a in recv])
    return pl.pallas_call(
        body, name=name, grid_spec=grid_spec,
        out_shape=[jax.ShapeDtypeStruct(a.shape, BF16) for a in recv],
        compiler_params=_params(32),
    )(core, *mine, *recv)


def _chip_exchange(arrs, name):
    n = len(arrs)

    def body(*refs):
        ins, outs = refs[:n], refs[n:2 * n]
        send_sems, recv_sems, local_sems = refs[2 * n:]
        x, y, c = _my_place()
        my_chip = 2 * x + y
        peers = [(x, 1 - y), (1 - x, y), (1 - x, 1 - y)]

        mine = [pltpu.make_async_copy(ins[a].at[my_chip], outs[a].at[my_chip], local_sems.at[a]) for a in range(n)]
        for cp in mine:
            cp.start()

        def copy(a, k, dst_slot):
            px, py = peers[k]
            return pltpu.make_async_remote_copy(
                src_ref=ins[a].at[2 * px + py], dst_ref=outs[a].at[dst_slot],
                send_sem=send_sems.at[a * 3 + k], recv_sem=recv_sems.at[a * 3 + k],
                device_id=(px, py, c), device_id_type=MESH)

        sent = [copy(a, k, my_chip) for k in range(3) for a in range(n)]
        for cp in sent:
            cp.start()
        for k in range(3):
            px, py = peers[k]
            for a in range(n):
                copy(a, k, 2 * px + py).wait_recv()
        for cp in sent:
            cp.wait_send()
        for cp in mine:
            cp.wait()

    return pl.pallas_call(
        body, name=name,
        out_shape=[jax.ShapeDtypeStruct(a.shape, a.dtype) for a in arrs],
        in_specs=[ANY] * n, out_specs=[ANY] * n,
        scratch_shapes=[pltpu.SemaphoreType.DMA((3 * n,)), pltpu.SemaphoreType.DMA((3 * n,)),
                        pltpu.SemaphoreType.DMA((n,))],
    )(*arrs)


def _proj_fwd(x, g_pre, w_int, tabs):
    s = x.shape[0]
    tm = min(512, s)

    def body(x_ref, g_ref, cs_ref, s1_ref, s2_ref, w_hbm,
             h_ref, pa_ref, pq_ref, pkv_ref, pbz_ref, pmq_ref, pmz_ref, pg_ref, w_vm, sems):
        _load_once([(w_hbm, w_vm)], sems)
        xf = x_ref[...]
        r = lax.rsqrt(jnp.mean(xf * xf, axis=-1, keepdims=True) + EPS)
        h = ((xf * r) * g_ref[...]).astype(BF16)
        h_ref[...] = h
        cs, s1, s2 = cs_ref[...], s1_ref[...], s2_ref[...]

        def mm(seg, c0, width):
            return _dot_nt(h, w_vm[seg[0] + c0:seg[0] + c0 + width, :])

        for c0 in range(0, SEG_A[1], 512):
            pa_ref[:, c0:c0 + 512] = mm(SEG_A, c0, 512).astype(BF16)
        q = mm(SEG_BQ, 0, 512)
        for b in range(4):
            pq_ref[:, 128 * b:128 * b + 128] = _rope(q[:, 128 * b:128 * b + 128], cs, s1, s2).astype(BF16)
        kv = mm(SEG_BKV, 0, 256)
        pkv_ref[:, 0:128] = _rope(kv[:, 0:128], cs, s1, s2).astype(BF16)
        pkv_ref[:, 128:256] = kv[:, 128:256].astype(BF16)
        pbz_ref[...] = mm(SEG_BZ, 0, 512).astype(BF16)
        pmq_ref[...] = mm(SEG_MQ, 0, 512).astype(BF16)
        pmz_ref[...] = mm(SEG_MZ, 0, 512).astype(BF16)
        for c0 in range(0, SEG_G[1], 512):
            pg_ref[:, c0:c0 + 512] = mm(SEG_G, c0, 512).astype(BF16)

    widths = (D_MODEL, 2048, 512, 256, 512, 512, 512, 3072)
    return pl.pallas_call(
        body, name="proj_fwd", grid=(s // tm,),
        out_shape=[jax.ShapeDtypeStruct((s, w), BF16) for w in widths],
        in_specs=[_rows(tm, D_MODEL), _full((1, D_MODEL)), _rows(tm, 128), _rows(tm, 128), _rows(tm, 128), ANY],
        out_specs=[_rows(tm, w) for w in widths],
        scratch_shapes=[pltpu.VMEM((IN_WIDTH, D_MODEL), BF16), pltpu.SemaphoreType.DMA((1,))],
        compiler_params=_params(52),
    )(x, g_pre, *tabs, w_int)


def _mem_kv_fwd(mem, g_mem, w_mkv):
    m = mem.shape[0]

    def body(mem_ref, g_ref, w_ref, mn_ref, mkv_ref):
        xf = mem_ref[...]
        r = lax.rsqrt(jnp.mean(xf * xf, axis=-1, keepdims=True) + EPS)
        mn = ((xf * r) * g_ref[...]).astype(BF16)
        mn_ref[...] = mn
        mkv_ref[...] = _dot(mn, w_ref[...]).astype(BF16)

    return pl.pallas_call(
        body, name="mem_kv_fwd", grid=(1,),
        out_shape=[jax.ShapeDtypeStruct((m, D_MODEL), BF16)] * 2,
        in_specs=[_full((m, D_MODEL)), _full((1, D_MODEL)), _full((D_MODEL, D_MODEL))],
        out_specs=[_full((m, D_MODEL))] * 2,
        compiler_params=_params(32),
    )(mem, g_mem, w_mkv)


def _halo_specs(s, tm, rows, width):
    nblk = s // rows
    prev = pl.BlockSpec((rows, width), lambda i: (jnp.maximum(i * (tm // rows) - 1, 0), 0))
    nxt = pl.BlockSpec((rows, width), lambda i: (jnp.minimum((i + 1) * (tm // rows), nblk - 1), 0))
    return prev, nxt


def _conv_common(pa, prev_row, next_row, w, first, last, tm):
    b, c, u, z = (pa[:, 512 * k:512 * k + 512] for k in range(4))
    cu = c * u
    cu_prev = jnp.where(first, 0.0, prev_row[:, 512:1024] * prev_row[:, 1024:1536])
    cu_next = jnp.where(last, 0.0, next_row[:, 512:1024] * next_row[:, 1024:1536])
    row = lax.broadcasted_iota(jnp.int32, (tm, 512), 0)
    cu_m1 = jnp.where(row == 0, cu_prev, pltpu.roll(cu, 1, 0))
    cu_p1 = jnp.where(row == tm - 1, cu_next, pltpu.roll(cu, tm - 1, 0))
    y = cu_m1 * w[0:1] + cu * w[1:2] + cu_p1 * w[2:3]
    sig = _sigmoid(z)
    return b, c, u, z, cu, cu_m1, cu_p1, y, sig, row


def _conv_fwd(pa, w_conv):
    s = pa.shape[0]
    tm = min(512, s)
    nt = s // tm

    def body(pa_ref, pp_ref, pn_ref, w_ref, ya_ref):
        i = pl.program_id(0)
        prev_row = pp_ref[...].astype(F32)[15:16, :]
        next_row = pn_ref[...].astype(F32)[0:1, :]
        b, _, _, z, _, _, _, y, sig, _ = _conv_common(
            pa_ref[...].astype(F32), prev_row, next_row, w_ref[...], i == 0, i == nt - 1, tm)
        ya_ref[...] = (b * y * (z * sig)).astype(BF16)

    prev, nxt = _halo_specs(s, tm, 16, 2048)
    return pl.pallas_call(
        body, name="conv_fwd", grid=(nt,),
        out_shape=jax.ShapeDtypeStruct((s, 512), BF16),
        in_specs=[_rows(tm, 2048), prev, nxt, _full((3, 512))],
        out_specs=_rows(tm, 512),
        compiler_params=_params(48),
    )(pa, pa, pa, w_conv)


def _stack_heads(a, g, lane):
    in_g = (lane >= 64 * g) & (lane < 64 * g + 64)
    parts = []
    for j in range(4):
        h = 4 * g + j
        blk = a[:, 128 * (h // 2):128 * (h // 2) + 128]
        if h % 2 != g:
            blk = pltpu.roll(blk, 64, 1)
        parts.append(jnp.where(in_g, blk, 0.0))
    return jnp.concatenate(parts, axis=0)


def _unstack_heads(o0, o1, lane):
    blocks = []
    for b in range(4):
        g = b // 2
        og = (o0, o1)[g]
        je, jo = (2 * b) % 4, (2 * b + 1) % 4
        even, odd = og[128 * je:128 * je + 128], og[128 * jo:128 * jo + 128]
        if g == 0:
            odd = pltpu.roll(odd, 64, 1)
        else:
            even = pltpu.roll(even, 64, 1)
        blocks.append(jnp.where(lane < 64, even, odd))
    return jnp.concatenate(blocks, axis=1)


def _window_mask(n, s):
    c = lax.broadcasted_iota(jnp.int32, (384, 512), 0)
    r = lax.broadcasted_iota(jnp.int32, (384, 512), 1) & 127
    kpos = c + (n - 1) * ATTN_BLOCK
    return (c >= r) & (c <= r + 2 * ATTN_BLOCK) & (kpos >= 0) & (kpos < s)


def _sink_row(sink_ref, g):
    return jnp.concatenate([jnp.full((1, ATTN_BLOCK), sink_ref[4 * g + j], F32) for j in range(4)], axis=1)


def _attn_group(qg, kw, vw, sink, valid):
    sc = jnp.where(valid, _dot_nt(kw, qg) * ATTN_SCALE, -jnp.inf)
    m = jnp.maximum(jnp.max(sc, axis=0, keepdims=True), sink)
    e = jnp.exp(sc - m)
    es = jnp.exp(sink - m)
    inv = 1.0 / (jnp.sum(e, axis=0, keepdims=True) + es)
    pt = e * inv
    o = _dot_tn(pt.astype(BF16), vw)
    return pt, o, es * inv


def _fill_padded(kv_ref, kpad, vpad, s):
    zero = jnp.zeros((ATTN_BLOCK, 128), BF16)
    kpad[0:ATTN_BLOCK, :] = zero
    vpad[0:ATTN_BLOCK, :] = zero
    kpad[ATTN_BLOCK + s:2 * ATTN_BLOCK + s, :] = zero
    vpad[ATTN_BLOCK + s:2 * ATTN_BLOCK + s, :] = zero
    kpad[ATTN_BLOCK:ATTN_BLOCK + s, :] = kv_ref[:, 0:128]
    vpad[ATTN_BLOCK:ATTN_BLOCK + s, :] = kv_ref[:, 128:256]


def _attn_fwd(pq, pkv, pbz, sink):
    s = pq.shape[0]
    nb = s // ATTN_BLOCK

    def body(sink_ref, q_ref, z_ref, kv_ref, yb_ref, kpad, vpad):
        n = pl.program_id(0)

        @pl.when(n == 0)
        def _():
            _fill_padded(kv_ref, kpad, vpad, s)

        lane = lax.broadcasted_iota(jnp.int32, (ATTN_BLOCK, 128), 1)
        start = pl.multiple_of(n * ATTN_BLOCK, ATTN_BLOCK)
        kw, vw = kpad[pl.ds(start, 384), :], vpad[pl.ds(start, 384), :]
        qf = q_ref[...].astype(F32)
        valid = _window_mask(n, s)
        outs = []
        for g in range(2):
            qg = _stack_heads(qf, g, lane).astype(BF16)
            _, o, _ = _attn_group(qg, kw, vw, _sink_row(sink_ref, g), valid)
            outs.append(o)
        attn = _unstack_heads(outs[0], outs[1], lane)
        z = z_ref[...].astype(F32)
        yb_ref[...] = (attn * (z * _sigmoid(z))).astype(BF16)

    return pl.pallas_call(
        body, name="attn_fwd", grid=(nb,),
        out_shape=jax.ShapeDtypeStruct((s, 512), BF16),
        in_specs=[pl.BlockSpec(memory_space=pltpu.SMEM), _rows(ATTN_BLOCK, 512), _rows(ATTN_BLOCK, 512),
                  _full((s, 256))],
        out_specs=_rows(ATTN_BLOCK, 512),
        scratch_shapes=[pltpu.VMEM((s + 2 * ATTN_BLOCK, 128), BF16)] * 2,
        compiler_params=_params(32),
    )(sink, pq, pbz, pkv)


def _mem_softmax_t(q, mk):
    sc = _dot_nt(mk, q) * MEM_SCALE
    e = jnp.exp(sc - jnp.max(sc, axis=0, keepdims=True))
    return e * (1.0 / jnp.sum(e, axis=0, keepdims=True))


def _mem_attn_fwd(pmq, pmz, mkv):
    s = pmq.shape[0]
    m = mkv.shape[0]
    tm = min(512, s)

    def body(q_ref, z_ref, mk_ref, mv_ref, ym_ref):
        z = z_ref[...].astype(F32)
        sz = z * _sigmoid(z)
        for h in range(MEM_HEADS):
            cols = slice(128 * h, 128 * h + 128)
            pt = _mem_softmax_t(q_ref[:, cols], mk_ref[:, cols])
            o = _dot_tn(pt.astype(BF16), mv_ref[:, cols])
            ym_ref[:, cols] = (o * sz[:, cols]).astype(BF16)

    return pl.pallas_call(
        body, name="mem_attn_fwd", grid=(s // tm,),
        out_shape=jax.ShapeDtypeStruct((s, 512), BF16),
        in_specs=[_rows(tm, 512), _rows(tm, 512), pl.BlockSpec((m, 512), lambda i: (0, 0)),
                  pl.BlockSpec((m, 512), lambda i: (0, 1))],
        out_specs=_rows(tm, 512),
        compiler_params=_params(32),
    )(pmq, pmz, mkv, mkv)


def _mid(ya, yb, ym, pg, x, target, g_post, w_up, w_out):
    s = x.shape[0]
    tm = min(256, s)
    nt = s // tm

    def body(ya_ref, yb_ref, ym_ref, pg_ref, x_ref, t_ref, gp_ref, wup_hbm, wout_hbm,
             dg_ref, dya_ref, dyb_ref, dym_ref, dy_ref, loss_ref, ggp_ref, gwout_hbm, gwup_hbm,
             wup_vm, wout_vm, acc_out, acc_up, st_out, st_up, sems):
        i = pl.program_id(0)
        _load_once([(wup_hbm.at[d], wup_vm.at[:, pl.ds(128 * d, 128)]) for d in range(N_DEV)]
                   + [(wout_hbm, wout_vm)], sems)

        @pl.when(i == 0)
        def _():
            acc_out[...] = jnp.zeros_like(acc_out)
            acc_up[...] = jnp.zeros_like(acc_up)
            loss_ref[...] = jnp.zeros_like(loss_ref)
            ggp_ref[...] = jnp.zeros_like(ggp_ref)

        ys = (ya_ref[...], yb_ref[...], ym_ref[...])
        us = [_dot(ys[k], wup_vm[512 * k:512 * k + 512, :]) for k in range(3)]
        gates = [_sigmoid(pg_ref[:, 1024 * k:1024 * k + 1024].astype(F32)) for k in range(3)]
        merged = gates[0] * us[0] + gates[1] * us[1] + gates[2] * us[2]
        mb = merged.astype(BF16)
        out = _dot(mb, wout_vm[...])
        r = lax.rsqrt(jnp.mean(out * out, axis=-1, keepdims=True) + EPS)
        on = out * r
        gp = gp_ref[...]
        err = (x_ref[...] + on * gp) - t_ref[...]
        loss_ref[...] += 0.5 * jnp.sum(err * err) * (1.0 / D_MODEL)
        dy = err * (1.0 / D_MODEL)
        dy_ref[...] = dy
        ggp_ref[...] += jnp.sum(dy * on, axis=0, keepdims=True)
        a = dy * gp
        d_out = r * (a - on * jnp.mean(a * on, axis=-1, keepdims=True))
        dob = d_out.astype(BF16)
        acc_out[...] += _dot_tn(mb, dob)
        d_merged = _dot_nt(dob, wout_vm[...])
        d_refs = (dya_ref, dyb_ref, dym_ref)
        for k in range(3):
            g = gates[k]
            dg_ref[:, 1024 * k:1024 * k + 1024] = (d_merged * us[k] * g * (1.0 - g)).astype(BF16)
            du = (d_merged * g).astype(BF16)
            d_refs[k][...] = _dot_nt(du, wup_vm[512 * k:512 * k + 512, :])
            acc_up[512 * k:512 * k + 512, :] += _dot_tn(ys[k], du)

        @pl.when(i == nt - 1)
        def _():
            st_out[...] = acc_out[...].astype(BF16)
            for d in range(N_DEV):
                st_up[d] = acc_up[:, 128 * d:128 * d + 128].astype(BF16)
            cps = [pltpu.make_async_copy(st_out, gwout_hbm, sems.at[0]),
                   pltpu.make_async_copy(st_up, gwup_hbm, sems.at[1])]
            for cp in cps:
                cp.start()
            for cp in cps:
                cp.wait()

    return pl.pallas_call(
        body, name="mid", grid=(nt,),
        out_shape=[jax.ShapeDtypeStruct((s, 3072), BF16)] + [jax.ShapeDtypeStruct((s, 512), F32)] * 3
        + [jax.ShapeDtypeStruct((s, D_MODEL), F32), jax.ShapeDtypeStruct((8, 128), F32),
           jax.ShapeDtypeStruct((1, D_MODEL), F32), jax.ShapeDtypeStruct((D_MODEL, D_MODEL), BF16),
           jax.ShapeDtypeStruct((N_DEV, 1536, 128), BF16)],
        in_specs=[_rows(tm, 512)] * 3 + [_rows(tm, 3072), _rows(tm, D_MODEL), _rows(tm, D_MODEL),
                                         _full((1, D_MODEL)), ANY, ANY],
        out_specs=[_rows(tm, 3072)] + [_rows(tm, 512)] * 3 + [_rows(tm, D_MODEL), _full((8, 128)),
                                                               _full((1, D_MODEL)), ANY, ANY],
        scratch_shapes=[pltpu.VMEM((1536, D_MODEL), BF16), pltpu.VMEM((D_MODEL, D_MODEL), BF16),
                        pltpu.VMEM((D_MODEL, D_MODEL), F32), pltpu.VMEM((1536, D_MODEL), F32),
                        pltpu.VMEM((D_MODEL, D_MODEL), BF16), pltpu.VMEM((N_DEV, 1536, 128), BF16),
                        pltpu.SemaphoreType.DMA((N_DEV + 1,))],
        compiler_params=_params(60),
    )(ya, yb, ym, pg, x, target, g_post, w_up, w_out)


def _conv_bwd(pa, dya, w_conv):
    s = pa.shape[0]
    tm = min(512, s)
    nt = s // tm

    def body(pa_ref, pp_ref, pn_ref, d_ref, dp_ref, dn_ref, w_ref, da_ref, gw_ref):
        i = pl.program_id(0)
        first, last = i == 0, i == nt - 1

        @pl.when(first)
        def _():
            gw_ref[...] = jnp.zeros_like(gw_ref)

        w = w_ref[...]
        prev_row = pp_ref[...].astype(F32)[15:16, :]
        next_row = pn_ref[...].astype(F32)[0:1, :]
        b, c, u, z, cu, cu_m1, cu_p1, y, sig, row = _conv_common(
            pa_ref[...].astype(F32), prev_row, next_row, w, first, last, tm)
        sz = z * sig
        dya_t = d_ref[...]
        d_y = dya_t * b * sz

        def halo_dy(p_row, d_row):
            zz = p_row[:, 1536:2048]
            return d_row * p_row[:, 0:512] * (zz * _sigmoid(zz))

        dy_prev = jnp.where(first, 0.0, halo_dy(prev_row, dp_ref[7:8, :]))
        dy_next = jnp.where(last, 0.0, halo_dy(next_row, dn_ref[0:1, :]))
        dy_m1 = jnp.where(row == 0, dy_prev, pltpu.roll(d_y, 1, 0))
        dy_p1 = jnp.where(row == tm - 1, dy_next, pltpu.roll(d_y, tm - 1, 0))
        d_cu = dy_p1 * w[0:1] + d_y * w[1:2] + dy_m1 * w[2:3]
        da_ref[:, 0:512] = (dya_t * y * sz).astype(BF16)
        da_ref[:, 512:1024] = (d_cu * u).astype(BF16)
        da_ref[:, 1024:1536] = (d_cu * c).astype(BF16)
        da_ref[:, 1536:2048] = (dya_t * b * y * (sig * (1.0 + z * (1.0 - sig)))).astype(BF16)
        gw_ref[0:1, :] += jnp.sum(d_y * cu_m1, axis=0, keepdims=True)
        gw_ref[1:2, :] += jnp.sum(d_y * cu, axis=0, keepdims=True)
        gw_ref[2:3, :] += jnp.sum(d_y * cu_p1, axis=0, keepdims=True)

    prev, nxt = _halo_specs(s, tm, 16, 2048)
    dprev, dnxt = _halo_specs(s, tm, 8, 512)
    return pl.pallas_call(
        body, name="conv_bwd", grid=(nt,),
        out_shape=[jax.ShapeDtypeStruct((s, 2048), BF16), jax.ShapeDtypeStruct((8, 512), F32)],
        in_specs=[_rows(tm, 2048), prev, nxt, _rows(tm, 512), dprev, dnxt, _full((3, 512))],
        out_specs=[_rows(tm, 2048), _full((8, 512))],
        compiler_params=_params(48),
    )(pa, pa, pa, dya, dya, dya, w_conv)


def _attn_bwd(pq, pkv, pbz, dyb, sink, tabs):
    s = pq.shape[0]
    nb = s // ATTN_BLOCK

    def body(sink_ref, q_ref, z_ref, d_ref, cs_ref, s1_ref, s2_ref, kv_ref, csf_ref, s1f_ref, s2f_ref,
             dq_ref, dz_ref, dkv_ref, gs_ref, kpad, vpad, dk_acc, dv_acc):
        n = pl.program_id(0)

        @pl.when(n == 0)
        def _():
            _fill_padded(kv_ref, kpad, vpad, s)
            dk_acc[...] = jnp.zeros_like(dk_acc)
            dv_acc[...] = jnp.zeros_like(dv_acc)
            gs_ref[...] = jnp.zeros_like(gs_ref)

        lane = lax.broadcasted_iota(jnp.int32, (ATTN_BLOCK, 128), 1)
        start = pl.multiple_of(n * ATTN_BLOCK, ATTN_BLOCK)
        kw, vw = kpad[pl.ds(start, 384), :], vpad[pl.ds(start, 384), :]
        qf = q_ref[...].astype(F32)
        valid = _window_mask(n, s)
        z = z_ref[...].astype(F32)
        sig = _sigmoid(z)
        dyb_t = d_ref[...]
        d_attn = dyb_t * (z * sig)
        outs, dqs = [], []
        dk_w = jnp.zeros((384, 128), F32)
        dv_w = jnp.zeros((384, 128), F32)
        for g in range(2):
            qg = _stack_heads(qf, g, lane).astype(BF16)
            pt, o, p_sink = _attn_group(qg, kw, vw, _sink_row(sink_ref, g), valid)
            outs.append(o)
            dob = _stack_heads(d_attn, g, lane).astype(BF16)
            dpt = _dot_nt(vw, dob)
            delta = jnp.sum(pt * dpt, axis=0, keepdims=True)
            dst = (pt * (dpt - delta)).astype(BF16)
            sink_part = p_sink * delta
            for j in range(4):
                h = 4 * g + j
                gs_ref[h:h + 1, :] -= jnp.sum(sink_part[:, 128 * j:128 * j + 128])
            dqs.append(_dot_tn(dst, kw) * ATTN_SCALE)
            dk_w += _dot(dst, qg) * ATTN_SCALE
            dv_w += _dot(pt.astype(BF16), dob)
        dk_acc[pl.ds(start, 384), :] += dk_w
        dv_acc[pl.ds(start, 384), :] += dv_w
        attn = _unstack_heads(outs[0], outs[1], lane)
        dz_ref[...] = (dyb_t * attn * (sig * (1.0 + z * (1.0 - sig)))).astype(BF16)
        dq = _unstack_heads(dqs[0], dqs[1], lane)
        cs, s1, s2 = cs_ref[...], s1_ref[...], s2_ref[...]
        for b in range(4):
            dq_ref[:, 128 * b:128 * b + 128] = _rope_t(dq[:, 128 * b:128 * b + 128], cs, s1, s2).astype(BF16)

        @pl.when(n == nb - 1)
        def _():
            dk = dk_acc[ATTN_BLOCK:ATTN_BLOCK + s, :]
            dkv_ref[:, 0:128] = _rope_t(dk, csf_ref[...], s1f_ref[...], s2f_ref[...]).astype(BF16)
            dkv_ref[:, 128:256] = dv_acc[ATTN_BLOCK:ATTN_BLOCK + s, :].astype(BF16)

    tile = _rows(ATTN_BLOCK, 512)
    tab = _rows(ATTN_BLOCK, 128)
    return pl.pallas_call(
        body, name="attn_bwd", grid=(nb,),
        out_shape=[jax.ShapeDtypeStruct((s, 512), BF16), jax.ShapeDtypeStruct((s, 512), BF16),
                   jax.ShapeDtypeStruct((s, 256), BF16), jax.ShapeDtypeStruct((8, 128), F32)],
        in_specs=[pl.BlockSpec(memory_space=pltpu.SMEM), tile, tile, tile, tab, tab, tab,
                  _full((s, 256)), _full((s, 128)), _full((s, 128)), _full((s, 128))],
        out_specs=[tile, tile, _full((s, 256)), _full((8, 128))],
        scratch_shapes=[pltpu.VMEM((s + 2 * ATTN_BLOCK, 128), BF16)] * 2
        + [pltpu.VMEM((s + 2 * ATTN_BLOCK, 128), F32)] * 2,
        compiler_params=_params(48),
    )(sink, pq, pbz, dyb, *tabs, pkv, *tabs)


def _mem_attn_bwd(pmq, pmz, mkv, dym):
    s = pmq.shape[0]
    m = mkv.shape[0]
    tm = min(512, s)

    def body(q_ref, z_ref, d_ref, mk_ref, mv_ref, dq_ref, dz_ref, dmkv_ref):
        @pl.when(pl.program_id(0) == 0)
        def _():
            dmkv_ref[...] = jnp.zeros_like(dmkv_ref)

        z = z_ref[...].astype(F32)
        sig = _sigmoid(z)
        dym_t = d_ref[...]
        d_attn = dym_t * (z * sig)
        dsilu = sig * (1.0 + z * (1.0 - sig))
        for h in range(MEM_HEADS):
            cols = slice(128 * h, 128 * h + 128)
            q, mk, mv = q_ref[:, cols], mk_ref[:, cols], mv_ref[:, cols]
            pt = _mem_softmax_t(q, mk)
            pb = pt.astype(BF16)
            o = _dot_tn(pb, mv)
            dob = d_attn[:, cols].astype(BF16)
            dpt = _dot_nt(mv, dob)
            dst = (pt * (dpt - jnp.sum(pt * dpt, axis=0, keepdims=True))).astype(BF16)
            dq_ref[:, cols] = (_dot_tn(dst, mk) * MEM_SCALE).astype(BF16)
            dz_ref[:, cols] = (dym_t[:, cols] * o * dsilu[:, cols]).astype(BF16)
            dmkv_ref[:, cols] += _dot(dst, q) * MEM_SCALE
            dmkv_ref[:, 512 + 128 * h:512 + 128 * h + 128] += _dot(pb, dob)

    return pl.pallas_call(
        body, name="mem_attn_bwd", grid=(s // tm,),
        out_shape=[jax.ShapeDtypeStruct((s, 512), BF16), jax.ShapeDtypeStruct((s, 512), BF16),
                   jax.ShapeDtypeStruct((m, D_MODEL), F32)],
        in_specs=[_rows(tm, 512), _rows(tm, 512), _rows(tm, 512), pl.BlockSpec((m, 512), lambda i: (0, 0)),
                  pl.BlockSpec((m, 512), lambda i: (0, 1))],
        out_specs=[_rows(tm, 512), _rows(tm, 512), _full((m, D_MODEL))],
        compiler_params=_params(32),
    )(pmq, pmz, dym, mkv, mkv)


def _mem_kv_bwd(mem, g_mem, mn, dmkv, w_mkv):
    m = mem.shape[0]

    def body(mem_ref, g_ref, mn_ref, d_ref, w_ref, gw_ref, gg_ref):
        db = d_ref[...].astype(BF16)
        gw_ref[...] = _dot_tn(mn_ref[...], db).astype(BF16)
        d_mn = _dot_nt(db, w_ref[...])
        xf = mem_ref[...]
        r = lax.rsqrt(jnp.mean(xf * xf, axis=-1, keepdims=True) + EPS)
        gg_ref[...] = jnp.sum(d_mn * (xf * r), axis=0, keepdims=True)

    return pl.pallas_call(
        body, name="mem_kv_bwd", grid=(1,),
        out_shape=[jax.ShapeDtypeStruct((D_MODEL, D_MODEL), BF16), jax.ShapeDtypeStruct((1, D_MODEL), F32)],
        in_specs=[_full((m, D_MODEL)), _full((1, D_MODEL)), _full((m, D_MODEL)), _full((m, D_MODEL)),
                  _full((D_MODEL, D_MODEL))],
        out_specs=[_full((D_MODEL, D_MODEL)), _full((1, D_MODEL))],
        compiler_params=_params(32),
    )(mem, g_mem, mn, dmkv, w_mkv)


def _dh_bwd(dparts, x, dy, g_pre, w_int):
    s = x.shape[0]
    tm = min(256, s)

    def body(*refs):
        d_refs = refs[:7]
        x_ref, dy_ref, g_ref, w_hbm, gx_ref, gg_ref, w_vm, sems = refs[7:]
        _load_once([(w_hbm, w_vm)], sems)

        @pl.when(pl.program_id(0) == 0)
        def _():
            gg_ref[...] = jnp.zeros_like(gg_ref)

        d_h = jnp.zeros((tm, D_MODEL), F32)
        for d_ref, (r0, width) in zip(d_refs, SEGS):
            for c0 in range(0, width, 512):
                cw = min(512, width - c0)
                d_h += _dot(d_ref[:, c0:c0 + cw], w_vm[r0 + c0:r0 + c0 + cw, :])
        xf = x_ref[...]
        r = lax.rsqrt(jnp.mean(xf * xf, axis=-1, keepdims=True) + EPS)
        xn = xf * r
        a = d_h * g_ref[...]
        gx_ref[...] = r * (a - xn * jnp.mean(a * xn, axis=-1, keepdims=True)) + dy_ref[...]
        gg_ref[...] += jnp.sum(d_h * xn, axis=0, keepdims=True)

    return pl.pallas_call(
        body, name="dh_bwd", grid=(s // tm,),
        out_shape=[jax.ShapeDtypeStruct((s, D_MODEL), F32), jax.ShapeDtypeStruct((1, D_MODEL), F32)],
        in_specs=[_rows(tm, w) for _, w in SEGS] + [_rows(tm, D_MODEL), _rows(tm, D_MODEL), _full((1, D_MODEL)), ANY],
        out_specs=[_rows(tm, D_MODEL), _full((1, D_MODEL))],
        scratch_shapes=[pltpu.VMEM((IN_WIDTH, D_MODEL), BF16), pltpu.SemaphoreType.DMA((1,))],
        compiler_params=_params(52),
    )(*dparts, x, dy, g_pre, w_int)


def _gw_in(dparts, h):
    s = h.shape[0]
    tn = 256
    starts, counts = [], []
    for r0, width in SEGS:
        starts.append(r0 // tn)
        counts.append(width // tn)

    def body(*refs):
        d_refs = refs[:7]
        h_hbm, o_ref, h_vm, sems = refs[7:]
        _load_once([(h_hbm, h_vm)], sems)
        j = pl.program_id(0)
        for d_ref, st, cnt in zip(d_refs, starts, counts):
            @pl.when((j >= st) & (j < st + cnt))
            def _(d_ref=d_ref):
                o_ref[...] = _dot_tn(d_ref[...], h_vm[...]).astype(BF16)

    def seg_spec(st, cnt):
        return pl.BlockSpec((s, tn), lambda j: (0, jnp.clip(j - st, 0, cnt - 1)))

    return pl.pallas_call(
        body, name="gw_in", grid=(IN_WIDTH // tn,),
        out_shape=jax.ShapeDtypeStruct((IN_WIDTH, D_MODEL), BF16),
        in_specs=[seg_spec(st, cnt) for st, cnt in zip(starts, counts)] + [ANY],
        out_specs=pl.BlockSpec((tn, D_MODEL), lambda j: (j, 0)),
        scratch_shapes=[pltpu.VMEM((s, D_MODEL), BF16), pltpu.SemaphoreType.DMA((1,))],
        compiler_params=_params(52),
    )(*dparts, h)


def _adamw_math(w, g, m, v):
    m2 = ADAM_B1 * m + (1.0 - ADAM_B1) * g
    v2 = ADAM_B2 * v + (1.0 - ADAM_B2) * (g * g)
    m_hat = m2 / (1.0 - ADAM_B1 ** ADAM_STEP)
    v_hat = v2 / (1.0 - ADAM_B2 ** ADAM_STEP)
    delta = -ADAM_LR * (m_hat / (jnp.sqrt(v_hat) + ADAM_EPS) + ADAM_WD * w)
    return delta, m2, v2


def _sum_adamw(parts, block, w, m, v, name, tiles=1):
    r, c = w.shape
    rt = r // tiles

    def body(p_ref, w_ref, m_ref, v_ref, g_ref, d_ref, m2_ref, v2_ref):
        g = p_ref[0].astype(F32)
        for k in range(1, N_CHIPS):
            g += p_ref[k].astype(F32)
        g_ref[...] = g
        d_ref[...], m2_ref[...], v2_ref[...] = _adamw_math(w_ref[...], g, m_ref[...], v_ref[...])

    spec = pl.BlockSpec((rt, c), lambda i: (i, 0))
    return pl.pallas_call(
        body, name=name, grid=(tiles,),
        out_shape=[jax.ShapeDtypeStruct((r, c), F32)] * 4,
        in_specs=[pl.BlockSpec((N_CHIPS, rt, c), lambda i: (0, block * tiles + i, 0))] + [spec] * 3,
        out_specs=[spec] * 4,
        compiler_params=_params(48),
    )(parts, w, m, v)


def _pack_sum(packs):
    def body(p_ref, o_ref):
        acc = p_ref[0]
        for k in range(1, N_DEV):
            acc += p_ref[k]
        o_ref[...] = acc

    return pl.pallas_call(
        body, name="pack_sum", grid=(1,),
        out_shape=jax.ShapeDtypeStruct((8, D_MODEL), F32),
        in_specs=[_full((N_DEV, 8, D_MODEL))], out_specs=_full((8, D_MODEL)),
    )(packs)


def _small_adamw(ws, gs, ms, vs):
    k = len(ws)

    def body(*refs):
        w_refs, g_refs, m_refs, v_refs = (refs[j * k:(j + 1) * k] for j in range(4))
        outs = refs[4 * k:]
        for j in range(k):
            outs[j][...], outs[k + j][...], outs[2 * k + j][...] = _adamw_math(
                w_refs[j][...], g_refs[j][...], m_refs[j][...], v_refs[j][...])

    specs = [_full(w.shape) for w in ws]
    res = pl.pallas_call(
        body, name="small_adamw", grid=(1,),
        out_shape=[jax.ShapeDtypeStruct(w.shape, F32) for w in ws] * 3,
        in_specs=specs * 4, out_specs=specs * 3,
    )(*ws, *gs, *ms, *vs)
    return res[:k], res[k:2 * k], res[2 * k:]


def kernel(x, mem, g_pre, w_in, w_conv, attn_sink, g_mem, w_mem_kv, w_up_a, w_up_b, w_up_m, w_out, g_post, loss_target, m_g_pre, m_w_in, m_w_conv, m_attn_sink, m_g_mem, m_w_mem_kv, m_w_up_a, m_w_up_b, m_w_up_m, m_w_out, m_g_post, v_g_pre, v_w_in, v_w_conv, v_attn_sink, v_g_mem, v_w_mem_kv, v_w_up_a, v_w_up_b, v_w_up_m, v_w_out, v_g_post):
    s = x.shape[1]
    x2, mem2, tgt2 = x[0], mem[0], loss_target[0]
    me = 4 * lax.axis_index("x") + 2 * lax.axis_index("y") + lax.axis_index("c")

    w_up_loc = jnp.concatenate([w_up_a[0], w_up_b[0], w_up_m[0]], axis=0).astype(BF16)
    w_conv_loc = jnp.zeros((8, 128), F32).at[:3, :64].set(w_conv[0])
    w_int_g, w_mkv_g, w_out_g, w_up_g, w_conv_g = _all_gather(
        [w_in[0].T.astype(BF16), w_mem_kv[0].astype(BF16), w_out[0].astype(BF16), w_up_loc, w_conv_loc],
        "gather_weights")
    w_int = w_int_g.reshape(IN_WIDTH, D_MODEL)
    w_mkv = w_mkv_g.reshape(D_MODEL, D_MODEL)
    w_out_f = w_out_g.reshape(D_MODEL, D_MODEL)
    w_conv_f = w_conv_g[:, :3, :64].transpose(1, 0, 2).reshape(3, 512)
    sink = attn_sink[0]
    tabs = _rope_tables(s)

    h, pa, pq, pkv, pbz, pmq, pmz, pg = _proj_fwd(x2, g_pre, w_int, tabs)
    mn, mkv = _mem_kv_fwd(mem2, g_mem, w_mkv)
    ya = _conv_fwd(pa, w_conv_f)
    yb = _attn_fwd(pq, pkv, pbz, sink)
    ym = _mem_attn_fwd(pmq, pmz, mkv)
    dg, dya, dyb, dym, dy, loss_p, gg_post, gw_out, gw_up = _mid(ya, yb, ym, pg, x2, tgt2, g_post, w_up_g, w_out_f)

    da, gw_conv = _conv_bwd(pa, dya, w_conv_f)
    dq, dbz, dkv, g_sink = _attn_bwd(pq, pkv, pbz, dyb, sink, tabs)
    dmq, dmz, dmkv = _mem_attn_bwd(pmq, pmz, mkv, dym)
    gw_mkv, gg_mem = _mem_kv_bwd(mem2, g_mem, mn, dmkv, w_mkv)
    dparts = (da, dq, dkv, dbz, dmq, dmz, dg)
    grad_x, gg_pre = _dh_bwd(dparts, x2, dy, g_pre, w_int)
    gw_int = _gw_in(dparts, h)

    shares = [gw_int.reshape(N_DEV, SHARD_IN, D_MODEL), gw_mkv.reshape(N_DEV, 128, D_MODEL),
              gw_out.reshape(N_DEV, 128, D_MODEL), gw_up]
    core = lax.axis_index("c").astype(jnp.int32).reshape(1)
    from_sibling = _sibling_exchange(shares, "grads_to_sibling")
    chip_shares = _pair_add(shares, from_sibling, core, "grads_pair_add")
    r_int, r_mkv, r_out, r_up = _chip_exchange(chip_shares, "grads_to_chips")
    row3 = jnp.concatenate([gw_conv[0:1], gw_conv[1:2]], axis=1)
    row4 = jnp.concatenate([gw_conv[2:3], g_sink[:, 0].reshape(1, 8), loss_p[0:1, 0:1],
                            jnp.zeros((1, 512 - 9), F32)], axis=1)
    pack = jnp.concatenate([gg_pre, gg_mem, gg_post, row3, row4, jnp.zeros((3, D_MODEL), F32)], axis=0)
    (packs,) = _all_gather([pack], "gather_small")
    tot = _pack_sum(packs)

    g_w_in, d_w_in, nm_w_in, nv_w_in = (t.T for t in _sum_adamw(
        r_int, 0, w_in[0].T, m_w_in[0].T, v_w_in[0].T, "adamw_w_in", tiles=2))
    g_mkv, d_mkv, nm_mkv, nv_mkv = _sum_adamw(r_mkv, 0, w_mem_kv[0], m_w_mem_kv[0], v_w_mem_kv[0], "adamw_w_mem_kv")
    g_out, d_out, nm_out, nv_out = _sum_adamw(r_out, 0, w_out[0], m_w_out[0], v_w_out[0], "adamw_w_out")
    up = [_sum_adamw(r_up, k, w[0], m[0], v[0], "adamw_w_up_" + "abm"[k])
          for k, (w, m, v) in enumerate([(w_up_a, m_w_up_a, v_w_up_a), (w_up_b, m_w_up_b, v_w_up_b),
                                         (w_up_m, m_w_up_m, v_w_up_m)])]

    g_g_pre, g_g_mem, g_g_post = tot[0:1], tot[1:2], tot[2:3]
    g_conv_full = jnp.concatenate([tot[3:4, 0:512], tot[3:4, 512:1024], tot[4:5, 0:512]], axis=0)
    g_conv = lax.dynamic_slice(g_conv_full, (0, 64 * me), (3, 64))
    g_sink_tot = tot[4:5, 512:520]
    loss = tot[4, 520]
    small_w = [g_pre, w_conv[0], attn_sink, g_mem, g_post]
    small_g = [g_g_pre, g_conv, g_sink_tot, g_g_mem, g_g_post]
    small_m = [m_g_pre, m_w_conv[0], m_attn_sink, m_g_mem, m_g_post]
    small_v = [v_g_pre, v_w_conv[0], v_attn_sink, v_g_mem, v_g_post]
    sd, sm, sv = _small_adamw(small_w, small_g, small_m, small_v)

    def lead(a):
        return a[None]

    grads = [g_g_pre, lead(g_w_in), lead(g_conv), g_sink_tot, g_g_mem, lead(g_mkv), lead(up[0][0]),
             lead(up[1][0]), lead(up[2][0]), lead(g_out), g_g_post]

    def assemble(small, big_in, big_mkv, big_up, big_out):
        return [small[0], lead(big_in), lead(small[1]), small[2], small[3], lead(big_mkv), lead(big_up[0]),
                lead(big_up[1]), lead(big_up[2]), lead(big_out), small[4]]

    deltas = assemble(sd, d_w_in, d_mkv, [u[1] for u in up], d_out)
    new_m = assemble(sm, nm_w_in, nm_mkv, [u[2] for u in up], nm_out)
    new_v = assemble(sv, nv_w_in, nv_mkv, [u[3] for u in up], nv_out)
    return (loss, grad_x[None], *grads, *deltas, *new_m, *new_v)
```

```python
import functools

import jax
import jax.numpy as jnp
from jax import lax
from jax.experimental import pallas as pl
from jax.experimental.pallas import tpu as pltpu

F32 = jnp.float32
BF16 = jnp.bfloat16
MESH = pl.DeviceIdType.MESH

N_DEV = 8
D_MODEL = 1024
EPS = 1e-6
ROPE_THETA = 500000.0
ROT_DIM = 16
HEAD_DIM = 64
ATTN_BLOCK = 128
MEM_HEADS = 4
MEM_HEAD_DIM = 128
ATTN_SCALE = HEAD_DIM ** -0.5
MEM_SCALE = MEM_HEAD_DIM ** -0.5

ADAM_LR = 0.001
ADAM_B1 = 0.9
ADAM_B2 = 0.999
ADAM_EPS = 1e-08
ADAM_WD = 0.01
ADAM_STEP = 10

SEG_A = (0, 2048)
SEG_BQ = (2048, 512)
SEG_BKV = (2560, 256)
SEG_BZ = (2816, 512)
SEG_MQ = (3328, 512)
SEG_MZ = (3840, 512)
SEG_G = (4352, 3072)
SEGS = (SEG_A, SEG_BQ, SEG_BKV, SEG_BZ, SEG_MQ, SEG_MZ, SEG_G)
IN_WIDTH = 7424
SHARD_IN = IN_WIDTH // N_DEV

V7X_VMEM_BYTES = 64 * 1024 * 1024
ANY = pl.BlockSpec(memory_space=pl.ANY)


def _params(vmem_mb):
    assert vmem_mb * 1024 * 1024 < V7X_VMEM_BYTES
    return pltpu.CompilerParams(dimension_semantics=("arbitrary",), vmem_limit_bytes=vmem_mb * 1024 * 1024)


def _full(shape):
    zeros = (0,) * len(shape)
    return pl.BlockSpec(shape, lambda i: zeros)


def _rows(tm, width):
    return pl.BlockSpec((tm, width), lambda i: (i, 0))


def _dot(a, b):
    return jnp.dot(a, b, preferred_element_type=F32)


def _dot_nt(a, b):
    return lax.dot_general(a, b, (((1,), (1,)), ((), ())), preferred_element_type=F32)


def _dot_tn(a, b):
    return lax.dot_general(a, b, (((0,), (0,)), ((), ())), preferred_element_type=F32)


def _sigmoid(z):
    return 1.0 / (1.0 + jnp.exp(-z))


def _rope(t, cs, s1, s2):
    return t * cs + pltpu.roll(t, 120, 1) * s1 + pltpu.roll(t, 8, 1) * s2


def _rope_t(d, cs, s1, s2):
    return d * cs + pltpu.roll(d * s1, 8, 1) + pltpu.roll(d * s2, 120, 1)


def _rope_tables(s):
    half = ROT_DIM // 2
    inv_freq = jnp.power(jnp.float32(ROPE_THETA), -jnp.arange(half, dtype=F32) * (2.0 / ROT_DIM))
    d = jnp.arange(128) % HEAD_DIM
    ang = jnp.arange(s).astype(F32)[:, None] * inv_freq[d % half][None, :]
    cos, sin = jnp.cos(ang), jnp.sin(ang)
    lo, hi = (d < half)[None, :], ((d >= half) & (d < ROT_DIM))[None, :]
    return jnp.where(lo | hi, cos, 1.0), jnp.where(lo, -sin, 0.0), jnp.where(hi, sin, 0.0)


def _load_once(pairs, sems):
    @pl.when(pl.program_id(0) == 0)
    def _():
        cps = [pltpu.make_async_copy(src, dst, sems.at[k]) for k, (src, dst) in enumerate(pairs)]
        for cp in cps:
            cp.start()
        for cp in cps:
            cp.wait()


def _my_place():
    x, y, c = lax.axis_index("x"), lax.axis_index("y"), lax.axis_index("c")
    return x, y, c


def _all_gather(arrs, name):
    n = len(arrs)

    def body(*refs):
        ins, outs = refs[:n], refs[n:2 * n]
        send_sems, recv_sems, local_sems = refs[2 * n:]
        x, y, c = _my_place()
        me, sibling = (x, y, c), (x, y, 1 - c)
        chips = [(1 - x, y), (x, 1 - y), (1 - x, 1 - y)]

        def idx(px, py, pc):
            return 4 * px + 2 * py + pc

        def copy(a, k, block, to, src=None):
            dst = outs[a].at[idx(*block)]
            return pltpu.make_async_remote_copy(
                src_ref=dst if src is None else src, dst_ref=dst,
                send_sem=send_sems.at[a * 7 + k], recv_sem=recv_sems.at[a * 7 + k],
                device_id=to, device_id_type=MESH)

        mine = [pltpu.make_async_copy(ins[a], outs[a].at[idx(*me)], local_sems.at[a]) for a in range(n)]
        for cp in mine:
            cp.start()
        first = []
        for a in range(n):
            first.append(copy(a, 0, me, sibling, src=ins[a]))
        for j, chip in enumerate(chips):
            for a in range(n):
                first.append(copy(a, 1 + j, me, (*chip, c), src=ins[a]))
        for cp in first:
            cp.start()
        passed = []
        for j, chip in enumerate(chips):
            for a in range(n):
                copy(a, 1 + j, (*chip, c), me).wait_recv()
                cp = copy(a, 4 + j, (*chip, c), sibling)
                cp.start()
                passed.append(cp)
        for a in range(n):
            copy(a, 0, sibling, me).wait_recv()
        for j, chip in enumerate(chips):
            for a in range(n):
                copy(a, 4 + j, (*chip, 1 - c), me).wait_recv()
        for cp in first + passed:
            cp.wait_send()
        for cp in mine:
            cp.wait()

    return pl.pallas_call(
        body, name=name,
        out_shape=[jax.ShapeDtypeStruct((N_DEV,) + a.shape, a.dtype) for a in arrs],
        in_specs=[ANY] * n, out_specs=[ANY] * n,
        scratch_shapes=[pltpu.SemaphoreType.DMA((7 * n,)), pltpu.SemaphoreType.DMA((7 * n,)),
                        pltpu.SemaphoreType.DMA((n,))],
    )(*arrs)


N_CHIPS = 4


def _sibling_exchange(arrs, name):
    n = len(arrs)

    def body(*refs):
        ins, outs = refs[:n], refs[n:2 * n]
        send_sems, recv_sems = refs[2 * n:]
        x, y, c = _my_place()
        sibling = (x, y, 1 - c)

        def copy(a, j):
            return pltpu.make_async_remote_copy(
                src_ref=ins[a].at[2 * j + (1 - c)], dst_ref=outs[a].at[j],
                send_sem=send_sems.at[a * N_CHIPS + j], recv_sem=recv_sems.at[a * N_CHIPS + j],
                device_id=sibling, device_id_type=MESH)

        cps = [copy(a, j) for j in range(N_CHIPS) for a in range(n)]
        for cp in cps:
            cp.start()
        for cp in cps:
            cp.wait_recv()
        for cp in cps:
            cp.wait_send()

    return pl.pallas_call(
        body, name=name,
        out_shape=[jax.ShapeDtypeStruct((N_CHIPS,) + a.shape[1:], a.dtype) for a in arrs],
        in_specs=[ANY] * n, out_specs=[ANY] * n,
        scratch_shapes=[pltpu.SemaphoreType.DMA((N_CHIPS * n,)), pltpu.SemaphoreType.DMA((N_CHIPS * n,))],
    )(*arrs)


def _pair_add(mine, recv, core, name):
    n = len(mine)

    def body(c_ref, *refs):
        for a in range(n):
            refs[2 * n + a][...] = (refs[a][...].astype(F32) + refs[n + a][...].astype(F32)).astype(BF16)

    def blk(a):
        return (None,) + a.shape[1:]

    grid_spec = pltpu.PrefetchScalarGridSpec(
        num_scalar_prefetch=1, grid=(N_CHIPS,),
        in_specs=[pl.BlockSpec(blk(a), lambda j, c_ref: (2 * j + c_ref[0], 0, 0)) for a in mine]
        + [pl.BlockSpec(blk(a), lambda j, c_ref: (j, 0, 0)) for a in recv],
        out_specs=[pl.BlockSpec(blk(a), lambda j, c_ref: (j, 0, 0)) for a in recv])
    return pl.pallas_call(
        body, name=name, grid_spec=grid_spec,
        out_shape=[jax.ShapeDtypeStruct(a.shape, BF16) for a in recv],
        compiler_params=_params(32),
    )(core, *mine, *recv)


HBM = pl.BlockSpec(memory_space=pltpu.HBM)
SEM = pl.BlockSpec(memory_space=pltpu.SEMAPHORE)
N_PEER_CHIPS = 3


def _chip_copies(srcs, lands, send_sems, recv_sems):
    x, y, c = _my_place()
    my_chip = 2 * x + y
    peers = [(x, 1 - y), (1 - x, y), (1 - x, 1 - y)]
    cps = []
    for k, (px, py) in enumerate(peers):
        for a in range(len(srcs)):
            j = a * N_PEER_CHIPS + k
            cps.append(pltpu.make_async_remote_copy(
                src_ref=srcs[a].at[2 * px + py], dst_ref=lands[a].at[my_chip],
                send_sem=send_sems[j], recv_sem=recv_sems[j],
                device_id=(px, py, c), device_id_type=MESH))
    return cps


def _chip_exchange_start(arrs, name):
    n = len(arrs)
    k = n * N_PEER_CHIPS

    def body(*refs):
        srcs, lands = refs[:n], refs[n:2 * n]
        send_sems, recv_sems = refs[2 * n:2 * n + k], refs[2 * n + k:2 * n + 2 * k]
        token = refs[-1]
        for cp in _chip_copies(srcs, lands, send_sems, recv_sems):
            cp.start()
        token[...] = jnp.zeros_like(token)

    hbm_arrs = [pltpu.with_memory_space_constraint(a, pltpu.HBM) for a in arrs]
    lands = [pltpu.with_memory_space_constraint(lax.empty(a.shape, a.dtype), pltpu.HBM) for a in arrs]
    res = pl.pallas_call(
        body, name=name,
        out_shape=[pltpu.SemaphoreType.DMA(())] * (2 * k) + [pltpu.HBM(a.shape, a.dtype) for a in arrs] * 2
        + [jax.ShapeDtypeStruct((8, 128), F32)],
        in_specs=[HBM] * (2 * n),
        out_specs=[SEM] * (2 * k) + [HBM] * (2 * n) + [pl.BlockSpec(memory_space=pltpu.VMEM)],
        input_output_aliases={a: 2 * k + a for a in range(2 * n)},
        compiler_params=pltpu.CompilerParams(has_side_effects=pltpu.SideEffectType.DATAFLOW_SIDE_EFFECTING),
    )(*hbm_arrs, *lands)
    return res[:k], res[k:2 * k], res[2 * k:2 * k + n], res[2 * k + n:2 * k + 2 * n], res[-1]


def _chip_exchange_wait(send_sems, recv_sems, srcs, lands, after, name):
    n = len(srcs)
    k = n * N_PEER_CHIPS

    def body(*refs):
        src_refs, land_refs = refs[:n], refs[n:2 * n]
        s_sems, r_sems = refs[2 * n:2 * n + k], refs[2 * n + k:2 * n + 2 * k]
        for cp in _chip_copies(src_refs, land_refs, s_sems, r_sems):
            cp.wait_send()
            cp.wait_recv()

    res = pl.pallas_call(
        body, name=name,
        out_shape=[pltpu.HBM(a.shape, a.dtype) for a in srcs] * 2,
        in_specs=[HBM] * (2 * n) + [SEM] * (2 * k) + [ANY],
        out_specs=[HBM] * (2 * n),
        input_output_aliases={a: a for a in range(2 * n)},
        compiler_params=pltpu.CompilerParams(has_side_effects=pltpu.SideEffectType.DATAFLOW_SIDE_EFFECTING),
    )(*srcs, *lands, *send_sems, *recv_sems, after)
    return res[:n], res[n:]


def _proj_fwd(x, g_pre, w_int, tabs):
    s = x.shape[0]
    tm = min(512, s)

    def body(x_ref, g_ref, cs_ref, s1_ref, s2_ref, w_hbm,
             h_ref, pa_ref, pq_ref, pkv_ref, pbz_ref, pmq_ref, pmz_ref, pg_ref, w_vm, sems):
        _load_once([(w_hbm, w_vm)], sems)
        xf = x_ref[...]
        r = lax.rsqrt(jnp.mean(xf * xf, axis=-1, keepdims=True) + EPS)
        h = ((xf * r) * g_ref[...]).astype(BF16)
        h_ref[...] = h
        cs, s1, s2 = cs_ref[...], s1_ref[...], s2_ref[...]

        def mm(seg, c0, width):
            return _dot_nt(h, w_vm[seg[0] + c0:seg[0] + c0 + width, :])

        for c0 in range(0, SEG_A[1], 512):
            pa_ref[:, c0:c0 + 512] = mm(SEG_A, c0, 512).astype(BF16)
        q = mm(SEG_BQ, 0, 512)
        for b in range(4):
            pq_ref[:, 128 * b:128 * b + 128] = _rope(q[:, 128 * b:128 * b + 128], cs, s1, s2).astype(BF16)
        kv = mm(SEG_BKV, 0, 256)
        pkv_ref[:, 0:128] = _rope(kv[:, 0:128], cs, s1, s2).astype(BF16)
        pkv_ref[:, 128:256] = kv[:, 128:256].astype(BF16)
        pbz_ref[...] = mm(SEG_BZ, 0, 512).astype(BF16)
        pmq_ref[...] = mm(SEG_MQ, 0, 512).astype(BF16)
        pmz_ref[...] = mm(SEG_MZ, 0, 512).astype(BF16)
        for c0 in range(0, SEG_G[1], 512):
            pg_ref[:, c0:c0 + 512] = mm(SEG_G, c0, 512).astype(BF16)

    widths = (D_MODEL, 2048, 512, 256, 512, 512, 512, 3072)
    return pl.pallas_call(
        body, name="proj_fwd", grid=(s // tm,),
        out_shape=[jax.ShapeDtypeStruct((s, w), BF16) for w in widths],
        in_specs=[_rows(tm, D_MODEL), _full((1, D_MODEL)), _rows(tm, 128), _rows(tm, 128), _rows(tm, 128), ANY],
        out_specs=[_rows(tm, w) for w in widths],
        scratch_shapes=[pltpu.VMEM((IN_WIDTH, D_MODEL), BF16), pltpu.SemaphoreType.DMA((1,))],
        compiler_params=_params(52),
    )(x, g_pre, *tabs, w_int)


def _mem_kv_fwd(mem, g_mem, w_mkv):
    m = mem.shape[0]

    def body(mem_ref, g_ref, w_ref, mn_ref, mkv_ref):
        xf = mem_ref[...]
        r = lax.rsqrt(jnp.mean(xf * xf, axis=-1, keepdims=True) + EPS)
        mn = ((xf * r) * g_ref[...]).astype(BF16)
        mn_ref[...] = mn
        mkv_ref[...] = _dot(mn, w_ref[...]).astype(BF16)

    return pl.pallas_call(
        body, name="mem_kv_fwd", grid=(1,),
        out_shape=[jax.ShapeDtypeStruct((m, D_MODEL), BF16)] * 2,
        in_specs=[_full((m, D_MODEL)), _full((1, D_MODEL)), _full((D_MODEL, D_MODEL))],
        out_specs=[_full((m, D_MODEL))] * 2,
        compiler_params=_params(32),
    )(mem, g_mem, w_mkv)


def _halo_specs(s, tm, rows, width):
    nblk = s // rows
    prev = pl.BlockSpec((rows, width), lambda i: (jnp.maximum(i * (tm // rows) - 1, 0), 0))
    nxt = pl.BlockSpec((rows, width), lambda i: (jnp.minimum((i + 1) * (tm // rows), nblk - 1), 0))
    return prev, nxt


def _conv_common(pa, prev_row, next_row, w, first, last, tm):
    b, c, u, z = (pa[:, 512 * k:512 * k + 512] for k in range(4))
    cu = c * u
    cu_prev = jnp.where(first, 0.0, prev_row[:, 512:1024] * prev_row[:, 1024:1536])
    cu_next = jnp.where(last, 0.0, next_row[:, 512:1024] * next_row[:, 1024:1536])
    row = lax.broadcasted_iota(jnp.int32, (tm, 512), 0)
    cu_m1 = jnp.where(row == 0, cu_prev, pltpu.roll(cu, 1, 0))
    cu_p1 = jnp.where(row == tm - 1, cu_next, pltpu.roll(cu, tm - 1, 0))
    y = cu_m1 * w[0:1] + cu * w[1:2] + cu_p1 * w[2:3]
    sig = _sigmoid(z)
    return b, c, u, z, cu, cu_m1, cu_p1, y, sig, row


def _conv_fwd(pa, w_conv):
    s = pa.shape[0]
    tm = min(512, s)
    nt = s // tm

    def body(pa_ref, pp_ref, pn_ref, w_ref, ya_ref):
        i = pl.program_id(0)
        prev_row = pp_ref[...].astype(F32)[15:16, :]
        next_row = pn_ref[...].astype(F32)[0:1, :]
        b, _, _, z, _, _, _, y, sig, _ = _conv_common(
            pa_ref[...].astype(F32), prev_row, next_row, w_ref[...], i == 0, i == nt - 1, tm)
        ya_ref[...] = (b * y * (z * sig)).astype(BF16)

    prev, nxt = _halo_specs(s, tm, 16, 2048)
    return pl.pallas_call(
        body, name="conv_fwd", grid=(nt,),
        out_shape=jax.ShapeDtypeStruct((s, 512), BF16),
        in_specs=[_rows(tm, 2048), prev, nxt, _full((3, 512))],
        out_specs=_rows(tm, 512),
        compiler_params=_params(48),
    )(pa, pa, pa, w_conv)


def _stack_heads(a, g, lane):
    in_g = (lane >= 64 * g) & (lane < 64 * g + 64)
    parts = []
    for j in range(4):
        h = 4 * g + j
        blk = a[:, 128 * (h // 2):128 * (h // 2) + 128]
        if h % 2 != g:
            blk = pltpu.roll(blk, 64, 1)
        parts.append(jnp.where(in_g, blk, 0.0))
    return jnp.concatenate(parts, axis=0)


def _unstack_heads(o0, o1, lane):
    blocks = []
    for b in range(4):
        g = b // 2
        og = (o0, o1)[g]
        je, jo = (2 * b) % 4, (2 * b + 1) % 4
        even, odd = og[128 * je:128 * je + 128], og[128 * jo:128 * jo + 128]
        if g == 0:
            odd = pltpu.roll(odd, 64, 1)
        else:
            even = pltpu.roll(even, 64, 1)
        blocks.append(jnp.where(lane < 64, even, odd))
    return jnp.concatenate(blocks, axis=1)


def _window_mask(n, s):
    c = lax.broadcasted_iota(jnp.int32, (384, 512), 0)
    r = lax.broadcasted_iota(jnp.int32, (384, 512), 1) & 127
    kpos = c + (n - 1) * ATTN_BLOCK
    return (c >= r) & (c <= r + 2 * ATTN_BLOCK) & (kpos >= 0) & (kpos < s)


def _sink_row(sink_ref, g):
    return jnp.concatenate([jnp.full((1, ATTN_BLOCK), sink_ref[4 * g + j], F32) for j in range(4)], axis=1)


def _attn_group(qg, kw, vw, sink, valid):
    sc = jnp.where(valid, _dot_nt(kw, qg) * ATTN_SCALE, -jnp.inf)
    m = jnp.maximum(jnp.max(sc, axis=0, keepdims=True), sink)
    e = jnp.exp(sc - m)
    es = jnp.exp(sink - m)
    inv = 1.0 / (jnp.sum(e, axis=0, keepdims=True) + es)
    pt = e * inv
    o = _dot_tn(pt.astype(BF16), vw)
    return pt, o, es * inv


def _fill_padded(kv_ref, kpad, vpad, s):
    zero = jnp.zeros((ATTN_BLOCK, 128), BF16)
    kpad[0:ATTN_BLOCK, :] = zero
    vpad[0:ATTN_BLOCK, :] = zero
    kpad[ATTN_BLOCK + s:2 * ATTN_BLOCK + s, :] = zero
    vpad[ATTN_BLOCK + s:2 * ATTN_BLOCK + s, :] = zero
    kpad[ATTN_BLOCK:ATTN_BLOCK + s, :] = kv_ref[:, 0:128]
    vpad[ATTN_BLOCK:ATTN_BLOCK + s, :] = kv_ref[:, 128:256]


def _attn_fwd(pq, pkv, pbz, sink):
    s = pq.shape[0]
    nb = s // ATTN_BLOCK

    def body(sink_ref, q_ref, z_ref, kv_ref, yb_ref, kpad, vpad):
        n = pl.program_id(0)

        @pl.when(n == 0)
        def _():
            _fill_padded(kv_ref, kpad, vpad, s)

        lane = lax.broadcasted_iota(jnp.int32, (ATTN_BLOCK, 128), 1)
        start = pl.multiple_of(n * ATTN_BLOCK, ATTN_BLOCK)
        kw, vw = kpad[pl.ds(start, 384), :], vpad[pl.ds(start, 384), :]
        qf = q_ref[...].astype(F32)
        valid = _window_mask(n, s)
        outs = []
        for g in range(2):
            qg = _stack_heads(qf, g, lane).astype(BF16)
            _, o, _ = _attn_group(qg, kw, vw, _sink_row(sink_ref, g), valid)
            outs.append(o)
        attn = _unstack_heads(outs[0], outs[1], lane)
        z = z_ref[...].astype(F32)
        yb_ref[...] = (attn * (z * _sigmoid(z))).astype(BF16)

    return pl.pallas_call(
        body, name="attn_fwd", grid=(nb,),
        out_shape=jax.ShapeDtypeStruct((s, 512), BF16),
        in_specs=[pl.BlockSpec(memory_space=pltpu.SMEM), _rows(ATTN_BLOCK, 512), _rows(ATTN_BLOCK, 512),
                  _full((s, 256))],
        out_specs=_rows(ATTN_BLOCK, 512),
        scratch_shapes=[pltpu.VMEM((s + 2 * ATTN_BLOCK, 128), BF16)] * 2,
        compiler_params=_params(32),
    )(sink, pq, pbz, pkv)


def _mem_softmax_t(q, mk):
    sc = _dot_nt(mk, q) * MEM_SCALE
    e = jnp.exp(sc - jnp.max(sc, axis=0, keepdims=True))
    return e * (1.0 / jnp.sum(e, axis=0, keepdims=True))


def _mem_attn_fwd(pmq, pmz, mkv):
    s = pmq.shape[0]
    m = mkv.shape[0]
    tm = min(512, s)

    def body(q_ref, z_ref, mk_ref, mv_ref, ym_ref):
        z = z_ref[...].astype(F32)
        sz = z * _sigmoid(z)
        for h in range(MEM_HEADS):
            cols = slice(128 * h, 128 * h + 128)
            pt = _mem_softmax_t(q_ref[:, cols], mk_ref[:, cols])
            o = _dot_tn(pt.astype(BF16), mv_ref[:, cols])
            ym_ref[:, cols] = (o * sz[:, cols]).astype(BF16)

    return pl.pallas_call(
        body, name="mem_attn_fwd", grid=(s // tm,),
        out_shape=jax.ShapeDtypeStruct((s, 512), BF16),
        in_specs=[_rows(tm, 512), _rows(tm, 512), pl.BlockSpec((m, 512), lambda i: (0, 0)),
                  pl.BlockSpec((m, 512), lambda i: (0, 1))],
        out_specs=_rows(tm, 512),
        compiler_params=_params(32),
    )(pmq, pmz, mkv, mkv)


def _mid(ya, yb, ym, pg, x, target, g_post, w_up, w_out):
    s = x.shape[0]
    tm = min(256, s)
    nt = s // tm

    def body(ya_ref, yb_ref, ym_ref, pg_ref, x_ref, t_ref, gp_ref, wup_hbm, wout_hbm,
             dg_ref, dya_ref, dyb_ref, dym_ref, dy_ref, loss_ref, ggp_ref, gwout_hbm, gwup_hbm,
             wup_vm, wout_vm, acc_out, acc_up, st_out, st_up, sems):
        i = pl.program_id(0)
        _load_once([(wup_hbm.at[d], wup_vm.at[:, pl.ds(128 * d, 128)]) for d in range(N_DEV)]
                   + [(wout_hbm, wout_vm)], sems)

        @pl.when(i == 0)
        def _():
            acc_out[...] = jnp.zeros_like(acc_out)
            acc_up[...] = jnp.zeros_like(acc_up)
            loss_ref[...] = jnp.zeros_like(loss_ref)
            ggp_ref[...] = jnp.zeros_like(ggp_ref)

        ys = (ya_ref[...], yb_ref[...], ym_ref[...])
        us = [_dot(ys[k], wup_vm[512 * k:512 * k + 512, :]) for k in range(3)]
        gates = [_sigmoid(pg_ref[:, 1024 * k:1024 * k + 1024].astype(F32)) for k in range(3)]
        merged = gates[0] * us[0] + gates[1] * us[1] + gates[2] * us[2]
        mb = merged.astype(BF16)
        out = _dot(mb, wout_vm[...])
        r = lax.rsqrt(jnp.mean(out * out, axis=-1, keepdims=True) + EPS)
        on = out * r
        gp = gp_ref[...]
        err = (x_ref[...] + on * gp) - t_ref[...]
        loss_ref[...] += 0.5 * jnp.sum(err * err) * (1.0 / D_MODEL)
        dy = err * (1.0 / D_MODEL)
        dy_ref[...] = dy
        ggp_ref[...] += jnp.sum(dy * on, axis=0, keepdims=True)
        a = dy * gp
        d_out = r * (a - on * jnp.mean(a * on, axis=-1, keepdims=True))
        dob = d_out.astype(BF16)
        acc_out[...] += _dot_tn(mb, dob)
        d_merged = _dot_nt(dob, wout_vm[...])
        d_refs = (dya_ref, dyb_ref, dym_ref)
        for k in range(3):
            g = gates[k]
            dg_ref[:, 1024 * k:1024 * k + 1024] = (d_merged * us[k] * g * (1.0 - g)).astype(BF16)
            du = (d_merged * g).astype(BF16)
            d_refs[k][...] = _dot_nt(du, wup_vm[512 * k:512 * k + 512, :])
            acc_up[512 * k:512 * k + 512, :] += _dot_tn(ys[k], du)

        @pl.when(i == nt - 1)
        def _():
            st_out[...] = acc_out[...].astype(BF16)
            for d in range(N_DEV):
                st_up[d] = acc_up[:, 128 * d:128 * d + 128].astype(BF16)
            cps = [pltpu.make_async_copy(st_out, gwout_hbm, sems.at[0]),
                   pltpu.make_async_copy(st_up, gwup_hbm, sems.at[1])]
            for cp in cps:
                cp.start()
            for cp in cps:
                cp.wait()

    return pl.pallas_call(
        body, name="mid", grid=(nt,),
        out_shape=[jax.ShapeDtypeStruct((s, 3072), BF16)] + [jax.ShapeDtypeStruct((s, 512), F32)] * 3
        + [jax.ShapeDtypeStruct((s, D_MODEL), F32), jax.ShapeDtypeStruct((8, 128), F32),
           jax.ShapeDtypeStruct((1, D_MODEL), F32), jax.ShapeDtypeStruct((D_MODEL, D_MODEL), BF16),
           jax.ShapeDtypeStruct((N_DEV, 1536, 128), BF16)],
        in_specs=[_rows(tm, 512)] * 3 + [_rows(tm, 3072), _rows(tm, D_MODEL), _rows(tm, D_MODEL),
                                         _full((1, D_MODEL)), ANY, ANY],
        out_specs=[_rows(tm, 3072)] + [_rows(tm, 512)] * 3 + [_rows(tm, D_MODEL), _full((8, 128)),
                                                               _full((1, D_MODEL)), ANY, ANY],
        scratch_shapes=[pltpu.VMEM((1536, D_MODEL), BF16), pltpu.VMEM((D_MODEL, D_MODEL), BF16),
                        pltpu.VMEM((D_MODEL, D_MODEL), F32), pltpu.VMEM((1536, D_MODEL), F32),
                        pltpu.VMEM((D_MODEL, D_MODEL), BF16), pltpu.VMEM((N_DEV, 1536, 128), BF16),
                        pltpu.SemaphoreType.DMA((N_DEV + 1,))],
        compiler_params=_params(60),
    )(ya, yb, ym, pg, x, target, g_post, w_up, w_out)


def _conv_bwd(pa, dya, w_conv):
    s = pa.shape[0]
    tm = min(512, s)
    nt = s // tm

    def body(pa_ref, pp_ref, pn_ref, d_ref, dp_ref, dn_ref, w_ref, da_ref, gw_ref):
        i = pl.program_id(0)
        first, last = i == 0, i == nt - 1

        @pl.when(first)
        def _():
            gw_ref[...] = jnp.zeros_like(gw_ref)

        w = w_ref[...]
        prev_row = pp_ref[...].astype(F32)[15:16, :]
        next_row = pn_ref[...].astype(F32)[0:1, :]
        b, c, u, z, cu, cu_m1, cu_p1, y, sig, row = _conv_common(
            pa_ref[...].astype(F32), prev_row, next_row, w, first, last, tm)
        sz = z * sig
        dya_t = d_ref[...]
        d_y = dya_t * b * sz

        def halo_dy(p_row, d_row):
            zz = p_row[:, 1536:2048]
            return d_row * p_row[:, 0:512] * (zz * _sigmoid(zz))

        dy_prev = jnp.where(first, 0.0, halo_dy(prev_row, dp_ref[7:8, :]))
        dy_next = jnp.where(last, 0.0, halo_dy(next_row, dn_ref[0:1, :]))
        dy_m1 = jnp.where(row == 0, dy_prev, pltpu.roll(d_y, 1, 0))
        dy_p1 = jnp.where(row == tm - 1, dy_next, pltpu.roll(d_y, tm - 1, 0))
        d_cu = dy_p1 * w[0:1] + d_y * w[1:2] + dy_m1 * w[2:3]
        da_ref[:, 0:512] = (dya_t * y * sz).astype(BF16)
        da_ref[:, 512:1024] = (d_cu * u).astype(BF16)
        da_ref[:, 1024:1536] = (d_cu * c).astype(BF16)
        da_ref[:, 1536:2048] = (dya_t * b * y * (sig * (1.0 + z * (1.0 - sig)))).astype(BF16)
        gw_ref[0:1, :] += jnp.sum(d_y * cu_m1, axis=0, keepdims=True)
        gw_ref[1:2, :] += jnp.sum(d_y * cu, axis=0, keepdims=True)
        gw_ref[2:3, :] += jnp.sum(d_y * cu_p1, axis=0, keepdims=True)

    prev, nxt = _halo_specs(s, tm, 16, 2048)
    dprev, dnxt = _halo_specs(s, tm, 8, 512)
    return pl.pallas_call(
        body, name="conv_bwd", grid=(nt,),
        out_shape=[jax.ShapeDtypeStruct((s, 2048), BF16), jax.ShapeDtypeStruct((8, 512), F32)],
        in_specs=[_rows(tm, 2048), prev, nxt, _rows(tm, 512), dprev, dnxt, _full((3, 512))],
        out_specs=[_rows(tm, 2048), _full((8, 512))],
        compiler_params=_params(48),
    )(pa, pa, pa, dya, dya, dya, w_conv)


def _attn_bwd(pq, pkv, pbz, dyb, sink, tabs):
    s = pq.shape[0]
    nb = s // ATTN_BLOCK

    def body(sink_ref, q_ref, z_ref, d_ref, cs_ref, s1_ref, s2_ref, kv_ref, csf_ref, s1f_ref, s2f_ref,
             dq_ref, dz_ref, dkv_ref, gs_ref, kpad, vpad, dk_acc, dv_acc):
        n = pl.program_id(0)

        @pl.when(n == 0)
        def _():
            _fill_padded(kv_ref, kpad, vpad, s)
            dk_acc[...] = jnp.zeros_like(dk_acc)
            dv_acc[...] = jnp.zeros_like(dv_acc)
            gs_ref[...] = jnp.zeros_like(gs_ref)

        lane = lax.broadcasted_iota(jnp.int32, (ATTN_BLOCK, 128), 1)
        start = pl.multiple_of(n * ATTN_BLOCK, ATTN_BLOCK)
        kw, vw = kpad[pl.ds(start, 384), :], vpad[pl.ds(start, 384), :]
        qf = q_ref[...].astype(F32)
        valid = _window_mask(n, s)
        z = z_ref[...].astype(F32)
        sig = _sigmoid(z)
        dyb_t = d_ref[...]
        d_attn = dyb_t * (z * sig)
        outs, dqs = [], []
        dk_w = jnp.zeros((384, 128), F32)
        dv_w = jnp.zeros((384, 128), F32)
        for g in range(2):
            qg = _stack_heads(qf, g, lane).astype(BF16)
            pt, o, p_sink = _attn_group(qg, kw, vw, _sink_row(sink_ref, g), valid)
            outs.append(o)
            dob = _stack_heads(d_attn, g, lane).astype(BF16)
            dpt = _dot_nt(vw, dob)
            delta = jnp.sum(pt * dpt, axis=0, keepdims=True)
            dst = (pt * (dpt - delta)).astype(BF16)
            sink_part = p_sink * delta
            for j in range(4):
                h = 4 * g + j
                gs_ref[h:h + 1, :] -= jnp.sum(sink_part[:, 128 * j:128 * j + 128])
            dqs.append(_dot_tn(dst, kw) * ATTN_SCALE)
            dk_w += _dot(dst, qg) * ATTN_SCALE
            dv_w += _dot(pt.astype(BF16), dob)
        dk_acc[pl.ds(start, 384), :] += dk_w
        dv_acc[pl.ds(start, 384), :] += dv_w
        attn = _unstack_heads(outs[0], outs[1], lane)
        dz_ref[...] = (dyb_t * attn * (sig * (1.0 + z * (1.0 - sig)))).astype(BF16)
        dq = _unstack_heads(dqs[0], dqs[1], lane)
        cs, s1, s2 = cs_ref[...], s1_ref[...], s2_ref[...]
        for b in range(4):
            dq_ref[:, 128 * b:128 * b + 128] = _rope_t(dq[:, 128 * b:128 * b + 128], cs, s1, s2).astype(BF16)

        @pl.when(n == nb - 1)
        def _():
            dk = dk_acc[ATTN_BLOCK:ATTN_BLOCK + s, :]
            dkv_ref[:, 0:128] = _rope_t(dk, csf_ref[...], s1f_ref[...], s2f_ref[...]).astype(BF16)
            dkv_ref[:, 128:256] = dv_acc[ATTN_BLOCK:ATTN_BLOCK + s, :].astype(BF16)

    tile = _rows(ATTN_BLOCK, 512)
    tab = _rows(ATTN_BLOCK, 128)
    return pl.pallas_call(
        body, name="attn_bwd", grid=(nb,),
        out_shape=[jax.ShapeDtypeStruct((s, 512), BF16), jax.ShapeDtypeStruct((s, 512), BF16),
                   jax.ShapeDtypeStruct((s, 256), BF16), jax.ShapeDtypeStruct((8, 128), F32)],
        in_specs=[pl.BlockSpec(memory_space=pltpu.SMEM), tile, tile, tile, tab, tab, tab,
                  _full((s, 256)), _full((s, 128)), _full((s, 128)), _full((s, 128))],
        out_specs=[tile, tile, _full((s, 256)), _full((8, 128))],
        scratch_shapes=[pltpu.VMEM((s + 2 * ATTN_BLOCK, 128), BF16)] * 2
        + [pltpu.VMEM((s + 2 * ATTN_BLOCK, 128), F32)] * 2,
        compiler_params=_params(48),
    )(sink, pq, pbz, dyb, *tabs, pkv, *tabs)


def _mem_attn_bwd(pmq, pmz, mkv, dym):
    s = pmq.shape[0]
    m = mkv.shape[0]
    tm = min(512, s)

    def body(q_ref, z_ref, d_ref, mk_ref, mv_ref, dq_ref, dz_ref, dmkv_ref):
        @pl.when(pl.program_id(0) == 0)
        def _():
            dmkv_ref[...] = jnp.zeros_like(dmkv_ref)

        z = z_ref[...].astype(F32)
        sig = _sigmoid(z)
        dym_t = d_ref[...]
        d_attn = dym_t * (z * sig)
        dsilu = sig * (1.0 + z * (1.0 - sig))
        for h in range(MEM_HEADS):
            cols = slice(128 * h, 128 * h + 128)
            q, mk, mv = q_ref[:, cols], mk_ref[:, cols], mv_ref[:, cols]
            pt = _mem_softmax_t(q, mk)
            pb = pt.astype(BF16)
            o = _dot_tn(pb, mv)
            dob = d_attn[:, cols].astype(BF16)
            dpt = _dot_nt(mv, dob)
            dst = (pt * (dpt - jnp.sum(pt * dpt, axis=0, keepdims=True))).astype(BF16)
            dq_ref[:, cols] = (_dot_tn(dst, mk) * MEM_SCALE).astype(BF16)
            dz_ref[:, cols] = (dym_t[:, cols] * o * dsilu[:, cols]).astype(BF16)
            dmkv_ref[:, cols] += _dot(dst, q) * MEM_SCALE
            dmkv_ref[:, 512 + 128 * h:512 + 128 * h + 128] += _dot(pb, dob)

    return pl.pallas_call(
        body, name="mem_attn_bwd", grid=(s // tm,),
        out_shape=[jax.ShapeDtypeStruct((s, 512), BF16), jax.ShapeDtypeStruct((s, 512), BF16),
                   jax.ShapeDtypeStruct((m, D_MODEL), F32)],
        in_specs=[_rows(tm, 512), _rows(tm, 512), _rows(tm, 512), pl.BlockSpec((m, 512), lambda i: (0, 0)),
                  pl.BlockSpec((m, 512), lambda i: (0, 1))],
        out_specs=[_rows(tm, 512), _rows(tm, 512), _full((m, D_MODEL))],
        compiler_params=_params(32),
    )(pmq, pmz, dym, mkv, mkv)


def _mem_kv_bwd(mem, g_mem, mn, dmkv, w_mkv):
    m = mem.shape[0]

    def body(mem_ref, g_ref, mn_ref, d_ref, w_ref, gw_ref, gg_ref):
        db = d_ref[...].astype(BF16)
        gw_ref[...] = _dot_tn(mn_ref[...], db).astype(BF16)
        d_mn = _dot_nt(db, w_ref[...])
        xf = mem_ref[...]
        r = lax.rsqrt(jnp.mean(xf * xf, axis=-1, keepdims=True) + EPS)
        gg_ref[...] = jnp.sum(d_mn * (xf * r), axis=0, keepdims=True)

    return pl.pallas_call(
        body, name="mem_kv_bwd", grid=(1,),
        out_shape=[jax.ShapeDtypeStruct((D_MODEL, D_MODEL), BF16), jax.ShapeDtypeStruct((1, D_MODEL), F32)],
        in_specs=[_full((m, D_MODEL)), _full((1, D_MODEL)), _full((m, D_MODEL)), _full((m, D_MODEL)),
                  _full((D_MODEL, D_MODEL))],
        out_specs=[_full((D_MODEL, D_MODEL)), _full((1, D_MODEL))],
        compiler_params=_params(32),
    )(mem, g_mem, mn, dmkv, w_mkv)


def _dh_bwd(dparts, x, dy, g_pre, w_int):
    s = x.shape[0]
    tm = min(256, s)

    def body(*refs):
        d_refs = refs[:7]
        x_ref, dy_ref, g_ref, w_hbm, gx_ref, gg_ref, w_vm, sems = refs[7:]
        _load_once([(w_hbm, w_vm)], sems)

        @pl.when(pl.program_id(0) == 0)
        def _():
            gg_ref[...] = jnp.zeros_like(gg_ref)

        d_h = jnp.zeros((tm, D_MODEL), F32)
        for d_ref, (r0, width) in zip(d_refs, SEGS):
            for c0 in range(0, width, 512):
                cw = min(512, width - c0)
                d_h += _dot(d_ref[:, c0:c0 + cw], w_vm[r0 + c0:r0 + c0 + cw, :])
        xf = x_ref[...]
        r = lax.rsqrt(jnp.mean(xf * xf, axis=-1, keepdims=True) + EPS)
        xn = xf * r
        a = d_h * g_ref[...]
        gx_ref[...] = r * (a - xn * jnp.mean(a * xn, axis=-1, keepdims=True)) + dy_ref[...]
        gg_ref[...] += jnp.sum(d_h * xn, axis=0, keepdims=True)

    return pl.pallas_call(
        body, name="dh_bwd", grid=(s // tm,),
        out_shape=[jax.ShapeDtypeStruct((s, D_MODEL), F32), jax.ShapeDtypeStruct((1, D_MODEL), F32)],
        in_specs=[_rows(tm, w) for _, w in SEGS] + [_rows(tm, D_MODEL), _rows(tm, D_MODEL), _full((1, D_MODEL)), ANY],
        out_specs=[_rows(tm, D_MODEL), _full((1, D_MODEL))],
        scratch_shapes=[pltpu.VMEM((IN_WIDTH, D_MODEL), BF16), pltpu.SemaphoreType.DMA((1,))],
        compiler_params=_params(52),
    )(*dparts, x, dy, g_pre, w_int)


def _gw_in(dparts, h):
    s = h.shape[0]
    tn = 256
    starts, counts = [], []
    for r0, width in SEGS:
        starts.append(r0 // tn)
        counts.append(width // tn)

    def body(*refs):
        d_refs = refs[:7]
        h_hbm, o_ref, h_vm, sems = refs[7:]
        _load_once([(h_hbm, h_vm)], sems)
        j = pl.program_id(0)
        for d_ref, st, cnt in zip(d_refs, starts, counts):
            @pl.when((j >= st) & (j < st + cnt))
            def _(d_ref=d_ref):
                o_ref[...] = _dot_tn(d_ref[...], h_vm[...]).astype(BF16)

    def seg_spec(st, cnt):
        return pl.BlockSpec((s, tn), lambda j: (0, jnp.clip(j - st, 0, cnt - 1)))

    return pl.pallas_call(
        body, name="gw_in", grid=(IN_WIDTH // tn,),
        out_shape=jax.ShapeDtypeStruct((IN_WIDTH, D_MODEL), BF16),
        in_specs=[seg_spec(st, cnt) for st, cnt in zip(starts, counts)] + [ANY],
        out_specs=pl.BlockSpec((tn, D_MODEL), lambda j: (j, 0)),
        scratch_shapes=[pltpu.VMEM((s, D_MODEL), BF16), pltpu.SemaphoreType.DMA((1,))],
        compiler_params=_params(52),
    )(*dparts, h)


def _adamw_math(w, g, m, v):
    m2 = ADAM_B1 * m + (1.0 - ADAM_B1) * g
    v2 = ADAM_B2 * v + (1.0 - ADAM_B2) * (g * g)
    m_hat = m2 / (1.0 - ADAM_B1 ** ADAM_STEP)
    v_hat = v2 / (1.0 - ADAM_B2 ** ADAM_STEP)
    delta = -ADAM_LR * (m_hat / (jnp.sqrt(v_hat) + ADAM_EPS) + ADAM_WD * w)
    return delta, m2, v2


def _sum_adamw(own, land, chip, block, w, m, v, name, tiles=1):
    r, c = w.shape
    rt = r // tiles

    def body(c_ref, own_ref, l1_ref, l2_ref, l3_ref, w_ref, m_ref, v_ref, g_ref, d_ref, m2_ref, v2_ref):
        g = own_ref[...].astype(F32)
        for l_ref in (l1_ref, l2_ref, l3_ref):
            g += l_ref[...].astype(F32)
        g_ref[...] = g
        d_ref[...], m2_ref[...], v2_ref[...] = _adamw_math(w_ref[...], g, m_ref[...], v_ref[...])

    def share(k):
        return pl.BlockSpec((None, rt, c), lambda i, c_ref: (jnp.bitwise_xor(c_ref[0], k), block * tiles + i, 0))

    spec = pl.BlockSpec((rt, c), lambda i, c_ref: (i, 0))
    grid_spec = pltpu.PrefetchScalarGridSpec(
        num_scalar_prefetch=1, grid=(tiles,),
        in_specs=[share(0), share(1), share(2), share(3)] + [spec] * 3, out_specs=[spec] * 4)
    return pl.pallas_call(
        body, name=name, grid_spec=grid_spec,
        out_shape=[jax.ShapeDtypeStruct((r, c), F32)] * 4,
        compiler_params=_params(48),
    )(chip, own, land, land, land, w, m, v)


def _pack_sum(packs):
    def body(p_ref, o_ref):
        acc = p_ref[0]
        for k in range(1, N_DEV):
            acc += p_ref[k]
        o_ref[...] = acc

    return pl.pallas_call(
        body, name="pack_sum", grid=(1,),
        out_shape=jax.ShapeDtypeStruct((8, D_MODEL), F32),
        in_specs=[_full((N_DEV, 8, D_MODEL))], out_specs=_full((8, D_MODEL)),
    )(packs)


def _small_adamw(ws, gs, ms, vs):
    k = len(ws)

    def body(*refs):
        w_refs, g_refs, m_refs, v_refs = (refs[j * k:(j + 1) * k] for j in range(4))
        outs = refs[4 * k:]
        for j in range(k):
            outs[j][...], outs[k + j][...], outs[2 * k + j][...] = _adamw_math(
                w_refs[j][...], g_refs[j][...], m_refs[j][...], v_refs[j][...])

    specs = [_full(w.shape) for w in ws]
    res = pl.pallas_call(
        body, name="small_adamw", grid=(1,),
        out_shape=[jax.ShapeDtypeStruct(w.shape, F32) for w in ws] * 3,
        in_specs=specs * 4, out_specs=specs * 3,
    )(*ws, *gs, *ms, *vs)
    return res[:k], res[k:2 * k], res[2 * k:]


def kernel(x, mem, g_pre, w_in, w_conv, attn_sink, g_mem, w_mem_kv, w_up_a, w_up_b, w_up_m, w_out, g_post, loss_target, m_g_pre, m_w_in, m_w_conv, m_attn_sink, m_g_mem, m_w_mem_kv, m_w_up_a, m_w_up_b, m_w_up_m, m_w_out, m_g_post, v_g_pre, v_w_in, v_w_conv, v_attn_sink, v_g_mem, v_w_mem_kv, v_w_up_a, v_w_up_b, v_w_up_m, v_w_out, v_g_post):
    s = x.shape[1]
    x2, mem2, tgt2 = x[0], mem[0], loss_target[0]
    me = 4 * lax.axis_index("x") + 2 * lax.axis_index("y") + lax.axis_index("c")

    w_up_loc = jnp.concatenate([w_up_a[0], w_up_b[0], w_up_m[0]], axis=0).astype(BF16)
    w_conv_loc = jnp.zeros((8, 128), F32).at[:3, :64].set(w_conv[0])
    w_int_g, w_mkv_g, w_out_g, w_up_g, w_conv_g = _all_gather(
        [w_in[0].T.astype(BF16), w_mem_kv[0].astype(BF16), w_out[0].astype(BF16), w_up_loc, w_conv_loc],
        "gather_weights")
    w_int = w_int_g.reshape(IN_WIDTH, D_MODEL)
    w_mkv = w_mkv_g.reshape(D_MODEL, D_MODEL)
    w_out_f = w_out_g.reshape(D_MODEL, D_MODEL)
    w_conv_f = w_conv_g[:, :3, :64].transpose(1, 0, 2).reshape(3, 512)
    sink = attn_sink[0]
    tabs = _rope_tables(s)

    h, pa, pq, pkv, pbz, pmq, pmz, pg = _proj_fwd(x2, g_pre, w_int, tabs)
    mn, mkv = _mem_kv_fwd(mem2, g_mem, w_mkv)
    ya = _conv_fwd(pa, w_conv_f)
    yb = _attn_fwd(pq, pkv, pbz, sink)
    ym = _mem_attn_fwd(pmq, pmz, mkv)
    dg, dya, dyb, dym, dy, loss_p, gg_post, gw_out, gw_up = _mid(ya, yb, ym, pg, x2, tgt2, g_post, w_up_g, w_out_f)

    da, gw_conv = _conv_bwd(pa, dya, w_conv_f)
    dq, dbz, dkv, g_sink = _attn_bwd(pq, pkv, pbz, dyb, sink, tabs)
    dmq, dmz, dmkv = _mem_attn_bwd(pmq, pmz, mkv, dym)
    gw_mkv, gg_mem = _mem_kv_bwd(mem2, g_mem, mn, dmkv, w_mkv)
    dparts = (da, dq, dkv, dbz, dmq, dmz, dg)
    gw_int = _gw_in(dparts, h)

    shares = [gw_int.reshape(N_DEV, SHARD_IN, D_MODEL), gw_mkv.reshape(N_DEV, 128, D_MODEL),
              gw_out.reshape(N_DEV, 128, D_MODEL), gw_up]
    core = lax.axis_index("c").astype(jnp.int32).reshape(1)
    chip = (2 * lax.axis_index("x") + lax.axis_index("y")).astype(jnp.int32).reshape(1)
    from_sibling = _sibling_exchange(shares, "grads_to_sibling")
    chip_shares = _pair_add(shares, from_sibling, core, "grads_pair_add")
    send_sems, recv_sems, srcs, lands, token = _chip_exchange_start(chip_shares, "grads_to_chips_start")
    grad_x, gg_pre = _dh_bwd(dparts, x2, dy, g_pre + token[0:1, 0:1], w_int)
    (o_int, o_mkv, o_out, o_up), (l_int, l_mkv, l_out, l_up) = _chip_exchange_wait(
        send_sems, recv_sems, srcs, lands, grad_x, "grads_to_chips_wait")
    row3 = jnp.concatenate([gw_conv[0:1], gw_conv[1:2]], axis=1)
    row4 = jnp.concatenate([gw_conv[2:3], g_sink[:, 0].reshape(1, 8), loss_p[0:1, 0:1],
                            jnp.zeros((1, 512 - 9), F32)], axis=1)
    pack = jnp.concatenate([gg_pre, gg_mem, gg_post, row3, row4, jnp.zeros((3, D_MODEL), F32)], axis=0)
    (packs,) = _all_gather([pack], "gather_small")
    tot = _pack_sum(packs)

    g_w_in, d_w_in, nm_w_in, nv_w_in = (t.T for t in _sum_adamw(
        o_int, l_int, chip, 0, w_in[0].T, m_w_in[0].T, v_w_in[0].T, "adamw_w_in", tiles=2))
    g_mkv, d_mkv, nm_mkv, nv_mkv = _sum_adamw(
        o_mkv, l_mkv, chip, 0, w_mem_kv[0], m_w_mem_kv[0], v_w_mem_kv[0], "adamw_w_mem_kv")
    g_out, d_out, nm_out, nv_out = _sum_adamw(o_out, l_out, chip, 0, w_out[0], m_w_out[0], v_w_out[0], "adamw_w_out")
    up = [_sum_adamw(o_up, l_up, chip, k, w[0], m[0], v[0], "adamw_w_up_" + "abm"[k])
          for k, (w, m, v) in enumerate([(w_up_a, m_w_up_a, v_w_up_a), (w_up_b, m_w_up_b, v_w_up_b),
                                         (w_up_m, m_w_up_m, v_w_up_m)])]

    g_g_pre, g_g_mem, g_g_post = tot[0:1], tot[1:2], tot[2:3]
    g_conv_full = jnp.concatenate([tot[3:4, 0:512], tot[3:4, 512:1024], tot[4:5, 0:512]], axis=0)
    g_conv = lax.dynamic_slice(g_conv_full, (0, 64 * me), (3, 64))
    g_sink_tot = tot[4:5, 512:520]
    loss = tot[4, 520]
    small_w = [g_pre, w_conv[0], attn_sink, g_mem, g_post]
    small_g = [g_g_pre, g_conv, g_sink_tot, g_g_mem, g_g_post]
    small_m = [m_g_pre, m_w_conv[0], m_attn_sink, m_g_mem, m_g_post]
    small_v = [v_g_pre, v_w_conv[0], v_attn_sink, v_g_mem, v_g_post]
    sd, sm, sv = _small_adamw(small_w, small_g, small_m, small_v)

    def lead(a):
        return a[None]

    grads = [g_g_pre, lead(g_w_in), lead(g_conv), g_sink_tot, g_g_mem, lead(g_mkv), lead(up[0][0]),
             lead(up[1][0]), lead(up[2][0]), lead(g_out), g_g_post]

    def assemble(small, big_in, big_mkv, big_up, big_out):
        return [small[0], lead(big_in), lead(small[1]), small[2], small[3], lead(big_mkv), lead(big_up[0]),
                lead(big_up[1]), lead(big_up[2]), lead(big_out), small[4]]

    deltas = assemble(sd, d_w_in, d_mkv, [u[1] for u in up], d_out)
    new_m = assemble(sm, nm_w_in, nm_mkv, [u[2] for u in up], nm_out)
    new_v = assemble(sv, nv_w_in, nv_mkv, [u[3] for u in up], nv_out)
    return (loss, grad_x[None], *grads, *deltas, *new_m, *new_v)
```

```python
import functools

import jax
import jax.numpy as jnp
from jax import lax
from jax.experimental import pallas as pl
from jax.experimental.pallas import tpu as pltpu

F32 = jnp.float32
BF16 = jnp.bfloat16
MESH = pl.DeviceIdType.MESH

N_DEV = 8
D_MODEL = 1024
EPS = 1e-6
ROPE_THETA = 500000.0
ROT_DIM = 16
HEAD_DIM = 64
ATTN_BLOCK = 128
MEM_HEADS = 4
MEM_HEAD_DIM = 128
ATTN_SCALE = HEAD_DIM ** -0.5
MEM_SCALE = MEM_HEAD_DIM ** -0.5

ADAM_LR = 0.001
ADAM_B1 = 0.9
ADAM_B2 = 0.999
ADAM_EPS = 1e-08
ADAM_WD = 0.01
ADAM_STEP = 10

SEG_A = (0, 2048)
SEG_BQ = (2048, 512)
SEG_BKV = (2560, 256)
SEG_BZ = (2816, 512)
SEG_MQ = (3328, 512)
SEG_MZ = (3840, 512)
SEG_G = (4352, 3072)
SEGS = (SEG_A, SEG_BQ, SEG_BKV, SEG_BZ, SEG_MQ, SEG_MZ, SEG_G)
IN_WIDTH = 7424
SHARD_IN = IN_WIDTH // N_DEV

V7X_VMEM_BYTES = 64 * 1024 * 1024
ANY = pl.BlockSpec(memory_space=pl.ANY)


def _params(vmem_mb):
    assert vmem_mb * 1024 * 1024 < V7X_VMEM_BYTES
    return pltpu.CompilerParams(dimension_semantics=("arbitrary",), vmem_limit_bytes=vmem_mb * 1024 * 1024)


def _full(shape):
    zeros = (0,) * len(shape)
    return pl.BlockSpec(shape, lambda i: zeros)


def _rows(tm, width):
    return pl.BlockSpec((tm, width), lambda i: (i, 0))


def _dot(a, b):
    return jnp.dot(a, b, preferred_element_type=F32)


def _dot_nt(a, b):
    return lax.dot_general(a, b, (((1,), (1,)), ((), ())), preferred_element_type=F32)


def _dot_tn(a, b):
    return lax.dot_general(a, b, (((0,), (0,)), ((), ())), preferred_element_type=F32)


def _sigmoid(z):
    return 1.0 / (1.0 + jnp.exp(-z))


def _rope(t, cs, s1, s2):
    return t * cs + pltpu.roll(t, 120, 1) * s1 + pltpu.roll(t, 8, 1) * s2


def _rope_t(d, cs, s1, s2):
    return d * cs + pltpu.roll(d * s1, 8, 1) + pltpu.roll(d * s2, 120, 1)


def _rope_tables(s):
    half = ROT_DIM // 2
    inv_freq = jnp.power(jnp.float32(ROPE_THETA), -jnp.arange(half, dtype=F32) * (2.0 / ROT_DIM))
    d = jnp.arange(128) % HEAD_DIM
    ang = jnp.arange(s).astype(F32)[:, None] * inv_freq[d % half][None, :]
    cos, sin = jnp.cos(ang), jnp.sin(ang)
    lo, hi = (d < half)[None, :], ((d >= half) & (d < ROT_DIM))[None, :]
    return jnp.where(lo | hi, cos, 1.0), jnp.where(lo, -sin, 0.0), jnp.where(hi, sin, 0.0)


def _load_once(pairs, sems):
    @pl.when(pl.program_id(0) == 0)
    def _():
        cps = [pltpu.make_async_copy(src, dst, sems.at[k]) for k, (src, dst) in enumerate(pairs)]
        for cp in cps:
            cp.start()
        for cp in cps:
            cp.wait()


def _my_place():
    x, y, c = lax.axis_index("x"), lax.axis_index("y"), lax.axis_index("c")
    return x, y, c


def _all_gather(arrs, name):
    n = len(arrs)

    def body(*refs):
        ins, outs = refs[:n], refs[n:2 * n]
        send_sems, recv_sems, local_sems = refs[2 * n:]
        x, y, c = _my_place()
        me, sibling = (x, y, c), (x, y, 1 - c)
        chips = [(1 - x, y), (x, 1 - y), (1 - x, 1 - y)]

        def idx(px, py, pc):
            return 4 * px + 2 * py + pc

        def copy(a, k, block, to, src=None):
            dst = outs[a].at[idx(*block)]
            return pltpu.make_async_remote_copy(
                src_ref=dst if src is None else src, dst_ref=dst,
                send_sem=send_sems.at[a * 7 + k], recv_sem=recv_sems.at[a * 7 + k],
                device_id=to, device_id_type=MESH)

        mine = [pltpu.make_async_copy(ins[a], outs[a].at[idx(*me)], local_sems.at[a]) for a in range(n)]
        for cp in mine:
            cp.start()
        first = []
        for a in range(n):
            first.append(copy(a, 0, me, sibling, src=ins[a]))
        for j, chip in enumerate(chips):
            for a in range(n):
                first.append(copy(a, 1 + j, me, (*chip, c), src=ins[a]))
        for cp in first:
            cp.start()
        passed = []
        for j, chip in enumerate(chips):
            for a in range(n):
                copy(a, 1 + j, (*chip, c), me).wait_recv()
                cp = copy(a, 4 + j, (*chip, c), sibling)
                cp.start()
                passed.append(cp)
        for a in range(n):
            copy(a, 0, sibling, me).wait_recv()
        for j, chip in enumerate(chips):
            for a in range(n):
                copy(a, 4 + j, (*chip, 1 - c), me).wait_recv()
        for cp in first + passed:
            cp.wait_send()
        for cp in mine:
            cp.wait()

    return pl.pallas_call(
        body, name=name,
        out_shape=[jax.ShapeDtypeStruct((N_DEV,) + a.shape, a.dtype) for a in arrs],
        in_specs=[ANY] * n, out_specs=[ANY] * n,
        scratch_shapes=[pltpu.SemaphoreType.DMA((7 * n,)), pltpu.SemaphoreType.DMA((7 * n,)),
                        pltpu.SemaphoreType.DMA((n,))],
    )(*arrs)


N_CHIPS = 4


def _sibling_exchange(arrs, name):
    n = len(arrs)

    def body(*refs):
        ins, outs = refs[:n], refs[n:2 * n]
        send_sems, recv_sems = refs[2 * n:]
        x, y, c = _my_place()
        sibling = (x, y, 1 - c)

        def copy(a, j):
            return pltpu.make_async_remote_copy(
                src_ref=ins[a].at[2 * j + (1 - c)], dst_ref=outs[a].at[j],
                send_sem=send_sems.at[a * N_CHIPS + j], recv_sem=recv_sems.at[a * N_CHIPS + j],
                device_id=sibling, device_id_type=MESH)

        cps = [copy(a, j) for j in range(N_CHIPS) for a in range(n)]
        for cp in cps:
            cp.start()
        for cp in cps:
            cp.wait_recv()
        for cp in cps:
            cp.wait_send()

    return pl.pallas_call(
        body, name=name,
        out_shape=[jax.ShapeDtypeStruct((N_CHIPS,) + a.shape[1:], a.dtype) for a in arrs],
        in_specs=[ANY] * n, out_specs=[ANY] * n,
        scratch_shapes=[pltpu.SemaphoreType.DMA((N_CHIPS * n,)), pltpu.SemaphoreType.DMA((N_CHIPS * n,))],
    )(*arrs)


def _pair_add(mine, recv, core, name):
    n = len(mine)

    def body(c_ref, *refs):
        for a in range(n):
            refs[2 * n + a][...] = (refs[a][...].astype(F32) + refs[n + a][...].astype(F32)).astype(BF16)

    def blk(a):
        return (None,) + a.shape[1:]

    grid_spec = pltpu.PrefetchScalarGridSpec(
        num_scalar_prefetch=1, grid=(N_CHIPS,),
        in_specs=[pl.BlockSpec(blk(a), lambda j, c_ref: (2 * j + c_ref[0], 0, 0)) for a in mine]
        + [pl.BlockSpec(blk(a), lambda j, c_ref: (j, 0, 0)) for a in recv],
        out_specs=[pl.BlockSpec(blk(a), lambda j, c_ref: (j, 0, 0)) for a in recv])
    return pl.pallas_call(
        body, name=name, grid_spec=grid_spec,
        out_shape=[jax.ShapeDtypeStruct(a.shape, BF16) for a in recv],
        compiler_params=_params(32),
    )(core, *mine, *recv)


HBM = pl.BlockSpec(memory_space=pltpu.HBM)
SEM = pl.BlockSpec(memory_space=pltpu.SEMAPHORE)
N_PEER_CHIPS = 3


def _chip_copies(srcs, lands, send_sems, recv_sems):
    x, y, c = _my_place()
    my_chip = 2 * x + y
    peers = [(x, 1 - y), (1 - x, y), (1 - x, 1 - y)]
    cps = []
    for k, (px, py) in enumerate(peers):
        for a in range(len(srcs)):
            j = a * N_PEER_CHIPS + k
            cps.append(pltpu.make_async_remote_copy(
                src_ref=srcs[a].at[2 * px + py], dst_ref=lands[a].at[my_chip],
                send_sem=send_sems[j], recv_sem=recv_sems[j],
                device_id=(px, py, c), device_id_type=MESH))
    return cps


def _chip_exchange_start(arrs, name):
    n = len(arrs)
    k = n * N_PEER_CHIPS

    def body(*refs):
        srcs, lands = refs[:n], refs[n:2 * n]
        send_sems, recv_sems = refs[2 * n:2 * n + k], refs[2 * n + k:2 * n + 2 * k]
        token = refs[-1]
        for cp in _chip_copies(srcs, lands, send_sems, recv_sems):
            cp.start()
        token[...] = jnp.zeros_like(token)

    hbm_arrs = [pltpu.with_memory_space_constraint(a, pltpu.HBM) for a in arrs]
    lands = [pltpu.with_memory_space_constraint(lax.empty(a.shape, a.dtype), pltpu.HBM) for a in arrs]
    res = pl.pallas_call(
        body, name=name,
        out_shape=[pltpu.SemaphoreType.DMA(())] * (2 * k) + [pltpu.HBM(a.shape, a.dtype) for a in arrs] * 2
        + [jax.ShapeDtypeStruct((8, 128), F32)],
        in_specs=[HBM] * (2 * n),
        out_specs=[SEM] * (2 * k) + [HBM] * (2 * n) + [pl.BlockSpec(memory_space=pltpu.VMEM)],
        input_output_aliases={a: 2 * k + a for a in range(2 * n)},
        compiler_params=pltpu.CompilerParams(has_side_effects=pltpu.SideEffectType.DATAFLOW_SIDE_EFFECTING),
    )(*hbm_arrs, *lands)
    return res[:k], res[k:2 * k], res[2 * k:2 * k + n], res[2 * k + n:2 * k + 2 * n], res[-1]


def _chip_exchange_wait(send_sems, recv_sems, srcs, lands, after, name):
    n = len(srcs)
    k = n * N_PEER_CHIPS

    def body(*refs):
        src_refs, land_refs = refs[:n], refs[n:2 * n]
        s_sems, r_sems = refs[2 * n:2 * n + k], refs[2 * n + k:2 * n + 2 * k]
        for cp in _chip_copies(src_refs, land_refs, s_sems, r_sems):
            cp.wait_send()
            cp.wait_recv()

    res = pl.pallas_call(
        body, name=name,
        out_shape=[pltpu.HBM(a.shape, a.dtype) for a in srcs] * 2,
        in_specs=[HBM] * (2 * n) + [SEM] * (2 * k) + [ANY],
        out_specs=[HBM] * (2 * n),
        input_output_aliases={a: a for a in range(2 * n)},
        compiler_params=pltpu.CompilerParams(has_side_effects=pltpu.SideEffectType.DATAFLOW_SIDE_EFFECTING),
    )(*srcs, *lands, *send_sems, *recv_sems, after)
    return res[:n], res[n:]


def _proj_fwd(x, g_pre, w_int, tabs):
    s = x.shape[0]
    tm = min(512, s)

    def body(x_ref, g_ref, cs_ref, s1_ref, s2_ref, w_hbm,
             h_ref, pa_ref, pq_ref, pkv_ref, pbz_ref, pmq_ref, pmz_ref, pg_ref, w_vm, sems):
        _load_once([(w_hbm, w_vm)], sems)
        xf = x_ref[...]
        r = lax.rsqrt(jnp.mean(xf * xf, axis=-1, keepdims=True) + EPS)
        h = ((xf * r) * g_ref[...]).astype(BF16)
        h_ref[...] = h
        cs, s1, s2 = cs_ref[...], s1_ref[...], s2_ref[...]

        def mm(seg, c0, width):
            return _dot_nt(h, w_vm[seg[0] + c0:seg[0] + c0 + width, :])

        for c0 in range(0, SEG_A[1], 512):
            pa_ref[:, c0:c0 + 512] = mm(SEG_A, c0, 512).astype(BF16)
        q = mm(SEG_BQ, 0, 512)
        for b in range(4):
            pq_ref[:, 128 * b:128 * b + 128] = _rope(q[:, 128 * b:128 * b + 128], cs, s1, s2).astype(BF16)
        kv = mm(SEG_BKV, 0, 256)
        pkv_ref[:, 0:128] = _rope(kv[:, 0:128], cs, s1, s2).astype(BF16)
        pkv_ref[:, 128:256] = kv[:, 128:256].astype(BF16)
        pbz_ref[...] = mm(SEG_BZ, 0, 512).astype(BF16)
        pmq_ref[...] = mm(SEG_MQ, 0, 512).astype(BF16)
        pmz_ref[...] = mm(SEG_MZ, 0, 512).astype(BF16)
        for c0 in range(0, SEG_G[1], 512):
            pg_ref[:, c0:c0 + 512] = mm(SEG_G, c0, 512).astype(BF16)

    widths = (D_MODEL, 2048, 512, 256, 512, 512, 512, 3072)
    return pl.pallas_call(
        body, name="proj_fwd", grid=(s // tm,),
        out_shape=[jax.ShapeDtypeStruct((s, w), BF16) for w in widths],
        in_specs=[_rows(tm, D_MODEL), _full((1, D_MODEL)), _rows(tm, 128), _rows(tm, 128), _rows(tm, 128), ANY],
        out_specs=[_rows(tm, w) for w in widths],
        scratch_shapes=[pltpu.VMEM((IN_WIDTH, D_MODEL), BF16), pltpu.SemaphoreType.DMA((1,))],
        compiler_params=_params(52),
    )(x, g_pre, *tabs, w_int)


def _mem_kv_fwd(mem, g_mem, w_mkv):
    m = mem.shape[0]

    def body(mem_ref, g_ref, w_ref, mn_ref, mkv_ref):
        xf = mem_ref[...]
        r = lax.rsqrt(jnp.mean(xf * xf, axis=-1, keepdims=True) + EPS)
        mn = ((xf * r) * g_ref[...]).astype(BF16)
        mn_ref[...] = mn
        mkv_ref[...] = _dot(mn, w_ref[...]).astype(BF16)

    return pl.pallas_call(
        body, name="mem_kv_fwd", grid=(1,),
        out_shape=[jax.ShapeDtypeStruct((m, D_MODEL), BF16)] * 2,
        in_specs=[_full((m, D_MODEL)), _full((1, D_MODEL)), _full((D_MODEL, D_MODEL))],
        out_specs=[_full((m, D_MODEL))] * 2,
        compiler_params=_params(32),
    )(mem, g_mem, w_mkv)


def _halo_specs(s, tm, rows, width):
    nblk = s // rows
    prev = pl.BlockSpec((rows, width), lambda i: (jnp.maximum(i * (tm // rows) - 1, 0), 0))
    nxt = pl.BlockSpec((rows, width), lambda i: (jnp.minimum((i + 1) * (tm // rows), nblk - 1), 0))
    return prev, nxt


def _conv_common(pa, prev_row, next_row, w, first, last, tm):
    b, c, u, z = (pa[:, 512 * k:512 * k + 512] for k in range(4))
    cu = c * u
    cu_prev = jnp.where(first, 0.0, prev_row[:, 512:1024] * prev_row[:, 1024:1536])
    cu_next = jnp.where(last, 0.0, next_row[:, 512:1024] * next_row[:, 1024:1536])
    row = lax.broadcasted_iota(jnp.int32, (tm, 512), 0)
    cu_m1 = jnp.where(row == 0, cu_prev, pltpu.roll(cu, 1, 0))
    cu_p1 = jnp.where(row == tm - 1, cu_next, pltpu.roll(cu, tm - 1, 0))
    y = cu_m1 * w[0:1] + cu * w[1:2] + cu_p1 * w[2:3]
    sig = _sigmoid(z)
    return b, c, u, z, cu, cu_m1, cu_p1, y, sig, row


def _conv_fwd(pa, w_conv):
    s = pa.shape[0]
    tm = min(512, s)
    nt = s // tm

    def body(pa_ref, pp_ref, pn_ref, w_ref, ya_ref):
        i = pl.program_id(0)
        prev_row = pp_ref[...].astype(F32)[15:16, :]
        next_row = pn_ref[...].astype(F32)[0:1, :]
        b, _, _, z, _, _, _, y, sig, _ = _conv_common(
            pa_ref[...].astype(F32), prev_row, next_row, w_ref[...], i == 0, i == nt - 1, tm)
        ya_ref[...] = (b * y * (z * sig)).astype(BF16)

    prev, nxt = _halo_specs(s, tm, 16, 2048)
    return pl.pallas_call(
        body, name="conv_fwd", grid=(nt,),
        out_shape=jax.ShapeDtypeStruct((s, 512), BF16),
        in_specs=[_rows(tm, 2048), prev, nxt, _full((3, 512))],
        out_specs=_rows(tm, 512),
        compiler_params=_params(48),
    )(pa, pa, pa, w_conv)


def _stack_heads(a, g, lane):
    in_g = (lane >= 64 * g) & (lane < 64 * g + 64)
    parts = []
    for j in range(4):
        h = 4 * g + j
        blk = a[:, 128 * (h // 2):128 * (h // 2) + 128]
        if h % 2 != g:
            blk = pltpu.roll(blk, 64, 1)
        parts.append(jnp.where(in_g, blk, 0.0))
    return jnp.concatenate(parts, axis=0)


def _unstack_heads(o0, o1, lane):
    blocks = []
    for b in range(4):
        g = b // 2
        og = (o0, o1)[g]
        je, jo = (2 * b) % 4, (2 * b + 1) % 4
        even, odd = og[128 * je:128 * je + 128], og[128 * jo:128 * jo + 128]
        if g == 0:
            odd = pltpu.roll(odd, 64, 1)
        else:
            even = pltpu.roll(even, 64, 1)
        blocks.append(jnp.where(lane < 64, even, odd))
    return jnp.concatenate(blocks, axis=1)


WINDOW_KEYS = 3 * ATTN_BLOCK
STACKED = 4 * ATTN_BLOCK
KEY_CHUNK = 32


def _fill_band_bias(bias, nb):
    assert nb >= 2
    c = lax.broadcasted_iota(jnp.int32, (WINDOW_KEYS, STACKED), 0)
    r = lax.broadcasted_iota(jnp.int32, (WINDOW_KEYS, STACKED), 1) & (ATTN_BLOCK - 1)
    band = (c >= r) & (c <= r + 2 * ATTN_BLOCK)
    for v, ok in enumerate((band, band & (c >= ATTN_BLOCK), band & (c < 2 * ATTN_BLOCK))):
        bias[v] = jnp.where(ok, 0.0, -jnp.inf)


def _bias_variant(n, nb):
    return jnp.where(n == 0, 1, jnp.where(n == nb - 1, 2, 0))


def _sink_row(sink_ref, g):
    return jnp.concatenate([jnp.full((1, ATTN_BLOCK), sink_ref[4 * g + j], F32) for j in range(4)], axis=1)


def _softmax_keys_major(sc, bias, variant, sink, e_scr):
    chunks = [pl.ds(k * KEY_CHUNK, KEY_CHUNK) for k in range(WINDOW_KEYS // KEY_CHUNK)]
    rows = [slice(k * KEY_CHUNK, (k + 1) * KEY_CHUNK) for k in range(WINDOW_KEYS // KEY_CHUNK)]
    m_run = jnp.full((KEY_CHUNK, STACKED), -jnp.inf, F32)
    for ck, rw in zip(chunks, rows):
        m_run = jnp.maximum(m_run, sc[rw] * ATTN_SCALE + bias[variant, ck, :])
    m = jnp.maximum(jnp.max(m_run, axis=0, keepdims=True), sink)
    l_run = jnp.zeros((KEY_CHUNK, STACKED), F32)
    for ck, rw in zip(chunks, rows):
        e = jnp.exp(sc[rw] * ATTN_SCALE + bias[variant, ck, :] - m)
        l_run += e
        e_scr[rw, :] = e.astype(BF16)
    es = jnp.exp(sink - m)
    inv = 1.0 / (jnp.sum(l_run, axis=0, keepdims=True) + es)
    return inv, es * inv


def _to_rows(v):
    return jnp.transpose(jnp.broadcast_to(v, (ATTN_BLOCK, STACKED)))


def _fill_padded(kv_ref, kpad, vpad, s):
    zero = jnp.zeros((ATTN_BLOCK, 128), BF16)
    kpad[0:ATTN_BLOCK, :] = zero
    vpad[0:ATTN_BLOCK, :] = zero
    kpad[ATTN_BLOCK + s:2 * ATTN_BLOCK + s, :] = zero
    vpad[ATTN_BLOCK + s:2 * ATTN_BLOCK + s, :] = zero
    kpad[ATTN_BLOCK:ATTN_BLOCK + s, :] = kv_ref[:, 0:128]
    vpad[ATTN_BLOCK:ATTN_BLOCK + s, :] = kv_ref[:, 128:256]


def _attn_fwd(pq, pkv, pbz, sink):
    s = pq.shape[0]
    nb = s // ATTN_BLOCK

    def body(sink_ref, q_ref, z_ref, kv_ref, yb_ref, kpad, vpad, bias, e_scr):
        n = pl.program_id(0)

        @pl.when(n == 0)
        def _():
            _fill_padded(kv_ref, kpad, vpad, s)
            _fill_band_bias(bias, nb)

        lane = lax.broadcasted_iota(jnp.int32, (ATTN_BLOCK, 128), 1)
        start = pl.multiple_of(n * ATTN_BLOCK, ATTN_BLOCK)
        kw, vw = kpad[pl.ds(start, WINDOW_KEYS), :], vpad[pl.ds(start, WINDOW_KEYS), :]
        qf = q_ref[...].astype(F32)
        variant = _bias_variant(n, nb)
        outs = []
        for g in range(2):
            qg = _stack_heads(qf, g, lane).astype(BF16)
            inv, _ = _softmax_keys_major(_dot_nt(kw, qg), bias, variant, _sink_row(sink_ref, g), e_scr)
            outs.append(_dot_tn(e_scr[...], vw) * _to_rows(inv))
        attn = _unstack_heads(outs[0], outs[1], lane)
        z = z_ref[...].astype(F32)
        yb_ref[...] = (attn * (z * _sigmoid(z))).astype(BF16)

    return pl.pallas_call(
        body, name="attn_fwd", grid=(nb,),
        out_shape=jax.ShapeDtypeStruct((s, 512), BF16),
        in_specs=[pl.BlockSpec(memory_space=pltpu.SMEM), _rows(ATTN_BLOCK, 512), _rows(ATTN_BLOCK, 512),
                  _full((s, 256))],
        out_specs=_rows(ATTN_BLOCK, 512),
        scratch_shapes=[pltpu.VMEM((s + 2 * ATTN_BLOCK, 128), BF16)] * 2
        + [pltpu.VMEM((3, WINDOW_KEYS, STACKED), F32), pltpu.VMEM((WINDOW_KEYS, STACKED), BF16)],
        compiler_params=_params(32),
    )(sink, pq, pbz, pkv)


def _mem_softmax_t(q, mk):
    sc = _dot_nt(mk, q) * MEM_SCALE
    e = jnp.exp(sc - jnp.max(sc, axis=0, keepdims=True))
    return e * (1.0 / jnp.sum(e, axis=0, keepdims=True))


def _mem_attn_fwd(pmq, pmz, mkv):
    s = pmq.shape[0]
    m = mkv.shape[0]
    tm = min(512, s)

    def body(q_ref, z_ref, mk_ref, mv_ref, ym_ref):
        z = z_ref[...].astype(F32)
        sz = z * _sigmoid(z)
        for h in range(MEM_HEADS):
            cols = slice(128 * h, 128 * h + 128)
            pt = _mem_softmax_t(q_ref[:, cols], mk_ref[:, cols])
            o = _dot_tn(pt.astype(BF16), mv_ref[:, cols])
            ym_ref[:, cols] = (o * sz[:, cols]).astype(BF16)

    return pl.pallas_call(
        body, name="mem_attn_fwd", grid=(s // tm,),
        out_shape=jax.ShapeDtypeStruct((s, 512), BF16),
        in_specs=[_rows(tm, 512), _rows(tm, 512), pl.BlockSpec((m, 512), lambda i: (0, 0)),
                  pl.BlockSpec((m, 512), lambda i: (0, 1))],
        out_specs=_rows(tm, 512),
        compiler_params=_params(32),
    )(pmq, pmz, mkv, mkv)


def _mid(ya, yb, ym, pg, x, target, g_post, w_up, w_out):
    s = x.shape[0]
    tm = min(256, s)
    nt = s // tm

    def body(ya_ref, yb_ref, ym_ref, pg_ref, x_ref, t_ref, gp_ref, wup_hbm, wout_hbm,
             dg_ref, dya_ref, dyb_ref, dym_ref, dy_ref, loss_ref, ggp_ref, gwout_hbm, gwup_hbm,
             wup_vm, wout_vm, acc_out, acc_up, st_out, st_up, sems):
        i = pl.program_id(0)
        _load_once([(wup_hbm.at[d], wup_vm.at[:, pl.ds(128 * d, 128)]) for d in range(N_DEV)]
                   + [(wout_hbm, wout_vm)], sems)

        @pl.when(i == 0)
        def _():
            acc_out[...] = jnp.zeros_like(acc_out)
            acc_up[...] = jnp.zeros_like(acc_up)
            loss_ref[...] = jnp.zeros_like(loss_ref)
            ggp_ref[...] = jnp.zeros_like(ggp_ref)

        ys = (ya_ref[...], yb_ref[...], ym_ref[...])
        us = [_dot(ys[k], wup_vm[512 * k:512 * k + 512, :]) for k in range(3)]
        gates = [_sigmoid(pg_ref[:, 1024 * k:1024 * k + 1024].astype(F32)) for k in range(3)]
        merged = gates[0] * us[0] + gates[1] * us[1] + gates[2] * us[2]
        mb = merged.astype(BF16)
        out = _dot(mb, wout_vm[...])
        r = lax.rsqrt(jnp.mean(out * out, axis=-1, keepdims=True) + EPS)
        on = out * r
        gp = gp_ref[...]
        err = (x_ref[...] + on * gp) - t_ref[...]
        loss_ref[...] += 0.5 * jnp.sum(err * err) * (1.0 / D_MODEL)
        dy = err * (1.0 / D_MODEL)
        dy_ref[...] = dy
        ggp_ref[...] += jnp.sum(dy * on, axis=0, keepdims=True)
        a = dy * gp
        d_out = r * (a - on * jnp.mean(a * on, axis=-1, keepdims=True))
        dob = d_out.astype(BF16)
        acc_out[...] += _dot_tn(mb, dob)
        d_merged = _dot_nt(dob, wout_vm[...])
        d_refs = (dya_ref, dyb_ref, dym_ref)
        for k in range(3):
            g = gates[k]
            dg_ref[:, 1024 * k:1024 * k + 1024] = (d_merged * us[k] * g * (1.0 - g)).astype(BF16)
            du = (d_merged * g).astype(BF16)
            d_refs[k][...] = _dot_nt(du, wup_vm[512 * k:512 * k + 512, :])
            acc_up[512 * k:512 * k + 512, :] += _dot_tn(ys[k], du)

        @pl.when(i == nt - 1)
        def _():
            st_out[...] = acc_out[...].astype(BF16)
            for d in range(N_DEV):
                st_up[d] = acc_up[:, 128 * d:128 * d + 128].astype(BF16)
            cps = [pltpu.make_async_copy(st_out, gwout_hbm, sems.at[0]),
                   pltpu.make_async_copy(st_up, gwup_hbm, sems.at[1])]
            for cp in cps:
                cp.start()
            for cp in cps:
                cp.wait()

    return pl.pallas_call(
        body, name="mid", grid=(nt,),
        out_shape=[jax.ShapeDtypeStruct((s, 3072), BF16)] + [jax.ShapeDtypeStruct((s, 512), F32)] * 3
        + [jax.ShapeDtypeStruct((s, D_MODEL), F32), jax.ShapeDtypeStruct((8, 128), F32),
           jax.ShapeDtypeStruct((1, D_MODEL), F32), jax.ShapeDtypeStruct((D_MODEL, D_MODEL), BF16),
           jax.ShapeDtypeStruct((N_DEV, 1536, 128), BF16)],
        in_specs=[_rows(tm, 512)] * 3 + [_rows(tm, 3072), _rows(tm, D_MODEL), _rows(tm, D_MODEL),
                                         _full((1, D_MODEL)), ANY, ANY],
        out_specs=[_rows(tm, 3072)] + [_rows(tm, 512)] * 3 + [_rows(tm, D_MODEL), _full((8, 128)),
                                                               _full((1, D_MODEL)), ANY, ANY],
        scratch_shapes=[pltpu.VMEM((1536, D_MODEL), BF16), pltpu.VMEM((D_MODEL, D_MODEL), BF16),
                        pltpu.VMEM((D_MODEL, D_MODEL), F32), pltpu.VMEM((1536, D_MODEL), F32),
                        pltpu.VMEM((D_MODEL, D_MODEL), BF16), pltpu.VMEM((N_DEV, 1536, 128), BF16),
                        pltpu.SemaphoreType.DMA((N_DEV + 1,))],
        compiler_params=_params(60),
    )(ya, yb, ym, pg, x, target, g_post, w_up, w_out)


def _conv_bwd(pa, dya, w_conv):
    s = pa.shape[0]
    tm = min(512, s)
    nt = s // tm

    def body(pa_ref, pp_ref, pn_ref, d_ref, dp_ref, dn_ref, w_ref, da_ref, gw_ref):
        i = pl.program_id(0)
        first, last = i == 0, i == nt - 1

        @pl.when(first)
        def _():
            gw_ref[...] = jnp.zeros_like(gw_ref)

        w = w_ref[...]
        prev_row = pp_ref[...].astype(F32)[15:16, :]
        next_row = pn_ref[...].astype(F32)[0:1, :]
        b, c, u, z, cu, cu_m1, cu_p1, y, sig, row = _conv_common(
            pa_ref[...].astype(F32), prev_row, next_row, w, first, last, tm)
        sz = z * sig
        dya_t = d_ref[...]
        d_y = dya_t * b * sz

        def halo_dy(p_row, d_row):
            zz = p_row[:, 1536:2048]
            return d_row * p_row[:, 0:512] * (zz * _sigmoid(zz))

        dy_prev = jnp.where(first, 0.0, halo_dy(prev_row, dp_ref[7:8, :]))
        dy_next = jnp.where(last, 0.0, halo_dy(next_row, dn_ref[0:1, :]))
        dy_m1 = jnp.where(row == 0, dy_prev, pltpu.roll(d_y, 1, 0))
        dy_p1 = jnp.where(row == tm - 1, dy_next, pltpu.roll(d_y, tm - 1, 0))
        d_cu = dy_p1 * w[0:1] + d_y * w[1:2] + dy_m1 * w[2:3]
        da_ref[:, 0:512] = (dya_t * y * sz).astype(BF16)
        da_ref[:, 512:1024] = (d_cu * u).astype(BF16)
        da_ref[:, 1024:1536] = (d_cu * c).astype(BF16)
        da_ref[:, 1536:2048] = (dya_t * b * y * (sig * (1.0 + z * (1.0 - sig)))).astype(BF16)
        gw_ref[0:1, :] += jnp.sum(d_y * cu_m1, axis=0, keepdims=True)
        gw_ref[1:2, :] += jnp.sum(d_y * cu, axis=0, keepdims=True)
        gw_ref[2:3, :] += jnp.sum(d_y * cu_p1, axis=0, keepdims=True)

    prev, nxt = _halo_specs(s, tm, 16, 2048)
    dprev, dnxt = _halo_specs(s, tm, 8, 512)
    return pl.pallas_call(
        body, name="conv_bwd", grid=(nt,),
        out_shape=[jax.ShapeDtypeStruct((s, 2048), BF16), jax.ShapeDtypeStruct((8, 512), F32)],
        in_specs=[_rows(tm, 2048), prev, nxt, _rows(tm, 512), dprev, dnxt, _full((3, 512))],
        out_specs=[_rows(tm, 2048), _full((8, 512))],
        compiler_params=_params(48),
    )(pa, pa, pa, dya, dya, dya, w_conv)


def _attn_bwd(pq, pkv, pbz, dyb, sink, tabs):
    s = pq.shape[0]
    nb = s // ATTN_BLOCK

    def body(sink_ref, q_ref, z_ref, d_ref, cs_ref, s1_ref, s2_ref, kv_ref, csf_ref, s1f_ref, s2f_ref,
             dq_ref, dz_ref, dkv_ref, gs_ref, kpad, vpad, dk_acc, dv_acc, bias, e_scr, ds_scr):
        n = pl.program_id(0)

        @pl.when(n == 0)
        def _():
            _fill_padded(kv_ref, kpad, vpad, s)
            _fill_band_bias(bias, nb)
            dk_acc[...] = jnp.zeros_like(dk_acc)
            dv_acc[...] = jnp.zeros_like(dv_acc)
            gs_ref[...] = jnp.zeros_like(gs_ref)

        lane = lax.broadcasted_iota(jnp.int32, (ATTN_BLOCK, 128), 1)
        start = pl.multiple_of(n * ATTN_BLOCK, ATTN_BLOCK)
        kw, vw = kpad[pl.ds(start, WINDOW_KEYS), :], vpad[pl.ds(start, WINDOW_KEYS), :]
        qf = q_ref[...].astype(F32)
        variant = _bias_variant(n, nb)
        z = z_ref[...].astype(F32)
        sig = _sigmoid(z)
        dyb_t = d_ref[...]
        d_attn = dyb_t * (z * sig)
        ones = jnp.ones((8, 128), BF16)
        outs, dqs = [], []
        dk_w = jnp.zeros((WINDOW_KEYS, 128), F32)
        dv_w = jnp.zeros((WINDOW_KEYS, 128), F32)
        for g in range(2):
            qgf = _stack_heads(qf, g, lane)
            qg = qgf.astype(BF16)
            inv, p_sink = _softmax_keys_major(_dot_nt(kw, qg), bias, variant, _sink_row(sink_ref, g), e_scr)
            inv_rows = _to_rows(inv)
            o = _dot_tn(e_scr[...], vw) * inv_rows
            outs.append(o)
            do = _stack_heads(d_attn, g, lane)
            prod = do * o
            hi = prod.astype(BF16)
            lo = (prod - hi.astype(F32)).astype(BF16)
            delta = (_dot_nt(ones, hi) + _dot_nt(ones, lo))[0:1, :]
            dpt = _dot_nt(vw, do.astype(BF16))
            for k in range(WINDOW_KEYS // KEY_CHUNK):
                rw = slice(k * KEY_CHUNK, (k + 1) * KEY_CHUNK)
                ds_scr[rw, :] = (e_scr[rw, :].astype(F32) * (dpt[rw] - delta)).astype(BF16)
            sink_part = p_sink * delta
            for j in range(4):
                h = 4 * g + j
                gs_ref[h:h + 1, :] -= jnp.sum(sink_part[:, 128 * j:128 * j + 128])
            dqs.append(_dot_tn(ds_scr[...], kw) * (inv_rows * ATTN_SCALE))
            dk_w += _dot(ds_scr[...], (qgf * inv_rows).astype(BF16)) * ATTN_SCALE
            dv_w += _dot(e_scr[...], (do * inv_rows).astype(BF16))
        dk_acc[pl.ds(start, WINDOW_KEYS), :] += dk_w
        dv_acc[pl.ds(start, WINDOW_KEYS), :] += dv_w
        attn = _unstack_heads(outs[0], outs[1], lane)
        dz_ref[...] = (dyb_t * attn * (sig * (1.0 + z * (1.0 - sig)))).astype(BF16)
        dq = _unstack_heads(dqs[0], dqs[1], lane)
        cs, s1, s2 = cs_ref[...], s1_ref[...], s2_ref[...]
        for b in range(4):
            dq_ref[:, 128 * b:128 * b + 128] = _rope_t(dq[:, 128 * b:128 * b + 128], cs, s1, s2).astype(BF16)

        @pl.when(n == nb - 1)
        def _():
            dk = dk_acc[ATTN_BLOCK:ATTN_BLOCK + s, :]
            dkv_ref[:, 0:128] = _rope_t(dk, csf_ref[...], s1f_ref[...], s2f_ref[...]).astype(BF16)
            dkv_ref[:, 128:256] = dv_acc[ATTN_BLOCK:ATTN_BLOCK + s, :].astype(BF16)

    tile = _rows(ATTN_BLOCK, 512)
    tab = _rows(ATTN_BLOCK, 128)
    return pl.pallas_call(
        body, name="attn_bwd", grid=(nb,),
        out_shape=[jax.ShapeDtypeStruct((s, 512), BF16), jax.ShapeDtypeStruct((s, 512), BF16),
                   jax.ShapeDtypeStruct((s, 256), BF16), jax.ShapeDtypeStruct((8, 128), F32)],
        in_specs=[pl.BlockSpec(memory_space=pltpu.SMEM), tile, tile, tile, tab, tab, tab,
                  _full((s, 256)), _full((s, 128)), _full((s, 128)), _full((s, 128))],
        out_specs=[tile, tile, _full((s, 256)), _full((8, 128))],
        scratch_shapes=[pltpu.VMEM((s + 2 * ATTN_BLOCK, 128), BF16)] * 2
        + [pltpu.VMEM((s + 2 * ATTN_BLOCK, 128), F32)] * 2
        + [pltpu.VMEM((3, WINDOW_KEYS, STACKED), F32)] + [pltpu.VMEM((WINDOW_KEYS, STACKED), BF16)] * 2,
        compiler_params=_params(48),
    )(sink, pq, pbz, dyb, *tabs, pkv, *tabs)


def _mem_attn_bwd(pmq, pmz, mkv, dym):
    s = pmq.shape[0]
    m = mkv.shape[0]
    tm = min(512, s)

    def body(q_ref, z_ref, d_ref, mk_ref, mv_ref, dq_ref, dz_ref, dmkv_ref):
        @pl.when(pl.program_id(0) == 0)
        def _():
            dmkv_ref[...] = jnp.zeros_like(dmkv_ref)

        z = z_ref[...].astype(F32)
        sig = _sigmoid(z)
        dym_t = d_ref[...]
        d_attn = dym_t * (z * sig)
        dsilu = sig * (1.0 + z * (1.0 - sig))
        for h in range(MEM_HEADS):
            cols = slice(128 * h, 128 * h + 128)
            q, mk, mv = q_ref[:, cols], mk_ref[:, cols], mv_ref[:, cols]
            pt = _mem_softmax_t(q, mk)
            pb = pt.astype(BF16)
            o = _dot_tn(pb, mv)
            dob = d_attn[:, cols].astype(BF16)
            dpt = _dot_nt(mv, dob)
            dst = (pt * (dpt - jnp.sum(pt * dpt, axis=0, keepdims=True))).astype(BF16)
            dq_ref[:, cols] = (_dot_tn(dst, mk) * MEM_SCALE).astype(BF16)
            dz_ref[:, cols] = (dym_t[:, cols] * o * dsilu[:, cols]).astype(BF16)
            dmkv_ref[:, cols] += _dot(dst, q) * MEM_SCALE
            dmkv_ref[:, 512 + 128 * h:512 + 128 * h + 128] += _dot(pb, dob)

    return pl.pallas_call(
        body, name="mem_attn_bwd", grid=(s // tm,),
        out_shape=[jax.ShapeDtypeStruct((s, 512), BF16), jax.ShapeDtypeStruct((s, 512), BF16),
                   jax.ShapeDtypeStruct((m, D_MODEL), F32)],
        in_specs=[_rows(tm, 512), _rows(tm, 512), _rows(tm, 512), pl.BlockSpec((m, 512), lambda i: (0, 0)),
                  pl.BlockSpec((m, 512), lambda i: (0, 1))],
        out_specs=[_rows(tm, 512), _rows(tm, 512), _full((m, D_MODEL))],
        compiler_params=_params(32),
    )(pmq, pmz, dym, mkv, mkv)


def _mem_kv_bwd(mem, g_mem, mn, dmkv, w_mkv):
    m = mem.shape[0]

    def body(mem_ref, g_ref, mn_ref, d_ref, w_ref, gw_ref, gg_ref):
        db = d_ref[...].astype(BF16)
        gw_ref[...] = _dot_tn(mn_ref[...], db).astype(BF16)
        d_mn = _dot_nt(db, w_ref[...])
        xf = mem_ref[...]
        r = lax.rsqrt(jnp.mean(xf * xf, axis=-1, keepdims=True) + EPS)
        gg_ref[...] = jnp.sum(d_mn * (xf * r), axis=0, keepdims=True)

    return pl.pallas_call(
        body, name="mem_kv_bwd", grid=(1,),
        out_shape=[jax.ShapeDtypeStruct((D_MODEL, D_MODEL), BF16), jax.ShapeDtypeStruct((1, D_MODEL), F32)],
        in_specs=[_full((m, D_MODEL)), _full((1, D_MODEL)), _full((m, D_MODEL)), _full((m, D_MODEL)),
                  _full((D_MODEL, D_MODEL))],
        out_specs=[_full((D_MODEL, D_MODEL)), _full((1, D_MODEL))],
        compiler_params=_params(32),
    )(mem, g_mem, mn, dmkv, w_mkv)


def _dh_bwd(dparts, x, dy, g_pre, w_int):
    s = x.shape[0]
    tm = min(256, s)

    def body(*refs):
        d_refs = refs[:7]
        x_ref, dy_ref, g_ref, w_hbm, gx_ref, gg_ref, w_vm, sems = refs[7:]
        _load_once([(w_hbm, w_vm)], sems)

        @pl.when(pl.program_id(0) == 0)
        def _():
            gg_ref[...] = jnp.zeros_like(gg_ref)

        d_h = jnp.zeros((tm, D_MODEL), F32)
        for d_ref, (r0, width) in zip(d_refs, SEGS):
            for c0 in range(0, width, 512):
                cw = min(512, width - c0)
                d_h += _dot(d_ref[:, c0:c0 + cw], w_vm[r0 + c0:r0 + c0 + cw, :])
        xf = x_ref[...]
        r = lax.rsqrt(jnp.mean(xf * xf, axis=-1, keepdims=True) + EPS)
        xn = xf * r
        a = d_h * g_ref[...]
        gx_ref[...] = r * (a - xn * jnp.mean(a * xn, axis=-1, keepdims=True)) + dy_ref[...]
        gg_ref[...] += jnp.sum(d_h * xn, axis=0, keepdims=True)

    return pl.pallas_call(
        body, name="dh_bwd", grid=(s // tm,),
        out_shape=[jax.ShapeDtypeStruct((s, D_MODEL), F32), jax.ShapeDtypeStruct((1, D_MODEL), F32)],
        in_specs=[_rows(tm, w) for _, w in SEGS] + [_rows(tm, D_MODEL), _rows(tm, D_MODEL), _full((1, D_MODEL)), ANY],
        out_specs=[_rows(tm, D_MODEL), _full((1, D_MODEL))],
        scratch_shapes=[pltpu.VMEM((IN_WIDTH, D_MODEL), BF16), pltpu.SemaphoreType.DMA((1,))],
        compiler_params=_params(52),
    )(*dparts, x, dy, g_pre, w_int)


def _gw_in(dparts, h):
    s = h.shape[0]
    tn = 256
    starts, counts = [], []
    for r0, width in SEGS:
        starts.append(r0 // tn)
        counts.append(width // tn)

    def body(*refs):
        d_refs = refs[:7]
        h_hbm, o_ref, h_vm, sems = refs[7:]
        _load_once([(h_hbm, h_vm)], sems)
        j = pl.program_id(0)
        for d_ref, st, cnt in zip(d_refs, starts, counts):
            @pl.when((j >= st) & (j < st + cnt))
            def _(d_ref=d_ref):
                o_ref[...] = _dot_tn(d_ref[...], h_vm[...]).astype(BF16)

    def seg_spec(st, cnt):
        return pl.BlockSpec((s, tn), lambda j: (0, jnp.clip(j - st, 0, cnt - 1)))

    return pl.pallas_call(
        body, name="gw_in", grid=(IN_WIDTH // tn,),
        out_shape=jax.ShapeDtypeStruct((IN_WIDTH, D_MODEL), BF16),
        in_specs=[seg_spec(st, cnt) for st, cnt in zip(starts, counts)] + [ANY],
        out_specs=pl.BlockSpec((tn, D_MODEL), lambda j: (j, 0)),
        scratch_shapes=[pltpu.VMEM((s, D_MODEL), BF16), pltpu.SemaphoreType.DMA((1,))],
        compiler_params=_params(52),
    )(*dparts, h)


def _adamw_math(w, g, m, v):
    m2 = ADAM_B1 * m + (1.0 - ADAM_B1) * g
    v2 = ADAM_B2 * v + (1.0 - ADAM_B2) * (g * g)
    m_hat = m2 / (1.0 - ADAM_B1 ** ADAM_STEP)
    v_hat = v2 / (1.0 - ADAM_B2 ** ADAM_STEP)
    delta = -ADAM_LR * (m_hat / (jnp.sqrt(v_hat) + ADAM_EPS) + ADAM_WD * w)
    return delta, m2, v2


def _sum_adamw(own, land, chip, block, w, m, v, name, tiles=1):
    r, c = w.shape
    rt = r // tiles

    def body(c_ref, own_ref, l1_ref, l2_ref, l3_ref, w_ref, m_ref, v_ref, g_ref, d_ref, m2_ref, v2_ref):
        g = own_ref[...].astype(F32)
        for l_ref in (l1_ref, l2_ref, l3_ref):
            g += l_ref[...].astype(F32)
        g_ref[...] = g
        d_ref[...], m2_ref[...], v2_ref[...] = _adamw_math(w_ref[...], g, m_ref[...], v_ref[...])

    def share(k):
        return pl.BlockSpec((None, rt, c), lambda i, c_ref: (jnp.bitwise_xor(c_ref[0], k), block * tiles + i, 0))

    spec = pl.BlockSpec((rt, c), lambda i, c_ref: (i, 0))
    grid_spec = pltpu.PrefetchScalarGridSpec(
        num_scalar_prefetch=1, grid=(tiles,),
        in_specs=[share(0), share(1), share(2), share(3)] + [spec] * 3, out_specs=[spec] * 4)
    return pl.pallas_call(
        body, name=name, grid_spec=grid_spec,
        out_shape=[jax.ShapeDtypeStruct((r, c), F32)] * 4,
        compiler_params=_params(48),
    )(chip, own, land, land, land, w, m, v)


def _pack_sum(packs):
    def body(p_ref, o_ref):
        acc = p_ref[0]
        for k in range(1, N_DEV):
            acc += p_ref[k]
        o_ref[...] = acc

    return pl.pallas_call(
        body, name="pack_sum", grid=(1,),
        out_shape=jax.ShapeDtypeStruct((8, D_MODEL), F32),
        in_specs=[_full((N_DEV, 8, D_MODEL))], out_specs=_full((8, D_MODEL)),
    )(packs)


def _small_adamw(ws, gs, ms, vs):
    k = len(ws)

    def body(*refs):
        w_refs, g_refs, m_refs, v_refs = (refs[j * k:(j + 1) * k] for j in range(4))
        outs = refs[4 * k:]
        for j in range(k):
            outs[j][...], outs[k + j][...], outs[2 * k + j][...] = _adamw_math(
                w_refs[j][...], g_refs[j][...], m_refs[j][...], v_refs[j][...])

    specs = [_full(w.shape) for w in ws]
    res = pl.pallas_call(
        body, name="small_adamw", grid=(1,),
        out_shape=[jax.ShapeDtypeStruct(w.shape, F32) for w in ws] * 3,
        in_specs=specs * 4, out_specs=specs * 3,
    )(*ws, *gs, *ms, *vs)
    return res[:k], res[k:2 * k], res[2 * k:]


def kernel(x, mem, g_pre, w_in, w_conv, attn_sink, g_mem, w_mem_kv, w_up_a, w_up_b, w_up_m, w_out, g_post, loss_target, m_g_pre, m_w_in, m_w_conv, m_attn_sink, m_g_mem, m_w_mem_kv, m_w_up_a, m_w_up_b, m_w_up_m, m_w_out, m_g_post, v_g_pre, v_w_in, v_w_conv, v_attn_sink, v_g_mem, v_w_mem_kv, v_w_up_a, v_w_up_b, v_w_up_m, v_w_out, v_g_post):
    s = x.shape[1]
    x2, mem2, tgt2 = x[0], mem[0], loss_target[0]
    me = 4 * lax.axis_index("x") + 2 * lax.axis_index("y") + lax.axis_index("c")

    w_up_loc = jnp.concatenate([w_up_a[0], w_up_b[0], w_up_m[0]], axis=0).astype(BF16)
    w_conv_loc = jnp.zeros((8, 128), F32).at[:3, :64].set(w_conv[0])
    w_int_g, w_mkv_g, w_out_g, w_up_g, w_conv_g = _all_gather(
        [w_in[0].T.astype(BF16), w_mem_kv[0].astype(BF16), w_out[0].astype(BF16), w_up_loc, w_conv_loc],
        "gather_weights")
    w_int = w_int_g.reshape(IN_WIDTH, D_MODEL)
    w_mkv = w_mkv_g.reshape(D_MODEL, D_MODEL)
    w_out_f = w_out_g.reshape(D_MODEL, D_MODEL)
    w_conv_f = w_conv_g[:, :3, :64].transpose(1, 0, 2).reshape(3, 512)
    sink = attn_sink[0]
    tabs = _rope_tables(s)

    h, pa, pq, pkv, pbz, pmq, pmz, pg = _proj_fwd(x2, g_pre, w_int, tabs)
    mn, mkv = _mem_kv_fwd(mem2, g_mem, w_mkv)
    ya = _conv_fwd(pa, w_conv_f)
    yb = _attn_fwd(pq, pkv, pbz, sink)
    ym = _mem_attn_fwd(pmq, pmz, mkv)
    dg, dya, dyb, dym, dy, loss_p, gg_post, gw_out, gw_up = _mid(ya, yb, ym, pg, x2, tgt2, g_post, w_up_g, w_out_f)

    da, gw_conv = _conv_bwd(pa, dya, w_conv_f)
    dq, dbz, dkv, g_sink = _attn_bwd(pq, pkv, pbz, dyb, sink, tabs)
    dmq, dmz, dmkv = _mem_attn_bwd(pmq, pmz, mkv, dym)
    gw_mkv, gg_mem = _mem_kv_bwd(mem2, g_mem, mn, dmkv, w_mkv)
    dparts = (da, dq, dkv, dbz, dmq, dmz, dg)
    gw_int = _gw_in(dparts, h)

    shares = [gw_int.reshape(N_DEV, SHARD_IN, D_MODEL), gw_mkv.reshape(N_DEV, 128, D_MODEL),
              gw_out.reshape(N_DEV, 128, D_MODEL), gw_up]
    core = lax.axis_index("c").astype(jnp.int32).reshape(1)
    chip = (2 * lax.axis_index("x") + lax.axis_index("y")).astype(jnp.int32).reshape(1)
    from_sibling = _sibling_exchange(shares, "grads_to_sibling")
    chip_shares = _pair_add(shares, from_sibling, core, "grads_pair_add")
    send_sems, recv_sems, srcs, lands, token = _chip_exchange_start(chip_shares, "grads_to_chips_start")
    grad_x, gg_pre = _dh_bwd(dparts, x2, dy, g_pre + token[0:1, 0:1], w_int)
    (o_int, o_mkv, o_out, o_up), (l_int, l_mkv, l_out, l_up) = _chip_exchange_wait(
        send_sems, recv_sems, srcs, lands, grad_x, "grads_to_chips_wait")
    row3 = jnp.concatenate([gw_conv[0:1], gw_conv[1:2]], axis=1)
    row4 = jnp.concatenate([gw_conv[2:3], g_sink[:, 0].reshape(1, 8), loss_p[0:1, 0:1],
                            jnp.zeros((1, 512 - 9), F32)], axis=1)
    pack = jnp.concatenate([gg_pre, gg_mem, gg_post, row3, row4, jnp.zeros((3, D_MODEL), F32)], axis=0)
    (packs,) = _all_gather([pack], "gather_small")
    tot = _pack_sum(packs)

    g_w_in, d_w_in, nm_w_in, nv_w_in = (t.T for t in _sum_adamw(
        o_int, l_int, chip, 0, w_in[0].T, m_w_in[0].T, v_w_in[0].T, "adamw_w_in", tiles=2))
    g_mkv, d_mkv, nm_mkv, nv_mkv = _sum_adamw(
        o_mkv, l_mkv, chip, 0, w_mem_kv[0], m_w_mem_kv[0], v_w_mem_kv[0], "adamw_w_mem_kv")
    g_out, d_out, nm_out, nv_out = _sum_adamw(o_out, l_out, chip, 0, w_out[0], m_w_out[0], v_w_out[0], "adamw_w_out")
    up = [_sum_adamw(o_up, l_up, chip, k, w[0], m[0], v[0], "adamw_w_up_" + "abm"[k])
          for k, (w, m, v) in enumerate([(w_up_a, m_w_up_a, v_w_up_a), (w_up_b, m_w_up_b, v_w_up_b),
                                         (w_up_m, m_w_up_m, v_w_up_m)])]

    g_g_pre, g_g_mem, g_g_post = tot[0:1], tot[1:2], tot[2:3]
    g_conv_full = jnp.concatenate([tot[3:4, 0:512], tot[3:4, 512:1024], tot[4:5, 0:512]], axis=0)
    g_conv = lax.dynamic_slice(g_conv_full, (0, 64 * me), (3, 64))
    g_sink_tot = tot[4:5, 512:520]
    loss = tot[4, 520]
    small_w = [g_pre, w_conv[0], attn_sink, g_mem, g_post]
    small_g = [g_g_pre, g_conv, g_sink_tot, g_g_mem, g_g_post]
    small_m = [m_g_pre, m_w_conv[0], m_attn_sink, m_g_mem, m_g_post]
    small_v = [v_g_pre, v_w_conv[0], v_attn_sink, v_g_mem, v_g_post]
    sd, sm, sv = _small_adamw(small_w, small_g, small_m, small_v)

    def lead(a):
        return a[None]

    grads = [g_g_pre, lead(g_w_in), lead(g_conv), g_sink_tot, g_g_mem, lead(g_mkv), lead(up[0][0]),
             lead(up[1][0]), lead(up[2][0]), lead(g_out), g_g_post]

    def assemble(small, big_in, big_mkv, big_up, big_out):
        return [small[0], lead(big_in), lead(small[1]), small[2], small[3], lead(big_mkv), lead(big_up[0]),
                lead(big_up[1]), lead(big_up[2]), lead(big_out), small[4]]

    deltas = assemble(sd, d_w_in, d_mkv, [u[1] for u in up], d_out)
    new_m = assemble(sm, nm_w_in, nm_mkv, [u[2] for u in up], nm_out)
    new_v = assemble(sv, nv_w_in, nv_mkv, [u[3] for u in up], nv_out)
    return (loss, grad_x[None], *grads, *deltas, *new_m, *new_v)
```

```python
import functools

import jax
import jax.numpy as jnp
from jax import lax
from jax.experimental import pallas as pl
from jax.experimental.pallas import tpu as pltpu

F32 = jnp.float32
BF16 = jnp.bfloat16
MESH = pl.DeviceIdType.MESH

N_DEV = 8
D_MODEL = 1024
EPS = 1e-6
ROPE_THETA = 500000.0
ROT_DIM = 16
HEAD_DIM = 64
ATTN_BLOCK = 128
MEM_HEADS = 4
MEM_HEAD_DIM = 128
ATTN_SCALE = HEAD_DIM ** -0.5
MEM_SCALE = MEM_HEAD_DIM ** -0.5

ADAM_LR = 0.001
ADAM_B1 = 0.9
ADAM_B2 = 0.999
ADAM_EPS = 1e-08
ADAM_WD = 0.01
ADAM_STEP = 10

SEG_A = (0, 2048)
SEG_BQ = (2048, 512)
SEG_BKV = (2560, 256)
SEG_BZ = (2816, 512)
SEG_MQ = (3328, 512)
SEG_MZ = (3840, 512)
SEG_G = (4352, 3072)
SEGS = (SEG_A, SEG_BQ, SEG_BKV, SEG_BZ, SEG_MQ, SEG_MZ, SEG_G)
IN_WIDTH = 7424
SHARD_IN = IN_WIDTH // N_DEV

V7X_VMEM_BYTES = 64 * 1024 * 1024
ANY = pl.BlockSpec(memory_space=pl.ANY)


def _params(vmem_mb):
    assert vmem_mb * 1024 * 1024 < V7X_VMEM_BYTES
    return pltpu.CompilerParams(dimension_semantics=("arbitrary",), vmem_limit_bytes=vmem_mb * 1024 * 1024)


def _full(shape):
    zeros = (0,) * len(shape)
    return pl.BlockSpec(shape, lambda i: zeros)


def _rows(tm, width):
    return pl.BlockSpec((tm, width), lambda i: (i, 0))


def _dot(a, b):
    return jnp.dot(a, b, preferred_element_type=F32)


def _dot_nt(a, b):
    return lax.dot_general(a, b, (((1,), (1,)), ((), ())), preferred_element_type=F32)


def _dot_tn(a, b):
    return lax.dot_general(a, b, (((0,), (0,)), ((), ())), preferred_element_type=F32)


def _sigmoid(z):
    return 1.0 / (1.0 + jnp.exp(-z))


def _rope(t, cs, s1, s2):
    return t * cs + pltpu.roll(t, 120, 1) * s1 + pltpu.roll(t, 8, 1) * s2


def _rope_t(d, cs, s1, s2):
    return d * cs + pltpu.roll(d * s1, 8, 1) + pltpu.roll(d * s2, 120, 1)


def _rope_tables(s):
    half = ROT_DIM // 2
    inv_freq = jnp.power(jnp.float32(ROPE_THETA), -jnp.arange(half, dtype=F32) * (2.0 / ROT_DIM))
    d = jnp.arange(128) % HEAD_DIM
    ang = jnp.arange(s).astype(F32)[:, None] * inv_freq[d % half][None, :]
    cos, sin = jnp.cos(ang), jnp.sin(ang)
    lo, hi = (d < half)[None, :], ((d >= half) & (d < ROT_DIM))[None, :]
    return jnp.where(lo | hi, cos, 1.0), jnp.where(lo, -sin, 0.0), jnp.where(hi, sin, 0.0)


def _load_once(pairs, sems):
    @pl.when(pl.program_id(0) == 0)
    def _():
        cps = [pltpu.make_async_copy(src, dst, sems.at[k]) for k, (src, dst) in enumerate(pairs)]
        for cp in cps:
            cp.start()
        for cp in cps:
            cp.wait()


def _my_place():
    x, y, c = lax.axis_index("x"), lax.axis_index("y"), lax.axis_index("c")
    return x, y, c


def _all_gather(arrs, name):
    n = len(arrs)

    def body(*refs):
        ins, outs = refs[:n], refs[n:2 * n]
        send_sems, recv_sems, local_sems = refs[2 * n:]
        x, y, c = _my_place()
        me, sibling = (x, y, c), (x, y, 1 - c)
        chips = [(1 - x, y), (x, 1 - y), (1 - x, 1 - y)]

        def idx(px, py, pc):
            return 4 * px + 2 * py + pc

        def copy(a, k, block, to, src=None):
            dst = outs[a].at[idx(*block)]
            return pltpu.make_async_remote_copy(
                src_ref=dst if src is None else src, dst_ref=dst,
                send_sem=send_sems.at[a * 7 + k], recv_sem=recv_sems.at[a * 7 + k],
                device_id=to, device_id_type=MESH)

        mine = [pltpu.make_async_copy(ins[a], outs[a].at[idx(*me)], local_sems.at[a]) for a in range(n)]
        for cp in mine:
            cp.start()
        first = []
        for a in range(n):
            first.append(copy(a, 0, me, sibling, src=ins[a]))
        for j, chip in enumerate(chips):
            for a in range(n):
                first.append(copy(a, 1 + j, me, (*chip, c), src=ins[a]))
        for cp in first:
            cp.start()
        passed = []
        for j, chip in enumerate(chips):
            for a in range(n):
                copy(a, 1 + j, (*chip, c), me).wait_recv()
                cp = copy(a, 4 + j, (*chip, c), sibling)
                cp.start()
                passed.append(cp)
        for a in range(n):
            copy(a, 0, sibling, me).wait_recv()
        for j, chip in enumerate(chips):
            for a in range(n):
                copy(a, 4 + j, (*chip, 1 - c), me).wait_recv()
        for cp in first + passed:
            cp.wait_send()
        for cp in mine:
            cp.wait()

    return pl.pallas_call(
        body, name=name,
        out_shape=[jax.ShapeDtypeStruct((N_DEV,) + a.shape, a.dtype) for a in arrs],
        in_specs=[ANY] * n, out_specs=[ANY] * n,
        scratch_shapes=[pltpu.SemaphoreType.DMA((7 * n,)), pltpu.SemaphoreType.DMA((7 * n,)),
                        pltpu.SemaphoreType.DMA((n,))],
    )(*arrs)


N_CHIPS = 4


def _sibling_exchange(arrs, name):
    n = len(arrs)

    def body(*refs):
        ins, outs = refs[:n], refs[n:2 * n]
        send_sems, recv_sems = refs[2 * n:]
        x, y, c = _my_place()
        sibling = (x, y, 1 - c)

        def copy(a, j):
            return pltpu.make_async_remote_copy(
                src_ref=ins[a].at[2 * j + (1 - c)], dst_ref=outs[a].at[j],
                send_sem=send_sems.at[a * N_CHIPS + j], recv_sem=recv_sems.at[a * N_CHIPS + j],
                device_id=sibling, device_id_type=MESH)

        cps = [copy(a, j) for j in range(N_CHIPS) for a in range(n)]
        for cp in cps:
            cp.start()
        for cp in cps:
            cp.wait_recv()
        for cp in cps:
            cp.wait_send()

    return pl.pallas_call(
        body, name=name,
        out_shape=[jax.ShapeDtypeStruct((N_CHIPS,) + a.shape[1:], a.dtype) for a in arrs],
        in_specs=[ANY] * n, out_specs=[ANY] * n,
        scratch_shapes=[pltpu.SemaphoreType.DMA((N_CHIPS * n,)), pltpu.SemaphoreType.DMA((N_CHIPS * n,))],
    )(*arrs)


def _pair_add(mine, recv, core, name):
    n = len(mine)

    def body(c_ref, *refs):
        for a in range(n):
            refs[2 * n + a][...] = (refs[a][...].astype(F32) + refs[n + a][...].astype(F32)).astype(BF16)

    def blk(a):
        return (None,) + a.shape[1:]

    grid_spec = pltpu.PrefetchScalarGridSpec(
        num_scalar_prefetch=1, grid=(N_CHIPS,),
        in_specs=[pl.BlockSpec(blk(a), lambda j, c_ref: (2 * j + c_ref[0], 0, 0)) for a in mine]
        + [pl.BlockSpec(blk(a), lambda j, c_ref: (j, 0, 0)) for a in recv],
        out_specs=[pl.BlockSpec(blk(a), lambda j, c_ref: (j, 0, 0)) for a in recv])
    return pl.pallas_call(
        body, name=name, grid_spec=grid_spec,
        out_shape=[jax.ShapeDtypeStruct(a.shape, BF16) for a in recv],
        compiler_params=_params(32),
    )(core, *mine, *recv)


HBM = pl.BlockSpec(memory_space=pltpu.HBM)
SEM = pl.BlockSpec(memory_space=pltpu.SEMAPHORE)
N_PEER_CHIPS = 3


def _chip_copies(srcs, lands, send_sems, recv_sems):
    x, y, c = _my_place()
    my_chip = 2 * x + y
    peers = [(x, 1 - y), (1 - x, y), (1 - x, 1 - y)]
    cps = []
    for k, (px, py) in enumerate(peers):
        for a in range(len(srcs)):
            j = a * N_PEER_CHIPS + k
            cps.append(pltpu.make_async_remote_copy(
                src_ref=srcs[a].at[2 * px + py], dst_ref=lands[a].at[my_chip],
                send_sem=send_sems[j], recv_sem=recv_sems[j],
                device_id=(px, py, c), device_id_type=MESH))
    return cps


def _chip_exchange_start(arrs, name):
    n = len(arrs)
    k = n * N_PEER_CHIPS

    def body(*refs):
        srcs, lands = refs[:n], refs[n:2 * n]
        send_sems, recv_sems = refs[2 * n:2 * n + k], refs[2 * n + k:2 * n + 2 * k]
        token = refs[-1]
        for cp in _chip_copies(srcs, lands, send_sems, recv_sems):
            cp.start()
        token[...] = jnp.zeros_like(token)

    hbm_arrs = [pltpu.with_memory_space_constraint(a, pltpu.HBM) for a in arrs]
    lands = [pltpu.with_memory_space_constraint(lax.empty(a.shape, a.dtype), pltpu.HBM) for a in arrs]
    res = pl.pallas_call(
        body, name=name,
        out_shape=[pltpu.SemaphoreType.DMA(())] * (2 * k) + [pltpu.HBM(a.shape, a.dtype) for a in arrs] * 2
        + [jax.ShapeDtypeStruct((8, 128), F32)],
        in_specs=[HBM] * (2 * n),
        out_specs=[SEM] * (2 * k) + [HBM] * (2 * n) + [pl.BlockSpec(memory_space=pltpu.VMEM)],
        input_output_aliases={a: 2 * k + a for a in range(2 * n)},
        compiler_params=pltpu.CompilerParams(has_side_effects=pltpu.SideEffectType.DATAFLOW_SIDE_EFFECTING),
    )(*hbm_arrs, *lands)
    return res[:k], res[k:2 * k], res[2 * k:2 * k + n], res[2 * k + n:2 * k + 2 * n], res[-1]


def _chip_exchange_wait(send_sems, recv_sems, srcs, lands, after, name):
    n = len(srcs)
    k = n * N_PEER_CHIPS

    def body(*refs):
        src_refs, land_refs = refs[:n], refs[n:2 * n]
        s_sems, r_sems = refs[2 * n:2 * n + k], refs[2 * n + k:2 * n + 2 * k]
        for cp in _chip_copies(src_refs, land_refs, s_sems, r_sems):
            cp.wait_send()
            cp.wait_recv()

    res = pl.pallas_call(
        body, name=name,
        out_shape=[pltpu.HBM(a.shape, a.dtype) for a in srcs] * 2,
        in_specs=[HBM] * (2 * n) + [SEM] * (2 * k) + [ANY],
        out_specs=[HBM] * (2 * n),
        input_output_aliases={a: a for a in range(2 * n)},
        compiler_params=pltpu.CompilerParams(has_side_effects=pltpu.SideEffectType.DATAFLOW_SIDE_EFFECTING),
    )(*srcs, *lands, *send_sems, *recv_sems, after)
    return res[:n], res[n:]


def _proj_fwd(x, g_pre, w_int, tabs):
    s = x.shape[0]
    tm = min(512, s)

    def body(x_ref, g_ref, cs_ref, s1_ref, s2_ref, w_hbm,
             h_ref, pa_ref, pq_ref, pkv_ref, pbz_ref, pmq_ref, pmz_ref, pg_ref, w_vm, sems):
        _load_once([(w_hbm, w_vm)], sems)
        xf = x_ref[...]
        r = lax.rsqrt(jnp.mean(xf * xf, axis=-1, keepdims=True) + EPS)
        h = ((xf * r) * g_ref[...]).astype(BF16)
        h_ref[...] = h
        cs, s1, s2 = cs_ref[...], s1_ref[...], s2_ref[...]

        def mm(seg, c0, width):
            return _dot_nt(h, w_vm[seg[0] + c0:seg[0] + c0 + width, :])

        for c0 in range(0, SEG_A[1], 512):
            pa_ref[:, c0:c0 + 512] = mm(SEG_A, c0, 512).astype(BF16)
        q = mm(SEG_BQ, 0, 512)
        for b in range(4):
            pq_ref[:, 128 * b:128 * b + 128] = _rope(q[:, 128 * b:128 * b + 128], cs, s1, s2).astype(BF16)
        kv = mm(SEG_BKV, 0, 256)
        pkv_ref[:, 0:128] = _rope(kv[:, 0:128], cs, s1, s2).astype(BF16)
        pkv_ref[:, 128:256] = kv[:, 128:256].astype(BF16)
        pbz_ref[...] = mm(SEG_BZ, 0, 512).astype(BF16)
        pmq_ref[...] = mm(SEG_MQ, 0, 512).astype(BF16)
        pmz_ref[...] = mm(SEG_MZ, 0, 512).astype(BF16)
        for c0 in range(0, SEG_G[1], 512):
            pg_ref[:, c0:c0 + 512] = mm(SEG_G, c0, 512).astype(BF16)

    widths = (D_MODEL, 2048, 512, 256, 512, 512, 512, 3072)
    return pl.pallas_call(
        body, name="proj_fwd", grid=(s // tm,),
        out_shape=[jax.ShapeDtypeStruct((s, w), BF16) for w in widths],
        in_specs=[_rows(tm, D_MODEL), _full((1, D_MODEL)), _rows(tm, 128), _rows(tm, 128), _rows(tm, 128), ANY],
        out_specs=[_rows(tm, w) for w in widths],
        scratch_shapes=[pltpu.VMEM((IN_WIDTH, D_MODEL), BF16), pltpu.SemaphoreType.DMA((1,))],
        compiler_params=_params(52),
    )(x, g_pre, *tabs, w_int)


def _mem_kv_fwd(mem, g_mem, w_mkv):
    m = mem.shape[0]

    def body(mem_ref, g_ref, w_ref, mn_ref, mkv_ref):
        xf = mem_ref[...]
        r = lax.rsqrt(jnp.mean(xf * xf, axis=-1, keepdims=True) + EPS)
        mn = ((xf * r) * g_ref[...]).astype(BF16)
        mn_ref[...] = mn
        mkv_ref[...] = _dot(mn, w_ref[...]).astype(BF16)

    return pl.pallas_call(
        body, name="mem_kv_fwd", grid=(1,),
        out_shape=[jax.ShapeDtypeStruct((m, D_MODEL), BF16)] * 2,
        in_specs=[_full((m, D_MODEL)), _full((1, D_MODEL)), _full((D_MODEL, D_MODEL))],
        out_specs=[_full((m, D_MODEL))] * 2,
        compiler_params=_params(32),
    )(mem, g_mem, w_mkv)


def _halo_specs(s, tm, rows, width):
    nblk = s // rows
    prev = pl.BlockSpec((rows, width), lambda i: (jnp.maximum(i * (tm // rows) - 1, 0), 0))
    nxt = pl.BlockSpec((rows, width), lambda i: (jnp.minimum((i + 1) * (tm // rows), nblk - 1), 0))
    return prev, nxt


def _conv_common(pa, prev_row, next_row, w, first, last, tm):
    b, c, u, z = (pa[:, 512 * k:512 * k + 512] for k in range(4))
    cu = c * u
    cu_prev = jnp.where(first, 0.0, prev_row[:, 512:1024] * prev_row[:, 1024:1536])
    cu_next = jnp.where(last, 0.0, next_row[:, 512:1024] * next_row[:, 1024:1536])
    row = lax.broadcasted_iota(jnp.int32, (tm, 512), 0)
    cu_m1 = jnp.where(row == 0, cu_prev, pltpu.roll(cu, 1, 0))
    cu_p1 = jnp.where(row == tm - 1, cu_next, pltpu.roll(cu, tm - 1, 0))
    y = cu_m1 * w[0:1] + cu * w[1:2] + cu_p1 * w[2:3]
    sig = _sigmoid(z)
    return b, c, u, z, cu, cu_m1, cu_p1, y, sig, row


def _conv_fwd(pa, w_conv):
    s = pa.shape[0]
    tm = min(512, s)
    nt = s // tm

    def body(pa_ref, pp_ref, pn_ref, w_ref, ya_ref):
        i = pl.program_id(0)
        prev_row = pp_ref[...].astype(F32)[15:16, :]
        next_row = pn_ref[...].astype(F32)[0:1, :]
        b, _, _, z, _, _, _, y, sig, _ = _conv_common(
            pa_ref[...].astype(F32), prev_row, next_row, w_ref[...], i == 0, i == nt - 1, tm)
        ya_ref[...] = (b * y * (z * sig)).astype(BF16)

    prev, nxt = _halo_specs(s, tm, 16, 2048)
    return pl.pallas_call(
        body, name="conv_fwd", grid=(nt,),
        out_shape=jax.ShapeDtypeStruct((s, 512), BF16),
        in_specs=[_rows(tm, 2048), prev, nxt, _full((3, 512))],
        out_specs=_rows(tm, 512),
        compiler_params=_params(48),
    )(pa, pa, pa, w_conv)


def _heads_to_lanes(a, g, row):
    low = row < HEAD_DIM
    parts = []
    for b in (2 * g, 2 * g + 1):
        t = jnp.transpose(a[:, 128 * b:128 * b + 128])
        swapped = pltpu.roll(t, HEAD_DIM, 0)
        if g == 0:
            parts += [jnp.where(low, t, 0.0), jnp.where(low, swapped, 0.0)]
        else:
            parts += [jnp.where(low, 0.0, swapped), jnp.where(low, 0.0, t)]
    return jnp.concatenate(parts, axis=1)


def _lanes_to_heads(t0, t1, row):
    low = row < HEAD_DIM
    blocks = []
    for b in range(4):
        g = b // 2
        tg = (t0, t1)[g]
        je = 2 * (b - 2 * g)
        even, odd = tg[:, 128 * je:128 * je + 128], tg[:, 128 * je + 128:128 * je + 256]
        if g == 0:
            t = jnp.where(low, even, pltpu.roll(odd, HEAD_DIM, 0))
        else:
            t = jnp.where(low, pltpu.roll(even, HEAD_DIM, 0), odd)
        blocks.append(jnp.transpose(t))
    return jnp.concatenate(blocks, axis=1)


WINDOW_KEYS = 3 * ATTN_BLOCK
STACKED = 4 * ATTN_BLOCK
KEY_CHUNK = 32


def _fill_band_bias(bias, nb):
    assert nb >= 2
    c = lax.broadcasted_iota(jnp.int32, (WINDOW_KEYS, STACKED), 0)
    r = lax.broadcasted_iota(jnp.int32, (WINDOW_KEYS, STACKED), 1) & (ATTN_BLOCK - 1)
    band = (c >= r) & (c <= r + 2 * ATTN_BLOCK)
    for v, ok in enumerate((band, band & (c >= ATTN_BLOCK), band & (c < 2 * ATTN_BLOCK))):
        bias[v] = jnp.where(ok, 0.0, -jnp.inf)


def _bias_variant(n, nb):
    return jnp.where(n == 0, 1, jnp.where(n == nb - 1, 2, 0))


def _sink_row(sink_ref, g):
    return jnp.concatenate([jnp.full((1, ATTN_BLOCK), sink_ref[4 * g + j], F32) for j in range(4)], axis=1)


def _softmax_keys_major(sc, bias, variant, sink, e_scr):
    chunks = [pl.ds(k * KEY_CHUNK, KEY_CHUNK) for k in range(WINDOW_KEYS // KEY_CHUNK)]
    rows = [slice(k * KEY_CHUNK, (k + 1) * KEY_CHUNK) for k in range(WINDOW_KEYS // KEY_CHUNK)]
    m_run = jnp.full((KEY_CHUNK, STACKED), -jnp.inf, F32)
    for ck, rw in zip(chunks, rows):
        m_run = jnp.maximum(m_run, sc[rw] * ATTN_SCALE + bias[variant, ck, :])
    m = jnp.maximum(jnp.max(m_run, axis=0, keepdims=True), sink)
    l_run = jnp.zeros((KEY_CHUNK, STACKED), F32)
    for ck, rw in zip(chunks, rows):
        e = jnp.exp(sc[rw] * ATTN_SCALE + bias[variant, ck, :] - m)
        l_run += e
        e_scr[rw, :] = e.astype(BF16)
    es = jnp.exp(sink - m)
    inv = 1.0 / (jnp.sum(l_run, axis=0, keepdims=True) + es)
    return inv, es * inv


def _fill_padded(kv_ref, kpad, vpad, s):
    zero = jnp.zeros((ATTN_BLOCK, 128), BF16)
    kpad[0:ATTN_BLOCK, :] = zero
    vpad[0:ATTN_BLOCK, :] = zero
    kpad[ATTN_BLOCK + s:2 * ATTN_BLOCK + s, :] = zero
    vpad[ATTN_BLOCK + s:2 * ATTN_BLOCK + s, :] = zero
    kpad[ATTN_BLOCK:ATTN_BLOCK + s, :] = kv_ref[:, 0:128]
    vpad[ATTN_BLOCK:ATTN_BLOCK + s, :] = kv_ref[:, 128:256]


def _attn_fwd(pq, pkv, pbz, sink):
    s = pq.shape[0]
    nb = s // ATTN_BLOCK

    def body(sink_ref, q_ref, z_ref, kv_ref, yb_ref, kpad, vpad, bias, e_scr):
        n = pl.program_id(0)

        @pl.when(n == 0)
        def _():
            _fill_padded(kv_ref, kpad, vpad, s)
            _fill_band_bias(bias, nb)

        row = lax.broadcasted_iota(jnp.int32, (ATTN_BLOCK, 128), 0)
        start = pl.multiple_of(n * ATTN_BLOCK, ATTN_BLOCK)
        kw, vw = kpad[pl.ds(start, WINDOW_KEYS), :], vpad[pl.ds(start, WINDOW_KEYS), :]
        qf = q_ref[...].astype(F32)
        variant = _bias_variant(n, nb)
        outs = []
        for g in range(2):
            qt = _heads_to_lanes(qf, g, row).astype(BF16)
            inv, _ = _softmax_keys_major(_dot(kw, qt), bias, variant, _sink_row(sink_ref, g), e_scr)
            outs.append(_dot_tn(vw, e_scr[...]) * inv)
        attn = _lanes_to_heads(outs[0], outs[1], row)
        z = z_ref[...].astype(F32)
        yb_ref[...] = (attn * (z * _sigmoid(z))).astype(BF16)

    return pl.pallas_call(
        body, name="attn_fwd", grid=(nb,),
        out_shape=jax.ShapeDtypeStruct((s, 512), BF16),
        in_specs=[pl.BlockSpec(memory_space=pltpu.SMEM), _rows(ATTN_BLOCK, 512), _rows(ATTN_BLOCK, 512),
                  _full((s, 256))],
        out_specs=_rows(ATTN_BLOCK, 512),
        scratch_shapes=[pltpu.VMEM((s + 2 * ATTN_BLOCK, 128), BF16)] * 2
        + [pltpu.VMEM((3, WINDOW_KEYS, STACKED), F32), pltpu.VMEM((WINDOW_KEYS, STACKED), BF16)],
        compiler_params=_params(32),
    )(sink, pq, pbz, pkv)


def _mem_softmax_t(q, mk):
    sc = _dot_nt(mk, q) * MEM_SCALE
    e = jnp.exp(sc - jnp.max(sc, axis=0, keepdims=True))
    return e * (1.0 / jnp.sum(e, axis=0, keepdims=True))


def _mem_attn_fwd(pmq, pmz, mkv):
    s = pmq.shape[0]
    m = mkv.shape[0]
    tm = min(512, s)

    def body(q_ref, z_ref, mk_ref, mv_ref, ym_ref):
        z = z_ref[...].astype(F32)
        sz = z * _sigmoid(z)
        for h in range(MEM_HEADS):
            cols = slice(128 * h, 128 * h + 128)
            pt = _mem_softmax_t(q_ref[:, cols], mk_ref[:, cols])
            o = _dot_tn(pt.astype(BF16), mv_ref[:, cols])
            ym_ref[:, cols] = (o * sz[:, cols]).astype(BF16)

    return pl.pallas_call(
        body, name="mem_attn_fwd", grid=(s // tm,),
        out_shape=jax.ShapeDtypeStruct((s, 512), BF16),
        in_specs=[_rows(tm, 512), _rows(tm, 512), pl.BlockSpec((m, 512), lambda i: (0, 0)),
                  pl.BlockSpec((m, 512), lambda i: (0, 1))],
        out_specs=_rows(tm, 512),
        compiler_params=_params(32),
    )(pmq, pmz, mkv, mkv)


def _mid(ya, yb, ym, pg, x, target, g_post, w_up, w_out):
    s = x.shape[0]
    tm = min(256, s)
    nt = s // tm

    def body(ya_ref, yb_ref, ym_ref, pg_ref, x_ref, t_ref, gp_ref, wup_hbm, wout_hbm,
             dg_ref, dya_ref, dyb_ref, dym_ref, dy_ref, loss_ref, ggp_ref, gwout_hbm, gwup_hbm,
             wup_vm, wout_vm, acc_out, acc_up, st_out, st_up, sems):
        i = pl.program_id(0)
        _load_once([(wup_hbm.at[d], wup_vm.at[:, pl.ds(128 * d, 128)]) for d in range(N_DEV)]
                   + [(wout_hbm, wout_vm)], sems)

        @pl.when(i == 0)
        def _():
            acc_out[...] = jnp.zeros_like(acc_out)
            acc_up[...] = jnp.zeros_like(acc_up)
            loss_ref[...] = jnp.zeros_like(loss_ref)
            ggp_ref[...] = jnp.zeros_like(ggp_ref)

        ys = (ya_ref[...], yb_ref[...], ym_ref[...])
        us = [_dot(ys[k], wup_vm[512 * k:512 * k + 512, :]) for k in range(3)]
        gates = [_sigmoid(pg_ref[:, 1024 * k:1024 * k + 1024].astype(F32)) for k in range(3)]
        merged = gates[0] * us[0] + gates[1] * us[1] + gates[2] * us[2]
        mb = merged.astype(BF16)
        out = _dot(mb, wout_vm[...])
        r = lax.rsqrt(jnp.mean(out * out, axis=-1, keepdims=True) + EPS)
        on = out * r
        gp = gp_ref[...]
        err = (x_ref[...] + on * gp) - t_ref[...]
        loss_ref[...] += 0.5 * jnp.sum(err * err) * (1.0 / D_MODEL)
        dy = err * (1.0 / D_MODEL)
        dy_ref[...] = dy
        ggp_ref[...] += jnp.sum(dy * on, axis=0, keepdims=True)
        a = dy * gp
        d_out = r * (a - on * jnp.mean(a * on, axis=-1, keepdims=True))
        dob = d_out.astype(BF16)
        acc_out[...] += _dot_tn(mb, dob)
        d_merged = _dot_nt(dob, wout_vm[...])
        d_refs = (dya_ref, dyb_ref, dym_ref)
        for k in range(3):
            g = gates[k]
            dg_ref[:, 1024 * k:1024 * k + 1024] = (d_merged * us[k] * g * (1.0 - g)).astype(BF16)
            du = (d_merged * g).astype(BF16)
            d_refs[k][...] = _dot_nt(du, wup_vm[512 * k:512 * k + 512, :])
            acc_up[512 * k:512 * k + 512, :] += _dot_tn(ys[k], du)

        @pl.when(i == nt - 1)
        def _():
            st_out[...] = acc_out[...].astype(BF16)
            for d in range(N_DEV):
                st_up[d] = acc_up[:, 128 * d:128 * d + 128].astype(BF16)
            cps = [pltpu.make_async_copy(st_out, gwout_hbm, sems.at[0]),
                   pltpu.make_async_copy(st_up, gwup_hbm, sems.at[1])]
            for cp in cps:
                cp.start()
            for cp in cps:
                cp.wait()

    return pl.pallas_call(
        body, name="mid", grid=(nt,),
        out_shape=[jax.ShapeDtypeStruct((s, 3072), BF16)] + [jax.ShapeDtypeStruct((s, 512), F32)] * 3
        + [jax.ShapeDtypeStruct((s, D_MODEL), F32), jax.ShapeDtypeStruct((8, 128), F32),
           jax.ShapeDtypeStruct((1, D_MODEL), F32), jax.ShapeDtypeStruct((D_MODEL, D_MODEL), BF16),
           jax.ShapeDtypeStruct((N_DEV, 1536, 128), BF16)],
        in_specs=[_rows(tm, 512)] * 3 + [_rows(tm, 3072), _rows(tm, D_MODEL), _rows(tm, D_MODEL),
                                         _full((1, D_MODEL)), ANY, ANY],
        out_specs=[_rows(tm, 3072)] + [_rows(tm, 512)] * 3 + [_rows(tm, D_MODEL), _full((8, 128)),
                                                               _full((1, D_MODEL)), ANY, ANY],
        scratch_shapes=[pltpu.VMEM((1536, D_MODEL), BF16), pltpu.VMEM((D_MODEL, D_MODEL), BF16),
                        pltpu.VMEM((D_MODEL, D_MODEL), F32), pltpu.VMEM((1536, D_MODEL), F32),
                        pltpu.VMEM((D_MODEL, D_MODEL), BF16), pltpu.VMEM((N_DEV, 1536, 128), BF16),
                        pltpu.SemaphoreType.DMA((N_DEV + 1,))],
        compiler_params=_params(60),
    )(ya, yb, ym, pg, x, target, g_post, w_up, w_out)


def _conv_bwd(pa, dya, w_conv):
    s = pa.shape[0]
    tm = min(512, s)
    nt = s // tm

    def body(pa_ref, pp_ref, pn_ref, d_ref, dp_ref, dn_ref, w_ref, da_ref, gw_ref):
        i = pl.program_id(0)
        first, last = i == 0, i == nt - 1

        @pl.when(first)
        def _():
            gw_ref[...] = jnp.zeros_like(gw_ref)

        w = w_ref[...]
        prev_row = pp_ref[...].astype(F32)[15:16, :]
        next_row = pn_ref[...].astype(F32)[0:1, :]
        b, c, u, z, cu, cu_m1, cu_p1, y, sig, row = _conv_common(
            pa_ref[...].astype(F32), prev_row, next_row, w, first, last, tm)
        sz = z * sig
        dya_t = d_ref[...]
        d_y = dya_t * b * sz

        def halo_dy(p_row, d_row):
            zz = p_row[:, 1536:2048]
            return d_row * p_row[:, 0:512] * (zz * _sigmoid(zz))

        dy_prev = jnp.where(first, 0.0, halo_dy(prev_row, dp_ref[7:8, :]))
        dy_next = jnp.where(last, 0.0, halo_dy(next_row, dn_ref[0:1, :]))
        dy_m1 = jnp.where(row == 0, dy_prev, pltpu.roll(d_y, 1, 0))
        dy_p1 = jnp.where(row == tm - 1, dy_next, pltpu.roll(d_y, tm - 1, 0))
        d_cu = dy_p1 * w[0:1] + d_y * w[1:2] + dy_m1 * w[2:3]
        da_ref[:, 0:512] = (dya_t * y * sz).astype(BF16)
        da_ref[:, 512:1024] = (d_cu * u).astype(BF16)
        da_ref[:, 1024:1536] = (d_cu * c).astype(BF16)
        da_ref[:, 1536:2048] = (dya_t * b * y * (sig * (1.0 + z * (1.0 - sig)))).astype(BF16)
        gw_ref[0:1, :] += jnp.sum(d_y * cu_m1, axis=0, keepdims=True)
        gw_ref[1:2, :] += jnp.sum(d_y * cu, axis=0, keepdims=True)
        gw_ref[2:3, :] += jnp.sum(d_y * cu_p1, axis=0, keepdims=True)

    prev, nxt = _halo_specs(s, tm, 16, 2048)
    dprev, dnxt = _halo_specs(s, tm, 8, 512)
    return pl.pallas_call(
        body, name="conv_bwd", grid=(nt,),
        out_shape=[jax.ShapeDtypeStruct((s, 2048), BF16), jax.ShapeDtypeStruct((8, 512), F32)],
        in_specs=[_rows(tm, 2048), prev, nxt, _rows(tm, 512), dprev, dnxt, _full((3, 512))],
        out_specs=[_rows(tm, 2048), _full((8, 512))],
        compiler_params=_params(48),
    )(pa, pa, pa, dya, dya, dya, w_conv)


def _attn_bwd(pq, pkv, pbz, dyb, sink, tabs):
    s = pq.shape[0]
    nb = s // ATTN_BLOCK

    def body(sink_ref, q_ref, z_ref, d_ref, cs_ref, s1_ref, s2_ref, kv_ref, csf_ref, s1f_ref, s2f_ref,
             dq_ref, dz_ref, dkv_ref, gs_ref, kpad, vpad, dk_acc, dv_acc, bias, e_scr, ds_scr):
        n = pl.program_id(0)

        @pl.when(n == 0)
        def _():
            _fill_padded(kv_ref, kpad, vpad, s)
            _fill_band_bias(bias, nb)
            dk_acc[...] = jnp.zeros_like(dk_acc)
            dv_acc[...] = jnp.zeros_like(dv_acc)
            gs_ref[...] = jnp.zeros_like(gs_ref)

        row = lax.broadcasted_iota(jnp.int32, (ATTN_BLOCK, 128), 0)
        start = pl.multiple_of(n * ATTN_BLOCK, ATTN_BLOCK)
        kw, vw = kpad[pl.ds(start, WINDOW_KEYS), :], vpad[pl.ds(start, WINDOW_KEYS), :]
        qf = q_ref[...].astype(F32)
        variant = _bias_variant(n, nb)
        z = z_ref[...].astype(F32)
        sig = _sigmoid(z)
        dyb_t = d_ref[...]
        d_attn = dyb_t * (z * sig)
        outs, dqs = [], []
        dk_w = jnp.zeros((WINDOW_KEYS, 128), F32)
        dv_w = jnp.zeros((WINDOW_KEYS, 128), F32)
        for g in range(2):
            qt = _heads_to_lanes(qf, g, row)
            inv, p_sink = _softmax_keys_major(
                _dot(kw, qt.astype(BF16)), bias, variant, _sink_row(sink_ref, g), e_scr)
            ot = _dot_tn(vw, e_scr[...]) * inv
            outs.append(ot)
            dot_ = _heads_to_lanes(d_attn, g, row)
            delta = jnp.sum(dot_ * ot, axis=0, keepdims=True)
            dpt = _dot(vw, dot_.astype(BF16))
            for k in range(WINDOW_KEYS // KEY_CHUNK):
                rw = slice(k * KEY_CHUNK, (k + 1) * KEY_CHUNK)
                ds_scr[rw, :] = (e_scr[rw, :].astype(F32) * (dpt[rw] - delta)).astype(BF16)
            sink_part = p_sink * delta
            for j in range(4):
                h = 4 * g + j
                gs_ref[h:h + 1, :] -= jnp.sum(sink_part[:, 128 * j:128 * j + 128])
            dqs.append(_dot_tn(kw, ds_scr[...]) * (inv * ATTN_SCALE))
            dk_w += _dot_nt(ds_scr[...], (qt * inv).astype(BF16)) * ATTN_SCALE
            dv_w += _dot_nt(e_scr[...], (dot_ * inv).astype(BF16))
        dk_acc[pl.ds(start, WINDOW_KEYS), :] += dk_w
        dv_acc[pl.ds(start, WINDOW_KEYS), :] += dv_w
        attn = _lanes_to_heads(outs[0], outs[1], row)
        dz_ref[...] = (dyb_t * attn * (sig * (1.0 + z * (1.0 - sig)))).astype(BF16)
        dq = _lanes_to_heads(dqs[0], dqs[1], row)
        cs, s1, s2 = cs_ref[...], s1_ref[...], s2_ref[...]
        for b in range(4):
            dq_ref[:, 128 * b:128 * b + 128] = _rope_t(dq[:, 128 * b:128 * b + 128], cs, s1, s2).astype(BF16)

        @pl.when(n == nb - 1)
        def _():
            dk = dk_acc[ATTN_BLOCK:ATTN_BLOCK + s, :]
            dkv_ref[:, 0:128] = _rope_t(dk, csf_ref[...], s1f_ref[...], s2f_ref[...]).astype(BF16)
            dkv_ref[:, 128:256] = dv_acc[ATTN_BLOCK:ATTN_BLOCK + s, :].astype(BF16)

    tile = _rows(ATTN_BLOCK, 512)
    tab = _rows(ATTN_BLOCK, 128)
    return pl.pallas_call(
        body, name="attn_bwd", grid=(nb,),
        out_shape=[jax.ShapeDtypeStruct((s, 512), BF16), jax.ShapeDtypeStruct((s, 512), BF16),
                   jax.ShapeDtypeStruct((s, 256), BF16), jax.ShapeDtypeStruct((8, 128), F32)],
        in_specs=[pl.BlockSpec(memory_space=pltpu.SMEM), tile, tile, tile, tab, tab, tab,
                  _full((s, 256)), _full((s, 128)), _full((s, 128)), _full((s, 128))],
        out_specs=[tile, tile, _full((s, 256)), _full((8, 128))],
        scratch_shapes=[pltpu.VMEM((s + 2 * ATTN_BLOCK, 128), BF16)] * 2
        + [pltpu.VMEM((s + 2 * ATTN_BLOCK, 128), F32)] * 2
        + [pltpu.VMEM((3, WINDOW_KEYS, STACKED), F32)] + [pltpu.VMEM((WINDOW_KEYS, STACKED), BF16)] * 2,
        compiler_params=_params(48),
    )(sink, pq, pbz, dyb, *tabs, pkv, *tabs)


def _mem_attn_bwd(pmq, pmz, mkv, dym):
    s = pmq.shape[0]
    m = mkv.shape[0]
    tm = min(512, s)

    def body(q_ref, z_ref, d_ref, mk_ref, mv_ref, dq_ref, dz_ref, dmkv_ref):
        @pl.when(pl.program_id(0) == 0)
        def _():
            dmkv_ref[...] = jnp.zeros_like(dmkv_ref)

        z = z_ref[...].astype(F32)
        sig = _sigmoid(z)
        dym_t = d_ref[...]
        d_attn = dym_t * (z * sig)
        dsilu = sig * (1.0 + z * (1.0 - sig))
        for h in range(MEM_HEADS):
            cols = slice(128 * h, 128 * h + 128)
            q, mk, mv = q_ref[:, cols], mk_ref[:, cols], mv_ref[:, cols]
            pt = _mem_softmax_t(q, mk)
            pb = pt.astype(BF16)
            o = _dot_tn(pb, mv)
            dob = d_attn[:, cols].astype(BF16)
            dpt = _dot_nt(mv, dob)
            dst = (pt * (dpt - jnp.sum(pt * dpt, axis=0, keepdims=True))).astype(BF16)
            dq_ref[:, cols] = (_dot_tn(dst, mk) * MEM_SCALE).astype(BF16)
            dz_ref[:, cols] = (dym_t[:, cols] * o * dsilu[:, cols]).astype(BF16)
            dmkv_ref[:, cols] += _dot(dst, q) * MEM_SCALE
            dmkv_ref[:, 512 + 128 * h:512 + 128 * h + 128] += _dot(pb, dob)

    return pl.pallas_call(
        body, name="mem_attn_bwd", grid=(s // tm,),
        out_shape=[jax.ShapeDtypeStruct((s, 512), BF16), jax.ShapeDtypeStruct((s, 512), BF16),
                   jax.ShapeDtypeStruct((m, D_MODEL), F32)],
        in_specs=[_rows(tm, 512), _rows(tm, 512), _rows(tm, 512), pl.BlockSpec((m, 512), lambda i: (0, 0)),
                  pl.BlockSpec((m, 512), lambda i: (0, 1))],
        out_specs=[_rows(tm, 512), _rows(tm, 512), _full((m, D_MODEL))],
        compiler_params=_params(32),
    )(pmq, pmz, dym, mkv, mkv)


def _mem_kv_bwd(mem, g_mem, mn, dmkv, w_mkv):
    m = mem.shape[0]

    def body(mem_ref, g_ref, mn_ref, d_ref, w_ref, gw_ref, gg_ref):
        db = d_ref[...].astype(BF16)
        gw_ref[...] = _dot_tn(mn_ref[...], db).astype(BF16)
        d_mn = _dot_nt(db, w_ref[...])
        xf = mem_ref[...]
        r = lax.rsqrt(jnp.mean(xf * xf, axis=-1, keepdims=True) + EPS)
        gg_ref[...] = jnp.sum(d_mn * (xf * r), axis=0, keepdims=True)

    return pl.pallas_call(
        body, name="mem_kv_bwd", grid=(1,),
        out_shape=[jax.ShapeDtypeStruct((D_MODEL, D_MODEL), BF16), jax.ShapeDtypeStruct((1, D_MODEL), F32)],
        in_specs=[_full((m, D_MODEL)), _full((1, D_MODEL)), _full((m, D_MODEL)), _full((m, D_MODEL)),
                  _full((D_MODEL, D_MODEL))],
        out_specs=[_full((D_MODEL, D_MODEL)), _full((1, D_MODEL))],
        compiler_params=_params(32),
    )(mem, g_mem, mn, dmkv, w_mkv)


def _dh_bwd(dparts, x, dy, g_pre, w_int):
    s = x.shape[0]
    tm = min(256, s)

    def body(*refs):
        d_refs = refs[:7]
        x_ref, dy_ref, g_ref, w_hbm, gx_ref, gg_ref, w_vm, sems = refs[7:]
        _load_once([(w_hbm, w_vm)], sems)

        @pl.when(pl.program_id(0) == 0)
        def _():
            gg_ref[...] = jnp.zeros_like(gg_ref)

        d_h = jnp.zeros((tm, D_MODEL), F32)
        for d_ref, (r0, width) in zip(d_refs, SEGS):
            for c0 in range(0, width, 512):
                cw = min(512, width - c0)
                d_h += _dot(d_ref[:, c0:c0 + cw], w_vm[r0 + c0:r0 + c0 + cw, :])
        xf = x_ref[...]
        r = lax.rsqrt(jnp.mean(xf * xf, axis=-1, keepdims=True) + EPS)
        xn = xf * r
        a = d_h * g_ref[...]
        gx_ref[...] = r * (a - xn * jnp.mean(a * xn, axis=-1, keepdims=True)) + dy_ref[...]
        gg_ref[...] += jnp.sum(d_h * xn, axis=0, keepdims=True)

    return pl.pallas_call(
        body, name="dh_bwd", grid=(s // tm,),
        out_shape=[jax.ShapeDtypeStruct((s, D_MODEL), F32), jax.ShapeDtypeStruct((1, D_MODEL), F32)],
        in_specs=[_rows(tm, w) for _, w in SEGS] + [_rows(tm, D_MODEL), _rows(tm, D_MODEL), _full((1, D_MODEL)), ANY],
        out_specs=[_rows(tm, D_MODEL), _full((1, D_MODEL))],
        scratch_shapes=[pltpu.VMEM((IN_WIDTH, D_MODEL), BF16), pltpu.SemaphoreType.DMA((1,))],
        compiler_params=_params(52),
    )(*dparts, x, dy, g_pre, w_int)


def _gw_in(dparts, h):
    s = h.shape[0]
    tn = 256
    starts, counts = [], []
    for r0, width in SEGS:
        starts.append(r0 // tn)
        counts.append(width // tn)

    def body(*refs):
        d_refs = refs[:7]
        h_hbm, o_ref, h_vm, sems = refs[7:]
        _load_once([(h_hbm, h_vm)], sems)
        j = pl.program_id(0)
        for d_ref, st, cnt in zip(d_refs, starts, counts):
            @pl.when((j >= st) & (j < st + cnt))
            def _(d_ref=d_ref):
                o_ref[...] = _dot_tn(d_ref[...], h_vm[...]).astype(BF16)

    def seg_spec(st, cnt):
        return pl.BlockSpec((s, tn), lambda j: (0, jnp.clip(j - st, 0, cnt - 1)))

    return pl.pallas_call(
        body, name="gw_in", grid=(IN_WIDTH // tn,),
        out_shape=jax.ShapeDtypeStruct((IN_WIDTH, D_MODEL), BF16),
        in_specs=[seg_spec(st, cnt) for st, cnt in zip(starts, counts)] + [ANY],
        out_specs=pl.BlockSpec((tn, D_MODEL), lambda j: (j, 0)),
        scratch_shapes=[pltpu.VMEM((s, D_MODEL), BF16), pltpu.SemaphoreType.DMA((1,))],
        compiler_params=_params(52),
    )(*dparts, h)


def _adamw_math(w, g, m, v):
    m2 = ADAM_B1 * m + (1.0 - ADAM_B1) * g
    v2 = ADAM_B2 * v + (1.0 - ADAM_B2) * (g * g)
    m_hat = m2 / (1.0 - ADAM_B1 ** ADAM_STEP)
    v_hat = v2 / (1.0 - ADAM_B2 ** ADAM_STEP)
    delta = -ADAM_LR * (m_hat / (jnp.sqrt(v_hat) + ADAM_EPS) + ADAM_WD * w)
    return delta, m2, v2


def _sum_adamw(own, land, chip, block, w, m, v, name, tiles=1):
    r, c = w.shape
    rt = r // tiles

    def body(c_ref, own_ref, l1_ref, l2_ref, l3_ref, w_ref, m_ref, v_ref, g_ref, d_ref, m2_ref, v2_ref):
        g = own_ref[...].astype(F32)
        for l_ref in (l1_ref, l2_ref, l3_ref):
            g += l_ref[...].astype(F32)
        g_ref[...] = g
        d_ref[...], m2_ref[...], v2_ref[...] = _adamw_math(w_ref[...], g, m_ref[...], v_ref[...])

    def share(k):
        return pl.BlockSpec((None, rt, c), lambda i, c_ref: (jnp.bitwise_xor(c_ref[0], k), block * tiles + i, 0))

    spec = pl.BlockSpec((rt, c), lambda i, c_ref: (i, 0))
    grid_spec = pltpu.PrefetchScalarGridSpec(
        num_scalar_prefetch=1, grid=(tiles,),
        in_specs=[share(0), share(1), share(2), share(3)] + [spec] * 3, out_specs=[spec] * 4)
    return pl.pallas_call(
        body, name=name, grid_spec=grid_spec,
        out_shape=[jax.ShapeDtypeStruct((r, c), F32)] * 4,
        compiler_params=_params(48),
    )(chip, own, land, land, land, w, m, v)


def _pack_sum(packs):
    def body(p_ref, o_ref):
        acc = p_ref[0]
        for k in range(1, N_DEV):
            acc += p_ref[k]
        o_ref[...] = acc

    return pl.pallas_call(
        body, name="pack_sum", grid=(1,),
        out_shape=jax.ShapeDtypeStruct((8, D_MODEL), F32),
        in_specs=[_full((N_DEV, 8, D_MODEL))], out_specs=_full((8, D_MODEL)),
    )(packs)


def _small_adamw(ws, gs, ms, vs):
    k = len(ws)

    def body(*refs):
        w_refs, g_refs, m_refs, v_refs = (refs[j * k:(j + 1) * k] for j in range(4))
        outs = refs[4 * k:]
        for j in range(k):
            outs[j][...], outs[k + j][...], outs[2 * k + j][...] = _adamw_math(
                w_refs[j][...], g_refs[j][...], m_refs[j][...], v_refs[j][...])

    specs = [_full(w.shape) for w in ws]
    res = pl.pallas_call(
        body, name="small_adamw", grid=(1,),
        out_shape=[jax.ShapeDtypeStruct(w.shape, F32) for w in ws] * 3,
        in_specs=specs * 4, out_specs=specs * 3,
    )(*ws, *gs, *ms, *vs)
    return res[:k], res[k:2 * k], res[2 * k:]


def kernel(x, mem, g_pre, w_in, w_conv, attn_sink, g_mem, w_mem_kv, w_up_a, w_up_b, w_up_m, w_out, g_post, loss_target, m_g_pre, m_w_in, m_w_conv, m_attn_sink, m_g_mem, m_w_mem_kv, m_w_up_a, m_w_up_b, m_w_up_m, m_w_out, m_g_post, v_g_pre, v_w_in, v_w_conv, v_attn_sink, v_g_mem, v_w_mem_kv, v_w_up_a, v_w_up_b, v_w_up_m, v_w_out, v_g_post):
    s = x.shape[1]
    x2, mem2, tgt2 = x[0], mem[0], loss_target[0]
    me = 4 * lax.axis_index("x") + 2 * lax.axis_index("y") + lax.axis_index("c")

    w_up_loc = jnp.concatenate([w_up_a[0], w_up_b[0], w_up_m[0]], axis=0).astype(BF16)
    w_conv_loc = jnp.zeros((8, 128), F32).at[:3, :64].set(w_conv[0])
    w_int_g, w_mkv_g, w_out_g, w_up_g, w_conv_g = _all_gather(
        [w_in[0].T.astype(BF16), w_mem_kv[0].astype(BF16), w_out[0].astype(BF16), w_up_loc, w_conv_loc],
        "gather_weights")
    w_int = w_int_g.reshape(IN_WIDTH, D_MODEL)
    w_mkv = w_mkv_g.reshape(D_MODEL, D_MODEL)
    w_out_f = w_out_g.reshape(D_MODEL, D_MODEL)
    w_conv_f = w_conv_g[:, :3, :64].transpose(1, 0, 2).reshape(3, 512)
    sink = attn_sink[0]
    tabs = _rope_tables(s)

    h, pa, pq, pkv, pbz, pmq, pmz, pg = _proj_fwd(x2, g_pre, w_int, tabs)
    mn, mkv = _mem_kv_fwd(mem2, g_mem, w_mkv)
    ya = _conv_fwd(pa, w_conv_f)
    yb = _attn_fwd(pq, pkv, pbz, sink)
    ym = _mem_attn_fwd(pmq, pmz, mkv)
    dg, dya, dyb, dym, dy, loss_p, gg_post, gw_out, gw_up = _mid(ya, yb, ym, pg, x2, tgt2, g_post, w_up_g, w_out_f)

    da, gw_conv = _conv_bwd(pa, dya, w_conv_f)
    dq, dbz, dkv, g_sink = _attn_bwd(pq, pkv, pbz, dyb, sink, tabs)
    dmq, dmz, dmkv = _mem_attn_bwd(pmq, pmz, mkv, dym)
    gw_mkv, gg_mem = _mem_kv_bwd(mem2, g_mem, mn, dmkv, w_mkv)
    dparts = (da, dq, dkv, dbz, dmq, dmz, dg)
    gw_int = _gw_in(dparts, h)

    shares = [gw_int.reshape(N_DEV, SHARD_IN, D_MODEL), gw_mkv.reshape(N_DEV, 128, D_MODEL),
              gw_out.reshape(N_DEV, 128, D_MODEL), gw_up]
    core = lax.axis_index("c").astype(jnp.int32).reshape(1)
    chip = (2 * lax.axis_index("x") + lax.axis_index("y")).astype(jnp.int32).reshape(1)
    from_sibling = _sibling_exchange(shares, "grads_to_sibling")
    chip_shares = _pair_add(shares, from_sibling, core, "grads_pair_add")
    send_sems, recv_sems, srcs, lands, token = _chip_exchange_start(chip_shares, "grads_to_chips_start")
    grad_x, gg_pre = _dh_bwd(dparts, x2, dy, g_pre + token[0:1, 0:1], w_int)
    (o_int, o_mkv, o_out, o_up), (l_int, l_mkv, l_out, l_up) = _chip_exchange_wait(
        send_sems, recv_sems, srcs, lands, grad_x, "grads_to_chips_wait")
    row3 = jnp.concatenate([gw_conv[0:1], gw_conv[1:2]], axis=1)
    row4 = jnp.concatenate([gw_conv[2:3], g_sink[:, 0].reshape(1, 8), loss_p[0:1, 0:1],
                            jnp.zeros((1, 512 - 9), F32)], axis=1)
    pack = jnp.concatenate([gg_pre, gg_mem, gg_post, row3, row4, jnp.zeros((3, D_MODEL), F32)], axis=0)
    (packs,) = _all_gather([pack], "gather_small")
    tot = _pack_sum(packs)

    g_w_in, d_w_in, nm_w_in, nv_w_in = (t.T for t in _sum_adamw(
        o_int, l_int, chip, 0, w_in[0].T, m_w_in[0].T, v_w_in[0].T, "adamw_w_in", tiles=2))
    g_mkv, d_mkv, nm_mkv, nv_mkv = _sum_adamw(
        o_mkv, l_mkv, chip, 0, w_mem_kv[0], m_w_mem_kv[0], v_w_mem_kv[0], "adamw_w_mem_kv")
    g_out, d_out, nm_out, nv_out = _sum_adamw(o_out, l_out, chip, 0, w_out[0], m_w_out[0], v_w_out[0], "adamw_w_out")
    up = [_sum_adamw(o_up, l_up, chip, k, w[0], m[0], v[0], "adamw_w_up_" + "abm"[k])
          for k, (w, m, v) in enumerate([(w_up_a, m_w_up_a, v_w_up_a), (w_up_b, m_w_up_b, v_w_up_b),
                                         (w_up_m, m_w_up_m, v_w_up_m)])]

    g_g_pre, g_g_mem, g_g_post = tot[0:1], tot[1:2], tot[2:3]
    g_conv_full = jnp.concatenate([tot[3:4, 0:512], tot[3:4, 512:1024], tot[4:5, 0:512]], axis=0)
    g_conv = lax.dynamic_slice(g_conv_full, (0, 64 * me), (3, 64))
    g_sink_tot = tot[4:5, 512:520]
    loss = tot[4, 520]
    small_w = [g_pre, w_conv[0], attn_sink, g_mem, g_post]
    small_g = [g_g_pre, g_conv, g_sink_tot, g_g_mem, g_g_post]
    small_m = [m_g_pre, m_w_conv[0], m_attn_sink, m_g_mem, m_g_post]
    small_v = [v_g_pre, v_w_conv[0], v_attn_sink, v_g_mem, v_g_post]
    sd, sm, sv = _small_adamw(small_w, small_g, small_m, small_v)

    def lead(a):
        return a[None]

    grads = [g_g_pre, lead(g_w_in), lead(g_conv), g_sink_tot, g_g_mem, lead(g_mkv), lead(up[0][0]),
             lead(up[1][0]), lead(up[2][0]), lead(g_out), g_g_post]

    def assemble(small, big_in, big_mkv, big_up, big_out):
        return [small[0], lead(big_in), lead(small[1]), small[2], small[3], lead(big_mkv), lead(big_up[0]),
                lead(big_up[1]), lead(big_up[2]), lead(big_out), small[4]]

    deltas = assemble(sd, d_w_in, d_mkv, [u[1] for u in up], d_out)
    new_m = assemble(sm, nm_w_in, nm_mkv, [u[2] for u in up], nm_out)
    new_v = assemble(sv, nv_w_in, nv_mkv, [u[3] for u in up], nv_out)
    return (loss, grad_x[None], *grads, *deltas, *new_m, *new_v)
```

```python
import functools

import jax
import jax.numpy as jnp
from jax import lax
from jax.experimental import pallas as pl
from jax.experimental.pallas import tpu as pltpu

F32 = jnp.float32
BF16 = jnp.bfloat16
MESH = pl.DeviceIdType.MESH

N_DEV = 8
D_MODEL = 1024
EPS = 1e-6
ROPE_THETA = 500000.0
ROT_DIM = 16
HEAD_DIM = 64
ATTN_BLOCK = 128
MEM_HEADS = 4
MEM_HEAD_DIM = 128
ATTN_SCALE = HEAD_DIM ** -0.5
MEM_SCALE = MEM_HEAD_DIM ** -0.5

ADAM_LR = 0.001
ADAM_B1 = 0.9
ADAM_B2 = 0.999
ADAM_EPS = 1e-08
ADAM_WD = 0.01
ADAM_STEP = 10

SEG_A = (0, 2048)
SEG_BQ = (2048, 512)
SEG_BKV = (2560, 256)
SEG_BZ = (2816, 512)
SEG_MQ = (3328, 512)
SEG_MZ = (3840, 512)
SEG_G = (4352, 3072)
SEGS = (SEG_A, SEG_BQ, SEG_BKV, SEG_BZ, SEG_MQ, SEG_MZ, SEG_G)
IN_WIDTH = 7424
SHARD_IN = IN_WIDTH // N_DEV

V7X_VMEM_BYTES = 64 * 1024 * 1024
ANY = pl.BlockSpec(memory_space=pl.ANY)


def _params(vmem_mb):
    assert vmem_mb * 1024 * 1024 < V7X_VMEM_BYTES
    return pltpu.CompilerParams(dimension_semantics=("arbitrary",), vmem_limit_bytes=vmem_mb * 1024 * 1024)


def _full(shape):
    zeros = (0,) * len(shape)
    return pl.BlockSpec(shape, lambda i: zeros)


def _rows(tm, width):
    return pl.BlockSpec((tm, width), lambda i: (i, 0))


def _dot(a, b):
    return jnp.dot(a, b, preferred_element_type=F32)


def _dot_nt(a, b):
    return lax.dot_general(a, b, (((1,), (1,)), ((), ())), preferred_element_type=F32)


def _dot_tn(a, b):
    return lax.dot_general(a, b, (((0,), (0,)), ((), ())), preferred_element_type=F32)


def _sigmoid(z):
    return 1.0 / (1.0 + jnp.exp(-z))


def _rope(t, cs, s1, s2):
    return t * cs + pltpu.roll(t, 120, 1) * s1 + pltpu.roll(t, 8, 1) * s2


def _rope_t(d, cs, s1, s2):
    return d * cs + pltpu.roll(d * s1, 8, 1) + pltpu.roll(d * s2, 120, 1)


def _rope_tables(s):
    half = ROT_DIM // 2
    inv_freq = jnp.power(jnp.float32(ROPE_THETA), -jnp.arange(half, dtype=F32) * (2.0 / ROT_DIM))
    d = jnp.arange(128) % HEAD_DIM
    ang = jnp.arange(s).astype(F32)[:, None] * inv_freq[d % half][None, :]
    cos, sin = jnp.cos(ang), jnp.sin(ang)
    lo, hi = (d < half)[None, :], ((d >= half) & (d < ROT_DIM))[None, :]
    return jnp.where(lo | hi, cos, 1.0), jnp.where(lo, -sin, 0.0), jnp.where(hi, sin, 0.0)


class _SegmentLoad:
    def __init__(self, w_hbm, w_vm, sems):
        self.first = pl.program_id(0) == 0
        self.copies = [pltpu.make_async_copy(w_hbm.at[pl.ds(r0, n)], w_vm.at[pl.ds(r0, n)], sems.at[k])
                       for k, (r0, n) in enumerate(SEGS)]

        @pl.when(self.first)
        def _():
            for cp in self.copies:
                cp.start()

    def wait(self, k):
        @pl.when(self.first)
        def _():
            self.copies[k].wait()


def _load_once(pairs, sems):
    @pl.when(pl.program_id(0) == 0)
    def _():
        cps = [pltpu.make_async_copy(src, dst, sems.at[k]) for k, (src, dst) in enumerate(pairs)]
        for cp in cps:
            cp.start()
        for cp in cps:
            cp.wait()


def _my_place():
    x, y, c = lax.axis_index("x"), lax.axis_index("y"), lax.axis_index("c")
    return x, y, c


def _all_gather(arrs, name):
    n = len(arrs)

    def body(*refs):
        ins, outs = refs[:n], refs[n:2 * n]
        send_sems, recv_sems, local_sems = refs[2 * n:]
        x, y, c = _my_place()
        me, sibling = (x, y, c), (x, y, 1 - c)
        chips = [(1 - x, y), (x, 1 - y), (1 - x, 1 - y)]

        def idx(px, py, pc):
            return 4 * px + 2 * py + pc

        def copy(a, k, block, to, src=None):
            dst = outs[a].at[idx(*block)]
            return pltpu.make_async_remote_copy(
                src_ref=dst if src is None else src, dst_ref=dst,
                send_sem=send_sems.at[a * 7 + k], recv_sem=recv_sems.at[a * 7 + k],
                device_id=to, device_id_type=MESH)

        mine = [pltpu.make_async_copy(ins[a], outs[a].at[idx(*me)], local_sems.at[a]) for a in range(n)]
        for cp in mine:
            cp.start()
        first = []
        for a in range(n):
            first.append(copy(a, 0, me, sibling, src=ins[a]))
        for j, chip in enumerate(chips):
            for a in range(n):
                first.append(copy(a, 1 + j, me, (*chip, c), src=ins[a]))
        for cp in first:
            cp.start()
        passed = []
        for j, chip in enumerate(chips):
            for a in range(n):
                copy(a, 1 + j, (*chip, c), me).wait_recv()
                cp = copy(a, 4 + j, (*chip, c), sibling)
                cp.start()
                passed.append(cp)
        for a in range(n):
            copy(a, 0, sibling, me).wait_recv()
        for j, chip in enumerate(chips):
            for a in range(n):
                copy(a, 4 + j, (*chip, 1 - c), me).wait_recv()
        for cp in first + passed:
            cp.wait_send()
        for cp in mine:
            cp.wait()

    return pl.pallas_call(
        body, name=name,
        out_shape=[jax.ShapeDtypeStruct((N_DEV,) + a.shape, a.dtype) for a in arrs],
        in_specs=[ANY] * n, out_specs=[ANY] * n,
        scratch_shapes=[pltpu.SemaphoreType.DMA((7 * n,)), pltpu.SemaphoreType.DMA((7 * n,)),
                        pltpu.SemaphoreType.DMA((n,))],
    )(*arrs)


N_CHIPS = 4


def _sibling_exchange(arrs, name):
    n = len(arrs)

    def body(*refs):
        ins, outs = refs[:n], refs[n:2 * n]
        send_sems, recv_sems = refs[2 * n:]
        x, y, c = _my_place()
        sibling = (x, y, 1 - c)

        def copy(a, j):
            return pltpu.make_async_remote_copy(
                src_ref=ins[a].at[2 * j + (1 - c)], dst_ref=outs[a].at[j],
                send_sem=send_sems.at[a * N_CHIPS + j], recv_sem=recv_sems.at[a * N_CHIPS + j],
                device_id=sibling, device_id_type=MESH)

        cps = [copy(a, j) for j in range(N_CHIPS) for a in range(n)]
        for cp in cps:
            cp.start()
        for cp in cps:
            cp.wait_recv()
        for cp in cps:
            cp.wait_send()

    return pl.pallas_call(
        body, name=name,
        out_shape=[jax.ShapeDtypeStruct((N_CHIPS,) + a.shape[1:], a.dtype) for a in arrs],
        in_specs=[ANY] * n, out_specs=[ANY] * n,
        scratch_shapes=[pltpu.SemaphoreType.DMA((N_CHIPS * n,)), pltpu.SemaphoreType.DMA((N_CHIPS * n,))],
    )(*arrs)


def _pair_add(mine, recv, core, name):
    n = len(mine)

    def body(c_ref, *refs):
        for a in range(n):
            refs[2 * n + a][...] = (refs[a][...].astype(F32) + refs[n + a][...].astype(F32)).astype(BF16)

    def blk(a):
        return (None,) + a.shape[1:]

    grid_spec = pltpu.PrefetchScalarGridSpec(
        num_scalar_prefetch=1, grid=(N_CHIPS,),
        in_specs=[pl.BlockSpec(blk(a), lambda j, c_ref: (2 * j + c_ref[0], 0, 0)) for a in mine]
        + [pl.BlockSpec(blk(a), lambda j, c_ref: (j, 0, 0)) for a in recv],
        out_specs=[pl.BlockSpec(blk(a), lambda j, c_ref: (j, 0, 0)) for a in recv])
    return pl.pallas_call(
        body, name=name, grid_spec=grid_spec,
        out_shape=[jax.ShapeDtypeStruct(a.shape, BF16) for a in recv],
        compiler_params=_params(32),
    )(core, *mine, *recv)


HBM = pl.BlockSpec(memory_space=pltpu.HBM)
SEM = pl.BlockSpec(memory_space=pltpu.SEMAPHORE)
N_PEER_CHIPS = 3


def _chip_copies(srcs, lands, send_sems, recv_sems):
    x, y, c = _my_place()
    my_chip = 2 * x + y
    peers = [(x, 1 - y), (1 - x, y), (1 - x, 1 - y)]
    cps = []
    for k, (px, py) in enumerate(peers):
        for a in range(len(srcs)):
            j = a * N_PEER_CHIPS + k
            cps.append(pltpu.make_async_remote_copy(
                src_ref=srcs[a].at[2 * px + py], dst_ref=lands[a].at[my_chip],
                send_sem=send_sems[j], recv_sem=recv_sems[j],
                device_id=(px, py, c), device_id_type=MESH))
    return cps


def _chip_exchange_start(arrs, name):
    n = len(arrs)
    k = n * N_PEER_CHIPS

    def body(*refs):
        srcs, lands = refs[:n], refs[n:2 * n]
        send_sems, recv_sems = refs[2 * n:2 * n + k], refs[2 * n + k:2 * n + 2 * k]
        token = refs[-1]
        for cp in _chip_copies(srcs, lands, send_sems, recv_sems):
            cp.start()
        token[...] = jnp.zeros_like(token)

    hbm_arrs = [pltpu.with_memory_space_constraint(a, pltpu.HBM) for a in arrs]
    lands = [pltpu.with_memory_space_constraint(lax.empty(a.shape, a.dtype), pltpu.HBM) for a in arrs]
    res = pl.pallas_call(
        body, name=name,
        out_shape=[pltpu.SemaphoreType.DMA(())] * (2 * k) + [pltpu.HBM(a.shape, a.dtype) for a in arrs] * 2
        + [jax.ShapeDtypeStruct((8, 128), F32)],
        in_specs=[HBM] * (2 * n),
        out_specs=[SEM] * (2 * k) + [HBM] * (2 * n) + [pl.BlockSpec(memory_space=pltpu.VMEM)],
        input_output_aliases={a: 2 * k + a for a in range(2 * n)},
        compiler_params=pltpu.CompilerParams(has_side_effects=pltpu.SideEffectType.DATAFLOW_SIDE_EFFECTING),
    )(*hbm_arrs, *lands)
    return res[:k], res[k:2 * k], res[2 * k:2 * k + n], res[2 * k + n:2 * k + 2 * n], res[-1]


def _chip_exchange_wait(send_sems, recv_sems, srcs, lands, after, name):
    n = len(srcs)
    k = n * N_PEER_CHIPS

    def body(*refs):
        src_refs, land_refs = refs[:n], refs[n:2 * n]
        s_sems, r_sems = refs[2 * n:2 * n + k], refs[2 * n + k:2 * n + 2 * k]
        for cp in _chip_copies(src_refs, land_refs, s_sems, r_sems):
            cp.wait_send()
            cp.wait_recv()

    res = pl.pallas_call(
        body, name=name,
        out_shape=[pltpu.HBM(a.shape, a.dtype) for a in srcs] * 2,
        in_specs=[HBM] * (2 * n) + [SEM] * (2 * k) + [ANY],
        out_specs=[HBM] * (2 * n),
        input_output_aliases={a: a for a in range(2 * n)},
        compiler_params=pltpu.CompilerParams(has_side_effects=pltpu.SideEffectType.DATAFLOW_SIDE_EFFECTING),
    )(*srcs, *lands, *send_sems, *recv_sems, after)
    return res[:n], res[n:]


def _proj_fwd(x, g_pre, w_int, tabs):
    s = x.shape[0]
    tm = min(512, s)

    def body(x_ref, g_ref, cs_ref, s1_ref, s2_ref, w_hbm,
             h_ref, pa_ref, pq_ref, pkv_ref, pbz_ref, pmq_ref, pmz_ref, pg_ref, w_vm, sems):
        load = _SegmentLoad(w_hbm, w_vm, sems)
        xf = x_ref[...]
        r = lax.rsqrt(jnp.mean(xf * xf, axis=-1, keepdims=True) + EPS)
        h = ((xf * r) * g_ref[...]).astype(BF16)
        h_ref[...] = h
        cs, s1, s2 = cs_ref[...], s1_ref[...], s2_ref[...]

        def mm(seg, c0, width):
            return _dot_nt(h, w_vm[seg[0] + c0:seg[0] + c0 + width, :])

        load.wait(0)
        for c0 in range(0, SEG_A[1], 512):
            pa_ref[:, c0:c0 + 512] = mm(SEG_A, c0, 512).astype(BF16)
        load.wait(1)
        q = mm(SEG_BQ, 0, 512)
        for b in range(4):
            pq_ref[:, 128 * b:128 * b + 128] = _rope(q[:, 128 * b:128 * b + 128], cs, s1, s2).astype(BF16)
        load.wait(2)
        kv = mm(SEG_BKV, 0, 256)
        pkv_ref[:, 0:128] = _rope(kv[:, 0:128], cs, s1, s2).astype(BF16)
        pkv_ref[:, 128:256] = kv[:, 128:256].astype(BF16)
        load.wait(3)
        pbz_ref[...] = mm(SEG_BZ, 0, 512).astype(BF16)
        load.wait(4)
        pmq_ref[...] = mm(SEG_MQ, 0, 512).astype(BF16)
        load.wait(5)
        pmz_ref[...] = mm(SEG_MZ, 0, 512).astype(BF16)
        load.wait(6)
        for c0 in range(0, SEG_G[1], 512):
            pg_ref[:, c0:c0 + 512] = mm(SEG_G, c0, 512).astype(BF16)

    widths = (D_MODEL, 2048, 512, 256, 512, 512, 512, 3072)
    return pl.pallas_call(
        body, name="proj_fwd", grid=(s // tm,),
        out_shape=[jax.ShapeDtypeStruct((s, w), BF16) for w in widths],
        in_specs=[_rows(tm, D_MODEL), _full((1, D_MODEL)), _rows(tm, 128), _rows(tm, 128), _rows(tm, 128), ANY],
        out_specs=[_rows(tm, w) for w in widths],
        scratch_shapes=[pltpu.VMEM((IN_WIDTH, D_MODEL), BF16), pltpu.SemaphoreType.DMA((len(SEGS),))],
        compiler_params=_params(52),
    )(x, g_pre, *tabs, w_int)


def _mem_kv_fwd(mem, g_mem, w_mkv):
    m = mem.shape[0]

    def body(mem_ref, g_ref, w_ref, mn_ref, mkv_ref):
        xf = mem_ref[...]
        r = lax.rsqrt(jnp.mean(xf * xf, axis=-1, keepdims=True) + EPS)
        mn = ((xf * r) * g_ref[...]).astype(BF16)
        mn_ref[...] = mn
        mkv_ref[...] = _dot(mn, w_ref[...]).astype(BF16)

    return pl.pallas_call(
        body, name="mem_kv_fwd", grid=(1,),
        out_shape=[jax.ShapeDtypeStruct((m, D_MODEL), BF16)] * 2,
        in_specs=[_full((m, D_MODEL)), _full((1, D_MODEL)), _full((D_MODEL, D_MODEL))],
        out_specs=[_full((m, D_MODEL))] * 2,
        compiler_params=_params(32),
    )(mem, g_mem, w_mkv)


def _halo_specs(s, tm, rows, width):
    nblk = s // rows
    prev = pl.BlockSpec((rows, width), lambda i: (jnp.maximum(i * (tm // rows) - 1, 0), 0))
    nxt = pl.BlockSpec((rows, width), lambda i: (jnp.minimum((i + 1) * (tm // rows), nblk - 1), 0))
    return prev, nxt


def _conv_common(pa, prev_row, next_row, w, first, last, tm):
    b, c, u, z = (pa[:, 512 * k:512 * k + 512] for k in range(4))
    cu = c * u
    cu_prev = jnp.where(first, 0.0, prev_row[:, 512:1024] * prev_row[:, 1024:1536])
    cu_next = jnp.where(last, 0.0, next_row[:, 512:1024] * next_row[:, 1024:1536])
    row = lax.broadcasted_iota(jnp.int32, (tm, 512), 0)
    cu_m1 = jnp.where(row == 0, cu_prev, pltpu.roll(cu, 1, 0))
    cu_p1 = jnp.where(row == tm - 1, cu_next, pltpu.roll(cu, tm - 1, 0))
    y = cu_m1 * w[0:1] + cu * w[1:2] + cu_p1 * w[2:3]
    sig = _sigmoid(z)
    return b, c, u, z, cu, cu_m1, cu_p1, y, sig, row


def _conv_fwd(pa, w_conv):
    s = pa.shape[0]
    tm = min(512, s)
    nt = s // tm

    def body(pa_ref, pp_ref, pn_ref, w_ref, ya_ref):
        i = pl.program_id(0)
        prev_row = pp_ref[...].astype(F32)[15:16, :]
        next_row = pn_ref[...].astype(F32)[0:1, :]
        b, _, _, z, _, _, _, y, sig, _ = _conv_common(
            pa_ref[...].astype(F32), prev_row, next_row, w_ref[...], i == 0, i == nt - 1, tm)
        ya_ref[...] = (b * y * (z * sig)).astype(BF16)

    prev, nxt = _halo_specs(s, tm, 16, 2048)
    return pl.pallas_call(
        body, name="conv_fwd", grid=(nt,),
        out_shape=jax.ShapeDtypeStruct((s, 512), BF16),
        in_specs=[_rows(tm, 2048), prev, nxt, _full((3, 512))],
        out_specs=_rows(tm, 512),
        compiler_params=_params(48),
    )(pa, pa, pa, w_conv)


def _heads_to_lanes(a, g, row):
    low = row < HEAD_DIM
    parts = []
    for b in (2 * g, 2 * g + 1):
        t = jnp.transpose(a[:, 128 * b:128 * b + 128])
        swapped = pltpu.roll(t, HEAD_DIM, 0)
        if g == 0:
            parts += [jnp.where(low, t, 0.0), jnp.where(low, swapped, 0.0)]
        else:
            parts += [jnp.where(low, 0.0, swapped), jnp.where(low, 0.0, t)]
    return jnp.concatenate(parts, axis=1)


def _lanes_to_heads(t0, t1, row):
    low = row < HEAD_DIM
    blocks = []
    for b in range(4):
        g = b // 2
        tg = (t0, t1)[g]
        je = 2 * (b - 2 * g)
        even, odd = tg[:, 128 * je:128 * je + 128], tg[:, 128 * je + 128:128 * je + 256]
        if g == 0:
            t = jnp.where(low, even, pltpu.roll(odd, HEAD_DIM, 0))
        else:
            t = jnp.where(low, pltpu.roll(even, HEAD_DIM, 0), odd)
        blocks.append(jnp.transpose(t))
    return jnp.concatenate(blocks, axis=1)


WINDOW_KEYS = 3 * ATTN_BLOCK
STACKED = 4 * ATTN_BLOCK
KEY_CHUNK = 32


def _fill_band_bias(bias, nb):
    assert nb >= 2
    c = lax.broadcasted_iota(jnp.int32, (WINDOW_KEYS, STACKED), 0)
    r = lax.broadcasted_iota(jnp.int32, (WINDOW_KEYS, STACKED), 1) & (ATTN_BLOCK - 1)
    band = (c >= r) & (c <= r + 2 * ATTN_BLOCK)
    for v, ok in enumerate((band, band & (c >= ATTN_BLOCK), band & (c < 2 * ATTN_BLOCK))):
        bias[v] = jnp.where(ok, 0.0, -jnp.inf)


def _bias_variant(n, nb):
    return jnp.where(n == 0, 1, jnp.where(n == nb - 1, 2, 0))


def _sink_row(sink_ref, g):
    return jnp.concatenate([jnp.full((1, ATTN_BLOCK), sink_ref[4 * g + j], F32) for j in range(4)], axis=1)


def _softmax_keys_major(sc, bias, variant, sink, e_scr):
    chunks = [pl.ds(k * KEY_CHUNK, KEY_CHUNK) for k in range(WINDOW_KEYS // KEY_CHUNK)]
    rows = [slice(k * KEY_CHUNK, (k + 1) * KEY_CHUNK) for k in range(WINDOW_KEYS // KEY_CHUNK)]
    m_run = jnp.full((KEY_CHUNK, STACKED), -jnp.inf, F32)
    for ck, rw in zip(chunks, rows):
        m_run = jnp.maximum(m_run, sc[rw] * ATTN_SCALE + bias[variant, ck, :])
    m = jnp.maximum(jnp.max(m_run, axis=0, keepdims=True), sink)
    l_run = jnp.zeros((KEY_CHUNK, STACKED), F32)
    for ck, rw in zip(chunks, rows):
        e = jnp.exp(sc[rw] * ATTN_SCALE + bias[variant, ck, :] - m)
        l_run += e
        e_scr[rw, :] = e.astype(BF16)
    es = jnp.exp(sink - m)
    inv = 1.0 / (jnp.sum(l_run, axis=0, keepdims=True) + es)
    return inv, es * inv


def _fill_padded(kv_ref, kpad, vpad, s):
    zero = jnp.zeros((ATTN_BLOCK, 128), BF16)
    kpad[0:ATTN_BLOCK, :] = zero
    vpad[0:ATTN_BLOCK, :] = zero
    kpad[ATTN_BLOCK + s:2 * ATTN_BLOCK + s, :] = zero
    vpad[ATTN_BLOCK + s:2 * ATTN_BLOCK + s, :] = zero
    kpad[ATTN_BLOCK:ATTN_BLOCK + s, :] = kv_ref[:, 0:128]
    vpad[ATTN_BLOCK:ATTN_BLOCK + s, :] = kv_ref[:, 128:256]


def _attn_fwd(pq, pkv, pbz, sink):
    s = pq.shape[0]
    nb = s // ATTN_BLOCK

    def body(sink_ref, q_ref, z_ref, kv_ref, yb_ref, kpad, vpad, bias, e_scr):
        n = pl.program_id(0)

        @pl.when(n == 0)
        def _():
            _fill_padded(kv_ref, kpad, vpad, s)
            _fill_band_bias(bias, nb)

        row = lax.broadcasted_iota(jnp.int32, (ATTN_BLOCK, 128), 0)
        start = pl.multiple_of(n * ATTN_BLOCK, ATTN_BLOCK)
        kw, vw = kpad[pl.ds(start, WINDOW_KEYS), :], vpad[pl.ds(start, WINDOW_KEYS), :]
        qf = q_ref[...].astype(F32)
        variant = _bias_variant(n, nb)
        outs = []
        for g in range(2):
            qt = _heads_to_lanes(qf, g, row).astype(BF16)
            inv, _ = _softmax_keys_major(_dot(kw, qt), bias, variant, _sink_row(sink_ref, g), e_scr)
            outs.append(_dot_tn(vw, e_scr[...]) * inv)
        attn = _lanes_to_heads(outs[0], outs[1], row)
        z = z_ref[...].astype(F32)
        yb_ref[...] = (attn * (z * _sigmoid(z))).astype(BF16)

    return pl.pallas_call(
        body, name="attn_fwd", grid=(nb,),
        out_shape=jax.ShapeDtypeStruct((s, 512), BF16),
        in_specs=[pl.BlockSpec(memory_space=pltpu.SMEM), _rows(ATTN_BLOCK, 512), _rows(ATTN_BLOCK, 512),
                  _full((s, 256))],
        out_specs=_rows(ATTN_BLOCK, 512),
        scratch_shapes=[pltpu.VMEM((s + 2 * ATTN_BLOCK, 128), BF16)] * 2
        + [pltpu.VMEM((3, WINDOW_KEYS, STACKED), F32), pltpu.VMEM((WINDOW_KEYS, STACKED), BF16)],
        compiler_params=_params(32),
    )(sink, pq, pbz, pkv)


def _mem_softmax_t(q, mk):
    sc = _dot_nt(mk, q) * MEM_SCALE
    e = jnp.exp(sc - jnp.max(sc, axis=0, keepdims=True))
    return e * (1.0 / jnp.sum(e, axis=0, keepdims=True))


def _mem_attn_fwd(pmq, pmz, mkv):
    s = pmq.shape[0]
    m = mkv.shape[0]
    tm = min(512, s)

    def body(q_ref, z_ref, mk_ref, mv_ref, ym_ref):
        z = z_ref[...].astype(F32)
        sz = z * _sigmoid(z)
        for h in range(MEM_HEADS):
            cols = slice(128 * h, 128 * h + 128)
            pt = _mem_softmax_t(q_ref[:, cols], mk_ref[:, cols])
            o = _dot_tn(pt.astype(BF16), mv_ref[:, cols])
            ym_ref[:, cols] = (o * sz[:, cols]).astype(BF16)

    return pl.pallas_call(
        body, name="mem_attn_fwd", grid=(s // tm,),
        out_shape=jax.ShapeDtypeStruct((s, 512), BF16),
        in_specs=[_rows(tm, 512), _rows(tm, 512), pl.BlockSpec((m, 512), lambda i: (0, 0)),
                  pl.BlockSpec((m, 512), lambda i: (0, 1))],
        out_specs=_rows(tm, 512),
        compiler_params=_params(32),
    )(pmq, pmz, mkv, mkv)


def _mid(ya, yb, ym, pg, x, target, g_post, w_up, w_out):
    s = x.shape[0]
    tm = min(256, s)
    nt = s // tm

    def body(ya_ref, yb_ref, ym_ref, pg_ref, x_ref, t_ref, gp_ref, wup_hbm, wout_hbm,
             dg_ref, dya_ref, dyb_ref, dym_ref, dy_ref, loss_ref, ggp_ref, mb_ref, dob_ref, du_ref,
             wup_vm, wout_vm, sems):
        i = pl.program_id(0)
        _load_once([(wup_hbm.at[d], wup_vm.at[:, pl.ds(128 * d, 128)]) for d in range(N_DEV)]
                   + [(wout_hbm, wout_vm)], sems)

        @pl.when(i == 0)
        def _():
            loss_ref[...] = jnp.zeros_like(loss_ref)
            ggp_ref[...] = jnp.zeros_like(ggp_ref)

        ys = (ya_ref[...], yb_ref[...], ym_ref[...])
        us = [_dot(ys[k], wup_vm[512 * k:512 * k + 512, :]) for k in range(3)]
        gates = [_sigmoid(pg_ref[:, 1024 * k:1024 * k + 1024].astype(F32)) for k in range(3)]
        merged = gates[0] * us[0] + gates[1] * us[1] + gates[2] * us[2]
        mb = merged.astype(BF16)
        mb_ref[...] = mb
        out = _dot(mb, wout_vm[...])
        r = lax.rsqrt(jnp.mean(out * out, axis=-1, keepdims=True) + EPS)
        on = out * r
        gp = gp_ref[...]
        err = (x_ref[...] + on * gp) - t_ref[...]
        loss_ref[...] += 0.5 * jnp.sum(err * err) * (1.0 / D_MODEL)
        dy = err * (1.0 / D_MODEL)
        dy_ref[...] = dy
        ggp_ref[...] += jnp.sum(dy * on, axis=0, keepdims=True)
        a = dy * gp
        d_out = r * (a - on * jnp.mean(a * on, axis=-1, keepdims=True))
        dob = d_out.astype(BF16)
        dob_ref[...] = dob
        d_merged = _dot_nt(dob, wout_vm[...])
        d_refs = (dya_ref, dyb_ref, dym_ref)
        for k in range(3):
            g = gates[k]
            dg_ref[:, 1024 * k:1024 * k + 1024] = (d_merged * us[k] * g * (1.0 - g)).astype(BF16)
            du = (d_merged * g).astype(BF16)
            du_ref[k] = du
            d_refs[k][...] = _dot_nt(du, wup_vm[512 * k:512 * k + 512, :])

    return pl.pallas_call(
        body, name="mid", grid=(nt,),
        out_shape=[jax.ShapeDtypeStruct((s, 3072), BF16)] + [jax.ShapeDtypeStruct((s, 512), F32)] * 3
        + [jax.ShapeDtypeStruct((s, D_MODEL), F32), jax.ShapeDtypeStruct((8, 128), F32),
           jax.ShapeDtypeStruct((1, D_MODEL), F32), jax.ShapeDtypeStruct((s, D_MODEL), BF16),
           jax.ShapeDtypeStruct((s, D_MODEL), BF16), jax.ShapeDtypeStruct((3, s, D_MODEL), BF16)],
        in_specs=[_rows(tm, 512)] * 3 + [_rows(tm, 3072), _rows(tm, D_MODEL), _rows(tm, D_MODEL),
                                         _full((1, D_MODEL)), ANY, ANY],
        out_specs=[_rows(tm, 3072)] + [_rows(tm, 512)] * 3
        + [_rows(tm, D_MODEL), _full((8, 128)), _full((1, D_MODEL)), _rows(tm, D_MODEL), _rows(tm, D_MODEL),
           pl.BlockSpec((3, tm, D_MODEL), lambda i: (0, i, 0))],
        scratch_shapes=[pltpu.VMEM((1536, D_MODEL), BF16), pltpu.VMEM((D_MODEL, D_MODEL), BF16),
                        pltpu.SemaphoreType.DMA((N_DEV + 1,))],
        compiler_params=_params(56),
    )(ya, yb, ym, pg, x, target, g_post, w_up, w_out)


def _gw_mid(mb, dob, ys, du):
    s = mb.shape[0]
    tn = 256

    def out_body(mb_ref, dob_ref, o_ref):
        o_ref[...] = _dot_tn(mb_ref[...], dob_ref[...]).astype(BF16)

    gw_out = pl.pallas_call(
        out_body, name="gw_out", grid=(D_MODEL // tn,),
        out_shape=jax.ShapeDtypeStruct((D_MODEL, D_MODEL), BF16),
        in_specs=[pl.BlockSpec((s, tn), lambda j: (0, j)), _full((s, D_MODEL))],
        out_specs=pl.BlockSpec((tn, D_MODEL), lambda j: (j, 0)),
        compiler_params=_params(48),
    )(mb, dob)

    per = 512 // tn

    def up_body(ya_ref, yb_ref, ym_ref, du_ref, o_ref):
        j = pl.program_id(0)
        for k, y_ref in enumerate((ya_ref, yb_ref, ym_ref)):
            @pl.when(j // per == k)
            def _(y_ref=y_ref):
                res = _dot_tn(y_ref[...], du_ref[...])
                for d in range(N_DEV):
                    o_ref[d] = res[:, 128 * d:128 * d + 128].astype(BF16)

    def y_spec(k):
        return pl.BlockSpec((s, tn), lambda j: (0, jnp.clip(j - per * k, 0, per - 1)))

    gw_up = pl.pallas_call(
        up_body, name="gw_up", grid=(3 * per,),
        out_shape=jax.ShapeDtypeStruct((N_DEV, 1536, 128), BF16),
        in_specs=[y_spec(0), y_spec(1), y_spec(2), pl.BlockSpec((None, s, D_MODEL), lambda j: (j // per, 0, 0))],
        out_specs=pl.BlockSpec((N_DEV, tn, 128), lambda j: (0, j, 0)),
        compiler_params=_params(48),
    )(*ys, du)
    return gw_out, gw_up


def _conv_bwd(pa, dya, w_conv):
    s = pa.shape[0]
    tm = min(512, s)
    nt = s // tm

    def body(pa_ref, pp_ref, pn_ref, d_ref, dp_ref, dn_ref, w_ref, da_ref, gw_ref):
        i = pl.program_id(0)
        first, last = i == 0, i == nt - 1

        @pl.when(first)
        def _():
            gw_ref[...] = jnp.zeros_like(gw_ref)

        w = w_ref[...]
        prev_row = pp_ref[...].astype(F32)[15:16, :]
        next_row = pn_ref[...].astype(F32)[0:1, :]
        b, c, u, z, cu, cu_m1, cu_p1, y, sig, row = _conv_common(
            pa_ref[...].astype(F32), prev_row, next_row, w, first, last, tm)
        sz = z * sig
        dya_t = d_ref[...]
        d_y = dya_t * b * sz

        def halo_dy(p_row, d_row):
            zz = p_row[:, 1536:2048]
            return d_row * p_row[:, 0:512] * (zz * _sigmoid(zz))

        dy_prev = jnp.where(first, 0.0, halo_dy(prev_row, dp_ref[7:8, :]))
        dy_next = jnp.where(last, 0.0, halo_dy(next_row, dn_ref[0:1, :]))
        dy_m1 = jnp.where(row == 0, dy_prev, pltpu.roll(d_y, 1, 0))
        dy_p1 = jnp.where(row == tm - 1, dy_next, pltpu.roll(d_y, tm - 1, 0))
        d_cu = dy_p1 * w[0:1] + d_y * w[1:2] + dy_m1 * w[2:3]
        da_ref[:, 0:512] = (dya_t * y * sz).astype(BF16)
        da_ref[:, 512:1024] = (d_cu * u).astype(BF16)
        da_ref[:, 1024:1536] = (d_cu * c).astype(BF16)
        da_ref[:, 1536:2048] = (dya_t * b * y * (sig * (1.0 + z * (1.0 - sig)))).astype(BF16)
        gw_ref[0:1, :] += jnp.sum(d_y * cu_m1, axis=0, keepdims=True)
        gw_ref[1:2, :] += jnp.sum(d_y * cu, axis=0, keepdims=True)
        gw_ref[2:3, :] += jnp.sum(d_y * cu_p1, axis=0, keepdims=True)

    prev, nxt = _halo_specs(s, tm, 16, 2048)
    dprev, dnxt = _halo_specs(s, tm, 8, 512)
    return pl.pallas_call(
        body, name="conv_bwd", grid=(nt,),
        out_shape=[jax.ShapeDtypeStruct((s, 2048), BF16), jax.ShapeDtypeStruct((8, 512), F32)],
        in_specs=[_rows(tm, 2048), prev, nxt, _rows(tm, 512), dprev, dnxt, _full((3, 512))],
        out_specs=[_rows(tm, 2048), _full((8, 512))],
        compiler_params=_params(48),
    )(pa, pa, pa, dya, dya, dya, w_conv)


def _attn_bwd(pq, pkv, pbz, dyb, sink, tabs):
    s = pq.shape[0]
    nb = s // ATTN_BLOCK

    def body(sink_ref, q_ref, z_ref, d_ref, cs_ref, s1_ref, s2_ref, kv_ref, csf_ref, s1f_ref, s2f_ref,
             dq_ref, dz_ref, dkv_ref, gs_ref, kpad, vpad, dk_acc, dv_acc, bias, e_scr, ds_scr):
        n = pl.program_id(0)

        @pl.when(n == 0)
        def _():
            _fill_padded(kv_ref, kpad, vpad, s)
            _fill_band_bias(bias, nb)
            dk_acc[...] = jnp.zeros_like(dk_acc)
            dv_acc[...] = jnp.zeros_like(dv_acc)
            gs_ref[...] = jnp.zeros_like(gs_ref)

        row = lax.broadcasted_iota(jnp.int32, (ATTN_BLOCK, 128), 0)
        start = pl.multiple_of(n * ATTN_BLOCK, ATTN_BLOCK)
        kw, vw = kpad[pl.ds(start, WINDOW_KEYS), :], vpad[pl.ds(start, WINDOW_KEYS), :]
        qf = q_ref[...].astype(F32)
        variant = _bias_variant(n, nb)
        z = z_ref[...].astype(F32)
        sig = _sigmoid(z)
        dyb_t = d_ref[...]
        d_attn = dyb_t * (z * sig)
        outs, dqs = [], []
        dk_w = jnp.zeros((WINDOW_KEYS, 128), F32)
        dv_w = jnp.zeros((WINDOW_KEYS, 128), F32)
        for g in range(2):
            qt = _heads_to_lanes(qf, g, row)
            inv, p_sink = _softmax_keys_major(
                _dot(kw, qt.astype(BF16)), bias, variant, _sink_row(sink_ref, g), e_scr)
            ot = _dot_tn(vw, e_scr[...]) * inv
            outs.append(ot)
            dot_ = _heads_to_lanes(d_attn, g, row)
            delta = jnp.sum(dot_ * ot, axis=0, keepdims=True)
            dpt = _dot(vw, dot_.astype(BF16))
            for k in range(WINDOW_KEYS // KEY_CHUNK):
                rw = slice(k * KEY_CHUNK, (k + 1) * KEY_CHUNK)
                ds_scr[rw, :] = (e_scr[rw, :].astype(F32) * (dpt[rw] - delta)).astype(BF16)
            sink_part = p_sink * delta
            for j in range(4):
                h = 4 * g + j
                gs_ref[h:h + 1, :] -= jnp.sum(sink_part[:, 128 * j:128 * j + 128])
            dqs.append(_dot_tn(kw, ds_scr[...]) * (inv * ATTN_SCALE))
            dk_w += _dot_nt(ds_scr[...], (qt * inv).astype(BF16)) * ATTN_SCALE
            dv_w += _dot_nt(e_scr[...], (dot_ * inv).astype(BF16))
        dk_acc[pl.ds(start, WINDOW_KEYS), :] += dk_w
        dv_acc[pl.ds(start, WINDOW_KEYS), :] += dv_w
        attn = _lanes_to_heads(outs[0], outs[1], row)
        dz_ref[...] = (dyb_t * attn * (sig * (1.0 + z * (1.0 - sig)))).astype(BF16)
        dq = _lanes_to_heads(dqs[0], dqs[1], row)
        cs, s1, s2 = cs_ref[...], s1_ref[...], s2_ref[...]
        for b in range(4):
            dq_ref[:, 128 * b:128 * b + 128] = _rope_t(dq[:, 128 * b:128 * b + 128], cs, s1, s2).astype(BF16)

        @pl.when(n == nb - 1)
        def _():
            dk = dk_acc[ATTN_BLOCK:ATTN_BLOCK + s, :]
            dkv_ref[:, 0:128] = _rope_t(dk, csf_ref[...], s1f_ref[...], s2f_ref[...]).astype(BF16)
            dkv_ref[:, 128:256] = dv_acc[ATTN_BLOCK:ATTN_BLOCK + s, :].astype(BF16)

    tile = _rows(ATTN_BLOCK, 512)
    tab = _rows(ATTN_BLOCK, 128)
    return pl.pallas_call(
        body, name="attn_bwd", grid=(nb,),
        out_shape=[jax.ShapeDtypeStruct((s, 512), BF16), jax.ShapeDtypeStruct((s, 512), BF16),
                   jax.ShapeDtypeStruct((s, 256), BF16), jax.ShapeDtypeStruct((8, 128), F32)],
        in_specs=[pl.BlockSpec(memory_space=pltpu.SMEM), tile, tile, tile, tab, tab, tab,
                  _full((s, 256)), _full((s, 128)), _full((s, 128)), _full((s, 128))],
        out_specs=[tile, tile, _full((s, 256)), _full((8, 128))],
        scratch_shapes=[pltpu.VMEM((s + 2 * ATTN_BLOCK, 128), BF16)] * 2
        + [pltpu.VMEM((s + 2 * ATTN_BLOCK, 128), F32)] * 2
        + [pltpu.VMEM((3, WINDOW_KEYS, STACKED), F32)] + [pltpu.VMEM((WINDOW_KEYS, STACKED), BF16)] * 2,
        compiler_params=_params(48),
    )(sink, pq, pbz, dyb, *tabs, pkv, *tabs)


def _mem_attn_bwd(pmq, pmz, mkv, dym):
    s = pmq.shape[0]
    m = mkv.shape[0]
    tm = min(512, s)

    def body(q_ref, z_ref, d_ref, mk_ref, mv_ref, dq_ref, dz_ref, dmkv_ref):
        @pl.when(pl.program_id(0) == 0)
        def _():
            dmkv_ref[...] = jnp.zeros_like(dmkv_ref)

        z = z_ref[...].astype(F32)
        sig = _sigmoid(z)
        dym_t = d_ref[...]
        d_attn = dym_t * (z * sig)
        dsilu = sig * (1.0 + z * (1.0 - sig))
        for h in range(MEM_HEADS):
            cols = slice(128 * h, 128 * h + 128)
            q, mk, mv = q_ref[:, cols], mk_ref[:, cols], mv_ref[:, cols]
            pt = _mem_softmax_t(q, mk)
            pb = pt.astype(BF16)
            o = _dot_tn(pb, mv)
            dob = d_attn[:, cols].astype(BF16)
            dpt = _dot_nt(mv, dob)
            dst = (pt * (dpt - jnp.sum(pt * dpt, axis=0, keepdims=True))).astype(BF16)
            dq_ref[:, cols] = (_dot_tn(dst, mk) * MEM_SCALE).astype(BF16)
            dz_ref[:, cols] = (dym_t[:, cols] * o * dsilu[:, cols]).astype(BF16)
            dmkv_ref[:, cols] += _dot(dst, q) * MEM_SCALE
            dmkv_ref[:, 512 + 128 * h:512 + 128 * h + 128] += _dot(pb, dob)

    return pl.pallas_call(
        body, name="mem_attn_bwd", grid=(s // tm,),
        out_shape=[jax.ShapeDtypeStruct((s, 512), BF16), jax.ShapeDtypeStruct((s, 512), BF16),
                   jax.ShapeDtypeStruct((m, D_MODEL), F32)],
        in_specs=[_rows(tm, 512), _rows(tm, 512), _rows(tm, 512), pl.BlockSpec((m, 512), lambda i: (0, 0)),
                  pl.BlockSpec((m, 512), lambda i: (0, 1))],
        out_specs=[_rows(tm, 512), _rows(tm, 512), _full((m, D_MODEL))],
        compiler_params=_params(32),
    )(pmq, pmz, dym, mkv, mkv)


def _mem_kv_bwd(mem, g_mem, mn, dmkv, w_mkv):
    m = mem.shape[0]

    def body(mem_ref, g_ref, mn_ref, d_ref, w_ref, gw_ref, gg_ref):
        db = d_ref[...].astype(BF16)
        gw_ref[...] = _dot_tn(mn_ref[...], db).astype(BF16)
        d_mn = _dot_nt(db, w_ref[...])
        xf = mem_ref[...]
        r = lax.rsqrt(jnp.mean(xf * xf, axis=-1, keepdims=True) + EPS)
        gg_ref[...] = jnp.sum(d_mn * (xf * r), axis=0, keepdims=True)

    return pl.pallas_call(
        body, name="mem_kv_bwd", grid=(1,),
        out_shape=[jax.ShapeDtypeStruct((D_MODEL, D_MODEL), BF16), jax.ShapeDtypeStruct((1, D_MODEL), F32)],
        in_specs=[_full((m, D_MODEL)), _full((1, D_MODEL)), _full((m, D_MODEL)), _full((m, D_MODEL)),
                  _full((D_MODEL, D_MODEL))],
        out_specs=[_full((D_MODEL, D_MODEL)), _full((1, D_MODEL))],
        compiler_params=_params(32),
    )(mem, g_mem, mn, dmkv, w_mkv)


def _dh_bwd(dparts, x, dy, g_pre, w_int):
    s = x.shape[0]
    tm = min(256, s)

    def body(*refs):
        d_refs = refs[:7]
        x_ref, dy_ref, g_ref, w_hbm, gx_ref, gg_ref, w_vm, sems = refs[7:]
        load = _SegmentLoad(w_hbm, w_vm, sems)

        @pl.when(pl.program_id(0) == 0)
        def _():
            gg_ref[...] = jnp.zeros_like(gg_ref)

        d_h = jnp.zeros((tm, D_MODEL), F32)
        for k, (d_ref, (r0, width)) in enumerate(zip(d_refs, SEGS)):
            load.wait(k)
            for c0 in range(0, width, 512):
                cw = min(512, width - c0)
                d_h += _dot(d_ref[:, c0:c0 + cw], w_vm[r0 + c0:r0 + c0 + cw, :])
        xf = x_ref[...]
        r = lax.rsqrt(jnp.mean(xf * xf, axis=-1, keepdims=True) + EPS)
        xn = xf * r
        a = d_h * g_ref[...]
        gx_ref[...] = r * (a - xn * jnp.mean(a * xn, axis=-1, keepdims=True)) + dy_ref[...]
        gg_ref[...] += jnp.sum(d_h * xn, axis=0, keepdims=True)

    return pl.pallas_call(
        body, name="dh_bwd", grid=(s // tm,),
        out_shape=[jax.ShapeDtypeStruct((s, D_MODEL), F32), jax.ShapeDtypeStruct((1, D_MODEL), F32)],
        in_specs=[_rows(tm, w) for _, w in SEGS] + [_rows(tm, D_MODEL), _rows(tm, D_MODEL), _full((1, D_MODEL)), ANY],
        out_specs=[_rows(tm, D_MODEL), _full((1, D_MODEL))],
        scratch_shapes=[pltpu.VMEM((IN_WIDTH, D_MODEL), BF16), pltpu.SemaphoreType.DMA((len(SEGS),))],
        compiler_params=_params(52),
    )(*dparts, x, dy, g_pre, w_int)


def _gw_in(dparts, h):
    s = h.shape[0]
    tn = 256
    starts, counts = [], []
    for r0, width in SEGS:
        starts.append(r0 // tn)
        counts.append(width // tn)

    def body(*refs):
        d_refs = refs[:7]
        h_hbm, o_ref, h_vm, sems = refs[7:]
        _load_once([(h_hbm, h_vm)], sems)
        j = pl.program_id(0)
        for d_ref, st, cnt in zip(d_refs, starts, counts):
            @pl.when((j >= st) & (j < st + cnt))
            def _(d_ref=d_ref):
                o_ref[...] = _dot_tn(d_ref[...], h_vm[...]).astype(BF16)

    def seg_spec(st, cnt):
        return pl.BlockSpec((s, tn), lambda j: (0, jnp.clip(j - st, 0, cnt - 1)))

    return pl.pallas_call(
        body, name="gw_in", grid=(IN_WIDTH // tn,),
        out_shape=jax.ShapeDtypeStruct((IN_WIDTH, D_MODEL), BF16),
        in_specs=[seg_spec(st, cnt) for st, cnt in zip(starts, counts)] + [ANY],
        out_specs=pl.BlockSpec((tn, D_MODEL), lambda j: (j, 0)),
        scratch_shapes=[pltpu.VMEM((s, D_MODEL), BF16), pltpu.SemaphoreType.DMA((1,))],
        compiler_params=_params(52),
    )(*dparts, h)


def _adamw_math(w, g, m, v):
    m2 = ADAM_B1 * m + (1.0 - ADAM_B1) * g
    v2 = ADAM_B2 * v + (1.0 - ADAM_B2) * (g * g)
    m_hat = m2 / (1.0 - ADAM_B1 ** ADAM_STEP)
    v_hat = v2 / (1.0 - ADAM_B2 ** ADAM_STEP)
    delta = -ADAM_LR * (m_hat / (jnp.sqrt(v_hat) + ADAM_EPS) + ADAM_WD * w)
    return delta, m2, v2


def _sum_adamw(own, land, chip, block, w, m, v, name, tiles=1):
    r, c = w.shape
    rt = r // tiles

    def body(c_ref, own_ref, l1_ref, l2_ref, l3_ref, w_ref, m_ref, v_ref, g_ref, d_ref, m2_ref, v2_ref):
        g = own_ref[...].astype(F32)
        for l_ref in (l1_ref, l2_ref, l3_ref):
            g += l_ref[...].astype(F32)
        g_ref[...] = g
        d_ref[...], m2_ref[...], v2_ref[...] = _adamw_math(w_ref[...], g, m_ref[...], v_ref[...])

    def share(k):
        return pl.BlockSpec((None, rt, c), lambda i, c_ref: (jnp.bitwise_xor(c_ref[0], k), block * tiles + i, 0))

    spec = pl.BlockSpec((rt, c), lambda i, c_ref: (i, 0))
    grid_spec = pltpu.PrefetchScalarGridSpec(
        num_scalar_prefetch=1, grid=(tiles,),
        in_specs=[share(0), share(1), share(2), share(3)] + [spec] * 3, out_specs=[spec] * 4)
    return pl.pallas_call(
        body, name=name, grid_spec=grid_spec,
        out_shape=[jax.ShapeDtypeStruct((r, c), F32)] * 4,
        compiler_params=_params(48),
    )(chip, own, land, land, land, w, m, v)


def _pack_sum(packs):
    def body(p_ref, o_ref):
        acc = p_ref[0]
        for k in range(1, N_DEV):
            acc += p_ref[k]
        o_ref[...] = acc

    return pl.pallas_call(
        body, name="pack_sum", grid=(1,),
        out_shape=jax.ShapeDtypeStruct((8, D_MODEL), F32),
        in_specs=[_full((N_DEV, 8, D_MODEL))], out_specs=_full((8, D_MODEL)),
    )(packs)


def _small_adamw(ws, gs, ms, vs):
    k = len(ws)

    def body(*refs):
        w_refs, g_refs, m_refs, v_refs = (refs[j * k:(j + 1) * k] for j in range(4))
        outs = refs[4 * k:]
        for j in range(k):
            outs[j][...], outs[k + j][...], outs[2 * k + j][...] = _adamw_math(
                w_refs[j][...], g_refs[j][...], m_refs[j][...], v_refs[j][...])

    specs = [_full(w.shape) for w in ws]
    res = pl.pallas_call(
        body, name="small_adamw", grid=(1,),
        out_shape=[jax.ShapeDtypeStruct(w.shape, F32) for w in ws] * 3,
        in_specs=specs * 4, out_specs=specs * 3,
    )(*ws, *gs, *ms, *vs)
    return res[:k], res[k:2 * k], res[2 * k:]


def kernel(x, mem, g_pre, w_in, w_conv, attn_sink, g_mem, w_mem_kv, w_up_a, w_up_b, w_up_m, w_out, g_post, loss_target, m_g_pre, m_w_in, m_w_conv, m_attn_sink, m_g_mem, m_w_mem_kv, m_w_up_a, m_w_up_b, m_w_up_m, m_w_out, m_g_post, v_g_pre, v_w_in, v_w_conv, v_attn_sink, v_g_mem, v_w_mem_kv, v_w_up_a, v_w_up_b, v_w_up_m, v_w_out, v_g_post):
    s = x.shape[1]
    x2, mem2, tgt2 = x[0], mem[0], loss_target[0]
    me = 4 * lax.axis_index("x") + 2 * lax.axis_index("y") + lax.axis_index("c")

    w_up_loc = jnp.concatenate([w_up_a[0], w_up_b[0], w_up_m[0]], axis=0).astype(BF16)
    w_conv_loc = jnp.zeros((8, 128), F32).at[:3, :64].set(w_conv[0])
    w_int_g, w_mkv_g, w_out_g, w_up_g, w_conv_g = _all_gather(
        [w_in[0].T.astype(BF16), w_mem_kv[0].astype(BF16), w_out[0].astype(BF16), w_up_loc, w_conv_loc],
        "gather_weights")
    w_int = w_int_g.reshape(IN_WIDTH, D_MODEL)
    w_mkv = w_mkv_g.reshape(D_MODEL, D_MODEL)
    w_out_f = w_out_g.reshape(D_MODEL, D_MODEL)
    w_conv_f = w_conv_g[:, :3, :64].transpose(1, 0, 2).reshape(3, 512)
    sink = attn_sink[0]
    tabs = _rope_tables(s)

    h, pa, pq, pkv, pbz, pmq, pmz, pg = _proj_fwd(x2, g_pre, w_int, tabs)
    mn, mkv = _mem_kv_fwd(mem2, g_mem, w_mkv)
    ya = _conv_fwd(pa, w_conv_f)
    yb = _attn_fwd(pq, pkv, pbz, sink)
    ym = _mem_attn_fwd(pmq, pmz, mkv)
    dg, dya, dyb, dym, dy, loss_p, gg_post, mb, dob, du = _mid(ya, yb, ym, pg, x2, tgt2, g_post, w_up_g, w_out_f)
    gw_out, gw_up = _gw_mid(mb, dob, (ya, yb, ym), du)

    da, gw_conv = _conv_bwd(pa, dya, w_conv_f)
    dq, dbz, dkv, g_sink = _attn_bwd(pq, pkv, pbz, dyb, sink, tabs)
    dmq, dmz, dmkv = _mem_attn_bwd(pmq, pmz, mkv, dym)
    gw_mkv, gg_mem = _mem_kv_bwd(mem2, g_mem, mn, dmkv, w_mkv)
    dparts = (da, dq, dkv, dbz, dmq, dmz, dg)
    gw_int = _gw_in(dparts, h)

    shares = [gw_int.reshape(N_DEV, SHARD_IN, D_MODEL), gw_mkv.reshape(N_DEV, 128, D_MODEL),
              gw_out.reshape(N_DEV, 128, D_MODEL), gw_up]
    core = lax.axis_index("c").astype(jnp.int32).reshape(1)
    chip = (2 * lax.axis_index("x") + lax.axis_index("y")).astype(jnp.int32).reshape(1)
    from_sibling = _sibling_exchange(shares, "grads_to_sibling")
    chip_shares = _pair_add(shares, from_sibling, core, "grads_pair_add")
    send_sems, recv_sems, srcs, lands, token = _chip_exchange_start(chip_shares, "grads_to_chips_start")
    grad_x, gg_pre = _dh_bwd(dparts, x2, dy, g_pre + token[0:1, 0:1], w_int)
    (o_int, o_mkv, o_out, o_up), (l_int, l_mkv, l_out, l_up) = _chip_exchange_wait(
        send_sems, recv_sems, srcs, lands, grad_x, "grads_to_chips_wait")
    row3 = jnp.concatenate([gw_conv[0:1], gw_conv[1:2]], axis=1)
    row4 = jnp.concatenate([gw_conv[2:3], g_sink[:, 0].reshape(1, 8), loss_p[0:1, 0:1],
                            jnp.zeros((1, 512 - 9), F32)], axis=1)
    pack = jnp.concatenate([gg_pre, gg_mem, gg_post, row3, row4, jnp.zeros((3, D_MODEL), F32)], axis=0)
    (packs,) = _all_gather([pack], "gather_small")
    tot = _pack_sum(packs)

    g_w_in, d_w_in, nm_w_in, nv_w_in = (t.T for t in _sum_adamw(
        o_int, l_int, chip, 0, w_in[0].T, m_w_in[0].T, v_w_in[0].T, "adamw_w_in", tiles=2))
    g_mkv, d_mkv, nm_mkv, nv_mkv = _sum_adamw(
        o_mkv, l_mkv, chip, 0, w_mem_kv[0], m_w_mem_kv[0], v_w_mem_kv[0], "adamw_w_mem_kv")
    g_out, d_out, nm_out, nv_out = _sum_adamw(o_out, l_out, chip, 0, w_out[0], m_w_out[0], v_w_out[0], "adamw_w_out")
    up = [_sum_adamw(o_up, l_up, chip, k, w[0], m[0], v[0], "adamw_w_up_" + "abm"[k])
          for k, (w, m, v) in enumerate([(w_up_a, m_w_up_a, v_w_up_a), (w_up_b, m_w_up_b, v_w_up_b),
                                         (w_up_m, m_w_up_m, v_w_up_m)])]

    g_g_pre, g_g_mem, g_g_post = tot[0:1], tot[1:2], tot[2:3]
    g_conv_full = jnp.concatenate([tot[3:4, 0:512], tot[3:4, 512:1024], tot[4:5, 0:512]], axis=0)
    g_conv = lax.dynamic_slice(g_conv_full, (0, 64 * me), (3, 64))
    g_sink_tot = tot[4:5, 512:520]
    loss = tot[4, 520]
    small_w = [g_pre, w_conv[0], attn_sink, g_mem, g_post]
    small_g = [g_g_pre, g_conv, g_sink_tot, g_g_mem, g_g_post]
    small_m = [m_g_pre, m_w_conv[0], m_attn_sink, m_g_mem, m_g_post]
    small_v = [v_g_pre, v_w_conv[0], v_attn_sink, v_g_mem, v_g_post]
    sd, sm, sv = _small_adamw(small_w, small_g, small_m, small_v)

    def lead(a):
        return a[None]

    grads = [g_g_pre, lead(g_w_in), lead(g_conv), g_sink_tot, g_g_mem, lead(g_mkv), lead(up[0][0]),
             lead(up[1][0]), lead(up[2][0]), lead(g_out), g_g_post]

    def assemble(small, big_in, big_mkv, big_up, big_out):
        return [small[0], lead(big_in), lead(small[1]), small[2], small[3], lead(big_mkv), lead(big_up[0]),
                lead(big_up[1]), lead(big_up[2]), lead(big_out), small[4]]

    deltas = assemble(sd, d_w_in, d_mkv, [u[1] for u in up], d_out)
    new_m = assemble(sm, nm_w_in, nm_mkv, [u[2] for u in up], nm_out)
    new_v = assemble(sv, nv_w_in, nv_mkv, [u[3] for u in up], nv_out)
    return (loss, grad_x[None], *grads, *deltas, *new_m, *new_v)
```

```python
import functools

import jax
import jax.numpy as jnp
from jax import lax
from jax.experimental import pallas as pl
from jax.experimental.pallas import tpu as pltpu

F32 = jnp.float32
BF16 = jnp.bfloat16
MESH = pl.DeviceIdType.MESH

N_DEV = 8
D_MODEL = 1024
EPS = 1e-6
ROPE_THETA = 500000.0
ROT_DIM = 16
HEAD_DIM = 64
ATTN_BLOCK = 128
MEM_HEADS = 4
MEM_HEAD_DIM = 128
ATTN_SCALE = HEAD_DIM ** -0.5
MEM_SCALE = MEM_HEAD_DIM ** -0.5

ADAM_LR = 0.001
ADAM_B1 = 0.9
ADAM_B2 = 0.999
ADAM_EPS = 1e-08
ADAM_WD = 0.01
ADAM_STEP = 10

SEG_A = (0, 2048)
SEG_BQ = (2048, 512)
SEG_BKV = (2560, 256)
SEG_BZ = (2816, 512)
SEG_MQ = (3328, 512)
SEG_MZ = (3840, 512)
SEG_G = (4352, 3072)
SEGS = (SEG_A, SEG_BQ, SEG_BKV, SEG_BZ, SEG_MQ, SEG_MZ, SEG_G)
IN_WIDTH = 7424
SHARD_IN = IN_WIDTH // N_DEV

V7X_VMEM_BYTES = 64 * 1024 * 1024
ANY = pl.BlockSpec(memory_space=pl.ANY)


def _params(vmem_mb):
    assert vmem_mb * 1024 * 1024 < V7X_VMEM_BYTES
    return pltpu.CompilerParams(dimension_semantics=("arbitrary",), vmem_limit_bytes=vmem_mb * 1024 * 1024)


def _full(shape):
    zeros = (0,) * len(shape)
    return pl.BlockSpec(shape, lambda i: zeros)


def _rows(tm, width):
    return pl.BlockSpec((tm, width), lambda i: (i, 0))


def _dot(a, b):
    return jnp.dot(a, b, preferred_element_type=F32)


def _dot_nt(a, b):
    return lax.dot_general(a, b, (((1,), (1,)), ((), ())), preferred_element_type=F32)


def _dot_tn(a, b):
    return lax.dot_general(a, b, (((0,), (0,)), ((), ())), preferred_element_type=F32)


def _sigmoid(z):
    return 1.0 / (1.0 + jnp.exp(-z))


def _rope(t, cs, s1, s2):
    return t * cs + pltpu.roll(t, 120, 1) * s1 + pltpu.roll(t, 8, 1) * s2


def _rope_t(d, cs, s1, s2):
    return d * cs + pltpu.roll(d * s1, 8, 1) + pltpu.roll(d * s2, 120, 1)


def _rope_tables(s):
    half = ROT_DIM // 2
    inv_freq = jnp.power(jnp.float32(ROPE_THETA), -jnp.arange(half, dtype=F32) * (2.0 / ROT_DIM))
    d = jnp.arange(128) % HEAD_DIM
    ang = jnp.arange(s).astype(F32)[:, None] * inv_freq[d % half][None, :]
    cos, sin = jnp.cos(ang), jnp.sin(ang)
    lo, hi = (d < half)[None, :], ((d >= half) & (d < ROT_DIM))[None, :]
    return jnp.where(lo | hi, cos, 1.0), jnp.where(lo, -sin, 0.0), jnp.where(hi, sin, 0.0)


def _load_once(pairs, sems):
    @pl.when(pl.program_id(0) == 0)
    def _():
        cps = [pltpu.make_async_copy(src, dst, sems.at[k]) for k, (src, dst) in enumerate(pairs)]
        for cp in cps:
            cp.start()
        for cp in cps:
            cp.wait()


def _my_place():
    x, y, c = lax.axis_index("x"), lax.axis_index("y"), lax.axis_index("c")
    return x, y, c


def _all_gather(arrs, name):
    n = len(arrs)

    def body(*refs):
        ins, outs = refs[:n], refs[n:2 * n]
        send_sems, recv_sems, local_sems = refs[2 * n:]
        x, y, c = _my_place()
        me, sibling = (x, y, c), (x, y, 1 - c)
        chips = [(1 - x, y), (x, 1 - y), (1 - x, 1 - y)]

        def idx(px, py, pc):
            return 4 * px + 2 * py + pc

        def copy(a, k, block, to, src=None):
            dst = outs[a].at[idx(*block)]
            return pltpu.make_async_remote_copy(
                src_ref=dst if src is None else src, dst_ref=dst,
                send_sem=send_sems.at[a * 7 + k], recv_sem=recv_sems.at[a * 7 + k],
                device_id=to, device_id_type=MESH)

        mine = [pltpu.make_async_copy(ins[a], outs[a].at[idx(*me)], local_sems.at[a]) for a in range(n)]
        for cp in mine:
            cp.start()
        first = []
        for a in range(n):
            first.append(copy(a, 0, me, sibling, src=ins[a]))
        for j, chip in enumerate(chips):
            for a in range(n):
                first.append(copy(a, 1 + j, me, (*chip, c), src=ins[a]))
        for cp in first:
            cp.start()
        passed = []
        for j, chip in enumerate(chips):
            for a in range(n):
                copy(a, 1 + j, (*chip, c), me).wait_recv()
                cp = copy(a, 4 + j, (*chip, c), sibling)
                cp.start()
                passed.append(cp)
        for a in range(n):
            copy(a, 0, sibling, me).wait_recv()
        for j, chip in enumerate(chips):
            for a in range(n):
                copy(a, 4 + j, (*chip, 1 - c), me).wait_recv()
        for cp in first + passed:
            cp.wait_send()
        for cp in mine:
            cp.wait()

    return pl.pallas_call(
        body, name=name,
        out_shape=[jax.ShapeDtypeStruct((N_DEV,) + a.shape, a.dtype) for a in arrs],
        in_specs=[ANY] * n, out_specs=[ANY] * n,
        scratch_shapes=[pltpu.SemaphoreType.DMA((7 * n,)), pltpu.SemaphoreType.DMA((7 * n,)),
                        pltpu.SemaphoreType.DMA((n,))],
    )(*arrs)


N_CHIPS = 4


def _sibling_exchange(arrs, name):
    n = len(arrs)

    def body(*refs):
        ins, outs = refs[:n], refs[n:2 * n]
        send_sems, recv_sems = refs[2 * n:]
        x, y, c = _my_place()
        sibling = (x, y, 1 - c)

        def copy(a, j):
            return pltpu.make_async_remote_copy(
                src_ref=ins[a].at[2 * j + (1 - c)], dst_ref=outs[a].at[j],
                send_sem=send_sems.at[a * N_CHIPS + j], recv_sem=recv_sems.at[a * N_CHIPS + j],
                device_id=sibling, device_id_type=MESH)

        cps = [copy(a, j) for j in range(N_CHIPS) for a in range(n)]
        for cp in cps:
            cp.start()
        for cp in cps:
            cp.wait_recv()
        for cp in cps:
            cp.wait_send()

    return pl.pallas_call(
        body, name=name,
        out_shape=[jax.ShapeDtypeStruct((N_CHIPS,) + a.shape[1:], a.dtype) for a in arrs],
        in_specs=[ANY] * n, out_specs=[ANY] * n,
        scratch_shapes=[pltpu.SemaphoreType.DMA((N_CHIPS * n,)), pltpu.SemaphoreType.DMA((N_CHIPS * n,))],
    )(*arrs)


def _pair_add(mine, recv, core, name):
    n = len(mine)

    def body(c_ref, *refs):
        for a in range(n):
            refs[2 * n + a][...] = (refs[a][...].astype(F32) + refs[n + a][...].astype(F32)).astype(BF16)

    def blk(a):
        return (None,) + a.shape[1:]

    grid_spec = pltpu.PrefetchScalarGridSpec(
        num_scalar_prefetch=1, grid=(N_CHIPS,),
        in_specs=[pl.BlockSpec(blk(a), lambda j, c_ref: (2 * j + c_ref[0], 0, 0)) for a in mine]
        + [pl.BlockSpec(blk(a), lambda j, c_ref: (j, 0, 0)) for a in recv],
        out_specs=[pl.BlockSpec(blk(a), lambda j, c_ref: (j, 0, 0)) for a in recv])
    return pl.pallas_call(
        body, name=name, grid_spec=grid_spec,
        out_shape=[jax.ShapeDtypeStruct(a.shape, BF16) for a in recv],
        compiler_params=_params(32),
    )(core, *mine, *recv)


HBM = pl.BlockSpec(memory_space=pltpu.HBM)
SEM = pl.BlockSpec(memory_space=pltpu.SEMAPHORE)
N_PEER_CHIPS = 3


def _chip_copies(srcs, lands, send_sems, recv_sems):
    x, y, c = _my_place()
    my_chip = 2 * x + y
    peers = [(x, 1 - y), (1 - x, y), (1 - x, 1 - y)]
    cps = []
    for k, (px, py) in enumerate(peers):
        for a in range(len(srcs)):
            j = a * N_PEER_CHIPS + k
            cps.append(pltpu.make_async_remote_copy(
                src_ref=srcs[a].at[2 * px + py], dst_ref=lands[a].at[my_chip],
                send_sem=send_sems[j], recv_sem=recv_sems[j],
                device_id=(px, py, c), device_id_type=MESH))
    return cps


def _chip_exchange_start(arrs, name):
    n = len(arrs)
    k = n * N_PEER_CHIPS

    def body(*refs):
        srcs, lands = refs[:n], refs[n:2 * n]
        send_sems, recv_sems = refs[2 * n:2 * n + k], refs[2 * n + k:2 * n + 2 * k]
        token = refs[-1]
        for cp in _chip_copies(srcs, lands, send_sems, recv_sems):
            cp.start()
        token[...] = jnp.zeros_like(token)

    hbm_arrs = [pltpu.with_memory_space_constraint(a, pltpu.HBM) for a in arrs]
    lands = [pltpu.with_memory_space_constraint(lax.empty(a.shape, a.dtype), pltpu.HBM) for a in arrs]
    res = pl.pallas_call(
        body, name=name,
        out_shape=[pltpu.SemaphoreType.DMA(())] * (2 * k) + [pltpu.HBM(a.shape, a.dtype) for a in arrs] * 2
        + [jax.ShapeDtypeStruct((8, 128), F32)],
        in_specs=[HBM] * (2 * n),
        out_specs=[SEM] * (2 * k) + [HBM] * (2 * n) + [pl.BlockSpec(memory_space=pltpu.VMEM)],
        input_output_aliases={a: 2 * k + a for a in range(2 * n)},
        compiler_params=pltpu.CompilerParams(has_side_effects=pltpu.SideEffectType.DATAFLOW_SIDE_EFFECTING),
    )(*hbm_arrs, *lands)
    return res[:k], res[k:2 * k], res[2 * k:2 * k + n], res[2 * k + n:2 * k + 2 * n], res[-1]


def _chip_exchange_wait(send_sems, recv_sems, srcs, lands, after, name):
    n = len(srcs)
    k = n * N_PEER_CHIPS

    def body(*refs):
        src_refs, land_refs = refs[:n], refs[n:2 * n]
        s_sems, r_sems = refs[2 * n:2 * n + k], refs[2 * n + k:2 * n + 2 * k]
        for cp in _chip_copies(src_refs, land_refs, s_sems, r_sems):
            cp.wait_send()
            cp.wait_recv()

    res = pl.pallas_call(
        body, name=name,
        out_shape=[pltpu.HBM(a.shape, a.dtype) for a in srcs] * 2,
        in_specs=[HBM] * (2 * n) + [SEM] * (2 * k) + [ANY],
        out_specs=[HBM] * (2 * n),
        input_output_aliases={a: a for a in range(2 * n)},
        compiler_params=pltpu.CompilerParams(has_side_effects=pltpu.SideEffectType.DATAFLOW_SIDE_EFFECTING),
    )(*srcs, *lands, *send_sems, *recv_sems, after)
    return res[:n], res[n:]


def _proj_fwd(x, g_pre, w_int, tabs):
    s = x.shape[0]
    tm = min(512, s)

    def body(x_ref, g_ref, cs_ref, s1_ref, s2_ref, w_hbm,
             h_ref, pa_ref, pq_ref, pkv_ref, pbz_ref, pmq_ref, pmz_ref, pg_ref, w_vm, sems):
        _load_once([(w_hbm, w_vm)], sems)
        xf = x_ref[...]
        r = lax.rsqrt(jnp.mean(xf * xf, axis=-1, keepdims=True) + EPS)
        h = ((xf * r) * g_ref[...]).astype(BF16)
        h_ref[...] = h
        cs, s1, s2 = cs_ref[...], s1_ref[...], s2_ref[...]

        def mm(seg, c0, width):
            return _dot_nt(h, w_vm[seg[0] + c0:seg[0] + c0 + width, :])

        for c0 in range(0, SEG_A[1], 512):
            pa_ref[:, c0:c0 + 512] = mm(SEG_A, c0, 512).astype(BF16)
        q = mm(SEG_BQ, 0, 512)
        for b in range(4):
            pq_ref[:, 128 * b:128 * b + 128] = _rope(q[:, 128 * b:128 * b + 128], cs, s1, s2).astype(BF16)
        kv = mm(SEG_BKV, 0, 256)
        pkv_ref[:, 0:128] = _rope(kv[:, 0:128], cs, s1, s2).astype(BF16)
        pkv_ref[:, 128:256] = kv[:, 128:256].astype(BF16)
        pbz_ref[...] = mm(SEG_BZ, 0, 512).astype(BF16)
        pmq_ref[...] = mm(SEG_MQ, 0, 512).astype(BF16)
        pmz_ref[...] = mm(SEG_MZ, 0, 512).astype(BF16)
        for c0 in range(0, SEG_G[1], 512):
            pg_ref[:, c0:c0 + 512] = mm(SEG_G, c0, 512).astype(BF16)

    widths = (D_MODEL, 2048, 512, 256, 512, 512, 512, 3072)
    return pl.pallas_call(
        body, name="proj_fwd", grid=(s // tm,),
        out_shape=[jax.ShapeDtypeStruct((s, w), BF16) for w in widths],
        in_specs=[_rows(tm, D_MODEL), _full((1, D_MODEL)), _rows(tm, 128), _rows(tm, 128), _rows(tm, 128), ANY],
        out_specs=[_rows(tm, w) for w in widths],
        scratch_shapes=[pltpu.VMEM((IN_WIDTH, D_MODEL), BF16), pltpu.SemaphoreType.DMA((1,))],
        compiler_params=_params(52),
    )(x, g_pre, *tabs, w_int)


def _mem_kv_fwd(mem, g_mem, w_mkv):
    m = mem.shape[0]

    def body(mem_ref, g_ref, w_ref, mn_ref, mkv_ref):
        xf = mem_ref[...]
        r = lax.rsqrt(jnp.mean(xf * xf, axis=-1, keepdims=True) + EPS)
        mn = ((xf * r) * g_ref[...]).astype(BF16)
        mn_ref[...] = mn
        mkv_ref[...] = _dot(mn, w_ref[...]).astype(BF16)

    return pl.pallas_call(
        body, name="mem_kv_fwd", grid=(1,),
        out_shape=[jax.ShapeDtypeStruct((m, D_MODEL), BF16)] * 2,
        in_specs=[_full((m, D_MODEL)), _full((1, D_MODEL)), _full((D_MODEL, D_MODEL))],
        out_specs=[_full((m, D_MODEL))] * 2,
        compiler_params=_params(32),
    )(mem, g_mem, w_mkv)


def _halo_specs(s, tm, rows, width):
    nblk = s // rows
    prev = pl.BlockSpec((rows, width), lambda i: (jnp.maximum(i * (tm // rows) - 1, 0), 0))
    nxt = pl.BlockSpec((rows, width), lambda i: (jnp.minimum((i + 1) * (tm // rows), nblk - 1), 0))
    return prev, nxt


def _conv_common(pa, prev_row, next_row, w, first, last, tm):
    b, c, u, z = (pa[:, 512 * k:512 * k + 512] for k in range(4))
    cu = c * u
    cu_prev = jnp.where(first, 0.0, prev_row[:, 512:1024] * prev_row[:, 1024:1536])
    cu_next = jnp.where(last, 0.0, next_row[:, 512:1024] * next_row[:, 1024:1536])
    row = lax.broadcasted_iota(jnp.int32, (tm, 512), 0)
    cu_m1 = jnp.where(row == 0, cu_prev, pltpu.roll(cu, 1, 0))
    cu_p1 = jnp.where(row == tm - 1, cu_next, pltpu.roll(cu, tm - 1, 0))
    y = cu_m1 * w[0:1] + cu * w[1:2] + cu_p1 * w[2:3]
    sig = _sigmoid(z)
    return b, c, u, z, cu, cu_m1, cu_p1, y, sig, row


def _conv_fwd(pa, w_conv):
    s = pa.shape[0]
    tm = min(512, s)
    nt = s // tm

    def body(pa_ref, pp_ref, pn_ref, w_ref, ya_ref):
        i = pl.program_id(0)
        prev_row = pp_ref[...].astype(F32)[15:16, :]
        next_row = pn_ref[...].astype(F32)[0:1, :]
        b, _, _, z, _, _, _, y, sig, _ = _conv_common(
            pa_ref[...].astype(F32), prev_row, next_row, w_ref[...], i == 0, i == nt - 1, tm)
        ya_ref[...] = (b * y * (z * sig)).astype(BF16)

    prev, nxt = _halo_specs(s, tm, 16, 2048)
    return pl.pallas_call(
        body, name="conv_fwd", grid=(nt,),
        out_shape=jax.ShapeDtypeStruct((s, 512), BF16),
        in_specs=[_rows(tm, 2048), prev, nxt, _full((3, 512))],
        out_specs=_rows(tm, 512),
        compiler_params=_params(48),
    )(pa, pa, pa, w_conv)


def _heads_to_lanes(a, g, row):
    low = row < HEAD_DIM
    parts = []
    for b in (2 * g, 2 * g + 1):
        t = jnp.transpose(a[:, 128 * b:128 * b + 128])
        swapped = pltpu.roll(t, HEAD_DIM, 0)
        if g == 0:
            parts += [jnp.where(low, t, 0.0), jnp.where(low, swapped, 0.0)]
        else:
            parts += [jnp.where(low, 0.0, swapped), jnp.where(low, 0.0, t)]
    return jnp.concatenate(parts, axis=1)


def _lanes_to_heads(t0, t1, row):
    low = row < HEAD_DIM
    blocks = []
    for b in range(4):
        g = b // 2
        tg = (t0, t1)[g]
        je = 2 * (b - 2 * g)
        even, odd = tg[:, 128 * je:128 * je + 128], tg[:, 128 * je + 128:128 * je + 256]
        if g == 0:
            t = jnp.where(low, even, pltpu.roll(odd, HEAD_DIM, 0))
        else:
            t = jnp.where(low, pltpu.roll(even, HEAD_DIM, 0), odd)
        blocks.append(jnp.transpose(t))
    return jnp.concatenate(blocks, axis=1)


WINDOW_KEYS = 3 * ATTN_BLOCK
STACKED = 4 * ATTN_BLOCK
KEY_CHUNK = 32


def _fill_band_bias(bias, nb):
    assert nb >= 2
    c = lax.broadcasted_iota(jnp.int32, (WINDOW_KEYS, STACKED), 0)
    r = lax.broadcasted_iota(jnp.int32, (WINDOW_KEYS, STACKED), 1) & (ATTN_BLOCK - 1)
    band = (c >= r) & (c <= r + 2 * ATTN_BLOCK)
    for v, ok in enumerate((band, band & (c >= ATTN_BLOCK), band & (c < 2 * ATTN_BLOCK))):
        bias[v] = jnp.where(ok, 0.0, -jnp.inf)


def _bias_variant(n, nb):
    return jnp.where(n == 0, 1, jnp.where(n == nb - 1, 2, 0))


def _sink_row(sink_ref, g):
    return jnp.concatenate([jnp.full((1, ATTN_BLOCK), sink_ref[4 * g + j], F32) for j in range(4)], axis=1)


def _softmax_keys_major(sc, bias, variant, sink, e_scr):
    chunks = [pl.ds(k * KEY_CHUNK, KEY_CHUNK) for k in range(WINDOW_KEYS // KEY_CHUNK)]
    rows = [slice(k * KEY_CHUNK, (k + 1) * KEY_CHUNK) for k in range(WINDOW_KEYS // KEY_CHUNK)]
    m_run = jnp.full((KEY_CHUNK, STACKED), -jnp.inf, F32)
    for ck, rw in zip(chunks, rows):
        m_run = jnp.maximum(m_run, sc[rw] * ATTN_SCALE + bias[variant, ck, :])
    m = jnp.maximum(jnp.max(m_run, axis=0, keepdims=True), sink)
    l_run = jnp.zeros((KEY_CHUNK, STACKED), F32)
    for ck, rw in zip(chunks, rows):
        e = jnp.exp(sc[rw] * ATTN_SCALE + bias[variant, ck, :] - m)
        l_run += e
        e_scr[rw, :] = e.astype(BF16)
    es = jnp.exp(sink - m)
    inv = 1.0 / (jnp.sum(l_run, axis=0, keepdims=True) + es)
    return inv, es * inv


def _fill_padded(kv_ref, kpad, vpad, s):
    zero = jnp.zeros((ATTN_BLOCK, 128), BF16)
    kpad[0:ATTN_BLOCK, :] = zero
    vpad[0:ATTN_BLOCK, :] = zero
    kpad[ATTN_BLOCK + s:2 * ATTN_BLOCK + s, :] = zero
    vpad[ATTN_BLOCK + s:2 * ATTN_BLOCK + s, :] = zero
    kpad[ATTN_BLOCK:ATTN_BLOCK + s, :] = kv_ref[:, 0:128]
    vpad[ATTN_BLOCK:ATTN_BLOCK + s, :] = kv_ref[:, 128:256]


def _attn_fwd(pq, pkv, pbz, sink):
    s = pq.shape[0]
    nb = s // ATTN_BLOCK

    def body(sink_ref, q_ref, z_ref, kv_ref, yb_ref, kpad, vpad, bias, e_scr):
        n = pl.program_id(0)

        @pl.when(n == 0)
        def _():
            _fill_padded(kv_ref, kpad, vpad, s)
            _fill_band_bias(bias, nb)

        row = lax.broadcasted_iota(jnp.int32, (ATTN_BLOCK, 128), 0)
        start = pl.multiple_of(n * ATTN_BLOCK, ATTN_BLOCK)
        kw, vw = kpad[pl.ds(start, WINDOW_KEYS), :], vpad[pl.ds(start, WINDOW_KEYS), :]
        qf = q_ref[...].astype(F32)
        variant = _bias_variant(n, nb)
        outs = []
        for g in range(2):
            qt = _heads_to_lanes(qf, g, row).astype(BF16)
            inv, _ = _softmax_keys_major(_dot(kw, qt), bias, variant, _sink_row(sink_ref, g), e_scr)
            outs.append(_dot_tn(vw, e_scr[...]) * inv)
        attn = _lanes_to_heads(outs[0], outs[1], row)
        z = z_ref[...].astype(F32)
        yb_ref[...] = (attn * (z * _sigmoid(z))).astype(BF16)

    return pl.pallas_call(
        body, name="attn_fwd", grid=(nb,),
        out_shape=jax.ShapeDtypeStruct((s, 512), BF16),
        in_specs=[pl.BlockSpec(memory_space=pltpu.SMEM), _rows(ATTN_BLOCK, 512), _rows(ATTN_BLOCK, 512),
                  _full((s, 256))],
        out_specs=_rows(ATTN_BLOCK, 512),
        scratch_shapes=[pltpu.VMEM((s + 2 * ATTN_BLOCK, 128), BF16)] * 2
        + [pltpu.VMEM((3, WINDOW_KEYS, STACKED), F32), pltpu.VMEM((WINDOW_KEYS, STACKED), BF16)],
        compiler_params=_params(32),
    )(sink, pq, pbz, pkv)


def _mem_softmax_t(q, mk):
    sc = _dot_nt(mk, q) * MEM_SCALE
    e = jnp.exp(sc - jnp.max(sc, axis=0, keepdims=True))
    return e * (1.0 / jnp.sum(e, axis=0, keepdims=True))


def _mem_attn_fwd(pmq, pmz, mkv):
    s = pmq.shape[0]
    m = mkv.shape[0]
    tm = min(512, s)

    def body(q_ref, z_ref, mk_ref, mv_ref, ym_ref):
        z = z_ref[...].astype(F32)
        sz = z * _sigmoid(z)
        for h in range(MEM_HEADS):
            cols = slice(128 * h, 128 * h + 128)
            pt = _mem_softmax_t(q_ref[:, cols], mk_ref[:, cols])
            o = _dot_tn(pt.astype(BF16), mv_ref[:, cols])
            ym_ref[:, cols] = (o * sz[:, cols]).astype(BF16)

    return pl.pallas_call(
        body, name="mem_attn_fwd", grid=(s // tm,),
        out_shape=jax.ShapeDtypeStruct((s, 512), BF16),
        in_specs=[_rows(tm, 512), _rows(tm, 512), pl.BlockSpec((m, 512), lambda i: (0, 0)),
                  pl.BlockSpec((m, 512), lambda i: (0, 1))],
        out_specs=_rows(tm, 512),
        compiler_params=_params(32),
    )(pmq, pmz, mkv, mkv)


def _mid(ya, yb, ym, pg, x, target, g_post, w_up, w_out):
    s = x.shape[0]
    tm = min(256, s)
    nt = s // tm

    def body(ya_ref, yb_ref, ym_ref, pg_ref, x_ref, t_ref, gp_ref, wup_hbm, wout_hbm,
             dg_ref, dya_ref, dyb_ref, dym_ref, dy_ref, loss_ref, ggp_ref, mb_ref, dob_ref, du_ref,
             wup_vm, wout_vm, sems):
        i = pl.program_id(0)
        _load_once([(wup_hbm.at[d], wup_vm.at[:, pl.ds(128 * d, 128)]) for d in range(N_DEV)]
                   + [(wout_hbm, wout_vm)], sems)

        @pl.when(i == 0)
        def _():
            loss_ref[...] = jnp.zeros_like(loss_ref)
            ggp_ref[...] = jnp.zeros_like(ggp_ref)

        ys = (ya_ref[...], yb_ref[...], ym_ref[...])
        us = [_dot(ys[k], wup_vm[512 * k:512 * k + 512, :]) for k in range(3)]
        gates = [_sigmoid(pg_ref[:, 1024 * k:1024 * k + 1024].astype(F32)) for k in range(3)]
        merged = gates[0] * us[0] + gates[1] * us[1] + gates[2] * us[2]
        mb = merged.astype(BF16)
        mb_ref[...] = mb
        out = _dot(mb, wout_vm[...])
        r = lax.rsqrt(jnp.mean(out * out, axis=-1, keepdims=True) + EPS)
        on = out * r
        gp = gp_ref[...]
        err = (x_ref[...] + on * gp) - t_ref[...]
        loss_ref[...] += 0.5 * jnp.sum(err * err) * (1.0 / D_MODEL)
        dy = err * (1.0 / D_MODEL)
        dy_ref[...] = dy
        ggp_ref[...] += jnp.sum(dy * on, axis=0, keepdims=True)
        a = dy * gp
        d_out = r * (a - on * jnp.mean(a * on, axis=-1, keepdims=True))
        dob = d_out.astype(BF16)
        dob_ref[...] = dob
        d_merged = _dot_nt(dob, wout_vm[...])
        d_refs = (dya_ref, dyb_ref, dym_ref)
        for k in range(3):
            g = gates[k]
            dg_ref[:, 1024 * k:1024 * k + 1024] = (d_merged * us[k] * g * (1.0 - g)).astype(BF16)
            du = (d_merged * g).astype(BF16)
            du_ref[k] = du
            d_refs[k][...] = _dot_nt(du, wup_vm[512 * k:512 * k + 512, :])

    return pl.pallas_call(
        body, name="mid", grid=(nt,),
        out_shape=[jax.ShapeDtypeStruct((s, 3072), BF16)] + [jax.ShapeDtypeStruct((s, 512), F32)] * 3
        + [jax.ShapeDtypeStruct((s, D_MODEL), F32), jax.ShapeDtypeStruct((8, 128), F32),
           jax.ShapeDtypeStruct((1, D_MODEL), F32), jax.ShapeDtypeStruct((s, D_MODEL), BF16),
           jax.ShapeDtypeStruct((s, D_MODEL), BF16), jax.ShapeDtypeStruct((3, s, D_MODEL), BF16)],
        in_specs=[_rows(tm, 512)] * 3 + [_rows(tm, 3072), _rows(tm, D_MODEL), _rows(tm, D_MODEL),
                                         _full((1, D_MODEL)), ANY, ANY],
        out_specs=[_rows(tm, 3072)] + [_rows(tm, 512)] * 3
        + [_rows(tm, D_MODEL), _full((8, 128)), _full((1, D_MODEL)), _rows(tm, D_MODEL), _rows(tm, D_MODEL),
           pl.BlockSpec((3, tm, D_MODEL), lambda i: (0, i, 0))],
        scratch_shapes=[pltpu.VMEM((1536, D_MODEL), BF16), pltpu.VMEM((D_MODEL, D_MODEL), BF16),
                        pltpu.SemaphoreType.DMA((N_DEV + 1,))],
        compiler_params=_params(56),
    )(ya, yb, ym, pg, x, target, g_post, w_up, w_out)


def _gw_mid(mb, dob, ys, du):
    s = mb.shape[0]
    tn = 256

    def out_body(mb_ref, dob_ref, o_ref):
        o_ref[...] = _dot_tn(mb_ref[...], dob_ref[...]).astype(BF16)

    gw_out = pl.pallas_call(
        out_body, name="gw_out", grid=(D_MODEL // tn,),
        out_shape=jax.ShapeDtypeStruct((D_MODEL, D_MODEL), BF16),
        in_specs=[pl.BlockSpec((s, tn), lambda j: (0, j)), _full((s, D_MODEL))],
        out_specs=pl.BlockSpec((tn, D_MODEL), lambda j: (j, 0)),
        compiler_params=_params(48),
    )(mb, dob)

    per = 512 // tn

    def up_body(ya_ref, yb_ref, ym_ref, du_ref, o_ref):
        j = pl.program_id(0)
        for k, y_ref in enumerate((ya_ref, yb_ref, ym_ref)):
            @pl.when(j // per == k)
            def _(y_ref=y_ref):
                res = _dot_tn(y_ref[...], du_ref[...])
                for d in range(N_DEV):
                    o_ref[d] = res[:, 128 * d:128 * d + 128].astype(BF16)

    def y_spec(k):
        return pl.BlockSpec((s, tn), lambda j: (0, jnp.clip(j - per * k, 0, per - 1)))

    gw_up = pl.pallas_call(
        up_body, name="gw_up", grid=(3 * per,),
        out_shape=jax.ShapeDtypeStruct((N_DEV, 1536, 128), BF16),
        in_specs=[y_spec(0), y_spec(1), y_spec(2), pl.BlockSpec((None, s, D_MODEL), lambda j: (j // per, 0, 0))],
        out_specs=pl.BlockSpec((N_DEV, tn, 128), lambda j: (0, j, 0)),
        compiler_params=_params(48),
    )(*ys, du)
    return gw_out, gw_up


def _conv_bwd(pa, dya, w_conv):
    s = pa.shape[0]
    tm = min(512, s)
    nt = s // tm

    def body(pa_ref, pp_ref, pn_ref, d_ref, dp_ref, dn_ref, w_ref, da_ref, gw_ref):
        i = pl.program_id(0)
        first, last = i == 0, i == nt - 1

        @pl.when(first)
        def _():
            gw_ref[...] = jnp.zeros_like(gw_ref)

        w = w_ref[...]
        prev_row = pp_ref[...].astype(F32)[15:16, :]
        next_row = pn_ref[...].astype(F32)[0:1, :]
        b, c, u, z, cu, cu_m1, cu_p1, y, sig, row = _conv_common(
            pa_ref[...].astype(F32), prev_row, next_row, w, first, last, tm)
        sz = z * sig
        dya_t = d_ref[...]
        d_y = dya_t * b * sz

        def halo_dy(p_row, d_row):
            zz = p_row[:, 1536:2048]
            return d_row * p_row[:, 0:512] * (zz * _sigmoid(zz))

        dy_prev = jnp.where(first, 0.0, halo_dy(prev_row, dp_ref[7:8, :]))
        dy_next = jnp.where(last, 0.0, halo_dy(next_row, dn_ref[0:1, :]))
        dy_m1 = jnp.where(row == 0, dy_prev, pltpu.roll(d_y, 1, 0))
        dy_p1 = jnp.where(row == tm - 1, dy_next, pltpu.roll(d_y, tm - 1, 0))
        d_cu = dy_p1 * w[0:1] + d_y * w[1:2] + dy_m1 * w[2:3]
        da_ref[:, 0:512] = (dya_t * y * sz).astype(BF16)
        da_ref[:, 512:1024] = (d_cu * u).astype(BF16)
        da_ref[:, 1024:1536] = (d_cu * c).astype(BF16)
        da_ref[:, 1536:2048] = (dya_t * b * y * (sig * (1.0 + z * (1.0 - sig)))).astype(BF16)
        gw_ref[0:1, :] += jnp.sum(d_y * cu_m1, axis=0, keepdims=True)
        gw_ref[1:2, :] += jnp.sum(d_y * cu, axis=0, keepdims=True)
        gw_ref[2:3, :] += jnp.sum(d_y * cu_p1, axis=0, keepdims=True)

    prev, nxt = _halo_specs(s, tm, 16, 2048)
    dprev, dnxt = _halo_specs(s, tm, 8, 512)
    return pl.pallas_call(
        body, name="conv_bwd", grid=(nt,),
        out_shape=[jax.ShapeDtypeStruct((s, 2048), BF16), jax.ShapeDtypeStruct((8, 512), F32)],
        in_specs=[_rows(tm, 2048), prev, nxt, _rows(tm, 512), dprev, dnxt, _full((3, 512))],
        out_specs=[_rows(tm, 2048), _full((8, 512))],
        compiler_params=_params(48),
    )(pa, pa, pa, dya, dya, dya, w_conv)


def _attn_bwd(pq, pkv, pbz, dyb, sink, tabs):
    s = pq.shape[0]
    nb = s // ATTN_BLOCK

    def body(sink_ref, q_ref, z_ref, d_ref, cs_ref, s1_ref, s2_ref, kv_ref, csf_ref, s1f_ref, s2f_ref,
             dq_ref, dz_ref, dkv_ref, gs_ref, kpad, vpad, dk_acc, dv_acc, bias, e_scr, ds_scr):
        n = pl.program_id(0)

        @pl.when(n == 0)
        def _():
            _fill_padded(kv_ref, kpad, vpad, s)
            _fill_band_bias(bias, nb)
            dk_acc[...] = jnp.zeros_like(dk_acc)
            dv_acc[...] = jnp.zeros_like(dv_acc)
            gs_ref[...] = jnp.zeros_like(gs_ref)

        row = lax.broadcasted_iota(jnp.int32, (ATTN_BLOCK, 128), 0)
        start = pl.multiple_of(n * ATTN_BLOCK, ATTN_BLOCK)
        kw, vw = kpad[pl.ds(start, WINDOW_KEYS), :], vpad[pl.ds(start, WINDOW_KEYS), :]
        qf = q_ref[...].astype(F32)
        variant = _bias_variant(n, nb)
        z = z_ref[...].astype(F32)
        sig = _sigmoid(z)
        dyb_t = d_ref[...]
        d_attn = dyb_t * (z * sig)
        outs, dqs = [], []
        dk_w = jnp.zeros((WINDOW_KEYS, 128), F32)
        dv_w = jnp.zeros((WINDOW_KEYS, 128), F32)
        for g in range(2):
            qt = _heads_to_lanes(qf, g, row)
            inv, p_sink = _softmax_keys_major(
                _dot(kw, qt.astype(BF16)), bias, variant, _sink_row(sink_ref, g), e_scr)
            ot = _dot_tn(vw, e_scr[...]) * inv
            outs.append(ot)
            dot_ = _heads_to_lanes(d_attn, g, row)
            delta = jnp.sum(dot_ * ot, axis=0, keepdims=True)
            dpt = _dot(vw, dot_.astype(BF16))
            for k in range(WINDOW_KEYS // KEY_CHUNK):
                rw = slice(k * KEY_CHUNK, (k + 1) * KEY_CHUNK)
                ds_scr[rw, :] = (e_scr[rw, :].astype(F32) * (dpt[rw] - delta)).astype(BF16)
            sink_part = p_sink * delta
            for j in range(4):
                h = 4 * g + j
                gs_ref[h:h + 1, :] -= jnp.sum(sink_part[:, 128 * j:128 * j + 128])
            dqs.append(_dot_tn(kw, ds_scr[...]) * (inv * ATTN_SCALE))
            dk_w += _dot_nt(ds_scr[...], (qt * inv).astype(BF16)) * ATTN_SCALE
            dv_w += _dot_nt(e_scr[...], (dot_ * inv).astype(BF16))
        dk_acc[pl.ds(start, WINDOW_KEYS), :] += dk_w
        dv_acc[pl.ds(start, WINDOW_KEYS), :] += dv_w
        attn = _lanes_to_heads(outs[0], outs[1], row)
        dz_ref[...] = (dyb_t * attn * (sig * (1.0 + z * (1.0 - sig)))).astype(BF16)
        dq = _lanes_to_heads(dqs[0], dqs[1], row)
        cs, s1, s2 = cs_ref[...], s1_ref[...], s2_ref[...]
        for b in range(4):
            dq_ref[:, 128 * b:128 * b + 128] = _rope_t(dq[:, 128 * b:128 * b + 128], cs, s1, s2).astype(BF16)

        @pl.when(n == nb - 1)
        def _():
            dk = dk_acc[ATTN_BLOCK:ATTN_BLOCK + s, :]
            dkv_ref[:, 0:128] = _rope_t(dk, csf_ref[...], s1f_ref[...], s2f_ref[...]).astype(BF16)
            dkv_ref[:, 128:256] = dv_acc[ATTN_BLOCK:ATTN_BLOCK + s, :].astype(BF16)

    tile = _rows(ATTN_BLOCK, 512)
    tab = _rows(ATTN_BLOCK, 128)
    return pl.pallas_call(
        body, name="attn_bwd", grid=(nb,),
        out_shape=[jax.ShapeDtypeStruct((s, 512), BF16), jax.ShapeDtypeStruct((s, 512), BF16),
                   jax.ShapeDtypeStruct((s, 256), BF16), jax.ShapeDtypeStruct((8, 128), F32)],
        in_specs=[pl.BlockSpec(memory_space=pltpu.SMEM), tile, tile, tile, tab, tab, tab,
                  _full((s, 256)), _full((s, 128)), _full((s, 128)), _full((s, 128))],
        out_specs=[tile, tile, _full((s, 256)), _full((8, 128))],
        scratch_shapes=[pltpu.VMEM((s + 2 * ATTN_BLOCK, 128), BF16)] * 2
        + [pltpu.VMEM((s + 2 * ATTN_BLOCK, 128), F32)] * 2
        + [pltpu.VMEM((3, WINDOW_KEYS, STACKED), F32)] + [pltpu.VMEM((WINDOW_KEYS, STACKED), BF16)] * 2,
        compiler_params=_params(48),
    )(sink, pq, pbz, dyb, *tabs, pkv, *tabs)


def _mem_attn_bwd(pmq, pmz, mkv, dym):
    s = pmq.shape[0]
    m = mkv.shape[0]
    tm = min(512, s)

    def body(q_ref, z_ref, d_ref, mk_ref, mv_ref, dq_ref, dz_ref, dmkv_ref):
        @pl.when(pl.program_id(0) == 0)
        def _():
            dmkv_ref[...] = jnp.zeros_like(dmkv_ref)

        z = z_ref[...].astype(F32)
        sig = _sigmoid(z)
        dym_t = d_ref[...]
        d_attn = dym_t * (z * sig)
        dsilu = sig * (1.0 + z * (1.0 - sig))
        for h in range(MEM_HEADS):
            cols = slice(128 * h, 128 * h + 128)
            q, mk, mv = q_ref[:, cols], mk_ref[:, cols], mv_ref[:, cols]
            pt = _mem_softmax_t(q, mk)
            pb = pt.astype(BF16)
            o = _dot_tn(pb, mv)
            dob = d_attn[:, cols].astype(BF16)
            dpt = _dot_nt(mv, dob)
            dst = (pt * (dpt - jnp.sum(pt * dpt, axis=0, keepdims=True))).astype(BF16)
            dq_ref[:, cols] = (_dot_tn(dst, mk) * MEM_SCALE).astype(BF16)
            dz_ref[:, cols] = (dym_t[:, cols] * o * dsilu[:, cols]).astype(BF16)
            dmkv_ref[:, cols] += _dot(dst, q) * MEM_SCALE
            dmkv_ref[:, 512 + 128 * h:512 + 128 * h + 128] += _dot(pb, dob)

    return pl.pallas_call(
        body, name="mem_attn_bwd", grid=(s // tm,),
        out_shape=[jax.ShapeDtypeStruct((s, 512), BF16), jax.ShapeDtypeStruct((s, 512), BF16),
                   jax.ShapeDtypeStruct((m, D_MODEL), F32)],
        in_specs=[_rows(tm, 512), _rows(tm, 512), _rows(tm, 512), pl.BlockSpec((m, 512), lambda i: (0, 0)),
                  pl.BlockSpec((m, 512), lambda i: (0, 1))],
        out_specs=[_rows(tm, 512), _rows(tm, 512), _full((m, D_MODEL))],
        compiler_params=_params(32),
    )(pmq, pmz, dym, mkv, mkv)


def _mem_kv_bwd(mem, g_mem, mn, dmkv, w_mkv):
    m = mem.shape[0]

    def body(mem_ref, g_ref, mn_ref, d_ref, w_ref, gw_ref, gg_ref):
        db = d_ref[...].astype(BF16)
        gw_ref[...] = _dot_tn(mn_ref[...], db).astype(BF16)
        d_mn = _dot_nt(db, w_ref[...])
        xf = mem_ref[...]
        r = lax.rsqrt(jnp.mean(xf * xf, axis=-1, keepdims=True) + EPS)
        gg_ref[...] = jnp.sum(d_mn * (xf * r), axis=0, keepdims=True)

    return pl.pallas_call(
        body, name="mem_kv_bwd", grid=(1,),
        out_shape=[jax.ShapeDtypeStruct((D_MODEL, D_MODEL), BF16), jax.ShapeDtypeStruct((1, D_MODEL), F32)],
        in_specs=[_full((m, D_MODEL)), _full((1, D_MODEL)), _full((m, D_MODEL)), _full((m, D_MODEL)),
                  _full((D_MODEL, D_MODEL))],
        out_specs=[_full((D_MODEL, D_MODEL)), _full((1, D_MODEL))],
        compiler_params=_params(32),
    )(mem, g_mem, mn, dmkv, w_mkv)


def _dh_bwd(dparts, x, dy, g_pre, w_int):
    s = x.shape[0]
    tm = min(256, s)

    def body(*refs):
        d_refs = refs[:7]
        x_ref, dy_ref, g_ref, w_hbm, gx_ref, gg_ref, w_vm, sems = refs[7:]
        _load_once([(w_hbm, w_vm)], sems)

        @pl.when(pl.program_id(0) == 0)
        def _():
            gg_ref[...] = jnp.zeros_like(gg_ref)

        d_h = jnp.zeros((tm, D_MODEL), F32)
        for d_ref, (r0, width) in zip(d_refs, SEGS):
            for c0 in range(0, width, 512):
                cw = min(512, width - c0)
                d_h += _dot(d_ref[:, c0:c0 + cw], w_vm[r0 + c0:r0 + c0 + cw, :])
        xf = x_ref[...]
        r = lax.rsqrt(jnp.mean(xf * xf, axis=-1, keepdims=True) + EPS)
        xn = xf * r
        a = d_h * g_ref[...]
        gx_ref[...] = r * (a - xn * jnp.mean(a * xn, axis=-1, keepdims=True)) + dy_ref[...]
        gg_ref[...] += jnp.sum(d_h * xn, axis=0, keepdims=True)

    return pl.pallas_call(
        body, name="dh_bwd", grid=(s // tm,),
        out_shape=[jax.ShapeDtypeStruct((s, D_MODEL), F32), jax.ShapeDtypeStruct((1, D_MODEL), F32)],
        in_specs=[_rows(tm, w) for _, w in SEGS] + [_rows(tm, D_MODEL), _rows(tm, D_MODEL), _full((1, D_MODEL)), ANY],
        out_specs=[_rows(tm, D_MODEL), _full((1, D_MODEL))],
        scratch_shapes=[pltpu.VMEM((IN_WIDTH, D_MODEL), BF16), pltpu.SemaphoreType.DMA((1,))],
        compiler_params=_params(52),
    )(*dparts, x, dy, g_pre, w_int)


def _gw_in(dparts, h):
    s = h.shape[0]
    tn = 256
    starts, counts = [], []
    for r0, width in SEGS:
        starts.append(r0 // tn)
        counts.append(width // tn)

    def body(*refs):
        d_refs = refs[:7]
        h_hbm, o_ref, h_vm, sems = refs[7:]
        _load_once([(h_hbm, h_vm)], sems)
        j = pl.program_id(0)
        for d_ref, st, cnt in zip(d_refs, starts, counts):
            @pl.when((j >= st) & (j < st + cnt))
            def _(d_ref=d_ref):
                o_ref[...] = _dot_tn(d_ref[...], h_vm[...]).astype(BF16)

    def seg_spec(st, cnt):
        return pl.BlockSpec((s, tn), lambda j: (0, jnp.clip(j - st, 0, cnt - 1)))

    return pl.pallas_call(
        body, name="gw_in", grid=(IN_WIDTH // tn,),
        out_shape=jax.ShapeDtypeStruct((IN_WIDTH, D_MODEL), BF16),
        in_specs=[seg_spec(st, cnt) for st, cnt in zip(starts, counts)] + [ANY],
        out_specs=pl.BlockSpec((tn, D_MODEL), lambda j: (j, 0)),
        scratch_shapes=[pltpu.VMEM((s, D_MODEL), BF16), pltpu.SemaphoreType.DMA((1,))],
        compiler_params=_params(52),
    )(*dparts, h)


def _adamw_math(w, g, m, v):
    m2 = ADAM_B1 * m + (1.0 - ADAM_B1) * g
    v2 = ADAM_B2 * v + (1.0 - ADAM_B2) * (g * g)
    m_hat = m2 / (1.0 - ADAM_B1 ** ADAM_STEP)
    v_hat = v2 / (1.0 - ADAM_B2 ** ADAM_STEP)
    delta = -ADAM_LR * (m_hat / (jnp.sqrt(v_hat) + ADAM_EPS) + ADAM_WD * w)
    return delta, m2, v2


def _sum_adamw(own, land, chip, block, w, m, v, name, tiles=1):
    r, c = w.shape
    rt = r // tiles

    def body(c_ref, own_ref, l1_ref, l2_ref, l3_ref, w_ref, m_ref, v_ref, g_ref, d_ref, m2_ref, v2_ref):
        g = own_ref[...].astype(F32)
        for l_ref in (l1_ref, l2_ref, l3_ref):
            g += l_ref[...].astype(F32)
        g_ref[...] = g
        d_ref[...], m2_ref[...], v2_ref[...] = _adamw_math(w_ref[...], g, m_ref[...], v_ref[...])

    def share(k):
        return pl.BlockSpec((None, rt, c), lambda i, c_ref: (jnp.bitwise_xor(c_ref[0], k), block * tiles + i, 0))

    spec = pl.BlockSpec((rt, c), lambda i, c_ref: (i, 0))
    grid_spec = pltpu.PrefetchScalarGridSpec(
        num_scalar_prefetch=1, grid=(tiles,),
        in_specs=[share(0), share(1), share(2), share(3)] + [spec] * 3, out_specs=[spec] * 4)
    return pl.pallas_call(
        body, name=name, grid_spec=grid_spec,
        out_shape=[jax.ShapeDtypeStruct((r, c), F32)] * 4,
        compiler_params=_params(48),
    )(chip, own, land, land, land, w, m, v)


def _pack_sum(packs):
    def body(p_ref, o_ref):
        acc = p_ref[0]
        for k in range(1, N_DEV):
            acc += p_ref[k]
        o_ref[...] = acc

    return pl.pallas_call(
        body, name="pack_sum", grid=(1,),
        out_shape=jax.ShapeDtypeStruct((8, D_MODEL), F32),
        in_specs=[_full((N_DEV, 8, D_MODEL))], out_specs=_full((8, D_MODEL)),
    )(packs)


def _small_adamw(ws, gs, ms, vs):
    k = len(ws)

    def body(*refs):
        w_refs, g_refs, m_refs, v_refs = (refs[j * k:(j + 1) * k] for j in range(4))
        outs = refs[4 * k:]
        for j in range(k):
            outs[j][...], outs[k + j][...], outs[2 * k + j][...] = _adamw_math(
                w_refs[j][...], g_refs[j][...], m_refs[j][...], v_refs[j][...])

    specs = [_full(w.shape) for w in ws]
    res = pl.pallas_call(
        body, name="small_adamw", grid=(1,),
        out_shape=[jax.ShapeDtypeStruct(w.shape, F32) for w in ws] * 3,
        in_specs=specs * 4, out_specs=specs * 3,
    )(*ws, *gs, *ms, *vs)
    return res[:k], res[k:2 * k], res[2 * k:]


def kernel(x, mem, g_pre, w_in, w_conv, attn_sink, g_mem, w_mem_kv, w_up_a, w_up_b, w_up_m, w_out, g_post, loss_target, m_g_pre, m_w_in, m_w_conv, m_attn_sink, m_g_mem, m_w_mem_kv, m_w_up_a, m_w_up_b, m_w_up_m, m_w_out, m_g_post, v_g_pre, v_w_in, v_w_conv, v_attn_sink, v_g_mem, v_w_mem_kv, v_w_up_a, v_w_up_b, v_w_up_m, v_w_out, v_g_post):
    s = x.shape[1]
    x2, mem2, tgt2 = x[0], mem[0], loss_target[0]
    me = 4 * lax.axis_index("x") + 2 * lax.axis_index("y") + lax.axis_index("c")

    w_up_loc = jnp.concatenate([w_up_a[0], w_up_b[0], w_up_m[0]], axis=0).astype(BF16)
    w_conv_loc = jnp.zeros((8, 128), F32).at[:3, :64].set(w_conv[0])
    w_int_g, w_mkv_g, w_out_g, w_up_g, w_conv_g = _all_gather(
        [w_in[0].T.astype(BF16), w_mem_kv[0].astype(BF16), w_out[0].astype(BF16), w_up_loc, w_conv_loc],
        "gather_weights")
    w_int = w_int_g.reshape(IN_WIDTH, D_MODEL)
    w_mkv = w_mkv_g.reshape(D_MODEL, D_MODEL)
    w_out_f = w_out_g.reshape(D_MODEL, D_MODEL)
    w_conv_f = w_conv_g[:, :3, :64].transpose(1, 0, 2).reshape(3, 512)
    sink = attn_sink[0]
    tabs = _rope_tables(s)

    h, pa, pq, pkv, pbz, pmq, pmz, pg = _proj_fwd(x2, g_pre, w_int, tabs)
    mn, mkv = _mem_kv_fwd(mem2, g_mem, w_mkv)
    ya = _conv_fwd(pa, w_conv_f)
    yb = _attn_fwd(pq, pkv, pbz, sink)
    ym = _mem_attn_fwd(pmq, pmz, mkv)
    dg, dya, dyb, dym, dy, loss_p, gg_post, mb, dob, du = _mid(ya, yb, ym, pg, x2, tgt2, g_post, w_up_g, w_out_f)
    gw_out, gw_up = _gw_mid(mb, dob, (ya, yb, ym), du)

    core = lax.axis_index("c").astype(jnp.int32).reshape(1)
    chip = (2 * lax.axis_index("x") + lax.axis_index("y")).astype(jnp.int32).reshape(1)

    def exchange_start(shares, tag):
        from_sibling = _sibling_exchange(shares, "grads_to_sibling_" + tag)
        chip_shares = _pair_add(shares, from_sibling, core, "grads_pair_add_" + tag)
        return _chip_exchange_start(chip_shares, "grads_to_chips_start_" + tag)

    dmq, dmz, dmkv = _mem_attn_bwd(pmq, pmz, mkv, dym)
    gw_mkv, gg_mem = _mem_kv_bwd(mem2, g_mem, mn, dmkv, w_mkv)
    send1, recv1, srcs1, lands1, token1 = exchange_start(
        [gw_mkv.reshape(N_DEV, 128, D_MODEL), gw_out.reshape(N_DEV, 128, D_MODEL), gw_up], "small")
    da, gw_conv = _conv_bwd(pa, dya, w_conv_f + token1[0:1, 0:1])
    dq, dbz, dkv, g_sink = _attn_bwd(pq, pkv, pbz, dyb, sink, tabs)
    dparts = (da, dq, dkv, dbz, dmq, dmz, dg)
    gw_int = _gw_in(dparts, h)
    send2, recv2, srcs2, lands2, token2 = exchange_start([gw_int.reshape(N_DEV, SHARD_IN, D_MODEL)], "w_in")
    grad_x, gg_pre = _dh_bwd(dparts, x2, dy, g_pre + token2[0:1, 0:1], w_int)
    (o_mkv, o_out, o_up, o_int), (l_mkv, l_out, l_up, l_int) = _chip_exchange_wait(
        send1 + send2, recv1 + recv2, srcs1 + srcs2, lands1 + lands2, grad_x, "grads_to_chips_wait")

    row3 = jnp.concatenate([gw_conv[0:1], gw_conv[1:2]], axis=1)
    row4 = jnp.concatenate([gw_conv[2:3], g_sink[:, 0].reshape(1, 8), loss_p[0:1, 0:1],
                            jnp.zeros((1, 512 - 9), F32)], axis=1)
    pack = jnp.concatenate([gg_pre, gg_mem, gg_post, row3, row4, jnp.zeros((3, D_MODEL), F32)], axis=0)
    (packs,) = _all_gather([pack], "gather_small")
    tot = _pack_sum(packs)

    g_w_in, d_w_in, nm_w_in, nv_w_in = (t.T for t in _sum_adamw(
        o_int, l_int, chip, 0, w_in[0].T, m_w_in[0].T, v_w_in[0].T, "adamw_w_in", tiles=2))
    g_mkv, d_mkv, nm_mkv, nv_mkv = _sum_adamw(
        o_mkv, l_mkv, chip, 0, w_mem_kv[0], m_w_mem_kv[0], v_w_mem_kv[0], "adamw_w_mem_kv")
    g_out, d_out, nm_out, nv_out = _sum_adamw(o_out, l_out, chip, 0, w_out[0], m_w_out[0], v_w_out[0], "adamw_w_out")
    up = [_sum_adamw(o_up, l_up, chip, k, w[0], m[0], v[0], "adamw_w_up_" + "abm"[k])
          for k, (w, m, v) in enumerate([(w_up_a, m_w_up_a, v_w_up_a), (w_up_b, m_w_up_b, v_w_up_b),
                                         (w_up_m, m_w_up_m, v_w_up_m)])]

    g_g_pre, g_g_mem, g_g_post = tot[0:1], tot[1:2], tot[2:3]
    g_conv_full = jnp.concatenate([tot[3:4, 0:512], tot[3:4, 512:1024], tot[4:5, 0:512]], axis=0)
    g_conv = lax.dynamic_slice(g_conv_full, (0, 64 * me), (3, 64))
    g_sink_tot = tot[4:5, 512:520]
    loss = tot[4, 520]
    small_w = [g_pre, w_conv[0], attn_sink, g_mem, g_post]
    small_g = [g_g_pre, g_conv, g_sink_tot, g_g_mem, g_g_post]
    small_m = [m_g_pre, m_w_conv[0], m_attn_sink, m_g_mem, m_g_post]
    small_v = [v_g_pre, v_w_conv[0], v_attn_sink, v_g_mem, v_g_post]
    sd, sm, sv = _small_adamw(small_w, small_g, small_m, small_v)

    def lead(a):
        return a[None]

    grads = [g_g_pre, lead(g_w_in), lead(g_conv), g_sink_tot, g_g_mem, lead(g_mkv), lead(up[0][0]),
             lead(up[1][0]), lead(up[2][0]), lead(g_out), g_g_post]

    def assemble(small, big_in, big_mkv, big_up, big_out):
        return [small[0], lead(big_in), lead(small[1]), small[2], small[3], lead(big_mkv), lead(big_up[0]),
                lead(big_up[1]), lead(big_up[2]), lead(big_out), small[4]]

    deltas = assemble(sd, d_w_in, d_mkv, [u[1] for u in up], d_out)
    new_m = assemble(sm, nm_w_in, nm_mkv, [u[2] for u in up], nm_out)
    new_v = assemble(sv, nv_w_in, nv_mkv, [u[3] for u in up], nv_out)
    return (loss, grad_x[None], *grads, *deltas, *new_m, *new_v)
```

```python
import functools

import jax
import jax.numpy as jnp
from jax import lax
from jax.experimental import pallas as pl
from jax.experimental.pallas import tpu as pltpu

F32 = jnp.float32
BF16 = jnp.bfloat16
MESH = pl.DeviceIdType.MESH

N_DEV = 8
D_MODEL = 1024
EPS = 1e-6
ROPE_THETA = 500000.0
ROT_DIM = 16
HEAD_DIM = 64
ATTN_BLOCK = 128
MEM_HEADS = 4
MEM_HEAD_DIM = 128
ATTN_SCALE = HEAD_DIM ** -0.5
MEM_SCALE = MEM_HEAD_DIM ** -0.5

ADAM_LR = 0.001
ADAM_B1 = 0.9
ADAM_B2 = 0.999
ADAM_EPS = 1e-08
ADAM_WD = 0.01
ADAM_STEP = 10

SEG_A = (0, 2048)
SEG_BQ = (2048, 512)
SEG_BKV = (2560, 256)
SEG_BZ = (2816, 512)
SEG_MQ = (3328, 512)
SEG_MZ = (3840, 512)
SEG_G = (4352, 3072)
SEGS = (SEG_A, SEG_BQ, SEG_BKV, SEG_BZ, SEG_MQ, SEG_MZ, SEG_G)
IN_WIDTH = 7424
SHARD_IN = IN_WIDTH // N_DEV

V7X_VMEM_BYTES = 64 * 1024 * 1024
ANY = pl.BlockSpec(memory_space=pl.ANY)


def _params(vmem_mb):
    assert vmem_mb * 1024 * 1024 < V7X_VMEM_BYTES
    return pltpu.CompilerParams(dimension_semantics=("arbitrary",), vmem_limit_bytes=vmem_mb * 1024 * 1024)


def _pallas_call(body, **kwargs):
    kwargs["out_shape"] = jax.tree.map(lambda t: pltpu.HBM(t.shape, t.dtype), kwargs["out_shape"])
    call = pl.pallas_call(body, **kwargs)

    def run(*args):
        return call(*[pltpu.with_memory_space_constraint(a, pltpu.HBM) if a.ndim >= 2 else a for a in args])

    return run


def _full(shape):
    zeros = (0,) * len(shape)
    return pl.BlockSpec(shape, lambda i: zeros)


def _rows(tm, width):
    return pl.BlockSpec((tm, width), lambda i: (i, 0))


def _dot(a, b):
    return jnp.dot(a, b, preferred_element_type=F32)


def _dot_nt(a, b):
    return lax.dot_general(a, b, (((1,), (1,)), ((), ())), preferred_element_type=F32)


def _dot_tn(a, b):
    return lax.dot_general(a, b, (((0,), (0,)), ((), ())), preferred_element_type=F32)


def _sigmoid(z):
    return 1.0 / (1.0 + jnp.exp(-z))


def _rope(t, cs, s1, s2):
    return t * cs + pltpu.roll(t, 120, 1) * s1 + pltpu.roll(t, 8, 1) * s2


def _rope_t(d, cs, s1, s2):
    return d * cs + pltpu.roll(d * s1, 8, 1) + pltpu.roll(d * s2, 120, 1)


def _rope_tables(s):
    half = ROT_DIM // 2
    inv_freq = jnp.power(jnp.float32(ROPE_THETA), -jnp.arange(half, dtype=F32) * (2.0 / ROT_DIM))
    d = jnp.arange(128) % HEAD_DIM
    ang = jnp.arange(s).astype(F32)[:, None] * inv_freq[d % half][None, :]
    cos, sin = jnp.cos(ang), jnp.sin(ang)
    lo, hi = (d < half)[None, :], ((d >= half) & (d < ROT_DIM))[None, :]
    return jnp.where(lo | hi, cos, 1.0), jnp.where(lo, -sin, 0.0), jnp.where(hi, sin, 0.0)


def _load_once(pairs, sems):
    @pl.when(pl.program_id(0) == 0)
    def _():
        cps = [pltpu.make_async_copy(src, dst, sems.at[k]) for k, (src, dst) in enumerate(pairs)]
        for cp in cps:
            cp.start()
        for cp in cps:
            cp.wait()


def _my_place():
    x, y, c = lax.axis_index("x"), lax.axis_index("y"), lax.axis_index("c")
    return x, y, c


def _all_gather(arrs, name):
    n = len(arrs)

    def body(*refs):
        ins, outs = refs[:n], refs[n:2 * n]
        send_sems, recv_sems, local_sems = refs[2 * n:]
        x, y, c = _my_place()
        me, sibling = (x, y, c), (x, y, 1 - c)
        chips = [(1 - x, y), (x, 1 - y), (1 - x, 1 - y)]

        def idx(px, py, pc):
            return 4 * px + 2 * py + pc

        def copy(a, k, block, to, src=None):
            dst = outs[a].at[idx(*block)]
            return pltpu.make_async_remote_copy(
                src_ref=dst if src is None else src, dst_ref=dst,
                send_sem=send_sems.at[a * 7 + k], recv_sem=recv_sems.at[a * 7 + k],
                device_id=to, device_id_type=MESH)

        mine = [pltpu.make_async_copy(ins[a], outs[a].at[idx(*me)], local_sems.at[a]) for a in range(n)]
        for cp in mine:
            cp.start()
        first = []
        for a in range(n):
            first.append(copy(a, 0, me, sibling, src=ins[a]))
        for j, chip in enumerate(chips):
            for a in range(n):
                first.append(copy(a, 1 + j, me, (*chip, c), src=ins[a]))
        for cp in first:
            cp.start()
        passed = []
        for j, chip in enumerate(chips):
            for a in range(n):
                copy(a, 1 + j, (*chip, c), me).wait_recv()
                cp = copy(a, 4 + j, (*chip, c), sibling)
                cp.start()
                passed.append(cp)
        for a in range(n):
            copy(a, 0, sibling, me).wait_recv()
        for j, chip in enumerate(chips):
            for a in range(n):
                copy(a, 4 + j, (*chip, 1 - c), me).wait_recv()
        for cp in first + passed:
            cp.wait_send()
        for cp in mine:
            cp.wait()

    return _pallas_call(
        body, name=name,
        out_shape=[jax.ShapeDtypeStruct((N_DEV,) + a.shape, a.dtype) for a in arrs],
        in_specs=[ANY] * n, out_specs=[ANY] * n,
        scratch_shapes=[pltpu.SemaphoreType.DMA((7 * n,)), pltpu.SemaphoreType.DMA((7 * n,)),
                        pltpu.SemaphoreType.DMA((n,))],
    )(*arrs)


N_CHIPS = 4


def _sibling_exchange(arrs, name):
    n = len(arrs)

    def body(*refs):
        ins, outs = refs[:n], refs[n:2 * n]
        send_sems, recv_sems = refs[2 * n:]
        x, y, c = _my_place()
        sibling = (x, y, 1 - c)

        def copy(a, j):
            return pltpu.make_async_remote_copy(
                src_ref=ins[a].at[2 * j + (1 - c)], dst_ref=outs[a].at[j],
                send_sem=send_sems.at[a * N_CHIPS + j], recv_sem=recv_sems.at[a * N_CHIPS + j],
                device_id=sibling, device_id_type=MESH)

        cps = [copy(a, j) for j in range(N_CHIPS) for a in range(n)]
        for cp in cps:
            cp.start()
        for cp in cps:
            cp.wait_recv()
        for cp in cps:
            cp.wait_send()

    return _pallas_call(
        body, name=name,
        out_shape=[jax.ShapeDtypeStruct((N_CHIPS,) + a.shape[1:], a.dtype) for a in arrs],
        in_specs=[ANY] * n, out_specs=[ANY] * n,
        scratch_shapes=[pltpu.SemaphoreType.DMA((N_CHIPS * n,)), pltpu.SemaphoreType.DMA((N_CHIPS * n,))],
    )(*arrs)


def _pair_add(mine, recv, core, name):
    n = len(mine)

    def body(c_ref, *refs):
        for a in range(n):
            refs[2 * n + a][...] = (refs[a][...].astype(F32) + refs[n + a][...].astype(F32)).astype(BF16)

    def blk(a):
        return (None,) + a.shape[1:]

    grid_spec = pltpu.PrefetchScalarGridSpec(
        num_scalar_prefetch=1, grid=(N_CHIPS,),
        in_specs=[pl.BlockSpec(blk(a), lambda j, c_ref: (2 * j + c_ref[0], 0, 0)) for a in mine]
        + [pl.BlockSpec(blk(a), lambda j, c_ref: (j, 0, 0)) for a in recv],
        out_specs=[pl.BlockSpec(blk(a), lambda j, c_ref: (j, 0, 0)) for a in recv])
    return _pallas_call(
        body, name=name, grid_spec=grid_spec,
        out_shape=[jax.ShapeDtypeStruct(a.shape, BF16) for a in recv],
        compiler_params=_params(32),
    )(core, *mine, *recv)


HBM = pl.BlockSpec(memory_space=pltpu.HBM)
SEM = pl.BlockSpec(memory_space=pltpu.SEMAPHORE)
N_PEER_CHIPS = 3


def _chip_copies(srcs, lands, send_sems, recv_sems):
    x, y, c = _my_place()
    my_chip = 2 * x + y
    peers = [(x, 1 - y), (1 - x, y), (1 - x, 1 - y)]
    cps = []
    for k, (px, py) in enumerate(peers):
        for a in range(len(srcs)):
            j = a * N_PEER_CHIPS + k
            cps.append(pltpu.make_async_remote_copy(
                src_ref=srcs[a].at[2 * px + py], dst_ref=lands[a].at[my_chip],
                send_sem=send_sems[j], recv_sem=recv_sems[j],
                device_id=(px, py, c), device_id_type=MESH))
    return cps


def _chip_exchange_start(arrs, name):
    n = len(arrs)
    k = n * N_PEER_CHIPS

    def body(*refs):
        srcs, lands = refs[:n], refs[n:2 * n]
        send_sems, recv_sems = refs[2 * n:2 * n + k], refs[2 * n + k:2 * n + 2 * k]
        token = refs[-1]
        for cp in _chip_copies(srcs, lands, send_sems, recv_sems):
            cp.start()
        token[...] = jnp.zeros_like(token)

    hbm_arrs = [pltpu.with_memory_space_constraint(a, pltpu.HBM) for a in arrs]
    lands = [pltpu.with_memory_space_constraint(lax.empty(a.shape, a.dtype), pltpu.HBM) for a in arrs]
    res = pl.pallas_call(
        body, name=name,
        out_shape=[pltpu.SemaphoreType.DMA(())] * (2 * k) + [pltpu.HBM(a.shape, a.dtype) for a in arrs] * 2
        + [jax.ShapeDtypeStruct((8, 128), F32)],
        in_specs=[HBM] * (2 * n),
        out_specs=[SEM] * (2 * k) + [HBM] * (2 * n) + [pl.BlockSpec(memory_space=pltpu.VMEM)],
        input_output_aliases={a: 2 * k + a for a in range(2 * n)},
        compiler_params=pltpu.CompilerParams(has_side_effects=pltpu.SideEffectType.DATAFLOW_SIDE_EFFECTING),
    )(*hbm_arrs, *lands)
    return res[:k], res[k:2 * k], res[2 * k:2 * k + n], res[2 * k + n:2 * k + 2 * n], res[-1]


def _chip_exchange_wait(send_sems, recv_sems, srcs, lands, after, name):
    n = len(srcs)
    k = n * N_PEER_CHIPS

    def body(*refs):
        src_refs, land_refs = refs[:n], refs[n:2 * n]
        s_sems, r_sems = refs[2 * n:2 * n + k], refs[2 * n + k:2 * n + 2 * k]
        for cp in _chip_copies(src_refs, land_refs, s_sems, r_sems):
            cp.wait_send()
            cp.wait_recv()

    res = pl.pallas_call(
        body, name=name,
        out_shape=[pltpu.HBM(a.shape, a.dtype) for a in srcs] * 2,
        in_specs=[HBM] * (2 * n) + [SEM] * (2 * k) + [ANY],
        out_specs=[HBM] * (2 * n),
        input_output_aliases={a: a for a in range(2 * n)},
        compiler_params=pltpu.CompilerParams(has_side_effects=pltpu.SideEffectType.DATAFLOW_SIDE_EFFECTING),
    )(*srcs, *lands, *send_sems, *recv_sems, after)
    return res[:n], res[n:]


def _proj_fwd(x, g_pre, w_int, tabs):
    s = x.shape[0]
    tm = min(512, s)

    def body(x_ref, g_ref, cs_ref, s1_ref, s2_ref, w_hbm,
             h_ref, pa_ref, pq_ref, pkv_ref, pbz_ref, pmq_ref, pmz_ref, pg_ref, w_vm, sems):
        _load_once([(w_hbm, w_vm)], sems)
        xf = x_ref[...]
        r = lax.rsqrt(jnp.mean(xf * xf, axis=-1, keepdims=True) + EPS)
        h = ((xf * r) * g_ref[...]).astype(BF16)
        h_ref[...] = h
        cs, s1, s2 = cs_ref[...], s1_ref[...], s2_ref[...]

        def mm(seg, c0, width):
            return _dot_nt(h, w_vm[seg[0] + c0:seg[0] + c0 + width, :])

        for c0 in range(0, SEG_A[1], 512):
            pa_ref[:, c0:c0 + 512] = mm(SEG_A, c0, 512).astype(BF16)
        q = mm(SEG_BQ, 0, 512)
        for b in range(4):
            pq_ref[:, 128 * b:128 * b + 128] = _rope(q[:, 128 * b:128 * b + 128], cs, s1, s2).astype(BF16)
        kv = mm(SEG_BKV, 0, 256)
        pkv_ref[:, 0:128] = _rope(kv[:, 0:128], cs, s1, s2).astype(BF16)
        pkv_ref[:, 128:256] = kv[:, 128:256].astype(BF16)
        pbz_ref[...] = mm(SEG_BZ, 0, 512).astype(BF16)
        pmq_ref[...] = mm(SEG_MQ, 0, 512).astype(BF16)
        pmz_ref[...] = mm(SEG_MZ, 0, 512).astype(BF16)
        for c0 in range(0, SEG_G[1], 512):
            pg_ref[:, c0:c0 + 512] = mm(SEG_G, c0, 512).astype(BF16)

    widths = (D_MODEL, 2048, 512, 256, 512, 512, 512, 3072)
    return _pallas_call(
        body, name="proj_fwd", grid=(s // tm,),
        out_shape=[jax.ShapeDtypeStruct((s, w), BF16) for w in widths],
        in_specs=[_rows(tm, D_MODEL), _full((1, D_MODEL)), _rows(tm, 128), _rows(tm, 128), _rows(tm, 128), ANY],
        out_specs=[_rows(tm, w) for w in widths],
        scratch_shapes=[pltpu.VMEM((IN_WIDTH, D_MODEL), BF16), pltpu.SemaphoreType.DMA((1,))],
        compiler_params=_params(52),
    )(x, g_pre, *tabs, w_int)


def _mem_kv_fwd(mem, g_mem, w_mkv):
    m = mem.shape[0]

    def body(mem_ref, g_ref, w_ref, mn_ref, mkv_ref):
        xf = mem_ref[...]
        r = lax.rsqrt(jnp.mean(xf * xf, axis=-1, keepdims=True) + EPS)
        mn = ((xf * r) * g_ref[...]).astype(BF16)
        mn_ref[...] = mn
        mkv_ref[...] = _dot(mn, w_ref[...]).astype(BF16)

    return _pallas_call(
        body, name="mem_kv_fwd", grid=(1,),
        out_shape=[jax.ShapeDtypeStruct((m, D_MODEL), BF16)] * 2,
        in_specs=[_full((m, D_MODEL)), _full((1, D_MODEL)), _full((D_MODEL, D_MODEL))],
        out_specs=[_full((m, D_MODEL))] * 2,
        compiler_params=_params(32),
    )(mem, g_mem, w_mkv)


def _halo_specs(s, tm, rows, width):
    nblk = s // rows
    prev = pl.BlockSpec((rows, width), lambda i: (jnp.maximum(i * (tm // rows) - 1, 0), 0))
    nxt = pl.BlockSpec((rows, width), lambda i: (jnp.minimum((i + 1) * (tm // rows), nblk - 1), 0))
    return prev, nxt


def _conv_common(pa, prev_row, next_row, w, first, last, tm):
    b, c, u, z = (pa[:, 512 * k:512 * k + 512] for k in range(4))
    cu = c * u
    cu_prev = jnp.where(first, 0.0, prev_row[:, 512:1024] * prev_row[:, 1024:1536])
    cu_next = jnp.where(last, 0.0, next_row[:, 512:1024] * next_row[:, 1024:1536])
    row = lax.broadcasted_iota(jnp.int32, (tm, 512), 0)
    cu_m1 = jnp.where(row == 0, cu_prev, pltpu.roll(cu, 1, 0))
    cu_p1 = jnp.where(row == tm - 1, cu_next, pltpu.roll(cu, tm - 1, 0))
    y = cu_m1 * w[0:1] + cu * w[1:2] + cu_p1 * w[2:3]
    sig = _sigmoid(z)
    return b, c, u, z, cu, cu_m1, cu_p1, y, sig, row


def _conv_fwd(pa, w_conv):
    s = pa.shape[0]
    tm = min(512, s)
    nt = s // tm

    def body(pa_ref, pp_ref, pn_ref, w_ref, ya_ref):
        i = pl.program_id(0)
        prev_row = pp_ref[...].astype(F32)[15:16, :]
        next_row = pn_ref[...].astype(F32)[0:1, :]
        b, _, _, z, _, _, _, y, sig, _ = _conv_common(
            pa_ref[...].astype(F32), prev_row, next_row, w_ref[...], i == 0, i == nt - 1, tm)
        ya_ref[...] = (b * y * (z * sig)).astype(BF16)

    prev, nxt = _halo_specs(s, tm, 16, 2048)
    return _pallas_call(
        body, name="conv_fwd", grid=(nt,),
        out_shape=jax.ShapeDtypeStruct((s, 512), BF16),
        in_specs=[_rows(tm, 2048), prev, nxt, _full((3, 512))],
        out_specs=_rows(tm, 512),
        compiler_params=_params(48),
    )(pa, pa, pa, w_conv)


def _heads_to_lanes(a, g, row):
    low = row < HEAD_DIM
    parts = []
    for b in (2 * g, 2 * g + 1):
        t = jnp.transpose(a[:, 128 * b:128 * b + 128])
        swapped = pltpu.roll(t, HEAD_DIM, 0)
        if g == 0:
            parts += [jnp.where(low, t, 0.0), jnp.where(low, swapped, 0.0)]
        else:
            parts += [jnp.where(low, 0.0, swapped), jnp.where(low, 0.0, t)]
    return jnp.concatenate(parts, axis=1)


def _lanes_to_heads(t0, t1, row):
    low = row < HEAD_DIM
    blocks = []
    for b in range(4):
        g = b // 2
        tg = (t0, t1)[g]
        je = 2 * (b - 2 * g)
        even, odd = tg[:, 128 * je:128 * je + 128], tg[:, 128 * je + 128:128 * je + 256]
        if g == 0:
            t = jnp.where(low, even, pltpu.roll(odd, HEAD_DIM, 0))
        else:
            t = jnp.where(low, pltpu.roll(even, HEAD_DIM, 0), odd)
        blocks.append(jnp.transpose(t))
    return jnp.concatenate(blocks, axis=1)


WINDOW_KEYS = 3 * ATTN_BLOCK
STACKED = 4 * ATTN_BLOCK
KEY_CHUNK = 32


def _fill_band_bias(bias, nb):
    assert nb >= 2
    c = lax.broadcasted_iota(jnp.int32, (WINDOW_KEYS, STACKED), 0)
    r = lax.broadcasted_iota(jnp.int32, (WINDOW_KEYS, STACKED), 1) & (ATTN_BLOCK - 1)
    band = (c >= r) & (c <= r + 2 * ATTN_BLOCK)
    for v, ok in enumerate((band, band & (c >= ATTN_BLOCK), band & (c < 2 * ATTN_BLOCK))):
        bias[v] = jnp.where(ok, 0.0, -jnp.inf)


def _bias_variant(n, nb):
    return jnp.where(n == 0, 1, jnp.where(n == nb - 1, 2, 0))


def _sink_row(sink_ref, g):
    return jnp.concatenate([jnp.full((1, ATTN_BLOCK), sink_ref[4 * g + j], F32) for j in range(4)], axis=1)


def _softmax_keys_major(sc, bias, variant, sink, e_scr):
    chunks = [pl.ds(k * KEY_CHUNK, KEY_CHUNK) for k in range(WINDOW_KEYS // KEY_CHUNK)]
    rows = [slice(k * KEY_CHUNK, (k + 1) * KEY_CHUNK) for k in range(WINDOW_KEYS // KEY_CHUNK)]
    m_run = jnp.full((KEY_CHUNK, STACKED), -jnp.inf, F32)
    for ck, rw in zip(chunks, rows):
        m_run = jnp.maximum(m_run, sc[rw] * ATTN_SCALE + bias[variant, ck, :])
    m = jnp.maximum(jnp.max(m_run, axis=0, keepdims=True), sink)
    l_run = jnp.zeros((KEY_CHUNK, STACKED), F32)
    for ck, rw in zip(chunks, rows):
        e = jnp.exp(sc[rw] * ATTN_SCALE + bias[variant, ck, :] - m)
        l_run += e
        e_scr[rw, :] = e.astype(BF16)
    es = jnp.exp(sink - m)
    inv = 1.0 / (jnp.sum(l_run, axis=0, keepdims=True) + es)
    return inv, es * inv


def _fill_padded(kv_ref, kpad, vpad, s):
    zero = jnp.zeros((ATTN_BLOCK, 128), BF16)
    kpad[0:ATTN_BLOCK, :] = zero
    vpad[0:ATTN_BLOCK, :] = zero
    kpad[ATTN_BLOCK + s:2 * ATTN_BLOCK + s, :] = zero
    vpad[ATTN_BLOCK + s:2 * ATTN_BLOCK + s, :] = zero
    kpad[ATTN_BLOCK:ATTN_BLOCK + s, :] = kv_ref[:, 0:128]
    vpad[ATTN_BLOCK:ATTN_BLOCK + s, :] = kv_ref[:, 128:256]


def _attn_fwd(pq, pkv, pbz, sink):
    s = pq.shape[0]
    nb = s // ATTN_BLOCK

    def body(sink_ref, q_ref, z_ref, kv_ref, yb_ref, kpad, vpad, bias, e_scr):
        n = pl.program_id(0)

        @pl.when(n == 0)
        def _():
            _fill_padded(kv_ref, kpad, vpad, s)
            _fill_band_bias(bias, nb)

        row = lax.broadcasted_iota(jnp.int32, (ATTN_BLOCK, 128), 0)
        start = pl.multiple_of(n * ATTN_BLOCK, ATTN_BLOCK)
        kw, vw = kpad[pl.ds(start, WINDOW_KEYS), :], vpad[pl.ds(start, WINDOW_KEYS), :]
        qf = q_ref[...].astype(F32)
        variant = _bias_variant(n, nb)
        outs = []
        for g in range(2):
            qt = _heads_to_lanes(qf, g, row).astype(BF16)
            inv, _ = _softmax_keys_major(_dot(kw, qt), bias, variant, _sink_row(sink_ref, g), e_scr)
            outs.append(_dot_tn(vw, e_scr[...]) * inv)
        attn = _lanes_to_heads(outs[0], outs[1], row)
        z = z_ref[...].astype(F32)
        yb_ref[...] = (attn * (z * _sigmoid(z))).astype(BF16)

    return _pallas_call(
        body, name="attn_fwd", grid=(nb,),
        out_shape=jax.ShapeDtypeStruct((s, 512), BF16),
        in_specs=[pl.BlockSpec(memory_space=pltpu.SMEM), _rows(ATTN_BLOCK, 512), _rows(ATTN_BLOCK, 512),
                  _full((s, 256))],
        out_specs=_rows(ATTN_BLOCK, 512),
        scratch_shapes=[pltpu.VMEM((s + 2 * ATTN_BLOCK, 128), BF16)] * 2
        + [pltpu.VMEM((3, WINDOW_KEYS, STACKED), F32), pltpu.VMEM((WINDOW_KEYS, STACKED), BF16)],
        compiler_params=_params(32),
    )(sink, pq, pbz, pkv)


def _mem_softmax_t(q, mk):
    sc = _dot_nt(mk, q) * MEM_SCALE
    e = jnp.exp(sc - jnp.max(sc, axis=0, keepdims=True))
    return e * (1.0 / jnp.sum(e, axis=0, keepdims=True))


def _mem_attn_fwd(pmq, pmz, mkv):
    s = pmq.shape[0]
    m = mkv.shape[0]
    tm = min(512, s)

    def body(q_ref, z_ref, mk_ref, mv_ref, ym_ref):
        z = z_ref[...].astype(F32)
        sz = z * _sigmoid(z)
        for h in range(MEM_HEADS):
            cols = slice(128 * h, 128 * h + 128)
            pt = _mem_softmax_t(q_ref[:, cols], mk_ref[:, cols])
            o = _dot_tn(pt.astype(BF16), mv_ref[:, cols])
            ym_ref[:, cols] = (o * sz[:, cols]).astype(BF16)

    return _pallas_call(
        body, name="mem_attn_fwd", grid=(s // tm,),
        out_shape=jax.ShapeDtypeStruct((s, 512), BF16),
        in_specs=[_rows(tm, 512), _rows(tm, 512), pl.BlockSpec((m, 512), lambda i: (0, 0)),
                  pl.BlockSpec((m, 512), lambda i: (0, 1))],
        out_specs=_rows(tm, 512),
        compiler_params=_params(32),
    )(pmq, pmz, mkv, mkv)


def _mid(ya, yb, ym, pg, x, target, g_post, w_up, w_out):
    s = x.shape[0]
    tm = min(256, s)
    nt = s // tm

    def body(ya_ref, yb_ref, ym_ref, pg_ref, x_ref, t_ref, gp_ref, wup_hbm, wout_hbm,
             dg_ref, dya_ref, dyb_ref, dym_ref, dy_ref, loss_ref, ggp_ref, mb_ref, dob_ref, du_ref,
             wup_vm, wout_vm, sems):
        i = pl.program_id(0)
        _load_once([(wup_hbm.at[d], wup_vm.at[:, pl.ds(128 * d, 128)]) for d in range(N_DEV)]
                   + [(wout_hbm, wout_vm)], sems)

        @pl.when(i == 0)
        def _():
            loss_ref[...] = jnp.zeros_like(loss_ref)
            ggp_ref[...] = jnp.zeros_like(ggp_ref)

        ys = (ya_ref[...], yb_ref[...], ym_ref[...])
        us = [_dot(ys[k], wup_vm[512 * k:512 * k + 512, :]) for k in range(3)]
        gates = [_sigmoid(pg_ref[:, 1024 * k:1024 * k + 1024].astype(F32)) for k in range(3)]
        merged = gates[0] * us[0] + gates[1] * us[1] + gates[2] * us[2]
        mb = merged.astype(BF16)
        mb_ref[...] = mb
        out = _dot(mb, wout_vm[...])
        r = lax.rsqrt(jnp.mean(out * out, axis=-1, keepdims=True) + EPS)
        on = out * r
        gp = gp_ref[...]
        err = (x_ref[...] + on * gp) - t_ref[...]
        loss_ref[...] += 0.5 * jnp.sum(err * err) * (1.0 / D_MODEL)
        dy = err * (1.0 / D_MODEL)
        dy_ref[...] = dy
        ggp_ref[...] += jnp.sum(dy * on, axis=0, keepdims=True)
        a = dy * gp
        d_out = r * (a - on * jnp.mean(a * on, axis=-1, keepdims=True))
        dob = d_out.astype(BF16)
        dob_ref[...] = dob
        d_merged = _dot_nt(dob, wout_vm[...])
        d_refs = (dya_ref, dyb_ref, dym_ref)
        for k in range(3):
            g = gates[k]
            dg_ref[:, 1024 * k:1024 * k + 1024] = (d_merged * us[k] * g * (1.0 - g)).astype(BF16)
            du = (d_merged * g).astype(BF16)
            du_ref[k] = du
            d_refs[k][...] = _dot_nt(du, wup_vm[512 * k:512 * k + 512, :])

    return _pallas_call(
        body, name="mid", grid=(nt,),
        out_shape=[jax.ShapeDtypeStruct((s, 3072), BF16)] + [jax.ShapeDtypeStruct((s, 512), F32)] * 3
        + [jax.ShapeDtypeStruct((s, D_MODEL), F32), jax.ShapeDtypeStruct((8, 128), F32),
           jax.ShapeDtypeStruct((1, D_MODEL), F32), jax.ShapeDtypeStruct((s, D_MODEL), BF16),
           jax.ShapeDtypeStruct((s, D_MODEL), BF16), jax.ShapeDtypeStruct((3, s, D_MODEL), BF16)],
        in_specs=[_rows(tm, 512)] * 3 + [_rows(tm, 3072), _rows(tm, D_MODEL), _rows(tm, D_MODEL),
                                         _full((1, D_MODEL)), ANY, ANY],
        out_specs=[_rows(tm, 3072)] + [_rows(tm, 512)] * 3
        + [_rows(tm, D_MODEL), _full((8, 128)), _full((1, D_MODEL)), _rows(tm, D_MODEL), _rows(tm, D_MODEL),
           pl.BlockSpec((3, tm, D_MODEL), lambda i: (0, i, 0))],
        scratch_shapes=[pltpu.VMEM((1536, D_MODEL), BF16), pltpu.VMEM((D_MODEL, D_MODEL), BF16),
                        pltpu.SemaphoreType.DMA((N_DEV + 1,))],
        compiler_params=_params(56),
    )(ya, yb, ym, pg, x, target, g_post, w_up, w_out)


def _gw_mid(mb, dob, ys, du):
    s = mb.shape[0]
    tn = 256

    def out_body(mb_ref, dob_ref, o_ref):
        o_ref[...] = _dot_tn(mb_ref[...], dob_ref[...]).astype(BF16)

    gw_out = _pallas_call(
        out_body, name="gw_out", grid=(D_MODEL // tn,),
        out_shape=jax.ShapeDtypeStruct((D_MODEL, D_MODEL), BF16),
        in_specs=[pl.BlockSpec((s, tn), lambda j: (0, j)), _full((s, D_MODEL))],
        out_specs=pl.BlockSpec((tn, D_MODEL), lambda j: (j, 0)),
        compiler_params=_params(48),
    )(mb, dob)

    per = 512 // tn

    def up_body(ya_ref, yb_ref, ym_ref, du_ref, o_ref):
        j = pl.program_id(0)
        for k, y_ref in enumerate((ya_ref, yb_ref, ym_ref)):
            @pl.when(j // per == k)
            def _(y_ref=y_ref):
                res = _dot_tn(y_ref[...], du_ref[...])
                for d in range(N_DEV):
                    o_ref[d] = res[:, 128 * d:128 * d + 128].astype(BF16)

    def y_spec(k):
        return pl.BlockSpec((s, tn), lambda j: (0, jnp.clip(j - per * k, 0, per - 1)))

    gw_up = _pallas_call(
        up_body, name="gw_up", grid=(3 * per,),
        out_shape=jax.ShapeDtypeStruct((N_DEV, 1536, 128), BF16),
        in_specs=[y_spec(0), y_spec(1), y_spec(2), pl.BlockSpec((None, s, D_MODEL), lambda j: (j // per, 0, 0))],
        out_specs=pl.BlockSpec((N_DEV, tn, 128), lambda j: (0, j, 0)),
        compiler_params=_params(48),
    )(*ys, du)
    return gw_out, gw_up


def _conv_bwd(pa, dya, w_conv):
    s = pa.shape[0]
    tm = min(512, s)
    nt = s // tm

    def body(pa_ref, pp_ref, pn_ref, d_ref, dp_ref, dn_ref, w_ref, da_ref, gw_ref):
        i = pl.program_id(0)
        first, last = i == 0, i == nt - 1

        @pl.when(first)
        def _():
            gw_ref[...] = jnp.zeros_like(gw_ref)

        w = w_ref[...]
        prev_row = pp_ref[...].astype(F32)[15:16, :]
        next_row = pn_ref[...].astype(F32)[0:1, :]
        b, c, u, z, cu, cu_m1, cu_p1, y, sig, row = _conv_common(
            pa_ref[...].astype(F32), prev_row, next_row, w, first, last, tm)
        sz = z * sig
        dya_t = d_ref[...]
        d_y = dya_t * b * sz

        def halo_dy(p_row, d_row):
            zz = p_row[:, 1536:2048]
            return d_row * p_row[:, 0:512] * (zz * _sigmoid(zz))

        dy_prev = jnp.where(first, 0.0, halo_dy(prev_row, dp_ref[7:8, :]))
        dy_next = jnp.where(last, 0.0, halo_dy(next_row, dn_ref[0:1, :]))
        dy_m1 = jnp.where(row == 0, dy_prev, pltpu.roll(d_y, 1, 0))
        dy_p1 = jnp.where(row == tm - 1, dy_next, pltpu.roll(d_y, tm - 1, 0))
        d_cu = dy_p1 * w[0:1] + d_y * w[1:2] + dy_m1 * w[2:3]
        da_ref[:, 0:512] = (dya_t * y * sz).astype(BF16)
        da_ref[:, 512:1024] = (d_cu * u).astype(BF16)
        da_ref[:, 1024:1536] = (d_cu * c).astype(BF16)
        da_ref[:, 1536:2048] = (dya_t * b * y * (sig * (1.0 + z * (1.0 - sig)))).astype(BF16)
        gw_ref[0:1, :] += jnp.sum(d_y * cu_m1, axis=0, keepdims=True)
        gw_ref[1:2, :] += jnp.sum(d_y * cu, axis=0, keepdims=True)
        gw_ref[2:3, :] += jnp.sum(d_y * cu_p1, axis=0, keepdims=True)

    prev, nxt = _halo_specs(s, tm, 16, 2048)
    dprev, dnxt = _halo_specs(s, tm, 8, 512)
    return _pallas_call(
        body, name="conv_bwd", grid=(nt,),
        out_shape=[jax.ShapeDtypeStruct((s, 2048), BF16), jax.ShapeDtypeStruct((8, 512), F32)],
        in_specs=[_rows(tm, 2048), prev, nxt, _rows(tm, 512), dprev, dnxt, _full((3, 512))],
        out_specs=[_rows(tm, 2048), _full((8, 512))],
        compiler_params=_params(48),
    )(pa, pa, pa, dya, dya, dya, w_conv)


def _attn_bwd(pq, pkv, pbz, dyb, sink, tabs):
    s = pq.shape[0]
    nb = s // ATTN_BLOCK

    def body(sink_ref, q_ref, z_ref, d_ref, cs_ref, s1_ref, s2_ref, kv_ref, csf_ref, s1f_ref, s2f_ref,
             dq_ref, dz_ref, dkv_ref, gs_ref, kpad, vpad, dk_acc, dv_acc, bias, e_scr, ds_scr):
        n = pl.program_id(0)

        @pl.when(n == 0)
        def _():
            _fill_padded(kv_ref, kpad, vpad, s)
            _fill_band_bias(bias, nb)
            dk_acc[...] = jnp.zeros_like(dk_acc)
            dv_acc[...] = jnp.zeros_like(dv_acc)
            gs_ref[...] = jnp.zeros_like(gs_ref)

        row = lax.broadcasted_iota(jnp.int32, (ATTN_BLOCK, 128), 0)
        start = pl.multiple_of(n * ATTN_BLOCK, ATTN_BLOCK)
        kw, vw = kpad[pl.ds(start, WINDOW_KEYS), :], vpad[pl.ds(start, WINDOW_KEYS), :]
        qf = q_ref[...].astype(F32)
        variant = _bias_variant(n, nb)
        z = z_ref[...].astype(F32)
        sig = _sigmoid(z)
        dyb_t = d_ref[...]
        d_attn = dyb_t * (z * sig)
        outs, dqs = [], []
        dk_w = jnp.zeros((WINDOW_KEYS, 128), F32)
        dv_w = jnp.zeros((WINDOW_KEYS, 128), F32)
        for g in range(2):
            qt = _heads_to_lanes(qf, g, row)
            inv, p_sink = _softmax_keys_major(
                _dot(kw, qt.astype(BF16)), bias, variant, _sink_row(sink_ref, g), e_scr)
            ot = _dot_tn(vw, e_scr[...]) * inv
            outs.append(ot)
            dot_ = _heads_to_lanes(d_attn, g, row)
            delta = jnp.sum(dot_ * ot, axis=0, keepdims=True)
            dpt = _dot(vw, dot_.astype(BF16))
            for k in range(WINDOW_KEYS // KEY_CHUNK):
                rw = slice(k * KEY_CHUNK, (k + 1) * KEY_CHUNK)
                ds_scr[rw, :] = (e_scr[rw, :].astype(F32) * (dpt[rw] - delta)).astype(BF16)
            sink_part = p_sink * delta
            for j in range(4):
                h = 4 * g + j
                gs_ref[h:h + 1, :] -= jnp.sum(sink_part[:, 128 * j:128 * j + 128])
            dqs.append(_dot_tn(kw, ds_scr[...]) * (inv * ATTN_SCALE))
            dk_w += _dot_nt(ds_scr[...], (qt * inv).astype(BF16)) * ATTN_SCALE
            dv_w += _dot_nt(e_scr[...], (dot_ * inv).astype(BF16))
        dk_acc[pl.ds(start, WINDOW_KEYS), :] += dk_w
        dv_acc[pl.ds(start, WINDOW_KEYS), :] += dv_w
        attn = _lanes_to_heads(outs[0], outs[1], row)
        dz_ref[...] = (dyb_t * attn * (sig * (1.0 + z * (1.0 - sig)))).astype(BF16)
        dq = _lanes_to_heads(dqs[0], dqs[1], row)
        cs, s1, s2 = cs_ref[...], s1_ref[...], s2_ref[...]
        for b in range(4):
            dq_ref[:, 128 * b:128 * b + 128] = _rope_t(dq[:, 128 * b:128 * b + 128], cs, s1, s2).astype(BF16)

        @pl.when(n == nb - 1)
        def _():
            dk = dk_acc[ATTN_BLOCK:ATTN_BLOCK + s, :]
            dkv_ref[:, 0:128] = _rope_t(dk, csf_ref[...], s1f_ref[...], s2f_ref[...]).astype(BF16)
            dkv_ref[:, 128:256] = dv_acc[ATTN_BLOCK:ATTN_BLOCK + s, :].astype(BF16)

    tile = _rows(ATTN_BLOCK, 512)
    tab = _rows(ATTN_BLOCK, 128)
    return _pallas_call(
        body, name="attn_bwd", grid=(nb,),
        out_shape=[jax.ShapeDtypeStruct((s, 512), BF16), jax.ShapeDtypeStruct((s, 512), BF16),
                   jax.ShapeDtypeStruct((s, 256), BF16), jax.ShapeDtypeStruct((8, 128), F32)],
        in_specs=[pl.BlockSpec(memory_space=pltpu.SMEM), tile, tile, tile, tab, tab, tab,
                  _full((s, 256)), _full((s, 128)), _full((s, 128)), _full((s, 128))],
        out_specs=[tile, tile, _full((s, 256)), _full((8, 128))],
        scratch_shapes=[pltpu.VMEM((s + 2 * ATTN_BLOCK, 128), BF16)] * 2
        + [pltpu.VMEM((s + 2 * ATTN_BLOCK, 128), F32)] * 2
        + [pltpu.VMEM((3, WINDOW_KEYS, STACKED), F32)] + [pltpu.VMEM((WINDOW_KEYS, STACKED), BF16)] * 2,
        compiler_params=_params(48),
    )(sink, pq, pbz, dyb, *tabs, pkv, *tabs)


def _mem_attn_bwd(pmq, pmz, mkv, dym):
    s = pmq.shape[0]
    m = mkv.shape[0]
    tm = min(512, s)

    def body(q_ref, z_ref, d_ref, mk_ref, mv_ref, dq_ref, dz_ref, dmkv_ref):
        @pl.when(pl.program_id(0) == 0)
        def _():
            dmkv_ref[...] = jnp.zeros_like(dmkv_ref)

        z = z_ref[...].astype(F32)
        sig = _sigmoid(z)
        dym_t = d_ref[...]
        d_attn = dym_t * (z * sig)
        dsilu = sig * (1.0 + z * (1.0 - sig))
        for h in range(MEM_HEADS):
            cols = slice(128 * h, 128 * h + 128)
            q, mk, mv = q_ref[:, cols], mk_ref[:, cols], mv_ref[:, cols]
            pt = _mem_softmax_t(q, mk)
            pb = pt.astype(BF16)
            o = _dot_tn(pb, mv)
            dob = d_attn[:, cols].astype(BF16)
            dpt = _dot_nt(mv, dob)
            dst = (pt * (dpt - jnp.sum(pt * dpt, axis=0, keepdims=True))).astype(BF16)
            dq_ref[:, cols] = (_dot_tn(dst, mk) * MEM_SCALE).astype(BF16)
            dz_ref[:, cols] = (dym_t[:, cols] * o * dsilu[:, cols]).astype(BF16)
            dmkv_ref[:, cols] += _dot(dst, q) * MEM_SCALE
            dmkv_ref[:, 512 + 128 * h:512 + 128 * h + 128] += _dot(pb, dob)

    return _pallas_call(
        body, name="mem_attn_bwd", grid=(s // tm,),
        out_shape=[jax.ShapeDtypeStruct((s, 512), BF16), jax.ShapeDtypeStruct((s, 512), BF16),
                   jax.ShapeDtypeStruct((m, D_MODEL), F32)],
        in_specs=[_rows(tm, 512), _rows(tm, 512), _rows(tm, 512), pl.BlockSpec((m, 512), lambda i: (0, 0)),
                  pl.BlockSpec((m, 512), lambda i: (0, 1))],
        out_specs=[_rows(tm, 512), _rows(tm, 512), _full((m, D_MODEL))],
        compiler_params=_params(32),
    )(pmq, pmz, dym, mkv, mkv)


def _mem_kv_bwd(mem, g_mem, mn, dmkv, w_mkv):
    m = mem.shape[0]

    def body(mem_ref, g_ref, mn_ref, d_ref, w_ref, gw_ref, gg_ref):
        db = d_ref[...].astype(BF16)
        gw_ref[...] = _dot_tn(mn_ref[...], db).astype(BF16)
        d_mn = _dot_nt(db, w_ref[...])
        xf = mem_ref[...]
        r = lax.rsqrt(jnp.mean(xf * xf, axis=-1, keepdims=True) + EPS)
        gg_ref[...] = jnp.sum(d_mn * (xf * r), axis=0, keepdims=True)

    return _pallas_call(
        body, name="mem_kv_bwd", grid=(1,),
        out_shape=[jax.ShapeDtypeStruct((D_MODEL, D_MODEL), BF16), jax.ShapeDtypeStruct((1, D_MODEL), F32)],
        in_specs=[_full((m, D_MODEL)), _full((1, D_MODEL)), _full((m, D_MODEL)), _full((m, D_MODEL)),
                  _full((D_MODEL, D_MODEL))],
        out_specs=[_full((D_MODEL, D_MODEL)), _full((1, D_MODEL))],
        compiler_params=_params(32),
    )(mem, g_mem, mn, dmkv, w_mkv)


def _dh_bwd(dparts, x, dy, g_pre, w_int):
    s = x.shape[0]
    tm = min(256, s)

    def body(*refs):
        d_refs = refs[:7]
        x_ref, dy_ref, g_ref, w_hbm, gx_ref, gg_ref, w_vm, sems = refs[7:]
        _load_once([(w_hbm, w_vm)], sems)

        @pl.when(pl.program_id(0) == 0)
        def _():
            gg_ref[...] = jnp.zeros_like(gg_ref)

        d_h = jnp.zeros((tm, D_MODEL), F32)
        for d_ref, (r0, width) in zip(d_refs, SEGS):
            for c0 in range(0, width, 512):
                cw = min(512, width - c0)
                d_h += _dot(d_ref[:, c0:c0 + cw], w_vm[r0 + c0:r0 + c0 + cw, :])
        xf = x_ref[...]
        r = lax.rsqrt(jnp.mean(xf * xf, axis=-1, keepdims=True) + EPS)
        xn = xf * r
        a = d_h * g_ref[...]
        gx_ref[...] = r * (a - xn * jnp.mean(a * xn, axis=-1, keepdims=True)) + dy_ref[...]
        gg_ref[...] += jnp.sum(d_h * xn, axis=0, keepdims=True)

    return _pallas_call(
        body, name="dh_bwd", grid=(s // tm,),
        out_shape=[jax.ShapeDtypeStruct((s, D_MODEL), F32), jax.ShapeDtypeStruct((1, D_MODEL), F32)],
        in_specs=[_rows(tm, w) for _, w in SEGS] + [_rows(tm, D_MODEL), _rows(tm, D_MODEL), _full((1, D_MODEL)), ANY],
        out_specs=[_rows(tm, D_MODEL), _full((1, D_MODEL))],
        scratch_shapes=[pltpu.VMEM((IN_WIDTH, D_MODEL), BF16), pltpu.SemaphoreType.DMA((1,))],
        compiler_params=_params(52),
    )(*dparts, x, dy, g_pre, w_int)


def _gw_in(dparts, h):
    s = h.shape[0]
    tn = 256
    starts, counts = [], []
    for r0, width in SEGS:
        starts.append(r0 // tn)
        counts.append(width // tn)

    def body(*refs):
        d_refs = refs[:7]
        h_hbm, o_ref, h_vm, sems = refs[7:]
        _load_once([(h_hbm, h_vm)], sems)
        j = pl.program_id(0)
        for d_ref, st, cnt in zip(d_refs, starts, counts):
            @pl.when((j >= st) & (j < st + cnt))
            def _(d_ref=d_ref):
                o_ref[...] = _dot_tn(d_ref[...], h_vm[...]).astype(BF16)

    def seg_spec(st, cnt):
        return pl.BlockSpec((s, tn), lambda j: (0, jnp.clip(j - st, 0, cnt - 1)))

    return _pallas_call(
        body, name="gw_in", grid=(IN_WIDTH // tn,),
        out_shape=jax.ShapeDtypeStruct((IN_WIDTH, D_MODEL), BF16),
        in_specs=[seg_spec(st, cnt) for st, cnt in zip(starts, counts)] + [ANY],
        out_specs=pl.BlockSpec((tn, D_MODEL), lambda j: (j, 0)),
        scratch_shapes=[pltpu.VMEM((s, D_MODEL), BF16), pltpu.SemaphoreType.DMA((1,))],
        compiler_params=_params(52),
    )(*dparts, h)


def _adamw_math(w, g, m, v):
    m2 = ADAM_B1 * m + (1.0 - ADAM_B1) * g
    v2 = ADAM_B2 * v + (1.0 - ADAM_B2) * (g * g)
    m_hat = m2 / (1.0 - ADAM_B1 ** ADAM_STEP)
    v_hat = v2 / (1.0 - ADAM_B2 ** ADAM_STEP)
    delta = -ADAM_LR * (m_hat / (jnp.sqrt(v_hat) + ADAM_EPS) + ADAM_WD * w)
    return delta, m2, v2


def _sum_adamw(own, land, chip, block, w, m, v, name, tiles=1):
    r, c = w.shape
    rt = r // tiles

    def body(c_ref, own_ref, l1_ref, l2_ref, l3_ref, w_ref, m_ref, v_ref, g_ref, d_ref, m2_ref, v2_ref):
        g = own_ref[...].astype(F32)
        for l_ref in (l1_ref, l2_ref, l3_ref):
            g += l_ref[...].astype(F32)
        g_ref[...] = g
        d_ref[...], m2_ref[...], v2_ref[...] = _adamw_math(w_ref[...], g, m_ref[...], v_ref[...])

    def share(k):
        return pl.BlockSpec((None, rt, c), lambda i, c_ref: (jnp.bitwise_xor(c_ref[0], k), block * tiles + i, 0))

    spec = pl.BlockSpec((rt, c), lambda i, c_ref: (i, 0))
    grid_spec = pltpu.PrefetchScalarGridSpec(
        num_scalar_prefetch=1, grid=(tiles,),
        in_specs=[share(0), share(1), share(2), share(3)] + [spec] * 3, out_specs=[spec] * 4)
    return _pallas_call(
        body, name=name, grid_spec=grid_spec,
        out_shape=[jax.ShapeDtypeStruct((r, c), F32)] * 4,
        compiler_params=_params(48),
    )(chip, own, land, land, land, w, m, v)


def _pack_sum(packs):
    def body(p_ref, o_ref):
        acc = p_ref[0]
        for k in range(1, N_DEV):
            acc += p_ref[k]
        o_ref[...] = acc

    return _pallas_call(
        body, name="pack_sum", grid=(1,),
        out_shape=jax.ShapeDtypeStruct((8, D_MODEL), F32),
        in_specs=[_full((N_DEV, 8, D_MODEL))], out_specs=_full((8, D_MODEL)),
    )(packs)


def _small_adamw(ws, gs, ms, vs):
    k = len(ws)

    def body(*refs):
        w_refs, g_refs, m_refs, v_refs = (refs[j * k:(j + 1) * k] for j in range(4))
        outs = refs[4 * k:]
        for j in range(k):
            outs[j][...], outs[k + j][...], outs[2 * k + j][...] = _adamw_math(
                w_refs[j][...], g_refs[j][...], m_refs[j][...], v_refs[j][...])

    specs = [_full(w.shape) for w in ws]
    res = _pallas_call(
        body, name="small_adamw", grid=(1,),
        out_shape=[jax.ShapeDtypeStruct(w.shape, F32) for w in ws] * 3,
        in_specs=specs * 4, out_specs=specs * 3,
    )(*ws, *gs, *ms, *vs)
    return res[:k], res[k:2 * k], res[2 * k:]


def kernel(x, mem, g_pre, w_in, w_conv, attn_sink, g_mem, w_mem_kv, w_up_a, w_up_b, w_up_m, w_out, g_post, loss_target, m_g_pre, m_w_in, m_w_conv, m_attn_sink, m_g_mem, m_w_mem_kv, m_w_up_a, m_w_up_b, m_w_up_m, m_w_out, m_g_post, v_g_pre, v_w_in, v_w_conv, v_attn_sink, v_g_mem, v_w_mem_kv, v_w_up_a, v_w_up_b, v_w_up_m, v_w_out, v_g_post):
    s = x.shape[1]
    x2, mem2, tgt2 = x[0], mem[0], loss_target[0]
    me = 4 * lax.axis_index("x") + 2 * lax.axis_index("y") + lax.axis_index("c")

    w_up_loc = jnp.concatenate([w_up_a[0], w_up_b[0], w_up_m[0]], axis=0).astype(BF16)
    w_conv_loc = jnp.zeros((8, 128), F32).at[:3, :64].set(w_conv[0])
    w_int_g, w_mkv_g, w_out_g, w_up_g, w_conv_g = _all_gather(
        [w_in[0].T.astype(BF16), w_mem_kv[0].astype(BF16), w_out[0].astype(BF16), w_up_loc, w_conv_loc],
        "gather_weights")
    w_int = w_int_g.reshape(IN_WIDTH, D_MODEL)
    w_mkv = w_mkv_g.reshape(D_MODEL, D_MODEL)
    w_out_f = w_out_g.reshape(D_MODEL, D_MODEL)
    w_conv_f = w_conv_g[:, :3, :64].transpose(1, 0, 2).reshape(3, 512)
    sink = attn_sink[0]
    tabs = _rope_tables(s)

    h, pa, pq, pkv, pbz, pmq, pmz, pg = _proj_fwd(x2, g_pre, w_int, tabs)
    mn, mkv = _mem_kv_fwd(mem2, g_mem, w_mkv)
    ya = _conv_fwd(pa, w_conv_f)
    yb = _attn_fwd(pq, pkv, pbz, sink)
    ym = _mem_attn_fwd(pmq, pmz, mkv)
    dg, dya, dyb, dym, dy, loss_p, gg_post, mb, dob, du = _mid(ya, yb, ym, pg, x2, tgt2, g_post, w_up_g, w_out_f)
    gw_out, gw_up = _gw_mid(mb, dob, (ya, yb, ym), du)

    core = lax.axis_index("c").astype(jnp.int32).reshape(1)
    chip = (2 * lax.axis_index("x") + lax.axis_index("y")).astype(jnp.int32).reshape(1)

    def exchange_start(shares, tag):
        from_sibling = _sibling_exchange(shares, "grads_to_sibling_" + tag)
        chip_shares = _pair_add(shares, from_sibling, core, "grads_pair_add_" + tag)
        return _chip_exchange_start(chip_shares, "grads_to_chips_start_" + tag)

    dmq, dmz, dmkv = _mem_attn_bwd(pmq, pmz, mkv, dym)
    gw_mkv, gg_mem = _mem_kv_bwd(mem2, g_mem, mn, dmkv, w_mkv)
    send1, recv1, srcs1, lands1, token1 = exchange_start(
        [gw_mkv.reshape(N_DEV, 128, D_MODEL), gw_out.reshape(N_DEV, 128, D_MODEL), gw_up], "small")
    da, gw_conv = _conv_bwd(pa, dya, w_conv_f + token1[0:1, 0:1])
    dq, dbz, dkv, g_sink = _attn_bwd(pq, pkv, pbz, dyb, sink, tabs)
    dparts = (da, dq, dkv, dbz, dmq, dmz, dg)
    gw_int = _gw_in(dparts, h)
    send2, recv2, srcs2, lands2, token2 = exchange_start([gw_int.reshape(N_DEV, SHARD_IN, D_MODEL)], "w_in")
    grad_x, gg_pre = _dh_bwd(dparts, x2, dy, g_pre + token2[0:1, 0:1], w_int)
    (o_mkv, o_out, o_up, o_int), (l_mkv, l_out, l_up, l_int) = _chip_exchange_wait(
        send1 + send2, recv1 + recv2, srcs1 + srcs2, lands1 + lands2, grad_x, "grads_to_chips_wait")

    row3 = jnp.concatenate([gw_conv[0:1], gw_conv[1:2]], axis=1)
    row4 = jnp.concatenate([gw_conv[2:3], g_sink[:, 0].reshape(1, 8), loss_p[0:1, 0:1],
                            jnp.zeros((1, 512 - 9), F32)], axis=1)
    pack = jnp.concatenate([gg_pre, gg_mem, gg_post, row3, row4, jnp.zeros((3, D_MODEL), F32)], axis=0)
    (packs,) = _all_gather([pack], "gather_small")
    tot = _pack_sum(packs)

    g_w_in, d_w_in, nm_w_in, nv_w_in = (t.T for t in _sum_adamw(
        o_int, l_int, chip, 0, w_in[0].T, m_w_in[0].T, v_w_in[0].T, "adamw_w_in", tiles=2))
    g_mkv, d_mkv, nm_mkv, nv_mkv = _sum_adamw(
        o_mkv, l_mkv, chip, 0, w_mem_kv[0], m_w_mem_kv[0], v_w_mem_kv[0], "adamw_w_mem_kv")
    g_out, d_out, nm_out, nv_out = _sum_adamw(o_out, l_out, chip, 0, w_out[0], m_w_out[0], v_w_out[0], "adamw_w_out")
    up = [_sum_adamw(o_up, l_up, chip, k, w[0], m[0], v[0], "adamw_w_up_" + "abm"[k])
          for k, (w, m, v) in enumerate([(w_up_a, m_w_up_a, v_w_up_a), (w_up_b, m_w_up_b, v_w_up_b),
                                         (w_up_m, m_w_up_m, v_w_up_m)])]

    g_g_pre, g_g_mem, g_g_post = tot[0:1], tot[1:2], tot[2:3]
    g_conv_full = jnp.concatenate([tot[3:4, 0:512], tot[3:4, 512:1024], tot[4:5, 0:512]], axis=0)
    g_conv = lax.dynamic_slice(g_conv_full, (0, 64 * me), (3, 64))
    g_sink_tot = tot[4:5, 512:520]
    loss = tot[4, 520]
    small_w = [g_pre, w_conv[0], attn_sink, g_mem, g_post]
    small_g = [g_g_pre, g_conv, g_sink_tot, g_g_mem, g_g_post]
    small_m = [m_g_pre, m_w_conv[0], m_attn_sink, m_g_mem, m_g_post]
    small_v = [v_g_pre, v_w_conv[0], v_attn_sink, v_g_mem, v_g_post]
    sd, sm, sv = _small_adamw(small_w, small_g, small_m, small_v)

    def lead(a):
        return a[None]

    grads = [g_g_pre, lead(g_w_in), lead(g_conv), g_sink_tot, g_g_mem, lead(g_mkv), lead(up[0][0]),
             lead(up[1][0]), lead(up[2][0]), lead(g_out), g_g_post]

    def assemble(small, big_in, big_mkv, big_up, big_out):
        return [small[0], lead(big_in), lead(small[1]), small[2], small[3], lead(big_mkv), lead(big_up[0]),
                lead(big_up[1]), lead(big_up[2]), lead(big_out), small[4]]

    deltas = assemble(sd, d_w_in, d_mkv, [u[1] for u in up], d_out)
    new_m = assemble(sm, nm_w_in, nm_mkv, [u[2] for u in up], nm_out)
    new_v = assemble(sv, nv_w_in, nv_mkv, [u[3] for u in up], nv_out)
    return (loss, grad_x[None], *grads, *deltas, *new_m, *new_v)
```

```python
import functools

import jax
import jax.numpy as jnp
from jax import lax
from jax.experimental import pallas as pl
from jax.experimental.pallas import tpu as pltpu

F32 = jnp.float32
BF16 = jnp.bfloat16
MESH = pl.DeviceIdType.MESH

N_DEV = 8
D_MODEL = 1024
EPS = 1e-6
ROPE_THETA = 500000.0
ROT_DIM = 16
HEAD_DIM = 64
ATTN_BLOCK = 128
MEM_HEADS = 4
MEM_HEAD_DIM = 128
ATTN_SCALE = HEAD_DIM ** -0.5
MEM_SCALE = MEM_HEAD_DIM ** -0.5

ADAM_LR = 0.001
ADAM_B1 = 0.9
ADAM_B2 = 0.999
ADAM_EPS = 1e-08
ADAM_WD = 0.01
ADAM_STEP = 10

SEG_A = (0, 2048)
SEG_BQ = (2048, 512)
SEG_BKV = (2560, 256)
SEG_BZ = (2816, 512)
SEG_MQ = (3328, 512)
SEG_MZ = (3840, 512)
SEG_G = (4352, 3072)
SEGS = (SEG_A, SEG_BQ, SEG_BKV, SEG_BZ, SEG_MQ, SEG_MZ, SEG_G)
IN_WIDTH = 7424
SHARD_IN = IN_WIDTH // N_DEV

V7X_VMEM_BYTES = 64 * 1024 * 1024
ANY = pl.BlockSpec(memory_space=pl.ANY)


def _params(vmem_mb):
    assert vmem_mb * 1024 * 1024 < V7X_VMEM_BYTES
    return pltpu.CompilerParams(dimension_semantics=("arbitrary",), vmem_limit_bytes=vmem_mb * 1024 * 1024)


def _full(shape):
    zeros = (0,) * len(shape)
    return pl.BlockSpec(shape, lambda i: zeros)


def _rows(tm, width):
    return pl.BlockSpec((tm, width), lambda i: (i, 0))


def _dot(a, b):
    return jnp.dot(a, b, preferred_element_type=F32)


def _dot_nt(a, b):
    return lax.dot_general(a, b, (((1,), (1,)), ((), ())), preferred_element_type=F32)


def _dot_tn(a, b):
    return lax.dot_general(a, b, (((0,), (0,)), ((), ())), preferred_element_type=F32)


def _sigmoid(z):
    return 1.0 / (1.0 + jnp.exp(-z))


def _rope(t, cs, s1, s2):
    return t * cs + pltpu.roll(t, 120, 1) * s1 + pltpu.roll(t, 8, 1) * s2


def _rope_t(d, cs, s1, s2):
    return d * cs + pltpu.roll(d * s1, 8, 1) + pltpu.roll(d * s2, 120, 1)


def _rope_tables(s):
    half = ROT_DIM // 2
    inv_freq = jnp.power(jnp.float32(ROPE_THETA), -jnp.arange(half, dtype=F32) * (2.0 / ROT_DIM))
    d = jnp.arange(128) % HEAD_DIM
    ang = jnp.arange(s).astype(F32)[:, None] * inv_freq[d % half][None, :]
    cos, sin = jnp.cos(ang), jnp.sin(ang)
    lo, hi = (d < half)[None, :], ((d >= half) & (d < ROT_DIM))[None, :]
    return jnp.where(lo | hi, cos, 1.0), jnp.where(lo, -sin, 0.0), jnp.where(hi, sin, 0.0)


def _load_once(pairs, sems):
    @pl.when(pl.program_id(0) == 0)
    def _():
        cps = [pltpu.make_async_copy(src, dst, sems.at[k]) for k, (src, dst) in enumerate(pairs)]
        for cp in cps:
            cp.start()
        for cp in cps:
            cp.wait()


def _my_place():
    x, y, c = lax.axis_index("x"), lax.axis_index("y"), lax.axis_index("c")
    return x, y, c


def _all_gather(arrs, name):
    n = len(arrs)

    def body(*refs):
        ins, outs = refs[:n], refs[n:2 * n]
        send_sems, recv_sems, local_sems = refs[2 * n:]
        x, y, c = _my_place()
        me, sibling = (x, y, c), (x, y, 1 - c)
        chips = [(1 - x, y), (x, 1 - y), (1 - x, 1 - y)]

        def idx(px, py, pc):
            return 4 * px + 2 * py + pc

        def copy(a, k, block, to, src=None):
            dst = outs[a].at[idx(*block)]
            return pltpu.make_async_remote_copy(
                src_ref=dst if src is None else src, dst_ref=dst,
                send_sem=send_sems.at[a * 7 + k], recv_sem=recv_sems.at[a * 7 + k],
                device_id=to, device_id_type=MESH)

        mine = [pltpu.make_async_copy(ins[a], outs[a].at[idx(*me)], local_sems.at[a]) for a in range(n)]
        for cp in mine:
            cp.start()
        first = []
        for a in range(n):
            first.append(copy(a, 0, me, sibling, src=ins[a]))
        for j, chip in enumerate(chips):
            for a in range(n):
                first.append(copy(a, 1 + j, me, (*chip, c), src=ins[a]))
        for cp in first:
            cp.start()
        passed = []
        for j, chip in enumerate(chips):
            for a in range(n):
                copy(a, 1 + j, (*chip, c), me).wait_recv()
                cp = copy(a, 4 + j, (*chip, c), sibling)
                cp.start()
                passed.append(cp)
        for a in range(n):
            copy(a, 0, sibling, me).wait_recv()
        for j, chip in enumerate(chips):
            for a in range(n):
                copy(a, 4 + j, (*chip, 1 - c), me).wait_recv()
        for cp in first + passed:
            cp.wait_send()
        for cp in mine:
            cp.wait()

    return pl.pallas_call(
        body, name=name,
        out_shape=[jax.ShapeDtypeStruct((N_DEV,) + a.shape, a.dtype) for a in arrs],
        in_specs=[ANY] * n, out_specs=[ANY] * n,
        scratch_shapes=[pltpu.SemaphoreType.DMA((7 * n,)), pltpu.SemaphoreType.DMA((7 * n,)),
                        pltpu.SemaphoreType.DMA((n,))],
    )(*arrs)


N_CHIPS = 4


def _sibling_exchange(arrs, name):
    n = len(arrs)

    def body(*refs):
        ins, outs = refs[:n], refs[n:2 * n]
        send_sems, recv_sems = refs[2 * n:]
        x, y, c = _my_place()
        sibling = (x, y, 1 - c)

        def copy(a, j):
            return pltpu.make_async_remote_copy(
                src_ref=ins[a].at[2 * j + (1 - c)], dst_ref=outs[a].at[j],
                send_sem=send_sems.at[a * N_CHIPS + j], recv_sem=recv_sems.at[a * N_CHIPS + j],
                device_id=sibling, device_id_type=MESH)

        cps = [copy(a, j) for j in range(N_CHIPS) for a in range(n)]
        for cp in cps:
            cp.start()
        for cp in cps:
            cp.wait_recv()
        for cp in cps:
            cp.wait_send()

    return pl.pallas_call(
        body, name=name,
        out_shape=[jax.ShapeDtypeStruct((N_CHIPS,) + a.shape[1:], a.dtype) for a in arrs],
        in_specs=[ANY] * n, out_specs=[ANY] * n,
        scratch_shapes=[pltpu.SemaphoreType.DMA((N_CHIPS * n,)), pltpu.SemaphoreType.DMA((N_CHIPS * n,))],
    )(*arrs)


def _pair_add(mine, recv, core, name):
    n = len(mine)

    def body(c_ref, *refs):
        for a in range(n):
            refs[2 * n + a][...] = (refs[a][...].astype(F32) + refs[n + a][...].astype(F32)).astype(BF16)

    def blk(a):
        return (None,) + a.shape[1:]

    grid_spec = pltpu.PrefetchScalarGridSpec(
        num_scalar_prefetch=1, grid=(N_CHIPS,),
        in_specs=[pl.BlockSpec(blk(a), lambda j, c_ref: (2 * j + c_ref[0], 0, 0)) for a in mine]
        + [pl.BlockSpec(blk(a), lambda j, c_ref: (j, 0, 0)) for a in recv],
        out_specs=[pl.BlockSpec(blk(a), lambda j, c_ref: (j, 0, 0)) for a in recv])
    return pl.pallas_call(
        body, name=name, grid_spec=grid_spec,
        out_shape=[jax.ShapeDtypeStruct(a.shape, BF16) for a in recv],
        compiler_params=_params(32),
    )(core, *mine, *recv)


HBM = pl.BlockSpec(memory_space=pltpu.HBM)
SEM = pl.BlockSpec(memory_space=pltpu.SEMAPHORE)
N_PEER_CHIPS = 3


def _chip_copies(srcs, lands, send_sems, recv_sems):
    x, y, c = _my_place()
    my_chip = 2 * x + y
    peers = [(x, 1 - y), (1 - x, y), (1 - x, 1 - y)]
    cps = []
    for k, (px, py) in enumerate(peers):
        for a in range(len(srcs)):
            j = a * N_PEER_CHIPS + k
            cps.append(pltpu.make_async_remote_copy(
                src_ref=srcs[a].at[2 * px + py], dst_ref=lands[a].at[my_chip],
                send_sem=send_sems[j], recv_sem=recv_sems[j],
                device_id=(px, py, c), device_id_type=MESH))
    return cps


N_PEERS = N_DEV - 1


def _gather_copies(srcs, lands, send_sems, recv_sems):
    x, y, c = _my_place()
    me_idx = 4 * x + 2 * y + c
    flips = [(0, 0, 1), (0, 1, 0), (1, 0, 0), (0, 1, 1), (1, 0, 1), (1, 1, 0), (1, 1, 1)]
    cps = []
    for k, (fx, fy, fc) in enumerate(flips):
        peer = ((1 - x) if fx else x, (1 - y) if fy else y, (1 - c) if fc else c)
        for a in range(len(srcs)):
            j = a * N_PEERS + k
            cps.append(pltpu.make_async_remote_copy(
                src_ref=srcs[a], dst_ref=lands[a].at[me_idx], send_sem=send_sems[j], recv_sem=recv_sems[j],
                device_id=peer, device_id_type=MESH))
    return cps


def _split_start(copies, per_array, arrs, lands, name):
    arrs, lands = list(arrs), list(lands)
    n = len(arrs)
    k = n * per_array

    def body(*refs):
        srcs, land_refs = refs[:n], refs[n:2 * n]
        send_sems, recv_sems = refs[2 * n:2 * n + k], refs[2 * n + k:2 * n + 2 * k]
        token = refs[-1]
        for cp in copies(srcs, land_refs, send_sems, recv_sems):
            cp.start()
        token[...] = jnp.zeros_like(token)

    hbm_arrs = [pltpu.with_memory_space_constraint(a, pltpu.HBM) for a in arrs]
    lands = [pltpu.with_memory_space_constraint(a, pltpu.HBM) for a in lands]
    res = pl.pallas_call(
        body, name=name,
        out_shape=[pltpu.SemaphoreType.DMA(())] * (2 * k) + [pltpu.HBM(a.shape, a.dtype) for a in arrs + lands]
        + [jax.ShapeDtypeStruct((8, 128), F32)],
        in_specs=[HBM] * (2 * n),
        out_specs=[SEM] * (2 * k) + [HBM] * (2 * n) + [pl.BlockSpec(memory_space=pltpu.VMEM)],
        input_output_aliases={a: 2 * k + a for a in range(2 * n)},
        compiler_params=pltpu.CompilerParams(has_side_effects=pltpu.SideEffectType.DATAFLOW_SIDE_EFFECTING),
    )(*hbm_arrs, *lands)
    return res[:k], res[k:2 * k], res[2 * k:2 * k + n], res[2 * k + n:2 * k + 2 * n], res[-1]


def _split_wait(copies, per_array, send_sems, recv_sems, srcs, lands, after, name):
    n = len(srcs)
    k = n * per_array

    def body(*refs):
        src_refs, land_refs = refs[:n], refs[n:2 * n]
        s_sems, r_sems = refs[2 * n:2 * n + k], refs[2 * n + k:2 * n + 2 * k]
        for cp in copies(src_refs, land_refs, s_sems, r_sems):
            cp.wait_send()
            cp.wait_recv()

    res = pl.pallas_call(
        body, name=name,
        out_shape=[pltpu.HBM(a.shape, a.dtype) for a in list(srcs) + list(lands)],
        in_specs=[HBM] * (2 * n) + [SEM] * (2 * k) + [ANY],
        out_specs=[HBM] * (2 * n),
        input_output_aliases={a: a for a in range(2 * n)},
        compiler_params=pltpu.CompilerParams(has_side_effects=pltpu.SideEffectType.DATAFLOW_SIDE_EFFECTING),
    )(*srcs, *lands, *send_sems, *recv_sems, after)
    return res[:n], res[n:]


def _chip_exchange_start(arrs, name):
    return _split_start(_chip_copies, N_PEER_CHIPS, arrs, [lax.empty(a.shape, a.dtype) for a in arrs], name)


def _chip_exchange_wait(send_sems, recv_sems, srcs, lands, after, name):
    return _split_wait(_chip_copies, N_PEER_CHIPS, send_sems, recv_sems, srcs, lands, after, name)


def _gather_start(arrs, me_idx, name):
    lands = [lax.dynamic_update_slice(lax.empty((N_DEV,) + a.shape, a.dtype), a[None], (me_idx, 0, 0)) for a in arrs]
    return _split_start(_gather_copies, N_PEERS, arrs, lands, name)


def _gather_wait(send_sems, recv_sems, srcs, lands, after, name):
    return _split_wait(_gather_copies, N_PEERS, send_sems, recv_sems, srcs, lands, after, name)[1]


def _proj_fwd(x, g_pre, w_int, tabs):
    s = x.shape[0]
    tm = min(512, s)

    def body(x_ref, g_ref, cs_ref, s1_ref, s2_ref, w_hbm,
             h_ref, pa_ref, pq_ref, pkv_ref, pbz_ref, pmq_ref, pmz_ref, pg_ref, w_vm, sems):
        _load_once([(w_hbm, w_vm)], sems)
        xf = x_ref[...]
        r = lax.rsqrt(jnp.mean(xf * xf, axis=-1, keepdims=True) + EPS)
        h = ((xf * r) * g_ref[...]).astype(BF16)
        h_ref[...] = h
        cs, s1, s2 = cs_ref[...], s1_ref[...], s2_ref[...]

        def mm(seg, c0, width):
            return _dot_nt(h, w_vm[seg[0] + c0:seg[0] + c0 + width, :])

        for c0 in range(0, SEG_A[1], 512):
            pa_ref[:, c0:c0 + 512] = mm(SEG_A, c0, 512).astype(BF16)
        q = mm(SEG_BQ, 0, 512)
        for b in range(4):
            pq_ref[:, 128 * b:128 * b + 128] = _rope(q[:, 128 * b:128 * b + 128], cs, s1, s2).astype(BF16)
        kv = mm(SEG_BKV, 0, 256)
        pkv_ref[:, 0:128] = _rope(kv[:, 0:128], cs, s1, s2).astype(BF16)
        pkv_ref[:, 128:256] = kv[:, 128:256].astype(BF16)
        pbz_ref[...] = mm(SEG_BZ, 0, 512).astype(BF16)
        pmq_ref[...] = mm(SEG_MQ, 0, 512).astype(BF16)
        pmz_ref[...] = mm(SEG_MZ, 0, 512).astype(BF16)
        for c0 in range(0, SEG_G[1], 512):
            pg_ref[:, c0:c0 + 512] = mm(SEG_G, c0, 512).astype(BF16)

    widths = (D_MODEL, 2048, 512, 256, 512, 512, 512, 3072)
    return pl.pallas_call(
        body, name="proj_fwd", grid=(s // tm,),
        out_shape=[jax.ShapeDtypeStruct((s, w), BF16) for w in widths],
        in_specs=[_rows(tm, D_MODEL), _full((1, D_MODEL)), _rows(tm, 128), _rows(tm, 128), _rows(tm, 128), ANY],
        out_specs=[_rows(tm, w) for w in widths],
        scratch_shapes=[pltpu.VMEM((IN_WIDTH, D_MODEL), BF16), pltpu.SemaphoreType.DMA((1,))],
        compiler_params=_params(52),
    )(x, g_pre, *tabs, w_int)


def _mem_kv_fwd(mem, g_mem, w_mkv):
    m = mem.shape[0]

    def body(mem_ref, g_ref, w_ref, mn_ref, mkv_ref):
        xf = mem_ref[...]
        r = lax.rsqrt(jnp.mean(xf * xf, axis=-1, keepdims=True) + EPS)
        mn = ((xf * r) * g_ref[...]).astype(BF16)
        mn_ref[...] = mn
        mkv_ref[...] = _dot(mn, w_ref[...]).astype(BF16)

    return pl.pallas_call(
        body, name="mem_kv_fwd", grid=(1,),
        out_shape=[jax.ShapeDtypeStruct((m, D_MODEL), BF16)] * 2,
        in_specs=[_full((m, D_MODEL)), _full((1, D_MODEL)), _full((D_MODEL, D_MODEL))],
        out_specs=[_full((m, D_MODEL))] * 2,
        compiler_params=_params(32),
    )(mem, g_mem, w_mkv)


def _halo_specs(s, tm, rows, width):
    nblk = s // rows
    prev = pl.BlockSpec((rows, width), lambda i: (jnp.maximum(i * (tm // rows) - 1, 0), 0))
    nxt = pl.BlockSpec((rows, width), lambda i: (jnp.minimum((i + 1) * (tm // rows), nblk - 1), 0))
    return prev, nxt


def _conv_common(pa, prev_row, next_row, w, first, last, tm):
    b, c, u, z = (pa[:, 512 * k:512 * k + 512] for k in range(4))
    cu = c * u
    cu_prev = jnp.where(first, 0.0, prev_row[:, 512:1024] * prev_row[:, 1024:1536])
    cu_next = jnp.where(last, 0.0, next_row[:, 512:1024] * next_row[:, 1024:1536])
    row = lax.broadcasted_iota(jnp.int32, (tm, 512), 0)
    cu_m1 = jnp.where(row == 0, cu_prev, pltpu.roll(cu, 1, 0))
    cu_p1 = jnp.where(row == tm - 1, cu_next, pltpu.roll(cu, tm - 1, 0))
    y = cu_m1 * w[0:1] + cu * w[1:2] + cu_p1 * w[2:3]
    sig = _sigmoid(z)
    return b, c, u, z, cu, cu_m1, cu_p1, y, sig, row


def _conv_fwd(pa, w_conv):
    s = pa.shape[0]
    tm = min(512, s)
    nt = s // tm

    def body(pa_ref, pp_ref, pn_ref, w_ref, ya_ref):
        i = pl.program_id(0)
        prev_row = pp_ref[...].astype(F32)[15:16, :]
        next_row = pn_ref[...].astype(F32)[0:1, :]
        b, _, _, z, _, _, _, y, sig, _ = _conv_common(
            pa_ref[...].astype(F32), prev_row, next_row, w_ref[...], i == 0, i == nt - 1, tm)
        ya_ref[...] = (b * y * (z * sig)).astype(BF16)

    prev, nxt = _halo_specs(s, tm, 16, 2048)
    return pl.pallas_call(
        body, name="conv_fwd", grid=(nt,),
        out_shape=jax.ShapeDtypeStruct((s, 512), BF16),
        in_specs=[_rows(tm, 2048), prev, nxt, _full((3, 512))],
        out_specs=_rows(tm, 512),
        compiler_params=_params(48),
    )(pa, pa, pa, w_conv)


def _heads_to_lanes(a, g, row):
    low = row < HEAD_DIM
    parts = []
    for b in (2 * g, 2 * g + 1):
        t = jnp.transpose(a[:, 128 * b:128 * b + 128])
        swapped = pltpu.roll(t, HEAD_DIM, 0)
        if g == 0:
            parts += [jnp.where(low, t, 0.0), jnp.where(low, swapped, 0.0)]
        else:
            parts += [jnp.where(low, 0.0, swapped), jnp.where(low, 0.0, t)]
    return jnp.concatenate(parts, axis=1)


def _lanes_to_heads(t0, t1, row):
    low = row < HEAD_DIM
    blocks = []
    for b in range(4):
        g = b // 2
        tg = (t0, t1)[g]
        je = 2 * (b - 2 * g)
        even, odd = tg[:, 128 * je:128 * je + 128], tg[:, 128 * je + 128:128 * je + 256]
        if g == 0:
            t = jnp.where(low, even, pltpu.roll(odd, HEAD_DIM, 0))
        else:
            t = jnp.where(low, pltpu.roll(even, HEAD_DIM, 0), odd)
        blocks.append(jnp.transpose(t))
    return jnp.concatenate(blocks, axis=1)


WINDOW_KEYS = 3 * ATTN_BLOCK
STACKED = 4 * ATTN_BLOCK
KEY_CHUNK = 32


def _fill_band_bias(bias, nb):
    assert nb >= 2
    c = lax.broadcasted_iota(jnp.int32, (WINDOW_KEYS, STACKED), 0)
    r = lax.broadcasted_iota(jnp.int32, (WINDOW_KEYS, STACKED), 1) & (ATTN_BLOCK - 1)
    band = (c >= r) & (c <= r + 2 * ATTN_BLOCK)
    for v, ok in enumerate((band, band & (c >= ATTN_BLOCK), band & (c < 2 * ATTN_BLOCK))):
        bias[v] = jnp.where(ok, 0.0, -jnp.inf)


def _bias_variant(n, nb):
    return jnp.where(n == 0, 1, jnp.where(n == nb - 1, 2, 0))


def _sink_row(sink_ref, g):
    return jnp.concatenate([jnp.full((1, ATTN_BLOCK), sink_ref[4 * g + j], F32) for j in range(4)], axis=1)


def _softmax_keys_major(sc, bias, variant, sink, e_scr):
    chunks = [pl.ds(k * KEY_CHUNK, KEY_CHUNK) for k in range(WINDOW_KEYS // KEY_CHUNK)]
    rows = [slice(k * KEY_CHUNK, (k + 1) * KEY_CHUNK) for k in range(WINDOW_KEYS // KEY_CHUNK)]
    m_run = jnp.full((KEY_CHUNK, STACKED), -jnp.inf, F32)
    for ck, rw in zip(chunks, rows):
        m_run = jnp.maximum(m_run, sc[rw] * ATTN_SCALE + bias[variant, ck, :])
    m = jnp.maximum(jnp.max(m_run, axis=0, keepdims=True), sink)
    l_run = jnp.zeros((KEY_CHUNK, STACKED), F32)
    for ck, rw in zip(chunks, rows):
        e = jnp.exp(sc[rw] * ATTN_SCALE + bias[variant, ck, :] - m)
        l_run += e
        e_scr[rw, :] = e.astype(BF16)
    es = jnp.exp(sink - m)
    inv = 1.0 / (jnp.sum(l_run, axis=0, keepdims=True) + es)
    return inv, es * inv


def _fill_padded(kv_ref, kpad, vpad, s):
    zero = jnp.zeros((ATTN_BLOCK, 128), BF16)
    kpad[0:ATTN_BLOCK, :] = zero
    vpad[0:ATTN_BLOCK, :] = zero
    kpad[ATTN_BLOCK + s:2 * ATTN_BLOCK + s, :] = zero
    vpad[ATTN_BLOCK + s:2 * ATTN_BLOCK + s, :] = zero
    kpad[ATTN_BLOCK:ATTN_BLOCK + s, :] = kv_ref[:, 0:128]
    vpad[ATTN_BLOCK:ATTN_BLOCK + s, :] = kv_ref[:, 128:256]


def _attn_fwd(pq, pkv, pbz, sink):
    s = pq.shape[0]
    nb = s // ATTN_BLOCK

    def body(sink_ref, q_ref, z_ref, kv_ref, yb_ref, kpad, vpad, bias, e_scr):
        n = pl.program_id(0)

        @pl.when(n == 0)
        def _():
            _fill_padded(kv_ref, kpad, vpad, s)
            _fill_band_bias(bias, nb)

        row = lax.broadcasted_iota(jnp.int32, (ATTN_BLOCK, 128), 0)
        start = pl.multiple_of(n * ATTN_BLOCK, ATTN_BLOCK)
        kw, vw = kpad[pl.ds(start, WINDOW_KEYS), :], vpad[pl.ds(start, WINDOW_KEYS), :]
        qf = q_ref[...].astype(F32)
        variant = _bias_variant(n, nb)
        outs = []
        for g in range(2):
            qt = _heads_to_lanes(qf, g, row).astype(BF16)
            inv, _ = _softmax_keys_major(_dot(kw, qt), bias, variant, _sink_row(sink_ref, g), e_scr)
            outs.append(_dot_tn(vw, e_scr[...]) * inv)
        attn = _lanes_to_heads(outs[0], outs[1], row)
        z = z_ref[...].astype(F32)
        yb_ref[...] = (attn * (z * _sigmoid(z))).astype(BF16)

    return pl.pallas_call(
        body, name="attn_fwd", grid=(nb,),
        out_shape=jax.ShapeDtypeStruct((s, 512), BF16),
        in_specs=[pl.BlockSpec(memory_space=pltpu.SMEM), _rows(ATTN_BLOCK, 512), _rows(ATTN_BLOCK, 512),
                  _full((s, 256))],
        out_specs=_rows(ATTN_BLOCK, 512),
        scratch_shapes=[pltpu.VMEM((s + 2 * ATTN_BLOCK, 128), BF16)] * 2
        + [pltpu.VMEM((3, WINDOW_KEYS, STACKED), F32), pltpu.VMEM((WINDOW_KEYS, STACKED), BF16)],
        compiler_params=_params(32),
    )(sink, pq, pbz, pkv)


def _mem_softmax_t(q, mk):
    sc = _dot_nt(mk, q) * MEM_SCALE
    e = jnp.exp(sc - jnp.max(sc, axis=0, keepdims=True))
    return e * (1.0 / jnp.sum(e, axis=0, keepdims=True))


def _mem_attn_fwd(pmq, pmz, mkv):
    s = pmq.shape[0]
    m = mkv.shape[0]
    tm = min(512, s)

    def body(q_ref, z_ref, mk_ref, mv_ref, ym_ref):
        z = z_ref[...].astype(F32)
        sz = z * _sigmoid(z)
        for h in range(MEM_HEADS):
            cols = slice(128 * h, 128 * h + 128)
            pt = _mem_softmax_t(q_ref[:, cols], mk_ref[:, cols])
            o = _dot_tn(pt.astype(BF16), mv_ref[:, cols])
            ym_ref[:, cols] = (o * sz[:, cols]).astype(BF16)

    return pl.pallas_call(
        body, name="mem_attn_fwd", grid=(s // tm,),
        out_shape=jax.ShapeDtypeStruct((s, 512), BF16),
        in_specs=[_rows(tm, 512), _rows(tm, 512), pl.BlockSpec((m, 512), lambda i: (0, 0)),
                  pl.BlockSpec((m, 512), lambda i: (0, 1))],
        out_specs=_rows(tm, 512),
        compiler_params=_params(32),
    )(pmq, pmz, mkv, mkv)


def _mid(ya, yb, ym, pg, x, target, g_post, w_up, w_out):
    s = x.shape[0]
    tm = min(256, s)
    nt = s // tm

    def body(ya_ref, yb_ref, ym_ref, pg_ref, x_ref, t_ref, gp_ref, wup_hbm, wout_hbm,
             dg_ref, dya_ref, dyb_ref, dym_ref, dy_ref, loss_ref, ggp_ref, mb_ref, dob_ref, du_ref,
             wup_vm, wout_vm, sems):
        i = pl.program_id(0)
        _load_once([(wup_hbm.at[d], wup_vm.at[:, pl.ds(128 * d, 128)]) for d in range(N_DEV)]
                   + [(wout_hbm, wout_vm)], sems)

        @pl.when(i == 0)
        def _():
            loss_ref[...] = jnp.zeros_like(loss_ref)
            ggp_ref[...] = jnp.zeros_like(ggp_ref)

        ys = (ya_ref[...], yb_ref[...], ym_ref[...])
        us = [_dot(ys[k], wup_vm[512 * k:512 * k + 512, :]) for k in range(3)]
        gates = [_sigmoid(pg_ref[:, 1024 * k:1024 * k + 1024].astype(F32)) for k in range(3)]
        merged = gates[0] * us[0] + gates[1] * us[1] + gates[2] * us[2]
        mb = merged.astype(BF16)
        mb_ref[...] = mb
        out = _dot(mb, wout_vm[...])
        r = lax.rsqrt(jnp.mean(out * out, axis=-1, keepdims=True) + EPS)
        on = out * r
        gp = gp_ref[...]
        err = (x_ref[...] + on * gp) - t_ref[...]
        loss_ref[...] += 0.5 * jnp.sum(err * err) * (1.0 / D_MODEL)
        dy = err * (1.0 / D_MODEL)
        dy_ref[...] = dy
        ggp_ref[...] += jnp.sum(dy * on, axis=0, keepdims=True)
        a = dy * gp
        d_out = r * (a - on * jnp.mean(a * on, axis=-1, keepdims=True))
        dob = d_out.astype(BF16)
        dob_ref[...] = dob
        d_merged = _dot_nt(dob, wout_vm[...])
        d_refs = (dya_ref, dyb_ref, dym_ref)
        for k in range(3):
            g = gates[k]
            dg_ref[:, 1024 * k:1024 * k + 1024] = (d_merged * us[k] * g * (1.0 - g)).astype(BF16)
            du = (d_merged * g).astype(BF16)
            du_ref[k] = du
            d_refs[k][...] = _dot_nt(du, wup_vm[512 * k:512 * k + 512, :])

    return pl.pallas_call(
        body, name="mid", grid=(nt,),
        out_shape=[jax.ShapeDtypeStruct((s, 3072), BF16)] + [jax.ShapeDtypeStruct((s, 512), F32)] * 3
        + [jax.ShapeDtypeStruct((s, D_MODEL), F32), jax.ShapeDtypeStruct((8, 128), F32),
           jax.ShapeDtypeStruct((1, D_MODEL), F32), jax.ShapeDtypeStruct((s, D_MODEL), BF16),
           jax.ShapeDtypeStruct((s, D_MODEL), BF16), jax.ShapeDtypeStruct((3, s, D_MODEL), BF16)],
        in_specs=[_rows(tm, 512)] * 3 + [_rows(tm, 3072), _rows(tm, D_MODEL), _rows(tm, D_MODEL),
                                         _full((1, D_MODEL)), ANY, ANY],
        out_specs=[_rows(tm, 3072)] + [_rows(tm, 512)] * 3
        + [_rows(tm, D_MODEL), _full((8, 128)), _full((1, D_MODEL)), _rows(tm, D_MODEL), _rows(tm, D_MODEL),
           pl.BlockSpec((3, tm, D_MODEL), lambda i: (0, i, 0))],
        scratch_shapes=[pltpu.VMEM((1536, D_MODEL), BF16), pltpu.VMEM((D_MODEL, D_MODEL), BF16),
                        pltpu.SemaphoreType.DMA((N_DEV + 1,))],
        compiler_params=_params(56),
    )(ya, yb, ym, pg, x, target, g_post, w_up, w_out)


def _gw_mid(mb, dob, ys, du):
    s = mb.shape[0]
    tn = 256

    def out_body(mb_ref, dob_ref, o_ref):
        o_ref[...] = _dot_tn(mb_ref[...], dob_ref[...]).astype(BF16)

    gw_out = pl.pallas_call(
        out_body, name="gw_out", grid=(D_MODEL // tn,),
        out_shape=jax.ShapeDtypeStruct((D_MODEL, D_MODEL), BF16),
        in_specs=[pl.BlockSpec((s, tn), lambda j: (0, j)), _full((s, D_MODEL))],
        out_specs=pl.BlockSpec((tn, D_MODEL), lambda j: (j, 0)),
        compiler_params=_params(48),
    )(mb, dob)

    per = 512 // tn

    def up_body(ya_ref, yb_ref, ym_ref, du_ref, o_ref):
        j = pl.program_id(0)
        for k, y_ref in enumerate((ya_ref, yb_ref, ym_ref)):
            @pl.when(j // per == k)
            def _(y_ref=y_ref):
                res = _dot_tn(y_ref[...], du_ref[...])
                for d in range(N_DEV):
                    o_ref[d] = res[:, 128 * d:128 * d + 128].astype(BF16)

    def y_spec(k):
        return pl.BlockSpec((s, tn), lambda j: (0, jnp.clip(j - per * k, 0, per - 1)))

    gw_up = pl.pallas_call(
        up_body, name="gw_up", grid=(3 * per,),
        out_shape=jax.ShapeDtypeStruct((N_DEV, 1536, 128), BF16),
        in_specs=[y_spec(0), y_spec(1), y_spec(2), pl.BlockSpec((None, s, D_MODEL), lambda j: (j // per, 0, 0))],
        out_specs=pl.BlockSpec((N_DEV, tn, 128), lambda j: (0, j, 0)),
        compiler_params=_params(48),
    )(*ys, du)
    return gw_out, gw_up


def _conv_bwd(pa, dya, w_conv):
    s = pa.shape[0]
    tm = min(512, s)
    nt = s // tm

    def body(pa_ref, pp_ref, pn_ref, d_ref, dp_ref, dn_ref, w_ref, da_ref, gw_ref):
        i = pl.program_id(0)
        first, last = i == 0, i == nt - 1

        @pl.when(first)
        def _():
            gw_ref[...] = jnp.zeros_like(gw_ref)

        w = w_ref[...]
        prev_row = pp_ref[...].astype(F32)[15:16, :]
        next_row = pn_ref[...].astype(F32)[0:1, :]
        b, c, u, z, cu, cu_m1, cu_p1, y, sig, row = _conv_common(
            pa_ref[...].astype(F32), prev_row, next_row, w, first, last, tm)
        sz = z * sig
        dya_t = d_ref[...]
        d_y = dya_t * b * sz

        def halo_dy(p_row, d_row):
            zz = p_row[:, 1536:2048]
            return d_row * p_row[:, 0:512] * (zz * _sigmoid(zz))

        dy_prev = jnp.where(first, 0.0, halo_dy(prev_row, dp_ref[7:8, :]))
        dy_next = jnp.where(last, 0.0, halo_dy(next_row, dn_ref[0:1, :]))
        dy_m1 = jnp.where(row == 0, dy_prev, pltpu.roll(d_y, 1, 0))
        dy_p1 = jnp.where(row == tm - 1, dy_next, pltpu.roll(d_y, tm - 1, 0))
        d_cu = dy_p1 * w[0:1] + d_y * w[1:2] + dy_m1 * w[2:3]
        da_ref[:, 0:512] = (dya_t * y * sz).astype(BF16)
        da_ref[:, 512:1024] = (d_cu * u).astype(BF16)
        da_ref[:, 1024:1536] = (d_cu * c).astype(BF16)
        da_ref[:, 1536:2048] = (dya_t * b * y * (sig * (1.0 + z * (1.0 - sig)))).astype(BF16)
        gw_ref[0:1, :] += jnp.sum(d_y * cu_m1, axis=0, keepdims=True)
        gw_ref[1:2, :] += jnp.sum(d_y * cu, axis=0, keepdims=True)
        gw_ref[2:3, :] += jnp.sum(d_y * cu_p1, axis=0, keepdims=True)

    prev, nxt = _halo_specs(s, tm, 16, 2048)
    dprev, dnxt = _halo_specs(s, tm, 8, 512)
    return pl.pallas_call(
        body, name="conv_bwd", grid=(nt,),
        out_shape=[jax.ShapeDtypeStruct((s, 2048), BF16), jax.ShapeDtypeStruct((8, 512), F32)],
        in_specs=[_rows(tm, 2048), prev, nxt, _rows(tm, 512), dprev, dnxt, _full((3, 512))],
        out_specs=[_rows(tm, 2048), _full((8, 512))],
        compiler_params=_params(48),
    )(pa, pa, pa, dya, dya, dya, w_conv)


def _attn_bwd(pq, pkv, pbz, dyb, sink, tabs):
    s = pq.shape[0]
    nb = s // ATTN_BLOCK

    def body(sink_ref, q_ref, z_ref, d_ref, cs_ref, s1_ref, s2_ref, kv_ref, csf_ref, s1f_ref, s2f_ref,
             dq_ref, dz_ref, dkv_ref, gs_ref, kpad, vpad, dk_acc, dv_acc, bias, e_scr, ds_scr):
        n = pl.program_id(0)

        @pl.when(n == 0)
        def _():
            _fill_padded(kv_ref, kpad, vpad, s)
            _fill_band_bias(bias, nb)
            dk_acc[...] = jnp.zeros_like(dk_acc)
            dv_acc[...] = jnp.zeros_like(dv_acc)
            gs_ref[...] = jnp.zeros_like(gs_ref)

        row = lax.broadcasted_iota(jnp.int32, (ATTN_BLOCK, 128), 0)
        start = pl.multiple_of(n * ATTN_BLOCK, ATTN_BLOCK)
        kw, vw = kpad[pl.ds(start, WINDOW_KEYS), :], vpad[pl.ds(start, WINDOW_KEYS), :]
        qf = q_ref[...].astype(F32)
        variant = _bias_variant(n, nb)
        z = z_ref[...].astype(F32)
        sig = _sigmoid(z)
        dyb_t = d_ref[...]
        d_attn = dyb_t * (z * sig)
        outs, dqs = [], []
        dk_w = jnp.zeros((WINDOW_KEYS, 128), F32)
        dv_w = jnp.zeros((WINDOW_KEYS, 128), F32)
        for g in range(2):
            qt = _heads_to_lanes(qf, g, row)
            inv, p_sink = _softmax_keys_major(
                _dot(kw, qt.astype(BF16)), bias, variant, _sink_row(sink_ref, g), e_scr)
            ot = _dot_tn(vw, e_scr[...]) * inv
            outs.append(ot)
            dot_ = _heads_to_lanes(d_attn, g, row)
            delta = jnp.sum(dot_ * ot, axis=0, keepdims=True)
            dpt = _dot(vw, dot_.astype(BF16))
            for k in range(WINDOW_KEYS // KEY_CHUNK):
                rw = slice(k * KEY_CHUNK, (k + 1) * KEY_CHUNK)
                ds_scr[rw, :] = (e_scr[rw, :].astype(F32) * (dpt[rw] - delta)).astype(BF16)
            sink_part = p_sink * delta
            for j in range(4):
                h = 4 * g + j
                gs_ref[h:h + 1, :] -= jnp.sum(sink_part[:, 128 * j:128 * j + 128])
            dqs.append(_dot_tn(kw, ds_scr[...]) * (inv * ATTN_SCALE))
            dk_w += _dot_nt(ds_scr[...], (qt * inv).astype(BF16)) * ATTN_SCALE
            dv_w += _dot_nt(e_scr[...], (dot_ * inv).astype(BF16))
        dk_acc[pl.ds(start, WINDOW_KEYS), :] += dk_w
        dv_acc[pl.ds(start, WINDOW_KEYS), :] += dv_w
        attn = _lanes_to_heads(outs[0], outs[1], row)
        dz_ref[...] = (dyb_t * attn * (sig * (1.0 + z * (1.0 - sig)))).astype(BF16)
        dq = _lanes_to_heads(dqs[0], dqs[1], row)
        cs, s1, s2 = cs_ref[...], s1_ref[...], s2_ref[...]
        for b in range(4):
            dq_ref[:, 128 * b:128 * b + 128] = _rope_t(dq[:, 128 * b:128 * b + 128], cs, s1, s2).astype(BF16)

        @pl.when(n == nb - 1)
        def _():
            dk = dk_acc[ATTN_BLOCK:ATTN_BLOCK + s, :]
            dkv_ref[:, 0:128] = _rope_t(dk, csf_ref[...], s1f_ref[...], s2f_ref[...]).astype(BF16)
            dkv_ref[:, 128:256] = dv_acc[ATTN_BLOCK:ATTN_BLOCK + s, :].astype(BF16)

    tile = _rows(ATTN_BLOCK, 512)
    tab = _rows(ATTN_BLOCK, 128)
    return pl.pallas_call(
        body, name="attn_bwd", grid=(nb,),
        out_shape=[jax.ShapeDtypeStruct((s, 512), BF16), jax.ShapeDtypeStruct((s, 512), BF16),
                   jax.ShapeDtypeStruct((s, 256), BF16), jax.ShapeDtypeStruct((8, 128), F32)],
        in_specs=[pl.BlockSpec(memory_space=pltpu.SMEM), tile, tile, tile, tab, tab, tab,
                  _full((s, 256)), _full((s, 128)), _full((s, 128)), _full((s, 128))],
        out_specs=[tile, tile, _full((s, 256)), _full((8, 128))],
        scratch_shapes=[pltpu.VMEM((s + 2 * ATTN_BLOCK, 128), BF16)] * 2
        + [pltpu.VMEM((s + 2 * ATTN_BLOCK, 128), F32)] * 2
        + [pltpu.VMEM((3, WINDOW_KEYS, STACKED), F32)] + [pltpu.VMEM((WINDOW_KEYS, STACKED), BF16)] * 2,
        compiler_params=_params(48),
    )(sink, pq, pbz, dyb, *tabs, pkv, *tabs)


def _mem_attn_bwd(pmq, pmz, mkv, dym):
    s = pmq.shape[0]
    m = mkv.shape[0]
    tm = min(512, s)

    def body(q_ref, z_ref, d_ref, mk_ref, mv_ref, dq_ref, dz_ref, dmkv_ref):
        @pl.when(pl.program_id(0) == 0)
        def _():
            dmkv_ref[...] = jnp.zeros_like(dmkv_ref)

        z = z_ref[...].astype(F32)
        sig = _sigmoid(z)
        dym_t = d_ref[...]
        d_attn = dym_t * (z * sig)
        dsilu = sig * (1.0 + z * (1.0 - sig))
        for h in range(MEM_HEADS):
            cols = slice(128 * h, 128 * h + 128)
            q, mk, mv = q_ref[:, cols], mk_ref[:, cols], mv_ref[:, cols]
            pt = _mem_softmax_t(q, mk)
            pb = pt.astype(BF16)
            o = _dot_tn(pb, mv)
            dob = d_attn[:, cols].astype(BF16)
            dpt = _dot_nt(mv, dob)
            dst = (pt * (dpt - jnp.sum(pt * dpt, axis=0, keepdims=True))).astype(BF16)
            dq_ref[:, cols] = (_dot_tn(dst, mk) * MEM_SCALE).astype(BF16)
            dz_ref[:, cols] = (dym_t[:, cols] * o * dsilu[:, cols]).astype(BF16)
            dmkv_ref[:, cols] += _dot(dst, q) * MEM_SCALE
            dmkv_ref[:, 512 + 128 * h:512 + 128 * h + 128] += _dot(pb, dob)

    return pl.pallas_call(
        body, name="mem_attn_bwd", grid=(s // tm,),
        out_shape=[jax.ShapeDtypeStruct((s, 512), BF16), jax.ShapeDtypeStruct((s, 512), BF16),
                   jax.ShapeDtypeStruct((m, D_MODEL), F32)],
        in_specs=[_rows(tm, 512), _rows(tm, 512), _rows(tm, 512), pl.BlockSpec((m, 512), lambda i: (0, 0)),
                  pl.BlockSpec((m, 512), lambda i: (0, 1))],
        out_specs=[_rows(tm, 512), _rows(tm, 512), _full((m, D_MODEL))],
        compiler_params=_params(32),
    )(pmq, pmz, dym, mkv, mkv)


def _mem_kv_bwd(mem, g_mem, mn, dmkv, w_mkv):
    m = mem.shape[0]

    def body(mem_ref, g_ref, mn_ref, d_ref, w_ref, gw_ref, gg_ref):
        db = d_ref[...].astype(BF16)
        gw_ref[...] = _dot_tn(mn_ref[...], db).astype(BF16)
        d_mn = _dot_nt(db, w_ref[...])
        xf = mem_ref[...]
        r = lax.rsqrt(jnp.mean(xf * xf, axis=-1, keepdims=True) + EPS)
        gg_ref[...] = jnp.sum(d_mn * (xf * r), axis=0, keepdims=True)

    return pl.pallas_call(
        body, name="mem_kv_bwd", grid=(1,),
        out_shape=[jax.ShapeDtypeStruct((D_MODEL, D_MODEL), BF16), jax.ShapeDtypeStruct((1, D_MODEL), F32)],
        in_specs=[_full((m, D_MODEL)), _full((1, D_MODEL)), _full((m, D_MODEL)), _full((m, D_MODEL)),
                  _full((D_MODEL, D_MODEL))],
        out_specs=[_full((D_MODEL, D_MODEL)), _full((1, D_MODEL))],
        compiler_params=_params(32),
    )(mem, g_mem, mn, dmkv, w_mkv)


def _dh_bwd(dparts, x, dy, g_pre, w_int):
    s = x.shape[0]
    tm = min(256, s)

    def body(*refs):
        d_refs = refs[:7]
        x_ref, dy_ref, g_ref, w_hbm, gx_ref, gg_ref, w_vm, sems = refs[7:]
        _load_once([(w_hbm, w_vm)], sems)

        @pl.when(pl.program_id(0) == 0)
        def _():
            gg_ref[...] = jnp.zeros_like(gg_ref)

        d_h = jnp.zeros((tm, D_MODEL), F32)
        for d_ref, (r0, width) in zip(d_refs, SEGS):
            for c0 in range(0, width, 512):
                cw = min(512, width - c0)
                d_h += _dot(d_ref[:, c0:c0 + cw], w_vm[r0 + c0:r0 + c0 + cw, :])
        xf = x_ref[...]
        r = lax.rsqrt(jnp.mean(xf * xf, axis=-1, keepdims=True) + EPS)
        xn = xf * r
        a = d_h * g_ref[...]
        gx_ref[...] = r * (a - xn * jnp.mean(a * xn, axis=-1, keepdims=True)) + dy_ref[...]
        gg_ref[...] += jnp.sum(d_h * xn, axis=0, keepdims=True)

    return pl.pallas_call(
        body, name="dh_bwd", grid=(s // tm,),
        out_shape=[jax.ShapeDtypeStruct((s, D_MODEL), F32), jax.ShapeDtypeStruct((1, D_MODEL), F32)],
        in_specs=[_rows(tm, w) for _, w in SEGS] + [_rows(tm, D_MODEL), _rows(tm, D_MODEL), _full((1, D_MODEL)), ANY],
        out_specs=[_rows(tm, D_MODEL), _full((1, D_MODEL))],
        scratch_shapes=[pltpu.VMEM((IN_WIDTH, D_MODEL), BF16), pltpu.SemaphoreType.DMA((1,))],
        compiler_params=_params(52),
    )(*dparts, x, dy, g_pre, w_int)


def _gw_in(dparts, h):
    s = h.shape[0]
    tn = 256
    starts, counts = [], []
    for r0, width in SEGS:
        starts.append(r0 // tn)
        counts.append(width // tn)

    def body(*refs):
        d_refs = refs[:7]
        h_hbm, o_ref, h_vm, sems = refs[7:]
        _load_once([(h_hbm, h_vm)], sems)
        j = pl.program_id(0)
        for d_ref, st, cnt in zip(d_refs, starts, counts):
            @pl.when((j >= st) & (j < st + cnt))
            def _(d_ref=d_ref):
                o_ref[...] = _dot_tn(d_ref[...], h_vm[...]).astype(BF16)

    def seg_spec(st, cnt):
        return pl.BlockSpec((s, tn), lambda j: (0, jnp.clip(j - st, 0, cnt - 1)))

    return pl.pallas_call(
        body, name="gw_in", grid=(IN_WIDTH // tn,),
        out_shape=jax.ShapeDtypeStruct((IN_WIDTH, D_MODEL), BF16),
        in_specs=[seg_spec(st, cnt) for st, cnt in zip(starts, counts)] + [ANY],
        out_specs=pl.BlockSpec((tn, D_MODEL), lambda j: (j, 0)),
        scratch_shapes=[pltpu.VMEM((s, D_MODEL), BF16), pltpu.SemaphoreType.DMA((1,))],
        compiler_params=_params(52),
    )(*dparts, h)


def _adamw_math(w, g, m, v):
    m2 = ADAM_B1 * m + (1.0 - ADAM_B1) * g
    v2 = ADAM_B2 * v + (1.0 - ADAM_B2) * (g * g)
    m_hat = m2 / (1.0 - ADAM_B1 ** ADAM_STEP)
    v_hat = v2 / (1.0 - ADAM_B2 ** ADAM_STEP)
    delta = -ADAM_LR * (m_hat / (jnp.sqrt(v_hat) + ADAM_EPS) + ADAM_WD * w)
    return delta, m2, v2


def _sum_adamw(own, land, chip, block, w, m, v, name, tiles=1):
    r, c = w.shape
    rt = r // tiles

    def body(c_ref, own_ref, l1_ref, l2_ref, l3_ref, w_ref, m_ref, v_ref, g_ref, d_ref, m2_ref, v2_ref):
        g = own_ref[...].astype(F32)
        for l_ref in (l1_ref, l2_ref, l3_ref):
            g += l_ref[...].astype(F32)
        g_ref[...] = g
        d_ref[...], m2_ref[...], v2_ref[...] = _adamw_math(w_ref[...], g, m_ref[...], v_ref[...])

    def share(k):
        return pl.BlockSpec((None, rt, c), lambda i, c_ref: (jnp.bitwise_xor(c_ref[0], k), block * tiles + i, 0))

    spec = pl.BlockSpec((rt, c), lambda i, c_ref: (i, 0))
    grid_spec = pltpu.PrefetchScalarGridSpec(
        num_scalar_prefetch=1, grid=(tiles,),
        in_specs=[share(0), share(1), share(2), share(3)] + [spec] * 3, out_specs=[spec] * 4)
    return pl.pallas_call(
        body, name=name, grid_spec=grid_spec,
        out_shape=[jax.ShapeDtypeStruct((r, c), F32)] * 4,
        compiler_params=_params(48),
    )(chip, own, land, land, land, w, m, v)


def _pack_sum(packs):
    def body(p_ref, o_ref):
        acc = p_ref[0]
        for k in range(1, N_DEV):
            acc += p_ref[k]
        o_ref[...] = acc

    return pl.pallas_call(
        body, name="pack_sum", grid=(1,),
        out_shape=jax.ShapeDtypeStruct((8, D_MODEL), F32),
        in_specs=[_full((N_DEV, 8, D_MODEL))], out_specs=_full((8, D_MODEL)),
    )(packs)


def _small_adamw(ws, gs, ms, vs):
    k = len(ws)

    def body(*refs):
        w_refs, g_refs, m_refs, v_refs = (refs[j * k:(j + 1) * k] for j in range(4))
        outs = refs[4 * k:]
        for j in range(k):
            outs[j][...], outs[k + j][...], outs[2 * k + j][...] = _adamw_math(
                w_refs[j][...], g_refs[j][...], m_refs[j][...], v_refs[j][...])

    specs = [_full(w.shape) for w in ws]
    res = pl.pallas_call(
        body, name="small_adamw", grid=(1,),
        out_shape=[jax.ShapeDtypeStruct(w.shape, F32) for w in ws] * 3,
        in_specs=specs * 4, out_specs=specs * 3,
    )(*ws, *gs, *ms, *vs)
    return res[:k], res[k:2 * k], res[2 * k:]


def kernel(x, mem, g_pre, w_in, w_conv, attn_sink, g_mem, w_mem_kv, w_up_a, w_up_b, w_up_m, w_out, g_post, loss_target, m_g_pre, m_w_in, m_w_conv, m_attn_sink, m_g_mem, m_w_mem_kv, m_w_up_a, m_w_up_b, m_w_up_m, m_w_out, m_g_post, v_g_pre, v_w_in, v_w_conv, v_attn_sink, v_g_mem, v_w_mem_kv, v_w_up_a, v_w_up_b, v_w_up_m, v_w_out, v_g_post):
    s = x.shape[1]
    x2, mem2, tgt2 = x[0], mem[0], loss_target[0]
    me = 4 * lax.axis_index("x") + 2 * lax.axis_index("y") + lax.axis_index("c")

    w_up_loc = jnp.concatenate([w_up_a[0], w_up_b[0], w_up_m[0]], axis=0).astype(BF16)
    w_conv_loc = jnp.zeros((8, 128), F32).at[:3, :64].set(w_conv[0])
    w_int_g, w_conv_g = _all_gather([w_in[0].T.astype(BF16), w_conv_loc], "gather_w_in")
    w_int = w_int_g.reshape(IN_WIDTH, D_MODEL)
    w_conv_f = w_conv_g[:, :3, :64].transpose(1, 0, 2).reshape(3, 512)
    late = _gather_start([w_mem_kv[0].astype(BF16) + w_conv_g[0, 7:8, 0:1].astype(BF16),
                          w_out[0].astype(BF16), w_up_loc], me, "gather_late_start")
    sink = attn_sink[0]
    tabs = _rope_tables(s)

    h, pa, pq, pkv, pbz, pmq, pmz, pg = _proj_fwd(x2, g_pre + late[4][0:1, 0:1], w_int, tabs)
    ya = _conv_fwd(pa, w_conv_f)
    yb = _attn_fwd(pq, pkv, pbz, sink)
    w_mkv_g, w_out_g, w_up_g = _gather_wait(*late[:4], yb, "gather_late_wait")
    w_mkv = w_mkv_g.reshape(D_MODEL, D_MODEL)
    w_out_f = w_out_g.reshape(D_MODEL, D_MODEL)
    mn, mkv = _mem_kv_fwd(mem2, g_mem, w_mkv)
    ym = _mem_attn_fwd(pmq, pmz, mkv)
    dg, dya, dyb, dym, dy, loss_p, gg_post, mb, dob, du = _mid(ya, yb, ym, pg, x2, tgt2, g_post, w_up_g, w_out_f)
    gw_out, gw_up = _gw_mid(mb, dob, (ya, yb, ym), du)

    core = lax.axis_index("c").astype(jnp.int32).reshape(1)
    chip = (2 * lax.axis_index("x") + lax.axis_index("y")).astype(jnp.int32).reshape(1)

    def exchange_start(shares, tag):
        from_sibling = _sibling_exchange(shares, "grads_to_sibling_" + tag)
        chip_shares = _pair_add(shares, from_sibling, core, "grads_pair_add_" + tag)
        return _chip_exchange_start(chip_shares, "grads_to_chips_start_" + tag)

    dmq, dmz, dmkv = _mem_attn_bwd(pmq, pmz, mkv, dym)
    gw_mkv, gg_mem = _mem_kv_bwd(mem2, g_mem, mn, dmkv, w_mkv)
    send1, recv1, srcs1, lands1, token1 = exchange_start(
        [gw_mkv.reshape(N_DEV, 128, D_MODEL), gw_out.reshape(N_DEV, 128, D_MODEL), gw_up], "small")
    da, gw_conv = _conv_bwd(pa, dya, w_conv_f + token1[0:1, 0:1])
    dq, dbz, dkv, g_sink = _attn_bwd(pq, pkv, pbz, dyb, sink, tabs)
    dparts = (da, dq, dkv, dbz, dmq, dmz, dg)
    gw_int = _gw_in(dparts, h)
    send2, recv2, srcs2, lands2, token2 = exchange_start([gw_int.reshape(N_DEV, SHARD_IN, D_MODEL)], "w_in")
    grad_x, gg_pre = _dh_bwd(dparts, x2, dy, g_pre + token2[0:1, 0:1], w_int)
    (o_mkv, o_out, o_up, o_int), (l_mkv, l_out, l_up, l_int) = _chip_exchange_wait(
        send1 + send2, recv1 + recv2, srcs1 + srcs2, lands1 + lands2, grad_x, "grads_to_chips_wait")

    row3 = jnp.concatenate([gw_conv[0:1], gw_conv[1:2]], axis=1)
    row4 = jnp.concatenate([gw_conv[2:3], g_sink[:, 0].reshape(1, 8), loss_p[0:1, 0:1],
                            jnp.zeros((1, 512 - 9), F32)], axis=1)
    pack = jnp.concatenate([gg_pre, gg_mem, gg_post, row3, row4, jnp.zeros((3, D_MODEL), F32)], axis=0)
    (packs,) = _all_gather([pack], "gather_small")
    tot = _pack_sum(packs)

    g_w_in, d_w_in, nm_w_in, nv_w_in = (t.T for t in _sum_adamw(
        o_int, l_int, chip, 0, w_in[0].T, m_w_in[0].T, v_w_in[0].T, "adamw_w_in", tiles=2))
    g_mkv, d_mkv, nm_mkv, nv_mkv = _sum_adamw(
        o_mkv, l_mkv, chip, 0, w_mem_kv[0], m_w_mem_kv[0], v_w_mem_kv[0], "adamw_w_mem_kv")
    g_out, d_out, nm_out, nv_out = _sum_adamw(o_out, l_out, chip, 0, w_out[0], m_w_out[0], v_w_out[0], "adamw_w_out")
    up = [_sum_adamw(o_up, l_up, chip, k, w[0], m[0], v[0], "adamw_w_up_" + "abm"[k])
          for k, (w, m, v) in enumerate([(w_up_a, m_w_up_a, v_w_up_a), (w_up_b, m_w_up_b, v_w_up_b),
                                         (w_up_m, m_w_up_m, v_w_up_m)])]

    g_g_pre, g_g_mem, g_g_post = tot[0:1], tot[1:2], tot[2:3]
    g_conv_full = jnp.concatenate([tot[3:4, 0:512], tot[3:4, 512:1024], tot[4:5, 0:512]], axis=0)
    g_conv = lax.dynamic_slice(g_conv_full, (0, 64 * me), (3, 64))
    g_sink_tot = tot[4:5, 512:520]
    loss = tot[4, 520]
    small_w = [g_pre, w_conv[0], attn_sink, g_mem, g_post]
    small_g = [g_g_pre, g_conv, g_sink_tot, g_g_mem, g_g_post]
    small_m = [m_g_pre, m_w_conv[0], m_attn_sink, m_g_mem, m_g_post]
    small_v = [v_g_pre, v_w_conv[0], v_attn_sink, v_g_mem, v_g_post]
    sd, sm, sv = _small_adamw(small_w, small_g, small_m, small_v)

    def lead(a):
        return a[None]

    grads = [g_g_pre, lead(g_w_in), lead(g_conv), g_sink_tot, g_g_mem, lead(g_mkv), lead(up[0][0]),
             lead(up[1][0]), lead(up[2][0]), lead(g_out), g_g_post]

    def assemble(small, big_in, big_mkv, big_up, big_out):
        return [small[0], lead(big_in), lead(small[1]), small[2], small[3], lead(big_mkv), lead(big_up[0]),
                lead(big_up[1]), lead(big_up[2]), lead(big_out), small[4]]

    deltas = assemble(sd, d_w_in, d_mkv, [u[1] for u in up], d_out)
    new_m = assemble(sm, nm_w_in, nm_mkv, [u[2] for u in up], nm_out)
    new_v = assemble(sv, nv_w_in, nv_mkv, [u[3] for u in up], nv_out)
    return (loss, grad_x[None], *grads, *deltas, *new_m, *new_v)
```

```python
import functools

import jax
import jax.numpy as jnp
from jax import lax
from jax.experimental import pallas as pl
from jax.experimental.pallas import tpu as pltpu

F32 = jnp.float32
BF16 = jnp.bfloat16
MESH = pl.DeviceIdType.MESH

N_DEV = 8
D_MODEL = 1024
EPS = 1e-6
ROPE_THETA = 500000.0
ROT_DIM = 16
HEAD_DIM = 64
ATTN_BLOCK = 128
MEM_HEADS = 4
MEM_HEAD_DIM = 128
ATTN_SCALE = HEAD_DIM ** -0.5
MEM_SCALE = MEM_HEAD_DIM ** -0.5

ADAM_LR = 0.001
ADAM_B1 = 0.9
ADAM_B2 = 0.999
ADAM_EPS = 1e-08
ADAM_WD = 0.01
ADAM_STEP = 10

SEG_A = (0, 2048)
SEG_BQ = (2048, 512)
SEG_BKV = (2560, 256)
SEG_BZ = (2816, 512)
SEG_MQ = (3328, 512)
SEG_MZ = (3840, 512)
SEG_G = (4352, 3072)
SEGS = (SEG_A, SEG_BQ, SEG_BKV, SEG_BZ, SEG_MQ, SEG_MZ, SEG_G)
IN_WIDTH = 7424
SHARD_IN = IN_WIDTH // N_DEV

V7X_VMEM_BYTES = 64 * 1024 * 1024
ANY = pl.BlockSpec(memory_space=pl.ANY)


def _params(vmem_mb):
    assert vmem_mb * 1024 * 1024 < V7X_VMEM_BYTES
    return pltpu.CompilerParams(dimension_semantics=("arbitrary",), vmem_limit_bytes=vmem_mb * 1024 * 1024)


def _full(shape):
    zeros = (0,) * len(shape)
    return pl.BlockSpec(shape, lambda i: zeros)


def _rows(tm, width):
    return pl.BlockSpec((tm, width), lambda i: (i, 0))


def _dot(a, b):
    return jnp.dot(a, b, preferred_element_type=F32)


def _dot_nt(a, b):
    return lax.dot_general(a, b, (((1,), (1,)), ((), ())), preferred_element_type=F32)


def _dot_tn(a, b):
    return lax.dot_general(a, b, (((0,), (0,)), ((), ())), preferred_element_type=F32)


def _sigmoid(z):
    return 1.0 / (1.0 + jnp.exp(-z))


def _rope(t, cs, s1, s2):
    return t * cs + pltpu.roll(t, 120, 1) * s1 + pltpu.roll(t, 8, 1) * s2


def _rope_t(d, cs, s1, s2):
    return d * cs + pltpu.roll(d * s1, 8, 1) + pltpu.roll(d * s2, 120, 1)


def _rope_tables(s):
    half = ROT_DIM // 2
    inv_freq = jnp.power(jnp.float32(ROPE_THETA), -jnp.arange(half, dtype=F32) * (2.0 / ROT_DIM))
    d = jnp.arange(128) % HEAD_DIM
    ang = jnp.arange(s).astype(F32)[:, None] * inv_freq[d % half][None, :]
    cos, sin = jnp.cos(ang), jnp.sin(ang)
    lo, hi = (d < half)[None, :], ((d >= half) & (d < ROT_DIM))[None, :]
    return jnp.where(lo | hi, cos, 1.0), jnp.where(lo, -sin, 0.0), jnp.where(hi, sin, 0.0)


def _load_once(pairs, sems):
    @pl.when(pl.program_id(0) == 0)
    def _():
        cps = [pltpu.make_async_copy(src, dst, sems.at[k]) for k, (src, dst) in enumerate(pairs)]
        for cp in cps:
            cp.start()
        for cp in cps:
            cp.wait()


def _my_place():
    x, y, c = lax.axis_index("x"), lax.axis_index("y"), lax.axis_index("c")
    return x, y, c


def _all_gather(arrs, name):
    n = len(arrs)

    def body(*refs):
        ins, outs = refs[:n], refs[n:2 * n]
        send_sems, recv_sems, local_sems = refs[2 * n:]
        x, y, c = _my_place()
        me, sibling = (x, y, c), (x, y, 1 - c)

        def route(core):
            first = (jnp.bitwise_xor(x, 1 - core), jnp.bitwise_xor(y, core), core)
            second = (jnp.bitwise_xor(x, core), jnp.bitwise_xor(y, 1 - core), core)
            return first, second, (1 - x, 1 - y, core)

        def idx(px, py, pc):
            return 4 * px + 2 * py + pc

        def copy(a, k, block, to, src=None):
            dst = outs[a].at[idx(*block)]
            return pltpu.make_async_remote_copy(
                src_ref=dst if src is None else src, dst_ref=dst,
                send_sem=send_sems.at[a * 7 + k], recv_sem=recv_sems.at[a * 7 + k],
                device_id=to, device_id_type=MESH)

        nbr1, nbr2, diag = route(c)
        mine = [pltpu.make_async_copy(ins[a], outs[a].at[idx(*me)], local_sems.at[a]) for a in range(n)]
        for cp in mine:
            cp.start()
        sent = []
        for k, to in enumerate((sibling, nbr1, nbr2)):
            for a in range(n):
                sent.append(copy(a, k, me, to, src=ins[a]))
        for cp in sent:
            cp.start()
        for k_in, block, onward in ((1, nbr1, ((3, nbr2), (4, sibling))), (2, nbr2, ((5, sibling),)),
                                    (3, diag, ((6, sibling),))):
            for a in range(n):
                copy(a, k_in, block, me).wait_recv()
                for k_out, to in onward:
                    cp = copy(a, k_out, block, to)
                    cp.start()
                    sent.append(cp)
        s1, s2, sd = route(1 - c)
        for k_in, block in ((0, sibling), (4, s1), (5, s2), (6, sd)):
            for a in range(n):
                copy(a, k_in, block, me).wait_recv()
        for cp in sent:
            cp.wait_send()
        for cp in mine:
            cp.wait()

    return pl.pallas_call(
        body, name=name,
        out_shape=[jax.ShapeDtypeStruct((N_DEV,) + a.shape, a.dtype) for a in arrs],
        in_specs=[ANY] * n, out_specs=[ANY] * n,
        scratch_shapes=[pltpu.SemaphoreType.DMA((7 * n,)), pltpu.SemaphoreType.DMA((7 * n,)),
                        pltpu.SemaphoreType.DMA((n,))],
    )(*arrs)


N_CHIPS = 4


def _sibling_exchange(arrs, name):
    n = len(arrs)

    def body(*refs):
        ins, outs = refs[:n], refs[n:2 * n]
        send_sems, recv_sems = refs[2 * n:]
        x, y, c = _my_place()
        sibling = (x, y, 1 - c)

        def copy(a, j):
            return pltpu.make_async_remote_copy(
                src_ref=ins[a].at[2 * j + (1 - c)], dst_ref=outs[a].at[j],
                send_sem=send_sems.at[a * N_CHIPS + j], recv_sem=recv_sems.at[a * N_CHIPS + j],
                device_id=sibling, device_id_type=MESH)

        cps = [copy(a, j) for j in range(N_CHIPS) for a in range(n)]
        for cp in cps:
            cp.start()
        for cp in cps:
            cp.wait_recv()
        for cp in cps:
            cp.wait_send()

    return pl.pallas_call(
        body, name=name,
        out_shape=[jax.ShapeDtypeStruct((N_CHIPS,) + a.shape[1:], a.dtype) for a in arrs],
        in_specs=[ANY] * n, out_specs=[ANY] * n,
        scratch_shapes=[pltpu.SemaphoreType.DMA((N_CHIPS * n,)), pltpu.SemaphoreType.DMA((N_CHIPS * n,))],
    )(*arrs)


def _pair_add(mine, recv, core, name):
    n = len(mine)

    def body(c_ref, *refs):
        for a in range(n):
            refs[2 * n + a][...] = (refs[a][...].astype(F32) + refs[n + a][...].astype(F32)).astype(BF16)

    def blk(a):
        return (None,) + a.shape[1:]

    grid_spec = pltpu.PrefetchScalarGridSpec(
        num_scalar_prefetch=1, grid=(N_CHIPS,),
        in_specs=[pl.BlockSpec(blk(a), lambda j, c_ref: (2 * j + c_ref[0], 0, 0)) for a in mine]
        + [pl.BlockSpec(blk(a), lambda j, c_ref: (j, 0, 0)) for a in recv],
        out_specs=[pl.BlockSpec(blk(a), lambda j, c_ref: (j, 0, 0)) for a in recv])
    return pl.pallas_call(
        body, name=name, grid_spec=grid_spec,
        out_shape=[jax.ShapeDtypeStruct(a.shape, BF16) for a in recv],
        compiler_params=_params(32),
    )(core, *mine, *recv)


HBM = pl.BlockSpec(memory_space=pltpu.HBM)
SEM = pl.BlockSpec(memory_space=pltpu.SEMAPHORE)
N_PEER_CHIPS = 3


def _chip_copies(srcs, lands, send_sems, recv_sems):
    x, y, c = _my_place()
    my_chip = 2 * x + y
    peers = [(x, 1 - y), (1 - x, y), (1 - x, 1 - y)]
    cps = []
    for k, (px, py) in enumerate(peers):
        for a in range(len(srcs)):
            j = a * N_PEER_CHIPS + k
            cps.append(pltpu.make_async_remote_copy(
                src_ref=srcs[a].at[2 * px + py], dst_ref=lands[a].at[my_chip],
                send_sem=send_sems[j], recv_sem=recv_sems[j],
                device_id=(px, py, c), device_id_type=MESH))
    return cps


N_PEERS = N_DEV - 1


def _gather_copies(srcs, lands, send_sems, recv_sems):
    x, y, c = _my_place()
    me_idx = 4 * x + 2 * y + c
    flips = [(0, 0, 1), (0, 1, 0), (1, 0, 0), (0, 1, 1), (1, 0, 1), (1, 1, 0), (1, 1, 1)]
    cps = []
    for k, (fx, fy, fc) in enumerate(flips):
        peer = ((1 - x) if fx else x, (1 - y) if fy else y, (1 - c) if fc else c)
        for a in range(len(srcs)):
            j = a * N_PEERS + k
            cps.append(pltpu.make_async_remote_copy(
                src_ref=srcs[a], dst_ref=lands[a].at[me_idx], send_sem=send_sems[j], recv_sem=recv_sems[j],
                device_id=peer, device_id_type=MESH))
    return cps


def _split_start(copies, per_array, arrs, lands, name):
    arrs, lands = list(arrs), list(lands)
    n = len(arrs)
    k = n * per_array

    def body(*refs):
        srcs, land_refs = refs[:n], refs[n:2 * n]
        send_sems, recv_sems = refs[2 * n:2 * n + k], refs[2 * n + k:2 * n + 2 * k]
        token = refs[-1]
        for cp in copies(srcs, land_refs, send_sems, recv_sems):
            cp.start()
        token[...] = jnp.zeros_like(token)

    hbm_arrs = [pltpu.with_memory_space_constraint(a, pltpu.HBM) for a in arrs]
    lands = [pltpu.with_memory_space_constraint(a, pltpu.HBM) for a in lands]
    res = pl.pallas_call(
        body, name=name,
        out_shape=[pltpu.SemaphoreType.DMA(())] * (2 * k) + [pltpu.HBM(a.shape, a.dtype) for a in arrs + lands]
        + [jax.ShapeDtypeStruct((8, 128), F32)],
        in_specs=[HBM] * (2 * n),
        out_specs=[SEM] * (2 * k) + [HBM] * (2 * n) + [pl.BlockSpec(memory_space=pltpu.VMEM)],
        input_output_aliases={a: 2 * k + a for a in range(2 * n)},
        compiler_params=pltpu.CompilerParams(has_side_effects=pltpu.SideEffectType.DATAFLOW_SIDE_EFFECTING),
    )(*hbm_arrs, *lands)
    return res[:k], res[k:2 * k], res[2 * k:2 * k + n], res[2 * k + n:2 * k + 2 * n], res[-1]


def _split_wait(copies, per_array, send_sems, recv_sems, srcs, lands, after, name):
    n = len(srcs)
    k = n * per_array

    def body(*refs):
        src_refs, land_refs = refs[:n], refs[n:2 * n]
        s_sems, r_sems = refs[2 * n:2 * n + k], refs[2 * n + k:2 * n + 2 * k]
        for cp in copies(src_refs, land_refs, s_sems, r_sems):
            cp.wait_send()
            cp.wait_recv()

    res = pl.pallas_call(
        body, name=name,
        out_shape=[pltpu.HBM(a.shape, a.dtype) for a in list(srcs) + list(lands)],
        in_specs=[HBM] * (2 * n) + [SEM] * (2 * k) + [ANY],
        out_specs=[HBM] * (2 * n),
        input_output_aliases={a: a for a in range(2 * n)},
        compiler_params=pltpu.CompilerParams(has_side_effects=pltpu.SideEffectType.DATAFLOW_SIDE_EFFECTING),
    )(*srcs, *lands, *send_sems, *recv_sems, after)
    return res[:n], res[n:]


def _chip_exchange_start(arrs, name):
    return _split_start(_chip_copies, N_PEER_CHIPS, arrs, [lax.empty(a.shape, a.dtype) for a in arrs], name)


def _chip_exchange_wait(send_sems, recv_sems, srcs, lands, after, name):
    return _split_wait(_chip_copies, N_PEER_CHIPS, send_sems, recv_sems, srcs, lands, after, name)


def _gather_start(arrs, me_idx, name):
    lands = [lax.dynamic_update_slice(lax.empty((N_DEV,) + a.shape, a.dtype), a[None], (me_idx, 0, 0)) for a in arrs]
    return _split_start(_gather_copies, N_PEERS, arrs, lands, name)


def _gather_wait(send_sems, recv_sems, srcs, lands, after, name):
    return _split_wait(_gather_copies, N_PEERS, send_sems, recv_sems, srcs, lands, after, name)[1]


def _proj_fwd(x, g_pre, w_int, tabs):
    s = x.shape[0]
    tm = min(512, s)

    def body(x_ref, g_ref, cs_ref, s1_ref, s2_ref, w_hbm,
             h_ref, pa_ref, pq_ref, pkv_ref, pbz_ref, pmq_ref, pmz_ref, pg_ref, w_vm, sems):
        _load_once([(w_hbm, w_vm)], sems)
        xf = x_ref[...]
        r = lax.rsqrt(jnp.mean(xf * xf, axis=-1, keepdims=True) + EPS)
        h = ((xf * r) * g_ref[...]).astype(BF16)
        h_ref[...] = h
        cs, s1, s2 = cs_ref[...], s1_ref[...], s2_ref[...]

        def mm(seg, c0, width):
            return _dot_nt(h, w_vm[seg[0] + c0:seg[0] + c0 + width, :])

        for c0 in range(0, SEG_A[1], 512):
            pa_ref[:, c0:c0 + 512] = mm(SEG_A, c0, 512).astype(BF16)
        q = mm(SEG_BQ, 0, 512)
        for b in range(4):
            pq_ref[:, 128 * b:128 * b + 128] = _rope(q[:, 128 * b:128 * b + 128], cs, s1, s2).astype(BF16)
        kv = mm(SEG_BKV, 0, 256)
        pkv_ref[:, 0:128] = _rope(kv[:, 0:128], cs, s1, s2).astype(BF16)
        pkv_ref[:, 128:256] = kv[:, 128:256].astype(BF16)
        pbz_ref[...] = mm(SEG_BZ, 0, 512).astype(BF16)
        pmq_ref[...] = mm(SEG_MQ, 0, 512).astype(BF16)
        pmz_ref[...] = mm(SEG_MZ, 0, 512).astype(BF16)
        for c0 in range(0, SEG_G[1], 512):
            pg_ref[:, c0:c0 + 512] = mm(SEG_G, c0, 512).astype(BF16)

    widths = (D_MODEL, 2048, 512, 256, 512, 512, 512, 3072)
    return pl.pallas_call(
        body, name="proj_fwd", grid=(s // tm,),
        out_shape=[jax.ShapeDtypeStruct((s, w), BF16) for w in widths],
        in_specs=[_rows(tm, D_MODEL), _full((1, D_MODEL)), _rows(tm, 128), _rows(tm, 128), _rows(tm, 128), ANY],
        out_specs=[_rows(tm, w) for w in widths],
        scratch_shapes=[pltpu.VMEM((IN_WIDTH, D_MODEL), BF16), pltpu.SemaphoreType.DMA((1,))],
        compiler_params=_params(52),
    )(x, g_pre, *tabs, w_int)


def _mem_kv_fwd(mem, g_mem, w_mkv):
    m = mem.shape[0]

    def body(mem_ref, g_ref, w_ref, mn_ref, mkv_ref):
        xf = mem_ref[...]
        r = lax.rsqrt(jnp.mean(xf * xf, axis=-1, keepdims=True) + EPS)
        mn = ((xf * r) * g_ref[...]).astype(BF16)
        mn_ref[...] = mn
        mkv_ref[...] = _dot(mn, w_ref[...]).astype(BF16)

    return pl.pallas_call(
        body, name="mem_kv_fwd", grid=(1,),
        out_shape=[jax.ShapeDtypeStruct((m, D_MODEL), BF16)] * 2,
        in_specs=[_full((m, D_MODEL)), _full((1, D_MODEL)), _full((D_MODEL, D_MODEL))],
        out_specs=[_full((m, D_MODEL))] * 2,
        compiler_params=_params(32),
    )(mem, g_mem, w_mkv)


def _halo_specs(s, tm, rows, width):
    nblk = s // rows
    prev = pl.BlockSpec((rows, width), lambda i: (jnp.maximum(i * (tm // rows) - 1, 0), 0))
    nxt = pl.BlockSpec((rows, width), lambda i: (jnp.minimum((i + 1) * (tm // rows), nblk - 1), 0))
    return prev, nxt


def _conv_common(pa, prev_row, next_row, w, first, last, tm):
    b, c, u, z = (pa[:, 512 * k:512 * k + 512] for k in range(4))
    cu = c * u
    cu_prev = jnp.where(first, 0.0, prev_row[:, 512:1024] * prev_row[:, 1024:1536])
    cu_next = jnp.where(last, 0.0, next_row[:, 512:1024] * next_row[:, 1024:1536])
    row = lax.broadcasted_iota(jnp.int32, (tm, 512), 0)
    cu_m1 = jnp.where(row == 0, cu_prev, pltpu.roll(cu, 1, 0))
    cu_p1 = jnp.where(row == tm - 1, cu_next, pltpu.roll(cu, tm - 1, 0))
    y = cu_m1 * w[0:1] + cu * w[1:2] + cu_p1 * w[2:3]
    sig = _sigmoid(z)
    return b, c, u, z, cu, cu_m1, cu_p1, y, sig, row


def _conv_fwd(pa, w_conv):
    s = pa.shape[0]
    tm = min(512, s)
    nt = s // tm

    def body(pa_ref, pp_ref, pn_ref, w_ref, ya_ref):
        i = pl.program_id(0)
        prev_row = pp_ref[...].astype(F32)[15:16, :]
        next_row = pn_ref[...].astype(F32)[0:1, :]
        b, _, _, z, _, _, _, y, sig, _ = _conv_common(
            pa_ref[...].astype(F32), prev_row, next_row, w_ref[...], i == 0, i == nt - 1, tm)
        ya_ref[...] = (b * y * (z * sig)).astype(BF16)

    prev, nxt = _halo_specs(s, tm, 16, 2048)
    return pl.pallas_call(
        body, name="conv_fwd", grid=(nt,),
        out_shape=jax.ShapeDtypeStruct((s, 512), BF16),
        in_specs=[_rows(tm, 2048), prev, nxt, _full((3, 512))],
        out_specs=_rows(tm, 512),
        compiler_params=_params(48),
    )(pa, pa, pa, w_conv)


def _heads_to_lanes(a, g, row):
    low = row < HEAD_DIM
    parts = []
    for b in (2 * g, 2 * g + 1):
        t = jnp.transpose(a[:, 128 * b:128 * b + 128])
        swapped = pltpu.roll(t, HEAD_DIM, 0)
        if g == 0:
            parts += [jnp.where(low, t, 0.0), jnp.where(low, swapped, 0.0)]
        else:
            parts += [jnp.where(low, 0.0, swapped), jnp.where(low, 0.0, t)]
    return jnp.concatenate(parts, axis=1)


def _lanes_to_heads(t0, t1, row):
    low = row < HEAD_DIM
    blocks = []
    for b in range(4):
        g = b // 2
        tg = (t0, t1)[g]
        je = 2 * (b - 2 * g)
        even, odd = tg[:, 128 * je:128 * je + 128], tg[:, 128 * je + 128:128 * je + 256]
        if g == 0:
            t = jnp.where(low, even, pltpu.roll(odd, HEAD_DIM, 0))
        else:
            t = jnp.where(low, pltpu.roll(even, HEAD_DIM, 0), odd)
        blocks.append(jnp.transpose(t))
    return jnp.concatenate(blocks, axis=1)


WINDOW_KEYS = 3 * ATTN_BLOCK
STACKED = 4 * ATTN_BLOCK
KEY_CHUNK = 32


def _fill_band_bias(bias, nb):
    assert nb >= 2
    c = lax.broadcasted_iota(jnp.int32, (WINDOW_KEYS, STACKED), 0)
    r = lax.broadcasted_iota(jnp.int32, (WINDOW_KEYS, STACKED), 1) & (ATTN_BLOCK - 1)
    band = (c >= r) & (c <= r + 2 * ATTN_BLOCK)
    for v, ok in enumerate((band, band & (c >= ATTN_BLOCK), band & (c < 2 * ATTN_BLOCK))):
        bias[v] = jnp.where(ok, 0.0, -jnp.inf)


def _bias_variant(n, nb):
    return jnp.where(n == 0, 1, jnp.where(n == nb - 1, 2, 0))


def _sink_row(sink_ref, g):
    return jnp.concatenate([jnp.full((1, ATTN_BLOCK), sink_ref[4 * g + j], F32) for j in range(4)], axis=1)


def _softmax_keys_major(sc, bias, variant, sink, e_scr):
    chunks = [pl.ds(k * KEY_CHUNK, KEY_CHUNK) for k in range(WINDOW_KEYS // KEY_CHUNK)]
    rows = [slice(k * KEY_CHUNK, (k + 1) * KEY_CHUNK) for k in range(WINDOW_KEYS // KEY_CHUNK)]
    m_run = jnp.full((KEY_CHUNK, STACKED), -jnp.inf, F32)
    for ck, rw in zip(chunks, rows):
        m_run = jnp.maximum(m_run, sc[rw] * ATTN_SCALE + bias[variant, ck, :])
    m = jnp.maximum(jnp.max(m_run, axis=0, keepdims=True), sink)
    l_run = jnp.zeros((KEY_CHUNK, STACKED), F32)
    for ck, rw in zip(chunks, rows):
        e = jnp.exp(sc[rw] * ATTN_SCALE + bias[variant, ck, :] - m)
        l_run += e
        e_scr[rw, :] = e.astype(BF16)
    es = jnp.exp(sink - m)
    inv = 1.0 / (jnp.sum(l_run, axis=0, keepdims=True) + es)
    return inv, es * inv


def _fill_padded(kv_ref, kpad, vpad, s):
    zero = jnp.zeros((ATTN_BLOCK, 128), BF16)
    kpad[0:ATTN_BLOCK, :] = zero
    vpad[0:ATTN_BLOCK, :] = zero
    kpad[ATTN_BLOCK + s:2 * ATTN_BLOCK + s, :] = zero
    vpad[ATTN_BLOCK + s:2 * ATTN_BLOCK + s, :] = zero
    kpad[ATTN_BLOCK:ATTN_BLOCK + s, :] = kv_ref[:, 0:128]
    vpad[ATTN_BLOCK:ATTN_BLOCK + s, :] = kv_ref[:, 128:256]


def _attn_fwd(pq, pkv, pbz, sink):
    s = pq.shape[0]
    nb = s // ATTN_BLOCK

    def body(sink_ref, q_ref, z_ref, kv_ref, yb_ref, kpad, vpad, bias, e_scr):
        n = pl.program_id(0)

        @pl.when(n == 0)
        def _():
            _fill_padded(kv_ref, kpad, vpad, s)
            _fill_band_bias(bias, nb)

        row = lax.broadcasted_iota(jnp.int32, (ATTN_BLOCK, 128), 0)
        start = pl.multiple_of(n * ATTN_BLOCK, ATTN_BLOCK)
        kw, vw = kpad[pl.ds(start, WINDOW_KEYS), :], vpad[pl.ds(start, WINDOW_KEYS), :]
        qf = q_ref[...].astype(F32)
        variant = _bias_variant(n, nb)
        outs = []
        for g in range(2):
            qt = _heads_to_lanes(qf, g, row).astype(BF16)
            inv, _ = _softmax_keys_major(_dot(kw, qt), bias, variant, _sink_row(sink_ref, g), e_scr)
            outs.append(_dot_tn(vw, e_scr[...]) * inv)
        attn = _lanes_to_heads(outs[0], outs[1], row)
        z = z_ref[...].astype(F32)
        yb_ref[...] = (attn * (z * _sigmoid(z))).astype(BF16)

    return pl.pallas_call(
        body, name="attn_fwd", grid=(nb,),
        out_shape=jax.ShapeDtypeStruct((s, 512), BF16),
        in_specs=[pl.BlockSpec(memory_space=pltpu.SMEM), _rows(ATTN_BLOCK, 512), _rows(ATTN_BLOCK, 512),
                  _full((s, 256))],
        out_specs=_rows(ATTN_BLOCK, 512),
        scratch_shapes=[pltpu.VMEM((s + 2 * ATTN_BLOCK, 128), BF16)] * 2
        + [pltpu.VMEM((3, WINDOW_KEYS, STACKED), F32), pltpu.VMEM((WINDOW_KEYS, STACKED), BF16)],
        compiler_params=_params(32),
    )(sink, pq, pbz, pkv)


def _mem_softmax_t(q, mk):
    sc = _dot_nt(mk, q) * MEM_SCALE
    e = jnp.exp(sc - jnp.max(sc, axis=0, keepdims=True))
    return e * (1.0 / jnp.sum(e, axis=0, keepdims=True))


def _mem_attn_fwd(pmq, pmz, mkv):
    s = pmq.shape[0]
    m = mkv.shape[0]
    tm = min(512, s)

    def body(q_ref, z_ref, mk_ref, mv_ref, ym_ref):
        z = z_ref[...].astype(F32)
        sz = z * _sigmoid(z)
        for h in range(MEM_HEADS):
            cols = slice(128 * h, 128 * h + 128)
            pt = _mem_softmax_t(q_ref[:, cols], mk_ref[:, cols])
            o = _dot_tn(pt.astype(BF16), mv_ref[:, cols])
            ym_ref[:, cols] = (o * sz[:, cols]).astype(BF16)

    return pl.pallas_call(
        body, name="mem_attn_fwd", grid=(s // tm,),
        out_shape=jax.ShapeDtypeStruct((s, 512), BF16),
        in_specs=[_rows(tm, 512), _rows(tm, 512), pl.BlockSpec((m, 512), lambda i: (0, 0)),
                  pl.BlockSpec((m, 512), lambda i: (0, 1))],
        out_specs=_rows(tm, 512),
        compiler_params=_params(32),
    )(pmq, pmz, mkv, mkv)


def _mid(ya, yb, ym, pg, x, target, g_post, w_up, w_out):
    s = x.shape[0]
    tm = min(256, s)
    nt = s // tm

    def body(ya_ref, yb_ref, ym_ref, pg_ref, x_ref, t_ref, gp_ref, wup_hbm, wout_hbm,
             dg_ref, dya_ref, dyb_ref, dym_ref, dy_ref, loss_ref, ggp_ref, mb_ref, dob_ref, du_ref,
             wup_vm, wout_vm, sems):
        i = pl.program_id(0)
        _load_once([(wup_hbm.at[d], wup_vm.at[:, pl.ds(128 * d, 128)]) for d in range(N_DEV)]
                   + [(wout_hbm, wout_vm)], sems)

        @pl.when(i == 0)
        def _():
            loss_ref[...] = jnp.zeros_like(loss_ref)
            ggp_ref[...] = jnp.zeros_like(ggp_ref)

        ys = (ya_ref[...], yb_ref[...], ym_ref[...])
        us = [_dot(ys[k], wup_vm[512 * k:512 * k + 512, :]) for k in range(3)]
        gates = [_sigmoid(pg_ref[:, 1024 * k:1024 * k + 1024].astype(F32)) for k in range(3)]
        merged = gates[0] * us[0] + gates[1] * us[1] + gates[2] * us[2]
        mb = merged.astype(BF16)
        mb_ref[...] = mb
        out = _dot(mb, wout_vm[...])
        r = lax.rsqrt(jnp.mean(out * out, axis=-1, keepdims=True) + EPS)
        on = out * r
        gp = gp_ref[...]
        err = (x_ref[...] + on * gp) - t_ref[...]
        loss_ref[...] += 0.5 * jnp.sum(err * err) * (1.0 / D_MODEL)
        dy = err * (1.0 / D_MODEL)
        dy_ref[...] = dy
        ggp_ref[...] += jnp.sum(dy * on, axis=0, keepdims=True)
        a = dy * gp
        d_out = r * (a - on * jnp.mean(a * on, axis=-1, keepdims=True))
        dob = d_out.astype(BF16)
        dob_ref[...] = dob
        d_merged = _dot_nt(dob, wout_vm[...])
        d_refs = (dya_ref, dyb_ref, dym_ref)
        for k in range(3):
            g = gates[k]
            dg_ref[:, 1024 * k:1024 * k + 1024] = (d_merged * us[k] * g * (1.0 - g)).astype(BF16)
            du = (d_merged * g).astype(BF16)
            du_ref[k] = du
            d_refs[k][...] = _dot_nt(du, wup_vm[512 * k:512 * k + 512, :])

    return pl.pallas_call(
        body, name="mid", grid=(nt,),
        out_shape=[jax.ShapeDtypeStruct((s, 3072), BF16)] + [jax.ShapeDtypeStruct((s, 512), F32)] * 3
        + [jax.ShapeDtypeStruct((s, D_MODEL), F32), jax.ShapeDtypeStruct((8, 128), F32),
           jax.ShapeDtypeStruct((1, D_MODEL), F32), jax.ShapeDtypeStruct((s, D_MODEL), BF16),
           jax.ShapeDtypeStruct((s, D_MODEL), BF16), jax.ShapeDtypeStruct((3, s, D_MODEL), BF16)],
        in_specs=[_rows(tm, 512)] * 3 + [_rows(tm, 3072), _rows(tm, D_MODEL), _rows(tm, D_MODEL),
                                         _full((1, D_MODEL)), ANY, ANY],
        out_specs=[_rows(tm, 3072)] + [_rows(tm, 512)] * 3
        + [_rows(tm, D_MODEL), _full((8, 128)), _full((1, D_MODEL)), _rows(tm, D_MODEL), _rows(tm, D_MODEL),
           pl.BlockSpec((3, tm, D_MODEL), lambda i: (0, i, 0))],
        scratch_shapes=[pltpu.VMEM((1536, D_MODEL), BF16), pltpu.VMEM((D_MODEL, D_MODEL), BF16),
                        pltpu.SemaphoreType.DMA((N_DEV + 1,))],
        compiler_params=_params(56),
    )(ya, yb, ym, pg, x, target, g_post, w_up, w_out)


def _gw_mid(mb, dob, ys, du):
    s = mb.shape[0]
    tn = 256

    def out_body(mb_ref, dob_ref, o_ref):
        o_ref[...] = _dot_tn(mb_ref[...], dob_ref[...]).astype(BF16)

    gw_out = pl.pallas_call(
        out_body, name="gw_out", grid=(D_MODEL // tn,),
        out_shape=jax.ShapeDtypeStruct((D_MODEL, D_MODEL), BF16),
        in_specs=[pl.BlockSpec((s, tn), lambda j: (0, j)), _full((s, D_MODEL))],
        out_specs=pl.BlockSpec((tn, D_MODEL), lambda j: (j, 0)),
        compiler_params=_params(48),
    )(mb, dob)

    per = 512 // tn

    def up_body(ya_ref, yb_ref, ym_ref, du_ref, o_ref):
        j = pl.program_id(0)
        for k, y_ref in enumerate((ya_ref, yb_ref, ym_ref)):
            @pl.when(j // per == k)
            def _(y_ref=y_ref):
                res = _dot_tn(y_ref[...], du_ref[...])
                for d in range(N_DEV):
                    o_ref[d] = res[:, 128 * d:128 * d + 128].astype(BF16)

    def y_spec(k):
        return pl.BlockSpec((s, tn), lambda j: (0, jnp.clip(j - per * k, 0, per - 1)))

    gw_up = pl.pallas_call(
        up_body, name="gw_up", grid=(3 * per,),
        out_shape=jax.ShapeDtypeStruct((N_DEV, 1536, 128), BF16),
        in_specs=[y_spec(0), y_spec(1), y_spec(2), pl.BlockSpec((None, s, D_MODEL), lambda j: (j // per, 0, 0))],
        out_specs=pl.BlockSpec((N_DEV, tn, 128), lambda j: (0, j, 0)),
        compiler_params=_params(48),
    )(*ys, du)
    return gw_out, gw_up


def _conv_bwd(pa, dya, w_conv):
    s = pa.shape[0]
    tm = min(512, s)
    nt = s // tm

    def body(pa_ref, pp_ref, pn_ref, d_ref, dp_ref, dn_ref, w_ref, da_ref, gw_ref):
        i = pl.program_id(0)
        first, last = i == 0, i == nt - 1

        @pl.when(first)
        def _():
            gw_ref[...] = jnp.zeros_like(gw_ref)

        w = w_ref[...]
        prev_row = pp_ref[...].astype(F32)[15:16, :]
        next_row = pn_ref[...].astype(F32)[0:1, :]
        b, c, u, z, cu, cu_m1, cu_p1, y, sig, row = _conv_common(
            pa_ref[...].astype(F32), prev_row, next_row, w, first, last, tm)
        sz = z * sig
        dya_t = d_ref[...]
        d_y = dya_t * b * sz

        def halo_dy(p_row, d_row):
            zz = p_row[:, 1536:2048]
            return d_row * p_row[:, 0:512] * (zz * _sigmoid(zz))

        dy_prev = jnp.where(first, 0.0, halo_dy(prev_row, dp_ref[7:8, :]))
        dy_next = jnp.where(last, 0.0, halo_dy(next_row, dn_ref[0:1, :]))
        dy_m1 = jnp.where(row == 0, dy_prev, pltpu.roll(d_y, 1, 0))
        dy_p1 = jnp.where(row == tm - 1, dy_next, pltpu.roll(d_y, tm - 1, 0))
        d_cu = dy_p1 * w[0:1] + d_y * w[1:2] + dy_m1 * w[2:3]
        da_ref[:, 0:512] = (dya_t * y * sz).astype(BF16)
        da_ref[:, 512:1024] = (d_cu * u).astype(BF16)
        da_ref[:, 1024:1536] = (d_cu * c).astype(BF16)
        da_ref[:, 1536:2048] = (dya_t * b * y * (sig * (1.0 + z * (1.0 - sig)))).astype(BF16)
        gw_ref[0:1, :] += jnp.sum(d_y * cu_m1, axis=0, keepdims=True)
        gw_ref[1:2, :] += jnp.sum(d_y * cu, axis=0, keepdims=True)
        gw_ref[2:3, :] += jnp.sum(d_y * cu_p1, axis=0, keepdims=True)

    prev, nxt = _halo_specs(s, tm, 16, 2048)
    dprev, dnxt = _halo_specs(s, tm, 8, 512)
    return pl.pallas_call(
        body, name="conv_bwd", grid=(nt,),
        out_shape=[jax.ShapeDtypeStruct((s, 2048), BF16), jax.ShapeDtypeStruct((8, 512), F32)],
        in_specs=[_rows(tm, 2048), prev, nxt, _rows(tm, 512), dprev, dnxt, _full((3, 512))],
        out_specs=[_rows(tm, 2048), _full((8, 512))],
        compiler_params=_params(48),
    )(pa, pa, pa, dya, dya, dya, w_conv)


def _attn_bwd(pq, pkv, pbz, dyb, sink, tabs):
    s = pq.shape[0]
    nb = s // ATTN_BLOCK

    def body(sink_ref, q_ref, z_ref, d_ref, cs_ref, s1_ref, s2_ref, kv_ref, csf_ref, s1f_ref, s2f_ref,
             dq_ref, dz_ref, dkv_ref, gs_ref, kpad, vpad, dk_acc, dv_acc, bias, e_scr, ds_scr):
        n = pl.program_id(0)

        @pl.when(n == 0)
        def _():
            _fill_padded(kv_ref, kpad, vpad, s)
            _fill_band_bias(bias, nb)
            dk_acc[...] = jnp.zeros_like(dk_acc)
            dv_acc[...] = jnp.zeros_like(dv_acc)
            gs_ref[...] = jnp.zeros_like(gs_ref)

        row = lax.broadcasted_iota(jnp.int32, (ATTN_BLOCK, 128), 0)
        start = pl.multiple_of(n * ATTN_BLOCK, ATTN_BLOCK)
        kw, vw = kpad[pl.ds(start, WINDOW_KEYS), :], vpad[pl.ds(start, WINDOW_KEYS), :]
        qf = q_ref[...].astype(F32)
        variant = _bias_variant(n, nb)
        z = z_ref[...].astype(F32)
        sig = _sigmoid(z)
        dyb_t = d_ref[...]
        d_attn = dyb_t * (z * sig)
        outs, dqs = [], []
        dk_w = jnp.zeros((WINDOW_KEYS, 128), F32)
        dv_w = jnp.zeros((WINDOW_KEYS, 128), F32)
        for g in range(2):
            qt = _heads_to_lanes(qf, g, row)
            inv, p_sink = _softmax_keys_major(
                _dot(kw, qt.astype(BF16)), bias, variant, _sink_row(sink_ref, g), e_scr)
            ot = _dot_tn(vw, e_scr[...]) * inv
            outs.append(ot)
            dot_ = _heads_to_lanes(d_attn, g, row)
            delta = jnp.sum(dot_ * ot, axis=0, keepdims=True)
            dpt = _dot(vw, dot_.astype(BF16))
            for k in range(WINDOW_KEYS // KEY_CHUNK):
                rw = slice(k * KEY_CHUNK, (k + 1) * KEY_CHUNK)
                ds_scr[rw, :] = (e_scr[rw, :].astype(F32) * (dpt[rw] - delta)).astype(BF16)
            sink_part = p_sink * delta
            for j in range(4):
                h = 4 * g + j
                gs_ref[h:h + 1, :] -= jnp.sum(sink_part[:, 128 * j:128 * j + 128])
            dqs.append(_dot_tn(kw, ds_scr[...]) * (inv * ATTN_SCALE))
            dk_w += _dot_nt(ds_scr[...], (qt * inv).astype(BF16)) * ATTN_SCALE
            dv_w += _dot_nt(e_scr[...], (dot_ * inv).astype(BF16))
        dk_acc[pl.ds(start, WINDOW_KEYS), :] += dk_w
        dv_acc[pl.ds(start, WINDOW_KEYS), :] += dv_w
        attn = _lanes_to_heads(outs[0], outs[1], row)
        dz_ref[...] = (dyb_t * attn * (sig * (1.0 + z * (1.0 - sig)))).astype(BF16)
        dq = _lanes_to_heads(dqs[0], dqs[1], row)
        cs, s1, s2 = cs_ref[...], s1_ref[...], s2_ref[...]
        for b in range(4):
            dq_ref[:, 128 * b:128 * b + 128] = _rope_t(dq[:, 128 * b:128 * b + 128], cs, s1, s2).astype(BF16)

        @pl.when(n == nb - 1)
        def _():
            dk = dk_acc[ATTN_BLOCK:ATTN_BLOCK + s, :]
            dkv_ref[:, 0:128] = _rope_t(dk, csf_ref[...], s1f_ref[...], s2f_ref[...]).astype(BF16)
            dkv_ref[:, 128:256] = dv_acc[ATTN_BLOCK:ATTN_BLOCK + s, :].astype(BF16)

    tile = _rows(ATTN_BLOCK, 512)
    tab = _rows(ATTN_BLOCK, 128)
    return pl.pallas_call(
        body, name="attn_bwd", grid=(nb,),
        out_shape=[jax.ShapeDtypeStruct((s, 512), BF16), jax.ShapeDtypeStruct((s, 512), BF16),
                   jax.ShapeDtypeStruct((s, 256), BF16), jax.ShapeDtypeStruct((8, 128), F32)],
        in_specs=[pl.BlockSpec(memory_space=pltpu.SMEM), tile, tile, tile, tab, tab, tab,
                  _full((s, 256)), _full((s, 128)), _full((s, 128)), _full((s, 128))],
        out_specs=[tile, tile, _full((s, 256)), _full((8, 128))],
        scratch_shapes=[pltpu.VMEM((s + 2 * ATTN_BLOCK, 128), BF16)] * 2
        + [pltpu.VMEM((s + 2 * ATTN_BLOCK, 128), F32)] * 2
        + [pltpu.VMEM((3, WINDOW_KEYS, STACKED), F32)] + [pltpu.VMEM((WINDOW_KEYS, STACKED), BF16)] * 2,
        compiler_params=_params(48),
    )(sink, pq, pbz, dyb, *tabs, pkv, *tabs)


def _mem_attn_bwd(pmq, pmz, mkv, dym):
    s = pmq.shape[0]
    m = mkv.shape[0]
    tm = min(512, s)

    def body(q_ref, z_ref, d_ref, mk_ref, mv_ref, dq_ref, dz_ref, dmkv_ref):
        @pl.when(pl.program_id(0) == 0)
        def _():
            dmkv_ref[...] = jnp.zeros_like(dmkv_ref)

        z = z_ref[...].astype(F32)
        sig = _sigmoid(z)
        dym_t = d_ref[...]
        d_attn = dym_t * (z * sig)
        dsilu = sig * (1.0 + z * (1.0 - sig))
        for h in range(MEM_HEADS):
            cols = slice(128 * h, 128 * h + 128)
            q, mk, mv = q_ref[:, cols], mk_ref[:, cols], mv_ref[:, cols]
            pt = _mem_softmax_t(q, mk)
            pb = pt.astype(BF16)
            o = _dot_tn(pb, mv)
            dob = d_attn[:, cols].astype(BF16)
            dpt = _dot_nt(mv, dob)
            dst = (pt * (dpt - jnp.sum(pt * dpt, axis=0, keepdims=True))).astype(BF16)
            dq_ref[:, cols] = (_dot_tn(dst, mk) * MEM_SCALE).astype(BF16)
            dz_ref[:, cols] = (dym_t[:, cols] * o * dsilu[:, cols]).astype(BF16)
            dmkv_ref[:, cols] += _dot(dst, q) * MEM_SCALE
            dmkv_ref[:, 512 + 128 * h:512 + 128 * h + 128] += _dot(pb, dob)

    return pl.pallas_call(
        body, name="mem_attn_bwd", grid=(s // tm,),
        out_shape=[jax.ShapeDtypeStruct((s, 512), BF16), jax.ShapeDtypeStruct((s, 512), BF16),
                   jax.ShapeDtypeStruct((m, D_MODEL), F32)],
        in_specs=[_rows(tm, 512), _rows(tm, 512), _rows(tm, 512), pl.BlockSpec((m, 512), lambda i: (0, 0)),
                  pl.BlockSpec((m, 512), lambda i: (0, 1))],
        out_specs=[_rows(tm, 512), _rows(tm, 512), _full((m, D_MODEL))],
        compiler_params=_params(32),
    )(pmq, pmz, dym, mkv, mkv)


def _mem_kv_bwd(mem, g_mem, mn, dmkv, w_mkv):
    m = mem.shape[0]

    def body(mem_ref, g_ref, mn_ref, d_ref, w_ref, gw_ref, gg_ref):
        db = d_ref[...].astype(BF16)
        gw_ref[...] = _dot_tn(mn_ref[...], db).astype(BF16)
        d_mn = _dot_nt(db, w_ref[...])
        xf = mem_ref[...]
        r = lax.rsqrt(jnp.mean(xf * xf, axis=-1, keepdims=True) + EPS)
        gg_ref[...] = jnp.sum(d_mn * (xf * r), axis=0, keepdims=True)

    return pl.pallas_call(
        body, name="mem_kv_bwd", grid=(1,),
        out_shape=[jax.ShapeDtypeStruct((D_MODEL, D_MODEL), BF16), jax.ShapeDtypeStruct((1, D_MODEL), F32)],
        in_specs=[_full((m, D_MODEL)), _full((1, D_MODEL)), _full((m, D_MODEL)), _full((m, D_MODEL)),
                  _full((D_MODEL, D_MODEL))],
        out_specs=[_full((D_MODEL, D_MODEL)), _full((1, D_MODEL))],
        compiler_params=_params(32),
    )(mem, g_mem, mn, dmkv, w_mkv)


def _dh_bwd(dparts, x, dy, g_pre, w_int):
    s = x.shape[0]
    tm = min(256, s)

    def body(*refs):
        d_refs = refs[:7]
        x_ref, dy_ref, g_ref, w_hbm, gx_ref, gg_ref, w_vm, sems = refs[7:]
        _load_once([(w_hbm, w_vm)], sems)

        @pl.when(pl.program_id(0) == 0)
        def _():
            gg_ref[...] = jnp.zeros_like(gg_ref)

        d_h = jnp.zeros((tm, D_MODEL), F32)
        for d_ref, (r0, width) in zip(d_refs, SEGS):
            for c0 in range(0, width, 512):
                cw = min(512, width - c0)
                d_h += _dot(d_ref[:, c0:c0 + cw], w_vm[r0 + c0:r0 + c0 + cw, :])
        xf = x_ref[...]
        r = lax.rsqrt(jnp.mean(xf * xf, axis=-1, keepdims=True) + EPS)
        xn = xf * r
        a = d_h * g_ref[...]
        gx_ref[...] = r * (a - xn * jnp.mean(a * xn, axis=-1, keepdims=True)) + dy_ref[...]
        gg_ref[...] += jnp.sum(d_h * xn, axis=0, keepdims=True)

    return pl.pallas_call(
        body, name="dh_bwd", grid=(s // tm,),
        out_shape=[jax.ShapeDtypeStruct((s, D_MODEL), F32), jax.ShapeDtypeStruct((1, D_MODEL), F32)],
        in_specs=[_rows(tm, w) for _, w in SEGS] + [_rows(tm, D_MODEL), _rows(tm, D_MODEL), _full((1, D_MODEL)), ANY],
        out_specs=[_rows(tm, D_MODEL), _full((1, D_MODEL))],
        scratch_shapes=[pltpu.VMEM((IN_WIDTH, D_MODEL), BF16), pltpu.SemaphoreType.DMA((1,))],
        compiler_params=_params(52),
    )(*dparts, x, dy, g_pre, w_int)


def _gw_in(dparts, h):
    s = h.shape[0]
    tn = 256
    starts, counts = [], []
    for r0, width in SEGS:
        starts.append(r0 // tn)
        counts.append(width // tn)

    def body(*refs):
        d_refs = refs[:7]
        h_hbm, o_ref, h_vm, sems = refs[7:]
        _load_once([(h_hbm, h_vm)], sems)
        j = pl.program_id(0)
        for d_ref, st, cnt in zip(d_refs, starts, counts):
            @pl.when((j >= st) & (j < st + cnt))
            def _(d_ref=d_ref):
                o_ref[...] = _dot_tn(d_ref[...], h_vm[...]).astype(BF16)

    def seg_spec(st, cnt):
        return pl.BlockSpec((s, tn), lambda j: (0, jnp.clip(j - st, 0, cnt - 1)))

    return pl.pallas_call(
        body, name="gw_in", grid=(IN_WIDTH // tn,),
        out_shape=jax.ShapeDtypeStruct((IN_WIDTH, D_MODEL), BF16),
        in_specs=[seg_spec(st, cnt) for st, cnt in zip(starts, counts)] + [ANY],
        out_specs=pl.BlockSpec((tn, D_MODEL), lambda j: (j, 0)),
        scratch_shapes=[pltpu.VMEM((s, D_MODEL), BF16), pltpu.SemaphoreType.DMA((1,))],
        compiler_params=_params(52),
    )(*dparts, h)


def _adamw_math(w, g, m, v):
    m2 = ADAM_B1 * m + (1.0 - ADAM_B1) * g
    v2 = ADAM_B2 * v + (1.0 - ADAM_B2) * (g * g)
    m_hat = m2 / (1.0 - ADAM_B1 ** ADAM_STEP)
    v_hat = v2 / (1.0 - ADAM_B2 ** ADAM_STEP)
    delta = -ADAM_LR * (m_hat / (jnp.sqrt(v_hat) + ADAM_EPS) + ADAM_WD * w)
    return delta, m2, v2


def _sum_adamw(own, land, chip, block, w, m, v, name, tiles=1):
    r, c = w.shape
    rt = r // tiles

    def body(c_ref, own_ref, l1_ref, l2_ref, l3_ref, w_ref, m_ref, v_ref, g_ref, d_ref, m2_ref, v2_ref):
        g = own_ref[...].astype(F32)
        for l_ref in (l1_ref, l2_ref, l3_ref):
            g += l_ref[...].astype(F32)
        g_ref[...] = g
        d_ref[...], m2_ref[...], v2_ref[...] = _adamw_math(w_ref[...], g, m_ref[...], v_ref[...])

    def share(k):
        return pl.BlockSpec((None, rt, c), lambda i, c_ref: (jnp.bitwise_xor(c_ref[0], k), block * tiles + i, 0))

    spec = pl.BlockSpec((rt, c), lambda i, c_ref: (i, 0))
    grid_spec = pltpu.PrefetchScalarGridSpec(
        num_scalar_prefetch=1, grid=(tiles,),
        in_specs=[share(0), share(1), share(2), share(3)] + [spec] * 3, out_specs=[spec] * 4)
    return pl.pallas_call(
        body, name=name, grid_spec=grid_spec,
        out_shape=[jax.ShapeDtypeStruct((r, c), F32)] * 4,
        compiler_params=_params(48),
    )(chip, own, land, land, land, w, m, v)


def _pack_sum(packs):
    def body(p_ref, o_ref):
        acc = p_ref[0]
        for k in range(1, N_DEV):
            acc += p_ref[k]
        o_ref[...] = acc

    return pl.pallas_call(
        body, name="pack_sum", grid=(1,),
        out_shape=jax.ShapeDtypeStruct((8, D_MODEL), F32),
        in_specs=[_full((N_DEV, 8, D_MODEL))], out_specs=_full((8, D_MODEL)),
    )(packs)


def _small_adamw(ws, gs, ms, vs):
    k = len(ws)

    def body(*refs):
        w_refs, g_refs, m_refs, v_refs = (refs[j * k:(j + 1) * k] for j in range(4))
        outs = refs[4 * k:]
        for j in range(k):
            outs[j][...], outs[k + j][...], outs[2 * k + j][...] = _adamw_math(
                w_refs[j][...], g_refs[j][...], m_refs[j][...], v_refs[j][...])

    specs = [_full(w.shape) for w in ws]
    res = pl.pallas_call(
        body, name="small_adamw", grid=(1,),
        out_shape=[jax.ShapeDtypeStruct(w.shape, F32) for w in ws] * 3,
        in_specs=specs * 4, out_specs=specs * 3,
    )(*ws, *gs, *ms, *vs)
    return res[:k], res[k:2 * k], res[2 * k:]


def kernel(x, mem, g_pre, w_in, w_conv, attn_sink, g_mem, w_mem_kv, w_up_a, w_up_b, w_up_m, w_out, g_post, loss_target, m_g_pre, m_w_in, m_w_conv, m_attn_sink, m_g_mem, m_w_mem_kv, m_w_up_a, m_w_up_b, m_w_up_m, m_w_out, m_g_post, v_g_pre, v_w_in, v_w_conv, v_attn_sink, v_g_mem, v_w_mem_kv, v_w_up_a, v_w_up_b, v_w_up_m, v_w_out, v_g_post):
    s = x.shape[1]
    x2, mem2, tgt2 = x[0], mem[0], loss_target[0]
    me = 4 * lax.axis_index("x") + 2 * lax.axis_index("y") + lax.axis_index("c")

    w_up_loc = jnp.concatenate([w_up_a[0], w_up_b[0], w_up_m[0]], axis=0).astype(BF16)
    w_conv_loc = jnp.zeros((8, 128), F32).at[:3, :64].set(w_conv[0])
    w_int_g, w_conv_g = _all_gather([w_in[0].T.astype(BF16), w_conv_loc], "gather_w_in")
    w_int = w_int_g.reshape(IN_WIDTH, D_MODEL)
    w_conv_f = w_conv_g[:, :3, :64].transpose(1, 0, 2).reshape(3, 512)
    late = _gather_start([w_mem_kv[0].astype(BF16) + w_conv_g[0, 7:8, 0:1].astype(BF16),
                          w_out[0].astype(BF16), w_up_loc], me, "gather_late_start")
    sink = attn_sink[0]
    tabs = _rope_tables(s)

    h, pa, pq, pkv, pbz, pmq, pmz, pg = _proj_fwd(x2, g_pre + late[4][0:1, 0:1], w_int, tabs)
    ya = _conv_fwd(pa, w_conv_f)
    yb = _attn_fwd(pq, pkv, pbz, sink)
    w_mkv_g, w_out_g, w_up_g = _gather_wait(*late[:4], yb, "gather_late_wait")
    w_mkv = w_mkv_g.reshape(D_MODEL, D_MODEL)
    w_out_f = w_out_g.reshape(D_MODEL, D_MODEL)
    mn, mkv = _mem_kv_fwd(mem2, g_mem, w_mkv)
    ym = _mem_attn_fwd(pmq, pmz, mkv)
    dg, dya, dyb, dym, dy, loss_p, gg_post, mb, dob, du = _mid(ya, yb, ym, pg, x2, tgt2, g_post, w_up_g, w_out_f)
    gw_out, gw_up = _gw_mid(mb, dob, (ya, yb, ym), du)

    core = lax.axis_index("c").astype(jnp.int32).reshape(1)
    chip = (2 * lax.axis_index("x") + lax.axis_index("y")).astype(jnp.int32).reshape(1)

    def exchange_start(shares, tag):
        from_sibling = _sibling_exchange(shares, "grads_to_sibling_" + tag)
        chip_shares = _pair_add(shares, from_sibling, core, "grads_pair_add_" + tag)
        return _chip_exchange_start(chip_shares, "grads_to_chips_start_" + tag)

    dmq, dmz, dmkv = _mem_attn_bwd(pmq, pmz, mkv, dym)
    gw_mkv, gg_mem = _mem_kv_bwd(mem2, g_mem, mn, dmkv, w_mkv)
    send1, recv1, srcs1, lands1, token1 = exchange_start(
        [gw_mkv.reshape(N_DEV, 128, D_MODEL), gw_out.reshape(N_DEV, 128, D_MODEL), gw_up], "small")
    da, gw_conv = _conv_bwd(pa, dya, w_conv_f + token1[0:1, 0:1])
    dq, dbz, dkv, g_sink = _attn_bwd(pq, pkv, pbz, dyb, sink, tabs)
    dparts = (da, dq, dkv, dbz, dmq, dmz, dg)
    gw_int = _gw_in(dparts, h)
    send2, recv2, srcs2, lands2, token2 = exchange_start([gw_int.reshape(N_DEV, SHARD_IN, D_MODEL)], "w_in")
    grad_x, gg_pre = _dh_bwd(dparts, x2, dy, g_pre + token2[0:1, 0:1], w_int)
    (o_mkv, o_out, o_up, o_int), (l_mkv, l_out, l_up, l_int) = _chip_exchange_wait(
        send1 + send2, recv1 + recv2, srcs1 + srcs2, lands1 + lands2, grad_x, "grads_to_chips_wait")

    row3 = jnp.concatenate([gw_conv[0:1], gw_conv[1:2]], axis=1)
    row4 = jnp.concatenate([gw_conv[2:3], g_sink[:, 0].reshape(1, 8), loss_p[0:1, 0:1],
                            jnp.zeros((1, 512 - 9), F32)], axis=1)
    pack = jnp.concatenate([gg_pre, gg_mem, gg_post, row3, row4, jnp.zeros((3, D_MODEL), F32)], axis=0)
    (packs,) = _all_gather([pack], "gather_small")
    tot = _pack_sum(packs)

    g_w_in, d_w_in, nm_w_in, nv_w_in = (t.T for t in _sum_adamw(
        o_int, l_int, chip, 0, w_in[0].T, m_w_in[0].T, v_w_in[0].T, "adamw_w_in", tiles=2))
    g_mkv, d_mkv, nm_mkv, nv_mkv = _sum_adamw(
        o_mkv, l_mkv, chip, 0, w_mem_kv[0], m_w_mem_kv[0], v_w_mem_kv[0], "adamw_w_mem_kv")
    g_out, d_out, nm_out, nv_out = _sum_adamw(o_out, l_out, chip, 0, w_out[0], m_w_out[0], v_w_out[0], "adamw_w_out")
    up = [_sum_adamw(o_up, l_up, chip, k, w[0], m[0], v[0], "adamw_w_up_" + "abm"[k])
          for k, (w, m, v) in enumerate([(w_up_a, m_w_up_a, v_w_up_a), (w_up_b, m_w_up_b, v_w_up_b),
                                         (w_up_m, m_w_up_m, v_w_up_m)])]

    g_g_pre, g_g_mem, g_g_post = tot[0:1], tot[1:2], tot[2:3]
    g_conv_full = jnp.concatenate([tot[3:4, 0:512], tot[3:4, 512:1024], tot[4:5, 0:512]], axis=0)
    g_conv = lax.dynamic_slice(g_conv_full, (0, 64 * me), (3, 64))
    g_sink_tot = tot[4:5, 512:520]
    loss = tot[4, 520]
    small_w = [g_pre, w_conv[0], attn_sink, g_mem, g_post]
    small_g = [g_g_pre, g_conv, g_sink_tot, g_g_mem, g_g_post]
    small_m = [m_g_pre, m_w_conv[0], m_attn_sink, m_g_mem, m_g_post]
    small_v = [v_g_pre, v_w_conv[0], v_attn_sink, v_g_mem, v_g_post]
    sd, sm, sv = _small_adamw(small_w, small_g, small_m, small_v)

    def lead(a):
        return a[None]

    grads = [g_g_pre, lead(g_w_in), lead(g_conv), g_sink_tot, g_g_mem, lead(g_mkv), lead(up[0][0]),
             lead(up[1][0]), lead(up[2][0]), lead(g_out), g_g_post]

    def assemble(small, big_in, big_mkv, big_up, big_out):
        return [small[0], lead(big_in), lead(small[1]), small[2], small[3], lead(big_mkv), lead(big_up[0]),
                lead(big_up[1]), lead(big_up[2]), lead(big_out), small[4]]

    deltas = assemble(sd, d_w_in, d_mkv, [u[1] for u in up], d_out)
    new_m = assemble(sm, nm_w_in, nm_mkv, [u[2] for u in up], nm_out)
    new_v = assemble(sv, nv_w_in, nv_mkv, [u[3] for u in up], nv_out)
    return (loss, grad_x[None], *grads, *deltas, *new_m, *new_v)
```

```python
import functools

import jax
import jax.numpy as jnp
from jax import lax
from jax.experimental import pallas as pl
from jax.experimental.pallas import tpu as pltpu

F32 = jnp.float32
BF16 = jnp.bfloat16
MESH = pl.DeviceIdType.MESH

N_DEV = 8
D_MODEL = 1024
EPS = 1e-6
ROPE_THETA = 500000.0
ROT_DIM = 16
HEAD_DIM = 64
ATTN_BLOCK = 128
MEM_HEADS = 4
MEM_HEAD_DIM = 128
ATTN_SCALE = HEAD_DIM ** -0.5
MEM_SCALE = MEM_HEAD_DIM ** -0.5

ADAM_LR = 0.001
ADAM_B1 = 0.9
ADAM_B2 = 0.999
ADAM_EPS = 1e-08
ADAM_WD = 0.01
ADAM_STEP = 10

SEG_A = (0, 2048)
SEG_BQ = (2048, 512)
SEG_BKV = (2560, 256)
SEG_BZ = (2816, 512)
SEG_MQ = (3328, 512)
SEG_MZ = (3840, 512)
SEG_G = (4352, 3072)
SEGS = (SEG_A, SEG_BQ, SEG_BKV, SEG_BZ, SEG_MQ, SEG_MZ, SEG_G)
IN_WIDTH = 7424
SHARD_IN = IN_WIDTH // N_DEV

V7X_VMEM_BYTES = 64 * 1024 * 1024
ANY = pl.BlockSpec(memory_space=pl.ANY)


def _params(vmem_mb):
    assert vmem_mb * 1024 * 1024 < V7X_VMEM_BYTES
    return pltpu.CompilerParams(dimension_semantics=("arbitrary",), vmem_limit_bytes=vmem_mb * 1024 * 1024)


def _full(shape):
    zeros = (0,) * len(shape)
    return pl.BlockSpec(shape, lambda i: zeros)


def _rows(tm, width):
    return pl.BlockSpec((tm, width), lambda i: (i, 0))


def _dot(a, b):
    return jnp.dot(a, b, preferred_element_type=F32)


def _dot_nt(a, b):
    return lax.dot_general(a, b, (((1,), (1,)), ((), ())), preferred_element_type=F32)


def _dot_tn(a, b):
    return lax.dot_general(a, b, (((0,), (0,)), ((), ())), preferred_element_type=F32)


def _sigmoid(z):
    return 1.0 / (1.0 + jnp.exp(-z))


def _rope(t, cs, s1, s2):
    return t * cs + pltpu.roll(t, 120, 1) * s1 + pltpu.roll(t, 8, 1) * s2


def _rope_t(d, cs, s1, s2):
    return d * cs + pltpu.roll(d * s1, 8, 1) + pltpu.roll(d * s2, 120, 1)


def _rope_tables(s):
    half = ROT_DIM // 2
    inv_freq = jnp.power(jnp.float32(ROPE_THETA), -jnp.arange(half, dtype=F32) * (2.0 / ROT_DIM))
    ang = jnp.arange(s).astype(F32)[:, None] * inv_freq[None, :]
    cos, sin = jnp.cos(ang), jnp.sin(ang)
    d = jnp.arange(128) % HEAD_DIM
    k = jnp.arange(half)[:, None]
    lo = (d[None, :] == k).astype(F32)
    hi = (d[None, :] == k + half).astype(F32)
    spread = functools.partial(jnp.dot, precision=lax.Precision.HIGHEST)
    return (spread(cos, lo + hi) + (d >= ROT_DIM).astype(F32)[None, :], -spread(sin, lo), spread(sin, hi))


def _load_once(pairs, sems):
    @pl.when(pl.program_id(0) == 0)
    def _():
        cps = [pltpu.make_async_copy(src, dst, sems.at[k]) for k, (src, dst) in enumerate(pairs)]
        for cp in cps:
            cp.start()
        for cp in cps:
            cp.wait()


def _my_place():
    x, y, c = lax.axis_index("x"), lax.axis_index("y"), lax.axis_index("c")
    return x, y, c


def _all_gather(arrs, name, splits=None):
    n = len(arrs)
    if splits is None:
        splits = [[(0, a.shape[0])] for a in arrs]
    pieces = [(a, r0, rn) for a in range(n) for r0, rn in splits[a]]
    n_p = len(pieces)

    def body(*refs):
        ins, outs = refs[:n], refs[n:2 * n]
        send_sems, recv_sems, local_sems = refs[2 * n:]
        x, y, c = _my_place()
        me, sibling = (x, y, c), (x, y, 1 - c)

        def route(core):
            first = (jnp.bitwise_xor(x, 1 - core), jnp.bitwise_xor(y, core), core)
            second = (jnp.bitwise_xor(x, core), jnp.bitwise_xor(y, 1 - core), core)
            return first, second, (1 - x, 1 - y, core)

        def idx(px, py, pc):
            return 4 * px + 2 * py + pc

        def copy(p, k, block, to, own=False):
            a, r0, rn = pieces[p]
            dst = outs[a].at[idx(*block), pl.ds(r0, rn)]
            return pltpu.make_async_remote_copy(
                src_ref=ins[a].at[pl.ds(r0, rn)] if own else dst, dst_ref=dst,
                send_sem=send_sems.at[p * 7 + k], recv_sem=recv_sems.at[p * 7 + k],
                device_id=to, device_id_type=MESH)

        nbr1, nbr2, diag = route(c)
        mine = [pltpu.make_async_copy(ins[a], outs[a].at[idx(*me)], local_sems.at[a]) for a in range(n)]
        for cp in mine:
            cp.start()
        sent = []
        for p in range(n_p):
            for k, to in enumerate((sibling, nbr1, nbr2)):
                sent.append(copy(p, k, me, to, own=True))
        for cp in sent:
            cp.start()
        for k_in, block, onward in ((1, nbr1, ((3, nbr2), (4, sibling))), (2, nbr2, ((5, sibling),)),
                                    (3, diag, ((6, sibling),))):
            for p in range(n_p):
                copy(p, k_in, block, me).wait_recv()
                for k_out, to in onward:
                    cp = copy(p, k_out, block, to)
                    cp.start()
                    sent.append(cp)
        s1, s2, sd = route(1 - c)
        for k_in, block in ((0, sibling), (4, s1), (5, s2), (6, sd)):
            for p in range(n_p):
                copy(p, k_in, block, me).wait_recv()
        for cp in sent:
            cp.wait_send()
        for cp in mine:
            cp.wait()

    return pl.pallas_call(
        body, name=name,
        out_shape=[jax.ShapeDtypeStruct((N_DEV,) + a.shape, a.dtype) for a in arrs],
        in_specs=[ANY] * n, out_specs=[ANY] * n,
        scratch_shapes=[pltpu.SemaphoreType.DMA((7 * n_p,)), pltpu.SemaphoreType.DMA((7 * n_p,)),
                        pltpu.SemaphoreType.DMA((n,))],
    )(*arrs)


N_CHIPS = 4


def _sibling_exchange(arrs, name):
    n = len(arrs)

    def body(*refs):
        ins, outs = refs[:n], refs[n:2 * n]
        send_sems, recv_sems = refs[2 * n:]
        x, y, c = _my_place()
        sibling = (x, y, 1 - c)

        def copy(a, j):
            return pltpu.make_async_remote_copy(
                src_ref=ins[a].at[2 * j + (1 - c)], dst_ref=outs[a].at[j],
                send_sem=send_sems.at[a * N_CHIPS + j], recv_sem=recv_sems.at[a * N_CHIPS + j],
                device_id=sibling, device_id_type=MESH)

        cps = [copy(a, j) for j in range(N_CHIPS) for a in range(n)]
        for cp in cps:
            cp.start()
        for cp in cps:
            cp.wait_recv()
        for cp in cps:
            cp.wait_send()

    return pl.pallas_call(
        body, name=name,
        out_shape=[jax.ShapeDtypeStruct((N_CHIPS,) + a.shape[1:], a.dtype) for a in arrs],
        in_specs=[ANY] * n, out_specs=[ANY] * n,
        scratch_shapes=[pltpu.SemaphoreType.DMA((N_CHIPS * n,)), pltpu.SemaphoreType.DMA((N_CHIPS * n,))],
    )(*arrs)


def _pair_add(mine, recv, core, name):
    n = len(mine)

    def body(c_ref, *refs):
        for a in range(n):
            refs[2 * n + a][...] = (refs[a][...].astype(F32) + refs[n + a][...].astype(F32)).astype(BF16)

    def blk(a):
        return (None,) + a.shape[1:]

    grid_spec = pltpu.PrefetchScalarGridSpec(
        num_scalar_prefetch=1, grid=(N_CHIPS,),
        in_specs=[pl.BlockSpec(blk(a), lambda j, c_ref: (2 * j + c_ref[0], 0, 0)) for a in mine]
        + [pl.BlockSpec(blk(a), lambda j, c_ref: (j, 0, 0)) for a in recv],
        out_specs=[pl.BlockSpec(blk(a), lambda j, c_ref: (j, 0, 0)) for a in recv])
    return pl.pallas_call(
        body, name=name, grid_spec=grid_spec,
        out_shape=[jax.ShapeDtypeStruct(a.shape, BF16) for a in recv],
        compiler_params=_params(32),
    )(core, *mine, *recv)


HBM = pl.BlockSpec(memory_space=pltpu.HBM)
SEM = pl.BlockSpec(memory_space=pltpu.SEMAPHORE)
N_PEER_CHIPS = 3


def _chip_copies(srcs, lands, send_sems, recv_sems):
    x, y, c = _my_place()
    my_chip = 2 * x + y
    peers = [(x, 1 - y), (1 - x, y), (1 - x, 1 - y)]
    cps = []
    for k, (px, py) in enumerate(peers):
        for a in range(len(srcs)):
            j = a * N_PEER_CHIPS + k
            cps.append(pltpu.make_async_remote_copy(
                src_ref=srcs[a].at[2 * px + py], dst_ref=lands[a].at[my_chip],
                send_sem=send_sems[j], recv_sem=recv_sems[j],
                device_id=(px, py, c), device_id_type=MESH))
    return cps


N_PEERS = N_DEV - 1


def _gather_copies(srcs, lands, send_sems, recv_sems):
    x, y, c = _my_place()
    me_idx = 4 * x + 2 * y + c
    flips = [(0, 0, 1), (0, 1, 0), (1, 0, 0), (0, 1, 1), (1, 0, 1), (1, 1, 0), (1, 1, 1)]
    cps = []
    for k, (fx, fy, fc) in enumerate(flips):
        peer = ((1 - x) if fx else x, (1 - y) if fy else y, (1 - c) if fc else c)
        for a in range(len(srcs)):
            j = a * N_PEERS + k
            cps.append(pltpu.make_async_remote_copy(
                src_ref=srcs[a], dst_ref=lands[a].at[me_idx], send_sem=send_sems[j], recv_sem=recv_sems[j],
                device_id=peer, device_id_type=MESH))
    return cps


def _split_start(copies, per_array, arrs, lands, name):
    arrs, lands = list(arrs), list(lands)
    n = len(arrs)
    k = n * per_array

    def body(*refs):
        srcs, land_refs = refs[:n], refs[n:2 * n]
        send_sems, recv_sems = refs[2 * n:2 * n + k], refs[2 * n + k:2 * n + 2 * k]
        token = refs[-1]
        for cp in copies(srcs, land_refs, send_sems, recv_sems):
            cp.start()
        token[...] = jnp.zeros_like(token)

    hbm_arrs = [pltpu.with_memory_space_constraint(a, pltpu.HBM) for a in arrs]
    lands = [pltpu.with_memory_space_constraint(a, pltpu.HBM) for a in lands]
    res = pl.pallas_call(
        body, name=name,
        out_shape=[pltpu.SemaphoreType.DMA(())] * (2 * k) + [pltpu.HBM(a.shape, a.dtype) for a in arrs + lands]
        + [jax.ShapeDtypeStruct((8, 128), F32)],
        in_specs=[HBM] * (2 * n),
        out_specs=[SEM] * (2 * k) + [HBM] * (2 * n) + [pl.BlockSpec(memory_space=pltpu.VMEM)],
        input_output_aliases={a: 2 * k + a for a in range(2 * n)},
        compiler_params=pltpu.CompilerParams(has_side_effects=pltpu.SideEffectType.DATAFLOW_SIDE_EFFECTING),
    )(*hbm_arrs, *lands)
    return res[:k], res[k:2 * k], res[2 * k:2 * k + n], res[2 * k + n:2 * k + 2 * n], res[-1]


def _split_wait(copies, per_array, send_sems, recv_sems, srcs, lands, after, name):
    n = len(srcs)
    k = n * per_array

    def body(*refs):
        src_refs, land_refs = refs[:n], refs[n:2 * n]
        s_sems, r_sems = refs[2 * n:2 * n + k], refs[2 * n + k:2 * n + 2 * k]
        for cp in copies(src_refs, land_refs, s_sems, r_sems):
            cp.wait_send()
            cp.wait_recv()

    res = pl.pallas_call(
        body, name=name,
        out_shape=[pltpu.HBM(a.shape, a.dtype) for a in list(srcs) + list(lands)],
        in_specs=[HBM] * (2 * n) + [SEM] * (2 * k) + [ANY],
        out_specs=[HBM] * (2 * n),
        input_output_aliases={a: a for a in range(2 * n)},
        compiler_params=pltpu.CompilerParams(has_side_effects=pltpu.SideEffectType.DATAFLOW_SIDE_EFFECTING),
    )(*srcs, *lands, *send_sems, *recv_sems, after)
    return res[:n], res[n:]


def _chip_exchange_start(arrs, name):
    return _split_start(_chip_copies, N_PEER_CHIPS, arrs, [lax.empty(a.shape, a.dtype) for a in arrs], name)


def _chip_exchange_wait(send_sems, recv_sems, srcs, lands, after, name):
    return _split_wait(_chip_copies, N_PEER_CHIPS, send_sems, recv_sems, srcs, lands, after, name)


def _gather_start(arrs, me_idx, name):
    lands = [lax.dynamic_update_slice(lax.empty((N_DEV,) + a.shape, a.dtype), a[None], (me_idx, 0, 0)) for a in arrs]
    return _split_start(_gather_copies, N_PEERS, arrs, lands, name)


def _gather_wait(send_sems, recv_sems, srcs, lands, after, name):
    return _split_wait(_gather_copies, N_PEERS, send_sems, recv_sems, srcs, lands, after, name)[1]


def _proj_fwd(x, g_pre, w_int, tabs):
    s = x.shape[0]
    tm = min(512, s)

    def body(x_ref, g_ref, cs_ref, s1_ref, s2_ref, w_hbm,
             h_ref, pa_ref, pq_ref, pkv_ref, pbz_ref, pmq_ref, pmz_ref, pg_ref, w_vm, sems):
        _load_once([(w_hbm, w_vm)], sems)
        xf = x_ref[...]
        r = lax.rsqrt(jnp.mean(xf * xf, axis=-1, keepdims=True) + EPS)
        h = ((xf * r) * g_ref[...]).astype(BF16)
        h_ref[...] = h
        cs, s1, s2 = cs_ref[...], s1_ref[...], s2_ref[...]

        def mm(seg, c0, width):
            return _dot_nt(h, w_vm[seg[0] + c0:seg[0] + c0 + width, :])

        for c0 in range(0, SEG_A[1], 512):
            pa_ref[:, c0:c0 + 512] = mm(SEG_A, c0, 512).astype(BF16)
        q = mm(SEG_BQ, 0, 512)
        for b in range(4):
            pq_ref[:, 128 * b:128 * b + 128] = _rope(q[:, 128 * b:128 * b + 128], cs, s1, s2).astype(BF16)
        kv = mm(SEG_BKV, 0, 256)
        pkv_ref[:, 0:128] = _rope(kv[:, 0:128], cs, s1, s2).astype(BF16)
        pkv_ref[:, 128:256] = kv[:, 128:256].astype(BF16)
        pbz_ref[...] = mm(SEG_BZ, 0, 512).astype(BF16)
        pmq_ref[...] = mm(SEG_MQ, 0, 512).astype(BF16)
        pmz_ref[...] = mm(SEG_MZ, 0, 512).astype(BF16)
        for c0 in range(0, SEG_G[1], 512):
            pg_ref[:, c0:c0 + 512] = mm(SEG_G, c0, 512).astype(BF16)

    widths = (D_MODEL, 2048, 512, 256, 512, 512, 512, 3072)
    return pl.pallas_call(
        body, name="proj_fwd", grid=(s // tm,),
        out_shape=[jax.ShapeDtypeStruct((s, w), BF16) for w in widths],
        in_specs=[_rows(tm, D_MODEL), _full((1, D_MODEL)), _rows(tm, 128), _rows(tm, 128), _rows(tm, 128), ANY],
        out_specs=[_rows(tm, w) for w in widths],
        scratch_shapes=[pltpu.VMEM((IN_WIDTH, D_MODEL), BF16), pltpu.SemaphoreType.DMA((1,))],
        compiler_params=_params(52),
    )(x, g_pre, *tabs, w_int)


def _mem_kv_fwd(mem, g_mem, w_mkv):
    m = mem.shape[0]

    def body(mem_ref, g_ref, w_ref, mn_ref, mkv_ref):
        xf = mem_ref[...]
        r = lax.rsqrt(jnp.mean(xf * xf, axis=-1, keepdims=True) + EPS)
        mn = ((xf * r) * g_ref[...]).astype(BF16)
        mn_ref[...] = mn
        mkv_ref[...] = _dot(mn, w_ref[...]).astype(BF16)

    return pl.pallas_call(
        body, name="mem_kv_fwd", grid=(1,),
        out_shape=[jax.ShapeDtypeStruct((m, D_MODEL), BF16)] * 2,
        in_specs=[_full((m, D_MODEL)), _full((1, D_MODEL)), _full((D_MODEL, D_MODEL))],
        out_specs=[_full((m, D_MODEL))] * 2,
        compiler_params=_params(32),
    )(mem, g_mem, w_mkv)


def _halo_specs(s, tm, rows, width):
    nblk = s // rows
    prev = pl.BlockSpec((rows, width), lambda i: (jnp.maximum(i * (tm // rows) - 1, 0), 0))
    nxt = pl.BlockSpec((rows, width), lambda i: (jnp.minimum((i + 1) * (tm // rows), nblk - 1), 0))
    return prev, nxt


def _conv_common(pa, prev_row, next_row, w, first, last, tm):
    b, c, u, z = (pa[:, 512 * k:512 * k + 512] for k in range(4))
    cu = c * u
    cu_prev = jnp.where(first, 0.0, prev_row[:, 512:1024] * prev_row[:, 1024:1536])
    cu_next = jnp.where(last, 0.0, next_row[:, 512:1024] * next_row[:, 1024:1536])
    row = lax.broadcasted_iota(jnp.int32, (tm, 512), 0)
    cu_m1 = jnp.where(row == 0, cu_prev, pltpu.roll(cu, 1, 0))
    cu_p1 = jnp.where(row == tm - 1, cu_next, pltpu.roll(cu, tm - 1, 0))
    y = cu_m1 * w[0:1] + cu * w[1:2] + cu_p1 * w[2:3]
    sig = _sigmoid(z)
    return b, c, u, z, cu, cu_m1, cu_p1, y, sig, row


def _conv_fwd(pa, w_conv):
    s = pa.shape[0]
    tm = min(512, s)
    nt = s // tm

    def body(pa_ref, pp_ref, pn_ref, w_ref, ya_ref):
        i = pl.program_id(0)
        prev_row = pp_ref[...].astype(F32)[15:16, :]
        next_row = pn_ref[...].astype(F32)[0:1, :]
        b, _, _, z, _, _, _, y, sig, _ = _conv_common(
            pa_ref[...].astype(F32), prev_row, next_row, w_ref[...], i == 0, i == nt - 1, tm)
        ya_ref[...] = (b * y * (z * sig)).astype(BF16)

    prev, nxt = _halo_specs(s, tm, 16, 2048)
    return pl.pallas_call(
        body, name="conv_fwd", grid=(nt,),
        out_shape=jax.ShapeDtypeStruct((s, 512), BF16),
        in_specs=[_rows(tm, 2048), prev, nxt, _full((3, 512))],
        out_specs=_rows(tm, 512),
        compiler_params=_params(48),
    )(pa, pa, pa, w_conv)


def _heads_to_lanes(a, g, row):
    low = row < HEAD_DIM
    parts = []
    for b in (2 * g, 2 * g + 1):
        t = jnp.transpose(a[:, 128 * b:128 * b + 128])
        swapped = pltpu.roll(t, HEAD_DIM, 0)
        if g == 0:
            parts += [jnp.where(low, t, 0.0), jnp.where(low, swapped, 0.0)]
        else:
            parts += [jnp.where(low, 0.0, swapped), jnp.where(low, 0.0, t)]
    return jnp.concatenate(parts, axis=1)


def _lanes_to_heads(t0, t1, row):
    low = row < HEAD_DIM
    blocks = []
    for b in range(4):
        g = b // 2
        tg = (t0, t1)[g]
        je = 2 * (b - 2 * g)
        even, odd = tg[:, 128 * je:128 * je + 128], tg[:, 128 * je + 128:128 * je + 256]
        if g == 0:
            t = jnp.where(low, even, pltpu.roll(odd, HEAD_DIM, 0))
        else:
            t = jnp.where(low, pltpu.roll(even, HEAD_DIM, 0), odd)
        blocks.append(jnp.transpose(t))
    return jnp.concatenate(blocks, axis=1)


WINDOW_KEYS = 3 * ATTN_BLOCK
STACKED = 4 * ATTN_BLOCK
KEY_CHUNK = 32


def _fill_band_bias(bias, nb):
    assert nb >= 2
    c = lax.broadcasted_iota(jnp.int32, (WINDOW_KEYS, STACKED), 0)
    r = lax.broadcasted_iota(jnp.int32, (WINDOW_KEYS, STACKED), 1) & (ATTN_BLOCK - 1)
    band = (c >= r) & (c <= r + 2 * ATTN_BLOCK)
    for v, ok in enumerate((band, band & (c >= ATTN_BLOCK), band & (c < 2 * ATTN_BLOCK))):
        bias[v] = jnp.where(ok, 0.0, -jnp.inf)


def _bias_variant(n, nb):
    return jnp.where(n == 0, 1, jnp.where(n == nb - 1, 2, 0))


def _sink_row(sink_ref, g):
    return jnp.concatenate([jnp.full((1, ATTN_BLOCK), sink_ref[4 * g + j], F32) for j in range(4)], axis=1)


def _softmax_keys_major(sc, bias, variant, sink, e_scr):
    chunks = [pl.ds(k * KEY_CHUNK, KEY_CHUNK) for k in range(WINDOW_KEYS // KEY_CHUNK)]
    rows = [slice(k * KEY_CHUNK, (k + 1) * KEY_CHUNK) for k in range(WINDOW_KEYS // KEY_CHUNK)]
    m_run = jnp.full((KEY_CHUNK, STACKED), -jnp.inf, F32)
    for ck, rw in zip(chunks, rows):
        m_run = jnp.maximum(m_run, sc[rw] * ATTN_SCALE + bias[variant, ck, :])
    m = jnp.maximum(jnp.max(m_run, axis=0, keepdims=True), sink)
    l_run = jnp.zeros((KEY_CHUNK, STACKED), F32)
    for ck, rw in zip(chunks, rows):
        e = jnp.exp(sc[rw] * ATTN_SCALE + bias[variant, ck, :] - m)
        l_run += e
        e_scr[rw, :] = e.astype(BF16)
    es = jnp.exp(sink - m)
    inv = 1.0 / (jnp.sum(l_run, axis=0, keepdims=True) + es)
    return inv, es * inv


def _fill_padded(kv_ref, kpad, vpad, s):
    zero = jnp.zeros((ATTN_BLOCK, 128), BF16)
    kpad[0:ATTN_BLOCK, :] = zero
    vpad[0:ATTN_BLOCK, :] = zero
    kpad[ATTN_BLOCK + s:2 * ATTN_BLOCK + s, :] = zero
    vpad[ATTN_BLOCK + s:2 * ATTN_BLOCK + s, :] = zero
    kpad[ATTN_BLOCK:ATTN_BLOCK + s, :] = kv_ref[:, 0:128]
    vpad[ATTN_BLOCK:ATTN_BLOCK + s, :] = kv_ref[:, 128:256]


def _attn_fwd(pq, pkv, pbz, sink):
    s = pq.shape[0]
    nb = s // ATTN_BLOCK

    def body(sink_ref, q_ref, z_ref, kv_ref, yb_ref, kpad, vpad, bias, e_scr):
        n = pl.program_id(0)

        @pl.when(n == 0)
        def _():
            _fill_padded(kv_ref, kpad, vpad, s)
            _fill_band_bias(bias, nb)

        row = lax.broadcasted_iota(jnp.int32, (ATTN_BLOCK, 128), 0)
        start = pl.multiple_of(n * ATTN_BLOCK, ATTN_BLOCK)
        kw, vw = kpad[pl.ds(start, WINDOW_KEYS), :], vpad[pl.ds(start, WINDOW_KEYS), :]
        qf = q_ref[...].astype(F32)
        variant = _bias_variant(n, nb)
        outs = []
        for g in range(2):
            qt = _heads_to_lanes(qf, g, row).astype(BF16)
            inv, _ = _softmax_keys_major(_dot(kw, qt), bias, variant, _sink_row(sink_ref, g), e_scr)
            outs.append(_dot_tn(vw, e_scr[...]) * inv)
        attn = _lanes_to_heads(outs[0], outs[1], row)
        z = z_ref[...].astype(F32)
        yb_ref[...] = (attn * (z * _sigmoid(z))).astype(BF16)

    return pl.pallas_call(
        body, name="attn_fwd", grid=(nb,),
        out_shape=jax.ShapeDtypeStruct((s, 512), BF16),
        in_specs=[pl.BlockSpec(memory_space=pltpu.SMEM), _rows(ATTN_BLOCK, 512), _rows(ATTN_BLOCK, 512),
                  _full((s, 256))],
        out_specs=_rows(ATTN_BLOCK, 512),
        scratch_shapes=[pltpu.VMEM((s + 2 * ATTN_BLOCK, 128), BF16)] * 2
        + [pltpu.VMEM((3, WINDOW_KEYS, STACKED), F32), pltpu.VMEM((WINDOW_KEYS, STACKED), BF16)],
        compiler_params=_params(32),
    )(sink, pq, pbz, pkv)


def _mem_softmax_t(q, mk):
    sc = _dot_nt(mk, q) * MEM_SCALE
    e = jnp.exp(sc - jnp.max(sc, axis=0, keepdims=True))
    return e * (1.0 / jnp.sum(e, axis=0, keepdims=True))


def _mem_attn_fwd(pmq, pmz, mkv):
    s = pmq.shape[0]
    m = mkv.shape[0]
    tm = min(512, s)

    def body(q_ref, z_ref, mk_ref, mv_ref, ym_ref):
        z = z_ref[...].astype(F32)
        sz = z * _sigmoid(z)
        for h in range(MEM_HEADS):
            cols = slice(128 * h, 128 * h + 128)
            pt = _mem_softmax_t(q_ref[:, cols], mk_ref[:, cols])
            o = _dot_tn(pt.astype(BF16), mv_ref[:, cols])
            ym_ref[:, cols] = (o * sz[:, cols]).astype(BF16)

    return pl.pallas_call(
        body, name="mem_attn_fwd", grid=(s // tm,),
        out_shape=jax.ShapeDtypeStruct((s, 512), BF16),
        in_specs=[_rows(tm, 512), _rows(tm, 512), pl.BlockSpec((m, 512), lambda i: (0, 0)),
                  pl.BlockSpec((m, 512), lambda i: (0, 1))],
        out_specs=_rows(tm, 512),
        compiler_params=_params(32),
    )(pmq, pmz, mkv, mkv)


def _mid(ya, yb, ym, pg, x, target, g_post, w_up, w_out):
    s = x.shape[0]
    tm = min(256, s)
    nt = s // tm

    def body(ya_ref, yb_ref, ym_ref, pg_ref, x_ref, t_ref, gp_ref, wup_hbm, wout_hbm,
             dg_ref, dya_ref, dyb_ref, dym_ref, dy_ref, loss_ref, ggp_ref, mb_ref, dob_ref, du_ref,
             wup_vm, wout_vm, sems):
        i = pl.program_id(0)
        _load_once([(wup_hbm.at[d], wup_vm.at[:, pl.ds(128 * d, 128)]) for d in range(N_DEV)]
                   + [(wout_hbm, wout_vm)], sems)

        @pl.when(i == 0)
        def _():
            loss_ref[...] = jnp.zeros_like(loss_ref)
            ggp_ref[...] = jnp.zeros_like(ggp_ref)

        ys = (ya_ref[...], yb_ref[...], ym_ref[...])
        us = [_dot(ys[k], wup_vm[512 * k:512 * k + 512, :]) for k in range(3)]
        gates = [_sigmoid(pg_ref[:, 1024 * k:1024 * k + 1024].astype(F32)) for k in range(3)]
        merged = gates[0] * us[0] + gates[1] * us[1] + gates[2] * us[2]
        mb = merged.astype(BF16)
        mb_ref[...] = mb
        out = _dot(mb, wout_vm[...])
        r = lax.rsqrt(jnp.mean(out * out, axis=-1, keepdims=True) + EPS)
        on = out * r
        gp = gp_ref[...]
        err = (x_ref[...] + on * gp) - t_ref[...]
        loss_ref[...] += 0.5 * jnp.sum(err * err) * (1.0 / D_MODEL)
        dy = err * (1.0 / D_MODEL)
        dy_ref[...] = dy
        ggp_ref[...] += jnp.sum(dy * on, axis=0, keepdims=True)
        a = dy * gp
        d_out = r * (a - on * jnp.mean(a * on, axis=-1, keepdims=True))
        dob = d_out.astype(BF16)
        dob_ref[...] = dob
        d_merged = _dot_nt(dob, wout_vm[...])
        d_refs = (dya_ref, dyb_ref, dym_ref)
        for k in range(3):
            g = gates[k]
            dg_ref[:, 1024 * k:1024 * k + 1024] = (d_merged * us[k] * g * (1.0 - g)).astype(BF16)
            du = (d_merged * g).astype(BF16)
            du_ref[k] = du
            d_refs[k][...] = _dot_nt(du, wup_vm[512 * k:512 * k + 512, :])

    return pl.pallas_call(
        body, name="mid", grid=(nt,),
        out_shape=[jax.ShapeDtypeStruct((s, 3072), BF16)] + [jax.ShapeDtypeStruct((s, 512), F32)] * 3
        + [jax.ShapeDtypeStruct((s, D_MODEL), F32), jax.ShapeDtypeStruct((8, 128), F32),
           jax.ShapeDtypeStruct((1, D_MODEL), F32), jax.ShapeDtypeStruct((s, D_MODEL), BF16),
           jax.ShapeDtypeStruct((s, D_MODEL), BF16), jax.ShapeDtypeStruct((3, s, D_MODEL), BF16)],
        in_specs=[_rows(tm, 512)] * 3 + [_rows(tm, 3072), _rows(tm, D_MODEL), _rows(tm, D_MODEL),
                                         _full((1, D_MODEL)), ANY, ANY],
        out_specs=[_rows(tm, 3072)] + [_rows(tm, 512)] * 3
        + [_rows(tm, D_MODEL), _full((8, 128)), _full((1, D_MODEL)), _rows(tm, D_MODEL), _rows(tm, D_MODEL),
           pl.BlockSpec((3, tm, D_MODEL), lambda i: (0, i, 0))],
        scratch_shapes=[pltpu.VMEM((1536, D_MODEL), BF16), pltpu.VMEM((D_MODEL, D_MODEL), BF16),
                        pltpu.SemaphoreType.DMA((N_DEV + 1,))],
        compiler_params=_params(56),
    )(ya, yb, ym, pg, x, target, g_post, w_up, w_out)


def _gw_mid(mb, dob, ys, du):
    s = mb.shape[0]
    tn = 256

    def out_body(mb_ref, dob_ref, o_ref):
        o_ref[...] = _dot_tn(mb_ref[...], dob_ref[...]).astype(BF16)

    gw_out = pl.pallas_call(
        out_body, name="gw_out", grid=(D_MODEL // tn,),
        out_shape=jax.ShapeDtypeStruct((D_MODEL, D_MODEL), BF16),
        in_specs=[pl.BlockSpec((s, tn), lambda j: (0, j)), _full((s, D_MODEL))],
        out_specs=pl.BlockSpec((tn, D_MODEL), lambda j: (j, 0)),
        compiler_params=_params(48),
    )(mb, dob)

    per = 512 // tn

    def up_body(ya_ref, yb_ref, ym_ref, du_ref, o_ref):
        j = pl.program_id(0)
        for k, y_ref in enumerate((ya_ref, yb_ref, ym_ref)):
            @pl.when(j // per == k)
            def _(y_ref=y_ref):
                res = _dot_tn(y_ref[...], du_ref[...])
                for d in range(N_DEV):
                    o_ref[d] = res[:, 128 * d:128 * d + 128].astype(BF16)

    def y_spec(k):
        return pl.BlockSpec((s, tn), lambda j: (0, jnp.clip(j - per * k, 0, per - 1)))

    gw_up = pl.pallas_call(
        up_body, name="gw_up", grid=(3 * per,),
        out_shape=jax.ShapeDtypeStruct((N_DEV, 1536, 128), BF16),
        in_specs=[y_spec(0), y_spec(1), y_spec(2), pl.BlockSpec((None, s, D_MODEL), lambda j: (j // per, 0, 0))],
        out_specs=pl.BlockSpec((N_DEV, tn, 128), lambda j: (0, j, 0)),
        compiler_params=_params(48),
    )(*ys, du)
    return gw_out, gw_up


def _conv_bwd(pa, dya, w_conv):
    s = pa.shape[0]
    tm = min(512, s)
    nt = s // tm

    def body(pa_ref, pp_ref, pn_ref, d_ref, dp_ref, dn_ref, w_ref, da_ref, gw_ref):
        i = pl.program_id(0)
        first, last = i == 0, i == nt - 1

        @pl.when(first)
        def _():
            gw_ref[...] = jnp.zeros_like(gw_ref)

        w = w_ref[...]
        prev_row = pp_ref[...].astype(F32)[15:16, :]
        next_row = pn_ref[...].astype(F32)[0:1, :]
        b, c, u, z, cu, cu_m1, cu_p1, y, sig, row = _conv_common(
            pa_ref[...].astype(F32), prev_row, next_row, w, first, last, tm)
        sz = z * sig
        dya_t = d_ref[...]
        d_y = dya_t * b * sz

        def halo_dy(p_row, d_row):
            zz = p_row[:, 1536:2048]
            return d_row * p_row[:, 0:512] * (zz * _sigmoid(zz))

        dy_prev = jnp.where(first, 0.0, halo_dy(prev_row, dp_ref[7:8, :]))
        dy_next = jnp.where(last, 0.0, halo_dy(next_row, dn_ref[0:1, :]))
        dy_m1 = jnp.where(row == 0, dy_prev, pltpu.roll(d_y, 1, 0))
        dy_p1 = jnp.where(row == tm - 1, dy_next, pltpu.roll(d_y, tm - 1, 0))
        d_cu = dy_p1 * w[0:1] + d_y * w[1:2] + dy_m1 * w[2:3]
        da_ref[:, 0:512] = (dya_t * y * sz).astype(BF16)
        da_ref[:, 512:1024] = (d_cu * u).astype(BF16)
        da_ref[:, 1024:1536] = (d_cu * c).astype(BF16)
        da_ref[:, 1536:2048] = (dya_t * b * y * (sig * (1.0 + z * (1.0 - sig)))).astype(BF16)
        gw_ref[0:1, :] += jnp.sum(d_y * cu_m1, axis=0, keepdims=True)
        gw_ref[1:2, :] += jnp.sum(d_y * cu, axis=0, keepdims=True)
        gw_ref[2:3, :] += jnp.sum(d_y * cu_p1, axis=0, keepdims=True)

    prev, nxt = _halo_specs(s, tm, 16, 2048)
    dprev, dnxt = _halo_specs(s, tm, 8, 512)
    return pl.pallas_call(
        body, name="conv_bwd", grid=(nt,),
        out_shape=[jax.ShapeDtypeStruct((s, 2048), BF16), jax.ShapeDtypeStruct((8, 512), F32)],
        in_specs=[_rows(tm, 2048), prev, nxt, _rows(tm, 512), dprev, dnxt, _full((3, 512))],
        out_specs=[_rows(tm, 2048), _full((8, 512))],
        compiler_params=_params(48),
    )(pa, pa, pa, dya, dya, dya, w_conv)


def _attn_bwd(pq, pkv, pbz, dyb, sink, tabs):
    s = pq.shape[0]
    nb = s // ATTN_BLOCK

    def body(sink_ref, q_ref, z_ref, d_ref, cs_ref, s1_ref, s2_ref, kv_ref, csf_ref, s1f_ref, s2f_ref,
             dq_ref, dz_ref, dkv_ref, gs_ref, kpad, vpad, dk_acc, dv_acc, bias, e_scr, ds_scr):
        n = pl.program_id(0)

        @pl.when(n == 0)
        def _():
            _fill_padded(kv_ref, kpad, vpad, s)
            _fill_band_bias(bias, nb)
            dk_acc[...] = jnp.zeros_like(dk_acc)
            dv_acc[...] = jnp.zeros_like(dv_acc)
            gs_ref[...] = jnp.zeros_like(gs_ref)

        row = lax.broadcasted_iota(jnp.int32, (ATTN_BLOCK, 128), 0)
        start = pl.multiple_of(n * ATTN_BLOCK, ATTN_BLOCK)
        kw, vw = kpad[pl.ds(start, WINDOW_KEYS), :], vpad[pl.ds(start, WINDOW_KEYS), :]
        qf = q_ref[...].astype(F32)
        variant = _bias_variant(n, nb)
        z = z_ref[...].astype(F32)
        sig = _sigmoid(z)
        dyb_t = d_ref[...]
        d_attn = dyb_t * (z * sig)
        outs, dqs = [], []
        dk_w = jnp.zeros((WINDOW_KEYS, 128), F32)
        dv_w = jnp.zeros((WINDOW_KEYS, 128), F32)
        for g in range(2):
            qt = _heads_to_lanes(qf, g, row)
            inv, p_sink = _softmax_keys_major(
                _dot(kw, qt.astype(BF16)), bias, variant, _sink_row(sink_ref, g), e_scr)
            ot = _dot_tn(vw, e_scr[...]) * inv
            outs.append(ot)
            dot_ = _heads_to_lanes(d_attn, g, row)
            delta = jnp.sum(dot_ * ot, axis=0, keepdims=True)
            dpt = _dot(vw, dot_.astype(BF16))
            for k in range(WINDOW_KEYS // KEY_CHUNK):
                rw = slice(k * KEY_CHUNK, (k + 1) * KEY_CHUNK)
                ds_scr[rw, :] = (e_scr[rw, :].astype(F32) * (dpt[rw] - delta)).astype(BF16)
            sink_part = p_sink * delta
            for j in range(4):
                h = 4 * g + j
                gs_ref[h:h + 1, :] -= jnp.sum(sink_part[:, 128 * j:128 * j + 128])
            dqs.append(_dot_tn(kw, ds_scr[...]) * (inv * ATTN_SCALE))
            dk_w += _dot_nt(ds_scr[...], (qt * inv).astype(BF16)) * ATTN_SCALE
            dv_w += _dot_nt(e_scr[...], (dot_ * inv).astype(BF16))
        dk_acc[pl.ds(start, WINDOW_KEYS), :] += dk_w
        dv_acc[pl.ds(start, WINDOW_KEYS), :] += dv_w
        attn = _lanes_to_heads(outs[0], outs[1], row)
        dz_ref[...] = (dyb_t * attn * (sig * (1.0 + z * (1.0 - sig)))).astype(BF16)
        dq = _lanes_to_heads(dqs[0], dqs[1], row)
        cs, s1, s2 = cs_ref[...], s1_ref[...], s2_ref[...]
        for b in range(4):
            dq_ref[:, 128 * b:128 * b + 128] = _rope_t(dq[:, 128 * b:128 * b + 128], cs, s1, s2).astype(BF16)

        @pl.when(n == nb - 1)
        def _():
            dk = dk_acc[ATTN_BLOCK:ATTN_BLOCK + s, :]
            dkv_ref[:, 0:128] = _rope_t(dk, csf_ref[...], s1f_ref[...], s2f_ref[...]).astype(BF16)
            dkv_ref[:, 128:256] = dv_acc[ATTN_BLOCK:ATTN_BLOCK + s, :].astype(BF16)

    tile = _rows(ATTN_BLOCK, 512)
    tab = _rows(ATTN_BLOCK, 128)
    return pl.pallas_call(
        body, name="attn_bwd", grid=(nb,),
        out_shape=[jax.ShapeDtypeStruct((s, 512), BF16), jax.ShapeDtypeStruct((s, 512), BF16),
                   jax.ShapeDtypeStruct((s, 256), BF16), jax.ShapeDtypeStruct((8, 128), F32)],
        in_specs=[pl.BlockSpec(memory_space=pltpu.SMEM), tile, tile, tile, tab, tab, tab,
                  _full((s, 256)), _full((s, 128)), _full((s, 128)), _full((s, 128))],
        out_specs=[tile, tile, _full((s, 256)), _full((8, 128))],
        scratch_shapes=[pltpu.VMEM((s + 2 * ATTN_BLOCK, 128), BF16)] * 2
        + [pltpu.VMEM((s + 2 * ATTN_BLOCK, 128), F32)] * 2
        + [pltpu.VMEM((3, WINDOW_KEYS, STACKED), F32)] + [pltpu.VMEM((WINDOW_KEYS, STACKED), BF16)] * 2,
        compiler_params=_params(48),
    )(sink, pq, pbz, dyb, *tabs, pkv, *tabs)


def _mem_attn_bwd(pmq, pmz, mkv, dym):
    s = pmq.shape[0]
    m = mkv.shape[0]
    tm = min(512, s)

    def body(q_ref, z_ref, d_ref, mk_ref, mv_ref, dq_ref, dz_ref, dmkv_ref):
        @pl.when(pl.program_id(0) == 0)
        def _():
            dmkv_ref[...] = jnp.zeros_like(dmkv_ref)

        z = z_ref[...].astype(F32)
        sig = _sigmoid(z)
        dym_t = d_ref[...]
        d_attn = dym_t * (z * sig)
        dsilu = sig * (1.0 + z * (1.0 - sig))
        for h in range(MEM_HEADS):
            cols = slice(128 * h, 128 * h + 128)
            q, mk, mv = q_ref[:, cols], mk_ref[:, cols], mv_ref[:, cols]
            pt = _mem_softmax_t(q, mk)
            pb = pt.astype(BF16)
            o = _dot_tn(pb, mv)
            dob = d_attn[:, cols].astype(BF16)
            dpt = _dot_nt(mv, dob)
            dst = (pt * (dpt - jnp.sum(pt * dpt, axis=0, keepdims=True))).astype(BF16)
            dq_ref[:, cols] = (_dot_tn(dst, mk) * MEM_SCALE).astype(BF16)
            dz_ref[:, cols] = (dym_t[:, cols] * o * dsilu[:, cols]).astype(BF16)
            dmkv_ref[:, cols] += _dot(dst, q) * MEM_SCALE
            dmkv_ref[:, 512 + 128 * h:512 + 128 * h + 128] += _dot(pb, dob)

    return pl.pallas_call(
        body, name="mem_attn_bwd", grid=(s // tm,),
        out_shape=[jax.ShapeDtypeStruct((s, 512), BF16), jax.ShapeDtypeStruct((s, 512), BF16),
                   jax.ShapeDtypeStruct((m, D_MODEL), F32)],
        in_specs=[_rows(tm, 512), _rows(tm, 512), _rows(tm, 512), pl.BlockSpec((m, 512), lambda i: (0, 0)),
                  pl.BlockSpec((m, 512), lambda i: (0, 1))],
        out_specs=[_rows(tm, 512), _rows(tm, 512), _full((m, D_MODEL))],
        compiler_params=_params(32),
    )(pmq, pmz, dym, mkv, mkv)


def _mem_kv_bwd(mem, g_mem, mn, dmkv, w_mkv):
    m = mem.shape[0]

    def body(mem_ref, g_ref, mn_ref, d_ref, w_ref, gw_ref, gg_ref):
        db = d_ref[...].astype(BF16)
        gw_ref[...] = _dot_tn(mn_ref[...], db).astype(BF16)
        d_mn = _dot_nt(db, w_ref[...])
        xf = mem_ref[...]
        r = lax.rsqrt(jnp.mean(xf * xf, axis=-1, keepdims=True) + EPS)
        gg_ref[...] = jnp.sum(d_mn * (xf * r), axis=0, keepdims=True)

    return pl.pallas_call(
        body, name="mem_kv_bwd", grid=(1,),
        out_shape=[jax.ShapeDtypeStruct((D_MODEL, D_MODEL), BF16), jax.ShapeDtypeStruct((1, D_MODEL), F32)],
        in_specs=[_full((m, D_MODEL)), _full((1, D_MODEL)), _full((m, D_MODEL)), _full((m, D_MODEL)),
                  _full((D_MODEL, D_MODEL))],
        out_specs=[_full((D_MODEL, D_MODEL)), _full((1, D_MODEL))],
        compiler_params=_params(32),
    )(mem, g_mem, mn, dmkv, w_mkv)


def _dh_bwd(dparts, x, dy, g_pre, w_int):
    s = x.shape[0]
    tm = min(256, s)

    def body(*refs):
        d_refs = refs[:7]
        x_ref, dy_ref, g_ref, w_hbm, gx_ref, gg_ref, w_vm, sems = refs[7:]
        _load_once([(w_hbm, w_vm)], sems)

        @pl.when(pl.program_id(0) == 0)
        def _():
            gg_ref[...] = jnp.zeros_like(gg_ref)

        d_h = jnp.zeros((tm, D_MODEL), F32)
        for d_ref, (r0, width) in zip(d_refs, SEGS):
            for c0 in range(0, width, 512):
                cw = min(512, width - c0)
                d_h += _dot(d_ref[:, c0:c0 + cw], w_vm[r0 + c0:r0 + c0 + cw, :])
        xf = x_ref[...]
        r = lax.rsqrt(jnp.mean(xf * xf, axis=-1, keepdims=True) + EPS)
        xn = xf * r
        a = d_h * g_ref[...]
        gx_ref[...] = r * (a - xn * jnp.mean(a * xn, axis=-1, keepdims=True)) + dy_ref[...]
        gg_ref[...] += jnp.sum(d_h * xn, axis=0, keepdims=True)

    return pl.pallas_call(
        body, name="dh_bwd", grid=(s // tm,),
        out_shape=[jax.ShapeDtypeStruct((s, D_MODEL), F32), jax.ShapeDtypeStruct((1, D_MODEL), F32)],
        in_specs=[_rows(tm, w) for _, w in SEGS] + [_rows(tm, D_MODEL), _rows(tm, D_MODEL), _full((1, D_MODEL)), ANY],
        out_specs=[_rows(tm, D_MODEL), _full((1, D_MODEL))],
        scratch_shapes=[pltpu.VMEM((IN_WIDTH, D_MODEL), BF16), pltpu.SemaphoreType.DMA((1,))],
        compiler_params=_params(52),
    )(*dparts, x, dy, g_pre, w_int)


def _gw_in(dparts, h):
    s = h.shape[0]
    tn = 256
    starts, counts = [], []
    for r0, width in SEGS:
        starts.append(r0 // tn)
        counts.append(width // tn)

    def body(*refs):
        d_refs = refs[:7]
        h_hbm, o_ref, h_vm, sems = refs[7:]
        _load_once([(h_hbm, h_vm)], sems)
        j = pl.program_id(0)
        for d_ref, st, cnt in zip(d_refs, starts, counts):
            @pl.when((j >= st) & (j < st + cnt))
            def _(d_ref=d_ref):
                o_ref[...] = _dot_tn(d_ref[...], h_vm[...]).astype(BF16)

    def seg_spec(st, cnt):
        return pl.BlockSpec((s, tn), lambda j: (0, jnp.clip(j - st, 0, cnt - 1)))

    return pl.pallas_call(
        body, name="gw_in", grid=(IN_WIDTH // tn,),
        out_shape=jax.ShapeDtypeStruct((IN_WIDTH, D_MODEL), BF16),
        in_specs=[seg_spec(st, cnt) for st, cnt in zip(starts, counts)] + [ANY],
        out_specs=pl.BlockSpec((tn, D_MODEL), lambda j: (j, 0)),
        scratch_shapes=[pltpu.VMEM((s, D_MODEL), BF16), pltpu.SemaphoreType.DMA((1,))],
        compiler_params=_params(52),
    )(*dparts, h)


def _adamw_math(w, g, m, v):
    m2 = ADAM_B1 * m + (1.0 - ADAM_B1) * g
    v2 = ADAM_B2 * v + (1.0 - ADAM_B2) * (g * g)
    m_hat = m2 / (1.0 - ADAM_B1 ** ADAM_STEP)
    v_hat = v2 / (1.0 - ADAM_B2 ** ADAM_STEP)
    delta = -ADAM_LR * (m_hat / (jnp.sqrt(v_hat) + ADAM_EPS) + ADAM_WD * w)
    return delta, m2, v2


def _sum_adamw(own, land, chip, block, w, m, v, name, tiles=1):
    r, c = w.shape
    rt = r // tiles

    def body(c_ref, own_ref, l1_ref, l2_ref, l3_ref, w_ref, m_ref, v_ref, g_ref, d_ref, m2_ref, v2_ref):
        g = own_ref[...].astype(F32)
        for l_ref in (l1_ref, l2_ref, l3_ref):
            g += l_ref[...].astype(F32)
        g_ref[...] = g
        d_ref[...], m2_ref[...], v2_ref[...] = _adamw_math(w_ref[...], g, m_ref[...], v_ref[...])

    def share(k):
        return pl.BlockSpec((None, rt, c), lambda i, c_ref: (jnp.bitwise_xor(c_ref[0], k), block * tiles + i, 0))

    spec = pl.BlockSpec((rt, c), lambda i, c_ref: (i, 0))
    grid_spec = pltpu.PrefetchScalarGridSpec(
        num_scalar_prefetch=1, grid=(tiles,),
        in_specs=[share(0), share(1), share(2), share(3)] + [spec] * 3, out_specs=[spec] * 4)
    return pl.pallas_call(
        body, name=name, grid_spec=grid_spec,
        out_shape=[jax.ShapeDtypeStruct((r, c), F32)] * 4,
        compiler_params=_params(48),
    )(chip, own, land, land, land, w, m, v)


def _pack_sum(packs):
    def body(p_ref, o_ref):
        acc = p_ref[0]
        for k in range(1, N_DEV):
            acc += p_ref[k]
        o_ref[...] = acc

    return pl.pallas_call(
        body, name="pack_sum", grid=(1,),
        out_shape=jax.ShapeDtypeStruct((8, D_MODEL), F32),
        in_specs=[_full((N_DEV, 8, D_MODEL))], out_specs=_full((8, D_MODEL)),
    )(packs)


def _small_adamw(ws, gs, ms, vs):
    k = len(ws)

    def body(*refs):
        w_refs, g_refs, m_refs, v_refs = (refs[j * k:(j + 1) * k] for j in range(4))
        outs = refs[4 * k:]
        for j in range(k):
            outs[j][...], outs[k + j][...], outs[2 * k + j][...] = _adamw_math(
                w_refs[j][...], g_refs[j][...], m_refs[j][...], v_refs[j][...])

    specs = [_full(w.shape) for w in ws]
    res = pl.pallas_call(
        body, name="small_adamw", grid=(1,),
        out_shape=[jax.ShapeDtypeStruct(w.shape, F32) for w in ws] * 3,
        in_specs=specs * 4, out_specs=specs * 3,
    )(*ws, *gs, *ms, *vs)
    return res[:k], res[k:2 * k], res[2 * k:]


def kernel(x, mem, g_pre, w_in, w_conv, attn_sink, g_mem, w_mem_kv, w_up_a, w_up_b, w_up_m, w_out, g_post, loss_target, m_g_pre, m_w_in, m_w_conv, m_attn_sink, m_g_mem, m_w_mem_kv, m_w_up_a, m_w_up_b, m_w_up_m, m_w_out, m_g_post, v_g_pre, v_w_in, v_w_conv, v_attn_sink, v_g_mem, v_w_mem_kv, v_w_up_a, v_w_up_b, v_w_up_m, v_w_out, v_g_post):
    s = x.shape[1]
    x2, mem2, tgt2 = x[0], mem[0], loss_target[0]
    me = 4 * lax.axis_index("x") + 2 * lax.axis_index("y") + lax.axis_index("c")

    w_up_loc = jnp.concatenate([w_up_a[0], w_up_b[0], w_up_m[0]], axis=0).astype(BF16)
    w_conv_loc = jnp.zeros((8, 128), F32).at[:3, :64].set(w_conv[0])
    w_int_g, w_conv_g = _all_gather([w_in[0].T.astype(BF16), w_conv_loc], "gather_w_in",
                                    splits=[[(0, 240), (240, 240), (480, 224), (704, 224)], [(0, 8)]])
    w_int = w_int_g.reshape(IN_WIDTH, D_MODEL)
    w_conv_f = w_conv_g[:, :3, :64].transpose(1, 0, 2).reshape(3, 512)
    late = _gather_start([w_mem_kv[0].astype(BF16) + w_conv_g[0, 7:8, 0:1].astype(BF16),
                          w_out[0].astype(BF16), w_up_loc], me, "gather_late_start")
    sink = attn_sink[0]
    tabs = _rope_tables(s)

    h, pa, pq, pkv, pbz, pmq, pmz, pg = _proj_fwd(x2, g_pre + late[4][0:1, 0:1], w_int, tabs)
    ya = _conv_fwd(pa, w_conv_f)
    yb = _attn_fwd(pq, pkv, pbz, sink)
    w_mkv_g, w_out_g, w_up_g = _gather_wait(*late[:4], yb, "gather_late_wait")
    w_mkv = w_mkv_g.reshape(D_MODEL, D_MODEL)
    w_out_f = w_out_g.reshape(D_MODEL, D_MODEL)
    mn, mkv = _mem_kv_fwd(mem2, g_mem, w_mkv)
    ym = _mem_attn_fwd(pmq, pmz, mkv)
    dg, dya, dyb, dym, dy, loss_p, gg_post, mb, dob, du = _mid(ya, yb, ym, pg, x2, tgt2, g_post, w_up_g, w_out_f)
    gw_out, gw_up = _gw_mid(mb, dob, (ya, yb, ym), du)

    core = lax.axis_index("c").astype(jnp.int32).reshape(1)
    chip = (2 * lax.axis_index("x") + lax.axis_index("y")).astype(jnp.int32).reshape(1)

    def exchange_start(shares, tag):
        from_sibling = _sibling_exchange(shares, "grads_to_sibling_" + tag)
        chip_shares = _pair_add(shares, from_sibling, core, "grads_pair_add_" + tag)
        return _chip_exchange_start(chip_shares, "grads_to_chips_start_" + tag)

    dmq, dmz, dmkv = _mem_attn_bwd(pmq, pmz, mkv, dym)
    gw_mkv, gg_mem = _mem_kv_bwd(mem2, g_mem, mn, dmkv, w_mkv)
    send1, recv1, srcs1, lands1, token1 = exchange_start(
        [gw_mkv.reshape(N_DEV, 128, D_MODEL), gw_out.reshape(N_DEV, 128, D_MODEL), gw_up], "small")
    da, gw_conv = _conv_bwd(pa, dya, w_conv_f + token1[0:1, 0:1])
    dq, dbz, dkv, g_sink = _attn_bwd(pq, pkv, pbz, dyb, sink, tabs)
    dparts = (da, dq, dkv, dbz, dmq, dmz, dg)
    gw_int = _gw_in(dparts, h)
    send2, recv2, srcs2, lands2, token2 = exchange_start([gw_int.reshape(N_DEV, SHARD_IN, D_MODEL)], "w_in")
    grad_x, gg_pre = _dh_bwd(dparts, x2, dy, g_pre + token2[0:1, 0:1], w_int)
    (o_mkv, o_out, o_up, o_int), (l_mkv, l_out, l_up, l_int) = _chip_exchange_wait(
        send1 + send2, recv1 + recv2, srcs1 + srcs2, lands1 + lands2, grad_x, "grads_to_chips_wait")

    row3 = jnp.concatenate([gw_conv[0:1], gw_conv[1:2]], axis=1)
    row4 = jnp.concatenate([gw_conv[2:3], g_sink[:, 0].reshape(1, 8), loss_p[0:1, 0:1],
                            jnp.zeros((1, 512 - 9), F32)], axis=1)
    pack = jnp.concatenate([gg_pre, gg_mem, gg_post, row3, row4, jnp.zeros((3, D_MODEL), F32)], axis=0)
    (packs,) = _all_gather([pack], "gather_small")
    tot = _pack_sum(packs)

    g_w_in, d_w_in, nm_w_in, nv_w_in = (t.T for t in _sum_adamw(
        o_int, l_int, chip, 0, w_in[0].T, m_w_in[0].T, v_w_in[0].T, "adamw_w_in", tiles=2))
    g_mkv, d_mkv, nm_mkv, nv_mkv = _sum_adamw(
        o_mkv, l_mkv, chip, 0, w_mem_kv[0], m_w_mem_kv[0], v_w_mem_kv[0], "adamw_w_mem_kv")
    g_out, d_out, nm_out, nv_out = _sum_adamw(o_out, l_out, chip, 0, w_out[0], m_w_out[0], v_w_out[0], "adamw_w_out")
    up = [_sum_adamw(o_up, l_up, chip, k, w[0], m[0], v[0], "adamw_w_up_" + "abm"[k])
          for k, (w, m, v) in enumerate([(w_up_a, m_w_up_a, v_w_up_a), (w_up_b, m_w_up_b, v_w_up_b),
                                         (w_up_m, m_w_up_m, v_w_up_m)])]

    g_g_pre, g_g_mem, g_g_post = tot[0:1], tot[1:2], tot[2:3]
    g_conv_full = jnp.concatenate([tot[3:4, 0:512], tot[3:4, 512:1024], tot[4:5, 0:512]], axis=0)
    g_conv = lax.dynamic_slice(g_conv_full, (0, 64 * me), (3, 64))
    g_sink_tot = tot[4:5, 512:520]
    loss = tot[4, 520]
    small_w = [g_pre, w_conv[0], attn_sink, g_mem, g_post]
    small_g = [g_g_pre, g_conv, g_sink_tot, g_g_mem, g_g_post]
    small_m = [m_g_pre, m_w_conv[0], m_attn_sink, m_g_mem, m_g_post]
    small_v = [v_g_pre, v_w_conv[0], v_attn_sink, v_g_mem, v_g_post]
    sd, sm, sv = _small_adamw(small_w, small_g, small_m, small_v)

    def lead(a):
        return a[None]

    grads = [g_g_pre, lead(g_w_in), lead(g_conv), g_sink_tot, g_g_mem, lead(g_mkv), lead(up[0][0]),
             lead(up[1][0]), lead(up[2][0]), lead(g_out), g_g_post]

    def assemble(small, big_in, big_mkv, big_up, big_out):
        return [small[0], lead(big_in), lead(small[1]), small[2], small[3], lead(big_mkv), lead(big_up[0]),
                lead(big_up[1]), lead(big_up[2]), lead(big_out), small[4]]

    deltas = assemble(sd, d_w_in, d_mkv, [u[1] for u in up], d_out)
    new_m = assemble(sm, nm_w_in, nm_mkv, [u[2] for u in up], nm_out)
    new_v = assemble(sv, nv_w_in, nv_mkv, [u[3] for u in up], nv_out)
    return (loss, grad_x[None], *grads, *deltas, *new_m, *new_v)
```

```python
import functools

import jax
import jax.numpy as jnp
from jax import lax
from jax.experimental import pallas as pl
from jax.experimental.pallas import tpu as pltpu

F32 = jnp.float32
BF16 = jnp.bfloat16
MESH = pl.DeviceIdType.MESH

N_DEV = 8
D_MODEL = 1024
EPS = 1e-6
ROPE_THETA = 500000.0
ROT_DIM = 16
HEAD_DIM = 64
ATTN_BLOCK = 128
MEM_HEADS = 4
MEM_HEAD_DIM = 128
ATTN_SCALE = HEAD_DIM ** -0.5
MEM_SCALE = MEM_HEAD_DIM ** -0.5

ADAM_LR = 0.001
ADAM_B1 = 0.9
ADAM_B2 = 0.999
ADAM_EPS = 1e-08
ADAM_WD = 0.01
ADAM_STEP = 10

SEG_A = (0, 2048)
SEG_BQ = (2048, 512)
SEG_BKV = (2560, 256)
SEG_BZ = (2816, 512)
SEG_MQ = (3328, 512)
SEG_MZ = (3840, 512)
SEG_G = (4352, 3072)
SEGS = (SEG_A, SEG_BQ, SEG_BKV, SEG_BZ, SEG_MQ, SEG_MZ, SEG_G)
IN_WIDTH = 7424
SHARD_IN = IN_WIDTH // N_DEV

V7X_VMEM_BYTES = 64 * 1024 * 1024
ANY = pl.BlockSpec(memory_space=pl.ANY)


def _params(vmem_mb):
    assert vmem_mb * 1024 * 1024 < V7X_VMEM_BYTES
    return pltpu.CompilerParams(dimension_semantics=("arbitrary",), vmem_limit_bytes=vmem_mb * 1024 * 1024)


def _full(shape):
    zeros = (0,) * len(shape)
    return pl.BlockSpec(shape, lambda i: zeros)


def _rows(tm, width):
    return pl.BlockSpec((tm, width), lambda i: (i, 0))


def _dot(a, b):
    return jnp.dot(a, b, preferred_element_type=F32)


def _dot_nt(a, b):
    return lax.dot_general(a, b, (((1,), (1,)), ((), ())), preferred_element_type=F32)


def _dot_tn(a, b):
    return lax.dot_general(a, b, (((0,), (0,)), ((), ())), preferred_element_type=F32)


def _sigmoid(z):
    return 1.0 / (1.0 + jnp.exp(-z))


def _rope(t, cs, s1, s2):
    return t * cs + pltpu.roll(t, 120, 1) * s1 + pltpu.roll(t, 8, 1) * s2


def _rope_t(d, cs, s1, s2):
    return d * cs + pltpu.roll(d * s1, 8, 1) + pltpu.roll(d * s2, 120, 1)


def _rope_tables(s):
    half = ROT_DIM // 2
    inv_freq = jnp.power(jnp.float32(ROPE_THETA), -jnp.arange(half, dtype=F32) * (2.0 / ROT_DIM))
    ang = jnp.arange(s).astype(F32)[:, None] * inv_freq[None, :]
    cos, sin = jnp.cos(ang), jnp.sin(ang)
    d = jnp.arange(128) % HEAD_DIM
    k = jnp.arange(half)[:, None]
    lo = (d[None, :] == k).astype(F32)
    hi = (d[None, :] == k + half).astype(F32)
    spread = functools.partial(jnp.dot, precision=lax.Precision.HIGHEST)
    return (spread(cos, lo + hi) + (d >= ROT_DIM).astype(F32)[None, :], -spread(sin, lo), spread(sin, hi))


def _load_once(pairs, sems):
    @pl.when(pl.program_id(0) == 0)
    def _():
        cps = [pltpu.make_async_copy(src, dst, sems.at[k]) for k, (src, dst) in enumerate(pairs)]
        for cp in cps:
            cp.start()
        for cp in cps:
            cp.wait()


def _my_place():
    x, y, c = lax.axis_index("x"), lax.axis_index("y"), lax.axis_index("c")
    return x, y, c


def _all_gather(arrs, name, splits=None):
    n = len(arrs)
    if splits is None:
        splits = [[(0, a.shape[0])] for a in arrs]
    pieces = [(a, r0, rn) for a in range(n) for r0, rn in splits[a]]
    n_p = len(pieces)

    def body(*refs):
        ins, outs = refs[:n], refs[n:2 * n]
        send_sems, recv_sems, local_sems = refs[2 * n:]
        x, y, c = _my_place()
        me, sibling = (x, y, c), (x, y, 1 - c)

        def route(core):
            first = (jnp.bitwise_xor(x, 1 - core), jnp.bitwise_xor(y, core), core)
            second = (jnp.bitwise_xor(x, core), jnp.bitwise_xor(y, 1 - core), core)
            return first, second, (1 - x, 1 - y, core)

        def idx(px, py, pc):
            return 4 * px + 2 * py + pc

        def copy(p, k, block, to, own=False):
            a, r0, rn = pieces[p]
            dst = outs[a].at[idx(*block), pl.ds(r0, rn)]
            return pltpu.make_async_remote_copy(
                src_ref=ins[a].at[pl.ds(r0, rn)] if own else dst, dst_ref=dst,
                send_sem=send_sems.at[p * 7 + k], recv_sem=recv_sems.at[p * 7 + k],
                device_id=to, device_id_type=MESH)

        nbr1, nbr2, diag = route(c)
        mine = [pltpu.make_async_copy(ins[a], outs[a].at[idx(*me)], local_sems.at[a]) for a in range(n)]
        for cp in mine:
            cp.start()
        sent = []
        for p in range(n_p):
            for k, to in enumerate((sibling, nbr1, nbr2)):
                sent.append(copy(p, k, me, to, own=True))
        for cp in sent:
            cp.start()
        for k_in, block, onward in ((1, nbr1, ((3, nbr2), (4, sibling))), (2, nbr2, ((5, sibling),)),
                                    (3, diag, ((6, sibling),))):
            for p in range(n_p):
                copy(p, k_in, block, me).wait_recv()
                for k_out, to in onward:
                    cp = copy(p, k_out, block, to)
                    cp.start()
                    sent.append(cp)
        s1, s2, sd = route(1 - c)
        for k_in, block in ((0, sibling), (4, s1), (5, s2), (6, sd)):
            for p in range(n_p):
                copy(p, k_in, block, me).wait_recv()
        for cp in sent:
            cp.wait_send()
        for cp in mine:
            cp.wait()

    return pl.pallas_call(
        body, name=name,
        out_shape=[jax.ShapeDtypeStruct((N_DEV,) + a.shape, a.dtype) for a in arrs],
        in_specs=[ANY] * n, out_specs=[ANY] * n,
        scratch_shapes=[pltpu.SemaphoreType.DMA((7 * n_p,)), pltpu.SemaphoreType.DMA((7 * n_p,)),
                        pltpu.SemaphoreType.DMA((n,))],
    )(*arrs)


N_CHIPS = 4


def _sibling_exchange(arrs, name):
    n = len(arrs)

    def body(*refs):
        ins, outs = refs[:n], refs[n:2 * n]
        send_sems, recv_sems = refs[2 * n:]
        x, y, c = _my_place()
        sibling = (x, y, 1 - c)

        def copy(a, j):
            return pltpu.make_async_remote_copy(
                src_ref=ins[a].at[2 * j + (1 - c)], dst_ref=outs[a].at[j],
                send_sem=send_sems.at[a * N_CHIPS + j], recv_sem=recv_sems.at[a * N_CHIPS + j],
                device_id=sibling, device_id_type=MESH)

        cps = [copy(a, j) for j in range(N_CHIPS) for a in range(n)]
        for cp in cps:
            cp.start()
        for cp in cps:
            cp.wait_recv()
        for cp in cps:
            cp.wait_send()

    return pl.pallas_call(
        body, name=name,
        out_shape=[jax.ShapeDtypeStruct((N_CHIPS,) + a.shape[1:], a.dtype) for a in arrs],
        in_specs=[ANY] * n, out_specs=[ANY] * n,
        scratch_shapes=[pltpu.SemaphoreType.DMA((N_CHIPS * n,)), pltpu.SemaphoreType.DMA((N_CHIPS * n,))],
    )(*arrs)


def _pair_add(mine, recv, core, name):
    n = len(mine)

    def body(c_ref, *refs):
        for a in range(n):
            refs[2 * n + a][...] = (refs[a][...].astype(F32) + refs[n + a][...].astype(F32)).astype(BF16)

    def blk(a):
        return (None,) + a.shape[1:]

    grid_spec = pltpu.PrefetchScalarGridSpec(
        num_scalar_prefetch=1, grid=(N_CHIPS,),
        in_specs=[pl.BlockSpec(blk(a), lambda j, c_ref: (2 * j + c_ref[0], 0, 0)) for a in mine]
        + [pl.BlockSpec(blk(a), lambda j, c_ref: (j, 0, 0)) for a in recv],
        out_specs=[pl.BlockSpec(blk(a), lambda j, c_ref: (j, 0, 0)) for a in recv])
    return pl.pallas_call(
        body, name=name, grid_spec=grid_spec,
        out_shape=[jax.ShapeDtypeStruct(a.shape, BF16) for a in recv],
        compiler_params=_params(32),
    )(core, *mine, *recv)


HBM = pl.BlockSpec(memory_space=pltpu.HBM)
SEM = pl.BlockSpec(memory_space=pltpu.SEMAPHORE)
N_PEER_CHIPS = 3


def _chip_copies(srcs, lands, send_sems, recv_sems):
    x, y, c = _my_place()
    my_chip = 2 * x + y
    peers = [(x, 1 - y), (1 - x, y), (1 - x, 1 - y)]
    cps = []
    for k, (px, py) in enumerate(peers):
        for a in range(len(srcs)):
            j = a * N_PEER_CHIPS + k
            cps.append(pltpu.make_async_remote_copy(
                src_ref=srcs[a].at[2 * px + py], dst_ref=lands[a].at[my_chip],
                send_sem=send_sems[j], recv_sem=recv_sems[j],
                device_id=(px, py, c), device_id_type=MESH))
    return cps


N_PEERS = N_DEV - 1


def _gather_copies(srcs, lands, send_sems, recv_sems):
    x, y, c = _my_place()
    me_idx = 4 * x + 2 * y + c
    flips = [(0, 0, 1), (0, 1, 0), (1, 0, 0), (0, 1, 1), (1, 0, 1), (1, 1, 0), (1, 1, 1)]
    cps = []
    for k, (fx, fy, fc) in enumerate(flips):
        peer = ((1 - x) if fx else x, (1 - y) if fy else y, (1 - c) if fc else c)
        for a in range(len(srcs)):
            j = a * N_PEERS + k
            cps.append(pltpu.make_async_remote_copy(
                src_ref=srcs[a], dst_ref=lands[a].at[me_idx], send_sem=send_sems[j], recv_sem=recv_sems[j],
                device_id=peer, device_id_type=MESH))
    return cps


def _split_start(copies, per_array, arrs, lands, name):
    arrs, lands = list(arrs), list(lands)
    n = len(arrs)
    k = n * per_array

    def body(*refs):
        srcs, land_refs = refs[:n], refs[n:2 * n]
        send_sems, recv_sems = refs[2 * n:2 * n + k], refs[2 * n + k:2 * n + 2 * k]
        token = refs[-1]
        for cp in copies(srcs, land_refs, send_sems, recv_sems):
            cp.start()
        token[...] = jnp.zeros_like(token)

    hbm_arrs = [pltpu.with_memory_space_constraint(a, pltpu.HBM) for a in arrs]
    lands = [pltpu.with_memory_space_constraint(a, pltpu.HBM) for a in lands]
    res = pl.pallas_call(
        body, name=name,
        out_shape=[pltpu.SemaphoreType.DMA(())] * (2 * k) + [pltpu.HBM(a.shape, a.dtype) for a in arrs + lands]
        + [jax.ShapeDtypeStruct((8, 128), F32)],
        in_specs=[HBM] * (2 * n),
        out_specs=[SEM] * (2 * k) + [HBM] * (2 * n) + [pl.BlockSpec(memory_space=pltpu.VMEM)],
        input_output_aliases={a: 2 * k + a for a in range(2 * n)},
        compiler_params=pltpu.CompilerParams(has_side_effects=pltpu.SideEffectType.DATAFLOW_SIDE_EFFECTING),
    )(*hbm_arrs, *lands)
    return res[:k], res[k:2 * k], res[2 * k:2 * k + n], res[2 * k + n:2 * k + 2 * n], res[-1]


def _split_wait(copies, per_array, send_sems, recv_sems, srcs, lands, after, name):
    n = len(srcs)
    k = n * per_array

    def body(*refs):
        src_refs, land_refs = refs[:n], refs[n:2 * n]
        s_sems, r_sems = refs[2 * n:2 * n + k], refs[2 * n + k:2 * n + 2 * k]
        for cp in copies(src_refs, land_refs, s_sems, r_sems):
            cp.wait_send()
            cp.wait_recv()

    res = pl.pallas_call(
        body, name=name,
        out_shape=[pltpu.HBM(a.shape, a.dtype) for a in list(srcs) + list(lands)],
        in_specs=[HBM] * (2 * n) + [SEM] * (2 * k) + [ANY],
        out_specs=[HBM] * (2 * n),
        input_output_aliases={a: a for a in range(2 * n)},
        compiler_params=pltpu.CompilerParams(has_side_effects=pltpu.SideEffectType.DATAFLOW_SIDE_EFFECTING),
    )(*srcs, *lands, *send_sems, *recv_sems, after)
    return res[:n], res[n:]


def _chip_exchange_start(arrs, name):
    return _split_start(_chip_copies, N_PEER_CHIPS, arrs, [lax.empty(a.shape, a.dtype) for a in arrs], name)


def _chip_exchange_wait(send_sems, recv_sems, srcs, lands, after, name):
    return _split_wait(_chip_copies, N_PEER_CHIPS, send_sems, recv_sems, srcs, lands, after, name)


def _gather_start(arrs, me_idx, name):
    lands = [lax.dynamic_update_slice(lax.empty((N_DEV,) + a.shape, a.dtype), a[None], (me_idx, 0, 0)) for a in arrs]
    return _split_start(_gather_copies, N_PEERS, arrs, lands, name)


def _gather_wait(send_sems, recv_sems, srcs, lands, after, name):
    return _split_wait(_gather_copies, N_PEERS, send_sems, recv_sems, srcs, lands, after, name)[1]


def _proj_fwd(x, g_pre, w_int, tabs):
    s = x.shape[0]
    tm = min(512, s)

    def body(x_ref, g_ref, cs_ref, s1_ref, s2_ref, w_hbm,
             h_ref, pa_ref, pq_ref, pkv_ref, pbz_ref, pmq_ref, pmz_ref, pg_ref, w_vm, sems):
        _load_once([(w_hbm, w_vm)], sems)
        xf = x_ref[...]
        r = lax.rsqrt(jnp.mean(xf * xf, axis=-1, keepdims=True) + EPS)
        h = ((xf * r) * g_ref[...]).astype(BF16)
        h_ref[...] = h
        cs, s1, s2 = cs_ref[...], s1_ref[...], s2_ref[...]

        def mm(seg, c0, width):
            return _dot_nt(h, w_vm[seg[0] + c0:seg[0] + c0 + width, :])

        for c0 in range(0, SEG_A[1], 512):
            pa_ref[:, c0:c0 + 512] = mm(SEG_A, c0, 512).astype(BF16)
        q = mm(SEG_BQ, 0, 512)
        for b in range(4):
            pq_ref[:, 128 * b:128 * b + 128] = _rope(q[:, 128 * b:128 * b + 128], cs, s1, s2).astype(BF16)
        kv = mm(SEG_BKV, 0, 256)
        pkv_ref[:, 0:128] = _rope(kv[:, 0:128], cs, s1, s2).astype(BF16)
        pkv_ref[:, 128:256] = kv[:, 128:256].astype(BF16)
        pbz_ref[...] = mm(SEG_BZ, 0, 512).astype(BF16)
        pmq_ref[...] = mm(SEG_MQ, 0, 512).astype(BF16)
        pmz_ref[...] = mm(SEG_MZ, 0, 512).astype(BF16)
        for c0 in range(0, SEG_G[1], 512):
            pg_ref[:, c0:c0 + 512] = mm(SEG_G, c0, 512).astype(BF16)

    widths = (D_MODEL, 2048, 512, 256, 512, 512, 512, 3072)
    return pl.pallas_call(
        body, name="proj_fwd", grid=(s // tm,),
        out_shape=[jax.ShapeDtypeStruct((s, w), BF16) for w in widths],
        in_specs=[_rows(tm, D_MODEL), _full((1, D_MODEL)), _rows(tm, 128), _rows(tm, 128), _rows(tm, 128), ANY],
        out_specs=[_rows(tm, w) for w in widths],
        scratch_shapes=[pltpu.VMEM((IN_WIDTH, D_MODEL), BF16), pltpu.SemaphoreType.DMA((1,))],
        compiler_params=_params(52),
    )(x, g_pre, *tabs, w_int)


def _mem_kv_fwd(mem, g_mem, w_mkv):
    m = mem.shape[0]

    def body(mem_ref, g_ref, w_ref, mn_ref, mkv_ref):
        xf = mem_ref[...]
        r = lax.rsqrt(jnp.mean(xf * xf, axis=-1, keepdims=True) + EPS)
        mn = ((xf * r) * g_ref[...]).astype(BF16)
        mn_ref[...] = mn
        mkv_ref[...] = _dot(mn, w_ref[...]).astype(BF16)

    return pl.pallas_call(
        body, name="mem_kv_fwd", grid=(1,),
        out_shape=[jax.ShapeDtypeStruct((m, D_MODEL), BF16)] * 2,
        in_specs=[_full((m, D_MODEL)), _full((1, D_MODEL)), _full((D_MODEL, D_MODEL))],
        out_specs=[_full((m, D_MODEL))] * 2,
        compiler_params=_params(32),
    )(mem, g_mem, w_mkv)


def _halo_specs(s, tm, rows, width):
    nblk = s // rows
    prev = pl.BlockSpec((rows, width), lambda i: (jnp.maximum(i * (tm // rows) - 1, 0), 0))
    nxt = pl.BlockSpec((rows, width), lambda i: (jnp.minimum((i + 1) * (tm // rows), nblk - 1), 0))
    return prev, nxt


def _conv_common(pa, prev_row, next_row, w, first, last, tm):
    b, c, u, z = (pa[:, 512 * k:512 * k + 512] for k in range(4))
    cu = c * u
    cu_prev = jnp.where(first, 0.0, prev_row[:, 512:1024] * prev_row[:, 1024:1536])
    cu_next = jnp.where(last, 0.0, next_row[:, 512:1024] * next_row[:, 1024:1536])
    row = lax.broadcasted_iota(jnp.int32, (tm, 512), 0)
    cu_m1 = jnp.where(row == 0, cu_prev, pltpu.roll(cu, 1, 0))
    cu_p1 = jnp.where(row == tm - 1, cu_next, pltpu.roll(cu, tm - 1, 0))
    y = cu_m1 * w[0:1] + cu * w[1:2] + cu_p1 * w[2:3]
    sig = _sigmoid(z)
    return b, c, u, z, cu, cu_m1, cu_p1, y, sig, row


def _conv_fwd(pa, w_conv):
    s = pa.shape[0]
    tm = min(512, s)
    nt = s // tm

    def body(pa_ref, pp_ref, pn_ref, w_ref, ya_ref):
        i = pl.program_id(0)
        prev_row = pp_ref[...].astype(F32)[15:16, :]
        next_row = pn_ref[...].astype(F32)[0:1, :]
        b, _, _, z, _, _, _, y, sig, _ = _conv_common(
            pa_ref[...].astype(F32), prev_row, next_row, w_ref[...], i == 0, i == nt - 1, tm)
        ya_ref[...] = (b * y * (z * sig)).astype(BF16)

    prev, nxt = _halo_specs(s, tm, 16, 2048)
    return pl.pallas_call(
        body, name="conv_fwd", grid=(nt,),
        out_shape=jax.ShapeDtypeStruct((s, 512), BF16),
        in_specs=[_rows(tm, 2048), prev, nxt, _full((3, 512))],
        out_specs=_rows(tm, 512),
        compiler_params=_params(48),
    )(pa, pa, pa, w_conv)


def _heads_to_lanes(a, g, row):
    low = row < HEAD_DIM
    parts = []
    for b in (2 * g, 2 * g + 1):
        t = jnp.transpose(a[:, 128 * b:128 * b + 128])
        swapped = pltpu.roll(t, HEAD_DIM, 0)
        if g == 0:
            parts += [jnp.where(low, t, 0.0), jnp.where(low, swapped, 0.0)]
        else:
            parts += [jnp.where(low, 0.0, swapped), jnp.where(low, 0.0, t)]
    return jnp.concatenate(parts, axis=1)


def _lanes_to_heads(t0, t1, row):
    low = row < HEAD_DIM
    blocks = []
    for b in range(4):
        g = b // 2
        tg = (t0, t1)[g]
        je = 2 * (b - 2 * g)
        even, odd = tg[:, 128 * je:128 * je + 128], tg[:, 128 * je + 128:128 * je + 256]
        if g == 0:
            t = jnp.where(low, even, pltpu.roll(odd, HEAD_DIM, 0))
        else:
            t = jnp.where(low, pltpu.roll(even, HEAD_DIM, 0), odd)
        blocks.append(jnp.transpose(t))
    return jnp.concatenate(blocks, axis=1)


WINDOW_KEYS = 3 * ATTN_BLOCK
STACKED = 4 * ATTN_BLOCK
KEY_CHUNK = 32


def _fill_band_bias(bias, nb):
    assert nb >= 2
    c = lax.broadcasted_iota(jnp.int32, (WINDOW_KEYS, STACKED), 0)
    r = lax.broadcasted_iota(jnp.int32, (WINDOW_KEYS, STACKED), 1) & (ATTN_BLOCK - 1)
    band = (c >= r) & (c <= r + 2 * ATTN_BLOCK)
    for v, ok in enumerate((band, band & (c >= ATTN_BLOCK), band & (c < 2 * ATTN_BLOCK))):
        bias[v] = jnp.where(ok, 0.0, -jnp.inf)


def _bias_variant(n, nb):
    return jnp.where(n == 0, 1, jnp.where(n == nb - 1, 2, 0))


def _sink_row(sink_ref, g):
    return jnp.concatenate([jnp.full((1, ATTN_BLOCK), sink_ref[4 * g + j], F32) for j in range(4)], axis=1)


def _softmax_keys_major(sc, bias, variant, sink, e_scr):
    chunks = [pl.ds(k * KEY_CHUNK, KEY_CHUNK) for k in range(WINDOW_KEYS // KEY_CHUNK)]
    rows = [slice(k * KEY_CHUNK, (k + 1) * KEY_CHUNK) for k in range(WINDOW_KEYS // KEY_CHUNK)]
    m_run = jnp.full((KEY_CHUNK, STACKED), -jnp.inf, F32)
    for ck, rw in zip(chunks, rows):
        m_run = jnp.maximum(m_run, sc[rw] * ATTN_SCALE + bias[variant, ck, :])
    m = jnp.maximum(jnp.max(m_run, axis=0, keepdims=True), sink)
    l_run = jnp.zeros((KEY_CHUNK, STACKED), F32)
    for ck, rw in zip(chunks, rows):
        e = jnp.exp(sc[rw] * ATTN_SCALE + bias[variant, ck, :] - m)
        l_run += e
        e_scr[rw, :] = e.astype(BF16)
    es = jnp.exp(sink - m)
    inv = 1.0 / (jnp.sum(l_run, axis=0, keepdims=True) + es)
    return inv, es * inv


def _fill_padded(kv_ref, kpad, vpad, s):
    zero = jnp.zeros((ATTN_BLOCK, 128), BF16)
    kpad[0:ATTN_BLOCK, :] = zero
    vpad[0:ATTN_BLOCK, :] = zero
    kpad[ATTN_BLOCK + s:2 * ATTN_BLOCK + s, :] = zero
    vpad[ATTN_BLOCK + s:2 * ATTN_BLOCK + s, :] = zero
    kpad[ATTN_BLOCK:ATTN_BLOCK + s, :] = kv_ref[:, 0:128]
    vpad[ATTN_BLOCK:ATTN_BLOCK + s, :] = kv_ref[:, 128:256]


def _attn_fwd(pq, pkv, pbz, sink):
    s = pq.shape[0]
    nb = s // ATTN_BLOCK

    def body(sink_ref, q_ref, z_ref, kv_ref, yb_ref, kpad, vpad, bias, e_scr):
        n = pl.program_id(0)

        @pl.when(n == 0)
        def _():
            _fill_padded(kv_ref, kpad, vpad, s)
            _fill_band_bias(bias, nb)

        row = lax.broadcasted_iota(jnp.int32, (ATTN_BLOCK, 128), 0)
        start = pl.multiple_of(n * ATTN_BLOCK, ATTN_BLOCK)
        kw, vw = kpad[pl.ds(start, WINDOW_KEYS), :], vpad[pl.ds(start, WINDOW_KEYS), :]
        qf = q_ref[...].astype(F32)
        variant = _bias_variant(n, nb)
        outs = []
        for g in range(2):
            qt = _heads_to_lanes(qf, g, row).astype(BF16)
            inv, _ = _softmax_keys_major(_dot(kw, qt), bias, variant, _sink_row(sink_ref, g), e_scr)
            outs.append(_dot_tn(vw, e_scr[...]) * inv)
        attn = _lanes_to_heads(outs[0], outs[1], row)
        z = z_ref[...].astype(F32)
        yb_ref[...] = (attn * (z * _sigmoid(z))).astype(BF16)

    return pl.pallas_call(
        body, name="attn_fwd", grid=(nb,),
        out_shape=jax.ShapeDtypeStruct((s, 512), BF16),
        in_specs=[pl.BlockSpec(memory_space=pltpu.SMEM), _rows(ATTN_BLOCK, 512), _rows(ATTN_BLOCK, 512),
                  _full((s, 256))],
        out_specs=_rows(ATTN_BLOCK, 512),
        scratch_shapes=[pltpu.VMEM((s + 2 * ATTN_BLOCK, 128), BF16)] * 2
        + [pltpu.VMEM((3, WINDOW_KEYS, STACKED), F32), pltpu.VMEM((WINDOW_KEYS, STACKED), BF16)],
        compiler_params=_params(32),
    )(sink, pq, pbz, pkv)


def _mem_softmax_t(q, mk):
    sc = _dot_nt(mk, q) * MEM_SCALE
    e = jnp.exp(sc - jnp.max(sc, axis=0, keepdims=True))
    return e * (1.0 / jnp.sum(e, axis=0, keepdims=True))


def _mem_attn_fwd(pmq, pmz, mkv):
    s = pmq.shape[0]
    m = mkv.shape[0]
    tm = min(512, s)

    def body(q_ref, z_ref, mk_ref, mv_ref, ym_ref):
        z = z_ref[...].astype(F32)
        sz = z * _sigmoid(z)
        for h in range(MEM_HEADS):
            cols = slice(128 * h, 128 * h + 128)
            pt = _mem_softmax_t(q_ref[:, cols], mk_ref[:, cols])
            o = _dot_tn(pt.astype(BF16), mv_ref[:, cols])
            ym_ref[:, cols] = (o * sz[:, cols]).astype(BF16)

    return pl.pallas_call(
        body, name="mem_attn_fwd", grid=(s // tm,),
        out_shape=jax.ShapeDtypeStruct((s, 512), BF16),
        in_specs=[_rows(tm, 512), _rows(tm, 512), pl.BlockSpec((m, 512), lambda i: (0, 0)),
                  pl.BlockSpec((m, 512), lambda i: (0, 1))],
        out_specs=_rows(tm, 512),
        compiler_params=_params(32),
    )(pmq, pmz, mkv, mkv)


def _mid(ya, yb, ym, pg, x, target, g_post, w_up, w_out):
    s = x.shape[0]
    tm = min(256, s)
    nt = s // tm

    def body(ya_ref, yb_ref, ym_ref, pg_ref, x_ref, t_ref, gp_ref, wup_hbm, wout_hbm,
             dg_ref, dya_ref, dyb_ref, dym_ref, dy_ref, loss_ref, ggp_ref, mb_ref, dob_ref, du_ref,
             wup_vm, wout_vm, sems):
        i = pl.program_id(0)
        _load_once([(wup_hbm.at[d], wup_vm.at[:, pl.ds(128 * d, 128)]) for d in range(N_DEV)]
                   + [(wout_hbm, wout_vm)], sems)

        @pl.when(i == 0)
        def _():
            loss_ref[...] = jnp.zeros_like(loss_ref)
            ggp_ref[...] = jnp.zeros_like(ggp_ref)

        ys = (ya_ref[...], yb_ref[...], ym_ref[...])
        us = [_dot(ys[k], wup_vm[512 * k:512 * k + 512, :]) for k in range(3)]
        gates = [_sigmoid(pg_ref[:, 1024 * k:1024 * k + 1024].astype(F32)) for k in range(3)]
        merged = gates[0] * us[0] + gates[1] * us[1] + gates[2] * us[2]
        mb = merged.astype(BF16)
        mb_ref[...] = mb
        out = _dot(mb, wout_vm[...])
        r = lax.rsqrt(jnp.mean(out * out, axis=-1, keepdims=True) + EPS)
        on = out * r
        gp = gp_ref[...]
        err = (x_ref[...] + on * gp) - t_ref[...]
        loss_ref[...] += 0.5 * jnp.sum(err * err) * (1.0 / D_MODEL)
        dy = err * (1.0 / D_MODEL)
        dy_ref[...] = dy
        ggp_ref[...] += jnp.sum(dy * on, axis=0, keepdims=True)
        a = dy * gp
        d_out = r * (a - on * jnp.mean(a * on, axis=-1, keepdims=True))
        dob = d_out.astype(BF16)
        dob_ref[...] = dob
        d_merged = _dot_nt(dob, wout_vm[...])
        d_refs = (dya_ref, dyb_ref, dym_ref)
        for k in range(3):
            g = gates[k]
            dg_ref[:, 1024 * k:1024 * k + 1024] = (d_merged * us[k] * g * (1.0 - g)).astype(BF16)
            du = (d_merged * g).astype(BF16)
            du_ref[k] = du
            d_refs[k][...] = _dot_nt(du, wup_vm[512 * k:512 * k + 512, :])

    return pl.pallas_call(
        body, name="mid", grid=(nt,),
        out_shape=[jax.ShapeDtypeStruct((s, 3072), BF16)] + [jax.ShapeDtypeStruct((s, 512), F32)] * 3
        + [jax.ShapeDtypeStruct((s, D_MODEL), F32), jax.ShapeDtypeStruct((8, 128), F32),
           jax.ShapeDtypeStruct((1, D_MODEL), F32), jax.ShapeDtypeStruct((s, D_MODEL), BF16),
           jax.ShapeDtypeStruct((s, D_MODEL), BF16), jax.ShapeDtypeStruct((3, s, D_MODEL), BF16)],
        in_specs=[_rows(tm, 512)] * 3 + [_rows(tm, 3072), _rows(tm, D_MODEL), _rows(tm, D_MODEL),
                                         _full((1, D_MODEL)), ANY, ANY],
        out_specs=[_rows(tm, 3072)] + [_rows(tm, 512)] * 3
        + [_rows(tm, D_MODEL), _full((8, 128)), _full((1, D_MODEL)), _rows(tm, D_MODEL), _rows(tm, D_MODEL),
           pl.BlockSpec((3, tm, D_MODEL), lambda i: (0, i, 0))],
        scratch_shapes=[pltpu.VMEM((1536, D_MODEL), BF16), pltpu.VMEM((D_MODEL, D_MODEL), BF16),
                        pltpu.SemaphoreType.DMA((N_DEV + 1,))],
        compiler_params=_params(56),
    )(ya, yb, ym, pg, x, target, g_post, w_up, w_out)


def _gw_mid(mb, dob, ys, du):
    s = mb.shape[0]
    tn = 256

    def out_body(mb_ref, dob_ref, o_ref):
        o_ref[...] = _dot_tn(mb_ref[...], dob_ref[...]).astype(BF16)

    gw_out = pl.pallas_call(
        out_body, name="gw_out", grid=(D_MODEL // tn,),
        out_shape=jax.ShapeDtypeStruct((D_MODEL, D_MODEL), BF16),
        in_specs=[pl.BlockSpec((s, tn), lambda j: (0, j)), _full((s, D_MODEL))],
        out_specs=pl.BlockSpec((tn, D_MODEL), lambda j: (j, 0)),
        compiler_params=_params(48),
    )(mb, dob)

    per = 512 // tn

    def up_body(ya_ref, yb_ref, ym_ref, du_ref, o_ref):
        j = pl.program_id(0)
        for k, y_ref in enumerate((ya_ref, yb_ref, ym_ref)):
            @pl.when(j // per == k)
            def _(y_ref=y_ref):
                res = _dot_tn(y_ref[...], du_ref[...])
                for d in range(N_DEV):
                    o_ref[d] = res[:, 128 * d:128 * d + 128].astype(BF16)

    def y_spec(k):
        return pl.BlockSpec((s, tn), lambda j: (0, jnp.clip(j - per * k, 0, per - 1)))

    gw_up = pl.pallas_call(
        up_body, name="gw_up", grid=(3 * per,),
        out_shape=jax.ShapeDtypeStruct((N_DEV, 1536, 128), BF16),
        in_specs=[y_spec(0), y_spec(1), y_spec(2), pl.BlockSpec((None, s, D_MODEL), lambda j: (j // per, 0, 0))],
        out_specs=pl.BlockSpec((N_DEV, tn, 128), lambda j: (0, j, 0)),
        compiler_params=_params(48),
    )(*ys, du)
    return gw_out, gw_up


def _conv_bwd(pa, dya, w_conv):
    s = pa.shape[0]
    tm = min(512, s)
    nt = s // tm

    def body(pa_ref, pp_ref, pn_ref, d_ref, dp_ref, dn_ref, w_ref, da_ref, gw_ref):
        i = pl.program_id(0)
        first, last = i == 0, i == nt - 1

        @pl.when(first)
        def _():
            gw_ref[...] = jnp.zeros_like(gw_ref)

        w = w_ref[...]
        prev_row = pp_ref[...].astype(F32)[15:16, :]
        next_row = pn_ref[...].astype(F32)[0:1, :]
        b, c, u, z, cu, cu_m1, cu_p1, y, sig, row = _conv_common(
            pa_ref[...].astype(F32), prev_row, next_row, w, first, last, tm)
        sz = z * sig
        dya_t = d_ref[...]
        d_y = dya_t * b * sz

        def halo_dy(p_row, d_row):
            zz = p_row[:, 1536:2048]
            return d_row * p_row[:, 0:512] * (zz * _sigmoid(zz))

        dy_prev = jnp.where(first, 0.0, halo_dy(prev_row, dp_ref[7:8, :]))
        dy_next = jnp.where(last, 0.0, halo_dy(next_row, dn_ref[0:1, :]))
        dy_m1 = jnp.where(row == 0, dy_prev, pltpu.roll(d_y, 1, 0))
        dy_p1 = jnp.where(row == tm - 1, dy_next, pltpu.roll(d_y, tm - 1, 0))
        d_cu = dy_p1 * w[0:1] + d_y * w[1:2] + dy_m1 * w[2:3]
        da_ref[:, 0:512] = (dya_t * y * sz).astype(BF16)
        da_ref[:, 512:1024] = (d_cu * u).astype(BF16)
        da_ref[:, 1024:1536] = (d_cu * c).astype(BF16)
        da_ref[:, 1536:2048] = (dya_t * b * y * (sig * (1.0 + z * (1.0 - sig)))).astype(BF16)
        gw_ref[0:1, :] += jnp.sum(d_y * cu_m1, axis=0, keepdims=True)
        gw_ref[1:2, :] += jnp.sum(d_y * cu, axis=0, keepdims=True)
        gw_ref[2:3, :] += jnp.sum(d_y * cu_p1, axis=0, keepdims=True)

    prev, nxt = _halo_specs(s, tm, 16, 2048)
    dprev, dnxt = _halo_specs(s, tm, 8, 512)
    return pl.pallas_call(
        body, name="conv_bwd", grid=(nt,),
        out_shape=[jax.ShapeDtypeStruct((s, 2048), BF16), jax.ShapeDtypeStruct((8, 512), F32)],
        in_specs=[_rows(tm, 2048), prev, nxt, _rows(tm, 512), dprev, dnxt, _full((3, 512))],
        out_specs=[_rows(tm, 2048), _full((8, 512))],
        compiler_params=_params(48),
    )(pa, pa, pa, dya, dya, dya, w_conv)


def _attn_bwd(pq, pkv, pbz, dyb, sink, tabs):
    s = pq.shape[0]
    nb = s // ATTN_BLOCK

    def body(sink_ref, q_ref, z_ref, d_ref, cs_ref, s1_ref, s2_ref, kv_ref, csf_ref, s1f_ref, s2f_ref,
             dq_ref, dz_ref, dkv_ref, gs_ref, kpad, vpad, dk_acc, dv_acc, bias, e_scr, ds_scr):
        n = pl.program_id(0)

        @pl.when(n == 0)
        def _():
            _fill_padded(kv_ref, kpad, vpad, s)
            _fill_band_bias(bias, nb)
            dk_acc[...] = jnp.zeros_like(dk_acc)
            dv_acc[...] = jnp.zeros_like(dv_acc)
            gs_ref[...] = jnp.zeros_like(gs_ref)

        row = lax.broadcasted_iota(jnp.int32, (ATTN_BLOCK, 128), 0)
        start = pl.multiple_of(n * ATTN_BLOCK, ATTN_BLOCK)
        kw, vw = kpad[pl.ds(start, WINDOW_KEYS), :], vpad[pl.ds(start, WINDOW_KEYS), :]
        qf = q_ref[...].astype(F32)
        variant = _bias_variant(n, nb)
        z = z_ref[...].astype(F32)
        sig = _sigmoid(z)
        dyb_t = d_ref[...]
        d_attn = dyb_t * (z * sig)
        outs, dqs = [], []
        dk_w = jnp.zeros((WINDOW_KEYS, 128), F32)
        dv_w = jnp.zeros((WINDOW_KEYS, 128), F32)
        for g in range(2):
            qt = _heads_to_lanes(qf, g, row)
            inv, p_sink = _softmax_keys_major(
                _dot(kw, qt.astype(BF16)), bias, variant, _sink_row(sink_ref, g), e_scr)
            ot = _dot_tn(vw, e_scr[...]) * inv
            outs.append(ot)
            dot_ = _heads_to_lanes(d_attn, g, row)
            delta = jnp.sum(dot_ * ot, axis=0, keepdims=True)
            dpt = _dot(vw, dot_.astype(BF16))
            for k in range(WINDOW_KEYS // KEY_CHUNK):
                rw = slice(k * KEY_CHUNK, (k + 1) * KEY_CHUNK)
                ds_scr[rw, :] = (e_scr[rw, :].astype(F32) * (dpt[rw] - delta)).astype(BF16)
            sink_part = p_sink * delta
            for j in range(4):
                h = 4 * g + j
                gs_ref[h:h + 1, :] -= jnp.sum(sink_part[:, 128 * j:128 * j + 128])
            dqs.append(_dot_tn(kw, ds_scr[...]) * (inv * ATTN_SCALE))
            dk_w += _dot_nt(ds_scr[...], (qt * inv).astype(BF16)) * ATTN_SCALE
            dv_w += _dot_nt(e_scr[...], (dot_ * inv).astype(BF16))
        dk_acc[pl.ds(start, WINDOW_KEYS), :] += dk_w
        dv_acc[pl.ds(start, WINDOW_KEYS), :] += dv_w
        attn = _lanes_to_heads(outs[0], outs[1], row)
        dz_ref[...] = (dyb_t * attn * (sig * (1.0 + z * (1.0 - sig)))).astype(BF16)
        dq = _lanes_to_heads(dqs[0], dqs[1], row)
        cs, s1, s2 = cs_ref[...], s1_ref[...], s2_ref[...]
        for b in range(4):
            dq_ref[:, 128 * b:128 * b + 128] = _rope_t(dq[:, 128 * b:128 * b + 128], cs, s1, s2).astype(BF16)

        @pl.when(n == nb - 1)
        def _():
            dk = dk_acc[ATTN_BLOCK:ATTN_BLOCK + s, :]
            dkv_ref[:, 0:128] = _rope_t(dk, csf_ref[...], s1f_ref[...], s2f_ref[...]).astype(BF16)
            dkv_ref[:, 128:256] = dv_acc[ATTN_BLOCK:ATTN_BLOCK + s, :].astype(BF16)

    tile = _rows(ATTN_BLOCK, 512)
    tab = _rows(ATTN_BLOCK, 128)
    return pl.pallas_call(
        body, name="attn_bwd", grid=(nb,),
        out_shape=[jax.ShapeDtypeStruct((s, 512), BF16), jax.ShapeDtypeStruct((s, 512), BF16),
                   jax.ShapeDtypeStruct((s, 256), BF16), jax.ShapeDtypeStruct((8, 128), F32)],
        in_specs=[pl.BlockSpec(memory_space=pltpu.SMEM), tile, tile, tile, tab, tab, tab,
                  _full((s, 256)), _full((s, 128)), _full((s, 128)), _full((s, 128))],
        out_specs=[tile, tile, _full((s, 256)), _full((8, 128))],
        scratch_shapes=[pltpu.VMEM((s + 2 * ATTN_BLOCK, 128), BF16)] * 2
        + [pltpu.VMEM((s + 2 * ATTN_BLOCK, 128), F32)] * 2
        + [pltpu.VMEM((3, WINDOW_KEYS, STACKED), F32)] + [pltpu.VMEM((WINDOW_KEYS, STACKED), BF16)] * 2,
        compiler_params=_params(48),
    )(sink, pq, pbz, dyb, *tabs, pkv, *tabs)


def _mem_attn_bwd(pmq, pmz, mkv, dym):
    s = pmq.shape[0]
    m = mkv.shape[0]
    tm = min(512, s)

    def body(q_ref, z_ref, d_ref, mk_ref, mv_ref, dq_ref, dz_ref, dmkv_ref):
        @pl.when(pl.program_id(0) == 0)
        def _():
            dmkv_ref[...] = jnp.zeros_like(dmkv_ref)

        z = z_ref[...].astype(F32)
        sig = _sigmoid(z)
        dym_t = d_ref[...]
        d_attn = dym_t * (z * sig)
        dsilu = sig * (1.0 + z * (1.0 - sig))
        for h in range(MEM_HEADS):
            cols = slice(128 * h, 128 * h + 128)
            q, mk, mv = q_ref[:, cols], mk_ref[:, cols], mv_ref[:, cols]
            pt = _mem_softmax_t(q, mk)
            pb = pt.astype(BF16)
            o = _dot_tn(pb, mv)
            dob = d_attn[:, cols].astype(BF16)
            dpt = _dot_nt(mv, dob)
            dst = (pt * (dpt - jnp.sum(pt * dpt, axis=0, keepdims=True))).astype(BF16)
            dq_ref[:, cols] = (_dot_tn(dst, mk) * MEM_SCALE).astype(BF16)
            dz_ref[:, cols] = (dym_t[:, cols] * o * dsilu[:, cols]).astype(BF16)
            dmkv_ref[:, cols] += _dot(dst, q) * MEM_SCALE
            dmkv_ref[:, 512 + 128 * h:512 + 128 * h + 128] += _dot(pb, dob)

    return pl.pallas_call(
        body, name="mem_attn_bwd", grid=(s // tm,),
        out_shape=[jax.ShapeDtypeStruct((s, 512), BF16), jax.ShapeDtypeStruct((s, 512), BF16),
                   jax.ShapeDtypeStruct((m, D_MODEL), F32)],
        in_specs=[_rows(tm, 512), _rows(tm, 512), _rows(tm, 512), pl.BlockSpec((m, 512), lambda i: (0, 0)),
                  pl.BlockSpec((m, 512), lambda i: (0, 1))],
        out_specs=[_rows(tm, 512), _rows(tm, 512), _full((m, D_MODEL))],
        compiler_params=_params(32),
    )(pmq, pmz, dym, mkv, mkv)


def _mem_kv_bwd(mem, g_mem, mn, dmkv, w_mkv):
    m = mem.shape[0]

    def body(mem_ref, g_ref, mn_ref, d_ref, w_ref, gw_ref, gg_ref):
        db = d_ref[...].astype(BF16)
        gw_ref[...] = _dot_tn(mn_ref[...], db).astype(BF16)
        d_mn = _dot_nt(db, w_ref[...])
        xf = mem_ref[...]
        r = lax.rsqrt(jnp.mean(xf * xf, axis=-1, keepdims=True) + EPS)
        gg_ref[...] = jnp.sum(d_mn * (xf * r), axis=0, keepdims=True)

    return pl.pallas_call(
        body, name="mem_kv_bwd", grid=(1,),
        out_shape=[jax.ShapeDtypeStruct((D_MODEL, D_MODEL), BF16), jax.ShapeDtypeStruct((1, D_MODEL), F32)],
        in_specs=[_full((m, D_MODEL)), _full((1, D_MODEL)), _full((m, D_MODEL)), _full((m, D_MODEL)),
                  _full((D_MODEL, D_MODEL))],
        out_specs=[_full((D_MODEL, D_MODEL)), _full((1, D_MODEL))],
        compiler_params=_params(32),
    )(mem, g_mem, mn, dmkv, w_mkv)


def _dh_bwd(dparts, x, dy, g_pre, w_int):
    s = x.shape[0]
    tm = min(256, s)

    def body(*refs):
        d_refs = refs[:7]
        x_ref, dy_ref, g_ref, w_hbm, gx_ref, gg_ref, w_vm, sems = refs[7:]
        _load_once([(w_hbm, w_vm)], sems)

        @pl.when(pl.program_id(0) == 0)
        def _():
            gg_ref[...] = jnp.zeros_like(gg_ref)

        d_h = jnp.zeros((tm, D_MODEL), F32)
        for d_ref, (r0, width) in zip(d_refs, SEGS):
            for c0 in range(0, width, 512):
                cw = min(512, width - c0)
                d_h += _dot(d_ref[:, c0:c0 + cw], w_vm[r0 + c0:r0 + c0 + cw, :])
        xf = x_ref[...]
        r = lax.rsqrt(jnp.mean(xf * xf, axis=-1, keepdims=True) + EPS)
        xn = xf * r
        a = d_h * g_ref[...]
        gx_ref[...] = r * (a - xn * jnp.mean(a * xn, axis=-1, keepdims=True)) + dy_ref[...]
        gg_ref[...] += jnp.sum(d_h * xn, axis=0, keepdims=True)

    return pl.pallas_call(
        body, name="dh_bwd", grid=(s // tm,),
        out_shape=[jax.ShapeDtypeStruct((s, D_MODEL), F32), jax.ShapeDtypeStruct((1, D_MODEL), F32)],
        in_specs=[_rows(tm, w) for _, w in SEGS] + [_rows(tm, D_MODEL), _rows(tm, D_MODEL), _full((1, D_MODEL)), ANY],
        out_specs=[_rows(tm, D_MODEL), _full((1, D_MODEL))],
        scratch_shapes=[pltpu.VMEM((IN_WIDTH, D_MODEL), BF16), pltpu.SemaphoreType.DMA((1,))],
        compiler_params=_params(52),
    )(*dparts, x, dy, g_pre, w_int)


def _gw_in(dparts, h):
    s = h.shape[0]
    tn = 256
    starts, counts = [], []
    for r0, width in SEGS:
        starts.append(r0 // tn)
        counts.append(width // tn)

    def body(*refs):
        d_refs = refs[:7]
        h_hbm, o_ref, h_vm, sems = refs[7:]
        _load_once([(h_hbm, h_vm)], sems)
        j = pl.program_id(0)
        for d_ref, st, cnt in zip(d_refs, starts, counts):
            @pl.when((j >= st) & (j < st + cnt))
            def _(d_ref=d_ref):
                o_ref[...] = _dot_tn(d_ref[...], h_vm[...]).astype(BF16)

    def seg_spec(st, cnt):
        return pl.BlockSpec((s, tn), lambda j: (0, jnp.clip(j - st, 0, cnt - 1)))

    return pl.pallas_call(
        body, name="gw_in", grid=(IN_WIDTH // tn,),
        out_shape=jax.ShapeDtypeStruct((IN_WIDTH, D_MODEL), BF16),
        in_specs=[seg_spec(st, cnt) for st, cnt in zip(starts, counts)] + [ANY],
        out_specs=pl.BlockSpec((tn, D_MODEL), lambda j: (j, 0)),
        scratch_shapes=[pltpu.VMEM((s, D_MODEL), BF16), pltpu.SemaphoreType.DMA((1,))],
        compiler_params=_params(52),
    )(*dparts, h)


def _adamw_math(w, g, m, v):
    m2 = ADAM_B1 * m + (1.0 - ADAM_B1) * g
    v2 = ADAM_B2 * v + (1.0 - ADAM_B2) * (g * g)
    m_hat = m2 / (1.0 - ADAM_B1 ** ADAM_STEP)
    v_hat = v2 / (1.0 - ADAM_B2 ** ADAM_STEP)
    delta = -ADAM_LR * (m_hat / (jnp.sqrt(v_hat) + ADAM_EPS) + ADAM_WD * w)
    return delta, m2, v2


def _sum_adamw(own, land, chip, block, w, m, v, name, tiles=1):
    r, c = w.shape
    rt = r // tiles

    def body(c_ref, own_ref, l1_ref, l2_ref, l3_ref, w_ref, m_ref, v_ref, g_ref, d_ref, m2_ref, v2_ref):
        g = own_ref[...].astype(F32)
        for l_ref in (l1_ref, l2_ref, l3_ref):
            g += l_ref[...].astype(F32)
        g_ref[...] = g
        d_ref[...], m2_ref[...], v2_ref[...] = _adamw_math(w_ref[...], g, m_ref[...], v_ref[...])

    def share(k):
        return pl.BlockSpec((None, rt, c), lambda i, c_ref: (jnp.bitwise_xor(c_ref[0], k), block * tiles + i, 0))

    spec = pl.BlockSpec((rt, c), lambda i, c_ref: (i, 0))
    grid_spec = pltpu.PrefetchScalarGridSpec(
        num_scalar_prefetch=1, grid=(tiles,),
        in_specs=[share(0), share(1), share(2), share(3)] + [spec] * 3, out_specs=[spec] * 4)
    return pl.pallas_call(
        body, name=name, grid_spec=grid_spec,
        out_shape=[jax.ShapeDtypeStruct((r, c), F32)] * 4,
        compiler_params=_params(48),
    )(chip, own, land, land, land, w, m, v)


def _sum_adamw_group(items, chip, name):
    k = len(items)

    def body(c_ref, *refs):
        shares, wmv, outs = refs[:4 * k], refs[4 * k:7 * k], refs[7 * k:]
        for j in range(k):
            g = shares[4 * j][...].astype(F32)
            for l_ref in shares[4 * j + 1:4 * j + 4]:
                g += l_ref[...].astype(F32)
            outs[4 * j][...] = g
            outs[4 * j + 1][...], outs[4 * j + 2][...], outs[4 * j + 3][...] = _adamw_math(
                wmv[3 * j][...], g, wmv[3 * j + 1][...], wmv[3 * j + 2][...])

    def share(shape, block, q):
        return pl.BlockSpec((None,) + shape, lambda i, c_ref: (jnp.bitwise_xor(c_ref[0], q), block, 0))

    in_specs, args = [], []
    for own, land, block, w, m, v in items:
        in_specs += [share(w.shape, block, q) for q in range(4)]
        args += [own, land, land, land]
    for own, land, block, w, m, v in items:
        in_specs += [pl.BlockSpec(w.shape, lambda i, c_ref: (0, 0))] * 3
        args += [w, m, v]
    out_specs = [pl.BlockSpec(w.shape, lambda i, c_ref: (0, 0)) for _, _, _, w, _, _ in items for _ in range(4)]
    res = pl.pallas_call(
        body, name=name,
        grid_spec=pltpu.PrefetchScalarGridSpec(num_scalar_prefetch=1, grid=(1,), in_specs=in_specs,
                                               out_specs=out_specs),
        out_shape=[jax.ShapeDtypeStruct(w.shape, F32) for _, _, _, w, _, _ in items for _ in range(4)],
        compiler_params=_params(48),
    )(chip, *args)
    return [res[4 * j:4 * j + 4] for j in range(k)]


def _small_step(parts, ws, ms, vs):
    def body(gpre_ref, gconv_ref, gsink_ref, gmem_ref, gpost_ref, loss_ref, *refs):
        w_refs, m_refs, v_refs = refs[0:5], refs[5:10], refs[10:15]
        loss_out = refs[15]
        g_outs, d_outs, m_outs, v_outs = refs[16:21], refs[21:26], refs[26:31], refs[31:36]
        pack, gathered, send_sems, recv_sems = refs[36:]
        x, y, c = _my_place()
        me_idx = 4 * x + 2 * y + c

        lane = lax.broadcasted_iota(jnp.int32, (1, 128), 1)
        sink_row = jnp.zeros((1, 128), F32)
        for h in range(8):
            sink_row = jnp.where(lane == h, gsink_ref[h:h + 1, :], sink_row)
        pack[...] = jnp.zeros_like(pack)
        pack[0:1, :] = gpre_ref[...]
        pack[1:2, :] = gmem_ref[...]
        pack[2:3, :] = gpost_ref[...]
        pack[3:6, 0:512] = gconv_ref[0:3, :]
        pack[6:7, 0:128] = sink_row
        pack[7:8, 0:128] = loss_ref[0:1, :]

        flips = [(0, 0, 1), (0, 1, 0), (1, 0, 0), (0, 1, 1), (1, 0, 1), (1, 1, 0), (1, 1, 1)]
        cps = []
        for k, (fx, fy, fc) in enumerate(flips):
            peer = ((1 - x) if fx else x, (1 - y) if fy else y, (1 - c) if fc else c)
            cps.append(pltpu.make_async_remote_copy(
                src_ref=pack, dst_ref=gathered.at[me_idx], send_sem=send_sems.at[k], recv_sem=recv_sems.at[k],
                device_id=peer, device_id_type=MESH))
        for cp in cps:
            cp.start()
        gathered[me_idx] = pack[...]
        for cp in cps:
            cp.wait_recv()
        for cp in cps:
            cp.wait_send()
        tot = gathered[0]
        for d in range(1, N_DEV):
            tot = tot + gathered[d]

        conv = pltpu.roll(tot[:, 0:512], (512 - 64 * me_idx) % 512, 1)[3:6, 0:64]
        grads = (tot[0:1, :], conv, tot[6:7, 0:8], tot[1:2, :], tot[2:3, :])
        loss_out[...] = tot[7:8, 0:128]
        for j in range(5):
            g_outs[j][...] = grads[j]
            d_outs[j][...], m_outs[j][...], v_outs[j][...] = _adamw_math(
                w_refs[j][...], grads[j], m_refs[j][...], v_refs[j][...])

    specs = [_full(w.shape) for w in ws]
    res = pl.pallas_call(
        body, name="small_step", grid=(1,),
        out_shape=[jax.ShapeDtypeStruct((1, 128), F32)] + [jax.ShapeDtypeStruct(w.shape, F32) for w in ws] * 4,
        in_specs=[_full(p.shape) for p in parts] + specs * 3,
        out_specs=[_full((1, 128))] + specs * 4,
        scratch_shapes=[pltpu.VMEM((8, D_MODEL), F32), pltpu.VMEM((N_DEV, 8, D_MODEL), F32),
                        pltpu.SemaphoreType.DMA((N_PEERS,)), pltpu.SemaphoreType.DMA((N_PEERS,))],
    )(*parts, *ws, *ms, *vs)
    return res[0], res[1:6], res[6:11], res[11:16], res[16:21]


def kernel(x, mem, g_pre, w_in, w_conv, attn_sink, g_mem, w_mem_kv, w_up_a, w_up_b, w_up_m, w_out, g_post, loss_target, m_g_pre, m_w_in, m_w_conv, m_attn_sink, m_g_mem, m_w_mem_kv, m_w_up_a, m_w_up_b, m_w_up_m, m_w_out, m_g_post, v_g_pre, v_w_in, v_w_conv, v_attn_sink, v_g_mem, v_w_mem_kv, v_w_up_a, v_w_up_b, v_w_up_m, v_w_out, v_g_post):
    s = x.shape[1]
    x2, mem2, tgt2 = x[0], mem[0], loss_target[0]
    me = 4 * lax.axis_index("x") + 2 * lax.axis_index("y") + lax.axis_index("c")

    w_up_loc = jnp.concatenate([w_up_a[0], w_up_b[0], w_up_m[0]], axis=0).astype(BF16)
    w_conv_loc = jnp.zeros((8, 128), F32).at[:3, :64].set(w_conv[0])
    w_int_g, w_conv_g = _all_gather([w_in[0].T.astype(BF16), w_conv_loc], "gather_w_in",
                                    splits=[[(0, 240), (240, 240), (480, 224), (704, 224)], [(0, 8)]])
    w_int = w_int_g.reshape(IN_WIDTH, D_MODEL)
    w_conv_f = w_conv_g[:, :3, :64].transpose(1, 0, 2).reshape(3, 512)
    late = _gather_start([w_mem_kv[0].astype(BF16) + w_conv_g[0, 7:8, 0:1].astype(BF16),
                          w_out[0].astype(BF16), w_up_loc], me, "gather_late_start")
    sink = attn_sink[0]
    tabs = _rope_tables(s)

    h, pa, pq, pkv, pbz, pmq, pmz, pg = _proj_fwd(x2, g_pre + late[4][0:1, 0:1], w_int, tabs)
    ya = _conv_fwd(pa, w_conv_f)
    yb = _attn_fwd(pq, pkv, pbz, sink)
    w_mkv_g, w_out_g, w_up_g = _gather_wait(*late[:4], yb, "gather_late_wait")
    w_mkv = w_mkv_g.reshape(D_MODEL, D_MODEL)
    w_out_f = w_out_g.reshape(D_MODEL, D_MODEL)
    mn, mkv = _mem_kv_fwd(mem2, g_mem, w_mkv)
    ym = _mem_attn_fwd(pmq, pmz, mkv)
    dg, dya, dyb, dym, dy, loss_p, gg_post, mb, dob, du = _mid(ya, yb, ym, pg, x2, tgt2, g_post, w_up_g, w_out_f)
    gw_out, gw_up = _gw_mid(mb, dob, (ya, yb, ym), du)

    core = lax.axis_index("c").astype(jnp.int32).reshape(1)
    chip = (2 * lax.axis_index("x") + lax.axis_index("y")).astype(jnp.int32).reshape(1)

    def exchange_start(shares, tag):
        from_sibling = _sibling_exchange(shares, "grads_to_sibling_" + tag)
        chip_shares = _pair_add(shares, from_sibling, core, "grads_pair_add_" + tag)
        return _chip_exchange_start(chip_shares, "grads_to_chips_start_" + tag)

    dmq, dmz, dmkv = _mem_attn_bwd(pmq, pmz, mkv, dym)
    gw_mkv, gg_mem = _mem_kv_bwd(mem2, g_mem, mn, dmkv, w_mkv)
    send1, recv1, srcs1, lands1, token1 = exchange_start(
        [gw_mkv.reshape(N_DEV, 128, D_MODEL), gw_out.reshape(N_DEV, 128, D_MODEL), gw_up], "small")
    da, gw_conv = _conv_bwd(pa, dya, w_conv_f + token1[0:1, 0:1])
    dq, dbz, dkv, g_sink = _attn_bwd(pq, pkv, pbz, dyb, sink, tabs)
    dparts = (da, dq, dkv, dbz, dmq, dmz, dg)
    gw_int = _gw_in(dparts, h)
    send2, recv2, srcs2, lands2, token2 = exchange_start([gw_int.reshape(N_DEV, SHARD_IN, D_MODEL)], "w_in")
    grad_x, gg_pre = _dh_bwd(dparts, x2, dy, g_pre + token2[0:1, 0:1], w_int)
    (o_mkv, o_out, o_up, o_int), (l_mkv, l_out, l_up, l_int) = _chip_exchange_wait(
        send1 + send2, recv1 + recv2, srcs1 + srcs2, lands1 + lands2, grad_x, "grads_to_chips_wait")

    loss_row, small_g, sd, sm, sv = _small_step(
        (gg_pre, gw_conv, g_sink, gg_mem, gg_post, loss_p),
        [g_pre, w_conv[0], attn_sink, g_mem, g_post],
        [m_g_pre, m_w_conv[0], m_attn_sink, m_g_mem, m_g_post],
        [v_g_pre, v_w_conv[0], v_attn_sink, v_g_mem, v_g_post])
    loss = loss_row[0, 0]
    g_g_pre, g_conv, g_sink_tot, g_g_mem, g_g_post = small_g

    g_w_in, d_w_in, nm_w_in, nv_w_in = (t.T for t in _sum_adamw(
        o_int, l_int, chip, 0, w_in[0].T, m_w_in[0].T, v_w_in[0].T, "adamw_w_in", tiles=2))
    (g_mkv, d_mkv, nm_mkv, nv_mkv), (g_out, d_out, nm_out, nv_out), *up = _sum_adamw_group(
        [(o_mkv, l_mkv, 0, w_mem_kv[0], m_w_mem_kv[0], v_w_mem_kv[0]),
         (o_out, l_out, 0, w_out[0], m_w_out[0], v_w_out[0]),
         (o_up, l_up, 0, w_up_a[0], m_w_up_a[0], v_w_up_a[0]),
         (o_up, l_up, 1, w_up_b[0], m_w_up_b[0], v_w_up_b[0]),
         (o_up, l_up, 2, w_up_m[0], m_w_up_m[0], v_w_up_m[0])], chip, "adamw_mid_weights")

    def lead(a):
        return a[None]

    grads = [g_g_pre, lead(g_w_in), lead(g_conv), g_sink_tot, g_g_mem, lead(g_mkv), lead(up[0][0]),
             lead(up[1][0]), lead(up[2][0]), lead(g_out), g_g_post]

    def assemble(small, big_in, big_mkv, big_up, big_out):
        return [small[0], lead(big_in), lead(small[1]), small[2], small[3], lead(big_mkv), lead(big_up[0]),
                lead(big_up[1]), lead(big_up[2]), lead(big_out), small[4]]

    deltas = assemble(sd, d_w_in, d_mkv, [u[1] for u in up], d_out)
    new_m = assemble(sm, nm_w_in, nm_mkv, [u[2] for u in up], nm_out)
    new_v = assemble(sv, nv_w_in, nv_mkv, [u[3] for u in up], nv_out)
    return (loss, grad_x[None], *grads, *deltas, *new_m, *new_v)
```

```python
import functools

import jax
import jax.numpy as jnp
from jax import lax
from jax.experimental import pallas as pl
from jax.experimental.pallas import tpu as pltpu

F32 = jnp.float32
BF16 = jnp.bfloat16
MESH = pl.DeviceIdType.MESH

N_DEV = 8
D_MODEL = 1024
EPS = 1e-6
ROPE_THETA = 500000.0
ROT_DIM = 16
HEAD_DIM = 64
ATTN_BLOCK = 128
MEM_HEADS = 4
MEM_HEAD_DIM = 128
ATTN_SCALE = HEAD_DIM ** -0.5
MEM_SCALE = MEM_HEAD_DIM ** -0.5

ADAM_LR = 0.001
ADAM_B1 = 0.9
ADAM_B2 = 0.999
ADAM_EPS = 1e-08
ADAM_WD = 0.01
ADAM_STEP = 10

SEG_A = (0, 2048)
SEG_BQ = (2048, 512)
SEG_BKV = (2560, 256)
SEG_BZ = (2816, 512)
SEG_MQ = (3328, 512)
SEG_MZ = (3840, 512)
SEG_G = (4352, 3072)
SEGS = (SEG_A, SEG_BQ, SEG_BKV, SEG_BZ, SEG_MQ, SEG_MZ, SEG_G)
IN_WIDTH = 7424
SHARD_IN = IN_WIDTH // N_DEV

V7X_VMEM_BYTES = 64 * 1024 * 1024
ANY = pl.BlockSpec(memory_space=pl.ANY)


def _params(vmem_mb):
    assert vmem_mb * 1024 * 1024 < V7X_VMEM_BYTES
    return pltpu.CompilerParams(dimension_semantics=("arbitrary",), vmem_limit_bytes=vmem_mb * 1024 * 1024)


def _full(shape):
    zeros = (0,) * len(shape)
    return pl.BlockSpec(shape, lambda i: zeros)


def _rows(tm, width):
    return pl.BlockSpec((tm, width), lambda i: (i, 0))


def _dot(a, b):
    return jnp.dot(a, b, preferred_element_type=F32)


def _dot_nt(a, b):
    return lax.dot_general(a, b, (((1,), (1,)), ((), ())), preferred_element_type=F32)


def _dot_tn(a, b):
    return lax.dot_general(a, b, (((0,), (0,)), ((), ())), preferred_element_type=F32)


def _sigmoid(z):
    return 1.0 / (1.0 + jnp.exp(-z))


def _rope(t, cs, s1, s2):
    return t * cs + pltpu.roll(t, 120, 1) * s1 + pltpu.roll(t, 8, 1) * s2


def _rope_t(d, cs, s1, s2):
    return d * cs + pltpu.roll(d * s1, 8, 1) + pltpu.roll(d * s2, 120, 1)


def _rope_tables(s):
    half = ROT_DIM // 2
    inv_freq = jnp.power(jnp.float32(ROPE_THETA), -jnp.arange(half, dtype=F32) * (2.0 / ROT_DIM))
    ang = jnp.arange(s).astype(F32)[:, None] * inv_freq[None, :]
    cos, sin = jnp.cos(ang), jnp.sin(ang)
    d = jnp.arange(128) % HEAD_DIM
    k = jnp.arange(half)[:, None]
    lo = (d[None, :] == k).astype(F32)
    hi = (d[None, :] == k + half).astype(F32)
    spread = functools.partial(jnp.dot, precision=lax.Precision.HIGHEST)
    return (spread(cos, lo + hi) + (d >= ROT_DIM).astype(F32)[None, :], -spread(sin, lo), spread(sin, hi))


def _load_once(pairs, sems):
    @pl.when(pl.program_id(0) == 0)
    def _():
        cps = [pltpu.make_async_copy(src, dst, sems.at[k]) for k, (src, dst) in enumerate(pairs)]
        for cp in cps:
            cp.start()
        for cp in cps:
            cp.wait()


def _my_place():
    x, y, c = lax.axis_index("x"), lax.axis_index("y"), lax.axis_index("c")
    return x, y, c


def _all_gather(arrs, name, splits=None):
    n = len(arrs)
    if splits is None:
        splits = [[(0, a.shape[0])] for a in arrs]
    pieces = [(a, r0, rn) for a in range(n) for r0, rn in splits[a]]
    n_p = len(pieces)

    def body(*refs):
        ins, outs = refs[:n], refs[n:2 * n]
        send_sems, recv_sems, local_sems = refs[2 * n:]
        x, y, c = _my_place()
        me, sibling = (x, y, c), (x, y, 1 - c)

        def route(core):
            first = (jnp.bitwise_xor(x, 1 - core), jnp.bitwise_xor(y, core), core)
            second = (jnp.bitwise_xor(x, core), jnp.bitwise_xor(y, 1 - core), core)
            return first, second, (1 - x, 1 - y, core)

        def idx(px, py, pc):
            return 4 * px + 2 * py + pc

        def copy(p, k, block, to, own=False):
            a, r0, rn = pieces[p]
            dst = outs[a].at[idx(*block), pl.ds(r0, rn)]
            return pltpu.make_async_remote_copy(
                src_ref=ins[a].at[pl.ds(r0, rn)] if own else dst, dst_ref=dst,
                send_sem=send_sems.at[p * 7 + k], recv_sem=recv_sems.at[p * 7 + k],
                device_id=to, device_id_type=MESH)

        nbr1, nbr2, diag = route(c)
        mine = [pltpu.make_async_copy(ins[a], outs[a].at[idx(*me)], local_sems.at[a]) for a in range(n)]
        for cp in mine:
            cp.start()
        sent = []
        for p in range(n_p):
            for k, to in enumerate((sibling, nbr1, nbr2)):
                sent.append(copy(p, k, me, to, own=True))
        for cp in sent:
            cp.start()
        for k_in, block, onward in ((1, nbr1, ((3, nbr2), (4, sibling))), (2, nbr2, ((5, sibling),)),
                                    (3, diag, ((6, sibling),))):
            for p in range(n_p):
                copy(p, k_in, block, me).wait_recv()
                for k_out, to in onward:
                    cp = copy(p, k_out, block, to)
                    cp.start()
                    sent.append(cp)
        s1, s2, sd = route(1 - c)
        for k_in, block in ((0, sibling), (4, s1), (5, s2), (6, sd)):
            for p in range(n_p):
                copy(p, k_in, block, me).wait_recv()
        for cp in sent:
            cp.wait_send()
        for cp in mine:
            cp.wait()

    return pl.pallas_call(
        body, name=name,
        out_shape=[jax.ShapeDtypeStruct((N_DEV,) + a.shape, a.dtype) for a in arrs],
        in_specs=[ANY] * n, out_specs=[ANY] * n,
        scratch_shapes=[pltpu.SemaphoreType.DMA((7 * n_p,)), pltpu.SemaphoreType.DMA((7 * n_p,)),
                        pltpu.SemaphoreType.DMA((n,))],
    )(*arrs)


N_CHIPS = 4


def _sibling_exchange(arrs, name):
    n = len(arrs)

    def body(*refs):
        ins, outs = refs[:n], refs[n:2 * n]
        send_sems, recv_sems = refs[2 * n:]
        x, y, c = _my_place()
        sibling = (x, y, 1 - c)

        def copy(a, j):
            return pltpu.make_async_remote_copy(
                src_ref=ins[a].at[2 * j + (1 - c)], dst_ref=outs[a].at[j],
                send_sem=send_sems.at[a * N_CHIPS + j], recv_sem=recv_sems.at[a * N_CHIPS + j],
                device_id=sibling, device_id_type=MESH)

        cps = [copy(a, j) for j in range(N_CHIPS) for a in range(n)]
        for cp in cps:
            cp.start()
        for cp in cps:
            cp.wait_recv()
        for cp in cps:
            cp.wait_send()

    return pl.pallas_call(
        body, name=name,
        out_shape=[jax.ShapeDtypeStruct((N_CHIPS,) + a.shape[1:], a.dtype) for a in arrs],
        in_specs=[ANY] * n, out_specs=[ANY] * n,
        scratch_shapes=[pltpu.SemaphoreType.DMA((N_CHIPS * n,)), pltpu.SemaphoreType.DMA((N_CHIPS * n,))],
    )(*arrs)


def _pair_add(mine, recv, core, name):
    n = len(mine)

    def body(c_ref, *refs):
        for a in range(n):
            refs[2 * n + a][...] = (refs[a][...].astype(F32) + refs[n + a][...].astype(F32)).astype(BF16)

    def blk(a):
        return (None,) + a.shape[1:]

    grid_spec = pltpu.PrefetchScalarGridSpec(
        num_scalar_prefetch=1, grid=(N_CHIPS,),
        in_specs=[pl.BlockSpec(blk(a), lambda j, c_ref: (2 * j + c_ref[0], 0, 0)) for a in mine]
        + [pl.BlockSpec(blk(a), lambda j, c_ref: (j, 0, 0)) for a in recv],
        out_specs=[pl.BlockSpec(blk(a), lambda j, c_ref: (j, 0, 0)) for a in recv])
    return pl.pallas_call(
        body, name=name, grid_spec=grid_spec,
        out_shape=[jax.ShapeDtypeStruct(a.shape, BF16) for a in recv],
        compiler_params=_params(32),
    )(core, *mine, *recv)


HBM = pl.BlockSpec(memory_space=pltpu.HBM)
SEM = pl.BlockSpec(memory_space=pltpu.SEMAPHORE)
N_PEER_CHIPS = 3


def _chip_copies(srcs, lands, send_sems, recv_sems):
    x, y, c = _my_place()
    my_chip = 2 * x + y
    peers = [(x, 1 - y), (1 - x, y), (1 - x, 1 - y)]
    cps = []
    for k, (px, py) in enumerate(peers):
        for a in range(len(srcs)):
            j = a * N_PEER_CHIPS + k
            cps.append(pltpu.make_async_remote_copy(
                src_ref=srcs[a].at[2 * px + py], dst_ref=lands[a].at[my_chip],
                send_sem=send_sems[j], recv_sem=recv_sems[j],
                device_id=(px, py, c), device_id_type=MESH))
    return cps


N_PEERS = N_DEV - 1


def _gather_copies(srcs, lands, send_sems, recv_sems):
    x, y, c = _my_place()
    me_idx = 4 * x + 2 * y + c
    flips = [(0, 0, 1), (0, 1, 0), (1, 0, 0), (0, 1, 1), (1, 0, 1), (1, 1, 0), (1, 1, 1)]
    cps = []
    for k, (fx, fy, fc) in enumerate(flips):
        peer = ((1 - x) if fx else x, (1 - y) if fy else y, (1 - c) if fc else c)
        for a in range(len(srcs)):
            j = a * N_PEERS + k
            cps.append(pltpu.make_async_remote_copy(
                src_ref=srcs[a], dst_ref=lands[a].at[me_idx], send_sem=send_sems[j], recv_sem=recv_sems[j],
                device_id=peer, device_id_type=MESH))
    return cps


def _split_start(copies, per_array, arrs, lands, name):
    arrs, lands = list(arrs), list(lands)
    n = len(arrs)
    k = n * per_array

    def body(*refs):
        srcs, land_refs = refs[:n], refs[n:2 * n]
        send_sems, recv_sems = refs[2 * n:2 * n + k], refs[2 * n + k:2 * n + 2 * k]
        token = refs[-1]
        for cp in copies(srcs, land_refs, send_sems, recv_sems):
            cp.start()
        token[...] = jnp.zeros_like(token)

    hbm_arrs = [pltpu.with_memory_space_constraint(a, pltpu.HBM) for a in arrs]
    lands = [pltpu.with_memory_space_constraint(a, pltpu.HBM) for a in lands]
    res = pl.pallas_call(
        body, name=name,
        out_shape=[pltpu.SemaphoreType.DMA(())] * (2 * k) + [pltpu.HBM(a.shape, a.dtype) for a in arrs + lands]
        + [jax.ShapeDtypeStruct((8, 128), F32)],
        in_specs=[HBM] * (2 * n),
        out_specs=[SEM] * (2 * k) + [HBM] * (2 * n) + [pl.BlockSpec(memory_space=pltpu.VMEM)],
        input_output_aliases={a: 2 * k + a for a in range(2 * n)},
        compiler_params=pltpu.CompilerParams(has_side_effects=pltpu.SideEffectType.DATAFLOW_SIDE_EFFECTING),
    )(*hbm_arrs, *lands)
    return res[:k], res[k:2 * k], res[2 * k:2 * k + n], res[2 * k + n:2 * k + 2 * n], res[-1]


def _split_wait(copies, per_array, send_sems, recv_sems, srcs, lands, after, name):
    n = len(srcs)
    k = n * per_array

    def body(*refs):
        src_refs, land_refs = refs[:n], refs[n:2 * n]
        s_sems, r_sems = refs[2 * n:2 * n + k], refs[2 * n + k:2 * n + 2 * k]
        for cp in copies(src_refs, land_refs, s_sems, r_sems):
            cp.wait_send()
            cp.wait_recv()

    res = pl.pallas_call(
        body, name=name,
        out_shape=[pltpu.HBM(a.shape, a.dtype) for a in list(srcs) + list(lands)],
        in_specs=[HBM] * (2 * n) + [SEM] * (2 * k) + [ANY],
        out_specs=[HBM] * (2 * n),
        input_output_aliases={a: a for a in range(2 * n)},
        compiler_params=pltpu.CompilerParams(has_side_effects=pltpu.SideEffectType.DATAFLOW_SIDE_EFFECTING),
    )(*srcs, *lands, *send_sems, *recv_sems, after)
    return res[:n], res[n:]


def _chip_exchange_start(arrs, name):
    return _split_start(_chip_copies, N_PEER_CHIPS, arrs, [lax.empty(a.shape, a.dtype) for a in arrs], name)


def _chip_exchange_wait(send_sems, recv_sems, srcs, lands, after, name):
    return _split_wait(_chip_copies, N_PEER_CHIPS, send_sems, recv_sems, srcs, lands, after, name)


def _gather_start(arrs, me_idx, name):
    lands = [lax.dynamic_update_slice(lax.empty((N_DEV,) + a.shape, a.dtype), a[None], (me_idx, 0, 0)) for a in arrs]
    return _split_start(_gather_copies, N_PEERS, arrs, lands, name)


def _gather_wait(send_sems, recv_sems, srcs, lands, after, name):
    return _split_wait(_gather_copies, N_PEERS, send_sems, recv_sems, srcs, lands, after, name)[1]


def _proj_fwd(x, g_pre, w_int, tabs):
    s = x.shape[0]
    tm = min(512, s)

    def body(x_ref, g_ref, cs_ref, s1_ref, s2_ref, w_hbm,
             h_ref, pa_ref, pq_ref, pkv_ref, pbz_ref, pmq_ref, pmz_ref, pg_ref, w_vm, sems):
        _load_once([(w_hbm, w_vm)], sems)
        xf = x_ref[...]
        r = lax.rsqrt(jnp.mean(xf * xf, axis=-1, keepdims=True) + EPS)
        h = ((xf * r) * g_ref[...]).astype(BF16)
        h_ref[...] = h
        cs, s1, s2 = cs_ref[...], s1_ref[...], s2_ref[...]

        def mm(seg, c0, width):
            return _dot_nt(h, w_vm[seg[0] + c0:seg[0] + c0 + width, :])

        for c0 in range(0, SEG_A[1], 512):
            pa_ref[:, c0:c0 + 512] = mm(SEG_A, c0, 512).astype(BF16)
        q = mm(SEG_BQ, 0, 512)
        for b in range(4):
            pq_ref[:, 128 * b:128 * b + 128] = _rope(q[:, 128 * b:128 * b + 128], cs, s1, s2).astype(BF16)
        kv = mm(SEG_BKV, 0, 256)
        pkv_ref[:, 0:128] = _rope(kv[:, 0:128], cs, s1, s2).astype(BF16)
        pkv_ref[:, 128:256] = kv[:, 128:256].astype(BF16)
        pbz_ref[...] = mm(SEG_BZ, 0, 512).astype(BF16)
        pmq_ref[...] = mm(SEG_MQ, 0, 512).astype(BF16)
        pmz_ref[...] = mm(SEG_MZ, 0, 512).astype(BF16)
        for c0 in range(0, SEG_G[1], 512):
            pg_ref[:, c0:c0 + 512] = mm(SEG_G, c0, 512).astype(BF16)

    widths = (D_MODEL, 2048, 512, 256, 512, 512, 512, 3072)
    return pl.pallas_call(
        body, name="proj_fwd", grid=(s // tm,),
        out_shape=[jax.ShapeDtypeStruct((s, w), BF16) for w in widths],
        in_specs=[_rows(tm, D_MODEL), _full((1, D_MODEL)), _rows(tm, 128), _rows(tm, 128), _rows(tm, 128), ANY],
        out_specs=[_rows(tm, w) for w in widths],
        scratch_shapes=[pltpu.VMEM((IN_WIDTH, D_MODEL), BF16), pltpu.SemaphoreType.DMA((1,))],
        compiler_params=_params(52),
    )(x, g_pre, *tabs, w_int)


def _mem_kv_fwd(mem, g_mem, w_mkv):
    m = mem.shape[0]

    def body(mem_ref, g_ref, w_ref, mn_ref, mkv_ref):
        xf = mem_ref[...]
        r = lax.rsqrt(jnp.mean(xf * xf, axis=-1, keepdims=True) + EPS)
        mn = ((xf * r) * g_ref[...]).astype(BF16)
        mn_ref[...] = mn
        mkv_ref[...] = _dot(mn, w_ref[...]).astype(BF16)

    return pl.pallas_call(
        body, name="mem_kv_fwd", grid=(1,),
        out_shape=[jax.ShapeDtypeStruct((m, D_MODEL), BF16)] * 2,
        in_specs=[_full((m, D_MODEL)), _full((1, D_MODEL)), _full((D_MODEL, D_MODEL))],
        out_specs=[_full((m, D_MODEL))] * 2,
        compiler_params=_params(32),
    )(mem, g_mem, w_mkv)


def _halo_specs(s, tm, rows, width):
    nblk = s // rows
    prev = pl.BlockSpec((rows, width), lambda i: (jnp.maximum(i * (tm // rows) - 1, 0), 0))
    nxt = pl.BlockSpec((rows, width), lambda i: (jnp.minimum((i + 1) * (tm // rows), nblk - 1), 0))
    return prev, nxt


def _conv_common(pa, prev_row, next_row, w, first, last, tm):
    b, c, u, z = (pa[:, 512 * k:512 * k + 512] for k in range(4))
    cu = c * u
    cu_prev = jnp.where(first, 0.0, prev_row[:, 512:1024] * prev_row[:, 1024:1536])
    cu_next = jnp.where(last, 0.0, next_row[:, 512:1024] * next_row[:, 1024:1536])
    row = lax.broadcasted_iota(jnp.int32, (tm, 512), 0)
    cu_m1 = jnp.where(row == 0, cu_prev, pltpu.roll(cu, 1, 0))
    cu_p1 = jnp.where(row == tm - 1, cu_next, pltpu.roll(cu, tm - 1, 0))
    y = cu_m1 * w[0:1] + cu * w[1:2] + cu_p1 * w[2:3]
    sig = _sigmoid(z)
    return b, c, u, z, cu, cu_m1, cu_p1, y, sig, row


def _conv_fwd(pa, w_conv):
    s = pa.shape[0]
    tm = min(512, s)
    nt = s // tm

    def body(pa_ref, pp_ref, pn_ref, w_ref, ya_ref):
        i = pl.program_id(0)
        prev_row = pp_ref[...].astype(F32)[15:16, :]
        next_row = pn_ref[...].astype(F32)[0:1, :]
        b, _, _, z, _, _, _, y, sig, _ = _conv_common(
            pa_ref[...].astype(F32), prev_row, next_row, w_ref[...], i == 0, i == nt - 1, tm)
        ya_ref[...] = (b * y * (z * sig)).astype(BF16)

    prev, nxt = _halo_specs(s, tm, 16, 2048)
    return pl.pallas_call(
        body, name="conv_fwd", grid=(nt,),
        out_shape=jax.ShapeDtypeStruct((s, 512), BF16),
        in_specs=[_rows(tm, 2048), prev, nxt, _full((3, 512))],
        out_specs=_rows(tm, 512),
        compiler_params=_params(48),
    )(pa, pa, pa, w_conv)


def _heads_to_lanes(a, g, row):
    low = row < HEAD_DIM
    parts = []
    for b in (2 * g, 2 * g + 1):
        t = jnp.transpose(a[:, 128 * b:128 * b + 128])
        swapped = pltpu.roll(t, HEAD_DIM, 0)
        if g == 0:
            parts += [jnp.where(low, t, 0.0), jnp.where(low, swapped, 0.0)]
        else:
            parts += [jnp.where(low, 0.0, swapped), jnp.where(low, 0.0, t)]
    return jnp.concatenate(parts, axis=1)


def _lanes_to_heads(t0, t1, row):
    low = row < HEAD_DIM
    blocks = []
    for b in range(4):
        g = b // 2
        tg = (t0, t1)[g]
        je = 2 * (b - 2 * g)
        even, odd = tg[:, 128 * je:128 * je + 128], tg[:, 128 * je + 128:128 * je + 256]
        if g == 0:
            t = jnp.where(low, even, pltpu.roll(odd, HEAD_DIM, 0))
        else:
            t = jnp.where(low, pltpu.roll(even, HEAD_DIM, 0), odd)
        blocks.append(jnp.transpose(t))
    return jnp.concatenate(blocks, axis=1)


WINDOW_KEYS = 3 * ATTN_BLOCK
STACKED = 4 * ATTN_BLOCK
KEY_CHUNK = 32


def _fill_band_bias(bias, nb):
    assert nb >= 2
    c = lax.broadcasted_iota(jnp.int32, (WINDOW_KEYS, STACKED), 0)
    r = lax.broadcasted_iota(jnp.int32, (WINDOW_KEYS, STACKED), 1) & (ATTN_BLOCK - 1)
    band = (c >= r) & (c <= r + 2 * ATTN_BLOCK)
    for v, ok in enumerate((band, band & (c >= ATTN_BLOCK), band & (c < 2 * ATTN_BLOCK))):
        bias[v] = jnp.where(ok, 0.0, -jnp.inf)


def _bias_variant(n, nb):
    return jnp.where(n == 0, 1, jnp.where(n == nb - 1, 2, 0))


def _sink_row(sink_ref, g):
    return jnp.concatenate([jnp.full((1, ATTN_BLOCK), sink_ref[4 * g + j], F32) for j in range(4)], axis=1)


def _softmax_keys_major(sc, bias, variant, sink, e_scr):
    chunks = [pl.ds(k * KEY_CHUNK, KEY_CHUNK) for k in range(WINDOW_KEYS // KEY_CHUNK)]
    rows = [slice(k * KEY_CHUNK, (k + 1) * KEY_CHUNK) for k in range(WINDOW_KEYS // KEY_CHUNK)]
    m_run = jnp.full((KEY_CHUNK, STACKED), -jnp.inf, F32)
    for ck, rw in zip(chunks, rows):
        m_run = jnp.maximum(m_run, sc[rw] * ATTN_SCALE + bias[variant, ck, :])
    m = jnp.maximum(jnp.max(m_run, axis=0, keepdims=True), sink)
    l_run = jnp.zeros((KEY_CHUNK, STACKED), F32)
    for ck, rw in zip(chunks, rows):
        e = jnp.exp(sc[rw] * ATTN_SCALE + bias[variant, ck, :] - m)
        l_run += e
        e_scr[rw, :] = e.astype(BF16)
    es = jnp.exp(sink - m)
    inv = 1.0 / (jnp.sum(l_run, axis=0, keepdims=True) + es)
    return inv, es * inv


def _fill_padded(kv_ref, kpad, vpad, s):
    zero = jnp.zeros((ATTN_BLOCK, 128), BF16)
    kpad[0:ATTN_BLOCK, :] = zero
    vpad[0:ATTN_BLOCK, :] = zero
    kpad[ATTN_BLOCK + s:2 * ATTN_BLOCK + s, :] = zero
    vpad[ATTN_BLOCK + s:2 * ATTN_BLOCK + s, :] = zero
    kpad[ATTN_BLOCK:ATTN_BLOCK + s, :] = kv_ref[:, 0:128]
    vpad[ATTN_BLOCK:ATTN_BLOCK + s, :] = kv_ref[:, 128:256]


def _attn_fwd(pq, pkv, pbz, sink):
    s = pq.shape[0]
    nb = s // ATTN_BLOCK

    def body(sink_ref, q_ref, z_ref, kv_ref, yb_ref, kpad, vpad, bias, e_scr):
        n = pl.program_id(0)

        @pl.when(n == 0)
        def _():
            _fill_padded(kv_ref, kpad, vpad, s)
            _fill_band_bias(bias, nb)

        row = lax.broadcasted_iota(jnp.int32, (ATTN_BLOCK, 128), 0)
        start = pl.multiple_of(n * ATTN_BLOCK, ATTN_BLOCK)
        kw, vw = kpad[pl.ds(start, WINDOW_KEYS), :], vpad[pl.ds(start, WINDOW_KEYS), :]
        qf = q_ref[...].astype(F32)
        variant = _bias_variant(n, nb)
        outs = []
        for g in range(2):
            qt = _heads_to_lanes(qf, g, row).astype(BF16)
            inv, _ = _softmax_keys_major(_dot(kw, qt), bias, variant, _sink_row(sink_ref, g), e_scr)
            outs.append(_dot_tn(vw, e_scr[...]) * inv)
        attn = _lanes_to_heads(outs[0], outs[1], row)
        z = z_ref[...].astype(F32)
        yb_ref[...] = (attn * (z * _sigmoid(z))).astype(BF16)

    return pl.pallas_call(
        body, name="attn_fwd", grid=(nb,),
        out_shape=jax.ShapeDtypeStruct((s, 512), BF16),
        in_specs=[pl.BlockSpec(memory_space=pltpu.SMEM), _rows(ATTN_BLOCK, 512), _rows(ATTN_BLOCK, 512),
                  _full((s, 256))],
        out_specs=_rows(ATTN_BLOCK, 512),
        scratch_shapes=[pltpu.VMEM((s + 2 * ATTN_BLOCK, 128), BF16)] * 2
        + [pltpu.VMEM((3, WINDOW_KEYS, STACKED), F32), pltpu.VMEM((WINDOW_KEYS, STACKED), BF16)],
        compiler_params=_params(32),
    )(sink, pq, pbz, pkv)


def _mem_softmax_t(q, mk):
    sc = _dot_nt(mk, q) * MEM_SCALE
    e = jnp.exp(sc - jnp.max(sc, axis=0, keepdims=True))
    return e * (1.0 / jnp.sum(e, axis=0, keepdims=True))


def _mem_attn_fwd(pmq, pmz, mkv):
    s = pmq.shape[0]
    m = mkv.shape[0]
    tm = min(512, s)

    def body(q_ref, z_ref, mk_ref, mv_ref, ym_ref):
        z = z_ref[...].astype(F32)
        sz = z * _sigmoid(z)
        for h in range(MEM_HEADS):
            cols = slice(128 * h, 128 * h + 128)
            pt = _mem_softmax_t(q_ref[:, cols], mk_ref[:, cols])
            o = _dot_tn(pt.astype(BF16), mv_ref[:, cols])
            ym_ref[:, cols] = (o * sz[:, cols]).astype(BF16)

    return pl.pallas_call(
        body, name="mem_attn_fwd", grid=(s // tm,),
        out_shape=jax.ShapeDtypeStruct((s, 512), BF16),
        in_specs=[_rows(tm, 512), _rows(tm, 512), pl.BlockSpec((m, 512), lambda i: (0, 0)),
                  pl.BlockSpec((m, 512), lambda i: (0, 1))],
        out_specs=_rows(tm, 512),
        compiler_params=_params(32),
    )(pmq, pmz, mkv, mkv)


def _mid(ya, yb, ym, pg, x, target, g_post, w_up, w_out):
    s = x.shape[0]
    tm = min(256, s)
    nt = s // tm

    def body(ya_ref, yb_ref, ym_ref, pg_ref, x_ref, t_ref, gp_ref, wup_hbm, wout_hbm,
             dg_ref, dya_ref, dyb_ref, dym_ref, dy_ref, loss_ref, ggp_ref, mb_ref, dob_ref, du_ref,
             wup_vm, wout_vm, sems):
        i = pl.program_id(0)
        _load_once([(wup_hbm.at[d], wup_vm.at[:, pl.ds(128 * d, 128)]) for d in range(N_DEV)]
                   + [(wout_hbm, wout_vm)], sems)

        @pl.when(i == 0)
        def _():
            loss_ref[...] = jnp.zeros_like(loss_ref)
            ggp_ref[...] = jnp.zeros_like(ggp_ref)

        ys = (ya_ref[...], yb_ref[...], ym_ref[...])
        us = [_dot(ys[k], wup_vm[512 * k:512 * k + 512, :]) for k in range(3)]
        gates = [_sigmoid(pg_ref[:, 1024 * k:1024 * k + 1024].astype(F32)) for k in range(3)]
        merged = gates[0] * us[0] + gates[1] * us[1] + gates[2] * us[2]
        mb = merged.astype(BF16)
        mb_ref[...] = mb
        out = _dot(mb, wout_vm[...])
        r = lax.rsqrt(jnp.mean(out * out, axis=-1, keepdims=True) + EPS)
        on = out * r
        gp = gp_ref[...]
        err = (x_ref[...] + on * gp) - t_ref[...]
        loss_ref[...] += 0.5 * jnp.sum(err * err) * (1.0 / D_MODEL)
        dy = err * (1.0 / D_MODEL)
        dy_ref[...] = dy
        ggp_ref[...] += jnp.sum(dy * on, axis=0, keepdims=True)
        a = dy * gp
        d_out = r * (a - on * jnp.mean(a * on, axis=-1, keepdims=True))
        dob = d_out.astype(BF16)
        dob_ref[...] = dob
        d_merged = _dot_nt(dob, wout_vm[...])
        d_refs = (dya_ref, dyb_ref, dym_ref)
        for k in range(3):
            g = gates[k]
            dg_ref[:, 1024 * k:1024 * k + 1024] = (d_merged * us[k] * g * (1.0 - g)).astype(BF16)
            du = (d_merged * g).astype(BF16)
            du_ref[k] = du
            d_refs[k][...] = _dot_nt(du, wup_vm[512 * k:512 * k + 512, :])

    return pl.pallas_call(
        body, name="mid", grid=(nt,),
        out_shape=[jax.ShapeDtypeStruct((s, 3072), BF16)] + [jax.ShapeDtypeStruct((s, 512), F32)] * 3
        + [jax.ShapeDtypeStruct((s, D_MODEL), F32), jax.ShapeDtypeStruct((8, 128), F32),
           jax.ShapeDtypeStruct((1, D_MODEL), F32), jax.ShapeDtypeStruct((s, D_MODEL), BF16),
           jax.ShapeDtypeStruct((s, D_MODEL), BF16), jax.ShapeDtypeStruct((3, s, D_MODEL), BF16)],
        in_specs=[_rows(tm, 512)] * 3 + [_rows(tm, 3072), _rows(tm, D_MODEL), _rows(tm, D_MODEL),
                                         _full((1, D_MODEL)), ANY, ANY],
        out_specs=[_rows(tm, 3072)] + [_rows(tm, 512)] * 3
        + [_rows(tm, D_MODEL), _full((8, 128)), _full((1, D_MODEL)), _rows(tm, D_MODEL), _rows(tm, D_MODEL),
           pl.BlockSpec((3, tm, D_MODEL), lambda i: (0, i, 0))],
        scratch_shapes=[pltpu.VMEM((1536, D_MODEL), BF16), pltpu.VMEM((D_MODEL, D_MODEL), BF16),
                        pltpu.SemaphoreType.DMA((N_DEV + 1,))],
        compiler_params=_params(56),
    )(ya, yb, ym, pg, x, target, g_post, w_up, w_out)


def _gw_mid(mb, dob, ys, du):
    s = mb.shape[0]
    tn = 256

    def out_body(mb_ref, dob_ref, o_ref):
        o_ref[...] = _dot_tn(mb_ref[...], dob_ref[...]).astype(BF16)

    gw_out = pl.pallas_call(
        out_body, name="gw_out", grid=(D_MODEL // tn,),
        out_shape=jax.ShapeDtypeStruct((D_MODEL, D_MODEL), BF16),
        in_specs=[pl.BlockSpec((s, tn), lambda j: (0, j)), _full((s, D_MODEL))],
        out_specs=pl.BlockSpec((tn, D_MODEL), lambda j: (j, 0)),
        compiler_params=_params(48),
    )(mb, dob)

    per = 512 // tn

    def up_body(ya_ref, yb_ref, ym_ref, du_ref, o_ref):
        j = pl.program_id(0)
        for k, y_ref in enumerate((ya_ref, yb_ref, ym_ref)):
            @pl.when(j // per == k)
            def _(y_ref=y_ref):
                res = _dot_tn(y_ref[...], du_ref[...])
                for d in range(N_DEV):
                    o_ref[d] = res[:, 128 * d:128 * d + 128].astype(BF16)

    def y_spec(k):
        return pl.BlockSpec((s, tn), lambda j: (0, jnp.clip(j - per * k, 0, per - 1)))

    gw_up = pl.pallas_call(
        up_body, name="gw_up", grid=(3 * per,),
        out_shape=jax.ShapeDtypeStruct((N_DEV, 1536, 128), BF16),
        in_specs=[y_spec(0), y_spec(1), y_spec(2), pl.BlockSpec((None, s, D_MODEL), lambda j: (j // per, 0, 0))],
        out_specs=pl.BlockSpec((N_DEV, tn, 128), lambda j: (0, j, 0)),
        compiler_params=_params(48),
    )(*ys, du)
    return gw_out, gw_up


def _conv_bwd(pa, dya, w_conv):
    s = pa.shape[0]
    tm = min(512, s)
    nt = s // tm

    def body(pa_ref, pp_ref, pn_ref, d_ref, dp_ref, dn_ref, w_ref, da_ref, gw_ref):
        i = pl.program_id(0)
        first, last = i == 0, i == nt - 1

        @pl.when(first)
        def _():
            gw_ref[...] = jnp.zeros_like(gw_ref)

        w = w_ref[...]
        prev_row = pp_ref[...].astype(F32)[15:16, :]
        next_row = pn_ref[...].astype(F32)[0:1, :]
        b, c, u, z, cu, cu_m1, cu_p1, y, sig, row = _conv_common(
            pa_ref[...].astype(F32), prev_row, next_row, w, first, last, tm)
        sz = z * sig
        dya_t = d_ref[...]
        d_y = dya_t * b * sz

        def halo_dy(p_row, d_row):
            zz = p_row[:, 1536:2048]
            return d_row * p_row[:, 0:512] * (zz * _sigmoid(zz))

        dy_prev = jnp.where(first, 0.0, halo_dy(prev_row, dp_ref[7:8, :]))
        dy_next = jnp.where(last, 0.0, halo_dy(next_row, dn_ref[0:1, :]))
        dy_m1 = jnp.where(row == 0, dy_prev, pltpu.roll(d_y, 1, 0))
        dy_p1 = jnp.where(row == tm - 1, dy_next, pltpu.roll(d_y, tm - 1, 0))
        d_cu = dy_p1 * w[0:1] + d_y * w[1:2] + dy_m1 * w[2:3]
        da_ref[:, 0:512] = (dya_t * y * sz).astype(BF16)
        da_ref[:, 512:1024] = (d_cu * u).astype(BF16)
        da_ref[:, 1024:1536] = (d_cu * c).astype(BF16)
        da_ref[:, 1536:2048] = (dya_t * b * y * (sig * (1.0 + z * (1.0 - sig)))).astype(BF16)
        gw_ref[0:1, :] += jnp.sum(d_y * cu_m1, axis=0, keepdims=True)
        gw_ref[1:2, :] += jnp.sum(d_y * cu, axis=0, keepdims=True)
        gw_ref[2:3, :] += jnp.sum(d_y * cu_p1, axis=0, keepdims=True)

    prev, nxt = _halo_specs(s, tm, 16, 2048)
    dprev, dnxt = _halo_specs(s, tm, 8, 512)
    return pl.pallas_call(
        body, name="conv_bwd", grid=(nt,),
        out_shape=[jax.ShapeDtypeStruct((s, 2048), BF16), jax.ShapeDtypeStruct((8, 512), F32)],
        in_specs=[_rows(tm, 2048), prev, nxt, _rows(tm, 512), dprev, dnxt, _full((3, 512))],
        out_specs=[_rows(tm, 2048), _full((8, 512))],
        compiler_params=_params(48),
    )(pa, pa, pa, dya, dya, dya, w_conv)


def _attn_bwd(pq, pkv, pbz, dyb, sink, tabs):
    s = pq.shape[0]
    nb = s // ATTN_BLOCK

    def body(sink_ref, q_ref, z_ref, d_ref, cs_ref, s1_ref, s2_ref, kv_ref, csf_ref, s1f_ref, s2f_ref,
             dq_ref, dz_ref, dkv_ref, gs_ref, kpad, vpad, dk_acc, dv_acc, bias, e_scr, ds_scr):
        n = pl.program_id(0)

        @pl.when(n == 0)
        def _():
            _fill_padded(kv_ref, kpad, vpad, s)
            _fill_band_bias(bias, nb)
            dk_acc[...] = jnp.zeros_like(dk_acc)
            dv_acc[...] = jnp.zeros_like(dv_acc)
            gs_ref[...] = jnp.zeros_like(gs_ref)

        row = lax.broadcasted_iota(jnp.int32, (ATTN_BLOCK, 128), 0)
        start = pl.multiple_of(n * ATTN_BLOCK, ATTN_BLOCK)
        kw, vw = kpad[pl.ds(start, WINDOW_KEYS), :], vpad[pl.ds(start, WINDOW_KEYS), :]
        qf = q_ref[...].astype(F32)
        variant = _bias_variant(n, nb)
        z = z_ref[...].astype(F32)
        sig = _sigmoid(z)
        dyb_t = d_ref[...]
        d_attn = dyb_t * (z * sig)
        outs, dqs = [], []
        dk_w = jnp.zeros((WINDOW_KEYS, 128), F32)
        dv_w = jnp.zeros((WINDOW_KEYS, 128), F32)
        for g in range(2):
            qt = _heads_to_lanes(qf, g, row)
            inv, p_sink = _softmax_keys_major(
                _dot(kw, qt.astype(BF16)), bias, variant, _sink_row(sink_ref, g), e_scr)
            ot = _dot_tn(vw, e_scr[...]) * inv
            outs.append(ot)
            dot_ = _heads_to_lanes(d_attn, g, row)
            delta = jnp.sum(dot_ * ot, axis=0, keepdims=True)
            dpt = _dot(vw, dot_.astype(BF16))
            for k in range(WINDOW_KEYS // KEY_CHUNK):
                rw = slice(k * KEY_CHUNK, (k + 1) * KEY_CHUNK)
                ds_scr[rw, :] = (e_scr[rw, :].astype(F32) * (dpt[rw] - delta)).astype(BF16)
            sink_part = p_sink * delta
            for j in range(4):
                h = 4 * g + j
                gs_ref[h:h + 1, :] -= jnp.sum(sink_part[:, 128 * j:128 * j + 128])
            dqs.append(_dot_tn(kw, ds_scr[...]) * (inv * ATTN_SCALE))
            dk_w += _dot_nt(ds_scr[...], (qt * inv).astype(BF16)) * ATTN_SCALE
            dv_w += _dot_nt(e_scr[...], (dot_ * inv).astype(BF16))
        dk_acc[pl.ds(start, WINDOW_KEYS), :] += dk_w
        dv_acc[pl.ds(start, WINDOW_KEYS), :] += dv_w
        attn = _lanes_to_heads(outs[0], outs[1], row)
        dz_ref[...] = (dyb_t * attn * (sig * (1.0 + z * (1.0 - sig)))).astype(BF16)
        dq = _lanes_to_heads(dqs[0], dqs[1], row)
        cs, s1, s2 = cs_ref[...], s1_ref[...], s2_ref[...]
        for b in range(4):
            dq_ref[:, 128 * b:128 * b + 128] = _rope_t(dq[:, 128 * b:128 * b + 128], cs, s1, s2).astype(BF16)

        @pl.when(n == nb - 1)
        def _():
            dk = dk_acc[ATTN_BLOCK:ATTN_BLOCK + s, :]
            dkv_ref[:, 0:128] = _rope_t(dk, csf_ref[...], s1f_ref[...], s2f_ref[...]).astype(BF16)
            dkv_ref[:, 128:256] = dv_acc[ATTN_BLOCK:ATTN_BLOCK + s, :].astype(BF16)

    tile = _rows(ATTN_BLOCK, 512)
    tab = _rows(ATTN_BLOCK, 128)
    return pl.pallas_call(
        body, name="attn_bwd", grid=(nb,),
        out_shape=[jax.ShapeDtypeStruct((s, 512), BF16), jax.ShapeDtypeStruct((s, 512), BF16),
                   jax.ShapeDtypeStruct((s, 256), BF16), jax.ShapeDtypeStruct((8, 128), F32)],
        in_specs=[pl.BlockSpec(memory_space=pltpu.SMEM), tile, tile, tile, tab, tab, tab,
                  _full((s, 256)), _full((s, 128)), _full((s, 128)), _full((s, 128))],
        out_specs=[tile, tile, _full((s, 256)), _full((8, 128))],
        scratch_shapes=[pltpu.VMEM((s + 2 * ATTN_BLOCK, 128), BF16)] * 2
        + [pltpu.VMEM((s + 2 * ATTN_BLOCK, 128), F32)] * 2
        + [pltpu.VMEM((3, WINDOW_KEYS, STACKED), F32)] + [pltpu.VMEM((WINDOW_KEYS, STACKED), BF16)] * 2,
        compiler_params=_params(48),
    )(sink, pq, pbz, dyb, *tabs, pkv, *tabs)


def _mem_attn_bwd(pmq, pmz, mkv, dym):
    s = pmq.shape[0]
    m = mkv.shape[0]
    tm = min(512, s)

    def body(q_ref, z_ref, d_ref, mk_ref, mv_ref, dq_ref, dz_ref, dmkv_ref):
        @pl.when(pl.program_id(0) == 0)
        def _():
            dmkv_ref[...] = jnp.zeros_like(dmkv_ref)

        z = z_ref[...].astype(F32)
        sig = _sigmoid(z)
        dym_t = d_ref[...]
        d_attn = dym_t * (z * sig)
        dsilu = sig * (1.0 + z * (1.0 - sig))
        for h in range(MEM_HEADS):
            cols = slice(128 * h, 128 * h + 128)
            q, mk, mv = q_ref[:, cols], mk_ref[:, cols], mv_ref[:, cols]
            pt = _mem_softmax_t(q, mk)
            pb = pt.astype(BF16)
            o = _dot_tn(pb, mv)
            dob = d_attn[:, cols].astype(BF16)
            dpt = _dot_nt(mv, dob)
            dst = (pt * (dpt - jnp.sum(pt * dpt, axis=0, keepdims=True))).astype(BF16)
            dq_ref[:, cols] = (_dot_tn(dst, mk) * MEM_SCALE).astype(BF16)
            dz_ref[:, cols] = (dym_t[:, cols] * o * dsilu[:, cols]).astype(BF16)
            dmkv_ref[:, cols] += _dot(dst, q) * MEM_SCALE
            dmkv_ref[:, 512 + 128 * h:512 + 128 * h + 128] += _dot(pb, dob)

    return pl.pallas_call(
        body, name="mem_attn_bwd", grid=(s // tm,),
        out_shape=[jax.ShapeDtypeStruct((s, 512), BF16), jax.ShapeDtypeStruct((s, 512), BF16),
                   jax.ShapeDtypeStruct((m, D_MODEL), F32)],
        in_specs=[_rows(tm, 512), _rows(tm, 512), _rows(tm, 512), pl.BlockSpec((m, 512), lambda i: (0, 0)),
                  pl.BlockSpec((m, 512), lambda i: (0, 1))],
        out_specs=[_rows(tm, 512), _rows(tm, 512), _full((m, D_MODEL))],
        compiler_params=_params(32),
    )(pmq, pmz, dym, mkv, mkv)


def _mem_kv_bwd(mem, g_mem, mn, dmkv, w_mkv):
    m = mem.shape[0]

    def body(mem_ref, g_ref, mn_ref, d_ref, w_ref, gw_ref, gg_ref):
        db = d_ref[...].astype(BF16)
        gw_ref[...] = _dot_tn(mn_ref[...], db).astype(BF16)
        d_mn = _dot_nt(db, w_ref[...])
        xf = mem_ref[...]
        r = lax.rsqrt(jnp.mean(xf * xf, axis=-1, keepdims=True) + EPS)
        gg_ref[...] = jnp.sum(d_mn * (xf * r), axis=0, keepdims=True)

    return pl.pallas_call(
        body, name="mem_kv_bwd", grid=(1,),
        out_shape=[jax.ShapeDtypeStruct((D_MODEL, D_MODEL), BF16), jax.ShapeDtypeStruct((1, D_MODEL), F32)],
        in_specs=[_full((m, D_MODEL)), _full((1, D_MODEL)), _full((m, D_MODEL)), _full((m, D_MODEL)),
                  _full((D_MODEL, D_MODEL))],
        out_specs=[_full((D_MODEL, D_MODEL)), _full((1, D_MODEL))],
        compiler_params=_params(32),
    )(mem, g_mem, mn, dmkv, w_mkv)


def _dh_bwd(dparts, x, dy, g_pre, w_int):
    s = x.shape[0]
    tm = min(256, s)

    def body(*refs):
        d_refs = refs[:7]
        x_ref, dy_ref, g_ref, w_hbm, gx_ref, gg_ref, w_vm, sems = refs[7:]
        _load_once([(w_hbm, w_vm)], sems)

        @pl.when(pl.program_id(0) == 0)
        def _():
            gg_ref[...] = jnp.zeros_like(gg_ref)

        d_h = jnp.zeros((tm, D_MODEL), F32)
        for d_ref, (r0, width) in zip(d_refs, SEGS):
            for c0 in range(0, width, 512):
                cw = min(512, width - c0)
                d_h += _dot(d_ref[:, c0:c0 + cw], w_vm[r0 + c0:r0 + c0 + cw, :])
        xf = x_ref[...]
        r = lax.rsqrt(jnp.mean(xf * xf, axis=-1, keepdims=True) + EPS)
        xn = xf * r
        a = d_h * g_ref[...]
        gx_ref[...] = r * (a - xn * jnp.mean(a * xn, axis=-1, keepdims=True)) + dy_ref[...]
        gg_ref[...] += jnp.sum(d_h * xn, axis=0, keepdims=True)

    return pl.pallas_call(
        body, name="dh_bwd", grid=(s // tm,),
        out_shape=[jax.ShapeDtypeStruct((s, D_MODEL), F32), jax.ShapeDtypeStruct((1, D_MODEL), F32)],
        in_specs=[_rows(tm, w) for _, w in SEGS] + [_rows(tm, D_MODEL), _rows(tm, D_MODEL), _full((1, D_MODEL)), ANY],
        out_specs=[_rows(tm, D_MODEL), _full((1, D_MODEL))],
        scratch_shapes=[pltpu.VMEM((IN_WIDTH, D_MODEL), BF16), pltpu.SemaphoreType.DMA((1,))],
        compiler_params=_params(52),
    )(*dparts, x, dy, g_pre, w_int)


def _gw_in(dparts, h):
    s = h.shape[0]
    tn = 256
    starts, counts = [], []
    for r0, width in SEGS:
        starts.append(r0 // tn)
        counts.append(width // tn)

    def body(*refs):
        d_refs = refs[:7]
        h_hbm, o_ref, h_vm, sems = refs[7:]
        _load_once([(h_hbm, h_vm)], sems)
        j = pl.program_id(0)
        for d_ref, st, cnt in zip(d_refs, starts, counts):
            @pl.when((j >= st) & (j < st + cnt))
            def _(d_ref=d_ref):
                o_ref[...] = _dot_tn(d_ref[...], h_vm[...]).astype(BF16)

    def seg_spec(st, cnt):
        return pl.BlockSpec((s, tn), lambda j: (0, jnp.clip(j - st, 0, cnt - 1)))

    return pl.pallas_call(
        body, name="gw_in", grid=(IN_WIDTH // tn,),
        out_shape=jax.ShapeDtypeStruct((IN_WIDTH, D_MODEL), BF16),
        in_specs=[seg_spec(st, cnt) for st, cnt in zip(starts, counts)] + [ANY],
        out_specs=pl.BlockSpec((tn, D_MODEL), lambda j: (j, 0)),
        scratch_shapes=[pltpu.VMEM((s, D_MODEL), BF16), pltpu.SemaphoreType.DMA((1,))],
        compiler_params=_params(52),
    )(*dparts, h)


def _adamw_math(w, g, m, v):
    m2 = ADAM_B1 * m + (1.0 - ADAM_B1) * g
    v2 = ADAM_B2 * v + (1.0 - ADAM_B2) * (g * g)
    m_hat = m2 / (1.0 - ADAM_B1 ** ADAM_STEP)
    v_hat = v2 / (1.0 - ADAM_B2 ** ADAM_STEP)
    delta = -ADAM_LR * (m_hat / (jnp.sqrt(v_hat) + ADAM_EPS) + ADAM_WD * w)
    return delta, m2, v2


def _sum_adamw(own, land, chip, block, w, m, v, name, tiles=1):
    r, c = w.shape
    rt = r // tiles

    def body(c_ref, own_ref, l1_ref, l2_ref, l3_ref, w_ref, m_ref, v_ref, g_ref, d_ref, m2_ref, v2_ref):
        g = own_ref[...].astype(F32)
        for l_ref in (l1_ref, l2_ref, l3_ref):
            g += l_ref[...].astype(F32)
        g_ref[...] = g
        d_ref[...], m2_ref[...], v2_ref[...] = _adamw_math(w_ref[...], g, m_ref[...], v_ref[...])

    def share(k):
        return pl.BlockSpec((None, rt, c), lambda i, c_ref: (jnp.bitwise_xor(c_ref[0], k), block * tiles + i, 0))

    spec = pl.BlockSpec((rt, c), lambda i, c_ref: (i, 0))
    grid_spec = pltpu.PrefetchScalarGridSpec(
        num_scalar_prefetch=1, grid=(tiles,),
        in_specs=[share(0), share(1), share(2), share(3)] + [spec] * 3, out_specs=[spec] * 4)
    return pl.pallas_call(
        body, name=name, grid_spec=grid_spec,
        out_shape=[jax.ShapeDtypeStruct((r, c), F32)] * 4,
        compiler_params=_params(48),
    )(chip, own, land, land, land, w, m, v)


def _sum_adamw_group(items, chip, name):
    k = len(items)

    def body(c_ref, *refs):
        shares, wmv, outs = refs[:4 * k], refs[4 * k:7 * k], refs[7 * k:]
        for j in range(k):
            g = shares[4 * j][...].astype(F32)
            for l_ref in shares[4 * j + 1:4 * j + 4]:
                g += l_ref[...].astype(F32)
            outs[4 * j][...] = g
            outs[4 * j + 1][...], outs[4 * j + 2][...], outs[4 * j + 3][...] = _adamw_math(
                wmv[3 * j][...], g, wmv[3 * j + 1][...], wmv[3 * j + 2][...])

    def share(shape, block, q):
        return pl.BlockSpec((None,) + shape, lambda i, c_ref: (jnp.bitwise_xor(c_ref[0], q), block, 0))

    in_specs, args = [], []
    for own, land, block, w, m, v in items:
        in_specs += [share(w.shape, block, q) for q in range(4)]
        args += [own, land, land, land]
    for own, land, block, w, m, v in items:
        in_specs += [pl.BlockSpec(w.shape, lambda i, c_ref: (0, 0))] * 3
        args += [w, m, v]
    out_specs = [pl.BlockSpec(w.shape, lambda i, c_ref: (0, 0)) for _, _, _, w, _, _ in items for _ in range(4)]
    res = pl.pallas_call(
        body, name=name,
        grid_spec=pltpu.PrefetchScalarGridSpec(num_scalar_prefetch=1, grid=(1,), in_specs=in_specs,
                                               out_specs=out_specs),
        out_shape=[jax.ShapeDtypeStruct(w.shape, F32) for _, _, _, w, _, _ in items for _ in range(4)],
        compiler_params=_params(48),
    )(chip, *args)
    return [res[4 * j:4 * j + 4] for j in range(k)]


def _small_step(parts, ws, ms, vs):
    def exchange(gpre_ref, gconv_ref, gsink_ref, gmem_ref, gpost_ref, loss_ref, tot_ref,
                 pack, gathered, send_sems, recv_sems):
        x, y, c = _my_place()
        me_idx = 4 * x + 2 * y + c

        lane = lax.broadcasted_iota(jnp.int32, (1, 128), 1)
        sink_row = jnp.zeros((1, 128), F32)
        for h in range(8):
            sink_row = jnp.where(lane == h, gsink_ref[h:h + 1, :], sink_row)
        pack[...] = jnp.zeros_like(pack)
        pack[0:1, :] = gpre_ref[...]
        pack[1:2, :] = gmem_ref[...]
        pack[2:3, :] = gpost_ref[...]
        pack[3:6, 0:512] = gconv_ref[0:3, :]
        pack[6:7, 0:128] = sink_row
        pack[7:8, 0:128] = loss_ref[0:1, :]

        flips = [(0, 0, 1), (0, 1, 0), (1, 0, 0), (0, 1, 1), (1, 0, 1), (1, 1, 0), (1, 1, 1)]
        cps = []
        for k, (fx, fy, fc) in enumerate(flips):
            peer = ((1 - x) if fx else x, (1 - y) if fy else y, (1 - c) if fc else c)
            cps.append(pltpu.make_async_remote_copy(
                src_ref=pack, dst_ref=gathered.at[me_idx], send_sem=send_sems.at[k], recv_sem=recv_sems.at[k],
                device_id=peer, device_id_type=MESH))
        for cp in cps:
            cp.start()
        gathered[me_idx] = pack[...]
        for cp in cps:
            cp.wait_recv()
        for cp in cps:
            cp.wait_send()
        tot = gathered[0]
        for d in range(1, N_DEV):
            tot = tot + gathered[d]
        tot_ref[...] = tot

    tot = pl.pallas_call(
        exchange, name="small_exchange", grid=(1,),
        out_shape=jax.ShapeDtypeStruct((8, D_MODEL), F32),
        in_specs=[_full(p.shape) for p in parts], out_specs=_full((8, D_MODEL)),
        scratch_shapes=[pltpu.VMEM((8, D_MODEL), F32), pltpu.VMEM((N_DEV, 8, D_MODEL), F32),
                        pltpu.SemaphoreType.DMA((N_PEERS,)), pltpu.SemaphoreType.DMA((N_PEERS,))],
    )(*parts)

    def apply(tot_ref, *refs):
        w_refs, m_refs, v_refs = refs[0:5], refs[5:10], refs[10:15]
        loss_out = refs[15]
        g_outs, d_outs, m_outs, v_outs = refs[16:21], refs[21:26], refs[26:31], refs[31:36]
        x, y, c = _my_place()
        tot = tot_ref[...]
        conv = pltpu.roll(tot[:, 0:512], (512 - 64 * (4 * x + 2 * y + c)) % 512, 1)[3:6, 0:64]
        grads = (tot[0:1, :], conv, tot[6:7, 0:8], tot[1:2, :], tot[2:3, :])
        loss_out[...] = tot[7:8, 0:128]
        for j in range(5):
            g_outs[j][...] = grads[j]
            d_outs[j][...], m_outs[j][...], v_outs[j][...] = _adamw_math(
                w_refs[j][...], grads[j], m_refs[j][...], v_refs[j][...])

    specs = [_full(w.shape) for w in ws]
    res = pl.pallas_call(
        apply, name="small_apply", grid=(1,),
        out_shape=[jax.ShapeDtypeStruct((1, 128), F32)] + [jax.ShapeDtypeStruct(w.shape, F32) for w in ws] * 4,
        in_specs=[_full((8, D_MODEL))] + specs * 3,
        out_specs=[_full((1, 128))] + specs * 4,
    )(tot, *ws, *ms, *vs)
    return res[0], res[1:6], res[6:11], res[11:16], res[16:21]


def kernel(x, mem, g_pre, w_in, w_conv, attn_sink, g_mem, w_mem_kv, w_up_a, w_up_b, w_up_m, w_out, g_post, loss_target, m_g_pre, m_w_in, m_w_conv, m_attn_sink, m_g_mem, m_w_mem_kv, m_w_up_a, m_w_up_b, m_w_up_m, m_w_out, m_g_post, v_g_pre, v_w_in, v_w_conv, v_attn_sink, v_g_mem, v_w_mem_kv, v_w_up_a, v_w_up_b, v_w_up_m, v_w_out, v_g_post):
    s = x.shape[1]
    x2, mem2, tgt2 = x[0], mem[0], loss_target[0]
    me = 4 * lax.axis_index("x") + 2 * lax.axis_index("y") + lax.axis_index("c")

    w_up_loc = jnp.concatenate([w_up_a[0], w_up_b[0], w_up_m[0]], axis=0).astype(BF16)
    w_conv_loc = jnp.zeros((8, 128), F32).at[:3, :64].set(w_conv[0])
    w_int_g, w_conv_g = _all_gather([w_in[0].T.astype(BF16), w_conv_loc], "gather_w_in",
                                    splits=[[(0, 240), (240, 240), (480, 224), (704, 224)], [(0, 8)]])
    w_int = w_int_g.reshape(IN_WIDTH, D_MODEL)
    w_conv_f = w_conv_g[:, :3, :64].transpose(1, 0, 2).reshape(3, 512)
    late = _gather_start([w_mem_kv[0].astype(BF16) + w_conv_g[0, 7:8, 0:1].astype(BF16),
                          w_out[0].astype(BF16), w_up_loc], me, "gather_late_start")
    sink = attn_sink[0]
    tabs = _rope_tables(s)

    h, pa, pq, pkv, pbz, pmq, pmz, pg = _proj_fwd(x2, g_pre + late[4][0:1, 0:1], w_int, tabs)
    ya = _conv_fwd(pa, w_conv_f)
    yb = _attn_fwd(pq, pkv, pbz, sink)
    w_mkv_g, w_out_g, w_up_g = _gather_wait(*late[:4], yb, "gather_late_wait")
    w_mkv = w_mkv_g.reshape(D_MODEL, D_MODEL)
    w_out_f = w_out_g.reshape(D_MODEL, D_MODEL)
    mn, mkv = _mem_kv_fwd(mem2, g_mem, w_mkv)
    ym = _mem_attn_fwd(pmq, pmz, mkv)
    dg, dya, dyb, dym, dy, loss_p, gg_post, mb, dob, du = _mid(ya, yb, ym, pg, x2, tgt2, g_post, w_up_g, w_out_f)
    gw_out, gw_up = _gw_mid(mb, dob, (ya, yb, ym), du)

    core = lax.axis_index("c").astype(jnp.int32).reshape(1)
    chip = (2 * lax.axis_index("x") + lax.axis_index("y")).astype(jnp.int32).reshape(1)

    def exchange_start(shares, tag):
        from_sibling = _sibling_exchange(shares, "grads_to_sibling_" + tag)
        chip_shares = _pair_add(shares, from_sibling, core, "grads_pair_add_" + tag)
        return _chip_exchange_start(chip_shares, "grads_to_chips_start_" + tag)

    dmq, dmz, dmkv = _mem_attn_bwd(pmq, pmz, mkv, dym)
    gw_mkv, gg_mem = _mem_kv_bwd(mem2, g_mem, mn, dmkv, w_mkv)
    send1, recv1, srcs1, lands1, token1 = exchange_start(
        [gw_mkv.reshape(N_DEV, 128, D_MODEL), gw_out.reshape(N_DEV, 128, D_MODEL), gw_up], "small")
    da, gw_conv = _conv_bwd(pa, dya, w_conv_f + token1[0:1, 0:1])
    dq, dbz, dkv, g_sink = _attn_bwd(pq, pkv, pbz, dyb, sink, tabs)
    dparts = (da, dq, dkv, dbz, dmq, dmz, dg)
    gw_int = _gw_in(dparts, h)
    send2, recv2, srcs2, lands2, token2 = exchange_start([gw_int.reshape(N_DEV, SHARD_IN, D_MODEL)], "w_in")
    grad_x, gg_pre = _dh_bwd(dparts, x2, dy, g_pre + token2[0:1, 0:1], w_int)
    (o_mkv, o_out, o_up, o_int), (l_mkv, l_out, l_up, l_int) = _chip_exchange_wait(
        send1 + send2, recv1 + recv2, srcs1 + srcs2, lands1 + lands2, grad_x, "grads_to_chips_wait")

    loss_row, small_g, sd, sm, sv = _small_step(
        (gg_pre, gw_conv, g_sink, gg_mem, gg_post, loss_p),
        [g_pre, w_conv[0], attn_sink, g_mem, g_post],
        [m_g_pre, m_w_conv[0], m_attn_sink, m_g_mem, m_g_post],
        [v_g_pre, v_w_conv[0], v_attn_sink, v_g_mem, v_g_post])
    loss = loss_row[0, 0]
    g_g_pre, g_conv, g_sink_tot, g_g_mem, g_g_post = small_g

    g_w_in, d_w_in, nm_w_in, nv_w_in = (t.T for t in _sum_adamw(
        o_int, l_int, chip, 0, w_in[0].T, m_w_in[0].T, v_w_in[0].T, "adamw_w_in", tiles=2))
    (g_mkv, d_mkv, nm_mkv, nv_mkv), (g_out, d_out, nm_out, nv_out), *up = _sum_adamw_group(
        [(o_mkv, l_mkv, 0, w_mem_kv[0], m_w_mem_kv[0], v_w_mem_kv[0]),
         (o_out, l_out, 0, w_out[0], m_w_out[0], v_w_out[0]),
         (o_up, l_up, 0, w_up_a[0], m_w_up_a[0], v_w_up_a[0]),
         (o_up, l_up, 1, w_up_b[0], m_w_up_b[0], v_w_up_b[0]),
         (o_up, l_up, 2, w_up_m[0], m_w_up_m[0], v_w_up_m[0])], chip, "adamw_mid_weights")

    def lead(a):
        return a[None]

    grads = [g_g_pre, lead(g_w_in), lead(g_conv), g_sink_tot, g_g_mem, lead(g_mkv), lead(up[0][0]),
             lead(up[1][0]), lead(up[2][0]), lead(g_out), g_g_post]

    def assemble(small, big_in, big_mkv, big_up, big_out):
        return [small[0], lead(big_in), lead(small[1]), small[2], small[3], lead(big_mkv), lead(big_up[0]),
                lead(big_up[1]), lead(big_up[2]), lead(big_out), small[4]]

    deltas = assemble(sd, d_w_in, d_mkv, [u[1] for u in up], d_out)
    new_m = assemble(sm, nm_w_in, nm_mkv, [u[2] for u in up], nm_out)
    new_v = assemble(sv, nv_w_in, nv_mkv, [u[3] for u in up], nv_out)
    return (loss, grad_x[None], *grads, *deltas, *new_m, *new_v)
```

```python
import functools

import jax
import jax.numpy as jnp
from jax import lax
from jax.experimental import pallas as pl
from jax.experimental.pallas import tpu as pltpu

F32 = jnp.float32
BF16 = jnp.bfloat16
MESH = pl.DeviceIdType.MESH

N_DEV = 8
D_MODEL = 1024
EPS = 1e-6
ROPE_THETA = 500000.0
ROT_DIM = 16
HEAD_DIM = 64
ATTN_BLOCK = 128
MEM_HEADS = 4
MEM_HEAD_DIM = 128
ATTN_SCALE = HEAD_DIM ** -0.5
MEM_SCALE = MEM_HEAD_DIM ** -0.5

ADAM_LR = 0.001
ADAM_B1 = 0.9
ADAM_B2 = 0.999
ADAM_EPS = 1e-08
ADAM_WD = 0.01
ADAM_STEP = 10

SEG_A = (0, 2048)
SEG_BQ = (2048, 512)
SEG_BKV = (2560, 256)
SEG_BZ = (2816, 512)
SEG_MQ = (3328, 512)
SEG_MZ = (3840, 512)
SEG_G = (4352, 3072)
SEGS = (SEG_A, SEG_BQ, SEG_BKV, SEG_BZ, SEG_MQ, SEG_MZ, SEG_G)
IN_WIDTH = 7424
SHARD_IN = IN_WIDTH // N_DEV

V7X_VMEM_BYTES = 64 * 1024 * 1024
ANY = pl.BlockSpec(memory_space=pl.ANY)


def _params(vmem_mb):
    assert vmem_mb * 1024 * 1024 < V7X_VMEM_BYTES
    return pltpu.CompilerParams(dimension_semantics=("arbitrary",), vmem_limit_bytes=vmem_mb * 1024 * 1024)


def _full(shape):
    zeros = (0,) * len(shape)
    return pl.BlockSpec(shape, lambda i: zeros)


def _rows(tm, width):
    return pl.BlockSpec((tm, width), lambda i: (i, 0))


def _dot(a, b):
    return jnp.dot(a, b, preferred_element_type=F32)


def _dot_nt(a, b):
    return lax.dot_general(a, b, (((1,), (1,)), ((), ())), preferred_element_type=F32)


def _dot_tn(a, b):
    return lax.dot_general(a, b, (((0,), (0,)), ((), ())), preferred_element_type=F32)


def _sigmoid(z):
    return 1.0 / (1.0 + jnp.exp(-z))


def _rope(t, cs, s1, s2):
    return t * cs + pltpu.roll(t, 120, 1) * s1 + pltpu.roll(t, 8, 1) * s2


def _rope_t(d, cs, s1, s2):
    return d * cs + pltpu.roll(d * s1, 8, 1) + pltpu.roll(d * s2, 120, 1)


def _rope_tables(s):
    half = ROT_DIM // 2
    inv_freq = jnp.power(jnp.float32(ROPE_THETA), -jnp.arange(half, dtype=F32) * (2.0 / ROT_DIM))
    ang = jnp.arange(s).astype(F32)[:, None] * inv_freq[None, :]
    cos, sin = jnp.cos(ang), jnp.sin(ang)
    d = jnp.arange(128) % HEAD_DIM
    k = jnp.arange(half)[:, None]
    lo = (d[None, :] == k).astype(F32)
    hi = (d[None, :] == k + half).astype(F32)
    spread = functools.partial(jnp.dot, precision=lax.Precision.HIGHEST)
    return (spread(cos, lo + hi) + (d >= ROT_DIM).astype(F32)[None, :], -spread(sin, lo), spread(sin, hi))


def _load_once(pairs, sems):
    @pl.when(pl.program_id(0) == 0)
    def _():
        cps = [pltpu.make_async_copy(src, dst, sems.at[k]) for k, (src, dst) in enumerate(pairs)]
        for cp in cps:
            cp.start()
        for cp in cps:
            cp.wait()


def _my_place():
    x, y, c = lax.axis_index("x"), lax.axis_index("y"), lax.axis_index("c")
    return x, y, c


def _all_gather(arrs, name, splits=None):
    n = len(arrs)
    if splits is None:
        splits = [[(0, a.shape[0])] for a in arrs]
    pieces = [(a, r0, rn) for a in range(n) for r0, rn in splits[a]]
    n_p = len(pieces)

    def body(*refs):
        ins, outs = refs[:n], refs[n:2 * n]
        send_sems, recv_sems, local_sems = refs[2 * n:]
        x, y, c = _my_place()
        me, sibling = (x, y, c), (x, y, 1 - c)

        def route(core):
            first = (jnp.bitwise_xor(x, 1 - core), jnp.bitwise_xor(y, core), core)
            second = (jnp.bitwise_xor(x, core), jnp.bitwise_xor(y, 1 - core), core)
            return first, second, (1 - x, 1 - y, core)

        def idx(px, py, pc):
            return 4 * px + 2 * py + pc

        def copy(p, k, block, to, own=False):
            a, r0, rn = pieces[p]
            dst = outs[a].at[idx(*block), pl.ds(r0, rn)]
            return pltpu.make_async_remote_copy(
                src_ref=ins[a].at[pl.ds(r0, rn)] if own else dst, dst_ref=dst,
                send_sem=send_sems.at[p * 7 + k], recv_sem=recv_sems.at[p * 7 + k],
                device_id=to, device_id_type=MESH)

        nbr1, nbr2, diag = route(c)
        mine = [pltpu.make_async_copy(ins[a], outs[a].at[idx(*me)], local_sems.at[a]) for a in range(n)]
        for cp in mine:
            cp.start()
        sent = []
        for p in range(n_p):
            for k, to in enumerate((sibling, nbr1, nbr2)):
                sent.append(copy(p, k, me, to, own=True))
        for cp in sent:
            cp.start()
        for k_in, block, onward in ((1, nbr1, ((3, nbr2), (4, sibling))), (2, nbr2, ((5, sibling),)),
                                    (3, diag, ((6, sibling),))):
            for p in range(n_p):
                copy(p, k_in, block, me).wait_recv()
                for k_out, to in onward:
                    cp = copy(p, k_out, block, to)
                    cp.start()
                    sent.append(cp)
        s1, s2, sd = route(1 - c)
        for k_in, block in ((0, sibling), (4, s1), (5, s2), (6, sd)):
            for p in range(n_p):
                copy(p, k_in, block, me).wait_recv()
        for cp in sent:
            cp.wait_send()
        for cp in mine:
            cp.wait()

    return pl.pallas_call(
        body, name=name,
        out_shape=[jax.ShapeDtypeStruct((N_DEV,) + a.shape, a.dtype) for a in arrs],
        in_specs=[ANY] * n, out_specs=[ANY] * n,
        scratch_shapes=[pltpu.SemaphoreType.DMA((7 * n_p,)), pltpu.SemaphoreType.DMA((7 * n_p,)),
                        pltpu.SemaphoreType.DMA((n,))],
    )(*arrs)


N_CHIPS = 4


def _sibling_exchange(arrs, name):
    n = len(arrs)

    def body(*refs):
        ins, outs = refs[:n], refs[n:2 * n]
        send_sems, recv_sems = refs[2 * n:]
        x, y, c = _my_place()
        sibling = (x, y, 1 - c)

        def copy(a, j):
            return pltpu.make_async_remote_copy(
                src_ref=ins[a].at[2 * j + (1 - c)], dst_ref=outs[a].at[j],
                send_sem=send_sems.at[a * N_CHIPS + j], recv_sem=recv_sems.at[a * N_CHIPS + j],
                device_id=sibling, device_id_type=MESH)

        cps = [copy(a, j) for j in range(N_CHIPS) for a in range(n)]
        for cp in cps:
            cp.start()
        for cp in cps:
            cp.wait_recv()
        for cp in cps:
            cp.wait_send()

    return pl.pallas_call(
        body, name=name,
        out_shape=[jax.ShapeDtypeStruct((N_CHIPS,) + a.shape[1:], a.dtype) for a in arrs],
        in_specs=[ANY] * n, out_specs=[ANY] * n,
        scratch_shapes=[pltpu.SemaphoreType.DMA((N_CHIPS * n,)), pltpu.SemaphoreType.DMA((N_CHIPS * n,))],
    )(*arrs)


def _pair_add(mine, recv, core, name):
    n = len(mine)

    def body(c_ref, *refs):
        for a in range(n):
            refs[2 * n + a][...] = (refs[a][...].astype(F32) + refs[n + a][...].astype(F32)).astype(BF16)

    def blk(a):
        return (None,) + a.shape[1:]

    grid_spec = pltpu.PrefetchScalarGridSpec(
        num_scalar_prefetch=1, grid=(N_CHIPS,),
        in_specs=[pl.BlockSpec(blk(a), lambda j, c_ref: (2 * j + c_ref[0], 0, 0)) for a in mine]
        + [pl.BlockSpec(blk(a), lambda j, c_ref: (j, 0, 0)) for a in recv],
        out_specs=[pl.BlockSpec(blk(a), lambda j, c_ref: (j, 0, 0)) for a in recv])
    return pl.pallas_call(
        body, name=name, grid_spec=grid_spec,
        out_shape=[jax.ShapeDtypeStruct(a.shape, BF16) for a in recv],
        compiler_params=_params(32),
    )(core, *mine, *recv)


HBM = pl.BlockSpec(memory_space=pltpu.HBM)
SEM = pl.BlockSpec(memory_space=pltpu.SEMAPHORE)
N_PEER_CHIPS = 3


def _chip_copies(srcs, lands, send_sems, recv_sems):
    x, y, c = _my_place()
    my_chip = 2 * x + y
    peers = [(x, 1 - y), (1 - x, y), (1 - x, 1 - y)]
    cps = []
    for k, (px, py) in enumerate(peers):
        for a in range(len(srcs)):
            j = a * N_PEER_CHIPS + k
            cps.append(pltpu.make_async_remote_copy(
                src_ref=srcs[a].at[2 * px + py], dst_ref=lands[a].at[my_chip],
                send_sem=send_sems[j], recv_sem=recv_sems[j],
                device_id=(px, py, c), device_id_type=MESH))
    return cps


N_PEERS = N_DEV - 1


def _gather_copies(srcs, lands, send_sems, recv_sems):
    x, y, c = _my_place()
    me_idx = 4 * x + 2 * y + c
    flips = [(0, 0, 1), (0, 1, 0), (1, 0, 0), (0, 1, 1), (1, 0, 1), (1, 1, 0), (1, 1, 1)]
    cps = []
    for k, (fx, fy, fc) in enumerate(flips):
        peer = ((1 - x) if fx else x, (1 - y) if fy else y, (1 - c) if fc else c)
        for a in range(len(srcs)):
            j = a * N_PEERS + k
            cps.append(pltpu.make_async_remote_copy(
                src_ref=srcs[a], dst_ref=lands[a].at[me_idx], send_sem=send_sems[j], recv_sem=recv_sems[j],
                device_id=peer, device_id_type=MESH))
    return cps


def _split_start(copies, per_array, arrs, lands, name):
    arrs, lands = list(arrs), list(lands)
    n = len(arrs)
    k = n * per_array

    def body(*refs):
        srcs, land_refs = refs[:n], refs[n:2 * n]
        send_sems, recv_sems = refs[2 * n:2 * n + k], refs[2 * n + k:2 * n + 2 * k]
        token = refs[-1]
        for cp in copies(srcs, land_refs, send_sems, recv_sems):
            cp.start()
        token[...] = jnp.zeros_like(token)

    hbm_arrs = [pltpu.with_memory_space_constraint(a, pltpu.HBM) for a in arrs]
    lands = [pltpu.with_memory_space_constraint(a, pltpu.HBM) for a in lands]
    res = pl.pallas_call(
        body, name=name,
        out_shape=[pltpu.SemaphoreType.DMA(())] * (2 * k) + [pltpu.HBM(a.shape, a.dtype) for a in arrs + lands]
        + [jax.ShapeDtypeStruct((8, 128), F32)],
        in_specs=[HBM] * (2 * n),
        out_specs=[SEM] * (2 * k) + [HBM] * (2 * n) + [pl.BlockSpec(memory_space=pltpu.VMEM)],
        input_output_aliases={a: 2 * k + a for a in range(2 * n)},
        compiler_params=pltpu.CompilerParams(has_side_effects=pltpu.SideEffectType.DATAFLOW_SIDE_EFFECTING),
    )(*hbm_arrs, *lands)
    return res[:k], res[k:2 * k], res[2 * k:2 * k + n], res[2 * k + n:2 * k + 2 * n], res[-1]


def _split_wait(copies, per_array, send_sems, recv_sems, srcs, lands, after, name):
    n = len(srcs)
    k = n * per_array

    def body(*refs):
        src_refs, land_refs = refs[:n], refs[n:2 * n]
        s_sems, r_sems = refs[2 * n:2 * n + k], refs[2 * n + k:2 * n + 2 * k]
        for cp in copies(src_refs, land_refs, s_sems, r_sems):
            cp.wait_send()
            cp.wait_recv()

    res = pl.pallas_call(
        body, name=name,
        out_shape=[pltpu.HBM(a.shape, a.dtype) for a in list(srcs) + list(lands)],
        in_specs=[HBM] * (2 * n) + [SEM] * (2 * k) + [ANY],
        out_specs=[HBM] * (2 * n),
        input_output_aliases={a: a for a in range(2 * n)},
        compiler_params=pltpu.CompilerParams(has_side_effects=pltpu.SideEffectType.DATAFLOW_SIDE_EFFECTING),
    )(*srcs, *lands, *send_sems, *recv_sems, after)
    return res[:n], res[n:]


def _chip_exchange_start(arrs, name):
    return _split_start(_chip_copies, N_PEER_CHIPS, arrs, [lax.empty(a.shape, a.dtype) for a in arrs], name)


def _chip_exchange_wait(send_sems, recv_sems, srcs, lands, after, name):
    return _split_wait(_chip_copies, N_PEER_CHIPS, send_sems, recv_sems, srcs, lands, after, name)


def _gather_start(arrs, me_idx, name):
    lands = [lax.dynamic_update_slice(lax.empty((N_DEV,) + a.shape, a.dtype), a[None], (me_idx, 0, 0)) for a in arrs]
    return _split_start(_gather_copies, N_PEERS, arrs, lands, name)


def _gather_wait(send_sems, recv_sems, srcs, lands, after, name):
    return _split_wait(_gather_copies, N_PEERS, send_sems, recv_sems, srcs, lands, after, name)[1]


def _proj_fwd(x, g_pre, w_int, tabs):
    s = x.shape[0]
    tm = min(512, s)

    def body(x_ref, g_ref, cs_ref, s1_ref, s2_ref, w_hbm,
             h_ref, pa_ref, pq_ref, pkv_ref, pbz_ref, pmq_ref, pmz_ref, pg_ref, w_vm, sems):
        _load_once([(w_hbm, w_vm)], sems)
        xf = x_ref[...]
        r = lax.rsqrt(jnp.mean(xf * xf, axis=-1, keepdims=True) + EPS)
        h = ((xf * r) * g_ref[...]).astype(BF16)
        h_ref[...] = h
        cs, s1, s2 = cs_ref[...], s1_ref[...], s2_ref[...]

        def mm(seg, c0, width):
            return _dot_nt(h, w_vm[seg[0] + c0:seg[0] + c0 + width, :])

        for c0 in range(0, SEG_A[1], 512):
            pa_ref[:, c0:c0 + 512] = mm(SEG_A, c0, 512).astype(BF16)
        q = mm(SEG_BQ, 0, 512)
        for b in range(4):
            pq_ref[:, 128 * b:128 * b + 128] = _rope(q[:, 128 * b:128 * b + 128], cs, s1, s2).astype(BF16)
        kv = mm(SEG_BKV, 0, 256)
        pkv_ref[:, 0:128] = _rope(kv[:, 0:128], cs, s1, s2).astype(BF16)
        pkv_ref[:, 128:256] = kv[:, 128:256].astype(BF16)
        pbz_ref[...] = mm(SEG_BZ, 0, 512).astype(BF16)
        pmq_ref[...] = mm(SEG_MQ, 0, 512).astype(BF16)
        pmz_ref[...] = mm(SEG_MZ, 0, 512).astype(BF16)
        for c0 in range(0, SEG_G[1], 512):
            pg_ref[:, c0:c0 + 512] = mm(SEG_G, c0, 512).astype(BF16)

    widths = (D_MODEL, 2048, 512, 256, 512, 512, 512, 3072)
    return pl.pallas_call(
        body, name="proj_fwd", grid=(s // tm,),
        out_shape=[jax.ShapeDtypeStruct((s, w), BF16) for w in widths],
        in_specs=[_rows(tm, D_MODEL), _full((1, D_MODEL)), _rows(tm, 128), _rows(tm, 128), _rows(tm, 128), ANY],
        out_specs=[_rows(tm, w) for w in widths],
        scratch_shapes=[pltpu.VMEM((IN_WIDTH, D_MODEL), BF16), pltpu.SemaphoreType.DMA((1,))],
        compiler_params=_params(52),
    )(x, g_pre, *tabs, w_int)


def _mem_kv_fwd(mem, g_mem, w_mkv):
    m = mem.shape[0]

    def body(mem_ref, g_ref, w_ref, mn_ref, mkv_ref):
        xf = mem_ref[...]
        r = lax.rsqrt(jnp.mean(xf * xf, axis=-1, keepdims=True) + EPS)
        mn = ((xf * r) * g_ref[...]).astype(BF16)
        mn_ref[...] = mn
        mkv_ref[...] = _dot(mn, w_ref[...]).astype(BF16)

    return pl.pallas_call(
        body, name="mem_kv_fwd", grid=(1,),
        out_shape=[jax.ShapeDtypeStruct((m, D_MODEL), BF16)] * 2,
        in_specs=[_full((m, D_MODEL)), _full((1, D_MODEL)), _full((D_MODEL, D_MODEL))],
        out_specs=[_full((m, D_MODEL))] * 2,
        compiler_params=_params(32),
    )(mem, g_mem, w_mkv)


def _halo_specs(s, tm, rows, width):
    nblk = s // rows
    prev = pl.BlockSpec((rows, width), lambda i: (jnp.maximum(i * (tm // rows) - 1, 0), 0))
    nxt = pl.BlockSpec((rows, width), lambda i: (jnp.minimum((i + 1) * (tm // rows), nblk - 1), 0))
    return prev, nxt


def _conv_common(pa, prev_row, next_row, w, first, last, tm):
    b, c, u, z = (pa[:, 512 * k:512 * k + 512] for k in range(4))
    cu = c * u
    cu_prev = jnp.where(first, 0.0, prev_row[:, 512:1024] * prev_row[:, 1024:1536])
    cu_next = jnp.where(last, 0.0, next_row[:, 512:1024] * next_row[:, 1024:1536])
    row = lax.broadcasted_iota(jnp.int32, (tm, 512), 0)
    cu_m1 = jnp.where(row == 0, cu_prev, pltpu.roll(cu, 1, 0))
    cu_p1 = jnp.where(row == tm - 1, cu_next, pltpu.roll(cu, tm - 1, 0))
    y = cu_m1 * w[0:1] + cu * w[1:2] + cu_p1 * w[2:3]
    sig = _sigmoid(z)
    return b, c, u, z, cu, cu_m1, cu_p1, y, sig, row


def _conv_fwd(pa, w_conv):
    s = pa.shape[0]
    tm = min(512, s)
    nt = s // tm

    def body(pa_ref, pp_ref, pn_ref, w_ref, ya_ref):
        i = pl.program_id(0)
        prev_row = pp_ref[...].astype(F32)[15:16, :]
        next_row = pn_ref[...].astype(F32)[0:1, :]
        b, _, _, z, _, _, _, y, sig, _ = _conv_common(
            pa_ref[...].astype(F32), prev_row, next_row, w_ref[...], i == 0, i == nt - 1, tm)
        ya_ref[...] = (b * y * (z * sig)).astype(BF16)

    prev, nxt = _halo_specs(s, tm, 16, 2048)
    return pl.pallas_call(
        body, name="conv_fwd", grid=(nt,),
        out_shape=jax.ShapeDtypeStruct((s, 512), BF16),
        in_specs=[_rows(tm, 2048), prev, nxt, _full((3, 512))],
        out_specs=_rows(tm, 512),
        compiler_params=_params(48),
    )(pa, pa, pa, w_conv)


def _heads_to_lanes(a, g, row):
    low = row < HEAD_DIM
    parts = []
    for b in (2 * g, 2 * g + 1):
        t = jnp.transpose(a[:, 128 * b:128 * b + 128])
        swapped = pltpu.roll(t, HEAD_DIM, 0)
        if g == 0:
            parts += [jnp.where(low, t, 0.0), jnp.where(low, swapped, 0.0)]
        else:
            parts += [jnp.where(low, 0.0, swapped), jnp.where(low, 0.0, t)]
    return jnp.concatenate(parts, axis=1)


def _lanes_to_heads(t0, t1, row):
    low = row < HEAD_DIM
    blocks = []
    for b in range(4):
        g = b // 2
        tg = (t0, t1)[g]
        je = 2 * (b - 2 * g)
        even, odd = tg[:, 128 * je:128 * je + 128], tg[:, 128 * je + 128:128 * je + 256]
        if g == 0:
            t = jnp.where(low, even, pltpu.roll(odd, HEAD_DIM, 0))
        else:
            t = jnp.where(low, pltpu.roll(even, HEAD_DIM, 0), odd)
        blocks.append(jnp.transpose(t))
    return jnp.concatenate(blocks, axis=1)


WINDOW_KEYS = 3 * ATTN_BLOCK
STACKED = 4 * ATTN_BLOCK
KEY_CHUNK = 32
BLOCKS_PER_STEP = 2


def _fill_band_bias(bias, nb):
    assert nb >= 2
    c = lax.broadcasted_iota(jnp.int32, (WINDOW_KEYS, STACKED), 0)
    r = lax.broadcasted_iota(jnp.int32, (WINDOW_KEYS, STACKED), 1) & (ATTN_BLOCK - 1)
    band = (c >= r) & (c <= r + 2 * ATTN_BLOCK)
    for v, ok in enumerate((band, band & (c >= ATTN_BLOCK), band & (c < 2 * ATTN_BLOCK))):
        bias[v] = jnp.where(ok, 0.0, -jnp.inf)


def _bias_variant(n, nb):
    return jnp.where(n == 0, 1, jnp.where(n == nb - 1, 2, 0))


def _sink_row(sink_ref, g):
    return jnp.concatenate([jnp.full((1, ATTN_BLOCK), sink_ref[4 * g + j], F32) for j in range(4)], axis=1)


def _softmax_keys_major(sc, bias, variant, sink, e_scr):
    chunks = [pl.ds(k * KEY_CHUNK, KEY_CHUNK) for k in range(WINDOW_KEYS // KEY_CHUNK)]
    rows = [slice(k * KEY_CHUNK, (k + 1) * KEY_CHUNK) for k in range(WINDOW_KEYS // KEY_CHUNK)]
    m_run = jnp.full((KEY_CHUNK, STACKED), -jnp.inf, F32)
    for ck, rw in zip(chunks, rows):
        m_run = jnp.maximum(m_run, sc[rw] * ATTN_SCALE + bias[variant, ck, :])
    m = jnp.maximum(jnp.max(m_run, axis=0, keepdims=True), sink)
    l_run = jnp.zeros((KEY_CHUNK, STACKED), F32)
    for ck, rw in zip(chunks, rows):
        e = jnp.exp(sc[rw] * ATTN_SCALE + bias[variant, ck, :] - m)
        l_run += e
        e_scr[rw, :] = e.astype(BF16)
    es = jnp.exp(sink - m)
    inv = 1.0 / (jnp.sum(l_run, axis=0, keepdims=True) + es)
    return inv, es * inv


def _fill_padded(kv_ref, kpad, vpad, s):
    zero = jnp.zeros((ATTN_BLOCK, 128), BF16)
    kpad[0:ATTN_BLOCK, :] = zero
    vpad[0:ATTN_BLOCK, :] = zero
    kpad[ATTN_BLOCK + s:2 * ATTN_BLOCK + s, :] = zero
    vpad[ATTN_BLOCK + s:2 * ATTN_BLOCK + s, :] = zero
    kpad[ATTN_BLOCK:ATTN_BLOCK + s, :] = kv_ref[:, 0:128]
    vpad[ATTN_BLOCK:ATTN_BLOCK + s, :] = kv_ref[:, 128:256]


def _attn_fwd(pq, pkv, pbz, sink):
    s = pq.shape[0]
    nb = s // ATTN_BLOCK

    def body(sink_ref, q_ref, z_ref, kv_ref, yb_ref, kpad, vpad, bias, e_scr):
        i = pl.program_id(0)

        @pl.when(i == 0)
        def _():
            _fill_padded(kv_ref, kpad, vpad, s)
            _fill_band_bias(bias, nb)

        row = lax.broadcasted_iota(jnp.int32, (ATTN_BLOCK, 128), 0)
        for b in range(BLOCKS_PER_STEP):
            n = i * BLOCKS_PER_STEP + b
            rows = slice(b * ATTN_BLOCK, (b + 1) * ATTN_BLOCK)
            start = pl.multiple_of(n * ATTN_BLOCK, ATTN_BLOCK)
            kw, vw = kpad[pl.ds(start, WINDOW_KEYS), :], vpad[pl.ds(start, WINDOW_KEYS), :]
            qf = q_ref[rows, :].astype(F32)
            variant = _bias_variant(n, nb)
            outs = []
            for g in range(2):
                e_bg = e_scr.at[2 * b + g]
                qt = _heads_to_lanes(qf, g, row).astype(BF16)
                inv, _ = _softmax_keys_major(_dot(kw, qt), bias, variant, _sink_row(sink_ref, g), e_bg)
                outs.append(_dot_tn(vw, e_bg[...]) * inv)
            attn = _lanes_to_heads(outs[0], outs[1], row)
            z = z_ref[rows, :].astype(F32)
            yb_ref[rows, :] = (attn * (z * _sigmoid(z))).astype(BF16)

    tq = BLOCKS_PER_STEP * ATTN_BLOCK
    return pl.pallas_call(
        body, name="attn_fwd", grid=(s // tq,),
        out_shape=jax.ShapeDtypeStruct((s, 512), BF16),
        in_specs=[pl.BlockSpec(memory_space=pltpu.SMEM), _rows(tq, 512), _rows(tq, 512), _full((s, 256))],
        out_specs=_rows(tq, 512),
        scratch_shapes=[pltpu.VMEM((s + 2 * ATTN_BLOCK, 128), BF16)] * 2
        + [pltpu.VMEM((3, WINDOW_KEYS, STACKED), F32),
           pltpu.VMEM((2 * BLOCKS_PER_STEP, WINDOW_KEYS, STACKED), BF16)],
        compiler_params=_params(32),
    )(sink, pq, pbz, pkv)


def _mem_softmax_t(q, mk):
    sc = _dot_nt(mk, q) * MEM_SCALE
    e = jnp.exp(sc - jnp.max(sc, axis=0, keepdims=True))
    return e * (1.0 / jnp.sum(e, axis=0, keepdims=True))


def _mem_attn_fwd(pmq, pmz, mkv):
    s = pmq.shape[0]
    m = mkv.shape[0]
    tm = min(512, s)

    def body(q_ref, z_ref, mk_ref, mv_ref, ym_ref):
        z = z_ref[...].astype(F32)
        sz = z * _sigmoid(z)
        for h in range(MEM_HEADS):
            cols = slice(128 * h, 128 * h + 128)
            pt = _mem_softmax_t(q_ref[:, cols], mk_ref[:, cols])
            o = _dot_tn(pt.astype(BF16), mv_ref[:, cols])
            ym_ref[:, cols] = (o * sz[:, cols]).astype(BF16)

    return pl.pallas_call(
        body, name="mem_attn_fwd", grid=(s // tm,),
        out_shape=jax.ShapeDtypeStruct((s, 512), BF16),
        in_specs=[_rows(tm, 512), _rows(tm, 512), pl.BlockSpec((m, 512), lambda i: (0, 0)),
                  pl.BlockSpec((m, 512), lambda i: (0, 1))],
        out_specs=_rows(tm, 512),
        compiler_params=_params(32),
    )(pmq, pmz, mkv, mkv)


def _mid(ya, yb, ym, pg, x, target, g_post, w_up, w_out):
    s = x.shape[0]
    tm = min(256, s)
    nt = s // tm

    def body(ya_ref, yb_ref, ym_ref, pg_ref, x_ref, t_ref, gp_ref, wup_hbm, wout_hbm,
             dg_ref, dya_ref, dyb_ref, dym_ref, dy_ref, loss_ref, ggp_ref, mb_ref, dob_ref, du_ref,
             wup_vm, wout_vm, sems):
        i = pl.program_id(0)
        _load_once([(wup_hbm.at[d], wup_vm.at[:, pl.ds(128 * d, 128)]) for d in range(N_DEV)]
                   + [(wout_hbm, wout_vm)], sems)

        @pl.when(i == 0)
        def _():
            loss_ref[...] = jnp.zeros_like(loss_ref)
            ggp_ref[...] = jnp.zeros_like(ggp_ref)

        ys = (ya_ref[...], yb_ref[...], ym_ref[...])
        us = [_dot(ys[k], wup_vm[512 * k:512 * k + 512, :]) for k in range(3)]
        gates = [_sigmoid(pg_ref[:, 1024 * k:1024 * k + 1024].astype(F32)) for k in range(3)]
        merged = gates[0] * us[0] + gates[1] * us[1] + gates[2] * us[2]
        mb = merged.astype(BF16)
        mb_ref[...] = mb
        out = _dot(mb, wout_vm[...])
        r = lax.rsqrt(jnp.mean(out * out, axis=-1, keepdims=True) + EPS)
        on = out * r
        gp = gp_ref[...]
        err = (x_ref[...] + on * gp) - t_ref[...]
        loss_ref[...] += 0.5 * jnp.sum(err * err) * (1.0 / D_MODEL)
        dy = err * (1.0 / D_MODEL)
        dy_ref[...] = dy
        ggp_ref[...] += jnp.sum(dy * on, axis=0, keepdims=True)
        a = dy * gp
        d_out = r * (a - on * jnp.mean(a * on, axis=-1, keepdims=True))
        dob = d_out.astype(BF16)
        dob_ref[...] = dob
        d_merged = _dot_nt(dob, wout_vm[...])
        d_refs = (dya_ref, dyb_ref, dym_ref)
        for k in range(3):
            g = gates[k]
            dg_ref[:, 1024 * k:1024 * k + 1024] = (d_merged * us[k] * g * (1.0 - g)).astype(BF16)
            du = (d_merged * g).astype(BF16)
            du_ref[k] = du
            d_refs[k][...] = _dot_nt(du, wup_vm[512 * k:512 * k + 512, :]).astype(BF16)

    return pl.pallas_call(
        body, name="mid", grid=(nt,),
        out_shape=[jax.ShapeDtypeStruct((s, 3072), BF16)] + [jax.ShapeDtypeStruct((s, 512), BF16)] * 3
        + [jax.ShapeDtypeStruct((s, D_MODEL), F32), jax.ShapeDtypeStruct((8, 128), F32),
           jax.ShapeDtypeStruct((1, D_MODEL), F32), jax.ShapeDtypeStruct((s, D_MODEL), BF16),
           jax.ShapeDtypeStruct((s, D_MODEL), BF16), jax.ShapeDtypeStruct((3, s, D_MODEL), BF16)],
        in_specs=[_rows(tm, 512)] * 3 + [_rows(tm, 3072), _rows(tm, D_MODEL), _rows(tm, D_MODEL),
                                         _full((1, D_MODEL)), ANY, ANY],
        out_specs=[_rows(tm, 3072)] + [_rows(tm, 512)] * 3
        + [_rows(tm, D_MODEL), _full((8, 128)), _full((1, D_MODEL)), _rows(tm, D_MODEL), _rows(tm, D_MODEL),
           pl.BlockSpec((3, tm, D_MODEL), lambda i: (0, i, 0))],
        scratch_shapes=[pltpu.VMEM((1536, D_MODEL), BF16), pltpu.VMEM((D_MODEL, D_MODEL), BF16),
                        pltpu.SemaphoreType.DMA((N_DEV + 1,))],
        compiler_params=_params(56),
    )(ya, yb, ym, pg, x, target, g_post, w_up, w_out)


def _gw_mid(mb, dob, ys, du):
    s = mb.shape[0]
    tn = 256

    def out_body(mb_ref, dob_ref, o_ref):
        o_ref[...] = _dot_tn(mb_ref[...], dob_ref[...]).astype(BF16)

    gw_out = pl.pallas_call(
        out_body, name="gw_out", grid=(D_MODEL // tn,),
        out_shape=jax.ShapeDtypeStruct((D_MODEL, D_MODEL), BF16),
        in_specs=[pl.BlockSpec((s, tn), lambda j: (0, j)), _full((s, D_MODEL))],
        out_specs=pl.BlockSpec((tn, D_MODEL), lambda j: (j, 0)),
        compiler_params=_params(48),
    )(mb, dob)

    per = 512 // tn

    def up_body(ya_ref, yb_ref, ym_ref, du_ref, o_ref):
        j = pl.program_id(0)
        for k, y_ref in enumerate((ya_ref, yb_ref, ym_ref)):
            @pl.when(j // per == k)
            def _(y_ref=y_ref):
                res = _dot_tn(y_ref[...], du_ref[...])
                for d in range(N_DEV):
                    o_ref[d] = res[:, 128 * d:128 * d + 128].astype(BF16)

    def y_spec(k):
        return pl.BlockSpec((s, tn), lambda j: (0, jnp.clip(j - per * k, 0, per - 1)))

    gw_up = pl.pallas_call(
        up_body, name="gw_up", grid=(3 * per,),
        out_shape=jax.ShapeDtypeStruct((N_DEV, 1536, 128), BF16),
        in_specs=[y_spec(0), y_spec(1), y_spec(2), pl.BlockSpec((None, s, D_MODEL), lambda j: (j // per, 0, 0))],
        out_specs=pl.BlockSpec((N_DEV, tn, 128), lambda j: (0, j, 0)),
        compiler_params=_params(48),
    )(*ys, du)
    return gw_out, gw_up


def _conv_bwd(pa, dya, w_conv):
    s = pa.shape[0]
    tm = min(512, s)
    nt = s // tm

    def body(pa_ref, pp_ref, pn_ref, d_ref, dp_ref, dn_ref, w_ref, da_ref, gw_ref):
        i = pl.program_id(0)
        first, last = i == 0, i == nt - 1

        @pl.when(first)
        def _():
            gw_ref[...] = jnp.zeros_like(gw_ref)

        w = w_ref[...]
        prev_row = pp_ref[...].astype(F32)[15:16, :]
        next_row = pn_ref[...].astype(F32)[0:1, :]
        b, c, u, z, cu, cu_m1, cu_p1, y, sig, row = _conv_common(
            pa_ref[...].astype(F32), prev_row, next_row, w, first, last, tm)
        sz = z * sig
        dya_t = d_ref[...].astype(F32)
        d_y = dya_t * b * sz

        def halo_dy(p_row, d_row):
            zz = p_row[:, 1536:2048]
            return d_row * p_row[:, 0:512] * (zz * _sigmoid(zz))

        dy_prev = jnp.where(first, 0.0, halo_dy(prev_row, dp_ref[...].astype(F32)[15:16, :]))
        dy_next = jnp.where(last, 0.0, halo_dy(next_row, dn_ref[...].astype(F32)[0:1, :]))
        dy_m1 = jnp.where(row == 0, dy_prev, pltpu.roll(d_y, 1, 0))
        dy_p1 = jnp.where(row == tm - 1, dy_next, pltpu.roll(d_y, tm - 1, 0))
        d_cu = dy_p1 * w[0:1] + d_y * w[1:2] + dy_m1 * w[2:3]
        da_ref[:, 0:512] = (dya_t * y * sz).astype(BF16)
        da_ref[:, 512:1024] = (d_cu * u).astype(BF16)
        da_ref[:, 1024:1536] = (d_cu * c).astype(BF16)
        da_ref[:, 1536:2048] = (dya_t * b * y * (sig * (1.0 + z * (1.0 - sig)))).astype(BF16)
        gw_ref[0:1, :] += jnp.sum(d_y * cu_m1, axis=0, keepdims=True)
        gw_ref[1:2, :] += jnp.sum(d_y * cu, axis=0, keepdims=True)
        gw_ref[2:3, :] += jnp.sum(d_y * cu_p1, axis=0, keepdims=True)

    prev, nxt = _halo_specs(s, tm, 16, 2048)
    dprev, dnxt = _halo_specs(s, tm, 16, 512)
    return pl.pallas_call(
        body, name="conv_bwd", grid=(nt,),
        out_shape=[jax.ShapeDtypeStruct((s, 2048), BF16), jax.ShapeDtypeStruct((8, 512), F32)],
        in_specs=[_rows(tm, 2048), prev, nxt, _rows(tm, 512), dprev, dnxt, _full((3, 512))],
        out_specs=[_rows(tm, 2048), _full((8, 512))],
        compiler_params=_params(48),
    )(pa, pa, pa, dya, dya, dya, w_conv)


def _attn_bwd(pq, pkv, pbz, dyb, sink, tabs):
    s = pq.shape[0]
    nb = s // ATTN_BLOCK

    def body(sink_ref, q_ref, z_ref, d_ref, cs_ref, s1_ref, s2_ref, kv_ref, csf_ref, s1f_ref, s2f_ref,
             dq_ref, dz_ref, dkv_ref, gs_ref, kpad, vpad, dk_acc, dv_acc, bias, e_scr, ds_scr):
        i = pl.program_id(0)

        @pl.when(i == 0)
        def _():
            _fill_padded(kv_ref, kpad, vpad, s)
            _fill_band_bias(bias, nb)
            dk_acc[...] = jnp.zeros_like(dk_acc)
            dv_acc[...] = jnp.zeros_like(dv_acc)
            gs_ref[...] = jnp.zeros_like(gs_ref)

        row = lax.broadcasted_iota(jnp.int32, (ATTN_BLOCK, 128), 0)
        for b in range(BLOCKS_PER_STEP):
            n = i * BLOCKS_PER_STEP + b
            rows = slice(b * ATTN_BLOCK, (b + 1) * ATTN_BLOCK)
            start = pl.multiple_of(n * ATTN_BLOCK, ATTN_BLOCK)
            kw, vw = kpad[pl.ds(start, WINDOW_KEYS), :], vpad[pl.ds(start, WINDOW_KEYS), :]
            qf = q_ref[rows, :].astype(F32)
            variant = _bias_variant(n, nb)
            z = z_ref[rows, :].astype(F32)
            sig = _sigmoid(z)
            dyb_t = d_ref[rows, :].astype(F32)
            d_attn = dyb_t * (z * sig)
            outs, dqs = [], []
            dk_w = jnp.zeros((WINDOW_KEYS, 128), F32)
            dv_w = jnp.zeros((WINDOW_KEYS, 128), F32)
            for g in range(2):
                e_bg, ds_bg = e_scr.at[2 * b + g], ds_scr.at[2 * b + g]
                qt = _heads_to_lanes(qf, g, row)
                inv, p_sink = _softmax_keys_major(
                    _dot(kw, qt.astype(BF16)), bias, variant, _sink_row(sink_ref, g), e_bg)
                ot = _dot_tn(vw, e_bg[...]) * inv
                outs.append(ot)
                dot_ = _heads_to_lanes(d_attn, g, row)
                delta = jnp.sum(dot_ * ot, axis=0, keepdims=True)
                dpt = _dot(vw, dot_.astype(BF16))
                for k in range(WINDOW_KEYS // KEY_CHUNK):
                    rw = slice(k * KEY_CHUNK, (k + 1) * KEY_CHUNK)
                    ds_bg[rw, :] = (e_bg[rw, :].astype(F32) * (dpt[rw] - delta)).astype(BF16)
                sink_part = p_sink * delta
                for j in range(4):
                    h = 4 * g + j
                    gs_ref[h:h + 1, :] -= jnp.sum(sink_part[:, 128 * j:128 * j + 128])
                dqs.append(_dot_tn(kw, ds_bg[...]) * (inv * ATTN_SCALE))
                dk_w += _dot_nt(ds_bg[...], (qt * inv).astype(BF16)) * ATTN_SCALE
                dv_w += _dot_nt(e_bg[...], (dot_ * inv).astype(BF16))
            dk_acc[pl.ds(start, WINDOW_KEYS), :] += dk_w
            dv_acc[pl.ds(start, WINDOW_KEYS), :] += dv_w
            attn = _lanes_to_heads(outs[0], outs[1], row)
            dz_ref[rows, :] = (dyb_t * attn * (sig * (1.0 + z * (1.0 - sig)))).astype(BF16)
            dq = _lanes_to_heads(dqs[0], dqs[1], row)
            cs, s1, s2 = cs_ref[rows, :], s1_ref[rows, :], s2_ref[rows, :]
            for blk in range(4):
                cols = slice(128 * blk, 128 * blk + 128)
                dq_ref[rows, cols] = _rope_t(dq[:, cols], cs, s1, s2).astype(BF16)

        @pl.when(i == nb // BLOCKS_PER_STEP - 1)
        def _():
            dk = dk_acc[ATTN_BLOCK:ATTN_BLOCK + s, :]
            dkv_ref[:, 0:128] = _rope_t(dk, csf_ref[...], s1f_ref[...], s2f_ref[...]).astype(BF16)
            dkv_ref[:, 128:256] = dv_acc[ATTN_BLOCK:ATTN_BLOCK + s, :].astype(BF16)

    tq = BLOCKS_PER_STEP * ATTN_BLOCK
    tile = _rows(tq, 512)
    tab = _rows(tq, 128)
    return pl.pallas_call(
        body, name="attn_bwd", grid=(s // tq,),
        out_shape=[jax.ShapeDtypeStruct((s, 512), BF16), jax.ShapeDtypeStruct((s, 512), BF16),
                   jax.ShapeDtypeStruct((s, 256), BF16), jax.ShapeDtypeStruct((8, 128), F32)],
        in_specs=[pl.BlockSpec(memory_space=pltpu.SMEM), tile, tile, tile, tab, tab, tab,
                  _full((s, 256)), _full((s, 128)), _full((s, 128)), _full((s, 128))],
        out_specs=[tile, tile, _full((s, 256)), _full((8, 128))],
        scratch_shapes=[pltpu.VMEM((s + 2 * ATTN_BLOCK, 128), BF16)] * 2
        + [pltpu.VMEM((s + 2 * ATTN_BLOCK, 128), F32)] * 2
        + [pltpu.VMEM((3, WINDOW_KEYS, STACKED), F32)]
        + [pltpu.VMEM((2 * BLOCKS_PER_STEP, WINDOW_KEYS, STACKED), BF16)] * 2,
        compiler_params=_params(48),
    )(sink, pq, pbz, dyb, *tabs, pkv, *tabs)


def _mem_attn_bwd(pmq, pmz, mkv, dym):
    s = pmq.shape[0]
    m = mkv.shape[0]
    tm = min(512, s)

    def body(q_ref, z_ref, d_ref, mk_ref, mv_ref, dq_ref, dz_ref, dmkv_ref):
        @pl.when(pl.program_id(0) == 0)
        def _():
            dmkv_ref[...] = jnp.zeros_like(dmkv_ref)

        z = z_ref[...].astype(F32)
        sig = _sigmoid(z)
        dym_t = d_ref[...].astype(F32)
        d_attn = dym_t * (z * sig)
        dsilu = sig * (1.0 + z * (1.0 - sig))
        for h in range(MEM_HEADS):
            cols = slice(128 * h, 128 * h + 128)
            q, mk, mv = q_ref[:, cols], mk_ref[:, cols], mv_ref[:, cols]
            pt = _mem_softmax_t(q, mk)
            pb = pt.astype(BF16)
            o = _dot_tn(pb, mv)
            dob = d_attn[:, cols].astype(BF16)
            dpt = _dot_nt(mv, dob)
            dst = (pt * (dpt - jnp.sum(pt * dpt, axis=0, keepdims=True))).astype(BF16)
            dq_ref[:, cols] = (_dot_tn(dst, mk) * MEM_SCALE).astype(BF16)
            dz_ref[:, cols] = (dym_t[:, cols] * o * dsilu[:, cols]).astype(BF16)
            dmkv_ref[:, cols] += _dot(dst, q) * MEM_SCALE
            dmkv_ref[:, 512 + 128 * h:512 + 128 * h + 128] += _dot(pb, dob)

    return pl.pallas_call(
        body, name="mem_attn_bwd", grid=(s // tm,),
        out_shape=[jax.ShapeDtypeStruct((s, 512), BF16), jax.ShapeDtypeStruct((s, 512), BF16),
                   jax.ShapeDtypeStruct((m, D_MODEL), F32)],
        in_specs=[_rows(tm, 512), _rows(tm, 512), _rows(tm, 512), pl.BlockSpec((m, 512), lambda i: (0, 0)),
                  pl.BlockSpec((m, 512), lambda i: (0, 1))],
        out_specs=[_rows(tm, 512), _rows(tm, 512), _full((m, D_MODEL))],
        compiler_params=_params(32),
    )(pmq, pmz, dym, mkv, mkv)


def _mem_kv_bwd(mem, g_mem, mn, dmkv, w_mkv):
    m = mem.shape[0]

    def body(mem_ref, g_ref, mn_ref, d_ref, w_ref, gw_ref, gg_ref):
        db = d_ref[...].astype(BF16)
        gw_ref[...] = _dot_tn(mn_ref[...], db).astype(BF16)
        d_mn = _dot_nt(db, w_ref[...])
        xf = mem_ref[...]
        r = lax.rsqrt(jnp.mean(xf * xf, axis=-1, keepdims=True) + EPS)
        gg_ref[...] = jnp.sum(d_mn * (xf * r), axis=0, keepdims=True)

    return pl.pallas_call(
        body, name="mem_kv_bwd", grid=(1,),
        out_shape=[jax.ShapeDtypeStruct((D_MODEL, D_MODEL), BF16), jax.ShapeDtypeStruct((1, D_MODEL), F32)],
        in_specs=[_full((m, D_MODEL)), _full((1, D_MODEL)), _full((m, D_MODEL)), _full((m, D_MODEL)),
                  _full((D_MODEL, D_MODEL))],
        out_specs=[_full((D_MODEL, D_MODEL)), _full((1, D_MODEL))],
        compiler_params=_params(32),
    )(mem, g_mem, mn, dmkv, w_mkv)


def _dh_bwd(dparts, x, dy, g_pre, w_int):
    s = x.shape[0]
    tm = min(256, s)

    def body(*refs):
        d_refs = refs[:7]
        x_ref, dy_ref, g_ref, w_hbm, gx_ref, gg_ref, w_vm, sems = refs[7:]
        _load_once([(w_hbm, w_vm)], sems)

        @pl.when(pl.program_id(0) == 0)
        def _():
            gg_ref[...] = jnp.zeros_like(gg_ref)

        d_h = jnp.zeros((tm, D_MODEL), F32)
        for d_ref, (r0, width) in zip(d_refs, SEGS):
            for c0 in range(0, width, 512):
                cw = min(512, width - c0)
                d_h += _dot(d_ref[:, c0:c0 + cw], w_vm[r0 + c0:r0 + c0 + cw, :])
        xf = x_ref[...]
        r = lax.rsqrt(jnp.mean(xf * xf, axis=-1, keepdims=True) + EPS)
        xn = xf * r
        a = d_h * g_ref[...]
        gx_ref[...] = r * (a - xn * jnp.mean(a * xn, axis=-1, keepdims=True)) + dy_ref[...]
        gg_ref[...] += jnp.sum(d_h * xn, axis=0, keepdims=True)

    return pl.pallas_call(
        body, name="dh_bwd", grid=(s // tm,),
        out_shape=[jax.ShapeDtypeStruct((s, D_MODEL), F32), jax.ShapeDtypeStruct((1, D_MODEL), F32)],
        in_specs=[_rows(tm, w) for _, w in SEGS] + [_rows(tm, D_MODEL), _rows(tm, D_MODEL), _full((1, D_MODEL)), ANY],
        out_specs=[_rows(tm, D_MODEL), _full((1, D_MODEL))],
        scratch_shapes=[pltpu.VMEM((IN_WIDTH, D_MODEL), BF16), pltpu.SemaphoreType.DMA((1,))],
        compiler_params=_params(52),
    )(*dparts, x, dy, g_pre, w_int)


def _gw_in(dparts, h):
    s = h.shape[0]
    tn = 256
    starts, counts = [], []
    for r0, width in SEGS:
        starts.append(r0 // tn)
        counts.append(width // tn)

    def body(*refs):
        d_refs = refs[:7]
        h_hbm, o_ref, h_vm, sems = refs[7:]
        _load_once([(h_hbm, h_vm)], sems)
        j = pl.program_id(0)
        for d_ref, st, cnt in zip(d_refs, starts, counts):
            @pl.when((j >= st) & (j < st + cnt))
            def _(d_ref=d_ref):
                o_ref[...] = _dot_tn(d_ref[...], h_vm[...]).astype(BF16)

    def seg_spec(st, cnt):
        return pl.BlockSpec((s, tn), lambda j: (0, jnp.clip(j - st, 0, cnt - 1)))

    return pl.pallas_call(
        body, name="gw_in", grid=(IN_WIDTH // tn,),
        out_shape=jax.ShapeDtypeStruct((IN_WIDTH, D_MODEL), BF16),
        in_specs=[seg_spec(st, cnt) for st, cnt in zip(starts, counts)] + [ANY],
        out_specs=pl.BlockSpec((tn, D_MODEL), lambda j: (j, 0)),
        scratch_shapes=[pltpu.VMEM((s, D_MODEL), BF16), pltpu.SemaphoreType.DMA((1,))],
        compiler_params=_params(52),
    )(*dparts, h)


def _adamw_math(w, g, m, v):
    m2 = ADAM_B1 * m + (1.0 - ADAM_B1) * g
    v2 = ADAM_B2 * v + (1.0 - ADAM_B2) * (g * g)
    m_hat = m2 / (1.0 - ADAM_B1 ** ADAM_STEP)
    v_hat = v2 / (1.0 - ADAM_B2 ** ADAM_STEP)
    delta = -ADAM_LR * (m_hat / (jnp.sqrt(v_hat) + ADAM_EPS) + ADAM_WD * w)
    return delta, m2, v2


def _sum_adamw(own, land, chip, block, w, m, v, name, tiles=1):
    r, c = w.shape
    rt = r // tiles

    def body(c_ref, own_ref, l1_ref, l2_ref, l3_ref, w_ref, m_ref, v_ref, g_ref, d_ref, m2_ref, v2_ref):
        g = own_ref[...].astype(F32)
        for l_ref in (l1_ref, l2_ref, l3_ref):
            g += l_ref[...].astype(F32)
        g_ref[...] = g
        d_ref[...], m2_ref[...], v2_ref[...] = _adamw_math(w_ref[...], g, m_ref[...], v_ref[...])

    def share(k):
        return pl.BlockSpec((None, rt, c), lambda i, c_ref: (jnp.bitwise_xor(c_ref[0], k), block * tiles + i, 0))

    spec = pl.BlockSpec((rt, c), lambda i, c_ref: (i, 0))
    grid_spec = pltpu.PrefetchScalarGridSpec(
        num_scalar_prefetch=1, grid=(tiles,),
        in_specs=[share(0), share(1), share(2), share(3)] + [spec] * 3, out_specs=[spec] * 4)
    return pl.pallas_call(
        body, name=name, grid_spec=grid_spec,
        out_shape=[jax.ShapeDtypeStruct((r, c), F32)] * 4,
        compiler_params=_params(48),
    )(chip, own, land, land, land, w, m, v)


def _sum_adamw_group(items, chip, name):
    k = len(items)

    def body(c_ref, *refs):
        shares, wmv, outs = refs[:4 * k], refs[4 * k:7 * k], refs[7 * k:]
        for j in range(k):
            g = shares[4 * j][...].astype(F32)
            for l_ref in shares[4 * j + 1:4 * j + 4]:
                g += l_ref[...].astype(F32)
            outs[4 * j][...] = g
            outs[4 * j + 1][...], outs[4 * j + 2][...], outs[4 * j + 3][...] = _adamw_math(
                wmv[3 * j][...], g, wmv[3 * j + 1][...], wmv[3 * j + 2][...])

    def share(shape, block, q):
        return pl.BlockSpec((None,) + shape, lambda i, c_ref: (jnp.bitwise_xor(c_ref[0], q), block, 0))

    in_specs, args = [], []
    for own, land, block, w, m, v in items:
        in_specs += [share(w.shape, block, q) for q in range(4)]
        args += [own, land, land, land]
    for own, land, block, w, m, v in items:
        in_specs += [pl.BlockSpec(w.shape, lambda i, c_ref: (0, 0))] * 3
        args += [w, m, v]
    out_specs = [pl.BlockSpec(w.shape, lambda i, c_ref: (0, 0)) for _, _, _, w, _, _ in items for _ in range(4)]
    res = pl.pallas_call(
        body, name=name,
        grid_spec=pltpu.PrefetchScalarGridSpec(num_scalar_prefetch=1, grid=(1,), in_specs=in_specs,
                                               out_specs=out_specs),
        out_shape=[jax.ShapeDtypeStruct(w.shape, F32) for _, _, _, w, _, _ in items for _ in range(4)],
        compiler_params=_params(48),
    )(chip, *args)
    return [res[4 * j:4 * j + 4] for j in range(k)]


def _small_step(parts, ws, ms, vs):
    def exchange(gpre_ref, gconv_ref, gsink_ref, gmem_ref, gpost_ref, loss_ref, tot_ref,
                 pack, gathered, send_sems, recv_sems):
        x, y, c = _my_place()
        me_idx = 4 * x + 2 * y + c

        lane = lax.broadcasted_iota(jnp.int32, (1, 128), 1)
        sink_row = jnp.zeros((1, 128), F32)
        for h in range(8):
            sink_row = jnp.where(lane == h, gsink_ref[h:h + 1, :], sink_row)
        pack[...] = jnp.zeros_like(pack)
        pack[0:1, :] = gpre_ref[...]
        pack[1:2, :] = gmem_ref[...]
        pack[2:3, :] = gpost_ref[...]
        pack[3:6, 0:512] = gconv_ref[0:3, :]
        pack[6:7, 0:128] = sink_row
        pack[7:8, 0:128] = loss_ref[0:1, :]

        flips = [(0, 0, 1), (0, 1, 0), (1, 0, 0), (0, 1, 1), (1, 0, 1), (1, 1, 0), (1, 1, 1)]
        cps = []
        for k, (fx, fy, fc) in enumerate(flips):
            peer = ((1 - x) if fx else x, (1 - y) if fy else y, (1 - c) if fc else c)
            cps.append(pltpu.make_async_remote_copy(
                src_ref=pack, dst_ref=gathered.at[me_idx], send_sem=send_sems.at[k], recv_sem=recv_sems.at[k],
                device_id=peer, device_id_type=MESH))
        for cp in cps:
            cp.start()
        gathered[me_idx] = pack[...]
        for cp in cps:
            cp.wait_recv()
        for cp in cps:
            cp.wait_send()
        tot = gathered[0]
        for d in range(1, N_DEV):
            tot = tot + gathered[d]
        tot_ref[...] = tot

    tot = pl.pallas_call(
        exchange, name="small_exchange", grid=(1,),
        out_shape=jax.ShapeDtypeStruct((8, D_MODEL), F32),
        in_specs=[_full(p.shape) for p in parts], out_specs=_full((8, D_MODEL)),
        scratch_shapes=[pltpu.VMEM((8, D_MODEL), F32), pltpu.VMEM((N_DEV, 8, D_MODEL), F32),
                        pltpu.SemaphoreType.DMA((N_PEERS,)), pltpu.SemaphoreType.DMA((N_PEERS,))],
    )(*parts)

    def apply(tot_ref, *refs):
        w_refs, m_refs, v_refs = refs[0:5], refs[5:10], refs[10:15]
        loss_out = refs[15]
        g_outs, d_outs, m_outs, v_outs = refs[16:21], refs[21:26], refs[26:31], refs[31:36]
        x, y, c = _my_place()
        tot = tot_ref[...]
        conv = pltpu.roll(tot[:, 0:512], (512 - 64 * (4 * x + 2 * y + c)) % 512, 1)[3:6, 0:64]
        grads = (tot[0:1, :], conv, tot[6:7, 0:8], tot[1:2, :], tot[2:3, :])
        loss_out[...] = tot[7:8, 0:128]
        for j in range(5):
            g_outs[j][...] = grads[j]
            d_outs[j][...], m_outs[j][...], v_outs[j][...] = _adamw_math(
                w_refs[j][...], grads[j], m_refs[j][...], v_refs[j][...])

    specs = [_full(w.shape) for w in ws]
    res = pl.pallas_call(
        apply, name="small_apply", grid=(1,),
        out_shape=[jax.ShapeDtypeStruct((1, 128), F32)] + [jax.ShapeDtypeStruct(w.shape, F32) for w in ws] * 4,
        in_specs=[_full((8, D_MODEL))] + specs * 3,
        out_specs=[_full((1, 128))] + specs * 4,
    )(tot, *ws, *ms, *vs)
    return res[0], res[1:6], res[6:11], res[11:16], res[16:21]


def kernel(x, mem, g_pre, w_in, w_conv, attn_sink, g_mem, w_mem_kv, w_up_a, w_up_b, w_up_m, w_out, g_post, loss_target, m_g_pre, m_w_in, m_w_conv, m_attn_sink, m_g_mem, m_w_mem_kv, m_w_up_a, m_w_up_b, m_w_up_m, m_w_out, m_g_post, v_g_pre, v_w_in, v_w_conv, v_attn_sink, v_g_mem, v_w_mem_kv, v_w_up_a, v_w_up_b, v_w_up_m, v_w_out, v_g_post):
    s = x.shape[1]
    x2, mem2, tgt2 = x[0], mem[0], loss_target[0]
    me = 4 * lax.axis_index("x") + 2 * lax.axis_index("y") + lax.axis_index("c")

    w_up_loc = jnp.concatenate([w_up_a[0], w_up_b[0], w_up_m[0]], axis=0).astype(BF16)
    w_conv_loc = jnp.zeros((8, 128), F32).at[:3, :64].set(w_conv[0])
    w_int_g, w_conv_g = _all_gather([w_in[0].T.astype(BF16), w_conv_loc], "gather_w_in",
                                    splits=[[(0, 240), (240, 240), (480, 224), (704, 224)], [(0, 8)]])
    w_int = w_int_g.reshape(IN_WIDTH, D_MODEL)
    w_conv_f = w_conv_g[:, :3, :64].transpose(1, 0, 2).reshape(3, 512)
    late = _gather_start([w_mem_kv[0].astype(BF16) + w_conv_g[0, 7:8, 0:1].astype(BF16),
                          w_out[0].astype(BF16), w_up_loc], me, "gather_late_start")
    sink = attn_sink[0]
    tabs = _rope_tables(s)

    h, pa, pq, pkv, pbz, pmq, pmz, pg = _proj_fwd(x2, g_pre + late[4][0:1, 0:1], w_int, tabs)
    ya = _conv_fwd(pa, w_conv_f)
    yb = _attn_fwd(pq, pkv, pbz, sink)
    w_mkv_g, w_out_g, w_up_g = _gather_wait(*late[:4], yb, "gather_late_wait")
    w_mkv = w_mkv_g.reshape(D_MODEL, D_MODEL)
    w_out_f = w_out_g.reshape(D_MODEL, D_MODEL)
    mn, mkv = _mem_kv_fwd(mem2, g_mem, w_mkv)
    ym = _mem_attn_fwd(pmq, pmz, mkv)
    dg, dya, dyb, dym, dy, loss_p, gg_post, mb, dob, du = _mid(ya, yb, ym, pg, x2, tgt2, g_post, w_up_g, w_out_f)
    gw_out, gw_up = _gw_mid(mb, dob, (ya, yb, ym), du)

    core = lax.axis_index("c").astype(jnp.int32).reshape(1)
    chip = (2 * lax.axis_index("x") + lax.axis_index("y")).astype(jnp.int32).reshape(1)

    def exchange_start(shares, tag):
        from_sibling = _sibling_exchange(shares, "grads_to_sibling_" + tag)
        chip_shares = _pair_add(shares, from_sibling, core, "grads_pair_add_" + tag)
        return _chip_exchange_start(chip_shares, "grads_to_chips_start_" + tag)

    dmq, dmz, dmkv = _mem_attn_bwd(pmq, pmz, mkv, dym)
    gw_mkv, gg_mem = _mem_kv_bwd(mem2, g_mem, mn, dmkv, w_mkv)
    send1, recv1, srcs1, lands1, token1 = exchange_start(
        [gw_mkv.reshape(N_DEV, 128, D_MODEL), gw_out.reshape(N_DEV, 128, D_MODEL), gw_up], "small")
    da, gw_conv = _conv_bwd(pa, dya, w_conv_f + token1[0:1, 0:1])
    dq, dbz, dkv, g_sink = _attn_bwd(pq, pkv, pbz, dyb, sink, tabs)
    dparts = (da, dq, dkv, dbz, dmq, dmz, dg)
    gw_int = _gw_in(dparts, h)
    send2, recv2, srcs2, lands2, token2 = exchange_start([gw_int.reshape(N_DEV, SHARD_IN, D_MODEL)], "w_in")
    grad_x, gg_pre = _dh_bwd(dparts, x2, dy, g_pre + token2[0:1, 0:1], w_int)
    (o_mkv, o_out, o_up, o_int), (l_mkv, l_out, l_up, l_int) = _chip_exchange_wait(
        send1 + send2, recv1 + recv2, srcs1 + srcs2, lands1 + lands2, grad_x, "grads_to_chips_wait")

    loss_row, small_g, sd, sm, sv = _small_step(
        (gg_pre, gw_conv, g_sink, gg_mem, gg_post, loss_p),
        [g_pre, w_conv[0], attn_sink, g_mem, g_post],
        [m_g_pre, m_w_conv[0], m_attn_sink, m_g_mem, m_g_post],
        [v_g_pre, v_w_conv[0], v_attn_sink, v_g_mem, v_g_post])
    loss = loss_row[0, 0]
    g_g_pre, g_conv, g_sink_tot, g_g_mem, g_g_post = small_g

    g_w_in, d_w_in, nm_w_in, nv_w_in = (t.T for t in _sum_adamw(
        o_int, l_int, chip, 0, w_in[0].T, m_w_in[0].T, v_w_in[0].T, "adamw_w_in", tiles=2))
    (g_mkv, d_mkv, nm_mkv, nv_mkv), (g_out, d_out, nm_out, nv_out), *up = _sum_adamw_group(
        [(o_mkv, l_mkv, 0, w_mem_kv[0], m_w_mem_kv[0], v_w_mem_kv[0]),
         (o_out, l_out, 0, w_out[0], m_w_out[0], v_w_out[0]),
         (o_up, l_up, 0, w_up_a[0], m_w_up_a[0], v_w_up_a[0]),
         (o_up, l_up, 1, w_up_b[0], m_w_up_b[0], v_w_up_b[0]),
         (o_up, l_up, 2, w_up_m[0], m_w_up_m[0], v_w_up_m[0])], chip, "adamw_mid_weights")

    def lead(a):
        return a[None]

    grads = [g_g_pre, lead(g_w_in), lead(g_conv), g_sink_tot, g_g_mem, lead(g_mkv), lead(up[0][0]),
             lead(up[1][0]), lead(up[2][0]), lead(g_out), g_g_post]

    def assemble(small, big_in, big_mkv, big_up, big_out):
        return [small[0], lead(big_in), lead(small[1]), small[2], small[3], lead(big_mkv), lead(big_up[0]),
                lead(big_up[1]), lead(big_up[2]), lead(big_out), small[4]]

    deltas = assemble(sd, d_w_in, d_mkv, [u[1] for u in up], d_out)
    new_m = assemble(sm, nm_w_in, nm_mkv, [u[2] for u in up], nm_out)
    new_v = assemble(sv, nv_w_in, nv_mkv, [u[3] for u in up], nv_out)
    return (loss, grad_x[None], *grads, *deltas, *new_m, *new_v)
```

```python
import functools

import jax
import jax.numpy as jnp
from jax import lax
from jax.experimental import pallas as pl
from jax.experimental.pallas import tpu as pltpu

F32 = jnp.float32
BF16 = jnp.bfloat16
MESH = pl.DeviceIdType.MESH

N_DEV = 8
D_MODEL = 1024
EPS = 1e-6
ROPE_THETA = 500000.0
ROT_DIM = 16
HEAD_DIM = 64
ATTN_BLOCK = 128
MEM_HEADS = 4
MEM_HEAD_DIM = 128
ATTN_SCALE = HEAD_DIM ** -0.5
MEM_SCALE = MEM_HEAD_DIM ** -0.5

ADAM_LR = 0.001
ADAM_B1 = 0.9
ADAM_B2 = 0.999
ADAM_EPS = 1e-08
ADAM_WD = 0.01
ADAM_STEP = 10

SEG_A = (0, 2048)
SEG_BQ = (2048, 512)
SEG_BKV = (2560, 256)
SEG_BZ = (2816, 512)
SEG_MQ = (3328, 512)
SEG_MZ = (3840, 512)
SEG_G = (4352, 3072)
SEGS = (SEG_A, SEG_BQ, SEG_BKV, SEG_BZ, SEG_MQ, SEG_MZ, SEG_G)
IN_WIDTH = 7424
SHARD_IN = IN_WIDTH // N_DEV

V7X_VMEM_BYTES = 64 * 1024 * 1024
ANY = pl.BlockSpec(memory_space=pl.ANY)


def _params(vmem_mb):
    assert vmem_mb * 1024 * 1024 < V7X_VMEM_BYTES
    return pltpu.CompilerParams(dimension_semantics=("arbitrary",), vmem_limit_bytes=vmem_mb * 1024 * 1024)


def _full(shape):
    zeros = (0,) * len(shape)
    return pl.BlockSpec(shape, lambda i: zeros)


def _rows(tm, width):
    return pl.BlockSpec((tm, width), lambda i: (i, 0))


def _dot(a, b):
    return jnp.dot(a, b, preferred_element_type=F32)


def _dot_nt(a, b):
    return lax.dot_general(a, b, (((1,), (1,)), ((), ())), preferred_element_type=F32)


def _dot_tn(a, b):
    return lax.dot_general(a, b, (((0,), (0,)), ((), ())), preferred_element_type=F32)


def _sigmoid(z):
    return 1.0 / (1.0 + jnp.exp(-z))


def _rope(t, cs, s1, s2):
    return t * cs + pltpu.roll(t, 120, 1) * s1 + pltpu.roll(t, 8, 1) * s2


def _rope_t(d, cs, s1, s2):
    return d * cs + pltpu.roll(d * s1, 8, 1) + pltpu.roll(d * s2, 120, 1)


def _rope_tables(s):
    half = ROT_DIM // 2
    inv_freq = jnp.power(jnp.float32(ROPE_THETA), -jnp.arange(half, dtype=F32) * (2.0 / ROT_DIM))
    ang = jnp.arange(s).astype(F32)[:, None] * inv_freq[None, :]
    cos, sin = jnp.cos(ang), jnp.sin(ang)
    d = jnp.arange(128) % HEAD_DIM
    k = jnp.arange(half)[:, None]
    lo = (d[None, :] == k).astype(F32)
    hi = (d[None, :] == k + half).astype(F32)
    spread = functools.partial(jnp.dot, precision=lax.Precision.HIGHEST)
    return (spread(cos, lo + hi) + (d >= ROT_DIM).astype(F32)[None, :], -spread(sin, lo), spread(sin, hi))


def _load_once(pairs, sems):
    @pl.when(pl.program_id(0) == 0)
    def _():
        cps = [pltpu.make_async_copy(src, dst, sems.at[k]) for k, (src, dst) in enumerate(pairs)]
        for cp in cps:
            cp.start()
        for cp in cps:
            cp.wait()


def _my_place():
    x, y, c = lax.axis_index("x"), lax.axis_index("y"), lax.axis_index("c")
    return x, y, c


def _all_gather(arrs, name, splits=None):
    n = len(arrs)
    if splits is None:
        splits = [[(0, a.shape[0])] for a in arrs]
    pieces = [(a, r0, rn) for a in range(n) for r0, rn in splits[a]]
    n_p = len(pieces)

    def body(*refs):
        ins, outs = refs[:n], refs[n:2 * n]
        send_sems, recv_sems, local_sems = refs[2 * n:]
        x, y, c = _my_place()
        me, sibling = (x, y, c), (x, y, 1 - c)

        def route(core):
            first = (jnp.bitwise_xor(x, 1 - core), jnp.bitwise_xor(y, core), core)
            second = (jnp.bitwise_xor(x, core), jnp.bitwise_xor(y, 1 - core), core)
            return first, second, (1 - x, 1 - y, core)

        def idx(px, py, pc):
            return 4 * px + 2 * py + pc

        def copy(p, k, block, to, own=False):
            a, r0, rn = pieces[p]
            dst = outs[a].at[idx(*block), pl.ds(r0, rn)]
            return pltpu.make_async_remote_copy(
                src_ref=ins[a].at[pl.ds(r0, rn)] if own else dst, dst_ref=dst,
                send_sem=send_sems.at[p * 7 + k], recv_sem=recv_sems.at[p * 7 + k],
                device_id=to, device_id_type=MESH)

        nbr1, nbr2, diag = route(c)
        mine = [pltpu.make_async_copy(ins[a], outs[a].at[idx(*me)], local_sems.at[a]) for a in range(n)]
        for cp in mine:
            cp.start()
        sent = []
        for p in range(n_p):
            for k, to in enumerate((sibling, nbr1, nbr2)):
                sent.append(copy(p, k, me, to, own=True))
        for cp in sent:
            cp.start()
        for k_in, block, onward in ((1, nbr1, ((3, nbr2), (4, sibling))), (2, nbr2, ((5, sibling),)),
                                    (3, diag, ((6, sibling),))):
            for p in range(n_p):
                copy(p, k_in, block, me).wait_recv()
                for k_out, to in onward:
                    cp = copy(p, k_out, block, to)
                    cp.start()
                    sent.append(cp)
        s1, s2, sd = route(1 - c)
        for k_in, block in ((0, sibling), (4, s1), (5, s2), (6, sd)):
            for p in range(n_p):
                copy(p, k_in, block, me).wait_recv()
        for cp in sent:
            cp.wait_send()
        for cp in mine:
            cp.wait()

    return pl.pallas_call(
        body, name=name,
        out_shape=[jax.ShapeDtypeStruct((N_DEV,) + a.shape, a.dtype) for a in arrs],
        in_specs=[ANY] * n, out_specs=[ANY] * n,
        scratch_shapes=[pltpu.SemaphoreType.DMA((7 * n_p,)), pltpu.SemaphoreType.DMA((7 * n_p,)),
                        pltpu.SemaphoreType.DMA((n,))],
    )(*arrs)


N_CHIPS = 4


def _sibling_exchange(arrs, name):
    n = len(arrs)

    def body(*refs):
        ins, outs = refs[:n], refs[n:2 * n]
        send_sems, recv_sems = refs[2 * n:]
        x, y, c = _my_place()
        sibling = (x, y, 1 - c)

        def copy(a, j):
            return pltpu.make_async_remote_copy(
                src_ref=ins[a].at[2 * j + (1 - c)], dst_ref=outs[a].at[j],
                send_sem=send_sems.at[a * N_CHIPS + j], recv_sem=recv_sems.at[a * N_CHIPS + j],
                device_id=sibling, device_id_type=MESH)

        cps = [copy(a, j) for j in range(N_CHIPS) for a in range(n)]
        for cp in cps:
            cp.start()
        for cp in cps:
            cp.wait_recv()
        for cp in cps:
            cp.wait_send()

    return pl.pallas_call(
        body, name=name,
        out_shape=[jax.ShapeDtypeStruct((N_CHIPS,) + a.shape[1:], a.dtype) for a in arrs],
        in_specs=[ANY] * n, out_specs=[ANY] * n,
        scratch_shapes=[pltpu.SemaphoreType.DMA((N_CHIPS * n,)), pltpu.SemaphoreType.DMA((N_CHIPS * n,))],
    )(*arrs)


def _pair_add(mine, recv, core, name):
    n = len(mine)

    def body(c_ref, *refs):
        for a in range(n):
            refs[2 * n + a][...] = (refs[a][...].astype(F32) + refs[n + a][...].astype(F32)).astype(BF16)

    def blk(a):
        return (None,) + a.shape[1:]

    grid_spec = pltpu.PrefetchScalarGridSpec(
        num_scalar_prefetch=1, grid=(N_CHIPS,),
        in_specs=[pl.BlockSpec(blk(a), lambda j, c_ref: (2 * j + c_ref[0], 0, 0)) for a in mine]
        + [pl.BlockSpec(blk(a), lambda j, c_ref: (j, 0, 0)) for a in recv],
        out_specs=[pl.BlockSpec(blk(a), lambda j, c_ref: (j, 0, 0)) for a in recv])
    return pl.pallas_call(
        body, name=name, grid_spec=grid_spec,
        out_shape=[jax.ShapeDtypeStruct(a.shape, BF16) for a in recv],
        compiler_params=_params(32),
    )(core, *mine, *recv)


HBM = pl.BlockSpec(memory_space=pltpu.HBM)
SEM = pl.BlockSpec(memory_space=pltpu.SEMAPHORE)
N_PEER_CHIPS = 3


def _chip_copies(srcs, lands, send_sems, recv_sems):
    x, y, c = _my_place()
    my_chip = 2 * x + y
    peers = [(x, 1 - y), (1 - x, y), (1 - x, 1 - y)]
    cps = []
    for k, (px, py) in enumerate(peers):
        for a in range(len(srcs)):
            j = a * N_PEER_CHIPS + k
            cps.append(pltpu.make_async_remote_copy(
                src_ref=srcs[a].at[2 * px + py], dst_ref=lands[a].at[my_chip],
                send_sem=send_sems[j], recv_sem=recv_sems[j],
                device_id=(px, py, c), device_id_type=MESH))
    return cps


N_PEERS = N_DEV - 1


def _gather_copies(srcs, lands, send_sems, recv_sems):
    x, y, c = _my_place()
    me_idx = 4 * x + 2 * y + c
    flips = [(0, 0, 1), (0, 1, 0), (1, 0, 0), (0, 1, 1), (1, 0, 1), (1, 1, 0), (1, 1, 1)]
    cps = []
    for k, (fx, fy, fc) in enumerate(flips):
        peer = ((1 - x) if fx else x, (1 - y) if fy else y, (1 - c) if fc else c)
        for a in range(len(srcs)):
            j = a * N_PEERS + k
            cps.append(pltpu.make_async_remote_copy(
                src_ref=srcs[a], dst_ref=lands[a].at[me_idx], send_sem=send_sems[j], recv_sem=recv_sems[j],
                device_id=peer, device_id_type=MESH))
    return cps


def _split_start(copies, per_array, arrs, lands, name):
    arrs, lands = list(arrs), list(lands)
    n = len(arrs)
    k = n * per_array

    def body(*refs):
        srcs, land_refs = refs[:n], refs[n:2 * n]
        send_sems, recv_sems = refs[2 * n:2 * n + k], refs[2 * n + k:2 * n + 2 * k]
        token = refs[-1]
        for cp in copies(srcs, land_refs, send_sems, recv_sems):
            cp.start()
        token[...] = jnp.zeros_like(token)

    hbm_arrs = [pltpu.with_memory_space_constraint(a, pltpu.HBM) for a in arrs]
    lands = [pltpu.with_memory_space_constraint(a, pltpu.HBM) for a in lands]
    res = pl.pallas_call(
        body, name=name,
        out_shape=[pltpu.SemaphoreType.DMA(())] * (2 * k) + [pltpu.HBM(a.shape, a.dtype) for a in arrs + lands]
        + [jax.ShapeDtypeStruct((8, 128), F32)],
        in_specs=[HBM] * (2 * n),
        out_specs=[SEM] * (2 * k) + [HBM] * (2 * n) + [pl.BlockSpec(memory_space=pltpu.VMEM)],
        input_output_aliases={a: 2 * k + a for a in range(2 * n)},
        compiler_params=pltpu.CompilerParams(has_side_effects=pltpu.SideEffectType.DATAFLOW_SIDE_EFFECTING),
    )(*hbm_arrs, *lands)
    return res[:k], res[k:2 * k], res[2 * k:2 * k + n], res[2 * k + n:2 * k + 2 * n], res[-1]


def _split_wait(copies, per_array, send_sems, recv_sems, srcs, lands, after, name):
    n = len(srcs)
    k = n * per_array

    def body(*refs):
        src_refs, land_refs = refs[:n], refs[n:2 * n]
        s_sems, r_sems = refs[2 * n:2 * n + k], refs[2 * n + k:2 * n + 2 * k]
        for cp in copies(src_refs, land_refs, s_sems, r_sems):
            cp.wait_send()
            cp.wait_recv()

    res = pl.pallas_call(
        body, name=name,
        out_shape=[pltpu.HBM(a.shape, a.dtype) for a in list(srcs) + list(lands)],
        in_specs=[HBM] * (2 * n) + [SEM] * (2 * k) + [ANY],
        out_specs=[HBM] * (2 * n),
        input_output_aliases={a: a for a in range(2 * n)},
        compiler_params=pltpu.CompilerParams(has_side_effects=pltpu.SideEffectType.DATAFLOW_SIDE_EFFECTING),
    )(*srcs, *lands, *send_sems, *recv_sems, after)
    return res[:n], res[n:]


def _chip_exchange_start(arrs, name):
    return _split_start(_chip_copies, N_PEER_CHIPS, arrs, [lax.empty(a.shape, a.dtype) for a in arrs], name)


def _chip_exchange_wait(send_sems, recv_sems, srcs, lands, after, name):
    return _split_wait(_chip_copies, N_PEER_CHIPS, send_sems, recv_sems, srcs, lands, after, name)


def _gather_start(arrs, me_idx, name):
    lands = [lax.dynamic_update_slice(lax.empty((N_DEV,) + a.shape, a.dtype), a[None], (me_idx, 0, 0)) for a in arrs]
    return _split_start(_gather_copies, N_PEERS, arrs, lands, name)


def _gather_wait(send_sems, recv_sems, srcs, lands, after, name):
    return _split_wait(_gather_copies, N_PEERS, send_sems, recv_sems, srcs, lands, after, name)[1]


def _proj_fwd(x, g_pre, w_int, tabs):
    s = x.shape[0]
    tm = min(512, s)

    def body(x_ref, g_ref, t_ref, w_hbm,
             h_ref, pa_ref, pq_ref, pkv_ref, pbz_ref, pmq_ref, pmz_ref, pg_ref, w_vm, sems):
        _load_once([(w_hbm, w_vm)], sems)
        xf = x_ref[...]
        r = lax.rsqrt(jnp.mean(xf * xf, axis=-1, keepdims=True) + EPS)
        h = ((xf * r) * g_ref[...]).astype(BF16)
        h_ref[...] = h
        cs, s1, s2 = t_ref[0], t_ref[1], t_ref[2]

        def mm(seg, c0, width):
            return _dot_nt(h, w_vm[seg[0] + c0:seg[0] + c0 + width, :])

        for c0 in range(0, SEG_A[1], 512):
            pa_ref[:, c0:c0 + 512] = mm(SEG_A, c0, 512).astype(BF16)
        q = mm(SEG_BQ, 0, 512)
        for b in range(4):
            pq_ref[:, 128 * b:128 * b + 128] = _rope(q[:, 128 * b:128 * b + 128], cs, s1, s2).astype(BF16)
        kv = mm(SEG_BKV, 0, 256)
        pkv_ref[:, 0:128] = _rope(kv[:, 0:128], cs, s1, s2).astype(BF16)
        pkv_ref[:, 128:256] = kv[:, 128:256].astype(BF16)
        pbz_ref[...] = mm(SEG_BZ, 0, 512).astype(BF16)
        pmq_ref[...] = mm(SEG_MQ, 0, 512).astype(BF16)
        pmz_ref[...] = mm(SEG_MZ, 0, 512).astype(BF16)
        for c0 in range(0, SEG_G[1], 512):
            pg_ref[:, c0:c0 + 512] = mm(SEG_G, c0, 512).astype(BF16)

    widths = (D_MODEL, 2048, 512, 256, 512, 512, 512, 3072)
    return pl.pallas_call(
        body, name="proj_fwd", grid=(s // tm,),
        out_shape=[jax.ShapeDtypeStruct((s, w), BF16) for w in widths],
        in_specs=[_rows(tm, D_MODEL), _full((1, D_MODEL)), pl.BlockSpec((3, tm, 128), lambda i: (0, i, 0)), ANY],
        out_specs=[_rows(tm, w) for w in widths],
        scratch_shapes=[pltpu.VMEM((IN_WIDTH, D_MODEL), BF16), pltpu.SemaphoreType.DMA((1,))],
        compiler_params=_params(52),
    )(x, g_pre, tabs, w_int)


def _mem_kv_fwd(mem, g_mem, w_mkv):
    m = mem.shape[0]

    def body(mem_ref, g_ref, w_ref, mn_ref, mkv_ref):
        xf = mem_ref[...]
        r = lax.rsqrt(jnp.mean(xf * xf, axis=-1, keepdims=True) + EPS)
        mn = ((xf * r) * g_ref[...]).astype(BF16)
        mn_ref[...] = mn
        mkv_ref[...] = _dot(mn, w_ref[...]).astype(BF16)

    return pl.pallas_call(
        body, name="mem_kv_fwd", grid=(1,),
        out_shape=[jax.ShapeDtypeStruct((m, D_MODEL), BF16)] * 2,
        in_specs=[_full((m, D_MODEL)), _full((1, D_MODEL)), _full((D_MODEL, D_MODEL))],
        out_specs=[_full((m, D_MODEL))] * 2,
        compiler_params=_params(32),
    )(mem, g_mem, w_mkv)


def _halo_specs(s, tm, rows, width):
    nblk = s // rows
    prev = pl.BlockSpec((rows, width), lambda i: (jnp.maximum(i * (tm // rows) - 1, 0), 0))
    nxt = pl.BlockSpec((rows, width), lambda i: (jnp.minimum((i + 1) * (tm // rows), nblk - 1), 0))
    return prev, nxt


def _conv_common(pa, prev_row, next_row, w, first, last, tm):
    b, c, u, z = (pa[:, 512 * k:512 * k + 512] for k in range(4))
    cu = c * u
    cu_prev = jnp.where(first, 0.0, prev_row[:, 512:1024] * prev_row[:, 1024:1536])
    cu_next = jnp.where(last, 0.0, next_row[:, 512:1024] * next_row[:, 1024:1536])
    row = lax.broadcasted_iota(jnp.int32, (tm, 512), 0)
    cu_m1 = jnp.where(row == 0, cu_prev, pltpu.roll(cu, 1, 0))
    cu_p1 = jnp.where(row == tm - 1, cu_next, pltpu.roll(cu, tm - 1, 0))
    y = cu_m1 * w[0:1] + cu * w[1:2] + cu_p1 * w[2:3]
    sig = _sigmoid(z)
    return b, c, u, z, cu, cu_m1, cu_p1, y, sig, row


def _conv_fwd(pa, w_conv):
    s = pa.shape[0]
    tm = min(512, s)
    nt = s // tm

    def body(pa_ref, pp_ref, pn_ref, w_ref, ya_ref):
        i = pl.program_id(0)
        prev_row = pp_ref[...].astype(F32)[15:16, :]
        next_row = pn_ref[...].astype(F32)[0:1, :]
        b, _, _, z, _, _, _, y, sig, _ = _conv_common(
            pa_ref[...].astype(F32), prev_row, next_row, w_ref[...], i == 0, i == nt - 1, tm)
        ya_ref[...] = (b * y * (z * sig)).astype(BF16)

    prev, nxt = _halo_specs(s, tm, 16, 2048)
    return pl.pallas_call(
        body, name="conv_fwd", grid=(nt,),
        out_shape=jax.ShapeDtypeStruct((s, 512), BF16),
        in_specs=[_rows(tm, 2048), prev, nxt, _full((3, 512))],
        out_specs=_rows(tm, 512),
        compiler_params=_params(48),
    )(pa, pa, pa, w_conv)


def _heads_to_lanes(a, g, row):
    low = row < HEAD_DIM
    parts = []
    for b in (2 * g, 2 * g + 1):
        t = jnp.transpose(a[:, 128 * b:128 * b + 128])
        swapped = pltpu.roll(t, HEAD_DIM, 0)
        if g == 0:
            parts += [jnp.where(low, t, 0.0), jnp.where(low, swapped, 0.0)]
        else:
            parts += [jnp.where(low, 0.0, swapped), jnp.where(low, 0.0, t)]
    return jnp.concatenate(parts, axis=1)


def _lanes_to_heads(t0, t1, row):
    low = row < HEAD_DIM
    blocks = []
    for b in range(4):
        g = b // 2
        tg = (t0, t1)[g]
        je = 2 * (b - 2 * g)
        even, odd = tg[:, 128 * je:128 * je + 128], tg[:, 128 * je + 128:128 * je + 256]
        if g == 0:
            t = jnp.where(low, even, pltpu.roll(odd, HEAD_DIM, 0))
        else:
            t = jnp.where(low, pltpu.roll(even, HEAD_DIM, 0), odd)
        blocks.append(jnp.transpose(t))
    return jnp.concatenate(blocks, axis=1)


WINDOW_KEYS = 3 * ATTN_BLOCK
STACKED = 4 * ATTN_BLOCK
KEY_CHUNK = 32
BLOCKS_PER_STEP = 4


def _fill_band_bias(bias, nb):
    assert nb >= 2
    c = lax.broadcasted_iota(jnp.int32, (WINDOW_KEYS, STACKED), 0)
    r = lax.broadcasted_iota(jnp.int32, (WINDOW_KEYS, STACKED), 1) & (ATTN_BLOCK - 1)
    band = (c >= r) & (c <= r + 2 * ATTN_BLOCK)
    for v, ok in enumerate((band, band & (c >= ATTN_BLOCK), band & (c < 2 * ATTN_BLOCK))):
        bias[v] = jnp.where(ok, 0.0, -jnp.inf)


def _bias_variant(n, nb):
    return jnp.where(n == 0, 1, jnp.where(n == nb - 1, 2, 0))


def _sink_row(sink_ref, g):
    return jnp.concatenate([jnp.full((1, ATTN_BLOCK), sink_ref[4 * g + j], F32) for j in range(4)], axis=1)


def _softmax_keys_major(sc, bias, variant, sink, e_scr):
    chunks = [pl.ds(k * KEY_CHUNK, KEY_CHUNK) for k in range(WINDOW_KEYS // KEY_CHUNK)]
    rows = [slice(k * KEY_CHUNK, (k + 1) * KEY_CHUNK) for k in range(WINDOW_KEYS // KEY_CHUNK)]
    m_run = jnp.full((KEY_CHUNK, STACKED), -jnp.inf, F32)
    for ck, rw in zip(chunks, rows):
        m_run = jnp.maximum(m_run, sc[rw] * ATTN_SCALE + bias[variant, ck, :])
    m = jnp.maximum(jnp.max(m_run, axis=0, keepdims=True), sink)
    l_run = jnp.zeros((KEY_CHUNK, STACKED), F32)
    for ck, rw in zip(chunks, rows):
        e = jnp.exp(sc[rw] * ATTN_SCALE + bias[variant, ck, :] - m)
        l_run += e
        e_scr[rw, :] = e.astype(BF16)
    es = jnp.exp(sink - m)
    inv = 1.0 / (jnp.sum(l_run, axis=0, keepdims=True) + es)
    return inv, es * inv


def _fill_padded(kv_ref, kpad, vpad, s):
    zero = jnp.zeros((ATTN_BLOCK, 128), BF16)
    kpad[0:ATTN_BLOCK, :] = zero
    vpad[0:ATTN_BLOCK, :] = zero
    kpad[ATTN_BLOCK + s:2 * ATTN_BLOCK + s, :] = zero
    vpad[ATTN_BLOCK + s:2 * ATTN_BLOCK + s, :] = zero
    kpad[ATTN_BLOCK:ATTN_BLOCK + s, :] = kv_ref[:, 0:128]
    vpad[ATTN_BLOCK:ATTN_BLOCK + s, :] = kv_ref[:, 128:256]


def _attn_fwd(pq, pkv, pbz, sink):
    s = pq.shape[0]
    nb = s // ATTN_BLOCK

    def body(sink_ref, q_ref, z_ref, kv_ref, yb_ref, kpad, vpad, bias, e_scr):
        i = pl.program_id(0)

        @pl.when(i == 0)
        def _():
            _fill_padded(kv_ref, kpad, vpad, s)
            _fill_band_bias(bias, nb)

        row = lax.broadcasted_iota(jnp.int32, (ATTN_BLOCK, 128), 0)
        for b in range(BLOCKS_PER_STEP):
            n = i * BLOCKS_PER_STEP + b
            rows = slice(b * ATTN_BLOCK, (b + 1) * ATTN_BLOCK)
            start = pl.multiple_of(n * ATTN_BLOCK, ATTN_BLOCK)
            kw, vw = kpad[pl.ds(start, WINDOW_KEYS), :], vpad[pl.ds(start, WINDOW_KEYS), :]
            qf = q_ref[rows, :].astype(F32)
            variant = _bias_variant(n, nb)
            outs = []
            for g in range(2):
                e_bg = e_scr.at[2 * b + g]
                qt = _heads_to_lanes(qf, g, row).astype(BF16)
                inv, _ = _softmax_keys_major(_dot(kw, qt), bias, variant, _sink_row(sink_ref, g), e_bg)
                outs.append(_dot_tn(vw, e_bg[...]) * inv)
            attn = _lanes_to_heads(outs[0], outs[1], row)
            z = z_ref[rows, :].astype(F32)
            yb_ref[rows, :] = (attn * (z * _sigmoid(z))).astype(BF16)

    tq = BLOCKS_PER_STEP * ATTN_BLOCK
    return pl.pallas_call(
        body, name="attn_fwd", grid=(s // tq,),
        out_shape=jax.ShapeDtypeStruct((s, 512), BF16),
        in_specs=[pl.BlockSpec(memory_space=pltpu.SMEM), _rows(tq, 512), _rows(tq, 512), _full((s, 256))],
        out_specs=_rows(tq, 512),
        scratch_shapes=[pltpu.VMEM((s + 2 * ATTN_BLOCK, 128), BF16)] * 2
        + [pltpu.VMEM((3, WINDOW_KEYS, STACKED), F32),
           pltpu.VMEM((2 * BLOCKS_PER_STEP, WINDOW_KEYS, STACKED), BF16)],
        compiler_params=_params(32),
    )(sink, pq, pbz, pkv)


def _mem_softmax_t(q, mk):
    sc = _dot_nt(mk, q) * MEM_SCALE
    e = jnp.exp(sc - jnp.max(sc, axis=0, keepdims=True))
    return e * (1.0 / jnp.sum(e, axis=0, keepdims=True))


def _mem_attn_fwd(pmq, pmz, mkv):
    s = pmq.shape[0]
    m = mkv.shape[0]
    tm = min(512, s)

    def body(q_ref, z_ref, mk_ref, mv_ref, ym_ref):
        z = z_ref[...].astype(F32)
        sz = z * _sigmoid(z)
        for h in range(MEM_HEADS):
            cols = slice(128 * h, 128 * h + 128)
            pt = _mem_softmax_t(q_ref[:, cols], mk_ref[:, cols])
            o = _dot_tn(pt.astype(BF16), mv_ref[:, cols])
            ym_ref[:, cols] = (o * sz[:, cols]).astype(BF16)

    return pl.pallas_call(
        body, name="mem_attn_fwd", grid=(s // tm,),
        out_shape=jax.ShapeDtypeStruct((s, 512), BF16),
        in_specs=[_rows(tm, 512), _rows(tm, 512), pl.BlockSpec((m, 512), lambda i: (0, 0)),
                  pl.BlockSpec((m, 512), lambda i: (0, 1))],
        out_specs=_rows(tm, 512),
        compiler_params=_params(32),
    )(pmq, pmz, mkv, mkv)


def _mid(ya, yb, ym, pg, x, target, g_post, w_up, w_out):
    s = x.shape[0]
    tm = min(256, s)
    nt = s // tm

    def body(ya_ref, yb_ref, ym_ref, pg_ref, x_ref, t_ref, gp_ref, wup_hbm, wout_hbm,
             dg_ref, dya_ref, dyb_ref, dym_ref, dy_ref, loss_ref, ggp_ref, mb_ref, dob_ref, du_ref,
             wup_vm, wout_vm, sems):
        i = pl.program_id(0)
        _load_once([(wup_hbm.at[d], wup_vm.at[:, pl.ds(128 * d, 128)]) for d in range(N_DEV)]
                   + [(wout_hbm, wout_vm)], sems)

        @pl.when(i == 0)
        def _():
            loss_ref[...] = jnp.zeros_like(loss_ref)
            ggp_ref[...] = jnp.zeros_like(ggp_ref)

        ys = (ya_ref[...], yb_ref[...], ym_ref[...])
        us = [_dot(ys[k], wup_vm[512 * k:512 * k + 512, :]) for k in range(3)]
        gates = [_sigmoid(pg_ref[:, 1024 * k:1024 * k + 1024].astype(F32)) for k in range(3)]
        merged = gates[0] * us[0] + gates[1] * us[1] + gates[2] * us[2]
        mb = merged.astype(BF16)
        mb_ref[...] = mb
        out = _dot(mb, wout_vm[...])
        r = lax.rsqrt(jnp.mean(out * out, axis=-1, keepdims=True) + EPS)
        on = out * r
        gp = gp_ref[...]
        err = (x_ref[...] + on * gp) - t_ref[...]
        loss_ref[...] += 0.5 * jnp.sum(err * err) * (1.0 / D_MODEL)
        dy = err * (1.0 / D_MODEL)
        dy_ref[...] = dy
        ggp_ref[...] += jnp.sum(dy * on, axis=0, keepdims=True)
        a = dy * gp
        d_out = r * (a - on * jnp.mean(a * on, axis=-1, keepdims=True))
        dob = d_out.astype(BF16)
        dob_ref[...] = dob
        d_merged = _dot_nt(dob, wout_vm[...])
        d_refs = (dya_ref, dyb_ref, dym_ref)
        for k in range(3):
            g = gates[k]
            dg_ref[:, 1024 * k:1024 * k + 1024] = (d_merged * us[k] * g * (1.0 - g)).astype(BF16)
            du = (d_merged * g).astype(BF16)
            du_ref[k] = du
            d_refs[k][...] = _dot_nt(du, wup_vm[512 * k:512 * k + 512, :]).astype(BF16)

    return pl.pallas_call(
        body, name="mid", grid=(nt,),
        out_shape=[jax.ShapeDtypeStruct((s, 3072), BF16)] + [jax.ShapeDtypeStruct((s, 512), BF16)] * 3
        + [jax.ShapeDtypeStruct((s, D_MODEL), F32), jax.ShapeDtypeStruct((8, 128), F32),
           jax.ShapeDtypeStruct((1, D_MODEL), F32), jax.ShapeDtypeStruct((s, D_MODEL), BF16),
           jax.ShapeDtypeStruct((s, D_MODEL), BF16), jax.ShapeDtypeStruct((3, s, D_MODEL), BF16)],
        in_specs=[_rows(tm, 512)] * 3 + [_rows(tm, 3072), _rows(tm, D_MODEL), _rows(tm, D_MODEL),
                                         _full((1, D_MODEL)), ANY, ANY],
        out_specs=[_rows(tm, 3072)] + [_rows(tm, 512)] * 3
        + [_rows(tm, D_MODEL), _full((8, 128)), _full((1, D_MODEL)), _rows(tm, D_MODEL), _rows(tm, D_MODEL),
           pl.BlockSpec((3, tm, D_MODEL), lambda i: (0, i, 0))],
        scratch_shapes=[pltpu.VMEM((1536, D_MODEL), BF16), pltpu.VMEM((D_MODEL, D_MODEL), BF16),
                        pltpu.SemaphoreType.DMA((N_DEV + 1,))],
        compiler_params=_params(56),
    )(ya, yb, ym, pg, x, target, g_post, w_up, w_out)


def _gw_mid(mb, dob, ys, du):
    s = mb.shape[0]
    tn = 256

    def out_body(mb_ref, dob_ref, o_ref):
        o_ref[...] = _dot_tn(mb_ref[...], dob_ref[...]).astype(BF16)

    gw_out = pl.pallas_call(
        out_body, name="gw_out", grid=(D_MODEL // tn,),
        out_shape=jax.ShapeDtypeStruct((D_MODEL, D_MODEL), BF16),
        in_specs=[pl.BlockSpec((s, tn), lambda j: (0, j)), _full((s, D_MODEL))],
        out_specs=pl.BlockSpec((tn, D_MODEL), lambda j: (j, 0)),
        compiler_params=_params(48),
    )(mb, dob)

    per = 512 // tn

    def up_body(ya_ref, yb_ref, ym_ref, du_ref, o_ref):
        j = pl.program_id(0)
        for k, y_ref in enumerate((ya_ref, yb_ref, ym_ref)):
            @pl.when(j // per == k)
            def _(y_ref=y_ref):
                res = _dot_tn(y_ref[...], du_ref[...])
                for d in range(N_DEV):
                    o_ref[d] = res[:, 128 * d:128 * d + 128].astype(BF16)

    def y_spec(k):
        return pl.BlockSpec((s, tn), lambda j: (0, jnp.clip(j - per * k, 0, per - 1)))

    gw_up = pl.pallas_call(
        up_body, name="gw_up", grid=(3 * per,),
        out_shape=jax.ShapeDtypeStruct((N_DEV, 1536, 128), BF16),
        in_specs=[y_spec(0), y_spec(1), y_spec(2), pl.BlockSpec((None, s, D_MODEL), lambda j: (j // per, 0, 0))],
        out_specs=pl.BlockSpec((N_DEV, tn, 128), lambda j: (0, j, 0)),
        compiler_params=_params(48),
    )(*ys, du)
    return gw_out, gw_up


def _conv_bwd(pa, dya, w_conv):
    s = pa.shape[0]
    tm = min(512, s)
    nt = s // tm

    def body(pa_ref, pp_ref, pn_ref, d_ref, dp_ref, dn_ref, w_ref, da_ref, gw_ref):
        i = pl.program_id(0)
        first, last = i == 0, i == nt - 1

        @pl.when(first)
        def _():
            gw_ref[...] = jnp.zeros_like(gw_ref)

        w = w_ref[...]
        prev_row = pp_ref[...].astype(F32)[15:16, :]
        next_row = pn_ref[...].astype(F32)[0:1, :]
        b, c, u, z, cu, cu_m1, cu_p1, y, sig, row = _conv_common(
            pa_ref[...].astype(F32), prev_row, next_row, w, first, last, tm)
        sz = z * sig
        dya_t = d_ref[...].astype(F32)
        d_y = dya_t * b * sz

        def halo_dy(p_row, d_row):
            zz = p_row[:, 1536:2048]
            return d_row * p_row[:, 0:512] * (zz * _sigmoid(zz))

        dy_prev = jnp.where(first, 0.0, halo_dy(prev_row, dp_ref[...].astype(F32)[15:16, :]))
        dy_next = jnp.where(last, 0.0, halo_dy(next_row, dn_ref[...].astype(F32)[0:1, :]))
        dy_m1 = jnp.where(row == 0, dy_prev, pltpu.roll(d_y, 1, 0))
        dy_p1 = jnp.where(row == tm - 1, dy_next, pltpu.roll(d_y, tm - 1, 0))
        d_cu = dy_p1 * w[0:1] + d_y * w[1:2] + dy_m1 * w[2:3]
        da_ref[:, 0:512] = (dya_t * y * sz).astype(BF16)
        da_ref[:, 512:1024] = (d_cu * u).astype(BF16)
        da_ref[:, 1024:1536] = (d_cu * c).astype(BF16)
        da_ref[:, 1536:2048] = (dya_t * b * y * (sig * (1.0 + z * (1.0 - sig)))).astype(BF16)
        gw_ref[0:1, :] += jnp.sum(d_y * cu_m1, axis=0, keepdims=True)
        gw_ref[1:2, :] += jnp.sum(d_y * cu, axis=0, keepdims=True)
        gw_ref[2:3, :] += jnp.sum(d_y * cu_p1, axis=0, keepdims=True)

    prev, nxt = _halo_specs(s, tm, 16, 2048)
    dprev, dnxt = _halo_specs(s, tm, 16, 512)
    return pl.pallas_call(
        body, name="conv_bwd", grid=(nt,),
        out_shape=[jax.ShapeDtypeStruct((s, 2048), BF16), jax.ShapeDtypeStruct((8, 512), F32)],
        in_specs=[_rows(tm, 2048), prev, nxt, _rows(tm, 512), dprev, dnxt, _full((3, 512))],
        out_specs=[_rows(tm, 2048), _full((8, 512))],
        compiler_params=_params(48),
    )(pa, pa, pa, dya, dya, dya, w_conv)


def _attn_bwd(pq, pkv, pbz, dyb, sink, tabs):
    s = pq.shape[0]
    nb = s // ATTN_BLOCK

    def body(sink_ref, q_ref, z_ref, d_ref, kv_ref, t_ref,
             dq_ref, dz_ref, dkv_ref, gs_ref, kpad, vpad, dk_acc, dv_acc, bias, e_scr, ds_scr):
        i = pl.program_id(0)

        @pl.when(i == 0)
        def _():
            _fill_padded(kv_ref, kpad, vpad, s)
            _fill_band_bias(bias, nb)
            dk_acc[...] = jnp.zeros_like(dk_acc)
            dv_acc[...] = jnp.zeros_like(dv_acc)
            gs_ref[...] = jnp.zeros_like(gs_ref)

        row = lax.broadcasted_iota(jnp.int32, (ATTN_BLOCK, 128), 0)
        for b in range(BLOCKS_PER_STEP):
            n = i * BLOCKS_PER_STEP + b
            rows = slice(b * ATTN_BLOCK, (b + 1) * ATTN_BLOCK)
            start = pl.multiple_of(n * ATTN_BLOCK, ATTN_BLOCK)
            kw, vw = kpad[pl.ds(start, WINDOW_KEYS), :], vpad[pl.ds(start, WINDOW_KEYS), :]
            qf = q_ref[rows, :].astype(F32)
            variant = _bias_variant(n, nb)
            z = z_ref[rows, :].astype(F32)
            sig = _sigmoid(z)
            dyb_t = d_ref[rows, :].astype(F32)
            d_attn = dyb_t * (z * sig)
            outs, dqs = [], []
            dk_w = jnp.zeros((WINDOW_KEYS, 128), F32)
            dv_w = jnp.zeros((WINDOW_KEYS, 128), F32)
            for g in range(2):
                e_bg, ds_bg = e_scr.at[2 * b + g], ds_scr.at[2 * b + g]
                qt = _heads_to_lanes(qf, g, row)
                inv, p_sink = _softmax_keys_major(
                    _dot(kw, qt.astype(BF16)), bias, variant, _sink_row(sink_ref, g), e_bg)
                ot = _dot_tn(vw, e_bg[...]) * inv
                outs.append(ot)
                dot_ = _heads_to_lanes(d_attn, g, row)
                delta = jnp.sum(dot_ * ot, axis=0, keepdims=True)
                dpt = _dot(vw, dot_.astype(BF16))
                for k in range(WINDOW_KEYS // KEY_CHUNK):
                    rw = slice(k * KEY_CHUNK, (k + 1) * KEY_CHUNK)
                    ds_bg[rw, :] = (e_bg[rw, :].astype(F32) * (dpt[rw] - delta)).astype(BF16)
                sink_part = p_sink * delta
                for j in range(4):
                    h = 4 * g + j
                    gs_ref[h:h + 1, :] -= jnp.sum(sink_part[:, 128 * j:128 * j + 128])
                dqs.append(_dot_tn(kw, ds_bg[...]) * (inv * ATTN_SCALE))
                dk_w += _dot_nt(ds_bg[...], (qt * inv).astype(BF16)) * ATTN_SCALE
                dv_w += _dot_nt(e_bg[...], (dot_ * inv).astype(BF16))
            dk_acc[pl.ds(start, WINDOW_KEYS), :] += dk_w
            dv_acc[pl.ds(start, WINDOW_KEYS), :] += dv_w
            attn = _lanes_to_heads(outs[0], outs[1], row)
            dz_ref[rows, :] = (dyb_t * attn * (sig * (1.0 + z * (1.0 - sig)))).astype(BF16)
            dq = _lanes_to_heads(dqs[0], dqs[1], row)
            trows = pl.ds(start, ATTN_BLOCK)
            cs, s1, s2 = t_ref[0, trows, :], t_ref[1, trows, :], t_ref[2, trows, :]
            for blk in range(4):
                cols = slice(128 * blk, 128 * blk + 128)
                dq_ref[rows, cols] = _rope_t(dq[:, cols], cs, s1, s2).astype(BF16)

        @pl.when(i == nb // BLOCKS_PER_STEP - 1)
        def _():
            dk = dk_acc[ATTN_BLOCK:ATTN_BLOCK + s, :]
            dkv_ref[:, 0:128] = _rope_t(dk, t_ref[0], t_ref[1], t_ref[2]).astype(BF16)
            dkv_ref[:, 128:256] = dv_acc[ATTN_BLOCK:ATTN_BLOCK + s, :].astype(BF16)

    tq = BLOCKS_PER_STEP * ATTN_BLOCK
    tile = _rows(tq, 512)
    return pl.pallas_call(
        body, name="attn_bwd", grid=(s // tq,),
        out_shape=[jax.ShapeDtypeStruct((s, 512), BF16), jax.ShapeDtypeStruct((s, 512), BF16),
                   jax.ShapeDtypeStruct((s, 256), BF16), jax.ShapeDtypeStruct((8, 128), F32)],
        in_specs=[pl.BlockSpec(memory_space=pltpu.SMEM), tile, tile, tile, _full((s, 256)), _full((3, s, 128))],
        out_specs=[tile, tile, _full((s, 256)), _full((8, 128))],
        scratch_shapes=[pltpu.VMEM((s + 2 * ATTN_BLOCK, 128), BF16)] * 2
        + [pltpu.VMEM((s + 2 * ATTN_BLOCK, 128), F32)] * 2
        + [pltpu.VMEM((3, WINDOW_KEYS, STACKED), F32)]
        + [pltpu.VMEM((2 * BLOCKS_PER_STEP, WINDOW_KEYS, STACKED), BF16)] * 2,
        compiler_params=_params(48),
    )(sink, pq, pbz, dyb, pkv, tabs)


def _mem_attn_bwd(pmq, pmz, mkv, dym):
    s = pmq.shape[0]
    m = mkv.shape[0]
    tm = min(512, s)

    def body(q_ref, z_ref, d_ref, mk_ref, mv_ref, dq_ref, dz_ref, dmkv_ref):
        @pl.when(pl.program_id(0) == 0)
        def _():
            dmkv_ref[...] = jnp.zeros_like(dmkv_ref)

        z = z_ref[...].astype(F32)
        sig = _sigmoid(z)
        dym_t = d_ref[...].astype(F32)
        d_attn = dym_t * (z * sig)
        dsilu = sig * (1.0 + z * (1.0 - sig))
        for h in range(MEM_HEADS):
            cols = slice(128 * h, 128 * h + 128)
            q, mk, mv = q_ref[:, cols], mk_ref[:, cols], mv_ref[:, cols]
            pt = _mem_softmax_t(q, mk)
            pb = pt.astype(BF16)
            o = _dot_tn(pb, mv)
            dob = d_attn[:, cols].astype(BF16)
            dpt = _dot_nt(mv, dob)
            dst = (pt * (dpt - jnp.sum(pt * dpt, axis=0, keepdims=True))).astype(BF16)
            dq_ref[:, cols] = (_dot_tn(dst, mk) * MEM_SCALE).astype(BF16)
            dz_ref[:, cols] = (dym_t[:, cols] * o * dsilu[:, cols]).astype(BF16)
            dmkv_ref[:, cols] += _dot(dst, q) * MEM_SCALE
            dmkv_ref[:, 512 + 128 * h:512 + 128 * h + 128] += _dot(pb, dob)

    return pl.pallas_call(
        body, name="mem_attn_bwd", grid=(s // tm,),
        out_shape=[jax.ShapeDtypeStruct((s, 512), BF16), jax.ShapeDtypeStruct((s, 512), BF16),
                   jax.ShapeDtypeStruct((m, D_MODEL), F32)],
        in_specs=[_rows(tm, 512), _rows(tm, 512), _rows(tm, 512), pl.BlockSpec((m, 512), lambda i: (0, 0)),
                  pl.BlockSpec((m, 512), lambda i: (0, 1))],
        out_specs=[_rows(tm, 512), _rows(tm, 512), _full((m, D_MODEL))],
        compiler_params=_params(32),
    )(pmq, pmz, dym, mkv, mkv)


def _mem_kv_bwd(mem, g_mem, mn, dmkv, w_mkv):
    m = mem.shape[0]

    def body(mem_ref, g_ref, mn_ref, d_ref, w_ref, gw_ref, gg_ref):
        db = d_ref[...].astype(BF16)
        gw_ref[...] = _dot_tn(mn_ref[...], db).astype(BF16)
        d_mn = _dot_nt(db, w_ref[...])
        xf = mem_ref[...]
        r = lax.rsqrt(jnp.mean(xf * xf, axis=-1, keepdims=True) + EPS)
        gg_ref[...] = jnp.sum(d_mn * (xf * r), axis=0, keepdims=True)

    return pl.pallas_call(
        body, name="mem_kv_bwd", grid=(1,),
        out_shape=[jax.ShapeDtypeStruct((D_MODEL, D_MODEL), BF16), jax.ShapeDtypeStruct((1, D_MODEL), F32)],
        in_specs=[_full((m, D_MODEL)), _full((1, D_MODEL)), _full((m, D_MODEL)), _full((m, D_MODEL)),
                  _full((D_MODEL, D_MODEL))],
        out_specs=[_full((D_MODEL, D_MODEL)), _full((1, D_MODEL))],
        compiler_params=_params(32),
    )(mem, g_mem, mn, dmkv, w_mkv)


def _dh_bwd(dparts, x, dy, g_pre, w_int):
    s = x.shape[0]
    tm = min(256, s)

    def body(*refs):
        d_refs = refs[:7]
        x_ref, dy_ref, g_ref, w_hbm, gx_ref, gg_ref, w_vm, sems = refs[7:]
        _load_once([(w_hbm, w_vm)], sems)

        @pl.when(pl.program_id(0) == 0)
        def _():
            gg_ref[...] = jnp.zeros_like(gg_ref)

        d_h = jnp.zeros((tm, D_MODEL), F32)
        for d_ref, (r0, width) in zip(d_refs, SEGS):
            for c0 in range(0, width, 512):
                cw = min(512, width - c0)
                d_h += _dot(d_ref[:, c0:c0 + cw], w_vm[r0 + c0:r0 + c0 + cw, :])
        xf = x_ref[...]
        r = lax.rsqrt(jnp.mean(xf * xf, axis=-1, keepdims=True) + EPS)
        xn = xf * r
        a = d_h * g_ref[...]
        gx_ref[...] = r * (a - xn * jnp.mean(a * xn, axis=-1, keepdims=True)) + dy_ref[...]
        gg_ref[...] += jnp.sum(d_h * xn, axis=0, keepdims=True)

    return pl.pallas_call(
        body, name="dh_bwd", grid=(s // tm,),
        out_shape=[jax.ShapeDtypeStruct((s, D_MODEL), F32), jax.ShapeDtypeStruct((1, D_MODEL), F32)],
        in_specs=[_rows(tm, w) for _, w in SEGS] + [_rows(tm, D_MODEL), _rows(tm, D_MODEL), _full((1, D_MODEL)), ANY],
        out_specs=[_rows(tm, D_MODEL), _full((1, D_MODEL))],
        scratch_shapes=[pltpu.VMEM((IN_WIDTH, D_MODEL), BF16), pltpu.SemaphoreType.DMA((1,))],
        compiler_params=_params(52),
    )(*dparts, x, dy, g_pre, w_int)


def _gw_in(dparts, h):
    s = h.shape[0]
    tn = 256
    starts, counts = [], []
    for r0, width in SEGS:
        starts.append(r0 // tn)
        counts.append(width // tn)

    def body(*refs):
        d_refs = refs[:7]
        h_hbm, o_ref, h_vm, sems = refs[7:]
        _load_once([(h_hbm, h_vm)], sems)
        j = pl.program_id(0)
        for d_ref, st, cnt in zip(d_refs, starts, counts):
            @pl.when((j >= st) & (j < st + cnt))
            def _(d_ref=d_ref):
                o_ref[...] = _dot_tn(d_ref[...], h_vm[...]).astype(BF16)

    def seg_spec(st, cnt):
        return pl.BlockSpec((s, tn), lambda j: (0, jnp.clip(j - st, 0, cnt - 1)))

    return pl.pallas_call(
        body, name="gw_in", grid=(IN_WIDTH // tn,),
        out_shape=jax.ShapeDtypeStruct((IN_WIDTH, D_MODEL), BF16),
        in_specs=[seg_spec(st, cnt) for st, cnt in zip(starts, counts)] + [ANY],
        out_specs=pl.BlockSpec((tn, D_MODEL), lambda j: (j, 0)),
        scratch_shapes=[pltpu.VMEM((s, D_MODEL), BF16), pltpu.SemaphoreType.DMA((1,))],
        compiler_params=_params(52),
    )(*dparts, h)


def _adamw_math(w, g, m, v):
    m2 = ADAM_B1 * m + (1.0 - ADAM_B1) * g
    v2 = ADAM_B2 * v + (1.0 - ADAM_B2) * (g * g)
    m_hat = m2 / (1.0 - ADAM_B1 ** ADAM_STEP)
    v_hat = v2 / (1.0 - ADAM_B2 ** ADAM_STEP)
    delta = -ADAM_LR * (m_hat / (jnp.sqrt(v_hat) + ADAM_EPS) + ADAM_WD * w)
    return delta, m2, v2


def _sum_adamw(own, land, chip, block, w, m, v, name, tiles=1):
    r, c = w.shape
    rt = r // tiles

    def body(c_ref, own_ref, l1_ref, l2_ref, l3_ref, w_ref, m_ref, v_ref, g_ref, d_ref, m2_ref, v2_ref):
        g = own_ref[...].astype(F32)
        for l_ref in (l1_ref, l2_ref, l3_ref):
            g += l_ref[...].astype(F32)
        g_ref[...] = g
        d_ref[...], m2_ref[...], v2_ref[...] = _adamw_math(w_ref[...], g, m_ref[...], v_ref[...])

    def share(k):
        return pl.BlockSpec((None, rt, c), lambda i, c_ref: (jnp.bitwise_xor(c_ref[0], k), block * tiles + i, 0))

    spec = pl.BlockSpec((rt, c), lambda i, c_ref: (i, 0))
    grid_spec = pltpu.PrefetchScalarGridSpec(
        num_scalar_prefetch=1, grid=(tiles,),
        in_specs=[share(0), share(1), share(2), share(3)] + [spec] * 3, out_specs=[spec] * 4)
    return pl.pallas_call(
        body, name=name, grid_spec=grid_spec,
        out_shape=[jax.ShapeDtypeStruct((r, c), F32)] * 4,
        compiler_params=_params(48),
    )(chip, own, land, land, land, w, m, v)


def _sum_adamw_group(items, chip, name):
    k = len(items)

    def body(c_ref, *refs):
        shares, wmv, outs = refs[:4 * k], refs[4 * k:7 * k], refs[7 * k:]
        for j in range(k):
            g = shares[4 * j][...].astype(F32)
            for l_ref in shares[4 * j + 1:4 * j + 4]:
                g += l_ref[...].astype(F32)
            outs[4 * j][...] = g
            outs[4 * j + 1][...], outs[4 * j + 2][...], outs[4 * j + 3][...] = _adamw_math(
                wmv[3 * j][...], g, wmv[3 * j + 1][...], wmv[3 * j + 2][...])

    def share(shape, block, q):
        return pl.BlockSpec((None,) + shape, lambda i, c_ref: (jnp.bitwise_xor(c_ref[0], q), block, 0))

    in_specs, args = [], []
    for own, land, block, w, m, v in items:
        in_specs += [share(w.shape, block, q) for q in range(4)]
        args += [own, land, land, land]
    for own, land, block, w, m, v in items:
        in_specs += [pl.BlockSpec(w.shape, lambda i, c_ref: (0, 0))] * 3
        args += [w, m, v]
    out_specs = [pl.BlockSpec(w.shape, lambda i, c_ref: (0, 0)) for _, _, _, w, _, _ in items for _ in range(4)]
    res = pl.pallas_call(
        body, name=name,
        grid_spec=pltpu.PrefetchScalarGridSpec(num_scalar_prefetch=1, grid=(1,), in_specs=in_specs,
                                               out_specs=out_specs),
        out_shape=[jax.ShapeDtypeStruct(w.shape, F32) for _, _, _, w, _, _ in items for _ in range(4)],
        compiler_params=_params(48),
    )(chip, *args)
    return [res[4 * j:4 * j + 4] for j in range(k)]


def _small_step(parts, ws, ms, vs):
    def exchange(gpre_ref, gconv_ref, gsink_ref, gmem_ref, gpost_ref, loss_ref, tot_ref,
                 pack, gathered, send_sems, recv_sems):
        x, y, c = _my_place()
        me_idx = 4 * x + 2 * y + c

        lane = lax.broadcasted_iota(jnp.int32, (1, 128), 1)
        sink_row = jnp.zeros((1, 128), F32)
        for h in range(8):
            sink_row = jnp.where(lane == h, gsink_ref[h:h + 1, :], sink_row)
        pack[...] = jnp.zeros_like(pack)
        pack[0:1, :] = gpre_ref[...]
        pack[1:2, :] = gmem_ref[...]
        pack[2:3, :] = gpost_ref[...]
        pack[3:6, 0:512] = gconv_ref[0:3, :]
        pack[6:7, 0:128] = sink_row
        pack[7:8, 0:128] = loss_ref[0:1, :]

        flips = [(0, 0, 1), (0, 1, 0), (1, 0, 0), (0, 1, 1), (1, 0, 1), (1, 1, 0), (1, 1, 1)]
        cps = []
        for k, (fx, fy, fc) in enumerate(flips):
            peer = ((1 - x) if fx else x, (1 - y) if fy else y, (1 - c) if fc else c)
            cps.append(pltpu.make_async_remote_copy(
                src_ref=pack, dst_ref=gathered.at[me_idx], send_sem=send_sems.at[k], recv_sem=recv_sems.at[k],
                device_id=peer, device_id_type=MESH))
        for cp in cps:
            cp.start()
        gathered[me_idx] = pack[...]
        for cp in cps:
            cp.wait_recv()
        for cp in cps:
            cp.wait_send()
        tot = gathered[0]
        for d in range(1, N_DEV):
            tot = tot + gathered[d]
        tot_ref[...] = tot

    tot = pl.pallas_call(
        exchange, name="small_exchange", grid=(1,),
        out_shape=jax.ShapeDtypeStruct((8, D_MODEL), F32),
        in_specs=[_full(p.shape) for p in parts], out_specs=_full((8, D_MODEL)),
        scratch_shapes=[pltpu.VMEM((8, D_MODEL), F32), pltpu.VMEM((N_DEV, 8, D_MODEL), F32),
                        pltpu.SemaphoreType.DMA((N_PEERS,)), pltpu.SemaphoreType.DMA((N_PEERS,))],
    )(*parts)

    def apply(tot_ref, *refs):
        w_refs, m_refs, v_refs = refs[0:5], refs[5:10], refs[10:15]
        loss_out = refs[15]
        g_outs, d_outs, m_outs, v_outs = refs[16:21], refs[21:26], refs[26:31], refs[31:36]
        x, y, c = _my_place()
        tot = tot_ref[...]
        conv = pltpu.roll(tot[:, 0:512], (512 - 64 * (4 * x + 2 * y + c)) % 512, 1)[3:6, 0:64]
        grads = (tot[0:1, :], conv, tot[6:7, 0:8], tot[1:2, :], tot[2:3, :])
        loss_out[...] = tot[7:8, 0:128]
        for j in range(5):
            g_outs[j][...] = grads[j]
            d_outs[j][...], m_outs[j][...], v_outs[j][...] = _adamw_math(
                w_refs[j][...], grads[j], m_refs[j][...], v_refs[j][...])

    specs = [_full(w.shape) for w in ws]
    res = pl.pallas_call(
        apply, name="small_apply", grid=(1,),
        out_shape=[jax.ShapeDtypeStruct((1, 128), F32)] + [jax.ShapeDtypeStruct(w.shape, F32) for w in ws] * 4,
        in_specs=[_full((8, D_MODEL))] + specs * 3,
        out_specs=[_full((1, 128))] + specs * 4,
    )(tot, *ws, *ms, *vs)
    return res[0], res[1:6], res[6:11], res[11:16], res[16:21]


def kernel(x, mem, g_pre, w_in, w_conv, attn_sink, g_mem, w_mem_kv, w_up_a, w_up_b, w_up_m, w_out, g_post, loss_target, m_g_pre, m_w_in, m_w_conv, m_attn_sink, m_g_mem, m_w_mem_kv, m_w_up_a, m_w_up_b, m_w_up_m, m_w_out, m_g_post, v_g_pre, v_w_in, v_w_conv, v_attn_sink, v_g_mem, v_w_mem_kv, v_w_up_a, v_w_up_b, v_w_up_m, v_w_out, v_g_post):
    s = x.shape[1]
    x2, mem2, tgt2 = x[0], mem[0], loss_target[0]
    me = 4 * lax.axis_index("x") + 2 * lax.axis_index("y") + lax.axis_index("c")

    w_up_loc = jnp.concatenate([w_up_a[0], w_up_b[0], w_up_m[0]], axis=0).astype(BF16)
    w_conv_loc = jnp.zeros((8, 128), F32).at[:3, :64].set(w_conv[0])
    w_int_g, w_conv_g = _all_gather([w_in[0].T.astype(BF16), w_conv_loc], "gather_w_in",
                                    splits=[[(112 * k, 112) for k in range(7)] + [(784, 144)], [(0, 8)]])
    w_int = w_int_g.reshape(IN_WIDTH, D_MODEL)
    w_conv_f = w_conv_g[:, :3, :64].transpose(1, 0, 2).reshape(3, 512)
    late = _gather_start([w_mem_kv[0].astype(BF16) + w_conv_g[0, 7:8, 0:1].astype(BF16),
                          w_out[0].astype(BF16), w_up_loc], me, "gather_late_start")
    sink = attn_sink[0]
    tabs = jnp.stack(_rope_tables(s))

    h, pa, pq, pkv, pbz, pmq, pmz, pg = _proj_fwd(x2, g_pre + late[4][0:1, 0:1], w_int, tabs)
    ya = _conv_fwd(pa, w_conv_f)
    yb = _attn_fwd(pq, pkv, pbz, sink)
    w_mkv_g, w_out_g, w_up_g = _gather_wait(*late[:4], yb, "gather_late_wait")
    w_mkv = w_mkv_g.reshape(D_MODEL, D_MODEL)
    w_out_f = w_out_g.reshape(D_MODEL, D_MODEL)
    mn, mkv = _mem_kv_fwd(mem2, g_mem, w_mkv)
    ym = _mem_attn_fwd(pmq, pmz, mkv)
    dg, dya, dyb, dym, dy, loss_p, gg_post, mb, dob, du = _mid(ya, yb, ym, pg, x2, tgt2, g_post, w_up_g, w_out_f)
    gw_out, gw_up = _gw_mid(mb, dob, (ya, yb, ym), du)

    core = lax.axis_index("c").astype(jnp.int32).reshape(1)
    chip = (2 * lax.axis_index("x") + lax.axis_index("y")).astype(jnp.int32).reshape(1)

    def exchange_start(shares, tag):
        from_sibling = _sibling_exchange(shares, "grads_to_sibling_" + tag)
        chip_shares = _pair_add(shares, from_sibling, core, "grads_pair_add_" + tag)
        return _chip_exchange_start(chip_shares, "grads_to_chips_start_" + tag)

    dmq, dmz, dmkv = _mem_attn_bwd(pmq, pmz, mkv, dym)
    gw_mkv, gg_mem = _mem_kv_bwd(mem2, g_mem, mn, dmkv, w_mkv)
    send1, recv1, srcs1, lands1, token1 = exchange_start(
        [gw_mkv.reshape(N_DEV, 128, D_MODEL), gw_out.reshape(N_DEV, 128, D_MODEL), gw_up], "small")
    da, gw_conv = _conv_bwd(pa, dya, w_conv_f + token1[0:1, 0:1])
    dq, dbz, dkv, g_sink = _attn_bwd(pq, pkv, pbz, dyb, sink, tabs)
    dparts = (da, dq, dkv, dbz, dmq, dmz, dg)
    gw_int = _gw_in(dparts, h)
    send2, recv2, srcs2, lands2, token2 = exchange_start([gw_int.reshape(N_DEV, SHARD_IN, D_MODEL)], "w_in")
    grad_x, gg_pre = _dh_bwd(dparts, x2, dy, g_pre + token2[0:1, 0:1], w_int)
    (o_mkv, o_out, o_up, o_int), (l_mkv, l_out, l_up, l_int) = _chip_exchange_wait(
        send1 + send2, recv1 + recv2, srcs1 + srcs2, lands1 + lands2, grad_x, "grads_to_chips_wait")

    loss_row, small_g, sd, sm, sv = _small_step(
        (gg_pre, gw_conv, g_sink, gg_mem, gg_post, loss_p),
        [g_pre, w_conv[0], attn_sink, g_mem, g_post],
        [m_g_pre, m_w_conv[0], m_attn_sink, m_g_mem, m_g_post],
        [v_g_pre, v_w_conv[0], v_attn_sink, v_g_mem, v_g_post])
    loss = loss_row[0, 0]
    g_g_pre, g_conv, g_sink_tot, g_g_mem, g_g_post = small_g

    g_w_in, d_w_in, nm_w_in, nv_w_in = (t.T for t in _sum_adamw(
        o_int, l_int, chip, 0, w_in[0].T, m_w_in[0].T, v_w_in[0].T, "adamw_w_in", tiles=2))
    (g_mkv, d_mkv, nm_mkv, nv_mkv), (g_out, d_out, nm_out, nv_out), *up = _sum_adamw_group(
        [(o_mkv, l_mkv, 0, w_mem_kv[0], m_w_mem_kv[0], v_w_mem_kv[0]),
         (o_out, l_out, 0, w_out[0], m_w_out[0], v_w_out[0]),
         (o_up, l_up, 0, w_up_a[0], m_w_up_a[0], v_w_up_a[0]),
         (o_up, l_up, 1, w_up_b[0], m_w_up_b[0], v_w_up_b[0]),
         (o_up, l_up, 2, w_up_m[0], m_w_up_m[0], v_w_up_m[0])], chip, "adamw_mid_weights")

    def lead(a):
        return a[None]

    grads = [g_g_pre, lead(g_w_in), lead(g_conv), g_sink_tot, g_g_mem, lead(g_mkv), lead(up[0][0]),
             lead(up[1][0]), lead(up[2][0]), lead(g_out), g_g_post]

    def assemble(small, big_in, big_mkv, big_up, big_out):
        return [small[0], lead(big_in), lead(small[1]), small[2], small[3], lead(big_mkv), lead(big_up[0]),
                lead(big_up[1]), lead(big_up[2]), lead(big_out), small[4]]

    deltas = assemble(sd, d_w_in, d_mkv, [u[1] for u in up], d_out)
    new_m = assemble(sm, nm_w_in, nm_mkv, [u[2] for u in up], nm_out)
    new_v = assemble(sv, nv_w_in, nv_mkv, [u[3] for u in up], nv_out)
    return (loss, grad_x[None], *grads, *deltas, *new_m, *new_v)
```

```python
import functools

import jax
import jax.numpy as jnp
from jax import lax
from jax.experimental import pallas as pl
from jax.experimental.pallas import tpu as pltpu

F32 = jnp.float32
BF16 = jnp.bfloat16
MESH = pl.DeviceIdType.MESH

N_DEV = 8
D_MODEL = 1024
EPS = 1e-6
ROPE_THETA = 500000.0
ROT_DIM = 16
HEAD_DIM = 64
ATTN_BLOCK = 128
MEM_HEADS = 4
MEM_HEAD_DIM = 128
ATTN_SCALE = HEAD_DIM ** -0.5
MEM_SCALE = MEM_HEAD_DIM ** -0.5

ADAM_LR = 0.001
ADAM_B1 = 0.9
ADAM_B2 = 0.999
ADAM_EPS = 1e-08
ADAM_WD = 0.01
ADAM_STEP = 10

SEG_A = (0, 2048)
SEG_BQ = (2048, 512)
SEG_BKV = (2560, 256)
SEG_BZ = (2816, 512)
SEG_MQ = (3328, 512)
SEG_MZ = (3840, 512)
SEG_G = (4352, 3072)
SEGS = (SEG_A, SEG_BQ, SEG_BKV, SEG_BZ, SEG_MQ, SEG_MZ, SEG_G)
IN_WIDTH = 7424
SHARD_IN = IN_WIDTH // N_DEV

V7X_VMEM_BYTES = 64 * 1024 * 1024
ANY = pl.BlockSpec(memory_space=pl.ANY)


def _params(vmem_mb):
    assert vmem_mb * 1024 * 1024 < V7X_VMEM_BYTES
    return pltpu.CompilerParams(dimension_semantics=("arbitrary",), vmem_limit_bytes=vmem_mb * 1024 * 1024)


def _full(shape):
    zeros = (0,) * len(shape)
    return pl.BlockSpec(shape, lambda i: zeros)


def _rows(tm, width):
    return pl.BlockSpec((tm, width), lambda i: (i, 0))


def _dot(a, b):
    return jnp.dot(a, b, preferred_element_type=F32)


def _dot_nt(a, b):
    return lax.dot_general(a, b, (((1,), (1,)), ((), ())), preferred_element_type=F32)


def _dot_tn(a, b):
    return lax.dot_general(a, b, (((0,), (0,)), ((), ())), preferred_element_type=F32)


def _sigmoid(z):
    return 1.0 / (1.0 + jnp.exp(-z))


def _rope(t, cs, s1, s2):
    return t * cs + pltpu.roll(t, 120, 1) * s1 + pltpu.roll(t, 8, 1) * s2


def _rope_t(d, cs, s1, s2):
    return d * cs + pltpu.roll(d * s1, 8, 1) + pltpu.roll(d * s2, 120, 1)


def _rope_tables(s):
    half = ROT_DIM // 2
    inv_freq = jnp.power(jnp.float32(ROPE_THETA), -jnp.arange(half, dtype=F32) * (2.0 / ROT_DIM))
    ang = jnp.arange(s).astype(F32)[:, None] * inv_freq[None, :]
    cos, sin = jnp.cos(ang), jnp.sin(ang)
    d = jnp.arange(128) % HEAD_DIM
    k = jnp.arange(half)[:, None]
    lo = (d[None, :] == k).astype(F32)
    hi = (d[None, :] == k + half).astype(F32)
    spread = functools.partial(jnp.dot, precision=lax.Precision.HIGHEST)
    return (spread(cos, lo + hi) + (d >= ROT_DIM).astype(F32)[None, :], -spread(sin, lo), spread(sin, hi))


def _load_once(pairs, sems):
    @pl.when(pl.program_id(0) == 0)
    def _():
        cps = [pltpu.make_async_copy(src, dst, sems.at[k]) for k, (src, dst) in enumerate(pairs)]
        for cp in cps:
            cp.start()
        for cp in cps:
            cp.wait()


def _my_place():
    x, y, c = lax.axis_index("x"), lax.axis_index("y"), lax.axis_index("c")
    return x, y, c


def _all_gather(arrs, name, splits=None):
    n = len(arrs)
    if splits is None:
        splits = [[(0, a.shape[0])] for a in arrs]
    pieces = [(a, r0, rn) for a in range(n) for r0, rn in splits[a]]
    n_p = len(pieces)

    def body(*refs):
        ins, outs = refs[:n], refs[n:2 * n]
        send_sems, recv_sems, local_sems = refs[2 * n:]
        x, y, c = _my_place()
        me, sibling = (x, y, c), (x, y, 1 - c)

        def route(core):
            first = (jnp.bitwise_xor(x, 1 - core), jnp.bitwise_xor(y, core), core)
            second = (jnp.bitwise_xor(x, core), jnp.bitwise_xor(y, 1 - core), core)
            return first, second, (1 - x, 1 - y, core)

        def idx(px, py, pc):
            return 4 * px + 2 * py + pc

        def copy(p, k, block, to, own=False):
            a, r0, rn = pieces[p]
            dst = outs[a].at[idx(*block), pl.ds(r0, rn)]
            return pltpu.make_async_remote_copy(
                src_ref=ins[a].at[pl.ds(r0, rn)] if own else dst, dst_ref=dst,
                send_sem=send_sems.at[p * 7 + k], recv_sem=recv_sems.at[p * 7 + k],
                device_id=to, device_id_type=MESH)

        nbr1, nbr2, diag = route(c)
        mine = [pltpu.make_async_copy(ins[a], outs[a].at[idx(*me)], local_sems.at[a]) for a in range(n)]
        for cp in mine:
            cp.start()
        sent = []
        for p in range(n_p):
            for k, to in enumerate((sibling, nbr1, nbr2)):
                sent.append(copy(p, k, me, to, own=True))
        for cp in sent:
            cp.start()
        for k_in, block, onward in ((1, nbr1, ((3, nbr2), (4, sibling))), (2, nbr2, ((5, sibling),)),
                                    (3, diag, ((6, sibling),))):
            for p in range(n_p):
                copy(p, k_in, block, me).wait_recv()
                for k_out, to in onward:
                    cp = copy(p, k_out, block, to)
                    cp.start()
                    sent.append(cp)
        s1, s2, sd = route(1 - c)
        for k_in, block in ((0, sibling), (4, s1), (5, s2), (6, sd)):
            for p in range(n_p):
                copy(p, k_in, block, me).wait_recv()
        for cp in sent:
            cp.wait_send()
        for cp in mine:
            cp.wait()

    return pl.pallas_call(
        body, name=name,
        out_shape=[jax.ShapeDtypeStruct((N_DEV,) + a.shape, a.dtype) for a in arrs],
        in_specs=[ANY] * n, out_specs=[ANY] * n,
        scratch_shapes=[pltpu.SemaphoreType.DMA((7 * n_p,)), pltpu.SemaphoreType.DMA((7 * n_p,)),
                        pltpu.SemaphoreType.DMA((n,))],
    )(*arrs)


N_CHIPS = 4


def _sibling_exchange(arrs, name):
    n = len(arrs)

    def body(*refs):
        ins, outs = refs[:n], refs[n:2 * n]
        send_sems, recv_sems = refs[2 * n:]
        x, y, c = _my_place()
        sibling = (x, y, 1 - c)

        def copy(a, j):
            return pltpu.make_async_remote_copy(
                src_ref=ins[a].at[2 * j + (1 - c)], dst_ref=outs[a].at[j],
                send_sem=send_sems.at[a * N_CHIPS + j], recv_sem=recv_sems.at[a * N_CHIPS + j],
                device_id=sibling, device_id_type=MESH)

        cps = [copy(a, j) for j in range(N_CHIPS) for a in range(n)]
        for cp in cps:
            cp.start()
        for cp in cps:
            cp.wait_recv()
        for cp in cps:
            cp.wait_send()

    return pl.pallas_call(
        body, name=name,
        out_shape=[jax.ShapeDtypeStruct((N_CHIPS,) + a.shape[1:], a.dtype) for a in arrs],
        in_specs=[ANY] * n, out_specs=[ANY] * n,
        scratch_shapes=[pltpu.SemaphoreType.DMA((N_CHIPS * n,)), pltpu.SemaphoreType.DMA((N_CHIPS * n,))],
    )(*arrs)


def _pair_add(mine, recv, core, name):
    n = len(mine)

    def body(c_ref, *refs):
        for a in range(n):
            refs[2 * n + a][...] = (refs[a][...].astype(F32) + refs[n + a][...].astype(F32)).astype(BF16)

    def blk(a):
        return (None,) + a.shape[1:]

    grid_spec = pltpu.PrefetchScalarGridSpec(
        num_scalar_prefetch=1, grid=(N_CHIPS,),
        in_specs=[pl.BlockSpec(blk(a), lambda j, c_ref: (2 * j + c_ref[0], 0, 0)) for a in mine]
        + [pl.BlockSpec(blk(a), lambda j, c_ref: (j, 0, 0)) for a in recv],
        out_specs=[pl.BlockSpec(blk(a), lambda j, c_ref: (j, 0, 0)) for a in recv])
    return pl.pallas_call(
        body, name=name, grid_spec=grid_spec,
        out_shape=[jax.ShapeDtypeStruct(a.shape, BF16) for a in recv],
        compiler_params=_params(32),
    )(core, *mine, *recv)


HBM = pl.BlockSpec(memory_space=pltpu.HBM)
SEM = pl.BlockSpec(memory_space=pltpu.SEMAPHORE)
N_PEER_CHIPS = 3


def _chip_copies(srcs, lands, send_sems, recv_sems):
    x, y, c = _my_place()
    my_chip = 2 * x + y
    peers = [(x, 1 - y), (1 - x, y), (1 - x, 1 - y)]
    cps = []
    for k, (px, py) in enumerate(peers):
        for a in range(len(srcs)):
            j = a * N_PEER_CHIPS + k
            cps.append(pltpu.make_async_remote_copy(
                src_ref=srcs[a].at[2 * px + py], dst_ref=lands[a].at[my_chip],
                send_sem=send_sems[j], recv_sem=recv_sems[j],
                device_id=(px, py, c), device_id_type=MESH))
    return cps


N_PEERS = N_DEV - 1


def _gather_copies(srcs, lands, send_sems, recv_sems):
    x, y, c = _my_place()
    me_idx = 4 * x + 2 * y + c
    flips = [(0, 0, 1), (0, 1, 0), (1, 0, 0), (0, 1, 1), (1, 0, 1), (1, 1, 0), (1, 1, 1)]
    cps = []
    for k, (fx, fy, fc) in enumerate(flips):
        peer = ((1 - x) if fx else x, (1 - y) if fy else y, (1 - c) if fc else c)
        for a in range(len(srcs)):
            j = a * N_PEERS + k
            cps.append(pltpu.make_async_remote_copy(
                src_ref=srcs[a], dst_ref=lands[a].at[me_idx], send_sem=send_sems[j], recv_sem=recv_sems[j],
                device_id=peer, device_id_type=MESH))
    return cps


def _split_start(copies, per_array, arrs, lands, name):
    arrs, lands = list(arrs), list(lands)
    n = len(arrs)
    k = n * per_array

    def body(*refs):
        srcs, land_refs = refs[:n], refs[n:2 * n]
        send_sems, recv_sems = refs[2 * n:2 * n + k], refs[2 * n + k:2 * n + 2 * k]
        token = refs[-1]
        for cp in copies(srcs, land_refs, send_sems, recv_sems):
            cp.start()
        token[...] = jnp.zeros_like(token)

    hbm_arrs = [pltpu.with_memory_space_constraint(a, pltpu.HBM) for a in arrs]
    lands = [pltpu.with_memory_space_constraint(a, pltpu.HBM) for a in lands]
    res = pl.pallas_call(
        body, name=name,
        out_shape=[pltpu.SemaphoreType.DMA(())] * (2 * k) + [pltpu.HBM(a.shape, a.dtype) for a in arrs + lands]
        + [jax.ShapeDtypeStruct((8, 128), F32)],
        in_specs=[HBM] * (2 * n),
        out_specs=[SEM] * (2 * k) + [HBM] * (2 * n) + [pl.BlockSpec(memory_space=pltpu.VMEM)],
        input_output_aliases={a: 2 * k + a for a in range(2 * n)},
        compiler_params=pltpu.CompilerParams(has_side_effects=pltpu.SideEffectType.DATAFLOW_SIDE_EFFECTING),
    )(*hbm_arrs, *lands)
    return res[:k], res[k:2 * k], res[2 * k:2 * k + n], res[2 * k + n:2 * k + 2 * n], res[-1]


def _split_wait(copies, per_array, send_sems, recv_sems, srcs, lands, after, name):
    n = len(srcs)
    k = n * per_array

    def body(*refs):
        src_refs, land_refs = refs[:n], refs[n:2 * n]
        s_sems, r_sems = refs[2 * n:2 * n + k], refs[2 * n + k:2 * n + 2 * k]
        for cp in copies(src_refs, land_refs, s_sems, r_sems):
            cp.wait_send()
            cp.wait_recv()

    res = pl.pallas_call(
        body, name=name,
        out_shape=[pltpu.HBM(a.shape, a.dtype) for a in list(srcs) + list(lands)],
        in_specs=[HBM] * (2 * n) + [SEM] * (2 * k) + [ANY],
        out_specs=[HBM] * (2 * n),
        input_output_aliases={a: a for a in range(2 * n)},
        compiler_params=pltpu.CompilerParams(has_side_effects=pltpu.SideEffectType.DATAFLOW_SIDE_EFFECTING),
    )(*srcs, *lands, *send_sems, *recv_sems, after)
    return res[:n], res[n:]


def _chip_exchange_start(arrs, name):
    return _split_start(_chip_copies, N_PEER_CHIPS, arrs, [lax.empty(a.shape, a.dtype) for a in arrs], name)


def _chip_exchange_wait(send_sems, recv_sems, srcs, lands, after, name):
    return _split_wait(_chip_copies, N_PEER_CHIPS, send_sems, recv_sems, srcs, lands, after, name)


def _gather_start(arrs, me_idx, name):
    lands = [lax.dynamic_update_slice(lax.empty((N_DEV,) + a.shape, a.dtype), a[None], (me_idx, 0, 0)) for a in arrs]
    return _split_start(_gather_copies, N_PEERS, arrs, lands, name)


def _gather_wait(send_sems, recv_sems, srcs, lands, after, name):
    return _split_wait(_gather_copies, N_PEERS, send_sems, recv_sems, srcs, lands, after, name)[1]


def _proj_fwd(x, g_pre, w_int, tabs, w_conv):
    s = x.shape[0]
    tm = min(512, s)
    nt = s // tm

    def body(x_ref, xp_ref, xn_ref, g_ref, t_ref, wc_ref, w_hbm,
             h_ref, pa_ref, pq_ref, pkv_ref, pbz_ref, pmq_ref, pmz_ref, pg_ref, ya_ref, w_vm, sems):
        _load_once([(w_hbm, w_vm)], sems)
        i = pl.program_id(0)

        def normed(xf):
            r = lax.rsqrt(jnp.mean(xf * xf, axis=-1, keepdims=True) + EPS)
            return ((xf * r) * g_ref[...]).astype(BF16)

        h = normed(x_ref[...])
        h_ref[...] = h
        cs, s1, s2 = t_ref[0], t_ref[1], t_ref[2]

        def mm(seg, c0, width):
            return _dot_nt(h, w_vm[seg[0] + c0:seg[0] + c0 + width, :])

        pa = [mm(SEG_A, c0, 512).astype(BF16) for c0 in range(0, SEG_A[1], 512)]
        for k in range(4):
            pa_ref[:, 512 * k:512 * k + 512] = pa[k]
        h_halo = normed(jnp.concatenate([xp_ref[...], xn_ref[...]], axis=0))
        cu_halo = (_dot_nt(h_halo, w_vm[512:1024, :]).astype(BF16).astype(F32)
                   * _dot_nt(h_halo, w_vm[1024:1536, :]).astype(BF16).astype(F32))
        b, _, _, z, _, _, _, y, sig, _ = _conv_common(
            jnp.concatenate([p.astype(F32) for p in pa], axis=1),
            jnp.where(i == 0, 0.0, cu_halo[7:8, :]), jnp.where(i == nt - 1, 0.0, cu_halo[8:9, :]), wc_ref[...], tm)
        ya_ref[...] = (b * y * (z * sig)).astype(BF16)
        q = mm(SEG_BQ, 0, 512)
        for b in range(4):
            pq_ref[:, 128 * b:128 * b + 128] = _rope(q[:, 128 * b:128 * b + 128], cs, s1, s2).astype(BF16)
        kv = mm(SEG_BKV, 0, 256)
        pkv_ref[:, 0:128] = _rope(kv[:, 0:128], cs, s1, s2).astype(BF16)
        pkv_ref[:, 128:256] = kv[:, 128:256].astype(BF16)
        pbz_ref[...] = mm(SEG_BZ, 0, 512).astype(BF16)
        pmq_ref[...] = mm(SEG_MQ, 0, 512).astype(BF16)
        pmz_ref[...] = mm(SEG_MZ, 0, 512).astype(BF16)
        for c0 in range(0, SEG_G[1], 512):
            pg_ref[:, c0:c0 + 512] = mm(SEG_G, c0, 512).astype(BF16)

    widths = (D_MODEL, 2048, 512, 256, 512, 512, 512, 3072, 512)
    x_prev, x_next = _halo_specs(s, tm, 8, D_MODEL)
    return pl.pallas_call(
        body, name="proj_fwd", grid=(nt,),
        out_shape=[jax.ShapeDtypeStruct((s, w), BF16) for w in widths],
        in_specs=[_rows(tm, D_MODEL), x_prev, x_next, _full((1, D_MODEL)),
                  pl.BlockSpec((3, tm, 128), lambda i: (0, i, 0)), _full((3, 512)), ANY],
        out_specs=[_rows(tm, w) for w in widths],
        scratch_shapes=[pltpu.VMEM((IN_WIDTH, D_MODEL), BF16), pltpu.SemaphoreType.DMA((1,))],
        compiler_params=_params(56),
    )(x, x, x, g_pre, tabs, w_conv, w_int)


def _mem_kv_fwd(mem, g_mem, w_mkv):
    m = mem.shape[0]

    def body(mem_ref, g_ref, w_ref, mn_ref, mkv_ref):
        xf = mem_ref[...]
        r = lax.rsqrt(jnp.mean(xf * xf, axis=-1, keepdims=True) + EPS)
        mn = ((xf * r) * g_ref[...]).astype(BF16)
        mn_ref[...] = mn
        mkv_ref[...] = _dot(mn, w_ref[...]).astype(BF16)

    return pl.pallas_call(
        body, name="mem_kv_fwd", grid=(1,),
        out_shape=[jax.ShapeDtypeStruct((m, D_MODEL), BF16)] * 2,
        in_specs=[_full((m, D_MODEL)), _full((1, D_MODEL)), _full((D_MODEL, D_MODEL))],
        out_specs=[_full((m, D_MODEL))] * 2,
        compiler_params=_params(32),
    )(mem, g_mem, w_mkv)


def _halo_specs(s, tm, rows, width):
    nblk = s // rows
    prev = pl.BlockSpec((rows, width), lambda i: (jnp.maximum(i * (tm // rows) - 1, 0), 0))
    nxt = pl.BlockSpec((rows, width), lambda i: (jnp.minimum((i + 1) * (tm // rows), nblk - 1), 0))
    return prev, nxt


def _conv_common(pa, cu_prev, cu_next, w, tm):
    b, c, u, z = (pa[:, 512 * k:512 * k + 512] for k in range(4))
    cu = c * u
    row = lax.broadcasted_iota(jnp.int32, (tm, 512), 0)
    cu_m1 = jnp.where(row == 0, cu_prev, pltpu.roll(cu, 1, 0))
    cu_p1 = jnp.where(row == tm - 1, cu_next, pltpu.roll(cu, tm - 1, 0))
    y = cu_m1 * w[0:1] + cu * w[1:2] + cu_p1 * w[2:3]
    sig = _sigmoid(z)
    return b, c, u, z, cu, cu_m1, cu_p1, y, sig, row


def _heads_to_lanes(a, g, row):
    low = row < HEAD_DIM
    parts = []
    for b in (2 * g, 2 * g + 1):
        t = jnp.transpose(a[:, 128 * b:128 * b + 128])
        swapped = pltpu.roll(t, HEAD_DIM, 0)
        if g == 0:
            parts += [jnp.where(low, t, 0.0), jnp.where(low, swapped, 0.0)]
        else:
            parts += [jnp.where(low, 0.0, swapped), jnp.where(low, 0.0, t)]
    return jnp.concatenate(parts, axis=1)


def _lanes_to_heads(t0, t1, row):
    low = row < HEAD_DIM
    blocks = []
    for b in range(4):
        g = b // 2
        tg = (t0, t1)[g]
        je = 2 * (b - 2 * g)
        even, odd = tg[:, 128 * je:128 * je + 128], tg[:, 128 * je + 128:128 * je + 256]
        if g == 0:
            t = jnp.where(low, even, pltpu.roll(odd, HEAD_DIM, 0))
        else:
            t = jnp.where(low, pltpu.roll(even, HEAD_DIM, 0), odd)
        blocks.append(jnp.transpose(t))
    return jnp.concatenate(blocks, axis=1)


WINDOW_KEYS = 3 * ATTN_BLOCK
STACKED = 4 * ATTN_BLOCK
KEY_CHUNK = 32
BLOCKS_PER_STEP = 4


def _fill_band_bias(bias, nb):
    assert nb >= 2
    c = lax.broadcasted_iota(jnp.int32, (WINDOW_KEYS, STACKED), 0)
    r = lax.broadcasted_iota(jnp.int32, (WINDOW_KEYS, STACKED), 1) & (ATTN_BLOCK - 1)
    band = (c >= r) & (c <= r + 2 * ATTN_BLOCK)
    for v, ok in enumerate((band, band & (c >= ATTN_BLOCK), band & (c < 2 * ATTN_BLOCK))):
        bias[v] = jnp.where(ok, 0.0, -jnp.inf)


def _bias_variant(n, nb):
    return jnp.where(n == 0, 1, jnp.where(n == nb - 1, 2, 0))


def _sink_row(sink_ref, g):
    return jnp.concatenate([jnp.full((1, ATTN_BLOCK), sink_ref[4 * g + j], F32) for j in range(4)], axis=1)


def _softmax_keys_major(sc, bias, variant, sink, e_scr):
    chunks = [pl.ds(k * KEY_CHUNK, KEY_CHUNK) for k in range(WINDOW_KEYS // KEY_CHUNK)]
    rows = [slice(k * KEY_CHUNK, (k + 1) * KEY_CHUNK) for k in range(WINDOW_KEYS // KEY_CHUNK)]
    m_run = jnp.full((KEY_CHUNK, STACKED), -jnp.inf, F32)
    for ck, rw in zip(chunks, rows):
        m_run = jnp.maximum(m_run, sc[rw] * ATTN_SCALE + bias[variant, ck, :])
    m = jnp.maximum(jnp.max(m_run, axis=0, keepdims=True), sink)
    l_run = jnp.zeros((KEY_CHUNK, STACKED), F32)
    for ck, rw in zip(chunks, rows):
        e = jnp.exp(sc[rw] * ATTN_SCALE + bias[variant, ck, :] - m)
        l_run += e
        e_scr[rw, :] = e.astype(BF16)
    es = jnp.exp(sink - m)
    inv = 1.0 / (jnp.sum(l_run, axis=0, keepdims=True) + es)
    return inv, es * inv


def _fill_padded(kv_ref, kpad, vpad, s):
    zero = jnp.zeros((ATTN_BLOCK, 128), BF16)
    kpad[0:ATTN_BLOCK, :] = zero
    vpad[0:ATTN_BLOCK, :] = zero
    kpad[ATTN_BLOCK + s:2 * ATTN_BLOCK + s, :] = zero
    vpad[ATTN_BLOCK + s:2 * ATTN_BLOCK + s, :] = zero
    kpad[ATTN_BLOCK:ATTN_BLOCK + s, :] = kv_ref[:, 0:128]
    vpad[ATTN_BLOCK:ATTN_BLOCK + s, :] = kv_ref[:, 128:256]


def _attn_fwd(pq, pkv, pbz, sink):
    s = pq.shape[0]
    nb = s // ATTN_BLOCK

    def body(sink_ref, q_ref, z_ref, kv_ref, yb_ref, kpad, vpad, bias, e_scr):
        i = pl.program_id(0)

        @pl.when(i == 0)
        def _():
            _fill_padded(kv_ref, kpad, vpad, s)
            _fill_band_bias(bias, nb)

        row = lax.broadcasted_iota(jnp.int32, (ATTN_BLOCK, 128), 0)
        for b in range(BLOCKS_PER_STEP):
            n = i * BLOCKS_PER_STEP + b
            rows = slice(b * ATTN_BLOCK, (b + 1) * ATTN_BLOCK)
            start = pl.multiple_of(n * ATTN_BLOCK, ATTN_BLOCK)
            kw, vw = kpad[pl.ds(start, WINDOW_KEYS), :], vpad[pl.ds(start, WINDOW_KEYS), :]
            qf = q_ref[rows, :].astype(F32)
            variant = _bias_variant(n, nb)
            outs = []
            for g in range(2):
                e_bg = e_scr.at[2 * b + g]
                qt = _heads_to_lanes(qf, g, row).astype(BF16)
                inv, _ = _softmax_keys_major(_dot(kw, qt), bias, variant, _sink_row(sink_ref, g), e_bg)
                outs.append(_dot_tn(vw, e_bg[...]) * inv)
            attn = _lanes_to_heads(outs[0], outs[1], row)
            z = z_ref[rows, :].astype(F32)
            yb_ref[rows, :] = (attn * (z * _sigmoid(z))).astype(BF16)

    tq = BLOCKS_PER_STEP * ATTN_BLOCK
    return pl.pallas_call(
        body, name="attn_fwd", grid=(s // tq,),
        out_shape=jax.ShapeDtypeStruct((s, 512), BF16),
        in_specs=[pl.BlockSpec(memory_space=pltpu.SMEM), _rows(tq, 512), _rows(tq, 512), _full((s, 256))],
        out_specs=_rows(tq, 512),
        scratch_shapes=[pltpu.VMEM((s + 2 * ATTN_BLOCK, 128), BF16)] * 2
        + [pltpu.VMEM((3, WINDOW_KEYS, STACKED), F32),
           pltpu.VMEM((2 * BLOCKS_PER_STEP, WINDOW_KEYS, STACKED), BF16)],
        compiler_params=_params(32),
    )(sink, pq, pbz, pkv)


def _mem_softmax_t(q, mk):
    sc = _dot_nt(mk, q) * MEM_SCALE
    e = jnp.exp(sc - jnp.max(sc, axis=0, keepdims=True))
    return e * (1.0 / jnp.sum(e, axis=0, keepdims=True))


def _mem_attn_fwd(pmq, pmz, mkv):
    s = pmq.shape[0]
    m = mkv.shape[0]
    tm = min(512, s)

    def body(q_ref, z_ref, mk_ref, mv_ref, ym_ref):
        z = z_ref[...].astype(F32)
        sz = z * _sigmoid(z)
        for h in range(MEM_HEADS):
            cols = slice(128 * h, 128 * h + 128)
            pt = _mem_softmax_t(q_ref[:, cols], mk_ref[:, cols])
            o = _dot_tn(pt.astype(BF16), mv_ref[:, cols])
            ym_ref[:, cols] = (o * sz[:, cols]).astype(BF16)

    return pl.pallas_call(
        body, name="mem_attn_fwd", grid=(s // tm,),
        out_shape=jax.ShapeDtypeStruct((s, 512), BF16),
        in_specs=[_rows(tm, 512), _rows(tm, 512), pl.BlockSpec((m, 512), lambda i: (0, 0)),
                  pl.BlockSpec((m, 512), lambda i: (0, 1))],
        out_specs=_rows(tm, 512),
        compiler_params=_params(32),
    )(pmq, pmz, mkv, mkv)


def _mid(ya, yb, ym, pg, x, target, g_post, w_up, w_out):
    s = x.shape[0]
    tm = min(256, s)
    nt = s // tm

    def body(ya_ref, yb_ref, ym_ref, pg_ref, x_ref, t_ref, gp_ref, wup_hbm, wout_hbm,
             dg_ref, dya_ref, dyb_ref, dym_ref, dy_ref, loss_ref, ggp_ref, mb_ref, dob_ref, du_ref,
             wup_vm, wout_vm, sems):
        i = pl.program_id(0)
        _load_once([(wup_hbm.at[d], wup_vm.at[:, pl.ds(128 * d, 128)]) for d in range(N_DEV)]
                   + [(wout_hbm, wout_vm)], sems)

        @pl.when(i == 0)
        def _():
            loss_ref[...] = jnp.zeros_like(loss_ref)
            ggp_ref[...] = jnp.zeros_like(ggp_ref)

        ys = (ya_ref[...], yb_ref[...], ym_ref[...])
        us = [_dot(ys[k], wup_vm[512 * k:512 * k + 512, :]) for k in range(3)]
        gates = [_sigmoid(pg_ref[:, 1024 * k:1024 * k + 1024].astype(F32)) for k in range(3)]
        merged = gates[0] * us[0] + gates[1] * us[1] + gates[2] * us[2]
        mb = merged.astype(BF16)
        mb_ref[...] = mb
        out = _dot(mb, wout_vm[...])
        r = lax.rsqrt(jnp.mean(out * out, axis=-1, keepdims=True) + EPS)
        on = out * r
        gp = gp_ref[...]
        err = (x_ref[...] + on * gp) - t_ref[...]
        loss_ref[...] += 0.5 * jnp.sum(err * err) * (1.0 / D_MODEL)
        dy = err * (1.0 / D_MODEL)
        dy_ref[...] = dy
        ggp_ref[...] += jnp.sum(dy * on, axis=0, keepdims=True)
        a = dy * gp
        d_out = r * (a - on * jnp.mean(a * on, axis=-1, keepdims=True))
        dob = d_out.astype(BF16)
        dob_ref[...] = dob
        d_merged = _dot_nt(dob, wout_vm[...])
        d_refs = (dya_ref, dyb_ref, dym_ref)
        for k in range(3):
            g = gates[k]
            dg_ref[:, 1024 * k:1024 * k + 1024] = (d_merged * us[k] * g * (1.0 - g)).astype(BF16)
            du = (d_merged * g).astype(BF16)
            du_ref[k] = du
            d_refs[k][...] = _dot_nt(du, wup_vm[512 * k:512 * k + 512, :]).astype(BF16)

    return pl.pallas_call(
        body, name="mid", grid=(nt,),
        out_shape=[jax.ShapeDtypeStruct((s, 3072), BF16)] + [jax.ShapeDtypeStruct((s, 512), BF16)] * 3
        + [jax.ShapeDtypeStruct((s, D_MODEL), F32), jax.ShapeDtypeStruct((8, 128), F32),
           jax.ShapeDtypeStruct((1, D_MODEL), F32), jax.ShapeDtypeStruct((s, D_MODEL), BF16),
           jax.ShapeDtypeStruct((s, D_MODEL), BF16), jax.ShapeDtypeStruct((3, s, D_MODEL), BF16)],
        in_specs=[_rows(tm, 512)] * 3 + [_rows(tm, 3072), _rows(tm, D_MODEL), _rows(tm, D_MODEL),
                                         _full((1, D_MODEL)), ANY, ANY],
        out_specs=[_rows(tm, 3072)] + [_rows(tm, 512)] * 3
        + [_rows(tm, D_MODEL), _full((8, 128)), _full((1, D_MODEL)), _rows(tm, D_MODEL), _rows(tm, D_MODEL),
           pl.BlockSpec((3, tm, D_MODEL), lambda i: (0, i, 0))],
        scratch_shapes=[pltpu.VMEM((1536, D_MODEL), BF16), pltpu.VMEM((D_MODEL, D_MODEL), BF16),
                        pltpu.SemaphoreType.DMA((N_DEV + 1,))],
        compiler_params=_params(56),
    )(ya, yb, ym, pg, x, target, g_post, w_up, w_out)


def _gw_mid(mb, dob, ys, du):
    s = mb.shape[0]
    tn = 256

    def out_body(mb_ref, dob_ref, o_ref):
        o_ref[...] = _dot_tn(mb_ref[...], dob_ref[...]).astype(BF16)

    gw_out = pl.pallas_call(
        out_body, name="gw_out", grid=(D_MODEL // tn,),
        out_shape=jax.ShapeDtypeStruct((D_MODEL, D_MODEL), BF16),
        in_specs=[pl.BlockSpec((s, tn), lambda j: (0, j)), _full((s, D_MODEL))],
        out_specs=pl.BlockSpec((tn, D_MODEL), lambda j: (j, 0)),
        compiler_params=_params(48),
    )(mb, dob)

    per = 512 // tn

    def up_body(ya_ref, yb_ref, ym_ref, du_ref, o_ref):
        j = pl.program_id(0)
        for k, y_ref in enumerate((ya_ref, yb_ref, ym_ref)):
            @pl.when(j // per == k)
            def _(y_ref=y_ref):
                res = _dot_tn(y_ref[...], du_ref[...])
                for d in range(N_DEV):
                    o_ref[d] = res[:, 128 * d:128 * d + 128].astype(BF16)

    def y_spec(k):
        return pl.BlockSpec((s, tn), lambda j: (0, jnp.clip(j - per * k, 0, per - 1)))

    gw_up = pl.pallas_call(
        up_body, name="gw_up", grid=(3 * per,),
        out_shape=jax.ShapeDtypeStruct((N_DEV, 1536, 128), BF16),
        in_specs=[y_spec(0), y_spec(1), y_spec(2), pl.BlockSpec((None, s, D_MODEL), lambda j: (j // per, 0, 0))],
        out_specs=pl.BlockSpec((N_DEV, tn, 128), lambda j: (0, j, 0)),
        compiler_params=_params(48),
    )(*ys, du)
    return gw_out, gw_up


def _conv_bwd(pa, dya, w_conv):
    s = pa.shape[0]
    tm = min(512, s)
    nt = s // tm

    def body(pa_ref, pp_ref, pn_ref, d_ref, dp_ref, dn_ref, w_ref, da_ref, gw_ref):
        i = pl.program_id(0)
        first, last = i == 0, i == nt - 1

        @pl.when(first)
        def _():
            gw_ref[...] = jnp.zeros_like(gw_ref)

        w = w_ref[...]
        prev_row = pp_ref[...].astype(F32)[15:16, :]
        next_row = pn_ref[...].astype(F32)[0:1, :]
        b, c, u, z, cu, cu_m1, cu_p1, y, sig, row = _conv_common(
            pa_ref[...].astype(F32),
            jnp.where(first, 0.0, prev_row[:, 512:1024] * prev_row[:, 1024:1536]),
            jnp.where(last, 0.0, next_row[:, 512:1024] * next_row[:, 1024:1536]), w, tm)
        sz = z * sig
        dya_t = d_ref[...].astype(F32)
        d_y = dya_t * b * sz

        def halo_dy(p_row, d_row):
            zz = p_row[:, 1536:2048]
            return d_row * p_row[:, 0:512] * (zz * _sigmoid(zz))

        dy_prev = jnp.where(first, 0.0, halo_dy(prev_row, dp_ref[...].astype(F32)[15:16, :]))
        dy_next = jnp.where(last, 0.0, halo_dy(next_row, dn_ref[...].astype(F32)[0:1, :]))
        dy_m1 = jnp.where(row == 0, dy_prev, pltpu.roll(d_y, 1, 0))
        dy_p1 = jnp.where(row == tm - 1, dy_next, pltpu.roll(d_y, tm - 1, 0))
        d_cu = dy_p1 * w[0:1] + d_y * w[1:2] + dy_m1 * w[2:3]
        da_ref[:, 0:512] = (dya_t * y * sz).astype(BF16)
        da_ref[:, 512:1024] = (d_cu * u).astype(BF16)
        da_ref[:, 1024:1536] = (d_cu * c).astype(BF16)
        da_ref[:, 1536:2048] = (dya_t * b * y * (sig * (1.0 + z * (1.0 - sig)))).astype(BF16)
        gw_ref[0:1, :] += jnp.sum(d_y * cu_m1, axis=0, keepdims=True)
        gw_ref[1:2, :] += jnp.sum(d_y * cu, axis=0, keepdims=True)
        gw_ref[2:3, :] += jnp.sum(d_y * cu_p1, axis=0, keepdims=True)

    prev, nxt = _halo_specs(s, tm, 16, 2048)
    dprev, dnxt = _halo_specs(s, tm, 16, 512)
    return pl.pallas_call(
        body, name="conv_bwd", grid=(nt,),
        out_shape=[jax.ShapeDtypeStruct((s, 2048), BF16), jax.ShapeDtypeStruct((8, 512), F32)],
        in_specs=[_rows(tm, 2048), prev, nxt, _rows(tm, 512), dprev, dnxt, _full((3, 512))],
        out_specs=[_rows(tm, 2048), _full((8, 512))],
        compiler_params=_params(48),
    )(pa, pa, pa, dya, dya, dya, w_conv)


def _attn_bwd(pq, pkv, pbz, dyb, sink, tabs):
    s = pq.shape[0]
    nb = s // ATTN_BLOCK

    def body(sink_ref, q_ref, z_ref, d_ref, kv_ref, t_ref,
             dq_ref, dz_ref, dkv_ref, gs_ref, kpad, vpad, dk_acc, dv_acc, bias, e_scr, ds_scr):
        i = pl.program_id(0)

        @pl.when(i == 0)
        def _():
            _fill_padded(kv_ref, kpad, vpad, s)
            _fill_band_bias(bias, nb)
            dk_acc[...] = jnp.zeros_like(dk_acc)
            dv_acc[...] = jnp.zeros_like(dv_acc)
            gs_ref[...] = jnp.zeros_like(gs_ref)

        row = lax.broadcasted_iota(jnp.int32, (ATTN_BLOCK, 128), 0)
        for b in range(BLOCKS_PER_STEP):
            n = i * BLOCKS_PER_STEP + b
            rows = slice(b * ATTN_BLOCK, (b + 1) * ATTN_BLOCK)
            start = pl.multiple_of(n * ATTN_BLOCK, ATTN_BLOCK)
            kw, vw = kpad[pl.ds(start, WINDOW_KEYS), :], vpad[pl.ds(start, WINDOW_KEYS), :]
            qf = q_ref[rows, :].astype(F32)
            variant = _bias_variant(n, nb)
            z = z_ref[rows, :].astype(F32)
            sig = _sigmoid(z)
            dyb_t = d_ref[rows, :].astype(F32)
            d_attn = dyb_t * (z * sig)
            outs, dqs = [], []
            dk_w = jnp.zeros((WINDOW_KEYS, 128), F32)
            dv_w = jnp.zeros((WINDOW_KEYS, 128), F32)
            for g in range(2):
                e_bg, ds_bg = e_scr.at[2 * b + g], ds_scr.at[2 * b + g]
                qt = _heads_to_lanes(qf, g, row)
                inv, p_sink = _softmax_keys_major(
                    _dot(kw, qt.astype(BF16)), bias, variant, _sink_row(sink_ref, g), e_bg)
                ot = _dot_tn(vw, e_bg[...]) * inv
                outs.append(ot)
                dot_ = _heads_to_lanes(d_attn, g, row)
                delta = jnp.sum(dot_ * ot, axis=0, keepdims=True)
                dpt = _dot(vw, dot_.astype(BF16))
                for k in range(WINDOW_KEYS // KEY_CHUNK):
                    rw = slice(k * KEY_CHUNK, (k + 1) * KEY_CHUNK)
                    ds_bg[rw, :] = (e_bg[rw, :].astype(F32) * (dpt[rw] - delta)).astype(BF16)
                sink_part = p_sink * delta
                for j in range(4):
                    h = 4 * g + j
                    gs_ref[h:h + 1, :] -= jnp.sum(sink_part[:, 128 * j:128 * j + 128])
                dqs.append(_dot_tn(kw, ds_bg[...]) * (inv * ATTN_SCALE))
                dk_w += _dot_nt(ds_bg[...], (qt * inv).astype(BF16)) * ATTN_SCALE
                dv_w += _dot_nt(e_bg[...], (dot_ * inv).astype(BF16))
            dk_acc[pl.ds(start, WINDOW_KEYS), :] += dk_w
            dv_acc[pl.ds(start, WINDOW_KEYS), :] += dv_w
            attn = _lanes_to_heads(outs[0], outs[1], row)
            dz_ref[rows, :] = (dyb_t * attn * (sig * (1.0 + z * (1.0 - sig)))).astype(BF16)
            dq = _lanes_to_heads(dqs[0], dqs[1], row)
            trows = pl.ds(start, ATTN_BLOCK)
            cs, s1, s2 = t_ref[0, trows, :], t_ref[1, trows, :], t_ref[2, trows, :]
            for blk in range(4):
                cols = slice(128 * blk, 128 * blk + 128)
                dq_ref[rows, cols] = _rope_t(dq[:, cols], cs, s1, s2).astype(BF16)

        @pl.when(i == nb // BLOCKS_PER_STEP - 1)
        def _():
            dk = dk_acc[ATTN_BLOCK:ATTN_BLOCK + s, :]
            dkv_ref[:, 0:128] = _rope_t(dk, t_ref[0], t_ref[1], t_ref[2]).astype(BF16)
            dkv_ref[:, 128:256] = dv_acc[ATTN_BLOCK:ATTN_BLOCK + s, :].astype(BF16)

    tq = BLOCKS_PER_STEP * ATTN_BLOCK
    tile = _rows(tq, 512)
    return pl.pallas_call(
        body, name="attn_bwd", grid=(s // tq,),
        out_shape=[jax.ShapeDtypeStruct((s, 512), BF16), jax.ShapeDtypeStruct((s, 512), BF16),
                   jax.ShapeDtypeStruct((s, 256), BF16), jax.ShapeDtypeStruct((8, 128), F32)],
        in_specs=[pl.BlockSpec(memory_space=pltpu.SMEM), tile, tile, tile, _full((s, 256)), _full((3, s, 128))],
        out_specs=[tile, tile, _full((s, 256)), _full((8, 128))],
        scratch_shapes=[pltpu.VMEM((s + 2 * ATTN_BLOCK, 128), BF16)] * 2
        + [pltpu.VMEM((s + 2 * ATTN_BLOCK, 128), F32)] * 2
        + [pltpu.VMEM((3, WINDOW_KEYS, STACKED), F32)]
        + [pltpu.VMEM((2 * BLOCKS_PER_STEP, WINDOW_KEYS, STACKED), BF16)] * 2,
        compiler_params=_params(48),
    )(sink, pq, pbz, dyb, pkv, tabs)


def _mem_attn_bwd(pmq, pmz, mkv, dym):
    s = pmq.shape[0]
    m = mkv.shape[0]
    tm = min(512, s)

    def body(q_ref, z_ref, d_ref, mk_ref, mv_ref, dq_ref, dz_ref, dmkv_ref):
        @pl.when(pl.program_id(0) == 0)
        def _():
            dmkv_ref[...] = jnp.zeros_like(dmkv_ref)

        z = z_ref[...].astype(F32)
        sig = _sigmoid(z)
        dym_t = d_ref[...].astype(F32)
        d_attn = dym_t * (z * sig)
        dsilu = sig * (1.0 + z * (1.0 - sig))
        for h in range(MEM_HEADS):
            cols = slice(128 * h, 128 * h + 128)
            q, mk, mv = q_ref[:, cols], mk_ref[:, cols], mv_ref[:, cols]
            pt = _mem_softmax_t(q, mk)
            pb = pt.astype(BF16)
            o = _dot_tn(pb, mv)
            dob = d_attn[:, cols].astype(BF16)
            dpt = _dot_nt(mv, dob)
            dst = (pt * (dpt - jnp.sum(pt * dpt, axis=0, keepdims=True))).astype(BF16)
            dq_ref[:, cols] = (_dot_tn(dst, mk) * MEM_SCALE).astype(BF16)
            dz_ref[:, cols] = (dym_t[:, cols] * o * dsilu[:, cols]).astype(BF16)
            dmkv_ref[:, cols] += _dot(dst, q) * MEM_SCALE
            dmkv_ref[:, 512 + 128 * h:512 + 128 * h + 128] += _dot(pb, dob)

    return pl.pallas_call(
        body, name="mem_attn_bwd", grid=(s // tm,),
        out_shape=[jax.ShapeDtypeStruct((s, 512), BF16), jax.ShapeDtypeStruct((s, 512), BF16),
                   jax.ShapeDtypeStruct((m, D_MODEL), F32)],
        in_specs=[_rows(tm, 512), _rows(tm, 512), _rows(tm, 512), pl.BlockSpec((m, 512), lambda i: (0, 0)),
                  pl.BlockSpec((m, 512), lambda i: (0, 1))],
        out_specs=[_rows(tm, 512), _rows(tm, 512), _full((m, D_MODEL))],
        compiler_params=_params(32),
    )(pmq, pmz, dym, mkv, mkv)


def _mem_kv_bwd(mem, g_mem, mn, dmkv, w_mkv):
    m = mem.shape[0]

    def body(mem_ref, g_ref, mn_ref, d_ref, w_ref, gw_ref, gg_ref):
        db = d_ref[...].astype(BF16)
        gw_ref[...] = _dot_tn(mn_ref[...], db).astype(BF16)
        d_mn = _dot_nt(db, w_ref[...])
        xf = mem_ref[...]
        r = lax.rsqrt(jnp.mean(xf * xf, axis=-1, keepdims=True) + EPS)
        gg_ref[...] = jnp.sum(d_mn * (xf * r), axis=0, keepdims=True)

    return pl.pallas_call(
        body, name="mem_kv_bwd", grid=(1,),
        out_shape=[jax.ShapeDtypeStruct((D_MODEL, D_MODEL), BF16), jax.ShapeDtypeStruct((1, D_MODEL), F32)],
        in_specs=[_full((m, D_MODEL)), _full((1, D_MODEL)), _full((m, D_MODEL)), _full((m, D_MODEL)),
                  _full((D_MODEL, D_MODEL))],
        out_specs=[_full((D_MODEL, D_MODEL)), _full((1, D_MODEL))],
        compiler_params=_params(32),
    )(mem, g_mem, mn, dmkv, w_mkv)


def _dh_bwd(dparts, x, dy, g_pre, w_int):
    s = x.shape[0]
    tm = min(512, s)

    def body(*refs):
        d_refs = refs[:7]
        x_ref, dy_ref, g_ref, w_hbm, gx_ref, gg_ref, w_vm, sems = refs[7:]
        _load_once([(w_hbm, w_vm)], sems)

        @pl.when(pl.program_id(0) == 0)
        def _():
            gg_ref[...] = jnp.zeros_like(gg_ref)

        d_h = jnp.zeros((tm, D_MODEL), F32)
        for d_ref, (r0, width) in zip(d_refs, SEGS):
            for c0 in range(0, width, 512):
                cw = min(512, width - c0)
                d_h += _dot(d_ref[:, c0:c0 + cw], w_vm[r0 + c0:r0 + c0 + cw, :])
        xf = x_ref[...]
        r = lax.rsqrt(jnp.mean(xf * xf, axis=-1, keepdims=True) + EPS)
        xn = xf * r
        a = d_h * g_ref[...]
        gx_ref[...] = r * (a - xn * jnp.mean(a * xn, axis=-1, keepdims=True)) + dy_ref[...]
        gg_ref[...] += jnp.sum(d_h * xn, axis=0, keepdims=True)

    return pl.pallas_call(
        body, name="dh_bwd", grid=(s // tm,),
        out_shape=[jax.ShapeDtypeStruct((s, D_MODEL), F32), jax.ShapeDtypeStruct((1, D_MODEL), F32)],
        in_specs=[_rows(tm, w) for _, w in SEGS] + [_rows(tm, D_MODEL), _rows(tm, D_MODEL), _full((1, D_MODEL)), ANY],
        out_specs=[_rows(tm, D_MODEL), _full((1, D_MODEL))],
        scratch_shapes=[pltpu.VMEM((IN_WIDTH, D_MODEL), BF16), pltpu.SemaphoreType.DMA((1,))],
        compiler_params=_params(52),
    )(*dparts, x, dy, g_pre, w_int)


def _gw_in(dparts, h):
    s = h.shape[0]
    tn = 256
    starts, counts = [], []
    for r0, width in SEGS:
        starts.append(r0 // tn)
        counts.append(width // tn)

    def body(*refs):
        d_refs = refs[:7]
        h_hbm, o_ref, h_vm, sems = refs[7:]
        _load_once([(h_hbm, h_vm)], sems)
        j = pl.program_id(0)
        for d_ref, st, cnt in zip(d_refs, starts, counts):
            @pl.when((j >= st) & (j < st + cnt))
            def _(d_ref=d_ref):
                o_ref[...] = _dot_tn(d_ref[...], h_vm[...]).astype(BF16)

    def seg_spec(st, cnt):
        return pl.BlockSpec((s, tn), lambda j: (0, jnp.clip(j - st, 0, cnt - 1)))

    return pl.pallas_call(
        body, name="gw_in", grid=(IN_WIDTH // tn,),
        out_shape=jax.ShapeDtypeStruct((IN_WIDTH, D_MODEL), BF16),
        in_specs=[seg_spec(st, cnt) for st, cnt in zip(starts, counts)] + [ANY],
        out_specs=pl.BlockSpec((tn, D_MODEL), lambda j: (j, 0)),
        scratch_shapes=[pltpu.VMEM((s, D_MODEL), BF16), pltpu.SemaphoreType.DMA((1,))],
        compiler_params=_params(52),
    )(*dparts, h)


def _adamw_math(w, g, m, v):
    m2 = ADAM_B1 * m + (1.0 - ADAM_B1) * g
    v2 = ADAM_B2 * v + (1.0 - ADAM_B2) * (g * g)
    m_hat = m2 / (1.0 - ADAM_B1 ** ADAM_STEP)
    v_hat = v2 / (1.0 - ADAM_B2 ** ADAM_STEP)
    delta = -ADAM_LR * (m_hat / (jnp.sqrt(v_hat) + ADAM_EPS) + ADAM_WD * w)
    return delta, m2, v2


def _sum_adamw(own, land, chip, block, w, m, v, name, tiles=1):
    r, c = w.shape
    rt = r // tiles

    def body(c_ref, own_ref, l1_ref, l2_ref, l3_ref, w_ref, m_ref, v_ref, g_ref, d_ref, m2_ref, v2_ref):
        g = own_ref[...].astype(F32)
        for l_ref in (l1_ref, l2_ref, l3_ref):
            g += l_ref[...].astype(F32)
        g_ref[...] = g
        d_ref[...], m2_ref[...], v2_ref[...] = _adamw_math(w_ref[...], g, m_ref[...], v_ref[...])

    def share(k):
        return pl.BlockSpec((None, rt, c), lambda i, c_ref: (jnp.bitwise_xor(c_ref[0], k), block * tiles + i, 0))

    spec = pl.BlockSpec((rt, c), lambda i, c_ref: (i, 0))
    grid_spec = pltpu.PrefetchScalarGridSpec(
        num_scalar_prefetch=1, grid=(tiles,),
        in_specs=[share(0), share(1), share(2), share(3)] + [spec] * 3, out_specs=[spec] * 4)
    return pl.pallas_call(
        body, name=name, grid_spec=grid_spec,
        out_shape=[jax.ShapeDtypeStruct((r, c), F32)] * 4,
        compiler_params=_params(48),
    )(chip, own, land, land, land, w, m, v)


def _sum_adamw_group(items, chip, name):
    k = len(items)

    def body(c_ref, *refs):
        shares, wmv, outs = refs[:4 * k], refs[4 * k:7 * k], refs[7 * k:]
        for j in range(k):
            g = shares[4 * j][...].astype(F32)
            for l_ref in shares[4 * j + 1:4 * j + 4]:
                g += l_ref[...].astype(F32)
            outs[4 * j][...] = g
            outs[4 * j + 1][...], outs[4 * j + 2][...], outs[4 * j + 3][...] = _adamw_math(
                wmv[3 * j][...], g, wmv[3 * j + 1][...], wmv[3 * j + 2][...])

    def share(shape, block, q):
        return pl.BlockSpec((None,) + shape, lambda i, c_ref: (jnp.bitwise_xor(c_ref[0], q), block, 0))

    in_specs, args = [], []
    for own, land, block, w, m, v in items:
        in_specs += [share(w.shape, block, q) for q in range(4)]
        args += [own, land, land, land]
    for own, land, block, w, m, v in items:
        in_specs += [pl.BlockSpec(w.shape, lambda i, c_ref: (0, 0))] * 3
        args += [w, m, v]
    out_specs = [pl.BlockSpec(w.shape, lambda i, c_ref: (0, 0)) for _, _, _, w, _, _ in items for _ in range(4)]
    res = pl.pallas_call(
        body, name=name,
        grid_spec=pltpu.PrefetchScalarGridSpec(num_scalar_prefetch=1, grid=(1,), in_specs=in_specs,
                                               out_specs=out_specs),
        out_shape=[jax.ShapeDtypeStruct(w.shape, F32) for _, _, _, w, _, _ in items for _ in range(4)],
        compiler_params=_params(48),
    )(chip, *args)
    return [res[4 * j:4 * j + 4] for j in range(k)]


def _small_step(parts, ws, ms, vs):
    def exchange(gpre_ref, gconv_ref, gsink_ref, gmem_ref, gpost_ref, loss_ref, tot_ref,
                 pack, gathered, send_sems, recv_sems):
        x, y, c = _my_place()
        me_idx = 4 * x + 2 * y + c

        lane = lax.broadcasted_iota(jnp.int32, (1, 128), 1)
        sink_row = jnp.zeros((1, 128), F32)
        for h in range(8):
            sink_row = jnp.where(lane == h, gsink_ref[h:h + 1, :], sink_row)
        pack[...] = jnp.zeros_like(pack)
        pack[0:1, :] = gpre_ref[...]
        pack[1:2, :] = gmem_ref[...]
        pack[2:3, :] = gpost_ref[...]
        pack[3:6, 0:512] = gconv_ref[0:3, :]
        pack[6:7, 0:128] = sink_row
        pack[7:8, 0:128] = loss_ref[0:1, :]

        flips = [(0, 0, 1), (0, 1, 0), (1, 0, 0), (0, 1, 1), (1, 0, 1), (1, 1, 0), (1, 1, 1)]
        cps = []
        for k, (fx, fy, fc) in enumerate(flips):
            peer = ((1 - x) if fx else x, (1 - y) if fy else y, (1 - c) if fc else c)
            cps.append(pltpu.make_async_remote_copy(
                src_ref=pack, dst_ref=gathered.at[me_idx], send_sem=send_sems.at[k], recv_sem=recv_sems.at[k],
                device_id=peer, device_id_type=MESH))
        for cp in cps:
            cp.start()
        gathered[me_idx] = pack[...]
        for cp in cps:
            cp.wait_recv()
        for cp in cps:
            cp.wait_send()
        tot = gathered[0]
        for d in range(1, N_DEV):
            tot = tot + gathered[d]
        tot_ref[...] = tot

    tot = pl.pallas_call(
        exchange, name="small_exchange", grid=(1,),
        out_shape=jax.ShapeDtypeStruct((8, D_MODEL), F32),
        in_specs=[_full(p.shape) for p in parts], out_specs=_full((8, D_MODEL)),
        scratch_shapes=[pltpu.VMEM((8, D_MODEL), F32), pltpu.VMEM((N_DEV, 8, D_MODEL), F32),
                        pltpu.SemaphoreType.DMA((N_PEERS,)), pltpu.SemaphoreType.DMA((N_PEERS,))],
    )(*parts)

    def apply(tot_ref, *refs):
        w_refs, m_refs, v_refs = refs[0:5], refs[5:10], refs[10:15]
        loss_out = refs[15]
        g_outs, d_outs, m_outs, v_outs = refs[16:21], refs[21:26], refs[26:31], refs[31:36]
        x, y, c = _my_place()
        tot = tot_ref[...]
        conv = pltpu.roll(tot[:, 0:512], (512 - 64 * (4 * x + 2 * y + c)) % 512, 1)[3:6, 0:64]
        grads = (tot[0:1, :], conv, tot[6:7, 0:8], tot[1:2, :], tot[2:3, :])
        loss_out[...] = tot[7:8, 0:128]
        for j in range(5):
            g_outs[j][...] = grads[j]
            d_outs[j][...], m_outs[j][...], v_outs[j][...] = _adamw_math(
                w_refs[j][...], grads[j], m_refs[j][...], v_refs[j][...])

    specs = [_full(w.shape) for w in ws]
    res = pl.pallas_call(
        apply, name="small_apply", grid=(1,),
        out_shape=[jax.ShapeDtypeStruct((1, 128), F32)] + [jax.ShapeDtypeStruct(w.shape, F32) for w in ws] * 4,
        in_specs=[_full((8, D_MODEL))] + specs * 3,
        out_specs=[_full((1, 128))] + specs * 4,
    )(tot, *ws, *ms, *vs)
    return res[0], res[1:6], res[6:11], res[11:16], res[16:21]


def kernel(x, mem, g_pre, w_in, w_conv, attn_sink, g_mem, w_mem_kv, w_up_a, w_up_b, w_up_m, w_out, g_post, loss_target, m_g_pre, m_w_in, m_w_conv, m_attn_sink, m_g_mem, m_w_mem_kv, m_w_up_a, m_w_up_b, m_w_up_m, m_w_out, m_g_post, v_g_pre, v_w_in, v_w_conv, v_attn_sink, v_g_mem, v_w_mem_kv, v_w_up_a, v_w_up_b, v_w_up_m, v_w_out, v_g_post):
    s = x.shape[1]
    x2, mem2, tgt2 = x[0], mem[0], loss_target[0]
    me = 4 * lax.axis_index("x") + 2 * lax.axis_index("y") + lax.axis_index("c")

    w_up_loc = jnp.concatenate([w_up_a[0], w_up_b[0], w_up_m[0]], axis=0).astype(BF16)
    w_conv_loc = jnp.zeros((8, 128), F32).at[:3, :64].set(w_conv[0])
    w_int_g, w_conv_g = _all_gather([w_in[0].T.astype(BF16), w_conv_loc], "gather_w_in",
                                    splits=[[(112 * k, 112) for k in range(7)] + [(784, 144)], [(0, 8)]])
    w_int = w_int_g.reshape(IN_WIDTH, D_MODEL)
    w_conv_f = w_conv_g[:, :3, :64].transpose(1, 0, 2).reshape(3, 512)
    late = _gather_start([w_mem_kv[0].astype(BF16) + w_conv_g[0, 7:8, 0:1].astype(BF16),
                          w_out[0].astype(BF16), w_up_loc], me, "gather_late_start")
    sink = attn_sink[0]
    tabs = jnp.stack(_rope_tables(s))

    h, pa, pq, pkv, pbz, pmq, pmz, pg, ya = _proj_fwd(x2, g_pre + late[4][0:1, 0:1], w_int, tabs, w_conv_f)
    yb = _attn_fwd(pq, pkv, pbz, sink)
    w_mkv_g, w_out_g, w_up_g = _gather_wait(*late[:4], yb, "gather_late_wait")
    w_mkv = w_mkv_g.reshape(D_MODEL, D_MODEL)
    w_out_f = w_out_g.reshape(D_MODEL, D_MODEL)
    mn, mkv = _mem_kv_fwd(mem2, g_mem, w_mkv)
    ym = _mem_attn_fwd(pmq, pmz, mkv)
    dg, dya, dyb, dym, dy, loss_p, gg_post, mb, dob, du = _mid(ya, yb, ym, pg, x2, tgt2, g_post, w_up_g, w_out_f)
    gw_out, gw_up = _gw_mid(mb, dob, (ya, yb, ym), du)

    core = lax.axis_index("c").astype(jnp.int32).reshape(1)
    chip = (2 * lax.axis_index("x") + lax.axis_index("y")).astype(jnp.int32).reshape(1)

    def exchange_start(shares, tag):
        from_sibling = _sibling_exchange(shares, "grads_to_sibling_" + tag)
        chip_shares = _pair_add(shares, from_sibling, core, "grads_pair_add_" + tag)
        return _chip_exchange_start(chip_shares, "grads_to_chips_start_" + tag)

    dmq, dmz, dmkv = _mem_attn_bwd(pmq, pmz, mkv, dym)
    gw_mkv, gg_mem = _mem_kv_bwd(mem2, g_mem, mn, dmkv, w_mkv)
    send1, recv1, srcs1, lands1, token1 = exchange_start(
        [gw_mkv.reshape(N_DEV, 128, D_MODEL), gw_out.reshape(N_DEV, 128, D_MODEL), gw_up], "small")
    da, gw_conv = _conv_bwd(pa, dya, w_conv_f + token1[0:1, 0:1])
    dq, dbz, dkv, g_sink = _attn_bwd(pq, pkv, pbz, dyb, sink, tabs)
    dparts = (da, dq, dkv, dbz, dmq, dmz, dg)
    gw_int = _gw_in(dparts, h)
    send2, recv2, srcs2, lands2, token2 = exchange_start([gw_int.reshape(N_DEV, SHARD_IN, D_MODEL)], "w_in")
    grad_x, gg_pre = _dh_bwd(dparts, x2, dy, g_pre + token2[0:1, 0:1], w_int)
    (o_mkv, o_out, o_up, o_int), (l_mkv, l_out, l_up, l_int) = _chip_exchange_wait(
        send1 + send2, recv1 + recv2, srcs1 + srcs2, lands1 + lands2, grad_x, "grads_to_chips_wait")

    loss_row, small_g, sd, sm, sv = _small_step(
        (gg_pre, gw_conv, g_sink, gg_mem, gg_post, loss_p),
        [g_pre, w_conv[0], attn_sink, g_mem, g_post],
        [m_g_pre, m_w_conv[0], m_attn_sink, m_g_mem, m_g_post],
        [v_g_pre, v_w_conv[0], v_attn_sink, v_g_mem, v_g_post])
    loss = loss_row[0, 0]
    g_g_pre, g_conv, g_sink_tot, g_g_mem, g_g_post = small_g

    g_w_in, d_w_in, nm_w_in, nv_w_in = (t.T for t in _sum_adamw(
        o_int, l_int, chip, 0, w_in[0].T, m_w_in[0].T, v_w_in[0].T, "adamw_w_in", tiles=2))
    (g_mkv, d_mkv, nm_mkv, nv_mkv), (g_out, d_out, nm_out, nv_out), *up = _sum_adamw_group(
        [(o_mkv, l_mkv, 0, w_mem_kv[0], m_w_mem_kv[0], v_w_mem_kv[0]),
         (o_out, l_out, 0, w_out[0], m_w_out[0], v_w_out[0]),
         (o_up, l_up, 0, w_up_a[0], m_w_up_a[0], v_w_up_a[0]),
         (o_up, l_up, 1, w_up_b[0], m_w_up_b[0], v_w_up_b[0]),
         (o_up, l_up, 2, w_up_m[0], m_w_up_m[0], v_w_up_m[0])], chip, "adamw_mid_weights")

    def lead(a):
        return a[None]

    grads = [g_g_pre, lead(g_w_in), lead(g_conv), g_sink_tot, g_g_mem, lead(g_mkv), lead(up[0][0]),
             lead(up[1][0]), lead(up[2][0]), lead(g_out), g_g_post]

    def assemble(small, big_in, big_mkv, big_up, big_out):
        return [small[0], lead(big_in), lead(small[1]), small[2], small[3], lead(big_mkv), lead(big_up[0]),
                lead(big_up[1]), lead(big_up[2]), lead(big_out), small[4]]

    deltas = assemble(sd, d_w_in, d_mkv, [u[1] for u in up], d_out)
    new_m = assemble(sm, nm_w_in, nm_mkv, [u[2] for u in up], nm_out)
    new_v = assemble(sv, nv_w_in, nv_mkv, [u[3] for u in up], nv_out)
    return (loss, grad_x[None], *grads, *deltas, *new_m, *new_v)
```

```python
import functools

import jax
import jax.numpy as jnp
from jax import lax
from jax.experimental import pallas as pl
from jax.experimental.pallas import tpu as pltpu

F32 = jnp.float32
BF16 = jnp.bfloat16
MESH = pl.DeviceIdType.MESH

N_DEV = 8
D_MODEL = 1024
EPS = 1e-6
ROPE_THETA = 500000.0
ROT_DIM = 16
HEAD_DIM = 64
ATTN_BLOCK = 128
MEM_HEADS = 4
MEM_HEAD_DIM = 128
ATTN_SCALE = HEAD_DIM ** -0.5
MEM_SCALE = MEM_HEAD_DIM ** -0.5

ADAM_LR = 0.001
ADAM_B1 = 0.9
ADAM_B2 = 0.999
ADAM_EPS = 1e-08
ADAM_WD = 0.01
ADAM_STEP = 10

SEG_A = (0, 2048)
SEG_BQ = (2048, 512)
SEG_BKV = (2560, 256)
SEG_BZ = (2816, 512)
SEG_MQ = (3328, 512)
SEG_MZ = (3840, 512)
SEG_G = (4352, 3072)
SEGS = (SEG_A, SEG_BQ, SEG_BKV, SEG_BZ, SEG_MQ, SEG_MZ, SEG_G)
IN_WIDTH = 7424
SHARD_IN = IN_WIDTH // N_DEV

V7X_VMEM_BYTES = 64 * 1024 * 1024
ANY = pl.BlockSpec(memory_space=pl.ANY)


def _params(vmem_mb):
    assert vmem_mb * 1024 * 1024 < V7X_VMEM_BYTES
    return pltpu.CompilerParams(dimension_semantics=("arbitrary",), vmem_limit_bytes=vmem_mb * 1024 * 1024)


def _full(shape):
    zeros = (0,) * len(shape)
    return pl.BlockSpec(shape, lambda i: zeros)


def _rows(tm, width):
    return pl.BlockSpec((tm, width), lambda i: (i, 0))


def _dot(a, b):
    return jnp.dot(a, b, preferred_element_type=F32)


def _dot_nt(a, b):
    return lax.dot_general(a, b, (((1,), (1,)), ((), ())), preferred_element_type=F32)


def _dot_tn(a, b):
    return lax.dot_general(a, b, (((0,), (0,)), ((), ())), preferred_element_type=F32)


def _sigmoid(z):
    return 1.0 / (1.0 + jnp.exp(-z))


def _rope(t, cs, s1, s2):
    return t * cs + pltpu.roll(t, 120, 1) * s1 + pltpu.roll(t, 8, 1) * s2


def _rope_t(d, cs, s1, s2):
    return d * cs + pltpu.roll(d * s1, 8, 1) + pltpu.roll(d * s2, 120, 1)


def _rope_tables(s):
    half = ROT_DIM // 2
    inv_freq = jnp.power(jnp.float32(ROPE_THETA), -jnp.arange(half, dtype=F32) * (2.0 / ROT_DIM))
    ang = jnp.arange(s).astype(F32)[:, None] * inv_freq[None, :]
    cos, sin = jnp.cos(ang), jnp.sin(ang)
    d = jnp.arange(128) % HEAD_DIM
    k = jnp.arange(half)[:, None]
    lo = (d[None, :] == k).astype(F32)
    hi = (d[None, :] == k + half).astype(F32)
    spread = functools.partial(jnp.dot, precision=lax.Precision.HIGHEST)
    return (spread(cos, lo + hi) + (d >= ROT_DIM).astype(F32)[None, :], -spread(sin, lo), spread(sin, hi))


def _load_once(pairs, sems):
    @pl.when(pl.program_id(0) == 0)
    def _():
        cps = [pltpu.make_async_copy(src, dst, sems.at[k]) for k, (src, dst) in enumerate(pairs)]
        for cp in cps:
            cp.start()
        for cp in cps:
            cp.wait()


def _my_place():
    x, y, c = lax.axis_index("x"), lax.axis_index("y"), lax.axis_index("c")
    return x, y, c


def _all_gather(arrs, name, splits=None):
    n = len(arrs)
    if splits is None:
        splits = [[(0, a.shape[0])] for a in arrs]
    pieces = [(a, r0, rn) for a in range(n) for r0, rn in splits[a]]
    n_p = len(pieces)

    def body(*refs):
        ins, outs = refs[:n], refs[n:2 * n]
        send_sems, recv_sems, local_sems = refs[2 * n:]
        x, y, c = _my_place()
        me, sibling = (x, y, c), (x, y, 1 - c)

        def route(core):
            first = (jnp.bitwise_xor(x, 1 - core), jnp.bitwise_xor(y, core), core)
            second = (jnp.bitwise_xor(x, core), jnp.bitwise_xor(y, 1 - core), core)
            return first, second, (1 - x, 1 - y, core)

        def idx(px, py, pc):
            return 4 * px + 2 * py + pc

        def copy(p, k, block, to, own=False):
            a, r0, rn = pieces[p]
            dst = outs[a].at[idx(*block), pl.ds(r0, rn)]
            return pltpu.make_async_remote_copy(
                src_ref=ins[a].at[pl.ds(r0, rn)] if own else dst, dst_ref=dst,
                send_sem=send_sems.at[p * 7 + k], recv_sem=recv_sems.at[p * 7 + k],
                device_id=to, device_id_type=MESH)

        nbr1, nbr2, diag = route(c)
        mine = [pltpu.make_async_copy(ins[a], outs[a].at[idx(*me)], local_sems.at[a]) for a in range(n)]
        for cp in mine:
            cp.start()
        sent = []
        for p in range(n_p):
            for k, to in enumerate((sibling, nbr1, nbr2)):
                sent.append(copy(p, k, me, to, own=True))
        for cp in sent:
            cp.start()
        for k_in, block, onward in ((1, nbr1, ((3, nbr2), (4, sibling))), (2, nbr2, ((5, sibling),)),
                                    (3, diag, ((6, sibling),))):
            for p in range(n_p):
                copy(p, k_in, block, me).wait_recv()
                for k_out, to in onward:
                    cp = copy(p, k_out, block, to)
                    cp.start()
                    sent.append(cp)
        s1, s2, sd = route(1 - c)
        for k_in, block in ((0, sibling), (4, s1), (5, s2), (6, sd)):
            for p in range(n_p):
                copy(p, k_in, block, me).wait_recv()
        for cp in sent:
            cp.wait_send()
        for cp in mine:
            cp.wait()

    return pl.pallas_call(
        body, name=name,
        out_shape=[jax.ShapeDtypeStruct((N_DEV,) + a.shape, a.dtype) for a in arrs],
        in_specs=[ANY] * n, out_specs=[ANY] * n,
        scratch_shapes=[pltpu.SemaphoreType.DMA((7 * n_p,)), pltpu.SemaphoreType.DMA((7 * n_p,)),
                        pltpu.SemaphoreType.DMA((n,))],
    )(*arrs)


N_CHIPS = 4


def _sibling_exchange(arrs, name):
    n = len(arrs)

    def body(*refs):
        ins, outs = refs[:n], refs[n:2 * n]
        send_sems, recv_sems = refs[2 * n:]
        x, y, c = _my_place()
        sibling = (x, y, 1 - c)

        def copy(a, j):
            return pltpu.make_async_remote_copy(
                src_ref=ins[a].at[2 * j + (1 - c)], dst_ref=outs[a].at[j],
                send_sem=send_sems.at[a * N_CHIPS + j], recv_sem=recv_sems.at[a * N_CHIPS + j],
                device_id=sibling, device_id_type=MESH)

        cps = [copy(a, j) for j in range(N_CHIPS) for a in range(n)]
        for cp in cps:
            cp.start()
        for cp in cps:
            cp.wait_recv()
        for cp in cps:
            cp.wait_send()

    return pl.pallas_call(
        body, name=name,
        out_shape=[jax.ShapeDtypeStruct((N_CHIPS,) + a.shape[1:], a.dtype) for a in arrs],
        in_specs=[ANY] * n, out_specs=[ANY] * n,
        scratch_shapes=[pltpu.SemaphoreType.DMA((N_CHIPS * n,)), pltpu.SemaphoreType.DMA((N_CHIPS * n,))],
    )(*arrs)


def _sibling_copies(srcs, lands, send_sems, recv_sems):
    x, y, c = _my_place()
    cps = []
    for j in range(N_CHIPS):
        for a in range(len(srcs)):
            k = a * N_CHIPS + j
            cps.append(pltpu.make_async_remote_copy(
                src_ref=srcs[a].at[2 * j + (1 - c)], dst_ref=lands[a].at[j], send_sem=send_sems[k],
                recv_sem=recv_sems[k], device_id=(x, y, 1 - c), device_id_type=MESH))
    return cps


def _pair_add(mine, recv, core, name):
    n = len(mine)

    def body(c_ref, *refs):
        for a in range(n):
            refs[2 * n + a][...] = (refs[a][...].astype(F32) + refs[n + a][...].astype(F32)).astype(BF16)

    def blk(a):
        return (None,) + a.shape[1:]

    grid_spec = pltpu.PrefetchScalarGridSpec(
        num_scalar_prefetch=1, grid=(N_CHIPS,),
        in_specs=[pl.BlockSpec(blk(a), lambda j, c_ref: (2 * j + c_ref[0], 0, 0)) for a in mine]
        + [pl.BlockSpec(blk(a), lambda j, c_ref: (j, 0, 0)) for a in recv],
        out_specs=[pl.BlockSpec(blk(a), lambda j, c_ref: (j, 0, 0)) for a in recv])
    return pl.pallas_call(
        body, name=name, grid_spec=grid_spec,
        out_shape=[jax.ShapeDtypeStruct(a.shape, BF16) for a in recv],
        compiler_params=_params(32),
    )(core, *mine, *recv)


HBM = pl.BlockSpec(memory_space=pltpu.HBM)
SEM = pl.BlockSpec(memory_space=pltpu.SEMAPHORE)
N_PEER_CHIPS = 3


def _chip_copies(srcs, lands, send_sems, recv_sems):
    x, y, c = _my_place()
    my_chip = 2 * x + y
    peers = [(x, 1 - y), (1 - x, y), (1 - x, 1 - y)]
    cps = []
    for k, (px, py) in enumerate(peers):
        for a in range(len(srcs)):
            j = a * N_PEER_CHIPS + k
            cps.append(pltpu.make_async_remote_copy(
                src_ref=srcs[a].at[2 * px + py], dst_ref=lands[a].at[my_chip],
                send_sem=send_sems[j], recv_sem=recv_sems[j],
                device_id=(px, py, c), device_id_type=MESH))
    return cps


N_PEERS = N_DEV - 1


def _gather_copies(srcs, lands, send_sems, recv_sems):
    x, y, c = _my_place()
    me_idx = 4 * x + 2 * y + c
    flips = [(0, 0, 1), (0, 1, 0), (1, 0, 0), (0, 1, 1), (1, 0, 1), (1, 1, 0), (1, 1, 1)]
    cps = []
    for k, (fx, fy, fc) in enumerate(flips):
        peer = ((1 - x) if fx else x, (1 - y) if fy else y, (1 - c) if fc else c)
        for a in range(len(srcs)):
            j = a * N_PEERS + k
            cps.append(pltpu.make_async_remote_copy(
                src_ref=srcs[a], dst_ref=lands[a].at[me_idx], send_sem=send_sems[j], recv_sem=recv_sems[j],
                device_id=peer, device_id_type=MESH))
    return cps


def _split_start(copies, per_array, arrs, lands, name):
    arrs, lands = list(arrs), list(lands)
    n = len(arrs)
    k = n * per_array

    def body(*refs):
        srcs, land_refs = refs[:n], refs[n:2 * n]
        send_sems, recv_sems = refs[2 * n:2 * n + k], refs[2 * n + k:2 * n + 2 * k]
        token = refs[-1]
        for cp in copies(srcs, land_refs, send_sems, recv_sems):
            cp.start()
        token[...] = jnp.zeros_like(token)

    hbm_arrs = [pltpu.with_memory_space_constraint(a, pltpu.HBM) for a in arrs]
    lands = [pltpu.with_memory_space_constraint(a, pltpu.HBM) for a in lands]
    res = pl.pallas_call(
        body, name=name,
        out_shape=[pltpu.SemaphoreType.DMA(())] * (2 * k) + [pltpu.HBM(a.shape, a.dtype) for a in arrs + lands]
        + [jax.ShapeDtypeStruct((8, 128), F32)],
        in_specs=[HBM] * (2 * n),
        out_specs=[SEM] * (2 * k) + [HBM] * (2 * n) + [pl.BlockSpec(memory_space=pltpu.VMEM)],
        input_output_aliases={a: 2 * k + a for a in range(2 * n)},
        compiler_params=pltpu.CompilerParams(has_side_effects=pltpu.SideEffectType.DATAFLOW_SIDE_EFFECTING),
    )(*hbm_arrs, *lands)
    return res[:k], res[k:2 * k], res[2 * k:2 * k + n], res[2 * k + n:2 * k + 2 * n], res[-1]


def _split_wait(copies, per_array, send_sems, recv_sems, srcs, lands, after, name):
    n = len(srcs)
    k = n * per_array

    def body(*refs):
        src_refs, land_refs = refs[:n], refs[n:2 * n]
        s_sems, r_sems = refs[2 * n:2 * n + k], refs[2 * n + k:2 * n + 2 * k]
        for cp in copies(src_refs, land_refs, s_sems, r_sems):
            cp.wait_send()
            cp.wait_recv()

    res = pl.pallas_call(
        body, name=name,
        out_shape=[pltpu.HBM(a.shape, a.dtype) for a in list(srcs) + list(lands)],
        in_specs=[HBM] * (2 * n) + [SEM] * (2 * k) + [ANY],
        out_specs=[HBM] * (2 * n),
        input_output_aliases={a: a for a in range(2 * n)},
        compiler_params=pltpu.CompilerParams(has_side_effects=pltpu.SideEffectType.DATAFLOW_SIDE_EFFECTING),
    )(*srcs, *lands, *send_sems, *recv_sems, after)
    return res[:n], res[n:]


def _chip_exchange_start(arrs, name):
    return _split_start(_chip_copies, N_PEER_CHIPS, arrs, [lax.empty(a.shape, a.dtype) for a in arrs], name)


def _chip_exchange_wait(send_sems, recv_sems, srcs, lands, after, name):
    return _split_wait(_chip_copies, N_PEER_CHIPS, send_sems, recv_sems, srcs, lands, after, name)


def _gather_start(arrs, me_idx, name):
    lands = [lax.dynamic_update_slice(lax.empty((N_DEV,) + a.shape, a.dtype), a[None], (me_idx, 0, 0)) for a in arrs]
    return _split_start(_gather_copies, N_PEERS, arrs, lands, name)


def _gather_wait(send_sems, recv_sems, srcs, lands, after, name):
    return _split_wait(_gather_copies, N_PEERS, send_sems, recv_sems, srcs, lands, after, name)[1]


def _proj_fwd(x, g_pre, w_int, tabs):
    s = x.shape[0]
    tm = min(512, s)

    def body(x_ref, g_ref, t_ref, w_hbm,
             h_ref, pa_ref, pq_ref, pkv_ref, pbz_ref, pmq_ref, pmz_ref, pg_ref, w_vm, sems):
        _load_once([(w_hbm, w_vm)], sems)
        xf = x_ref[...]
        r = lax.rsqrt(jnp.mean(xf * xf, axis=-1, keepdims=True) + EPS)
        h = ((xf * r) * g_ref[...]).astype(BF16)
        h_ref[...] = h
        cs, s1, s2 = t_ref[0], t_ref[1], t_ref[2]

        def mm(seg, c0, width):
            return _dot_nt(h, w_vm[seg[0] + c0:seg[0] + c0 + width, :])

        for c0 in range(0, SEG_A[1], 512):
            pa_ref[:, c0:c0 + 512] = mm(SEG_A, c0, 512).astype(BF16)
        q = mm(SEG_BQ, 0, 512)
        for b in range(4):
            pq_ref[:, 128 * b:128 * b + 128] = _rope(q[:, 128 * b:128 * b + 128], cs, s1, s2).astype(BF16)
        kv = mm(SEG_BKV, 0, 256)
        pkv_ref[:, 0:128] = _rope(kv[:, 0:128], cs, s1, s2).astype(BF16)
        pkv_ref[:, 128:256] = kv[:, 128:256].astype(BF16)
        pbz_ref[...] = mm(SEG_BZ, 0, 512).astype(BF16)
        pmq_ref[...] = mm(SEG_MQ, 0, 512).astype(BF16)
        pmz_ref[...] = mm(SEG_MZ, 0, 512).astype(BF16)
        for c0 in range(0, SEG_G[1], 512):
            pg_ref[:, c0:c0 + 512] = mm(SEG_G, c0, 512).astype(BF16)

    widths = (D_MODEL, 2048, 512, 256, 512, 512, 512, 3072)
    return pl.pallas_call(
        body, name="proj_fwd", grid=(s // tm,),
        out_shape=[jax.ShapeDtypeStruct((s, w), BF16) for w in widths],
        in_specs=[_rows(tm, D_MODEL), _full((1, D_MODEL)), pl.BlockSpec((3, tm, 128), lambda i: (0, i, 0)), ANY],
        out_specs=[_rows(tm, w) for w in widths],
        scratch_shapes=[pltpu.VMEM((IN_WIDTH, D_MODEL), BF16), pltpu.SemaphoreType.DMA((1,))],
        compiler_params=_params(52),
    )(x, g_pre, tabs, w_int)


def _mem_kv_fwd(mem, g_mem, w_mkv):
    m = mem.shape[0]

    def body(mem_ref, g_ref, w_ref, mn_ref, mkv_ref):
        xf = mem_ref[...]
        r = lax.rsqrt(jnp.mean(xf * xf, axis=-1, keepdims=True) + EPS)
        mn = ((xf * r) * g_ref[...]).astype(BF16)
        mn_ref[...] = mn
        mkv_ref[...] = _dot(mn, w_ref[...]).astype(BF16)

    return pl.pallas_call(
        body, name="mem_kv_fwd", grid=(1,),
        out_shape=[jax.ShapeDtypeStruct((m, D_MODEL), BF16)] * 2,
        in_specs=[_full((m, D_MODEL)), _full((1, D_MODEL)), _full((D_MODEL, D_MODEL))],
        out_specs=[_full((m, D_MODEL))] * 2,
        compiler_params=_params(32),
    )(mem, g_mem, w_mkv)


def _halo_specs(s, tm, rows, width):
    nblk = s // rows
    prev = pl.BlockSpec((rows, width), lambda i: (jnp.maximum(i * (tm // rows) - 1, 0), 0))
    nxt = pl.BlockSpec((rows, width), lambda i: (jnp.minimum((i + 1) * (tm // rows), nblk - 1), 0))
    return prev, nxt


def _conv_common(pa, cu_prev, cu_next, w, tm):
    b, c, u, z = (pa[:, 512 * k:512 * k + 512] for k in range(4))
    cu = c * u
    row = lax.broadcasted_iota(jnp.int32, (tm, 512), 0)
    cu_m1 = jnp.where(row == 0, cu_prev, pltpu.roll(cu, 1, 0))
    cu_p1 = jnp.where(row == tm - 1, cu_next, pltpu.roll(cu, tm - 1, 0))
    y = cu_m1 * w[0:1] + cu * w[1:2] + cu_p1 * w[2:3]
    sig = _sigmoid(z)
    return b, c, u, z, cu, cu_m1, cu_p1, y, sig, row


def _conv_fwd(pa, w_conv):
    s = pa.shape[0]
    tm = min(512, s)
    nt = s // tm

    def body(pa_ref, pp_ref, pn_ref, w_ref, ya_ref):
        i = pl.program_id(0)
        prev_row = pp_ref[...].astype(F32)[15:16, :]
        next_row = pn_ref[...].astype(F32)[0:1, :]
        b, _, _, z, _, _, _, y, sig, _ = _conv_common(
            pa_ref[...].astype(F32),
            jnp.where(i == 0, 0.0, prev_row[:, 512:1024] * prev_row[:, 1024:1536]),
            jnp.where(i == nt - 1, 0.0, next_row[:, 512:1024] * next_row[:, 1024:1536]), w_ref[...], tm)
        ya_ref[...] = (b * y * (z * sig)).astype(BF16)

    prev, nxt = _halo_specs(s, tm, 16, 2048)
    return pl.pallas_call(
        body, name="conv_fwd", grid=(nt,),
        out_shape=jax.ShapeDtypeStruct((s, 512), BF16),
        in_specs=[_rows(tm, 2048), prev, nxt, _full((3, 512))],
        out_specs=_rows(tm, 512),
        compiler_params=_params(48),
    )(pa, pa, pa, w_conv)


def _heads_to_lanes(a, g, row):
    low = row < HEAD_DIM
    parts = []
    for b in (2 * g, 2 * g + 1):
        t = jnp.transpose(a[:, 128 * b:128 * b + 128])
        swapped = pltpu.roll(t, HEAD_DIM, 0)
        if g == 0:
            parts += [jnp.where(low, t, 0.0), jnp.where(low, swapped, 0.0)]
        else:
            parts += [jnp.where(low, 0.0, swapped), jnp.where(low, 0.0, t)]
    return jnp.concatenate(parts, axis=1)


def _lanes_to_heads(t0, t1, row):
    low = row < HEAD_DIM
    blocks = []
    for b in range(4):
        g = b // 2
        tg = (t0, t1)[g]
        je = 2 * (b - 2 * g)
        even, odd = tg[:, 128 * je:128 * je + 128], tg[:, 128 * je + 128:128 * je + 256]
        if g == 0:
            t = jnp.where(low, even, pltpu.roll(odd, HEAD_DIM, 0))
        else:
            t = jnp.where(low, pltpu.roll(even, HEAD_DIM, 0), odd)
        blocks.append(jnp.transpose(t))
    return jnp.concatenate(blocks, axis=1)


WINDOW_KEYS = 3 * ATTN_BLOCK
STACKED = 4 * ATTN_BLOCK
KEY_CHUNK = 32
BLOCKS_PER_STEP = 4


def _fill_band_bias(bias, nb):
    assert nb >= 2
    c = lax.broadcasted_iota(jnp.int32, (WINDOW_KEYS, STACKED), 0)
    r = lax.broadcasted_iota(jnp.int32, (WINDOW_KEYS, STACKED), 1) & (ATTN_BLOCK - 1)
    band = (c >= r) & (c <= r + 2 * ATTN_BLOCK)
    for v, ok in enumerate((band, band & (c >= ATTN_BLOCK), band & (c < 2 * ATTN_BLOCK))):
        bias[v] = jnp.where(ok, 0.0, -jnp.inf)


def _bias_variant(n, nb):
    return jnp.where(n == 0, 1, jnp.where(n == nb - 1, 2, 0))


def _sink_row(sink_ref, g):
    return jnp.concatenate([jnp.full((1, ATTN_BLOCK), sink_ref[4 * g + j], F32) for j in range(4)], axis=1)


def _softmax_keys_major(sc, bias, variant, sink, e_scr):
    chunks = [pl.ds(k * KEY_CHUNK, KEY_CHUNK) for k in range(WINDOW_KEYS // KEY_CHUNK)]
    rows = [slice(k * KEY_CHUNK, (k + 1) * KEY_CHUNK) for k in range(WINDOW_KEYS // KEY_CHUNK)]
    m_run = jnp.full((KEY_CHUNK, STACKED), -jnp.inf, F32)
    for ck, rw in zip(chunks, rows):
        m_run = jnp.maximum(m_run, sc[rw] * ATTN_SCALE + bias[variant, ck, :])
    m = jnp.maximum(jnp.max(m_run, axis=0, keepdims=True), sink)
    l_run = jnp.zeros((KEY_CHUNK, STACKED), F32)
    for ck, rw in zip(chunks, rows):
        e = jnp.exp(sc[rw] * ATTN_SCALE + bias[variant, ck, :] - m)
        l_run += e
        e_scr[rw, :] = e.astype(BF16)
    es = jnp.exp(sink - m)
    inv = 1.0 / (jnp.sum(l_run, axis=0, keepdims=True) + es)
    return inv, es * inv


def _fill_padded(kv_ref, kpad, vpad, s):
    zero = jnp.zeros((ATTN_BLOCK, 128), BF16)
    kpad[0:ATTN_BLOCK, :] = zero
    vpad[0:ATTN_BLOCK, :] = zero
    kpad[ATTN_BLOCK + s:2 * ATTN_BLOCK + s, :] = zero
    vpad[ATTN_BLOCK + s:2 * ATTN_BLOCK + s, :] = zero
    kpad[ATTN_BLOCK:ATTN_BLOCK + s, :] = kv_ref[:, 0:128]
    vpad[ATTN_BLOCK:ATTN_BLOCK + s, :] = kv_ref[:, 128:256]


def _attn_fwd(pq, pkv, pbz, sink):
    s = pq.shape[0]
    nb = s // ATTN_BLOCK

    def body(sink_ref, q_ref, z_ref, kv_ref, yb_ref, kpad, vpad, bias, e_scr):
        i = pl.program_id(0)

        @pl.when(i == 0)
        def _():
            _fill_padded(kv_ref, kpad, vpad, s)
            _fill_band_bias(bias, nb)

        row = lax.broadcasted_iota(jnp.int32, (ATTN_BLOCK, 128), 0)
        for b in range(BLOCKS_PER_STEP):
            n = i * BLOCKS_PER_STEP + b
            rows = slice(b * ATTN_BLOCK, (b + 1) * ATTN_BLOCK)
            start = pl.multiple_of(n * ATTN_BLOCK, ATTN_BLOCK)
            kw, vw = kpad[pl.ds(start, WINDOW_KEYS), :], vpad[pl.ds(start, WINDOW_KEYS), :]
            qf = q_ref[rows, :].astype(F32)
            variant = _bias_variant(n, nb)
            outs = []
            for g in range(2):
                e_bg = e_scr.at[2 * b + g]
                qt = _heads_to_lanes(qf, g, row).astype(BF16)
                inv, _ = _softmax_keys_major(_dot(kw, qt), bias, variant, _sink_row(sink_ref, g), e_bg)
                outs.append(_dot_tn(vw, e_bg[...]) * inv)
            attn = _lanes_to_heads(outs[0], outs[1], row)
            z = z_ref[rows, :].astype(F32)
            yb_ref[rows, :] = (attn * (z * _sigmoid(z))).astype(BF16)

    tq = BLOCKS_PER_STEP * ATTN_BLOCK
    return pl.pallas_call(
        body, name="attn_fwd", grid=(s // tq,),
        out_shape=jax.ShapeDtypeStruct((s, 512), BF16),
        in_specs=[pl.BlockSpec(memory_space=pltpu.SMEM), _rows(tq, 512), _rows(tq, 512), _full((s, 256))],
        out_specs=_rows(tq, 512),
        scratch_shapes=[pltpu.VMEM((s + 2 * ATTN_BLOCK, 128), BF16)] * 2
        + [pltpu.VMEM((3, WINDOW_KEYS, STACKED), F32),
           pltpu.VMEM((2 * BLOCKS_PER_STEP, WINDOW_KEYS, STACKED), BF16)],
        compiler_params=_params(32),
    )(sink, pq, pbz, pkv)


def _mem_softmax_t(q, mk):
    sc = _dot_nt(mk, q) * MEM_SCALE
    e = jnp.exp(sc - jnp.max(sc, axis=0, keepdims=True))
    return e * (1.0 / jnp.sum(e, axis=0, keepdims=True))


def _mem_attn_fwd(pmq, pmz, mkv):
    s = pmq.shape[0]
    m = mkv.shape[0]
    tm = min(512, s)

    def body(q_ref, z_ref, mk_ref, mv_ref, ym_ref):
        z = z_ref[...].astype(F32)
        sz = z * _sigmoid(z)
        for h in range(MEM_HEADS):
            cols = slice(128 * h, 128 * h + 128)
            pt = _mem_softmax_t(q_ref[:, cols], mk_ref[:, cols])
            o = _dot_tn(pt.astype(BF16), mv_ref[:, cols])
            ym_ref[:, cols] = (o * sz[:, cols]).astype(BF16)

    return pl.pallas_call(
        body, name="mem_attn_fwd", grid=(s // tm,),
        out_shape=jax.ShapeDtypeStruct((s, 512), BF16),
        in_specs=[_rows(tm, 512), _rows(tm, 512), pl.BlockSpec((m, 512), lambda i: (0, 0)),
                  pl.BlockSpec((m, 512), lambda i: (0, 1))],
        out_specs=_rows(tm, 512),
        compiler_params=_params(32),
    )(pmq, pmz, mkv, mkv)


def _mid(ya, yb, ym, pg, x, target, g_post, w_up, w_out):
    s = x.shape[0]
    tm = min(256, s)
    nt = s // tm

    def body(ya_ref, yb_ref, ym_ref, pg_ref, x_ref, t_ref, gp_ref, wup_hbm, wout_hbm,
             dg_ref, dya_ref, dyb_ref, dym_ref, dy_ref, loss_ref, ggp_ref, mb_ref, dob_ref, du_ref,
             wup_vm, wout_vm, sems):
        i = pl.program_id(0)
        _load_once([(wup_hbm.at[d], wup_vm.at[:, pl.ds(128 * d, 128)]) for d in range(N_DEV)]
                   + [(wout_hbm, wout_vm)], sems)

        @pl.when(i == 0)
        def _():
            loss_ref[...] = jnp.zeros_like(loss_ref)
            ggp_ref[...] = jnp.zeros_like(ggp_ref)

        ys = (ya_ref[...], yb_ref[...], ym_ref[...])
        us = [_dot(ys[k], wup_vm[512 * k:512 * k + 512, :]) for k in range(3)]
        gates = [_sigmoid(pg_ref[:, 1024 * k:1024 * k + 1024].astype(F32)) for k in range(3)]
        merged = gates[0] * us[0] + gates[1] * us[1] + gates[2] * us[2]
        mb = merged.astype(BF16)
        mb_ref[...] = mb
        out = _dot(mb, wout_vm[...])
        r = lax.rsqrt(jnp.mean(out * out, axis=-1, keepdims=True) + EPS)
        on = out * r
        gp = gp_ref[...]
        err = (x_ref[...] + on * gp) - t_ref[...]
        loss_ref[...] += 0.5 * jnp.sum(err * err) * (1.0 / D_MODEL)
        dy = err * (1.0 / D_MODEL)
        dy_ref[...] = dy
        ggp_ref[...] += jnp.sum(dy * on, axis=0, keepdims=True)
        a = dy * gp
        d_out = r * (a - on * jnp.mean(a * on, axis=-1, keepdims=True))
        dob = d_out.astype(BF16)
        dob_ref[...] = dob
        d_merged = _dot_nt(dob, wout_vm[...])
        d_refs = (dya_ref, dyb_ref, dym_ref)
        for k in range(3):
            g = gates[k]
            dg_ref[:, 1024 * k:1024 * k + 1024] = (d_merged * us[k] * g * (1.0 - g)).astype(BF16)
            du = (d_merged * g).astype(BF16)
            du_ref[k] = du
            d_refs[k][...] = _dot_nt(du, wup_vm[512 * k:512 * k + 512, :]).astype(BF16)

    return pl.pallas_call(
        body, name="mid", grid=(nt,),
        out_shape=[jax.ShapeDtypeStruct((s, 3072), BF16)] + [jax.ShapeDtypeStruct((s, 512), BF16)] * 3
        + [jax.ShapeDtypeStruct((s, D_MODEL), F32), jax.ShapeDtypeStruct((8, 128), F32),
           jax.ShapeDtypeStruct((1, D_MODEL), F32), jax.ShapeDtypeStruct((s, D_MODEL), BF16),
           jax.ShapeDtypeStruct((s, D_MODEL), BF16), jax.ShapeDtypeStruct((3, s, D_MODEL), BF16)],
        in_specs=[_rows(tm, 512)] * 3 + [_rows(tm, 3072), _rows(tm, D_MODEL), _rows(tm, D_MODEL),
                                         _full((1, D_MODEL)), ANY, ANY],
        out_specs=[_rows(tm, 3072)] + [_rows(tm, 512)] * 3
        + [_rows(tm, D_MODEL), _full((8, 128)), _full((1, D_MODEL)), _rows(tm, D_MODEL), _rows(tm, D_MODEL),
           pl.BlockSpec((3, tm, D_MODEL), lambda i: (0, i, 0))],
        scratch_shapes=[pltpu.VMEM((1536, D_MODEL), BF16), pltpu.VMEM((D_MODEL, D_MODEL), BF16),
                        pltpu.SemaphoreType.DMA((N_DEV + 1,))],
        compiler_params=_params(56),
    )(ya, yb, ym, pg, x, target, g_post, w_up, w_out)


def _gw_mid(mb, dob, ys, du):
    s = mb.shape[0]
    tn = 256

    def out_body(mb_ref, dob_ref, o_ref):
        o_ref[...] = _dot_tn(mb_ref[...], dob_ref[...]).astype(BF16)

    gw_out = pl.pallas_call(
        out_body, name="gw_out", grid=(D_MODEL // tn,),
        out_shape=jax.ShapeDtypeStruct((D_MODEL, D_MODEL), BF16),
        in_specs=[pl.BlockSpec((s, tn), lambda j: (0, j)), _full((s, D_MODEL))],
        out_specs=pl.BlockSpec((tn, D_MODEL), lambda j: (j, 0)),
        compiler_params=_params(48),
    )(mb, dob)

    per = 512 // tn

    def up_body(ya_ref, yb_ref, ym_ref, du_ref, o_ref):
        j = pl.program_id(0)
        for k, y_ref in enumerate((ya_ref, yb_ref, ym_ref)):
            @pl.when(j // per == k)
            def _(y_ref=y_ref):
                res = _dot_tn(y_ref[...], du_ref[...])
                for d in range(N_DEV):
                    o_ref[d] = res[:, 128 * d:128 * d + 128].astype(BF16)

    def y_spec(k):
        return pl.BlockSpec((s, tn), lambda j: (0, jnp.clip(j - per * k, 0, per - 1)))

    gw_up = pl.pallas_call(
        up_body, name="gw_up", grid=(3 * per,),
        out_shape=jax.ShapeDtypeStruct((N_DEV, 1536, 128), BF16),
        in_specs=[y_spec(0), y_spec(1), y_spec(2), pl.BlockSpec((None, s, D_MODEL), lambda j: (j // per, 0, 0))],
        out_specs=pl.BlockSpec((N_DEV, tn, 128), lambda j: (0, j, 0)),
        compiler_params=_params(48),
    )(*ys, du)
    return gw_out, gw_up


def _conv_bwd(pa, dya, w_conv):
    s = pa.shape[0]
    tm = min(512, s)
    nt = s // tm

    def body(pa_ref, pp_ref, pn_ref, d_ref, dp_ref, dn_ref, w_ref, da_ref, gw_ref):
        i = pl.program_id(0)
        first, last = i == 0, i == nt - 1

        @pl.when(first)
        def _():
            gw_ref[...] = jnp.zeros_like(gw_ref)

        w = w_ref[...]
        prev_row = pp_ref[...].astype(F32)[15:16, :]
        next_row = pn_ref[...].astype(F32)[0:1, :]
        b, c, u, z, cu, cu_m1, cu_p1, y, sig, row = _conv_common(
            pa_ref[...].astype(F32),
            jnp.where(first, 0.0, prev_row[:, 512:1024] * prev_row[:, 1024:1536]),
            jnp.where(last, 0.0, next_row[:, 512:1024] * next_row[:, 1024:1536]), w, tm)
        sz = z * sig
        dya_t = d_ref[...].astype(F32)
        d_y = dya_t * b * sz

        def halo_dy(p_row, d_row):
            zz = p_row[:, 1536:2048]
            return d_row * p_row[:, 0:512] * (zz * _sigmoid(zz))

        dy_prev = jnp.where(first, 0.0, halo_dy(prev_row, dp_ref[...].astype(F32)[15:16, :]))
        dy_next = jnp.where(last, 0.0, halo_dy(next_row, dn_ref[...].astype(F32)[0:1, :]))
        dy_m1 = jnp.where(row == 0, dy_prev, pltpu.roll(d_y, 1, 0))
        dy_p1 = jnp.where(row == tm - 1, dy_next, pltpu.roll(d_y, tm - 1, 0))
        d_cu = dy_p1 * w[0:1] + d_y * w[1:2] + dy_m1 * w[2:3]
        da_ref[:, 0:512] = (dya_t * y * sz).astype(BF16)
        da_ref[:, 512:1024] = (d_cu * u).astype(BF16)
        da_ref[:, 1024:1536] = (d_cu * c).astype(BF16)
        da_ref[:, 1536:2048] = (dya_t * b * y * (sig * (1.0 + z * (1.0 - sig)))).astype(BF16)
        gw_ref[0:1, :] += jnp.sum(d_y * cu_m1, axis=0, keepdims=True)
        gw_ref[1:2, :] += jnp.sum(d_y * cu, axis=0, keepdims=True)
        gw_ref[2:3, :] += jnp.sum(d_y * cu_p1, axis=0, keepdims=True)

    prev, nxt = _halo_specs(s, tm, 16, 2048)
    dprev, dnxt = _halo_specs(s, tm, 16, 512)
    return pl.pallas_call(
        body, name="conv_bwd", grid=(nt,),
        out_shape=[jax.ShapeDtypeStruct((s, 2048), BF16), jax.ShapeDtypeStruct((8, 512), F32)],
        in_specs=[_rows(tm, 2048), prev, nxt, _rows(tm, 512), dprev, dnxt, _full((3, 512))],
        out_specs=[_rows(tm, 2048), _full((8, 512))],
        compiler_params=_params(48),
    )(pa, pa, pa, dya, dya, dya, w_conv)


def _attn_bwd(pq, pkv, pbz, dyb, sink, tabs):
    s = pq.shape[0]
    nb = s // ATTN_BLOCK

    def body(sink_ref, q_ref, z_ref, d_ref, kv_ref, t_ref,
             dq_ref, dz_ref, dkv_ref, gs_ref, kpad, vpad, dk_acc, dv_acc, bias, e_scr, ds_scr):
        i = pl.program_id(0)

        @pl.when(i == 0)
        def _():
            _fill_padded(kv_ref, kpad, vpad, s)
            _fill_band_bias(bias, nb)
            dk_acc[...] = jnp.zeros_like(dk_acc)
            dv_acc[...] = jnp.zeros_like(dv_acc)
            gs_ref[...] = jnp.zeros_like(gs_ref)

        row = lax.broadcasted_iota(jnp.int32, (ATTN_BLOCK, 128), 0)
        for b in range(BLOCKS_PER_STEP):
            n = i * BLOCKS_PER_STEP + b
            rows = slice(b * ATTN_BLOCK, (b + 1) * ATTN_BLOCK)
            start = pl.multiple_of(n * ATTN_BLOCK, ATTN_BLOCK)
            kw, vw = kpad[pl.ds(start, WINDOW_KEYS), :], vpad[pl.ds(start, WINDOW_KEYS), :]
            qf = q_ref[rows, :].astype(F32)
            variant = _bias_variant(n, nb)
            z = z_ref[rows, :].astype(F32)
            sig = _sigmoid(z)
            dyb_t = d_ref[rows, :].astype(F32)
            d_attn = dyb_t * (z * sig)
            outs, dqs = [], []
            dk_w = jnp.zeros((WINDOW_KEYS, 128), F32)
            dv_w = jnp.zeros((WINDOW_KEYS, 128), F32)
            for g in range(2):
                e_bg, ds_bg = e_scr.at[2 * b + g], ds_scr.at[2 * b + g]
                qt = _heads_to_lanes(qf, g, row)
                inv, p_sink = _softmax_keys_major(
                    _dot(kw, qt.astype(BF16)), bias, variant, _sink_row(sink_ref, g), e_bg)
                ot = _dot_tn(vw, e_bg[...]) * inv
                outs.append(ot)
                dot_ = _heads_to_lanes(d_attn, g, row)
                delta = jnp.sum(dot_ * ot, axis=0, keepdims=True)
                dpt = _dot(vw, dot_.astype(BF16))
                for k in range(WINDOW_KEYS // KEY_CHUNK):
                    rw = slice(k * KEY_CHUNK, (k + 1) * KEY_CHUNK)
                    ds_bg[rw, :] = (e_bg[rw, :].astype(F32) * (dpt[rw] - delta)).astype(BF16)
                sink_part = p_sink * delta
                for j in range(4):
                    h = 4 * g + j
                    gs_ref[h:h + 1, :] -= jnp.sum(sink_part[:, 128 * j:128 * j + 128])
                dqs.append(_dot_tn(kw, ds_bg[...]) * (inv * ATTN_SCALE))
                dk_w += _dot_nt(ds_bg[...], (qt * inv).astype(BF16)) * ATTN_SCALE
                dv_w += _dot_nt(e_bg[...], (dot_ * inv).astype(BF16))
            dk_acc[pl.ds(start, WINDOW_KEYS), :] += dk_w
            dv_acc[pl.ds(start, WINDOW_KEYS), :] += dv_w
            attn = _lanes_to_heads(outs[0], outs[1], row)
            dz_ref[rows, :] = (dyb_t * attn * (sig * (1.0 + z * (1.0 - sig)))).astype(BF16)
            dq = _lanes_to_heads(dqs[0], dqs[1], row)
            trows = pl.ds(start, ATTN_BLOCK)
            cs, s1, s2 = t_ref[0, trows, :], t_ref[1, trows, :], t_ref[2, trows, :]
            for blk in range(4):
                cols = slice(128 * blk, 128 * blk + 128)
                dq_ref[rows, cols] = _rope_t(dq[:, cols], cs, s1, s2).astype(BF16)

        @pl.when(i == nb // BLOCKS_PER_STEP - 1)
        def _():
            dk = dk_acc[ATTN_BLOCK:ATTN_BLOCK + s, :]
            dkv_ref[:, 0:128] = _rope_t(dk, t_ref[0], t_ref[1], t_ref[2]).astype(BF16)
            dkv_ref[:, 128:256] = dv_acc[ATTN_BLOCK:ATTN_BLOCK + s, :].astype(BF16)

    tq = BLOCKS_PER_STEP * ATTN_BLOCK
    tile = _rows(tq, 512)
    return pl.pallas_call(
        body, name="attn_bwd", grid=(s // tq,),
        out_shape=[jax.ShapeDtypeStruct((s, 512), BF16), jax.ShapeDtypeStruct((s, 512), BF16),
                   jax.ShapeDtypeStruct((s, 256), BF16), jax.ShapeDtypeStruct((8, 128), F32)],
        in_specs=[pl.BlockSpec(memory_space=pltpu.SMEM), tile, tile, tile, _full((s, 256)), _full((3, s, 128))],
        out_specs=[tile, tile, _full((s, 256)), _full((8, 128))],
        scratch_shapes=[pltpu.VMEM((s + 2 * ATTN_BLOCK, 128), BF16)] * 2
        + [pltpu.VMEM((s + 2 * ATTN_BLOCK, 128), F32)] * 2
        + [pltpu.VMEM((3, WINDOW_KEYS, STACKED), F32)]
        + [pltpu.VMEM((2 * BLOCKS_PER_STEP, WINDOW_KEYS, STACKED), BF16)] * 2,
        compiler_params=_params(48),
    )(sink, pq, pbz, dyb, pkv, tabs)


def _mem_attn_bwd(pmq, pmz, mkv, dym):
    s = pmq.shape[0]
    m = mkv.shape[0]
    tm = min(512, s)

    def body(q_ref, z_ref, d_ref, mk_ref, mv_ref, dq_ref, dz_ref, dmkv_ref):
        @pl.when(pl.program_id(0) == 0)
        def _():
            dmkv_ref[...] = jnp.zeros_like(dmkv_ref)

        z = z_ref[...].astype(F32)
        sig = _sigmoid(z)
        dym_t = d_ref[...].astype(F32)
        d_attn = dym_t * (z * sig)
        dsilu = sig * (1.0 + z * (1.0 - sig))
        for h in range(MEM_HEADS):
            cols = slice(128 * h, 128 * h + 128)
            q, mk, mv = q_ref[:, cols], mk_ref[:, cols], mv_ref[:, cols]
            pt = _mem_softmax_t(q, mk)
            pb = pt.astype(BF16)
            o = _dot_tn(pb, mv)
            dob = d_attn[:, cols].astype(BF16)
            dpt = _dot_nt(mv, dob)
            dst = (pt * (dpt - jnp.sum(pt * dpt, axis=0, keepdims=True))).astype(BF16)
            dq_ref[:, cols] = (_dot_tn(dst, mk) * MEM_SCALE).astype(BF16)
            dz_ref[:, cols] = (dym_t[:, cols] * o * dsilu[:, cols]).astype(BF16)
            dmkv_ref[:, cols] += _dot(dst, q) * MEM_SCALE
            dmkv_ref[:, 512 + 128 * h:512 + 128 * h + 128] += _dot(pb, dob)

    return pl.pallas_call(
        body, name="mem_attn_bwd", grid=(s // tm,),
        out_shape=[jax.ShapeDtypeStruct((s, 512), BF16), jax.ShapeDtypeStruct((s, 512), BF16),
                   jax.ShapeDtypeStruct((m, D_MODEL), F32)],
        in_specs=[_rows(tm, 512), _rows(tm, 512), _rows(tm, 512), pl.BlockSpec((m, 512), lambda i: (0, 0)),
                  pl.BlockSpec((m, 512), lambda i: (0, 1))],
        out_specs=[_rows(tm, 512), _rows(tm, 512), _full((m, D_MODEL))],
        compiler_params=_params(32),
    )(pmq, pmz, dym, mkv, mkv)


def _mem_kv_bwd(mem, g_mem, mn, dmkv, w_mkv):
    m = mem.shape[0]

    def body(mem_ref, g_ref, mn_ref, d_ref, w_ref, gw_ref, gg_ref):
        db = d_ref[...].astype(BF16)
        gw_ref[...] = _dot_tn(mn_ref[...], db).astype(BF16)
        d_mn = _dot_nt(db, w_ref[...])
        xf = mem_ref[...]
        r = lax.rsqrt(jnp.mean(xf * xf, axis=-1, keepdims=True) + EPS)
        gg_ref[...] = jnp.sum(d_mn * (xf * r), axis=0, keepdims=True)

    return pl.pallas_call(
        body, name="mem_kv_bwd", grid=(1,),
        out_shape=[jax.ShapeDtypeStruct((D_MODEL, D_MODEL), BF16), jax.ShapeDtypeStruct((1, D_MODEL), F32)],
        in_specs=[_full((m, D_MODEL)), _full((1, D_MODEL)), _full((m, D_MODEL)), _full((m, D_MODEL)),
                  _full((D_MODEL, D_MODEL))],
        out_specs=[_full((D_MODEL, D_MODEL)), _full((1, D_MODEL))],
        compiler_params=_params(32),
    )(mem, g_mem, mn, dmkv, w_mkv)


def _dh_bwd(dparts, x, dy, g_pre, w_int):
    s = x.shape[0]
    tm = min(256, s)

    def body(*refs):
        d_refs = refs[:7]
        x_ref, dy_ref, g_ref, w_hbm, gx_ref, gg_ref, w_vm, sems = refs[7:]
        _load_once([(w_hbm, w_vm)], sems)

        @pl.when(pl.program_id(0) == 0)
        def _():
            gg_ref[...] = jnp.zeros_like(gg_ref)

        d_h = jnp.zeros((tm, D_MODEL), F32)
        for d_ref, (r0, width) in zip(d_refs, SEGS):
            for c0 in range(0, width, 512):
                cw = min(512, width - c0)
                d_h += _dot(d_ref[:, c0:c0 + cw], w_vm[r0 + c0:r0 + c0 + cw, :])
        xf = x_ref[...]
        r = lax.rsqrt(jnp.mean(xf * xf, axis=-1, keepdims=True) + EPS)
        xn = xf * r
        a = d_h * g_ref[...]
        gx_ref[...] = r * (a - xn * jnp.mean(a * xn, axis=-1, keepdims=True)) + dy_ref[...]
        gg_ref[...] += jnp.sum(d_h * xn, axis=0, keepdims=True)

    return pl.pallas_call(
        body, name="dh_bwd", grid=(s // tm,),
        out_shape=[jax.ShapeDtypeStruct((s, D_MODEL), F32), jax.ShapeDtypeStruct((1, D_MODEL), F32)],
        in_specs=[_rows(tm, w) for _, w in SEGS] + [_rows(tm, D_MODEL), _rows(tm, D_MODEL), _full((1, D_MODEL)), ANY],
        out_specs=[_rows(tm, D_MODEL), _full((1, D_MODEL))],
        scratch_shapes=[pltpu.VMEM((IN_WIDTH, D_MODEL), BF16), pltpu.SemaphoreType.DMA((1,))],
        compiler_params=_params(52),
    )(*dparts, x, dy, g_pre, w_int)


def _gw_in(dparts, h):
    s = h.shape[0]
    tn = 256
    starts, counts = [], []
    for r0, width in SEGS:
        starts.append(r0 // tn)
        counts.append(width // tn)

    def body(*refs):
        d_refs = refs[:7]
        h_hbm, o_ref, h_vm, sems = refs[7:]
        _load_once([(h_hbm, h_vm)], sems)
        j = pl.program_id(0)
        for d_ref, st, cnt in zip(d_refs, starts, counts):
            @pl.when((j >= st) & (j < st + cnt))
            def _(d_ref=d_ref):
                o_ref[...] = _dot_tn(d_ref[...], h_vm[...]).astype(BF16)

    def seg_spec(st, cnt):
        return pl.BlockSpec((s, tn), lambda j: (0, jnp.clip(j - st, 0, cnt - 1)))

    return pl.pallas_call(
        body, name="gw_in", grid=(IN_WIDTH // tn,),
        out_shape=jax.ShapeDtypeStruct((IN_WIDTH, D_MODEL), BF16),
        in_specs=[seg_spec(st, cnt) for st, cnt in zip(starts, counts)] + [ANY],
        out_specs=pl.BlockSpec((tn, D_MODEL), lambda j: (j, 0)),
        scratch_shapes=[pltpu.VMEM((s, D_MODEL), BF16), pltpu.SemaphoreType.DMA((1,))],
        compiler_params=_params(52),
    )(*dparts, h)


def _adamw_math(w, g, m, v):
    m2 = ADAM_B1 * m + (1.0 - ADAM_B1) * g
    v2 = ADAM_B2 * v + (1.0 - ADAM_B2) * (g * g)
    m_hat = m2 / (1.0 - ADAM_B1 ** ADAM_STEP)
    v_hat = v2 / (1.0 - ADAM_B2 ** ADAM_STEP)
    delta = -ADAM_LR * (m_hat / (jnp.sqrt(v_hat) + ADAM_EPS) + ADAM_WD * w)
    return delta, m2, v2


def _sum_adamw(own, land, chip, block, w, m, v, name, tiles=1):
    r, c = w.shape
    rt = r // tiles

    def body(c_ref, own_ref, l1_ref, l2_ref, l3_ref, w_ref, m_ref, v_ref, g_ref, d_ref, m2_ref, v2_ref):
        g = own_ref[...].astype(F32)
        for l_ref in (l1_ref, l2_ref, l3_ref):
            g += l_ref[...].astype(F32)
        g_ref[...] = g
        d_ref[...], m2_ref[...], v2_ref[...] = _adamw_math(w_ref[...], g, m_ref[...], v_ref[...])

    def share(k):
        return pl.BlockSpec((None, rt, c), lambda i, c_ref: (jnp.bitwise_xor(c_ref[0], k), block * tiles + i, 0))

    spec = pl.BlockSpec((rt, c), lambda i, c_ref: (i, 0))
    grid_spec = pltpu.PrefetchScalarGridSpec(
        num_scalar_prefetch=1, grid=(tiles,),
        in_specs=[share(0), share(1), share(2), share(3)] + [spec] * 3, out_specs=[spec] * 4)
    return pl.pallas_call(
        body, name=name, grid_spec=grid_spec,
        out_shape=[jax.ShapeDtypeStruct((r, c), F32)] * 4,
        compiler_params=_params(48),
    )(chip, own, land, land, land, w, m, v)


def _sum_adamw_group(items, chip, name):
    k = len(items)

    def body(c_ref, *refs):
        shares, wmv, outs = refs[:4 * k], refs[4 * k:7 * k], refs[7 * k:]
        for j in range(k):
            g = shares[4 * j][...].astype(F32)
            for l_ref in shares[4 * j + 1:4 * j + 4]:
                g += l_ref[...].astype(F32)
            outs[4 * j][...] = g
            outs[4 * j + 1][...], outs[4 * j + 2][...], outs[4 * j + 3][...] = _adamw_math(
                wmv[3 * j][...], g, wmv[3 * j + 1][...], wmv[3 * j + 2][...])

    def share(shape, block, q):
        return pl.BlockSpec((None,) + shape, lambda i, c_ref: (jnp.bitwise_xor(c_ref[0], q), block, 0))

    in_specs, args = [], []
    for own, land, block, w, m, v in items:
        in_specs += [share(w.shape, block, q) for q in range(4)]
        args += [own, land, land, land]
    for own, land, block, w, m, v in items:
        in_specs += [pl.BlockSpec(w.shape, lambda i, c_ref: (0, 0))] * 3
        args += [w, m, v]
    out_specs = [pl.BlockSpec(w.shape, lambda i, c_ref: (0, 0)) for _, _, _, w, _, _ in items for _ in range(4)]
    res = pl.pallas_call(
        body, name=name,
        grid_spec=pltpu.PrefetchScalarGridSpec(num_scalar_prefetch=1, grid=(1,), in_specs=in_specs,
                                               out_specs=out_specs),
        out_shape=[jax.ShapeDtypeStruct(w.shape, F32) for _, _, _, w, _, _ in items for _ in range(4)],
        compiler_params=_params(48),
    )(chip, *args)
    return [res[4 * j:4 * j + 4] for j in range(k)]


def _small_step(parts, ws, ms, vs):
    def exchange(gpre_ref, gconv_ref, gsink_ref, gmem_ref, gpost_ref, loss_ref, tot_ref,
                 pack, gathered, send_sems, recv_sems):
        x, y, c = _my_place()
        me_idx = 4 * x + 2 * y + c

        lane = lax.broadcasted_iota(jnp.int32, (1, 128), 1)
        sink_row = jnp.zeros((1, 128), F32)
        for h in range(8):
            sink_row = jnp.where(lane == h, gsink_ref[h:h + 1, :], sink_row)
        pack[...] = jnp.zeros_like(pack)
        pack[0:1, :] = gpre_ref[...]
        pack[1:2, :] = gmem_ref[...]
        pack[2:3, :] = gpost_ref[...]
        pack[3:6, 0:512] = gconv_ref[0:3, :]
        pack[6:7, 0:128] = sink_row
        pack[7:8, 0:128] = loss_ref[0:1, :]

        flips = [(0, 0, 1), (0, 1, 0), (1, 0, 0), (0, 1, 1), (1, 0, 1), (1, 1, 0), (1, 1, 1)]
        cps = []
        for k, (fx, fy, fc) in enumerate(flips):
            peer = ((1 - x) if fx else x, (1 - y) if fy else y, (1 - c) if fc else c)
            cps.append(pltpu.make_async_remote_copy(
                src_ref=pack, dst_ref=gathered.at[me_idx], send_sem=send_sems.at[k], recv_sem=recv_sems.at[k],
                device_id=peer, device_id_type=MESH))
        for cp in cps:
            cp.start()
        gathered[me_idx] = pack[...]
        for cp in cps:
            cp.wait_recv()
        for cp in cps:
            cp.wait_send()
        tot = gathered[0]
        for d in range(1, N_DEV):
            tot = tot + gathered[d]
        tot_ref[...] = tot

    tot = pl.pallas_call(
        exchange, name="small_exchange", grid=(1,),
        out_shape=jax.ShapeDtypeStruct((8, D_MODEL), F32),
        in_specs=[_full(p.shape) for p in parts], out_specs=_full((8, D_MODEL)),
        scratch_shapes=[pltpu.VMEM((8, D_MODEL), F32), pltpu.VMEM((N_DEV, 8, D_MODEL), F32),
                        pltpu.SemaphoreType.DMA((N_PEERS,)), pltpu.SemaphoreType.DMA((N_PEERS,))],
    )(*parts)

    def apply(tot_ref, *refs):
        w_refs, m_refs, v_refs = refs[0:5], refs[5:10], refs[10:15]
        loss_out = refs[15]
        g_outs, d_outs, m_outs, v_outs = refs[16:21], refs[21:26], refs[26:31], refs[31:36]
        x, y, c = _my_place()
        tot = tot_ref[...]
        conv = pltpu.roll(tot[:, 0:512], (512 - 64 * (4 * x + 2 * y + c)) % 512, 1)[3:6, 0:64]
        grads = (tot[0:1, :], conv, tot[6:7, 0:8], tot[1:2, :], tot[2:3, :])
        loss_out[...] = tot[7:8, 0:128]
        for j in range(5):
            g_outs[j][...] = grads[j]
            d_outs[j][...], m_outs[j][...], v_outs[j][...] = _adamw_math(
                w_refs[j][...], grads[j], m_refs[j][...], v_refs[j][...])

    specs = [_full(w.shape) for w in ws]
    res = pl.pallas_call(
        apply, name="small_apply", grid=(1,),
        out_shape=[jax.ShapeDtypeStruct((1, 128), F32)] + [jax.ShapeDtypeStruct(w.shape, F32) for w in ws] * 4,
        in_specs=[_full((8, D_MODEL))] + specs * 3,
        out_specs=[_full((1, 128))] + specs * 4,
    )(tot, *ws, *ms, *vs)
    return res[0], res[1:6], res[6:11], res[11:16], res[16:21]


def kernel(x, mem, g_pre, w_in, w_conv, attn_sink, g_mem, w_mem_kv, w_up_a, w_up_b, w_up_m, w_out, g_post, loss_target, m_g_pre, m_w_in, m_w_conv, m_attn_sink, m_g_mem, m_w_mem_kv, m_w_up_a, m_w_up_b, m_w_up_m, m_w_out, m_g_post, v_g_pre, v_w_in, v_w_conv, v_attn_sink, v_g_mem, v_w_mem_kv, v_w_up_a, v_w_up_b, v_w_up_m, v_w_out, v_g_post):
    s = x.shape[1]
    x2, mem2, tgt2 = x[0], mem[0], loss_target[0]
    me = 4 * lax.axis_index("x") + 2 * lax.axis_index("y") + lax.axis_index("c")

    w_up_loc = jnp.concatenate([w_up_a[0], w_up_b[0], w_up_m[0]], axis=0).astype(BF16)
    w_conv_loc = jnp.zeros((8, 128), F32).at[:3, :64].set(w_conv[0])
    w_int_g, w_conv_g = _all_gather([w_in[0].T.astype(BF16), w_conv_loc], "gather_w_in",
                                    splits=[[(112 * k, 112) for k in range(7)] + [(784, 144)], [(0, 8)]])
    w_int = w_int_g.reshape(IN_WIDTH, D_MODEL)
    w_conv_f = w_conv_g[:, :3, :64].transpose(1, 0, 2).reshape(3, 512)
    late = _gather_start([w_mem_kv[0].astype(BF16) + w_conv_g[0, 7:8, 0:1].astype(BF16),
                          w_out[0].astype(BF16), w_up_loc], me, "gather_late_start")
    sink = attn_sink[0]
    tabs = jnp.stack(_rope_tables(s))

    h, pa, pq, pkv, pbz, pmq, pmz, pg = _proj_fwd(x2, g_pre + late[4][0:1, 0:1], w_int, tabs)
    ya = _conv_fwd(pa, w_conv_f)
    yb = _attn_fwd(pq, pkv, pbz, sink)
    w_mkv_g, w_out_g, w_up_g = _gather_wait(*late[:4], yb, "gather_late_wait")
    w_mkv = w_mkv_g.reshape(D_MODEL, D_MODEL)
    w_out_f = w_out_g.reshape(D_MODEL, D_MODEL)
    mn, mkv = _mem_kv_fwd(mem2, g_mem, w_mkv)
    ym = _mem_attn_fwd(pmq, pmz, mkv)
    dg, dya, dyb, dym, dy, loss_p, gg_post, mb, dob, du = _mid(ya, yb, ym, pg, x2, tgt2, g_post, w_up_g, w_out_f)
    gw_out, gw_up = _gw_mid(mb, dob, (ya, yb, ym), du)

    core = lax.axis_index("c").astype(jnp.int32).reshape(1)
    chip = (2 * lax.axis_index("x") + lax.axis_index("y")).astype(jnp.int32).reshape(1)

    def exchange_start(shares, tag):
        from_sibling = _sibling_exchange(shares, "grads_to_sibling_" + tag)
        chip_shares = _pair_add(shares, from_sibling, core, "grads_pair_add_" + tag)
        return _chip_exchange_start(chip_shares, "grads_to_chips_start_" + tag)

    dmq, dmz, dmkv = _mem_attn_bwd(pmq, pmz, mkv, dym)
    gw_mkv, gg_mem = _mem_kv_bwd(mem2, g_mem, mn, dmkv, w_mkv)
    shares1 = [gw_mkv.reshape(N_DEV, 128, D_MODEL), gw_out.reshape(N_DEV, 128, D_MODEL), gw_up]
    sib = _split_start(_sibling_copies, N_CHIPS, shares1,
                       [lax.empty((N_CHIPS,) + a.shape[1:], a.dtype) for a in shares1], "grads_to_sibling_small_start")
    da, gw_conv = _conv_bwd(pa, dya, w_conv_f + sib[4][0:1, 0:1])
    shares1, from_sibling = _split_wait(_sibling_copies, N_CHIPS, *sib[:4], da, "grads_to_sibling_small_wait")
    send1, recv1, srcs1, lands1, token1 = _chip_exchange_start(
        _pair_add(shares1, from_sibling, core, "grads_pair_add_small"), "grads_to_chips_start_small")
    dq, dbz, dkv, g_sink = _attn_bwd(pq, pkv, pbz, dyb, sink + token1[0, 0], tabs)
    dparts = (da, dq, dkv, dbz, dmq, dmz, dg)
    gw_int = _gw_in(dparts, h)
    send2, recv2, srcs2, lands2, token2 = exchange_start([gw_int.reshape(N_DEV, SHARD_IN, D_MODEL)], "w_in")
    grad_x, gg_pre = _dh_bwd(dparts, x2, dy, g_pre + token2[0:1, 0:1], w_int)
    (o_mkv, o_out, o_up, o_int), (l_mkv, l_out, l_up, l_int) = _chip_exchange_wait(
        send1 + send2, recv1 + recv2, srcs1 + srcs2, lands1 + lands2, grad_x, "grads_to_chips_wait")

    loss_row, small_g, sd, sm, sv = _small_step(
        (gg_pre, gw_conv, g_sink, gg_mem, gg_post, loss_p),
        [g_pre, w_conv[0], attn_sink, g_mem, g_post],
        [m_g_pre, m_w_conv[0], m_attn_sink, m_g_mem, m_g_post],
        [v_g_pre, v_w_conv[0], v_attn_sink, v_g_mem, v_g_post])
    loss = loss_row[0, 0]
    g_g_pre, g_conv, g_sink_tot, g_g_mem, g_g_post = small_g

    g_w_in, d_w_in, nm_w_in, nv_w_in = (t.T for t in _sum_adamw(
        o_int, l_int, chip, 0, w_in[0].T, m_w_in[0].T, v_w_in[0].T, "adamw_w_in", tiles=2))
    (g_mkv, d_mkv, nm_mkv, nv_mkv), (g_out, d_out, nm_out, nv_out), *up = _sum_adamw_group(
        [(o_mkv, l_mkv, 0, w_mem_kv[0], m_w_mem_kv[0], v_w_mem_kv[0]),
         (o_out, l_out, 0, w_out[0], m_w_out[0], v_w_out[0]),
         (o_up, l_up, 0, w_up_a[0], m_w_up_a[0], v_w_up_a[0]),
         (o_up, l_up, 1, w_up_b[0], m_w_up_b[0], v_w_up_b[0]),
         (o_up, l_up, 2, w_up_m[0], m_w_up_m[0], v_w_up_m[0])], chip, "adamw_mid_weights")

    def lead(a):
        return a[None]

    grads = [g_g_pre, lead(g_w_in), lead(g_conv), g_sink_tot, g_g_mem, lead(g_mkv), lead(up[0][0]),
             lead(up[1][0]), lead(up[2][0]), lead(g_out), g_g_post]

    def assemble(small, big_in, big_mkv, big_up, big_out):
        return [small[0], lead(big_in), lead(small[1]), small[2], small[3], lead(big_mkv), lead(big_up[0]),
                lead(big_up[1]), lead(big_up[2]), lead(big_out), small[4]]

    deltas = assemble(sd, d_w_in, d_mkv, [u[1] for u in up], d_out)
    new_m = assemble(sm, nm_w_in, nm_mkv, [u[2] for u in up], nm_out)
    new_v = assemble(sv, nv_w_in, nv_mkv, [u[3] for u in up], nv_out)
    return (loss, grad_x[None], *grads, *deltas, *new_m, *new_v)
```

```python
import functools

import jax
import jax.numpy as jnp
from jax import lax
from jax.experimental import pallas as pl
from jax.experimental.pallas import tpu as pltpu

F32 = jnp.float32
BF16 = jnp.bfloat16
MESH = pl.DeviceIdType.MESH

N_DEV = 8
D_MODEL = 1024
EPS = 1e-6
ROPE_THETA = 500000.0
ROT_DIM = 16
HEAD_DIM = 64
ATTN_BLOCK = 128
MEM_HEADS = 4
MEM_HEAD_DIM = 128
ATTN_SCALE = HEAD_DIM ** -0.5
MEM_SCALE = MEM_HEAD_DIM ** -0.5

ADAM_LR = 0.001
ADAM_B1 = 0.9
ADAM_B2 = 0.999
ADAM_EPS = 1e-08
ADAM_WD = 0.01
ADAM_STEP = 10

SEG_A = (0, 2048)
SEG_BQ = (2048, 512)
SEG_BKV = (2560, 256)
SEG_BZ = (2816, 512)
SEG_MQ = (3328, 512)
SEG_MZ = (3840, 512)
SEG_G = (4352, 3072)
SEGS = (SEG_A, SEG_BQ, SEG_BKV, SEG_BZ, SEG_MQ, SEG_MZ, SEG_G)
IN_WIDTH = 7424
SHARD_IN = IN_WIDTH // N_DEV

V7X_VMEM_BYTES = 64 * 1024 * 1024
ANY = pl.BlockSpec(memory_space=pl.ANY)


def _params(vmem_mb):
    assert vmem_mb * 1024 * 1024 < V7X_VMEM_BYTES
    return pltpu.CompilerParams(dimension_semantics=("arbitrary",), vmem_limit_bytes=vmem_mb * 1024 * 1024)


def _full(shape):
    zeros = (0,) * len(shape)
    return pl.BlockSpec(shape, lambda i: zeros)


def _rows(tm, width):
    return pl.BlockSpec((tm, width), lambda i: (i, 0))


def _dot(a, b):
    return jnp.dot(a, b, preferred_element_type=F32)


def _dot_nt(a, b):
    return lax.dot_general(a, b, (((1,), (1,)), ((), ())), preferred_element_type=F32)


def _dot_tn(a, b):
    return lax.dot_general(a, b, (((0,), (0,)), ((), ())), preferred_element_type=F32)


def _sigmoid(z):
    return 1.0 / (1.0 + jnp.exp(-z))


def _rope(t, cs, s1, s2):
    return t * cs + pltpu.roll(t, 120, 1) * s1 + pltpu.roll(t, 8, 1) * s2


def _rope_t(d, cs, s1, s2):
    return d * cs + pltpu.roll(d * s1, 8, 1) + pltpu.roll(d * s2, 120, 1)


def _rope_side(s):
    half = ROT_DIM // 2
    inv_freq = jnp.power(jnp.float32(ROPE_THETA), -jnp.arange(half, dtype=F32) * (2.0 / ROT_DIM))
    freq_row = jnp.tile(jnp.concatenate([inv_freq, inv_freq, jnp.zeros((HEAD_DIM - ROT_DIM,), F32)]), 2)[None, :]

    def fn(in_refs, out_refs):
        (f_ref,), (t_ref,) = in_refs, out_refs
        pos = lax.broadcasted_iota(jnp.int32, (s, 128), 0).astype(F32)
        d = lax.broadcasted_iota(jnp.int32, (s, 128), 1) & (HEAD_DIM - 1)
        ang = pos * f_ref[...]
        cos, sin = jnp.cos(ang), jnp.sin(ang)
        lo, hi = d < half, (d >= half) & (d < ROT_DIM)
        t_ref[0] = jnp.where(lo | hi, cos, 1.0)
        t_ref[1] = jnp.where(lo, -sin, 0.0)
        t_ref[2] = jnp.where(hi, sin, 0.0)

    return [freq_row], [jax.ShapeDtypeStruct((3, s, 128), F32)], fn


def _load_once(pairs, sems):
    @pl.when(pl.program_id(0) == 0)
    def _():
        cps = [pltpu.make_async_copy(src, dst, sems.at[k]) for k, (src, dst) in enumerate(pairs)]
        for cp in cps:
            cp.start()
        for cp in cps:
            cp.wait()


def _my_place():
    x, y, c = lax.axis_index("x"), lax.axis_index("y"), lax.axis_index("c")
    return x, y, c


def _all_gather(arrs, name, splits=None, side=None):
    n = len(arrs)
    if splits is None:
        splits = [[(0, a.shape[0])] for a in arrs]
    pieces = [(a, r0, rn) for a in range(n) for r0, rn in splits[a]]
    n_p = len(pieces)
    side_in, side_out, side_fn = side if side is not None else ((), (), None)
    m, q = len(side_in), len(side_out)

    def body(*refs):
        ins, outs = refs[:n], refs[n + m:2 * n + m]
        send_sems, recv_sems, local_sems = refs[2 * n + m + q:]
        x, y, c = _my_place()
        me, sibling = (x, y, c), (x, y, 1 - c)

        def route(core):
            first = (jnp.bitwise_xor(x, 1 - core), jnp.bitwise_xor(y, core), core)
            second = (jnp.bitwise_xor(x, core), jnp.bitwise_xor(y, 1 - core), core)
            return first, second, (1 - x, 1 - y, core)

        def idx(px, py, pc):
            return 4 * px + 2 * py + pc

        def copy(p, k, block, to, own=False):
            a, r0, rn = pieces[p]
            dst = outs[a].at[idx(*block), pl.ds(r0, rn)]
            return pltpu.make_async_remote_copy(
                src_ref=ins[a].at[pl.ds(r0, rn)] if own else dst, dst_ref=dst,
                send_sem=send_sems.at[p * 7 + k], recv_sem=recv_sems.at[p * 7 + k],
                device_id=to, device_id_type=MESH)

        nbr1, nbr2, diag = route(c)
        mine = [pltpu.make_async_copy(ins[a], outs[a].at[idx(*me)], local_sems.at[a]) for a in range(n)]
        for cp in mine:
            cp.start()
        sent = []
        for p in range(n_p):
            for k, to in enumerate((sibling, nbr1, nbr2)):
                sent.append(copy(p, k, me, to, own=True))
        for cp in sent:
            cp.start()
        if side_fn is not None:
            side_fn(refs[n:n + m], refs[2 * n + m:2 * n + m + q])
        for k_in, block, onward in ((1, nbr1, ((3, nbr2), (4, sibling))), (2, nbr2, ((5, sibling),)),
                                    (3, diag, ((6, sibling),))):
            for p in range(n_p):
                copy(p, k_in, block, me).wait_recv()
                for k_out, to in onward:
                    cp = copy(p, k_out, block, to)
                    cp.start()
                    sent.append(cp)
        s1, s2, sd = route(1 - c)
        for k_in, block in ((0, sibling), (4, s1), (5, s2), (6, sd)):
            for p in range(n_p):
                copy(p, k_in, block, me).wait_recv()
        for cp in sent:
            cp.wait_send()
        for cp in mine:
            cp.wait()

    return pl.pallas_call(
        body, name=name,
        out_shape=[jax.ShapeDtypeStruct((N_DEV,) + a.shape, a.dtype) for a in arrs] + list(side_out),
        in_specs=[ANY] * n + [pl.BlockSpec(memory_space=pltpu.VMEM)] * m,
        out_specs=[ANY] * n + [pl.BlockSpec(memory_space=pltpu.VMEM)] * q,
        scratch_shapes=[pltpu.SemaphoreType.DMA((7 * n_p,)), pltpu.SemaphoreType.DMA((7 * n_p,)),
                        pltpu.SemaphoreType.DMA((n,))],
        compiler_params=pltpu.CompilerParams(vmem_limit_bytes=32 * 1024 * 1024),
    )(*arrs, *side_in)


N_CHIPS = 4


def _sibling_exchange(arrs, name):
    n = len(arrs)

    def body(*refs):
        ins, outs = refs[:n], refs[n:2 * n]
        send_sems, recv_sems = refs[2 * n:]
        x, y, c = _my_place()
        sibling = (x, y, 1 - c)

        def copy(a, j):
            return pltpu.make_async_remote_copy(
                src_ref=ins[a].at[2 * j + (1 - c)], dst_ref=outs[a].at[j],
                send_sem=send_sems.at[a * N_CHIPS + j], recv_sem=recv_sems.at[a * N_CHIPS + j],
                device_id=sibling, device_id_type=MESH)

        cps = [copy(a, j) for j in range(N_CHIPS) for a in range(n)]
        for cp in cps:
            cp.start()
        for cp in cps:
            cp.wait_recv()
        for cp in cps:
            cp.wait_send()

    return pl.pallas_call(
        body, name=name,
        out_shape=[jax.ShapeDtypeStruct((N_CHIPS,) + a.shape[1:], a.dtype) for a in arrs],
        in_specs=[ANY] * n, out_specs=[ANY] * n,
        scratch_shapes=[pltpu.SemaphoreType.DMA((N_CHIPS * n,)), pltpu.SemaphoreType.DMA((N_CHIPS * n,))],
    )(*arrs)


def _sibling_copies(srcs, lands, send_sems, recv_sems):
    x, y, c = _my_place()
    cps = []
    for j in range(N_CHIPS):
        for a in range(len(srcs)):
            k = a * N_CHIPS + j
            cps.append(pltpu.make_async_remote_copy(
                src_ref=srcs[a].at[2 * j + (1 - c)], dst_ref=lands[a].at[j], send_sem=send_sems[k],
                recv_sem=recv_sems[k], device_id=(x, y, 1 - c), device_id_type=MESH))
    return cps


def _pair_add(mine, recv, core, name):
    n = len(mine)

    def body(c_ref, *refs):
        for a in range(n):
            refs[2 * n + a][...] = (refs[a][...].astype(F32) + refs[n + a][...].astype(F32)).astype(BF16)

    def blk(a):
        return (None,) + a.shape[1:]

    grid_spec = pltpu.PrefetchScalarGridSpec(
        num_scalar_prefetch=1, grid=(N_CHIPS,),
        in_specs=[pl.BlockSpec(blk(a), lambda j, c_ref: (2 * j + c_ref[0], 0, 0)) for a in mine]
        + [pl.BlockSpec(blk(a), lambda j, c_ref: (j, 0, 0)) for a in recv],
        out_specs=[pl.BlockSpec(blk(a), lambda j, c_ref: (j, 0, 0)) for a in recv])
    return pl.pallas_call(
        body, name=name, grid_spec=grid_spec,
        out_shape=[jax.ShapeDtypeStruct(a.shape, BF16) for a in recv],
        compiler_params=_params(32),
    )(core, *mine, *recv)


HBM = pl.BlockSpec(memory_space=pltpu.HBM)
SEM = pl.BlockSpec(memory_space=pltpu.SEMAPHORE)
N_PEER_CHIPS = 3


def _chip_copies(srcs, lands, send_sems, recv_sems):
    x, y, c = _my_place()
    my_chip = 2 * x + y
    peers = [(x, 1 - y), (1 - x, y), (1 - x, 1 - y)]
    cps = []
    for k, (px, py) in enumerate(peers):
        for a in range(len(srcs)):
            j = a * N_PEER_CHIPS + k
            cps.append(pltpu.make_async_remote_copy(
                src_ref=srcs[a].at[2 * px + py], dst_ref=lands[a].at[my_chip],
                send_sem=send_sems[j], recv_sem=recv_sems[j],
                device_id=(px, py, c), device_id_type=MESH))
    return cps


N_PEERS = N_DEV - 1


def _gather_copies(srcs, lands, send_sems, recv_sems):
    x, y, c = _my_place()
    me_idx = 4 * x + 2 * y + c
    flips = [(0, 0, 1), (0, 1, 0), (1, 0, 0), (0, 1, 1), (1, 0, 1), (1, 1, 0), (1, 1, 1)]
    cps = []
    for k, (fx, fy, fc) in enumerate(flips):
        peer = ((1 - x) if fx else x, (1 - y) if fy else y, (1 - c) if fc else c)
        for a in range(len(srcs)):
            j = a * N_PEERS + k
            cps.append(pltpu.make_async_remote_copy(
                src_ref=srcs[a], dst_ref=lands[a].at[me_idx], send_sem=send_sems[j], recv_sem=recv_sems[j],
                device_id=peer, device_id_type=MESH))
    return cps


def _split_start(copies, per_array, arrs, lands, name):
    arrs, lands = list(arrs), list(lands)
    n = len(arrs)
    k = n * per_array

    def body(*refs):
        srcs, land_refs = refs[:n], refs[n:2 * n]
        send_sems, recv_sems = refs[2 * n:2 * n + k], refs[2 * n + k:2 * n + 2 * k]
        token = refs[-1]
        for cp in copies(srcs, land_refs, send_sems, recv_sems):
            cp.start()
        token[...] = jnp.zeros_like(token)

    hbm_arrs = [pltpu.with_memory_space_constraint(a, pltpu.HBM) for a in arrs]
    lands = [pltpu.with_memory_space_constraint(a, pltpu.HBM) for a in lands]
    res = pl.pallas_call(
        body, name=name,
        out_shape=[pltpu.SemaphoreType.DMA(())] * (2 * k) + [pltpu.HBM(a.shape, a.dtype) for a in arrs + lands]
        + [jax.ShapeDtypeStruct((8, 128), F32)],
        in_specs=[HBM] * (2 * n),
        out_specs=[SEM] * (2 * k) + [HBM] * (2 * n) + [pl.BlockSpec(memory_space=pltpu.VMEM)],
        input_output_aliases={a: 2 * k + a for a in range(2 * n)},
        compiler_params=pltpu.CompilerParams(has_side_effects=pltpu.SideEffectType.DATAFLOW_SIDE_EFFECTING),
    )(*hbm_arrs, *lands)
    return res[:k], res[k:2 * k], res[2 * k:2 * k + n], res[2 * k + n:2 * k + 2 * n], res[-1]


def _split_wait(copies, per_array, send_sems, recv_sems, srcs, lands, after, name):
    n = len(srcs)
    k = n * per_array

    def body(*refs):
        src_refs, land_refs = refs[:n], refs[n:2 * n]
        s_sems, r_sems = refs[2 * n:2 * n + k], refs[2 * n + k:2 * n + 2 * k]
        for cp in copies(src_refs, land_refs, s_sems, r_sems):
            cp.wait_send()
            cp.wait_recv()

    res = pl.pallas_call(
        body, name=name,
        out_shape=[pltpu.HBM(a.shape, a.dtype) for a in list(srcs) + list(lands)],
        in_specs=[HBM] * (2 * n) + [SEM] * (2 * k) + [ANY],
        out_specs=[HBM] * (2 * n),
        input_output_aliases={a: a for a in range(2 * n)},
        compiler_params=pltpu.CompilerParams(has_side_effects=pltpu.SideEffectType.DATAFLOW_SIDE_EFFECTING),
    )(*srcs, *lands, *send_sems, *recv_sems, after)
    return res[:n], res[n:]


def _chip_exchange_start(arrs, name):
    return _split_start(_chip_copies, N_PEER_CHIPS, arrs, [lax.empty(a.shape, a.dtype) for a in arrs], name)


def _chip_exchange_wait(send_sems, recv_sems, srcs, lands, after, name):
    return _split_wait(_chip_copies, N_PEER_CHIPS, send_sems, recv_sems, srcs, lands, after, name)


def _gather_start(arrs, me_idx, name):
    lands = [lax.dynamic_update_slice(lax.empty((N_DEV,) + a.shape, a.dtype), a[None], (me_idx, 0, 0)) for a in arrs]
    return _split_start(_gather_copies, N_PEERS, arrs, lands, name)


def _gather_wait(send_sems, recv_sems, srcs, lands, after, name):
    return _split_wait(_gather_copies, N_PEERS, send_sems, recv_sems, srcs, lands, after, name)[1]


def _proj_fwd(x, g_pre, w_int, tabs):
    s = x.shape[0]
    tm = min(512, s)

    def body(x_ref, g_ref, t_ref, w_hbm,
             h_ref, pa_ref, pq_ref, pkv_ref, pbz_ref, pmq_ref, pmz_ref, pg_ref, w_vm, sems):
        _load_once([(w_hbm, w_vm)], sems)
        xf = x_ref[...]
        r = lax.rsqrt(jnp.mean(xf * xf, axis=-1, keepdims=True) + EPS)
        h = ((xf * r) * g_ref[...]).astype(BF16)
        h_ref[...] = h
        cs, s1, s2 = t_ref[0], t_ref[1], t_ref[2]

        def mm(seg, c0, width):
            return _dot_nt(h, w_vm[seg[0] + c0:seg[0] + c0 + width, :])

        for c0 in range(0, SEG_A[1], 512):
            pa_ref[:, c0:c0 + 512] = mm(SEG_A, c0, 512).astype(BF16)
        q = mm(SEG_BQ, 0, 512)
        for b in range(4):
            pq_ref[:, 128 * b:128 * b + 128] = _rope(q[:, 128 * b:128 * b + 128], cs, s1, s2).astype(BF16)
        kv = mm(SEG_BKV, 0, 256)
        pkv_ref[:, 0:128] = _rope(kv[:, 0:128], cs, s1, s2).astype(BF16)
        pkv_ref[:, 128:256] = kv[:, 128:256].astype(BF16)
        pbz_ref[...] = mm(SEG_BZ, 0, 512).astype(BF16)
        pmq_ref[...] = mm(SEG_MQ, 0, 512).astype(BF16)
        pmz_ref[...] = mm(SEG_MZ, 0, 512).astype(BF16)
        for c0 in range(0, SEG_G[1], 512):
            pg_ref[:, c0:c0 + 512] = mm(SEG_G, c0, 512).astype(BF16)

    widths = (D_MODEL, 2048, 512, 256, 512, 512, 512, 3072)
    return pl.pallas_call(
        body, name="proj_fwd", grid=(s // tm,),
        out_shape=[jax.ShapeDtypeStruct((s, w), BF16) for w in widths],
        in_specs=[_rows(tm, D_MODEL), _full((1, D_MODEL)), pl.BlockSpec((3, tm, 128), lambda i: (0, i, 0)), ANY],
        out_specs=[_rows(tm, w) for w in widths],
        scratch_shapes=[pltpu.VMEM((IN_WIDTH, D_MODEL), BF16), pltpu.SemaphoreType.DMA((1,))],
        compiler_params=_params(52),
    )(x, g_pre, tabs, w_int)


def _mem_kv_fwd(mem, g_mem, w_mkv):
    m = mem.shape[0]

    def body(mem_ref, g_ref, w_ref, mn_ref, mkv_ref):
        xf = mem_ref[...]
        r = lax.rsqrt(jnp.mean(xf * xf, axis=-1, keepdims=True) + EPS)
        mn = ((xf * r) * g_ref[...]).astype(BF16)
        mn_ref[...] = mn
        mkv_ref[...] = _dot(mn, w_ref[...]).astype(BF16)

    return pl.pallas_call(
        body, name="mem_kv_fwd", grid=(1,),
        out_shape=[jax.ShapeDtypeStruct((m, D_MODEL), BF16)] * 2,
        in_specs=[_full((m, D_MODEL)), _full((1, D_MODEL)), _full((D_MODEL, D_MODEL))],
        out_specs=[_full((m, D_MODEL))] * 2,
        compiler_params=_params(32),
    )(mem, g_mem, w_mkv)


def _halo_specs(s, tm, rows, width):
    nblk = s // rows
    prev = pl.BlockSpec((rows, width), lambda i: (jnp.maximum(i * (tm // rows) - 1, 0), 0))
    nxt = pl.BlockSpec((rows, width), lambda i: (jnp.minimum((i + 1) * (tm // rows), nblk - 1), 0))
    return prev, nxt


def _conv_common(pa, cu_prev, cu_next, w, tm):
    b, c, u, z = (pa[:, 512 * k:512 * k + 512] for k in range(4))
    cu = c * u
    row = lax.broadcasted_iota(jnp.int32, (tm, 512), 0)
    cu_m1 = jnp.where(row == 0, cu_prev, pltpu.roll(cu, 1, 0))
    cu_p1 = jnp.where(row == tm - 1, cu_next, pltpu.roll(cu, tm - 1, 0))
    y = cu_m1 * w[0:1] + cu * w[1:2] + cu_p1 * w[2:3]
    sig = _sigmoid(z)
    return b, c, u, z, cu, cu_m1, cu_p1, y, sig, row


def _conv_fwd(pa, w_conv):
    s = pa.shape[0]
    tm = min(512, s)
    nt = s // tm

    def body(pa_ref, pp_ref, pn_ref, w_ref, ya_ref):
        i = pl.program_id(0)
        prev_row = pp_ref[...].astype(F32)[15:16, :]
        next_row = pn_ref[...].astype(F32)[0:1, :]
        b, _, _, z, _, _, _, y, sig, _ = _conv_common(
            pa_ref[...].astype(F32),
            jnp.where(i == 0, 0.0, prev_row[:, 512:1024] * prev_row[:, 1024:1536]),
            jnp.where(i == nt - 1, 0.0, next_row[:, 512:1024] * next_row[:, 1024:1536]), w_ref[...], tm)
        ya_ref[...] = (b * y * (z * sig)).astype(BF16)

    prev, nxt = _halo_specs(s, tm, 16, 2048)
    return pl.pallas_call(
        body, name="conv_fwd", grid=(nt,),
        out_shape=jax.ShapeDtypeStruct((s, 512), BF16),
        in_specs=[_rows(tm, 2048), prev, nxt, _full((3, 512))],
        out_specs=_rows(tm, 512),
        compiler_params=_params(48),
    )(pa, pa, pa, w_conv)


def _heads_to_lanes(a, g, row):
    low = row < HEAD_DIM
    parts = []
    for b in (2 * g, 2 * g + 1):
        t = jnp.transpose(a[:, 128 * b:128 * b + 128])
        swapped = pltpu.roll(t, HEAD_DIM, 0)
        if g == 0:
            parts += [jnp.where(low, t, 0.0), jnp.where(low, swapped, 0.0)]
        else:
            parts += [jnp.where(low, 0.0, swapped), jnp.where(low, 0.0, t)]
    return jnp.concatenate(parts, axis=1)


def _lanes_to_heads(t0, t1, row):
    low = row < HEAD_DIM
    blocks = []
    for b in range(4):
        g = b // 2
        tg = (t0, t1)[g]
        je = 2 * (b - 2 * g)
        even, odd = tg[:, 128 * je:128 * je + 128], tg[:, 128 * je + 128:128 * je + 256]
        if g == 0:
            t = jnp.where(low, even, pltpu.roll(odd, HEAD_DIM, 0))
        else:
            t = jnp.where(low, pltpu.roll(even, HEAD_DIM, 0), odd)
        blocks.append(jnp.transpose(t))
    return jnp.concatenate(blocks, axis=1)


WINDOW_KEYS = 3 * ATTN_BLOCK
STACKED = 4 * ATTN_BLOCK
KEY_CHUNK = 32
BLOCKS_PER_STEP = 4


def _fill_band_bias(bias, nb):
    assert nb >= 2
    c = lax.broadcasted_iota(jnp.int32, (WINDOW_KEYS, STACKED), 0)
    r = lax.broadcasted_iota(jnp.int32, (WINDOW_KEYS, STACKED), 1) & (ATTN_BLOCK - 1)
    band = (c >= r) & (c <= r + 2 * ATTN_BLOCK)
    for v, ok in enumerate((band, band & (c >= ATTN_BLOCK), band & (c < 2 * ATTN_BLOCK))):
        bias[v] = jnp.where(ok, 0.0, -jnp.inf)


def _bias_variant(n, nb):
    return jnp.where(n == 0, 1, jnp.where(n == nb - 1, 2, 0))


def _sink_row(sink_ref, g):
    return jnp.concatenate([jnp.full((1, ATTN_BLOCK), sink_ref[4 * g + j], F32) for j in range(4)], axis=1)


def _softmax_keys_major(sc, bias, variant, sink, e_scr):
    chunks = [pl.ds(k * KEY_CHUNK, KEY_CHUNK) for k in range(WINDOW_KEYS // KEY_CHUNK)]
    rows = [slice(k * KEY_CHUNK, (k + 1) * KEY_CHUNK) for k in range(WINDOW_KEYS // KEY_CHUNK)]
    m_run = jnp.full((KEY_CHUNK, STACKED), -jnp.inf, F32)
    for ck, rw in zip(chunks, rows):
        m_run = jnp.maximum(m_run, sc[rw] * ATTN_SCALE + bias[variant, ck, :])
    m = jnp.maximum(jnp.max(m_run, axis=0, keepdims=True), sink)
    l_run = jnp.zeros((KEY_CHUNK, STACKED), F32)
    for ck, rw in zip(chunks, rows):
        e = jnp.exp(sc[rw] * ATTN_SCALE + bias[variant, ck, :] - m)
        l_run += e
        e_scr[rw, :] = e.astype(BF16)
    es = jnp.exp(sink - m)
    inv = 1.0 / (jnp.sum(l_run, axis=0, keepdims=True) + es)
    return inv, es * inv


def _fill_padded(kv_ref, kpad, vpad, s):
    zero = jnp.zeros((ATTN_BLOCK, 128), BF16)
    kpad[0:ATTN_BLOCK, :] = zero
    vpad[0:ATTN_BLOCK, :] = zero
    kpad[ATTN_BLOCK + s:2 * ATTN_BLOCK + s, :] = zero
    vpad[ATTN_BLOCK + s:2 * ATTN_BLOCK + s, :] = zero
    kpad[ATTN_BLOCK:ATTN_BLOCK + s, :] = kv_ref[:, 0:128]
    vpad[ATTN_BLOCK:ATTN_BLOCK + s, :] = kv_ref[:, 128:256]


def _attn_fwd(pq, pkv, pbz, sink):
    s = pq.shape[0]
    nb = s // ATTN_BLOCK

    def body(sink_ref, q_ref, z_ref, kv_ref, yb_ref, kpad, vpad, bias, e_scr):
        i = pl.program_id(0)

        @pl.when(i == 0)
        def _():
            _fill_padded(kv_ref, kpad, vpad, s)
            _fill_band_bias(bias, nb)

        row = lax.broadcasted_iota(jnp.int32, (ATTN_BLOCK, 128), 0)
        for b in range(BLOCKS_PER_STEP):
            n = i * BLOCKS_PER_STEP + b
            rows = slice(b * ATTN_BLOCK, (b + 1) * ATTN_BLOCK)
            start = pl.multiple_of(n * ATTN_BLOCK, ATTN_BLOCK)
            kw, vw = kpad[pl.ds(start, WINDOW_KEYS), :], vpad[pl.ds(start, WINDOW_KEYS), :]
            qf = q_ref[rows, :].astype(F32)
            variant = _bias_variant(n, nb)
            outs = []
            for g in range(2):
                e_bg = e_scr.at[2 * b + g]
                qt = _heads_to_lanes(qf, g, row).astype(BF16)
                inv, _ = _softmax_keys_major(_dot(kw, qt), bias, variant, _sink_row(sink_ref, g), e_bg)
                outs.append(_dot_tn(vw, e_bg[...]) * inv)
            attn = _lanes_to_heads(outs[0], outs[1], row)
            z = z_ref[rows, :].astype(F32)
            yb_ref[rows, :] = (attn * (z * _sigmoid(z))).astype(BF16)

    tq = BLOCKS_PER_STEP * ATTN_BLOCK
    return pl.pallas_call(
        body, name="attn_fwd", grid=(s // tq,),
        out_shape=jax.ShapeDtypeStruct((s, 512), BF16),
        in_specs=[pl.BlockSpec(memory_space=pltpu.SMEM), _rows(tq, 512), _rows(tq, 512), _full((s, 256))],
        out_specs=_rows(tq, 512),
        scratch_shapes=[pltpu.VMEM((s + 2 * ATTN_BLOCK, 128), BF16)] * 2
        + [pltpu.VMEM((3, WINDOW_KEYS, STACKED), F32),
           pltpu.VMEM((2 * BLOCKS_PER_STEP, WINDOW_KEYS, STACKED), BF16)],
        compiler_params=_params(32),
    )(sink, pq, pbz, pkv)


def _mem_softmax_t(q, mk):
    sc = _dot_nt(mk, q) * MEM_SCALE
    e = jnp.exp(sc - jnp.max(sc, axis=0, keepdims=True))
    return e * (1.0 / jnp.sum(e, axis=0, keepdims=True))


def _mem_attn_fwd(pmq, pmz, mkv):
    s = pmq.shape[0]
    m = mkv.shape[0]
    tm = min(512, s)

    def body(q_ref, z_ref, mk_ref, mv_ref, ym_ref):
        z = z_ref[...].astype(F32)
        sz = z * _sigmoid(z)
        for h in range(MEM_HEADS):
            cols = slice(128 * h, 128 * h + 128)
            pt = _mem_softmax_t(q_ref[:, cols], mk_ref[:, cols])
            o = _dot_tn(pt.astype(BF16), mv_ref[:, cols])
            ym_ref[:, cols] = (o * sz[:, cols]).astype(BF16)

    return pl.pallas_call(
        body, name="mem_attn_fwd", grid=(s // tm,),
        out_shape=jax.ShapeDtypeStruct((s, 512), BF16),
        in_specs=[_rows(tm, 512), _rows(tm, 512), pl.BlockSpec((m, 512), lambda i: (0, 0)),
                  pl.BlockSpec((m, 512), lambda i: (0, 1))],
        out_specs=_rows(tm, 512),
        compiler_params=_params(32),
    )(pmq, pmz, mkv, mkv)


def _mid(ya, yb, ym, pg, x, target, g_post, w_up, w_out):
    s = x.shape[0]
    tm = min(256, s)
    nt = s // tm

    def body(ya_ref, yb_ref, ym_ref, pg_ref, x_ref, t_ref, gp_ref, wup_hbm, wout_hbm,
             dg_ref, dya_ref, dyb_ref, dym_ref, dy_ref, loss_ref, ggp_ref, mb_ref, dob_ref, du_ref,
             wup_vm, wout_vm, sems):
        i = pl.program_id(0)
        _load_once([(wup_hbm.at[d], wup_vm.at[:, pl.ds(128 * d, 128)]) for d in range(N_DEV)]
                   + [(wout_hbm, wout_vm)], sems)

        @pl.when(i == 0)
        def _():
            loss_ref[...] = jnp.zeros_like(loss_ref)
            ggp_ref[...] = jnp.zeros_like(ggp_ref)

        ys = (ya_ref[...], yb_ref[...], ym_ref[...])
        us = [_dot(ys[k], wup_vm[512 * k:512 * k + 512, :]) for k in range(3)]
        gates = [_sigmoid(pg_ref[:, 1024 * k:1024 * k + 1024].astype(F32)) for k in range(3)]
        merged = gates[0] * us[0] + gates[1] * us[1] + gates[2] * us[2]
        mb = merged.astype(BF16)
        mb_ref[...] = mb
        out = _dot(mb, wout_vm[...])
        r = lax.rsqrt(jnp.mean(out * out, axis=-1, keepdims=True) + EPS)
        on = out * r
        gp = gp_ref[...]
        err = (x_ref[...] + on * gp) - t_ref[...]
        loss_ref[...] += 0.5 * jnp.sum(err * err) * (1.0 / D_MODEL)
        dy = err * (1.0 / D_MODEL)
        dy_ref[...] = dy
        ggp_ref[...] += jnp.sum(dy * on, axis=0, keepdims=True)
        a = dy * gp
        d_out = r * (a - on * jnp.mean(a * on, axis=-1, keepdims=True))
        dob = d_out.astype(BF16)
        dob_ref[...] = dob
        d_merged = _dot_nt(dob, wout_vm[...])
        d_refs = (dya_ref, dyb_ref, dym_ref)
        for k in range(3):
            g = gates[k]
            dg_ref[:, 1024 * k:1024 * k + 1024] = (d_merged * us[k] * g * (1.0 - g)).astype(BF16)
            du = (d_merged * g).astype(BF16)
            du_ref[k] = du
            d_refs[k][...] = _dot_nt(du, wup_vm[512 * k:512 * k + 512, :]).astype(BF16)

    return pl.pallas_call(
        body, name="mid", grid=(nt,),
        out_shape=[jax.ShapeDtypeStruct((s, 3072), BF16)] + [jax.ShapeDtypeStruct((s, 512), BF16)] * 3
        + [jax.ShapeDtypeStruct((s, D_MODEL), F32), jax.ShapeDtypeStruct((8, 128), F32),
           jax.ShapeDtypeStruct((1, D_MODEL), F32), jax.ShapeDtypeStruct((s, D_MODEL), BF16),
           jax.ShapeDtypeStruct((s, D_MODEL), BF16), jax.ShapeDtypeStruct((3, s, D_MODEL), BF16)],
        in_specs=[_rows(tm, 512)] * 3 + [_rows(tm, 3072), _rows(tm, D_MODEL), _rows(tm, D_MODEL),
                                         _full((1, D_MODEL)), ANY, ANY],
        out_specs=[_rows(tm, 3072)] + [_rows(tm, 512)] * 3
        + [_rows(tm, D_MODEL), _full((8, 128)), _full((1, D_MODEL)), _rows(tm, D_MODEL), _rows(tm, D_MODEL),
           pl.BlockSpec((3, tm, D_MODEL), lambda i: (0, i, 0))],
        scratch_shapes=[pltpu.VMEM((1536, D_MODEL), BF16), pltpu.VMEM((D_MODEL, D_MODEL), BF16),
                        pltpu.SemaphoreType.DMA((N_DEV + 1,))],
        compiler_params=_params(56),
    )(ya, yb, ym, pg, x, target, g_post, w_up, w_out)


def _gw_mid(mb, dob, ys, du):
    s = mb.shape[0]
    tn = 256

    def out_body(mb_ref, dob_ref, o_ref):
        o_ref[...] = _dot_tn(mb_ref[...], dob_ref[...]).astype(BF16)

    gw_out = pl.pallas_call(
        out_body, name="gw_out", grid=(D_MODEL // tn,),
        out_shape=jax.ShapeDtypeStruct((D_MODEL, D_MODEL), BF16),
        in_specs=[pl.BlockSpec((s, tn), lambda j: (0, j)), _full((s, D_MODEL))],
        out_specs=pl.BlockSpec((tn, D_MODEL), lambda j: (j, 0)),
        compiler_params=_params(48),
    )(mb, dob)

    per = 512 // tn

    def up_body(ya_ref, yb_ref, ym_ref, du_ref, o_ref):
        j = pl.program_id(0)
        for k, y_ref in enumerate((ya_ref, yb_ref, ym_ref)):
            @pl.when(j // per == k)
            def _(y_ref=y_ref):
                res = _dot_tn(y_ref[...], du_ref[...])
                for d in range(N_DEV):
                    o_ref[d] = res[:, 128 * d:128 * d + 128].astype(BF16)

    def y_spec(k):
        return pl.BlockSpec((s, tn), lambda j: (0, jnp.clip(j - per * k, 0, per - 1)))

    gw_up = pl.pallas_call(
        up_body, name="gw_up", grid=(3 * per,),
        out_shape=jax.ShapeDtypeStruct((N_DEV, 1536, 128), BF16),
        in_specs=[y_spec(0), y_spec(1), y_spec(2), pl.BlockSpec((None, s, D_MODEL), lambda j: (j // per, 0, 0))],
        out_specs=pl.BlockSpec((N_DEV, tn, 128), lambda j: (0, j, 0)),
        compiler_params=_params(48),
    )(*ys, du)
    return gw_out, gw_up


def _conv_bwd(pa, dya, w_conv):
    s = pa.shape[0]
    tm = min(512, s)
    nt = s // tm

    def body(pa_ref, pp_ref, pn_ref, d_ref, dp_ref, dn_ref, w_ref, da_ref, gw_ref):
        i = pl.program_id(0)
        first, last = i == 0, i == nt - 1

        @pl.when(first)
        def _():
            gw_ref[...] = jnp.zeros_like(gw_ref)

        w = w_ref[...]
        prev_row = pp_ref[...].astype(F32)[15:16, :]
        next_row = pn_ref[...].astype(F32)[0:1, :]
        b, c, u, z, cu, cu_m1, cu_p1, y, sig, row = _conv_common(
            pa_ref[...].astype(F32),
            jnp.where(first, 0.0, prev_row[:, 512:1024] * prev_row[:, 1024:1536]),
            jnp.where(last, 0.0, next_row[:, 512:1024] * next_row[:, 1024:1536]), w, tm)
        sz = z * sig
        dya_t = d_ref[...].astype(F32)
        d_y = dya_t * b * sz

        def halo_dy(p_row, d_row):
            zz = p_row[:, 1536:2048]
            return d_row * p_row[:, 0:512] * (zz * _sigmoid(zz))

        dy_prev = jnp.where(first, 0.0, halo_dy(prev_row, dp_ref[...].astype(F32)[15:16, :]))
        dy_next = jnp.where(last, 0.0, halo_dy(next_row, dn_ref[...].astype(F32)[0:1, :]))
        dy_m1 = jnp.where(row == 0, dy_prev, pltpu.roll(d_y, 1, 0))
        dy_p1 = jnp.where(row == tm - 1, dy_next, pltpu.roll(d_y, tm - 1, 0))
        d_cu = dy_p1 * w[0:1] + d_y * w[1:2] + dy_m1 * w[2:3]
        da_ref[:, 0:512] = (dya_t * y * sz).astype(BF16)
        da_ref[:, 512:1024] = (d_cu * u).astype(BF16)
        da_ref[:, 1024:1536] = (d_cu * c).astype(BF16)
        da_ref[:, 1536:2048] = (dya_t * b * y * (sig * (1.0 + z * (1.0 - sig)))).astype(BF16)
        gw_ref[0:1, :] += jnp.sum(d_y * cu_m1, axis=0, keepdims=True)
        gw_ref[1:2, :] += jnp.sum(d_y * cu, axis=0, keepdims=True)
        gw_ref[2:3, :] += jnp.sum(d_y * cu_p1, axis=0, keepdims=True)

    prev, nxt = _halo_specs(s, tm, 16, 2048)
    dprev, dnxt = _halo_specs(s, tm, 16, 512)
    return pl.pallas_call(
        body, name="conv_bwd", grid=(nt,),
        out_shape=[jax.ShapeDtypeStruct((s, 2048), BF16), jax.ShapeDtypeStruct((8, 512), F32)],
        in_specs=[_rows(tm, 2048), prev, nxt, _rows(tm, 512), dprev, dnxt, _full((3, 512))],
        out_specs=[_rows(tm, 2048), _full((8, 512))],
        compiler_params=_params(48),
    )(pa, pa, pa, dya, dya, dya, w_conv)


def _attn_bwd(pq, pkv, pbz, dyb, sink, tabs):
    s = pq.shape[0]
    nb = s // ATTN_BLOCK

    def body(sink_ref, q_ref, z_ref, d_ref, kv_ref, t_ref,
             dq_ref, dz_ref, dkv_ref, gs_ref, kpad, vpad, dk_acc, dv_acc, bias, e_scr, ds_scr):
        i = pl.program_id(0)

        @pl.when(i == 0)
        def _():
            _fill_padded(kv_ref, kpad, vpad, s)
            _fill_band_bias(bias, nb)
            dk_acc[...] = jnp.zeros_like(dk_acc)
            dv_acc[...] = jnp.zeros_like(dv_acc)
            gs_ref[...] = jnp.zeros_like(gs_ref)

        row = lax.broadcasted_iota(jnp.int32, (ATTN_BLOCK, 128), 0)
        for b in range(BLOCKS_PER_STEP):
            n = i * BLOCKS_PER_STEP + b
            rows = slice(b * ATTN_BLOCK, (b + 1) * ATTN_BLOCK)
            start = pl.multiple_of(n * ATTN_BLOCK, ATTN_BLOCK)
            kw, vw = kpad[pl.ds(start, WINDOW_KEYS), :], vpad[pl.ds(start, WINDOW_KEYS), :]
            qf = q_ref[rows, :].astype(F32)
            variant = _bias_variant(n, nb)
            z = z_ref[rows, :].astype(F32)
            sig = _sigmoid(z)
            dyb_t = d_ref[rows, :].astype(F32)
            d_attn = dyb_t * (z * sig)
            outs, dqs = [], []
            dk_w = jnp.zeros((WINDOW_KEYS, 128), F32)
            dv_w = jnp.zeros((WINDOW_KEYS, 128), F32)
            for g in range(2):
                e_bg, ds_bg = e_scr.at[2 * b + g], ds_scr.at[2 * b + g]
                qt = _heads_to_lanes(qf, g, row)
                inv, p_sink = _softmax_keys_major(
                    _dot(kw, qt.astype(BF16)), bias, variant, _sink_row(sink_ref, g), e_bg)
                ot = _dot_tn(vw, e_bg[...]) * inv
                outs.append(ot)
                dot_ = _heads_to_lanes(d_attn, g, row)
                delta = jnp.sum(dot_ * ot, axis=0, keepdims=True)
                dpt = _dot(vw, dot_.astype(BF16))
                for k in range(WINDOW_KEYS // KEY_CHUNK):
                    rw = slice(k * KEY_CHUNK, (k + 1) * KEY_CHUNK)
                    ds_bg[rw, :] = (e_bg[rw, :].astype(F32) * (dpt[rw] - delta)).astype(BF16)
                sink_part = p_sink * delta
                for j in range(4):
                    h = 4 * g + j
                    gs_ref[h:h + 1, :] -= jnp.sum(sink_part[:, 128 * j:128 * j + 128])
                dqs.append(_dot_tn(kw, ds_bg[...]) * (inv * ATTN_SCALE))
                dk_w += _dot_nt(ds_bg[...], (qt * inv).astype(BF16)) * ATTN_SCALE
                dv_w += _dot_nt(e_bg[...], (dot_ * inv).astype(BF16))
            dk_acc[pl.ds(start, WINDOW_KEYS), :] += dk_w
            dv_acc[pl.ds(start, WINDOW_KEYS), :] += dv_w
            attn = _lanes_to_heads(outs[0], outs[1], row)
            dz_ref[rows, :] = (dyb_t * attn * (sig * (1.0 + z * (1.0 - sig)))).astype(BF16)
            dq = _lanes_to_heads(dqs[0], dqs[1], row)
            trows = pl.ds(start, ATTN_BLOCK)
            cs, s1, s2 = t_ref[0, trows, :], t_ref[1, trows, :], t_ref[2, trows, :]
            for blk in range(4):
                cols = slice(128 * blk, 128 * blk + 128)
                dq_ref[rows, cols] = _rope_t(dq[:, cols], cs, s1, s2).astype(BF16)

        @pl.when(i == nb // BLOCKS_PER_STEP - 1)
        def _():
            dk = dk_acc[ATTN_BLOCK:ATTN_BLOCK + s, :]
            dkv_ref[:, 0:128] = _rope_t(dk, t_ref[0], t_ref[1], t_ref[2]).astype(BF16)
            dkv_ref[:, 128:256] = dv_acc[ATTN_BLOCK:ATTN_BLOCK + s, :].astype(BF16)

    tq = BLOCKS_PER_STEP * ATTN_BLOCK
    tile = _rows(tq, 512)
    return pl.pallas_call(
        body, name="attn_bwd", grid=(s // tq,),
        out_shape=[jax.ShapeDtypeStruct((s, 512), BF16), jax.ShapeDtypeStruct((s, 512), BF16),
                   jax.ShapeDtypeStruct((s, 256), BF16), jax.ShapeDtypeStruct((8, 128), F32)],
        in_specs=[pl.BlockSpec(memory_space=pltpu.SMEM), tile, tile, tile, _full((s, 256)), _full((3, s, 128))],
        out_specs=[tile, tile, _full((s, 256)), _full((8, 128))],
        scratch_shapes=[pltpu.VMEM((s + 2 * ATTN_BLOCK, 128), BF16)] * 2
        + [pltpu.VMEM((s + 2 * ATTN_BLOCK, 128), F32)] * 2
        + [pltpu.VMEM((3, WINDOW_KEYS, STACKED), F32)]
        + [pltpu.VMEM((2 * BLOCKS_PER_STEP, WINDOW_KEYS, STACKED), BF16)] * 2,
        compiler_params=_params(48),
    )(sink, pq, pbz, dyb, pkv, tabs)


def _mem_attn_bwd(pmq, pmz, mkv, dym):
    s = pmq.shape[0]
    m = mkv.shape[0]
    tm = min(512, s)

    def body(q_ref, z_ref, d_ref, mk_ref, mv_ref, dq_ref, dz_ref, dmkv_ref):
        @pl.when(pl.program_id(0) == 0)
        def _():
            dmkv_ref[...] = jnp.zeros_like(dmkv_ref)

        z = z_ref[...].astype(F32)
        sig = _sigmoid(z)
        dym_t = d_ref[...].astype(F32)
        d_attn = dym_t * (z * sig)
        dsilu = sig * (1.0 + z * (1.0 - sig))
        for h in range(MEM_HEADS):
            cols = slice(128 * h, 128 * h + 128)
            q, mk, mv = q_ref[:, cols], mk_ref[:, cols], mv_ref[:, cols]
            pt = _mem_softmax_t(q, mk)
            pb = pt.astype(BF16)
            o = _dot_tn(pb, mv)
            dob = d_attn[:, cols].astype(BF16)
            dpt = _dot_nt(mv, dob)
            dst = (pt * (dpt - jnp.sum(pt * dpt, axis=0, keepdims=True))).astype(BF16)
            dq_ref[:, cols] = (_dot_tn(dst, mk) * MEM_SCALE).astype(BF16)
            dz_ref[:, cols] = (dym_t[:, cols] * o * dsilu[:, cols]).astype(BF16)
            dmkv_ref[:, cols] += _dot(dst, q) * MEM_SCALE
            dmkv_ref[:, 512 + 128 * h:512 + 128 * h + 128] += _dot(pb, dob)

    return pl.pallas_call(
        body, name="mem_attn_bwd", grid=(s // tm,),
        out_shape=[jax.ShapeDtypeStruct((s, 512), BF16), jax.ShapeDtypeStruct((s, 512), BF16),
                   jax.ShapeDtypeStruct((m, D_MODEL), F32)],
        in_specs=[_rows(tm, 512), _rows(tm, 512), _rows(tm, 512), pl.BlockSpec((m, 512), lambda i: (0, 0)),
                  pl.BlockSpec((m, 512), lambda i: (0, 1))],
        out_specs=[_rows(tm, 512), _rows(tm, 512), _full((m, D_MODEL))],
        compiler_params=_params(32),
    )(pmq, pmz, dym, mkv, mkv)


def _mem_kv_bwd(mem, g_mem, mn, dmkv, w_mkv):
    m = mem.shape[0]

    def body(mem_ref, g_ref, mn_ref, d_ref, w_ref, gw_ref, gg_ref):
        db = d_ref[...].astype(BF16)
        gw_ref[...] = _dot_tn(mn_ref[...], db).astype(BF16)
        d_mn = _dot_nt(db, w_ref[...])
        xf = mem_ref[...]
        r = lax.rsqrt(jnp.mean(xf * xf, axis=-1, keepdims=True) + EPS)
        gg_ref[...] = jnp.sum(d_mn * (xf * r), axis=0, keepdims=True)

    return pl.pallas_call(
        body, name="mem_kv_bwd", grid=(1,),
        out_shape=[jax.ShapeDtypeStruct((D_MODEL, D_MODEL), BF16), jax.ShapeDtypeStruct((1, D_MODEL), F32)],
        in_specs=[_full((m, D_MODEL)), _full((1, D_MODEL)), _full((m, D_MODEL)), _full((m, D_MODEL)),
                  _full((D_MODEL, D_MODEL))],
        out_specs=[_full((D_MODEL, D_MODEL)), _full((1, D_MODEL))],
        compiler_params=_params(32),
    )(mem, g_mem, mn, dmkv, w_mkv)


def _dh_bwd(dparts, x, dy, g_pre, w_int):
    s = x.shape[0]
    tm = min(256, s)

    def body(*refs):
        d_refs = refs[:7]
        x_ref, dy_ref, g_ref, w_hbm, gx_ref, gg_ref, w_vm, sems = refs[7:]
        _load_once([(w_hbm, w_vm)], sems)

        @pl.when(pl.program_id(0) == 0)
        def _():
            gg_ref[...] = jnp.zeros_like(gg_ref)

        d_h = jnp.zeros((tm, D_MODEL), F32)
        for d_ref, (r0, width) in zip(d_refs, SEGS):
            for c0 in range(0, width, 512):
                cw = min(512, width - c0)
                d_h += _dot(d_ref[:, c0:c0 + cw], w_vm[r0 + c0:r0 + c0 + cw, :])
        xf = x_ref[...]
        r = lax.rsqrt(jnp.mean(xf * xf, axis=-1, keepdims=True) + EPS)
        xn = xf * r
        a = d_h * g_ref[...]
        gx_ref[...] = r * (a - xn * jnp.mean(a * xn, axis=-1, keepdims=True)) + dy_ref[...]
        gg_ref[...] += jnp.sum(d_h * xn, axis=0, keepdims=True)

    return pl.pallas_call(
        body, name="dh_bwd", grid=(s // tm,),
        out_shape=[jax.ShapeDtypeStruct((s, D_MODEL), F32), jax.ShapeDtypeStruct((1, D_MODEL), F32)],
        in_specs=[_rows(tm, w) for _, w in SEGS] + [_rows(tm, D_MODEL), _rows(tm, D_MODEL), _full((1, D_MODEL)), ANY],
        out_specs=[_rows(tm, D_MODEL), _full((1, D_MODEL))],
        scratch_shapes=[pltpu.VMEM((IN_WIDTH, D_MODEL), BF16), pltpu.SemaphoreType.DMA((1,))],
        compiler_params=_params(52),
    )(*dparts, x, dy, g_pre, w_int)


def _gw_in(dparts, h):
    s = h.shape[0]
    tn = 256
    starts, counts = [], []
    for r0, width in SEGS:
        starts.append(r0 // tn)
        counts.append(width // tn)

    def body(*refs):
        d_refs = refs[:7]
        h_hbm, o_ref, h_vm, sems = refs[7:]
        _load_once([(h_hbm, h_vm)], sems)
        j = pl.program_id(0)
        for d_ref, st, cnt in zip(d_refs, starts, counts):
            @pl.when((j >= st) & (j < st + cnt))
            def _(d_ref=d_ref):
                o_ref[...] = _dot_tn(d_ref[...], h_vm[...]).astype(BF16)

    def seg_spec(st, cnt):
        return pl.BlockSpec((s, tn), lambda j: (0, jnp.clip(j - st, 0, cnt - 1)))

    return pl.pallas_call(
        body, name="gw_in", grid=(IN_WIDTH // tn,),
        out_shape=jax.ShapeDtypeStruct((IN_WIDTH, D_MODEL), BF16),
        in_specs=[seg_spec(st, cnt) for st, cnt in zip(starts, counts)] + [ANY],
        out_specs=pl.BlockSpec((tn, D_MODEL), lambda j: (j, 0)),
        scratch_shapes=[pltpu.VMEM((s, D_MODEL), BF16), pltpu.SemaphoreType.DMA((1,))],
        compiler_params=_params(52),
    )(*dparts, h)


def _adamw_math(w, g, m, v):
    m2 = ADAM_B1 * m + (1.0 - ADAM_B1) * g
    v2 = ADAM_B2 * v + (1.0 - ADAM_B2) * (g * g)
    m_hat = m2 / (1.0 - ADAM_B1 ** ADAM_STEP)
    v_hat = v2 / (1.0 - ADAM_B2 ** ADAM_STEP)
    delta = -ADAM_LR * (m_hat / (jnp.sqrt(v_hat) + ADAM_EPS) + ADAM_WD * w)
    return delta, m2, v2


def _sum_adamw(own, land, chip, block, w, m, v, name, tiles=1):
    r, c = w.shape
    rt = r // tiles

    def body(c_ref, own_ref, l1_ref, l2_ref, l3_ref, w_ref, m_ref, v_ref, g_ref, d_ref, m2_ref, v2_ref):
        g = own_ref[...].astype(F32)
        for l_ref in (l1_ref, l2_ref, l3_ref):
            g += l_ref[...].astype(F32)
        g_ref[...] = g
        d_ref[...], m2_ref[...], v2_ref[...] = _adamw_math(w_ref[...], g, m_ref[...], v_ref[...])

    def share(k):
        return pl.BlockSpec((None, rt, c), lambda i, c_ref: (jnp.bitwise_xor(c_ref[0], k), block * tiles + i, 0))

    spec = pl.BlockSpec((rt, c), lambda i, c_ref: (i, 0))
    grid_spec = pltpu.PrefetchScalarGridSpec(
        num_scalar_prefetch=1, grid=(tiles,),
        in_specs=[share(0), share(1), share(2), share(3)] + [spec] * 3, out_specs=[spec] * 4)
    return pl.pallas_call(
        body, name=name, grid_spec=grid_spec,
        out_shape=[jax.ShapeDtypeStruct((r, c), F32)] * 4,
        compiler_params=_params(48),
    )(chip, own, land, land, land, w, m, v)


def _sum_adamw_group(items, chip, name):
    k = len(items)

    def body(c_ref, *refs):
        shares, wmv, outs = refs[:4 * k], refs[4 * k:7 * k], refs[7 * k:]
        for j in range(k):
            g = shares[4 * j][...].astype(F32)
            for l_ref in shares[4 * j + 1:4 * j + 4]:
                g += l_ref[...].astype(F32)
            outs[4 * j][...] = g
            outs[4 * j + 1][...], outs[4 * j + 2][...], outs[4 * j + 3][...] = _adamw_math(
                wmv[3 * j][...], g, wmv[3 * j + 1][...], wmv[3 * j + 2][...])

    def share(shape, block, q):
        return pl.BlockSpec((None,) + shape, lambda i, c_ref: (jnp.bitwise_xor(c_ref[0], q), block, 0))

    in_specs, args = [], []
    for own, land, block, w, m, v in items:
        in_specs += [share(w.shape, block, q) for q in range(4)]
        args += [own, land, land, land]
    for own, land, block, w, m, v in items:
        in_specs += [pl.BlockSpec(w.shape, lambda i, c_ref: (0, 0))] * 3
        args += [w, m, v]
    out_specs = [pl.BlockSpec(w.shape, lambda i, c_ref: (0, 0)) for _, _, _, w, _, _ in items for _ in range(4)]
    res = pl.pallas_call(
        body, name=name,
        grid_spec=pltpu.PrefetchScalarGridSpec(num_scalar_prefetch=1, grid=(1,), in_specs=in_specs,
                                               out_specs=out_specs),
        out_shape=[jax.ShapeDtypeStruct(w.shape, F32) for _, _, _, w, _, _ in items for _ in range(4)],
        compiler_params=_params(48),
    )(chip, *args)
    return [res[4 * j:4 * j + 4] for j in range(k)]


def _small_step(parts, ws, ms, vs):
    def exchange(gpre_ref, gconv_ref, gsink_ref, gmem_ref, gpost_ref, loss_ref, tot_ref,
                 pack, gathered, send_sems, recv_sems):
        x, y, c = _my_place()
        me_idx = 4 * x + 2 * y + c

        lane = lax.broadcasted_iota(jnp.int32, (1, 128), 1)
        sink_row = jnp.zeros((1, 128), F32)
        for h in range(8):
            sink_row = jnp.where(lane == h, gsink_ref[h:h + 1, :], sink_row)
        pack[...] = jnp.zeros_like(pack)
        pack[0:1, :] = gpre_ref[...]
        pack[1:2, :] = gmem_ref[...]
        pack[2:3, :] = gpost_ref[...]
        pack[3:6, 0:512] = gconv_ref[0:3, :]
        pack[6:7, 0:128] = sink_row
        pack[7:8, 0:128] = loss_ref[0:1, :]

        flips = [(0, 0, 1), (0, 1, 0), (1, 0, 0), (0, 1, 1), (1, 0, 1), (1, 1, 0), (1, 1, 1)]
        cps = []
        for k, (fx, fy, fc) in enumerate(flips):
            peer = ((1 - x) if fx else x, (1 - y) if fy else y, (1 - c) if fc else c)
            cps.append(pltpu.make_async_remote_copy(
                src_ref=pack, dst_ref=gathered.at[me_idx], send_sem=send_sems.at[k], recv_sem=recv_sems.at[k],
                device_id=peer, device_id_type=MESH))
        for cp in cps:
            cp.start()
        gathered[me_idx] = pack[...]
        for cp in cps:
            cp.wait_recv()
        for cp in cps:
            cp.wait_send()
        tot = gathered[0]
        for d in range(1, N_DEV):
            tot = tot + gathered[d]
        tot_ref[...] = tot

    tot = pl.pallas_call(
        exchange, name="small_exchange", grid=(1,),
        out_shape=jax.ShapeDtypeStruct((8, D_MODEL), F32),
        in_specs=[_full(p.shape) for p in parts], out_specs=_full((8, D_MODEL)),
        scratch_shapes=[pltpu.VMEM((8, D_MODEL), F32), pltpu.VMEM((N_DEV, 8, D_MODEL), F32),
                        pltpu.SemaphoreType.DMA((N_PEERS,)), pltpu.SemaphoreType.DMA((N_PEERS,))],
    )(*parts)

    def apply(tot_ref, *refs):
        w_refs, m_refs, v_refs = refs[0:5], refs[5:10], refs[10:15]
        loss_out = refs[15]
        g_outs, d_outs, m_outs, v_outs = refs[16:21], refs[21:26], refs[26:31], refs[31:36]
        x, y, c = _my_place()
        tot = tot_ref[...]
        conv = pltpu.roll(tot[:, 0:512], (512 - 64 * (4 * x + 2 * y + c)) % 512, 1)[3:6, 0:64]
        grads = (tot[0:1, :], conv, tot[6:7, 0:8], tot[1:2, :], tot[2:3, :])
        loss_out[...] = tot[7:8, 0:128]
        for j in range(5):
            g_outs[j][...] = grads[j]
            d_outs[j][...], m_outs[j][...], v_outs[j][...] = _adamw_math(
                w_refs[j][...], grads[j], m_refs[j][...], v_refs[j][...])

    specs = [_full(w.shape) for w in ws]
    res = pl.pallas_call(
        apply, name="small_apply", grid=(1,),
        out_shape=[jax.ShapeDtypeStruct((1, 128), F32)] + [jax.ShapeDtypeStruct(w.shape, F32) for w in ws] * 4,
        in_specs=[_full((8, D_MODEL))] + specs * 3,
        out_specs=[_full((1, 128))] + specs * 4,
    )(tot, *ws, *ms, *vs)
    return res[0], res[1:6], res[6:11], res[11:16], res[16:21]


def kernel(x, mem, g_pre, w_in, w_conv, attn_sink, g_mem, w_mem_kv, w_up_a, w_up_b, w_up_m, w_out, g_post, loss_target, m_g_pre, m_w_in, m_w_conv, m_attn_sink, m_g_mem, m_w_mem_kv, m_w_up_a, m_w_up_b, m_w_up_m, m_w_out, m_g_post, v_g_pre, v_w_in, v_w_conv, v_attn_sink, v_g_mem, v_w_mem_kv, v_w_up_a, v_w_up_b, v_w_up_m, v_w_out, v_g_post):
    s = x.shape[1]
    x2, mem2, tgt2 = x[0], mem[0], loss_target[0]
    me = 4 * lax.axis_index("x") + 2 * lax.axis_index("y") + lax.axis_index("c")

    w_up_loc = jnp.concatenate([w_up_a[0], w_up_b[0], w_up_m[0]], axis=0).astype(BF16)
    w_conv_loc = jnp.zeros((8, 128), F32).at[:3, :64].set(w_conv[0])
    w_int_g, w_conv_g, tabs = _all_gather(
        [w_in[0].T.astype(BF16), w_conv_loc], "gather_w_in",
        splits=[[(112 * k, 112) for k in range(7)] + [(784, 144)], [(0, 8)]], side=_rope_side(s))
    w_int = w_int_g.reshape(IN_WIDTH, D_MODEL)
    w_conv_f = w_conv_g[:, :3, :64].transpose(1, 0, 2).reshape(3, 512)
    late = _gather_start([w_mem_kv[0].astype(BF16) + w_conv_g[0, 7:8, 0:1].astype(BF16),
                          w_out[0].astype(BF16), w_up_loc], me, "gather_late_start")
    sink = attn_sink[0]

    h, pa, pq, pkv, pbz, pmq, pmz, pg = _proj_fwd(x2, g_pre + late[4][0:1, 0:1], w_int, tabs)
    ya = _conv_fwd(pa, w_conv_f)
    yb = _attn_fwd(pq, pkv, pbz, sink)
    w_mkv_g, w_out_g, w_up_g = _gather_wait(*late[:4], yb, "gather_late_wait")
    w_mkv = w_mkv_g.reshape(D_MODEL, D_MODEL)
    w_out_f = w_out_g.reshape(D_MODEL, D_MODEL)
    mn, mkv = _mem_kv_fwd(mem2, g_mem, w_mkv)
    ym = _mem_attn_fwd(pmq, pmz, mkv)
    dg, dya, dyb, dym, dy, loss_p, gg_post, mb, dob, du = _mid(ya, yb, ym, pg, x2, tgt2, g_post, w_up_g, w_out_f)
    gw_out, gw_up = _gw_mid(mb, dob, (ya, yb, ym), du)

    core = lax.axis_index("c").astype(jnp.int32).reshape(1)
    chip = (2 * lax.axis_index("x") + lax.axis_index("y")).astype(jnp.int32).reshape(1)

    def exchange_start(shares, tag):
        from_sibling = _sibling_exchange(shares, "grads_to_sibling_" + tag)
        chip_shares = _pair_add(shares, from_sibling, core, "grads_pair_add_" + tag)
        return _chip_exchange_start(chip_shares, "grads_to_chips_start_" + tag)

    dmq, dmz, dmkv = _mem_attn_bwd(pmq, pmz, mkv, dym)
    gw_mkv, gg_mem = _mem_kv_bwd(mem2, g_mem, mn, dmkv, w_mkv)
    shares1 = [gw_mkv.reshape(N_DEV, 128, D_MODEL), gw_out.reshape(N_DEV, 128, D_MODEL), gw_up]
    sib = _split_start(_sibling_copies, N_CHIPS, shares1,
                       [lax.empty((N_CHIPS,) + a.shape[1:], a.dtype) for a in shares1], "grads_to_sibling_small_start")
    da, gw_conv = _conv_bwd(pa, dya, w_conv_f + sib[4][0:1, 0:1])
    shares1, from_sibling = _split_wait(_sibling_copies, N_CHIPS, *sib[:4], da, "grads_to_sibling_small_wait")
    send1, recv1, srcs1, lands1, token1 = _chip_exchange_start(
        _pair_add(shares1, from_sibling, core, "grads_pair_add_small"), "grads_to_chips_start_small")
    dq, dbz, dkv, g_sink = _attn_bwd(pq, pkv, pbz, dyb, sink + token1[0, 0], tabs)
    dparts = (da, dq, dkv, dbz, dmq, dmz, dg)
    gw_int = _gw_in(dparts, h)
    send2, recv2, srcs2, lands2, token2 = exchange_start([gw_int.reshape(N_DEV, SHARD_IN, D_MODEL)], "w_in")
    grad_x, gg_pre = _dh_bwd(dparts, x2, dy, g_pre + token2[0:1, 0:1], w_int)
    (o_mkv, o_out, o_up, o_int), (l_mkv, l_out, l_up, l_int) = _chip_exchange_wait(
        send1 + send2, recv1 + recv2, srcs1 + srcs2, lands1 + lands2, grad_x, "grads_to_chips_wait")

    loss_row, small_g, sd, sm, sv = _small_step(
        (gg_pre, gw_conv, g_sink, gg_mem, gg_post, loss_p),
        [g_pre, w_conv[0], attn_sink, g_mem, g_post],
        [m_g_pre, m_w_conv[0], m_attn_sink, m_g_mem, m_g_post],
        [v_g_pre, v_w_conv[0], v_attn_sink, v_g_mem, v_g_post])
    loss = loss_row[0, 0]
    g_g_pre, g_conv, g_sink_tot, g_g_mem, g_g_post = small_g

    g_w_in, d_w_in, nm_w_in, nv_w_in = (t.T for t in _sum_adamw(
        o_int, l_int, chip, 0, w_in[0].T, m_w_in[0].T, v_w_in[0].T, "adamw_w_in", tiles=2))
    (g_mkv, d_mkv, nm_mkv, nv_mkv), (g_out, d_out, nm_out, nv_out), *up = _sum_adamw_group(
        [(o_mkv, l_mkv, 0, w_mem_kv[0], m_w_mem_kv[0], v_w_mem_kv[0]),
         (o_out, l_out, 0, w_out[0], m_w_out[0], v_w_out[0]),
         (o_up, l_up, 0, w_up_a[0], m_w_up_a[0], v_w_up_a[0]),
         (o_up, l_up, 1, w_up_b[0], m_w_up_b[0], v_w_up_b[0]),
         (o_up, l_up, 2, w_up_m[0], m_w_up_m[0], v_w_up_m[0])], chip, "adamw_mid_weights")

    def lead(a):
        return a[None]

    grads = [g_g_pre, lead(g_w_in), lead(g_conv), g_sink_tot, g_g_mem, lead(g_mkv), lead(up[0][0]),
             lead(up[1][0]), lead(up[2][0]), lead(g_out), g_g_post]

    def assemble(small, big_in, big_mkv, big_up, big_out):
        return [small[0], lead(big_in), lead(small[1]), small[2], small[3], lead(big_mkv), lead(big_up[0]),
                lead(big_up[1]), lead(big_up[2]), lead(big_out), small[4]]

    deltas = assemble(sd, d_w_in, d_mkv, [u[1] for u in up], d_out)
    new_m = assemble(sm, nm_w_in, nm_mkv, [u[2] for u in up], nm_out)
    new_v = assemble(sv, nv_w_in, nv_mkv, [u[3] for u in up], nv_out)
    return (loss, grad_x[None], *grads, *deltas, *new_m, *new_v)
```

```python
import functools

import jax
import jax.numpy as jnp
from jax import lax
from jax.experimental import pallas as pl
from jax.experimental.pallas import tpu as pltpu

F32 = jnp.float32
BF16 = jnp.bfloat16
MESH = pl.DeviceIdType.MESH

N_DEV = 8
D_MODEL = 1024
EPS = 1e-6
ROPE_THETA = 500000.0
ROT_DIM = 16
HEAD_DIM = 64
ATTN_BLOCK = 128
MEM_HEADS = 4
MEM_HEAD_DIM = 128
ATTN_SCALE = HEAD_DIM ** -0.5
MEM_SCALE = MEM_HEAD_DIM ** -0.5

ADAM_LR = 0.001
ADAM_B1 = 0.9
ADAM_B2 = 0.999
ADAM_EPS = 1e-08
ADAM_WD = 0.01
ADAM_STEP = 10

SEG_A = (0, 2048)
SEG_BQ = (2048, 512)
SEG_BKV = (2560, 256)
SEG_BZ = (2816, 512)
SEG_MQ = (3328, 512)
SEG_MZ = (3840, 512)
SEG_G = (4352, 3072)
SEGS = (SEG_A, SEG_BQ, SEG_BKV, SEG_BZ, SEG_MQ, SEG_MZ, SEG_G)
IN_WIDTH = 7424
SHARD_IN = IN_WIDTH // N_DEV

V7X_VMEM_BYTES = 64 * 1024 * 1024
ANY = pl.BlockSpec(memory_space=pl.ANY)


def _params(vmem_mb):
    assert vmem_mb * 1024 * 1024 < V7X_VMEM_BYTES
    return pltpu.CompilerParams(dimension_semantics=("arbitrary",), vmem_limit_bytes=vmem_mb * 1024 * 1024)


def _full(shape):
    zeros = (0,) * len(shape)
    return pl.BlockSpec(shape, lambda i: zeros)


def _rows(tm, width):
    return pl.BlockSpec((tm, width), lambda i: (i, 0))


def _dot(a, b):
    return jnp.dot(a, b, preferred_element_type=F32)


def _dot_nt(a, b):
    return lax.dot_general(a, b, (((1,), (1,)), ((), ())), preferred_element_type=F32)


def _dot_tn(a, b):
    return lax.dot_general(a, b, (((0,), (0,)), ((), ())), preferred_element_type=F32)


def _sigmoid(z):
    return 1.0 / (1.0 + jnp.exp(-z))


def _rope(t, cs, s1, s2):
    return t * cs + pltpu.roll(t, 120, 1) * s1 + pltpu.roll(t, 8, 1) * s2


def _rope_t(d, cs, s1, s2):
    return d * cs + pltpu.roll(d * s1, 8, 1) + pltpu.roll(d * s2, 120, 1)


def _gather_side(s, w_mkv, w_out, w_ups):
    half = ROT_DIM // 2
    inv_freq = jnp.power(jnp.float32(ROPE_THETA), -jnp.arange(half, dtype=F32) * (2.0 / ROT_DIM))
    freq_row = jnp.tile(jnp.concatenate([inv_freq, inv_freq, jnp.zeros((HEAD_DIM - ROT_DIM,), F32)]), 2)[None, :]

    def fn(in_refs, out_refs):
        f_ref, mkv_ref, out_ref, *up_refs = in_refs
        t_ref, mkv_bf, out_bf, up_bf = out_refs
        mkv_bf[...] = mkv_ref[...].astype(BF16)
        out_bf[...] = out_ref[...].astype(BF16)
        for k, up_ref in enumerate(up_refs):
            up_bf[512 * k:512 * k + 512, :] = up_ref[...].astype(BF16)
        pos = lax.broadcasted_iota(jnp.int32, (s, 128), 0).astype(F32)
        d = lax.broadcasted_iota(jnp.int32, (s, 128), 1) & (HEAD_DIM - 1)
        ang = pos * f_ref[...]
        cos, sin = jnp.cos(ang), jnp.sin(ang)
        lo, hi = d < half, (d >= half) & (d < ROT_DIM)
        t_ref[0] = jnp.where(lo | hi, cos, 1.0)
        t_ref[1] = jnp.where(lo, -sin, 0.0)
        t_ref[2] = jnp.where(hi, sin, 0.0)

    return ([freq_row, w_mkv, w_out, *w_ups],
            [jax.ShapeDtypeStruct((3, s, 128), F32), jax.ShapeDtypeStruct(w_mkv.shape, BF16),
             jax.ShapeDtypeStruct(w_out.shape, BF16), jax.ShapeDtypeStruct((1536, 128), BF16)], fn)


def _load_once(pairs, sems):
    @pl.when(pl.program_id(0) == 0)
    def _():
        cps = [pltpu.make_async_copy(src, dst, sems.at[k]) for k, (src, dst) in enumerate(pairs)]
        for cp in cps:
            cp.start()
        for cp in cps:
            cp.wait()


def _my_place():
    x, y, c = lax.axis_index("x"), lax.axis_index("y"), lax.axis_index("c")
    return x, y, c


def _all_gather(arrs, name, splits=None, side=None):
    n = len(arrs)
    if splits is None:
        splits = [[(0, a.shape[0])] for a in arrs]
    pieces = [(a, r0, rn) for a in range(n) for r0, rn in splits[a]]
    n_p = len(pieces)
    side_in, side_out, side_fn = side if side is not None else ((), (), None)
    m, q = len(side_in), len(side_out)

    def body(*refs):
        ins, outs = refs[:n], refs[n + m:2 * n + m]
        send_sems, recv_sems, local_sems = refs[2 * n + m + q:]
        x, y, c = _my_place()
        me, sibling = (x, y, c), (x, y, 1 - c)

        def route(core):
            first = (jnp.bitwise_xor(x, 1 - core), jnp.bitwise_xor(y, core), core)
            second = (jnp.bitwise_xor(x, core), jnp.bitwise_xor(y, 1 - core), core)
            return first, second, (1 - x, 1 - y, core)

        def idx(px, py, pc):
            return 4 * px + 2 * py + pc

        def copy(p, k, block, to, own=False):
            a, r0, rn = pieces[p]
            dst = outs[a].at[idx(*block), pl.ds(r0, rn)]
            return pltpu.make_async_remote_copy(
                src_ref=ins[a].at[pl.ds(r0, rn)] if own else dst, dst_ref=dst,
                send_sem=send_sems.at[p * 7 + k], recv_sem=recv_sems.at[p * 7 + k],
                device_id=to, device_id_type=MESH)

        nbr1, nbr2, diag = route(c)
        mine = [pltpu.make_async_copy(ins[a], outs[a].at[idx(*me)], local_sems.at[a]) for a in range(n)]
        for cp in mine:
            cp.start()
        sent = []
        for p in range(n_p):
            for k, to in enumerate((sibling, nbr1, nbr2)):
                sent.append(copy(p, k, me, to, own=True))
        for cp in sent:
            cp.start()
        if side_fn is not None:
            side_fn(refs[n:n + m], refs[2 * n + m:2 * n + m + q])
        for k_in, block, onward in ((1, nbr1, ((3, nbr2), (4, sibling))), (2, nbr2, ((5, sibling),)),
                                    (3, diag, ((6, sibling),))):
            for p in range(n_p):
                copy(p, k_in, block, me).wait_recv()
                for k_out, to in onward:
                    cp = copy(p, k_out, block, to)
                    cp.start()
                    sent.append(cp)
        s1, s2, sd = route(1 - c)
        for k_in, block in ((0, sibling), (4, s1), (5, s2), (6, sd)):
            for p in range(n_p):
                copy(p, k_in, block, me).wait_recv()
        for cp in sent:
            cp.wait_send()
        for cp in mine:
            cp.wait()

    return pl.pallas_call(
        body, name=name,
        out_shape=[jax.ShapeDtypeStruct((N_DEV,) + a.shape, a.dtype) for a in arrs] + list(side_out),
        in_specs=[ANY] * n + [pl.BlockSpec(memory_space=pltpu.VMEM)] * m,
        out_specs=[ANY] * n + [pl.BlockSpec(memory_space=pltpu.VMEM)] * q,
        scratch_shapes=[pltpu.SemaphoreType.DMA((7 * n_p,)), pltpu.SemaphoreType.DMA((7 * n_p,)),
                        pltpu.SemaphoreType.DMA((n,))],
        compiler_params=pltpu.CompilerParams(vmem_limit_bytes=32 * 1024 * 1024),
    )(*arrs, *side_in)


N_CHIPS = 4


def _sibling_exchange(arrs, name):
    n = len(arrs)

    def body(*refs):
        ins, outs = refs[:n], refs[n:2 * n]
        send_sems, recv_sems = refs[2 * n:]
        x, y, c = _my_place()
        sibling = (x, y, 1 - c)

        def copy(a, j):
            return pltpu.make_async_remote_copy(
                src_ref=ins[a].at[2 * j + (1 - c)], dst_ref=outs[a].at[j],
                send_sem=send_sems.at[a * N_CHIPS + j], recv_sem=recv_sems.at[a * N_CHIPS + j],
                device_id=sibling, device_id_type=MESH)

        cps = [copy(a, j) for j in range(N_CHIPS) for a in range(n)]
        for cp in cps:
            cp.start()
        for cp in cps:
            cp.wait_recv()
        for cp in cps:
            cp.wait_send()

    return pl.pallas_call(
        body, name=name,
        out_shape=[jax.ShapeDtypeStruct((N_CHIPS,) + a.shape[1:], a.dtype) for a in arrs],
        in_specs=[ANY] * n, out_specs=[ANY] * n,
        scratch_shapes=[pltpu.SemaphoreType.DMA((N_CHIPS * n,)), pltpu.SemaphoreType.DMA((N_CHIPS * n,))],
    )(*arrs)


def _sibling_copies(srcs, lands, send_sems, recv_sems):
    x, y, c = _my_place()
    cps = []
    for j in range(N_CHIPS):
        for a in range(len(srcs)):
            k = a * N_CHIPS + j
            cps.append(pltpu.make_async_remote_copy(
                src_ref=srcs[a].at[2 * j + (1 - c)], dst_ref=lands[a].at[j], send_sem=send_sems[k],
                recv_sem=recv_sems[k], device_id=(x, y, 1 - c), device_id_type=MESH))
    return cps


def _pair_add(mine, recv, core, name):
    n = len(mine)

    def body(c_ref, *refs):
        for a in range(n):
            refs[2 * n + a][...] = (refs[a][...].astype(F32) + refs[n + a][...].astype(F32)).astype(BF16)

    def blk(a):
        return (None,) + a.shape[1:]

    grid_spec = pltpu.PrefetchScalarGridSpec(
        num_scalar_prefetch=1, grid=(N_CHIPS,),
        in_specs=[pl.BlockSpec(blk(a), lambda j, c_ref: (2 * j + c_ref[0], 0, 0)) for a in mine]
        + [pl.BlockSpec(blk(a), lambda j, c_ref: (j, 0, 0)) for a in recv],
        out_specs=[pl.BlockSpec(blk(a), lambda j, c_ref: (j, 0, 0)) for a in recv])
    return pl.pallas_call(
        body, name=name, grid_spec=grid_spec,
        out_shape=[jax.ShapeDtypeStruct(a.shape, BF16) for a in recv],
        compiler_params=_params(32),
    )(core, *mine, *recv)


HBM = pl.BlockSpec(memory_space=pltpu.HBM)
SEM = pl.BlockSpec(memory_space=pltpu.SEMAPHORE)
N_PEER_CHIPS = 3


def _chip_copies(srcs, lands, send_sems, recv_sems):
    x, y, c = _my_place()
    my_chip = 2 * x + y
    peers = [(x, 1 - y), (1 - x, y), (1 - x, 1 - y)]
    cps = []
    for k, (px, py) in enumerate(peers):
        for a in range(len(srcs)):
            j = a * N_PEER_CHIPS + k
            cps.append(pltpu.make_async_remote_copy(
                src_ref=srcs[a].at[2 * px + py], dst_ref=lands[a].at[my_chip],
                send_sem=send_sems[j], recv_sem=recv_sems[j],
                device_id=(px, py, c), device_id_type=MESH))
    return cps


N_PEERS = N_DEV - 1


def _gather_copies(srcs, lands, send_sems, recv_sems):
    x, y, c = _my_place()
    me_idx = 4 * x + 2 * y + c
    flips = [(0, 0, 1), (0, 1, 0), (1, 0, 0), (0, 1, 1), (1, 0, 1), (1, 1, 0), (1, 1, 1)]
    cps = []
    for k, (fx, fy, fc) in enumerate(flips):
        peer = ((1 - x) if fx else x, (1 - y) if fy else y, (1 - c) if fc else c)
        for a in range(len(srcs)):
            j = a * N_PEERS + k
            cps.append(pltpu.make_async_remote_copy(
                src_ref=srcs[a], dst_ref=lands[a].at[me_idx], send_sem=send_sems[j], recv_sem=recv_sems[j],
                device_id=peer, device_id_type=MESH))
    return cps


def _split_start(copies, per_array, arrs, lands, name):
    arrs, lands = list(arrs), list(lands)
    n = len(arrs)
    k = n * per_array

    def body(*refs):
        srcs, land_refs = refs[:n], refs[n:2 * n]
        send_sems, recv_sems = refs[2 * n:2 * n + k], refs[2 * n + k:2 * n + 2 * k]
        token = refs[-1]
        for cp in copies(srcs, land_refs, send_sems, recv_sems):
            cp.start()
        token[...] = jnp.zeros_like(token)

    hbm_arrs = [pltpu.with_memory_space_constraint(a, pltpu.HBM) for a in arrs]
    lands = [pltpu.with_memory_space_constraint(a, pltpu.HBM) for a in lands]
    res = pl.pallas_call(
        body, name=name,
        out_shape=[pltpu.SemaphoreType.DMA(())] * (2 * k) + [pltpu.HBM(a.shape, a.dtype) for a in arrs + lands]
        + [jax.ShapeDtypeStruct((8, 128), F32)],
        in_specs=[HBM] * (2 * n),
        out_specs=[SEM] * (2 * k) + [HBM] * (2 * n) + [pl.BlockSpec(memory_space=pltpu.VMEM)],
        input_output_aliases={a: 2 * k + a for a in range(2 * n)},
        compiler_params=pltpu.CompilerParams(has_side_effects=pltpu.SideEffectType.DATAFLOW_SIDE_EFFECTING),
    )(*hbm_arrs, *lands)
    return res[:k], res[k:2 * k], res[2 * k:2 * k + n], res[2 * k + n:2 * k + 2 * n], res[-1]


def _split_wait(copies, per_array, send_sems, recv_sems, srcs, lands, after, name):
    n = len(srcs)
    k = n * per_array

    def body(*refs):
        src_refs, land_refs = refs[:n], refs[n:2 * n]
        s_sems, r_sems = refs[2 * n:2 * n + k], refs[2 * n + k:2 * n + 2 * k]
        for cp in copies(src_refs, land_refs, s_sems, r_sems):
            cp.wait_send()
            cp.wait_recv()

    res = pl.pallas_call(
        body, name=name,
        out_shape=[pltpu.HBM(a.shape, a.dtype) for a in list(srcs) + list(lands)],
        in_specs=[HBM] * (2 * n) + [SEM] * (2 * k) + [ANY],
        out_specs=[HBM] * (2 * n),
        input_output_aliases={a: a for a in range(2 * n)},
        compiler_params=pltpu.CompilerParams(has_side_effects=pltpu.SideEffectType.DATAFLOW_SIDE_EFFECTING),
    )(*srcs, *lands, *send_sems, *recv_sems, after)
    return res[:n], res[n:]


def _chip_exchange_start(arrs, name):
    return _split_start(_chip_copies, N_PEER_CHIPS, arrs, [lax.empty(a.shape, a.dtype) for a in arrs], name)


def _chip_exchange_wait(send_sems, recv_sems, srcs, lands, after, name):
    return _split_wait(_chip_copies, N_PEER_CHIPS, send_sems, recv_sems, srcs, lands, after, name)


def _gather_start(arrs, me_idx, name):
    lands = [lax.dynamic_update_slice(lax.empty((N_DEV,) + a.shape, a.dtype), a[None], (me_idx, 0, 0)) for a in arrs]
    return _split_start(_gather_copies, N_PEERS, arrs, lands, name)


def _gather_wait(send_sems, recv_sems, srcs, lands, after, name):
    return _split_wait(_gather_copies, N_PEERS, send_sems, recv_sems, srcs, lands, after, name)[1]


def _proj_fwd(x, g_pre, w_int, tabs):
    s = x.shape[0]
    tm = min(512, s)

    def body(x_ref, g_ref, t_ref, w_hbm,
             h_ref, pa_ref, pq_ref, pkv_ref, pbz_ref, pmq_ref, pmz_ref, pg_ref, w_vm, sems):
        _load_once([(w_hbm, w_vm)], sems)
        xf = x_ref[...]
        r = lax.rsqrt(jnp.mean(xf * xf, axis=-1, keepdims=True) + EPS)
        h = ((xf * r) * g_ref[...]).astype(BF16)
        h_ref[...] = h
        cs, s1, s2 = t_ref[0], t_ref[1], t_ref[2]

        def mm(seg, c0, width):
            return _dot_nt(h, w_vm[seg[0] + c0:seg[0] + c0 + width, :])

        for c0 in range(0, SEG_A[1], 512):
            pa_ref[:, c0:c0 + 512] = mm(SEG_A, c0, 512).astype(BF16)
        q = mm(SEG_BQ, 0, 512)
        for b in range(4):
            pq_ref[:, 128 * b:128 * b + 128] = _rope(q[:, 128 * b:128 * b + 128], cs, s1, s2).astype(BF16)
        kv = mm(SEG_BKV, 0, 256)
        pkv_ref[:, 0:128] = _rope(kv[:, 0:128], cs, s1, s2).astype(BF16)
        pkv_ref[:, 128:256] = kv[:, 128:256].astype(BF16)
        pbz_ref[...] = mm(SEG_BZ, 0, 512).astype(BF16)
        pmq_ref[...] = mm(SEG_MQ, 0, 512).astype(BF16)
        pmz_ref[...] = mm(SEG_MZ, 0, 512).astype(BF16)
        for c0 in range(0, SEG_G[1], 512):
            pg_ref[:, c0:c0 + 512] = mm(SEG_G, c0, 512).astype(BF16)

    widths = (D_MODEL, 2048, 512, 256, 512, 512, 512, 3072)
    return pl.pallas_call(
        body, name="proj_fwd", grid=(s // tm,),
        out_shape=[jax.ShapeDtypeStruct((s, w), BF16) for w in widths],
        in_specs=[_rows(tm, D_MODEL), _full((1, D_MODEL)), pl.BlockSpec((3, tm, 128), lambda i: (0, i, 0)), ANY],
        out_specs=[_rows(tm, w) for w in widths],
        scratch_shapes=[pltpu.VMEM((IN_WIDTH, D_MODEL), BF16), pltpu.SemaphoreType.DMA((1,))],
        compiler_params=_params(52),
    )(x, g_pre, tabs, w_int)


def _mem_kv_fwd(mem, g_mem, w_mkv):
    m = mem.shape[0]

    def body(mem_ref, g_ref, w_ref, mn_ref, mkv_ref):
        xf = mem_ref[...]
        r = lax.rsqrt(jnp.mean(xf * xf, axis=-1, keepdims=True) + EPS)
        mn = ((xf * r) * g_ref[...]).astype(BF16)
        mn_ref[...] = mn
        mkv_ref[...] = _dot(mn, w_ref[...]).astype(BF16)

    return pl.pallas_call(
        body, name="mem_kv_fwd", grid=(1,),
        out_shape=[jax.ShapeDtypeStruct((m, D_MODEL), BF16)] * 2,
        in_specs=[_full((m, D_MODEL)), _full((1, D_MODEL)), _full((D_MODEL, D_MODEL))],
        out_specs=[_full((m, D_MODEL))] * 2,
        compiler_params=_params(32),
    )(mem, g_mem, w_mkv)


def _halo_specs(s, tm, rows, width):
    nblk = s // rows
    prev = pl.BlockSpec((rows, width), lambda i: (jnp.maximum(i * (tm // rows) - 1, 0), 0))
    nxt = pl.BlockSpec((rows, width), lambda i: (jnp.minimum((i + 1) * (tm // rows), nblk - 1), 0))
    return prev, nxt


def _conv_common(pa, cu_prev, cu_next, w, tm):
    b, c, u, z = (pa[:, 512 * k:512 * k + 512] for k in range(4))
    cu = c * u
    row = lax.broadcasted_iota(jnp.int32, (tm, 512), 0)
    cu_m1 = jnp.where(row == 0, cu_prev, pltpu.roll(cu, 1, 0))
    cu_p1 = jnp.where(row == tm - 1, cu_next, pltpu.roll(cu, tm - 1, 0))
    y = cu_m1 * w[0:1] + cu * w[1:2] + cu_p1 * w[2:3]
    sig = _sigmoid(z)
    return b, c, u, z, cu, cu_m1, cu_p1, y, sig, row


def _conv_fwd(pa, w_conv):
    s = pa.shape[0]
    tm = min(512, s)
    nt = s // tm

    def body(pa_ref, pp_ref, pn_ref, w_ref, ya_ref):
        i = pl.program_id(0)
        prev_row = pp_ref[...].astype(F32)[15:16, :]
        next_row = pn_ref[...].astype(F32)[0:1, :]
        b, _, _, z, _, _, _, y, sig, _ = _conv_common(
            pa_ref[...].astype(F32),
            jnp.where(i == 0, 0.0, prev_row[:, 512:1024] * prev_row[:, 1024:1536]),
            jnp.where(i == nt - 1, 0.0, next_row[:, 512:1024] * next_row[:, 1024:1536]), w_ref[...], tm)
        ya_ref[...] = (b * y * (z * sig)).astype(BF16)

    prev, nxt = _halo_specs(s, tm, 16, 2048)
    return pl.pallas_call(
        body, name="conv_fwd", grid=(nt,),
        out_shape=jax.ShapeDtypeStruct((s, 512), BF16),
        in_specs=[_rows(tm, 2048), prev, nxt, _full((3, 512))],
        out_specs=_rows(tm, 512),
        compiler_params=_params(48),
    )(pa, pa, pa, w_conv)


def _heads_to_lanes(a, g, row):
    low = row < HEAD_DIM
    parts = []
    for b in (2 * g, 2 * g + 1):
        t = jnp.transpose(a[:, 128 * b:128 * b + 128])
        swapped = pltpu.roll(t, HEAD_DIM, 0)
        if g == 0:
            parts += [jnp.where(low, t, 0.0), jnp.where(low, swapped, 0.0)]
        else:
            parts += [jnp.where(low, 0.0, swapped), jnp.where(low, 0.0, t)]
    return jnp.concatenate(parts, axis=1)


def _lanes_to_heads(t0, t1, row):
    low = row < HEAD_DIM
    blocks = []
    for b in range(4):
        g = b // 2
        tg = (t0, t1)[g]
        je = 2 * (b - 2 * g)
        even, odd = tg[:, 128 * je:128 * je + 128], tg[:, 128 * je + 128:128 * je + 256]
        if g == 0:
            t = jnp.where(low, even, pltpu.roll(odd, HEAD_DIM, 0))
        else:
            t = jnp.where(low, pltpu.roll(even, HEAD_DIM, 0), odd)
        blocks.append(jnp.transpose(t))
    return jnp.concatenate(blocks, axis=1)


WINDOW_KEYS = 3 * ATTN_BLOCK
STACKED = 4 * ATTN_BLOCK
KEY_CHUNK = 32
BLOCKS_PER_STEP = 4


def _fill_band_bias(bias, nb):
    assert nb >= 2
    c = lax.broadcasted_iota(jnp.int32, (WINDOW_KEYS, STACKED), 0)
    r = lax.broadcasted_iota(jnp.int32, (WINDOW_KEYS, STACKED), 1) & (ATTN_BLOCK - 1)
    band = (c >= r) & (c <= r + 2 * ATTN_BLOCK)
    for v, ok in enumerate((band, band & (c >= ATTN_BLOCK), band & (c < 2 * ATTN_BLOCK))):
        bias[v] = jnp.where(ok, 0.0, -jnp.inf)


def _bias_variant(n, nb):
    return jnp.where(n == 0, 1, jnp.where(n == nb - 1, 2, 0))


def _sink_row(sink_ref, g):
    return jnp.concatenate([jnp.full((1, ATTN_BLOCK), sink_ref[4 * g + j], F32) for j in range(4)], axis=1)


def _softmax_keys_major(sc, bias, variant, sink, e_scr):
    chunks = [pl.ds(k * KEY_CHUNK, KEY_CHUNK) for k in range(WINDOW_KEYS // KEY_CHUNK)]
    rows = [slice(k * KEY_CHUNK, (k + 1) * KEY_CHUNK) for k in range(WINDOW_KEYS // KEY_CHUNK)]
    m_run = jnp.full((KEY_CHUNK, STACKED), -jnp.inf, F32)
    for ck, rw in zip(chunks, rows):
        m_run = jnp.maximum(m_run, sc[rw] * ATTN_SCALE + bias[variant, ck, :])
    m = jnp.maximum(jnp.max(m_run, axis=0, keepdims=True), sink)
    l_run = jnp.zeros((KEY_CHUNK, STACKED), F32)
    for ck, rw in zip(chunks, rows):
        e = jnp.exp(sc[rw] * ATTN_SCALE + bias[variant, ck, :] - m)
        l_run += e
        e_scr[rw, :] = e.astype(BF16)
    es = jnp.exp(sink - m)
    inv = 1.0 / (jnp.sum(l_run, axis=0, keepdims=True) + es)
    return inv, es * inv


def _fill_padded(kv_ref, kpad, vpad, s):
    zero = jnp.zeros((ATTN_BLOCK, 128), BF16)
    kpad[0:ATTN_BLOCK, :] = zero
    vpad[0:ATTN_BLOCK, :] = zero
    kpad[ATTN_BLOCK + s:2 * ATTN_BLOCK + s, :] = zero
    vpad[ATTN_BLOCK + s:2 * ATTN_BLOCK + s, :] = zero
    kpad[ATTN_BLOCK:ATTN_BLOCK + s, :] = kv_ref[:, 0:128]
    vpad[ATTN_BLOCK:ATTN_BLOCK + s, :] = kv_ref[:, 128:256]


def _attn_fwd(pq, pkv, pbz, sink):
    s = pq.shape[0]
    nb = s // ATTN_BLOCK

    def body(sink_ref, q_ref, z_ref, kv_ref, yb_ref, kpad, vpad, bias, e_scr):
        i = pl.program_id(0)

        @pl.when(i == 0)
        def _():
            _fill_padded(kv_ref, kpad, vpad, s)
            _fill_band_bias(bias, nb)

        row = lax.broadcasted_iota(jnp.int32, (ATTN_BLOCK, 128), 0)
        for b in range(BLOCKS_PER_STEP):
            n = i * BLOCKS_PER_STEP + b
            rows = slice(b * ATTN_BLOCK, (b + 1) * ATTN_BLOCK)
            start = pl.multiple_of(n * ATTN_BLOCK, ATTN_BLOCK)
            kw, vw = kpad[pl.ds(start, WINDOW_KEYS), :], vpad[pl.ds(start, WINDOW_KEYS), :]
            qf = q_ref[rows, :].astype(F32)
            variant = _bias_variant(n, nb)
            outs = []
            for g in range(2):
                e_bg = e_scr.at[2 * b + g]
                qt = _heads_to_lanes(qf, g, row).astype(BF16)
                inv, _ = _softmax_keys_major(_dot(kw, qt), bias, variant, _sink_row(sink_ref, g), e_bg)
                outs.append(_dot_tn(vw, e_bg[...]) * inv)
            attn = _lanes_to_heads(outs[0], outs[1], row)
            z = z_ref[rows, :].astype(F32)
            yb_ref[rows, :] = (attn * (z * _sigmoid(z))).astype(BF16)

    tq = BLOCKS_PER_STEP * ATTN_BLOCK
    return pl.pallas_call(
        body, name="attn_fwd", grid=(s // tq,),
        out_shape=jax.ShapeDtypeStruct((s, 512), BF16),
        in_specs=[pl.BlockSpec(memory_space=pltpu.SMEM), _rows(tq, 512), _rows(tq, 512), _full((s, 256))],
        out_specs=_rows(tq, 512),
        scratch_shapes=[pltpu.VMEM((s + 2 * ATTN_BLOCK, 128), BF16)] * 2
        + [pltpu.VMEM((3, WINDOW_KEYS, STACKED), F32),
           pltpu.VMEM((2 * BLOCKS_PER_STEP, WINDOW_KEYS, STACKED), BF16)],
        compiler_params=_params(32),
    )(sink, pq, pbz, pkv)


def _mem_softmax_t(q, mk):
    sc = _dot_nt(mk, q) * MEM_SCALE
    e = jnp.exp(sc - jnp.max(sc, axis=0, keepdims=True))
    return e * (1.0 / jnp.sum(e, axis=0, keepdims=True))


def _mem_attn_fwd(pmq, pmz, mkv):
    s = pmq.shape[0]
    m = mkv.shape[0]
    tm = min(512, s)

    def body(q_ref, z_ref, mk_ref, mv_ref, ym_ref):
        z = z_ref[...].astype(F32)
        sz = z * _sigmoid(z)
        for h in range(MEM_HEADS):
            cols = slice(128 * h, 128 * h + 128)
            pt = _mem_softmax_t(q_ref[:, cols], mk_ref[:, cols])
            o = _dot_tn(pt.astype(BF16), mv_ref[:, cols])
            ym_ref[:, cols] = (o * sz[:, cols]).astype(BF16)

    return pl.pallas_call(
        body, name="mem_attn_fwd", grid=(s // tm,),
        out_shape=jax.ShapeDtypeStruct((s, 512), BF16),
        in_specs=[_rows(tm, 512), _rows(tm, 512), pl.BlockSpec((m, 512), lambda i: (0, 0)),
                  pl.BlockSpec((m, 512), lambda i: (0, 1))],
        out_specs=_rows(tm, 512),
        compiler_params=_params(32),
    )(pmq, pmz, mkv, mkv)


def _mid(ya, yb, ym, pg, x, target, g_post, w_up, w_out):
    s = x.shape[0]
    tm = min(256, s)
    nt = s // tm

    def body(ya_ref, yb_ref, ym_ref, pg_ref, x_ref, t_ref, gp_ref, wup_hbm, wout_hbm,
             dg_ref, dya_ref, dyb_ref, dym_ref, dy_ref, loss_ref, ggp_ref, mb_ref, dob_ref, du_ref,
             wup_vm, wout_vm, sems):
        i = pl.program_id(0)
        _load_once([(wup_hbm.at[d], wup_vm.at[:, pl.ds(128 * d, 128)]) for d in range(N_DEV)]
                   + [(wout_hbm, wout_vm)], sems)

        @pl.when(i == 0)
        def _():
            loss_ref[...] = jnp.zeros_like(loss_ref)
            ggp_ref[...] = jnp.zeros_like(ggp_ref)

        ys = (ya_ref[...], yb_ref[...], ym_ref[...])
        us = [_dot(ys[k], wup_vm[512 * k:512 * k + 512, :]) for k in range(3)]
        gates = [_sigmoid(pg_ref[:, 1024 * k:1024 * k + 1024].astype(F32)) for k in range(3)]
        merged = gates[0] * us[0] + gates[1] * us[1] + gates[2] * us[2]
        mb = merged.astype(BF16)
        mb_ref[...] = mb
        out = _dot(mb, wout_vm[...])
        r = lax.rsqrt(jnp.mean(out * out, axis=-1, keepdims=True) + EPS)
        on = out * r
        gp = gp_ref[...]
        err = (x_ref[...] + on * gp) - t_ref[...]
        loss_ref[...] += 0.5 * jnp.sum(err * err) * (1.0 / D_MODEL)
        dy = err * (1.0 / D_MODEL)
        dy_ref[...] = dy
        ggp_ref[...] += jnp.sum(dy * on, axis=0, keepdims=True)
        a = dy * gp
        d_out = r * (a - on * jnp.mean(a * on, axis=-1, keepdims=True))
        dob = d_out.astype(BF16)
        dob_ref[...] = dob
        d_merged = _dot_nt(dob, wout_vm[...])
        d_refs = (dya_ref, dyb_ref, dym_ref)
        for k in range(3):
            g = gates[k]
            du_f = d_merged * g
            dg_ref[:, 1024 * k:1024 * k + 1024] = (du_f * us[k] * (1.0 - g)).astype(BF16)
            du = du_f.astype(BF16)
            du_ref[k] = du
            d_refs[k][...] = _dot_nt(du, wup_vm[512 * k:512 * k + 512, :]).astype(BF16)

    return pl.pallas_call(
        body, name="mid", grid=(nt,),
        out_shape=[jax.ShapeDtypeStruct((s, 3072), BF16)] + [jax.ShapeDtypeStruct((s, 512), BF16)] * 3
        + [jax.ShapeDtypeStruct((s, D_MODEL), F32), jax.ShapeDtypeStruct((8, 128), F32),
           jax.ShapeDtypeStruct((1, D_MODEL), F32), jax.ShapeDtypeStruct((s, D_MODEL), BF16),
           jax.ShapeDtypeStruct((s, D_MODEL), BF16), jax.ShapeDtypeStruct((3, s, D_MODEL), BF16)],
        in_specs=[_rows(tm, 512)] * 3 + [_rows(tm, 3072), _rows(tm, D_MODEL), _rows(tm, D_MODEL),
                                         _full((1, D_MODEL)), ANY, ANY],
        out_specs=[_rows(tm, 3072)] + [_rows(tm, 512)] * 3
        + [_rows(tm, D_MODEL), _full((8, 128)), _full((1, D_MODEL)), _rows(tm, D_MODEL), _rows(tm, D_MODEL),
           pl.BlockSpec((3, tm, D_MODEL), lambda i: (0, i, 0))],
        scratch_shapes=[pltpu.VMEM((1536, D_MODEL), BF16), pltpu.VMEM((D_MODEL, D_MODEL), BF16),
                        pltpu.SemaphoreType.DMA((N_DEV + 1,))],
        compiler_params=_params(56),
    )(ya, yb, ym, pg, x, target, g_post, w_up, w_out)


def _gw_mid(mb, dob, ys, du):
    s = mb.shape[0]
    tn = 256

    def out_body(mb_ref, dob_ref, o_ref):
        o_ref[...] = _dot_tn(mb_ref[...], dob_ref[...]).astype(BF16)

    gw_out = pl.pallas_call(
        out_body, name="gw_out", grid=(D_MODEL // tn,),
        out_shape=jax.ShapeDtypeStruct((D_MODEL, D_MODEL), BF16),
        in_specs=[pl.BlockSpec((s, tn), lambda j: (0, j)), _full((s, D_MODEL))],
        out_specs=pl.BlockSpec((tn, D_MODEL), lambda j: (j, 0)),
        compiler_params=_params(48),
    )(mb, dob)

    per = 512 // tn

    def up_body(ya_ref, yb_ref, ym_ref, du_ref, o_ref):
        j = pl.program_id(0)
        for k, y_ref in enumerate((ya_ref, yb_ref, ym_ref)):
            @pl.when(j // per == k)
            def _(y_ref=y_ref):
                res = _dot_tn(y_ref[...], du_ref[...])
                for d in range(N_DEV):
                    o_ref[d] = res[:, 128 * d:128 * d + 128].astype(BF16)

    def y_spec(k):
        return pl.BlockSpec((s, tn), lambda j: (0, jnp.clip(j - per * k, 0, per - 1)))

    gw_up = pl.pallas_call(
        up_body, name="gw_up", grid=(3 * per,),
        out_shape=jax.ShapeDtypeStruct((N_DEV, 1536, 128), BF16),
        in_specs=[y_spec(0), y_spec(1), y_spec(2), pl.BlockSpec((None, s, D_MODEL), lambda j: (j // per, 0, 0))],
        out_specs=pl.BlockSpec((N_DEV, tn, 128), lambda j: (0, j, 0)),
        compiler_params=_params(48),
    )(*ys, du)
    return gw_out, gw_up


def _conv_bwd(pa, dya, w_conv):
    s = pa.shape[0]
    tm = min(512, s)
    nt = s // tm

    def body(pa_ref, pp_ref, pn_ref, d_ref, dp_ref, dn_ref, w_ref, da_ref, gw_ref):
        i = pl.program_id(0)
        first, last = i == 0, i == nt - 1

        @pl.when(first)
        def _():
            gw_ref[...] = jnp.zeros_like(gw_ref)

        w = w_ref[...]
        prev_row = pp_ref[...].astype(F32)[15:16, :]
        next_row = pn_ref[...].astype(F32)[0:1, :]
        b, c, u, z, cu, cu_m1, cu_p1, y, sig, row = _conv_common(
            pa_ref[...].astype(F32),
            jnp.where(first, 0.0, prev_row[:, 512:1024] * prev_row[:, 1024:1536]),
            jnp.where(last, 0.0, next_row[:, 512:1024] * next_row[:, 1024:1536]), w, tm)
        sz = z * sig
        dya_t = d_ref[...].astype(F32)
        d_y = dya_t * b * sz

        def halo_dy(p_row, d_row):
            zz = p_row[:, 1536:2048]
            return d_row * p_row[:, 0:512] * (zz * _sigmoid(zz))

        dy_prev = jnp.where(first, 0.0, halo_dy(prev_row, dp_ref[...].astype(F32)[15:16, :]))
        dy_next = jnp.where(last, 0.0, halo_dy(next_row, dn_ref[...].astype(F32)[0:1, :]))
        dy_m1 = jnp.where(row == 0, dy_prev, pltpu.roll(d_y, 1, 0))
        dy_p1 = jnp.where(row == tm - 1, dy_next, pltpu.roll(d_y, tm - 1, 0))
        d_cu = dy_p1 * w[0:1] + d_y * w[1:2] + dy_m1 * w[2:3]
        da_ref[:, 0:512] = (dya_t * y * sz).astype(BF16)
        da_ref[:, 512:1024] = (d_cu * u).astype(BF16)
        da_ref[:, 1024:1536] = (d_cu * c).astype(BF16)
        da_ref[:, 1536:2048] = (dya_t * b * y * (sig * (1.0 + z * (1.0 - sig)))).astype(BF16)
        gw_ref[0:1, :] += jnp.sum(d_y * cu_m1, axis=0, keepdims=True)
        gw_ref[1:2, :] += jnp.sum(d_y * cu, axis=0, keepdims=True)
        gw_ref[2:3, :] += jnp.sum(d_y * cu_p1, axis=0, keepdims=True)

    prev, nxt = _halo_specs(s, tm, 16, 2048)
    dprev, dnxt = _halo_specs(s, tm, 16, 512)
    return pl.pallas_call(
        body, name="conv_bwd", grid=(nt,),
        out_shape=[jax.ShapeDtypeStruct((s, 2048), BF16), jax.ShapeDtypeStruct((8, 512), F32)],
        in_specs=[_rows(tm, 2048), prev, nxt, _rows(tm, 512), dprev, dnxt, _full((3, 512))],
        out_specs=[_rows(tm, 2048), _full((8, 512))],
        compiler_params=_params(48),
    )(pa, pa, pa, dya, dya, dya, w_conv)


def _attn_bwd(pq, pkv, pbz, dyb, sink, tabs):
    s = pq.shape[0]
    nb = s // ATTN_BLOCK

    def body(sink_ref, q_ref, z_ref, d_ref, kv_ref, t_ref,
             dq_ref, dz_ref, dkv_ref, gs_ref, kpad, vpad, dk_acc, dv_acc, bias, e_scr, ds_scr):
        i = pl.program_id(0)

        @pl.when(i == 0)
        def _():
            _fill_padded(kv_ref, kpad, vpad, s)
            _fill_band_bias(bias, nb)
            dk_acc[...] = jnp.zeros_like(dk_acc)
            dv_acc[...] = jnp.zeros_like(dv_acc)
            gs_ref[...] = jnp.zeros_like(gs_ref)

        row = lax.broadcasted_iota(jnp.int32, (ATTN_BLOCK, 128), 0)
        for b in range(BLOCKS_PER_STEP):
            n = i * BLOCKS_PER_STEP + b
            rows = slice(b * ATTN_BLOCK, (b + 1) * ATTN_BLOCK)
            start = pl.multiple_of(n * ATTN_BLOCK, ATTN_BLOCK)
            kw, vw = kpad[pl.ds(start, WINDOW_KEYS), :], vpad[pl.ds(start, WINDOW_KEYS), :]
            qf = q_ref[rows, :].astype(F32)
            variant = _bias_variant(n, nb)
            z = z_ref[rows, :].astype(F32)
            sig = _sigmoid(z)
            dyb_t = d_ref[rows, :].astype(F32)
            d_attn = dyb_t * (z * sig)
            outs, dqs = [], []
            dk_w = jnp.zeros((WINDOW_KEYS, 128), F32)
            dv_w = jnp.zeros((WINDOW_KEYS, 128), F32)
            for g in range(2):
                e_bg, ds_bg = e_scr.at[2 * b + g], ds_scr.at[2 * b + g]
                qt = _heads_to_lanes(qf, g, row)
                inv, p_sink = _softmax_keys_major(
                    _dot(kw, qt.astype(BF16)), bias, variant, _sink_row(sink_ref, g), e_bg)
                ot = _dot_tn(vw, e_bg[...]) * inv
                outs.append(ot)
                dot_ = _heads_to_lanes(d_attn, g, row)
                delta = jnp.sum(dot_ * ot, axis=0, keepdims=True)
                dpt = _dot(vw, dot_.astype(BF16))
                for k in range(WINDOW_KEYS // KEY_CHUNK):
                    rw = slice(k * KEY_CHUNK, (k + 1) * KEY_CHUNK)
                    ds_bg[rw, :] = (e_bg[rw, :].astype(F32) * (dpt[rw] - delta)).astype(BF16)
                sink_part = p_sink * delta
                for j in range(4):
                    h = 4 * g + j
                    gs_ref[h:h + 1, :] -= jnp.sum(sink_part[:, 128 * j:128 * j + 128])
                dqs.append(_dot_tn(kw, ds_bg[...]) * (inv * ATTN_SCALE))
                dk_w += _dot_nt(ds_bg[...], (qt * inv).astype(BF16)) * ATTN_SCALE
                dv_w += _dot_nt(e_bg[...], (dot_ * inv).astype(BF16))
            dk_acc[pl.ds(start, WINDOW_KEYS), :] += dk_w
            dv_acc[pl.ds(start, WINDOW_KEYS), :] += dv_w
            attn = _lanes_to_heads(outs[0], outs[1], row)
            dz_ref[rows, :] = (dyb_t * attn * (sig * (1.0 + z * (1.0 - sig)))).astype(BF16)
            dq = _lanes_to_heads(dqs[0], dqs[1], row)
            trows = pl.ds(start, ATTN_BLOCK)
            cs, s1, s2 = t_ref[0, trows, :], t_ref[1, trows, :], t_ref[2, trows, :]
            for blk in range(4):
                cols = slice(128 * blk, 128 * blk + 128)
                dq_ref[rows, cols] = _rope_t(dq[:, cols], cs, s1, s2).astype(BF16)

        @pl.when(i == nb // BLOCKS_PER_STEP - 1)
        def _():
            dk = dk_acc[ATTN_BLOCK:ATTN_BLOCK + s, :]
            dkv_ref[:, 0:128] = _rope_t(dk, t_ref[0], t_ref[1], t_ref[2]).astype(BF16)
            dkv_ref[:, 128:256] = dv_acc[ATTN_BLOCK:ATTN_BLOCK + s, :].astype(BF16)

    tq = BLOCKS_PER_STEP * ATTN_BLOCK
    tile = _rows(tq, 512)
    return pl.pallas_call(
        body, name="attn_bwd", grid=(s // tq,),
        out_shape=[jax.ShapeDtypeStruct((s, 512), BF16), jax.ShapeDtypeStruct((s, 512), BF16),
                   jax.ShapeDtypeStruct((s, 256), BF16), jax.ShapeDtypeStruct((8, 128), F32)],
        in_specs=[pl.BlockSpec(memory_space=pltpu.SMEM), tile, tile, tile, _full((s, 256)), _full((3, s, 128))],
        out_specs=[tile, tile, _full((s, 256)), _full((8, 128))],
        scratch_shapes=[pltpu.VMEM((s + 2 * ATTN_BLOCK, 128), BF16)] * 2
        + [pltpu.VMEM((s + 2 * ATTN_BLOCK, 128), F32)] * 2
        + [pltpu.VMEM((3, WINDOW_KEYS, STACKED), F32)]
        + [pltpu.VMEM((2 * BLOCKS_PER_STEP, WINDOW_KEYS, STACKED), BF16)] * 2,
        compiler_params=_params(48),
    )(sink, pq, pbz, dyb, pkv, tabs)


def _mem_attn_bwd(pmq, pmz, mkv, dym):
    s = pmq.shape[0]
    m = mkv.shape[0]
    tm = min(512, s)

    def body(q_ref, z_ref, d_ref, mk_ref, mv_ref, dq_ref, dz_ref, dmkv_ref):
        @pl.when(pl.program_id(0) == 0)
        def _():
            dmkv_ref[...] = jnp.zeros_like(dmkv_ref)

        z = z_ref[...].astype(F32)
        sig = _sigmoid(z)
        dym_t = d_ref[...].astype(F32)
        d_attn = dym_t * (z * sig)
        dsilu = sig * (1.0 + z * (1.0 - sig))
        for h in range(MEM_HEADS):
            cols = slice(128 * h, 128 * h + 128)
            q, mk, mv = q_ref[:, cols], mk_ref[:, cols], mv_ref[:, cols]
            pt = _mem_softmax_t(q, mk)
            pb = pt.astype(BF16)
            o = _dot_tn(pb, mv)
            dob = d_attn[:, cols].astype(BF16)
            dpt = _dot_nt(mv, dob)
            dst = (pt * (dpt - jnp.sum(pt * dpt, axis=0, keepdims=True))).astype(BF16)
            dq_ref[:, cols] = (_dot_tn(dst, mk) * MEM_SCALE).astype(BF16)
            dz_ref[:, cols] = (dym_t[:, cols] * o * dsilu[:, cols]).astype(BF16)
            dmkv_ref[:, cols] += _dot(dst, q) * MEM_SCALE
            dmkv_ref[:, 512 + 128 * h:512 + 128 * h + 128] += _dot(pb, dob)

    return pl.pallas_call(
        body, name="mem_attn_bwd", grid=(s // tm,),
        out_shape=[jax.ShapeDtypeStruct((s, 512), BF16), jax.ShapeDtypeStruct((s, 512), BF16),
                   jax.ShapeDtypeStruct((m, D_MODEL), F32)],
        in_specs=[_rows(tm, 512), _rows(tm, 512), _rows(tm, 512), pl.BlockSpec((m, 512), lambda i: (0, 0)),
                  pl.BlockSpec((m, 512), lambda i: (0, 1))],
        out_specs=[_rows(tm, 512), _rows(tm, 512), _full((m, D_MODEL))],
        compiler_params=_params(32),
    )(pmq, pmz, dym, mkv, mkv)


def _mem_kv_bwd(mem, g_mem, mn, dmkv, w_mkv):
    m = mem.shape[0]

    def body(mem_ref, g_ref, mn_ref, d_ref, w_ref, gw_ref, gg_ref):
        db = d_ref[...].astype(BF16)
        gw_ref[...] = _dot_tn(mn_ref[...], db).astype(BF16)
        d_mn = _dot_nt(db, w_ref[...])
        xf = mem_ref[...]
        r = lax.rsqrt(jnp.mean(xf * xf, axis=-1, keepdims=True) + EPS)
        gg_ref[...] = jnp.sum(d_mn * (xf * r), axis=0, keepdims=True)

    return pl.pallas_call(
        body, name="mem_kv_bwd", grid=(1,),
        out_shape=[jax.ShapeDtypeStruct((D_MODEL, D_MODEL), BF16), jax.ShapeDtypeStruct((1, D_MODEL), F32)],
        in_specs=[_full((m, D_MODEL)), _full((1, D_MODEL)), _full((m, D_MODEL)), _full((m, D_MODEL)),
                  _full((D_MODEL, D_MODEL))],
        out_specs=[_full((D_MODEL, D_MODEL)), _full((1, D_MODEL))],
        compiler_params=_params(32),
    )(mem, g_mem, mn, dmkv, w_mkv)


def _dh_bwd(dparts, x, dy, g_pre, w_int):
    s = x.shape[0]
    tm = min(256, s)

    def body(*refs):
        d_refs = refs[:7]
        x_ref, dy_ref, g_ref, w_hbm, gx_ref, gg_ref, w_vm, sems = refs[7:]
        _load_once([(w_hbm, w_vm)], sems)

        @pl.when(pl.program_id(0) == 0)
        def _():
            gg_ref[...] = jnp.zeros_like(gg_ref)

        d_h = jnp.zeros((tm, D_MODEL), F32)
        for d_ref, (r0, width) in zip(d_refs, SEGS):
            for c0 in range(0, width, 512):
                cw = min(512, width - c0)
                d_h += _dot(d_ref[:, c0:c0 + cw], w_vm[r0 + c0:r0 + c0 + cw, :])
        xf = x_ref[...]
        r = lax.rsqrt(jnp.mean(xf * xf, axis=-1, keepdims=True) + EPS)
        xn = xf * r
        a = d_h * g_ref[...]
        gx_ref[...] = r * (a - xn * jnp.mean(a * xn, axis=-1, keepdims=True)) + dy_ref[...]
        gg_ref[...] += jnp.sum(d_h * xn, axis=0, keepdims=True)

    return pl.pallas_call(
        body, name="dh_bwd", grid=(s // tm,),
        out_shape=[jax.ShapeDtypeStruct((s, D_MODEL), F32), jax.ShapeDtypeStruct((1, D_MODEL), F32)],
        in_specs=[_rows(tm, w) for _, w in SEGS] + [_rows(tm, D_MODEL), _rows(tm, D_MODEL), _full((1, D_MODEL)), ANY],
        out_specs=[_rows(tm, D_MODEL), _full((1, D_MODEL))],
        scratch_shapes=[pltpu.VMEM((IN_WIDTH, D_MODEL), BF16), pltpu.SemaphoreType.DMA((1,))],
        compiler_params=_params(52),
    )(*dparts, x, dy, g_pre, w_int)


def _gw_in(dparts, h):
    s = h.shape[0]
    tn = 256
    starts, counts = [], []
    for r0, width in SEGS:
        starts.append(r0 // tn)
        counts.append(width // tn)

    def body(*refs):
        d_refs = refs[:7]
        h_hbm, o_ref, h_vm, sems = refs[7:]
        _load_once([(h_hbm, h_vm)], sems)
        j = pl.program_id(0)
        for d_ref, st, cnt in zip(d_refs, starts, counts):
            @pl.when((j >= st) & (j < st + cnt))
            def _(d_ref=d_ref):
                o_ref[...] = _dot_tn(d_ref[...], h_vm[...]).astype(BF16)

    def seg_spec(st, cnt):
        return pl.BlockSpec((s, tn), lambda j: (0, jnp.clip(j - st, 0, cnt - 1)))

    return pl.pallas_call(
        body, name="gw_in", grid=(IN_WIDTH // tn,),
        out_shape=jax.ShapeDtypeStruct((IN_WIDTH, D_MODEL), BF16),
        in_specs=[seg_spec(st, cnt) for st, cnt in zip(starts, counts)] + [ANY],
        out_specs=pl.BlockSpec((tn, D_MODEL), lambda j: (j, 0)),
        scratch_shapes=[pltpu.VMEM((s, D_MODEL), BF16), pltpu.SemaphoreType.DMA((1,))],
        compiler_params=_params(52),
    )(*dparts, h)


def _adamw_math(w, g, m, v):
    m2 = ADAM_B1 * m + (1.0 - ADAM_B1) * g
    v2 = ADAM_B2 * v + (1.0 - ADAM_B2) * (g * g)
    m_hat = m2 / (1.0 - ADAM_B1 ** ADAM_STEP)
    v_hat = v2 / (1.0 - ADAM_B2 ** ADAM_STEP)
    delta = -ADAM_LR * (m_hat / (jnp.sqrt(v_hat) + ADAM_EPS) + ADAM_WD * w)
    return delta, m2, v2


def _sum_adamw(own, land, chip, block, w, m, v, name, tiles=1):
    r, c = w.shape
    rt = r // tiles

    def body(c_ref, own_ref, l1_ref, l2_ref, l3_ref, w_ref, m_ref, v_ref, g_ref, d_ref, m2_ref, v2_ref):
        g = own_ref[...].astype(F32)
        for l_ref in (l1_ref, l2_ref, l3_ref):
            g += l_ref[...].astype(F32)
        g_ref[...] = g
        d_ref[...], m2_ref[...], v2_ref[...] = _adamw_math(w_ref[...], g, m_ref[...], v_ref[...])

    def share(k):
        return pl.BlockSpec((None, rt, c), lambda i, c_ref: (jnp.bitwise_xor(c_ref[0], k), block * tiles + i, 0))

    spec = pl.BlockSpec((rt, c), lambda i, c_ref: (i, 0))
    grid_spec = pltpu.PrefetchScalarGridSpec(
        num_scalar_prefetch=1, grid=(tiles,),
        in_specs=[share(0), share(1), share(2), share(3)] + [spec] * 3, out_specs=[spec] * 4)
    return pl.pallas_call(
        body, name=name, grid_spec=grid_spec,
        out_shape=[jax.ShapeDtypeStruct((r, c), F32)] * 4,
        compiler_params=_params(48),
    )(chip, own, land, land, land, w, m, v)


def _sum_adamw_group(items, chip, name):
    k = len(items)

    def body(c_ref, *refs):
        shares, wmv, outs = refs[:4 * k], refs[4 * k:7 * k], refs[7 * k:]
        for j in range(k):
            g = shares[4 * j][...].astype(F32)
            for l_ref in shares[4 * j + 1:4 * j + 4]:
                g += l_ref[...].astype(F32)
            outs[4 * j][...] = g
            outs[4 * j + 1][...], outs[4 * j + 2][...], outs[4 * j + 3][...] = _adamw_math(
                wmv[3 * j][...], g, wmv[3 * j + 1][...], wmv[3 * j + 2][...])

    def share(shape, block, q):
        return pl.BlockSpec((None,) + shape, lambda i, c_ref: (jnp.bitwise_xor(c_ref[0], q), block, 0))

    in_specs, args = [], []
    for own, land, block, w, m, v in items:
        in_specs += [share(w.shape, block, q) for q in range(4)]
        args += [own, land, land, land]
    for own, land, block, w, m, v in items:
        in_specs += [pl.BlockSpec(w.shape, lambda i, c_ref: (0, 0))] * 3
        args += [w, m, v]
    out_specs = [pl.BlockSpec(w.shape, lambda i, c_ref: (0, 0)) for _, _, _, w, _, _ in items for _ in range(4)]
    res = pl.pallas_call(
        body, name=name,
        grid_spec=pltpu.PrefetchScalarGridSpec(num_scalar_prefetch=1, grid=(1,), in_specs=in_specs,
                                               out_specs=out_specs),
        out_shape=[jax.ShapeDtypeStruct(w.shape, F32) for _, _, _, w, _, _ in items for _ in range(4)],
        compiler_params=_params(48),
    )(chip, *args)
    return [res[4 * j:4 * j + 4] for j in range(k)]


def _small_step(parts, ws, ms, vs):
    def exchange(gpre_ref, gconv_ref, gsink_ref, gmem_ref, gpost_ref, loss_ref, tot_ref,
                 pack, gathered, send_sems, recv_sems):
        x, y, c = _my_place()
        me_idx = 4 * x + 2 * y + c

        lane = lax.broadcasted_iota(jnp.int32, (1, 128), 1)
        sink_row = jnp.zeros((1, 128), F32)
        for h in range(8):
            sink_row = jnp.where(lane == h, gsink_ref[h:h + 1, :], sink_row)
        pack[...] = jnp.zeros_like(pack)
        pack[0:1, :] = gpre_ref[...]
        pack[1:2, :] = gmem_ref[...]
        pack[2:3, :] = gpost_ref[...]
        pack[3:6, 0:512] = gconv_ref[0:3, :]
        pack[6:7, 0:128] = sink_row
        pack[7:8, 0:128] = loss_ref[0:1, :]

        flips = [(0, 0, 1), (0, 1, 0), (1, 0, 0), (0, 1, 1), (1, 0, 1), (1, 1, 0), (1, 1, 1)]
        cps = []
        for k, (fx, fy, fc) in enumerate(flips):
            peer = ((1 - x) if fx else x, (1 - y) if fy else y, (1 - c) if fc else c)
            cps.append(pltpu.make_async_remote_copy(
                src_ref=pack, dst_ref=gathered.at[me_idx], send_sem=send_sems.at[k], recv_sem=recv_sems.at[k],
                device_id=peer, device_id_type=MESH))
        for cp in cps:
            cp.start()
        gathered[me_idx] = pack[...]
        for cp in cps:
            cp.wait_recv()
        for cp in cps:
            cp.wait_send()
        tot = gathered[0]
        for d in range(1, N_DEV):
            tot = tot + gathered[d]
        tot_ref[...] = tot

    tot = pl.pallas_call(
        exchange, name="small_exchange", grid=(1,),
        out_shape=jax.ShapeDtypeStruct((8, D_MODEL), F32),
        in_specs=[_full(p.shape) for p in parts], out_specs=_full((8, D_MODEL)),
        scratch_shapes=[pltpu.VMEM((8, D_MODEL), F32), pltpu.VMEM((N_DEV, 8, D_MODEL), F32),
                        pltpu.SemaphoreType.DMA((N_PEERS,)), pltpu.SemaphoreType.DMA((N_PEERS,))],
    )(*parts)

    def apply(tot_ref, *refs):
        w_refs, m_refs, v_refs = refs[0:5], refs[5:10], refs[10:15]
        loss_out = refs[15]
        g_outs, d_outs, m_outs, v_outs = refs[16:21], refs[21:26], refs[26:31], refs[31:36]
        x, y, c = _my_place()
        tot = tot_ref[...]
        conv = pltpu.roll(tot[:, 0:512], (512 - 64 * (4 * x + 2 * y + c)) % 512, 1)[3:6, 0:64]
        grads = (tot[0:1, :], conv, tot[6:7, 0:8], tot[1:2, :], tot[2:3, :])
        loss_out[...] = tot[7:8, 0:128]
        for j in range(5):
            g_outs[j][...] = grads[j]
            d_outs[j][...], m_outs[j][...], v_outs[j][...] = _adamw_math(
                w_refs[j][...], grads[j], m_refs[j][...], v_refs[j][...])

    specs = [_full(w.shape) for w in ws]
    res = pl.pallas_call(
        apply, name="small_apply", grid=(1,),
        out_shape=[jax.ShapeDtypeStruct((1, 128), F32)] + [jax.ShapeDtypeStruct(w.shape, F32) for w in ws] * 4,
        in_specs=[_full((8, D_MODEL))] + specs * 3,
        out_specs=[_full((1, 128))] + specs * 4,
    )(tot, *ws, *ms, *vs)
    return res[0], res[1:6], res[6:11], res[11:16], res[16:21]


def kernel(x, mem, g_pre, w_in, w_conv, attn_sink, g_mem, w_mem_kv, w_up_a, w_up_b, w_up_m, w_out, g_post, loss_target, m_g_pre, m_w_in, m_w_conv, m_attn_sink, m_g_mem, m_w_mem_kv, m_w_up_a, m_w_up_b, m_w_up_m, m_w_out, m_g_post, v_g_pre, v_w_in, v_w_conv, v_attn_sink, v_g_mem, v_w_mem_kv, v_w_up_a, v_w_up_b, v_w_up_m, v_w_out, v_g_post):
    s = x.shape[1]
    x2, mem2, tgt2 = x[0], mem[0], loss_target[0]
    me = 4 * lax.axis_index("x") + 2 * lax.axis_index("y") + lax.axis_index("c")

    w_conv_loc = jnp.zeros((8, 128), F32).at[:3, :64].set(w_conv[0])
    w_int_g, w_conv_g, tabs, w_mkv_loc, w_out_loc, w_up_loc = _all_gather(
        [w_in[0].T.astype(BF16), w_conv_loc], "gather_w_in",
        splits=[[(112 * k, 112) for k in range(7)] + [(784, 144)], [(0, 8)]],
        side=_gather_side(s, w_mem_kv[0], w_out[0], (w_up_a[0], w_up_b[0], w_up_m[0])))
    w_int = w_int_g.reshape(IN_WIDTH, D_MODEL)
    w_conv_f = w_conv_g[:, :3, :64].transpose(1, 0, 2).reshape(3, 512)
    late = _gather_start([w_mkv_loc, w_out_loc, w_up_loc], me, "gather_late_start")
    sink = attn_sink[0]

    h, pa, pq, pkv, pbz, pmq, pmz, pg = _proj_fwd(x2, g_pre + late[4][0:1, 0:1], w_int, tabs)
    ya = _conv_fwd(pa, w_conv_f)
    yb = _attn_fwd(pq, pkv, pbz, sink)
    w_mkv_g, w_out_g, w_up_g = _gather_wait(*late[:4], yb, "gather_late_wait")
    w_mkv = w_mkv_g.reshape(D_MODEL, D_MODEL)
    w_out_f = w_out_g.reshape(D_MODEL, D_MODEL)
    mn, mkv = _mem_kv_fwd(mem2, g_mem, w_mkv)
    ym = _mem_attn_fwd(pmq, pmz, mkv)
    dg, dya, dyb, dym, dy, loss_p, gg_post, mb, dob, du = _mid(ya, yb, ym, pg, x2, tgt2, g_post, w_up_g, w_out_f)
    gw_out, gw_up = _gw_mid(mb, dob, (ya, yb, ym), du)

    core = lax.axis_index("c").astype(jnp.int32).reshape(1)
    chip = (2 * lax.axis_index("x") + lax.axis_index("y")).astype(jnp.int32).reshape(1)

    def exchange_start(shares, tag):
        from_sibling = _sibling_exchange(shares, "grads_to_sibling_" + tag)
        chip_shares = _pair_add(shares, from_sibling, core, "grads_pair_add_" + tag)
        return _chip_exchange_start(chip_shares, "grads_to_chips_start_" + tag)

    dmq, dmz, dmkv = _mem_attn_bwd(pmq, pmz, mkv, dym)
    gw_mkv, gg_mem = _mem_kv_bwd(mem2, g_mem, mn, dmkv, w_mkv)
    shares1 = [gw_mkv.reshape(N_DEV, 128, D_MODEL), gw_out.reshape(N_DEV, 128, D_MODEL), gw_up]
    sib = _split_start(_sibling_copies, N_CHIPS, shares1,
                       [lax.empty((N_CHIPS,) + a.shape[1:], a.dtype) for a in shares1], "grads_to_sibling_small_start")
    da, gw_conv = _conv_bwd(pa, dya, w_conv_f + sib[4][0:1, 0:1])
    shares1, from_sibling = _split_wait(_sibling_copies, N_CHIPS, *sib[:4], da, "grads_to_sibling_small_wait")
    send1, recv1, srcs1, lands1, token1 = _chip_exchange_start(
        _pair_add(shares1, from_sibling, core, "grads_pair_add_small"), "grads_to_chips_start_small")
    dq, dbz, dkv, g_sink = _attn_bwd(pq, pkv, pbz, dyb, sink + token1[0, 0], tabs)
    dparts = (da, dq, dkv, dbz, dmq, dmz, dg)
    gw_int = _gw_in(dparts, h)
    send2, recv2, srcs2, lands2, token2 = exchange_start([gw_int.reshape(N_DEV, SHARD_IN, D_MODEL)], "w_in")
    grad_x, gg_pre = _dh_bwd(dparts, x2, dy, g_pre + token2[0:1, 0:1], w_int)
    (o_mkv, o_out, o_up, o_int), (l_mkv, l_out, l_up, l_int) = _chip_exchange_wait(
        send1 + send2, recv1 + recv2, srcs1 + srcs2, lands1 + lands2, grad_x, "grads_to_chips_wait")

    loss_row, small_g, sd, sm, sv = _small_step(
        (gg_pre, gw_conv, g_sink, gg_mem, gg_post, loss_p),
        [g_pre, w_conv[0], attn_sink, g_mem, g_post],
        [m_g_pre, m_w_conv[0], m_attn_sink, m_g_mem, m_g_post],
        [v_g_pre, v_w_conv[0], v_attn_sink, v_g_mem, v_g_post])
    loss = loss_row[0, 0]
    g_g_pre, g_conv, g_sink_tot, g_g_mem, g_g_post = small_g

    g_w_in, d_w_in, nm_w_in, nv_w_in = (t.T for t in _sum_adamw(
        o_int, l_int, chip, 0, w_in[0].T, m_w_in[0].T, v_w_in[0].T, "adamw_w_in", tiles=2))
    (g_mkv, d_mkv, nm_mkv, nv_mkv), (g_out, d_out, nm_out, nv_out), *up = _sum_adamw_group(
        [(o_mkv, l_mkv, 0, w_mem_kv[0], m_w_mem_kv[0], v_w_mem_kv[0]),
         (o_out, l_out, 0, w_out[0], m_w_out[0], v_w_out[0]),
         (o_up, l_up, 0, w_up_a[0], m_w_up_a[0], v_w_up_a[0]),
         (o_up, l_up, 1, w_up_b[0], m_w_up_b[0], v_w_up_b[0]),
         (o_up, l_up, 2, w_up_m[0], m_w_up_m[0], v_w_up_m[0])], chip, "adamw_mid_weights")

    def lead(a):
        return a[None]

    grads = [g_g_pre, lead(g_w_in), lead(g_conv), g_sink_tot, g_g_mem, lead(g_mkv), lead(up[0][0]),
             lead(up[1][0]), lead(up[2][0]), lead(g_out), g_g_post]

    def assemble(small, big_in, big_mkv, big_up, big_out):
        return [small[0], lead(big_in), lead(small[1]), small[2], small[3], lead(big_mkv), lead(big_up[0]),
                lead(big_up[1]), lead(big_up[2]), lead(big_out), small[4]]

    deltas = assemble(sd, d_w_in, d_mkv, [u[1] for u in up], d_out)
    new_m = assemble(sm, nm_w_in, nm_mkv, [u[2] for u in up], nm_out)
    new_v = assemble(sv, nv_w_in, nv_mkv, [u[3] for u in up], nv_out)
    return (loss, grad_x[None], *grads, *deltas, *new_m, *new_v)
```

```python
import functools

import jax
import jax.numpy as jnp
from jax import lax
from jax.experimental import pallas as pl
from jax.experimental.pallas import tpu as pltpu

F32 = jnp.float32
BF16 = jnp.bfloat16
MESH = pl.DeviceIdType.MESH

N_DEV = 8
D_MODEL = 1024
EPS = 1e-6
ROPE_THETA = 500000.0
ROT_DIM = 16
HEAD_DIM = 64
ATTN_BLOCK = 128
MEM_HEADS = 4
MEM_HEAD_DIM = 128
ATTN_SCALE = HEAD_DIM ** -0.5
MEM_SCALE = MEM_HEAD_DIM ** -0.5

ADAM_LR = 0.001
ADAM_B1 = 0.9
ADAM_B2 = 0.999
ADAM_EPS = 1e-08
ADAM_WD = 0.01
ADAM_STEP = 10

SEG_A = (0, 2048)
SEG_BQ = (2048, 512)
SEG_BKV = (2560, 256)
SEG_BZ = (2816, 512)
SEG_MQ = (3328, 512)
SEG_MZ = (3840, 512)
SEG_G = (4352, 3072)
SEGS = (SEG_A, SEG_BQ, SEG_BKV, SEG_BZ, SEG_MQ, SEG_MZ, SEG_G)
IN_WIDTH = 7424
SHARD_IN = IN_WIDTH // N_DEV

V7X_VMEM_BYTES = 64 * 1024 * 1024
ANY = pl.BlockSpec(memory_space=pl.ANY)


def _params(vmem_mb):
    assert vmem_mb * 1024 * 1024 < V7X_VMEM_BYTES
    return pltpu.CompilerParams(dimension_semantics=("arbitrary",), vmem_limit_bytes=vmem_mb * 1024 * 1024)


def _full(shape):
    zeros = (0,) * len(shape)
    return pl.BlockSpec(shape, lambda i: zeros)


def _rows(tm, width):
    return pl.BlockSpec((tm, width), lambda i: (i, 0))


def _dot(a, b):
    return jnp.dot(a, b, preferred_element_type=F32)


def _dot_nt(a, b):
    return lax.dot_general(a, b, (((1,), (1,)), ((), ())), preferred_element_type=F32)


def _dot_tn(a, b):
    return lax.dot_general(a, b, (((0,), (0,)), ((), ())), preferred_element_type=F32)


def _sigmoid(z):
    return 1.0 / (1.0 + jnp.exp(-z))


def _rope(t, cs, s1, s2):
    return t * cs + pltpu.roll(t, 120, 1) * s1 + pltpu.roll(t, 8, 1) * s2


def _rope_t(d, cs, s1, s2):
    return d * cs + pltpu.roll(d * s1, 8, 1) + pltpu.roll(d * s2, 120, 1)


def _gather_side(s, w_mkv, w_out, w_ups):
    half = ROT_DIM // 2
    inv_freq = jnp.power(jnp.float32(ROPE_THETA), -jnp.arange(half, dtype=F32) * (2.0 / ROT_DIM))
    freq_row = jnp.tile(jnp.concatenate([inv_freq, inv_freq, jnp.zeros((HEAD_DIM - ROT_DIM,), F32)]), 2)[None, :]

    def fn(in_refs, out_refs):
        f_ref, mkv_ref, out_ref, *up_refs = in_refs
        t_ref, mkv_bf, out_bf, up_bf = out_refs
        mkv_bf[...] = mkv_ref[...].astype(BF16)
        out_bf[...] = out_ref[...].astype(BF16)
        for k, up_ref in enumerate(up_refs):
            up_bf[512 * k:512 * k + 512, :] = up_ref[...].astype(BF16)
        pos = lax.broadcasted_iota(jnp.int32, (s, 128), 0).astype(F32)
        d = lax.broadcasted_iota(jnp.int32, (s, 128), 1) & (HEAD_DIM - 1)
        ang = pos * f_ref[...]
        cos, sin = jnp.cos(ang), jnp.sin(ang)
        lo, hi = d < half, (d >= half) & (d < ROT_DIM)
        t_ref[0] = jnp.where(lo | hi, cos, 1.0)
        t_ref[1] = jnp.where(lo, -sin, 0.0)
        t_ref[2] = jnp.where(hi, sin, 0.0)

    return ([freq_row, w_mkv, w_out, *w_ups],
            [jax.ShapeDtypeStruct((3, s, 128), F32), jax.ShapeDtypeStruct(w_mkv.shape, BF16),
             jax.ShapeDtypeStruct(w_out.shape, BF16), jax.ShapeDtypeStruct((1536, 128), BF16)], fn)


def _load_once(pairs, sems):
    @pl.when(pl.program_id(0) == 0)
    def _():
        cps = [pltpu.make_async_copy(src, dst, sems.at[k]) for k, (src, dst) in enumerate(pairs)]
        for cp in cps:
            cp.start()
        for cp in cps:
            cp.wait()


def _my_place():
    x, y, c = lax.axis_index("x"), lax.axis_index("y"), lax.axis_index("c")
    return x, y, c


def _all_gather(arrs, name, splits=None, side=None):
    n = len(arrs)
    if splits is None:
        splits = [[(0, a.shape[0])] for a in arrs]
    pieces = [(a, r0, rn) for a in range(n) for r0, rn in splits[a]]
    n_p = len(pieces)
    side_in, side_out, side_fn = side if side is not None else ((), (), None)
    m, q = len(side_in), len(side_out)

    def body(*refs):
        ins, outs = refs[:n], refs[n + m:2 * n + m]
        send_sems, recv_sems, local_sems = refs[2 * n + m + q:]
        x, y, c = _my_place()
        me, sibling = (x, y, c), (x, y, 1 - c)

        def route(core):
            first = (jnp.bitwise_xor(x, 1 - core), jnp.bitwise_xor(y, core), core)
            second = (jnp.bitwise_xor(x, core), jnp.bitwise_xor(y, 1 - core), core)
            return first, second, (1 - x, 1 - y, core)

        def idx(px, py, pc):
            return 4 * px + 2 * py + pc

        def copy(p, k, block, to, own=False):
            a, r0, rn = pieces[p]
            dst = outs[a].at[idx(*block), pl.ds(r0, rn)]
            return pltpu.make_async_remote_copy(
                src_ref=ins[a].at[pl.ds(r0, rn)] if own else dst, dst_ref=dst,
                send_sem=send_sems.at[p * 7 + k], recv_sem=recv_sems.at[p * 7 + k],
                device_id=to, device_id_type=MESH)

        nbr1, nbr2, diag = route(c)
        mine = [pltpu.make_async_copy(ins[a], outs[a].at[idx(*me)], local_sems.at[a]) for a in range(n)]
        for cp in mine:
            cp.start()
        sent = []
        for p in range(n_p):
            for k, to in enumerate((sibling, nbr1, nbr2)):
                sent.append(copy(p, k, me, to, own=True))
        for cp in sent:
            cp.start()
        if side_fn is not None:
            side_fn(refs[n:n + m], refs[2 * n + m:2 * n + m + q])
        for k_in, block, onward in ((1, nbr1, ((3, nbr2), (4, sibling))), (2, nbr2, ((5, sibling),)),
                                    (3, diag, ((6, sibling),))):
            for p in range(n_p):
                copy(p, k_in, block, me).wait_recv()
                for k_out, to in onward:
                    cp = copy(p, k_out, block, to)
                    cp.start()
                    sent.append(cp)
        s1, s2, sd = route(1 - c)
        for k_in, block in ((0, sibling), (4, s1), (5, s2), (6, sd)):
            for p in range(n_p):
                copy(p, k_in, block, me).wait_recv()
        for cp in sent:
            cp.wait_send()
        for cp in mine:
            cp.wait()

    return pl.pallas_call(
        body, name=name,
        out_shape=[jax.ShapeDtypeStruct((N_DEV,) + a.shape, a.dtype) for a in arrs] + list(side_out),
        in_specs=[ANY] * n + [pl.BlockSpec(memory_space=pltpu.VMEM)] * m,
        out_specs=[ANY] * n + [pl.BlockSpec(memory_space=pltpu.VMEM)] * q,
        scratch_shapes=[pltpu.SemaphoreType.DMA((7 * n_p,)), pltpu.SemaphoreType.DMA((7 * n_p,)),
                        pltpu.SemaphoreType.DMA((n,))],
        compiler_params=pltpu.CompilerParams(vmem_limit_bytes=32 * 1024 * 1024),
    )(*arrs, *side_in)


N_CHIPS = 4


def _sibling_exchange(arrs, name):
    n = len(arrs)

    def body(*refs):
        ins, outs = refs[:n], refs[n:2 * n]
        send_sems, recv_sems = refs[2 * n:]
        x, y, c = _my_place()
        sibling = (x, y, 1 - c)

        def copy(a, j):
            return pltpu.make_async_remote_copy(
                src_ref=ins[a].at[2 * j + (1 - c)], dst_ref=outs[a].at[j],
                send_sem=send_sems.at[a * N_CHIPS + j], recv_sem=recv_sems.at[a * N_CHIPS + j],
                device_id=sibling, device_id_type=MESH)

        cps = [copy(a, j) for j in range(N_CHIPS) for a in range(n)]
        for cp in cps:
            cp.start()
        for cp in cps:
            cp.wait_recv()
        for cp in cps:
            cp.wait_send()

    return pl.pallas_call(
        body, name=name,
        out_shape=[jax.ShapeDtypeStruct((N_CHIPS,) + a.shape[1:], a.dtype) for a in arrs],
        in_specs=[ANY] * n, out_specs=[ANY] * n,
        scratch_shapes=[pltpu.SemaphoreType.DMA((N_CHIPS * n,)), pltpu.SemaphoreType.DMA((N_CHIPS * n,))],
    )(*arrs)


def _sibling_copies(srcs, lands, send_sems, recv_sems):
    x, y, c = _my_place()
    cps = []
    for j in range(N_CHIPS):
        for a in range(len(srcs)):
            k = a * N_CHIPS + j
            cps.append(pltpu.make_async_remote_copy(
                src_ref=srcs[a].at[2 * j + (1 - c)], dst_ref=lands[a].at[j], send_sem=send_sems[k],
                recv_sem=recv_sems[k], device_id=(x, y, 1 - c), device_id_type=MESH))
    return cps


def _pair_add(mine, recv, core, name):
    n = len(mine)

    def body(c_ref, *refs):
        for a in range(n):
            refs[2 * n + a][...] = (refs[a][...].astype(F32) + refs[n + a][...].astype(F32)).astype(BF16)

    def blk(a):
        return (None,) + a.shape[1:]

    grid_spec = pltpu.PrefetchScalarGridSpec(
        num_scalar_prefetch=1, grid=(N_CHIPS,),
        in_specs=[pl.BlockSpec(blk(a), lambda j, c_ref: (2 * j + c_ref[0], 0, 0)) for a in mine]
        + [pl.BlockSpec(blk(a), lambda j, c_ref: (j, 0, 0)) for a in recv],
        out_specs=[pl.BlockSpec(blk(a), lambda j, c_ref: (j, 0, 0)) for a in recv])
    return pl.pallas_call(
        body, name=name, grid_spec=grid_spec,
        out_shape=[jax.ShapeDtypeStruct(a.shape, BF16) for a in recv],
        compiler_params=_params(32),
    )(core, *mine, *recv)


HBM = pl.BlockSpec(memory_space=pltpu.HBM)
SEM = pl.BlockSpec(memory_space=pltpu.SEMAPHORE)
N_PEER_CHIPS = 3


def _chip_copies(srcs, lands, send_sems, recv_sems):
    x, y, c = _my_place()
    my_chip = 2 * x + y
    peers = [(x, 1 - y), (1 - x, y), (1 - x, 1 - y)]
    cps = []
    for k, (px, py) in enumerate(peers):
        for a in range(len(srcs)):
            j = a * N_PEER_CHIPS + k
            cps.append(pltpu.make_async_remote_copy(
                src_ref=srcs[a].at[2 * px + py], dst_ref=lands[a].at[my_chip],
                send_sem=send_sems[j], recv_sem=recv_sems[j],
                device_id=(px, py, c), device_id_type=MESH))
    return cps


N_PEERS = N_DEV - 1


def _gather_copies(srcs, lands, send_sems, recv_sems):
    x, y, c = _my_place()
    me_idx = 4 * x + 2 * y + c
    flips = [(0, 0, 1), (0, 1, 0), (1, 0, 0), (0, 1, 1), (1, 0, 1), (1, 1, 0), (1, 1, 1)]
    cps = []
    for k, (fx, fy, fc) in enumerate(flips):
        peer = ((1 - x) if fx else x, (1 - y) if fy else y, (1 - c) if fc else c)
        for a in range(len(srcs)):
            j = a * N_PEERS + k
            cps.append(pltpu.make_async_remote_copy(
                src_ref=srcs[a], dst_ref=lands[a].at[me_idx], send_sem=send_sems[j], recv_sem=recv_sems[j],
                device_id=peer, device_id_type=MESH))
    return cps


def _split_start(copies, per_array, arrs, lands, name):
    arrs, lands = list(arrs), list(lands)
    n = len(arrs)
    k = n * per_array

    def body(*refs):
        srcs, land_refs = refs[:n], refs[n:2 * n]
        send_sems, recv_sems = refs[2 * n:2 * n + k], refs[2 * n + k:2 * n + 2 * k]
        token = refs[-1]
        for cp in copies(srcs, land_refs, send_sems, recv_sems):
            cp.start()
        token[...] = jnp.zeros_like(token)

    hbm_arrs = [pltpu.with_memory_space_constraint(a, pltpu.HBM) for a in arrs]
    lands = [pltpu.with_memory_space_constraint(a, pltpu.HBM) for a in lands]
    res = pl.pallas_call(
        body, name=name,
        out_shape=[pltpu.SemaphoreType.DMA(())] * (2 * k) + [pltpu.HBM(a.shape, a.dtype) for a in arrs + lands]
        + [jax.ShapeDtypeStruct((8, 128), F32)],
        in_specs=[HBM] * (2 * n),
        out_specs=[SEM] * (2 * k) + [HBM] * (2 * n) + [pl.BlockSpec(memory_space=pltpu.VMEM)],
        input_output_aliases={a: 2 * k + a for a in range(2 * n)},
        compiler_params=pltpu.CompilerParams(has_side_effects=pltpu.SideEffectType.DATAFLOW_SIDE_EFFECTING),
    )(*hbm_arrs, *lands)
    return res[:k], res[k:2 * k], res[2 * k:2 * k + n], res[2 * k + n:2 * k + 2 * n], res[-1]


def _split_wait(copies, per_array, send_sems, recv_sems, srcs, lands, after, name):
    n = len(srcs)
    k = n * per_array

    def body(*refs):
        src_refs, land_refs = refs[:n], refs[n:2 * n]
        s_sems, r_sems = refs[2 * n:2 * n + k], refs[2 * n + k:2 * n + 2 * k]
        for cp in copies(src_refs, land_refs, s_sems, r_sems):
            cp.wait_send()
            cp.wait_recv()

    res = pl.pallas_call(
        body, name=name,
        out_shape=[pltpu.HBM(a.shape, a.dtype) for a in list(srcs) + list(lands)],
        in_specs=[HBM] * (2 * n) + [SEM] * (2 * k) + [ANY],
        out_specs=[HBM] * (2 * n),
        input_output_aliases={a: a for a in range(2 * n)},
        compiler_params=pltpu.CompilerParams(has_side_effects=pltpu.SideEffectType.DATAFLOW_SIDE_EFFECTING),
    )(*srcs, *lands, *send_sems, *recv_sems, after)
    return res[:n], res[n:]


def _chip_exchange_start(arrs, name):
    return _split_start(_chip_copies, N_PEER_CHIPS, arrs, [lax.empty(a.shape, a.dtype) for a in arrs], name)


def _chip_exchange_wait(send_sems, recv_sems, srcs, lands, after, name):
    return _split_wait(_chip_copies, N_PEER_CHIPS, send_sems, recv_sems, srcs, lands, after, name)


def _gather_start(arrs, me_idx, name):
    lands = [lax.dynamic_update_slice(lax.empty((N_DEV,) + a.shape, a.dtype), a[None], (me_idx, 0, 0)) for a in arrs]
    return _split_start(_gather_copies, N_PEERS, arrs, lands, name)


def _gather_wait(send_sems, recv_sems, srcs, lands, after, name):
    return _split_wait(_gather_copies, N_PEERS, send_sems, recv_sems, srcs, lands, after, name)[1]


def _proj_fwd(x, g_pre, w_int, tabs):
    s = x.shape[0]
    tm = min(512, s)

    def body(x_ref, g_ref, t_ref, w_hbm,
             h_ref, pa_ref, pq_ref, pkv_ref, pbz_ref, pmq_ref, pmz_ref, pg_ref, w_vm, sems):
        _load_once([(w_hbm, w_vm)], sems)
        xf = x_ref[...]
        r = lax.rsqrt(jnp.mean(xf * xf, axis=-1, keepdims=True) + EPS)
        h = ((xf * r) * g_ref[...]).astype(BF16)
        h_ref[...] = h
        cs, s1, s2 = t_ref[0], t_ref[1], t_ref[2]

        def mm(seg, c0, width):
            return _dot_nt(h, w_vm[seg[0] + c0:seg[0] + c0 + width, :])

        for c0 in range(0, SEG_A[1], 512):
            pa_ref[:, c0:c0 + 512] = mm(SEG_A, c0, 512).astype(BF16)
        q = mm(SEG_BQ, 0, 512)
        for b in range(4):
            pq_ref[:, 128 * b:128 * b + 128] = _rope(q[:, 128 * b:128 * b + 128], cs, s1, s2).astype(BF16)
        kv = mm(SEG_BKV, 0, 256)
        pkv_ref[:, 0:128] = _rope(kv[:, 0:128], cs, s1, s2).astype(BF16)
        pkv_ref[:, 128:256] = kv[:, 128:256].astype(BF16)
        pbz_ref[...] = mm(SEG_BZ, 0, 512).astype(BF16)
        pmq_ref[...] = mm(SEG_MQ, 0, 512).astype(BF16)
        pmz_ref[...] = mm(SEG_MZ, 0, 512).astype(BF16)
        for c0 in range(0, SEG_G[1], 512):
            pg_ref[:, c0:c0 + 512] = mm(SEG_G, c0, 512).astype(BF16)

    widths = (D_MODEL, 2048, 512, 256, 512, 512, 512, 3072)
    return pl.pallas_call(
        body, name="proj_fwd", grid=(s // tm,),
        out_shape=[jax.ShapeDtypeStruct((s, w), BF16) for w in widths],
        in_specs=[_rows(tm, D_MODEL), _full((1, D_MODEL)), pl.BlockSpec((3, tm, 128), lambda i: (0, i, 0)), ANY],
        out_specs=[_rows(tm, w) for w in widths],
        scratch_shapes=[pltpu.VMEM((IN_WIDTH, D_MODEL), BF16), pltpu.SemaphoreType.DMA((1,))],
        compiler_params=_params(52),
    )(x, g_pre, tabs, w_int)


def _mem_kv_fwd(mem, g_mem, w_mkv):
    m = mem.shape[0]

    def body(mem_ref, g_ref, w_ref, mn_ref, mkv_ref):
        xf = mem_ref[...]
        r = lax.rsqrt(jnp.mean(xf * xf, axis=-1, keepdims=True) + EPS)
        mn = ((xf * r) * g_ref[...]).astype(BF16)
        mn_ref[...] = mn
        mkv_ref[...] = _dot(mn, w_ref[...]).astype(BF16)

    return pl.pallas_call(
        body, name="mem_kv_fwd", grid=(1,),
        out_shape=[jax.ShapeDtypeStruct((m, D_MODEL), BF16)] * 2,
        in_specs=[_full((m, D_MODEL)), _full((1, D_MODEL)), _full((D_MODEL, D_MODEL))],
        out_specs=[_full((m, D_MODEL))] * 2,
        compiler_params=_params(32),
    )(mem, g_mem, w_mkv)


def _halo_specs(s, tm, rows, width):
    nblk = s // rows
    prev = pl.BlockSpec((rows, width), lambda i: (jnp.maximum(i * (tm // rows) - 1, 0), 0))
    nxt = pl.BlockSpec((rows, width), lambda i: (jnp.minimum((i + 1) * (tm // rows), nblk - 1), 0))
    return prev, nxt


def _conv_common(pa, cu_prev, cu_next, w, tm):
    b, c, u, z = (pa[:, 512 * k:512 * k + 512] for k in range(4))
    cu = c * u
    row = lax.broadcasted_iota(jnp.int32, (tm, 512), 0)
    cu_m1 = jnp.where(row == 0, cu_prev, pltpu.roll(cu, 1, 0))
    cu_p1 = jnp.where(row == tm - 1, cu_next, pltpu.roll(cu, tm - 1, 0))
    y = cu_m1 * w[0:1] + cu * w[1:2] + cu_p1 * w[2:3]
    sig = _sigmoid(z)
    return b, c, u, z, cu, cu_m1, cu_p1, y, sig, row


def _conv_fwd(pa, w_conv):
    s = pa.shape[0]
    tm = min(512, s)
    nt = s // tm

    def body(pa_ref, pp_ref, pn_ref, w_ref, ya_ref):
        i = pl.program_id(0)
        prev_row = pp_ref[...].astype(F32)[15:16, :]
        next_row = pn_ref[...].astype(F32)[0:1, :]
        b, _, _, z, _, _, _, y, sig, _ = _conv_common(
            pa_ref[...].astype(F32),
            jnp.where(i == 0, 0.0, prev_row[:, 512:1024] * prev_row[:, 1024:1536]),
            jnp.where(i == nt - 1, 0.0, next_row[:, 512:1024] * next_row[:, 1024:1536]), w_ref[...], tm)
        ya_ref[...] = (b * y * (z * sig)).astype(BF16)

    prev, nxt = _halo_specs(s, tm, 16, 2048)
    return pl.pallas_call(
        body, name="conv_fwd", grid=(nt,),
        out_shape=jax.ShapeDtypeStruct((s, 512), BF16),
        in_specs=[_rows(tm, 2048), prev, nxt, _full((3, 512))],
        out_specs=_rows(tm, 512),
        compiler_params=_params(48),
    )(pa, pa, pa, w_conv)


def _heads_to_lanes(a, g, row):
    low = row < HEAD_DIM
    parts = []
    for b in (2 * g, 2 * g + 1):
        t = jnp.transpose(a[:, 128 * b:128 * b + 128])
        swapped = pltpu.roll(t, HEAD_DIM, 0)
        if g == 0:
            parts += [jnp.where(low, t, 0.0), jnp.where(low, swapped, 0.0)]
        else:
            parts += [jnp.where(low, 0.0, swapped), jnp.where(low, 0.0, t)]
    return jnp.concatenate(parts, axis=1)


def _lanes_to_heads(t0, t1, row):
    low = row < HEAD_DIM
    blocks = []
    for b in range(4):
        g = b // 2
        tg = (t0, t1)[g]
        je = 2 * (b - 2 * g)
        even, odd = tg[:, 128 * je:128 * je + 128], tg[:, 128 * je + 128:128 * je + 256]
        if g == 0:
            t = jnp.where(low, even, pltpu.roll(odd, HEAD_DIM, 0))
        else:
            t = jnp.where(low, pltpu.roll(even, HEAD_DIM, 0), odd)
        blocks.append(jnp.transpose(t))
    return jnp.concatenate(blocks, axis=1)


WINDOW_KEYS = 3 * ATTN_BLOCK
STACKED = 4 * ATTN_BLOCK
KEY_CHUNK = 32
BLOCKS_PER_STEP = 4


def _fill_band_bias(bias, nb):
    assert nb >= 2
    c = lax.broadcasted_iota(jnp.int32, (WINDOW_KEYS, STACKED), 0)
    r = lax.broadcasted_iota(jnp.int32, (WINDOW_KEYS, STACKED), 1) & (ATTN_BLOCK - 1)
    band = (c >= r) & (c <= r + 2 * ATTN_BLOCK)
    for v, ok in enumerate((band, band & (c >= ATTN_BLOCK), band & (c < 2 * ATTN_BLOCK))):
        bias[v] = jnp.where(ok, 0.0, -jnp.inf)


def _bias_variant(n, nb):
    return jnp.where(n == 0, 1, jnp.where(n == nb - 1, 2, 0))


def _sink_row(sink_ref, g):
    return jnp.concatenate([jnp.full((1, ATTN_BLOCK), sink_ref[4 * g + j], F32) for j in range(4)], axis=1)


def _softmax_keys_major(sc, bias, variant, sink, e_scr):
    chunks = [pl.ds(k * KEY_CHUNK, KEY_CHUNK) for k in range(WINDOW_KEYS // KEY_CHUNK)]
    rows = [slice(k * KEY_CHUNK, (k + 1) * KEY_CHUNK) for k in range(WINDOW_KEYS // KEY_CHUNK)]
    m_run = jnp.full((KEY_CHUNK, STACKED), -jnp.inf, F32)
    for ck, rw in zip(chunks, rows):
        m_run = jnp.maximum(m_run, sc[rw] * ATTN_SCALE + bias[variant, ck, :])
    m = jnp.maximum(jnp.max(m_run, axis=0, keepdims=True), sink)
    l_run = jnp.zeros((KEY_CHUNK, STACKED), F32)
    for ck, rw in zip(chunks, rows):
        e = jnp.exp(sc[rw] * ATTN_SCALE + bias[variant, ck, :] - m)
        l_run += e
        e_scr[rw, :] = e.astype(BF16)
    es = jnp.exp(sink - m)
    inv = 1.0 / (jnp.sum(l_run, axis=0, keepdims=True) + es)
    return inv, es * inv


def _fill_padded(kv_ref, kpad, vpad, s):
    zero = jnp.zeros((ATTN_BLOCK, 128), BF16)
    kpad[0:ATTN_BLOCK, :] = zero
    vpad[0:ATTN_BLOCK, :] = zero
    kpad[ATTN_BLOCK + s:2 * ATTN_BLOCK + s, :] = zero
    vpad[ATTN_BLOCK + s:2 * ATTN_BLOCK + s, :] = zero
    kpad[ATTN_BLOCK:ATTN_BLOCK + s, :] = kv_ref[:, 0:128]
    vpad[ATTN_BLOCK:ATTN_BLOCK + s, :] = kv_ref[:, 128:256]


def _attn_fwd(pq, pkv, pbz, sink):
    s = pq.shape[0]
    nb = s // ATTN_BLOCK

    def body(sink_ref, q_ref, z_ref, kv_ref, yb_ref, kpad, vpad, bias, e_scr):
        i = pl.program_id(0)

        @pl.when(i == 0)
        def _():
            _fill_padded(kv_ref, kpad, vpad, s)
            _fill_band_bias(bias, nb)

        row = lax.broadcasted_iota(jnp.int32, (ATTN_BLOCK, 128), 0)
        for b in range(BLOCKS_PER_STEP):
            n = i * BLOCKS_PER_STEP + b
            rows = slice(b * ATTN_BLOCK, (b + 1) * ATTN_BLOCK)
            start = pl.multiple_of(n * ATTN_BLOCK, ATTN_BLOCK)
            kw, vw = kpad[pl.ds(start, WINDOW_KEYS), :], vpad[pl.ds(start, WINDOW_KEYS), :]
            qf = q_ref[rows, :].astype(F32)
            variant = _bias_variant(n, nb)
            outs = []
            for g in range(2):
                e_bg = e_scr.at[2 * b + g]
                qt = _heads_to_lanes(qf, g, row).astype(BF16)
                inv, _ = _softmax_keys_major(_dot(kw, qt), bias, variant, _sink_row(sink_ref, g), e_bg)
                outs.append(_dot_tn(vw, e_bg[...]) * inv)
            attn = _lanes_to_heads(outs[0], outs[1], row)
            z = z_ref[rows, :].astype(F32)
            yb_ref[rows, :] = (attn * (z * _sigmoid(z))).astype(BF16)

    tq = BLOCKS_PER_STEP * ATTN_BLOCK
    return pl.pallas_call(
        body, name="attn_fwd", grid=(s // tq,),
        out_shape=jax.ShapeDtypeStruct((s, 512), BF16),
        in_specs=[pl.BlockSpec(memory_space=pltpu.SMEM), _rows(tq, 512), _rows(tq, 512), _full((s, 256))],
        out_specs=_rows(tq, 512),
        scratch_shapes=[pltpu.VMEM((s + 2 * ATTN_BLOCK, 128), BF16)] * 2
        + [pltpu.VMEM((3, WINDOW_KEYS, STACKED), F32),
           pltpu.VMEM((2 * BLOCKS_PER_STEP, WINDOW_KEYS, STACKED), BF16)],
        compiler_params=_params(32),
    )(sink, pq, pbz, pkv)


def _mem_softmax_t(q, mk):
    sc = _dot_nt(mk, q) * MEM_SCALE
    e = jnp.exp(sc - jnp.max(sc, axis=0, keepdims=True))
    return e * (1.0 / jnp.sum(e, axis=0, keepdims=True))


def _mem_attn_fwd(pmq, pmz, mkv):
    s = pmq.shape[0]
    m = mkv.shape[0]
    tm = min(512, s)

    def body(q_ref, z_ref, mk_ref, mv_ref, ym_ref):
        z = z_ref[...].astype(F32)
        sz = z * _sigmoid(z)
        for h in range(MEM_HEADS):
            cols = slice(128 * h, 128 * h + 128)
            pt = _mem_softmax_t(q_ref[:, cols], mk_ref[:, cols])
            o = _dot_tn(pt.astype(BF16), mv_ref[:, cols])
            ym_ref[:, cols] = (o * sz[:, cols]).astype(BF16)

    return pl.pallas_call(
        body, name="mem_attn_fwd", grid=(s // tm,),
        out_shape=jax.ShapeDtypeStruct((s, 512), BF16),
        in_specs=[_rows(tm, 512), _rows(tm, 512), pl.BlockSpec((m, 512), lambda i: (0, 0)),
                  pl.BlockSpec((m, 512), lambda i: (0, 1))],
        out_specs=_rows(tm, 512),
        compiler_params=_params(32),
    )(pmq, pmz, mkv, mkv)


def _mid(ya, yb, ym, pg, x, target, g_post, w_up, w_out):
    s = x.shape[0]
    tm = min(256, s)
    nt = s // tm

    def body(ya_ref, yb_ref, ym_ref, pg_ref, x_ref, t_ref, gp_ref, wup_hbm, wout_hbm,
             dg_ref, dya_ref, dyb_ref, dym_ref, dy_ref, loss_ref, ggp_ref, mb_ref, dob_ref, du_ref,
             wup_vm, wout_vm, sems):
        i = pl.program_id(0)
        _load_once([(wup_hbm.at[d], wup_vm.at[:, pl.ds(128 * d, 128)]) for d in range(N_DEV)]
                   + [(wout_hbm, wout_vm)], sems)

        @pl.when(i == 0)
        def _():
            loss_ref[...] = jnp.zeros_like(loss_ref)
            ggp_ref[...] = jnp.zeros_like(ggp_ref)

        ys = (ya_ref[...], yb_ref[...], ym_ref[...])
        us = [_dot(ys[k], wup_vm[512 * k:512 * k + 512, :]) for k in range(3)]
        gates = [_sigmoid(pg_ref[:, 1024 * k:1024 * k + 1024].astype(F32)) for k in range(3)]
        merged = gates[0] * us[0] + gates[1] * us[1] + gates[2] * us[2]
        mb = merged.astype(BF16)
        mb_ref[...] = mb
        out = _dot(mb, wout_vm[...])
        r = lax.rsqrt(jnp.mean(out * out, axis=-1, keepdims=True) + EPS)
        on = out * r
        gp = gp_ref[...]
        err = (x_ref[...] + on * gp) - t_ref[...]
        loss_ref[...] += 0.5 * jnp.sum(err * err) * (1.0 / D_MODEL)
        dy = err * (1.0 / D_MODEL)
        dy_ref[...] = dy
        ggp_ref[...] += jnp.sum(dy * on, axis=0, keepdims=True)
        a = dy * gp
        d_out = r * (a - on * jnp.mean(a * on, axis=-1, keepdims=True))
        dob = d_out.astype(BF16)
        dob_ref[...] = dob
        d_merged = _dot_nt(dob, wout_vm[...])
        d_refs = (dya_ref, dyb_ref, dym_ref)
        for k in range(3):
            g = gates[k]
            du_f = d_merged * g
            dg_ref[:, 1024 * k:1024 * k + 1024] = (du_f * us[k] * (1.0 - g)).astype(BF16)
            du = du_f.astype(BF16)
            du_ref[k] = du
            d_refs[k][...] = _dot_nt(du, wup_vm[512 * k:512 * k + 512, :]).astype(BF16)

    return pl.pallas_call(
        body, name="mid", grid=(nt,),
        out_shape=[jax.ShapeDtypeStruct((s, 3072), BF16)] + [jax.ShapeDtypeStruct((s, 512), BF16)] * 3
        + [jax.ShapeDtypeStruct((s, D_MODEL), F32), jax.ShapeDtypeStruct((8, 128), F32),
           jax.ShapeDtypeStruct((1, D_MODEL), F32), jax.ShapeDtypeStruct((s, D_MODEL), BF16),
           jax.ShapeDtypeStruct((s, D_MODEL), BF16), jax.ShapeDtypeStruct((3, s, D_MODEL), BF16)],
        in_specs=[_rows(tm, 512)] * 3 + [_rows(tm, 3072), _rows(tm, D_MODEL), _rows(tm, D_MODEL),
                                         _full((1, D_MODEL)), ANY, ANY],
        out_specs=[_rows(tm, 3072)] + [_rows(tm, 512)] * 3
        + [_rows(tm, D_MODEL), _full((8, 128)), _full((1, D_MODEL)), _rows(tm, D_MODEL), _rows(tm, D_MODEL),
           pl.BlockSpec((3, tm, D_MODEL), lambda i: (0, i, 0))],
        scratch_shapes=[pltpu.VMEM((1536, D_MODEL), BF16), pltpu.VMEM((D_MODEL, D_MODEL), BF16),
                        pltpu.SemaphoreType.DMA((N_DEV + 1,))],
        compiler_params=_params(56),
    )(ya, yb, ym, pg, x, target, g_post, w_up, w_out)


def _gw_mid(mb, dob, ys, du):
    s = mb.shape[0]
    tn = 256

    def out_body(mb_ref, dob_ref, o_ref):
        o_ref[...] = _dot_tn(mb_ref[...], dob_ref[...]).astype(BF16)

    gw_out = pl.pallas_call(
        out_body, name="gw_out", grid=(D_MODEL // tn,),
        out_shape=jax.ShapeDtypeStruct((D_MODEL, D_MODEL), BF16),
        in_specs=[pl.BlockSpec((s, tn), lambda j: (0, j)), _full((s, D_MODEL))],
        out_specs=pl.BlockSpec((tn, D_MODEL), lambda j: (j, 0)),
        compiler_params=_params(48),
    )(mb, dob)

    per = 512 // tn

    def up_body(ya_ref, yb_ref, ym_ref, du_ref, o_ref):
        j = pl.program_id(0)
        for k, y_ref in enumerate((ya_ref, yb_ref, ym_ref)):
            @pl.when(j // per == k)
            def _(y_ref=y_ref):
                res = _dot_tn(y_ref[...], du_ref[...])
                for d in range(N_DEV):
                    o_ref[d] = res[:, 128 * d:128 * d + 128].astype(BF16)

    def y_spec(k):
        return pl.BlockSpec((s, tn), lambda j: (0, jnp.clip(j - per * k, 0, per - 1)))

    gw_up = pl.pallas_call(
        up_body, name="gw_up", grid=(3 * per,),
        out_shape=jax.ShapeDtypeStruct((N_DEV, 1536, 128), BF16),
        in_specs=[y_spec(0), y_spec(1), y_spec(2), pl.BlockSpec((None, s, D_MODEL), lambda j: (j // per, 0, 0))],
        out_specs=pl.BlockSpec((N_DEV, tn, 128), lambda j: (0, j, 0)),
        compiler_params=_params(48),
    )(*ys, du)
    return gw_out, gw_up


def _conv_bwd(pa, dya, w_conv):
    s = pa.shape[0]
    tm = min(512, s)
    nt = s // tm

    def body(pa_ref, pp_ref, pn_ref, d_ref, dp_ref, dn_ref, w_ref, da_ref, gw_ref):
        i = pl.program_id(0)
        first, last = i == 0, i == nt - 1

        @pl.when(first)
        def _():
            gw_ref[...] = jnp.zeros_like(gw_ref)

        w = w_ref[...]
        prev_row = pp_ref[...].astype(F32)[15:16, :]
        next_row = pn_ref[...].astype(F32)[0:1, :]
        b, c, u, z, cu, cu_m1, cu_p1, y, sig, row = _conv_common(
            pa_ref[...].astype(F32),
            jnp.where(first, 0.0, prev_row[:, 512:1024] * prev_row[:, 1024:1536]),
            jnp.where(last, 0.0, next_row[:, 512:1024] * next_row[:, 1024:1536]), w, tm)
        sz = z * sig
        dya_t = d_ref[...].astype(F32)
        d_y = dya_t * b * sz

        def halo_dy(p_row, d_row):
            zz = p_row[:, 1536:2048]
            return d_row * p_row[:, 0:512] * (zz * _sigmoid(zz))

        dy_prev = jnp.where(first, 0.0, halo_dy(prev_row, dp_ref[...].astype(F32)[15:16, :]))
        dy_next = jnp.where(last, 0.0, halo_dy(next_row, dn_ref[...].astype(F32)[0:1, :]))
        dy_m1 = jnp.where(row == 0, dy_prev, pltpu.roll(d_y, 1, 0))
        dy_p1 = jnp.where(row == tm - 1, dy_next, pltpu.roll(d_y, tm - 1, 0))
        d_cu = dy_p1 * w[0:1] + d_y * w[1:2] + dy_m1 * w[2:3]
        da_ref[:, 0:512] = (dya_t * y * sz).astype(BF16)
        da_ref[:, 512:1024] = (d_cu * u).astype(BF16)
        da_ref[:, 1024:1536] = (d_cu * c).astype(BF16)
        da_ref[:, 1536:2048] = (dya_t * b * y * (sig * (1.0 + z * (1.0 - sig)))).astype(BF16)
        gw_ref[0:1, :] += jnp.sum(d_y * cu_m1, axis=0, keepdims=True)
        gw_ref[1:2, :] += jnp.sum(d_y * cu, axis=0, keepdims=True)
        gw_ref[2:3, :] += jnp.sum(d_y * cu_p1, axis=0, keepdims=True)

    prev, nxt = _halo_specs(s, tm, 16, 2048)
    dprev, dnxt = _halo_specs(s, tm, 16, 512)
    return pl.pallas_call(
        body, name="conv_bwd", grid=(nt,),
        out_shape=[jax.ShapeDtypeStruct((s, 2048), BF16), jax.ShapeDtypeStruct((8, 512), F32)],
        in_specs=[_rows(tm, 2048), prev, nxt, _rows(tm, 512), dprev, dnxt, _full((3, 512))],
        out_specs=[_rows(tm, 2048), _full((8, 512))],
        compiler_params=_params(48),
    )(pa, pa, pa, dya, dya, dya, w_conv)


def _attn_bwd(pq, pkv, pbz, dyb, sink, tabs):
    s = pq.shape[0]
    nb = s // ATTN_BLOCK

    def body(sink_ref, q_ref, z_ref, d_ref, kv_ref, t_ref,
             dq_ref, dz_ref, dkv_ref, gs_ref, kpad, vpad, dk_acc, dv_acc, bias, e_scr, ds_scr):
        i = pl.program_id(0)

        @pl.when(i == 0)
        def _():
            _fill_padded(kv_ref, kpad, vpad, s)
            _fill_band_bias(bias, nb)
            dk_acc[...] = jnp.zeros_like(dk_acc)
            dv_acc[...] = jnp.zeros_like(dv_acc)
            gs_ref[...] = jnp.zeros_like(gs_ref)

        row = lax.broadcasted_iota(jnp.int32, (ATTN_BLOCK, 128), 0)
        for b in range(BLOCKS_PER_STEP):
            n = i * BLOCKS_PER_STEP + b
            rows = slice(b * ATTN_BLOCK, (b + 1) * ATTN_BLOCK)
            start = pl.multiple_of(n * ATTN_BLOCK, ATTN_BLOCK)
            kw, vw = kpad[pl.ds(start, WINDOW_KEYS), :], vpad[pl.ds(start, WINDOW_KEYS), :]
            qf = q_ref[rows, :].astype(F32)
            variant = _bias_variant(n, nb)
            z = z_ref[rows, :].astype(F32)
            sig = _sigmoid(z)
            dyb_t = d_ref[rows, :].astype(F32)
            d_attn = dyb_t * (z * sig)
            outs, dqs = [], []
            dk_w = jnp.zeros((WINDOW_KEYS, 128), F32)
            dv_w = jnp.zeros((WINDOW_KEYS, 128), F32)
            for g in range(2):
                e_bg, ds_bg = e_scr.at[2 * b + g], ds_scr.at[2 * b + g]
                qt = _heads_to_lanes(qf, g, row)
                inv, p_sink = _softmax_keys_major(
                    _dot(kw, qt.astype(BF16)), bias, variant, _sink_row(sink_ref, g), e_bg)
                ot = _dot_tn(vw, e_bg[...]) * inv
                outs.append(ot)
                dot_ = _heads_to_lanes(d_attn, g, row)
                delta = jnp.sum(dot_ * ot, axis=0, keepdims=True)
                dpt = _dot(vw, dot_.astype(BF16))
                for k in range(WINDOW_KEYS // KEY_CHUNK):
                    rw = slice(k * KEY_CHUNK, (k + 1) * KEY_CHUNK)
                    ds_bg[rw, :] = (e_bg[rw, :].astype(F32) * (dpt[rw] - delta)).astype(BF16)
                sink_part = p_sink * delta
                for j in range(4):
                    h = 4 * g + j
                    gs_ref[h:h + 1, :] -= jnp.sum(sink_part[:, 128 * j:128 * j + 128])
                dqs.append(_dot_tn(kw, ds_bg[...]) * (inv * ATTN_SCALE))
                dk_w += _dot_nt(ds_bg[...], (qt * inv).astype(BF16)) * ATTN_SCALE
                dv_w += _dot_nt(e_bg[...], (dot_ * inv).astype(BF16))
            dk_acc[pl.ds(start, WINDOW_KEYS), :] += dk_w
            dv_acc[pl.ds(start, WINDOW_KEYS), :] += dv_w
            attn = _lanes_to_heads(outs[0], outs[1], row)
            dz_ref[rows, :] = (dyb_t * attn * (sig * (1.0 + z * (1.0 - sig)))).astype(BF16)
            dq = _lanes_to_heads(dqs[0], dqs[1], row)
            trows = pl.ds(start, ATTN_BLOCK)
            cs, s1, s2 = t_ref[0, trows, :], t_ref[1, trows, :], t_ref[2, trows, :]
            for blk in range(4):
                cols = slice(128 * blk, 128 * blk + 128)
                dq_ref[rows, cols] = _rope_t(dq[:, cols], cs, s1, s2).astype(BF16)

        @pl.when(i == nb // BLOCKS_PER_STEP - 1)
        def _():
            dk = dk_acc[ATTN_BLOCK:ATTN_BLOCK + s, :]
            dkv_ref[:, 0:128] = _rope_t(dk, t_ref[0], t_ref[1], t_ref[2]).astype(BF16)
            dkv_ref[:, 128:256] = dv_acc[ATTN_BLOCK:ATTN_BLOCK + s, :].astype(BF16)

    tq = BLOCKS_PER_STEP * ATTN_BLOCK
    tile = _rows(tq, 512)
    return pl.pallas_call(
        body, name="attn_bwd", grid=(s // tq,),
        out_shape=[jax.ShapeDtypeStruct((s, 512), BF16), jax.ShapeDtypeStruct((s, 512), BF16),
                   jax.ShapeDtypeStruct((s, 256), BF16), jax.ShapeDtypeStruct((8, 128), F32)],
        in_specs=[pl.BlockSpec(memory_space=pltpu.SMEM), tile, tile, tile, _full((s, 256)), _full((3, s, 128))],
        out_specs=[tile, tile, _full((s, 256)), _full((8, 128))],
        scratch_shapes=[pltpu.VMEM((s + 2 * ATTN_BLOCK, 128), BF16)] * 2
        + [pltpu.VMEM((s + 2 * ATTN_BLOCK, 128), F32)] * 2
        + [pltpu.VMEM((3, WINDOW_KEYS, STACKED), F32)]
        + [pltpu.VMEM((2 * BLOCKS_PER_STEP, WINDOW_KEYS, STACKED), BF16)] * 2,
        compiler_params=_params(48),
    )(sink, pq, pbz, dyb, pkv, tabs)


def _mem_attn_bwd(pmq, pmz, mkv, dym):
    s = pmq.shape[0]
    m = mkv.shape[0]
    tm = min(512, s)

    def body(q_ref, z_ref, d_ref, mk_ref, mv_ref, dq_ref, dz_ref, dmkv_ref):
        @pl.when(pl.program_id(0) == 0)
        def _():
            dmkv_ref[...] = jnp.zeros_like(dmkv_ref)

        z = z_ref[...].astype(F32)
        sig = _sigmoid(z)
        dym_t = d_ref[...].astype(F32)
        d_attn = dym_t * (z * sig)
        dsilu = sig * (1.0 + z * (1.0 - sig))
        for h in range(MEM_HEADS):
            cols = slice(128 * h, 128 * h + 128)
            q, mk, mv = q_ref[:, cols], mk_ref[:, cols], mv_ref[:, cols]
            pt = _mem_softmax_t(q, mk)
            pb = pt.astype(BF16)
            o = _dot_tn(pb, mv)
            dob = d_attn[:, cols].astype(BF16)
            dpt = _dot_nt(mv, dob)
            dst = (pt * (dpt - jnp.sum(pt * dpt, axis=0, keepdims=True))).astype(BF16)
            dq_ref[:, cols] = (_dot_tn(dst, mk) * MEM_SCALE).astype(BF16)
            dz_ref[:, cols] = (dym_t[:, cols] * o * dsilu[:, cols]).astype(BF16)
            dmkv_ref[:, cols] += _dot(dst, q) * MEM_SCALE
            dmkv_ref[:, 512 + 128 * h:512 + 128 * h + 128] += _dot(pb, dob)

    return pl.pallas_call(
        body, name="mem_attn_bwd", grid=(s // tm,),
        out_shape=[jax.ShapeDtypeStruct((s, 512), BF16), jax.ShapeDtypeStruct((s, 512), BF16),
                   jax.ShapeDtypeStruct((m, D_MODEL), F32)],
        in_specs=[_rows(tm, 512), _rows(tm, 512), _rows(tm, 512), pl.BlockSpec((m, 512), lambda i: (0, 0)),
                  pl.BlockSpec((m, 512), lambda i: (0, 1))],
        out_specs=[_rows(tm, 512), _rows(tm, 512), _full((m, D_MODEL))],
        compiler_params=_params(32),
    )(pmq, pmz, dym, mkv, mkv)


def _mem_kv_bwd(mem, g_mem, mn, dmkv, w_mkv):
    m = mem.shape[0]

    def body(mem_ref, g_ref, mn_ref, d_ref, w_ref, gw_ref, gg_ref):
        db = d_ref[...].astype(BF16)
        gw_ref[...] = _dot_tn(mn_ref[...], db).astype(BF16)
        d_mn = _dot_nt(db, w_ref[...])
        xf = mem_ref[...]
        r = lax.rsqrt(jnp.mean(xf * xf, axis=-1, keepdims=True) + EPS)
        gg_ref[...] = jnp.sum(d_mn * (xf * r), axis=0, keepdims=True)

    return pl.pallas_call(
        body, name="mem_kv_bwd", grid=(1,),
        out_shape=[jax.ShapeDtypeStruct((D_MODEL, D_MODEL), BF16), jax.ShapeDtypeStruct((1, D_MODEL), F32)],
        in_specs=[_full((m, D_MODEL)), _full((1, D_MODEL)), _full((m, D_MODEL)), _full((m, D_MODEL)),
                  _full((D_MODEL, D_MODEL))],
        out_specs=[_full((D_MODEL, D_MODEL)), _full((1, D_MODEL))],
        compiler_params=_params(32),
    )(mem, g_mem, mn, dmkv, w_mkv)


def _dh_bwd(dparts, x, dy, g_pre, w_int):
    s = x.shape[0]
    tm = min(256, s)

    def body(*refs):
        d_refs = refs[:7]
        x_ref, dy_ref, g_ref, w_hbm, gx_ref, gg_ref, w_vm, sems = refs[7:]
        _load_once([(w_hbm, w_vm)], sems)

        @pl.when(pl.program_id(0) == 0)
        def _():
            gg_ref[...] = jnp.zeros_like(gg_ref)

        d_h = jnp.zeros((tm, D_MODEL), F32)
        for d_ref, (r0, width) in zip(d_refs, SEGS):
            for c0 in range(0, width, 512):
                cw = min(512, width - c0)
                d_h += _dot(d_ref[:, c0:c0 + cw], w_vm[r0 + c0:r0 + c0 + cw, :])
        xf = x_ref[...]
        r = lax.rsqrt(jnp.mean(xf * xf, axis=-1, keepdims=True) + EPS)
        xn = xf * r
        a = d_h * g_ref[...]
        gx_ref[...] = r * (a - xn * jnp.mean(a * xn, axis=-1, keepdims=True)) + dy_ref[...]
        gg_ref[...] += jnp.sum(d_h * xn, axis=0, keepdims=True)

    return pl.pallas_call(
        body, name="dh_bwd", grid=(s // tm,),
        out_shape=[jax.ShapeDtypeStruct((s, D_MODEL), F32), jax.ShapeDtypeStruct((1, D_MODEL), F32)],
        in_specs=[_rows(tm, w) for _, w in SEGS] + [_rows(tm, D_MODEL), _rows(tm, D_MODEL), _full((1, D_MODEL)), ANY],
        out_specs=[_rows(tm, D_MODEL), _full((1, D_MODEL))],
        scratch_shapes=[pltpu.VMEM((IN_WIDTH, D_MODEL), BF16), pltpu.SemaphoreType.DMA((1,))],
        compiler_params=_params(52),
    )(*dparts, x, dy, g_pre, w_int)


def _gw_in(dparts, h):
    s = h.shape[0]
    tn = 256
    starts, counts = [], []
    for r0, width in SEGS:
        starts.append(r0 // tn)
        counts.append(width // tn)

    def body(*refs):
        d_refs = refs[:7]
        h_hbm, o_ref, h_vm, sems = refs[7:]
        _load_once([(h_hbm, h_vm)], sems)
        j = pl.program_id(0)
        for d_ref, st, cnt in zip(d_refs, starts, counts):
            @pl.when((j >= st) & (j < st + cnt))
            def _(d_ref=d_ref):
                o_ref[...] = _dot_tn(d_ref[...], h_vm[...]).astype(BF16)

    def seg_spec(st, cnt):
        return pl.BlockSpec((s, tn), lambda j: (0, jnp.clip(j - st, 0, cnt - 1)))

    return pl.pallas_call(
        body, name="gw_in", grid=(IN_WIDTH // tn,),
        out_shape=jax.ShapeDtypeStruct((IN_WIDTH, D_MODEL), BF16),
        in_specs=[seg_spec(st, cnt) for st, cnt in zip(starts, counts)] + [ANY],
        out_specs=pl.BlockSpec((tn, D_MODEL), lambda j: (j, 0)),
        scratch_shapes=[pltpu.VMEM((s, D_MODEL), BF16), pltpu.SemaphoreType.DMA((1,))],
        compiler_params=_params(52),
    )(*dparts, h)


def _adamw_math(w, g, m, v):
    m2 = ADAM_B1 * m + (1.0 - ADAM_B1) * g
    v2 = ADAM_B2 * v + (1.0 - ADAM_B2) * (g * g)
    m_hat = m2 / (1.0 - ADAM_B1 ** ADAM_STEP)
    v_hat = v2 / (1.0 - ADAM_B2 ** ADAM_STEP)
    delta = -ADAM_LR * (m_hat / (jnp.sqrt(v_hat) + ADAM_EPS) + ADAM_WD * w)
    return delta, m2, v2


def _sum_adamw(own, land, chip, after, block, w, m, v, name, tiles=1):
    r, c = w.shape
    rt = r // tiles

    def body(c_ref, after_ref, own_ref, l1_ref, l2_ref, l3_ref, w_ref, m_ref, v_ref, g_ref, d_ref, m2_ref, v2_ref):
        g = own_ref[...].astype(F32)
        for l_ref in (l1_ref, l2_ref, l3_ref):
            g += l_ref[...].astype(F32)
        g_ref[...] = g
        d_ref[...], m2_ref[...], v2_ref[...] = _adamw_math(w_ref[...], g, m_ref[...], v_ref[...])

    def share(k):
        return pl.BlockSpec((None, rt, c), lambda i, c_ref: (jnp.bitwise_xor(c_ref[0], k), block * tiles + i, 0))

    spec = pl.BlockSpec((rt, c), lambda i, c_ref: (i, 0))
    grid_spec = pltpu.PrefetchScalarGridSpec(
        num_scalar_prefetch=1, grid=(tiles,),
        in_specs=[pl.BlockSpec((8, 128), lambda i, c_ref: (0, 0)), share(0), share(1), share(2), share(3)]
        + [spec] * 3, out_specs=[spec] * 4)
    return pl.pallas_call(
        body, name=name, grid_spec=grid_spec,
        out_shape=[jax.ShapeDtypeStruct((r, c), F32)] * 4,
        compiler_params=_params(48),
    )(chip, after, own, land, land, land, w, m, v)


def _sum_adamw_group(items, chip, after, name):
    k = len(items)

    def body(c_ref, after_ref, *refs):
        shares, wmv, outs = refs[:4 * k], refs[4 * k:7 * k], refs[7 * k:]
        for j in range(k):
            g = shares[4 * j][...].astype(F32)
            for l_ref in shares[4 * j + 1:4 * j + 4]:
                g += l_ref[...].astype(F32)
            outs[4 * j][...] = g
            outs[4 * j + 1][...], outs[4 * j + 2][...], outs[4 * j + 3][...] = _adamw_math(
                wmv[3 * j][...], g, wmv[3 * j + 1][...], wmv[3 * j + 2][...])

    def share(shape, block, q):
        return pl.BlockSpec((None,) + shape, lambda i, c_ref: (jnp.bitwise_xor(c_ref[0], q), block, 0))

    in_specs, args = [pl.BlockSpec((8, 128), lambda i, c_ref: (0, 0))], [after]
    for own, land, block, w, m, v in items:
        in_specs += [share(w.shape, block, q) for q in range(4)]
        args += [own, land, land, land]
    for own, land, block, w, m, v in items:
        in_specs += [pl.BlockSpec(w.shape, lambda i, c_ref: (0, 0))] * 3
        args += [w, m, v]
    out_specs = [pl.BlockSpec(w.shape, lambda i, c_ref: (0, 0)) for _, _, _, w, _, _ in items for _ in range(4)]
    res = pl.pallas_call(
        body, name=name,
        grid_spec=pltpu.PrefetchScalarGridSpec(num_scalar_prefetch=1, grid=(1,), in_specs=in_specs,
                                               out_specs=out_specs),
        out_shape=[jax.ShapeDtypeStruct(w.shape, F32) for _, _, _, w, _, _ in items for _ in range(4)],
        compiler_params=_params(48),
    )(chip, *args)
    return [res[4 * j:4 * j + 4] for j in range(k)]


def _small_pack(parts):
    def pack_body(gpre_ref, gconv_ref, gsink_ref, gmem_ref, gpost_ref, loss_ref, pack):
        lane = lax.broadcasted_iota(jnp.int32, (1, 128), 1)
        sink_row = jnp.zeros((1, 128), F32)
        for h in range(8):
            sink_row = jnp.where(lane == h, gsink_ref[h:h + 1, :], sink_row)
        pack[...] = jnp.zeros_like(pack)
        pack[0:1, :] = gpre_ref[...]
        pack[1:2, :] = gmem_ref[...]
        pack[2:3, :] = gpost_ref[...]
        pack[3:6, 0:512] = gconv_ref[0:3, :]
        pack[6:7, 0:128] = sink_row
        pack[7:8, 0:128] = loss_ref[0:1, :]

    return pl.pallas_call(
        pack_body, name="small_pack", grid=(1,),
        out_shape=jax.ShapeDtypeStruct((8, D_MODEL), F32),
        in_specs=[_full(p.shape) for p in parts], out_specs=_full((8, D_MODEL)),
    )(*parts)


def _small_apply(packs, ws, ms, vs):
    def apply(p_ref, *refs):
        w_refs, m_refs, v_refs = refs[0:5], refs[5:10], refs[10:15]
        loss_out = refs[15]
        g_outs, d_outs, m_outs, v_outs = refs[16:21], refs[21:26], refs[26:31], refs[31:36]
        x, y, c = _my_place()
        tot = p_ref[0]
        for d in range(1, N_DEV):
            tot = tot + p_ref[d]
        conv = pltpu.roll(tot[:, 0:512], (512 - 64 * (4 * x + 2 * y + c)) % 512, 1)[3:6, 0:64]
        grads = (tot[0:1, :], conv, tot[6:7, 0:8], tot[1:2, :], tot[2:3, :])
        loss_out[...] = tot[7:8, 0:128]
        for j in range(5):
            g_outs[j][...] = grads[j]
            d_outs[j][...], m_outs[j][...], v_outs[j][...] = _adamw_math(
                w_refs[j][...], grads[j], m_refs[j][...], v_refs[j][...])

    specs = [_full(w.shape) for w in ws]
    res = pl.pallas_call(
        apply, name="small_apply", grid=(1,),
        out_shape=[jax.ShapeDtypeStruct((1, 128), F32)] + [jax.ShapeDtypeStruct(w.shape, F32) for w in ws] * 4,
        in_specs=[_full((N_DEV, 8, D_MODEL))] + specs * 3,
        out_specs=[_full((1, 128))] + specs * 4,
    )(packs, *ws, *ms, *vs)
    return res[0], res[1:6], res[6:11], res[11:16], res[16:21]


def kernel(x, mem, g_pre, w_in, w_conv, attn_sink, g_mem, w_mem_kv, w_up_a, w_up_b, w_up_m, w_out, g_post, loss_target, m_g_pre, m_w_in, m_w_conv, m_attn_sink, m_g_mem, m_w_mem_kv, m_w_up_a, m_w_up_b, m_w_up_m, m_w_out, m_g_post, v_g_pre, v_w_in, v_w_conv, v_attn_sink, v_g_mem, v_w_mem_kv, v_w_up_a, v_w_up_b, v_w_up_m, v_w_out, v_g_post):
    s = x.shape[1]
    x2, mem2, tgt2 = x[0], mem[0], loss_target[0]
    me = 4 * lax.axis_index("x") + 2 * lax.axis_index("y") + lax.axis_index("c")

    w_conv_loc = jnp.zeros((8, 128), F32).at[:3, :64].set(w_conv[0])
    w_int_g, w_conv_g, tabs, w_mkv_loc, w_out_loc, w_up_loc = _all_gather(
        [w_in[0].T.astype(BF16), w_conv_loc], "gather_w_in",
        splits=[[(112 * k, 112) for k in range(7)] + [(784, 144)], [(0, 8)]],
        side=_gather_side(s, w_mem_kv[0], w_out[0], (w_up_a[0], w_up_b[0], w_up_m[0])))
    w_int = w_int_g.reshape(IN_WIDTH, D_MODEL)
    w_conv_f = w_conv_g[:, :3, :64].transpose(1, 0, 2).reshape(3, 512)
    late = _gather_start([w_mkv_loc, w_out_loc, w_up_loc], me, "gather_late_start")
    sink = attn_sink[0]

    h, pa, pq, pkv, pbz, pmq, pmz, pg = _proj_fwd(x2, g_pre + late[4][0:1, 0:1], w_int, tabs)
    ya = _conv_fwd(pa, w_conv_f)
    yb = _attn_fwd(pq, pkv, pbz, sink)
    w_mkv_g, w_out_g, w_up_g = _gather_wait(*late[:4], yb, "gather_late_wait")
    w_mkv = w_mkv_g.reshape(D_MODEL, D_MODEL)
    w_out_f = w_out_g.reshape(D_MODEL, D_MODEL)
    mn, mkv = _mem_kv_fwd(mem2, g_mem, w_mkv)
    ym = _mem_attn_fwd(pmq, pmz, mkv)
    dg, dya, dyb, dym, dy, loss_p, gg_post, mb, dob, du = _mid(ya, yb, ym, pg, x2, tgt2, g_post, w_up_g, w_out_f)
    gw_out, gw_up = _gw_mid(mb, dob, (ya, yb, ym), du)

    core = lax.axis_index("c").astype(jnp.int32).reshape(1)
    chip = (2 * lax.axis_index("x") + lax.axis_index("y")).astype(jnp.int32).reshape(1)

    def exchange_start(shares, tag):
        from_sibling = _sibling_exchange(shares, "grads_to_sibling_" + tag)
        chip_shares = _pair_add(shares, from_sibling, core, "grads_pair_add_" + tag)
        return _chip_exchange_start(chip_shares, "grads_to_chips_start_" + tag)

    dmq, dmz, dmkv = _mem_attn_bwd(pmq, pmz, mkv, dym)
    gw_mkv, gg_mem = _mem_kv_bwd(mem2, g_mem, mn, dmkv, w_mkv)
    shares1 = [gw_mkv.reshape(N_DEV, 128, D_MODEL), gw_out.reshape(N_DEV, 128, D_MODEL), gw_up]
    sib = _split_start(_sibling_copies, N_CHIPS, shares1,
                       [lax.empty((N_CHIPS,) + a.shape[1:], a.dtype) for a in shares1], "grads_to_sibling_small_start")
    da, gw_conv = _conv_bwd(pa, dya, w_conv_f + sib[4][0:1, 0:1])
    shares1, from_sibling = _split_wait(_sibling_copies, N_CHIPS, *sib[:4], da, "grads_to_sibling_small_wait")
    send1, recv1, srcs1, lands1, token1 = _chip_exchange_start(
        _pair_add(shares1, from_sibling, core, "grads_pair_add_small"), "grads_to_chips_start_small")
    dq, dbz, dkv, g_sink = _attn_bwd(pq, pkv, pbz, dyb, sink + token1[0, 0], tabs)
    dparts = (da, dq, dkv, dbz, dmq, dmz, dg)
    gw_int = _gw_in(dparts, h)
    send2, recv2, srcs2, lands2, token2 = exchange_start([gw_int.reshape(N_DEV, SHARD_IN, D_MODEL)], "w_in")
    grad_x, gg_pre = _dh_bwd(dparts, x2, dy, g_pre + token2[0:1, 0:1], w_int)
    (o_mkv, o_out, o_up, o_int), (l_mkv, l_out, l_up, l_int) = _chip_exchange_wait(
        send1 + send2, recv1 + recv2, srcs1 + srcs2, lands1 + lands2, grad_x, "grads_to_chips_wait")

    small = _gather_start([_small_pack((gg_pre, gw_conv, g_sink, gg_mem, gg_post, loss_p))], me,
                          "small_gather_start")

    g_w_in, d_w_in, nm_w_in, nv_w_in = (t.T for t in _sum_adamw(
        o_int, l_int, chip, small[4], 0, w_in[0].T, m_w_in[0].T, v_w_in[0].T, "adamw_w_in", tiles=2))
    (g_mkv, d_mkv, nm_mkv, nv_mkv), (g_out, d_out, nm_out, nv_out), *up = _sum_adamw_group(
        [(o_mkv, l_mkv, 0, w_mem_kv[0], m_w_mem_kv[0], v_w_mem_kv[0]),
         (o_out, l_out, 0, w_out[0], m_w_out[0], v_w_out[0]),
         (o_up, l_up, 0, w_up_a[0], m_w_up_a[0], v_w_up_a[0]),
         (o_up, l_up, 1, w_up_b[0], m_w_up_b[0], v_w_up_b[0]),
         (o_up, l_up, 2, w_up_m[0], m_w_up_m[0], v_w_up_m[0])], chip, small[4], "adamw_mid_weights")

    (packs,) = _gather_wait(*small[:4], g_out[0:8, 0:128] + g_w_in[0:8, 0:128], "small_gather_wait")
    loss_row, small_g, sd, sm, sv = _small_apply(
        packs, [g_pre, w_conv[0], attn_sink, g_mem, g_post],
        [m_g_pre, m_w_conv[0], m_attn_sink, m_g_mem, m_g_post],
        [v_g_pre, v_w_conv[0], v_attn_sink, v_g_mem, v_g_post])
    loss = loss_row[0, 0]
    g_g_pre, g_conv, g_sink_tot, g_g_mem, g_g_post = small_g

    def lead(a):
        return a[None]

    grads = [g_g_pre, lead(g_w_in), lead(g_conv), g_sink_tot, g_g_mem, lead(g_mkv), lead(up[0][0]),
             lead(up[1][0]), lead(up[2][0]), lead(g_out), g_g_post]

    def assemble(small, big_in, big_mkv, big_up, big_out):
        return [small[0], lead(big_in), lead(small[1]), small[2], small[3], lead(big_mkv), lead(big_up[0]),
                lead(big_up[1]), lead(big_up[2]), lead(big_out), small[4]]

    deltas = assemble(sd, d_w_in, d_mkv, [u[1] for u in up], d_out)
    new_m = assemble(sm, nm_w_in, nm_mkv, [u[2] for u in up], nm_out)
    new_v = assemble(sv, nv_w_in, nv_mkv, [u[3] for u in up], nv_out)
    return (loss, grad_x[None], *grads, *deltas, *new_m, *new_v)
```

```python
import functools

import jax
import jax.numpy as jnp
from jax import lax
from jax.experimental import pallas as pl
from jax.experimental.pallas import tpu as pltpu

F32 = jnp.float32
BF16 = jnp.bfloat16
MESH = pl.DeviceIdType.MESH

N_DEV = 8
D_MODEL = 1024
EPS = 1e-6
ROPE_THETA = 500000.0
ROT_DIM = 16
HEAD_DIM = 64
ATTN_BLOCK = 128
MEM_HEADS = 4
MEM_HEAD_DIM = 128
ATTN_SCALE = HEAD_DIM ** -0.5
MEM_SCALE = MEM_HEAD_DIM ** -0.5

ADAM_LR = 0.001
ADAM_B1 = 0.9
ADAM_B2 = 0.999
ADAM_EPS = 1e-08
ADAM_WD = 0.01
ADAM_STEP = 10

SEG_A = (0, 2048)
SEG_BQ = (2048, 512)
SEG_BKV = (2560, 256)
SEG_BZ = (2816, 512)
SEG_MQ = (3328, 512)
SEG_MZ = (3840, 512)
SEG_G = (4352, 3072)
SEGS = (SEG_A, SEG_BQ, SEG_BKV, SEG_BZ, SEG_MQ, SEG_MZ, SEG_G)
IN_WIDTH = 7424
SHARD_IN = IN_WIDTH // N_DEV

V7X_VMEM_BYTES = 64 * 1024 * 1024
ANY = pl.BlockSpec(memory_space=pl.ANY)


def _params(vmem_mb):
    assert vmem_mb * 1024 * 1024 < V7X_VMEM_BYTES
    return pltpu.CompilerParams(dimension_semantics=("arbitrary",), vmem_limit_bytes=vmem_mb * 1024 * 1024)


def _full(shape):
    zeros = (0,) * len(shape)
    return pl.BlockSpec(shape, lambda i: zeros)


def _rows(tm, width):
    return pl.BlockSpec((tm, width), lambda i: (i, 0))


def _dot(a, b):
    return jnp.dot(a, b, preferred_element_type=F32)


def _dot_nt(a, b):
    return lax.dot_general(a, b, (((1,), (1,)), ((), ())), preferred_element_type=F32)


def _dot_tn(a, b):
    return lax.dot_general(a, b, (((0,), (0,)), ((), ())), preferred_element_type=F32)


def _sigmoid(z):
    return 1.0 / (1.0 + jnp.exp(-z))


def _rope(t, cs, s1, s2):
    return t * cs + pltpu.roll(t, 120, 1) * s1 + pltpu.roll(t, 8, 1) * s2


def _rope_t(d, cs, s1, s2):
    return d * cs + pltpu.roll(d * s1, 8, 1) + pltpu.roll(d * s2, 120, 1)


def _gather_side(s, w_mkv, w_out, w_ups):
    half = ROT_DIM // 2
    inv_freq = jnp.power(jnp.float32(ROPE_THETA), -jnp.arange(half, dtype=F32) * (2.0 / ROT_DIM))
    freq_row = jnp.tile(jnp.concatenate([inv_freq, inv_freq, jnp.zeros((HEAD_DIM - ROT_DIM,), F32)]), 2)[None, :]

    def fn(in_refs, out_refs):
        f_ref, mkv_ref, out_ref, *up_refs = in_refs
        t_ref, mkv_bf, out_bf, up_bf = out_refs
        mkv_bf[...] = mkv_ref[...].astype(BF16)
        out_bf[...] = out_ref[...].astype(BF16)
        for k, up_ref in enumerate(up_refs):
            up_bf[512 * k:512 * k + 512, :] = up_ref[...].astype(BF16)
        pos = lax.broadcasted_iota(jnp.int32, (s, 128), 0).astype(F32)
        d = lax.broadcasted_iota(jnp.int32, (s, 128), 1) & (HEAD_DIM - 1)
        ang = pos * f_ref[...]
        cos, sin = jnp.cos(ang), jnp.sin(ang)
        lo, hi = d < half, (d >= half) & (d < ROT_DIM)
        t_ref[0] = jnp.where(lo | hi, cos, 1.0)
        t_ref[1] = jnp.where(lo, -sin, 0.0)
        t_ref[2] = jnp.where(hi, sin, 0.0)

    return ([freq_row, w_mkv, w_out, *w_ups],
            [jax.ShapeDtypeStruct((3, s, 128), F32), jax.ShapeDtypeStruct(w_mkv.shape, BF16),
             jax.ShapeDtypeStruct(w_out.shape, BF16), jax.ShapeDtypeStruct((1536, 128), BF16)], fn)


def _load_once(pairs, sems):
    @pl.when(pl.program_id(0) == 0)
    def _():
        cps = [pltpu.make_async_copy(src, dst, sems.at[k]) for k, (src, dst) in enumerate(pairs)]
        for cp in cps:
            cp.start()
        for cp in cps:
            cp.wait()


def _my_place():
    x, y, c = lax.axis_index("x"), lax.axis_index("y"), lax.axis_index("c")
    return x, y, c


def _all_gather(arrs, name, splits=None, side=None):
    n = len(arrs)
    if splits is None:
        splits = [[(0, a.shape[0])] for a in arrs]
    pieces = [(a, r0, rn) for a in range(n) for r0, rn in splits[a]]
    n_p = len(pieces)
    side_in, side_out, side_fn = side if side is not None else ((), (), None)
    m, q = len(side_in), len(side_out)

    def body(*refs):
        ins, outs = refs[:n], refs[n + m:2 * n + m]
        send_sems, recv_sems, local_sems = refs[2 * n + m + q:]
        x, y, c = _my_place()
        me, sibling = (x, y, c), (x, y, 1 - c)

        def route(core):
            first = (jnp.bitwise_xor(x, 1 - core), jnp.bitwise_xor(y, core), core)
            second = (jnp.bitwise_xor(x, core), jnp.bitwise_xor(y, 1 - core), core)
            return first, second, (1 - x, 1 - y, core)

        def idx(px, py, pc):
            return 4 * px + 2 * py + pc

        def copy(p, k, block, to, own=False):
            a, r0, rn = pieces[p]
            dst = outs[a].at[idx(*block), pl.ds(r0, rn)]
            return pltpu.make_async_remote_copy(
                src_ref=ins[a].at[pl.ds(r0, rn)] if own else dst, dst_ref=dst,
                send_sem=send_sems.at[p * 7 + k], recv_sem=recv_sems.at[p * 7 + k],
                device_id=to, device_id_type=MESH)

        nbr1, nbr2, diag = route(c)
        mine = [pltpu.make_async_copy(ins[a], outs[a].at[idx(*me)], local_sems.at[a]) for a in range(n)]
        for cp in mine:
            cp.start()
        sent = []
        for p in range(n_p):
            for k, to in enumerate((sibling, nbr1, nbr2)):
                sent.append(copy(p, k, me, to, own=True))
        for cp in sent:
            cp.start()
        if side_fn is not None:
            side_fn(refs[n:n + m], refs[2 * n + m:2 * n + m + q])
        for k_in, block, onward in ((1, nbr1, ((3, nbr2), (4, sibling))), (2, nbr2, ((5, sibling),)),
                                    (3, diag, ((6, sibling),))):
            for p in range(n_p):
                copy(p, k_in, block, me).wait_recv()
                for k_out, to in onward:
                    cp = copy(p, k_out, block, to)
                    cp.start()
                    sent.append(cp)
        s1, s2, sd = route(1 - c)
        for k_in, block in ((0, sibling), (4, s1), (5, s2), (6, sd)):
            for p in range(n_p):
                copy(p, k_in, block, me).wait_recv()
        for cp in sent:
            cp.wait_send()
        for cp in mine:
            cp.wait()

    return pl.pallas_call(
        body, name=name,
        out_shape=[jax.ShapeDtypeStruct((N_DEV,) + a.shape, a.dtype) for a in arrs] + list(side_out),
        in_specs=[ANY] * n + [pl.BlockSpec(memory_space=pltpu.VMEM)] * m,
        out_specs=[ANY] * n + [pl.BlockSpec(memory_space=pltpu.VMEM)] * q,
        scratch_shapes=[pltpu.SemaphoreType.DMA((7 * n_p,)), pltpu.SemaphoreType.DMA((7 * n_p,)),
                        pltpu.SemaphoreType.DMA((n,))],
        compiler_params=pltpu.CompilerParams(vmem_limit_bytes=32 * 1024 * 1024),
    )(*arrs, *side_in)


N_CHIPS = 4


def _sibling_exchange(arrs, name):
    n = len(arrs)

    def body(*refs):
        ins, outs = refs[:n], refs[n:2 * n]
        send_sems, recv_sems = refs[2 * n:]
        x, y, c = _my_place()
        sibling = (x, y, 1 - c)

        def copy(a, j):
            return pltpu.make_async_remote_copy(
                src_ref=ins[a].at[2 * j + (1 - c)], dst_ref=outs[a].at[j],
                send_sem=send_sems.at[a * N_CHIPS + j], recv_sem=recv_sems.at[a * N_CHIPS + j],
                device_id=sibling, device_id_type=MESH)

        cps = [copy(a, j) for j in range(N_CHIPS) for a in range(n)]
        for cp in cps:
            cp.start()
        for cp in cps:
            cp.wait_recv()
        for cp in cps:
            cp.wait_send()

    return pl.pallas_call(
        body, name=name,
        out_shape=[jax.ShapeDtypeStruct((N_CHIPS,) + a.shape[1:], a.dtype) for a in arrs],
        in_specs=[ANY] * n, out_specs=[ANY] * n,
        scratch_shapes=[pltpu.SemaphoreType.DMA((N_CHIPS * n,)), pltpu.SemaphoreType.DMA((N_CHIPS * n,))],
    )(*arrs)


def _sibling_copies(srcs, lands, send_sems, recv_sems):
    x, y, c = _my_place()
    cps = []
    for j in range(N_CHIPS):
        for a in range(len(srcs)):
            k = a * N_CHIPS + j
            cps.append(pltpu.make_async_remote_copy(
                src_ref=srcs[a].at[2 * j + (1 - c)], dst_ref=lands[a].at[j], send_sem=send_sems[k],
                recv_sem=recv_sems[k], device_id=(x, y, 1 - c), device_id_type=MESH))
    return cps


def _pair_add(mine, recv, core, name):
    n = len(mine)

    def body(c_ref, *refs):
        for a in range(n):
            refs[2 * n + a][...] = (refs[a][...].astype(F32) + refs[n + a][...].astype(F32)).astype(BF16)

    def blk(a):
        return (None,) + a.shape[1:]

    grid_spec = pltpu.PrefetchScalarGridSpec(
        num_scalar_prefetch=1, grid=(N_CHIPS,),
        in_specs=[pl.BlockSpec(blk(a), lambda j, c_ref: (2 * j + c_ref[0], 0, 0)) for a in mine]
        + [pl.BlockSpec(blk(a), lambda j, c_ref: (j, 0, 0)) for a in recv],
        out_specs=[pl.BlockSpec(blk(a), lambda j, c_ref: (j, 0, 0)) for a in recv])
    return pl.pallas_call(
        body, name=name, grid_spec=grid_spec,
        out_shape=[jax.ShapeDtypeStruct(a.shape, BF16) for a in recv],
        compiler_params=_params(32),
    )(core, *mine, *recv)


HBM = pl.BlockSpec(memory_space=pltpu.HBM)
SEM = pl.BlockSpec(memory_space=pltpu.SEMAPHORE)
N_PEER_CHIPS = 3


def _chip_copies(srcs, lands, send_sems, recv_sems):
    x, y, c = _my_place()
    my_chip = 2 * x + y
    peers = [(x, 1 - y), (1 - x, y), (1 - x, 1 - y)]
    cps = []
    for k, (px, py) in enumerate(peers):
        for a in range(len(srcs)):
            j = a * N_PEER_CHIPS + k
            cps.append(pltpu.make_async_remote_copy(
                src_ref=srcs[a].at[2 * px + py], dst_ref=lands[a].at[my_chip],
                send_sem=send_sems[j], recv_sem=recv_sems[j],
                device_id=(px, py, c), device_id_type=MESH))
    return cps


N_PEERS = N_DEV - 1


def _gather_copies(srcs, lands, send_sems, recv_sems):
    x, y, c = _my_place()
    me_idx = 4 * x + 2 * y + c
    flips = [(0, 0, 1), (0, 1, 0), (1, 0, 0), (0, 1, 1), (1, 0, 1), (1, 1, 0), (1, 1, 1)]
    cps = []
    for k, (fx, fy, fc) in enumerate(flips):
        peer = ((1 - x) if fx else x, (1 - y) if fy else y, (1 - c) if fc else c)
        for a in range(len(srcs)):
            j = a * N_PEERS + k
            cps.append(pltpu.make_async_remote_copy(
                src_ref=srcs[a], dst_ref=lands[a].at[me_idx], send_sem=send_sems[j], recv_sem=recv_sems[j],
                device_id=peer, device_id_type=MESH))
    return cps


def _split_start(copies, per_array, arrs, lands, name):
    arrs, lands = list(arrs), list(lands)
    n = len(arrs)
    k = n * per_array

    def body(*refs):
        srcs, land_refs = refs[:n], refs[n:2 * n]
        send_sems, recv_sems = refs[2 * n:2 * n + k], refs[2 * n + k:2 * n + 2 * k]
        token = refs[-1]
        for cp in copies(srcs, land_refs, send_sems, recv_sems):
            cp.start()
        token[...] = jnp.zeros_like(token)

    hbm_arrs = [pltpu.with_memory_space_constraint(a, pltpu.HBM) for a in arrs]
    lands = [pltpu.with_memory_space_constraint(a, pltpu.HBM) for a in lands]
    res = pl.pallas_call(
        body, name=name,
        out_shape=[pltpu.SemaphoreType.DMA(())] * (2 * k) + [pltpu.HBM(a.shape, a.dtype) for a in arrs + lands]
        + [jax.ShapeDtypeStruct((8, 128), F32)],
        in_specs=[HBM] * (2 * n),
        out_specs=[SEM] * (2 * k) + [HBM] * (2 * n) + [pl.BlockSpec(memory_space=pltpu.VMEM)],
        input_output_aliases={a: 2 * k + a for a in range(2 * n)},
        compiler_params=pltpu.CompilerParams(has_side_effects=pltpu.SideEffectType.DATAFLOW_SIDE_EFFECTING),
    )(*hbm_arrs, *lands)
    return res[:k], res[k:2 * k], res[2 * k:2 * k + n], res[2 * k + n:2 * k + 2 * n], res[-1]


def _split_wait(copies, per_array, send_sems, recv_sems, srcs, lands, after, name):
    n = len(srcs)
    k = n * per_array

    def body(*refs):
        src_refs, land_refs = refs[:n], refs[n:2 * n]
        s_sems, r_sems = refs[2 * n:2 * n + k], refs[2 * n + k:2 * n + 2 * k]
        for cp in copies(src_refs, land_refs, s_sems, r_sems):
            cp.wait_send()
            cp.wait_recv()

    res = pl.pallas_call(
        body, name=name,
        out_shape=[pltpu.HBM(a.shape, a.dtype) for a in list(srcs) + list(lands)],
        in_specs=[HBM] * (2 * n) + [SEM] * (2 * k) + [ANY],
        out_specs=[HBM] * (2 * n),
        input_output_aliases={a: a for a in range(2 * n)},
        compiler_params=pltpu.CompilerParams(has_side_effects=pltpu.SideEffectType.DATAFLOW_SIDE_EFFECTING),
    )(*srcs, *lands, *send_sems, *recv_sems, after)
    return res[:n], res[n:]


def _chip_exchange_start(arrs, name):
    return _split_start(_chip_copies, N_PEER_CHIPS, arrs, [lax.empty(a.shape, a.dtype) for a in arrs], name)


def _chip_exchange_wait(send_sems, recv_sems, srcs, lands, after, name):
    return _split_wait(_chip_copies, N_PEER_CHIPS, send_sems, recv_sems, srcs, lands, after, name)


def _gather_start(arrs, me_idx, name):
    lands = [lax.dynamic_update_slice(lax.empty((N_DEV,) + a.shape, a.dtype), a[None], (me_idx, 0, 0)) for a in arrs]
    return _split_start(_gather_copies, N_PEERS, arrs, lands, name)


def _gather_wait(send_sems, recv_sems, srcs, lands, after, name):
    return _split_wait(_gather_copies, N_PEERS, send_sems, recv_sems, srcs, lands, after, name)[1]


def _proj_fwd(x, g_pre, w_int, tabs):
    s = x.shape[0]
    tm = min(512, s)

    def body(x_ref, g_ref, t_ref, w_hbm,
             h_ref, pa_ref, pq_ref, pkv_ref, pbz_ref, pmq_ref, pmz_ref, pg_ref, w_vm, sems):
        _load_once([(w_hbm, w_vm)], sems)
        xf = x_ref[...]
        r = lax.rsqrt(jnp.mean(xf * xf, axis=-1, keepdims=True) + EPS)
        h = ((xf * r) * g_ref[...]).astype(BF16)
        h_ref[...] = h
        cs, s1, s2 = t_ref[0], t_ref[1], t_ref[2]

        def mm(seg, c0, width):
            return _dot_nt(h, w_vm[seg[0] + c0:seg[0] + c0 + width, :])

        for c0 in range(0, SEG_A[1], 512):
            pa_ref[:, c0:c0 + 512] = mm(SEG_A, c0, 512).astype(BF16)
        q = mm(SEG_BQ, 0, 512)
        for b in range(4):
            pq_ref[:, 128 * b:128 * b + 128] = _rope(q[:, 128 * b:128 * b + 128], cs, s1, s2).astype(BF16)
        kv = mm(SEG_BKV, 0, 256)
        pkv_ref[:, 0:128] = _rope(kv[:, 0:128], cs, s1, s2).astype(BF16)
        pkv_ref[:, 128:256] = kv[:, 128:256].astype(BF16)
        pbz_ref[...] = mm(SEG_BZ, 0, 512).astype(BF16)
        pmq_ref[...] = mm(SEG_MQ, 0, 512).astype(BF16)
        pmz_ref[...] = mm(SEG_MZ, 0, 512).astype(BF16)
        for c0 in range(0, SEG_G[1], 512):
            pg_ref[:, c0:c0 + 512] = mm(SEG_G, c0, 512).astype(BF16)

    widths = (D_MODEL, 2048, 512, 256, 512, 512, 512, 3072)
    return pl.pallas_call(
        body, name="proj_fwd", grid=(s // tm,),
        out_shape=[jax.ShapeDtypeStruct((s, w), BF16) for w in widths],
        in_specs=[_rows(tm, D_MODEL), _full((1, D_MODEL)), pl.BlockSpec((3, tm, 128), lambda i: (0, i, 0)), ANY],
        out_specs=[_rows(tm, w) for w in widths],
        scratch_shapes=[pltpu.VMEM((IN_WIDTH, D_MODEL), BF16), pltpu.SemaphoreType.DMA((1,))],
        compiler_params=_params(52),
    )(x, g_pre, tabs, w_int)


def _mem_kv_fwd(mem, g_mem, w_mkv):
    m = mem.shape[0]

    def body(mem_ref, g_ref, w_ref, mn_ref, mkv_ref):
        xf = mem_ref[...]
        r = lax.rsqrt(jnp.mean(xf * xf, axis=-1, keepdims=True) + EPS)
        mn = ((xf * r) * g_ref[...]).astype(BF16)
        mn_ref[...] = mn
        mkv_ref[...] = _dot(mn, w_ref[...]).astype(BF16)

    return pl.pallas_call(
        body, name="mem_kv_fwd", grid=(1,),
        out_shape=[jax.ShapeDtypeStruct((m, D_MODEL), BF16)] * 2,
        in_specs=[_full((m, D_MODEL)), _full((1, D_MODEL)), _full((D_MODEL, D_MODEL))],
        out_specs=[_full((m, D_MODEL))] * 2,
        compiler_params=_params(32),
    )(mem, g_mem, w_mkv)


def _halo_specs(s, tm, rows, width):
    nblk = s // rows
    prev = pl.BlockSpec((rows, width), lambda i: (jnp.maximum(i * (tm // rows) - 1, 0), 0))
    nxt = pl.BlockSpec((rows, width), lambda i: (jnp.minimum((i + 1) * (tm // rows), nblk - 1), 0))
    return prev, nxt


def _conv_common(pa, cu_prev, cu_next, w, tm):
    b, c, u, z = (pa[:, 512 * k:512 * k + 512] for k in range(4))
    cu = c * u
    row = lax.broadcasted_iota(jnp.int32, (tm, 512), 0)
    cu_m1 = jnp.where(row == 0, cu_prev, pltpu.roll(cu, 1, 0))
    cu_p1 = jnp.where(row == tm - 1, cu_next, pltpu.roll(cu, tm - 1, 0))
    y = cu_m1 * w[0:1] + cu * w[1:2] + cu_p1 * w[2:3]
    sig = _sigmoid(z)
    return b, c, u, z, cu, cu_m1, cu_p1, y, sig, row


def _conv_fwd(pa, w_conv):
    s = pa.shape[0]
    tm = min(512, s)
    nt = s // tm

    def body(pa_ref, pp_ref, pn_ref, w_ref, ya_ref):
        i = pl.program_id(0)
        prev_row = pp_ref[...].astype(F32)[15:16, :]
        next_row = pn_ref[...].astype(F32)[0:1, :]
        b, _, _, z, _, _, _, y, sig, _ = _conv_common(
            pa_ref[...].astype(F32),
            jnp.where(i == 0, 0.0, prev_row[:, 512:1024] * prev_row[:, 1024:1536]),
            jnp.where(i == nt - 1, 0.0, next_row[:, 512:1024] * next_row[:, 1024:1536]), w_ref[...], tm)
        ya_ref[...] = (b * y * (z * sig)).astype(BF16)

    prev, nxt = _halo_specs(s, tm, 16, 2048)
    return pl.pallas_call(
        body, name="conv_fwd", grid=(nt,),
        out_shape=jax.ShapeDtypeStruct((s, 512), BF16),
        in_specs=[_rows(tm, 2048), prev, nxt, _full((3, 512))],
        out_specs=_rows(tm, 512),
        compiler_params=_params(48),
    )(pa, pa, pa, w_conv)


def _heads_to_lanes(a, g, row):
    low = row < HEAD_DIM
    parts = []
    for b in (2 * g, 2 * g + 1):
        t = jnp.transpose(a[:, 128 * b:128 * b + 128])
        swapped = pltpu.roll(t, HEAD_DIM, 0)
        if g == 0:
            parts += [jnp.where(low, t, 0.0), jnp.where(low, swapped, 0.0)]
        else:
            parts += [jnp.where(low, 0.0, swapped), jnp.where(low, 0.0, t)]
    return jnp.concatenate(parts, axis=1)


def _lanes_to_heads(t0, t1, row):
    low = row < HEAD_DIM
    blocks = []
    for b in range(4):
        g = b // 2
        tg = (t0, t1)[g]
        je = 2 * (b - 2 * g)
        even, odd = tg[:, 128 * je:128 * je + 128], tg[:, 128 * je + 128:128 * je + 256]
        if g == 0:
            t = jnp.where(low, even, pltpu.roll(odd, HEAD_DIM, 0))
        else:
            t = jnp.where(low, pltpu.roll(even, HEAD_DIM, 0), odd)
        blocks.append(jnp.transpose(t))
    return jnp.concatenate(blocks, axis=1)


WINDOW_KEYS = 3 * ATTN_BLOCK
STACKED = 4 * ATTN_BLOCK
KEY_CHUNK = 32
MAX_BLOCKS_IN_STEP = 8


def _fill_band_bias(bias, nb):
    assert nb >= 2
    c = lax.broadcasted_iota(jnp.int32, (WINDOW_KEYS, STACKED), 0)
    r = lax.broadcasted_iota(jnp.int32, (WINDOW_KEYS, STACKED), 1) & (ATTN_BLOCK - 1)
    band = (c >= r) & (c <= r + 2 * ATTN_BLOCK)
    for v, ok in enumerate((band, band & (c >= ATTN_BLOCK), band & (c < 2 * ATTN_BLOCK))):
        bias[v] = jnp.where(ok, 0.0, -jnp.inf)


def _bias_variant(n, nb):
    return jnp.where(n == 0, 1, jnp.where(n == nb - 1, 2, 0))


def _sink_row(sink_ref, g):
    return jnp.concatenate([jnp.full((1, ATTN_BLOCK), sink_ref[4 * g + j], F32) for j in range(4)], axis=1)


def _softmax_keys_major(sc, bias, variant, sink, e_scr):
    chunks = [pl.ds(k * KEY_CHUNK, KEY_CHUNK) for k in range(WINDOW_KEYS // KEY_CHUNK)]
    rows = [slice(k * KEY_CHUNK, (k + 1) * KEY_CHUNK) for k in range(WINDOW_KEYS // KEY_CHUNK)]
    m_run = jnp.full((KEY_CHUNK, STACKED), -jnp.inf, F32)
    for ck, rw in zip(chunks, rows):
        m_run = jnp.maximum(m_run, sc[rw] + bias[variant, ck, :])
    m = jnp.maximum(jnp.max(m_run, axis=0, keepdims=True), sink)
    l_run = jnp.zeros((KEY_CHUNK, STACKED), F32)
    for ck, rw in zip(chunks, rows):
        e = jnp.exp(sc[rw] + bias[variant, ck, :] - m)
        l_run += e
        e_scr[rw, :] = e.astype(BF16)
    es = jnp.exp(sink - m)
    inv = 1.0 / (jnp.sum(l_run, axis=0, keepdims=True) + es)
    return inv, es * inv


def _fill_padded(kv_ref, kpad, vpad, s):
    zero = jnp.zeros((ATTN_BLOCK, 128), BF16)
    kpad[0:ATTN_BLOCK, :] = zero
    vpad[0:ATTN_BLOCK, :] = zero
    kpad[ATTN_BLOCK + s:2 * ATTN_BLOCK + s, :] = zero
    vpad[ATTN_BLOCK + s:2 * ATTN_BLOCK + s, :] = zero
    kpad[ATTN_BLOCK:ATTN_BLOCK + s, :] = kv_ref[:, 0:128]
    vpad[ATTN_BLOCK:ATTN_BLOCK + s, :] = kv_ref[:, 128:256]


def _attn_fwd(pq, pkv, pbz, sink):
    s = pq.shape[0]
    nb = s // ATTN_BLOCK
    bps = min(MAX_BLOCKS_IN_STEP, nb)

    def body(sink_ref, q_ref, z_ref, kv_ref, yb_ref, kpad, vpad, bias, e_scr):
        i = pl.program_id(0)

        @pl.when(i == 0)
        def _():
            _fill_padded(kv_ref, kpad, vpad, s)
            _fill_band_bias(bias, nb)

        row = lax.broadcasted_iota(jnp.int32, (ATTN_BLOCK, 128), 0)
        for b in range(bps):
            n = i * bps + b
            rows = slice(b * ATTN_BLOCK, (b + 1) * ATTN_BLOCK)
            start = pl.multiple_of(n * ATTN_BLOCK, ATTN_BLOCK)
            kw, vw = kpad[pl.ds(start, WINDOW_KEYS), :], vpad[pl.ds(start, WINDOW_KEYS), :]
            qf = q_ref[rows, :].astype(F32)
            variant = _bias_variant(n, nb)
            outs = []
            for g in range(2):
                e_bg = e_scr.at[2 * b + g]
                qt = (_heads_to_lanes(qf, g, row) * ATTN_SCALE).astype(BF16)
                inv, _ = _softmax_keys_major(_dot(kw, qt), bias, variant, _sink_row(sink_ref, g), e_bg)
                outs.append(_dot_tn(vw, e_bg[...]) * inv)
            attn = _lanes_to_heads(outs[0], outs[1], row)
            z = z_ref[rows, :].astype(F32)
            yb_ref[rows, :] = (attn * (z * _sigmoid(z))).astype(BF16)

    tq = bps * ATTN_BLOCK
    return pl.pallas_call(
        body, name="attn_fwd", grid=(s // tq,),
        out_shape=jax.ShapeDtypeStruct((s, 512), BF16),
        in_specs=[pl.BlockSpec(memory_space=pltpu.SMEM), _rows(tq, 512), _rows(tq, 512), _full((s, 256))],
        out_specs=_rows(tq, 512),
        scratch_shapes=[pltpu.VMEM((s + 2 * ATTN_BLOCK, 128), BF16)] * 2
        + [pltpu.VMEM((3, WINDOW_KEYS, STACKED), F32),
           pltpu.VMEM((2 * bps, WINDOW_KEYS, STACKED), BF16)],
        compiler_params=_params(32),
    )(sink, pq, pbz, pkv)


def _mem_softmax_t(q, mk):
    sc = _dot_nt(mk, q) * MEM_SCALE
    e = jnp.exp(sc - jnp.max(sc, axis=0, keepdims=True))
    return e * (1.0 / jnp.sum(e, axis=0, keepdims=True))


def _mem_attn_fwd(pmq, pmz, mkv):
    s = pmq.shape[0]
    m = mkv.shape[0]
    tm = min(512, s)

    def body(q_ref, z_ref, mk_ref, mv_ref, ym_ref):
        z = z_ref[...].astype(F32)
        sz = z * _sigmoid(z)
        for h in range(MEM_HEADS):
            cols = slice(128 * h, 128 * h + 128)
            pt = _mem_softmax_t(q_ref[:, cols], mk_ref[:, cols])
            o = _dot_tn(pt.astype(BF16), mv_ref[:, cols])
            ym_ref[:, cols] = (o * sz[:, cols]).astype(BF16)

    return pl.pallas_call(
        body, name="mem_attn_fwd", grid=(s // tm,),
        out_shape=jax.ShapeDtypeStruct((s, 512), BF16),
        in_specs=[_rows(tm, 512), _rows(tm, 512), pl.BlockSpec((m, 512), lambda i: (0, 0)),
                  pl.BlockSpec((m, 512), lambda i: (0, 1))],
        out_specs=_rows(tm, 512),
        compiler_params=_params(32),
    )(pmq, pmz, mkv, mkv)


def _mid(ya, yb, ym, pg, x, target, g_post, w_up, w_out):
    s = x.shape[0]
    tm = min(256, s)
    nt = s // tm

    def body(ya_ref, yb_ref, ym_ref, pg_ref, x_ref, t_ref, gp_ref, wup_hbm, wout_hbm,
             dg_ref, dya_ref, dyb_ref, dym_ref, dy_ref, loss_ref, ggp_ref, mb_ref, dob_ref, du_ref,
             wup_vm, wout_vm, sems):
        i = pl.program_id(0)
        _load_once([(wup_hbm.at[d], wup_vm.at[:, pl.ds(128 * d, 128)]) for d in range(N_DEV)]
                   + [(wout_hbm, wout_vm)], sems)

        @pl.when(i == 0)
        def _():
            loss_ref[...] = jnp.zeros_like(loss_ref)
            ggp_ref[...] = jnp.zeros_like(ggp_ref)

        ys = (ya_ref[...], yb_ref[...], ym_ref[...])
        us = [_dot(ys[k], wup_vm[512 * k:512 * k + 512, :]) for k in range(3)]
        gates = [_sigmoid(pg_ref[:, 1024 * k:1024 * k + 1024].astype(F32)) for k in range(3)]
        merged = gates[0] * us[0] + gates[1] * us[1] + gates[2] * us[2]
        mb = merged.astype(BF16)
        mb_ref[...] = mb
        out = _dot(mb, wout_vm[...])
        r = lax.rsqrt(jnp.mean(out * out, axis=-1, keepdims=True) + EPS)
        on = out * r
        gp = gp_ref[...]
        err = (x_ref[...] + on * gp) - t_ref[...]
        loss_ref[...] += 0.5 * jnp.sum(err * err) * (1.0 / D_MODEL)
        dy = err * (1.0 / D_MODEL)
        dy_ref[...] = dy
        ggp_ref[...] += jnp.sum(dy * on, axis=0, keepdims=True)
        a = dy * gp
        d_out = r * (a - on * jnp.mean(a * on, axis=-1, keepdims=True))
        dob = d_out.astype(BF16)
        dob_ref[...] = dob
        d_merged = _dot_nt(dob, wout_vm[...])
        d_refs = (dya_ref, dyb_ref, dym_ref)
        for k in range(3):
            g = gates[k]
            du_f = d_merged * g
            dg_ref[:, 1024 * k:1024 * k + 1024] = (du_f * us[k] * (1.0 - g)).astype(BF16)
            du = du_f.astype(BF16)
            du_ref[k] = du
            d_refs[k][...] = _dot_nt(du, wup_vm[512 * k:512 * k + 512, :]).astype(BF16)

    return pl.pallas_call(
        body, name="mid", grid=(nt,),
        out_shape=[jax.ShapeDtypeStruct((s, 3072), BF16)] + [jax.ShapeDtypeStruct((s, 512), BF16)] * 3
        + [jax.ShapeDtypeStruct((s, D_MODEL), F32), jax.ShapeDtypeStruct((8, 128), F32),
           jax.ShapeDtypeStruct((1, D_MODEL), F32), jax.ShapeDtypeStruct((s, D_MODEL), BF16),
           jax.ShapeDtypeStruct((s, D_MODEL), BF16), jax.ShapeDtypeStruct((3, s, D_MODEL), BF16)],
        in_specs=[_rows(tm, 512)] * 3 + [_rows(tm, 3072), _rows(tm, D_MODEL), _rows(tm, D_MODEL),
                                         _full((1, D_MODEL)), ANY, ANY],
        out_specs=[_rows(tm, 3072)] + [_rows(tm, 512)] * 3
        + [_rows(tm, D_MODEL), _full((8, 128)), _full((1, D_MODEL)), _rows(tm, D_MODEL), _rows(tm, D_MODEL),
           pl.BlockSpec((3, tm, D_MODEL), lambda i: (0, i, 0))],
        scratch_shapes=[pltpu.VMEM((1536, D_MODEL), BF16), pltpu.VMEM((D_MODEL, D_MODEL), BF16),
                        pltpu.SemaphoreType.DMA((N_DEV + 1,))],
        compiler_params=_params(56),
    )(ya, yb, ym, pg, x, target, g_post, w_up, w_out)


def _gw_mid(mb, dob, ys, du):
    s = mb.shape[0]
    tn = 256

    def out_body(mb_ref, dob_ref, o_ref):
        o_ref[...] = _dot_tn(mb_ref[...], dob_ref[...]).astype(BF16)

    gw_out = pl.pallas_call(
        out_body, name="gw_out", grid=(D_MODEL // tn,),
        out_shape=jax.ShapeDtypeStruct((D_MODEL, D_MODEL), BF16),
        in_specs=[pl.BlockSpec((s, tn), lambda j: (0, j)), _full((s, D_MODEL))],
        out_specs=pl.BlockSpec((tn, D_MODEL), lambda j: (j, 0)),
        compiler_params=_params(48),
    )(mb, dob)

    per = 512 // tn

    def up_body(ya_ref, yb_ref, ym_ref, du_ref, o_ref):
        j = pl.program_id(0)
        for k, y_ref in enumerate((ya_ref, yb_ref, ym_ref)):
            @pl.when(j // per == k)
            def _(y_ref=y_ref):
                res = _dot_tn(y_ref[...], du_ref[...])
                for d in range(N_DEV):
                    o_ref[d] = res[:, 128 * d:128 * d + 128].astype(BF16)

    def y_spec(k):
        return pl.BlockSpec((s, tn), lambda j: (0, jnp.clip(j - per * k, 0, per - 1)))

    gw_up = pl.pallas_call(
        up_body, name="gw_up", grid=(3 * per,),
        out_shape=jax.ShapeDtypeStruct((N_DEV, 1536, 128), BF16),
        in_specs=[y_spec(0), y_spec(1), y_spec(2), pl.BlockSpec((None, s, D_MODEL), lambda j: (j // per, 0, 0))],
        out_specs=pl.BlockSpec((N_DEV, tn, 128), lambda j: (0, j, 0)),
        compiler_params=_params(48),
    )(*ys, du)
    return gw_out, gw_up


def _conv_bwd(pa, dya, w_conv):
    s = pa.shape[0]
    tm = min(512, s)
    nt = s // tm

    def body(pa_ref, pp_ref, pn_ref, d_ref, dp_ref, dn_ref, w_ref, da_ref, gw_ref):
        i = pl.program_id(0)
        first, last = i == 0, i == nt - 1

        @pl.when(first)
        def _():
            gw_ref[...] = jnp.zeros_like(gw_ref)

        w = w_ref[...]
        prev_row = pp_ref[...].astype(F32)[15:16, :]
        next_row = pn_ref[...].astype(F32)[0:1, :]
        b, c, u, z, cu, cu_m1, cu_p1, y, sig, row = _conv_common(
            pa_ref[...].astype(F32),
            jnp.where(first, 0.0, prev_row[:, 512:1024] * prev_row[:, 1024:1536]),
            jnp.where(last, 0.0, next_row[:, 512:1024] * next_row[:, 1024:1536]), w, tm)
        sz = z * sig
        dya_t = d_ref[...].astype(F32)
        d_y = dya_t * b * sz

        def halo_dy(p_row, d_row):
            zz = p_row[:, 1536:2048]
            return d_row * p_row[:, 0:512] * (zz * _sigmoid(zz))

        dy_prev = jnp.where(first, 0.0, halo_dy(prev_row, dp_ref[...].astype(F32)[15:16, :]))
        dy_next = jnp.where(last, 0.0, halo_dy(next_row, dn_ref[...].astype(F32)[0:1, :]))
        dy_m1 = jnp.where(row == 0, dy_prev, pltpu.roll(d_y, 1, 0))
        dy_p1 = jnp.where(row == tm - 1, dy_next, pltpu.roll(d_y, tm - 1, 0))
        d_cu = dy_p1 * w[0:1] + d_y * w[1:2] + dy_m1 * w[2:3]
        da_ref[:, 0:512] = (dya_t * y * sz).astype(BF16)
        da_ref[:, 512:1024] = (d_cu * u).astype(BF16)
        da_ref[:, 1024:1536] = (d_cu * c).astype(BF16)
        da_ref[:, 1536:2048] = (dya_t * b * y * (sig + sz * (1.0 - sig))).astype(BF16)
        gw_ref[0:1, :] += jnp.sum(d_y * cu_m1, axis=0, keepdims=True)
        gw_ref[1:2, :] += jnp.sum(d_y * cu, axis=0, keepdims=True)
        gw_ref[2:3, :] += jnp.sum(d_y * cu_p1, axis=0, keepdims=True)

    prev, nxt = _halo_specs(s, tm, 16, 2048)
    dprev, dnxt = _halo_specs(s, tm, 16, 512)
    return pl.pallas_call(
        body, name="conv_bwd", grid=(nt,),
        out_shape=[jax.ShapeDtypeStruct((s, 2048), BF16), jax.ShapeDtypeStruct((8, 512), F32)],
        in_specs=[_rows(tm, 2048), prev, nxt, _rows(tm, 512), dprev, dnxt, _full((3, 512))],
        out_specs=[_rows(tm, 2048), _full((8, 512))],
        compiler_params=_params(48),
    )(pa, pa, pa, dya, dya, dya, w_conv)


def _attn_bwd(pq, pkv, pbz, dyb, sink, tabs):
    s = pq.shape[0]
    nb = s // ATTN_BLOCK
    bps = min(MAX_BLOCKS_IN_STEP, nb)

    def body(sink_ref, q_ref, z_ref, d_ref, kv_ref, t_ref,
             dq_ref, dz_ref, dkv_ref, gs_ref, kpad, vpad, dk_acc, dv_acc, bias, e_scr, ds_scr):
        i = pl.program_id(0)

        @pl.when(i == 0)
        def _():
            _fill_padded(kv_ref, kpad, vpad, s)
            _fill_band_bias(bias, nb)
            dk_acc[...] = jnp.zeros_like(dk_acc)
            dv_acc[...] = jnp.zeros_like(dv_acc)
            gs_ref[...] = jnp.zeros_like(gs_ref)

        row = lax.broadcasted_iota(jnp.int32, (ATTN_BLOCK, 128), 0)
        for b in range(bps):
            n = i * bps + b
            rows = slice(b * ATTN_BLOCK, (b + 1) * ATTN_BLOCK)
            start = pl.multiple_of(n * ATTN_BLOCK, ATTN_BLOCK)
            kw, vw = kpad[pl.ds(start, WINDOW_KEYS), :], vpad[pl.ds(start, WINDOW_KEYS), :]
            qf = q_ref[rows, :].astype(F32)
            variant = _bias_variant(n, nb)
            z = z_ref[rows, :].astype(F32)
            sig = _sigmoid(z)
            dyb_t = d_ref[rows, :].astype(F32)
            d_attn = dyb_t * (z * sig)
            outs, dqs = [], []
            dk_w = jnp.zeros((WINDOW_KEYS, 128), F32)
            dv_w = jnp.zeros((WINDOW_KEYS, 128), F32)
            for g in range(2):
                e_bg, ds_bg = e_scr.at[2 * b + g], ds_scr.at[2 * b + g]
                qt = _heads_to_lanes(qf, g, row)
                inv, p_sink = _softmax_keys_major(
                    _dot(kw, (qt * ATTN_SCALE).astype(BF16)), bias, variant, _sink_row(sink_ref, g), e_bg)
                ot = _dot_tn(vw, e_bg[...]) * inv
                outs.append(ot)
                dot_ = _heads_to_lanes(d_attn, g, row)
                delta = jnp.sum(dot_ * ot, axis=0, keepdims=True)
                dpt = _dot(vw, dot_.astype(BF16))
                for k in range(WINDOW_KEYS // KEY_CHUNK):
                    rw = slice(k * KEY_CHUNK, (k + 1) * KEY_CHUNK)
                    ds_bg[rw, :] = (e_bg[rw, :].astype(F32) * (dpt[rw] - delta)).astype(BF16)
                sink_part = p_sink * delta
                for j in range(4):
                    h = 4 * g + j
                    gs_ref[h:h + 1, :] -= jnp.sum(sink_part[:, 128 * j:128 * j + 128])
                dqs.append(_dot_tn(kw, ds_bg[...]) * (inv * ATTN_SCALE))
                dk_w += _dot_nt(ds_bg[...], (qt * inv).astype(BF16)) * ATTN_SCALE
                dv_w += _dot_nt(e_bg[...], (dot_ * inv).astype(BF16))
            dk_acc[pl.ds(start, WINDOW_KEYS), :] += dk_w
            dv_acc[pl.ds(start, WINDOW_KEYS), :] += dv_w
            attn = _lanes_to_heads(outs[0], outs[1], row)
            dz_ref[rows, :] = (dyb_t * attn * (sig * (1.0 + z * (1.0 - sig)))).astype(BF16)
            dq = _lanes_to_heads(dqs[0], dqs[1], row)
            trows = pl.ds(start, ATTN_BLOCK)
            cs, s1, s2 = t_ref[0, trows, :], t_ref[1, trows, :], t_ref[2, trows, :]
            for blk in range(4):
                cols = slice(128 * blk, 128 * blk + 128)
                dq_ref[rows, cols] = _rope_t(dq[:, cols], cs, s1, s2).astype(BF16)

        @pl.when(i == nb // bps - 1)
        def _():
            dk = dk_acc[ATTN_BLOCK:ATTN_BLOCK + s, :]
            dkv_ref[:, 0:128] = _rope_t(dk, t_ref[0], t_ref[1], t_ref[2]).astype(BF16)
            dkv_ref[:, 128:256] = dv_acc[ATTN_BLOCK:ATTN_BLOCK + s, :].astype(BF16)

    tq = bps * ATTN_BLOCK
    tile = _rows(tq, 512)
    return pl.pallas_call(
        body, name="attn_bwd", grid=(s // tq,),
        out_shape=[jax.ShapeDtypeStruct((s, 512), BF16), jax.ShapeDtypeStruct((s, 512), BF16),
                   jax.ShapeDtypeStruct((s, 256), BF16), jax.ShapeDtypeStruct((8, 128), F32)],
        in_specs=[pl.BlockSpec(memory_space=pltpu.SMEM), tile, tile, tile, _full((s, 256)), _full((3, s, 128))],
        out_specs=[tile, tile, _full((s, 256)), _full((8, 128))],
        scratch_shapes=[pltpu.VMEM((s + 2 * ATTN_BLOCK, 128), BF16)] * 2
        + [pltpu.VMEM((s + 2 * ATTN_BLOCK, 128), F32)] * 2
        + [pltpu.VMEM((3, WINDOW_KEYS, STACKED), F32)]
        + [pltpu.VMEM((2 * bps, WINDOW_KEYS, STACKED), BF16)] * 2,
        compiler_params=_params(48),
    )(sink, pq, pbz, dyb, pkv, tabs)


def _mem_attn_bwd(pmq, pmz, mkv, dym):
    s = pmq.shape[0]
    m = mkv.shape[0]
    tm = min(512, s)

    def body(q_ref, z_ref, d_ref, mk_ref, mv_ref, dq_ref, dz_ref, dmkv_ref):
        @pl.when(pl.program_id(0) == 0)
        def _():
            dmkv_ref[...] = jnp.zeros_like(dmkv_ref)

        z = z_ref[...].astype(F32)
        sig = _sigmoid(z)
        dym_t = d_ref[...].astype(F32)
        d_attn = dym_t * (z * sig)
        dsilu = sig * (1.0 + z * (1.0 - sig))
        for h in range(MEM_HEADS):
            cols = slice(128 * h, 128 * h + 128)
            q, mk, mv = q_ref[:, cols], mk_ref[:, cols], mv_ref[:, cols]
            pt = _mem_softmax_t(q, mk)
            pb = pt.astype(BF16)
            o = _dot_tn(pb, mv)
            dob = d_attn[:, cols].astype(BF16)
            dpt = _dot_nt(mv, dob)
            dst = (pt * (dpt - jnp.sum(pt * dpt, axis=0, keepdims=True))).astype(BF16)
            dq_ref[:, cols] = (_dot_tn(dst, mk) * MEM_SCALE).astype(BF16)
            dz_ref[:, cols] = (dym_t[:, cols] * o * dsilu[:, cols]).astype(BF16)
            dmkv_ref[:, cols] += _dot(dst, q) * MEM_SCALE
            dmkv_ref[:, 512 + 128 * h:512 + 128 * h + 128] += _dot(pb, dob)

    return pl.pallas_call(
        body, name="mem_attn_bwd", grid=(s // tm,),
        out_shape=[jax.ShapeDtypeStruct((s, 512), BF16), jax.ShapeDtypeStruct((s, 512), BF16),
                   jax.ShapeDtypeStruct((m, D_MODEL), F32)],
        in_specs=[_rows(tm, 512), _rows(tm, 512), _rows(tm, 512), pl.BlockSpec((m, 512), lambda i: (0, 0)),
                  pl.BlockSpec((m, 512), lambda i: (0, 1))],
        out_specs=[_rows(tm, 512), _rows(tm, 512), _full((m, D_MODEL))],
        compiler_params=_params(32),
    )(pmq, pmz, dym, mkv, mkv)


def _mem_kv_bwd(mem, g_mem, mn, dmkv, w_mkv):
    m = mem.shape[0]

    def body(mem_ref, g_ref, mn_ref, d_ref, w_ref, gw_ref, gg_ref):
        db = d_ref[...].astype(BF16)
        gw_ref[...] = _dot_tn(mn_ref[...], db).astype(BF16)
        d_mn = _dot_nt(db, w_ref[...])
        xf = mem_ref[...]
        r = lax.rsqrt(jnp.mean(xf * xf, axis=-1, keepdims=True) + EPS)
        gg_ref[...] = jnp.sum(d_mn * (xf * r), axis=0, keepdims=True)

    return pl.pallas_call(
        body, name="mem_kv_bwd", grid=(1,),
        out_shape=[jax.ShapeDtypeStruct((D_MODEL, D_MODEL), BF16), jax.ShapeDtypeStruct((1, D_MODEL), F32)],
        in_specs=[_full((m, D_MODEL)), _full((1, D_MODEL)), _full((m, D_MODEL)), _full((m, D_MODEL)),
                  _full((D_MODEL, D_MODEL))],
        out_specs=[_full((D_MODEL, D_MODEL)), _full((1, D_MODEL))],
        compiler_params=_params(32),
    )(mem, g_mem, mn, dmkv, w_mkv)


def _dh_bwd(dparts, x, dy, g_pre, w_int):
    s = x.shape[0]
    tm = min(256, s)

    def body(*refs):
        d_refs = refs[:7]
        x_ref, dy_ref, g_ref, w_hbm, gx_ref, gg_ref, w_vm, sems = refs[7:]
        _load_once([(w_hbm, w_vm)], sems)

        @pl.when(pl.program_id(0) == 0)
        def _():
            gg_ref[...] = jnp.zeros_like(gg_ref)

        d_h = jnp.zeros((tm, D_MODEL), F32)
        for d_ref, (r0, width) in zip(d_refs, SEGS):
            for c0 in range(0, width, 512):
                cw = min(512, width - c0)
                d_h += _dot(d_ref[:, c0:c0 + cw], w_vm[r0 + c0:r0 + c0 + cw, :])
        xf = x_ref[...]
        r = lax.rsqrt(jnp.mean(xf * xf, axis=-1, keepdims=True) + EPS)
        xn = xf * r
        a = d_h * g_ref[...]
        gx_ref[...] = r * (a - xn * jnp.mean(a * xn, axis=-1, keepdims=True)) + dy_ref[...]
        gg_ref[...] += jnp.sum(d_h * xn, axis=0, keepdims=True)

    return pl.pallas_call(
        body, name="dh_bwd", grid=(s // tm,),
        out_shape=[jax.ShapeDtypeStruct((s, D_MODEL), F32), jax.ShapeDtypeStruct((1, D_MODEL), F32)],
        in_specs=[_rows(tm, w) for _, w in SEGS] + [_rows(tm, D_MODEL), _rows(tm, D_MODEL), _full((1, D_MODEL)), ANY],
        out_specs=[_rows(tm, D_MODEL), _full((1, D_MODEL))],
        scratch_shapes=[pltpu.VMEM((IN_WIDTH, D_MODEL), BF16), pltpu.SemaphoreType.DMA((1,))],
        compiler_params=_params(52),
    )(*dparts, x, dy, g_pre, w_int)


def _gw_in(dparts, h):
    s = h.shape[0]
    tn = 256
    starts, counts = [], []
    for r0, width in SEGS:
        starts.append(r0 // tn)
        counts.append(width // tn)

    def body(*refs):
        d_refs = refs[:7]
        h_hbm, o_ref, h_vm, sems = refs[7:]
        _load_once([(h_hbm, h_vm)], sems)
        j = pl.program_id(0)
        for d_ref, st, cnt in zip(d_refs, starts, counts):
            @pl.when((j >= st) & (j < st + cnt))
            def _(d_ref=d_ref):
                o_ref[...] = _dot_tn(d_ref[...], h_vm[...]).astype(BF16)

    def seg_spec(st, cnt):
        return pl.BlockSpec((s, tn), lambda j: (0, jnp.clip(j - st, 0, cnt - 1)))

    return pl.pallas_call(
        body, name="gw_in", grid=(IN_WIDTH // tn,),
        out_shape=jax.ShapeDtypeStruct((IN_WIDTH, D_MODEL), BF16),
        in_specs=[seg_spec(st, cnt) for st, cnt in zip(starts, counts)] + [ANY],
        out_specs=pl.BlockSpec((tn, D_MODEL), lambda j: (j, 0)),
        scratch_shapes=[pltpu.VMEM((s, D_MODEL), BF16), pltpu.SemaphoreType.DMA((1,))],
        compiler_params=_params(52),
    )(*dparts, h)


def _adamw_math(w, g, m, v):
    m2 = ADAM_B1 * m + (1.0 - ADAM_B1) * g
    v2 = ADAM_B2 * v + (1.0 - ADAM_B2) * (g * g)
    m_hat = m2 / (1.0 - ADAM_B1 ** ADAM_STEP)
    v_hat = v2 / (1.0 - ADAM_B2 ** ADAM_STEP)
    delta = -ADAM_LR * (m_hat / (jnp.sqrt(v_hat) + ADAM_EPS) + ADAM_WD * w)
    return delta, m2, v2


def _sum_adamw(own, land, chip, block, w, m, v, name, tiles=1):
    r, c = w.shape
    rt = r // tiles

    def body(c_ref, own_ref, l1_ref, l2_ref, l3_ref, w_ref, m_ref, v_ref, g_ref, d_ref, m2_ref, v2_ref):
        g = own_ref[...].astype(F32)
        for l_ref in (l1_ref, l2_ref, l3_ref):
            g += l_ref[...].astype(F32)
        g_ref[...] = g
        d_ref[...], m2_ref[...], v2_ref[...] = _adamw_math(w_ref[...], g, m_ref[...], v_ref[...])

    def share(k):
        return pl.BlockSpec((None, rt, c), lambda i, c_ref: (jnp.bitwise_xor(c_ref[0], k), block * tiles + i, 0))

    spec = pl.BlockSpec((rt, c), lambda i, c_ref: (i, 0))
    grid_spec = pltpu.PrefetchScalarGridSpec(
        num_scalar_prefetch=1, grid=(tiles,),
        in_specs=[share(0), share(1), share(2), share(3)] + [spec] * 3, out_specs=[spec] * 4)
    return pl.pallas_call(
        body, name=name, grid_spec=grid_spec,
        out_shape=[jax.ShapeDtypeStruct((r, c), F32)] * 4,
        compiler_params=_params(48),
    )(chip, own, land, land, land, w, m, v)


def _sum_adamw_group(items, chip, name):
    k = len(items)

    def body(c_ref, *refs):
        shares, wmv, outs = refs[:4 * k], refs[4 * k:7 * k], refs[7 * k:]
        for j in range(k):
            g = shares[4 * j][...].astype(F32)
            for l_ref in shares[4 * j + 1:4 * j + 4]:
                g += l_ref[...].astype(F32)
            outs[4 * j][...] = g
            outs[4 * j + 1][...], outs[4 * j + 2][...], outs[4 * j + 3][...] = _adamw_math(
                wmv[3 * j][...], g, wmv[3 * j + 1][...], wmv[3 * j + 2][...])

    def share(shape, block, q):
        return pl.BlockSpec((None,) + shape, lambda i, c_ref: (jnp.bitwise_xor(c_ref[0], q), block, 0))

    in_specs, args = [], []
    for own, land, block, w, m, v in items:
        in_specs += [share(w.shape, block, q) for q in range(4)]
        args += [own, land, land, land]
    for own, land, block, w, m, v in items:
        in_specs += [pl.BlockSpec(w.shape, lambda i, c_ref: (0, 0))] * 3
        args += [w, m, v]
    out_specs = [pl.BlockSpec(w.shape, lambda i, c_ref: (0, 0)) for _, _, _, w, _, _ in items for _ in range(4)]
    res = pl.pallas_call(
        body, name=name,
        grid_spec=pltpu.PrefetchScalarGridSpec(num_scalar_prefetch=1, grid=(1,), in_specs=in_specs,
                                               out_specs=out_specs),
        out_shape=[jax.ShapeDtypeStruct(w.shape, F32) for _, _, _, w, _, _ in items for _ in range(4)],
        compiler_params=_params(48),
    )(chip, *args)
    return [res[4 * j:4 * j + 4] for j in range(k)]


def _small_step(parts, ws, ms, vs):
    def exchange(gpre_ref, gconv_ref, gsink_ref, gmem_ref, gpost_ref, loss_ref, tot_ref,
                 pack, gathered, send_sems, recv_sems):
        x, y, c = _my_place()
        me_idx = 4 * x + 2 * y + c

        lane = lax.broadcasted_iota(jnp.int32, (1, 128), 1)
        sink_row = jnp.zeros((1, 128), F32)
        for h in range(8):
            sink_row = jnp.where(lane == h, gsink_ref[h:h + 1, :], sink_row)
        pack[...] = jnp.zeros_like(pack)
        pack[0:1, :] = gpre_ref[...]
        pack[1:2, :] = gmem_ref[...]
        pack[2:3, :] = gpost_ref[...]
        pack[3:6, 0:512] = gconv_ref[0:3, :]
        pack[6:7, 0:128] = sink_row
        pack[7:8, 0:128] = loss_ref[0:1, :]

        flips = [(0, 0, 1), (0, 1, 0), (1, 0, 0), (0, 1, 1), (1, 0, 1), (1, 1, 0), (1, 1, 1)]
        cps = []
        for k, (fx, fy, fc) in enumerate(flips):
            peer = ((1 - x) if fx else x, (1 - y) if fy else y, (1 - c) if fc else c)
            cps.append(pltpu.make_async_remote_copy(
                src_ref=pack, dst_ref=gathered.at[me_idx], send_sem=send_sems.at[k], recv_sem=recv_sems.at[k],
                device_id=peer, device_id_type=MESH))
        for cp in cps:
            cp.start()
        gathered[me_idx] = pack[...]
        for cp in cps:
            cp.wait_recv()
        for cp in cps:
            cp.wait_send()
        tot = gathered[0]
        for d in range(1, N_DEV):
            tot = tot + gathered[d]
        tot_ref[...] = tot

    tot = pl.pallas_call(
        exchange, name="small_exchange", grid=(1,),
        out_shape=jax.ShapeDtypeStruct((8, D_MODEL), F32),
        in_specs=[_full(p.shape) for p in parts], out_specs=_full((8, D_MODEL)),
        scratch_shapes=[pltpu.VMEM((8, D_MODEL), F32), pltpu.VMEM((N_DEV, 8, D_MODEL), F32),
                        pltpu.SemaphoreType.DMA((N_PEERS,)), pltpu.SemaphoreType.DMA((N_PEERS,))],
    )(*parts)

    def apply(tot_ref, *refs):
        w_refs, m_refs, v_refs = refs[0:5], refs[5:10], refs[10:15]
        loss_out = refs[15]
        g_outs, d_outs, m_outs, v_outs = refs[16:21], refs[21:26], refs[26:31], refs[31:36]
        x, y, c = _my_place()
        tot = tot_ref[...]
        conv = pltpu.roll(tot[:, 0:512], (512 - 64 * (4 * x + 2 * y + c)) % 512, 1)[3:6, 0:64]
        grads = (tot[0:1, :], conv, tot[6:7, 0:8], tot[1:2, :], tot[2:3, :])
        loss_out[...] = tot[7:8, 0:128]
        for j in range(5):
            g_outs[j][...] = grads[j]
            d_outs[j][...], m_outs[j][...], v_outs[j][...] = _adamw_math(
                w_refs[j][...], grads[j], m_refs[j][...], v_refs[j][...])

    specs = [_full(w.shape) for w in ws]
    res = pl.pallas_call(
        apply, name="small_apply", grid=(1,),
        out_shape=[jax.ShapeDtypeStruct((1, 128), F32)] + [jax.ShapeDtypeStruct(w.shape, F32) for w in ws] * 4,
        in_specs=[_full((8, D_MODEL))] + specs * 3,
        out_specs=[_full((1, 128))] + specs * 4,
    )(tot, *ws, *ms, *vs)
    return res[0], res[1:6], res[6:11], res[11:16], res[16:21]


def kernel(x, mem, g_pre, w_in, w_conv, attn_sink, g_mem, w_mem_kv, w_up_a, w_up_b, w_up_m, w_out, g_post, loss_target, m_g_pre, m_w_in, m_w_conv, m_attn_sink, m_g_mem, m_w_mem_kv, m_w_up_a, m_w_up_b, m_w_up_m, m_w_out, m_g_post, v_g_pre, v_w_in, v_w_conv, v_attn_sink, v_g_mem, v_w_mem_kv, v_w_up_a, v_w_up_b, v_w_up_m, v_w_out, v_g_post):
    s = x.shape[1]
    x2, mem2, tgt2 = x[0], mem[0], loss_target[0]
    me = 4 * lax.axis_index("x") + 2 * lax.axis_index("y") + lax.axis_index("c")

    w_conv_loc = jnp.zeros((8, 128), F32).at[:3, :64].set(w_conv[0])
    w_int_g, w_conv_g, tabs, w_mkv_loc, w_out_loc, w_up_loc = _all_gather(
        [w_in[0].T.astype(BF16), w_conv_loc], "gather_w_in",
        splits=[[(112 * k, 112) for k in range(7)] + [(784, 144)], [(0, 8)]],
        side=_gather_side(s, w_mem_kv[0], w_out[0], (w_up_a[0], w_up_b[0], w_up_m[0])))
    w_int = w_int_g.reshape(IN_WIDTH, D_MODEL)
    w_conv_f = w_conv_g[:, :3, :64].transpose(1, 0, 2).reshape(3, 512)
    late = _gather_start([w_mkv_loc, w_out_loc, w_up_loc], me, "gather_late_start")
    sink = attn_sink[0]

    h, pa, pq, pkv, pbz, pmq, pmz, pg = _proj_fwd(x2, g_pre + late[4][0:1, 0:1], w_int, tabs)
    ya = _conv_fwd(pa, w_conv_f)
    yb = _attn_fwd(pq, pkv, pbz, sink)
    w_mkv_g, w_out_g, w_up_g = _gather_wait(*late[:4], yb, "gather_late_wait")
    w_mkv = w_mkv_g.reshape(D_MODEL, D_MODEL)
    w_out_f = w_out_g.reshape(D_MODEL, D_MODEL)
    mn, mkv = _mem_kv_fwd(mem2, g_mem, w_mkv)
    ym = _mem_attn_fwd(pmq, pmz, mkv)
    dg, dya, dyb, dym, dy, loss_p, gg_post, mb, dob, du = _mid(ya, yb, ym, pg, x2, tgt2, g_post, w_up_g, w_out_f)
    gw_out, gw_up = _gw_mid(mb, dob, (ya, yb, ym), du)

    core = lax.axis_index("c").astype(jnp.int32).reshape(1)
    chip = (2 * lax.axis_index("x") + lax.axis_index("y")).astype(jnp.int32).reshape(1)

    def exchange_start(shares, tag):
        from_sibling = _sibling_exchange(shares, "grads_to_sibling_" + tag)
        chip_shares = _pair_add(shares, from_sibling, core, "grads_pair_add_" + tag)
        return _chip_exchange_start(chip_shares, "grads_to_chips_start_" + tag)

    dmq, dmz, dmkv = _mem_attn_bwd(pmq, pmz, mkv, dym)
    gw_mkv, gg_mem = _mem_kv_bwd(mem2, g_mem, mn, dmkv, w_mkv)
    shares1 = [gw_mkv.reshape(N_DEV, 128, D_MODEL), gw_out.reshape(N_DEV, 128, D_MODEL), gw_up]
    sib = _split_start(_sibling_copies, N_CHIPS, shares1,
                       [lax.empty((N_CHIPS,) + a.shape[1:], a.dtype) for a in shares1], "grads_to_sibling_small_start")
    da, gw_conv = _conv_bwd(pa, dya, w_conv_f + sib[4][0:1, 0:1])
    shares1, from_sibling = _split_wait(_sibling_copies, N_CHIPS, *sib[:4], da, "grads_to_sibling_small_wait")
    send1, recv1, srcs1, lands1, token1 = _chip_exchange_start(
        _pair_add(shares1, from_sibling, core, "grads_pair_add_small"), "grads_to_chips_start_small")
    dq, dbz, dkv, g_sink = _attn_bwd(pq, pkv, pbz, dyb, sink + token1[0, 0], tabs)
    dparts = (da, dq, dkv, dbz, dmq, dmz, dg)
    gw_int = _gw_in(dparts, h)
    send2, recv2, srcs2, lands2, token2 = exchange_start([gw_int.reshape(N_DEV, SHARD_IN, D_MODEL)], "w_in")
    grad_x, gg_pre = _dh_bwd(dparts, x2, dy, g_pre + token2[0:1, 0:1], w_int)
    (o_mkv, o_out, o_up, o_int), (l_mkv, l_out, l_up, l_int) = _chip_exchange_wait(
        send1 + send2, recv1 + recv2, srcs1 + srcs2, lands1 + lands2, grad_x, "grads_to_chips_wait")

    loss_row, small_g, sd, sm, sv = _small_step(
        (gg_pre, gw_conv, g_sink, gg_mem, gg_post, loss_p),
        [g_pre, w_conv[0], attn_sink, g_mem, g_post],
        [m_g_pre, m_w_conv[0], m_attn_sink, m_g_mem, m_g_post],
        [v_g_pre, v_w_conv[0], v_attn_sink, v_g_mem, v_g_post])
    loss = loss_row[0, 0]
    g_g_pre, g_conv, g_sink_tot, g_g_mem, g_g_post = small_g

    g_w_in, d_w_in, nm_w_in, nv_w_in = (t.T for t in _sum_adamw(
        o_int, l_int, chip, 0, w_in[0].T, m_w_in[0].T, v_w_in[0].T, "adamw_w_in", tiles=2))
    (g_mkv, d_mkv, nm_mkv, nv_mkv), (g_out, d_out, nm_out, nv_out), *up = _sum_adamw_group(
        [(o_mkv, l_mkv, 0, w_mem_kv[0], m_w_mem_kv[0], v_w_mem_kv[0]),
         (o_out, l_out, 0, w_out[0], m_w_out[0], v_w_out[0]),
         (o_up, l_up, 0, w_up_a[0], m_w_up_a[0], v_w_up_a[0]),
         (o_up, l_up, 1, w_up_b[0], m_w_up_b[0], v_w_up_b[0]),
         (o_up, l_up, 2, w_up_m[0], m_w_up_m[0], v_w_up_m[0])], chip, "adamw_mid_weights")

    def lead(a):
        return a[None]

    grads = [g_g_pre, lead(g_w_in), lead(g_conv), g_sink_tot, g_g_mem, lead(g_mkv), lead(up[0][0]),
             lead(up[1][0]), lead(up[2][0]), lead(g_out), g_g_post]

    def assemble(small, big_in, big_mkv, big_up, big_out):
        return [small[0], lead(big_in), lead(small[1]), small[2], small[3], lead(big_mkv), lead(big_up[0]),
                lead(big_up[1]), lead(big_up[2]), lead(big_out), small[4]]

    deltas = assemble(sd, d_w_in, d_mkv, [u[1] for u in up], d_out)
    new_m = assemble(sm, nm_w_in, nm_mkv, [u[2] for u in up], nm_out)
    new_v = assemble(sv, nv_w_in, nv_mkv, [u[3] for u in up], nv_out)
    return (loss, grad_x[None], *grads, *deltas, *new_m, *new_v)
```

```python
import functools

import jax
import jax.numpy as jnp
from jax import lax
from jax.experimental import pallas as pl
from jax.experimental.pallas import tpu as pltpu

F32 = jnp.float32
BF16 = jnp.bfloat16
MESH = pl.DeviceIdType.MESH

N_DEV = 8
D_MODEL = 1024
EPS = 1e-6
ROPE_THETA = 500000.0
ROT_DIM = 16
HEAD_DIM = 64
ATTN_BLOCK = 128
MEM_HEADS = 4
MEM_HEAD_DIM = 128
ATTN_SCALE = HEAD_DIM ** -0.5
MEM_SCALE = MEM_HEAD_DIM ** -0.5

ADAM_LR = 0.001
ADAM_B1 = 0.9
ADAM_B2 = 0.999
ADAM_EPS = 1e-08
ADAM_WD = 0.01
ADAM_STEP = 10

SEG_A = (0, 2048)
SEG_BQ = (2048, 512)
SEG_BKV = (2560, 256)
SEG_BZ = (2816, 512)
SEG_MQ = (3328, 512)
SEG_MZ = (3840, 512)
SEG_G = (4352, 3072)
SEGS = (SEG_A, SEG_BQ, SEG_BKV, SEG_BZ, SEG_MQ, SEG_MZ, SEG_G)
IN_WIDTH = 7424
SHARD_IN = IN_WIDTH // N_DEV

V7X_VMEM_BYTES = 64 * 1024 * 1024
ANY = pl.BlockSpec(memory_space=pl.ANY)


def _params(vmem_mb):
    assert vmem_mb * 1024 * 1024 < V7X_VMEM_BYTES
    return pltpu.CompilerParams(dimension_semantics=("arbitrary",), vmem_limit_bytes=vmem_mb * 1024 * 1024)


def _full(shape):
    zeros = (0,) * len(shape)
    return pl.BlockSpec(shape, lambda i: zeros)


def _rows(tm, width):
    return pl.BlockSpec((tm, width), lambda i: (i, 0))


def _dot(a, b):
    return jnp.dot(a, b, preferred_element_type=F32)


def _dot_nt(a, b):
    return lax.dot_general(a, b, (((1,), (1,)), ((), ())), preferred_element_type=F32)


def _dot_tn(a, b):
    return lax.dot_general(a, b, (((0,), (0,)), ((), ())), preferred_element_type=F32)


def _sigmoid(z):
    return 1.0 / (1.0 + jnp.exp(-z))


def _rope(t, cs, s1, s2):
    return t * cs + pltpu.roll(t, 120, 1) * s1 + pltpu.roll(t, 8, 1) * s2


def _rope_t(d, cs, s1, s2):
    return d * cs + pltpu.roll(d * s1, 8, 1) + pltpu.roll(d * s2, 120, 1)


def _gather_side(s, w_mkv, w_out, w_ups):
    half = ROT_DIM // 2
    inv_freq = jnp.power(jnp.float32(ROPE_THETA), -jnp.arange(half, dtype=F32) * (2.0 / ROT_DIM))
    freq_row = jnp.tile(jnp.concatenate([inv_freq, inv_freq, jnp.zeros((HEAD_DIM - ROT_DIM,), F32)]), 2)[None, :]

    def fn(in_refs, out_refs):
        f_ref, mkv_ref, out_ref, *up_refs = in_refs
        t_ref, mkv_bf, out_bf, up_bf = out_refs
        mkv_bf[...] = mkv_ref[...].astype(BF16)
        out_bf[...] = out_ref[...].astype(BF16)
        for k, up_ref in enumerate(up_refs):
            up_bf[512 * k:512 * k + 512, :] = up_ref[...].astype(BF16)
        pos = lax.broadcasted_iota(jnp.int32, (s, 128), 0).astype(F32)
        d = lax.broadcasted_iota(jnp.int32, (s, 128), 1) & (HEAD_DIM - 1)
        ang = pos * f_ref[...]
        cos, sin = jnp.cos(ang), jnp.sin(ang)
        lo, hi = d < half, (d >= half) & (d < ROT_DIM)
        t_ref[0] = jnp.where(lo | hi, cos, 1.0)
        t_ref[1] = jnp.where(lo, -sin, 0.0)
        t_ref[2] = jnp.where(hi, sin, 0.0)

    return ([freq_row, w_mkv, w_out, *w_ups],
            [jax.ShapeDtypeStruct((3, s, 128), F32), jax.ShapeDtypeStruct(w_mkv.shape, BF16),
             jax.ShapeDtypeStruct(w_out.shape, BF16), jax.ShapeDtypeStruct((1536, 128), BF16)], fn)


def _load_once(pairs, sems):
    @pl.when(pl.program_id(0) == 0)
    def _():
        cps = [pltpu.make_async_copy(src, dst, sems.at[k]) for k, (src, dst) in enumerate(pairs)]
        for cp in cps:
            cp.start()
        for cp in cps:
            cp.wait()


def _my_place():
    x, y, c = lax.axis_index("x"), lax.axis_index("y"), lax.axis_index("c")
    return x, y, c


def _all_gather(arrs, name, splits=None, side=None):
    n = len(arrs)
    if splits is None:
        splits = [[(0, a.shape[0])] for a in arrs]
    pieces = [(a, r0, rn) for a in range(n) for r0, rn in splits[a]]
    n_p = len(pieces)
    side_in, side_out, side_fn = side if side is not None else ((), (), None)
    m, q = len(side_in), len(side_out)

    def body(*refs):
        ins, outs = refs[:n], refs[n + m:2 * n + m]
        send_sems, recv_sems, local_sems = refs[2 * n + m + q:]
        x, y, c = _my_place()
        me, sibling = (x, y, c), (x, y, 1 - c)

        def route(core):
            first = (jnp.bitwise_xor(x, 1 - core), jnp.bitwise_xor(y, core), core)
            second = (jnp.bitwise_xor(x, core), jnp.bitwise_xor(y, 1 - core), core)
            return first, second, (1 - x, 1 - y, core)

        def idx(px, py, pc):
            return 4 * px + 2 * py + pc

        def copy(p, k, block, to, own=False):
            a, r0, rn = pieces[p]
            dst = outs[a].at[idx(*block), pl.ds(r0, rn)]
            return pltpu.make_async_remote_copy(
                src_ref=ins[a].at[pl.ds(r0, rn)] if own else dst, dst_ref=dst,
                send_sem=send_sems.at[p * 7 + k], recv_sem=recv_sems.at[p * 7 + k],
                device_id=to, device_id_type=MESH)

        nbr1, nbr2, diag = route(c)
        mine = [pltpu.make_async_copy(ins[a], outs[a].at[idx(*me)], local_sems.at[a]) for a in range(n)]
        for cp in mine:
            cp.start()
        sent = []
        for p in range(n_p):
            for k, to in enumerate((sibling, nbr1, nbr2)):
                sent.append(copy(p, k, me, to, own=True))
        for cp in sent:
            cp.start()
        if side_fn is not None:
            side_fn(refs[n:n + m], refs[2 * n + m:2 * n + m + q])
        for k_in, block, onward in ((1, nbr1, ((3, nbr2), (4, sibling))), (2, nbr2, ((5, sibling),)),
                                    (3, diag, ((6, sibling),))):
            for p in range(n_p):
                copy(p, k_in, block, me).wait_recv()
                for k_out, to in onward:
                    cp = copy(p, k_out, block, to)
                    cp.start()
                    sent.append(cp)
        s1, s2, sd = route(1 - c)
        for k_in, block in ((0, sibling), (4, s1), (5, s2), (6, sd)):
            for p in range(n_p):
                copy(p, k_in, block, me).wait_recv()
        for cp in sent:
            cp.wait_send()
        for cp in mine:
            cp.wait()

    return pl.pallas_call(
        body, name=name,
        out_shape=[jax.ShapeDtypeStruct((N_DEV,) + a.shape, a.dtype) for a in arrs] + list(side_out),
        in_specs=[ANY] * n + [pl.BlockSpec(memory_space=pltpu.VMEM)] * m,
        out_specs=[ANY] * n + [pl.BlockSpec(memory_space=pltpu.VMEM)] * q,
        scratch_shapes=[pltpu.SemaphoreType.DMA((7 * n_p,)), pltpu.SemaphoreType.DMA((7 * n_p,)),
                        pltpu.SemaphoreType.DMA((n,))],
        compiler_params=pltpu.CompilerParams(vmem_limit_bytes=32 * 1024 * 1024),
    )(*arrs, *side_in)


N_CHIPS = 4


def _sibling_exchange(arrs, name):
    n = len(arrs)

    def body(*refs):
        ins, outs = refs[:n], refs[n:2 * n]
        send_sems, recv_sems = refs[2 * n:]
        x, y, c = _my_place()
        sibling = (x, y, 1 - c)

        def copy(a, j):
            return pltpu.make_async_remote_copy(
                src_ref=ins[a].at[2 * j + (1 - c)], dst_ref=outs[a].at[j],
                send_sem=send_sems.at[a * N_CHIPS + j], recv_sem=recv_sems.at[a * N_CHIPS + j],
                device_id=sibling, device_id_type=MESH)

        cps = [copy(a, j) for j in range(N_CHIPS) for a in range(n)]
        for cp in cps:
            cp.start()
        for cp in cps:
            cp.wait_recv()
        for cp in cps:
            cp.wait_send()

    return pl.pallas_call(
        body, name=name,
        out_shape=[jax.ShapeDtypeStruct((N_CHIPS,) + a.shape[1:], a.dtype) for a in arrs],
        in_specs=[ANY] * n, out_specs=[ANY] * n,
        scratch_shapes=[pltpu.SemaphoreType.DMA((N_CHIPS * n,)), pltpu.SemaphoreType.DMA((N_CHIPS * n,))],
    )(*arrs)


def _sibling_copies(srcs, lands, send_sems, recv_sems):
    x, y, c = _my_place()
    cps = []
    for j in range(N_CHIPS):
        for a in range(len(srcs)):
            k = a * N_CHIPS + j
            cps.append(pltpu.make_async_remote_copy(
                src_ref=srcs[a].at[2 * j + (1 - c)], dst_ref=lands[a].at[j], send_sem=send_sems[k],
                recv_sem=recv_sems[k], device_id=(x, y, 1 - c), device_id_type=MESH))
    return cps


def _pair_add(mine, recv, core, name):
    n = len(mine)

    def body(c_ref, *refs):
        for a in range(n):
            refs[2 * n + a][...] = (refs[a][...].astype(F32) + refs[n + a][...].astype(F32)).astype(BF16)

    def blk(a):
        return (None,) + a.shape[1:]

    grid_spec = pltpu.PrefetchScalarGridSpec(
        num_scalar_prefetch=1, grid=(N_CHIPS,),
        in_specs=[pl.BlockSpec(blk(a), lambda j, c_ref: (2 * j + c_ref[0], 0, 0)) for a in mine]
        + [pl.BlockSpec(blk(a), lambda j, c_ref: (j, 0, 0)) for a in recv],
        out_specs=[pl.BlockSpec(blk(a), lambda j, c_ref: (j, 0, 0)) for a in recv])
    return pl.pallas_call(
        body, name=name, grid_spec=grid_spec,
        out_shape=[jax.ShapeDtypeStruct(a.shape, BF16) for a in recv],
        compiler_params=_params(24),
    )(core, *mine, *recv)


HBM = pl.BlockSpec(memory_space=pltpu.HBM)
SEM = pl.BlockSpec(memory_space=pltpu.SEMAPHORE)
N_PEER_CHIPS = 3


def _chip_copies(srcs, lands, send_sems, recv_sems):
    x, y, c = _my_place()
    my_chip = 2 * x + y
    peers = [(x, 1 - y), (1 - x, y), (1 - x, 1 - y)]
    cps = []
    for k, (px, py) in enumerate(peers):
        for a in range(len(srcs)):
            j = a * N_PEER_CHIPS + k
            cps.append(pltpu.make_async_remote_copy(
                src_ref=srcs[a].at[2 * px + py], dst_ref=lands[a].at[my_chip],
                send_sem=send_sems[j], recv_sem=recv_sems[j],
                device_id=(px, py, c), device_id_type=MESH))
    return cps


N_PEERS = N_DEV - 1


def _gather_copies(srcs, lands, send_sems, recv_sems):
    x, y, c = _my_place()
    me_idx = 4 * x + 2 * y + c
    flips = [(0, 0, 1), (0, 1, 0), (1, 0, 0), (0, 1, 1), (1, 0, 1), (1, 1, 0), (1, 1, 1)]
    cps = []
    for k, (fx, fy, fc) in enumerate(flips):
        peer = ((1 - x) if fx else x, (1 - y) if fy else y, (1 - c) if fc else c)
        for a in range(len(srcs)):
            j = a * N_PEERS + k
            cps.append(pltpu.make_async_remote_copy(
                src_ref=srcs[a], dst_ref=lands[a].at[me_idx], send_sem=send_sems[j], recv_sem=recv_sems[j],
                device_id=peer, device_id_type=MESH))
    return cps


def _split_start(copies, per_array, arrs, lands, name):
    arrs, lands = list(arrs), list(lands)
    n = len(arrs)
    k = n * per_array

    def body(*refs):
        srcs, land_refs = refs[:n], refs[n:2 * n]
        send_sems, recv_sems = refs[2 * n:2 * n + k], refs[2 * n + k:2 * n + 2 * k]
        token = refs[-1]
        for cp in copies(srcs, land_refs, send_sems, recv_sems):
            cp.start()
        token[...] = jnp.zeros_like(token)

    hbm_arrs = [pltpu.with_memory_space_constraint(a, pltpu.HBM) for a in arrs]
    lands = [pltpu.with_memory_space_constraint(a, pltpu.HBM) for a in lands]
    res = pl.pallas_call(
        body, name=name,
        out_shape=[pltpu.SemaphoreType.DMA(())] * (2 * k) + [pltpu.HBM(a.shape, a.dtype) for a in arrs + lands]
        + [jax.ShapeDtypeStruct((8, 128), F32)],
        in_specs=[HBM] * (2 * n),
        out_specs=[SEM] * (2 * k) + [HBM] * (2 * n) + [pl.BlockSpec(memory_space=pltpu.VMEM)],
        input_output_aliases={a: 2 * k + a for a in range(2 * n)},
        compiler_params=pltpu.CompilerParams(has_side_effects=pltpu.SideEffectType.DATAFLOW_SIDE_EFFECTING),
    )(*hbm_arrs, *lands)
    return res[:k], res[k:2 * k], res[2 * k:2 * k + n], res[2 * k + n:2 * k + 2 * n], res[-1]


def _split_wait(copies, per_array, send_sems, recv_sems, srcs, lands, after, name):
    n = len(srcs)
    k = n * per_array

    def body(*refs):
        src_refs, land_refs = refs[:n], refs[n:2 * n]
        s_sems, r_sems = refs[2 * n:2 * n + k], refs[2 * n + k:2 * n + 2 * k]
        for cp in copies(src_refs, land_refs, s_sems, r_sems):
            cp.wait_send()
            cp.wait_recv()

    res = pl.pallas_call(
        body, name=name,
        out_shape=[pltpu.HBM(a.shape, a.dtype) for a in list(srcs) + list(lands)],
        in_specs=[HBM] * (2 * n) + [SEM] * (2 * k) + [ANY],
        out_specs=[HBM] * (2 * n),
        input_output_aliases={a: a for a in range(2 * n)},
        compiler_params=pltpu.CompilerParams(has_side_effects=pltpu.SideEffectType.DATAFLOW_SIDE_EFFECTING),
    )(*srcs, *lands, *send_sems, *recv_sems, after)
    return res[:n], res[n:]


def _chip_exchange_start(arrs, name):
    return _split_start(_chip_copies, N_PEER_CHIPS, arrs, [lax.empty(a.shape, a.dtype) for a in arrs], name)


def _chip_exchange_wait(send_sems, recv_sems, srcs, lands, after, name):
    return _split_wait(_chip_copies, N_PEER_CHIPS, send_sems, recv_sems, srcs, lands, after, name)


def _gather_start(arrs, me_idx, name):
    lands = [lax.dynamic_update_slice(lax.empty((N_DEV,) + a.shape, a.dtype), a[None], (me_idx, 0, 0)) for a in arrs]
    return _split_start(_gather_copies, N_PEERS, arrs, lands, name)


def _gather_wait(send_sems, recv_sems, srcs, lands, after, name):
    return _split_wait(_gather_copies, N_PEERS, send_sems, recv_sems, srcs, lands, after, name)[1]


def _proj_fwd(x, g_pre, w_int, tabs):
    s = x.shape[0]
    tm = min(512, s)

    def body(x_ref, g_ref, t_ref, w_hbm,
             h_ref, pa_ref, pq_ref, pkv_ref, pbz_ref, pmq_ref, pmz_ref, pg_ref, w_vm, sems):
        _load_once([(w_hbm, w_vm)], sems)
        xf = x_ref[...]
        r = lax.rsqrt(jnp.mean(xf * xf, axis=-1, keepdims=True) + EPS)
        h = ((xf * r) * g_ref[...]).astype(BF16)
        h_ref[...] = h
        cs, s1, s2 = t_ref[0], t_ref[1], t_ref[2]

        def mm(seg, c0, width):
            return _dot_nt(h, w_vm[seg[0] + c0:seg[0] + c0 + width, :])

        for c0 in range(0, SEG_A[1], 512):
            pa_ref[:, c0:c0 + 512] = mm(SEG_A, c0, 512).astype(BF16)
        q = mm(SEG_BQ, 0, 512)
        for b in range(4):
            pq_ref[:, 128 * b:128 * b + 128] = _rope(q[:, 128 * b:128 * b + 128], cs, s1, s2).astype(BF16)
        kv = mm(SEG_BKV, 0, 256)
        pkv_ref[:, 0:128] = _rope(kv[:, 0:128], cs, s1, s2).astype(BF16)
        pkv_ref[:, 128:256] = kv[:, 128:256].astype(BF16)
        pbz_ref[...] = mm(SEG_BZ, 0, 512).astype(BF16)
        pmq_ref[...] = mm(SEG_MQ, 0, 512).astype(BF16)
        pmz_ref[...] = mm(SEG_MZ, 0, 512).astype(BF16)
        for c0 in range(0, SEG_G[1], 512):
            pg_ref[:, c0:c0 + 512] = mm(SEG_G, c0, 512).astype(BF16)

    widths = (D_MODEL, 2048, 512, 256, 512, 512, 512, 3072)
    return pl.pallas_call(
        body, name="proj_fwd", grid=(s // tm,),
        out_shape=[jax.ShapeDtypeStruct((s, w), BF16) for w in widths],
        in_specs=[_rows(tm, D_MODEL), _full((1, D_MODEL)), pl.BlockSpec((3, tm, 128), lambda i: (0, i, 0)), ANY],
        out_specs=[_rows(tm, w) for w in widths],
        scratch_shapes=[pltpu.VMEM((IN_WIDTH, D_MODEL), BF16), pltpu.SemaphoreType.DMA((1,))],
        compiler_params=_params(46),
    )(x, g_pre, tabs, w_int)


def _mem_kv_fwd(mem, g_mem, w_mkv):
    m = mem.shape[0]

    def body(mem_ref, g_ref, w_ref, mn_ref, mkv_ref):
        xf = mem_ref[...]
        r = lax.rsqrt(jnp.mean(xf * xf, axis=-1, keepdims=True) + EPS)
        mn = ((xf * r) * g_ref[...]).astype(BF16)
        mn_ref[...] = mn
        mkv_ref[...] = _dot(mn, w_ref[...]).astype(BF16)

    return pl.pallas_call(
        body, name="mem_kv_fwd", grid=(1,),
        out_shape=[jax.ShapeDtypeStruct((m, D_MODEL), BF16)] * 2,
        in_specs=[_full((m, D_MODEL)), _full((1, D_MODEL)), _full((D_MODEL, D_MODEL))],
        out_specs=[_full((m, D_MODEL))] * 2,
        compiler_params=_params(12),
    )(mem, g_mem, w_mkv)


def _halo_specs(s, tm, rows, width):
    nblk = s // rows
    prev = pl.BlockSpec((rows, width), lambda i: (jnp.maximum(i * (tm // rows) - 1, 0), 0))
    nxt = pl.BlockSpec((rows, width), lambda i: (jnp.minimum((i + 1) * (tm // rows), nblk - 1), 0))
    return prev, nxt


def _conv_common(pa, cu_prev, cu_next, w, tm):
    b, c, u, z = (pa[:, 512 * k:512 * k + 512] for k in range(4))
    cu = c * u
    row = lax.broadcasted_iota(jnp.int32, (tm, 512), 0)
    cu_m1 = jnp.where(row == 0, cu_prev, pltpu.roll(cu, 1, 0))
    cu_p1 = jnp.where(row == tm - 1, cu_next, pltpu.roll(cu, tm - 1, 0))
    y = cu_m1 * w[0:1] + cu * w[1:2] + cu_p1 * w[2:3]
    sig = _sigmoid(z)
    return b, c, u, z, cu, cu_m1, cu_p1, y, sig, row


def _conv_fwd(pa, w_conv):
    s = pa.shape[0]
    tm = min(512, s)
    nt = s // tm

    def body(pa_ref, pp_ref, pn_ref, w_ref, ya_ref):
        i = pl.program_id(0)
        prev_row = pp_ref[...].astype(F32)[15:16, :]
        next_row = pn_ref[...].astype(F32)[0:1, :]
        b, _, _, z, _, _, _, y, sig, _ = _conv_common(
            pa_ref[...].astype(F32),
            jnp.where(i == 0, 0.0, prev_row[:, 512:1024] * prev_row[:, 1024:1536]),
            jnp.where(i == nt - 1, 0.0, next_row[:, 512:1024] * next_row[:, 1024:1536]), w_ref[...], tm)
        ya_ref[...] = (b * y * (z * sig)).astype(BF16)

    prev, nxt = _halo_specs(s, tm, 16, 2048)
    return pl.pallas_call(
        body, name="conv_fwd", grid=(nt,),
        out_shape=jax.ShapeDtypeStruct((s, 512), BF16),
        in_specs=[_rows(tm, 2048), prev, nxt, _full((3, 512))],
        out_specs=_rows(tm, 512),
        compiler_params=_params(32),
    )(pa, pa, pa, w_conv)


def _heads_to_lanes(a, g, row):
    low = row < HEAD_DIM
    parts = []
    for b in (2 * g, 2 * g + 1):
        t = jnp.transpose(a[:, 128 * b:128 * b + 128])
        swapped = pltpu.roll(t, HEAD_DIM, 0)
        if g == 0:
            parts += [jnp.where(low, t, 0.0), jnp.where(low, swapped, 0.0)]
        else:
            parts += [jnp.where(low, 0.0, swapped), jnp.where(low, 0.0, t)]
    return jnp.concatenate(parts, axis=1)


def _lanes_to_heads(t0, t1, row):
    low = row < HEAD_DIM
    blocks = []
    for b in range(4):
        g = b // 2
        tg = (t0, t1)[g]
        je = 2 * (b - 2 * g)
        even, odd = tg[:, 128 * je:128 * je + 128], tg[:, 128 * je + 128:128 * je + 256]
        if g == 0:
            t = jnp.where(low, even, pltpu.roll(odd, HEAD_DIM, 0))
        else:
            t = jnp.where(low, pltpu.roll(even, HEAD_DIM, 0), odd)
        blocks.append(jnp.transpose(t))
    return jnp.concatenate(blocks, axis=1)


WINDOW_KEYS = 3 * ATTN_BLOCK
STACKED = 4 * ATTN_BLOCK
KEY_CHUNK = 32
MAX_BLOCKS_IN_STEP = 8


def _fill_band_bias(bias, nb):
    assert nb >= 2
    c = lax.broadcasted_iota(jnp.int32, (WINDOW_KEYS, STACKED), 0)
    r = lax.broadcasted_iota(jnp.int32, (WINDOW_KEYS, STACKED), 1) & (ATTN_BLOCK - 1)
    band = (c >= r) & (c <= r + 2 * ATTN_BLOCK)
    for v, ok in enumerate((band, band & (c >= ATTN_BLOCK), band & (c < 2 * ATTN_BLOCK))):
        bias[v] = jnp.where(ok, 0.0, -jnp.inf)


def _bias_variant(n, nb):
    return jnp.where(n == 0, 1, jnp.where(n == nb - 1, 2, 0))


def _sink_row(sink_ref, g):
    return jnp.concatenate([jnp.full((1, ATTN_BLOCK), sink_ref[4 * g + j], F32) for j in range(4)], axis=1)


def _softmax_keys_major(sc, bias, variant, sink, e_scr):
    chunks = [pl.ds(k * KEY_CHUNK, KEY_CHUNK) for k in range(WINDOW_KEYS // KEY_CHUNK)]
    rows = [slice(k * KEY_CHUNK, (k + 1) * KEY_CHUNK) for k in range(WINDOW_KEYS // KEY_CHUNK)]
    m_run = jnp.full((KEY_CHUNK, STACKED), -jnp.inf, F32)
    for ck, rw in zip(chunks, rows):
        m_run = jnp.maximum(m_run, sc[rw] + bias[variant, ck, :])
    m = jnp.maximum(jnp.max(m_run, axis=0, keepdims=True), sink)
    l_run = jnp.zeros((KEY_CHUNK, STACKED), F32)
    for ck, rw in zip(chunks, rows):
        e = jnp.exp(sc[rw] + bias[variant, ck, :] - m)
        l_run += e
        e_scr[rw, :] = e.astype(BF16)
    es = jnp.exp(sink - m)
    inv = 1.0 / (jnp.sum(l_run, axis=0, keepdims=True) + es)
    return inv, es * inv


def _fill_padded(kv_ref, kpad, vpad, s):
    zero = jnp.zeros((ATTN_BLOCK, 128), BF16)
    kpad[0:ATTN_BLOCK, :] = zero
    vpad[0:ATTN_BLOCK, :] = zero
    kpad[ATTN_BLOCK + s:2 * ATTN_BLOCK + s, :] = zero
    vpad[ATTN_BLOCK + s:2 * ATTN_BLOCK + s, :] = zero
    kpad[ATTN_BLOCK:ATTN_BLOCK + s, :] = kv_ref[:, 0:128]
    vpad[ATTN_BLOCK:ATTN_BLOCK + s, :] = kv_ref[:, 128:256]


def _attn_fwd(pq, pkv, pbz, sink):
    s = pq.shape[0]
    nb = s // ATTN_BLOCK
    bps = min(MAX_BLOCKS_IN_STEP, nb)

    def body(sink_ref, q_ref, z_ref, kv_ref, yb_ref, kpad, vpad, bias, e_scr):
        i = pl.program_id(0)

        @pl.when(i == 0)
        def _():
            _fill_padded(kv_ref, kpad, vpad, s)
            _fill_band_bias(bias, nb)

        row = lax.broadcasted_iota(jnp.int32, (ATTN_BLOCK, 128), 0)
        for b in range(bps):
            n = i * bps + b
            rows = slice(b * ATTN_BLOCK, (b + 1) * ATTN_BLOCK)
            start = pl.multiple_of(n * ATTN_BLOCK, ATTN_BLOCK)
            kw, vw = kpad[pl.ds(start, WINDOW_KEYS), :], vpad[pl.ds(start, WINDOW_KEYS), :]
            qf = q_ref[rows, :].astype(F32)
            variant = _bias_variant(n, nb)
            outs = []
            for g in range(2):
                e_bg = e_scr.at[2 * b + g]
                qt = (_heads_to_lanes(qf, g, row) * ATTN_SCALE).astype(BF16)
                inv, _ = _softmax_keys_major(_dot(kw, qt), bias, variant, _sink_row(sink_ref, g), e_bg)
                outs.append(_dot_tn(vw, e_bg[...]) * inv)
            attn = _lanes_to_heads(outs[0], outs[1], row)
            z = z_ref[rows, :].astype(F32)
            yb_ref[rows, :] = (attn * (z * _sigmoid(z))).astype(BF16)

    tq = bps * ATTN_BLOCK
    return pl.pallas_call(
        body, name="attn_fwd", grid=(s // tq,),
        out_shape=jax.ShapeDtypeStruct((s, 512), BF16),
        in_specs=[pl.BlockSpec(memory_space=pltpu.SMEM), _rows(tq, 512), _rows(tq, 512), _full((s, 256))],
        out_specs=_rows(tq, 512),
        scratch_shapes=[pltpu.VMEM((s + 2 * ATTN_BLOCK, 128), BF16)] * 2
        + [pltpu.VMEM((3, WINDOW_KEYS, STACKED), F32),
           pltpu.VMEM((2 * bps, WINDOW_KEYS, STACKED), BF16)],
        compiler_params=_params(32),
    )(sink, pq, pbz, pkv)


def _mem_softmax_t(q, mk):
    sc = _dot_nt(mk, q) * MEM_SCALE
    e = jnp.exp(sc - jnp.max(sc, axis=0, keepdims=True))
    return e * (1.0 / jnp.sum(e, axis=0, keepdims=True))


def _mem_attn_fwd(pmq, pmz, mkv):
    s = pmq.shape[0]
    m = mkv.shape[0]
    tm = min(512, s)

    def body(q_ref, z_ref, mk_ref, mv_ref, ym_ref):
        z = z_ref[...].astype(F32)
        sz = z * _sigmoid(z)
        for h in range(MEM_HEADS):
            cols = slice(128 * h, 128 * h + 128)
            pt = _mem_softmax_t(q_ref[:, cols], mk_ref[:, cols])
            o = _dot_tn(pt.astype(BF16), mv_ref[:, cols])
            ym_ref[:, cols] = (o * sz[:, cols]).astype(BF16)

    return pl.pallas_call(
        body, name="mem_attn_fwd", grid=(s // tm,),
        out_shape=jax.ShapeDtypeStruct((s, 512), BF16),
        in_specs=[_rows(tm, 512), _rows(tm, 512), pl.BlockSpec((m, 512), lambda i: (0, 0)),
                  pl.BlockSpec((m, 512), lambda i: (0, 1))],
        out_specs=_rows(tm, 512),
        compiler_params=_params(16),
    )(pmq, pmz, mkv, mkv)


def _mid(ya, yb, ym, pg, x, target, g_post, w_up, w_out):
    s = x.shape[0]
    tm = min(256, s)
    nt = s // tm

    def body(ya_ref, yb_ref, ym_ref, pg_ref, x_ref, t_ref, gp_ref, wup_hbm, wout_hbm,
             dg_ref, dya_ref, dyb_ref, dym_ref, dy_ref, loss_ref, ggp_ref, mb_ref, dob_ref, du_ref,
             wup_vm, wout_vm, sems):
        i = pl.program_id(0)
        _load_once([(wup_hbm.at[d], wup_vm.at[:, pl.ds(128 * d, 128)]) for d in range(N_DEV)]
                   + [(wout_hbm, wout_vm)], sems)

        @pl.when(i == 0)
        def _():
            loss_ref[...] = jnp.zeros_like(loss_ref)
            ggp_ref[...] = jnp.zeros_like(ggp_ref)

        ys = (ya_ref[...], yb_ref[...], ym_ref[...])
        us = [_dot(ys[k], wup_vm[512 * k:512 * k + 512, :]) for k in range(3)]
        gates = [_sigmoid(pg_ref[:, 1024 * k:1024 * k + 1024].astype(F32)) for k in range(3)]
        merged = gates[0] * us[0] + gates[1] * us[1] + gates[2] * us[2]
        mb = merged.astype(BF16)
        mb_ref[...] = mb
        out = _dot(mb, wout_vm[...])
        r = lax.rsqrt(jnp.mean(out * out, axis=-1, keepdims=True) + EPS)
        on = out * r
        gp = gp_ref[...]
        err = (x_ref[...] + on * gp) - t_ref[...]
        loss_ref[...] += 0.5 * jnp.sum(err * err) * (1.0 / D_MODEL)
        dy = err * (1.0 / D_MODEL)
        dy_ref[...] = dy
        ggp_ref[...] += jnp.sum(dy * on, axis=0, keepdims=True)
        a = dy * gp
        d_out = r * (a - on * jnp.mean(a * on, axis=-1, keepdims=True))
        dob = d_out.astype(BF16)
        dob_ref[...] = dob
        d_merged = _dot_nt(dob, wout_vm[...])
        d_refs = (dya_ref, dyb_ref, dym_ref)
        for k in range(3):
            g = gates[k]
            du_f = d_merged * g
            dg_ref[:, 1024 * k:1024 * k + 1024] = (du_f * us[k] * (1.0 - g)).astype(BF16)
            du = du_f.astype(BF16)
            du_ref[k] = du
            d_refs[k][...] = _dot_nt(du, wup_vm[512 * k:512 * k + 512, :]).astype(BF16)

    return pl.pallas_call(
        body, name="mid", grid=(nt,),
        out_shape=[jax.ShapeDtypeStruct((s, 3072), BF16)] + [jax.ShapeDtypeStruct((s, 512), BF16)] * 3
        + [jax.ShapeDtypeStruct((s, D_MODEL), F32), jax.ShapeDtypeStruct((8, 128), F32),
           jax.ShapeDtypeStruct((1, D_MODEL), F32), jax.ShapeDtypeStruct((s, D_MODEL), BF16),
           jax.ShapeDtypeStruct((s, D_MODEL), BF16), jax.ShapeDtypeStruct((3, s, D_MODEL), BF16)],
        in_specs=[_rows(tm, 512)] * 3 + [_rows(tm, 3072), _rows(tm, D_MODEL), _rows(tm, D_MODEL),
                                         _full((1, D_MODEL)), ANY, ANY],
        out_specs=[_rows(tm, 3072)] + [_rows(tm, 512)] * 3
        + [_rows(tm, D_MODEL), _full((8, 128)), _full((1, D_MODEL)), _rows(tm, D_MODEL), _rows(tm, D_MODEL),
           pl.BlockSpec((3, tm, D_MODEL), lambda i: (0, i, 0))],
        scratch_shapes=[pltpu.VMEM((1536, D_MODEL), BF16), pltpu.VMEM((D_MODEL, D_MODEL), BF16),
                        pltpu.SemaphoreType.DMA((N_DEV + 1,))],
        compiler_params=_params(48),
    )(ya, yb, ym, pg, x, target, g_post, w_up, w_out)


def _gw_mid(mb, dob, ys, du):
    s = mb.shape[0]
    tn = 256

    def out_body(mb_ref, dob_ref, o_ref):
        o_ref[...] = _dot_tn(mb_ref[...], dob_ref[...]).astype(BF16)

    gw_out = pl.pallas_call(
        out_body, name="gw_out", grid=(D_MODEL // tn,),
        out_shape=jax.ShapeDtypeStruct((D_MODEL, D_MODEL), BF16),
        in_specs=[pl.BlockSpec((s, tn), lambda j: (0, j)), _full((s, D_MODEL))],
        out_specs=pl.BlockSpec((tn, D_MODEL), lambda j: (j, 0)),
        compiler_params=_params(28),
    )(mb, dob)

    per = 512 // tn

    def up_body(ya_ref, yb_ref, ym_ref, du_ref, o_ref):
        j = pl.program_id(0)
        for k, y_ref in enumerate((ya_ref, yb_ref, ym_ref)):
            @pl.when(j // per == k)
            def _(y_ref=y_ref):
                res = _dot_tn(y_ref[...], du_ref[...])
                for d in range(N_DEV):
                    o_ref[d] = res[:, 128 * d:128 * d + 128].astype(BF16)

    def y_spec(k):
        return pl.BlockSpec((s, tn), lambda j: (0, jnp.clip(j - per * k, 0, per - 1)))

    gw_up = pl.pallas_call(
        up_body, name="gw_up", grid=(3 * per,),
        out_shape=jax.ShapeDtypeStruct((N_DEV, 1536, 128), BF16),
        in_specs=[y_spec(0), y_spec(1), y_spec(2), pl.BlockSpec((None, s, D_MODEL), lambda j: (j // per, 0, 0))],
        out_specs=pl.BlockSpec((N_DEV, tn, 128), lambda j: (0, j, 0)),
        compiler_params=_params(36),
    )(*ys, du)
    return gw_out, gw_up


def _conv_bwd(pa, dya, w_conv):
    s = pa.shape[0]
    tm = min(512, s)
    nt = s // tm

    def body(pa_ref, pp_ref, pn_ref, d_ref, dp_ref, dn_ref, w_ref, da_ref, gw_ref):
        i = pl.program_id(0)
        first, last = i == 0, i == nt - 1

        @pl.when(first)
        def _():
            gw_ref[...] = jnp.zeros_like(gw_ref)

        w = w_ref[...]
        prev_row = pp_ref[...].astype(F32)[15:16, :]
        next_row = pn_ref[...].astype(F32)[0:1, :]
        b, c, u, z, cu, cu_m1, cu_p1, y, sig, row = _conv_common(
            pa_ref[...].astype(F32),
            jnp.where(first, 0.0, prev_row[:, 512:1024] * prev_row[:, 1024:1536]),
            jnp.where(last, 0.0, next_row[:, 512:1024] * next_row[:, 1024:1536]), w, tm)
        sz = z * sig
        dya_t = d_ref[...].astype(F32)
        d_y = dya_t * b * sz

        def halo_dy(p_row, d_row):
            zz = p_row[:, 1536:2048]
            return d_row * p_row[:, 0:512] * (zz * _sigmoid(zz))

        dy_prev = jnp.where(first, 0.0, halo_dy(prev_row, dp_ref[...].astype(F32)[15:16, :]))
        dy_next = jnp.where(last, 0.0, halo_dy(next_row, dn_ref[...].astype(F32)[0:1, :]))
        dy_m1 = jnp.where(row == 0, dy_prev, pltpu.roll(d_y, 1, 0))
        dy_p1 = jnp.where(row == tm - 1, dy_next, pltpu.roll(d_y, tm - 1, 0))
        d_cu = dy_p1 * w[0:1] + d_y * w[1:2] + dy_m1 * w[2:3]
        da_ref[:, 0:512] = (dya_t * y * sz).astype(BF16)
        da_ref[:, 512:1024] = (d_cu * u).astype(BF16)
        da_ref[:, 1024:1536] = (d_cu * c).astype(BF16)
        da_ref[:, 1536:2048] = (dya_t * b * y * (sig + sz * (1.0 - sig))).astype(BF16)
        gw_ref[0:1, :] += jnp.sum(d_y * cu_m1, axis=0, keepdims=True)
        gw_ref[1:2, :] += jnp.sum(d_y * cu, axis=0, keepdims=True)
        gw_ref[2:3, :] += jnp.sum(d_y * cu_p1, axis=0, keepdims=True)

    prev, nxt = _halo_specs(s, tm, 16, 2048)
    dprev, dnxt = _halo_specs(s, tm, 16, 512)
    return pl.pallas_call(
        body, name="conv_bwd", grid=(nt,),
        out_shape=[jax.ShapeDtypeStruct((s, 2048), BF16), jax.ShapeDtypeStruct((8, 512), F32)],
        in_specs=[_rows(tm, 2048), prev, nxt, _rows(tm, 512), dprev, dnxt, _full((3, 512))],
        out_specs=[_rows(tm, 2048), _full((8, 512))],
        compiler_params=_params(36),
    )(pa, pa, pa, dya, dya, dya, w_conv)


def _attn_bwd(pq, pkv, pbz, dyb, sink, tabs):
    s = pq.shape[0]
    nb = s // ATTN_BLOCK
    bps = min(MAX_BLOCKS_IN_STEP, nb)

    def body(sink_ref, q_ref, z_ref, d_ref, kv_ref, t_ref,
             dq_ref, dz_ref, dkv_ref, gs_ref, kpad, vpad, dk_acc, dv_acc, bias, e_scr, ds_scr):
        i = pl.program_id(0)

        @pl.when(i == 0)
        def _():
            _fill_padded(kv_ref, kpad, vpad, s)
            _fill_band_bias(bias, nb)
            dk_acc[...] = jnp.zeros_like(dk_acc)
            dv_acc[...] = jnp.zeros_like(dv_acc)
            gs_ref[...] = jnp.zeros_like(gs_ref)

        row = lax.broadcasted_iota(jnp.int32, (ATTN_BLOCK, 128), 0)
        for b in range(bps):
            n = i * bps + b
            rows = slice(b * ATTN_BLOCK, (b + 1) * ATTN_BLOCK)
            start = pl.multiple_of(n * ATTN_BLOCK, ATTN_BLOCK)
            kw, vw = kpad[pl.ds(start, WINDOW_KEYS), :], vpad[pl.ds(start, WINDOW_KEYS), :]
            qf = q_ref[rows, :].astype(F32)
            variant = _bias_variant(n, nb)
            z = z_ref[rows, :].astype(F32)
            sig = _sigmoid(z)
            dyb_t = d_ref[rows, :].astype(F32)
            d_attn = dyb_t * (z * sig)
            outs, dqs = [], []
            dk_w = jnp.zeros((WINDOW_KEYS, 128), F32)
            dv_w = jnp.zeros((WINDOW_KEYS, 128), F32)
            for g in range(2):
                e_bg, ds_bg = e_scr.at[2 * b + g], ds_scr.at[2 * b + g]
                qt = _heads_to_lanes(qf, g, row)
                inv, p_sink = _softmax_keys_major(
                    _dot(kw, (qt * ATTN_SCALE).astype(BF16)), bias, variant, _sink_row(sink_ref, g), e_bg)
                ot = _dot_tn(vw, e_bg[...]) * inv
                outs.append(ot)
                dot_ = _heads_to_lanes(d_attn, g, row)
                delta = jnp.sum(dot_ * ot, axis=0, keepdims=True)
                dpt = _dot(vw, dot_.astype(BF16))
                for k in range(WINDOW_KEYS // KEY_CHUNK):
                    rw = slice(k * KEY_CHUNK, (k + 1) * KEY_CHUNK)
                    ds_bg[rw, :] = (e_bg[rw, :].astype(F32) * (dpt[rw] - delta)).astype(BF16)
                sink_part = p_sink * delta
                for j in range(4):
                    h = 4 * g + j
                    gs_ref[h:h + 1, :] -= jnp.sum(sink_part[:, 128 * j:128 * j + 128])
                dqs.append(_dot_tn(kw, ds_bg[...]) * (inv * ATTN_SCALE))
                dk_w += _dot_nt(ds_bg[...], (qt * inv).astype(BF16)) * ATTN_SCALE
                dv_w += _dot_nt(e_bg[...], (dot_ * inv).astype(BF16))
            dk_acc[pl.ds(start, WINDOW_KEYS), :] += dk_w
            dv_acc[pl.ds(start, WINDOW_KEYS), :] += dv_w
            attn = _lanes_to_heads(outs[0], outs[1], row)
            dz_ref[rows, :] = (dyb_t * attn * (sig * (1.0 + z * (1.0 - sig)))).astype(BF16)
            dq = _lanes_to_heads(dqs[0], dqs[1], row)
            trows = pl.ds(start, ATTN_BLOCK)
            cs, s1, s2 = t_ref[0, trows, :], t_ref[1, trows, :], t_ref[2, trows, :]
            for blk in range(4):
                cols = slice(128 * blk, 128 * blk + 128)
                dq_ref[rows, cols] = _rope_t(dq[:, cols], cs, s1, s2).astype(BF16)

        @pl.when(i == nb // bps - 1)
        def _():
            dk = dk_acc[ATTN_BLOCK:ATTN_BLOCK + s, :]
            dkv_ref[:, 0:128] = _rope_t(dk, t_ref[0], t_ref[1], t_ref[2]).astype(BF16)
            dkv_ref[:, 128:256] = dv_acc[ATTN_BLOCK:ATTN_BLOCK + s, :].astype(BF16)

    tq = bps * ATTN_BLOCK
    tile = _rows(tq, 512)
    return pl.pallas_call(
        body, name="attn_bwd", grid=(s // tq,),
        out_shape=[jax.ShapeDtypeStruct((s, 512), BF16), jax.ShapeDtypeStruct((s, 512), BF16),
                   jax.ShapeDtypeStruct((s, 256), BF16), jax.ShapeDtypeStruct((8, 128), F32)],
        in_specs=[pl.BlockSpec(memory_space=pltpu.SMEM), tile, tile, tile, _full((s, 256)), _full((3, s, 128))],
        out_specs=[tile, tile, _full((s, 256)), _full((8, 128))],
        scratch_shapes=[pltpu.VMEM((s + 2 * ATTN_BLOCK, 128), BF16)] * 2
        + [pltpu.VMEM((s + 2 * ATTN_BLOCK, 128), F32)] * 2
        + [pltpu.VMEM((3, WINDOW_KEYS, STACKED), F32)]
        + [pltpu.VMEM((2 * bps, WINDOW_KEYS, STACKED), BF16)] * 2,
        compiler_params=_params(48),
    )(sink, pq, pbz, dyb, pkv, tabs)


def _mem_attn_bwd(pmq, pmz, mkv, dym):
    s = pmq.shape[0]
    m = mkv.shape[0]
    tm = min(512, s)

    def body(q_ref, z_ref, d_ref, mk_ref, mv_ref, dq_ref, dz_ref, dmkv_ref):
        @pl.when(pl.program_id(0) == 0)
        def _():
            dmkv_ref[...] = jnp.zeros_like(dmkv_ref)

        z = z_ref[...].astype(F32)
        sig = _sigmoid(z)
        dym_t = d_ref[...].astype(F32)
        d_attn = dym_t * (z * sig)
        dsilu = sig * (1.0 + z * (1.0 - sig))
        for h in range(MEM_HEADS):
            cols = slice(128 * h, 128 * h + 128)
            q, mk, mv = q_ref[:, cols], mk_ref[:, cols], mv_ref[:, cols]
            pt = _mem_softmax_t(q, mk)
            pb = pt.astype(BF16)
            o = _dot_tn(pb, mv)
            dob = d_attn[:, cols].astype(BF16)
            dpt = _dot_nt(mv, dob)
            dst = (pt * (dpt - jnp.sum(pt * dpt, axis=0, keepdims=True))).astype(BF16)
            dq_ref[:, cols] = (_dot_tn(dst, mk) * MEM_SCALE).astype(BF16)
            dz_ref[:, cols] = (dym_t[:, cols] * o * dsilu[:, cols]).astype(BF16)
            dmkv_ref[:, cols] += _dot(dst, q) * MEM_SCALE
            dmkv_ref[:, 512 + 128 * h:512 + 128 * h + 128] += _dot(pb, dob)

    return pl.pallas_call(
        body, name="mem_attn_bwd", grid=(s // tm,),
        out_shape=[jax.ShapeDtypeStruct((s, 512), BF16), jax.ShapeDtypeStruct((s, 512), BF16),
                   jax.ShapeDtypeStruct((m, D_MODEL), F32)],
        in_specs=[_rows(tm, 512), _rows(tm, 512), _rows(tm, 512), pl.BlockSpec((m, 512), lambda i: (0, 0)),
                  pl.BlockSpec((m, 512), lambda i: (0, 1))],
        out_specs=[_rows(tm, 512), _rows(tm, 512), _full((m, D_MODEL))],
        compiler_params=_params(20),
    )(pmq, pmz, dym, mkv, mkv)


def _mem_kv_bwd(mem, g_mem, mn, dmkv, w_mkv):
    m = mem.shape[0]

    def body(mem_ref, g_ref, mn_ref, d_ref, w_ref, gw_ref, gg_ref):
        db = d_ref[...].astype(BF16)
        gw_ref[...] = _dot_tn(mn_ref[...], db).astype(BF16)
        d_mn = _dot_nt(db, w_ref[...])
        xf = mem_ref[...]
        r = lax.rsqrt(jnp.mean(xf * xf, axis=-1, keepdims=True) + EPS)
        gg_ref[...] = jnp.sum(d_mn * (xf * r), axis=0, keepdims=True)

    return pl.pallas_call(
        body, name="mem_kv_bwd", grid=(1,),
        out_shape=[jax.ShapeDtypeStruct((D_MODEL, D_MODEL), BF16), jax.ShapeDtypeStruct((1, D_MODEL), F32)],
        in_specs=[_full((m, D_MODEL)), _full((1, D_MODEL)), _full((m, D_MODEL)), _full((m, D_MODEL)),
                  _full((D_MODEL, D_MODEL))],
        out_specs=[_full((D_MODEL, D_MODEL)), _full((1, D_MODEL))],
        compiler_params=_params(16),
    )(mem, g_mem, mn, dmkv, w_mkv)


def _dh_bwd(dparts, x, dy, g_pre, w_int):
    s = x.shape[0]
    tm = min(256, s)

    def body(*refs):
        d_refs = refs[:7]
        x_ref, dy_ref, g_ref, w_hbm, gx_ref, gg_ref, w_vm, sems = refs[7:]
        _load_once([(w_hbm, w_vm)], sems)

        @pl.when(pl.program_id(0) == 0)
        def _():
            gg_ref[...] = jnp.zeros_like(gg_ref)

        d_h = jnp.zeros((tm, D_MODEL), F32)
        for d_ref, (r0, width) in zip(d_refs, SEGS):
            for c0 in range(0, width, 512):
                cw = min(512, width - c0)
                d_h += _dot(d_ref[:, c0:c0 + cw], w_vm[r0 + c0:r0 + c0 + cw, :])
        xf = x_ref[...]
        r = lax.rsqrt(jnp.mean(xf * xf, axis=-1, keepdims=True) + EPS)
        xn = xf * r
        a = d_h * g_ref[...]
        gx_ref[...] = r * (a - xn * jnp.mean(a * xn, axis=-1, keepdims=True)) + dy_ref[...]
        gg_ref[...] += jnp.sum(d_h * xn, axis=0, keepdims=True)

    return pl.pallas_call(
        body, name="dh_bwd", grid=(s // tm,),
        out_shape=[jax.ShapeDtypeStruct((s, D_MODEL), F32), jax.ShapeDtypeStruct((1, D_MODEL), F32)],
        in_specs=[_rows(tm, w) for _, w in SEGS] + [_rows(tm, D_MODEL), _rows(tm, D_MODEL), _full((1, D_MODEL)), ANY],
        out_specs=[_rows(tm, D_MODEL), _full((1, D_MODEL))],
        scratch_shapes=[pltpu.VMEM((IN_WIDTH, D_MODEL), BF16), pltpu.SemaphoreType.DMA((1,))],
        compiler_params=_params(38),
    )(*dparts, x, dy, g_pre, w_int)


def _gw_in(dparts, h):
    s = h.shape[0]
    tn = 256
    starts, counts = [], []
    for r0, width in SEGS:
        starts.append(r0 // tn)
        counts.append(width // tn)

    def body(*refs):
        d_refs = refs[:7]
        h_hbm, o_ref, h_vm, sems = refs[7:]
        _load_once([(h_hbm, h_vm)], sems)
        j = pl.program_id(0)
        for d_ref, st, cnt in zip(d_refs, starts, counts):
            @pl.when((j >= st) & (j < st + cnt))
            def _(d_ref=d_ref):
                o_ref[...] = _dot_tn(d_ref[...], h_vm[...]).astype(BF16)

    def seg_spec(st, cnt):
        return pl.BlockSpec((s, tn), lambda j: (0, jnp.clip(j - st, 0, cnt - 1)))

    return pl.pallas_call(
        body, name="gw_in", grid=(IN_WIDTH // tn,),
        out_shape=jax.ShapeDtypeStruct((IN_WIDTH, D_MODEL), BF16),
        in_specs=[seg_spec(st, cnt) for st, cnt in zip(starts, counts)] + [ANY],
        out_specs=pl.BlockSpec((tn, D_MODEL), lambda j: (j, 0)),
        scratch_shapes=[pltpu.VMEM((s, D_MODEL), BF16), pltpu.SemaphoreType.DMA((1,))],
        compiler_params=_params(44),
    )(*dparts, h)


def _adamw_math(w, g, m, v):
    m2 = ADAM_B1 * m + (1.0 - ADAM_B1) * g
    v2 = ADAM_B2 * v + (1.0 - ADAM_B2) * (g * g)
    m_hat = m2 / (1.0 - ADAM_B1 ** ADAM_STEP)
    v_hat = v2 / (1.0 - ADAM_B2 ** ADAM_STEP)
    delta = -ADAM_LR * (m_hat / (jnp.sqrt(v_hat) + ADAM_EPS) + ADAM_WD * w)
    return delta, m2, v2


def _sum_adamw(own, land, chip, block, w, m, v, name, tiles=1):
    r, c = w.shape
    rt = r // tiles

    def body(c_ref, own_ref, l1_ref, l2_ref, l3_ref, w_ref, m_ref, v_ref, g_ref, d_ref, m2_ref, v2_ref):
        g = own_ref[...].astype(F32)
        for l_ref in (l1_ref, l2_ref, l3_ref):
            g += l_ref[...].astype(F32)
        g_ref[...] = g
        d_ref[...], m2_ref[...], v2_ref[...] = _adamw_math(w_ref[...], g, m_ref[...], v_ref[...])

    def share(k):
        return pl.BlockSpec((None, rt, c), lambda i, c_ref: (jnp.bitwise_xor(c_ref[0], k), block * tiles + i, 0))

    spec = pl.BlockSpec((rt, c), lambda i, c_ref: (i, 0))
    grid_spec = pltpu.PrefetchScalarGridSpec(
        num_scalar_prefetch=1, grid=(tiles,),
        in_specs=[share(0), share(1), share(2), share(3)] + [spec] * 3, out_specs=[spec] * 4)
    return pl.pallas_call(
        body, name=name, grid_spec=grid_spec,
        out_shape=[jax.ShapeDtypeStruct((r, c), F32)] * 4,
        compiler_params=_params(48),
    )(chip, own, land, land, land, w, m, v)


def _sum_adamw_group(items, chip, name):
    k = len(items)

    def body(c_ref, *refs):
        shares, wmv, outs = refs[:4 * k], refs[4 * k:7 * k], refs[7 * k:]
        for j in range(k):
            g = shares[4 * j][...].astype(F32)
            for l_ref in shares[4 * j + 1:4 * j + 4]:
                g += l_ref[...].astype(F32)
            outs[4 * j][...] = g
            outs[4 * j + 1][...], outs[4 * j + 2][...], outs[4 * j + 3][...] = _adamw_math(
                wmv[3 * j][...], g, wmv[3 * j + 1][...], wmv[3 * j + 2][...])

    def share(shape, block, q):
        return pl.BlockSpec((None,) + shape, lambda i, c_ref: (jnp.bitwise_xor(c_ref[0], q), block, 0))

    in_specs, args = [], []
    for own, land, block, w, m, v in items:
        in_specs += [share(w.shape, block, q) for q in range(4)]
        args += [own, land, land, land]
    for own, land, block, w, m, v in items:
        in_specs += [pl.BlockSpec(w.shape, lambda i, c_ref: (0, 0))] * 3
        args += [w, m, v]
    out_specs = [pl.BlockSpec(w.shape, lambda i, c_ref: (0, 0)) for _, _, _, w, _, _ in items for _ in range(4)]
    res = pl.pallas_call(
        body, name=name,
        grid_spec=pltpu.PrefetchScalarGridSpec(num_scalar_prefetch=1, grid=(1,), in_specs=in_specs,
                                               out_specs=out_specs),
        out_shape=[jax.ShapeDtypeStruct(w.shape, F32) for _, _, _, w, _, _ in items for _ in range(4)],
        compiler_params=_params(12),
    )(chip, *args)
    return [res[4 * j:4 * j + 4] for j in range(k)]


def _small_step(parts, ws, ms, vs):
    def exchange(gpre_ref, gconv_ref, gsink_ref, gmem_ref, gpost_ref, loss_ref, tot_ref,
                 pack, gathered, send_sems, recv_sems):
        x, y, c = _my_place()
        me_idx = 4 * x + 2 * y + c

        lane = lax.broadcasted_iota(jnp.int32, (1, 128), 1)
        sink_row = jnp.zeros((1, 128), F32)
        for h in range(8):
            sink_row = jnp.where(lane == h, gsink_ref[h:h + 1, :], sink_row)
        pack[...] = jnp.zeros_like(pack)
        pack[0:1, :] = gpre_ref[...]
        pack[1:2, :] = gmem_ref[...]
        pack[2:3, :] = gpost_ref[...]
        pack[3:6, 0:512] = gconv_ref[0:3, :]
        pack[6:7, 0:128] = sink_row
        pack[7:8, 0:128] = loss_ref[0:1, :]

        flips = [(0, 0, 1), (0, 1, 0), (1, 0, 0), (0, 1, 1), (1, 0, 1), (1, 1, 0), (1, 1, 1)]
        cps = []
        for k, (fx, fy, fc) in enumerate(flips):
            peer = ((1 - x) if fx else x, (1 - y) if fy else y, (1 - c) if fc else c)
            cps.append(pltpu.make_async_remote_copy(
                src_ref=pack, dst_ref=gathered.at[me_idx], send_sem=send_sems.at[k], recv_sem=recv_sems.at[k],
                device_id=peer, device_id_type=MESH))
        for cp in cps:
            cp.start()
        gathered[me_idx] = pack[...]
        for cp in cps:
            cp.wait_recv()
        for cp in cps:
            cp.wait_send()
        tot = gathered[0]
        for d in range(1, N_DEV):
            tot = tot + gathered[d]
        tot_ref[...] = tot

    tot = pl.pallas_call(
        exchange, name="small_exchange", grid=(1,),
        out_shape=jax.ShapeDtypeStruct((8, D_MODEL), F32),
        in_specs=[_full(p.shape) for p in parts], out_specs=_full((8, D_MODEL)),
        scratch_shapes=[pltpu.VMEM((8, D_MODEL), F32), pltpu.VMEM((N_DEV, 8, D_MODEL), F32),
                        pltpu.SemaphoreType.DMA((N_PEERS,)), pltpu.SemaphoreType.DMA((N_PEERS,))],
    )(*parts)

    def apply(tot_ref, *refs):
        w_refs, m_refs, v_refs = refs[0:5], refs[5:10], refs[10:15]
        loss_out = refs[15]
        g_outs, d_outs, m_outs, v_outs = refs[16:21], refs[21:26], refs[26:31], refs[31:36]
        x, y, c = _my_place()
        tot = tot_ref[...]
        conv = pltpu.roll(tot[:, 0:512], (512 - 64 * (4 * x + 2 * y + c)) % 512, 1)[3:6, 0:64]
        grads = (tot[0:1, :], conv, tot[6:7, 0:8], tot[1:2, :], tot[2:3, :])
        loss_out[...] = tot[7:8, 0:128]
        for j in range(5):
            g_outs[j][...] = grads[j]
            d_outs[j][...], m_outs[j][...], v_outs[j][...] = _adamw_math(
                w_refs[j][...], grads[j], m_refs[j][...], v_refs[j][...])

    specs = [_full(w.shape) for w in ws]
    res = pl.pallas_call(
        apply, name="small_apply", grid=(1,),
        out_shape=[jax.ShapeDtypeStruct((1, 128), F32)] + [jax.ShapeDtypeStruct(w.shape, F32) for w in ws] * 4,
        in_specs=[_full((8, D_MODEL))] + specs * 3,
        out_specs=[_full((1, 128))] + specs * 4,
    )(tot, *ws, *ms, *vs)
    return res[0], res[1:6], res[6:11], res[11:16], res[16:21]


def kernel(x, mem, g_pre, w_in, w_conv, attn_sink, g_mem, w_mem_kv, w_up_a, w_up_b, w_up_m, w_out, g_post, loss_target, m_g_pre, m_w_in, m_w_conv, m_attn_sink, m_g_mem, m_w_mem_kv, m_w_up_a, m_w_up_b, m_w_up_m, m_w_out, m_g_post, v_g_pre, v_w_in, v_w_conv, v_attn_sink, v_g_mem, v_w_mem_kv, v_w_up_a, v_w_up_b, v_w_up_m, v_w_out, v_g_post):
    s = x.shape[1]
    x2, mem2, tgt2 = x[0], mem[0], loss_target[0]
    me = 4 * lax.axis_index("x") + 2 * lax.axis_index("y") + lax.axis_index("c")

    w_conv_loc = jnp.zeros((8, 128), F32).at[:3, :64].set(w_conv[0])
    w_int_g, w_conv_g, tabs, w_mkv_loc, w_out_loc, w_up_loc = _all_gather(
        [w_in[0].T.astype(BF16), w_conv_loc], "gather_w_in",
        splits=[[(112 * k, 112) for k in range(7)] + [(784, 144)], [(0, 8)]],
        side=_gather_side(s, w_mem_kv[0], w_out[0], (w_up_a[0], w_up_b[0], w_up_m[0])))
    w_int = w_int_g.reshape(IN_WIDTH, D_MODEL)
    w_conv_f = w_conv_g[:, :3, :64].transpose(1, 0, 2).reshape(3, 512)
    late = _gather_start([w_mkv_loc, w_out_loc, w_up_loc], me, "gather_late_start")
    sink = attn_sink[0]

    h, pa, pq, pkv, pbz, pmq, pmz, pg = _proj_fwd(x2, g_pre + late[4][0:1, 0:1], w_int, tabs)
    ya = _conv_fwd(pa, w_conv_f)
    yb = _attn_fwd(pq, pkv, pbz, sink)
    w_mkv_g, w_out_g, w_up_g = _gather_wait(*late[:4], yb, "gather_late_wait")
    w_mkv = w_mkv_g.reshape(D_MODEL, D_MODEL)
    w_out_f = w_out_g.reshape(D_MODEL, D_MODEL)
    mn, mkv = _mem_kv_fwd(mem2, g_mem, w_mkv)
    ym = _mem_attn_fwd(pmq, pmz, mkv)
    dg, dya, dyb, dym, dy, loss_p, gg_post, mb, dob, du = _mid(ya, yb, ym, pg, x2, tgt2, g_post, w_up_g, w_out_f)
    gw_out, gw_up = _gw_mid(mb, dob, (ya, yb, ym), du)

    core = lax.axis_index("c").astype(jnp.int32).reshape(1)
    chip = (2 * lax.axis_index("x") + lax.axis_index("y")).astype(jnp.int32).reshape(1)

    def exchange_start(shares, tag):
        from_sibling = _sibling_exchange(shares, "grads_to_sibling_" + tag)
        chip_shares = _pair_add(shares, from_sibling, core, "grads_pair_add_" + tag)
        return _chip_exchange_start(chip_shares, "grads_to_chips_start_" + tag)

    dmq, dmz, dmkv = _mem_attn_bwd(pmq, pmz, mkv, dym)
    gw_mkv, gg_mem = _mem_kv_bwd(mem2, g_mem, mn, dmkv, w_mkv)
    shares1 = [gw_mkv.reshape(N_DEV, 128, D_MODEL), gw_out.reshape(N_DEV, 128, D_MODEL), gw_up]
    sib = _split_start(_sibling_copies, N_CHIPS, shares1,
                       [lax.empty((N_CHIPS,) + a.shape[1:], a.dtype) for a in shares1], "grads_to_sibling_small_start")
    da, gw_conv = _conv_bwd(pa, dya, w_conv_f + sib[4][0:1, 0:1])
    shares1, from_sibling = _split_wait(_sibling_copies, N_CHIPS, *sib[:4], da, "grads_to_sibling_small_wait")
    send1, recv1, srcs1, lands1, token1 = _chip_exchange_start(
        _pair_add(shares1, from_sibling, core, "grads_pair_add_small"), "grads_to_chips_start_small")
    dq, dbz, dkv, g_sink = _attn_bwd(pq, pkv, pbz, dyb, sink + token1[0, 0], tabs)
    dparts = (da, dq, dkv, dbz, dmq, dmz, dg)
    gw_int = _gw_in(dparts, h)
    send2, recv2, srcs2, lands2, token2 = exchange_start([gw_int.reshape(N_DEV, SHARD_IN, D_MODEL)], "w_in")
    grad_x, gg_pre = _dh_bwd(dparts, x2, dy, g_pre + token2[0:1, 0:1], w_int)
    (o_mkv, o_out, o_up, o_int), (l_mkv, l_out, l_up, l_int) = _chip_exchange_wait(
        send1 + send2, recv1 + recv2, srcs1 + srcs2, lands1 + lands2, grad_x, "grads_to_chips_wait")

    loss_row, small_g, sd, sm, sv = _small_step(
        (gg_pre, gw_conv, g_sink, gg_mem, gg_post, loss_p),
        [g_pre, w_conv[0], attn_sink, g_mem, g_post],
        [m_g_pre, m_w_conv[0], m_attn_sink, m_g_mem, m_g_post],
        [v_g_pre, v_w_conv[0], v_attn_sink, v_g_mem, v_g_post])
    loss = loss_row[0, 0]
    g_g_pre, g_conv, g_sink_tot, g_g_mem, g_g_post = small_g

    g_w_in, d_w_in, nm_w_in, nv_w_in = (t.T for t in _sum_adamw(
        o_int, l_int, chip, 0, w_in[0].T, m_w_in[0].T, v_w_in[0].T, "adamw_w_in", tiles=2))
    (g_mkv, d_mkv, nm_mkv, nv_mkv), (g_out, d_out, nm_out, nv_out), *up = _sum_adamw_group(
        [(o_mkv, l_mkv, 0, w_mem_kv[0], m_w_mem_kv[0], v_w_mem_kv[0]),
         (o_out, l_out, 0, w_out[0], m_w_out[0], v_w_out[0]),
         (o_up, l_up, 0, w_up_a[0], m_w_up_a[0], v_w_up_a[0]),
         (o_up, l_up, 1, w_up_b[0], m_w_up_b[0], v_w_up_b[0]),
         (o_up, l_up, 2, w_up_m[0], m_w_up_m[0], v_w_up_m[0])], chip, "adamw_mid_weights")

    def lead(a):
        return a[None]

    grads = [g_g_pre, lead(g_w_in), lead(g_conv), g_sink_tot, g_g_mem, lead(g_mkv), lead(up[0][0]),
             lead(up[1][0]), lead(up[2][0]), lead(g_out), g_g_post]

    def assemble(small, big_in, big_mkv, big_up, big_out):
        return [small[0], lead(big_in), lead(small[1]), small[2], small[3], lead(big_mkv), lead(big_up[0]),
                lead(big_up[1]), lead(big_up[2]), lead(big_out), small[4]]

    deltas = assemble(sd, d_w_in, d_mkv, [u[1] for u in up], d_out)
    new_m = assemble(sm, nm_w_in, nm_mkv, [u[2] for u in up], nm_out)
    new_v = assemble(sv, nv_w_in, nv_mkv, [u[3] for u in up], nv_out)
    return (loss, grad_x[None], *grads, *deltas, *new_m, *new_v)
```

```python
import functools

import jax
import jax.numpy as jnp
from jax import lax
from jax.experimental import pallas as pl
from jax.experimental.pallas import tpu as pltpu

F32 = jnp.float32
BF16 = jnp.bfloat16
MESH = pl.DeviceIdType.MESH

N_DEV = 8
D_MODEL = 1024
EPS = 1e-6
ROPE_THETA = 500000.0
ROT_DIM = 16
HEAD_DIM = 64
ATTN_BLOCK = 128
MEM_HEADS = 4
MEM_HEAD_DIM = 128
ATTN_SCALE = HEAD_DIM ** -0.5
MEM_SCALE = MEM_HEAD_DIM ** -0.5

ADAM_LR = 0.001
ADAM_B1 = 0.9
ADAM_B2 = 0.999
ADAM_EPS = 1e-08
ADAM_WD = 0.01
ADAM_STEP = 10

SEG_A = (0, 2048)
SEG_BQ = (2048, 512)
SEG_BKV = (2560, 256)
SEG_BZ = (2816, 512)
SEG_MQ = (3328, 512)
SEG_MZ = (3840, 512)
SEG_G = (4352, 3072)
SEGS = (SEG_A, SEG_BQ, SEG_BKV, SEG_BZ, SEG_MQ, SEG_MZ, SEG_G)
IN_WIDTH = 7424
SHARD_IN = IN_WIDTH // N_DEV

V7X_VMEM_BYTES = 64 * 1024 * 1024
ANY = pl.BlockSpec(memory_space=pl.ANY)


def _params(vmem_mb):
    vmem_mb = 60
    assert vmem_mb * 1024 * 1024 < V7X_VMEM_BYTES
    return pltpu.CompilerParams(dimension_semantics=("arbitrary",), vmem_limit_bytes=vmem_mb * 1024 * 1024)


def _full(shape):
    zeros = (0,) * len(shape)
    return pl.BlockSpec(shape, lambda i: zeros)


def _rows(tm, width):
    return pl.BlockSpec((tm, width), lambda i: (i, 0))


def _dot(a, b):
    return jnp.dot(a, b, preferred_element_type=F32)


def _dot_nt(a, b):
    return lax.dot_general(a, b, (((1,), (1,)), ((), ())), preferred_element_type=F32)


def _dot_tn(a, b):
    return lax.dot_general(a, b, (((0,), (0,)), ((), ())), preferred_element_type=F32)


def _sigmoid(z):
    return 1.0 / (1.0 + jnp.exp(-z))


def _rope(t, cs, s1, s2):
    return t * cs + pltpu.roll(t, 120, 1) * s1 + pltpu.roll(t, 8, 1) * s2


def _rope_t(d, cs, s1, s2):
    return d * cs + pltpu.roll(d * s1, 8, 1) + pltpu.roll(d * s2, 120, 1)


def _gather_side(s, w_mkv, w_out, w_ups):
    half = ROT_DIM // 2
    inv_freq = jnp.power(jnp.float32(ROPE_THETA), -jnp.arange(half, dtype=F32) * (2.0 / ROT_DIM))
    freq_row = jnp.tile(jnp.concatenate([inv_freq, inv_freq, jnp.zeros((HEAD_DIM - ROT_DIM,), F32)]), 2)[None, :]

    def fn(in_refs, out_refs):
        f_ref, mkv_ref, out_ref, *up_refs = in_refs
        t_ref, mkv_bf, out_bf, up_bf = out_refs
        mkv_bf[...] = mkv_ref[...].astype(BF16)
        out_bf[...] = out_ref[...].astype(BF16)
        for k, up_ref in enumerate(up_refs):
            up_bf[512 * k:512 * k + 512, :] = up_ref[...].astype(BF16)
        pos = lax.broadcasted_iota(jnp.int32, (s, 128), 0).astype(F32)
        d = lax.broadcasted_iota(jnp.int32, (s, 128), 1) & (HEAD_DIM - 1)
        ang = pos * f_ref[...]
        cos, sin = jnp.cos(ang), jnp.sin(ang)
        lo, hi = d < half, (d >= half) & (d < ROT_DIM)
        t_ref[0] = jnp.where(lo | hi, cos, 1.0)
        t_ref[1] = jnp.where(lo, -sin, 0.0)
        t_ref[2] = jnp.where(hi, sin, 0.0)

    return ([freq_row, w_mkv, w_out, *w_ups],
            [jax.ShapeDtypeStruct((3, s, 128), F32), jax.ShapeDtypeStruct(w_mkv.shape, BF16),
             jax.ShapeDtypeStruct(w_out.shape, BF16), jax.ShapeDtypeStruct((1536, 128), BF16)], fn)


def _load_once(pairs, sems):
    @pl.when(pl.program_id(0) == 0)
    def _():
        cps = [pltpu.make_async_copy(src, dst, sems.at[k]) for k, (src, dst) in enumerate(pairs)]
        for cp in cps:
            cp.start()
        for cp in cps:
            cp.wait()


def _my_place():
    x, y, c = lax.axis_index("x"), lax.axis_index("y"), lax.axis_index("c")
    return x, y, c


def _all_gather(arrs, name, splits=None, side=None):
    n = len(arrs)
    if splits is None:
        splits = [[(0, a.shape[0])] for a in arrs]
    pieces = [(a, r0, rn) for a in range(n) for r0, rn in splits[a]]
    n_p = len(pieces)
    side_in, side_out, side_fn = side if side is not None else ((), (), None)
    m, q = len(side_in), len(side_out)

    def body(*refs):
        ins, outs = refs[:n], refs[n + m:2 * n + m]
        send_sems, recv_sems, local_sems = refs[2 * n + m + q:]
        x, y, c = _my_place()
        me, sibling = (x, y, c), (x, y, 1 - c)

        def route(core):
            first = (jnp.bitwise_xor(x, 1 - core), jnp.bitwise_xor(y, core), core)
            second = (jnp.bitwise_xor(x, core), jnp.bitwise_xor(y, 1 - core), core)
            return first, second, (1 - x, 1 - y, core)

        def idx(px, py, pc):
            return 4 * px + 2 * py + pc

        def copy(p, k, block, to, own=False):
            a, r0, rn = pieces[p]
            dst = outs[a].at[idx(*block), pl.ds(r0, rn)]
            return pltpu.make_async_remote_copy(
                src_ref=ins[a].at[pl.ds(r0, rn)] if own else dst, dst_ref=dst,
                send_sem=send_sems.at[p * 7 + k], recv_sem=recv_sems.at[p * 7 + k],
                device_id=to, device_id_type=MESH)

        nbr1, nbr2, diag = route(c)
        mine = [pltpu.make_async_copy(ins[a], outs[a].at[idx(*me)], local_sems.at[a]) for a in range(n)]
        for cp in mine:
            cp.start()
        sent = []
        for p in range(n_p):
            for k, to in enumerate((sibling, nbr1, nbr2)):
                sent.append(copy(p, k, me, to, own=True))
        for cp in sent:
            cp.start()
        if side_fn is not None:
            side_fn(refs[n:n + m], refs[2 * n + m:2 * n + m + q])
        for k_in, block, onward in ((1, nbr1, ((3, nbr2), (4, sibling))), (2, nbr2, ((5, sibling),)),
                                    (3, diag, ((6, sibling),))):
            for p in range(n_p):
                copy(p, k_in, block, me).wait_recv()
                for k_out, to in onward:
                    cp = copy(p, k_out, block, to)
                    cp.start()
                    sent.append(cp)
        s1, s2, sd = route(1 - c)
        for k_in, block in ((0, sibling), (4, s1), (5, s2), (6, sd)):
            for p in range(n_p):
                copy(p, k_in, block, me).wait_recv()
        for cp in sent:
            cp.wait_send()
        for cp in mine:
            cp.wait()

    return pl.pallas_call(
        body, name=name,
        out_shape=[jax.ShapeDtypeStruct((N_DEV,) + a.shape, a.dtype) for a in arrs] + list(side_out),
        in_specs=[ANY] * n + [pl.BlockSpec(memory_space=pltpu.VMEM)] * m,
        out_specs=[ANY] * n + [pl.BlockSpec(memory_space=pltpu.VMEM)] * q,
        scratch_shapes=[pltpu.SemaphoreType.DMA((7 * n_p,)), pltpu.SemaphoreType.DMA((7 * n_p,)),
                        pltpu.SemaphoreType.DMA((n,))],
        compiler_params=pltpu.CompilerParams(vmem_limit_bytes=32 * 1024 * 1024),
    )(*arrs, *side_in)


N_CHIPS = 4


def _sibling_exchange(arrs, name):
    n = len(arrs)

    def body(*refs):
        ins, outs = refs[:n], refs[n:2 * n]
        send_sems, recv_sems = refs[2 * n:]
        x, y, c = _my_place()
        sibling = (x, y, 1 - c)

        def copy(a, j):
            return pltpu.make_async_remote_copy(
                src_ref=ins[a].at[2 * j + (1 - c)], dst_ref=outs[a].at[j],
                send_sem=send_sems.at[a * N_CHIPS + j], recv_sem=recv_sems.at[a * N_CHIPS + j],
                device_id=sibling, device_id_type=MESH)

        cps = [copy(a, j) for j in range(N_CHIPS) for a in range(n)]
        for cp in cps:
            cp.start()
        for cp in cps:
            cp.wait_recv()
        for cp in cps:
            cp.wait_send()

    return pl.pallas_call(
        body, name=name,
        out_shape=[jax.ShapeDtypeStruct((N_CHIPS,) + a.shape[1:], a.dtype) for a in arrs],
        in_specs=[ANY] * n, out_specs=[ANY] * n,
        scratch_shapes=[pltpu.SemaphoreType.DMA((N_CHIPS * n,)), pltpu.SemaphoreType.DMA((N_CHIPS * n,))],
    )(*arrs)


def _sibling_copies(srcs, lands, send_sems, recv_sems):
    x, y, c = _my_place()
    cps = []
    for j in range(N_CHIPS):
        for a in range(len(srcs)):
            k = a * N_CHIPS + j
            cps.append(pltpu.make_async_remote_copy(
                src_ref=srcs[a].at[2 * j + (1 - c)], dst_ref=lands[a].at[j], send_sem=send_sems[k],
                recv_sem=recv_sems[k], device_id=(x, y, 1 - c), device_id_type=MESH))
    return cps


def _pair_add(mine, recv, core, name):
    n = len(mine)

    def body(c_ref, *refs):
        for a in range(n):
            refs[2 * n + a][...] = (refs[a][...].astype(F32) + refs[n + a][...].astype(F32)).astype(BF16)

    def blk(a):
        return (None,) + a.shape[1:]

    grid_spec = pltpu.PrefetchScalarGridSpec(
        num_scalar_prefetch=1, grid=(N_CHIPS,),
        in_specs=[pl.BlockSpec(blk(a), lambda j, c_ref: (2 * j + c_ref[0], 0, 0)) for a in mine]
        + [pl.BlockSpec(blk(a), lambda j, c_ref: (j, 0, 0)) for a in recv],
        out_specs=[pl.BlockSpec(blk(a), lambda j, c_ref: (j, 0, 0)) for a in recv])
    return pl.pallas_call(
        body, name=name, grid_spec=grid_spec,
        out_shape=[jax.ShapeDtypeStruct(a.shape, BF16) for a in recv],
        compiler_params=_params(24),
    )(core, *mine, *recv)


HBM = pl.BlockSpec(memory_space=pltpu.HBM)
SEM = pl.BlockSpec(memory_space=pltpu.SEMAPHORE)
N_PEER_CHIPS = 3


def _chip_copies(srcs, lands, send_sems, recv_sems):
    x, y, c = _my_place()
    my_chip = 2 * x + y
    peers = [(x, 1 - y), (1 - x, y), (1 - x, 1 - y)]
    cps = []
    for k, (px, py) in enumerate(peers):
        for a in range(len(srcs)):
            j = a * N_PEER_CHIPS + k
            cps.append(pltpu.make_async_remote_copy(
                src_ref=srcs[a].at[2 * px + py], dst_ref=lands[a].at[my_chip],
                send_sem=send_sems[j], recv_sem=recv_sems[j],
                device_id=(px, py, c), device_id_type=MESH))
    return cps


N_PEERS = N_DEV - 1


def _gather_copies(srcs, lands, send_sems, recv_sems):
    x, y, c = _my_place()
    me_idx = 4 * x + 2 * y + c
    flips = [(0, 0, 1), (0, 1, 0), (1, 0, 0), (0, 1, 1), (1, 0, 1), (1, 1, 0), (1, 1, 1)]
    cps = []
    for k, (fx, fy, fc) in enumerate(flips):
        peer = ((1 - x) if fx else x, (1 - y) if fy else y, (1 - c) if fc else c)
        for a in range(len(srcs)):
            j = a * N_PEERS + k
            cps.append(pltpu.make_async_remote_copy(
                src_ref=srcs[a], dst_ref=lands[a].at[me_idx], send_sem=send_sems[j], recv_sem=recv_sems[j],
                device_id=peer, device_id_type=MESH))
    return cps


def _split_start(copies, per_array, arrs, lands, name):
    arrs, lands = list(arrs), list(lands)
    n = len(arrs)
    k = n * per_array

    def body(*refs):
        srcs, land_refs = refs[:n], refs[n:2 * n]
        send_sems, recv_sems = refs[2 * n:2 * n + k], refs[2 * n + k:2 * n + 2 * k]
        token = refs[-1]
        for cp in copies(srcs, land_refs, send_sems, recv_sems):
            cp.start()
        token[...] = jnp.zeros_like(token)

    hbm_arrs = [pltpu.with_memory_space_constraint(a, pltpu.HBM) for a in arrs]
    lands = [pltpu.with_memory_space_constraint(a, pltpu.HBM) for a in lands]
    res = pl.pallas_call(
        body, name=name,
        out_shape=[pltpu.SemaphoreType.DMA(())] * (2 * k) + [pltpu.HBM(a.shape, a.dtype) for a in arrs + lands]
        + [jax.ShapeDtypeStruct((8, 128), F32)],
        in_specs=[HBM] * (2 * n),
        out_specs=[SEM] * (2 * k) + [HBM] * (2 * n) + [pl.BlockSpec(memory_space=pltpu.VMEM)],
        input_output_aliases={a: 2 * k + a for a in range(2 * n)},
        compiler_params=pltpu.CompilerParams(has_side_effects=pltpu.SideEffectType.DATAFLOW_SIDE_EFFECTING),
    )(*hbm_arrs, *lands)
    return res[:k], res[k:2 * k], res[2 * k:2 * k + n], res[2 * k + n:2 * k + 2 * n], res[-1]


def _split_wait(copies, per_array, send_sems, recv_sems, srcs, lands, after, name):
    n = len(srcs)
    k = n * per_array

    def body(*refs):
        src_refs, land_refs = refs[:n], refs[n:2 * n]
        s_sems, r_sems = refs[2 * n:2 * n + k], refs[2 * n + k:2 * n + 2 * k]
        for cp in copies(src_refs, land_refs, s_sems, r_sems):
            cp.wait_send()
            cp.wait_recv()

    res = pl.pallas_call(
        body, name=name,
        out_shape=[pltpu.HBM(a.shape, a.dtype) for a in list(srcs) + list(lands)],
        in_specs=[HBM] * (2 * n) + [SEM] * (2 * k) + [ANY],
        out_specs=[HBM] * (2 * n),
        input_output_aliases={a: a for a in range(2 * n)},
        compiler_params=pltpu.CompilerParams(has_side_effects=pltpu.SideEffectType.DATAFLOW_SIDE_EFFECTING),
    )(*srcs, *lands, *send_sems, *recv_sems, after)
    return res[:n], res[n:]


def _chip_exchange_start(arrs, name):
    return _split_start(_chip_copies, N_PEER_CHIPS, arrs, [lax.empty(a.shape, a.dtype) for a in arrs], name)


def _chip_exchange_wait(send_sems, recv_sems, srcs, lands, after, name):
    return _split_wait(_chip_copies, N_PEER_CHIPS, send_sems, recv_sems, srcs, lands, after, name)


def _gather_start(arrs, me_idx, name):
    lands = [lax.dynamic_update_slice(lax.empty((N_DEV,) + a.shape, a.dtype), a[None], (me_idx, 0, 0)) for a in arrs]
    return _split_start(_gather_copies, N_PEERS, arrs, lands, name)


def _gather_wait(send_sems, recv_sems, srcs, lands, after, name):
    return _split_wait(_gather_copies, N_PEERS, send_sems, recv_sems, srcs, lands, after, name)[1]


def _proj_fwd(x, g_pre, w_int, tabs):
    s = x.shape[0]
    tm = min(512, s)

    def body(x_ref, g_ref, t_ref, w_hbm,
             h_ref, pa_ref, pq_ref, pkv_ref, pbz_ref, pmq_ref, pmz_ref, pg_ref, w_vm, sems):
        _load_once([(w_hbm, w_vm)], sems)
        xf = x_ref[...]
        r = lax.rsqrt(jnp.mean(xf * xf, axis=-1, keepdims=True) + EPS)
        h = ((xf * r) * g_ref[...]).astype(BF16)
        h_ref[...] = h
        cs, s1, s2 = t_ref[0], t_ref[1], t_ref[2]

        def mm(seg, c0, width):
            return _dot_nt(h, w_vm[seg[0] + c0:seg[0] + c0 + width, :])

        for c0 in range(0, SEG_A[1], 512):
            pa_ref[:, c0:c0 + 512] = mm(SEG_A, c0, 512).astype(BF16)
        q = mm(SEG_BQ, 0, 512)
        for b in range(4):
            pq_ref[:, 128 * b:128 * b + 128] = _rope(q[:, 128 * b:128 * b + 128], cs, s1, s2).astype(BF16)
        kv = mm(SEG_BKV, 0, 256)
        pkv_ref[:, 0:128] = _rope(kv[:, 0:128], cs, s1, s2).astype(BF16)
        pkv_ref[:, 128:256] = kv[:, 128:256].astype(BF16)
        pbz_ref[...] = mm(SEG_BZ, 0, 512).astype(BF16)
        pmq_ref[...] = mm(SEG_MQ, 0, 512).astype(BF16)
        pmz_ref[...] = mm(SEG_MZ, 0, 512).astype(BF16)
        for c0 in range(0, SEG_G[1], 512):
            pg_ref[:, c0:c0 + 512] = mm(SEG_G, c0, 512).astype(BF16)

    widths = (D_MODEL, 2048, 512, 256, 512, 512, 512, 3072)
    return pl.pallas_call(
        body, name="proj_fwd", grid=(s // tm,),
        out_shape=[jax.ShapeDtypeStruct((s, w), BF16) for w in widths],
        in_specs=[_rows(tm, D_MODEL), _full((1, D_MODEL)), pl.BlockSpec((3, tm, 128), lambda i: (0, i, 0)), ANY],
        out_specs=[_rows(tm, w) for w in widths],
        scratch_shapes=[pltpu.VMEM((IN_WIDTH, D_MODEL), BF16), pltpu.SemaphoreType.DMA((1,))],
        compiler_params=_params(46),
    )(x, g_pre, tabs, w_int)


def _mem_kv_fwd(mem, g_mem, w_mkv):
    m = mem.shape[0]

    def body(mem_ref, g_ref, w_ref, mn_ref, mkv_ref):
        xf = mem_ref[...]
        r = lax.rsqrt(jnp.mean(xf * xf, axis=-1, keepdims=True) + EPS)
        mn = ((xf * r) * g_ref[...]).astype(BF16)
        mn_ref[...] = mn
        mkv_ref[...] = _dot(mn, w_ref[...]).astype(BF16)

    return pl.pallas_call(
        body, name="mem_kv_fwd", grid=(1,),
        out_shape=[jax.ShapeDtypeStruct((m, D_MODEL), BF16)] * 2,
        in_specs=[_full((m, D_MODEL)), _full((1, D_MODEL)), _full((D_MODEL, D_MODEL))],
        out_specs=[_full((m, D_MODEL))] * 2,
        compiler_params=_params(12),
    )(mem, g_mem, w_mkv)


def _halo_specs(s, tm, rows, width):
    nblk = s // rows
    prev = pl.BlockSpec((rows, width), lambda i: (jnp.maximum(i * (tm // rows) - 1, 0), 0))
    nxt = pl.BlockSpec((rows, width), lambda i: (jnp.minimum((i + 1) * (tm // rows), nblk - 1), 0))
    return prev, nxt


def _conv_common(pa, cu_prev, cu_next, w, tm):
    b, c, u, z = (pa[:, 512 * k:512 * k + 512] for k in range(4))
    cu = c * u
    row = lax.broadcasted_iota(jnp.int32, (tm, 512), 0)
    cu_m1 = jnp.where(row == 0, cu_prev, pltpu.roll(cu, 1, 0))
    cu_p1 = jnp.where(row == tm - 1, cu_next, pltpu.roll(cu, tm - 1, 0))
    y = cu_m1 * w[0:1] + cu * w[1:2] + cu_p1 * w[2:3]
    sig = _sigmoid(z)
    return b, c, u, z, cu, cu_m1, cu_p1, y, sig, row


def _conv_fwd(pa, w_conv):
    s = pa.shape[0]
    tm = min(512, s)
    nt = s // tm

    def body(pa_ref, pp_ref, pn_ref, w_ref, ya_ref):
        i = pl.program_id(0)
        prev_row = pp_ref[...].astype(F32)[15:16, :]
        next_row = pn_ref[...].astype(F32)[0:1, :]
        b, _, _, z, _, _, _, y, sig, _ = _conv_common(
            pa_ref[...].astype(F32),
            jnp.where(i == 0, 0.0, prev_row[:, 512:1024] * prev_row[:, 1024:1536]),
            jnp.where(i == nt - 1, 0.0, next_row[:, 512:1024] * next_row[:, 1024:1536]), w_ref[...], tm)
        ya_ref[...] = (b * y * (z * sig)).astype(BF16)

    prev, nxt = _halo_specs(s, tm, 16, 2048)
    return pl.pallas_call(
        body, name="conv_fwd", grid=(nt,),
        out_shape=jax.ShapeDtypeStruct((s, 512), BF16),
        in_specs=[_rows(tm, 2048), prev, nxt, _full((3, 512))],
        out_specs=_rows(tm, 512),
        compiler_params=_params(32),
    )(pa, pa, pa, w_conv)


def _heads_to_lanes(a, g, row):
    low = row < HEAD_DIM
    parts = []
    for b in (2 * g, 2 * g + 1):
        t = jnp.transpose(a[:, 128 * b:128 * b + 128])
        swapped = pltpu.roll(t, HEAD_DIM, 0)
        if g == 0:
            parts += [jnp.where(low, t, 0.0), jnp.where(low, swapped, 0.0)]
        else:
            parts += [jnp.where(low, 0.0, swapped), jnp.where(low, 0.0, t)]
    return jnp.concatenate(parts, axis=1)


def _lanes_to_heads(t0, t1, row):
    low = row < HEAD_DIM
    blocks = []
    for b in range(4):
        g = b // 2
        tg = (t0, t1)[g]
        je = 2 * (b - 2 * g)
        even, odd = tg[:, 128 * je:128 * je + 128], tg[:, 128 * je + 128:128 * je + 256]
        if g == 0:
            t = jnp.where(low, even, pltpu.roll(odd, HEAD_DIM, 0))
        else:
            t = jnp.where(low, pltpu.roll(even, HEAD_DIM, 0), odd)
        blocks.append(jnp.transpose(t))
    return jnp.concatenate(blocks, axis=1)


WINDOW_KEYS = 3 * ATTN_BLOCK
STACKED = 4 * ATTN_BLOCK
KEY_CHUNK = 32
MAX_BLOCKS_IN_STEP = 8


def _fill_band_bias(bias, nb):
    assert nb >= 2
    c = lax.broadcasted_iota(jnp.int32, (WINDOW_KEYS, STACKED), 0)
    r = lax.broadcasted_iota(jnp.int32, (WINDOW_KEYS, STACKED), 1) & (ATTN_BLOCK - 1)
    band = (c >= r) & (c <= r + 2 * ATTN_BLOCK)
    for v, ok in enumerate((band, band & (c >= ATTN_BLOCK), band & (c < 2 * ATTN_BLOCK))):
        bias[v] = jnp.where(ok, 0.0, -jnp.inf)


def _bias_variant(n, nb):
    return jnp.where(n == 0, 1, jnp.where(n == nb - 1, 2, 0))


def _sink_row(sink_ref, g):
    return jnp.concatenate([jnp.full((1, ATTN_BLOCK), sink_ref[4 * g + j], F32) for j in range(4)], axis=1)


def _softmax_keys_major(sc, bias, variant, sink, e_scr):
    chunks = [pl.ds(k * KEY_CHUNK, KEY_CHUNK) for k in range(WINDOW_KEYS // KEY_CHUNK)]
    rows = [slice(k * KEY_CHUNK, (k + 1) * KEY_CHUNK) for k in range(WINDOW_KEYS // KEY_CHUNK)]
    m_run = jnp.full((KEY_CHUNK, STACKED), -jnp.inf, F32)
    for ck, rw in zip(chunks, rows):
        m_run = jnp.maximum(m_run, sc[rw] + bias[variant, ck, :])
    m = jnp.maximum(jnp.max(m_run, axis=0, keepdims=True), sink)
    l_run = jnp.zeros((KEY_CHUNK, STACKED), F32)
    for ck, rw in zip(chunks, rows):
        e = jnp.exp(sc[rw] + bias[variant, ck, :] - m)
        l_run += e
        e_scr[rw, :] = e.astype(BF16)
    es = jnp.exp(sink - m)
    inv = 1.0 / (jnp.sum(l_run, axis=0, keepdims=True) + es)
    return inv, es * inv


def _fill_padded(kv_ref, kpad, vpad, s):
    zero = jnp.zeros((ATTN_BLOCK, 128), BF16)
    kpad[0:ATTN_BLOCK, :] = zero
    vpad[0:ATTN_BLOCK, :] = zero
    kpad[ATTN_BLOCK + s:2 * ATTN_BLOCK + s, :] = zero
    vpad[ATTN_BLOCK + s:2 * ATTN_BLOCK + s, :] = zero
    kpad[ATTN_BLOCK:ATTN_BLOCK + s, :] = kv_ref[:, 0:128]
    vpad[ATTN_BLOCK:ATTN_BLOCK + s, :] = kv_ref[:, 128:256]


def _attn_fwd(pq, pkv, pbz, sink):
    s = pq.shape[0]
    nb = s // ATTN_BLOCK
    bps = min(MAX_BLOCKS_IN_STEP, nb)

    def body(sink_ref, q_ref, z_ref, kv_ref, yb_ref, kpad, vpad, bias, e_scr):
        i = pl.program_id(0)

        @pl.when(i == 0)
        def _():
            _fill_padded(kv_ref, kpad, vpad, s)
            _fill_band_bias(bias, nb)

        row = lax.broadcasted_iota(jnp.int32, (ATTN_BLOCK, 128), 0)
        for b in range(bps):
            n = i * bps + b
            rows = slice(b * ATTN_BLOCK, (b + 1) * ATTN_BLOCK)
            start = pl.multiple_of(n * ATTN_BLOCK, ATTN_BLOCK)
            kw, vw = kpad[pl.ds(start, WINDOW_KEYS), :], vpad[pl.ds(start, WINDOW_KEYS), :]
            qf = q_ref[rows, :].astype(F32)
            variant = _bias_variant(n, nb)
            outs = []
            for g in range(2):
                e_bg = e_scr.at[2 * b + g]
                qt = (_heads_to_lanes(qf, g, row) * ATTN_SCALE).astype(BF16)
                inv, _ = _softmax_keys_major(_dot(kw, qt), bias, variant, _sink_row(sink_ref, g), e_bg)
                outs.append(_dot_tn(vw, e_bg[...]) * inv)
            attn = _lanes_to_heads(outs[0], outs[1], row)
            z = z_ref[rows, :].astype(F32)
            yb_ref[rows, :] = (attn * (z * _sigmoid(z))).astype(BF16)

    tq = bps * ATTN_BLOCK
    return pl.pallas_call(
        body, name="attn_fwd", grid=(s // tq,),
        out_shape=jax.ShapeDtypeStruct((s, 512), BF16),
        in_specs=[pl.BlockSpec(memory_space=pltpu.SMEM), _rows(tq, 512), _rows(tq, 512), _full((s, 256))],
        out_specs=_rows(tq, 512),
        scratch_shapes=[pltpu.VMEM((s + 2 * ATTN_BLOCK, 128), BF16)] * 2
        + [pltpu.VMEM((3, WINDOW_KEYS, STACKED), F32),
           pltpu.VMEM((2 * bps, WINDOW_KEYS, STACKED), BF16)],
        compiler_params=_params(32),
    )(sink, pq, pbz, pkv)


def _mem_softmax_t(q, mk):
    sc = _dot_nt(mk, q) * MEM_SCALE
    e = jnp.exp(sc - jnp.max(sc, axis=0, keepdims=True))
    return e * (1.0 / jnp.sum(e, axis=0, keepdims=True))


def _mem_attn_fwd(pmq, pmz, mkv):
    s = pmq.shape[0]
    m = mkv.shape[0]
    tm = min(512, s)

    def body(q_ref, z_ref, mk_ref, mv_ref, ym_ref):
        z = z_ref[...].astype(F32)
        sz = z * _sigmoid(z)
        for h in range(MEM_HEADS):
            cols = slice(128 * h, 128 * h + 128)
            pt = _mem_softmax_t(q_ref[:, cols], mk_ref[:, cols])
            o = _dot_tn(pt.astype(BF16), mv_ref[:, cols])
            ym_ref[:, cols] = (o * sz[:, cols]).astype(BF16)

    return pl.pallas_call(
        body, name="mem_attn_fwd", grid=(s // tm,),
        out_shape=jax.ShapeDtypeStruct((s, 512), BF16),
        in_specs=[_rows(tm, 512), _rows(tm, 512), pl.BlockSpec((m, 512), lambda i: (0, 0)),
                  pl.BlockSpec((m, 512), lambda i: (0, 1))],
        out_specs=_rows(tm, 512),
        compiler_params=_params(16),
    )(pmq, pmz, mkv, mkv)


def _mid(ya, yb, ym, pg, x, target, g_post, w_up, w_out):
    s = x.shape[0]
    tm = min(256, s)
    nt = s // tm

    def body(ya_ref, yb_ref, ym_ref, pg_ref, x_ref, t_ref, gp_ref, wup_hbm, wout_hbm,
             dg_ref, dya_ref, dyb_ref, dym_ref, dy_ref, loss_ref, ggp_ref, mb_ref, dob_ref, du_ref,
             wup_vm, wout_vm, sems):
        i = pl.program_id(0)
        _load_once([(wup_hbm.at[d], wup_vm.at[:, pl.ds(128 * d, 128)]) for d in range(N_DEV)]
                   + [(wout_hbm, wout_vm)], sems)

        @pl.when(i == 0)
        def _():
            loss_ref[...] = jnp.zeros_like(loss_ref)
            ggp_ref[...] = jnp.zeros_like(ggp_ref)

        ys = (ya_ref[...], yb_ref[...], ym_ref[...])
        us = [_dot(ys[k], wup_vm[512 * k:512 * k + 512, :]) for k in range(3)]
        gates = [_sigmoid(pg_ref[:, 1024 * k:1024 * k + 1024].astype(F32)) for k in range(3)]
        merged = gates[0] * us[0] + gates[1] * us[1] + gates[2] * us[2]
        mb = merged.astype(BF16)
        mb_ref[...] = mb
        out = _dot(mb, wout_vm[...])
        r = lax.rsqrt(jnp.mean(out * out, axis=-1, keepdims=True) + EPS)
        on = out * r
        gp = gp_ref[...]
        err = (x_ref[...] + on * gp) - t_ref[...]
        loss_ref[...] += 0.5 * jnp.sum(err * err) * (1.0 / D_MODEL)
        dy = err * (1.0 / D_MODEL)
        dy_ref[...] = dy
        ggp_ref[...] += jnp.sum(dy * on, axis=0, keepdims=True)
        a = dy * gp
        d_out = r * (a - on * jnp.mean(a * on, axis=-1, keepdims=True))
        dob = d_out.astype(BF16)
        dob_ref[...] = dob
        d_merged = _dot_nt(dob, wout_vm[...])
        d_refs = (dya_ref, dyb_ref, dym_ref)
        for k in range(3):
            g = gates[k]
            du_f = d_merged * g
            dg_ref[:, 1024 * k:1024 * k + 1024] = (du_f * us[k] * (1.0 - g)).astype(BF16)
            du = du_f.astype(BF16)
            du_ref[k] = du
            d_refs[k][...] = _dot_nt(du, wup_vm[512 * k:512 * k + 512, :]).astype(BF16)

    return pl.pallas_call(
        body, name="mid", grid=(nt,),
        out_shape=[jax.ShapeDtypeStruct((s, 3072), BF16)] + [jax.ShapeDtypeStruct((s, 512), BF16)] * 3
        + [jax.ShapeDtypeStruct((s, D_MODEL), F32), jax.ShapeDtypeStruct((8, 128), F32),
           jax.ShapeDtypeStruct((1, D_MODEL), F32), jax.ShapeDtypeStruct((s, D_MODEL), BF16),
           jax.ShapeDtypeStruct((s, D_MODEL), BF16), jax.ShapeDtypeStruct((3, s, D_MODEL), BF16)],
        in_specs=[_rows(tm, 512)] * 3 + [_rows(tm, 3072), _rows(tm, D_MODEL), _rows(tm, D_MODEL),
                                         _full((1, D_MODEL)), ANY, ANY],
        out_specs=[_rows(tm, 3072)] + [_rows(tm, 512)] * 3
        + [_rows(tm, D_MODEL), _full((8, 128)), _full((1, D_MODEL)), _rows(tm, D_MODEL), _rows(tm, D_MODEL),
           pl.BlockSpec((3, tm, D_MODEL), lambda i: (0, i, 0))],
        scratch_shapes=[pltpu.VMEM((1536, D_MODEL), BF16), pltpu.VMEM((D_MODEL, D_MODEL), BF16),
                        pltpu.SemaphoreType.DMA((N_DEV + 1,))],
        compiler_params=_params(48),
    )(ya, yb, ym, pg, x, target, g_post, w_up, w_out)


def _gw_mid(mb, dob, ys, du):
    s = mb.shape[0]
    tn = 256

    def out_body(mb_ref, dob_ref, o_ref):
        o_ref[...] = _dot_tn(mb_ref[...], dob_ref[...]).astype(BF16)

    gw_out = pl.pallas_call(
        out_body, name="gw_out", grid=(D_MODEL // tn,),
        out_shape=jax.ShapeDtypeStruct((D_MODEL, D_MODEL), BF16),
        in_specs=[pl.BlockSpec((s, tn), lambda j: (0, j)), _full((s, D_MODEL))],
        out_specs=pl.BlockSpec((tn, D_MODEL), lambda j: (j, 0)),
        compiler_params=_params(28),
    )(mb, dob)

    per = 512 // tn

    def up_body(ya_ref, yb_ref, ym_ref, du_ref, o_ref):
        j = pl.program_id(0)
        for k, y_ref in enumerate((ya_ref, yb_ref, ym_ref)):
            @pl.when(j // per == k)
            def _(y_ref=y_ref):
                res = _dot_tn(y_ref[...], du_ref[...])
                for d in range(N_DEV):
                    o_ref[d] = res[:, 128 * d:128 * d + 128].astype(BF16)

    def y_spec(k):
        return pl.BlockSpec((s, tn), lambda j: (0, jnp.clip(j - per * k, 0, per - 1)))

    gw_up = pl.pallas_call(
        up_body, name="gw_up", grid=(3 * per,),
        out_shape=jax.ShapeDtypeStruct((N_DEV, 1536, 128), BF16),
        in_specs=[y_spec(0), y_spec(1), y_spec(2), pl.BlockSpec((None, s, D_MODEL), lambda j: (j // per, 0, 0))],
        out_specs=pl.BlockSpec((N_DEV, tn, 128), lambda j: (0, j, 0)),
        compiler_params=_params(36),
    )(*ys, du)
    return gw_out, gw_up


def _conv_bwd(pa, dya, w_conv):
    s = pa.shape[0]
    tm = min(512, s)
    nt = s // tm

    def body(pa_ref, pp_ref, pn_ref, d_ref, dp_ref, dn_ref, w_ref, da_ref, gw_ref):
        i = pl.program_id(0)
        first, last = i == 0, i == nt - 1

        @pl.when(first)
        def _():
            gw_ref[...] = jnp.zeros_like(gw_ref)

        w = w_ref[...]
        prev_row = pp_ref[...].astype(F32)[15:16, :]
        next_row = pn_ref[...].astype(F32)[0:1, :]
        b, c, u, z, cu, cu_m1, cu_p1, y, sig, row = _conv_common(
            pa_ref[...].astype(F32),
            jnp.where(first, 0.0, prev_row[:, 512:1024] * prev_row[:, 1024:1536]),
            jnp.where(last, 0.0, next_row[:, 512:1024] * next_row[:, 1024:1536]), w, tm)
        sz = z * sig
        dya_t = d_ref[...].astype(F32)
        d_y = dya_t * b * sz

        def halo_dy(p_row, d_row):
            zz = p_row[:, 1536:2048]
            return d_row * p_row[:, 0:512] * (zz * _sigmoid(zz))

        dy_prev = jnp.where(first, 0.0, halo_dy(prev_row, dp_ref[...].astype(F32)[15:16, :]))
        dy_next = jnp.where(last, 0.0, halo_dy(next_row, dn_ref[...].astype(F32)[0:1, :]))
        dy_m1 = jnp.where(row == 0, dy_prev, pltpu.roll(d_y, 1, 0))
        dy_p1 = jnp.where(row == tm - 1, dy_next, pltpu.roll(d_y, tm - 1, 0))
        d_cu = dy_p1 * w[0:1] + d_y * w[1:2] + dy_m1 * w[2:3]
        da_ref[:, 0:512] = (dya_t * y * sz).astype(BF16)
        da_ref[:, 512:1024] = (d_cu * u).astype(BF16)
        da_ref[:, 1024:1536] = (d_cu * c).astype(BF16)
        da_ref[:, 1536:2048] = (dya_t * b * y * (sig + sz * (1.0 - sig))).astype(BF16)
        gw_ref[0:1, :] += jnp.sum(d_y * cu_m1, axis=0, keepdims=True)
        gw_ref[1:2, :] += jnp.sum(d_y * cu, axis=0, keepdims=True)
        gw_ref[2:3, :] += jnp.sum(d_y * cu_p1, axis=0, keepdims=True)

    prev, nxt = _halo_specs(s, tm, 16, 2048)
    dprev, dnxt = _halo_specs(s, tm, 16, 512)
    return pl.pallas_call(
        body, name="conv_bwd", grid=(nt,),
        out_shape=[jax.ShapeDtypeStruct((s, 2048), BF16), jax.ShapeDtypeStruct((8, 512), F32)],
        in_specs=[_rows(tm, 2048), prev, nxt, _rows(tm, 512), dprev, dnxt, _full((3, 512))],
        out_specs=[_rows(tm, 2048), _full((8, 512))],
        compiler_params=_params(36),
    )(pa, pa, pa, dya, dya, dya, w_conv)


def _attn_bwd(pq, pkv, pbz, dyb, sink, tabs):
    s = pq.shape[0]
    nb = s // ATTN_BLOCK
    bps = min(MAX_BLOCKS_IN_STEP, nb)

    def body(sink_ref, q_ref, z_ref, d_ref, kv_ref, t_ref,
             dq_ref, dz_ref, dkv_ref, gs_ref, kpad, vpad, dk_acc, dv_acc, bias, e_scr, ds_scr):
        i = pl.program_id(0)

        @pl.when(i == 0)
        def _():
            _fill_padded(kv_ref, kpad, vpad, s)
            _fill_band_bias(bias, nb)
            dk_acc[...] = jnp.zeros_like(dk_acc)
            dv_acc[...] = jnp.zeros_like(dv_acc)
            gs_ref[...] = jnp.zeros_like(gs_ref)

        row = lax.broadcasted_iota(jnp.int32, (ATTN_BLOCK, 128), 0)
        for b in range(bps):
            n = i * bps + b
            rows = slice(b * ATTN_BLOCK, (b + 1) * ATTN_BLOCK)
            start = pl.multiple_of(n * ATTN_BLOCK, ATTN_BLOCK)
            kw, vw = kpad[pl.ds(start, WINDOW_KEYS), :], vpad[pl.ds(start, WINDOW_KEYS), :]
            qf = q_ref[rows, :].astype(F32)
            variant = _bias_variant(n, nb)
            z = z_ref[rows, :].astype(F32)
            sig = _sigmoid(z)
            dyb_t = d_ref[rows, :].astype(F32)
            d_attn = dyb_t * (z * sig)
            outs, dqs = [], []
            dk_w = jnp.zeros((WINDOW_KEYS, 128), F32)
            dv_w = jnp.zeros((WINDOW_KEYS, 128), F32)
            for g in range(2):
                e_bg, ds_bg = e_scr.at[2 * b + g], ds_scr.at[2 * b + g]
                qt = _heads_to_lanes(qf, g, row)
                inv, p_sink = _softmax_keys_major(
                    _dot(kw, (qt * ATTN_SCALE).astype(BF16)), bias, variant, _sink_row(sink_ref, g), e_bg)
                ot = _dot_tn(vw, e_bg[...]) * inv
                outs.append(ot)
                dot_ = _heads_to_lanes(d_attn, g, row)
                delta = jnp.sum(dot_ * ot, axis=0, keepdims=True)
                dpt = _dot(vw, dot_.astype(BF16))
                for k in range(WINDOW_KEYS // KEY_CHUNK):
                    rw = slice(k * KEY_CHUNK, (k + 1) * KEY_CHUNK)
                    ds_bg[rw, :] = (e_bg[rw, :].astype(F32) * (dpt[rw] - delta)).astype(BF16)
                sink_part = p_sink * delta
                for j in range(4):
                    h = 4 * g + j
                    gs_ref[h:h + 1, :] -= jnp.sum(sink_part[:, 128 * j:128 * j + 128])
                dqs.append(_dot_tn(kw, ds_bg[...]) * (inv * ATTN_SCALE))
                dk_w += _dot_nt(ds_bg[...], (qt * inv).astype(BF16)) * ATTN_SCALE
                dv_w += _dot_nt(e_bg[...], (dot_ * inv).astype(BF16))
            dk_acc[pl.ds(start, WINDOW_KEYS), :] += dk_w
            dv_acc[pl.ds(start, WINDOW_KEYS), :] += dv_w
            attn = _lanes_to_heads(outs[0], outs[1], row)
            dz_ref[rows, :] = (dyb_t * attn * (sig * (1.0 + z * (1.0 - sig)))).astype(BF16)
            dq = _lanes_to_heads(dqs[0], dqs[1], row)
            trows = pl.ds(start, ATTN_BLOCK)
            cs, s1, s2 = t_ref[0, trows, :], t_ref[1, trows, :], t_ref[2, trows, :]
            for blk in range(4):
                cols = slice(128 * blk, 128 * blk + 128)
                dq_ref[rows, cols] = _rope_t(dq[:, cols], cs, s1, s2).astype(BF16)

        @pl.when(i == nb // bps - 1)
        def _():
            dk = dk_acc[ATTN_BLOCK:ATTN_BLOCK + s, :]
            dkv_ref[:, 0:128] = _rope_t(dk, t_ref[0], t_ref[1], t_ref[2]).astype(BF16)
            dkv_ref[:, 128:256] = dv_acc[ATTN_BLOCK:ATTN_BLOCK + s, :].astype(BF16)

    tq = bps * ATTN_BLOCK
    tile = _rows(tq, 512)
    return pl.pallas_call(
        body, name="attn_bwd", grid=(s // tq,),
        out_shape=[jax.ShapeDtypeStruct((s, 512), BF16), jax.ShapeDtypeStruct((s, 512), BF16),
                   jax.ShapeDtypeStruct((s, 256), BF16), jax.ShapeDtypeStruct((8, 128), F32)],
        in_specs=[pl.BlockSpec(memory_space=pltpu.SMEM), tile, tile, tile, _full((s, 256)), _full((3, s, 128))],
        out_specs=[tile, tile, _full((s, 256)), _full((8, 128))],
        scratch_shapes=[pltpu.VMEM((s + 2 * ATTN_BLOCK, 128), BF16)] * 2
        + [pltpu.VMEM((s + 2 * ATTN_BLOCK, 128), F32)] * 2
        + [pltpu.VMEM((3, WINDOW_KEYS, STACKED), F32)]
        + [pltpu.VMEM((2 * bps, WINDOW_KEYS, STACKED), BF16)] * 2,
        compiler_params=_params(48),
    )(sink, pq, pbz, dyb, pkv, tabs)


def _mem_attn_bwd(pmq, pmz, mkv, dym):
    s = pmq.shape[0]
    m = mkv.shape[0]
    tm = min(512, s)

    def body(q_ref, z_ref, d_ref, mk_ref, mv_ref, dq_ref, dz_ref, dmkv_ref):
        @pl.when(pl.program_id(0) == 0)
        def _():
            dmkv_ref[...] = jnp.zeros_like(dmkv_ref)

        z = z_ref[...].astype(F32)
        sig = _sigmoid(z)
        dym_t = d_ref[...].astype(F32)
        d_attn = dym_t * (z * sig)
        dsilu = sig * (1.0 + z * (1.0 - sig))
        for h in range(MEM_HEADS):
            cols = slice(128 * h, 128 * h + 128)
            q, mk, mv = q_ref[:, cols], mk_ref[:, cols], mv_ref[:, cols]
            pt = _mem_softmax_t(q, mk)
            pb = pt.astype(BF16)
            o = _dot_tn(pb, mv)
            dob = d_attn[:, cols].astype(BF16)
            dpt = _dot_nt(mv, dob)
            dst = (pt * (dpt - jnp.sum(pt * dpt, axis=0, keepdims=True))).astype(BF16)
            dq_ref[:, cols] = (_dot_tn(dst, mk) * MEM_SCALE).astype(BF16)
            dz_ref[:, cols] = (dym_t[:, cols] * o * dsilu[:, cols]).astype(BF16)
            dmkv_ref[:, cols] += _dot(dst, q) * MEM_SCALE
            dmkv_ref[:, 512 + 128 * h:512 + 128 * h + 128] += _dot(pb, dob)

    return pl.pallas_call(
        body, name="mem_attn_bwd", grid=(s // tm,),
        out_shape=[jax.ShapeDtypeStruct((s, 512), BF16), jax.ShapeDtypeStruct((s, 512), BF16),
                   jax.ShapeDtypeStruct((m, D_MODEL), F32)],
        in_specs=[_rows(tm, 512), _rows(tm, 512), _rows(tm, 512), pl.BlockSpec((m, 512), lambda i: (0, 0)),
                  pl.BlockSpec((m, 512), lambda i: (0, 1))],
        out_specs=[_rows(tm, 512), _rows(tm, 512), _full((m, D_MODEL))],
        compiler_params=_params(20),
    )(pmq, pmz, dym, mkv, mkv)


def _mem_kv_bwd(mem, g_mem, mn, dmkv, w_mkv):
    m = mem.shape[0]

    def body(mem_ref, g_ref, mn_ref, d_ref, w_ref, gw_ref, gg_ref):
        db = d_ref[...].astype(BF16)
        gw_ref[...] = _dot_tn(mn_ref[...], db).astype(BF16)
        d_mn = _dot_nt(db, w_ref[...])
        xf = mem_ref[...]
        r = lax.rsqrt(jnp.mean(xf * xf, axis=-1, keepdims=True) + EPS)
        gg_ref[...] = jnp.sum(d_mn * (xf * r), axis=0, keepdims=True)

    return pl.pallas_call(
        body, name="mem_kv_bwd", grid=(1,),
        out_shape=[jax.ShapeDtypeStruct((D_MODEL, D_MODEL), BF16), jax.ShapeDtypeStruct((1, D_MODEL), F32)],
        in_specs=[_full((m, D_MODEL)), _full((1, D_MODEL)), _full((m, D_MODEL)), _full((m, D_MODEL)),
                  _full((D_MODEL, D_MODEL))],
        out_specs=[_full((D_MODEL, D_MODEL)), _full((1, D_MODEL))],
        compiler_params=_params(16),
    )(mem, g_mem, mn, dmkv, w_mkv)


def _dh_bwd(dparts, x, dy, g_pre, w_int):
    s = x.shape[0]
    tm = min(256, s)

    def body(*refs):
        d_refs = refs[:7]
        x_ref, dy_ref, g_ref, w_hbm, gx_ref, gg_ref, w_vm, sems = refs[7:]
        _load_once([(w_hbm, w_vm)], sems)

        @pl.when(pl.program_id(0) == 0)
        def _():
            gg_ref[...] = jnp.zeros_like(gg_ref)

        d_h = jnp.zeros((tm, D_MODEL), F32)
        for d_ref, (r0, width) in zip(d_refs, SEGS):
            for c0 in range(0, width, 512):
                cw = min(512, width - c0)
                d_h += _dot(d_ref[:, c0:c0 + cw], w_vm[r0 + c0:r0 + c0 + cw, :])
        xf = x_ref[...]
        r = lax.rsqrt(jnp.mean(xf * xf, axis=-1, keepdims=True) + EPS)
        xn = xf * r
        a = d_h * g_ref[...]
        gx_ref[...] = r * (a - xn * jnp.mean(a * xn, axis=-1, keepdims=True)) + dy_ref[...]
        gg_ref[...] += jnp.sum(d_h * xn, axis=0, keepdims=True)

    return pl.pallas_call(
        body, name="dh_bwd", grid=(s // tm,),
        out_shape=[jax.ShapeDtypeStruct((s, D_MODEL), F32), jax.ShapeDtypeStruct((1, D_MODEL), F32)],
        in_specs=[_rows(tm, w) for _, w in SEGS] + [_rows(tm, D_MODEL), _rows(tm, D_MODEL), _full((1, D_MODEL)), ANY],
        out_specs=[_rows(tm, D_MODEL), _full((1, D_MODEL))],
        scratch_shapes=[pltpu.VMEM((IN_WIDTH, D_MODEL), BF16), pltpu.SemaphoreType.DMA((1,))],
        compiler_params=_params(38),
    )(*dparts, x, dy, g_pre, w_int)


def _gw_in(dparts, h):
    s = h.shape[0]
    tn = 256
    starts, counts = [], []
    for r0, width in SEGS:
        starts.append(r0 // tn)
        counts.append(width // tn)

    def body(*refs):
        d_refs = refs[:7]
        h_hbm, o_ref, h_vm, sems = refs[7:]
        _load_once([(h_hbm, h_vm)], sems)
        j = pl.program_id(0)
        for d_ref, st, cnt in zip(d_refs, starts, counts):
            @pl.when((j >= st) & (j < st + cnt))
            def _(d_ref=d_ref):
                o_ref[...] = _dot_tn(d_ref[...], h_vm[...]).astype(BF16)

    def seg_spec(st, cnt):
        return pl.BlockSpec((s, tn), lambda j: (0, jnp.clip(j - st, 0, cnt - 1)))

    return pl.pallas_call(
        body, name="gw_in", grid=(IN_WIDTH // tn,),
        out_shape=jax.ShapeDtypeStruct((IN_WIDTH, D_MODEL), BF16),
        in_specs=[seg_spec(st, cnt) for st, cnt in zip(starts, counts)] + [ANY],
        out_specs=pl.BlockSpec((tn, D_MODEL), lambda j: (j, 0)),
        scratch_shapes=[pltpu.VMEM((s, D_MODEL), BF16), pltpu.SemaphoreType.DMA((1,))],
        compiler_params=_params(44),
    )(*dparts, h)


def _adamw_math(w, g, m, v):
    m2 = ADAM_B1 * m + (1.0 - ADAM_B1) * g
    v2 = ADAM_B2 * v + (1.0 - ADAM_B2) * (g * g)
    m_hat = m2 / (1.0 - ADAM_B1 ** ADAM_STEP)
    v_hat = v2 / (1.0 - ADAM_B2 ** ADAM_STEP)
    delta = -ADAM_LR * (m_hat / (jnp.sqrt(v_hat) + ADAM_EPS) + ADAM_WD * w)
    return delta, m2, v2


def _sum_adamw(own, land, chip, block, w, m, v, name, tiles=1):
    r, c = w.shape
    rt = r // tiles

    def body(c_ref, own_ref, l1_ref, l2_ref, l3_ref, w_ref, m_ref, v_ref, g_ref, d_ref, m2_ref, v2_ref):
        g = own_ref[...].astype(F32)
        for l_ref in (l1_ref, l2_ref, l3_ref):
            g += l_ref[...].astype(F32)
        g_ref[...] = g
        d_ref[...], m2_ref[...], v2_ref[...] = _adamw_math(w_ref[...], g, m_ref[...], v_ref[...])

    def share(k):
        return pl.BlockSpec((None, rt, c), lambda i, c_ref: (jnp.bitwise_xor(c_ref[0], k), block * tiles + i, 0))

    spec = pl.BlockSpec((rt, c), lambda i, c_ref: (i, 0))
    grid_spec = pltpu.PrefetchScalarGridSpec(
        num_scalar_prefetch=1, grid=(tiles,),
        in_specs=[share(0), share(1), share(2), share(3)] + [spec] * 3, out_specs=[spec] * 4)
    return pl.pallas_call(
        body, name=name, grid_spec=grid_spec,
        out_shape=[jax.ShapeDtypeStruct((r, c), F32)] * 4,
        compiler_params=_params(48),
    )(chip, own, land, land, land, w, m, v)


def _sum_adamw_group(items, chip, name):
    k = len(items)

    def body(c_ref, *refs):
        shares, wmv, outs = refs[:4 * k], refs[4 * k:7 * k], refs[7 * k:]
        for j in range(k):
            g = shares[4 * j][...].astype(F32)
            for l_ref in shares[4 * j + 1:4 * j + 4]:
                g += l_ref[...].astype(F32)
            outs[4 * j][...] = g
            outs[4 * j + 1][...], outs[4 * j + 2][...], outs[4 * j + 3][...] = _adamw_math(
                wmv[3 * j][...], g, wmv[3 * j + 1][...], wmv[3 * j + 2][...])

    def share(shape, block, q):
        return pl.BlockSpec((None,) + shape, lambda i, c_ref: (jnp.bitwise_xor(c_ref[0], q), block, 0))

    in_specs, args = [], []
    for own, land, block, w, m, v in items:
        in_specs += [share(w.shape, block, q) for q in range(4)]
        args += [own, land, land, land]
    for own, land, block, w, m, v in items:
        in_specs += [pl.BlockSpec(w.shape, lambda i, c_ref: (0, 0))] * 3
        args += [w, m, v]
    out_specs = [pl.BlockSpec(w.shape, lambda i, c_ref: (0, 0)) for _, _, _, w, _, _ in items for _ in range(4)]
    res = pl.pallas_call(
        body, name=name,
        grid_spec=pltpu.PrefetchScalarGridSpec(num_scalar_prefetch=1, grid=(1,), in_specs=in_specs,
                                               out_specs=out_specs),
        out_shape=[jax.ShapeDtypeStruct(w.shape, F32) for _, _, _, w, _, _ in items for _ in range(4)],
        compiler_params=_params(12),
    )(chip, *args)
    return [res[4 * j:4 * j + 4] for j in range(k)]


def _small_step(parts, ws, ms, vs):
    def exchange(gpre_ref, gconv_ref, gsink_ref, gmem_ref, gpost_ref, loss_ref, tot_ref,
                 pack, gathered, send_sems, recv_sems):
        x, y, c = _my_place()
        me_idx = 4 * x + 2 * y + c

        lane = lax.broadcasted_iota(jnp.int32, (1, 128), 1)
        sink_row = jnp.zeros((1, 128), F32)
        for h in range(8):
            sink_row = jnp.where(lane == h, gsink_ref[h:h + 1, :], sink_row)
        pack[...] = jnp.zeros_like(pack)
        pack[0:1, :] = gpre_ref[...]
        pack[1:2, :] = gmem_ref[...]
        pack[2:3, :] = gpost_ref[...]
        pack[3:6, 0:512] = gconv_ref[0:3, :]
        pack[6:7, 0:128] = sink_row
        pack[7:8, 0:128] = loss_ref[0:1, :]

        flips = [(0, 0, 1), (0, 1, 0), (1, 0, 0), (0, 1, 1), (1, 0, 1), (1, 1, 0), (1, 1, 1)]
        cps = []
        for k, (fx, fy, fc) in enumerate(flips):
            peer = ((1 - x) if fx else x, (1 - y) if fy else y, (1 - c) if fc else c)
            cps.append(pltpu.make_async_remote_copy(
                src_ref=pack, dst_ref=gathered.at[me_idx], send_sem=send_sems.at[k], recv_sem=recv_sems.at[k],
                device_id=peer, device_id_type=MESH))
        for cp in cps:
            cp.start()
        gathered[me_idx] = pack[...]
        for cp in cps:
            cp.wait_recv()
        for cp in cps:
            cp.wait_send()
        tot = gathered[0]
        for d in range(1, N_DEV):
            tot = tot + gathered[d]
        tot_ref[...] = tot

    tot = pl.pallas_call(
        exchange, name="small_exchange", grid=(1,),
        out_shape=jax.ShapeDtypeStruct((8, D_MODEL), F32),
        in_specs=[_full(p.shape) for p in parts], out_specs=_full((8, D_MODEL)),
        scratch_shapes=[pltpu.VMEM((8, D_MODEL), F32), pltpu.VMEM((N_DEV, 8, D_MODEL), F32),
                        pltpu.SemaphoreType.DMA((N_PEERS,)), pltpu.SemaphoreType.DMA((N_PEERS,))],
    )(*parts)

    def apply(tot_ref, *refs):
        w_refs, m_refs, v_refs = refs[0:5], refs[5:10], refs[10:15]
        loss_out = refs[15]
        g_outs, d_outs, m_outs, v_outs = refs[16:21], refs[21:26], refs[26:31], refs[31:36]
        x, y, c = _my_place()
        tot = tot_ref[...]
        conv = pltpu.roll(tot[:, 0:512], (512 - 64 * (4 * x + 2 * y + c)) % 512, 1)[3:6, 0:64]
        grads = (tot[0:1, :], conv, tot[6:7, 0:8], tot[1:2, :], tot[2:3, :])
        loss_out[...] = tot[7:8, 0:128]
        for j in range(5):
            g_outs[j][...] = grads[j]
            d_outs[j][...], m_outs[j][...], v_outs[j][...] = _adamw_math(
                w_refs[j][...], grads[j], m_refs[j][...], v_refs[j][...])

    specs = [_full(w.shape) for w in ws]
    res = pl.pallas_call(
        apply, name="small_apply", grid=(1,),
        out_shape=[jax.ShapeDtypeStruct((1, 128), F32)] + [jax.ShapeDtypeStruct(w.shape, F32) for w in ws] * 4,
        in_specs=[_full((8, D_MODEL))] + specs * 3,
        out_specs=[_full((1, 128))] + specs * 4,
    )(tot, *ws, *ms, *vs)
    return res[0], res[1:6], res[6:11], res[11:16], res[16:21]


def kernel(x, mem, g_pre, w_in, w_conv, attn_sink, g_mem, w_mem_kv, w_up_a, w_up_b, w_up_m, w_out, g_post, loss_target, m_g_pre, m_w_in, m_w_conv, m_attn_sink, m_g_mem, m_w_mem_kv, m_w_up_a, m_w_up_b, m_w_up_m, m_w_out, m_g_post, v_g_pre, v_w_in, v_w_conv, v_attn_sink, v_g_mem, v_w_mem_kv, v_w_up_a, v_w_up_b, v_w_up_m, v_w_out, v_g_post):
    s = x.shape[1]
    x2, mem2, tgt2 = x[0], mem[0], loss_target[0]
    me = 4 * lax.axis_index("x") + 2 * lax.axis_index("y") + lax.axis_index("c")

    w_conv_loc = jnp.zeros((8, 128), F32).at[:3, :64].set(w_conv[0])
    w_int_g, w_conv_g, tabs, w_mkv_loc, w_out_loc, w_up_loc = _all_gather(
        [w_in[0].T.astype(BF16), w_conv_loc], "gather_w_in",
        splits=[[(112 * k, 112) for k in range(7)] + [(784, 144)], [(0, 8)]],
        side=_gather_side(s, w_mem_kv[0], w_out[0], (w_up_a[0], w_up_b[0], w_up_m[0])))
    w_int = w_int_g.reshape(IN_WIDTH, D_MODEL)
    w_conv_f = w_conv_g[:, :3, :64].transpose(1, 0, 2).reshape(3, 512)
    late = _gather_start([w_mkv_loc, w_out_loc, w_up_loc], me, "gather_late_start")
    sink = attn_sink[0]

    h, pa, pq, pkv, pbz, pmq, pmz, pg = _proj_fwd(x2, g_pre + late[4][0:1, 0:1], w_int, tabs)
    ya = _conv_fwd(pa, w_conv_f)
    yb = _attn_fwd(pq, pkv, pbz, sink)
    w_mkv_g, w_out_g, w_up_g = _gather_wait(*late[:4], yb, "gather_late_wait")
    w_mkv = w_mkv_g.reshape(D_MODEL, D_MODEL)
    w_out_f = w_out_g.reshape(D_MODEL, D_MODEL)
    mn, mkv = _mem_kv_fwd(mem2, g_mem, w_mkv)
    ym = _mem_attn_fwd(pmq, pmz, mkv)
    dg, dya, dyb, dym, dy, loss_p, gg_post, mb, dob, du = _mid(ya, yb, ym, pg, x2, tgt2, g_post, w_up_g, w_out_f)
    gw_out, gw_up = _gw_mid(mb, dob, (ya, yb, ym), du)

    core = lax.axis_index("c").astype(jnp.int32).reshape(1)
    chip = (2 * lax.axis_index("x") + lax.axis_index("y")).astype(jnp.int32).reshape(1)

    def exchange_start(shares, tag):
        from_sibling = _sibling_exchange(shares, "grads_to_sibling_" + tag)
        chip_shares = _pair_add(shares, from_sibling, core, "grads_pair_add_" + tag)
        return _chip_exchange_start(chip_shares, "grads_to_chips_start_" + tag)

    dmq, dmz, dmkv = _mem_attn_bwd(pmq, pmz, mkv, dym)
    gw_mkv, gg_mem = _mem_kv_bwd(mem2, g_mem, mn, dmkv, w_mkv)
    shares1 = [gw_mkv.reshape(N_DEV, 128, D_MODEL), gw_out.reshape(N_DEV, 128, D_MODEL), gw_up]
    sib = _split_start(_sibling_copies, N_CHIPS, shares1,
                       [lax.empty((N_CHIPS,) + a.shape[1:], a.dtype) for a in shares1], "grads_to_sibling_small_start")
    da, gw_conv = _conv_bwd(pa, dya, w_conv_f + sib[4][0:1, 0:1])
    shares1, from_sibling = _split_wait(_sibling_copies, N_CHIPS, *sib[:4], da, "grads_to_sibling_small_wait")
    send1, recv1, srcs1, lands1, token1 = _chip_exchange_start(
        _pair_add(shares1, from_sibling, core, "grads_pair_add_small"), "grads_to_chips_start_small")
    dq, dbz, dkv, g_sink = _attn_bwd(pq, pkv, pbz, dyb, sink + token1[0, 0], tabs)
    dparts = (da, dq, dkv, dbz, dmq, dmz, dg)
    gw_int = _gw_in(dparts, h)
    send2, recv2, srcs2, lands2, token2 = exchange_start([gw_int.reshape(N_DEV, SHARD_IN, D_MODEL)], "w_in")
    grad_x, gg_pre = _dh_bwd(dparts, x2, dy, g_pre + token2[0:1, 0:1], w_int)
    (o_mkv, o_out, o_up, o_int), (l_mkv, l_out, l_up, l_int) = _chip_exchange_wait(
        send1 + send2, recv1 + recv2, srcs1 + srcs2, lands1 + lands2, grad_x, "grads_to_chips_wait")

    loss_row, small_g, sd, sm, sv = _small_step(
        (gg_pre, gw_conv, g_sink, gg_mem, gg_post, loss_p),
        [g_pre, w_conv[0], attn_sink, g_mem, g_post],
        [m_g_pre, m_w_conv[0], m_attn_sink, m_g_mem, m_g_post],
        [v_g_pre, v_w_conv[0], v_attn_sink, v_g_mem, v_g_post])
    loss = loss_row[0, 0]
    g_g_pre, g_conv, g_sink_tot, g_g_mem, g_g_post = small_g

    g_w_in, d_w_in, nm_w_in, nv_w_in = (t.T for t in _sum_adamw(
        o_int, l_int, chip, 0, w_in[0].T, m_w_in[0].T, v_w_in[0].T, "adamw_w_in", tiles=2))
    (g_mkv, d_mkv, nm_mkv, nv_mkv), (g_out, d_out, nm_out, nv_out), *up = _sum_adamw_group(
        [(o_mkv, l_mkv, 0, w_mem_kv[0], m_w_mem_kv[0], v_w_mem_kv[0]),
         (o_out, l_out, 0, w_out[0], m_w_out[0], v_w_out[0]),
         (o_up, l_up, 0, w_up_a[0], m_w_up_a[0], v_w_up_a[0]),
         (o_up, l_up, 1, w_up_b[0], m_w_up_b[0], v_w_up_b[0]),
         (o_up, l_up, 2, w_up_m[0], m_w_up_m[0], v_w_up_m[0])], chip, "adamw_mid_weights")

    def lead(a):
        return a[None]

    grads = [g_g_pre, lead(g_w_in), lead(g_conv), g_sink_tot, g_g_mem, lead(g_mkv), lead(up[0][0]),
             lead(up[1][0]), lead(up[2][0]), lead(g_out), g_g_post]

    def assemble(small, big_in, big_mkv, big_up, big_out):
        return [small[0], lead(big_in), lead(small[1]), small[2], small[3], lead(big_mkv), lead(big_up[0]),
                lead(big_up[1]), lead(big_up[2]), lead(big_out), small[4]]

    deltas = assemble(sd, d_w_in, d_mkv, [u[1] for u in up], d_out)
    new_m = assemble(sm, nm_w_in, nm_mkv, [u[2] for u in up], nm_out)
    new_v = assemble(sv, nv_w_in, nv_mkv, [u[3] for u in up], nv_out)
    return (loss, grad_x[None], *grads, *deltas, *new_m, *new_v)
```

```python
import functools

import jax
import jax.numpy as jnp
from jax import lax
from jax.experimental import pallas as pl
from jax.experimental.pallas import tpu as pltpu

F32 = jnp.float32
BF16 = jnp.bfloat16
MESH = pl.DeviceIdType.MESH

N_DEV = 8
D_MODEL = 1024
EPS = 1e-6
ROPE_THETA = 500000.0
ROT_DIM = 16
HEAD_DIM = 64
ATTN_BLOCK = 128
MEM_HEADS = 4
MEM_HEAD_DIM = 128
ATTN_SCALE = HEAD_DIM ** -0.5
MEM_SCALE = MEM_HEAD_DIM ** -0.5

ADAM_LR = 0.001
ADAM_B1 = 0.9
ADAM_B2 = 0.999
ADAM_EPS = 1e-08
ADAM_WD = 0.01
ADAM_STEP = 10

SEG_A = (0, 2048)
SEG_BQ = (2048, 512)
SEG_BKV = (2560, 256)
SEG_BZ = (2816, 512)
SEG_MQ = (3328, 512)
SEG_MZ = (3840, 512)
SEG_G = (4352, 3072)
SEGS = (SEG_A, SEG_BQ, SEG_BKV, SEG_BZ, SEG_MQ, SEG_MZ, SEG_G)
IN_WIDTH = 7424
SHARD_IN = IN_WIDTH // N_DEV

V7X_VMEM_BYTES = 64 * 1024 * 1024
CALL_VMEM_MB = 60
ANY = pl.BlockSpec(memory_space=pl.ANY)


def _params(vmem_mb=CALL_VMEM_MB):
    assert vmem_mb * 1024 * 1024 < V7X_VMEM_BYTES
    return pltpu.CompilerParams(dimension_semantics=("arbitrary",), vmem_limit_bytes=vmem_mb * 1024 * 1024)


def _full(shape):
    zeros = (0,) * len(shape)
    return pl.BlockSpec(shape, lambda i: zeros)


def _rows(tm, width):
    return pl.BlockSpec((tm, width), lambda i: (i, 0))


def _dot(a, b):
    return jnp.dot(a, b, preferred_element_type=F32)


def _dot_nt(a, b):
    return lax.dot_general(a, b, (((1,), (1,)), ((), ())), preferred_element_type=F32)


def _dot_tn(a, b):
    return lax.dot_general(a, b, (((0,), (0,)), ((), ())), preferred_element_type=F32)


def _sigmoid(z):
    return 1.0 / (1.0 + jnp.exp(-z))


def _rope(t, cs, s1, s2):
    return t * cs + pltpu.roll(t, 120, 1) * s1 + pltpu.roll(t, 8, 1) * s2


def _rope_t(d, cs, s1, s2):
    return d * cs + pltpu.roll(d * s1, 8, 1) + pltpu.roll(d * s2, 120, 1)


def _gather_side(s, w_mkv, w_out, w_ups):
    half = ROT_DIM // 2
    inv_freq = jnp.power(jnp.float32(ROPE_THETA), -jnp.arange(half, dtype=F32) * (2.0 / ROT_DIM))
    freq_row = jnp.tile(jnp.concatenate([inv_freq, inv_freq, jnp.zeros((HEAD_DIM - ROT_DIM,), F32)]), 2)[None, :]

    def fn(in_refs, out_refs):
        f_ref, mkv_ref, out_ref, *up_refs = in_refs
        t_ref, mkv_bf, out_bf, up_bf = out_refs
        mkv_bf[...] = mkv_ref[...].astype(BF16)
        out_bf[...] = out_ref[...].astype(BF16)
        for k, up_ref in enumerate(up_refs):
            up_bf[512 * k:512 * k + 512, :] = up_ref[...].astype(BF16)
        pos = lax.broadcasted_iota(jnp.int32, (s, 128), 0).astype(F32)
        d = lax.broadcasted_iota(jnp.int32, (s, 128), 1) & (HEAD_DIM - 1)
        ang = pos * f_ref[...]
        cos, sin = jnp.cos(ang), jnp.sin(ang)
        lo, hi = d < half, (d >= half) & (d < ROT_DIM)
        t_ref[0] = jnp.where(lo | hi, cos, 1.0)
        t_ref[1] = jnp.where(lo, -sin, 0.0)
        t_ref[2] = jnp.where(hi, sin, 0.0)

    return ([freq_row, w_mkv, w_out, *w_ups],
            [jax.ShapeDtypeStruct((3, s, 128), F32), jax.ShapeDtypeStruct(w_mkv.shape, BF16),
             jax.ShapeDtypeStruct(w_out.shape, BF16), jax.ShapeDtypeStruct((1536, 128), BF16)], fn)


def _load_once(pairs, sems):
    @pl.when(pl.program_id(0) == 0)
    def _():
        cps = [pltpu.make_async_copy(src, dst, sems.at[k]) for k, (src, dst) in enumerate(pairs)]
        for cp in cps:
            cp.start()
        for cp in cps:
            cp.wait()


def _my_place():
    x, y, c = lax.axis_index("x"), lax.axis_index("y"), lax.axis_index("c")
    return x, y, c


def _all_gather(arrs, name, splits=None, side=None):
    n = len(arrs)
    if splits is None:
        splits = [[(0, a.shape[0])] for a in arrs]
    pieces = [(a, r0, rn) for a in range(n) for r0, rn in splits[a]]
    n_p = len(pieces)
    side_in, side_out, side_fn = side if side is not None else ((), (), None)
    m, q = len(side_in), len(side_out)

    def body(*refs):
        ins, outs = refs[:n], refs[n + m:2 * n + m]
        send_sems, recv_sems, local_sems = refs[2 * n + m + q:]
        x, y, c = _my_place()
        me, sibling = (x, y, c), (x, y, 1 - c)

        def route(core):
            first = (jnp.bitwise_xor(x, 1 - core), jnp.bitwise_xor(y, core), core)
            second = (jnp.bitwise_xor(x, core), jnp.bitwise_xor(y, 1 - core), core)
            return first, second, (1 - x, 1 - y, core)

        def idx(px, py, pc):
            return 4 * px + 2 * py + pc

        def copy(p, k, block, to, own=False):
            a, r0, rn = pieces[p]
            dst = outs[a].at[idx(*block), pl.ds(r0, rn)]
            return pltpu.make_async_remote_copy(
                src_ref=ins[a].at[pl.ds(r0, rn)] if own else dst, dst_ref=dst,
                send_sem=send_sems.at[p * 7 + k], recv_sem=recv_sems.at[p * 7 + k],
                device_id=to, device_id_type=MESH)

        nbr1, nbr2, diag = route(c)
        mine = [pltpu.make_async_copy(ins[a], outs[a].at[idx(*me)], local_sems.at[a]) for a in range(n)]
        for cp in mine:
            cp.start()
        sent = []
        for p in range(n_p):
            for k, to in enumerate((sibling, nbr1, nbr2)):
                sent.append(copy(p, k, me, to, own=True))
        for cp in sent:
            cp.start()
        if side_fn is not None:
            side_fn(refs[n:n + m], refs[2 * n + m:2 * n + m + q])
        for k_in, block, onward in ((1, nbr1, ((3, nbr2), (4, sibling))), (2, nbr2, ((5, sibling),)),
                                    (3, diag, ((6, sibling),))):
            for p in range(n_p):
                copy(p, k_in, block, me).wait_recv()
                for k_out, to in onward:
                    cp = copy(p, k_out, block, to)
                    cp.start()
                    sent.append(cp)
        s1, s2, sd = route(1 - c)
        for k_in, block in ((0, sibling), (4, s1), (5, s2), (6, sd)):
            for p in range(n_p):
                copy(p, k_in, block, me).wait_recv()
        for cp in sent:
            cp.wait_send()
        for cp in mine:
            cp.wait()

    return pl.pallas_call(
        body, name=name,
        out_shape=[jax.ShapeDtypeStruct((N_DEV,) + a.shape, a.dtype) for a in arrs] + list(side_out),
        in_specs=[ANY] * n + [pl.BlockSpec(memory_space=pltpu.VMEM)] * m,
        out_specs=[ANY] * n + [pl.BlockSpec(memory_space=pltpu.VMEM)] * q,
        scratch_shapes=[pltpu.SemaphoreType.DMA((7 * n_p,)), pltpu.SemaphoreType.DMA((7 * n_p,)),
                        pltpu.SemaphoreType.DMA((n,))],
        compiler_params=pltpu.CompilerParams(vmem_limit_bytes=32 * 1024 * 1024),
    )(*arrs, *side_in)


N_CHIPS = 4


def _sibling_exchange(arrs, name):
    n = len(arrs)

    def body(*refs):
        ins, outs = refs[:n], refs[n:2 * n]
        send_sems, recv_sems = refs[2 * n:]
        x, y, c = _my_place()
        sibling = (x, y, 1 - c)

        def copy(a, j):
            return pltpu.make_async_remote_copy(
                src_ref=ins[a].at[2 * j + (1 - c)], dst_ref=outs[a].at[j],
                send_sem=send_sems.at[a * N_CHIPS + j], recv_sem=recv_sems.at[a * N_CHIPS + j],
                device_id=sibling, device_id_type=MESH)

        cps = [copy(a, j) for j in range(N_CHIPS) for a in range(n)]
        for cp in cps:
            cp.start()
        for cp in cps:
            cp.wait_recv()
        for cp in cps:
            cp.wait_send()

    return pl.pallas_call(
        body, name=name,
        out_shape=[jax.ShapeDtypeStruct((N_CHIPS,) + a.shape[1:], a.dtype) for a in arrs],
        in_specs=[ANY] * n, out_specs=[ANY] * n,
        scratch_shapes=[pltpu.SemaphoreType.DMA((N_CHIPS * n,)), pltpu.SemaphoreType.DMA((N_CHIPS * n,))],
    )(*arrs)


def _sibling_copies(srcs, lands, send_sems, recv_sems):
    x, y, c = _my_place()
    cps = []
    for j in range(N_CHIPS):
        for a in range(len(srcs)):
            k = a * N_CHIPS + j
            cps.append(pltpu.make_async_remote_copy(
                src_ref=srcs[a].at[2 * j + (1 - c)], dst_ref=lands[a].at[j], send_sem=send_sems[k],
                recv_sem=recv_sems[k], device_id=(x, y, 1 - c), device_id_type=MESH))
    return cps


def _pair_add(mine, recv, core, name):
    n = len(mine)

    def body(c_ref, *refs):
        for a in range(n):
            refs[2 * n + a][...] = (refs[a][...].astype(F32) + refs[n + a][...].astype(F32)).astype(BF16)

    def blk(a):
        return (None,) + a.shape[1:]

    grid_spec = pltpu.PrefetchScalarGridSpec(
        num_scalar_prefetch=1, grid=(N_CHIPS,),
        in_specs=[pl.BlockSpec(blk(a), lambda j, c_ref: (2 * j + c_ref[0], 0, 0)) for a in mine]
        + [pl.BlockSpec(blk(a), lambda j, c_ref: (j, 0, 0)) for a in recv],
        out_specs=[pl.BlockSpec(blk(a), lambda j, c_ref: (j, 0, 0)) for a in recv])
    return pl.pallas_call(
        body, name=name, grid_spec=grid_spec,
        out_shape=[jax.ShapeDtypeStruct(a.shape, BF16) for a in recv],
        compiler_params=_params(24),
    )(core, *mine, *recv)


HBM = pl.BlockSpec(memory_space=pltpu.HBM)
SEM = pl.BlockSpec(memory_space=pltpu.SEMAPHORE)
N_PEER_CHIPS = 3


def _chip_copies(srcs, lands, send_sems, recv_sems):
    x, y, c = _my_place()
    my_chip = 2 * x + y
    peers = [(x, 1 - y), (1 - x, y), (1 - x, 1 - y)]
    cps = []
    for k, (px, py) in enumerate(peers):
        for a in range(len(srcs)):
            j = a * N_PEER_CHIPS + k
            cps.append(pltpu.make_async_remote_copy(
                src_ref=srcs[a].at[2 * px + py], dst_ref=lands[a].at[my_chip],
                send_sem=send_sems[j], recv_sem=recv_sems[j],
                device_id=(px, py, c), device_id_type=MESH))
    return cps


N_PEERS = N_DEV - 1


def _gather_copies(srcs, lands, send_sems, recv_sems):
    x, y, c = _my_place()
    me_idx = 4 * x + 2 * y + c
    flips = [(0, 0, 1), (0, 1, 0), (1, 0, 0), (0, 1, 1), (1, 0, 1), (1, 1, 0), (1, 1, 1)]
    cps = []
    for k, (fx, fy, fc) in enumerate(flips):
        peer = ((1 - x) if fx else x, (1 - y) if fy else y, (1 - c) if fc else c)
        for a in range(len(srcs)):
            j = a * N_PEERS + k
            cps.append(pltpu.make_async_remote_copy(
                src_ref=srcs[a], dst_ref=lands[a].at[me_idx], send_sem=send_sems[j], recv_sem=recv_sems[j],
                device_id=peer, device_id_type=MESH))
    return cps


def _split_start(copies, per_array, arrs, lands, name):
    arrs, lands = list(arrs), list(lands)
    n = len(arrs)
    k = n * per_array

    def body(*refs):
        srcs, land_refs = refs[:n], refs[n:2 * n]
        send_sems, recv_sems = refs[2 * n:2 * n + k], refs[2 * n + k:2 * n + 2 * k]
        token = refs[-1]
        for cp in copies(srcs, land_refs, send_sems, recv_sems):
            cp.start()
        token[...] = jnp.zeros_like(token)

    hbm_arrs = [pltpu.with_memory_space_constraint(a, pltpu.HBM) for a in arrs]
    lands = [pltpu.with_memory_space_constraint(a, pltpu.HBM) for a in lands]
    res = pl.pallas_call(
        body, name=name,
        out_shape=[pltpu.SemaphoreType.DMA(())] * (2 * k) + [pltpu.HBM(a.shape, a.dtype) for a in arrs + lands]
        + [jax.ShapeDtypeStruct((8, 128), F32)],
        in_specs=[HBM] * (2 * n),
        out_specs=[SEM] * (2 * k) + [HBM] * (2 * n) + [pl.BlockSpec(memory_space=pltpu.VMEM)],
        input_output_aliases={a: 2 * k + a for a in range(2 * n)},
        compiler_params=pltpu.CompilerParams(has_side_effects=pltpu.SideEffectType.DATAFLOW_SIDE_EFFECTING),
    )(*hbm_arrs, *lands)
    return res[:k], res[k:2 * k], res[2 * k:2 * k + n], res[2 * k + n:2 * k + 2 * n], res[-1]


def _split_wait(copies, per_array, send_sems, recv_sems, srcs, lands, after, name):
    n = len(srcs)
    k = n * per_array

    def body(*refs):
        src_refs, land_refs = refs[:n], refs[n:2 * n]
        s_sems, r_sems = refs[2 * n:2 * n + k], refs[2 * n + k:2 * n + 2 * k]
        for cp in copies(src_refs, land_refs, s_sems, r_sems):
            cp.wait_send()
            cp.wait_recv()

    res = pl.pallas_call(
        body, name=name,
        out_shape=[pltpu.HBM(a.shape, a.dtype) for a in list(srcs) + list(lands)],
        in_specs=[HBM] * (2 * n) + [SEM] * (2 * k) + [ANY],
        out_specs=[HBM] * (2 * n),
        input_output_aliases={a: a for a in range(2 * n)},
        compiler_params=pltpu.CompilerParams(has_side_effects=pltpu.SideEffectType.DATAFLOW_SIDE_EFFECTING),
    )(*srcs, *lands, *send_sems, *recv_sems, after)
    return res[:n], res[n:]


def _chip_exchange_start(arrs, name):
    return _split_start(_chip_copies, N_PEER_CHIPS, arrs, [lax.empty(a.shape, a.dtype) for a in arrs], name)


def _chip_exchange_wait(send_sems, recv_sems, srcs, lands, after, name):
    return _split_wait(_chip_copies, N_PEER_CHIPS, send_sems, recv_sems, srcs, lands, after, name)


def _gather_start(arrs, me_idx, name):
    lands = [lax.dynamic_update_slice(lax.empty((N_DEV,) + a.shape, a.dtype), a[None], (me_idx, 0, 0)) for a in arrs]
    return _split_start(_gather_copies, N_PEERS, arrs, lands, name)


def _gather_wait(send_sems, recv_sems, srcs, lands, after, name):
    return _split_wait(_gather_copies, N_PEERS, send_sems, recv_sems, srcs, lands, after, name)[1]


def _proj_fwd(x, g_pre, w_int, tabs):
    s = x.shape[0]
    tm = min(512, s)

    def body(x_ref, g_ref, t_ref, w_hbm,
             h_ref, pa_ref, pq_ref, pkv_ref, pbz_ref, pmq_ref, pmz_ref, pg_ref, w_vm, sems):
        _load_once([(w_hbm, w_vm)], sems)
        xf = x_ref[...]
        r = lax.rsqrt(jnp.mean(xf * xf, axis=-1, keepdims=True) + EPS)
        h = ((xf * r) * g_ref[...]).astype(BF16)
        h_ref[...] = h
        cs, s1, s2 = t_ref[0], t_ref[1], t_ref[2]

        def mm(seg, c0, width):
            return _dot_nt(h, w_vm[seg[0] + c0:seg[0] + c0 + width, :])

        for c0 in range(0, SEG_A[1], 512):
            pa_ref[:, c0:c0 + 512] = mm(SEG_A, c0, 512).astype(BF16)
        q = mm(SEG_BQ, 0, 512)
        for b in range(4):
            pq_ref[:, 128 * b:128 * b + 128] = _rope(q[:, 128 * b:128 * b + 128], cs, s1, s2).astype(BF16)
        kv = mm(SEG_BKV, 0, 256)
        pkv_ref[:, 0:128] = _rope(kv[:, 0:128], cs, s1, s2).astype(BF16)
        pkv_ref[:, 128:256] = kv[:, 128:256].astype(BF16)
        pbz_ref[...] = mm(SEG_BZ, 0, 512).astype(BF16)
        pmq_ref[...] = mm(SEG_MQ, 0, 512).astype(BF16)
        pmz_ref[...] = mm(SEG_MZ, 0, 512).astype(BF16)
        for c0 in range(0, SEG_G[1], 512):
            pg_ref[:, c0:c0 + 512] = mm(SEG_G, c0, 512).astype(BF16)

    widths = (D_MODEL, 2048, 512, 256, 512, 512, 512, 3072)
    return pl.pallas_call(
        body, name="proj_fwd", grid=(s // tm,),
        out_shape=[jax.ShapeDtypeStruct((s, w), BF16) for w in widths],
        in_specs=[_rows(tm, D_MODEL), _full((1, D_MODEL)), pl.BlockSpec((3, tm, 128), lambda i: (0, i, 0)), ANY],
        out_specs=[_rows(tm, w) for w in widths],
        scratch_shapes=[pltpu.VMEM((IN_WIDTH, D_MODEL), BF16), pltpu.SemaphoreType.DMA((1,))],
        compiler_params=_params(),
    )(x, g_pre, tabs, w_int)


def _mem_kv_fwd(mem, g_mem, w_mkv):
    m = mem.shape[0]

    def body(mem_ref, g_ref, w_ref, mn_ref, mkv_ref):
        xf = mem_ref[...]
        r = lax.rsqrt(jnp.mean(xf * xf, axis=-1, keepdims=True) + EPS)
        mn = ((xf * r) * g_ref[...]).astype(BF16)
        mn_ref[...] = mn
        mkv_ref[...] = _dot(mn, w_ref[...]).astype(BF16)

    return pl.pallas_call(
        body, name="mem_kv_fwd", grid=(1,),
        out_shape=[jax.ShapeDtypeStruct((m, D_MODEL), BF16)] * 2,
        in_specs=[_full((m, D_MODEL)), _full((1, D_MODEL)), _full((D_MODEL, D_MODEL))],
        out_specs=[_full((m, D_MODEL))] * 2,
        compiler_params=_params(12),
    )(mem, g_mem, w_mkv)


def _halo_specs(s, tm, rows, width):
    nblk = s // rows
    prev = pl.BlockSpec((rows, width), lambda i: (jnp.maximum(i * (tm // rows) - 1, 0), 0))
    nxt = pl.BlockSpec((rows, width), lambda i: (jnp.minimum((i + 1) * (tm // rows), nblk - 1), 0))
    return prev, nxt


def _conv_common(pa, cu_prev, cu_next, w, tm):
    b, c, u, z = (pa[:, 512 * k:512 * k + 512] for k in range(4))
    cu = c * u
    row = lax.broadcasted_iota(jnp.int32, (tm, 512), 0)
    cu_m1 = jnp.where(row == 0, cu_prev, pltpu.roll(cu, 1, 0))
    cu_p1 = jnp.where(row == tm - 1, cu_next, pltpu.roll(cu, tm - 1, 0))
    y = cu_m1 * w[0:1] + cu * w[1:2] + cu_p1 * w[2:3]
    sig = _sigmoid(z)
    return b, c, u, z, cu, cu_m1, cu_p1, y, sig, row


def _conv_fwd(pa, w_conv):
    s = pa.shape[0]
    tm = min(512, s)
    nt = s // tm

    def body(pa_ref, pp_ref, pn_ref, w_ref, ya_ref):
        i = pl.program_id(0)
        prev_row = pp_ref[...].astype(F32)[15:16, :]
        next_row = pn_ref[...].astype(F32)[0:1, :]
        b, _, _, z, _, _, _, y, sig, _ = _conv_common(
            pa_ref[...].astype(F32),
            jnp.where(i == 0, 0.0, prev_row[:, 512:1024] * prev_row[:, 1024:1536]),
            jnp.where(i == nt - 1, 0.0, next_row[:, 512:1024] * next_row[:, 1024:1536]), w_ref[...], tm)
        ya_ref[...] = (b * y * (z * sig)).astype(BF16)

    prev, nxt = _halo_specs(s, tm, 16, 2048)
    return pl.pallas_call(
        body, name="conv_fwd", grid=(nt,),
        out_shape=jax.ShapeDtypeStruct((s, 512), BF16),
        in_specs=[_rows(tm, 2048), prev, nxt, _full((3, 512))],
        out_specs=_rows(tm, 512),
        compiler_params=_params(),
    )(pa, pa, pa, w_conv)


def _heads_to_lanes(a, g, row):
    low = row < HEAD_DIM
    parts = []
    for b in (2 * g, 2 * g + 1):
        t = jnp.transpose(a[:, 128 * b:128 * b + 128])
        swapped = pltpu.roll(t, HEAD_DIM, 0)
        if g == 0:
            parts += [jnp.where(low, t, 0.0), jnp.where(low, swapped, 0.0)]
        else:
            parts += [jnp.where(low, 0.0, swapped), jnp.where(low, 0.0, t)]
    return jnp.concatenate(parts, axis=1)


def _lanes_to_heads(t0, t1, row):
    low = row < HEAD_DIM
    blocks = []
    for b in range(4):
        g = b // 2
        tg = (t0, t1)[g]
        je = 2 * (b - 2 * g)
        even, odd = tg[:, 128 * je:128 * je + 128], tg[:, 128 * je + 128:128 * je + 256]
        if g == 0:
            t = jnp.where(low, even, pltpu.roll(odd, HEAD_DIM, 0))
        else:
            t = jnp.where(low, pltpu.roll(even, HEAD_DIM, 0), odd)
        blocks.append(jnp.transpose(t))
    return jnp.concatenate(blocks, axis=1)


WINDOW_KEYS = 3 * ATTN_BLOCK
STACKED = 4 * ATTN_BLOCK
KEY_CHUNK = 32
MAX_BLOCKS_IN_STEP = 8


def _fill_band_bias(bias, nb):
    assert nb >= 2
    c = lax.broadcasted_iota(jnp.int32, (WINDOW_KEYS, STACKED), 0)
    r = lax.broadcasted_iota(jnp.int32, (WINDOW_KEYS, STACKED), 1) & (ATTN_BLOCK - 1)
    band = (c >= r) & (c <= r + 2 * ATTN_BLOCK)
    for v, ok in enumerate((band, band & (c >= ATTN_BLOCK), band & (c < 2 * ATTN_BLOCK))):
        bias[v] = jnp.where(ok, 0.0, -jnp.inf)


def _bias_variant(n, nb):
    return jnp.where(n == 0, 1, jnp.where(n == nb - 1, 2, 0))


def _sink_row(sink_ref, g):
    return jnp.concatenate([jnp.full((1, ATTN_BLOCK), sink_ref[4 * g + j], F32) for j in range(4)], axis=1)


def _softmax_keys_major(sc, bias, variant, sink, e_scr):
    chunks = [pl.ds(k * KEY_CHUNK, KEY_CHUNK) for k in range(WINDOW_KEYS // KEY_CHUNK)]
    rows = [slice(k * KEY_CHUNK, (k + 1) * KEY_CHUNK) for k in range(WINDOW_KEYS // KEY_CHUNK)]
    m_run = jnp.full((KEY_CHUNK, STACKED), -jnp.inf, F32)
    for ck, rw in zip(chunks, rows):
        m_run = jnp.maximum(m_run, sc[rw] + bias[variant, ck, :])
    m = jnp.maximum(jnp.max(m_run, axis=0, keepdims=True), sink)
    l_run = jnp.zeros((KEY_CHUNK, STACKED), F32)
    for ck, rw in zip(chunks, rows):
        e = jnp.exp(sc[rw] + bias[variant, ck, :] - m)
        l_run += e
        e_scr[rw, :] = e.astype(BF16)
    es = jnp.exp(sink - m)
    inv = 1.0 / (jnp.sum(l_run, axis=0, keepdims=True) + es)
    return inv, es * inv


def _fill_padded(kv_ref, kpad, vpad, s):
    zero = jnp.zeros((ATTN_BLOCK, 128), BF16)
    kpad[0:ATTN_BLOCK, :] = zero
    vpad[0:ATTN_BLOCK, :] = zero
    kpad[ATTN_BLOCK + s:2 * ATTN_BLOCK + s, :] = zero
    vpad[ATTN_BLOCK + s:2 * ATTN_BLOCK + s, :] = zero
    kpad[ATTN_BLOCK:ATTN_BLOCK + s, :] = kv_ref[:, 0:128]
    vpad[ATTN_BLOCK:ATTN_BLOCK + s, :] = kv_ref[:, 128:256]


def _attn_fwd(pq, pkv, pbz, sink):
    s = pq.shape[0]
    nb = s // ATTN_BLOCK
    bps = min(MAX_BLOCKS_IN_STEP, nb)

    def body(sink_ref, q_ref, z_ref, kv_ref, yb_ref, kpad, vpad, bias, e_scr):
        i = pl.program_id(0)

        @pl.when(i == 0)
        def _():
            _fill_padded(kv_ref, kpad, vpad, s)
            _fill_band_bias(bias, nb)

        row = lax.broadcasted_iota(jnp.int32, (ATTN_BLOCK, 128), 0)
        for b in range(bps):
            n = i * bps + b
            rows = slice(b * ATTN_BLOCK, (b + 1) * ATTN_BLOCK)
            start = pl.multiple_of(n * ATTN_BLOCK, ATTN_BLOCK)
            kw, vw = kpad[pl.ds(start, WINDOW_KEYS), :], vpad[pl.ds(start, WINDOW_KEYS), :]
            qf = q_ref[rows, :].astype(F32)
            variant = _bias_variant(n, nb)
            outs = []
            for g in range(2):
                e_bg = e_scr.at[2 * b + g]
                qt = (_heads_to_lanes(qf, g, row) * ATTN_SCALE).astype(BF16)
                inv, _ = _softmax_keys_major(_dot(kw, qt), bias, variant, _sink_row(sink_ref, g), e_bg)
                outs.append(_dot_tn(vw, e_bg[...]) * inv)
            attn = _lanes_to_heads(outs[0], outs[1], row)
            z = z_ref[rows, :].astype(F32)
            yb_ref[rows, :] = (attn * (z * _sigmoid(z))).astype(BF16)

    tq = bps * ATTN_BLOCK
    return pl.pallas_call(
        body, name="attn_fwd", grid=(s // tq,),
        out_shape=jax.ShapeDtypeStruct((s, 512), BF16),
        in_specs=[pl.BlockSpec(memory_space=pltpu.SMEM), _rows(tq, 512), _rows(tq, 512), _full((s, 256))],
        out_specs=_rows(tq, 512),
        scratch_shapes=[pltpu.VMEM((s + 2 * ATTN_BLOCK, 128), BF16)] * 2
        + [pltpu.VMEM((3, WINDOW_KEYS, STACKED), F32),
           pltpu.VMEM((2 * bps, WINDOW_KEYS, STACKED), BF16)],
        compiler_params=_params(),
    )(sink, pq, pbz, pkv)


def _mem_softmax_t(q, mk):
    sc = _dot_nt(mk, q) * MEM_SCALE
    e = jnp.exp(sc - jnp.max(sc, axis=0, keepdims=True))
    return e * (1.0 / jnp.sum(e, axis=0, keepdims=True))


def _mem_attn_fwd(pmq, pmz, mkv):
    s = pmq.shape[0]
    m = mkv.shape[0]
    tm = min(512, s)

    def body(q_ref, z_ref, mk_ref, mv_ref, ym_ref):
        z = z_ref[...].astype(F32)
        sz = z * _sigmoid(z)
        for h in range(MEM_HEADS):
            cols = slice(128 * h, 128 * h + 128)
            pt = _mem_softmax_t(q_ref[:, cols], mk_ref[:, cols])
            o = _dot_tn(pt.astype(BF16), mv_ref[:, cols])
            ym_ref[:, cols] = (o * sz[:, cols]).astype(BF16)

    return pl.pallas_call(
        body, name="mem_attn_fwd", grid=(s // tm,),
        out_shape=jax.ShapeDtypeStruct((s, 512), BF16),
        in_specs=[_rows(tm, 512), _rows(tm, 512), pl.BlockSpec((m, 512), lambda i: (0, 0)),
                  pl.BlockSpec((m, 512), lambda i: (0, 1))],
        out_specs=_rows(tm, 512),
        compiler_params=_params(),
    )(pmq, pmz, mkv, mkv)


def _mid(ya, yb, ym, pg, x, target, g_post, w_up, w_out):
    s = x.shape[0]
    tm = min(256, s)
    nt = s // tm

    def body(ya_ref, yb_ref, ym_ref, pg_ref, x_ref, t_ref, gp_ref, wup_hbm, wout_hbm,
             dg_ref, dya_ref, dyb_ref, dym_ref, dy_ref, loss_ref, ggp_ref, mb_ref, dob_ref, du_ref,
             wup_vm, wout_vm, sems):
        i = pl.program_id(0)
        _load_once([(wup_hbm.at[d], wup_vm.at[:, pl.ds(128 * d, 128)]) for d in range(N_DEV)]
                   + [(wout_hbm, wout_vm)], sems)

        @pl.when(i == 0)
        def _():
            loss_ref[...] = jnp.zeros_like(loss_ref)
            ggp_ref[...] = jnp.zeros_like(ggp_ref)

        ys = (ya_ref[...], yb_ref[...], ym_ref[...])
        us = [_dot(ys[k], wup_vm[512 * k:512 * k + 512, :]) for k in range(3)]
        gates = [_sigmoid(pg_ref[:, 1024 * k:1024 * k + 1024].astype(F32)) for k in range(3)]
        merged = gates[0] * us[0] + gates[1] * us[1] + gates[2] * us[2]
        mb = merged.astype(BF16)
        mb_ref[...] = mb
        out = _dot(mb, wout_vm[...])
        r = lax.rsqrt(jnp.mean(out * out, axis=-1, keepdims=True) + EPS)
        on = out * r
        gp = gp_ref[...]
        err = (x_ref[...] + on * gp) - t_ref[...]
        loss_ref[...] += 0.5 * jnp.sum(err * err) * (1.0 / D_MODEL)
        dy = err * (1.0 / D_MODEL)
        dy_ref[...] = dy
        ggp_ref[...] += jnp.sum(dy * on, axis=0, keepdims=True)
        a = dy * gp
        d_out = r * (a - on * jnp.mean(a * on, axis=-1, keepdims=True))
        dob = d_out.astype(BF16)
        dob_ref[...] = dob
        d_merged = _dot_nt(dob, wout_vm[...])
        d_refs = (dya_ref, dyb_ref, dym_ref)
        for k in range(3):
            g = gates[k]
            du_f = d_merged * g
            dg_ref[:, 1024 * k:1024 * k + 1024] = (du_f * us[k] * (1.0 - g)).astype(BF16)
            du = du_f.astype(BF16)
            du_ref[k] = du
            d_refs[k][...] = _dot_nt(du, wup_vm[512 * k:512 * k + 512, :]).astype(BF16)

    return pl.pallas_call(
        body, name="mid", grid=(nt,),
        out_shape=[jax.ShapeDtypeStruct((s, 3072), BF16)] + [jax.ShapeDtypeStruct((s, 512), BF16)] * 3
        + [jax.ShapeDtypeStruct((s, D_MODEL), F32), jax.ShapeDtypeStruct((8, 128), F32),
           jax.ShapeDtypeStruct((1, D_MODEL), F32), jax.ShapeDtypeStruct((s, D_MODEL), BF16),
           jax.ShapeDtypeStruct((s, D_MODEL), BF16), jax.ShapeDtypeStruct((3, s, D_MODEL), BF16)],
        in_specs=[_rows(tm, 512)] * 3 + [_rows(tm, 3072), _rows(tm, D_MODEL), _rows(tm, D_MODEL),
                                         _full((1, D_MODEL)), ANY, ANY],
        out_specs=[_rows(tm, 3072)] + [_rows(tm, 512)] * 3
        + [_rows(tm, D_MODEL), _full((8, 128)), _full((1, D_MODEL)), _rows(tm, D_MODEL), _rows(tm, D_MODEL),
           pl.BlockSpec((3, tm, D_MODEL), lambda i: (0, i, 0))],
        scratch_shapes=[pltpu.VMEM((1536, D_MODEL), BF16), pltpu.VMEM((D_MODEL, D_MODEL), BF16),
                        pltpu.SemaphoreType.DMA((N_DEV + 1,))],
        compiler_params=_params(),
    )(ya, yb, ym, pg, x, target, g_post, w_up, w_out)


def _gw_mid(mb, dob, ys, du):
    s = mb.shape[0]
    tn = 256

    def out_body(mb_ref, dob_ref, o_ref):
        o_ref[...] = _dot_tn(mb_ref[...], dob_ref[...]).astype(BF16)

    gw_out = pl.pallas_call(
        out_body, name="gw_out", grid=(D_MODEL // tn,),
        out_shape=jax.ShapeDtypeStruct((D_MODEL, D_MODEL), BF16),
        in_specs=[pl.BlockSpec((s, tn), lambda j: (0, j)), _full((s, D_MODEL))],
        out_specs=pl.BlockSpec((tn, D_MODEL), lambda j: (j, 0)),
        compiler_params=_params(),
    )(mb, dob)

    per = 512 // tn

    def up_body(ya_ref, yb_ref, ym_ref, du_ref, o_ref):
        j = pl.program_id(0)
        for k, y_ref in enumerate((ya_ref, yb_ref, ym_ref)):
            @pl.when(j // per == k)
            def _(y_ref=y_ref):
                res = _dot_tn(y_ref[...], du_ref[...])
                for d in range(N_DEV):
                    o_ref[d] = res[:, 128 * d:128 * d + 128].astype(BF16)

    def y_spec(k):
        return pl.BlockSpec((s, tn), lambda j: (0, jnp.clip(j - per * k, 0, per - 1)))

    gw_up = pl.pallas_call(
        up_body, name="gw_up", grid=(3 * per,),
        out_shape=jax.ShapeDtypeStruct((N_DEV, 1536, 128), BF16),
        in_specs=[y_spec(0), y_spec(1), y_spec(2), pl.BlockSpec((None, s, D_MODEL), lambda j: (j // per, 0, 0))],
        out_specs=pl.BlockSpec((N_DEV, tn, 128), lambda j: (0, j, 0)),
        compiler_params=_params(),
    )(*ys, du)
    return gw_out, gw_up


def _conv_bwd(pa, dya, w_conv):
    s = pa.shape[0]
    tm = min(512, s)
    nt = s // tm

    def body(pa_ref, pp_ref, pn_ref, d_ref, dp_ref, dn_ref, w_ref, da_ref, gw_ref):
        i = pl.program_id(0)
        first, last = i == 0, i == nt - 1

        @pl.when(first)
        def _():
            gw_ref[...] = jnp.zeros_like(gw_ref)

        w = w_ref[...]
        prev_row = pp_ref[...].astype(F32)[15:16, :]
        next_row = pn_ref[...].astype(F32)[0:1, :]
        b, c, u, z, cu, cu_m1, cu_p1, y, sig, row = _conv_common(
            pa_ref[...].astype(F32),
            jnp.where(first, 0.0, prev_row[:, 512:1024] * prev_row[:, 1024:1536]),
            jnp.where(last, 0.0, next_row[:, 512:1024] * next_row[:, 1024:1536]), w, tm)
        sz = z * sig
        dya_t = d_ref[...].astype(F32)
        d_y = dya_t * b * sz

        def halo_dy(p_row, d_row):
            zz = p_row[:, 1536:2048]
            return d_row * p_row[:, 0:512] * (zz * _sigmoid(zz))

        dy_prev = jnp.where(first, 0.0, halo_dy(prev_row, dp_ref[...].astype(F32)[15:16, :]))
        dy_next = jnp.where(last, 0.0, halo_dy(next_row, dn_ref[...].astype(F32)[0:1, :]))
        dy_m1 = jnp.where(row == 0, dy_prev, pltpu.roll(d_y, 1, 0))
        dy_p1 = jnp.where(row == tm - 1, dy_next, pltpu.roll(d_y, tm - 1, 0))
        d_cu = dy_p1 * w[0:1] + d_y * w[1:2] + dy_m1 * w[2:3]
        da_ref[:, 0:512] = (dya_t * y * sz).astype(BF16)
        da_ref[:, 512:1024] = (d_cu * u).astype(BF16)
        da_ref[:, 1024:1536] = (d_cu * c).astype(BF16)
        da_ref[:, 1536:2048] = (dya_t * b * y * (sig + sz * (1.0 - sig))).astype(BF16)
        gw_ref[0:1, :] += jnp.sum(d_y * cu_m1, axis=0, keepdims=True)
        gw_ref[1:2, :] += jnp.sum(d_y * cu, axis=0, keepdims=True)
        gw_ref[2:3, :] += jnp.sum(d_y * cu_p1, axis=0, keepdims=True)

    prev, nxt = _halo_specs(s, tm, 16, 2048)
    dprev, dnxt = _halo_specs(s, tm, 16, 512)
    return pl.pallas_call(
        body, name="conv_bwd", grid=(nt,),
        out_shape=[jax.ShapeDtypeStruct((s, 2048), BF16), jax.ShapeDtypeStruct((8, 512), F32)],
        in_specs=[_rows(tm, 2048), prev, nxt, _rows(tm, 512), dprev, dnxt, _full((3, 512))],
        out_specs=[_rows(tm, 2048), _full((8, 512))],
        compiler_params=_params(),
    )(pa, pa, pa, dya, dya, dya, w_conv)


def _attn_bwd(pq, pkv, pbz, dyb, sink, tabs):
    s = pq.shape[0]
    nb = s // ATTN_BLOCK
    bps = min(MAX_BLOCKS_IN_STEP, nb)

    def body(sink_ref, q_ref, z_ref, d_ref, kv_ref, t_ref,
             dq_ref, dz_ref, dkv_ref, gs_ref, kpad, vpad, dk_acc, dv_acc, bias, e_scr, ds_scr):
        i = pl.program_id(0)

        @pl.when(i == 0)
        def _():
            _fill_padded(kv_ref, kpad, vpad, s)
            _fill_band_bias(bias, nb)
            dk_acc[...] = jnp.zeros_like(dk_acc)
            dv_acc[...] = jnp.zeros_like(dv_acc)
            gs_ref[...] = jnp.zeros_like(gs_ref)

        row = lax.broadcasted_iota(jnp.int32, (ATTN_BLOCK, 128), 0)
        for b in range(bps):
            n = i * bps + b
            rows = slice(b * ATTN_BLOCK, (b + 1) * ATTN_BLOCK)
            start = pl.multiple_of(n * ATTN_BLOCK, ATTN_BLOCK)
            kw, vw = kpad[pl.ds(start, WINDOW_KEYS), :], vpad[pl.ds(start, WINDOW_KEYS), :]
            qf = q_ref[rows, :].astype(F32)
            variant = _bias_variant(n, nb)
            z = z_ref[rows, :].astype(F32)
            sig = _sigmoid(z)
            dyb_t = d_ref[rows, :].astype(F32)
            d_attn = dyb_t * (z * sig)
            outs, dqs = [], []
            dk_w = jnp.zeros((WINDOW_KEYS, 128), F32)
            dv_w = jnp.zeros((WINDOW_KEYS, 128), F32)
            for g in range(2):
                e_bg, ds_bg = e_scr.at[2 * b + g], ds_scr.at[2 * b + g]
                qt = _heads_to_lanes(qf, g, row)
                inv, p_sink = _softmax_keys_major(
                    _dot(kw, (qt * ATTN_SCALE).astype(BF16)), bias, variant, _sink_row(sink_ref, g), e_bg)
                ot = _dot_tn(vw, e_bg[...]) * inv
                outs.append(ot)
                dot_ = _heads_to_lanes(d_attn, g, row)
                delta = jnp.sum(dot_ * ot, axis=0, keepdims=True)
                dpt = _dot(vw, dot_.astype(BF16))
                for k in range(WINDOW_KEYS // KEY_CHUNK):
                    rw = slice(k * KEY_CHUNK, (k + 1) * KEY_CHUNK)
                    ds_bg[rw, :] = (e_bg[rw, :].astype(F32) * (dpt[rw] - delta)).astype(BF16)
                sink_part = p_sink * delta
                for j in range(4):
                    h = 4 * g + j
                    gs_ref[h:h + 1, :] -= jnp.sum(sink_part[:, 128 * j:128 * j + 128])
                dqs.append(_dot_tn(kw, ds_bg[...]) * (inv * ATTN_SCALE))
                dk_w += _dot_nt(ds_bg[...], (qt * inv).astype(BF16)) * ATTN_SCALE
                dv_w += _dot_nt(e_bg[...], (dot_ * inv).astype(BF16))
            dk_acc[pl.ds(start, WINDOW_KEYS), :] += dk_w
            dv_acc[pl.ds(start, WINDOW_KEYS), :] += dv_w
            attn = _lanes_to_heads(outs[0], outs[1], row)
            dz_ref[rows, :] = (dyb_t * attn * (sig * (1.0 + z * (1.0 - sig)))).astype(BF16)
            dq = _lanes_to_heads(dqs[0], dqs[1], row)
            trows = pl.ds(start, ATTN_BLOCK)
            cs, s1, s2 = t_ref[0, trows, :], t_ref[1, trows, :], t_ref[2, trows, :]
            for blk in range(4):
                cols = slice(128 * blk, 128 * blk + 128)
                dq_ref[rows, cols] = _rope_t(dq[:, cols], cs, s1, s2).astype(BF16)

        @pl.when(i == nb // bps - 1)
        def _():
            dk = dk_acc[ATTN_BLOCK:ATTN_BLOCK + s, :]
            dkv_ref[:, 0:128] = _rope_t(dk, t_ref[0], t_ref[1], t_ref[2]).astype(BF16)
            dkv_ref[:, 128:256] = dv_acc[ATTN_BLOCK:ATTN_BLOCK + s, :].astype(BF16)

    tq = bps * ATTN_BLOCK
    tile = _rows(tq, 512)
    return pl.pallas_call(
        body, name="attn_bwd", grid=(s // tq,),
        out_shape=[jax.ShapeDtypeStruct((s, 512), BF16), jax.ShapeDtypeStruct((s, 512), BF16),
                   jax.ShapeDtypeStruct((s, 256), BF16), jax.ShapeDtypeStruct((8, 128), F32)],
        in_specs=[pl.BlockSpec(memory_space=pltpu.SMEM), tile, tile, tile, _full((s, 256)), _full((3, s, 128))],
        out_specs=[tile, tile, _full((s, 256)), _full((8, 128))],
        scratch_shapes=[pltpu.VMEM((s + 2 * ATTN_BLOCK, 128), BF16)] * 2
        + [pltpu.VMEM((s + 2 * ATTN_BLOCK, 128), F32)] * 2
        + [pltpu.VMEM((3, WINDOW_KEYS, STACKED), F32)]
        + [pltpu.VMEM((2 * bps, WINDOW_KEYS, STACKED), BF16)] * 2,
        compiler_params=_params(),
    )(sink, pq, pbz, dyb, pkv, tabs)


def _mem_attn_bwd(pmq, pmz, mkv, dym):
    s = pmq.shape[0]
    m = mkv.shape[0]
    tm = min(512, s)

    def body(q_ref, z_ref, d_ref, mk_ref, mv_ref, dq_ref, dz_ref, dmkv_ref):
        @pl.when(pl.program_id(0) == 0)
        def _():
            dmkv_ref[...] = jnp.zeros_like(dmkv_ref)

        z = z_ref[...].astype(F32)
        sig = _sigmoid(z)
        dym_t = d_ref[...].astype(F32)
        d_attn = dym_t * (z * sig)
        dsilu = sig * (1.0 + z * (1.0 - sig))
        for h in range(MEM_HEADS):
            cols = slice(128 * h, 128 * h + 128)
            q, mk, mv = q_ref[:, cols], mk_ref[:, cols], mv_ref[:, cols]
            pt = _mem_softmax_t(q, mk)
            pb = pt.astype(BF16)
            o = _dot_tn(pb, mv)
            dob = d_attn[:, cols].astype(BF16)
            dpt = _dot_nt(mv, dob)
            dst = (pt * (dpt - jnp.sum(pt * dpt, axis=0, keepdims=True))).astype(BF16)
            dq_ref[:, cols] = (_dot_tn(dst, mk) * MEM_SCALE).astype(BF16)
            dz_ref[:, cols] = (dym_t[:, cols] * o * dsilu[:, cols]).astype(BF16)
            dmkv_ref[:, cols] += _dot(dst, q) * MEM_SCALE
            dmkv_ref[:, 512 + 128 * h:512 + 128 * h + 128] += _dot(pb, dob)

    return pl.pallas_call(
        body, name="mem_attn_bwd", grid=(s // tm,),
        out_shape=[jax.ShapeDtypeStruct((s, 512), BF16), jax.ShapeDtypeStruct((s, 512), BF16),
                   jax.ShapeDtypeStruct((m, D_MODEL), F32)],
        in_specs=[_rows(tm, 512), _rows(tm, 512), _rows(tm, 512), pl.BlockSpec((m, 512), lambda i: (0, 0)),
                  pl.BlockSpec((m, 512), lambda i: (0, 1))],
        out_specs=[_rows(tm, 512), _rows(tm, 512), _full((m, D_MODEL))],
        compiler_params=_params(),
    )(pmq, pmz, dym, mkv, mkv)


def _mem_kv_bwd(mem, g_mem, mn, dmkv, w_mkv):
    m = mem.shape[0]

    def body(mem_ref, g_ref, mn_ref, d_ref, w_ref, gw_ref, gg_ref):
        db = d_ref[...].astype(BF16)
        gw_ref[...] = _dot_tn(mn_ref[...], db).astype(BF16)
        d_mn = _dot_nt(db, w_ref[...])
        xf = mem_ref[...]
        r = lax.rsqrt(jnp.mean(xf * xf, axis=-1, keepdims=True) + EPS)
        gg_ref[...] = jnp.sum(d_mn * (xf * r), axis=0, keepdims=True)

    return pl.pallas_call(
        body, name="mem_kv_bwd", grid=(1,),
        out_shape=[jax.ShapeDtypeStruct((D_MODEL, D_MODEL), BF16), jax.ShapeDtypeStruct((1, D_MODEL), F32)],
        in_specs=[_full((m, D_MODEL)), _full((1, D_MODEL)), _full((m, D_MODEL)), _full((m, D_MODEL)),
                  _full((D_MODEL, D_MODEL))],
        out_specs=[_full((D_MODEL, D_MODEL)), _full((1, D_MODEL))],
        compiler_params=_params(16),
    )(mem, g_mem, mn, dmkv, w_mkv)


def _dh_bwd(dparts, x, dy, g_pre, w_int):
    s = x.shape[0]
    tm = min(256, s)

    def body(*refs):
        d_refs = refs[:7]
        x_ref, dy_ref, g_ref, w_hbm, gx_ref, gg_ref, w_vm, sems = refs[7:]
        _load_once([(w_hbm, w_vm)], sems)

        @pl.when(pl.program_id(0) == 0)
        def _():
            gg_ref[...] = jnp.zeros_like(gg_ref)

        d_h = jnp.zeros((tm, D_MODEL), F32)
        for d_ref, (r0, width) in zip(d_refs, SEGS):
            for c0 in range(0, width, 512):
                cw = min(512, width - c0)
                d_h += _dot(d_ref[:, c0:c0 + cw], w_vm[r0 + c0:r0 + c0 + cw, :])
        xf = x_ref[...]
        r = lax.rsqrt(jnp.mean(xf * xf, axis=-1, keepdims=True) + EPS)
        xn = xf * r
        a = d_h * g_ref[...]
        gx_ref[...] = r * (a - xn * jnp.mean(a * xn, axis=-1, keepdims=True)) + dy_ref[...]
        gg_ref[...] += jnp.sum(d_h * xn, axis=0, keepdims=True)

    return pl.pallas_call(
        body, name="dh_bwd", grid=(s // tm,),
        out_shape=[jax.ShapeDtypeStruct((s, D_MODEL), F32), jax.ShapeDtypeStruct((1, D_MODEL), F32)],
        in_specs=[_rows(tm, w) for _, w in SEGS] + [_rows(tm, D_MODEL), _rows(tm, D_MODEL), _full((1, D_MODEL)), ANY],
        out_specs=[_rows(tm, D_MODEL), _full((1, D_MODEL))],
        scratch_shapes=[pltpu.VMEM((IN_WIDTH, D_MODEL), BF16), pltpu.SemaphoreType.DMA((1,))],
        compiler_params=_params(),
    )(*dparts, x, dy, g_pre, w_int)


def _gw_in(dparts, h):
    s = h.shape[0]
    tn = 256
    starts, counts = [], []
    for r0, width in SEGS:
        starts.append(r0 // tn)
        counts.append(width // tn)

    def body(*refs):
        d_refs = refs[:7]
        h_hbm, o_ref, h_vm, sems = refs[7:]
        _load_once([(h_hbm, h_vm)], sems)
        j = pl.program_id(0)
        for d_ref, st, cnt in zip(d_refs, starts, counts):
            @pl.when((j >= st) & (j < st + cnt))
            def _(d_ref=d_ref):
                o_ref[...] = _dot_tn(d_ref[...], h_vm[...]).astype(BF16)

    def seg_spec(st, cnt):
        return pl.BlockSpec((s, tn), lambda j: (0, jnp.clip(j - st, 0, cnt - 1)))

    return pl.pallas_call(
        body, name="gw_in", grid=(IN_WIDTH // tn,),
        out_shape=jax.ShapeDtypeStruct((IN_WIDTH, D_MODEL), BF16),
        in_specs=[seg_spec(st, cnt) for st, cnt in zip(starts, counts)] + [ANY],
        out_specs=pl.BlockSpec((tn, D_MODEL), lambda j: (j, 0)),
        scratch_shapes=[pltpu.VMEM((s, D_MODEL), BF16), pltpu.SemaphoreType.DMA((1,))],
        compiler_params=_params(),
    )(*dparts, h)


def _adamw_math(w, g, m, v):
    m2 = ADAM_B1 * m + (1.0 - ADAM_B1) * g
    v2 = ADAM_B2 * v + (1.0 - ADAM_B2) * (g * g)
    m_hat = m2 / (1.0 - ADAM_B1 ** ADAM_STEP)
    v_hat = v2 / (1.0 - ADAM_B2 ** ADAM_STEP)
    delta = -ADAM_LR * (m_hat / (jnp.sqrt(v_hat) + ADAM_EPS) + ADAM_WD * w)
    return delta, m2, v2


def _sum_adamw(own, land, chip, block, w, m, v, name, tiles=1):
    r, c = w.shape
    rt = r // tiles

    def body(c_ref, own_ref, l1_ref, l2_ref, l3_ref, w_ref, m_ref, v_ref, g_ref, d_ref, m2_ref, v2_ref):
        g = own_ref[...].astype(F32)
        for l_ref in (l1_ref, l2_ref, l3_ref):
            g += l_ref[...].astype(F32)
        g_ref[...] = g
        d_ref[...], m2_ref[...], v2_ref[...] = _adamw_math(w_ref[...], g, m_ref[...], v_ref[...])

    def share(k):
        return pl.BlockSpec((None, rt, c), lambda i, c_ref: (jnp.bitwise_xor(c_ref[0], k), block * tiles + i, 0))

    spec = pl.BlockSpec((rt, c), lambda i, c_ref: (i, 0))
    grid_spec = pltpu.PrefetchScalarGridSpec(
        num_scalar_prefetch=1, grid=(tiles,),
        in_specs=[share(0), share(1), share(2), share(3)] + [spec] * 3, out_specs=[spec] * 4)
    return pl.pallas_call(
        body, name=name, grid_spec=grid_spec,
        out_shape=[jax.ShapeDtypeStruct((r, c), F32)] * 4,
        compiler_params=_params(),
    )(chip, own, land, land, land, w, m, v)


def _sum_adamw_group(items, chip, name):
    k = len(items)

    def body(c_ref, *refs):
        shares, wmv, outs = refs[:4 * k], refs[4 * k:7 * k], refs[7 * k:]
        for j in range(k):
            g = shares[4 * j][...].astype(F32)
            for l_ref in shares[4 * j + 1:4 * j + 4]:
                g += l_ref[...].astype(F32)
            outs[4 * j][...] = g
            outs[4 * j + 1][...], outs[4 * j + 2][...], outs[4 * j + 3][...] = _adamw_math(
                wmv[3 * j][...], g, wmv[3 * j + 1][...], wmv[3 * j + 2][...])

    def share(shape, block, q):
        return pl.BlockSpec((None,) + shape, lambda i, c_ref: (jnp.bitwise_xor(c_ref[0], q), block, 0))

    in_specs, args = [], []
    for own, land, block, w, m, v in items:
        in_specs += [share(w.shape, block, q) for q in range(4)]
        args += [own, land, land, land]
    for own, land, block, w, m, v in items:
        in_specs += [pl.BlockSpec(w.shape, lambda i, c_ref: (0, 0))] * 3
        args += [w, m, v]
    out_specs = [pl.BlockSpec(w.shape, lambda i, c_ref: (0, 0)) for _, _, _, w, _, _ in items for _ in range(4)]
    res = pl.pallas_call(
        body, name=name,
        grid_spec=pltpu.PrefetchScalarGridSpec(num_scalar_prefetch=1, grid=(1,), in_specs=in_specs,
                                               out_specs=out_specs),
        out_shape=[jax.ShapeDtypeStruct(w.shape, F32) for _, _, _, w, _, _ in items for _ in range(4)],
        compiler_params=_params(12),
    )(chip, *args)
    return [res[4 * j:4 * j + 4] for j in range(k)]


def _small_step(parts, ws, ms, vs):
    def exchange(gpre_ref, gconv_ref, gsink_ref, gmem_ref, gpost_ref, loss_ref, tot_ref,
                 pack, gathered, send_sems, recv_sems):
        x, y, c = _my_place()
        me_idx = 4 * x + 2 * y + c

        lane = lax.broadcasted_iota(jnp.int32, (1, 128), 1)
        sink_row = jnp.zeros((1, 128), F32)
        for h in range(8):
            sink_row = jnp.where(lane == h, gsink_ref[h:h + 1, :], sink_row)
        pack[...] = jnp.zeros_like(pack)
        pack[0:1, :] = gpre_ref[...]
        pack[1:2, :] = gmem_ref[...]
        pack[2:3, :] = gpost_ref[...]
        pack[3:6, 0:512] = gconv_ref[0:3, :]
        pack[6:7, 0:128] = sink_row
        pack[7:8, 0:128] = loss_ref[0:1, :]

        flips = [(0, 0, 1), (0, 1, 0), (1, 0, 0), (0, 1, 1), (1, 0, 1), (1, 1, 0), (1, 1, 1)]
        cps = []
        for k, (fx, fy, fc) in enumerate(flips):
            peer = ((1 - x) if fx else x, (1 - y) if fy else y, (1 - c) if fc else c)
            cps.append(pltpu.make_async_remote_copy(
                src_ref=pack, dst_ref=gathered.at[me_idx], send_sem=send_sems.at[k], recv_sem=recv_sems.at[k],
                device_id=peer, device_id_type=MESH))
        for cp in cps:
            cp.start()
        gathered[me_idx] = pack[...]
        for cp in cps:
            cp.wait_recv()
        for cp in cps:
            cp.wait_send()
        tot = gathered[0]
        for d in range(1, N_DEV):
            tot = tot + gathered[d]
        tot_ref[...] = tot

    tot = pl.pallas_call(
        exchange, name="small_exchange", grid=(1,),
        out_shape=jax.ShapeDtypeStruct((8, D_MODEL), F32),
        in_specs=[_full(p.shape) for p in parts], out_specs=_full((8, D_MODEL)),
        scratch_shapes=[pltpu.VMEM((8, D_MODEL), F32), pltpu.VMEM((N_DEV, 8, D_MODEL), F32),
                        pltpu.SemaphoreType.DMA((N_PEERS,)), pltpu.SemaphoreType.DMA((N_PEERS,))],
    )(*parts)

    def apply(tot_ref, *refs):
        w_refs, m_refs, v_refs = refs[0:5], refs[5:10], refs[10:15]
        loss_out = refs[15]
        g_outs, d_outs, m_outs, v_outs = refs[16:21], refs[21:26], refs[26:31], refs[31:36]
        x, y, c = _my_place()
        tot = tot_ref[...]
        conv = pltpu.roll(tot[:, 0:512], (512 - 64 * (4 * x + 2 * y + c)) % 512, 1)[3:6, 0:64]
        grads = (tot[0:1, :], conv, tot[6:7, 0:8], tot[1:2, :], tot[2:3, :])
        loss_out[...] = tot[7:8, 0:128]
        for j in range(5):
            g_outs[j][...] = grads[j]
            d_outs[j][...], m_outs[j][...], v_outs[j][...] = _adamw_math(
                w_refs[j][...], grads[j], m_refs[j][...], v_refs[j][...])

    specs = [_full(w.shape) for w in ws]
    res = pl.pallas_call(
        apply, name="small_apply", grid=(1,),
        out_shape=[jax.ShapeDtypeStruct((1, 128), F32)] + [jax.ShapeDtypeStruct(w.shape, F32) for w in ws] * 4,
        in_specs=[_full((8, D_MODEL))] + specs * 3,
        out_specs=[_full((1, 128))] + specs * 4,
    )(tot, *ws, *ms, *vs)
    return res[0], res[1:6], res[6:11], res[11:16], res[16:21]


def kernel(x, mem, g_pre, w_in, w_conv, attn_sink, g_mem, w_mem_kv, w_up_a, w_up_b, w_up_m, w_out, g_post, loss_target, m_g_pre, m_w_in, m_w_conv, m_attn_sink, m_g_mem, m_w_mem_kv, m_w_up_a, m_w_up_b, m_w_up_m, m_w_out, m_g_post, v_g_pre, v_w_in, v_w_conv, v_attn_sink, v_g_mem, v_w_mem_kv, v_w_up_a, v_w_up_b, v_w_up_m, v_w_out, v_g_post):
    s = x.shape[1]
    x2, mem2, tgt2 = x[0], mem[0], loss_target[0]
    me = 4 * lax.axis_index("x") + 2 * lax.axis_index("y") + lax.axis_index("c")

    w_conv_loc = jnp.zeros((8, 128), F32).at[:3, :64].set(w_conv[0])
    w_int_g, w_conv_g, tabs, w_mkv_loc, w_out_loc, w_up_loc = _all_gather(
        [w_in[0].T.astype(BF16), w_conv_loc], "gather_w_in",
        splits=[[(112 * k, 112) for k in range(7)] + [(784, 144)], [(0, 8)]],
        side=_gather_side(s, w_mem_kv[0], w_out[0], (w_up_a[0], w_up_b[0], w_up_m[0])))
    w_int = w_int_g.reshape(IN_WIDTH, D_MODEL)
    w_conv_f = w_conv_g[:, :3, :64].transpose(1, 0, 2).reshape(3, 512)
    late = _gather_start([w_mkv_loc, w_out_loc, w_up_loc], me, "gather_late_start")
    sink = attn_sink[0]

    h, pa, pq, pkv, pbz, pmq, pmz, pg = _proj_fwd(x2, g_pre + late[4][0:1, 0:1], w_int, tabs)
    ya = _conv_fwd(pa, w_conv_f)
    yb = _attn_fwd(pq, pkv, pbz, sink)
    w_mkv_g, w_out_g, w_up_g = _gather_wait(*late[:4], yb, "gather_late_wait")
    w_mkv = w_mkv_g.reshape(D_MODEL, D_MODEL)
    w_out_f = w_out_g.reshape(D_MODEL, D_MODEL)
    mn, mkv = _mem_kv_fwd(mem2, g_mem, w_mkv)
    ym = _mem_attn_fwd(pmq, pmz, mkv)
    dg, dya, dyb, dym, dy, loss_p, gg_post, mb, dob, du = _mid(ya, yb, ym, pg, x2, tgt2, g_post, w_up_g, w_out_f)
    gw_out, gw_up = _gw_mid(mb, dob, (ya, yb, ym), du)

    core = lax.axis_index("c").astype(jnp.int32).reshape(1)
    chip = (2 * lax.axis_index("x") + lax.axis_index("y")).astype(jnp.int32).reshape(1)

    def exchange_start(shares, tag):
        from_sibling = _sibling_exchange(shares, "grads_to_sibling_" + tag)
        chip_shares = _pair_add(shares, from_sibling, core, "grads_pair_add_" + tag)
        return _chip_exchange_start(chip_shares, "grads_to_chips_start_" + tag)

    dmq, dmz, dmkv = _mem_attn_bwd(pmq, pmz, mkv, dym)
    gw_mkv, gg_mem = _mem_kv_bwd(mem2, g_mem, mn, dmkv, w_mkv)
    shares1 = [gw_mkv.reshape(N_DEV, 128, D_MODEL), gw_out.reshape(N_DEV, 128, D_MODEL), gw_up]
    sib = _split_start(_sibling_copies, N_CHIPS, shares1,
                       [lax.empty((N_CHIPS,) + a.shape[1:], a.dtype) for a in shares1], "grads_to_sibling_small_start")
    da, gw_conv = _conv_bwd(pa, dya, w_conv_f + sib[4][0:1, 0:1])
    shares1, from_sibling = _split_wait(_sibling_copies, N_CHIPS, *sib[:4], da, "grads_to_sibling_small_wait")
    send1, recv1, srcs1, lands1, token1 = _chip_exchange_start(
        _pair_add(shares1, from_sibling, core, "grads_pair_add_small"), "grads_to_chips_start_small")
    dq, dbz, dkv, g_sink = _attn_bwd(pq, pkv, pbz, dyb, sink + token1[0, 0], tabs)
    dparts = (da, dq, dkv, dbz, dmq, dmz, dg)
    gw_int = _gw_in(dparts, h)
    send2, recv2, srcs2, lands2, token2 = exchange_start([gw_int.reshape(N_DEV, SHARD_IN, D_MODEL)], "w_in")
    grad_x, gg_pre = _dh_bwd(dparts, x2, dy, g_pre + token2[0:1, 0:1], w_int)
    (o_mkv, o_out, o_up, o_int), (l_mkv, l_out, l_up, l_int) = _chip_exchange_wait(
        send1 + send2, recv1 + recv2, srcs1 + srcs2, lands1 + lands2, grad_x, "grads_to_chips_wait")

    loss_row, small_g, sd, sm, sv = _small_step(
        (gg_pre, gw_conv, g_sink, gg_mem, gg_post, loss_p),
        [g_pre, w_conv[0], attn_sink, g_mem, g_post],
        [m_g_pre, m_w_conv[0], m_attn_sink, m_g_mem, m_g_post],
        [v_g_pre, v_w_conv[0], v_attn_sink, v_g_mem, v_g_post])
    loss = loss_row[0, 0]
    g_g_pre, g_conv, g_sink_tot, g_g_mem, g_g_post = small_g

    g_w_in, d_w_in, nm_w_in, nv_w_in = (t.T for t in _sum_adamw(
        o_int, l_int, chip, 0, w_in[0].T, m_w_in[0].T, v_w_in[0].T, "adamw_w_in", tiles=2))
    (g_mkv, d_mkv, nm_mkv, nv_mkv), (g_out, d_out, nm_out, nv_out), *up = _sum_adamw_group(
        [(o_mkv, l_mkv, 0, w_mem_kv[0], m_w_mem_kv[0], v_w_mem_kv[0]),
         (o_out, l_out, 0, w_out[0], m_w_out[0], v_w_out[0]),
         (o_up, l_up, 0, w_up_a[0], m_w_up_a[0], v_w_up_a[0]),
         (o_up, l_up, 1, w_up_b[0], m_w_up_b[0], v_w_up_b[0]),
         (o_up, l_up, 2, w_up_m[0], m_w_up_m[0], v_w_up_m[0])], chip, "adamw_mid_weights")

    def lead(a):
        return a[None]

    grads = [g_g_pre, lead(g_w_in), lead(g_conv), g_sink_tot, g_g_mem, lead(g_mkv), lead(up[0][0]),
             lead(up[1][0]), lead(up[2][0]), lead(g_out), g_g_post]

    def assemble(small, big_in, big_mkv, big_up, big_out):
        return [small[0], lead(big_in), lead(small[1]), small[2], small[3], lead(big_mkv), lead(big_up[0]),
                lead(big_up[1]), lead(big_up[2]), lead(big_out), small[4]]

    deltas = assemble(sd, d_w_in, d_mkv, [u[1] for u in up], d_out)
    new_m = assemble(sm, nm_w_in, nm_mkv, [u[2] for u in up], nm_out)
    new_v = assemble(sv, nv_w_in, nv_mkv, [u[3] for u in up], nv_out)
    return (loss, grad_x[None], *grads, *deltas, *new_m, *new_v)
```

```python
import functools

import jax
import jax.numpy as jnp
from jax import lax
from jax.experimental import pallas as pl
from jax.experimental.pallas import tpu as pltpu

F32 = jnp.float32
BF16 = jnp.bfloat16
MESH = pl.DeviceIdType.MESH

N_DEV = 8
D_MODEL = 1024
EPS = 1e-6
ROPE_THETA = 500000.0
ROT_DIM = 16
HEAD_DIM = 64
ATTN_BLOCK = 128
MEM_HEADS = 4
MEM_HEAD_DIM = 128
ATTN_SCALE = HEAD_DIM ** -0.5
MEM_SCALE = MEM_HEAD_DIM ** -0.5

ADAM_LR = 0.001
ADAM_B1 = 0.9
ADAM_B2 = 0.999
ADAM_EPS = 1e-08
ADAM_WD = 0.01
ADAM_STEP = 10

SEG_A = (0, 2048)
SEG_BQ = (2048, 512)
SEG_BKV = (2560, 256)
SEG_BZ = (2816, 512)
SEG_MQ = (3328, 512)
SEG_MZ = (3840, 512)
SEG_G = (4352, 3072)
SEGS = (SEG_A, SEG_BQ, SEG_BKV, SEG_BZ, SEG_MQ, SEG_MZ, SEG_G)
IN_WIDTH = 7424
SHARD_IN = IN_WIDTH // N_DEV

V7X_VMEM_BYTES = 64 * 1024 * 1024
CALL_VMEM_MB = 60
ANY = pl.BlockSpec(memory_space=pl.ANY)


def _params():
    assert CALL_VMEM_MB * 1024 * 1024 < V7X_VMEM_BYTES
    return pltpu.CompilerParams(dimension_semantics=("arbitrary",), vmem_limit_bytes=CALL_VMEM_MB * 1024 * 1024)


def _full(shape):
    zeros = (0,) * len(shape)
    return pl.BlockSpec(shape, lambda i: zeros)


def _rows(tm, width):
    return pl.BlockSpec((tm, width), lambda i: (i, 0))


def _dot(a, b):
    return jnp.dot(a, b, preferred_element_type=F32)


def _dot_nt(a, b):
    return lax.dot_general(a, b, (((1,), (1,)), ((), ())), preferred_element_type=F32)


def _dot_tn(a, b):
    return lax.dot_general(a, b, (((0,), (0,)), ((), ())), preferred_element_type=F32)


def _sigmoid(z):
    return 1.0 / (1.0 + jnp.exp(-z))


def _rope(t, cs, s1, s2):
    return t * cs + pltpu.roll(t, 120, 1) * s1 + pltpu.roll(t, 8, 1) * s2


def _rope_t(d, cs, s1, s2):
    return d * cs + pltpu.roll(d * s1, 8, 1) + pltpu.roll(d * s2, 120, 1)


def _gather_side(s, w_mkv, w_out, w_ups):
    half = ROT_DIM // 2
    inv_freq = jnp.power(jnp.float32(ROPE_THETA), -jnp.arange(half, dtype=F32) * (2.0 / ROT_DIM))
    freq_row = jnp.tile(jnp.concatenate([inv_freq, inv_freq, jnp.zeros((HEAD_DIM - ROT_DIM,), F32)]), 2)[None, :]

    def fn(in_refs, out_refs):
        f_ref, mkv_ref, out_ref, *up_refs = in_refs
        t_ref, mkv_bf, out_bf, up_bf = out_refs
        mkv_bf[...] = mkv_ref[...].astype(BF16)
        out_bf[...] = out_ref[...].astype(BF16)
        for k, up_ref in enumerate(up_refs):
            up_bf[512 * k:512 * k + 512, :] = up_ref[...].astype(BF16)
        pos = lax.broadcasted_iota(jnp.int32, (s, 128), 0).astype(F32)
        d = lax.broadcasted_iota(jnp.int32, (s, 128), 1) & (HEAD_DIM - 1)
        ang = pos * f_ref[...]
        cos, sin = jnp.cos(ang), jnp.sin(ang)
        lo, hi = d < half, (d >= half) & (d < ROT_DIM)
        t_ref[0] = jnp.where(lo | hi, cos, 1.0)
        t_ref[1] = jnp.where(lo, -sin, 0.0)
        t_ref[2] = jnp.where(hi, sin, 0.0)

    return ([freq_row, w_mkv, w_out, *w_ups],
            [jax.ShapeDtypeStruct((3, s, 128), F32), jax.ShapeDtypeStruct(w_mkv.shape, BF16),
             jax.ShapeDtypeStruct(w_out.shape, BF16), jax.ShapeDtypeStruct((1536, 128), BF16)], fn)


def _load_once(pairs, sems):
    @pl.when(pl.program_id(0) == 0)
    def _():
        cps = [pltpu.make_async_copy(src, dst, sems.at[k]) for k, (src, dst) in enumerate(pairs)]
        for cp in cps:
            cp.start()
        for cp in cps:
            cp.wait()


def _my_place():
    x, y, c = lax.axis_index("x"), lax.axis_index("y"), lax.axis_index("c")
    return x, y, c


def _all_gather(arrs, name, splits=None, side=None):
    n = len(arrs)
    if splits is None:
        splits = [[(0, a.shape[0])] for a in arrs]
    pieces = [(a, r0, rn) for a in range(n) for r0, rn in splits[a]]
    n_p = len(pieces)
    side_in, side_out, side_fn = side if side is not None else ((), (), None)
    m, q = len(side_in), len(side_out)

    def body(*refs):
        ins, outs = refs[:n], refs[n + m:2 * n + m]
        send_sems, recv_sems, local_sems = refs[2 * n + m + q:]
        x, y, c = _my_place()
        me, sibling = (x, y, c), (x, y, 1 - c)

        def route(core):
            first = (jnp.bitwise_xor(x, 1 - core), jnp.bitwise_xor(y, core), core)
            second = (jnp.bitwise_xor(x, core), jnp.bitwise_xor(y, 1 - core), core)
            return first, second, (1 - x, 1 - y, core)

        def idx(px, py, pc):
            return 4 * px + 2 * py + pc

        def copy(p, k, block, to, own=False):
            a, r0, rn = pieces[p]
            dst = outs[a].at[idx(*block), pl.ds(r0, rn)]
            return pltpu.make_async_remote_copy(
                src_ref=ins[a].at[pl.ds(r0, rn)] if own else dst, dst_ref=dst,
                send_sem=send_sems.at[p * 7 + k], recv_sem=recv_sems.at[p * 7 + k],
                device_id=to, device_id_type=MESH)

        nbr1, nbr2, diag = route(c)
        mine = [pltpu.make_async_copy(ins[a], outs[a].at[idx(*me)], local_sems.at[a]) for a in range(n)]
        for cp in mine:
            cp.start()
        sent = []
        for p in range(n_p):
            for k, to in enumerate((sibling, nbr1, nbr2)):
                sent.append(copy(p, k, me, to, own=True))
        for cp in sent:
            cp.start()
        if side_fn is not None:
            side_fn(refs[n:n + m], refs[2 * n + m:2 * n + m + q])
        for k_in, block, onward in ((1, nbr1, ((3, nbr2), (4, sibling))), (2, nbr2, ((5, sibling),)),
                                    (3, diag, ((6, sibling),))):
            for p in range(n_p):
                copy(p, k_in, block, me).wait_recv()
                for k_out, to in onward:
                    cp = copy(p, k_out, block, to)
                    cp.start()
                    sent.append(cp)
        s1, s2, sd = route(1 - c)
        for k_in, block in ((0, sibling), (4, s1), (5, s2), (6, sd)):
            for p in range(n_p):
                copy(p, k_in, block, me).wait_recv()
        for cp in sent:
            cp.wait_send()
        for cp in mine:
            cp.wait()

    return pl.pallas_call(
        body, name=name,
        out_shape=[jax.ShapeDtypeStruct((N_DEV,) + a.shape, a.dtype) for a in arrs] + list(side_out),
        in_specs=[ANY] * n + [pl.BlockSpec(memory_space=pltpu.VMEM)] * m,
        out_specs=[ANY] * n + [pl.BlockSpec(memory_space=pltpu.VMEM)] * q,
        scratch_shapes=[pltpu.SemaphoreType.DMA((7 * n_p,)), pltpu.SemaphoreType.DMA((7 * n_p,)),
                        pltpu.SemaphoreType.DMA((n,))],
        compiler_params=pltpu.CompilerParams(vmem_limit_bytes=CALL_VMEM_MB * 1024 * 1024),
    )(*arrs, *side_in)


N_CHIPS = 4


def _sibling_exchange(arrs, name):
    n = len(arrs)

    def body(*refs):
        ins, outs = refs[:n], refs[n:2 * n]
        send_sems, recv_sems = refs[2 * n:]
        x, y, c = _my_place()
        sibling = (x, y, 1 - c)

        def copy(a, j):
            return pltpu.make_async_remote_copy(
                src_ref=ins[a].at[2 * j + (1 - c)], dst_ref=outs[a].at[j],
                send_sem=send_sems.at[a * N_CHIPS + j], recv_sem=recv_sems.at[a * N_CHIPS + j],
                device_id=sibling, device_id_type=MESH)

        cps = [copy(a, j) for j in range(N_CHIPS) for a in range(n)]
        for cp in cps:
            cp.start()
        for cp in cps:
            cp.wait_recv()
        for cp in cps:
            cp.wait_send()

    return pl.pallas_call(
        body, name=name,
        out_shape=[jax.ShapeDtypeStruct((N_CHIPS,) + a.shape[1:], a.dtype) for a in arrs],
        in_specs=[ANY] * n, out_specs=[ANY] * n,
        scratch_shapes=[pltpu.SemaphoreType.DMA((N_CHIPS * n,)), pltpu.SemaphoreType.DMA((N_CHIPS * n,))],
        compiler_params=pltpu.CompilerParams(vmem_limit_bytes=CALL_VMEM_MB * 1024 * 1024),
    )(*arrs)


def _sibling_copies(srcs, lands, send_sems, recv_sems):
    x, y, c = _my_place()
    cps = []
    for j in range(N_CHIPS):
        for a in range(len(srcs)):
            k = a * N_CHIPS + j
            cps.append(pltpu.make_async_remote_copy(
                src_ref=srcs[a].at[2 * j + (1 - c)], dst_ref=lands[a].at[j], send_sem=send_sems[k],
                recv_sem=recv_sems[k], device_id=(x, y, 1 - c), device_id_type=MESH))
    return cps


def _pair_add(mine, recv, core, name):
    n = len(mine)

    def body(c_ref, *refs):
        for a in range(n):
            refs[2 * n + a][...] = (refs[a][...].astype(F32) + refs[n + a][...].astype(F32)).astype(BF16)

    def blk(a):
        return (None,) + a.shape[1:]

    grid_spec = pltpu.PrefetchScalarGridSpec(
        num_scalar_prefetch=1, grid=(N_CHIPS,),
        in_specs=[pl.BlockSpec(blk(a), lambda j, c_ref: (2 * j + c_ref[0], 0, 0)) for a in mine]
        + [pl.BlockSpec(blk(a), lambda j, c_ref: (j, 0, 0)) for a in recv],
        out_specs=[pl.BlockSpec(blk(a), lambda j, c_ref: (j, 0, 0)) for a in recv])
    return pl.pallas_call(
        body, name=name, grid_spec=grid_spec,
        out_shape=[jax.ShapeDtypeStruct(a.shape, BF16) for a in recv],
        compiler_params=_params(),
    )(core, *mine, *recv)


HBM = pl.BlockSpec(memory_space=pltpu.HBM)
SEM = pl.BlockSpec(memory_space=pltpu.SEMAPHORE)
N_PEER_CHIPS = 3


def _chip_copies(srcs, lands, send_sems, recv_sems):
    x, y, c = _my_place()
    my_chip = 2 * x + y
    peers = [(x, 1 - y), (1 - x, y), (1 - x, 1 - y)]
    cps = []
    for k, (px, py) in enumerate(peers):
        for a in range(len(srcs)):
            j = a * N_PEER_CHIPS + k
            cps.append(pltpu.make_async_remote_copy(
                src_ref=srcs[a].at[2 * px + py], dst_ref=lands[a].at[my_chip],
                send_sem=send_sems[j], recv_sem=recv_sems[j],
                device_id=(px, py, c), device_id_type=MESH))
    return cps


N_PEERS = N_DEV - 1


def _gather_copies(srcs, lands, send_sems, recv_sems):
    x, y, c = _my_place()
    me_idx = 4 * x + 2 * y + c
    flips = [(0, 0, 1), (0, 1, 0), (1, 0, 0), (0, 1, 1), (1, 0, 1), (1, 1, 0), (1, 1, 1)]
    cps = []
    for k, (fx, fy, fc) in enumerate(flips):
        peer = ((1 - x) if fx else x, (1 - y) if fy else y, (1 - c) if fc else c)
        for a in range(len(srcs)):
            j = a * N_PEERS + k
            cps.append(pltpu.make_async_remote_copy(
                src_ref=srcs[a], dst_ref=lands[a].at[me_idx], send_sem=send_sems[j], recv_sem=recv_sems[j],
                device_id=peer, device_id_type=MESH))
    return cps


def _split_start(copies, per_array, arrs, lands, name):
    arrs, lands = list(arrs), list(lands)
    n = len(arrs)
    k = n * per_array

    def body(*refs):
        srcs, land_refs = refs[:n], refs[n:2 * n]
        send_sems, recv_sems = refs[2 * n:2 * n + k], refs[2 * n + k:2 * n + 2 * k]
        token = refs[-1]
        for cp in copies(srcs, land_refs, send_sems, recv_sems):
            cp.start()
        token[...] = jnp.zeros_like(token)

    hbm_arrs = [pltpu.with_memory_space_constraint(a, pltpu.HBM) for a in arrs]
    lands = [pltpu.with_memory_space_constraint(a, pltpu.HBM) for a in lands]
    res = pl.pallas_call(
        body, name=name,
        out_shape=[pltpu.SemaphoreType.DMA(())] * (2 * k) + [pltpu.HBM(a.shape, a.dtype) for a in arrs + lands]
        + [jax.ShapeDtypeStruct((8, 128), F32)],
        in_specs=[HBM] * (2 * n),
        out_specs=[SEM] * (2 * k) + [HBM] * (2 * n) + [pl.BlockSpec(memory_space=pltpu.VMEM)],
        input_output_aliases={a: 2 * k + a for a in range(2 * n)},
        compiler_params=pltpu.CompilerParams(has_side_effects=pltpu.SideEffectType.DATAFLOW_SIDE_EFFECTING,
                                             vmem_limit_bytes=CALL_VMEM_MB * 1024 * 1024),
    )(*hbm_arrs, *lands)
    return res[:k], res[k:2 * k], res[2 * k:2 * k + n], res[2 * k + n:2 * k + 2 * n], res[-1]


def _split_wait(copies, per_array, send_sems, recv_sems, srcs, lands, after, name):
    n = len(srcs)
    k = n * per_array

    def body(*refs):
        src_refs, land_refs = refs[:n], refs[n:2 * n]
        s_sems, r_sems = refs[2 * n:2 * n + k], refs[2 * n + k:2 * n + 2 * k]
        for cp in copies(src_refs, land_refs, s_sems, r_sems):
            cp.wait_send()
            cp.wait_recv()

    res = pl.pallas_call(
        body, name=name,
        out_shape=[pltpu.HBM(a.shape, a.dtype) for a in list(srcs) + list(lands)],
        in_specs=[HBM] * (2 * n) + [SEM] * (2 * k) + [ANY],
        out_specs=[HBM] * (2 * n),
        input_output_aliases={a: a for a in range(2 * n)},
        compiler_params=pltpu.CompilerParams(has_side_effects=pltpu.SideEffectType.DATAFLOW_SIDE_EFFECTING,
                                             vmem_limit_bytes=CALL_VMEM_MB * 1024 * 1024),
    )(*srcs, *lands, *send_sems, *recv_sems, after)
    return res[:n], res[n:]


def _chip_exchange_start(arrs, name):
    return _split_start(_chip_copies, N_PEER_CHIPS, arrs, [lax.empty(a.shape, a.dtype) for a in arrs], name)


def _chip_exchange_wait(send_sems, recv_sems, srcs, lands, after, name):
    return _split_wait(_chip_copies, N_PEER_CHIPS, send_sems, recv_sems, srcs, lands, after, name)


def _gather_start(arrs, me_idx, name):
    lands = [lax.dynamic_update_slice(lax.empty((N_DEV,) + a.shape, a.dtype), a[None], (me_idx, 0, 0)) for a in arrs]
    return _split_start(_gather_copies, N_PEERS, arrs, lands, name)


def _gather_wait(send_sems, recv_sems, srcs, lands, after, name):
    return _split_wait(_gather_copies, N_PEERS, send_sems, recv_sems, srcs, lands, after, name)[1]


def _proj_fwd(x, g_pre, w_int, tabs):
    s = x.shape[0]
    tm = min(512, s)

    def body(x_ref, g_ref, t_ref, w_hbm,
             h_ref, pa_ref, pq_ref, pkv_ref, pbz_ref, pmq_ref, pmz_ref, pg_ref, w_vm, sems):
        _load_once([(w_hbm, w_vm)], sems)
        xf = x_ref[...]
        r = lax.rsqrt(jnp.mean(xf * xf, axis=-1, keepdims=True) + EPS)
        h = ((xf * r) * g_ref[...]).astype(BF16)
        h_ref[...] = h
        cs, s1, s2 = t_ref[0], t_ref[1], t_ref[2]

        def mm(seg, c0, width):
            return _dot_nt(h, w_vm[seg[0] + c0:seg[0] + c0 + width, :])

        for c0 in range(0, SEG_A[1], 512):
            pa_ref[:, c0:c0 + 512] = mm(SEG_A, c0, 512).astype(BF16)
        q = mm(SEG_BQ, 0, 512)
        for b in range(4):
            pq_ref[:, 128 * b:128 * b + 128] = _rope(q[:, 128 * b:128 * b + 128], cs, s1, s2).astype(BF16)
        kv = mm(SEG_BKV, 0, 256)
        pkv_ref[:, 0:128] = _rope(kv[:, 0:128], cs, s1, s2).astype(BF16)
        pkv_ref[:, 128:256] = kv[:, 128:256].astype(BF16)
        pbz_ref[...] = mm(SEG_BZ, 0, 512).astype(BF16)
        pmq_ref[...] = mm(SEG_MQ, 0, 512).astype(BF16)
        pmz_ref[...] = mm(SEG_MZ, 0, 512).astype(BF16)
        for c0 in range(0, SEG_G[1], 512):
            pg_ref[:, c0:c0 + 512] = mm(SEG_G, c0, 512).astype(BF16)

    widths = (D_MODEL, 2048, 512, 256, 512, 512, 512, 3072)
    return pl.pallas_call(
        body, name="proj_fwd", grid=(s // tm,),
        out_shape=[jax.ShapeDtypeStruct((s, w), BF16) for w in widths],
        in_specs=[_rows(tm, D_MODEL), _full((1, D_MODEL)), pl.BlockSpec((3, tm, 128), lambda i: (0, i, 0)), ANY],
        out_specs=[_rows(tm, w) for w in widths],
        scratch_shapes=[pltpu.VMEM((IN_WIDTH, D_MODEL), BF16), pltpu.SemaphoreType.DMA((1,))],
        compiler_params=_params(),
    )(x, g_pre, tabs, w_int)


def _mem_kv_fwd(mem, g_mem, w_mkv):
    m = mem.shape[0]

    def body(mem_ref, g_ref, w_ref, mn_ref, mkv_ref):
        xf = mem_ref[...]
        r = lax.rsqrt(jnp.mean(xf * xf, axis=-1, keepdims=True) + EPS)
        mn = ((xf * r) * g_ref[...]).astype(BF16)
        mn_ref[...] = mn
        mkv_ref[...] = _dot(mn, w_ref[...]).astype(BF16)

    return pl.pallas_call(
        body, name="mem_kv_fwd", grid=(1,),
        out_shape=[jax.ShapeDtypeStruct((m, D_MODEL), BF16)] * 2,
        in_specs=[_full((m, D_MODEL)), _full((1, D_MODEL)), _full((D_MODEL, D_MODEL))],
        out_specs=[_full((m, D_MODEL))] * 2,
        compiler_params=_params(),
    )(mem, g_mem, w_mkv)


def _halo_specs(s, tm, rows, width):
    nblk = s // rows
    prev = pl.BlockSpec((rows, width), lambda i: (jnp.maximum(i * (tm // rows) - 1, 0), 0))
    nxt = pl.BlockSpec((rows, width), lambda i: (jnp.minimum((i + 1) * (tm // rows), nblk - 1), 0))
    return prev, nxt


def _conv_common(pa, cu_prev, cu_next, w, tm):
    b, c, u, z = (pa[:, 512 * k:512 * k + 512] for k in range(4))
    cu = c * u
    row = lax.broadcasted_iota(jnp.int32, (tm, 512), 0)
    cu_m1 = jnp.where(row == 0, cu_prev, pltpu.roll(cu, 1, 0))
    cu_p1 = jnp.where(row == tm - 1, cu_next, pltpu.roll(cu, tm - 1, 0))
    y = cu_m1 * w[0:1] + cu * w[1:2] + cu_p1 * w[2:3]
    sig = _sigmoid(z)
    return b, c, u, z, cu, cu_m1, cu_p1, y, sig, row


def _conv_fwd(pa, w_conv):
    s = pa.shape[0]
    tm = min(512, s)
    nt = s // tm

    def body(pa_ref, pp_ref, pn_ref, w_ref, ya_ref):
        i = pl.program_id(0)
        prev_row = pp_ref[...].astype(F32)[15:16, :]
        next_row = pn_ref[...].astype(F32)[0:1, :]
        b, _, _, z, _, _, _, y, sig, _ = _conv_common(
            pa_ref[...].astype(F32),
            jnp.where(i == 0, 0.0, prev_row[:, 512:1024] * prev_row[:, 1024:1536]),
            jnp.where(i == nt - 1, 0.0, next_row[:, 512:1024] * next_row[:, 1024:1536]), w_ref[...], tm)
        ya_ref[...] = (b * y * (z * sig)).astype(BF16)

    prev, nxt = _halo_specs(s, tm, 16, 2048)
    return pl.pallas_call(
        body, name="conv_fwd", grid=(nt,),
        out_shape=jax.ShapeDtypeStruct((s, 512), BF16),
        in_specs=[_rows(tm, 2048), prev, nxt, _full((3, 512))],
        out_specs=_rows(tm, 512),
        compiler_params=_params(),
    )(pa, pa, pa, w_conv)


def _heads_to_lanes(a, g, row):
    low = row < HEAD_DIM
    parts = []
    for b in (2 * g, 2 * g + 1):
        t = jnp.transpose(a[:, 128 * b:128 * b + 128])
        swapped = pltpu.roll(t, HEAD_DIM, 0)
        if g == 0:
            parts += [jnp.where(low, t, 0.0), jnp.where(low, swapped, 0.0)]
        else:
            parts += [jnp.where(low, 0.0, swapped), jnp.where(low, 0.0, t)]
    return jnp.concatenate(parts, axis=1)


def _lanes_to_heads(t0, t1, row):
    low = row < HEAD_DIM
    blocks = []
    for b in range(4):
        g = b // 2
        tg = (t0, t1)[g]
        je = 2 * (b - 2 * g)
        even, odd = tg[:, 128 * je:128 * je + 128], tg[:, 128 * je + 128:128 * je + 256]
        if g == 0:
            t = jnp.where(low, even, pltpu.roll(odd, HEAD_DIM, 0))
        else:
            t = jnp.where(low, pltpu.roll(even, HEAD_DIM, 0), odd)
        blocks.append(jnp.transpose(t))
    return jnp.concatenate(blocks, axis=1)


WINDOW_KEYS = 3 * ATTN_BLOCK
STACKED = 4 * ATTN_BLOCK
KEY_CHUNK = 32
MAX_BLOCKS_IN_STEP = 8


def _fill_band_bias(bias, nb):
    assert nb >= 2
    c = lax.broadcasted_iota(jnp.int32, (WINDOW_KEYS, STACKED), 0)
    r = lax.broadcasted_iota(jnp.int32, (WINDOW_KEYS, STACKED), 1) & (ATTN_BLOCK - 1)
    band = (c >= r) & (c <= r + 2 * ATTN_BLOCK)
    for v, ok in enumerate((band, band & (c >= ATTN_BLOCK), band & (c < 2 * ATTN_BLOCK))):
        bias[v] = jnp.where(ok, 0.0, -jnp.inf)


def _bias_variant(n, nb):
    return jnp.where(n == 0, 1, jnp.where(n == nb - 1, 2, 0))


def _sink_row(sink_ref, g):
    return jnp.concatenate([jnp.full((1, ATTN_BLOCK), sink_ref[4 * g + j], F32) for j in range(4)], axis=1)


def _softmax_keys_major(sc, bias, variant, sink, e_scr):
    chunks = [pl.ds(k * KEY_CHUNK, KEY_CHUNK) for k in range(WINDOW_KEYS // KEY_CHUNK)]
    rows = [slice(k * KEY_CHUNK, (k + 1) * KEY_CHUNK) for k in range(WINDOW_KEYS // KEY_CHUNK)]
    m_run = jnp.full((KEY_CHUNK, STACKED), -jnp.inf, F32)
    for ck, rw in zip(chunks, rows):
        m_run = jnp.maximum(m_run, sc[rw] + bias[variant, ck, :])
    m = jnp.maximum(jnp.max(m_run, axis=0, keepdims=True), sink)
    l_run = jnp.zeros((KEY_CHUNK, STACKED), F32)
    for ck, rw in zip(chunks, rows):
        e = jnp.exp(sc[rw] + bias[variant, ck, :] - m)
        l_run += e
        e_scr[rw, :] = e.astype(BF16)
    es = jnp.exp(sink - m)
    inv = 1.0 / (jnp.sum(l_run, axis=0, keepdims=True) + es)
    return inv, es * inv


def _fill_padded(kv_ref, kpad, vpad, s):
    zero = jnp.zeros((ATTN_BLOCK, 128), BF16)
    kpad[0:ATTN_BLOCK, :] = zero
    vpad[0:ATTN_BLOCK, :] = zero
    kpad[ATTN_BLOCK + s:2 * ATTN_BLOCK + s, :] = zero
    vpad[ATTN_BLOCK + s:2 * ATTN_BLOCK + s, :] = zero
    kpad[ATTN_BLOCK:ATTN_BLOCK + s, :] = kv_ref[:, 0:128]
    vpad[ATTN_BLOCK:ATTN_BLOCK + s, :] = kv_ref[:, 128:256]


def _attn_fwd(pq, pkv, pbz, sink):
    s = pq.shape[0]
    nb = s // ATTN_BLOCK
    bps = min(MAX_BLOCKS_IN_STEP, nb)

    def body(sink_ref, q_ref, z_ref, kv_ref, yb_ref, kpad, vpad, bias, e_scr):
        i = pl.program_id(0)

        @pl.when(i == 0)
        def _():
            _fill_padded(kv_ref, kpad, vpad, s)
            _fill_band_bias(bias, nb)

        row = lax.broadcasted_iota(jnp.int32, (ATTN_BLOCK, 128), 0)
        for b in range(bps):
            n = i * bps + b
            rows = slice(b * ATTN_BLOCK, (b + 1) * ATTN_BLOCK)
            start = pl.multiple_of(n * ATTN_BLOCK, ATTN_BLOCK)
            kw, vw = kpad[pl.ds(start, WINDOW_KEYS), :], vpad[pl.ds(start, WINDOW_KEYS), :]
            qf = q_ref[rows, :].astype(F32)
            variant = _bias_variant(n, nb)
            outs = []
            for g in range(2):
                e_bg = e_scr.at[2 * b + g]
                qt = (_heads_to_lanes(qf, g, row) * ATTN_SCALE).astype(BF16)
                inv, _ = _softmax_keys_major(_dot(kw, qt), bias, variant, _sink_row(sink_ref, g), e_bg)
                outs.append(_dot_tn(vw, e_bg[...]) * inv)
            attn = _lanes_to_heads(outs[0], outs[1], row)
            z = z_ref[rows, :].astype(F32)
            yb_ref[rows, :] = (attn * (z * _sigmoid(z))).astype(BF16)

    tq = bps * ATTN_BLOCK
    return pl.pallas_call(
        body, name="attn_fwd", grid=(s // tq,),
        out_shape=jax.ShapeDtypeStruct((s, 512), BF16),
        in_specs=[pl.BlockSpec(memory_space=pltpu.SMEM), _rows(tq, 512), _rows(tq, 512), _full((s, 256))],
        out_specs=_rows(tq, 512),
        scratch_shapes=[pltpu.VMEM((s + 2 * ATTN_BLOCK, 128), BF16)] * 2
        + [pltpu.VMEM((3, WINDOW_KEYS, STACKED), F32),
           pltpu.VMEM((2 * bps, WINDOW_KEYS, STACKED), BF16)],
        compiler_params=_params(),
    )(sink, pq, pbz, pkv)


def _mem_softmax_t(q, mk):
    sc = _dot_nt(mk, q) * MEM_SCALE
    e = jnp.exp(sc - jnp.max(sc, axis=0, keepdims=True))
    return e * (1.0 / jnp.sum(e, axis=0, keepdims=True))


def _mem_attn_fwd(pmq, pmz, mkv):
    s = pmq.shape[0]
    m = mkv.shape[0]
    tm = min(512, s)

    def body(q_ref, z_ref, mk_ref, mv_ref, ym_ref):
        z = z_ref[...].astype(F32)
        sz = z * _sigmoid(z)
        for h in range(MEM_HEADS):
            cols = slice(128 * h, 128 * h + 128)
            pt = _mem_softmax_t(q_ref[:, cols], mk_ref[:, cols])
            o = _dot_tn(pt.astype(BF16), mv_ref[:, cols])
            ym_ref[:, cols] = (o * sz[:, cols]).astype(BF16)

    return pl.pallas_call(
        body, name="mem_attn_fwd", grid=(s // tm,),
        out_shape=jax.ShapeDtypeStruct((s, 512), BF16),
        in_specs=[_rows(tm, 512), _rows(tm, 512), pl.BlockSpec((m, 512), lambda i: (0, 0)),
                  pl.BlockSpec((m, 512), lambda i: (0, 1))],
        out_specs=_rows(tm, 512),
        compiler_params=_params(),
    )(pmq, pmz, mkv, mkv)


def _mid(ya, yb, ym, pg, x, target, g_post, w_up, w_out):
    s = x.shape[0]
    tm = min(256, s)
    nt = s // tm

    def body(ya_ref, yb_ref, ym_ref, pg_ref, x_ref, t_ref, gp_ref, wup_hbm, wout_hbm,
             dg_ref, dya_ref, dyb_ref, dym_ref, dy_ref, loss_ref, ggp_ref, mb_ref, dob_ref, du_ref,
             wup_vm, wout_vm, sems):
        i = pl.program_id(0)
        _load_once([(wup_hbm.at[d], wup_vm.at[:, pl.ds(128 * d, 128)]) for d in range(N_DEV)]
                   + [(wout_hbm, wout_vm)], sems)

        @pl.when(i == 0)
        def _():
            loss_ref[...] = jnp.zeros_like(loss_ref)
            ggp_ref[...] = jnp.zeros_like(ggp_ref)

        ys = (ya_ref[...], yb_ref[...], ym_ref[...])
        us = [_dot(ys[k], wup_vm[512 * k:512 * k + 512, :]) for k in range(3)]
        gates = [_sigmoid(pg_ref[:, 1024 * k:1024 * k + 1024].astype(F32)) for k in range(3)]
        merged = gates[0] * us[0] + gates[1] * us[1] + gates[2] * us[2]
        mb = merged.astype(BF16)
        mb_ref[...] = mb
        out = _dot(mb, wout_vm[...])
        r = lax.rsqrt(jnp.mean(out * out, axis=-1, keepdims=True) + EPS)
        on = out * r
        gp = gp_ref[...]
        err = (x_ref[...] + on * gp) - t_ref[...]
        loss_ref[...] += 0.5 * jnp.sum(err * err) * (1.0 / D_MODEL)
        dy = err * (1.0 / D_MODEL)
        dy_ref[...] = dy
        ggp_ref[...] += jnp.sum(dy * on, axis=0, keepdims=True)
        a = dy * gp
        d_out = r * (a - on * jnp.mean(a * on, axis=-1, keepdims=True))
        dob = d_out.astype(BF16)
        dob_ref[...] = dob
        d_merged = _dot_nt(dob, wout_vm[...])
        d_refs = (dya_ref, dyb_ref, dym_ref)
        for k in range(3):
            g = gates[k]
            du_f = d_merged * g
            dg_ref[:, 1024 * k:1024 * k + 1024] = (du_f * us[k] * (1.0 - g)).astype(BF16)
            du = du_f.astype(BF16)
            du_ref[k] = du
            d_refs[k][...] = _dot_nt(du, wup_vm[512 * k:512 * k + 512, :]).astype(BF16)

    return pl.pallas_call(
        body, name="mid", grid=(nt,),
        out_shape=[jax.ShapeDtypeStruct((s, 3072), BF16)] + [jax.ShapeDtypeStruct((s, 512), BF16)] * 3
        + [jax.ShapeDtypeStruct((s, D_MODEL), F32), jax.ShapeDtypeStruct((8, 128), F32),
           jax.ShapeDtypeStruct((1, D_MODEL), F32), jax.ShapeDtypeStruct((s, D_MODEL), BF16),
           jax.ShapeDtypeStruct((s, D_MODEL), BF16), jax.ShapeDtypeStruct((3, s, D_MODEL), BF16)],
        in_specs=[_rows(tm, 512)] * 3 + [_rows(tm, 3072), _rows(tm, D_MODEL), _rows(tm, D_MODEL),
                                         _full((1, D_MODEL)), ANY, ANY],
        out_specs=[_rows(tm, 3072)] + [_rows(tm, 512)] * 3
        + [_rows(tm, D_MODEL), _full((8, 128)), _full((1, D_MODEL)), _rows(tm, D_MODEL), _rows(tm, D_MODEL),
           pl.BlockSpec((3, tm, D_MODEL), lambda i: (0, i, 0))],
        scratch_shapes=[pltpu.VMEM((1536, D_MODEL), BF16), pltpu.VMEM((D_MODEL, D_MODEL), BF16),
                        pltpu.SemaphoreType.DMA((N_DEV + 1,))],
        compiler_params=_params(),
    )(ya, yb, ym, pg, x, target, g_post, w_up, w_out)


def _gw_mid(mb, dob, ys, du):
    s = mb.shape[0]
    tn = 256

    def out_body(mb_ref, dob_ref, o_ref):
        o_ref[...] = _dot_tn(mb_ref[...], dob_ref[...]).astype(BF16)

    gw_out = pl.pallas_call(
        out_body, name="gw_out", grid=(D_MODEL // tn,),
        out_shape=jax.ShapeDtypeStruct((D_MODEL, D_MODEL), BF16),
        in_specs=[pl.BlockSpec((s, tn), lambda j: (0, j)), _full((s, D_MODEL))],
        out_specs=pl.BlockSpec((tn, D_MODEL), lambda j: (j, 0)),
        compiler_params=_params(),
    )(mb, dob)

    per = 512 // tn

    def up_body(ya_ref, yb_ref, ym_ref, du_ref, o_ref):
        j = pl.program_id(0)
        for k, y_ref in enumerate((ya_ref, yb_ref, ym_ref)):
            @pl.when(j // per == k)
            def _(y_ref=y_ref):
                res = _dot_tn(y_ref[...], du_ref[...])
                for d in range(N_DEV):
                    o_ref[d] = res[:, 128 * d:128 * d + 128].astype(BF16)

    def y_spec(k):
        return pl.BlockSpec((s, tn), lambda j: (0, jnp.clip(j - per * k, 0, per - 1)))

    gw_up = pl.pallas_call(
        up_body, name="gw_up", grid=(3 * per,),
        out_shape=jax.ShapeDtypeStruct((N_DEV, 1536, 128), BF16),
        in_specs=[y_spec(0), y_spec(1), y_spec(2), pl.BlockSpec((None, s, D_MODEL), lambda j: (j // per, 0, 0))],
        out_specs=pl.BlockSpec((N_DEV, tn, 128), lambda j: (0, j, 0)),
        compiler_params=_params(),
    )(*ys, du)
    return gw_out, gw_up


def _conv_bwd(pa, dya, w_conv):
    s = pa.shape[0]
    tm = min(512, s)
    nt = s // tm

    def body(pa_ref, pp_ref, pn_ref, d_ref, dp_ref, dn_ref, w_ref, da_ref, gw_ref):
        i = pl.program_id(0)
        first, last = i == 0, i == nt - 1

        @pl.when(first)
        def _():
            gw_ref[...] = jnp.zeros_like(gw_ref)

        w = w_ref[...]
        prev_row = pp_ref[...].astype(F32)[15:16, :]
        next_row = pn_ref[...].astype(F32)[0:1, :]
        b, c, u, z, cu, cu_m1, cu_p1, y, sig, row = _conv_common(
            pa_ref[...].astype(F32),
            jnp.where(first, 0.0, prev_row[:, 512:1024] * prev_row[:, 1024:1536]),
            jnp.where(last, 0.0, next_row[:, 512:1024] * next_row[:, 1024:1536]), w, tm)
        sz = z * sig
        dya_t = d_ref[...].astype(F32)
        d_y = dya_t * b * sz

        def halo_dy(p_row, d_row):
            zz = p_row[:, 1536:2048]
            return d_row * p_row[:, 0:512] * (zz * _sigmoid(zz))

        dy_prev = jnp.where(first, 0.0, halo_dy(prev_row, dp_ref[...].astype(F32)[15:16, :]))
        dy_next = jnp.where(last, 0.0, halo_dy(next_row, dn_ref[...].astype(F32)[0:1, :]))
        dy_m1 = jnp.where(row == 0, dy_prev, pltpu.roll(d_y, 1, 0))
        dy_p1 = jnp.where(row == tm - 1, dy_next, pltpu.roll(d_y, tm - 1, 0))
        d_cu = dy_p1 * w[0:1] + d_y * w[1:2] + dy_m1 * w[2:3]
        da_ref[:, 0:512] = (dya_t * y * sz).astype(BF16)
        da_ref[:, 512:1024] = (d_cu * u).astype(BF16)
        da_ref[:, 1024:1536] = (d_cu * c).astype(BF16)
        da_ref[:, 1536:2048] = (dya_t * b * y * (sig + sz * (1.0 - sig))).astype(BF16)
        gw_ref[0:1, :] += jnp.sum(d_y * cu_m1, axis=0, keepdims=True)
        gw_ref[1:2, :] += jnp.sum(d_y * cu, axis=0, keepdims=True)
        gw_ref[2:3, :] += jnp.sum(d_y * cu_p1, axis=0, keepdims=True)

    prev, nxt = _halo_specs(s, tm, 16, 2048)
    dprev, dnxt = _halo_specs(s, tm, 16, 512)
    return pl.pallas_call(
        body, name="conv_bwd", grid=(nt,),
        out_shape=[jax.ShapeDtypeStruct((s, 2048), BF16), jax.ShapeDtypeStruct((8, 512), F32)],
        in_specs=[_rows(tm, 2048), prev, nxt, _rows(tm, 512), dprev, dnxt, _full((3, 512))],
        out_specs=[_rows(tm, 2048), _full((8, 512))],
        compiler_params=_params(),
    )(pa, pa, pa, dya, dya, dya, w_conv)


def _attn_bwd(pq, pkv, pbz, dyb, sink, tabs):
    s = pq.shape[0]
    nb = s // ATTN_BLOCK
    bps = min(MAX_BLOCKS_IN_STEP, nb)

    def body(sink_ref, q_ref, z_ref, d_ref, kv_ref, t_ref,
             dq_ref, dz_ref, dkv_ref, gs_ref, kpad, vpad, dk_acc, dv_acc, bias, e_scr, ds_scr):
        i = pl.program_id(0)

        @pl.when(i == 0)
        def _():
            _fill_padded(kv_ref, kpad, vpad, s)
            _fill_band_bias(bias, nb)
            dk_acc[...] = jnp.zeros_like(dk_acc)
            dv_acc[...] = jnp.zeros_like(dv_acc)
            gs_ref[...] = jnp.zeros_like(gs_ref)

        row = lax.broadcasted_iota(jnp.int32, (ATTN_BLOCK, 128), 0)
        for b in range(bps):
            n = i * bps + b
            rows = slice(b * ATTN_BLOCK, (b + 1) * ATTN_BLOCK)
            start = pl.multiple_of(n * ATTN_BLOCK, ATTN_BLOCK)
            kw, vw = kpad[pl.ds(start, WINDOW_KEYS), :], vpad[pl.ds(start, WINDOW_KEYS), :]
            qf = q_ref[rows, :].astype(F32)
            variant = _bias_variant(n, nb)
            z = z_ref[rows, :].astype(F32)
            sig = _sigmoid(z)
            dyb_t = d_ref[rows, :].astype(F32)
            d_attn = dyb_t * (z * sig)
            outs, dqs = [], []
            dk_w = jnp.zeros((WINDOW_KEYS, 128), F32)
            dv_w = jnp.zeros((WINDOW_KEYS, 128), F32)
            for g in range(2):
                e_bg, ds_bg = e_scr.at[2 * b + g], ds_scr.at[2 * b + g]
                qt = _heads_to_lanes(qf, g, row)
                inv, p_sink = _softmax_keys_major(
                    _dot(kw, (qt * ATTN_SCALE).astype(BF16)), bias, variant, _sink_row(sink_ref, g), e_bg)
                ot = _dot_tn(vw, e_bg[...]) * inv
                outs.append(ot)
                dot_ = _heads_to_lanes(d_attn, g, row)
                delta = jnp.sum(dot_ * ot, axis=0, keepdims=True)
                dpt = _dot(vw, dot_.astype(BF16))
                for k in range(WINDOW_KEYS // KEY_CHUNK):
                    rw = slice(k * KEY_CHUNK, (k + 1) * KEY_CHUNK)
                    ds_bg[rw, :] = (e_bg[rw, :].astype(F32) * (dpt[rw] - delta)).astype(BF16)
                sink_part = p_sink * delta
                for j in range(4):
                    h = 4 * g + j
                    gs_ref[h:h + 1, :] -= jnp.sum(sink_part[:, 128 * j:128 * j + 128])
                dqs.append(_dot_tn(kw, ds_bg[...]) * (inv * ATTN_SCALE))
                dk_w += _dot_nt(ds_bg[...], (qt * inv).astype(BF16)) * ATTN_SCALE
                dv_w += _dot_nt(e_bg[...], (dot_ * inv).astype(BF16))
            dk_acc[pl.ds(start, WINDOW_KEYS), :] += dk_w
            dv_acc[pl.ds(start, WINDOW_KEYS), :] += dv_w
            attn = _lanes_to_heads(outs[0], outs[1], row)
            dz_ref[rows, :] = (dyb_t * attn * (sig * (1.0 + z * (1.0 - sig)))).astype(BF16)
            dq = _lanes_to_heads(dqs[0], dqs[1], row)
            trows = pl.ds(start, ATTN_BLOCK)
            cs, s1, s2 = t_ref[0, trows, :], t_ref[1, trows, :], t_ref[2, trows, :]
            for blk in range(4):
                cols = slice(128 * blk, 128 * blk + 128)
                dq_ref[rows, cols] = _rope_t(dq[:, cols], cs, s1, s2).astype(BF16)

        @pl.when(i == nb // bps - 1)
        def _():
            dk = dk_acc[ATTN_BLOCK:ATTN_BLOCK + s, :]
            dkv_ref[:, 0:128] = _rope_t(dk, t_ref[0], t_ref[1], t_ref[2]).astype(BF16)
            dkv_ref[:, 128:256] = dv_acc[ATTN_BLOCK:ATTN_BLOCK + s, :].astype(BF16)

    tq = bps * ATTN_BLOCK
    tile = _rows(tq, 512)
    return pl.pallas_call(
        body, name="attn_bwd", grid=(s // tq,),
        out_shape=[jax.ShapeDtypeStruct((s, 512), BF16), jax.ShapeDtypeStruct((s, 512), BF16),
                   jax.ShapeDtypeStruct((s, 256), BF16), jax.ShapeDtypeStruct((8, 128), F32)],
        in_specs=[pl.BlockSpec(memory_space=pltpu.SMEM), tile, tile, tile, _full((s, 256)), _full((3, s, 128))],
        out_specs=[tile, tile, _full((s, 256)), _full((8, 128))],
        scratch_shapes=[pltpu.VMEM((s + 2 * ATTN_BLOCK, 128), BF16)] * 2
        + [pltpu.VMEM((s + 2 * ATTN_BLOCK, 128), F32)] * 2
        + [pltpu.VMEM((3, WINDOW_KEYS, STACKED), F32)]
        + [pltpu.VMEM((2 * bps, WINDOW_KEYS, STACKED), BF16)] * 2,
        compiler_params=_params(),
    )(sink, pq, pbz, dyb, pkv, tabs)


def _mem_attn_bwd(pmq, pmz, mkv, dym):
    s = pmq.shape[0]
    m = mkv.shape[0]
    tm = min(512, s)

    def body(q_ref, z_ref, d_ref, mk_ref, mv_ref, dq_ref, dz_ref, dmkv_ref):
        @pl.when(pl.program_id(0) == 0)
        def _():
            dmkv_ref[...] = jnp.zeros_like(dmkv_ref)

        z = z_ref[...].astype(F32)
        sig = _sigmoid(z)
        dym_t = d_ref[...].astype(F32)
        d_attn = dym_t * (z * sig)
        dsilu = sig * (1.0 + z * (1.0 - sig))
        for h in range(MEM_HEADS):
            cols = slice(128 * h, 128 * h + 128)
            q, mk, mv = q_ref[:, cols], mk_ref[:, cols], mv_ref[:, cols]
            pt = _mem_softmax_t(q, mk)
            pb = pt.astype(BF16)
            o = _dot_tn(pb, mv)
            dob = d_attn[:, cols].astype(BF16)
            dpt = _dot_nt(mv, dob)
            dst = (pt * (dpt - jnp.sum(pt * dpt, axis=0, keepdims=True))).astype(BF16)
            dq_ref[:, cols] = (_dot_tn(dst, mk) * MEM_SCALE).astype(BF16)
            dz_ref[:, cols] = (dym_t[:, cols] * o * dsilu[:, cols]).astype(BF16)
            dmkv_ref[:, cols] += _dot(dst, q) * MEM_SCALE
            dmkv_ref[:, 512 + 128 * h:512 + 128 * h + 128] += _dot(pb, dob)

    return pl.pallas_call(
        body, name="mem_attn_bwd", grid=(s // tm,),
        out_shape=[jax.ShapeDtypeStruct((s, 512), BF16), jax.ShapeDtypeStruct((s, 512), BF16),
                   jax.ShapeDtypeStruct((m, D_MODEL), F32)],
        in_specs=[_rows(tm, 512), _rows(tm, 512), _rows(tm, 512), pl.BlockSpec((m, 512), lambda i: (0, 0)),
                  pl.BlockSpec((m, 512), lambda i: (0, 1))],
        out_specs=[_rows(tm, 512), _rows(tm, 512), _full((m, D_MODEL))],
        compiler_params=_params(),
    )(pmq, pmz, dym, mkv, mkv)


def _mem_kv_bwd(mem, g_mem, mn, dmkv, w_mkv):
    m = mem.shape[0]

    def body(mem_ref, g_ref, mn_ref, d_ref, w_ref, gw_ref, gg_ref):
        db = d_ref[...].astype(BF16)
        gw_ref[...] = _dot_tn(mn_ref[...], db).astype(BF16)
        d_mn = _dot_nt(db, w_ref[...])
        xf = mem_ref[...]
        r = lax.rsqrt(jnp.mean(xf * xf, axis=-1, keepdims=True) + EPS)
        gg_ref[...] = jnp.sum(d_mn * (xf * r), axis=0, keepdims=True)

    return pl.pallas_call(
        body, name="mem_kv_bwd", grid=(1,),
        out_shape=[jax.ShapeDtypeStruct((D_MODEL, D_MODEL), BF16), jax.ShapeDtypeStruct((1, D_MODEL), F32)],
        in_specs=[_full((m, D_MODEL)), _full((1, D_MODEL)), _full((m, D_MODEL)), _full((m, D_MODEL)),
                  _full((D_MODEL, D_MODEL))],
        out_specs=[_full((D_MODEL, D_MODEL)), _full((1, D_MODEL))],
        compiler_params=_params(),
    )(mem, g_mem, mn, dmkv, w_mkv)


def _dh_bwd(dparts, x, dy, g_pre, w_int):
    s = x.shape[0]
    tm = min(256, s)

    def body(*refs):
        d_refs = refs[:7]
        x_ref, dy_ref, g_ref, w_hbm, gx_ref, gg_ref, w_vm, sems = refs[7:]
        _load_once([(w_hbm, w_vm)], sems)

        @pl.when(pl.program_id(0) == 0)
        def _():
            gg_ref[...] = jnp.zeros_like(gg_ref)

        d_h = jnp.zeros((tm, D_MODEL), F32)
        for d_ref, (r0, width) in zip(d_refs, SEGS):
            for c0 in range(0, width, 512):
                cw = min(512, width - c0)
                d_h += _dot(d_ref[:, c0:c0 + cw], w_vm[r0 + c0:r0 + c0 + cw, :])
        xf = x_ref[...]
        r = lax.rsqrt(jnp.mean(xf * xf, axis=-1, keepdims=True) + EPS)
        xn = xf * r
        a = d_h * g_ref[...]
        gx_ref[...] = r * (a - xn * jnp.mean(a * xn, axis=-1, keepdims=True)) + dy_ref[...]
        gg_ref[...] += jnp.sum(d_h * xn, axis=0, keepdims=True)

    return pl.pallas_call(
        body, name="dh_bwd", grid=(s // tm,),
        out_shape=[jax.ShapeDtypeStruct((s, D_MODEL), F32), jax.ShapeDtypeStruct((1, D_MODEL), F32)],
        in_specs=[_rows(tm, w) for _, w in SEGS] + [_rows(tm, D_MODEL), _rows(tm, D_MODEL), _full((1, D_MODEL)), ANY],
        out_specs=[_rows(tm, D_MODEL), _full((1, D_MODEL))],
        scratch_shapes=[pltpu.VMEM((IN_WIDTH, D_MODEL), BF16), pltpu.SemaphoreType.DMA((1,))],
        compiler_params=_params(),
    )(*dparts, x, dy, g_pre, w_int)


def _gw_in(dparts, h):
    s = h.shape[0]
    tn = 256
    starts, counts = [], []
    for r0, width in SEGS:
        starts.append(r0 // tn)
        counts.append(width // tn)

    def body(*refs):
        d_refs = refs[:7]
        h_hbm, o_ref, h_vm, sems = refs[7:]
        _load_once([(h_hbm, h_vm)], sems)
        j = pl.program_id(0)
        for d_ref, st, cnt in zip(d_refs, starts, counts):
            @pl.when((j >= st) & (j < st + cnt))
            def _(d_ref=d_ref):
                o_ref[...] = _dot_tn(d_ref[...], h_vm[...]).astype(BF16)

    def seg_spec(st, cnt):
        return pl.BlockSpec((s, tn), lambda j: (0, jnp.clip(j - st, 0, cnt - 1)))

    return pl.pallas_call(
        body, name="gw_in", grid=(IN_WIDTH // tn,),
        out_shape=jax.ShapeDtypeStruct((IN_WIDTH, D_MODEL), BF16),
        in_specs=[seg_spec(st, cnt) for st, cnt in zip(starts, counts)] + [ANY],
        out_specs=pl.BlockSpec((tn, D_MODEL), lambda j: (j, 0)),
        scratch_shapes=[pltpu.VMEM((s, D_MODEL), BF16), pltpu.SemaphoreType.DMA((1,))],
        compiler_params=_params(),
    )(*dparts, h)


def _adamw_math(w, g, m, v):
    m2 = ADAM_B1 * m + (1.0 - ADAM_B1) * g
    v2 = ADAM_B2 * v + (1.0 - ADAM_B2) * (g * g)
    m_hat = m2 / (1.0 - ADAM_B1 ** ADAM_STEP)
    v_hat = v2 / (1.0 - ADAM_B2 ** ADAM_STEP)
    delta = -ADAM_LR * (m_hat / (jnp.sqrt(v_hat) + ADAM_EPS) + ADAM_WD * w)
    return delta, m2, v2


def _sum_adamw(own, land, chip, block, w, m, v, name, tiles=1):
    r, c = w.shape
    rt = r // tiles

    def body(c_ref, own_ref, l1_ref, l2_ref, l3_ref, w_ref, m_ref, v_ref, g_ref, d_ref, m2_ref, v2_ref):
        g = own_ref[...].astype(F32)
        for l_ref in (l1_ref, l2_ref, l3_ref):
            g += l_ref[...].astype(F32)
        g_ref[...] = g
        d_ref[...], m2_ref[...], v2_ref[...] = _adamw_math(w_ref[...], g, m_ref[...], v_ref[...])

    def share(k):
        return pl.BlockSpec((None, rt, c), lambda i, c_ref: (jnp.bitwise_xor(c_ref[0], k), block * tiles + i, 0))

    spec = pl.BlockSpec((rt, c), lambda i, c_ref: (i, 0))
    grid_spec = pltpu.PrefetchScalarGridSpec(
        num_scalar_prefetch=1, grid=(tiles,),
        in_specs=[share(0), share(1), share(2), share(3)] + [spec] * 3, out_specs=[spec] * 4)
    return pl.pallas_call(
        body, name=name, grid_spec=grid_spec,
        out_shape=[jax.ShapeDtypeStruct((r, c), F32)] * 4,
        compiler_params=_params(),
    )(chip, own, land, land, land, w, m, v)


def _sum_adamw_group(items, chip, name):
    k = len(items)

    def body(c_ref, *refs):
        shares, wmv, outs = refs[:4 * k], refs[4 * k:7 * k], refs[7 * k:]
        for j in range(k):
            g = shares[4 * j][...].astype(F32)
            for l_ref in shares[4 * j + 1:4 * j + 4]:
                g += l_ref[...].astype(F32)
            outs[4 * j][...] = g
            outs[4 * j + 1][...], outs[4 * j + 2][...], outs[4 * j + 3][...] = _adamw_math(
                wmv[3 * j][...], g, wmv[3 * j + 1][...], wmv[3 * j + 2][...])

    def share(shape, block, q):
        return pl.BlockSpec((None,) + shape, lambda i, c_ref: (jnp.bitwise_xor(c_ref[0], q), block, 0))

    in_specs, args = [], []
    for own, land, block, w, m, v in items:
        in_specs += [share(w.shape, block, q) for q in range(4)]
        args += [own, land, land, land]
    for own, land, block, w, m, v in items:
        in_specs += [pl.BlockSpec(w.shape, lambda i, c_ref: (0, 0))] * 3
        args += [w, m, v]
    out_specs = [pl.BlockSpec(w.shape, lambda i, c_ref: (0, 0)) for _, _, _, w, _, _ in items for _ in range(4)]
    res = pl.pallas_call(
        body, name=name,
        grid_spec=pltpu.PrefetchScalarGridSpec(num_scalar_prefetch=1, grid=(1,), in_specs=in_specs,
                                               out_specs=out_specs),
        out_shape=[jax.ShapeDtypeStruct(w.shape, F32) for _, _, _, w, _, _ in items for _ in range(4)],
        compiler_params=_params(),
    )(chip, *args)
    return [res[4 * j:4 * j + 4] for j in range(k)]


def _small_step(parts, ws, ms, vs):
    def exchange(gpre_ref, gconv_ref, gsink_ref, gmem_ref, gpost_ref, loss_ref, tot_ref,
                 pack, gathered, send_sems, recv_sems):
        x, y, c = _my_place()
        me_idx = 4 * x + 2 * y + c

        lane = lax.broadcasted_iota(jnp.int32, (1, 128), 1)
        sink_row = jnp.zeros((1, 128), F32)
        for h in range(8):
            sink_row = jnp.where(lane == h, gsink_ref[h:h + 1, :], sink_row)
        pack[...] = jnp.zeros_like(pack)
        pack[0:1, :] = gpre_ref[...]
        pack[1:2, :] = gmem_ref[...]
        pack[2:3, :] = gpost_ref[...]
        pack[3:6, 0:512] = gconv_ref[0:3, :]
        pack[6:7, 0:128] = sink_row
        pack[7:8, 0:128] = loss_ref[0:1, :]

        flips = [(0, 0, 1), (0, 1, 0), (1, 0, 0), (0, 1, 1), (1, 0, 1), (1, 1, 0), (1, 1, 1)]
        cps = []
        for k, (fx, fy, fc) in enumerate(flips):
            peer = ((1 - x) if fx else x, (1 - y) if fy else y, (1 - c) if fc else c)
            cps.append(pltpu.make_async_remote_copy(
                src_ref=pack, dst_ref=gathered.at[me_idx], send_sem=send_sems.at[k], recv_sem=recv_sems.at[k],
                device_id=peer, device_id_type=MESH))
        for cp in cps:
            cp.start()
        gathered[me_idx] = pack[...]
        for cp in cps:
            cp.wait_recv()
        for cp in cps:
            cp.wait_send()
        tot = gathered[0]
        for d in range(1, N_DEV):
            tot = tot + gathered[d]
        tot_ref[...] = tot

    tot = pl.pallas_call(
        exchange, name="small_exchange", grid=(1,),
        out_shape=jax.ShapeDtypeStruct((8, D_MODEL), F32),
        in_specs=[_full(p.shape) for p in parts], out_specs=_full((8, D_MODEL)),
        scratch_shapes=[pltpu.VMEM((8, D_MODEL), F32), pltpu.VMEM((N_DEV, 8, D_MODEL), F32),
                        pltpu.SemaphoreType.DMA((N_PEERS,)), pltpu.SemaphoreType.DMA((N_PEERS,))],
        compiler_params=_params(),
    )(*parts)

    def apply(tot_ref, *refs):
        w_refs, m_refs, v_refs = refs[0:5], refs[5:10], refs[10:15]
        loss_out = refs[15]
        g_outs, d_outs, m_outs, v_outs = refs[16:21], refs[21:26], refs[26:31], refs[31:36]
        x, y, c = _my_place()
        tot = tot_ref[...]
        conv = pltpu.roll(tot[:, 0:512], (512 - 64 * (4 * x + 2 * y + c)) % 512, 1)[3:6, 0:64]
        grads = (tot[0:1, :], conv, tot[6:7, 0:8], tot[1:2, :], tot[2:3, :])
        loss_out[...] = tot[7:8, 0:128]
        for j in range(5):
            g_outs[j][...] = grads[j]
            d_outs[j][...], m_outs[j][...], v_outs[j][...] = _adamw_math(
                w_refs[j][...], grads[j], m_refs[j][...], v_refs[j][...])

    specs = [_full(w.shape) for w in ws]
    res = pl.pallas_call(
        apply, name="small_apply", grid=(1,),
        out_shape=[jax.ShapeDtypeStruct((1, 128), F32)] + [jax.ShapeDtypeStruct(w.shape, F32) for w in ws] * 4,
        in_specs=[_full((8, D_MODEL))] + specs * 3,
        out_specs=[_full((1, 128))] + specs * 4,
        compiler_params=_params(),
    )(tot, *ws, *ms, *vs)
    return res[0], res[1:6], res[6:11], res[11:16], res[16:21]


def kernel(x, mem, g_pre, w_in, w_conv, attn_sink, g_mem, w_mem_kv, w_up_a, w_up_b, w_up_m, w_out, g_post, loss_target, m_g_pre, m_w_in, m_w_conv, m_attn_sink, m_g_mem, m_w_mem_kv, m_w_up_a, m_w_up_b, m_w_up_m, m_w_out, m_g_post, v_g_pre, v_w_in, v_w_conv, v_attn_sink, v_g_mem, v_w_mem_kv, v_w_up_a, v_w_up_b, v_w_up_m, v_w_out, v_g_post):
    s = x.shape[1]
    x2, mem2, tgt2 = x[0], mem[0], loss_target[0]
    me = 4 * lax.axis_index("x") + 2 * lax.axis_index("y") + lax.axis_index("c")

    w_conv_loc = jnp.zeros((8, 128), F32).at[:3, :64].set(w_conv[0])
    w_int_g, w_conv_g, tabs, w_mkv_loc, w_out_loc, w_up_loc = _all_gather(
        [w_in[0].T.astype(BF16), w_conv_loc], "gather_w_in",
        splits=[[(112 * k, 112) for k in range(7)] + [(784, 144)], [(0, 8)]],
        side=_gather_side(s, w_mem_kv[0], w_out[0], (w_up_a[0], w_up_b[0], w_up_m[0])))
    w_int = w_int_g.reshape(IN_WIDTH, D_MODEL)
    w_conv_f = w_conv_g[:, :3, :64].transpose(1, 0, 2).reshape(3, 512)
    late = _gather_start([w_mkv_loc, w_out_loc, w_up_loc], me, "gather_late_start")
    sink = attn_sink[0]

    h, pa, pq, pkv, pbz, pmq, pmz, pg = _proj_fwd(x2, g_pre + late[4][0:1, 0:1], w_int, tabs)
    ya = _conv_fwd(pa, w_conv_f)
    yb = _attn_fwd(pq, pkv, pbz, sink)
    w_mkv_g, w_out_g, w_up_g = _gather_wait(*late[:4], yb, "gather_late_wait")
    w_mkv = w_mkv_g.reshape(D_MODEL, D_MODEL)
    w_out_f = w_out_g.reshape(D_MODEL, D_MODEL)
    mn, mkv = _mem_kv_fwd(mem2, g_mem, w_mkv)
    ym = _mem_attn_fwd(pmq, pmz, mkv)
    dg, dya, dyb, dym, dy, loss_p, gg_post, mb, dob, du = _mid(ya, yb, ym, pg, x2, tgt2, g_post, w_up_g, w_out_f)
    gw_out, gw_up = _gw_mid(mb, dob, (ya, yb, ym), du)

    core = lax.axis_index("c").astype(jnp.int32).reshape(1)
    chip = (2 * lax.axis_index("x") + lax.axis_index("y")).astype(jnp.int32).reshape(1)

    def exchange_start(shares, tag):
        from_sibling = _sibling_exchange(shares, "grads_to_sibling_" + tag)
        chip_shares = _pair_add(shares, from_sibling, core, "grads_pair_add_" + tag)
        return _chip_exchange_start(chip_shares, "grads_to_chips_start_" + tag)

    dmq, dmz, dmkv = _mem_attn_bwd(pmq, pmz, mkv, dym)
    gw_mkv, gg_mem = _mem_kv_bwd(mem2, g_mem, mn, dmkv, w_mkv)
    shares1 = [gw_mkv.reshape(N_DEV, 128, D_MODEL), gw_out.reshape(N_DEV, 128, D_MODEL), gw_up]
    sib = _split_start(_sibling_copies, N_CHIPS, shares1,
                       [lax.empty((N_CHIPS,) + a.shape[1:], a.dtype) for a in shares1], "grads_to_sibling_small_start")
    da, gw_conv = _conv_bwd(pa, dya, w_conv_f + sib[4][0:1, 0:1])
    shares1, from_sibling = _split_wait(_sibling_copies, N_CHIPS, *sib[:4], da, "grads_to_sibling_small_wait")
    send1, recv1, srcs1, lands1, token1 = _chip_exchange_start(
        _pair_add(shares1, from_sibling, core, "grads_pair_add_small"), "grads_to_chips_start_small")
    dq, dbz, dkv, g_sink = _attn_bwd(pq, pkv, pbz, dyb, sink + token1[0, 0], tabs)
    dparts = (da, dq, dkv, dbz, dmq, dmz, dg)
    gw_int = _gw_in(dparts, h)
    send2, recv2, srcs2, lands2, token2 = exchange_start([gw_int.reshape(N_DEV, SHARD_IN, D_MODEL)], "w_in")
    grad_x, gg_pre = _dh_bwd(dparts, x2, dy, g_pre + token2[0:1, 0:1], w_int)
    (o_mkv, o_out, o_up, o_int), (l_mkv, l_out, l_up, l_int) = _chip_exchange_wait(
        send1 + send2, recv1 + recv2, srcs1 + srcs2, lands1 + lands2, grad_x, "grads_to_chips_wait")

    loss_row, small_g, sd, sm, sv = _small_step(
        (gg_pre, gw_conv, g_sink, gg_mem, gg_post, loss_p),
        [g_pre, w_conv[0], attn_sink, g_mem, g_post],
        [m_g_pre, m_w_conv[0], m_attn_sink, m_g_mem, m_g_post],
        [v_g_pre, v_w_conv[0], v_attn_sink, v_g_mem, v_g_post])
    loss = loss_row[0, 0]
    g_g_pre, g_conv, g_sink_tot, g_g_mem, g_g_post = small_g

    g_w_in, d_w_in, nm_w_in, nv_w_in = (t.T for t in _sum_adamw(
        o_int, l_int, chip, 0, w_in[0].T, m_w_in[0].T, v_w_in[0].T, "adamw_w_in", tiles=2))
    (g_mkv, d_mkv, nm_mkv, nv_mkv), (g_out, d_out, nm_out, nv_out), *up = _sum_adamw_group(
        [(o_mkv, l_mkv, 0, w_mem_kv[0], m_w_mem_kv[0], v_w_mem_kv[0]),
         (o_out, l_out, 0, w_out[0], m_w_out[0], v_w_out[0]),
         (o_up, l_up, 0, w_up_a[0], m_w_up_a[0], v_w_up_a[0]),
         (o_up, l_up, 1, w_up_b[0], m_w_up_b[0], v_w_up_b[0]),
         (o_up, l_up, 2, w_up_m[0], m_w_up_m[0], v_w_up_m[0])], chip, "adamw_mid_weights")

    def lead(a):
        return a[None]

    grads = [g_g_pre, lead(g_w_in), lead(g_conv), g_sink_tot, g_g_mem, lead(g_mkv), lead(up[0][0]),
             lead(up[1][0]), lead(up[2][0]), lead(g_out), g_g_post]

    def assemble(small, big_in, big_mkv, big_up, big_out):
        return [small[0], lead(big_in), lead(small[1]), small[2], small[3], lead(big_mkv), lead(big_up[0]),
                lead(big_up[1]), lead(big_up[2]), lead(big_out), small[4]]

    deltas = assemble(sd, d_w_in, d_mkv, [u[1] for u in up], d_out)
    new_m = assemble(sm, nm_w_in, nm_mkv, [u[2] for u in up], nm_out)
    new_v = assemble(sv, nv_w_in, nv_mkv, [u[3] for u in up], nv_out)
    return (loss, grad_x[None], *grads, *deltas, *new_m, *new_v)
```

```python
import functools

import jax
import jax.numpy as jnp
from jax import lax
from jax.experimental import pallas as pl
from jax.experimental.pallas import tpu as pltpu

F32 = jnp.float32
BF16 = jnp.bfloat16
MESH = pl.DeviceIdType.MESH

N_DEV = 8
D_MODEL = 1024
EPS = 1e-6
ROPE_THETA = 500000.0
ROT_DIM = 16
HEAD_DIM = 64
ATTN_BLOCK = 128
MEM_HEADS = 4
MEM_HEAD_DIM = 128
ATTN_SCALE = HEAD_DIM ** -0.5
MEM_SCALE = MEM_HEAD_DIM ** -0.5

ADAM_LR = 0.001
ADAM_B1 = 0.9
ADAM_B2 = 0.999
ADAM_EPS = 1e-08
ADAM_WD = 0.01
ADAM_STEP = 10

SEG_A = (0, 2048)
SEG_BQ = (2048, 512)
SEG_BKV = (2560, 256)
SEG_BZ = (2816, 512)
SEG_MQ = (3328, 512)
SEG_MZ = (3840, 512)
SEG_G = (4352, 3072)
SEGS = (SEG_A, SEG_BQ, SEG_BKV, SEG_BZ, SEG_MQ, SEG_MZ, SEG_G)
IN_WIDTH = 7424
SHARD_IN = IN_WIDTH // N_DEV

V7X_VMEM_BYTES = 64 * 1024 * 1024
CALL_VMEM_MB = 57
ANY = pl.BlockSpec(memory_space=pl.ANY)


def _params():
    assert CALL_VMEM_MB * 1024 * 1024 < V7X_VMEM_BYTES
    return pltpu.CompilerParams(dimension_semantics=("arbitrary",), vmem_limit_bytes=CALL_VMEM_MB * 1024 * 1024)


def _full(shape):
    zeros = (0,) * len(shape)
    return pl.BlockSpec(shape, lambda i: zeros)


def _rows(tm, width):
    return pl.BlockSpec((tm, width), lambda i: (i, 0))


def _dot(a, b):
    return jnp.dot(a, b, preferred_element_type=F32)


def _dot_nt(a, b):
    return lax.dot_general(a, b, (((1,), (1,)), ((), ())), preferred_element_type=F32)


def _dot_tn(a, b):
    return lax.dot_general(a, b, (((0,), (0,)), ((), ())), preferred_element_type=F32)


def _sigmoid(z):
    return 1.0 / (1.0 + jnp.exp(-z))


def _rope(t, cs, s1, s2):
    return t * cs + pltpu.roll(t, 120, 1) * s1 + pltpu.roll(t, 8, 1) * s2


def _rope_t(d, cs, s1, s2):
    return d * cs + pltpu.roll(d * s1, 8, 1) + pltpu.roll(d * s2, 120, 1)


def _gather_side(s, w_mkv, w_out, w_ups):
    half = ROT_DIM // 2
    inv_freq = jnp.power(jnp.float32(ROPE_THETA), -jnp.arange(half, dtype=F32) * (2.0 / ROT_DIM))
    freq_row = jnp.tile(jnp.concatenate([inv_freq, inv_freq, jnp.zeros((HEAD_DIM - ROT_DIM,), F32)]), 2)[None, :]

    def fn(in_refs, out_refs):
        f_ref, mkv_ref, out_ref, *up_refs = in_refs
        t_ref, mkv_bf, out_bf, up_bf = out_refs
        mkv_bf[...] = mkv_ref[...].astype(BF16)
        out_bf[...] = out_ref[...].astype(BF16)
        for k, up_ref in enumerate(up_refs):
            up_bf[512 * k:512 * k + 512, :] = up_ref[...].astype(BF16)
        pos = lax.broadcasted_iota(jnp.int32, (s, 128), 0).astype(F32)
        d = lax.broadcasted_iota(jnp.int32, (s, 128), 1) & (HEAD_DIM - 1)
        ang = pos * f_ref[...]
        cos, sin = jnp.cos(ang), jnp.sin(ang)
        lo, hi = d < half, (d >= half) & (d < ROT_DIM)
        t_ref[0] = jnp.where(lo | hi, cos, 1.0)
        t_ref[1] = jnp.where(lo, -sin, 0.0)
        t_ref[2] = jnp.where(hi, sin, 0.0)

    return ([freq_row, w_mkv, w_out, *w_ups],
            [jax.ShapeDtypeStruct((3, s, 128), F32), jax.ShapeDtypeStruct(w_mkv.shape, BF16),
             jax.ShapeDtypeStruct(w_out.shape, BF16), jax.ShapeDtypeStruct((1536, 128), BF16)], fn)


def _load_once(pairs, sems):
    @pl.when(pl.program_id(0) == 0)
    def _():
        cps = [pltpu.make_async_copy(src, dst, sems.at[k]) for k, (src, dst) in enumerate(pairs)]
        for cp in cps:
            cp.start()
        for cp in cps:
            cp.wait()


def _my_place():
    x, y, c = lax.axis_index("x"), lax.axis_index("y"), lax.axis_index("c")
    return x, y, c


def _all_gather(arrs, name, splits=None, side=None):
    n = len(arrs)
    if splits is None:
        splits = [[(0, a.shape[0])] for a in arrs]
    pieces = [(a, r0, rn) for a in range(n) for r0, rn in splits[a]]
    n_p = len(pieces)
    side_in, side_out, side_fn = side if side is not None else ((), (), None)
    m, q = len(side_in), len(side_out)

    def body(*refs):
        ins, outs = refs[:n], refs[n + m:2 * n + m]
        send_sems, recv_sems, local_sems = refs[2 * n + m + q:]
        x, y, c = _my_place()
        me, sibling = (x, y, c), (x, y, 1 - c)

        def route(core):
            first = (jnp.bitwise_xor(x, 1 - core), jnp.bitwise_xor(y, core), core)
            second = (jnp.bitwise_xor(x, core), jnp.bitwise_xor(y, 1 - core), core)
            return first, second, (1 - x, 1 - y, core)

        def idx(px, py, pc):
            return 4 * px + 2 * py + pc

        def copy(p, k, block, to, own=False):
            a, r0, rn = pieces[p]
            dst = outs[a].at[idx(*block), pl.ds(r0, rn)]
            return pltpu.make_async_remote_copy(
                src_ref=ins[a].at[pl.ds(r0, rn)] if own else dst, dst_ref=dst,
                send_sem=send_sems.at[p * 7 + k], recv_sem=recv_sems.at[p * 7 + k],
                device_id=to, device_id_type=MESH)

        nbr1, nbr2, diag = route(c)
        mine = [pltpu.make_async_copy(ins[a], outs[a].at[idx(*me)], local_sems.at[a]) for a in range(n)]
        for cp in mine:
            cp.start()
        sent = []
        for p in range(n_p):
            for k, to in enumerate((sibling, nbr1, nbr2)):
                sent.append(copy(p, k, me, to, own=True))
        for cp in sent:
            cp.start()
        if side_fn is not None:
            side_fn(refs[n:n + m], refs[2 * n + m:2 * n + m + q])
        for k_in, block, onward in ((1, nbr1, ((3, nbr2), (4, sibling))), (2, nbr2, ((5, sibling),)),
                                    (3, diag, ((6, sibling),))):
            for p in range(n_p):
                copy(p, k_in, block, me).wait_recv()
                for k_out, to in onward:
                    cp = copy(p, k_out, block, to)
                    cp.start()
                    sent.append(cp)
        s1, s2, sd = route(1 - c)
        for k_in, block in ((0, sibling), (4, s1), (5, s2), (6, sd)):
            for p in range(n_p):
                copy(p, k_in, block, me).wait_recv()
        for cp in sent:
            cp.wait_send()
        for cp in mine:
            cp.wait()

    return pl.pallas_call(
        body, name=name,
        out_shape=[jax.ShapeDtypeStruct((N_DEV,) + a.shape, a.dtype) for a in arrs] + list(side_out),
        in_specs=[ANY] * n + [pl.BlockSpec(memory_space=pltpu.VMEM)] * m,
        out_specs=[ANY] * n + [pl.BlockSpec(memory_space=pltpu.VMEM)] * q,
        scratch_shapes=[pltpu.SemaphoreType.DMA((7 * n_p,)), pltpu.SemaphoreType.DMA((7 * n_p,)),
                        pltpu.SemaphoreType.DMA((n,))],
        compiler_params=pltpu.CompilerParams(vmem_limit_bytes=32 * 1024 * 1024),
    )(*arrs, *side_in)


N_CHIPS = 4


def _sibling_exchange(arrs, name):
    n = len(arrs)

    def body(*refs):
        ins, outs = refs[:n], refs[n:2 * n]
        send_sems, recv_sems = refs[2 * n:]
        x, y, c = _my_place()
        sibling = (x, y, 1 - c)

        def copy(a, j):
            return pltpu.make_async_remote_copy(
                src_ref=ins[a].at[2 * j + (1 - c)], dst_ref=outs[a].at[j],
                send_sem=send_sems.at[a * N_CHIPS + j], recv_sem=recv_sems.at[a * N_CHIPS + j],
                device_id=sibling, device_id_type=MESH)

        cps = [copy(a, j) for j in range(N_CHIPS) for a in range(n)]
        for cp in cps:
            cp.start()
        for cp in cps:
            cp.wait_recv()
        for cp in cps:
            cp.wait_send()

    return pl.pallas_call(
        body, name=name,
        out_shape=[jax.ShapeDtypeStruct((N_CHIPS,) + a.shape[1:], a.dtype) for a in arrs],
        in_specs=[ANY] * n, out_specs=[ANY] * n,
        scratch_shapes=[pltpu.SemaphoreType.DMA((N_CHIPS * n,)), pltpu.SemaphoreType.DMA((N_CHIPS * n,))],
    )(*arrs)


def _sibling_copies(srcs, lands, send_sems, recv_sems):
    x, y, c = _my_place()
    cps = []
    for j in range(N_CHIPS):
        for a in range(len(srcs)):
            k = a * N_CHIPS + j
            cps.append(pltpu.make_async_remote_copy(
                src_ref=srcs[a].at[2 * j + (1 - c)], dst_ref=lands[a].at[j], send_sem=send_sems[k],
                recv_sem=recv_sems[k], device_id=(x, y, 1 - c), device_id_type=MESH))
    return cps


def _pair_add(mine, recv, core, name):
    n = len(mine)

    def body(c_ref, *refs):
        for a in range(n):
            refs[2 * n + a][...] = (refs[a][...].astype(F32) + refs[n + a][...].astype(F32)).astype(BF16)

    def blk(a):
        return (None,) + a.shape[1:]

    grid_spec = pltpu.PrefetchScalarGridSpec(
        num_scalar_prefetch=1, grid=(N_CHIPS,),
        in_specs=[pl.BlockSpec(blk(a), lambda j, c_ref: (2 * j + c_ref[0], 0, 0)) for a in mine]
        + [pl.BlockSpec(blk(a), lambda j, c_ref: (j, 0, 0)) for a in recv],
        out_specs=[pl.BlockSpec(blk(a), lambda j, c_ref: (j, 0, 0)) for a in recv])
    return pl.pallas_call(
        body, name=name, grid_spec=grid_spec,
        out_shape=[jax.ShapeDtypeStruct(a.shape, BF16) for a in recv],
        compiler_params=_params(),
    )(core, *mine, *recv)


HBM = pl.BlockSpec(memory_space=pltpu.HBM)
SEM = pl.BlockSpec(memory_space=pltpu.SEMAPHORE)
N_PEER_CHIPS = 3


def _chip_copies(srcs, lands, send_sems, recv_sems):
    x, y, c = _my_place()
    my_chip = 2 * x + y
    peers = [(x, 1 - y), (1 - x, y), (1 - x, 1 - y)]
    cps = []
    for k, (px, py) in enumerate(peers):
        for a in range(len(srcs)):
            j = a * N_PEER_CHIPS + k
            cps.append(pltpu.make_async_remote_copy(
                src_ref=srcs[a].at[2 * px + py], dst_ref=lands[a].at[my_chip],
                send_sem=send_sems[j], recv_sem=recv_sems[j],
                device_id=(px, py, c), device_id_type=MESH))
    return cps


N_PEERS = N_DEV - 1


def _gather_copies(srcs, lands, send_sems, recv_sems):
    x, y, c = _my_place()
    me_idx = 4 * x + 2 * y + c
    flips = [(0, 0, 1), (0, 1, 0), (1, 0, 0), (0, 1, 1), (1, 0, 1), (1, 1, 0), (1, 1, 1)]
    cps = []
    for k, (fx, fy, fc) in enumerate(flips):
        peer = ((1 - x) if fx else x, (1 - y) if fy else y, (1 - c) if fc else c)
        for a in range(len(srcs)):
            j = a * N_PEERS + k
            cps.append(pltpu.make_async_remote_copy(
                src_ref=srcs[a], dst_ref=lands[a].at[me_idx], send_sem=send_sems[j], recv_sem=recv_sems[j],
                device_id=peer, device_id_type=MESH))
    return cps


def _split_start(copies, per_array, arrs, lands, name):
    arrs, lands = list(arrs), list(lands)
    n = len(arrs)
    k = n * per_array

    def body(*refs):
        srcs, land_refs = refs[:n], refs[n:2 * n]
        send_sems, recv_sems = refs[2 * n:2 * n + k], refs[2 * n + k:2 * n + 2 * k]
        token = refs[-1]
        for cp in copies(srcs, land_refs, send_sems, recv_sems):
            cp.start()
        token[...] = jnp.zeros_like(token)

    hbm_arrs = [pltpu.with_memory_space_constraint(a, pltpu.HBM) for a in arrs]
    lands = [pltpu.with_memory_space_constraint(a, pltpu.HBM) for a in lands]
    res = pl.pallas_call(
        body, name=name,
        out_shape=[pltpu.SemaphoreType.DMA(())] * (2 * k) + [pltpu.HBM(a.shape, a.dtype) for a in arrs + lands]
        + [jax.ShapeDtypeStruct((8, 128), F32)],
        in_specs=[HBM] * (2 * n),
        out_specs=[SEM] * (2 * k) + [HBM] * (2 * n) + [pl.BlockSpec(memory_space=pltpu.VMEM)],
        input_output_aliases={a: 2 * k + a for a in range(2 * n)},
        compiler_params=pltpu.CompilerParams(has_side_effects=pltpu.SideEffectType.DATAFLOW_SIDE_EFFECTING),
    )(*hbm_arrs, *lands)
    return res[:k], res[k:2 * k], res[2 * k:2 * k + n], res[2 * k + n:2 * k + 2 * n], res[-1]


def _split_wait(copies, per_array, send_sems, recv_sems, srcs, lands, after, name):
    n = len(srcs)
    k = n * per_array

    def body(*refs):
        src_refs, land_refs = refs[:n], refs[n:2 * n]
        s_sems, r_sems = refs[2 * n:2 * n + k], refs[2 * n + k:2 * n + 2 * k]
        for cp in copies(src_refs, land_refs, s_sems, r_sems):
            cp.wait_send()
            cp.wait_recv()

    res = pl.pallas_call(
        body, name=name,
        out_shape=[pltpu.HBM(a.shape, a.dtype) for a in list(srcs) + list(lands)],
        in_specs=[HBM] * (2 * n) + [SEM] * (2 * k) + [ANY],
        out_specs=[HBM] * (2 * n),
        input_output_aliases={a: a for a in range(2 * n)},
        compiler_params=pltpu.CompilerParams(has_side_effects=pltpu.SideEffectType.DATAFLOW_SIDE_EFFECTING),
    )(*srcs, *lands, *send_sems, *recv_sems, after)
    return res[:n], res[n:]


def _chip_exchange_start(arrs, name):
    return _split_start(_chip_copies, N_PEER_CHIPS, arrs, [lax.empty(a.shape, a.dtype) for a in arrs], name)


def _chip_exchange_wait(send_sems, recv_sems, srcs, lands, after, name):
    return _split_wait(_chip_copies, N_PEER_CHIPS, send_sems, recv_sems, srcs, lands, after, name)


def _gather_start(arrs, me_idx, name):
    lands = [lax.dynamic_update_slice(lax.empty((N_DEV,) + a.shape, a.dtype), a[None], (me_idx, 0, 0)) for a in arrs]
    return _split_start(_gather_copies, N_PEERS, arrs, lands, name)


def _gather_wait(send_sems, recv_sems, srcs, lands, after, name):
    return _split_wait(_gather_copies, N_PEERS, send_sems, recv_sems, srcs, lands, after, name)[1]


def _proj_fwd(x, g_pre, w_int, tabs):
    s = x.shape[0]
    tm = min(512, s)

    def body(x_ref, g_ref, t_ref, w_hbm,
             h_ref, pa_ref, pq_ref, pkv_ref, pbz_ref, pmq_ref, pmz_ref, pg_ref, w_vm, sems):
        _load_once([(w_hbm, w_vm)], sems)
        xf = x_ref[...]
        r = lax.rsqrt(jnp.mean(xf * xf, axis=-1, keepdims=True) + EPS)
        h = ((xf * r) * g_ref[...]).astype(BF16)
        h_ref[...] = h
        cs, s1, s2 = t_ref[0], t_ref[1], t_ref[2]

        def mm(seg, c0, width):
            return _dot_nt(h, w_vm[seg[0] + c0:seg[0] + c0 + width, :])

        for c0 in range(0, SEG_A[1], 512):
            pa_ref[:, c0:c0 + 512] = mm(SEG_A, c0, 512).astype(BF16)
        q = mm(SEG_BQ, 0, 512)
        for b in range(4):
            pq_ref[:, 128 * b:128 * b + 128] = _rope(q[:, 128 * b:128 * b + 128], cs, s1, s2).astype(BF16)
        kv = mm(SEG_BKV, 0, 256)
        pkv_ref[:, 0:128] = _rope(kv[:, 0:128], cs, s1, s2).astype(BF16)
        pkv_ref[:, 128:256] = kv[:, 128:256].astype(BF16)
        pbz_ref[...] = mm(SEG_BZ, 0, 512).astype(BF16)
        pmq_ref[...] = mm(SEG_MQ, 0, 512).astype(BF16)
        pmz_ref[...] = mm(SEG_MZ, 0, 512).astype(BF16)
        for c0 in range(0, SEG_G[1], 512):
            pg_ref[:, c0:c0 + 512] = mm(SEG_G, c0, 512).astype(BF16)

    widths = (D_MODEL, 2048, 512, 256, 512, 512, 512, 3072)
    return pl.pallas_call(
        body, name="proj_fwd", grid=(s // tm,),
        out_shape=[jax.ShapeDtypeStruct((s, w), BF16) for w in widths],
        in_specs=[_rows(tm, D_MODEL), _full((1, D_MODEL)), pl.BlockSpec((3, tm, 128), lambda i: (0, i, 0)), ANY],
        out_specs=[_rows(tm, w) for w in widths],
        scratch_shapes=[pltpu.VMEM((IN_WIDTH, D_MODEL), BF16), pltpu.SemaphoreType.DMA((1,))],
        compiler_params=_params(),
    )(x, g_pre, tabs, w_int)


def _mem_kv_fwd(mem, g_mem, w_mkv):
    m = mem.shape[0]

    def body(mem_ref, g_ref, w_ref, mn_ref, mkv_ref):
        xf = mem_ref[...]
        r = lax.rsqrt(jnp.mean(xf * xf, axis=-1, keepdims=True) + EPS)
        mn = ((xf * r) * g_ref[...]).astype(BF16)
        mn_ref[...] = mn
        mkv_ref[...] = _dot(mn, w_ref[...]).astype(BF16)

    return pl.pallas_call(
        body, name="mem_kv_fwd", grid=(1,),
        out_shape=[jax.ShapeDtypeStruct((m, D_MODEL), BF16)] * 2,
        in_specs=[_full((m, D_MODEL)), _full((1, D_MODEL)), _full((D_MODEL, D_MODEL))],
        out_specs=[_full((m, D_MODEL))] * 2,
        compiler_params=_params(),
    )(mem, g_mem, w_mkv)


def _halo_specs(s, tm, rows, width):
    nblk = s // rows
    prev = pl.BlockSpec((rows, width), lambda i: (jnp.maximum(i * (tm // rows) - 1, 0), 0))
    nxt = pl.BlockSpec((rows, width), lambda i: (jnp.minimum((i + 1) * (tm // rows), nblk - 1), 0))
    return prev, nxt


def _conv_common(pa, cu_prev, cu_next, w, tm):
    b, c, u, z = (pa[:, 512 * k:512 * k + 512] for k in range(4))
    cu = c * u
    row = lax.broadcasted_iota(jnp.int32, (tm, 512), 0)
    cu_m1 = jnp.where(row == 0, cu_prev, pltpu.roll(cu, 1, 0))
    cu_p1 = jnp.where(row == tm - 1, cu_next, pltpu.roll(cu, tm - 1, 0))
    y = cu_m1 * w[0:1] + cu * w[1:2] + cu_p1 * w[2:3]
    sig = _sigmoid(z)
    return b, c, u, z, cu, cu_m1, cu_p1, y, sig, row


def _conv_fwd(pa, w_conv):
    s = pa.shape[0]
    tm = min(512, s)
    nt = s // tm

    def body(pa_ref, pp_ref, pn_ref, w_ref, ya_ref):
        i = pl.program_id(0)
        prev_row = pp_ref[...].astype(F32)[15:16, :]
        next_row = pn_ref[...].astype(F32)[0:1, :]
        b, _, _, z, _, _, _, y, sig, _ = _conv_common(
            pa_ref[...].astype(F32),
            jnp.where(i == 0, 0.0, prev_row[:, 512:1024] * prev_row[:, 1024:1536]),
            jnp.where(i == nt - 1, 0.0, next_row[:, 512:1024] * next_row[:, 1024:1536]), w_ref[...], tm)
        ya_ref[...] = (b * y * (z * sig)).astype(BF16)

    prev, nxt = _halo_specs(s, tm, 16, 2048)
    return pl.pallas_call(
        body, name="conv_fwd", grid=(nt,),
        out_shape=jax.ShapeDtypeStruct((s, 512), BF16),
        in_specs=[_rows(tm, 2048), prev, nxt, _full((3, 512))],
        out_specs=_rows(tm, 512),
        compiler_params=_params(),
    )(pa, pa, pa, w_conv)


def _heads_to_lanes(a, g, row):
    low = row < HEAD_DIM
    parts = []
    for b in (2 * g, 2 * g + 1):
        t = jnp.transpose(a[:, 128 * b:128 * b + 128])
        swapped = pltpu.roll(t, HEAD_DIM, 0)
        if g == 0:
            parts += [jnp.where(low, t, 0.0), jnp.where(low, swapped, 0.0)]
        else:
            parts += [jnp.where(low, 0.0, swapped), jnp.where(low, 0.0, t)]
    return jnp.concatenate(parts, axis=1)


def _lanes_to_heads(t0, t1, row):
    low = row < HEAD_DIM
    blocks = []
    for b in range(4):
        g = b // 2
        tg = (t0, t1)[g]
        je = 2 * (b - 2 * g)
        even, odd = tg[:, 128 * je:128 * je + 128], tg[:, 128 * je + 128:128 * je + 256]
        if g == 0:
            t = jnp.where(low, even, pltpu.roll(odd, HEAD_DIM, 0))
        else:
            t = jnp.where(low, pltpu.roll(even, HEAD_DIM, 0), odd)
        blocks.append(jnp.transpose(t))
    return jnp.concatenate(blocks, axis=1)


WINDOW_KEYS = 3 * ATTN_BLOCK
STACKED = 4 * ATTN_BLOCK
KEY_CHUNK = 32
MAX_BLOCKS_IN_STEP = 8


def _fill_band_bias(bias, nb):
    assert nb >= 2
    c = lax.broadcasted_iota(jnp.int32, (WINDOW_KEYS, STACKED), 0)
    r = lax.broadcasted_iota(jnp.int32, (WINDOW_KEYS, STACKED), 1) & (ATTN_BLOCK - 1)
    band = (c >= r) & (c <= r + 2 * ATTN_BLOCK)
    for v, ok in enumerate((band, band & (c >= ATTN_BLOCK), band & (c < 2 * ATTN_BLOCK))):
        bias[v] = jnp.where(ok, 0.0, -jnp.inf)


def _bias_variant(n, nb):
    return jnp.where(n == 0, 1, jnp.where(n == nb - 1, 2, 0))


def _sink_row(sink_ref, g):
    return jnp.concatenate([jnp.full((1, ATTN_BLOCK), sink_ref[4 * g + j], F32) for j in range(4)], axis=1)


def _softmax_keys_major(sc, bias, variant, sink, e_scr):
    chunks = [pl.ds(k * KEY_CHUNK, KEY_CHUNK) for k in range(WINDOW_KEYS // KEY_CHUNK)]
    rows = [slice(k * KEY_CHUNK, (k + 1) * KEY_CHUNK) for k in range(WINDOW_KEYS // KEY_CHUNK)]
    m_run = jnp.full((KEY_CHUNK, STACKED), -jnp.inf, F32)
    for ck, rw in zip(chunks, rows):
        m_run = jnp.maximum(m_run, sc[rw] + bias[variant, ck, :])
    m = jnp.maximum(jnp.max(m_run, axis=0, keepdims=True), sink)
    l_run = jnp.zeros((KEY_CHUNK, STACKED), F32)
    for ck, rw in zip(chunks, rows):
        e = jnp.exp(sc[rw] + bias[variant, ck, :] - m)
        l_run += e
        e_scr[rw, :] = e.astype(BF16)
    es = jnp.exp(sink - m)
    inv = 1.0 / (jnp.sum(l_run, axis=0, keepdims=True) + es)
    return inv, es * inv


def _fill_padded(kv_ref, kpad, vpad, s):
    zero = jnp.zeros((ATTN_BLOCK, 128), BF16)
    kpad[0:ATTN_BLOCK, :] = zero
    vpad[0:ATTN_BLOCK, :] = zero
    kpad[ATTN_BLOCK + s:2 * ATTN_BLOCK + s, :] = zero
    vpad[ATTN_BLOCK + s:2 * ATTN_BLOCK + s, :] = zero
    kpad[ATTN_BLOCK:ATTN_BLOCK + s, :] = kv_ref[:, 0:128]
    vpad[ATTN_BLOCK:ATTN_BLOCK + s, :] = kv_ref[:, 128:256]


def _attn_fwd(pq, pkv, pbz, sink):
    s = pq.shape[0]
    nb = s // ATTN_BLOCK
    bps = min(MAX_BLOCKS_IN_STEP, nb)

    def body(sink_ref, q_ref, z_ref, kv_ref, yb_ref, kpad, vpad, bias, e_scr):
        i = pl.program_id(0)

        @pl.when(i == 0)
        def _():
            _fill_padded(kv_ref, kpad, vpad, s)
            _fill_band_bias(bias, nb)

        row = lax.broadcasted_iota(jnp.int32, (ATTN_BLOCK, 128), 0)
        for b in range(bps):
            n = i * bps + b
            rows = slice(b * ATTN_BLOCK, (b + 1) * ATTN_BLOCK)
            start = pl.multiple_of(n * ATTN_BLOCK, ATTN_BLOCK)
            kw, vw = kpad[pl.ds(start, WINDOW_KEYS), :], vpad[pl.ds(start, WINDOW_KEYS), :]
            qf = q_ref[rows, :].astype(F32)
            variant = _bias_variant(n, nb)
            outs = []
            for g in range(2):
                e_bg = e_scr.at[2 * b + g]
                qt = (_heads_to_lanes(qf, g, row) * ATTN_SCALE).astype(BF16)
                inv, _ = _softmax_keys_major(_dot(kw, qt), bias, variant, _sink_row(sink_ref, g), e_bg)
                outs.append(_dot_tn(vw, e_bg[...]) * inv)
            attn = _lanes_to_heads(outs[0], outs[1], row)
            z = z_ref[rows, :].astype(F32)
            yb_ref[rows, :] = (attn * (z * _sigmoid(z))).astype(BF16)

    tq = bps * ATTN_BLOCK
    return pl.pallas_call(
        body, name="attn_fwd", grid=(s // tq,),
        out_shape=jax.ShapeDtypeStruct((s, 512), BF16),
        in_specs=[pl.BlockSpec(memory_space=pltpu.SMEM), _rows(tq, 512), _rows(tq, 512), _full((s, 256))],
        out_specs=_rows(tq, 512),
        scratch_shapes=[pltpu.VMEM((s + 2 * ATTN_BLOCK, 128), BF16)] * 2
        + [pltpu.VMEM((3, WINDOW_KEYS, STACKED), F32),
           pltpu.VMEM((2 * bps, WINDOW_KEYS, STACKED), BF16)],
        compiler_params=_params(),
    )(sink, pq, pbz, pkv)


def _mem_softmax_t(q, mk):
    sc = _dot_nt(mk, q) * MEM_SCALE
    e = jnp.exp(sc - jnp.max(sc, axis=0, keepdims=True))
    return e * (1.0 / jnp.sum(e, axis=0, keepdims=True))


def _mem_attn_fwd(pmq, pmz, mkv):
    s = pmq.shape[0]
    m = mkv.shape[0]
    tm = min(512, s)

    def body(q_ref, z_ref, mk_ref, mv_ref, ym_ref):
        z = z_ref[...].astype(F32)
        sz = z * _sigmoid(z)
        for h in range(MEM_HEADS):
            cols = slice(128 * h, 128 * h + 128)
            pt = _mem_softmax_t(q_ref[:, cols], mk_ref[:, cols])
            o = _dot_tn(pt.astype(BF16), mv_ref[:, cols])
            ym_ref[:, cols] = (o * sz[:, cols]).astype(BF16)

    return pl.pallas_call(
        body, name="mem_attn_fwd", grid=(s // tm,),
        out_shape=jax.ShapeDtypeStruct((s, 512), BF16),
        in_specs=[_rows(tm, 512), _rows(tm, 512), pl.BlockSpec((m, 512), lambda i: (0, 0)),
                  pl.BlockSpec((m, 512), lambda i: (0, 1))],
        out_specs=_rows(tm, 512),
        compiler_params=_params(),
    )(pmq, pmz, mkv, mkv)


def _mid(ya, yb, ym, pg, x, target, g_post, w_up, w_out):
    s = x.shape[0]
    tm = min(256, s)
    nt = s // tm

    def body(ya_ref, yb_ref, ym_ref, pg_ref, x_ref, t_ref, gp_ref, wup_hbm, wout_hbm,
             dg_ref, dya_ref, dyb_ref, dym_ref, dy_ref, loss_ref, ggp_ref, mb_ref, dob_ref, du_ref,
             wup_vm, wout_vm, sems):
        i = pl.program_id(0)
        _load_once([(wup_hbm.at[d], wup_vm.at[:, pl.ds(128 * d, 128)]) for d in range(N_DEV)]
                   + [(wout_hbm, wout_vm)], sems)

        @pl.when(i == 0)
        def _():
            loss_ref[...] = jnp.zeros_like(loss_ref)
            ggp_ref[...] = jnp.zeros_like(ggp_ref)

        ys = (ya_ref[...], yb_ref[...], ym_ref[...])
        us = [_dot(ys[k], wup_vm[512 * k:512 * k + 512, :]) for k in range(3)]
        gates = [_sigmoid(pg_ref[:, 1024 * k:1024 * k + 1024].astype(F32)) for k in range(3)]
        merged = gates[0] * us[0] + gates[1] * us[1] + gates[2] * us[2]
        mb = merged.astype(BF16)
        mb_ref[...] = mb
        out = _dot(mb, wout_vm[...])
        r = lax.rsqrt(jnp.mean(out * out, axis=-1, keepdims=True) + EPS)
        on = out * r
        gp = gp_ref[...]
        err = (x_ref[...] + on * gp) - t_ref[...]
        loss_ref[...] += 0.5 * jnp.sum(err * err) * (1.0 / D_MODEL)
        dy = err * (1.0 / D_MODEL)
        dy_ref[...] = dy
        ggp_ref[...] += jnp.sum(dy * on, axis=0, keepdims=True)
        a = dy * gp
        d_out = r * (a - on * jnp.mean(a * on, axis=-1, keepdims=True))
        dob = d_out.astype(BF16)
        dob_ref[...] = dob
        d_merged = _dot_nt(dob, wout_vm[...])
        d_refs = (dya_ref, dyb_ref, dym_ref)
        for k in range(3):
            g = gates[k]
            du_f = d_merged * g
            dg_ref[:, 1024 * k:1024 * k + 1024] = (du_f * us[k] * (1.0 - g)).astype(BF16)
            du = du_f.astype(BF16)
            du_ref[k] = du
            d_refs[k][...] = _dot_nt(du, wup_vm[512 * k:512 * k + 512, :]).astype(BF16)

    return pl.pallas_call(
        body, name="mid", grid=(nt,),
        out_shape=[jax.ShapeDtypeStruct((s, 3072), BF16)] + [jax.ShapeDtypeStruct((s, 512), BF16)] * 3
        + [jax.ShapeDtypeStruct((s, D_MODEL), F32), jax.ShapeDtypeStruct((8, 128), F32),
           jax.ShapeDtypeStruct((1, D_MODEL), F32), jax.ShapeDtypeStruct((s, D_MODEL), BF16),
           jax.ShapeDtypeStruct((s, D_MODEL), BF16), jax.ShapeDtypeStruct((3, s, D_MODEL), BF16)],
        in_specs=[_rows(tm, 512)] * 3 + [_rows(tm, 3072), _rows(tm, D_MODEL), _rows(tm, D_MODEL),
                                         _full((1, D_MODEL)), ANY, ANY],
        out_specs=[_rows(tm, 3072)] + [_rows(tm, 512)] * 3
        + [_rows(tm, D_MODEL), _full((8, 128)), _full((1, D_MODEL)), _rows(tm, D_MODEL), _rows(tm, D_MODEL),
           pl.BlockSpec((3, tm, D_MODEL), lambda i: (0, i, 0))],
        scratch_shapes=[pltpu.VMEM((1536, D_MODEL), BF16), pltpu.VMEM((D_MODEL, D_MODEL), BF16),
                        pltpu.SemaphoreType.DMA((N_DEV + 1,))],
        compiler_params=_params(),
    )(ya, yb, ym, pg, x, target, g_post, w_up, w_out)


def _gw_mid(mb, dob, ys, du):
    s = mb.shape[0]
    tn = 256

    def out_body(mb_ref, dob_ref, o_ref):
        o_ref[...] = _dot_tn(mb_ref[...], dob_ref[...]).astype(BF16)

    gw_out = pl.pallas_call(
        out_body, name="gw_out", grid=(D_MODEL // tn,),
        out_shape=jax.ShapeDtypeStruct((D_MODEL, D_MODEL), BF16),
        in_specs=[pl.BlockSpec((s, tn), lambda j: (0, j)), _full((s, D_MODEL))],
        out_specs=pl.BlockSpec((tn, D_MODEL), lambda j: (j, 0)),
        compiler_params=_params(),
    )(mb, dob)

    per = 512 // tn

    def up_body(ya_ref, yb_ref, ym_ref, du_ref, o_ref):
        j = pl.program_id(0)
        for k, y_ref in enumerate((ya_ref, yb_ref, ym_ref)):
            @pl.when(j // per == k)
            def _(y_ref=y_ref):
                res = _dot_tn(y_ref[...], du_ref[...])
                for d in range(N_DEV):
                    o_ref[d] = res[:, 128 * d:128 * d + 128].astype(BF16)

    def y_spec(k):
        return pl.BlockSpec((s, tn), lambda j: (0, jnp.clip(j - per * k, 0, per - 1)))

    gw_up = pl.pallas_call(
        up_body, name="gw_up", grid=(3 * per,),
        out_shape=jax.ShapeDtypeStruct((N_DEV, 1536, 128), BF16),
        in_specs=[y_spec(0), y_spec(1), y_spec(2), pl.BlockSpec((None, s, D_MODEL), lambda j: (j // per, 0, 0))],
        out_specs=pl.BlockSpec((N_DEV, tn, 128), lambda j: (0, j, 0)),
        compiler_params=_params(),
    )(*ys, du)
    return gw_out, gw_up


def _conv_bwd(pa, dya, w_conv):
    s = pa.shape[0]
    tm = min(512, s)
    nt = s // tm

    def body(pa_ref, pp_ref, pn_ref, d_ref, dp_ref, dn_ref, w_ref, da_ref, gw_ref):
        i = pl.program_id(0)
        first, last = i == 0, i == nt - 1

        @pl.when(first)
        def _():
            gw_ref[...] = jnp.zeros_like(gw_ref)

        w = w_ref[...]
        prev_row = pp_ref[...].astype(F32)[15:16, :]
        next_row = pn_ref[...].astype(F32)[0:1, :]
        b, c, u, z, cu, cu_m1, cu_p1, y, sig, row = _conv_common(
            pa_ref[...].astype(F32),
            jnp.where(first, 0.0, prev_row[:, 512:1024] * prev_row[:, 1024:1536]),
            jnp.where(last, 0.0, next_row[:, 512:1024] * next_row[:, 1024:1536]), w, tm)
        sz = z * sig
        dya_t = d_ref[...].astype(F32)
        d_y = dya_t * b * sz

        def halo_dy(p_row, d_row):
            zz = p_row[:, 1536:2048]
            return d_row * p_row[:, 0:512] * (zz * _sigmoid(zz))

        dy_prev = jnp.where(first, 0.0, halo_dy(prev_row, dp_ref[...].astype(F32)[15:16, :]))
        dy_next = jnp.where(last, 0.0, halo_dy(next_row, dn_ref[...].astype(F32)[0:1, :]))
        dy_m1 = jnp.where(row == 0, dy_prev, pltpu.roll(d_y, 1, 0))
        dy_p1 = jnp.where(row == tm - 1, dy_next, pltpu.roll(d_y, tm - 1, 0))
        d_cu = dy_p1 * w[0:1] + d_y * w[1:2] + dy_m1 * w[2:3]
        da_ref[:, 0:512] = (dya_t * y * sz).astype(BF16)
        da_ref[:, 512:1024] = (d_cu * u).astype(BF16)
        da_ref[:, 1024:1536] = (d_cu * c).astype(BF16)
        da_ref[:, 1536:2048] = (dya_t * b * y * (sig + sz * (1.0 - sig))).astype(BF16)
        gw_ref[0:1, :] += jnp.sum(d_y * cu_m1, axis=0, keepdims=True)
        gw_ref[1:2, :] += jnp.sum(d_y * cu, axis=0, keepdims=True)
        gw_ref[2:3, :] += jnp.sum(d_y * cu_p1, axis=0, keepdims=True)

    prev, nxt = _halo_specs(s, tm, 16, 2048)
    dprev, dnxt = _halo_specs(s, tm, 16, 512)
    return pl.pallas_call(
        body, name="conv_bwd", grid=(nt,),
        out_shape=[jax.ShapeDtypeStruct((s, 2048), BF16), jax.ShapeDtypeStruct((8, 512), F32)],
        in_specs=[_rows(tm, 2048), prev, nxt, _rows(tm, 512), dprev, dnxt, _full((3, 512))],
        out_specs=[_rows(tm, 2048), _full((8, 512))],
        compiler_params=_params(),
    )(pa, pa, pa, dya, dya, dya, w_conv)


def _attn_bwd(pq, pkv, pbz, dyb, sink, tabs):
    s = pq.shape[0]
    nb = s // ATTN_BLOCK
    bps = min(MAX_BLOCKS_IN_STEP, nb)

    def body(sink_ref, q_ref, z_ref, d_ref, kv_ref, t_ref,
             dq_ref, dz_ref, dkv_ref, gs_ref, kpad, vpad, dk_acc, dv_acc, bias, e_scr, ds_scr):
        i = pl.program_id(0)

        @pl.when(i == 0)
        def _():
            _fill_padded(kv_ref, kpad, vpad, s)
            _fill_band_bias(bias, nb)
            dk_acc[...] = jnp.zeros_like(dk_acc)
            dv_acc[...] = jnp.zeros_like(dv_acc)
            gs_ref[...] = jnp.zeros_like(gs_ref)

        row = lax.broadcasted_iota(jnp.int32, (ATTN_BLOCK, 128), 0)
        for b in range(bps):
            n = i * bps + b
            rows = slice(b * ATTN_BLOCK, (b + 1) * ATTN_BLOCK)
            start = pl.multiple_of(n * ATTN_BLOCK, ATTN_BLOCK)
            kw, vw = kpad[pl.ds(start, WINDOW_KEYS), :], vpad[pl.ds(start, WINDOW_KEYS), :]
            qf = q_ref[rows, :].astype(F32)
            variant = _bias_variant(n, nb)
            z = z_ref[rows, :].astype(F32)
            sig = _sigmoid(z)
            dyb_t = d_ref[rows, :].astype(F32)
            d_attn = dyb_t * (z * sig)
            outs, dqs = [], []
            dk_w = jnp.zeros((WINDOW_KEYS, 128), F32)
            dv_w = jnp.zeros((WINDOW_KEYS, 128), F32)
            for g in range(2):
                e_bg, ds_bg = e_scr.at[2 * b + g], ds_scr.at[2 * b + g]
                qt = _heads_to_lanes(qf, g, row)
                inv, p_sink = _softmax_keys_major(
                    _dot(kw, (qt * ATTN_SCALE).astype(BF16)), bias, variant, _sink_row(sink_ref, g), e_bg)
                ot = _dot_tn(vw, e_bg[...]) * inv
                outs.append(ot)
                dot_ = _heads_to_lanes(d_attn, g, row)
                delta = jnp.sum(dot_ * ot, axis=0, keepdims=True)
                dpt = _dot(vw, dot_.astype(BF16))
                for k in range(WINDOW_KEYS // KEY_CHUNK):
                    rw = slice(k * KEY_CHUNK, (k + 1) * KEY_CHUNK)
                    ds_bg[rw, :] = (e_bg[rw, :].astype(F32) * (dpt[rw] - delta)).astype(BF16)
                sink_part = p_sink * delta
                for j in range(4):
                    h = 4 * g + j
                    gs_ref[h:h + 1, :] -= jnp.sum(sink_part[:, 128 * j:128 * j + 128])
                dqs.append(_dot_tn(kw, ds_bg[...]) * (inv * ATTN_SCALE))
                dk_w += _dot_nt(ds_bg[...], (qt * inv).astype(BF16)) * ATTN_SCALE
                dv_w += _dot_nt(e_bg[...], (dot_ * inv).astype(BF16))
            dk_acc[pl.ds(start, WINDOW_KEYS), :] += dk_w
            dv_acc[pl.ds(start, WINDOW_KEYS), :] += dv_w
            attn = _lanes_to_heads(outs[0], outs[1], row)
            dz_ref[rows, :] = (dyb_t * attn * (sig * (1.0 + z * (1.0 - sig)))).astype(BF16)
            dq = _lanes_to_heads(dqs[0], dqs[1], row)
            trows = pl.ds(start, ATTN_BLOCK)
            cs, s1, s2 = t_ref[0, trows, :], t_ref[1, trows, :], t_ref[2, trows, :]
            for blk in range(4):
                cols = slice(128 * blk, 128 * blk + 128)
                dq_ref[rows, cols] = _rope_t(dq[:, cols], cs, s1, s2).astype(BF16)

        @pl.when(i == nb // bps - 1)
        def _():
            dk = dk_acc[ATTN_BLOCK:ATTN_BLOCK + s, :]
            dkv_ref[:, 0:128] = _rope_t(dk, t_ref[0], t_ref[1], t_ref[2]).astype(BF16)
            dkv_ref[:, 128:256] = dv_acc[ATTN_BLOCK:ATTN_BLOCK + s, :].astype(BF16)

    tq = bps * ATTN_BLOCK
    tile = _rows(tq, 512)
    return pl.pallas_call(
        body, name="attn_bwd", grid=(s // tq,),
        out_shape=[jax.ShapeDtypeStruct((s, 512), BF16), jax.ShapeDtypeStruct((s, 512), BF16),
                   jax.ShapeDtypeStruct((s, 256), BF16), jax.ShapeDtypeStruct((8, 128), F32)],
        in_specs=[pl.BlockSpec(memory_space=pltpu.SMEM), tile, tile, tile, _full((s, 256)), _full((3, s, 128))],
        out_specs=[tile, tile, _full((s, 256)), _full((8, 128))],
        scratch_shapes=[pltpu.VMEM((s + 2 * ATTN_BLOCK, 128), BF16)] * 2
        + [pltpu.VMEM((s + 2 * ATTN_BLOCK, 128), F32)] * 2
        + [pltpu.VMEM((3, WINDOW_KEYS, STACKED), F32)]
        + [pltpu.VMEM((2 * bps, WINDOW_KEYS, STACKED), BF16)] * 2,
        compiler_params=_params(),
    )(sink, pq, pbz, dyb, pkv, tabs)


def _mem_attn_bwd(pmq, pmz, mkv, dym):
    s = pmq.shape[0]
    m = mkv.shape[0]
    tm = min(512, s)

    def body(q_ref, z_ref, d_ref, mk_ref, mv_ref, dq_ref, dz_ref, dmkv_ref):
        @pl.when(pl.program_id(0) == 0)
        def _():
            dmkv_ref[...] = jnp.zeros_like(dmkv_ref)

        z = z_ref[...].astype(F32)
        sig = _sigmoid(z)
        dym_t = d_ref[...].astype(F32)
        d_attn = dym_t * (z * sig)
        dsilu = sig * (1.0 + z * (1.0 - sig))
        for h in range(MEM_HEADS):
            cols = slice(128 * h, 128 * h + 128)
            q, mk, mv = q_ref[:, cols], mk_ref[:, cols], mv_ref[:, cols]
            pt = _mem_softmax_t(q, mk)
            pb = pt.astype(BF16)
            o = _dot_tn(pb, mv)
            dob = d_attn[:, cols].astype(BF16)
            dpt = _dot_nt(mv, dob)
            dst = (pt * (dpt - jnp.sum(pt * dpt, axis=0, keepdims=True))).astype(BF16)
            dq_ref[:, cols] = (_dot_tn(dst, mk) * MEM_SCALE).astype(BF16)
            dz_ref[:, cols] = (dym_t[:, cols] * o * dsilu[:, cols]).astype(BF16)
            dmkv_ref[:, cols] += _dot(dst, q) * MEM_SCALE
            dmkv_ref[:, 512 + 128 * h:512 + 128 * h + 128] += _dot(pb, dob)

    return pl.pallas_call(
        body, name="mem_attn_bwd", grid=(s // tm,),
        out_shape=[jax.ShapeDtypeStruct((s, 512), BF16), jax.ShapeDtypeStruct((s, 512), BF16),
                   jax.ShapeDtypeStruct((m, D_MODEL), F32)],
        in_specs=[_rows(tm, 512), _rows(tm, 512), _rows(tm, 512), pl.BlockSpec((m, 512), lambda i: (0, 0)),
                  pl.BlockSpec((m, 512), lambda i: (0, 1))],
        out_specs=[_rows(tm, 512), _rows(tm, 512), _full((m, D_MODEL))],
        compiler_params=_params(),
    )(pmq, pmz, dym, mkv, mkv)


def _mem_kv_bwd(mem, g_mem, mn, dmkv, w_mkv):
    m = mem.shape[0]

    def body(mem_ref, g_ref, mn_ref, d_ref, w_ref, gw_ref, gg_ref):
        db = d_ref[...].astype(BF16)
        gw_ref[...] = _dot_tn(mn_ref[...], db).astype(BF16)
        d_mn = _dot_nt(db, w_ref[...])
        xf = mem_ref[...]
        r = lax.rsqrt(jnp.mean(xf * xf, axis=-1, keepdims=True) + EPS)
        gg_ref[...] = jnp.sum(d_mn * (xf * r), axis=0, keepdims=True)

    return pl.pallas_call(
        body, name="mem_kv_bwd", grid=(1,),
        out_shape=[jax.ShapeDtypeStruct((D_MODEL, D_MODEL), BF16), jax.ShapeDtypeStruct((1, D_MODEL), F32)],
        in_specs=[_full((m, D_MODEL)), _full((1, D_MODEL)), _full((m, D_MODEL)), _full((m, D_MODEL)),
                  _full((D_MODEL, D_MODEL))],
        out_specs=[_full((D_MODEL, D_MODEL)), _full((1, D_MODEL))],
        compiler_params=_params(),
    )(mem, g_mem, mn, dmkv, w_mkv)


def _dh_bwd(dparts, x, dy, g_pre, w_int):
    s = x.shape[0]
    tm = min(256, s)

    def body(*refs):
        d_refs = refs[:7]
        x_ref, dy_ref, g_ref, w_hbm, gx_ref, gg_ref, w_vm, sems = refs[7:]
        _load_once([(w_hbm, w_vm)], sems)

        @pl.when(pl.program_id(0) == 0)
        def _():
            gg_ref[...] = jnp.zeros_like(gg_ref)

        d_h = jnp.zeros((tm, D_MODEL), F32)
        for d_ref, (r0, width) in zip(d_refs, SEGS):
            for c0 in range(0, width, 512):
                cw = min(512, width - c0)
                d_h += _dot(d_ref[:, c0:c0 + cw], w_vm[r0 + c0:r0 + c0 + cw, :])
        xf = x_ref[...]
        r = lax.rsqrt(jnp.mean(xf * xf, axis=-1, keepdims=True) + EPS)
        xn = xf * r
        a = d_h * g_ref[...]
        gx_ref[...] = r * (a - xn * jnp.mean(a * xn, axis=-1, keepdims=True)) + dy_ref[...]
        gg_ref[...] += jnp.sum(d_h * xn, axis=0, keepdims=True)

    return pl.pallas_call(
        body, name="dh_bwd", grid=(s // tm,),
        out_shape=[jax.ShapeDtypeStruct((s, D_MODEL), F32), jax.ShapeDtypeStruct((1, D_MODEL), F32)],
        in_specs=[_rows(tm, w) for _, w in SEGS] + [_rows(tm, D_MODEL), _rows(tm, D_MODEL), _full((1, D_MODEL)), ANY],
        out_specs=[_rows(tm, D_MODEL), _full((1, D_MODEL))],
        scratch_shapes=[pltpu.VMEM((IN_WIDTH, D_MODEL), BF16), pltpu.SemaphoreType.DMA((1,))],
        compiler_params=_params(),
    )(*dparts, x, dy, g_pre, w_int)


def _gw_in(dparts, h):
    s = h.shape[0]
    tn = 256
    starts, counts = [], []
    for r0, width in SEGS:
        starts.append(r0 // tn)
        counts.append(width // tn)

    def body(*refs):
        d_refs = refs[:7]
        h_hbm, o_ref, h_vm, sems = refs[7:]
        _load_once([(h_hbm, h_vm)], sems)
        j = pl.program_id(0)
        for d_ref, st, cnt in zip(d_refs, starts, counts):
            @pl.when((j >= st) & (j < st + cnt))
            def _(d_ref=d_ref):
                o_ref[...] = _dot_tn(d_ref[...], h_vm[...]).astype(BF16)

    def seg_spec(st, cnt):
        return pl.BlockSpec((s, tn), lambda j: (0, jnp.clip(j - st, 0, cnt - 1)))

    return pl.pallas_call(
        body, name="gw_in", grid=(IN_WIDTH // tn,),
        out_shape=jax.ShapeDtypeStruct((IN_WIDTH, D_MODEL), BF16),
        in_specs=[seg_spec(st, cnt) for st, cnt in zip(starts, counts)] + [ANY],
        out_specs=pl.BlockSpec((tn, D_MODEL), lambda j: (j, 0)),
        scratch_shapes=[pltpu.VMEM((s, D_MODEL), BF16), pltpu.SemaphoreType.DMA((1,))],
        compiler_params=_params(),
    )(*dparts, h)


def _adamw_math(w, g, m, v):
    m2 = ADAM_B1 * m + (1.0 - ADAM_B1) * g
    v2 = ADAM_B2 * v + (1.0 - ADAM_B2) * (g * g)
    m_hat = m2 / (1.0 - ADAM_B1 ** ADAM_STEP)
    v_hat = v2 / (1.0 - ADAM_B2 ** ADAM_STEP)
    delta = -ADAM_LR * (m_hat / (jnp.sqrt(v_hat) + ADAM_EPS) + ADAM_WD * w)
    return delta, m2, v2


def _sum_adamw(own, land, chip, block, w, m, v, name, tiles=1):
    r, c = w.shape
    rt = r // tiles

    def body(c_ref, own_ref, l1_ref, l2_ref, l3_ref, w_ref, m_ref, v_ref, g_ref, d_ref, m2_ref, v2_ref):
        g = own_ref[...].astype(F32)
        for l_ref in (l1_ref, l2_ref, l3_ref):
            g += l_ref[...].astype(F32)
        g_ref[...] = g
        d_ref[...], m2_ref[...], v2_ref[...] = _adamw_math(w_ref[...], g, m_ref[...], v_ref[...])

    def share(k):
        return pl.BlockSpec((None, rt, c), lambda i, c_ref: (jnp.bitwise_xor(c_ref[0], k), block * tiles + i, 0))

    spec = pl.BlockSpec((rt, c), lambda i, c_ref: (i, 0))
    grid_spec = pltpu.PrefetchScalarGridSpec(
        num_scalar_prefetch=1, grid=(tiles,),
        in_specs=[share(0), share(1), share(2), share(3)] + [spec] * 3, out_specs=[spec] * 4)
    return pl.pallas_call(
        body, name=name, grid_spec=grid_spec,
        out_shape=[jax.ShapeDtypeStruct((r, c), F32)] * 4,
        compiler_params=_params(),
    )(chip, own, land, land, land, w, m, v)


def _sum_adamw_group(items, chip, name):
    k = len(items)

    def body(c_ref, *refs):
        shares, wmv, outs = refs[:4 * k], refs[4 * k:7 * k], refs[7 * k:]
        for j in range(k):
            g = shares[4 * j][...].astype(F32)
            for l_ref in shares[4 * j + 1:4 * j + 4]:
                g += l_ref[...].astype(F32)
            outs[4 * j][...] = g
            outs[4 * j + 1][...], outs[4 * j + 2][...], outs[4 * j + 3][...] = _adamw_math(
                wmv[3 * j][...], g, wmv[3 * j + 1][...], wmv[3 * j + 2][...])

    def share(shape, block, q):
        return pl.BlockSpec((None,) + shape, lambda i, c_ref: (jnp.bitwise_xor(c_ref[0], q), block, 0))

    in_specs, args = [], []
    for own, land, block, w, m, v in items:
        in_specs += [share(w.shape, block, q) for q in range(4)]
        args += [own, land, land, land]
    for own, land, block, w, m, v in items:
        in_specs += [pl.BlockSpec(w.shape, lambda i, c_ref: (0, 0))] * 3
        args += [w, m, v]
    out_specs = [pl.BlockSpec(w.shape, lambda i, c_ref: (0, 0)) for _, _, _, w, _, _ in items for _ in range(4)]
    res = pl.pallas_call(
        body, name=name,
        grid_spec=pltpu.PrefetchScalarGridSpec(num_scalar_prefetch=1, grid=(1,), in_specs=in_specs,
                                               out_specs=out_specs),
        out_shape=[jax.ShapeDtypeStruct(w.shape, F32) for _, _, _, w, _, _ in items for _ in range(4)],
        compiler_params=_params(),
    )(chip, *args)
    return [res[4 * j:4 * j + 4] for j in range(k)]


def _small_step(parts, ws, ms, vs):
    def exchange(gpre_ref, gconv_ref, gsink_ref, gmem_ref, gpost_ref, loss_ref, tot_ref,
                 pack, gathered, send_sems, recv_sems):
        x, y, c = _my_place()
        me_idx = 4 * x + 2 * y + c

        lane = lax.broadcasted_iota(jnp.int32, (1, 128), 1)
        sink_row = jnp.zeros((1, 128), F32)
        for h in range(8):
            sink_row = jnp.where(lane == h, gsink_ref[h:h + 1, :], sink_row)
        pack[...] = jnp.zeros_like(pack)
        pack[0:1, :] = gpre_ref[...]
        pack[1:2, :] = gmem_ref[...]
        pack[2:3, :] = gpost_ref[...]
        pack[3:6, 0:512] = gconv_ref[0:3, :]
        pack[6:7, 0:128] = sink_row
        pack[7:8, 0:128] = loss_ref[0:1, :]

        flips = [(0, 0, 1), (0, 1, 0), (1, 0, 0), (0, 1, 1), (1, 0, 1), (1, 1, 0), (1, 1, 1)]
        cps = []
        for k, (fx, fy, fc) in enumerate(flips):
            peer = ((1 - x) if fx else x, (1 - y) if fy else y, (1 - c) if fc else c)
            cps.append(pltpu.make_async_remote_copy(
                src_ref=pack, dst_ref=gathered.at[me_idx], send_sem=send_sems.at[k], recv_sem=recv_sems.at[k],
                device_id=peer, device_id_type=MESH))
        for cp in cps:
            cp.start()
        gathered[me_idx] = pack[...]
        for cp in cps:
            cp.wait_recv()
        for cp in cps:
            cp.wait_send()
        tot = gathered[0]
        for d in range(1, N_DEV):
            tot = tot + gathered[d]
        tot_ref[...] = tot

    tot = pl.pallas_call(
        exchange, name="small_exchange", grid=(1,),
        out_shape=jax.ShapeDtypeStruct((8, D_MODEL), F32),
        in_specs=[_full(p.shape) for p in parts], out_specs=_full((8, D_MODEL)),
        scratch_shapes=[pltpu.VMEM((8, D_MODEL), F32), pltpu.VMEM((N_DEV, 8, D_MODEL), F32),
                        pltpu.SemaphoreType.DMA((N_PEERS,)), pltpu.SemaphoreType.DMA((N_PEERS,))],
    )(*parts)

    def apply(tot_ref, *refs):
        w_refs, m_refs, v_refs = refs[0:5], refs[5:10], refs[10:15]
        loss_out = refs[15]
        g_outs, d_outs, m_outs, v_outs = refs[16:21], refs[21:26], refs[26:31], refs[31:36]
        x, y, c = _my_place()
        tot = tot_ref[...]
        conv = pltpu.roll(tot[:, 0:512], (512 - 64 * (4 * x + 2 * y + c)) % 512, 1)[3:6, 0:64]
        grads = (tot[0:1, :], conv, tot[6:7, 0:8], tot[1:2, :], tot[2:3, :])
        loss_out[...] = tot[7:8, 0:128]
        for j in range(5):
            g_outs[j][...] = grads[j]
            d_outs[j][...], m_outs[j][...], v_outs[j][...] = _adamw_math(
                w_refs[j][...], grads[j], m_refs[j][...], v_refs[j][...])

    specs = [_full(w.shape) for w in ws]
    res = pl.pallas_call(
        apply, name="small_apply", grid=(1,),
        out_shape=[jax.ShapeDtypeStruct((1, 128), F32)] + [jax.ShapeDtypeStruct(w.shape, F32) for w in ws] * 4,
        in_specs=[_full((8, D_MODEL))] + specs * 3,
        out_specs=[_full((1, 128))] + specs * 4,
    )(tot, *ws, *ms, *vs)
    return res[0], res[1:6], res[6:11], res[11:16], res[16:21]


def kernel(x, mem, g_pre, w_in, w_conv, attn_sink, g_mem, w_mem_kv, w_up_a, w_up_b, w_up_m, w_out, g_post, loss_target, m_g_pre, m_w_in, m_w_conv, m_attn_sink, m_g_mem, m_w_mem_kv, m_w_up_a, m_w_up_b, m_w_up_m, m_w_out, m_g_post, v_g_pre, v_w_in, v_w_conv, v_attn_sink, v_g_mem, v_w_mem_kv, v_w_up_a, v_w_up_b, v_w_up_m, v_w_out, v_g_post):
    s = x.shape[1]
    x2, mem2, tgt2 = x[0], mem[0], loss_target[0]
    me = 4 * lax.axis_index("x") + 2 * lax.axis_index("y") + lax.axis_index("c")

    w_conv_loc = jnp.zeros((8, 128), F32).at[:3, :64].set(w_conv[0])
    w_int_g, w_conv_g, tabs, w_mkv_loc, w_out_loc, w_up_loc = _all_gather(
        [w_in[0].T.astype(BF16), w_conv_loc], "gather_w_in",
        splits=[[(112 * k, 112) for k in range(7)] + [(784, 144)], [(0, 8)]],
        side=_gather_side(s, w_mem_kv[0], w_out[0], (w_up_a[0], w_up_b[0], w_up_m[0])))
    w_int = w_int_g.reshape(IN_WIDTH, D_MODEL)
    w_conv_f = w_conv_g[:, :3, :64].transpose(1, 0, 2).reshape(3, 512)
    late = _gather_start([w_mkv_loc, w_out_loc, w_up_loc], me, "gather_late_start")
    sink = attn_sink[0]

    h, pa, pq, pkv, pbz, pmq, pmz, pg = _proj_fwd(x2, g_pre + late[4][0:1, 0:1], w_int, tabs)
    ya = _conv_fwd(pa, w_conv_f)
    yb = _attn_fwd(pq, pkv, pbz, sink)
    w_mkv_g, w_out_g, w_up_g = _gather_wait(*late[:4], yb, "gather_late_wait")
    w_mkv = w_mkv_g.reshape(D_MODEL, D_MODEL)
    w_out_f = w_out_g.reshape(D_MODEL, D_MODEL)
    mn, mkv = _mem_kv_fwd(mem2, g_mem, w_mkv)
    ym = _mem_attn_fwd(pmq, pmz, mkv)
    dg, dya, dyb, dym, dy, loss_p, gg_post, mb, dob, du = _mid(ya, yb, ym, pg, x2, tgt2, g_post, w_up_g, w_out_f)
    gw_out, gw_up = _gw_mid(mb, dob, (ya, yb, ym), du)

    core = lax.axis_index("c").astype(jnp.int32).reshape(1)
    chip = (2 * lax.axis_index("x") + lax.axis_index("y")).astype(jnp.int32).reshape(1)

    def exchange_start(shares, tag):
        from_sibling = _sibling_exchange(shares, "grads_to_sibling_" + tag)
        chip_shares = _pair_add(shares, from_sibling, core, "grads_pair_add_" + tag)
        return _chip_exchange_start(chip_shares, "grads_to_chips_start_" + tag)

    dmq, dmz, dmkv = _mem_attn_bwd(pmq, pmz, mkv, dym)
    gw_mkv, gg_mem = _mem_kv_bwd(mem2, g_mem, mn, dmkv, w_mkv)
    shares1 = [gw_mkv.reshape(N_DEV, 128, D_MODEL), gw_out.reshape(N_DEV, 128, D_MODEL), gw_up]
    sib = _split_start(_sibling_copies, N_CHIPS, shares1,
                       [lax.empty((N_CHIPS,) + a.shape[1:], a.dtype) for a in shares1], "grads_to_sibling_small_start")
    da, gw_conv = _conv_bwd(pa, dya, w_conv_f + sib[4][0:1, 0:1])
    shares1, from_sibling = _split_wait(_sibling_copies, N_CHIPS, *sib[:4], da, "grads_to_sibling_small_wait")
    send1, recv1, srcs1, lands1, token1 = _chip_exchange_start(
        _pair_add(shares1, from_sibling, core, "grads_pair_add_small"), "grads_to_chips_start_small")
    dq, dbz, dkv, g_sink = _attn_bwd(pq, pkv, pbz, dyb, sink + token1[0, 0], tabs)
    dparts = (da, dq, dkv, dbz, dmq, dmz, dg)
    gw_int = _gw_in(dparts, h)
    send2, recv2, srcs2, lands2, token2 = exchange_start([gw_int.reshape(N_DEV, SHARD_IN, D_MODEL)], "w_in")
    grad_x, gg_pre = _dh_bwd(dparts, x2, dy, g_pre + token2[0:1, 0:1], w_int)
    (o_mkv, o_out, o_up, o_int), (l_mkv, l_out, l_up, l_int) = _chip_exchange_wait(
        send1 + send2, recv1 + recv2, srcs1 + srcs2, lands1 + lands2, grad_x, "grads_to_chips_wait")

    loss_row, small_g, sd, sm, sv = _small_step(
        (gg_pre, gw_conv, g_sink, gg_mem, gg_post, loss_p),
        [g_pre, w_conv[0], attn_sink, g_mem, g_post],
        [m_g_pre, m_w_conv[0], m_attn_sink, m_g_mem, m_g_post],
        [v_g_pre, v_w_conv[0], v_attn_sink, v_g_mem, v_g_post])
    loss = loss_row[0, 0]
    g_g_pre, g_conv, g_sink_tot, g_g_mem, g_g_post = small_g

    g_w_in, d_w_in, nm_w_in, nv_w_in = (t.T for t in _sum_adamw(
        o_int, l_int, chip, 0, w_in[0].T, m_w_in[0].T, v_w_in[0].T, "adamw_w_in", tiles=2))
    (g_mkv, d_mkv, nm_mkv, nv_mkv), (g_out, d_out, nm_out, nv_out), *up = _sum_adamw_group(
        [(o_mkv, l_mkv, 0, w_mem_kv[0], m_w_mem_kv[0], v_w_mem_kv[0]),
         (o_out, l_out, 0, w_out[0], m_w_out[0], v_w_out[0]),
         (o_up, l_up, 0, w_up_a[0], m_w_up_a[0], v_w_up_a[0]),
         (o_up, l_up, 1, w_up_b[0], m_w_up_b[0], v_w_up_b[0]),
         (o_up, l_up, 2, w_up_m[0], m_w_up_m[0], v_w_up_m[0])], chip, "adamw_mid_weights")

    def lead(a):
        return a[None]

    grads = [g_g_pre, lead(g_w_in), lead(g_conv), g_sink_tot, g_g_mem, lead(g_mkv), lead(up[0][0]),
             lead(up[1][0]), lead(up[2][0]), lead(g_out), g_g_post]

    def assemble(small, big_in, big_mkv, big_up, big_out):
        return [small[0], lead(big_in), lead(small[1]), small[2], small[3], lead(big_mkv), lead(big_up[0]),
                lead(big_up[1]), lead(big_up[2]), lead(big_out), small[4]]

    deltas = assemble(sd, d_w_in, d_mkv, [u[1] for u in up], d_out)
    new_m = assemble(sm, nm_w_in, nm_mkv, [u[2] for u in up], nm_out)
    new_v = assemble(sv, nv_w_in, nv_mkv, [u[3] for u in up], nv_out)
    return (loss, grad_x[None], *grads, *deltas, *new_m, *new_v)
```

```python
import functools

import jax
import jax.numpy as jnp
from jax import lax
from jax.experimental import pallas as pl
from jax.experimental.pallas import tpu as pltpu

F32 = jnp.float32
BF16 = jnp.bfloat16
MESH = pl.DeviceIdType.MESH

N_DEV = 8
D_MODEL = 1024
EPS = 1e-6
ROPE_THETA = 500000.0
ROT_DIM = 16
HEAD_DIM = 64
ATTN_BLOCK = 128
MEM_HEADS = 4
MEM_HEAD_DIM = 128
ATTN_SCALE = HEAD_DIM ** -0.5
MEM_SCALE = MEM_HEAD_DIM ** -0.5

ADAM_LR = 0.001
ADAM_B1 = 0.9
ADAM_B2 = 0.999
ADAM_EPS = 1e-08
ADAM_WD = 0.01
ADAM_STEP = 10

SEG_A = (0, 2048)
SEG_BQ = (2048, 512)
SEG_BKV = (2560, 256)
SEG_BZ = (2816, 512)
SEG_MQ = (3328, 512)
SEG_MZ = (3840, 512)
SEG_G = (4352, 3072)
SEGS = (SEG_A, SEG_BQ, SEG_BKV, SEG_BZ, SEG_MQ, SEG_MZ, SEG_G)
IN_WIDTH = 7424
SHARD_IN = IN_WIDTH // N_DEV

V7X_VMEM_BYTES = 64 * 1024 * 1024
CALL_VMEM_MB = 57
ANY = pl.BlockSpec(memory_space=pl.ANY)


def _params():
    assert CALL_VMEM_MB * 1024 * 1024 < V7X_VMEM_BYTES
    return pltpu.CompilerParams(dimension_semantics=("arbitrary",), vmem_limit_bytes=CALL_VMEM_MB * 1024 * 1024)


def _full(shape):
    zeros = (0,) * len(shape)
    return pl.BlockSpec(shape, lambda i: zeros)


def _rows(tm, width):
    return pl.BlockSpec((tm, width), lambda i: (i, 0))


def _dot(a, b):
    return jnp.dot(a, b, preferred_element_type=F32)


def _dot_nt(a, b):
    return lax.dot_general(a, b, (((1,), (1,)), ((), ())), preferred_element_type=F32)


def _dot_tn(a, b):
    return lax.dot_general(a, b, (((0,), (0,)), ((), ())), preferred_element_type=F32)


def _sigmoid(z):
    return 1.0 / (1.0 + jnp.exp(-z))


def _rope(t, cs, s1, s2):
    return t * cs + pltpu.roll(t, 120, 1) * s1 + pltpu.roll(t, 8, 1) * s2


def _rope_t(d, cs, s1, s2):
    return d * cs + pltpu.roll(d * s1, 8, 1) + pltpu.roll(d * s2, 120, 1)


def _gather_side(s, w_mkv, w_out, w_ups):
    half = ROT_DIM // 2
    inv_freq = jnp.power(jnp.float32(ROPE_THETA), -jnp.arange(half, dtype=F32) * (2.0 / ROT_DIM))
    freq_row = jnp.tile(jnp.concatenate([inv_freq, inv_freq, jnp.zeros((HEAD_DIM - ROT_DIM,), F32)]), 2)[None, :]

    def fn(in_refs, out_refs):
        f_ref, mkv_ref, out_ref, *up_refs = in_refs
        t_ref, mkv_bf, out_bf, up_bf = out_refs
        mkv_bf[...] = mkv_ref[...].astype(BF16)
        out_bf[...] = out_ref[...].astype(BF16)
        for k, up_ref in enumerate(up_refs):
            up_bf[512 * k:512 * k + 512, :] = up_ref[...].astype(BF16)
        pos = lax.broadcasted_iota(jnp.int32, (s, 128), 0).astype(F32)
        d = lax.broadcasted_iota(jnp.int32, (s, 128), 1) & (HEAD_DIM - 1)
        ang = pos * f_ref[...]
        cos, sin = jnp.cos(ang), jnp.sin(ang)
        lo, hi = d < half, (d >= half) & (d < ROT_DIM)
        t_ref[0] = jnp.where(lo | hi, cos, 1.0)
        t_ref[1] = jnp.where(lo, -sin, 0.0)
        t_ref[2] = jnp.where(hi, sin, 0.0)

    return ([freq_row, w_mkv, w_out, *w_ups],
            [jax.ShapeDtypeStruct((3, s, 128), F32), jax.ShapeDtypeStruct(w_mkv.shape, BF16),
             jax.ShapeDtypeStruct(w_out.shape, BF16), jax.ShapeDtypeStruct((1536, 128), BF16)], fn)


def _load_once(pairs, sems):
    @pl.when(pl.program_id(0) == 0)
    def _():
        cps = [pltpu.make_async_copy(src, dst, sems.at[k]) for k, (src, dst) in enumerate(pairs)]
        for cp in cps:
            cp.start()
        for cp in cps:
            cp.wait()


def _my_place():
    x, y, c = lax.axis_index("x"), lax.axis_index("y"), lax.axis_index("c")
    return x, y, c


def _all_gather(arrs, name, splits=None, side=None):
    n = len(arrs)
    if splits is None:
        splits = [[(0, a.shape[0])] for a in arrs]
    pieces = [(a, r0, rn) for a in range(n) for r0, rn in splits[a]]
    n_p = len(pieces)
    side_in, side_out, side_fn = side if side is not None else ((), (), None)
    m, q = len(side_in), len(side_out)

    def body(*refs):
        ins, outs = refs[:n], refs[n + m:2 * n + m]
        send_sems, recv_sems, local_sems = refs[2 * n + m + q:]
        x, y, c = _my_place()
        me, sibling = (x, y, c), (x, y, 1 - c)

        def route(core):
            first = (jnp.bitwise_xor(x, 1 - core), jnp.bitwise_xor(y, core), core)
            second = (jnp.bitwise_xor(x, core), jnp.bitwise_xor(y, 1 - core), core)
            return first, second, (1 - x, 1 - y, core)

        def idx(px, py, pc):
            return 4 * px + 2 * py + pc

        def copy(p, k, block, to, own=False):
            a, r0, rn = pieces[p]
            dst = outs[a].at[idx(*block), pl.ds(r0, rn)]
            return pltpu.make_async_remote_copy(
                src_ref=ins[a].at[pl.ds(r0, rn)] if own else dst, dst_ref=dst,
                send_sem=send_sems.at[p * 7 + k], recv_sem=recv_sems.at[p * 7 + k],
                device_id=to, device_id_type=MESH)

        nbr1, nbr2, diag = route(c)
        mine = [pltpu.make_async_copy(ins[a], outs[a].at[idx(*me)], local_sems.at[a]) for a in range(n)]
        for cp in mine:
            cp.start()
        sent = []
        for p in range(n_p):
            for k, to in enumerate((sibling, nbr1, nbr2)):
                sent.append(copy(p, k, me, to, own=True))
        for cp in sent:
            cp.start()
        if side_fn is not None:
            side_fn(refs[n:n + m], refs[2 * n + m:2 * n + m + q])
        for k_in, block, onward in ((1, nbr1, ((3, nbr2), (4, sibling))), (2, nbr2, ((5, sibling),)),
                                    (3, diag, ((6, sibling),))):
            for p in range(n_p):
                copy(p, k_in, block, me).wait_recv()
                for k_out, to in onward:
                    cp = copy(p, k_out, block, to)
                    cp.start()
                    sent.append(cp)
        s1, s2, sd = route(1 - c)
        for k_in, block in ((0, sibling), (4, s1), (5, s2), (6, sd)):
            for p in range(n_p):
                copy(p, k_in, block, me).wait_recv()
        for cp in sent:
            cp.wait_send()
        for cp in mine:
            cp.wait()

    return pl.pallas_call(
        body, name=name,
        out_shape=[jax.ShapeDtypeStruct((N_DEV,) + a.shape, a.dtype) for a in arrs] + list(side_out),
        in_specs=[ANY] * n + [pl.BlockSpec(memory_space=pltpu.VMEM)] * m,
        out_specs=[ANY] * n + [pl.BlockSpec(memory_space=pltpu.VMEM)] * q,
        scratch_shapes=[pltpu.SemaphoreType.DMA((7 * n_p,)), pltpu.SemaphoreType.DMA((7 * n_p,)),
                        pltpu.SemaphoreType.DMA((n,))],
        compiler_params=pltpu.CompilerParams(vmem_limit_bytes=32 * 1024 * 1024),
    )(*arrs, *side_in)


N_CHIPS = 4


def _sibling_exchange(arrs, name):
    n = len(arrs)

    def body(*refs):
        ins, outs = refs[:n], refs[n:2 * n]
        send_sems, recv_sems = refs[2 * n:]
        x, y, c = _my_place()
        sibling = (x, y, 1 - c)

        def copy(a, j):
            return pltpu.make_async_remote_copy(
                src_ref=ins[a].at[2 * j + (1 - c)], dst_ref=outs[a].at[j],
                send_sem=send_sems.at[a * N_CHIPS + j], recv_sem=recv_sems.at[a * N_CHIPS + j],
                device_id=sibling, device_id_type=MESH)

        cps = [copy(a, j) for j in range(N_CHIPS) for a in range(n)]
        for cp in cps:
            cp.start()
        for cp in cps:
            cp.wait_recv()
        for cp in cps:
            cp.wait_send()

    return pl.pallas_call(
        body, name=name,
        out_shape=[jax.ShapeDtypeStruct((N_CHIPS,) + a.shape[1:], a.dtype) for a in arrs],
        in_specs=[ANY] * n, out_specs=[ANY] * n,
        scratch_shapes=[pltpu.SemaphoreType.DMA((N_CHIPS * n,)), pltpu.SemaphoreType.DMA((N_CHIPS * n,))],
    )(*arrs)


def _sibling_copies(srcs, lands, send_sems, recv_sems):
    x, y, c = _my_place()
    cps = []
    for j in range(N_CHIPS):
        for a in range(len(srcs)):
            k = a * N_CHIPS + j
            cps.append(pltpu.make_async_remote_copy(
                src_ref=srcs[a].at[2 * j + (1 - c)], dst_ref=lands[a].at[j], send_sem=send_sems[k],
                recv_sem=recv_sems[k], device_id=(x, y, 1 - c), device_id_type=MESH))
    return cps


def _pair_add(mine, recv, core, name):
    n = len(mine)

    def body(c_ref, *refs):
        for a in range(n):
            refs[2 * n + a][...] = (refs[a][...].astype(F32) + refs[n + a][...].astype(F32)).astype(BF16)

    def blk(a):
        return (None,) + a.shape[1:]

    grid_spec = pltpu.PrefetchScalarGridSpec(
        num_scalar_prefetch=1, grid=(N_CHIPS,),
        in_specs=[pl.BlockSpec(blk(a), lambda j, c_ref: (2 * j + c_ref[0], 0, 0)) for a in mine]
        + [pl.BlockSpec(blk(a), lambda j, c_ref: (j, 0, 0)) for a in recv],
        out_specs=[pl.BlockSpec(blk(a), lambda j, c_ref: (j, 0, 0)) for a in recv])
    return pl.pallas_call(
        body, name=name, grid_spec=grid_spec,
        out_shape=[jax.ShapeDtypeStruct(a.shape, BF16) for a in recv],
        compiler_params=_params(),
    )(core, *mine, *recv)


HBM = pl.BlockSpec(memory_space=pltpu.HBM)
SEM = pl.BlockSpec(memory_space=pltpu.SEMAPHORE)
N_PEER_CHIPS = 3
TOKEN = (8, 128)


def _chip_copies(srcs, lands, send_sems, recv_sems):
    x, y, c = _my_place()
    my_chip = 2 * x + y
    peers = [(x, 1 - y), (1 - x, y), (1 - x, 1 - y)]
    cps = []
    for k, (px, py) in enumerate(peers):
        for a in range(len(srcs)):
            j = a * N_PEER_CHIPS + k
            cps.append(pltpu.make_async_remote_copy(
                src_ref=srcs[a].at[2 * px + py], dst_ref=lands[a].at[my_chip],
                send_sem=send_sems[j], recv_sem=recv_sems[j],
                device_id=(px, py, c), device_id_type=MESH))
    return cps


N_PEERS = N_DEV - 1


def _gather_copies(srcs, lands, send_sems, recv_sems):
    x, y, c = _my_place()
    me_idx = 4 * x + 2 * y + c
    flips = [(0, 0, 1), (0, 1, 0), (1, 0, 0), (0, 1, 1), (1, 0, 1), (1, 1, 0), (1, 1, 1)]
    cps = []
    for k, (fx, fy, fc) in enumerate(flips):
        peer = ((1 - x) if fx else x, (1 - y) if fy else y, (1 - c) if fc else c)
        for a in range(len(srcs)):
            j = a * N_PEERS + k
            cps.append(pltpu.make_async_remote_copy(
                src_ref=srcs[a], dst_ref=lands[a].at[me_idx], send_sem=send_sems[j], recv_sem=recv_sems[j],
                device_id=peer, device_id_type=MESH))
    return cps


def _split_start(copies, per_array, arrs, lands, name):
    arrs, lands = list(arrs), list(lands)
    n = len(arrs)
    k = n * per_array

    def body(*refs):
        srcs, land_refs = refs[:n], refs[n:2 * n]
        send_sems, recv_sems = refs[2 * n:2 * n + k], refs[2 * n + k:2 * n + 2 * k]
        token = refs[-1]
        for cp in copies(srcs, land_refs, send_sems, recv_sems):
            cp.start()
        token[...] = jnp.zeros_like(token)

    hbm_arrs = [pltpu.with_memory_space_constraint(a, pltpu.HBM) for a in arrs]
    lands = [pltpu.with_memory_space_constraint(a, pltpu.HBM) for a in lands]
    res = pl.pallas_call(
        body, name=name,
        out_shape=[pltpu.SemaphoreType.DMA(())] * (2 * k) + [pltpu.HBM(a.shape, a.dtype) for a in arrs + lands]
        + [jax.ShapeDtypeStruct(TOKEN, F32)],
        in_specs=[HBM] * (2 * n),
        out_specs=[SEM] * (2 * k) + [HBM] * (2 * n) + [pl.BlockSpec(memory_space=pltpu.VMEM)],
        input_output_aliases={a: 2 * k + a for a in range(2 * n)},
        compiler_params=pltpu.CompilerParams(has_side_effects=pltpu.SideEffectType.DATAFLOW_SIDE_EFFECTING),
    )(*hbm_arrs, *lands)
    return res[:k], res[k:2 * k], res[2 * k:2 * k + n], res[2 * k + n:2 * k + 2 * n], res[-1]


def _split_wait(copies, per_array, send_sems, recv_sems, srcs, lands, after, name):
    n = len(srcs)
    k = n * per_array

    def body(*refs):
        src_refs, land_refs = refs[:n], refs[n:2 * n]
        s_sems, r_sems = refs[2 * n:2 * n + k], refs[2 * n + k:2 * n + 2 * k]
        for cp in copies(src_refs, land_refs, s_sems, r_sems):
            cp.wait_send()
            cp.wait_recv()

    res = pl.pallas_call(
        body, name=name,
        out_shape=[pltpu.HBM(a.shape, a.dtype) for a in list(srcs) + list(lands)],
        in_specs=[HBM] * (2 * n) + [SEM] * (2 * k) + [ANY],
        out_specs=[HBM] * (2 * n),
        input_output_aliases={a: a for a in range(2 * n)},
        compiler_params=pltpu.CompilerParams(has_side_effects=pltpu.SideEffectType.DATAFLOW_SIDE_EFFECTING),
    )(*srcs, *lands, *send_sems, *recv_sems, after)
    return res[:n], res[n:]


def _chip_exchange_start(arrs, name):
    return _split_start(_chip_copies, N_PEER_CHIPS, arrs, [lax.empty(a.shape, a.dtype) for a in arrs], name)


def _chip_exchange_wait(send_sems, recv_sems, srcs, lands, after, name):
    return _split_wait(_chip_copies, N_PEER_CHIPS, send_sems, recv_sems, srcs, lands, after, name)


def _gather_start(arrs, me_idx, name):
    lands = [lax.dynamic_update_slice(lax.empty((N_DEV,) + a.shape, a.dtype), a[None], (me_idx, 0, 0)) for a in arrs]
    return _split_start(_gather_copies, N_PEERS, arrs, lands, name)


def _gather_wait(send_sems, recv_sems, srcs, lands, after, name):
    return _split_wait(_gather_copies, N_PEERS, send_sems, recv_sems, srcs, lands, after, name)[1]


def _proj_fwd(after, x, g_pre, w_int, tabs):
    s = x.shape[0]
    tm = min(512, s)

    def body(after_ref, x_ref, g_ref, t_ref, w_hbm,
             h_ref, pa_ref, pq_ref, pkv_ref, pbz_ref, pmq_ref, pmz_ref, pg_ref, w_vm, sems):
        _load_once([(w_hbm, w_vm)], sems)
        xf = x_ref[...]
        r = lax.rsqrt(jnp.mean(xf * xf, axis=-1, keepdims=True) + EPS)
        h = ((xf * r) * g_ref[...]).astype(BF16)
        h_ref[...] = h
        cs, s1, s2 = t_ref[0], t_ref[1], t_ref[2]

        def mm(seg, c0, width):
            return _dot_nt(h, w_vm[seg[0] + c0:seg[0] + c0 + width, :])

        for c0 in range(0, SEG_A[1], 512):
            pa_ref[:, c0:c0 + 512] = mm(SEG_A, c0, 512).astype(BF16)
        q = mm(SEG_BQ, 0, 512)
        for b in range(4):
            pq_ref[:, 128 * b:128 * b + 128] = _rope(q[:, 128 * b:128 * b + 128], cs, s1, s2).astype(BF16)
        kv = mm(SEG_BKV, 0, 256)
        pkv_ref[:, 0:128] = _rope(kv[:, 0:128], cs, s1, s2).astype(BF16)
        pkv_ref[:, 128:256] = kv[:, 128:256].astype(BF16)
        pbz_ref[...] = mm(SEG_BZ, 0, 512).astype(BF16)
        pmq_ref[...] = mm(SEG_MQ, 0, 512).astype(BF16)
        pmz_ref[...] = mm(SEG_MZ, 0, 512).astype(BF16)
        for c0 in range(0, SEG_G[1], 512):
            pg_ref[:, c0:c0 + 512] = mm(SEG_G, c0, 512).astype(BF16)

    widths = (D_MODEL, 2048, 512, 256, 512, 512, 512, 3072)
    return pl.pallas_call(
        body, name="proj_fwd", grid=(s // tm,),
        out_shape=[jax.ShapeDtypeStruct((s, w), BF16) for w in widths],
        in_specs=[_full(TOKEN), _rows(tm, D_MODEL), _full((1, D_MODEL)),
                  pl.BlockSpec((3, tm, 128), lambda i: (0, i, 0)), ANY],
        out_specs=[_rows(tm, w) for w in widths],
        scratch_shapes=[pltpu.VMEM((IN_WIDTH, D_MODEL), BF16), pltpu.SemaphoreType.DMA((1,))],
        compiler_params=_params(),
    )(after, x, g_pre, tabs, w_int)


def _mem_kv_fwd(mem, g_mem, w_mkv):
    m = mem.shape[0]

    def body(mem_ref, g_ref, w_ref, mn_ref, mkv_ref):
        xf = mem_ref[...]
        r = lax.rsqrt(jnp.mean(xf * xf, axis=-1, keepdims=True) + EPS)
        mn = ((xf * r) * g_ref[...]).astype(BF16)
        mn_ref[...] = mn
        mkv_ref[...] = _dot(mn, w_ref[...]).astype(BF16)

    return pl.pallas_call(
        body, name="mem_kv_fwd", grid=(1,),
        out_shape=[jax.ShapeDtypeStruct((m, D_MODEL), BF16)] * 2,
        in_specs=[_full((m, D_MODEL)), _full((1, D_MODEL)), _full((D_MODEL, D_MODEL))],
        out_specs=[_full((m, D_MODEL))] * 2,
        compiler_params=_params(),
    )(mem, g_mem, w_mkv)


def _halo_specs(s, tm, rows, width):
    nblk = s // rows
    prev = pl.BlockSpec((rows, width), lambda i: (jnp.maximum(i * (tm // rows) - 1, 0), 0))
    nxt = pl.BlockSpec((rows, width), lambda i: (jnp.minimum((i + 1) * (tm // rows), nblk - 1), 0))
    return prev, nxt


def _conv_common(pa, cu_prev, cu_next, w, tm):
    b, c, u, z = (pa[:, 512 * k:512 * k + 512] for k in range(4))
    cu = c * u
    row = lax.broadcasted_iota(jnp.int32, (tm, 512), 0)
    cu_m1 = jnp.where(row == 0, cu_prev, pltpu.roll(cu, 1, 0))
    cu_p1 = jnp.where(row == tm - 1, cu_next, pltpu.roll(cu, tm - 1, 0))
    y = cu_m1 * w[0:1] + cu * w[1:2] + cu_p1 * w[2:3]
    sig = _sigmoid(z)
    return b, c, u, z, cu, cu_m1, cu_p1, y, sig, row


def _conv_fwd(pa, w_conv):
    s = pa.shape[0]
    tm = min(512, s)
    nt = s // tm

    def body(pa_ref, pp_ref, pn_ref, w_ref, ya_ref):
        i = pl.program_id(0)
        prev_row = pp_ref[...].astype(F32)[15:16, :]
        next_row = pn_ref[...].astype(F32)[0:1, :]
        b, _, _, z, _, _, _, y, sig, _ = _conv_common(
            pa_ref[...].astype(F32),
            jnp.where(i == 0, 0.0, prev_row[:, 512:1024] * prev_row[:, 1024:1536]),
            jnp.where(i == nt - 1, 0.0, next_row[:, 512:1024] * next_row[:, 1024:1536]), w_ref[...], tm)
        ya_ref[...] = (b * y * (z * sig)).astype(BF16)

    prev, nxt = _halo_specs(s, tm, 16, 2048)
    return pl.pallas_call(
        body, name="conv_fwd", grid=(nt,),
        out_shape=jax.ShapeDtypeStruct((s, 512), BF16),
        in_specs=[_rows(tm, 2048), prev, nxt, _full((3, 512))],
        out_specs=_rows(tm, 512),
        compiler_params=_params(),
    )(pa, pa, pa, w_conv)


def _heads_to_lanes(a, g, row):
    low = row < HEAD_DIM
    parts = []
    for b in (2 * g, 2 * g + 1):
        t = jnp.transpose(a[:, 128 * b:128 * b + 128])
        swapped = pltpu.roll(t, HEAD_DIM, 0)
        if g == 0:
            parts += [jnp.where(low, t, 0.0), jnp.where(low, swapped, 0.0)]
        else:
            parts += [jnp.where(low, 0.0, swapped), jnp.where(low, 0.0, t)]
    return jnp.concatenate(parts, axis=1)


def _lanes_to_heads(t0, t1, row):
    low = row < HEAD_DIM
    blocks = []
    for b in range(4):
        g = b // 2
        tg = (t0, t1)[g]
        je = 2 * (b - 2 * g)
        even, odd = tg[:, 128 * je:128 * je + 128], tg[:, 128 * je + 128:128 * je + 256]
        if g == 0:
            t = jnp.where(low, even, pltpu.roll(odd, HEAD_DIM, 0))
        else:
            t = jnp.where(low, pltpu.roll(even, HEAD_DIM, 0), odd)
        blocks.append(jnp.transpose(t))
    return jnp.concatenate(blocks, axis=1)


WINDOW_KEYS = 3 * ATTN_BLOCK
STACKED = 4 * ATTN_BLOCK
KEY_CHUNK = 32
MAX_BLOCKS_IN_STEP = 8


def _fill_band_bias(bias, nb):
    assert nb >= 2
    c = lax.broadcasted_iota(jnp.int32, (WINDOW_KEYS, STACKED), 0)
    r = lax.broadcasted_iota(jnp.int32, (WINDOW_KEYS, STACKED), 1) & (ATTN_BLOCK - 1)
    band = (c >= r) & (c <= r + 2 * ATTN_BLOCK)
    for v, ok in enumerate((band, band & (c >= ATTN_BLOCK), band & (c < 2 * ATTN_BLOCK))):
        bias[v] = jnp.where(ok, 0.0, -jnp.inf)


def _bias_variant(n, nb):
    return jnp.where(n == 0, 1, jnp.where(n == nb - 1, 2, 0))


def _sink_row(sink_ref, g):
    return jnp.concatenate([jnp.full((1, ATTN_BLOCK), sink_ref[4 * g + j], F32) for j in range(4)], axis=1)


def _softmax_keys_major(sc, bias, variant, sink, e_scr):
    chunks = [pl.ds(k * KEY_CHUNK, KEY_CHUNK) for k in range(WINDOW_KEYS // KEY_CHUNK)]
    rows = [slice(k * KEY_CHUNK, (k + 1) * KEY_CHUNK) for k in range(WINDOW_KEYS // KEY_CHUNK)]
    m_run = jnp.full((KEY_CHUNK, STACKED), -jnp.inf, F32)
    for ck, rw in zip(chunks, rows):
        m_run = jnp.maximum(m_run, sc[rw] + bias[variant, ck, :])
    m = jnp.maximum(jnp.max(m_run, axis=0, keepdims=True), sink)
    l_run = jnp.zeros((KEY_CHUNK, STACKED), F32)
    for ck, rw in zip(chunks, rows):
        e = jnp.exp(sc[rw] + bias[variant, ck, :] - m)
        l_run += e
        e_scr[rw, :] = e.astype(BF16)
    es = jnp.exp(sink - m)
    inv = 1.0 / (jnp.sum(l_run, axis=0, keepdims=True) + es)
    return inv, es * inv


def _fill_padded(kv_ref, kpad, vpad, s):
    zero = jnp.zeros((ATTN_BLOCK, 128), BF16)
    kpad[0:ATTN_BLOCK, :] = zero
    vpad[0:ATTN_BLOCK, :] = zero
    kpad[ATTN_BLOCK + s:2 * ATTN_BLOCK + s, :] = zero
    vpad[ATTN_BLOCK + s:2 * ATTN_BLOCK + s, :] = zero
    kpad[ATTN_BLOCK:ATTN_BLOCK + s, :] = kv_ref[:, 0:128]
    vpad[ATTN_BLOCK:ATTN_BLOCK + s, :] = kv_ref[:, 128:256]


def _attn_fwd(pq, pkv, pbz, sink):
    s = pq.shape[0]
    nb = s // ATTN_BLOCK
    bps = min(MAX_BLOCKS_IN_STEP, nb)

    def body(sink_ref, q_ref, z_ref, kv_ref, yb_ref, kpad, vpad, bias, e_scr):
        i = pl.program_id(0)

        @pl.when(i == 0)
        def _():
            _fill_padded(kv_ref, kpad, vpad, s)
            _fill_band_bias(bias, nb)

        row = lax.broadcasted_iota(jnp.int32, (ATTN_BLOCK, 128), 0)
        for b in range(bps):
            n = i * bps + b
            rows = slice(b * ATTN_BLOCK, (b + 1) * ATTN_BLOCK)
            start = pl.multiple_of(n * ATTN_BLOCK, ATTN_BLOCK)
            kw, vw = kpad[pl.ds(start, WINDOW_KEYS), :], vpad[pl.ds(start, WINDOW_KEYS), :]
            qf = q_ref[rows, :].astype(F32)
            variant = _bias_variant(n, nb)
            outs = []
            for g in range(2):
                e_bg = e_scr.at[2 * b + g]
                qt = (_heads_to_lanes(qf, g, row) * ATTN_SCALE).astype(BF16)
                inv, _ = _softmax_keys_major(_dot(kw, qt), bias, variant, _sink_row(sink_ref, g), e_bg)
                outs.append(_dot_tn(vw, e_bg[...]) * inv)
            attn = _lanes_to_heads(outs[0], outs[1], row)
            z = z_ref[rows, :].astype(F32)
            yb_ref[rows, :] = (attn * (z * _sigmoid(z))).astype(BF16)

    tq = bps * ATTN_BLOCK
    return pl.pallas_call(
        body, name="attn_fwd", grid=(s // tq,),
        out_shape=jax.ShapeDtypeStruct((s, 512), BF16),
        in_specs=[pl.BlockSpec(memory_space=pltpu.SMEM), _rows(tq, 512), _rows(tq, 512), _full((s, 256))],
        out_specs=_rows(tq, 512),
        scratch_shapes=[pltpu.VMEM((s + 2 * ATTN_BLOCK, 128), BF16)] * 2
        + [pltpu.VMEM((3, WINDOW_KEYS, STACKED), F32),
           pltpu.VMEM((2 * bps, WINDOW_KEYS, STACKED), BF16)],
        compiler_params=_params(),
    )(sink, pq, pbz, pkv)


def _mem_softmax_t(q, mk):
    sc = _dot_nt(mk, q) * MEM_SCALE
    e = jnp.exp(sc - jnp.max(sc, axis=0, keepdims=True))
    return e * (1.0 / jnp.sum(e, axis=0, keepdims=True))


def _mem_attn_fwd(pmq, pmz, mkv):
    s = pmq.shape[0]
    m = mkv.shape[0]
    tm = min(512, s)

    def body(q_ref, z_ref, mk_ref, mv_ref, ym_ref):
        z = z_ref[...].astype(F32)
        sz = z * _sigmoid(z)
        for h in range(MEM_HEADS):
            cols = slice(128 * h, 128 * h + 128)
            pt = _mem_softmax_t(q_ref[:, cols], mk_ref[:, cols])
            o = _dot_tn(pt.astype(BF16), mv_ref[:, cols])
            ym_ref[:, cols] = (o * sz[:, cols]).astype(BF16)

    return pl.pallas_call(
        body, name="mem_attn_fwd", grid=(s // tm,),
        out_shape=jax.ShapeDtypeStruct((s, 512), BF16),
        in_specs=[_rows(tm, 512), _rows(tm, 512), pl.BlockSpec((m, 512), lambda i: (0, 0)),
                  pl.BlockSpec((m, 512), lambda i: (0, 1))],
        out_specs=_rows(tm, 512),
        compiler_params=_params(),
    )(pmq, pmz, mkv, mkv)


def _mid(ya, yb, ym, pg, x, target, g_post, w_up, w_out):
    s = x.shape[0]
    tm = min(256, s)
    nt = s // tm

    def body(ya_ref, yb_ref, ym_ref, pg_ref, x_ref, t_ref, gp_ref, wup_hbm, wout_hbm,
             dg_ref, dya_ref, dyb_ref, dym_ref, dy_ref, loss_ref, ggp_ref, mb_ref, dob_ref, du_ref,
             wup_vm, wout_vm, sems):
        i = pl.program_id(0)
        _load_once([(wup_hbm.at[d], wup_vm.at[:, pl.ds(128 * d, 128)]) for d in range(N_DEV)]
                   + [(wout_hbm, wout_vm)], sems)

        @pl.when(i == 0)
        def _():
            loss_ref[...] = jnp.zeros_like(loss_ref)
            ggp_ref[...] = jnp.zeros_like(ggp_ref)

        ys = (ya_ref[...], yb_ref[...], ym_ref[...])
        us = [_dot(ys[k], wup_vm[512 * k:512 * k + 512, :]) for k in range(3)]
        gates = [_sigmoid(pg_ref[:, 1024 * k:1024 * k + 1024].astype(F32)) for k in range(3)]
        merged = gates[0] * us[0] + gates[1] * us[1] + gates[2] * us[2]
        mb = merged.astype(BF16)
        mb_ref[...] = mb
        out = _dot(mb, wout_vm[...])
        r = lax.rsqrt(jnp.mean(out * out, axis=-1, keepdims=True) + EPS)
        on = out * r
        gp = gp_ref[...]
        err = (x_ref[...] + on * gp) - t_ref[...]
        loss_ref[...] += 0.5 * jnp.sum(err * err) * (1.0 / D_MODEL)
        dy = err * (1.0 / D_MODEL)
        dy_ref[...] = dy
        ggp_ref[...] += jnp.sum(dy * on, axis=0, keepdims=True)
        a = dy * gp
        d_out = r * (a - on * jnp.mean(a * on, axis=-1, keepdims=True))
        dob = d_out.astype(BF16)
        dob_ref[...] = dob
        d_merged = _dot_nt(dob, wout_vm[...])
        d_refs = (dya_ref, dyb_ref, dym_ref)
        for k in range(3):
            g = gates[k]
            du_f = d_merged * g
            dg_ref[:, 1024 * k:1024 * k + 1024] = (du_f * us[k] * (1.0 - g)).astype(BF16)
            du = du_f.astype(BF16)
            du_ref[k] = du
            d_refs[k][...] = _dot_nt(du, wup_vm[512 * k:512 * k + 512, :]).astype(BF16)

    return pl.pallas_call(
        body, name="mid", grid=(nt,),
        out_shape=[jax.ShapeDtypeStruct((s, 3072), BF16)] + [jax.ShapeDtypeStruct((s, 512), BF16)] * 3
        + [jax.ShapeDtypeStruct((s, D_MODEL), F32), jax.ShapeDtypeStruct((8, 128), F32),
           jax.ShapeDtypeStruct((1, D_MODEL), F32), jax.ShapeDtypeStruct((s, D_MODEL), BF16),
           jax.ShapeDtypeStruct((s, D_MODEL), BF16), jax.ShapeDtypeStruct((3, s, D_MODEL), BF16)],
        in_specs=[_rows(tm, 512)] * 3 + [_rows(tm, 3072), _rows(tm, D_MODEL), _rows(tm, D_MODEL),
                                         _full((1, D_MODEL)), ANY, ANY],
        out_specs=[_rows(tm, 3072)] + [_rows(tm, 512)] * 3
        + [_rows(tm, D_MODEL), _full((8, 128)), _full((1, D_MODEL)), _rows(tm, D_MODEL), _rows(tm, D_MODEL),
           pl.BlockSpec((3, tm, D_MODEL), lambda i: (0, i, 0))],
        scratch_shapes=[pltpu.VMEM((1536, D_MODEL), BF16), pltpu.VMEM((D_MODEL, D_MODEL), BF16),
                        pltpu.SemaphoreType.DMA((N_DEV + 1,))],
        compiler_params=_params(),
    )(ya, yb, ym, pg, x, target, g_post, w_up, w_out)


def _gw_mid(mb, dob, ys, du):
    s = mb.shape[0]
    tn = 256

    def out_body(mb_ref, dob_ref, o_ref):
        o_ref[...] = _dot_tn(mb_ref[...], dob_ref[...]).astype(BF16)

    gw_out = pl.pallas_call(
        out_body, name="gw_out", grid=(D_MODEL // tn,),
        out_shape=jax.ShapeDtypeStruct((D_MODEL, D_MODEL), BF16),
        in_specs=[pl.BlockSpec((s, tn), lambda j: (0, j)), _full((s, D_MODEL))],
        out_specs=pl.BlockSpec((tn, D_MODEL), lambda j: (j, 0)),
        compiler_params=_params(),
    )(mb, dob)

    per = 512 // tn

    def up_body(ya_ref, yb_ref, ym_ref, du_ref, o_ref):
        j = pl.program_id(0)
        for k, y_ref in enumerate((ya_ref, yb_ref, ym_ref)):
            @pl.when(j // per == k)
            def _(y_ref=y_ref):
                res = _dot_tn(y_ref[...], du_ref[...])
                for d in range(N_DEV):
                    o_ref[d] = res[:, 128 * d:128 * d + 128].astype(BF16)

    def y_spec(k):
        return pl.BlockSpec((s, tn), lambda j: (0, jnp.clip(j - per * k, 0, per - 1)))

    gw_up = pl.pallas_call(
        up_body, name="gw_up", grid=(3 * per,),
        out_shape=jax.ShapeDtypeStruct((N_DEV, 1536, 128), BF16),
        in_specs=[y_spec(0), y_spec(1), y_spec(2), pl.BlockSpec((None, s, D_MODEL), lambda j: (j // per, 0, 0))],
        out_specs=pl.BlockSpec((N_DEV, tn, 128), lambda j: (0, j, 0)),
        compiler_params=_params(),
    )(*ys, du)
    return gw_out, gw_up


def _conv_bwd(after, pa, dya, w_conv):
    s = pa.shape[0]
    tm = min(512, s)
    nt = s // tm

    def body(after_ref, pa_ref, pp_ref, pn_ref, d_ref, dp_ref, dn_ref, w_ref, da_ref, gw_ref):
        i = pl.program_id(0)
        first, last = i == 0, i == nt - 1

        @pl.when(first)
        def _():
            gw_ref[...] = jnp.zeros_like(gw_ref)

        w = w_ref[...]
        prev_row = pp_ref[...].astype(F32)[15:16, :]
        next_row = pn_ref[...].astype(F32)[0:1, :]
        b, c, u, z, cu, cu_m1, cu_p1, y, sig, row = _conv_common(
            pa_ref[...].astype(F32),
            jnp.where(first, 0.0, prev_row[:, 512:1024] * prev_row[:, 1024:1536]),
            jnp.where(last, 0.0, next_row[:, 512:1024] * next_row[:, 1024:1536]), w, tm)
        sz = z * sig
        dya_t = d_ref[...].astype(F32)
        d_y = dya_t * b * sz

        def halo_dy(p_row, d_row):
            zz = p_row[:, 1536:2048]
            return d_row * p_row[:, 0:512] * (zz * _sigmoid(zz))

        dy_prev = jnp.where(first, 0.0, halo_dy(prev_row, dp_ref[...].astype(F32)[15:16, :]))
        dy_next = jnp.where(last, 0.0, halo_dy(next_row, dn_ref[...].astype(F32)[0:1, :]))
        dy_m1 = jnp.where(row == 0, dy_prev, pltpu.roll(d_y, 1, 0))
        dy_p1 = jnp.where(row == tm - 1, dy_next, pltpu.roll(d_y, tm - 1, 0))
        d_cu = dy_p1 * w[0:1] + d_y * w[1:2] + dy_m1 * w[2:3]
        da_ref[:, 0:512] = (dya_t * y * sz).astype(BF16)
        da_ref[:, 512:1024] = (d_cu * u).astype(BF16)
        da_ref[:, 1024:1536] = (d_cu * c).astype(BF16)
        da_ref[:, 1536:2048] = (dya_t * b * y * (sig + sz * (1.0 - sig))).astype(BF16)
        gw_ref[0:1, :] += jnp.sum(d_y * cu_m1, axis=0, keepdims=True)
        gw_ref[1:2, :] += jnp.sum(d_y * cu, axis=0, keepdims=True)
        gw_ref[2:3, :] += jnp.sum(d_y * cu_p1, axis=0, keepdims=True)

    prev, nxt = _halo_specs(s, tm, 16, 2048)
    dprev, dnxt = _halo_specs(s, tm, 16, 512)
    return pl.pallas_call(
        body, name="conv_bwd", grid=(nt,),
        out_shape=[jax.ShapeDtypeStruct((s, 2048), BF16), jax.ShapeDtypeStruct((8, 512), F32)],
        in_specs=[_full(TOKEN), _rows(tm, 2048), prev, nxt, _rows(tm, 512), dprev, dnxt, _full((3, 512))],
        out_specs=[_rows(tm, 2048), _full((8, 512))],
        compiler_params=_params(),
    )(after, pa, pa, pa, dya, dya, dya, w_conv)


def _attn_bwd(after, pq, pkv, pbz, dyb, sink, tabs):
    s = pq.shape[0]
    nb = s // ATTN_BLOCK
    bps = min(MAX_BLOCKS_IN_STEP, nb)

    def body(sink_ref, after_ref, q_ref, z_ref, d_ref, kv_ref, t_ref,
             dq_ref, dz_ref, dkv_ref, gs_ref, kpad, vpad, dk_acc, dv_acc, bias, e_scr, ds_scr):
        i = pl.program_id(0)

        @pl.when(i == 0)
        def _():
            _fill_padded(kv_ref, kpad, vpad, s)
            _fill_band_bias(bias, nb)
            dk_acc[...] = jnp.zeros_like(dk_acc)
            dv_acc[...] = jnp.zeros_like(dv_acc)
            gs_ref[...] = jnp.zeros_like(gs_ref)

        row = lax.broadcasted_iota(jnp.int32, (ATTN_BLOCK, 128), 0)
        for b in range(bps):
            n = i * bps + b
            rows = slice(b * ATTN_BLOCK, (b + 1) * ATTN_BLOCK)
            start = pl.multiple_of(n * ATTN_BLOCK, ATTN_BLOCK)
            kw, vw = kpad[pl.ds(start, WINDOW_KEYS), :], vpad[pl.ds(start, WINDOW_KEYS), :]
            qf = q_ref[rows, :].astype(F32)
            variant = _bias_variant(n, nb)
            z = z_ref[rows, :].astype(F32)
            sig = _sigmoid(z)
            dyb_t = d_ref[rows, :].astype(F32)
            d_attn = dyb_t * (z * sig)
            outs, dqs = [], []
            dk_w = jnp.zeros((WINDOW_KEYS, 128), F32)
            dv_w = jnp.zeros((WINDOW_KEYS, 128), F32)
            for g in range(2):
                e_bg, ds_bg = e_scr.at[2 * b + g], ds_scr.at[2 * b + g]
                qt = _heads_to_lanes(qf, g, row)
                inv, p_sink = _softmax_keys_major(
                    _dot(kw, (qt * ATTN_SCALE).astype(BF16)), bias, variant, _sink_row(sink_ref, g), e_bg)
                ot = _dot_tn(vw, e_bg[...]) * inv
                outs.append(ot)
                dot_ = _heads_to_lanes(d_attn, g, row)
                delta = jnp.sum(dot_ * ot, axis=0, keepdims=True)
                dpt = _dot(vw, dot_.astype(BF16))
                for k in range(WINDOW_KEYS // KEY_CHUNK):
                    rw = slice(k * KEY_CHUNK, (k + 1) * KEY_CHUNK)
                    ds_bg[rw, :] = (e_bg[rw, :].astype(F32) * (dpt[rw] - delta)).astype(BF16)
                sink_part = p_sink * delta
                for j in range(4):
                    h = 4 * g + j
                    gs_ref[h:h + 1, :] -= jnp.sum(sink_part[:, 128 * j:128 * j + 128])
                dqs.append(_dot_tn(kw, ds_bg[...]) * (inv * ATTN_SCALE))
                dk_w += _dot_nt(ds_bg[...], (qt * inv).astype(BF16)) * ATTN_SCALE
                dv_w += _dot_nt(e_bg[...], (dot_ * inv).astype(BF16))
            dk_acc[pl.ds(start, WINDOW_KEYS), :] += dk_w
            dv_acc[pl.ds(start, WINDOW_KEYS), :] += dv_w
            attn = _lanes_to_heads(outs[0], outs[1], row)
            dz_ref[rows, :] = (dyb_t * attn * (sig * (1.0 + z * (1.0 - sig)))).astype(BF16)
            dq = _lanes_to_heads(dqs[0], dqs[1], row)
            trows = pl.ds(start, ATTN_BLOCK)
            cs, s1, s2 = t_ref[0, trows, :], t_ref[1, trows, :], t_ref[2, trows, :]
            for blk in range(4):
                cols = slice(128 * blk, 128 * blk + 128)
                dq_ref[rows, cols] = _rope_t(dq[:, cols], cs, s1, s2).astype(BF16)

        @pl.when(i == nb // bps - 1)
        def _():
            dk = dk_acc[ATTN_BLOCK:ATTN_BLOCK + s, :]
            dkv_ref[:, 0:128] = _rope_t(dk, t_ref[0], t_ref[1], t_ref[2]).astype(BF16)
            dkv_ref[:, 128:256] = dv_acc[ATTN_BLOCK:ATTN_BLOCK + s, :].astype(BF16)

    tq = bps * ATTN_BLOCK
    tile = _rows(tq, 512)
    return pl.pallas_call(
        body, name="attn_bwd", grid=(s // tq,),
        out_shape=[jax.ShapeDtypeStruct((s, 512), BF16), jax.ShapeDtypeStruct((s, 512), BF16),
                   jax.ShapeDtypeStruct((s, 256), BF16), jax.ShapeDtypeStruct((8, 128), F32)],
        in_specs=[pl.BlockSpec(memory_space=pltpu.SMEM), _full(TOKEN), tile, tile, tile, _full((s, 256)),
                  _full((3, s, 128))],
        out_specs=[tile, tile, _full((s, 256)), _full((8, 128))],
        scratch_shapes=[pltpu.VMEM((s + 2 * ATTN_BLOCK, 128), BF16)] * 2
        + [pltpu.VMEM((s + 2 * ATTN_BLOCK, 128), F32)] * 2
        + [pltpu.VMEM((3, WINDOW_KEYS, STACKED), F32)]
        + [pltpu.VMEM((2 * bps, WINDOW_KEYS, STACKED), BF16)] * 2,
        compiler_params=_params(),
    )(sink, after, pq, pbz, dyb, pkv, tabs)


def _mem_attn_bwd(pmq, pmz, mkv, dym):
    s = pmq.shape[0]
    m = mkv.shape[0]
    tm = min(512, s)

    def body(q_ref, z_ref, d_ref, mk_ref, mv_ref, dq_ref, dz_ref, dmkv_ref):
        @pl.when(pl.program_id(0) == 0)
        def _():
            dmkv_ref[...] = jnp.zeros_like(dmkv_ref)

        z = z_ref[...].astype(F32)
        sig = _sigmoid(z)
        dym_t = d_ref[...].astype(F32)
        d_attn = dym_t * (z * sig)
        dsilu = sig * (1.0 + z * (1.0 - sig))
        for h in range(MEM_HEADS):
            cols = slice(128 * h, 128 * h + 128)
            q, mk, mv = q_ref[:, cols], mk_ref[:, cols], mv_ref[:, cols]
            pt = _mem_softmax_t(q, mk)
            pb = pt.astype(BF16)
            o = _dot_tn(pb, mv)
            dob = d_attn[:, cols].astype(BF16)
            dpt = _dot_nt(mv, dob)
            dst = (pt * (dpt - jnp.sum(pt * dpt, axis=0, keepdims=True))).astype(BF16)
            dq_ref[:, cols] = (_dot_tn(dst, mk) * MEM_SCALE).astype(BF16)
            dz_ref[:, cols] = (dym_t[:, cols] * o * dsilu[:, cols]).astype(BF16)
            dmkv_ref[:, cols] += _dot(dst, q) * MEM_SCALE
            dmkv_ref[:, 512 + 128 * h:512 + 128 * h + 128] += _dot(pb, dob)

    return pl.pallas_call(
        body, name="mem_attn_bwd", grid=(s // tm,),
        out_shape=[jax.ShapeDtypeStruct((s, 512), BF16), jax.ShapeDtypeStruct((s, 512), BF16),
                   jax.ShapeDtypeStruct((m, D_MODEL), F32)],
        in_specs=[_rows(tm, 512), _rows(tm, 512), _rows(tm, 512), pl.BlockSpec((m, 512), lambda i: (0, 0)),
                  pl.BlockSpec((m, 512), lambda i: (0, 1))],
        out_specs=[_rows(tm, 512), _rows(tm, 512), _full((m, D_MODEL))],
        compiler_params=_params(),
    )(pmq, pmz, dym, mkv, mkv)


def _mem_kv_bwd(mem, g_mem, mn, dmkv, w_mkv):
    m = mem.shape[0]

    def body(mem_ref, g_ref, mn_ref, d_ref, w_ref, gw_ref, gg_ref):
        db = d_ref[...].astype(BF16)
        gw_ref[...] = _dot_tn(mn_ref[...], db).astype(BF16)
        d_mn = _dot_nt(db, w_ref[...])
        xf = mem_ref[...]
        r = lax.rsqrt(jnp.mean(xf * xf, axis=-1, keepdims=True) + EPS)
        gg_ref[...] = jnp.sum(d_mn * (xf * r), axis=0, keepdims=True)

    return pl.pallas_call(
        body, name="mem_kv_bwd", grid=(1,),
        out_shape=[jax.ShapeDtypeStruct((D_MODEL, D_MODEL), BF16), jax.ShapeDtypeStruct((1, D_MODEL), F32)],
        in_specs=[_full((m, D_MODEL)), _full((1, D_MODEL)), _full((m, D_MODEL)), _full((m, D_MODEL)),
                  _full((D_MODEL, D_MODEL))],
        out_specs=[_full((D_MODEL, D_MODEL)), _full((1, D_MODEL))],
        compiler_params=_params(),
    )(mem, g_mem, mn, dmkv, w_mkv)


def _dh_bwd(after, dparts, x, dy, g_pre, w_int):
    s = x.shape[0]
    tm = min(256, s)

    def body(after_ref, *refs):
        d_refs = refs[:7]
        x_ref, dy_ref, g_ref, w_hbm, gx_ref, gg_ref, w_vm, sems = refs[7:]
        _load_once([(w_hbm, w_vm)], sems)

        @pl.when(pl.program_id(0) == 0)
        def _():
            gg_ref[...] = jnp.zeros_like(gg_ref)

        d_h = jnp.zeros((tm, D_MODEL), F32)
        for d_ref, (r0, width) in zip(d_refs, SEGS):
            for c0 in range(0, width, 512):
                cw = min(512, width - c0)
                d_h += _dot(d_ref[:, c0:c0 + cw], w_vm[r0 + c0:r0 + c0 + cw, :])
        xf = x_ref[...]
        r = lax.rsqrt(jnp.mean(xf * xf, axis=-1, keepdims=True) + EPS)
        xn = xf * r
        a = d_h * g_ref[...]
        gx_ref[...] = r * (a - xn * jnp.mean(a * xn, axis=-1, keepdims=True)) + dy_ref[...]
        gg_ref[...] += jnp.sum(d_h * xn, axis=0, keepdims=True)

    return pl.pallas_call(
        body, name="dh_bwd", grid=(s // tm,),
        out_shape=[jax.ShapeDtypeStruct((s, D_MODEL), F32), jax.ShapeDtypeStruct((1, D_MODEL), F32)],
        in_specs=[_full(TOKEN)] + [_rows(tm, w) for _, w in SEGS]
        + [_rows(tm, D_MODEL), _rows(tm, D_MODEL), _full((1, D_MODEL)), ANY],
        out_specs=[_rows(tm, D_MODEL), _full((1, D_MODEL))],
        scratch_shapes=[pltpu.VMEM((IN_WIDTH, D_MODEL), BF16), pltpu.SemaphoreType.DMA((1,))],
        compiler_params=_params(),
    )(after, *dparts, x, dy, g_pre, w_int)


def _gw_in(dparts, h):
    s = h.shape[0]
    tn = 256
    starts, counts = [], []
    for r0, width in SEGS:
        starts.append(r0 // tn)
        counts.append(width // tn)

    def body(*refs):
        d_refs = refs[:7]
        h_hbm, o_ref, h_vm, sems = refs[7:]
        _load_once([(h_hbm, h_vm)], sems)
        j = pl.program_id(0)
        for d_ref, st, cnt in zip(d_refs, starts, counts):
            @pl.when((j >= st) & (j < st + cnt))
            def _(d_ref=d_ref):
                o_ref[...] = _dot_tn(d_ref[...], h_vm[...]).astype(BF16)

    def seg_spec(st, cnt):
        return pl.BlockSpec((s, tn), lambda j: (0, jnp.clip(j - st, 0, cnt - 1)))

    return pl.pallas_call(
        body, name="gw_in", grid=(IN_WIDTH // tn,),
        out_shape=jax.ShapeDtypeStruct((IN_WIDTH, D_MODEL), BF16),
        in_specs=[seg_spec(st, cnt) for st, cnt in zip(starts, counts)] + [ANY],
        out_specs=pl.BlockSpec((tn, D_MODEL), lambda j: (j, 0)),
        scratch_shapes=[pltpu.VMEM((s, D_MODEL), BF16), pltpu.SemaphoreType.DMA((1,))],
        compiler_params=_params(),
    )(*dparts, h)


def _adamw_math(w, g, m, v):
    m2 = ADAM_B1 * m + (1.0 - ADAM_B1) * g
    v2 = ADAM_B2 * v + (1.0 - ADAM_B2) * (g * g)
    m_hat = m2 / (1.0 - ADAM_B1 ** ADAM_STEP)
    v_hat = v2 / (1.0 - ADAM_B2 ** ADAM_STEP)
    delta = -ADAM_LR * (m_hat / (jnp.sqrt(v_hat) + ADAM_EPS) + ADAM_WD * w)
    return delta, m2, v2


def _sum_adamw(own, land, chip, block, w, m, v, name, tiles=1):
    r, c = w.shape
    rt = r // tiles

    def body(c_ref, own_ref, l1_ref, l2_ref, l3_ref, w_ref, m_ref, v_ref, g_ref, d_ref, m2_ref, v2_ref):
        g = own_ref[...].astype(F32)
        for l_ref in (l1_ref, l2_ref, l3_ref):
            g += l_ref[...].astype(F32)
        g_ref[...] = g
        d_ref[...], m2_ref[...], v2_ref[...] = _adamw_math(w_ref[...], g, m_ref[...], v_ref[...])

    def share(k):
        return pl.BlockSpec((None, rt, c), lambda i, c_ref: (jnp.bitwise_xor(c_ref[0], k), block * tiles + i, 0))

    spec = pl.BlockSpec((rt, c), lambda i, c_ref: (i, 0))
    grid_spec = pltpu.PrefetchScalarGridSpec(
        num_scalar_prefetch=1, grid=(tiles,),
        in_specs=[share(0), share(1), share(2), share(3)] + [spec] * 3, out_specs=[spec] * 4)
    return pl.pallas_call(
        body, name=name, grid_spec=grid_spec,
        out_shape=[jax.ShapeDtypeStruct((r, c), F32)] * 4,
        compiler_params=_params(),
    )(chip, own, land, land, land, w, m, v)


def _sum_adamw_group(items, chip, name):
    k = len(items)

    def body(c_ref, *refs):
        shares, wmv, outs = refs[:4 * k], refs[4 * k:7 * k], refs[7 * k:]
        for j in range(k):
            g = shares[4 * j][...].astype(F32)
            for l_ref in shares[4 * j + 1:4 * j + 4]:
                g += l_ref[...].astype(F32)
            outs[4 * j][...] = g
            outs[4 * j + 1][...], outs[4 * j + 2][...], outs[4 * j + 3][...] = _adamw_math(
                wmv[3 * j][...], g, wmv[3 * j + 1][...], wmv[3 * j + 2][...])

    def share(shape, block, q):
        return pl.BlockSpec((None,) + shape, lambda i, c_ref: (jnp.bitwise_xor(c_ref[0], q), block, 0))

    in_specs, args = [], []
    for own, land, block, w, m, v in items:
        in_specs += [share(w.shape, block, q) for q in range(4)]
        args += [own, land, land, land]
    for own, land, block, w, m, v in items:
        in_specs += [pl.BlockSpec(w.shape, lambda i, c_ref: (0, 0))] * 3
        args += [w, m, v]
    out_specs = [pl.BlockSpec(w.shape, lambda i, c_ref: (0, 0)) for _, _, _, w, _, _ in items for _ in range(4)]
    res = pl.pallas_call(
        body, name=name,
        grid_spec=pltpu.PrefetchScalarGridSpec(num_scalar_prefetch=1, grid=(1,), in_specs=in_specs,
                                               out_specs=out_specs),
        out_shape=[jax.ShapeDtypeStruct(w.shape, F32) for _, _, _, w, _, _ in items for _ in range(4)],
        compiler_params=_params(),
    )(chip, *args)
    return [res[4 * j:4 * j + 4] for j in range(k)]


def _small_step(parts, ws, ms, vs):
    def exchange(gpre_ref, gconv_ref, gsink_ref, gmem_ref, gpost_ref, loss_ref, tot_ref,
                 pack, gathered, send_sems, recv_sems):
        x, y, c = _my_place()
        me_idx = 4 * x + 2 * y + c

        lane = lax.broadcasted_iota(jnp.int32, (1, 128), 1)
        sink_row = jnp.zeros((1, 128), F32)
        for h in range(8):
            sink_row = jnp.where(lane == h, gsink_ref[h:h + 1, :], sink_row)
        pack[...] = jnp.zeros_like(pack)
        pack[0:1, :] = gpre_ref[...]
        pack[1:2, :] = gmem_ref[...]
        pack[2:3, :] = gpost_ref[...]
        pack[3:6, 0:512] = gconv_ref[0:3, :]
        pack[6:7, 0:128] = sink_row
        pack[7:8, 0:128] = loss_ref[0:1, :]

        flips = [(0, 0, 1), (0, 1, 0), (1, 0, 0), (0, 1, 1), (1, 0, 1), (1, 1, 0), (1, 1, 1)]
        cps = []
        for k, (fx, fy, fc) in enumerate(flips):
            peer = ((1 - x) if fx else x, (1 - y) if fy else y, (1 - c) if fc else c)
            cps.append(pltpu.make_async_remote_copy(
                src_ref=pack, dst_ref=gathered.at[me_idx], send_sem=send_sems.at[k], recv_sem=recv_sems.at[k],
                device_id=peer, device_id_type=MESH))
        for cp in cps:
            cp.start()
        gathered[me_idx] = pack[...]
        for cp in cps:
            cp.wait_recv()
        for cp in cps:
            cp.wait_send()
        tot = gathered[0]
        for d in range(1, N_DEV):
            tot = tot + gathered[d]
        tot_ref[...] = tot

    tot = pl.pallas_call(
        exchange, name="small_exchange", grid=(1,),
        out_shape=jax.ShapeDtypeStruct((8, D_MODEL), F32),
        in_specs=[_full(p.shape) for p in parts], out_specs=_full((8, D_MODEL)),
        scratch_shapes=[pltpu.VMEM((8, D_MODEL), F32), pltpu.VMEM((N_DEV, 8, D_MODEL), F32),
                        pltpu.SemaphoreType.DMA((N_PEERS,)), pltpu.SemaphoreType.DMA((N_PEERS,))],
    )(*parts)

    def apply(tot_ref, *refs):
        w_refs, m_refs, v_refs = refs[0:5], refs[5:10], refs[10:15]
        loss_out = refs[15]
        g_outs, d_outs, m_outs, v_outs = refs[16:21], refs[21:26], refs[26:31], refs[31:36]
        x, y, c = _my_place()
        tot = tot_ref[...]
        conv = pltpu.roll(tot[:, 0:512], (512 - 64 * (4 * x + 2 * y + c)) % 512, 1)[3:6, 0:64]
        grads = (tot[0:1, :], conv, tot[6:7, 0:8], tot[1:2, :], tot[2:3, :])
        loss_out[...] = tot[7:8, 0:128]
        for j in range(5):
            g_outs[j][...] = grads[j]
            d_outs[j][...], m_outs[j][...], v_outs[j][...] = _adamw_math(
                w_refs[j][...], grads[j], m_refs[j][...], v_refs[j][...])

    specs = [_full(w.shape) for w in ws]
    res = pl.pallas_call(
        apply, name="small_apply", grid=(1,),
        out_shape=[jax.ShapeDtypeStruct((1, 128), F32)] + [jax.ShapeDtypeStruct(w.shape, F32) for w in ws] * 4,
        in_specs=[_full((8, D_MODEL))] + specs * 3,
        out_specs=[_full((1, 128))] + specs * 4,
    )(tot, *ws, *ms, *vs)
    return res[0], res[1:6], res[6:11], res[11:16], res[16:21]


def kernel(x, mem, g_pre, w_in, w_conv, attn_sink, g_mem, w_mem_kv, w_up_a, w_up_b, w_up_m, w_out, g_post, loss_target, m_g_pre, m_w_in, m_w_conv, m_attn_sink, m_g_mem, m_w_mem_kv, m_w_up_a, m_w_up_b, m_w_up_m, m_w_out, m_g_post, v_g_pre, v_w_in, v_w_conv, v_attn_sink, v_g_mem, v_w_mem_kv, v_w_up_a, v_w_up_b, v_w_up_m, v_w_out, v_g_post):
    s = x.shape[1]
    x2, mem2, tgt2 = x[0], mem[0], loss_target[0]
    me = 4 * lax.axis_index("x") + 2 * lax.axis_index("y") + lax.axis_index("c")

    w_conv_loc = jnp.zeros((8, 128), F32).at[:3, :64].set(w_conv[0])
    w_int_g, w_conv_g, tabs, w_mkv_loc, w_out_loc, w_up_loc = _all_gather(
        [w_in[0].T.astype(BF16), w_conv_loc], "gather_w_in",
        splits=[[(112 * k, 112) for k in range(7)] + [(784, 144)], [(0, 8)]],
        side=_gather_side(s, w_mem_kv[0], w_out[0], (w_up_a[0], w_up_b[0], w_up_m[0])))
    w_int = w_int_g.reshape(IN_WIDTH, D_MODEL)
    w_conv_f = w_conv_g[:, :3, :64].transpose(1, 0, 2).reshape(3, 512)
    late = _gather_start([w_mkv_loc, w_out_loc, w_up_loc], me, "gather_late_start")
    sink = attn_sink[0]

    h, pa, pq, pkv, pbz, pmq, pmz, pg = _proj_fwd(late[4], x2, g_pre, w_int, tabs)
    ya = _conv_fwd(pa, w_conv_f)
    yb = _attn_fwd(pq, pkv, pbz, sink)
    w_mkv_g, w_out_g, w_up_g = _gather_wait(*late[:4], yb, "gather_late_wait")
    w_mkv = w_mkv_g.reshape(D_MODEL, D_MODEL)
    w_out_f = w_out_g.reshape(D_MODEL, D_MODEL)
    mn, mkv = _mem_kv_fwd(mem2, g_mem, w_mkv)
    ym = _mem_attn_fwd(pmq, pmz, mkv)
    dg, dya, dyb, dym, dy, loss_p, gg_post, mb, dob, du = _mid(ya, yb, ym, pg, x2, tgt2, g_post, w_up_g, w_out_f)
    gw_out, gw_up = _gw_mid(mb, dob, (ya, yb, ym), du)

    core = lax.axis_index("c").astype(jnp.int32).reshape(1)
    chip = (2 * lax.axis_index("x") + lax.axis_index("y")).astype(jnp.int32).reshape(1)

    def exchange_start(shares, tag):
        from_sibling = _sibling_exchange(shares, "grads_to_sibling_" + tag)
        chip_shares = _pair_add(shares, from_sibling, core, "grads_pair_add_" + tag)
        return _chip_exchange_start(chip_shares, "grads_to_chips_start_" + tag)

    dmq, dmz, dmkv = _mem_attn_bwd(pmq, pmz, mkv, dym)
    gw_mkv, gg_mem = _mem_kv_bwd(mem2, g_mem, mn, dmkv, w_mkv)
    shares1 = [gw_mkv.reshape(N_DEV, 128, D_MODEL), gw_out.reshape(N_DEV, 128, D_MODEL), gw_up]
    sib = _split_start(_sibling_copies, N_CHIPS, shares1,
                       [lax.empty((N_CHIPS,) + a.shape[1:], a.dtype) for a in shares1], "grads_to_sibling_small_start")
    da, gw_conv = _conv_bwd(sib[4], pa, dya, w_conv_f)
    shares1, from_sibling = _split_wait(_sibling_copies, N_CHIPS, *sib[:4], da, "grads_to_sibling_small_wait")
    send1, recv1, srcs1, lands1, token1 = _chip_exchange_start(
        _pair_add(shares1, from_sibling, core, "grads_pair_add_small"), "grads_to_chips_start_small")
    dq, dbz, dkv, g_sink = _attn_bwd(token1, pq, pkv, pbz, dyb, sink, tabs)
    dparts = (da, dq, dkv, dbz, dmq, dmz, dg)
    gw_int = _gw_in(dparts, h)
    send2, recv2, srcs2, lands2, token2 = exchange_start([gw_int.reshape(N_DEV, SHARD_IN, D_MODEL)], "w_in")
    grad_x, gg_pre = _dh_bwd(token2, dparts, x2, dy, g_pre, w_int)
    (o_mkv, o_out, o_up, o_int), (l_mkv, l_out, l_up, l_int) = _chip_exchange_wait(
        send1 + send2, recv1 + recv2, srcs1 + srcs2, lands1 + lands2, grad_x, "grads_to_chips_wait")

    loss_row, small_g, sd, sm, sv = _small_step(
        (gg_pre, gw_conv, g_sink, gg_mem, gg_post, loss_p),
        [g_pre, w_conv[0], attn_sink, g_mem, g_post],
        [m_g_pre, m_w_conv[0], m_attn_sink, m_g_mem, m_g_post],
        [v_g_pre, v_w_conv[0], v_attn_sink, v_g_mem, v_g_post])
    loss = loss_row[0, 0]
    g_g_pre, g_conv, g_sink_tot, g_g_mem, g_g_post = small_g

    g_w_in, d_w_in, nm_w_in, nv_w_in = (t.T for t in _sum_adamw(
        o_int, l_int, chip, 0, w_in[0].T, m_w_in[0].T, v_w_in[0].T, "adamw_w_in", tiles=2))
    (g_mkv, d_mkv, nm_mkv, nv_mkv), (g_out, d_out, nm_out, nv_out), *up = _sum_adamw_group(
        [(o_mkv, l_mkv, 0, w_mem_kv[0], m_w_mem_kv[0], v_w_mem_kv[0]),
         (o_out, l_out, 0, w_out[0], m_w_out[0], v_w_out[0]),
         (o_up, l_up, 0, w_up_a[0], m_w_up_a[0], v_w_up_a[0]),
         (o_up, l_up, 1, w_up_b[0], m_w_up_b[0], v_w_up_b[0]),
         (o_up, l_up, 2, w_up_m[0], m_w_up_m[0], v_w_up_m[0])], chip, "adamw_mid_weights")

    def lead(a):
        return a[None]

    grads = [g_g_pre, lead(g_w_in), lead(g_conv), g_sink_tot, g_g_mem, lead(g_mkv), lead(up[0][0]),
             lead(up[1][0]), lead(up[2][0]), lead(g_out), g_g_post]

    def assemble(small, big_in, big_mkv, big_up, big_out):
        return [small[0], lead(big_in), lead(small[1]), small[2], small[3], lead(big_mkv), lead(big_up[0]),
                lead(big_up[1]), lead(big_up[2]), lead(big_out), small[4]]

    deltas = assemble(sd, d_w_in, d_mkv, [u[1] for u in up], d_out)
    new_m = assemble(sm, nm_w_in, nm_mkv, [u[2] for u in up], nm_out)
    new_v = assemble(sv, nv_w_in, nv_mkv, [u[3] for u in up], nv_out)
    return (loss, grad_x[None], *grads, *deltas, *new_m, *new_v)
```

```python
import functools

import jax
import jax.numpy as jnp
from jax import lax
from jax.experimental import pallas as pl
from jax.experimental.pallas import tpu as pltpu

F32 = jnp.float32
BF16 = jnp.bfloat16
MESH = pl.DeviceIdType.MESH

N_DEV = 8
D_MODEL = 1024
EPS = 1e-6
ROPE_THETA = 500000.0
ROT_DIM = 16
HEAD_DIM = 64
ATTN_BLOCK = 128
MEM_HEADS = 4
MEM_HEAD_DIM = 128
ATTN_SCALE = HEAD_DIM ** -0.5
MEM_SCALE = MEM_HEAD_DIM ** -0.5

ADAM_LR = 0.001
ADAM_B1 = 0.9
ADAM_B2 = 0.999
ADAM_EPS = 1e-08
ADAM_WD = 0.01
ADAM_STEP = 10

SEG_A = (0, 2048)
SEG_BQ = (2048, 512)
SEG_BKV = (2560, 256)
SEG_BZ = (2816, 512)
SEG_MQ = (3328, 512)
SEG_MZ = (3840, 512)
SEG_G = (4352, 3072)
SEGS = (SEG_A, SEG_BQ, SEG_BKV, SEG_BZ, SEG_MQ, SEG_MZ, SEG_G)
IN_WIDTH = 7424
SHARD_IN = IN_WIDTH // N_DEV

V7X_VMEM_BYTES = 64 * 1024 * 1024
CALL_VMEM_MB = 57
ANY = pl.BlockSpec(memory_space=pl.ANY)


def _params():
    assert CALL_VMEM_MB * 1024 * 1024 < V7X_VMEM_BYTES
    return pltpu.CompilerParams(dimension_semantics=("arbitrary",), vmem_limit_bytes=CALL_VMEM_MB * 1024 * 1024)


def _full(shape):
    zeros = (0,) * len(shape)
    return pl.BlockSpec(shape, lambda i: zeros)


def _rows(tm, width):
    return pl.BlockSpec((tm, width), lambda i: (i, 0))


def _dot(a, b):
    return jnp.dot(a, b, preferred_element_type=F32)


def _dot_nt(a, b):
    return lax.dot_general(a, b, (((1,), (1,)), ((), ())), preferred_element_type=F32)


def _dot_tn(a, b):
    return lax.dot_general(a, b, (((0,), (0,)), ((), ())), preferred_element_type=F32)


def _sigmoid(z):
    return 1.0 / (1.0 + jnp.exp(-z))


def _rope(t, cs, s1, s2):
    return t * cs + pltpu.roll(t, 120, 1) * s1 + pltpu.roll(t, 8, 1) * s2


def _rope_t(d, cs, s1, s2):
    return d * cs + pltpu.roll(d * s1, 8, 1) + pltpu.roll(d * s2, 120, 1)


def _gather_side(s, w_mkv, w_out, w_ups):
    half = ROT_DIM // 2
    inv_freq = jnp.power(jnp.float32(ROPE_THETA), -jnp.arange(half, dtype=F32) * (2.0 / ROT_DIM))
    freq_row = jnp.tile(jnp.concatenate([inv_freq, inv_freq, jnp.zeros((HEAD_DIM - ROT_DIM,), F32)]), 2)[None, :]

    def fn(in_refs, out_refs):
        f_ref, mkv_ref, out_ref, *up_refs = in_refs
        t_ref, mkv_bf, out_bf, up_bf = out_refs
        mkv_bf[...] = mkv_ref[...].astype(BF16)
        out_bf[...] = out_ref[...].astype(BF16)
        for k, up_ref in enumerate(up_refs):
            up_bf[512 * k:512 * k + 512, :] = up_ref[...].astype(BF16)
        pos = lax.broadcasted_iota(jnp.int32, (s, 128), 0).astype(F32)
        d = lax.broadcasted_iota(jnp.int32, (s, 128), 1) & (HEAD_DIM - 1)
        ang = pos * f_ref[...]
        cos, sin = jnp.cos(ang), jnp.sin(ang)
        lo, hi = d < half, (d >= half) & (d < ROT_DIM)
        t_ref[0] = jnp.where(lo | hi, cos, 1.0)
        t_ref[1] = jnp.where(lo, -sin, 0.0)
        t_ref[2] = jnp.where(hi, sin, 0.0)

    return ([freq_row, w_mkv, w_out, *w_ups],
            [jax.ShapeDtypeStruct((3, s, 128), F32), jax.ShapeDtypeStruct(w_mkv.shape, BF16),
             jax.ShapeDtypeStruct(w_out.shape, BF16), jax.ShapeDtypeStruct((1536, 128), BF16)], fn)


def _load_once(pairs, sems):
    @pl.when(pl.program_id(0) == 0)
    def _():
        cps = [pltpu.make_async_copy(src, dst, sems.at[k]) for k, (src, dst) in enumerate(pairs)]
        for cp in cps:
            cp.start()
        for cp in cps:
            cp.wait()


def _my_place():
    x, y, c = lax.axis_index("x"), lax.axis_index("y"), lax.axis_index("c")
    return x, y, c


def _all_gather(arrs, name, splits=None, side=None):
    n = len(arrs)
    if splits is None:
        splits = [[(0, a.shape[0])] for a in arrs]
    pieces = [(a, r0, rn) for a in range(n) for r0, rn in splits[a]]
    n_p = len(pieces)
    side_in, side_out, side_fn = side if side is not None else ((), (), None)
    m, q = len(side_in), len(side_out)

    def body(*refs):
        ins, outs = refs[:n], refs[n + m:2 * n + m]
        send_sems, recv_sems, local_sems = refs[2 * n + m + q:]
        x, y, c = _my_place()
        me, sibling = (x, y, c), (x, y, 1 - c)

        def route(core):
            first = (jnp.bitwise_xor(x, 1 - core), jnp.bitwise_xor(y, core), core)
            second = (jnp.bitwise_xor(x, core), jnp.bitwise_xor(y, 1 - core), core)
            return first, second, (1 - x, 1 - y, core)

        def idx(px, py, pc):
            return 4 * px + 2 * py + pc

        def copy(p, k, block, to, own=False):
            a, r0, rn = pieces[p]
            dst = outs[a].at[idx(*block), pl.ds(r0, rn)]
            return pltpu.make_async_remote_copy(
                src_ref=ins[a].at[pl.ds(r0, rn)] if own else dst, dst_ref=dst,
                send_sem=send_sems.at[p * 7 + k], recv_sem=recv_sems.at[p * 7 + k],
                device_id=to, device_id_type=MESH)

        nbr1, nbr2, diag = route(c)
        mine = [pltpu.make_async_copy(ins[a], outs[a].at[idx(*me)], local_sems.at[a]) for a in range(n)]
        for cp in mine:
            cp.start()
        sent = []
        for p in range(n_p):
            for k, to in enumerate((sibling, nbr1, nbr2)):
                sent.append(copy(p, k, me, to, own=True))
        for cp in sent:
            cp.start()
        if side_fn is not None:
            side_fn(refs[n:n + m], refs[2 * n + m:2 * n + m + q])
        for k_in, block, onward in ((1, nbr1, ((3, nbr2), (4, sibling))), (2, nbr2, ((5, sibling),)),
                                    (3, diag, ((6, sibling),))):
            for p in range(n_p):
                copy(p, k_in, block, me).wait_recv()
                for k_out, to in onward:
                    cp = copy(p, k_out, block, to)
                    cp.start()
                    sent.append(cp)
        s1, s2, sd = route(1 - c)
        for k_in, block in ((0, sibling), (4, s1), (5, s2), (6, sd)):
            for p in range(n_p):
                copy(p, k_in, block, me).wait_recv()
        for cp in sent:
            cp.wait_send()
        for cp in mine:
            cp.wait()

    return pl.pallas_call(
        body, name=name,
        out_shape=[jax.ShapeDtypeStruct((N_DEV,) + a.shape, a.dtype) for a in arrs] + list(side_out),
        in_specs=[ANY] * n + [pl.BlockSpec(memory_space=pltpu.VMEM)] * m,
        out_specs=[ANY] * n + [pl.BlockSpec(memory_space=pltpu.VMEM)] * q,
        scratch_shapes=[pltpu.SemaphoreType.DMA((7 * n_p,)), pltpu.SemaphoreType.DMA((7 * n_p,)),
                        pltpu.SemaphoreType.DMA((n,))],
        compiler_params=pltpu.CompilerParams(vmem_limit_bytes=32 * 1024 * 1024),
    )(*arrs, *side_in)


N_CHIPS = 4


def _sibling_reduce(arr, name):
    _, r, c = arr.shape

    def body(in_ref, out_ref, land, a_buf, b_buf, o_buf, send_sems, recv_sems, local_sems):
        x, y, core = _my_place()
        cps = [pltpu.make_async_remote_copy(
            src_ref=in_ref.at[2 * j + (1 - core)], dst_ref=land.at[j], send_sem=send_sems.at[j],
            recv_sem=recv_sems.at[j], device_id=(x, y, 1 - core), device_id_type=MESH) for j in range(N_CHIPS)]
        for cp in cps:
            cp.start()
        store = None
        for j in range(N_CHIPS):
            mine = pltpu.make_async_copy(in_ref.at[2 * j + core], a_buf, local_sems.at[0])
            mine.start()
            cps[j].wait_recv()
            theirs = pltpu.make_async_copy(land.at[j], b_buf, local_sems.at[1])
            theirs.start()
            mine.wait()
            theirs.wait()
            if store is not None:
                store.wait()
            o_buf[...] = (a_buf[...].astype(F32) + b_buf[...].astype(F32)).astype(BF16)
            store = pltpu.make_async_copy(o_buf, out_ref.at[j], local_sems.at[2])
            store.start()
        store.wait()
        for cp in cps:
            cp.wait_send()

    return pl.pallas_call(
        body, name=name,
        out_shape=[jax.ShapeDtypeStruct((N_CHIPS, r, c), BF16)] * 2,
        in_specs=[ANY], out_specs=[ANY, ANY],
        scratch_shapes=[pltpu.VMEM((r, c), BF16)] * 3
        + [pltpu.SemaphoreType.DMA((N_CHIPS,)), pltpu.SemaphoreType.DMA((N_CHIPS,)), pltpu.SemaphoreType.DMA((3,))],
        compiler_params=pltpu.CompilerParams(vmem_limit_bytes=32 * 1024 * 1024),
    )(arr)[0]


def _sibling_copies(srcs, lands, send_sems, recv_sems):
    x, y, c = _my_place()
    cps = []
    for j in range(N_CHIPS):
        for a in range(len(srcs)):
            k = a * N_CHIPS + j
            cps.append(pltpu.make_async_remote_copy(
                src_ref=srcs[a].at[2 * j + (1 - c)], dst_ref=lands[a].at[j], send_sem=send_sems[k],
                recv_sem=recv_sems[k], device_id=(x, y, 1 - c), device_id_type=MESH))
    return cps


def _pair_add(mine, recv, core, name):
    n = len(mine)

    def body(c_ref, *refs):
        for a in range(n):
            refs[2 * n + a][...] = (refs[a][...].astype(F32) + refs[n + a][...].astype(F32)).astype(BF16)

    def blk(a):
        return (None,) + a.shape[1:]

    grid_spec = pltpu.PrefetchScalarGridSpec(
        num_scalar_prefetch=1, grid=(N_CHIPS,),
        in_specs=[pl.BlockSpec(blk(a), lambda j, c_ref: (2 * j + c_ref[0], 0, 0)) for a in mine]
        + [pl.BlockSpec(blk(a), lambda j, c_ref: (j, 0, 0)) for a in recv],
        out_specs=[pl.BlockSpec(blk(a), lambda j, c_ref: (j, 0, 0)) for a in recv])
    return pl.pallas_call(
        body, name=name, grid_spec=grid_spec,
        out_shape=[jax.ShapeDtypeStruct(a.shape, BF16) for a in recv],
        compiler_params=_params(),
    )(core, *mine, *recv)


HBM = pl.BlockSpec(memory_space=pltpu.HBM)
SEM = pl.BlockSpec(memory_space=pltpu.SEMAPHORE)
N_PEER_CHIPS = 3
TOKEN = (8, 128)


def _chip_copies(srcs, lands, send_sems, recv_sems):
    x, y, c = _my_place()
    my_chip = 2 * x + y
    peers = [(x, 1 - y), (1 - x, y), (1 - x, 1 - y)]
    cps = []
    for k, (px, py) in enumerate(peers):
        for a in range(len(srcs)):
            j = a * N_PEER_CHIPS + k
            cps.append(pltpu.make_async_remote_copy(
                src_ref=srcs[a].at[2 * px + py], dst_ref=lands[a].at[my_chip],
                send_sem=send_sems[j], recv_sem=recv_sems[j],
                device_id=(px, py, c), device_id_type=MESH))
    return cps


N_PEERS = N_DEV - 1


def _gather_copies(srcs, lands, send_sems, recv_sems):
    x, y, c = _my_place()
    me_idx = 4 * x + 2 * y + c
    flips = [(0, 0, 1), (0, 1, 0), (1, 0, 0), (0, 1, 1), (1, 0, 1), (1, 1, 0), (1, 1, 1)]
    cps = []
    for k, (fx, fy, fc) in enumerate(flips):
        peer = ((1 - x) if fx else x, (1 - y) if fy else y, (1 - c) if fc else c)
        for a in range(len(srcs)):
            j = a * N_PEERS + k
            cps.append(pltpu.make_async_remote_copy(
                src_ref=srcs[a], dst_ref=lands[a].at[me_idx], send_sem=send_sems[j], recv_sem=recv_sems[j],
                device_id=peer, device_id_type=MESH))
    return cps


def _split_start(copies, per_array, arrs, lands, name):
    arrs, lands = list(arrs), list(lands)
    n = len(arrs)
    k = n * per_array

    def body(*refs):
        srcs, land_refs = refs[:n], refs[n:2 * n]
        send_sems, recv_sems = refs[2 * n:2 * n + k], refs[2 * n + k:2 * n + 2 * k]
        token = refs[-1]
        for cp in copies(srcs, land_refs, send_sems, recv_sems):
            cp.start()
        token[...] = jnp.zeros_like(token)

    hbm_arrs = [pltpu.with_memory_space_constraint(a, pltpu.HBM) for a in arrs]
    lands = [pltpu.with_memory_space_constraint(a, pltpu.HBM) for a in lands]
    res = pl.pallas_call(
        body, name=name,
        out_shape=[pltpu.SemaphoreType.DMA(())] * (2 * k) + [pltpu.HBM(a.shape, a.dtype) for a in arrs + lands]
        + [jax.ShapeDtypeStruct(TOKEN, F32)],
        in_specs=[HBM] * (2 * n),
        out_specs=[SEM] * (2 * k) + [HBM] * (2 * n) + [pl.BlockSpec(memory_space=pltpu.VMEM)],
        input_output_aliases={a: 2 * k + a for a in range(2 * n)},
        compiler_params=pltpu.CompilerParams(has_side_effects=pltpu.SideEffectType.DATAFLOW_SIDE_EFFECTING),
    )(*hbm_arrs, *lands)
    return res[:k], res[k:2 * k], res[2 * k:2 * k + n], res[2 * k + n:2 * k + 2 * n], res[-1]


def _split_wait(copies, per_array, send_sems, recv_sems, srcs, lands, after, name):
    n = len(srcs)
    k = n * per_array

    def body(*refs):
        src_refs, land_refs = refs[:n], refs[n:2 * n]
        s_sems, r_sems = refs[2 * n:2 * n + k], refs[2 * n + k:2 * n + 2 * k]
        for cp in copies(src_refs, land_refs, s_sems, r_sems):
            cp.wait_send()
            cp.wait_recv()

    res = pl.pallas_call(
        body, name=name,
        out_shape=[pltpu.HBM(a.shape, a.dtype) for a in list(srcs) + list(lands)],
        in_specs=[HBM] * (2 * n) + [SEM] * (2 * k) + [ANY],
        out_specs=[HBM] * (2 * n),
        input_output_aliases={a: a for a in range(2 * n)},
        compiler_params=pltpu.CompilerParams(has_side_effects=pltpu.SideEffectType.DATAFLOW_SIDE_EFFECTING),
    )(*srcs, *lands, *send_sems, *recv_sems, after)
    return res[:n], res[n:]


def _chip_exchange_start(arrs, name):
    return _split_start(_chip_copies, N_PEER_CHIPS, arrs, [lax.empty(a.shape, a.dtype) for a in arrs], name)


def _chip_exchange_wait(send_sems, recv_sems, srcs, lands, after, name):
    return _split_wait(_chip_copies, N_PEER_CHIPS, send_sems, recv_sems, srcs, lands, after, name)


def _gather_start(arrs, me_idx, name):
    lands = [lax.dynamic_update_slice(lax.empty((N_DEV,) + a.shape, a.dtype), a[None], (me_idx, 0, 0)) for a in arrs]
    return _split_start(_gather_copies, N_PEERS, arrs, lands, name)


def _gather_wait(send_sems, recv_sems, srcs, lands, after, name):
    return _split_wait(_gather_copies, N_PEERS, send_sems, recv_sems, srcs, lands, after, name)[1]


def _proj_fwd(after, x, g_pre, w_int, tabs):
    s = x.shape[0]
    tm = min(512, s)

    def body(after_ref, x_ref, g_ref, t_ref, w_hbm,
             h_ref, pa_ref, pq_ref, pkv_ref, pbz_ref, pmq_ref, pmz_ref, pg_ref, w_vm, sems):
        _load_once([(w_hbm, w_vm)], sems)
        xf = x_ref[...]
        r = lax.rsqrt(jnp.mean(xf * xf, axis=-1, keepdims=True) + EPS)
        h = ((xf * r) * g_ref[...]).astype(BF16)
        h_ref[...] = h
        cs, s1, s2 = t_ref[0], t_ref[1], t_ref[2]

        def mm(seg, c0, width):
            return _dot_nt(h, w_vm[seg[0] + c0:seg[0] + c0 + width, :])

        for c0 in range(0, SEG_A[1], 512):
            pa_ref[:, c0:c0 + 512] = mm(SEG_A, c0, 512).astype(BF16)
        q = mm(SEG_BQ, 0, 512)
        for b in range(4):
            pq_ref[:, 128 * b:128 * b + 128] = _rope(q[:, 128 * b:128 * b + 128], cs, s1, s2).astype(BF16)
        kv = mm(SEG_BKV, 0, 256)
        pkv_ref[:, 0:128] = _rope(kv[:, 0:128], cs, s1, s2).astype(BF16)
        pkv_ref[:, 128:256] = kv[:, 128:256].astype(BF16)
        pbz_ref[...] = mm(SEG_BZ, 0, 512).astype(BF16)
        pmq_ref[...] = mm(SEG_MQ, 0, 512).astype(BF16)
        pmz_ref[...] = mm(SEG_MZ, 0, 512).astype(BF16)
        for c0 in range(0, SEG_G[1], 512):
            pg_ref[:, c0:c0 + 512] = mm(SEG_G, c0, 512).astype(BF16)

    widths = (D_MODEL, 2048, 512, 256, 512, 512, 512, 3072)
    return pl.pallas_call(
        body, name="proj_fwd", grid=(s // tm,),
        out_shape=[jax.ShapeDtypeStruct((s, w), BF16) for w in widths],
        in_specs=[_full(TOKEN), _rows(tm, D_MODEL), _full((1, D_MODEL)),
                  pl.BlockSpec((3, tm, 128), lambda i: (0, i, 0)), ANY],
        out_specs=[_rows(tm, w) for w in widths],
        scratch_shapes=[pltpu.VMEM((IN_WIDTH, D_MODEL), BF16), pltpu.SemaphoreType.DMA((1,))],
        compiler_params=_params(),
    )(after, x, g_pre, tabs, w_int)


def _mem_kv_fwd(mem, g_mem, w_mkv):
    m = mem.shape[0]

    def body(mem_ref, g_ref, w_ref, mn_ref, mkv_ref):
        xf = mem_ref[...]
        r = lax.rsqrt(jnp.mean(xf * xf, axis=-1, keepdims=True) + EPS)
        mn = ((xf * r) * g_ref[...]).astype(BF16)
        mn_ref[...] = mn
        mkv_ref[...] = _dot(mn, w_ref[...]).astype(BF16)

    return pl.pallas_call(
        body, name="mem_kv_fwd", grid=(1,),
        out_shape=[jax.ShapeDtypeStruct((m, D_MODEL), BF16)] * 2,
        in_specs=[_full((m, D_MODEL)), _full((1, D_MODEL)), _full((D_MODEL, D_MODEL))],
        out_specs=[_full((m, D_MODEL))] * 2,
        compiler_params=_params(),
    )(mem, g_mem, w_mkv)


def _halo_specs(s, tm, rows, width):
    nblk = s // rows
    prev = pl.BlockSpec((rows, width), lambda i: (jnp.maximum(i * (tm // rows) - 1, 0), 0))
    nxt = pl.BlockSpec((rows, width), lambda i: (jnp.minimum((i + 1) * (tm // rows), nblk - 1), 0))
    return prev, nxt


def _conv_common(pa, cu_prev, cu_next, w, tm):
    b, c, u, z = (pa[:, 512 * k:512 * k + 512] for k in range(4))
    cu = c * u
    row = lax.broadcasted_iota(jnp.int32, (tm, 512), 0)
    cu_m1 = jnp.where(row == 0, cu_prev, pltpu.roll(cu, 1, 0))
    cu_p1 = jnp.where(row == tm - 1, cu_next, pltpu.roll(cu, tm - 1, 0))
    y = cu_m1 * w[0:1] + cu * w[1:2] + cu_p1 * w[2:3]
    sig = _sigmoid(z)
    return b, c, u, z, cu, cu_m1, cu_p1, y, sig, row


def _conv_fwd(pa, w_conv):
    s = pa.shape[0]
    tm = min(512, s)
    nt = s // tm

    def body(pa_ref, pp_ref, pn_ref, w_ref, ya_ref):
        i = pl.program_id(0)
        prev_row = pp_ref[...].astype(F32)[15:16, :]
        next_row = pn_ref[...].astype(F32)[0:1, :]
        b, _, _, z, _, _, _, y, sig, _ = _conv_common(
            pa_ref[...].astype(F32),
            jnp.where(i == 0, 0.0, prev_row[:, 512:1024] * prev_row[:, 1024:1536]),
            jnp.where(i == nt - 1, 0.0, next_row[:, 512:1024] * next_row[:, 1024:1536]), w_ref[...], tm)
        ya_ref[...] = (b * y * (z * sig)).astype(BF16)

    prev, nxt = _halo_specs(s, tm, 16, 2048)
    return pl.pallas_call(
        body, name="conv_fwd", grid=(nt,),
        out_shape=jax.ShapeDtypeStruct((s, 512), BF16),
        in_specs=[_rows(tm, 2048), prev, nxt, _full((3, 512))],
        out_specs=_rows(tm, 512),
        compiler_params=_params(),
    )(pa, pa, pa, w_conv)


def _heads_to_lanes(a, g, row):
    low = row < HEAD_DIM
    parts = []
    for b in (2 * g, 2 * g + 1):
        t = jnp.transpose(a[:, 128 * b:128 * b + 128])
        swapped = pltpu.roll(t, HEAD_DIM, 0)
        if g == 0:
            parts += [jnp.where(low, t, 0.0), jnp.where(low, swapped, 0.0)]
        else:
            parts += [jnp.where(low, 0.0, swapped), jnp.where(low, 0.0, t)]
    return jnp.concatenate(parts, axis=1)


def _lanes_to_heads(t0, t1, row):
    low = row < HEAD_DIM
    blocks = []
    for b in range(4):
        g = b // 2
        tg = (t0, t1)[g]
        je = 2 * (b - 2 * g)
        even, odd = tg[:, 128 * je:128 * je + 128], tg[:, 128 * je + 128:128 * je + 256]
        if g == 0:
            t = jnp.where(low, even, pltpu.roll(odd, HEAD_DIM, 0))
        else:
            t = jnp.where(low, pltpu.roll(even, HEAD_DIM, 0), odd)
        blocks.append(jnp.transpose(t))
    return jnp.concatenate(blocks, axis=1)


WINDOW_KEYS = 3 * ATTN_BLOCK
STACKED = 4 * ATTN_BLOCK
KEY_CHUNK = 32
MAX_BLOCKS_IN_STEP = 8


def _fill_band_bias(bias, nb):
    assert nb >= 2
    c = lax.broadcasted_iota(jnp.int32, (WINDOW_KEYS, STACKED), 0)
    r = lax.broadcasted_iota(jnp.int32, (WINDOW_KEYS, STACKED), 1) & (ATTN_BLOCK - 1)
    band = (c >= r) & (c <= r + 2 * ATTN_BLOCK)
    for v, ok in enumerate((band, band & (c >= ATTN_BLOCK), band & (c < 2 * ATTN_BLOCK))):
        bias[v] = jnp.where(ok, 0.0, -jnp.inf)


def _bias_variant(n, nb):
    return jnp.where(n == 0, 1, jnp.where(n == nb - 1, 2, 0))


def _sink_row(sink_ref, g):
    return jnp.concatenate([jnp.full((1, ATTN_BLOCK), sink_ref[4 * g + j], F32) for j in range(4)], axis=1)


def _softmax_keys_major(sc, bias, variant, sink, e_scr):
    chunks = [pl.ds(k * KEY_CHUNK, KEY_CHUNK) for k in range(WINDOW_KEYS // KEY_CHUNK)]
    rows = [slice(k * KEY_CHUNK, (k + 1) * KEY_CHUNK) for k in range(WINDOW_KEYS // KEY_CHUNK)]
    m_run = jnp.full((KEY_CHUNK, STACKED), -jnp.inf, F32)
    for ck, rw in zip(chunks, rows):
        m_run = jnp.maximum(m_run, sc[rw] + bias[variant, ck, :])
    m = jnp.maximum(jnp.max(m_run, axis=0, keepdims=True), sink)
    l_run = jnp.zeros((KEY_CHUNK, STACKED), F32)
    for ck, rw in zip(chunks, rows):
        e = jnp.exp(sc[rw] + bias[variant, ck, :] - m)
        l_run += e
        e_scr[rw, :] = e.astype(BF16)
    es = jnp.exp(sink - m)
    inv = 1.0 / (jnp.sum(l_run, axis=0, keepdims=True) + es)
    return inv, es * inv


def _fill_padded(kv_ref, kpad, vpad, s):
    zero = jnp.zeros((ATTN_BLOCK, 128), BF16)
    kpad[0:ATTN_BLOCK, :] = zero
    vpad[0:ATTN_BLOCK, :] = zero
    kpad[ATTN_BLOCK + s:2 * ATTN_BLOCK + s, :] = zero
    vpad[ATTN_BLOCK + s:2 * ATTN_BLOCK + s, :] = zero
    kpad[ATTN_BLOCK:ATTN_BLOCK + s, :] = kv_ref[:, 0:128]
    vpad[ATTN_BLOCK:ATTN_BLOCK + s, :] = kv_ref[:, 128:256]


def _attn_fwd(pq, pkv, pbz, sink):
    s = pq.shape[0]
    nb = s // ATTN_BLOCK
    bps = min(MAX_BLOCKS_IN_STEP, nb)

    def body(sink_ref, q_ref, z_ref, kv_ref, yb_ref, kpad, vpad, bias, e_scr):
        i = pl.program_id(0)

        @pl.when(i == 0)
        def _():
            _fill_padded(kv_ref, kpad, vpad, s)
            _fill_band_bias(bias, nb)

        row = lax.broadcasted_iota(jnp.int32, (ATTN_BLOCK, 128), 0)
        for b in range(bps):
            n = i * bps + b
            rows = slice(b * ATTN_BLOCK, (b + 1) * ATTN_BLOCK)
            start = pl.multiple_of(n * ATTN_BLOCK, ATTN_BLOCK)
            kw, vw = kpad[pl.ds(start, WINDOW_KEYS), :], vpad[pl.ds(start, WINDOW_KEYS), :]
            qf = q_ref[rows, :].astype(F32)
            variant = _bias_variant(n, nb)
            outs = []
            for g in range(2):
                e_bg = e_scr.at[2 * b + g]
                qt = (_heads_to_lanes(qf, g, row) * ATTN_SCALE).astype(BF16)
                inv, _ = _softmax_keys_major(_dot(kw, qt), bias, variant, _sink_row(sink_ref, g), e_bg)
                outs.append(_dot_tn(vw, e_bg[...]) * inv)
            attn = _lanes_to_heads(outs[0], outs[1], row)
            z = z_ref[rows, :].astype(F32)
            yb_ref[rows, :] = (attn * (z * _sigmoid(z))).astype(BF16)

    tq = bps * ATTN_BLOCK
    return pl.pallas_call(
        body, name="attn_fwd", grid=(s // tq,),
        out_shape=jax.ShapeDtypeStruct((s, 512), BF16),
        in_specs=[pl.BlockSpec(memory_space=pltpu.SMEM), _rows(tq, 512), _rows(tq, 512), _full((s, 256))],
        out_specs=_rows(tq, 512),
        scratch_shapes=[pltpu.VMEM((s + 2 * ATTN_BLOCK, 128), BF16)] * 2
        + [pltpu.VMEM((3, WINDOW_KEYS, STACKED), F32),
           pltpu.VMEM((2 * bps, WINDOW_KEYS, STACKED), BF16)],
        compiler_params=_params(),
    )(sink, pq, pbz, pkv)


def _mem_softmax_t(q, mk):
    sc = _dot_nt(mk, q) * MEM_SCALE
    e = jnp.exp(sc - jnp.max(sc, axis=0, keepdims=True))
    return e * (1.0 / jnp.sum(e, axis=0, keepdims=True))


def _mem_attn_fwd(pmq, pmz, mkv):
    s = pmq.shape[0]
    m = mkv.shape[0]
    tm = min(512, s)

    def body(q_ref, z_ref, mk_ref, mv_ref, ym_ref):
        z = z_ref[...].astype(F32)
        sz = z * _sigmoid(z)
        for h in range(MEM_HEADS):
            cols = slice(128 * h, 128 * h + 128)
            pt = _mem_softmax_t(q_ref[:, cols], mk_ref[:, cols])
            o = _dot_tn(pt.astype(BF16), mv_ref[:, cols])
            ym_ref[:, cols] = (o * sz[:, cols]).astype(BF16)

    return pl.pallas_call(
        body, name="mem_attn_fwd", grid=(s // tm,),
        out_shape=jax.ShapeDtypeStruct((s, 512), BF16),
        in_specs=[_rows(tm, 512), _rows(tm, 512), pl.BlockSpec((m, 512), lambda i: (0, 0)),
                  pl.BlockSpec((m, 512), lambda i: (0, 1))],
        out_specs=_rows(tm, 512),
        compiler_params=_params(),
    )(pmq, pmz, mkv, mkv)


def _mid(ya, yb, ym, pg, x, target, g_post, w_up, w_out):
    s = x.shape[0]
    tm = min(256, s)
    nt = s // tm

    def body(ya_ref, yb_ref, ym_ref, pg_ref, x_ref, t_ref, gp_ref, wup_hbm, wout_hbm,
             dg_ref, dya_ref, dyb_ref, dym_ref, dy_ref, loss_ref, ggp_ref, mb_ref, dob_ref, du_ref,
             wup_vm, wout_vm, sems):
        i = pl.program_id(0)
        _load_once([(wup_hbm.at[d], wup_vm.at[:, pl.ds(128 * d, 128)]) for d in range(N_DEV)]
                   + [(wout_hbm, wout_vm)], sems)

        @pl.when(i == 0)
        def _():
            loss_ref[...] = jnp.zeros_like(loss_ref)
            ggp_ref[...] = jnp.zeros_like(ggp_ref)

        ys = (ya_ref[...], yb_ref[...], ym_ref[...])
        us = [_dot(ys[k], wup_vm[512 * k:512 * k + 512, :]) for k in range(3)]
        gates = [_sigmoid(pg_ref[:, 1024 * k:1024 * k + 1024].astype(F32)) for k in range(3)]
        merged = gates[0] * us[0] + gates[1] * us[1] + gates[2] * us[2]
        mb = merged.astype(BF16)
        mb_ref[...] = mb
        out = _dot(mb, wout_vm[...])
        r = lax.rsqrt(jnp.mean(out * out, axis=-1, keepdims=True) + EPS)
        on = out * r
        gp = gp_ref[...]
        err = (x_ref[...] + on * gp) - t_ref[...]
        loss_ref[...] += 0.5 * jnp.sum(err * err) * (1.0 / D_MODEL)
        dy = err * (1.0 / D_MODEL)
        dy_ref[...] = dy
        ggp_ref[...] += jnp.sum(dy * on, axis=0, keepdims=True)
        a = dy * gp
        d_out = r * (a - on * jnp.mean(a * on, axis=-1, keepdims=True))
        dob = d_out.astype(BF16)
        dob_ref[...] = dob
        d_merged = _dot_nt(dob, wout_vm[...])
        d_refs = (dya_ref, dyb_ref, dym_ref)
        for k in range(3):
            g = gates[k]
            du_f = d_merged * g
            dg_ref[:, 1024 * k:1024 * k + 1024] = (du_f * us[k] * (1.0 - g)).astype(BF16)
            du = du_f.astype(BF16)
            du_ref[k] = du
            d_refs[k][...] = _dot_nt(du, wup_vm[512 * k:512 * k + 512, :]).astype(BF16)

    return pl.pallas_call(
        body, name="mid", grid=(nt,),
        out_shape=[jax.ShapeDtypeStruct((s, 3072), BF16)] + [jax.ShapeDtypeStruct((s, 512), BF16)] * 3
        + [jax.ShapeDtypeStruct((s, D_MODEL), F32), jax.ShapeDtypeStruct((8, 128), F32),
           jax.ShapeDtypeStruct((1, D_MODEL), F32), jax.ShapeDtypeStruct((s, D_MODEL), BF16),
           jax.ShapeDtypeStruct((s, D_MODEL), BF16), jax.ShapeDtypeStruct((3, s, D_MODEL), BF16)],
        in_specs=[_rows(tm, 512)] * 3 + [_rows(tm, 3072), _rows(tm, D_MODEL), _rows(tm, D_MODEL),
                                         _full((1, D_MODEL)), ANY, ANY],
        out_specs=[_rows(tm, 3072)] + [_rows(tm, 512)] * 3
        + [_rows(tm, D_MODEL), _full((8, 128)), _full((1, D_MODEL)), _rows(tm, D_MODEL), _rows(tm, D_MODEL),
           pl.BlockSpec((3, tm, D_MODEL), lambda i: (0, i, 0))],
        scratch_shapes=[pltpu.VMEM((1536, D_MODEL), BF16), pltpu.VMEM((D_MODEL, D_MODEL), BF16),
                        pltpu.SemaphoreType.DMA((N_DEV + 1,))],
        compiler_params=_params(),
    )(ya, yb, ym, pg, x, target, g_post, w_up, w_out)


def _gw_mid(mb, dob, ys, du):
    s = mb.shape[0]
    tn = 256

    def out_body(mb_ref, dob_ref, o_ref):
        o_ref[...] = _dot_tn(mb_ref[...], dob_ref[...]).astype(BF16)

    gw_out = pl.pallas_call(
        out_body, name="gw_out", grid=(D_MODEL // tn,),
        out_shape=jax.ShapeDtypeStruct((D_MODEL, D_MODEL), BF16),
        in_specs=[pl.BlockSpec((s, tn), lambda j: (0, j)), _full((s, D_MODEL))],
        out_specs=pl.BlockSpec((tn, D_MODEL), lambda j: (j, 0)),
        compiler_params=_params(),
    )(mb, dob)

    per = 512 // tn

    def up_body(ya_ref, yb_ref, ym_ref, du_ref, o_ref):
        j = pl.program_id(0)
        for k, y_ref in enumerate((ya_ref, yb_ref, ym_ref)):
            @pl.when(j // per == k)
            def _(y_ref=y_ref):
                res = _dot_tn(y_ref[...], du_ref[...])
                for d in range(N_DEV):
                    o_ref[d] = res[:, 128 * d:128 * d + 128].astype(BF16)

    def y_spec(k):
        return pl.BlockSpec((s, tn), lambda j: (0, jnp.clip(j - per * k, 0, per - 1)))

    gw_up = pl.pallas_call(
        up_body, name="gw_up", grid=(3 * per,),
        out_shape=jax.ShapeDtypeStruct((N_DEV, 1536, 128), BF16),
        in_specs=[y_spec(0), y_spec(1), y_spec(2), pl.BlockSpec((None, s, D_MODEL), lambda j: (j // per, 0, 0))],
        out_specs=pl.BlockSpec((N_DEV, tn, 128), lambda j: (0, j, 0)),
        compiler_params=_params(),
    )(*ys, du)
    return gw_out, gw_up


def _conv_bwd(after, pa, dya, w_conv):
    s = pa.shape[0]
    tm = min(512, s)
    nt = s // tm

    def body(after_ref, pa_ref, pp_ref, pn_ref, d_ref, dp_ref, dn_ref, w_ref, da_ref, gw_ref):
        i = pl.program_id(0)
        first, last = i == 0, i == nt - 1

        @pl.when(first)
        def _():
            gw_ref[...] = jnp.zeros_like(gw_ref)

        w = w_ref[...]
        prev_row = pp_ref[...].astype(F32)[15:16, :]
        next_row = pn_ref[...].astype(F32)[0:1, :]
        b, c, u, z, cu, cu_m1, cu_p1, y, sig, row = _conv_common(
            pa_ref[...].astype(F32),
            jnp.where(first, 0.0, prev_row[:, 512:1024] * prev_row[:, 1024:1536]),
            jnp.where(last, 0.0, next_row[:, 512:1024] * next_row[:, 1024:1536]), w, tm)
        sz = z * sig
        dya_t = d_ref[...].astype(F32)
        d_y = dya_t * b * sz

        def halo_dy(p_row, d_row):
            zz = p_row[:, 1536:2048]
            return d_row * p_row[:, 0:512] * (zz * _sigmoid(zz))

        dy_prev = jnp.where(first, 0.0, halo_dy(prev_row, dp_ref[...].astype(F32)[15:16, :]))
        dy_next = jnp.where(last, 0.0, halo_dy(next_row, dn_ref[...].astype(F32)[0:1, :]))
        dy_m1 = jnp.where(row == 0, dy_prev, pltpu.roll(d_y, 1, 0))
        dy_p1 = jnp.where(row == tm - 1, dy_next, pltpu.roll(d_y, tm - 1, 0))
        d_cu = dy_p1 * w[0:1] + d_y * w[1:2] + dy_m1 * w[2:3]
        da_ref[:, 0:512] = (dya_t * y * sz).astype(BF16)
        da_ref[:, 512:1024] = (d_cu * u).astype(BF16)
        da_ref[:, 1024:1536] = (d_cu * c).astype(BF16)
        da_ref[:, 1536:2048] = (dya_t * b * y * (sig + sz * (1.0 - sig))).astype(BF16)
        gw_ref[0:1, :] += jnp.sum(d_y * cu_m1, axis=0, keepdims=True)
        gw_ref[1:2, :] += jnp.sum(d_y * cu, axis=0, keepdims=True)
        gw_ref[2:3, :] += jnp.sum(d_y * cu_p1, axis=0, keepdims=True)

    prev, nxt = _halo_specs(s, tm, 16, 2048)
    dprev, dnxt = _halo_specs(s, tm, 16, 512)
    return pl.pallas_call(
        body, name="conv_bwd", grid=(nt,),
        out_shape=[jax.ShapeDtypeStruct((s, 2048), BF16), jax.ShapeDtypeStruct((8, 512), F32)],
        in_specs=[_full(TOKEN), _rows(tm, 2048), prev, nxt, _rows(tm, 512), dprev, dnxt, _full((3, 512))],
        out_specs=[_rows(tm, 2048), _full((8, 512))],
        compiler_params=_params(),
    )(after, pa, pa, pa, dya, dya, dya, w_conv)


def _attn_bwd(after, pq, pkv, pbz, dyb, sink, tabs):
    s = pq.shape[0]
    nb = s // ATTN_BLOCK
    bps = min(MAX_BLOCKS_IN_STEP, nb)

    def body(sink_ref, after_ref, q_ref, z_ref, d_ref, kv_ref, t_ref,
             dq_ref, dz_ref, dkv_ref, gs_ref, kpad, vpad, dk_acc, dv_acc, bias, e_scr, ds_scr):
        i = pl.program_id(0)

        @pl.when(i == 0)
        def _():
            _fill_padded(kv_ref, kpad, vpad, s)
            _fill_band_bias(bias, nb)
            dk_acc[...] = jnp.zeros_like(dk_acc)
            dv_acc[...] = jnp.zeros_like(dv_acc)
            gs_ref[...] = jnp.zeros_like(gs_ref)

        row = lax.broadcasted_iota(jnp.int32, (ATTN_BLOCK, 128), 0)
        for b in range(bps):
            n = i * bps + b
            rows = slice(b * ATTN_BLOCK, (b + 1) * ATTN_BLOCK)
            start = pl.multiple_of(n * ATTN_BLOCK, ATTN_BLOCK)
            kw, vw = kpad[pl.ds(start, WINDOW_KEYS), :], vpad[pl.ds(start, WINDOW_KEYS), :]
            qf = q_ref[rows, :].astype(F32)
            variant = _bias_variant(n, nb)
            z = z_ref[rows, :].astype(F32)
            sig = _sigmoid(z)
            dyb_t = d_ref[rows, :].astype(F32)
            d_attn = dyb_t * (z * sig)
            outs, dqs = [], []
            dk_w = jnp.zeros((WINDOW_KEYS, 128), F32)
            dv_w = jnp.zeros((WINDOW_KEYS, 128), F32)
            for g in range(2):
                e_bg, ds_bg = e_scr.at[2 * b + g], ds_scr.at[2 * b + g]
                qt = _heads_to_lanes(qf, g, row)
                inv, p_sink = _softmax_keys_major(
                    _dot(kw, (qt * ATTN_SCALE).astype(BF16)), bias, variant, _sink_row(sink_ref, g), e_bg)
                ot = _dot_tn(vw, e_bg[...]) * inv
                outs.append(ot)
                dot_ = _heads_to_lanes(d_attn, g, row)
                delta = jnp.sum(dot_ * ot, axis=0, keepdims=True)
                dpt = _dot(vw, dot_.astype(BF16))
                for k in range(WINDOW_KEYS // KEY_CHUNK):
                    rw = slice(k * KEY_CHUNK, (k + 1) * KEY_CHUNK)
                    ds_bg[rw, :] = (e_bg[rw, :].astype(F32) * (dpt[rw] - delta)).astype(BF16)
                sink_part = p_sink * delta
                for j in range(4):
                    h = 4 * g + j
                    gs_ref[h:h + 1, :] -= jnp.sum(sink_part[:, 128 * j:128 * j + 128])
                dqs.append(_dot_tn(kw, ds_bg[...]) * (inv * ATTN_SCALE))
                dk_w += _dot_nt(ds_bg[...], (qt * inv).astype(BF16)) * ATTN_SCALE
                dv_w += _dot_nt(e_bg[...], (dot_ * inv).astype(BF16))
            dk_acc[pl.ds(start, WINDOW_KEYS), :] += dk_w
            dv_acc[pl.ds(start, WINDOW_KEYS), :] += dv_w
            attn = _lanes_to_heads(outs[0], outs[1], row)
            dz_ref[rows, :] = (dyb_t * attn * (sig * (1.0 + z * (1.0 - sig)))).astype(BF16)
            dq = _lanes_to_heads(dqs[0], dqs[1], row)
            trows = pl.ds(start, ATTN_BLOCK)
            cs, s1, s2 = t_ref[0, trows, :], t_ref[1, trows, :], t_ref[2, trows, :]
            for blk in range(4):
                cols = slice(128 * blk, 128 * blk + 128)
                dq_ref[rows, cols] = _rope_t(dq[:, cols], cs, s1, s2).astype(BF16)

        @pl.when(i == nb // bps - 1)
        def _():
            dk = dk_acc[ATTN_BLOCK:ATTN_BLOCK + s, :]
            dkv_ref[:, 0:128] = _rope_t(dk, t_ref[0], t_ref[1], t_ref[2]).astype(BF16)
            dkv_ref[:, 128:256] = dv_acc[ATTN_BLOCK:ATTN_BLOCK + s, :].astype(BF16)

    tq = bps * ATTN_BLOCK
    tile = _rows(tq, 512)
    return pl.pallas_call(
        body, name="attn_bwd", grid=(s // tq,),
        out_shape=[jax.ShapeDtypeStruct((s, 512), BF16), jax.ShapeDtypeStruct((s, 512), BF16),
                   jax.ShapeDtypeStruct((s, 256), BF16), jax.ShapeDtypeStruct((8, 128), F32)],
        in_specs=[pl.BlockSpec(memory_space=pltpu.SMEM), _full(TOKEN), tile, tile, tile, _full((s, 256)),
                  _full((3, s, 128))],
        out_specs=[tile, tile, _full((s, 256)), _full((8, 128))],
        scratch_shapes=[pltpu.VMEM((s + 2 * ATTN_BLOCK, 128), BF16)] * 2
        + [pltpu.VMEM((s + 2 * ATTN_BLOCK, 128), F32)] * 2
        + [pltpu.VMEM((3, WINDOW_KEYS, STACKED), F32)]
        + [pltpu.VMEM((2 * bps, WINDOW_KEYS, STACKED), BF16)] * 2,
        compiler_params=_params(),
    )(sink, after, pq, pbz, dyb, pkv, tabs)


def _mem_attn_bwd(pmq, pmz, mkv, dym):
    s = pmq.shape[0]
    m = mkv.shape[0]
    tm = min(512, s)

    def body(q_ref, z_ref, d_ref, mk_ref, mv_ref, dq_ref, dz_ref, dmkv_ref):
        @pl.when(pl.program_id(0) == 0)
        def _():
            dmkv_ref[...] = jnp.zeros_like(dmkv_ref)

        z = z_ref[...].astype(F32)
        sig = _sigmoid(z)
        dym_t = d_ref[...].astype(F32)
        d_attn = dym_t * (z * sig)
        dsilu = sig * (1.0 + z * (1.0 - sig))
        for h in range(MEM_HEADS):
            cols = slice(128 * h, 128 * h + 128)
            q, mk, mv = q_ref[:, cols], mk_ref[:, cols], mv_ref[:, cols]
            pt = _mem_softmax_t(q, mk)
            pb = pt.astype(BF16)
            o = _dot_tn(pb, mv)
            dob = d_attn[:, cols].astype(BF16)
            dpt = _dot_nt(mv, dob)
            dst = (pt * (dpt - jnp.sum(pt * dpt, axis=0, keepdims=True))).astype(BF16)
            dq_ref[:, cols] = (_dot_tn(dst, mk) * MEM_SCALE).astype(BF16)
            dz_ref[:, cols] = (dym_t[:, cols] * o * dsilu[:, cols]).astype(BF16)
            dmkv_ref[:, cols] += _dot(dst, q) * MEM_SCALE
            dmkv_ref[:, 512 + 128 * h:512 + 128 * h + 128] += _dot(pb, dob)

    return pl.pallas_call(
        body, name="mem_attn_bwd", grid=(s // tm,),
        out_shape=[jax.ShapeDtypeStruct((s, 512), BF16), jax.ShapeDtypeStruct((s, 512), BF16),
                   jax.ShapeDtypeStruct((m, D_MODEL), F32)],
        in_specs=[_rows(tm, 512), _rows(tm, 512), _rows(tm, 512), pl.BlockSpec((m, 512), lambda i: (0, 0)),
                  pl.BlockSpec((m, 512), lambda i: (0, 1))],
        out_specs=[_rows(tm, 512), _rows(tm, 512), _full((m, D_MODEL))],
        compiler_params=_params(),
    )(pmq, pmz, dym, mkv, mkv)


def _mem_kv_bwd(mem, g_mem, mn, dmkv, w_mkv):
    m = mem.shape[0]

    def body(mem_ref, g_ref, mn_ref, d_ref, w_ref, gw_ref, gg_ref):
        db = d_ref[...].astype(BF16)
        gw_ref[...] = _dot_tn(mn_ref[...], db).astype(BF16)
        d_mn = _dot_nt(db, w_ref[...])
        xf = mem_ref[...]
        r = lax.rsqrt(jnp.mean(xf * xf, axis=-1, keepdims=True) + EPS)
        gg_ref[...] = jnp.sum(d_mn * (xf * r), axis=0, keepdims=True)

    return pl.pallas_call(
        body, name="mem_kv_bwd", grid=(1,),
        out_shape=[jax.ShapeDtypeStruct((D_MODEL, D_MODEL), BF16), jax.ShapeDtypeStruct((1, D_MODEL), F32)],
        in_specs=[_full((m, D_MODEL)), _full((1, D_MODEL)), _full((m, D_MODEL)), _full((m, D_MODEL)),
                  _full((D_MODEL, D_MODEL))],
        out_specs=[_full((D_MODEL, D_MODEL)), _full((1, D_MODEL))],
        compiler_params=_params(),
    )(mem, g_mem, mn, dmkv, w_mkv)


def _dh_bwd(after, dparts, x, dy, g_pre, w_int):
    s = x.shape[0]
    tm = min(256, s)

    def body(after_ref, *refs):
        d_refs = refs[:7]
        x_ref, dy_ref, g_ref, w_hbm, gx_ref, gg_ref, w_vm, sems = refs[7:]
        _load_once([(w_hbm, w_vm)], sems)

        @pl.when(pl.program_id(0) == 0)
        def _():
            gg_ref[...] = jnp.zeros_like(gg_ref)

        d_h = jnp.zeros((tm, D_MODEL), F32)
        for d_ref, (r0, width) in zip(d_refs, SEGS):
            for c0 in range(0, width, 512):
                cw = min(512, width - c0)
                d_h += _dot(d_ref[:, c0:c0 + cw], w_vm[r0 + c0:r0 + c0 + cw, :])
        xf = x_ref[...]
        r = lax.rsqrt(jnp.mean(xf * xf, axis=-1, keepdims=True) + EPS)
        xn = xf * r
        a = d_h * g_ref[...]
        gx_ref[...] = r * (a - xn * jnp.mean(a * xn, axis=-1, keepdims=True)) + dy_ref[...]
        gg_ref[...] += jnp.sum(d_h * xn, axis=0, keepdims=True)

    return pl.pallas_call(
        body, name="dh_bwd", grid=(s // tm,),
        out_shape=[jax.ShapeDtypeStruct((s, D_MODEL), F32), jax.ShapeDtypeStruct((1, D_MODEL), F32)],
        in_specs=[_full(TOKEN)] + [_rows(tm, w) for _, w in SEGS]
        + [_rows(tm, D_MODEL), _rows(tm, D_MODEL), _full((1, D_MODEL)), ANY],
        out_specs=[_rows(tm, D_MODEL), _full((1, D_MODEL))],
        scratch_shapes=[pltpu.VMEM((IN_WIDTH, D_MODEL), BF16), pltpu.SemaphoreType.DMA((1,))],
        compiler_params=_params(),
    )(after, *dparts, x, dy, g_pre, w_int)


def _gw_in(dparts, h):
    s = h.shape[0]
    tn = 256
    starts, counts = [], []
    for r0, width in SEGS:
        starts.append(r0 // tn)
        counts.append(width // tn)

    def body(*refs):
        d_refs = refs[:7]
        h_hbm, o_ref, h_vm, sems = refs[7:]
        _load_once([(h_hbm, h_vm)], sems)
        j = pl.program_id(0)
        for d_ref, st, cnt in zip(d_refs, starts, counts):
            @pl.when((j >= st) & (j < st + cnt))
            def _(d_ref=d_ref):
                o_ref[...] = _dot_tn(d_ref[...], h_vm[...]).astype(BF16)

    def seg_spec(st, cnt):
        return pl.BlockSpec((s, tn), lambda j: (0, jnp.clip(j - st, 0, cnt - 1)))

    return pl.pallas_call(
        body, name="gw_in", grid=(IN_WIDTH // tn,),
        out_shape=jax.ShapeDtypeStruct((IN_WIDTH, D_MODEL), BF16),
        in_specs=[seg_spec(st, cnt) for st, cnt in zip(starts, counts)] + [ANY],
        out_specs=pl.BlockSpec((tn, D_MODEL), lambda j: (j, 0)),
        scratch_shapes=[pltpu.VMEM((s, D_MODEL), BF16), pltpu.SemaphoreType.DMA((1,))],
        compiler_params=_params(),
    )(*dparts, h)


def _adamw_math(w, g, m, v):
    m2 = ADAM_B1 * m + (1.0 - ADAM_B1) * g
    v2 = ADAM_B2 * v + (1.0 - ADAM_B2) * (g * g)
    m_hat = m2 / (1.0 - ADAM_B1 ** ADAM_STEP)
    v_hat = v2 / (1.0 - ADAM_B2 ** ADAM_STEP)
    delta = -ADAM_LR * (m_hat / (jnp.sqrt(v_hat) + ADAM_EPS) + ADAM_WD * w)
    return delta, m2, v2


def _sum_adamw(own, land, chip, block, w, m, v, name, tiles=1):
    r, c = w.shape
    rt = r // tiles

    def body(c_ref, own_ref, l1_ref, l2_ref, l3_ref, w_ref, m_ref, v_ref, g_ref, d_ref, m2_ref, v2_ref):
        g = own_ref[...].astype(F32)
        for l_ref in (l1_ref, l2_ref, l3_ref):
            g += l_ref[...].astype(F32)
        g_ref[...] = g
        d_ref[...], m2_ref[...], v2_ref[...] = _adamw_math(w_ref[...], g, m_ref[...], v_ref[...])

    def share(k):
        return pl.BlockSpec((None, rt, c), lambda i, c_ref: (jnp.bitwise_xor(c_ref[0], k), block * tiles + i, 0))

    spec = pl.BlockSpec((rt, c), lambda i, c_ref: (i, 0))
    grid_spec = pltpu.PrefetchScalarGridSpec(
        num_scalar_prefetch=1, grid=(tiles,),
        in_specs=[share(0), share(1), share(2), share(3)] + [spec] * 3, out_specs=[spec] * 4)
    return pl.pallas_call(
        body, name=name, grid_spec=grid_spec,
        out_shape=[jax.ShapeDtypeStruct((r, c), F32)] * 4,
        compiler_params=_params(),
    )(chip, own, land, land, land, w, m, v)


def _sum_adamw_group(items, chip, name):
    k = len(items)

    def body(c_ref, *refs):
        shares, wmv, outs = refs[:4 * k], refs[4 * k:7 * k], refs[7 * k:]
        for j in range(k):
            g = shares[4 * j][...].astype(F32)
            for l_ref in shares[4 * j + 1:4 * j + 4]:
                g += l_ref[...].astype(F32)
            outs[4 * j][...] = g
            outs[4 * j + 1][...], outs[4 * j + 2][...], outs[4 * j + 3][...] = _adamw_math(
                wmv[3 * j][...], g, wmv[3 * j + 1][...], wmv[3 * j + 2][...])

    def share(shape, block, q):
        return pl.BlockSpec((None,) + shape, lambda i, c_ref: (jnp.bitwise_xor(c_ref[0], q), block, 0))

    in_specs, args = [], []
    for own, land, block, w, m, v in items:
        in_specs += [share(w.shape, block, q) for q in range(4)]
        args += [own, land, land, land]
    for own, land, block, w, m, v in items:
        in_specs += [pl.BlockSpec(w.shape, lambda i, c_ref: (0, 0))] * 3
        args += [w, m, v]
    out_specs = [pl.BlockSpec(w.shape, lambda i, c_ref: (0, 0)) for _, _, _, w, _, _ in items for _ in range(4)]
    res = pl.pallas_call(
        body, name=name,
        grid_spec=pltpu.PrefetchScalarGridSpec(num_scalar_prefetch=1, grid=(1,), in_specs=in_specs,
                                               out_specs=out_specs),
        out_shape=[jax.ShapeDtypeStruct(w.shape, F32) for _, _, _, w, _, _ in items for _ in range(4)],
        compiler_params=_params(),
    )(chip, *args)
    return [res[4 * j:4 * j + 4] for j in range(k)]


def _small_step(parts, ws, ms, vs):
    def exchange(gpre_ref, gconv_ref, gsink_ref, gmem_ref, gpost_ref, loss_ref, tot_ref,
                 pack, gathered, send_sems, recv_sems):
        x, y, c = _my_place()
        me_idx = 4 * x + 2 * y + c

        lane = lax.broadcasted_iota(jnp.int32, (1, 128), 1)
        sink_row = jnp.zeros((1, 128), F32)
        for h in range(8):
            sink_row = jnp.where(lane == h, gsink_ref[h:h + 1, :], sink_row)
        pack[...] = jnp.zeros_like(pack)
        pack[0:1, :] = gpre_ref[...]
        pack[1:2, :] = gmem_ref[...]
        pack[2:3, :] = gpost_ref[...]
        pack[3:6, 0:512] = gconv_ref[0:3, :]
        pack[6:7, 0:128] = sink_row
        pack[7:8, 0:128] = loss_ref[0:1, :]

        flips = [(0, 0, 1), (0, 1, 0), (1, 0, 0), (0, 1, 1), (1, 0, 1), (1, 1, 0), (1, 1, 1)]
        cps = []
        for k, (fx, fy, fc) in enumerate(flips):
            peer = ((1 - x) if fx else x, (1 - y) if fy else y, (1 - c) if fc else c)
            cps.append(pltpu.make_async_remote_copy(
                src_ref=pack, dst_ref=gathered.at[me_idx], send_sem=send_sems.at[k], recv_sem=recv_sems.at[k],
                device_id=peer, device_id_type=MESH))
        for cp in cps:
            cp.start()
        gathered[me_idx] = pack[...]
        for cp in cps:
            cp.wait_recv()
        for cp in cps:
            cp.wait_send()
        tot = gathered[0]
        for d in range(1, N_DEV):
            tot = tot + gathered[d]
        tot_ref[...] = tot

    tot = pl.pallas_call(
        exchange, name="small_exchange", grid=(1,),
        out_shape=jax.ShapeDtypeStruct((8, D_MODEL), F32),
        in_specs=[_full(p.shape) for p in parts], out_specs=_full((8, D_MODEL)),
        scratch_shapes=[pltpu.VMEM((8, D_MODEL), F32), pltpu.VMEM((N_DEV, 8, D_MODEL), F32),
                        pltpu.SemaphoreType.DMA((N_PEERS,)), pltpu.SemaphoreType.DMA((N_PEERS,))],
    )(*parts)

    def apply(tot_ref, *refs):
        w_refs, m_refs, v_refs = refs[0:5], refs[5:10], refs[10:15]
        loss_out = refs[15]
        g_outs, d_outs, m_outs, v_outs = refs[16:21], refs[21:26], refs[26:31], refs[31:36]
        x, y, c = _my_place()
        tot = tot_ref[...]
        conv = pltpu.roll(tot[:, 0:512], (512 - 64 * (4 * x + 2 * y + c)) % 512, 1)[3:6, 0:64]
        grads = (tot[0:1, :], conv, tot[6:7, 0:8], tot[1:2, :], tot[2:3, :])
        loss_out[...] = tot[7:8, 0:128]
        for j in range(5):
            g_outs[j][...] = grads[j]
            d_outs[j][...], m_outs[j][...], v_outs[j][...] = _adamw_math(
                w_refs[j][...], grads[j], m_refs[j][...], v_refs[j][...])

    specs = [_full(w.shape) for w in ws]
    res = pl.pallas_call(
        apply, name="small_apply", grid=(1,),
        out_shape=[jax.ShapeDtypeStruct((1, 128), F32)] + [jax.ShapeDtypeStruct(w.shape, F32) for w in ws] * 4,
        in_specs=[_full((8, D_MODEL))] + specs * 3,
        out_specs=[_full((1, 128))] + specs * 4,
    )(tot, *ws, *ms, *vs)
    return res[0], res[1:6], res[6:11], res[11:16], res[16:21]


def kernel(x, mem, g_pre, w_in, w_conv, attn_sink, g_mem, w_mem_kv, w_up_a, w_up_b, w_up_m, w_out, g_post, loss_target, m_g_pre, m_w_in, m_w_conv, m_attn_sink, m_g_mem, m_w_mem_kv, m_w_up_a, m_w_up_b, m_w_up_m, m_w_out, m_g_post, v_g_pre, v_w_in, v_w_conv, v_attn_sink, v_g_mem, v_w_mem_kv, v_w_up_a, v_w_up_b, v_w_up_m, v_w_out, v_g_post):
    s = x.shape[1]
    x2, mem2, tgt2 = x[0], mem[0], loss_target[0]
    me = 4 * lax.axis_index("x") + 2 * lax.axis_index("y") + lax.axis_index("c")

    w_conv_loc = jnp.zeros((8, 128), F32).at[:3, :64].set(w_conv[0])
    w_int_g, w_conv_g, tabs, w_mkv_loc, w_out_loc, w_up_loc = _all_gather(
        [w_in[0].T.astype(BF16), w_conv_loc], "gather_w_in",
        splits=[[(112 * k, 112) for k in range(7)] + [(784, 144)], [(0, 8)]],
        side=_gather_side(s, w_mem_kv[0], w_out[0], (w_up_a[0], w_up_b[0], w_up_m[0])))
    w_int = w_int_g.reshape(IN_WIDTH, D_MODEL)
    w_conv_f = w_conv_g[:, :3, :64].transpose(1, 0, 2).reshape(3, 512)
    late = _gather_start([w_mkv_loc, w_out_loc, w_up_loc], me, "gather_late_start")
    sink = attn_sink[0]

    h, pa, pq, pkv, pbz, pmq, pmz, pg = _proj_fwd(late[4], x2, g_pre, w_int, tabs)
    ya = _conv_fwd(pa, w_conv_f)
    yb = _attn_fwd(pq, pkv, pbz, sink)
    w_mkv_g, w_out_g, w_up_g = _gather_wait(*late[:4], yb, "gather_late_wait")
    w_mkv = w_mkv_g.reshape(D_MODEL, D_MODEL)
    w_out_f = w_out_g.reshape(D_MODEL, D_MODEL)
    mn, mkv = _mem_kv_fwd(mem2, g_mem, w_mkv)
    ym = _mem_attn_fwd(pmq, pmz, mkv)
    dg, dya, dyb, dym, dy, loss_p, gg_post, mb, dob, du = _mid(ya, yb, ym, pg, x2, tgt2, g_post, w_up_g, w_out_f)
    gw_out, gw_up = _gw_mid(mb, dob, (ya, yb, ym), du)

    core = lax.axis_index("c").astype(jnp.int32).reshape(1)
    chip = (2 * lax.axis_index("x") + lax.axis_index("y")).astype(jnp.int32).reshape(1)

    dmq, dmz, dmkv = _mem_attn_bwd(pmq, pmz, mkv, dym)
    gw_mkv, gg_mem = _mem_kv_bwd(mem2, g_mem, mn, dmkv, w_mkv)
    shares1 = [gw_mkv.reshape(N_DEV, 128, D_MODEL), gw_out.reshape(N_DEV, 128, D_MODEL), gw_up]
    sib = _split_start(_sibling_copies, N_CHIPS, shares1,
                       [lax.empty((N_CHIPS,) + a.shape[1:], a.dtype) for a in shares1], "grads_to_sibling_small_start")
    da, gw_conv = _conv_bwd(sib[4], pa, dya, w_conv_f)
    shares1, from_sibling = _split_wait(_sibling_copies, N_CHIPS, *sib[:4], da, "grads_to_sibling_small_wait")
    send1, recv1, srcs1, lands1, token1 = _chip_exchange_start(
        _pair_add(shares1, from_sibling, core, "grads_pair_add_small"), "grads_to_chips_start_small")
    dq, dbz, dkv, g_sink = _attn_bwd(token1, pq, pkv, pbz, dyb, sink, tabs)
    dparts = (da, dq, dkv, dbz, dmq, dmz, dg)
    gw_int = _gw_in(dparts, h)
    send2, recv2, srcs2, lands2, token2 = _chip_exchange_start(
        [_sibling_reduce(gw_int.reshape(N_DEV, SHARD_IN, D_MODEL), "grads_sibling_reduce_w_in")],
        "grads_to_chips_start_w_in")
    grad_x, gg_pre = _dh_bwd(token2, dparts, x2, dy, g_pre, w_int)
    (o_mkv, o_out, o_up, o_int), (l_mkv, l_out, l_up, l_int) = _chip_exchange_wait(
        send1 + send2, recv1 + recv2, srcs1 + srcs2, lands1 + lands2, grad_x, "grads_to_chips_wait")

    loss_row, small_g, sd, sm, sv = _small_step(
        (gg_pre, gw_conv, g_sink, gg_mem, gg_post, loss_p),
        [g_pre, w_conv[0], attn_sink, g_mem, g_post],
        [m_g_pre, m_w_conv[0], m_attn_sink, m_g_mem, m_g_post],
        [v_g_pre, v_w_conv[0], v_attn_sink, v_g_mem, v_g_post])
    loss = loss_row[0, 0]
    g_g_pre, g_conv, g_sink_tot, g_g_mem, g_g_post = small_g

    g_w_in, d_w_in, nm_w_in, nv_w_in = (t.T for t in _sum_adamw(
        o_int, l_int, chip, 0, w_in[0].T, m_w_in[0].T, v_w_in[0].T, "adamw_w_in", tiles=2))
    (g_mkv, d_mkv, nm_mkv, nv_mkv), (g_out, d_out, nm_out, nv_out), *up = _sum_adamw_group(
        [(o_mkv, l_mkv, 0, w_mem_kv[0], m_w_mem_kv[0], v_w_mem_kv[0]),
         (o_out, l_out, 0, w_out[0], m_w_out[0], v_w_out[0]),
         (o_up, l_up, 0, w_up_a[0], m_w_up_a[0], v_w_up_a[0]),
         (o_up, l_up, 1, w_up_b[0], m_w_up_b[0], v_w_up_b[0]),
         (o_up, l_up, 2, w_up_m[0], m_w_up_m[0], v_w_up_m[0])], chip, "adamw_mid_weights")

    def lead(a):
        return a[None]

    grads = [g_g_pre, lead(g_w_in), lead(g_conv), g_sink_tot, g_g_mem, lead(g_mkv), lead(up[0][0]),
             lead(up[1][0]), lead(up[2][0]), lead(g_out), g_g_post]

    def assemble(small, big_in, big_mkv, big_up, big_out):
        return [small[0], lead(big_in), lead(small[1]), small[2], small[3], lead(big_mkv), lead(big_up[0]),
                lead(big_up[1]), lead(big_up[2]), lead(big_out), small[4]]

    deltas = assemble(sd, d_w_in, d_mkv, [u[1] for u in up], d_out)
    new_m = assemble(sm, nm_w_in, nm_mkv, [u[2] for u in up], nm_out)
    new_v = assemble(sv, nv_w_in, nv_mkv, [u[3] for u in up], nv_out)
    return (loss, grad_x[None], *grads, *deltas, *new_m, *new_v)
```

```python
import functools

import jax
import jax.numpy as jnp
from jax import lax
from jax.experimental import pallas as pl
from jax.experimental.pallas import tpu as pltpu

F32 = jnp.float32
BF16 = jnp.bfloat16
MESH = pl.DeviceIdType.MESH

N_DEV = 8
D_MODEL = 1024
EPS = 1e-6
ROPE_THETA = 500000.0
ROT_DIM = 16
HEAD_DIM = 64
ATTN_BLOCK = 128
MEM_HEADS = 4
MEM_HEAD_DIM = 128
ATTN_SCALE = HEAD_DIM ** -0.5
MEM_SCALE = MEM_HEAD_DIM ** -0.5

ADAM_LR = 0.001
ADAM_B1 = 0.9
ADAM_B2 = 0.999
ADAM_EPS = 1e-08
ADAM_WD = 0.01
ADAM_STEP = 10

SEG_A = (0, 2048)
SEG_BQ = (2048, 512)
SEG_BKV = (2560, 256)
SEG_BZ = (2816, 512)
SEG_MQ = (3328, 512)
SEG_MZ = (3840, 512)
SEG_G = (4352, 3072)
SEGS = (SEG_A, SEG_BQ, SEG_BKV, SEG_BZ, SEG_MQ, SEG_MZ, SEG_G)
IN_WIDTH = 7424
SHARD_IN = IN_WIDTH // N_DEV

V7X_VMEM_BYTES = 64 * 1024 * 1024
CALL_VMEM_MB = 57
ANY = pl.BlockSpec(memory_space=pl.ANY)


def _params():
    assert CALL_VMEM_MB * 1024 * 1024 < V7X_VMEM_BYTES
    return pltpu.CompilerParams(dimension_semantics=("arbitrary",), vmem_limit_bytes=CALL_VMEM_MB * 1024 * 1024)


def _full(shape):
    zeros = (0,) * len(shape)
    return pl.BlockSpec(shape, lambda i: zeros)


def _rows(tm, width):
    return pl.BlockSpec((tm, width), lambda i: (i, 0))


def _dot(a, b):
    return jnp.dot(a, b, preferred_element_type=F32)


def _dot_nt(a, b):
    return lax.dot_general(a, b, (((1,), (1,)), ((), ())), preferred_element_type=F32)


def _dot_tn(a, b):
    return lax.dot_general(a, b, (((0,), (0,)), ((), ())), preferred_element_type=F32)


def _sigmoid(z):
    return 1.0 / (1.0 + jnp.exp(-z))


def _rope(t, cs, s1, s2):
    return t * cs + pltpu.roll(t, 120, 1) * s1 + pltpu.roll(t, 8, 1) * s2


def _rope_t(d, cs, s1, s2):
    return d * cs + pltpu.roll(d * s1, 8, 1) + pltpu.roll(d * s2, 120, 1)


def _gather_side(s, w_mkv, w_out, w_ups):
    half = ROT_DIM // 2
    inv_freq = jnp.power(jnp.float32(ROPE_THETA), -jnp.arange(half, dtype=F32) * (2.0 / ROT_DIM))
    freq_row = jnp.tile(jnp.concatenate([inv_freq, inv_freq, jnp.zeros((HEAD_DIM - ROT_DIM,), F32)]), 2)[None, :]

    def fn(in_refs, out_refs):
        f_ref, mkv_ref, out_ref, *up_refs = in_refs
        t_ref, mkv_bf, out_bf, up_bf = out_refs
        mkv_bf[...] = mkv_ref[...].astype(BF16)
        out_bf[...] = out_ref[...].astype(BF16)
        for k, up_ref in enumerate(up_refs):
            up_bf[512 * k:512 * k + 512, :] = up_ref[...].astype(BF16)
        pos = lax.broadcasted_iota(jnp.int32, (s, 128), 0).astype(F32)
        d = lax.broadcasted_iota(jnp.int32, (s, 128), 1) & (HEAD_DIM - 1)
        ang = pos * f_ref[...]
        cos, sin = jnp.cos(ang), jnp.sin(ang)
        lo, hi = d < half, (d >= half) & (d < ROT_DIM)
        t_ref[0] = jnp.where(lo | hi, cos, 1.0)
        t_ref[1] = jnp.where(lo, -sin, 0.0)
        t_ref[2] = jnp.where(hi, sin, 0.0)

    return ([freq_row, w_mkv, w_out, *w_ups],
            [jax.ShapeDtypeStruct((3, s, 128), F32), jax.ShapeDtypeStruct(w_mkv.shape, BF16),
             jax.ShapeDtypeStruct(w_out.shape, BF16), jax.ShapeDtypeStruct((1536, 128), BF16)], fn)


def _load_once(pairs, sems):
    @pl.when(pl.program_id(0) == 0)
    def _():
        cps = [pltpu.make_async_copy(src, dst, sems.at[k]) for k, (src, dst) in enumerate(pairs)]
        for cp in cps:
            cp.start()
        for cp in cps:
            cp.wait()


def _my_place():
    x, y, c = lax.axis_index("x"), lax.axis_index("y"), lax.axis_index("c")
    return x, y, c


def _all_gather(arrs, name, splits=None, side=None):
    n = len(arrs)
    if splits is None:
        splits = [[(0, a.shape[0])] for a in arrs]
    pieces = [(a, r0, rn) for a in range(n) for r0, rn in splits[a]]
    n_p = len(pieces)
    side_in, side_out, side_fn = side if side is not None else ((), (), None)
    m, q = len(side_in), len(side_out)

    def body(*refs):
        ins, outs = refs[:n], refs[n + m:2 * n + m]
        send_sems, recv_sems, local_sems = refs[2 * n + m + q:]
        x, y, c = _my_place()
        me, sibling = (x, y, c), (x, y, 1 - c)

        def route(core):
            first = (jnp.bitwise_xor(x, 1 - core), jnp.bitwise_xor(y, core), core)
            second = (jnp.bitwise_xor(x, core), jnp.bitwise_xor(y, 1 - core), core)
            return first, second, (1 - x, 1 - y, core)

        def idx(px, py, pc):
            return 4 * px + 2 * py + pc

        def copy(p, k, block, to, own=False):
            a, r0, rn = pieces[p]
            dst = outs[a].at[idx(*block), pl.ds(r0, rn)]
            return pltpu.make_async_remote_copy(
                src_ref=ins[a].at[pl.ds(r0, rn)] if own else dst, dst_ref=dst,
                send_sem=send_sems.at[p * 7 + k], recv_sem=recv_sems.at[p * 7 + k],
                device_id=to, device_id_type=MESH)

        nbr1, nbr2, diag = route(c)
        mine = [pltpu.make_async_copy(ins[a], outs[a].at[idx(*me)], local_sems.at[a]) for a in range(n)]
        for cp in mine:
            cp.start()
        sent = []
        for p in range(n_p):
            for k, to in enumerate((sibling, nbr1, nbr2)):
                sent.append(copy(p, k, me, to, own=True))
        for cp in sent:
            cp.start()
        if side_fn is not None:
            side_fn(refs[n:n + m], refs[2 * n + m:2 * n + m + q])
        for k_in, block, onward in ((1, nbr1, ((3, nbr2), (4, sibling))), (2, nbr2, ((5, sibling),)),
                                    (3, diag, ((6, sibling),))):
            for p in range(n_p):
                copy(p, k_in, block, me).wait_recv()
                for k_out, to in onward:
                    cp = copy(p, k_out, block, to)
                    cp.start()
                    sent.append(cp)
        s1, s2, sd = route(1 - c)
        for k_in, block in ((0, sibling), (4, s1), (5, s2), (6, sd)):
            for p in range(n_p):
                copy(p, k_in, block, me).wait_recv()
        for cp in sent:
            cp.wait_send()
        for cp in mine:
            cp.wait()

    return pl.pallas_call(
        body, name=name,
        out_shape=[jax.ShapeDtypeStruct((N_DEV,) + a.shape, a.dtype) for a in arrs] + list(side_out),
        in_specs=[ANY] * n + [pl.BlockSpec(memory_space=pltpu.VMEM)] * m,
        out_specs=[ANY] * n + [pl.BlockSpec(memory_space=pltpu.VMEM)] * q,
        scratch_shapes=[pltpu.SemaphoreType.DMA((7 * n_p,)), pltpu.SemaphoreType.DMA((7 * n_p,)),
                        pltpu.SemaphoreType.DMA((n,))],
        compiler_params=pltpu.CompilerParams(vmem_limit_bytes=32 * 1024 * 1024),
    )(*arrs, *side_in)


N_CHIPS = 4


def _sibling_reduce(arr, name):
    _, r, c = arr.shape

    def body(in_ref, out_ref, land, a_buf, b_buf, o_buf, send_sems, recv_sems, local_sems):
        x, y, core = _my_place()
        cps = [pltpu.make_async_remote_copy(
            src_ref=in_ref.at[2 * j + (1 - core)], dst_ref=land.at[j], send_sem=send_sems.at[j],
            recv_sem=recv_sems.at[j], device_id=(x, y, 1 - core), device_id_type=MESH) for j in range(N_CHIPS)]
        for cp in cps:
            cp.start()
        store = None
        for j in range(N_CHIPS):
            mine = pltpu.make_async_copy(in_ref.at[2 * j + core], a_buf, local_sems.at[0])
            mine.start()
            cps[j].wait_recv()
            theirs = pltpu.make_async_copy(land.at[j], b_buf, local_sems.at[1])
            theirs.start()
            mine.wait()
            theirs.wait()
            if store is not None:
                store.wait()
            o_buf[...] = (a_buf[...].astype(F32) + b_buf[...].astype(F32)).astype(BF16)
            store = pltpu.make_async_copy(o_buf, out_ref.at[j], local_sems.at[2])
            store.start()
        store.wait()
        for cp in cps:
            cp.wait_send()

    return pl.pallas_call(
        body, name=name,
        out_shape=[jax.ShapeDtypeStruct((N_CHIPS, r, c), BF16)] * 2,
        in_specs=[ANY], out_specs=[ANY, ANY],
        scratch_shapes=[pltpu.VMEM((r, c), BF16)] * 3
        + [pltpu.SemaphoreType.DMA((N_CHIPS,)), pltpu.SemaphoreType.DMA((N_CHIPS,)), pltpu.SemaphoreType.DMA((3,))],
        compiler_params=pltpu.CompilerParams(vmem_limit_bytes=32 * 1024 * 1024),
    )(arr)[0]


def _sibling_copies(srcs, lands, send_sems, recv_sems):
    x, y, c = _my_place()
    cps = []
    for j in range(N_CHIPS):
        for a in range(len(srcs)):
            k = a * N_CHIPS + j
            cps.append(pltpu.make_async_remote_copy(
                src_ref=srcs[a].at[2 * j + (1 - c)], dst_ref=lands[a].at[j], send_sem=send_sems[k],
                recv_sem=recv_sems[k], device_id=(x, y, 1 - c), device_id_type=MESH))
    return cps


def _pair_add(mine, recv, core, name):
    n = len(mine)

    def body(c_ref, *refs):
        for a in range(n):
            refs[2 * n + a][...] = (refs[a][...].astype(F32) + refs[n + a][...].astype(F32)).astype(BF16)

    def blk(a):
        return (None,) + a.shape[1:]

    grid_spec = pltpu.PrefetchScalarGridSpec(
        num_scalar_prefetch=1, grid=(N_CHIPS,),
        in_specs=[pl.BlockSpec(blk(a), lambda j, c_ref: (2 * j + c_ref[0], 0, 0)) for a in mine]
        + [pl.BlockSpec(blk(a), lambda j, c_ref: (j, 0, 0)) for a in recv],
        out_specs=[pl.BlockSpec(blk(a), lambda j, c_ref: (j, 0, 0)) for a in recv])
    return pl.pallas_call(
        body, name=name, grid_spec=grid_spec,
        out_shape=[jax.ShapeDtypeStruct(a.shape, BF16) for a in recv],
        compiler_params=_params(),
    )(core, *mine, *recv)


HBM = pl.BlockSpec(memory_space=pltpu.HBM)
SEM = pl.BlockSpec(memory_space=pltpu.SEMAPHORE)
N_PEER_CHIPS = 3
TOKEN = (8, 128)


def _chip_copies(srcs, lands, send_sems, recv_sems):
    x, y, c = _my_place()
    my_chip = 2 * x + y
    peers = [(x, 1 - y), (1 - x, y), (1 - x, 1 - y)]
    cps = []
    for k, (px, py) in enumerate(peers):
        for a in range(len(srcs)):
            j = a * N_PEER_CHIPS + k
            cps.append(pltpu.make_async_remote_copy(
                src_ref=srcs[a].at[2 * px + py], dst_ref=lands[a].at[my_chip],
                send_sem=send_sems[j], recv_sem=recv_sems[j],
                device_id=(px, py, c), device_id_type=MESH))
    return cps


N_PEERS = N_DEV - 1


def _gather_copies(srcs, lands, send_sems, recv_sems):
    x, y, c = _my_place()
    me_idx = 4 * x + 2 * y + c
    flips = [(0, 0, 1), (0, 1, 0), (1, 0, 0), (0, 1, 1), (1, 0, 1), (1, 1, 0), (1, 1, 1)]
    cps = []
    for k, (fx, fy, fc) in enumerate(flips):
        peer = ((1 - x) if fx else x, (1 - y) if fy else y, (1 - c) if fc else c)
        for a in range(len(srcs)):
            j = a * N_PEERS + k
            cps.append(pltpu.make_async_remote_copy(
                src_ref=srcs[a], dst_ref=lands[a].at[me_idx], send_sem=send_sems[j], recv_sem=recv_sems[j],
                device_id=peer, device_id_type=MESH))
    return cps


def _split_start(copies, per_array, arrs, lands, name):
    arrs, lands = list(arrs), list(lands)
    n = len(arrs)
    k = n * per_array

    def body(*refs):
        srcs, land_refs = refs[:n], refs[n:2 * n]
        send_sems, recv_sems = refs[2 * n:2 * n + k], refs[2 * n + k:2 * n + 2 * k]
        token = refs[-1]
        for cp in copies(srcs, land_refs, send_sems, recv_sems):
            cp.start()
        token[...] = jnp.zeros_like(token)

    hbm_arrs = [pltpu.with_memory_space_constraint(a, pltpu.HBM) for a in arrs]
    lands = [pltpu.with_memory_space_constraint(a, pltpu.HBM) for a in lands]
    res = pl.pallas_call(
        body, name=name,
        out_shape=[pltpu.SemaphoreType.DMA(())] * (2 * k) + [pltpu.HBM(a.shape, a.dtype) for a in arrs + lands]
        + [jax.ShapeDtypeStruct(TOKEN, F32)],
        in_specs=[HBM] * (2 * n),
        out_specs=[SEM] * (2 * k) + [HBM] * (2 * n) + [pl.BlockSpec(memory_space=pltpu.VMEM)],
        input_output_aliases={a: 2 * k + a for a in range(2 * n)},
        compiler_params=pltpu.CompilerParams(has_side_effects=pltpu.SideEffectType.DATAFLOW_SIDE_EFFECTING),
    )(*hbm_arrs, *lands)
    return res[:k], res[k:2 * k], res[2 * k:2 * k + n], res[2 * k + n:2 * k + 2 * n], res[-1]


def _split_wait(copies, per_array, send_sems, recv_sems, srcs, lands, after, name):
    n = len(srcs)
    k = n * per_array

    def body(*refs):
        src_refs, land_refs = refs[:n], refs[n:2 * n]
        s_sems, r_sems = refs[2 * n:2 * n + k], refs[2 * n + k:2 * n + 2 * k]
        for cp in copies(src_refs, land_refs, s_sems, r_sems):
            cp.wait_send()
            cp.wait_recv()

    res = pl.pallas_call(
        body, name=name,
        out_shape=[pltpu.HBM(a.shape, a.dtype) for a in list(srcs) + list(lands)],
        in_specs=[HBM] * (2 * n) + [SEM] * (2 * k) + [ANY],
        out_specs=[HBM] * (2 * n),
        input_output_aliases={a: a for a in range(2 * n)},
        compiler_params=pltpu.CompilerParams(has_side_effects=pltpu.SideEffectType.DATAFLOW_SIDE_EFFECTING),
    )(*srcs, *lands, *send_sems, *recv_sems, after)
    return res[:n], res[n:]


def _chip_exchange_start(arrs, name):
    return _split_start(_chip_copies, N_PEER_CHIPS, arrs, [lax.empty(a.shape, a.dtype) for a in arrs], name)


def _chip_exchange_wait(send_sems, recv_sems, srcs, lands, after, name):
    return _split_wait(_chip_copies, N_PEER_CHIPS, send_sems, recv_sems, srcs, lands, after, name)


def _gather_start(arrs, me_idx, name):
    lands = [lax.dynamic_update_slice(lax.empty((N_DEV,) + a.shape, a.dtype), a[None], (me_idx, 0, 0)) for a in arrs]
    return _split_start(_gather_copies, N_PEERS, arrs, lands, name)


def _gather_wait(send_sems, recv_sems, srcs, lands, after, name):
    return _split_wait(_gather_copies, N_PEERS, send_sems, recv_sems, srcs, lands, after, name)[1]


def _proj_fwd(after, x, g_pre, w_int, tabs):
    s = x.shape[0]
    tm = min(512, s)

    def body(after_ref, x_ref, g_ref, t_ref, w_hbm,
             h_ref, pa_ref, pq_ref, pkv_ref, pbz_ref, pmq_ref, pmz_ref, pg_ref, w_vm, sems):
        _load_once([(w_hbm, w_vm)], sems)
        xf = x_ref[...]
        r = lax.rsqrt(jnp.mean(xf * xf, axis=-1, keepdims=True) + EPS)
        h = ((xf * r) * g_ref[...]).astype(BF16)
        h_ref[...] = h
        cs, s1, s2 = t_ref[0], t_ref[1], t_ref[2]

        def mm(seg, c0, width):
            return _dot_nt(h, w_vm[seg[0] + c0:seg[0] + c0 + width, :])

        for c0 in range(0, SEG_A[1], 512):
            pa_ref[:, c0:c0 + 512] = mm(SEG_A, c0, 512).astype(BF16)
        q = mm(SEG_BQ, 0, 512)
        for b in range(4):
            pq_ref[:, 128 * b:128 * b + 128] = _rope(q[:, 128 * b:128 * b + 128], cs, s1, s2).astype(BF16)
        kv = mm(SEG_BKV, 0, 256)
        pkv_ref[:, 0:128] = _rope(kv[:, 0:128], cs, s1, s2).astype(BF16)
        pkv_ref[:, 128:256] = kv[:, 128:256].astype(BF16)
        pbz_ref[...] = mm(SEG_BZ, 0, 512).astype(BF16)
        pmq_ref[...] = mm(SEG_MQ, 0, 512).astype(BF16)
        pmz_ref[...] = mm(SEG_MZ, 0, 512).astype(BF16)
        for c0 in range(0, SEG_G[1], 512):
            pg_ref[:, c0:c0 + 512] = mm(SEG_G, c0, 512).astype(BF16)

    widths = (D_MODEL, 2048, 512, 256, 512, 512, 512, 3072)
    return pl.pallas_call(
        body, name="proj_fwd", grid=(s // tm,),
        out_shape=[jax.ShapeDtypeStruct((s, w), BF16) for w in widths],
        in_specs=[_full(TOKEN), _rows(tm, D_MODEL), _full((1, D_MODEL)),
                  pl.BlockSpec((3, tm, 128), lambda i: (0, i, 0)), ANY],
        out_specs=[_rows(tm, w) for w in widths],
        scratch_shapes=[pltpu.VMEM((IN_WIDTH, D_MODEL), BF16), pltpu.SemaphoreType.DMA((1,))],
        compiler_params=_params(),
    )(after, x, g_pre, tabs, w_int)


def _mem_kv_fwd(mem, g_mem, w_mkv):
    m = mem.shape[0]

    def body(mem_ref, g_ref, w_ref, mn_ref, mkv_ref):
        xf = mem_ref[...]
        r = lax.rsqrt(jnp.mean(xf * xf, axis=-1, keepdims=True) + EPS)
        mn = ((xf * r) * g_ref[...]).astype(BF16)
        mn_ref[...] = mn
        mkv_ref[...] = _dot(mn, w_ref[...]).astype(BF16)

    return pl.pallas_call(
        body, name="mem_kv_fwd", grid=(1,),
        out_shape=[jax.ShapeDtypeStruct((m, D_MODEL), BF16)] * 2,
        in_specs=[_full((m, D_MODEL)), _full((1, D_MODEL)), _full((D_MODEL, D_MODEL))],
        out_specs=[_full((m, D_MODEL))] * 2,
        compiler_params=_params(),
    )(mem, g_mem, w_mkv)


def _halo_specs(s, tm, rows, width):
    nblk = s // rows
    prev = pl.BlockSpec((rows, width), lambda i: (jnp.maximum(i * (tm // rows) - 1, 0), 0))
    nxt = pl.BlockSpec((rows, width), lambda i: (jnp.minimum((i + 1) * (tm // rows), nblk - 1), 0))
    return prev, nxt


def _conv_common(pa, cu_prev, cu_next, w, tm):
    b, c, u, z = (pa[:, 512 * k:512 * k + 512] for k in range(4))
    cu = c * u
    row = lax.broadcasted_iota(jnp.int32, (tm, 512), 0)
    cu_m1 = jnp.where(row == 0, cu_prev, pltpu.roll(cu, 1, 0))
    cu_p1 = jnp.where(row == tm - 1, cu_next, pltpu.roll(cu, tm - 1, 0))
    y = cu_m1 * w[0:1] + cu * w[1:2] + cu_p1 * w[2:3]
    sig = _sigmoid(z)
    return b, c, u, z, cu, cu_m1, cu_p1, y, sig, row


def _conv_fwd(pa, w_conv):
    s = pa.shape[0]
    tm = min(512, s)
    nt = s // tm

    def body(pa_ref, pp_ref, pn_ref, w_ref, ya_ref):
        i = pl.program_id(0)
        prev_row = pp_ref[...].astype(F32)[15:16, :]
        next_row = pn_ref[...].astype(F32)[0:1, :]
        b, _, _, z, _, _, _, y, sig, _ = _conv_common(
            pa_ref[...].astype(F32),
            jnp.where(i == 0, 0.0, prev_row[:, 512:1024] * prev_row[:, 1024:1536]),
            jnp.where(i == nt - 1, 0.0, next_row[:, 512:1024] * next_row[:, 1024:1536]), w_ref[...], tm)
        ya_ref[...] = (b * y * (z * sig)).astype(BF16)

    prev, nxt = _halo_specs(s, tm, 16, 2048)
    return pl.pallas_call(
        body, name="conv_fwd", grid=(nt,),
        out_shape=jax.ShapeDtypeStruct((s, 512), BF16),
        in_specs=[_rows(tm, 2048), prev, nxt, _full((3, 512))],
        out_specs=_rows(tm, 512),
        compiler_params=_params(),
    )(pa, pa, pa, w_conv)


def _heads_to_lanes(a, g, row):
    low = row < HEAD_DIM
    parts = []
    for b in (2 * g, 2 * g + 1):
        t = jnp.transpose(a[:, 128 * b:128 * b + 128])
        swapped = pltpu.roll(t, HEAD_DIM, 0)
        if g == 0:
            parts += [jnp.where(low, t, 0.0), jnp.where(low, swapped, 0.0)]
        else:
            parts += [jnp.where(low, 0.0, swapped), jnp.where(low, 0.0, t)]
    return jnp.concatenate(parts, axis=1)


def _lanes_to_heads(t0, t1, row):
    low = row < HEAD_DIM
    blocks = []
    for b in range(4):
        g = b // 2
        tg = (t0, t1)[g]
        je = 2 * (b - 2 * g)
        even, odd = tg[:, 128 * je:128 * je + 128], tg[:, 128 * je + 128:128 * je + 256]
        if g == 0:
            t = jnp.where(low, even, pltpu.roll(odd, HEAD_DIM, 0))
        else:
            t = jnp.where(low, pltpu.roll(even, HEAD_DIM, 0), odd)
        blocks.append(jnp.transpose(t))
    return jnp.concatenate(blocks, axis=1)


WINDOW_KEYS = 3 * ATTN_BLOCK
STACKED = 4 * ATTN_BLOCK
KEY_CHUNK = 32
MAX_BLOCKS_IN_STEP = 8


def _fill_band_bias(bias, nb):
    assert nb >= 2
    c = lax.broadcasted_iota(jnp.int32, (WINDOW_KEYS, STACKED), 0)
    r = lax.broadcasted_iota(jnp.int32, (WINDOW_KEYS, STACKED), 1) & (ATTN_BLOCK - 1)
    band = (c >= r) & (c <= r + 2 * ATTN_BLOCK)
    for v, ok in enumerate((band, band & (c >= ATTN_BLOCK), band & (c < 2 * ATTN_BLOCK))):
        bias[v] = jnp.where(ok, 0.0, -jnp.inf)


def _bias_variant(n, nb):
    return jnp.where(n == 0, 1, jnp.where(n == nb - 1, 2, 0))


def _sink_row(sink_ref, g):
    return jnp.concatenate([jnp.full((1, ATTN_BLOCK), sink_ref[4 * g + j], F32) for j in range(4)], axis=1)


def _softmax_keys_major(sc, bias, variant, sink, e_scr):
    chunks = [pl.ds(k * KEY_CHUNK, KEY_CHUNK) for k in range(WINDOW_KEYS // KEY_CHUNK)]
    rows = [slice(k * KEY_CHUNK, (k + 1) * KEY_CHUNK) for k in range(WINDOW_KEYS // KEY_CHUNK)]
    m_run = jnp.full((KEY_CHUNK, STACKED), -jnp.inf, F32)
    for ck, rw in zip(chunks, rows):
        m_run = jnp.maximum(m_run, sc[rw] + bias[variant, ck, :])
    m = jnp.maximum(jnp.max(m_run, axis=0, keepdims=True), sink)
    l_run = jnp.zeros((KEY_CHUNK, STACKED), F32)
    for ck, rw in zip(chunks, rows):
        e = jnp.exp(sc[rw] + bias[variant, ck, :] - m)
        l_run += e
        e_scr[rw, :] = e.astype(BF16)
    es = jnp.exp(sink - m)
    inv = 1.0 / (jnp.sum(l_run, axis=0, keepdims=True) + es)
    return inv, es * inv


def _fill_padded(kv_ref, kpad, vpad, s):
    zero = jnp.zeros((ATTN_BLOCK, 128), BF16)
    kpad[0:ATTN_BLOCK, :] = zero
    vpad[0:ATTN_BLOCK, :] = zero
    kpad[ATTN_BLOCK + s:2 * ATTN_BLOCK + s, :] = zero
    vpad[ATTN_BLOCK + s:2 * ATTN_BLOCK + s, :] = zero
    kpad[ATTN_BLOCK:ATTN_BLOCK + s, :] = kv_ref[:, 0:128]
    vpad[ATTN_BLOCK:ATTN_BLOCK + s, :] = kv_ref[:, 128:256]


def _attn_fwd(pq, pkv, pbz, sink):
    s = pq.shape[0]
    nb = s // ATTN_BLOCK
    bps = min(MAX_BLOCKS_IN_STEP, nb)

    def body(sink_ref, q_ref, z_ref, kv_ref, yb_ref, kpad, vpad, bias, e_scr):
        i = pl.program_id(0)

        @pl.when(i == 0)
        def _():
            _fill_padded(kv_ref, kpad, vpad, s)
            _fill_band_bias(bias, nb)

        row = lax.broadcasted_iota(jnp.int32, (ATTN_BLOCK, 128), 0)
        for b in range(bps):
            n = i * bps + b
            rows = slice(b * ATTN_BLOCK, (b + 1) * ATTN_BLOCK)
            start = pl.multiple_of(n * ATTN_BLOCK, ATTN_BLOCK)
            kw, vw = kpad[pl.ds(start, WINDOW_KEYS), :], vpad[pl.ds(start, WINDOW_KEYS), :]
            qf = q_ref[rows, :].astype(F32)
            variant = _bias_variant(n, nb)
            outs = []
            for g in range(2):
                e_bg = e_scr.at[2 * b + g]
                qt = (_heads_to_lanes(qf, g, row) * ATTN_SCALE).astype(BF16)
                inv, _ = _softmax_keys_major(_dot(kw, qt), bias, variant, _sink_row(sink_ref, g), e_bg)
                outs.append(_dot_tn(vw, e_bg[...]) * inv)
            attn = _lanes_to_heads(outs[0], outs[1], row)
            z = z_ref[rows, :].astype(F32)
            yb_ref[rows, :] = (attn * (z * _sigmoid(z))).astype(BF16)

    tq = bps * ATTN_BLOCK
    return pl.pallas_call(
        body, name="attn_fwd", grid=(s // tq,),
        out_shape=jax.ShapeDtypeStruct((s, 512), BF16),
        in_specs=[pl.BlockSpec(memory_space=pltpu.SMEM), _rows(tq, 512), _rows(tq, 512), _full((s, 256))],
        out_specs=_rows(tq, 512),
        scratch_shapes=[pltpu.VMEM((s + 2 * ATTN_BLOCK, 128), BF16)] * 2
        + [pltpu.VMEM((3, WINDOW_KEYS, STACKED), F32),
           pltpu.VMEM((2 * bps, WINDOW_KEYS, STACKED), BF16)],
        compiler_params=_params(),
    )(sink, pq, pbz, pkv)


def _mem_softmax_t(q, mk):
    sc = _dot_nt(mk, q) * MEM_SCALE
    e = jnp.exp(sc - jnp.max(sc, axis=0, keepdims=True))
    return e * (1.0 / jnp.sum(e, axis=0, keepdims=True))


def _mem_attn_fwd(pmq, pmz, mkv):
    s = pmq.shape[0]
    m = mkv.shape[0]
    tm = min(512, s)

    def body(q_ref, z_ref, mk_ref, mv_ref, ym_ref):
        z = z_ref[...].astype(F32)
        sz = z * _sigmoid(z)
        for h in range(MEM_HEADS):
            cols = slice(128 * h, 128 * h + 128)
            pt = _mem_softmax_t(q_ref[:, cols], mk_ref[:, cols])
            o = _dot_tn(pt.astype(BF16), mv_ref[:, cols])
            ym_ref[:, cols] = (o * sz[:, cols]).astype(BF16)

    return pl.pallas_call(
        body, name="mem_attn_fwd", grid=(s // tm,),
        out_shape=jax.ShapeDtypeStruct((s, 512), BF16),
        in_specs=[_rows(tm, 512), _rows(tm, 512), pl.BlockSpec((m, 512), lambda i: (0, 0)),
                  pl.BlockSpec((m, 512), lambda i: (0, 1))],
        out_specs=_rows(tm, 512),
        compiler_params=_params(),
    )(pmq, pmz, mkv, mkv)


def _mid(ya, yb, ym, pg, x, target, g_post, w_up, w_out):
    s = x.shape[0]
    tm = min(256, s)
    nt = s // tm

    def body(ya_ref, yb_ref, ym_ref, pg_ref, x_ref, t_ref, gp_ref, wup_hbm, wout_hbm,
             dg_ref, dya_ref, dyb_ref, dym_ref, dy_ref, loss_ref, ggp_ref, mb_ref, dob_ref, du_ref,
             wup_vm, wout_vm, sems):
        i = pl.program_id(0)
        _load_once([(wup_hbm.at[d], wup_vm.at[:, pl.ds(128 * d, 128)]) for d in range(N_DEV)]
                   + [(wout_hbm, wout_vm)], sems)

        @pl.when(i == 0)
        def _():
            loss_ref[...] = jnp.zeros_like(loss_ref)
            ggp_ref[...] = jnp.zeros_like(ggp_ref)

        ys = (ya_ref[...], yb_ref[...], ym_ref[...])
        us = [_dot(ys[k], wup_vm[512 * k:512 * k + 512, :]) for k in range(3)]
        gates = [_sigmoid(pg_ref[:, 1024 * k:1024 * k + 1024].astype(F32)) for k in range(3)]
        merged = gates[0] * us[0] + gates[1] * us[1] + gates[2] * us[2]
        mb = merged.astype(BF16)
        mb_ref[...] = mb
        out = _dot(mb, wout_vm[...])
        r = lax.rsqrt(jnp.mean(out * out, axis=-1, keepdims=True) + EPS)
        on = out * r
        gp = gp_ref[...]
        err = (x_ref[...] + on * gp) - t_ref[...]
        loss_ref[...] += 0.5 * jnp.sum(err * err) * (1.0 / D_MODEL)
        dy = err * (1.0 / D_MODEL)
        dy_ref[...] = dy
        ggp_ref[...] += jnp.sum(dy * on, axis=0, keepdims=True)
        a = dy * gp
        d_out = r * (a - on * jnp.mean(a * on, axis=-1, keepdims=True))
        dob = d_out.astype(BF16)
        dob_ref[...] = dob
        d_merged = _dot_nt(dob, wout_vm[...])
        d_refs = (dya_ref, dyb_ref, dym_ref)
        for k in range(3):
            g = gates[k]
            du_f = d_merged * g
            dg_ref[:, 1024 * k:1024 * k + 1024] = (du_f * us[k] * (1.0 - g)).astype(BF16)
            du = du_f.astype(BF16)
            du_ref[k] = du
            d_refs[k][...] = _dot_nt(du, wup_vm[512 * k:512 * k + 512, :]).astype(BF16)

    return pl.pallas_call(
        body, name="mid", grid=(nt,),
        out_shape=[jax.ShapeDtypeStruct((s, 3072), BF16)] + [jax.ShapeDtypeStruct((s, 512), BF16)] * 3
        + [jax.ShapeDtypeStruct((s, D_MODEL), F32), jax.ShapeDtypeStruct((8, 128), F32),
           jax.ShapeDtypeStruct((1, D_MODEL), F32), jax.ShapeDtypeStruct((s, D_MODEL), BF16),
           jax.ShapeDtypeStruct((s, D_MODEL), BF16), jax.ShapeDtypeStruct((3, s, D_MODEL), BF16)],
        in_specs=[_rows(tm, 512)] * 3 + [_rows(tm, 3072), _rows(tm, D_MODEL), _rows(tm, D_MODEL),
                                         _full((1, D_MODEL)), ANY, ANY],
        out_specs=[_rows(tm, 3072)] + [_rows(tm, 512)] * 3
        + [_rows(tm, D_MODEL), _full((8, 128)), _full((1, D_MODEL)), _rows(tm, D_MODEL), _rows(tm, D_MODEL),
           pl.BlockSpec((3, tm, D_MODEL), lambda i: (0, i, 0))],
        scratch_shapes=[pltpu.VMEM((1536, D_MODEL), BF16), pltpu.VMEM((D_MODEL, D_MODEL), BF16),
                        pltpu.SemaphoreType.DMA((N_DEV + 1,))],
        compiler_params=_params(),
    )(ya, yb, ym, pg, x, target, g_post, w_up, w_out)


def _gw_mid(mb, dob, ys, du):
    s = mb.shape[0]
    tn = 256

    def out_body(mb_ref, dob_ref, o_ref):
        o_ref[...] = _dot_tn(mb_ref[...], dob_ref[...]).astype(BF16)

    gw_out = pl.pallas_call(
        out_body, name="gw_out", grid=(D_MODEL // tn,),
        out_shape=jax.ShapeDtypeStruct((D_MODEL, D_MODEL), BF16),
        in_specs=[pl.BlockSpec((s, tn), lambda j: (0, j)), _full((s, D_MODEL))],
        out_specs=pl.BlockSpec((tn, D_MODEL), lambda j: (j, 0)),
        compiler_params=_params(),
    )(mb, dob)

    per = 512 // tn

    def up_body(ya_ref, yb_ref, ym_ref, du_ref, o_ref):
        j = pl.program_id(0)
        for k, y_ref in enumerate((ya_ref, yb_ref, ym_ref)):
            @pl.when(j // per == k)
            def _(y_ref=y_ref):
                res = _dot_tn(y_ref[...], du_ref[...])
                for d in range(N_DEV):
                    o_ref[d] = res[:, 128 * d:128 * d + 128].astype(BF16)

    def y_spec(k):
        return pl.BlockSpec((s, tn), lambda j: (0, jnp.clip(j - per * k, 0, per - 1)))

    gw_up = pl.pallas_call(
        up_body, name="gw_up", grid=(3 * per,),
        out_shape=jax.ShapeDtypeStruct((N_DEV, 1536, 128), BF16),
        in_specs=[y_spec(0), y_spec(1), y_spec(2), pl.BlockSpec((None, s, D_MODEL), lambda j: (j // per, 0, 0))],
        out_specs=pl.BlockSpec((N_DEV, tn, 128), lambda j: (0, j, 0)),
        compiler_params=_params(),
    )(*ys, du)
    return gw_out, gw_up


def _conv_bwd(after, pa, dya, w_conv):
    s = pa.shape[0]
    tm = min(512, s)
    nt = s // tm

    def body(after_ref, pa_ref, pp_ref, pn_ref, d_ref, dp_ref, dn_ref, w_ref, da_ref, gw_ref):
        i = pl.program_id(0)
        first, last = i == 0, i == nt - 1

        @pl.when(first)
        def _():
            gw_ref[...] = jnp.zeros_like(gw_ref)

        w = w_ref[...]
        prev_row = pp_ref[...].astype(F32)[15:16, :]
        next_row = pn_ref[...].astype(F32)[0:1, :]
        b, c, u, z, cu, cu_m1, cu_p1, y, sig, row = _conv_common(
            pa_ref[...].astype(F32),
            jnp.where(first, 0.0, prev_row[:, 512:1024] * prev_row[:, 1024:1536]),
            jnp.where(last, 0.0, next_row[:, 512:1024] * next_row[:, 1024:1536]), w, tm)
        sz = z * sig
        dya_t = d_ref[...].astype(F32)
        d_y = dya_t * b * sz

        def halo_dy(p_row, d_row):
            zz = p_row[:, 1536:2048]
            return d_row * p_row[:, 0:512] * (zz * _sigmoid(zz))

        dy_prev = jnp.where(first, 0.0, halo_dy(prev_row, dp_ref[...].astype(F32)[15:16, :]))
        dy_next = jnp.where(last, 0.0, halo_dy(next_row, dn_ref[...].astype(F32)[0:1, :]))
        dy_m1 = jnp.where(row == 0, dy_prev, pltpu.roll(d_y, 1, 0))
        dy_p1 = jnp.where(row == tm - 1, dy_next, pltpu.roll(d_y, tm - 1, 0))
        d_cu = dy_p1 * w[0:1] + d_y * w[1:2] + dy_m1 * w[2:3]
        da_ref[:, 0:512] = (dya_t * y * sz).astype(BF16)
        da_ref[:, 512:1024] = (d_cu * u).astype(BF16)
        da_ref[:, 1024:1536] = (d_cu * c).astype(BF16)
        da_ref[:, 1536:2048] = (dya_t * b * y * (sig + sz * (1.0 - sig))).astype(BF16)
        gw_ref[0:1, :] += jnp.sum(d_y * cu_m1, axis=0, keepdims=True)
        gw_ref[1:2, :] += jnp.sum(d_y * cu, axis=0, keepdims=True)
        gw_ref[2:3, :] += jnp.sum(d_y * cu_p1, axis=0, keepdims=True)

    prev, nxt = _halo_specs(s, tm, 16, 2048)
    dprev, dnxt = _halo_specs(s, tm, 16, 512)
    return pl.pallas_call(
        body, name="conv_bwd", grid=(nt,),
        out_shape=[jax.ShapeDtypeStruct((s, 2048), BF16), jax.ShapeDtypeStruct((8, 512), F32)],
        in_specs=[_full(TOKEN), _rows(tm, 2048), prev, nxt, _rows(tm, 512), dprev, dnxt, _full((3, 512))],
        out_specs=[_rows(tm, 2048), _full((8, 512))],
        compiler_params=_params(),
    )(after, pa, pa, pa, dya, dya, dya, w_conv)


def _attn_bwd(after, pq, pkv, pbz, dyb, sink, tabs):
    s = pq.shape[0]
    nb = s // ATTN_BLOCK
    bps = min(MAX_BLOCKS_IN_STEP, nb)

    def body(sink_ref, after_ref, q_ref, z_ref, d_ref, kv_ref, t_ref,
             dq_ref, dz_ref, dkv_ref, gs_ref, kpad, vpad, dk_acc, dv_acc, bias, e_scr, ds_scr):
        i = pl.program_id(0)

        @pl.when(i == 0)
        def _():
            _fill_padded(kv_ref, kpad, vpad, s)
            _fill_band_bias(bias, nb)
            dk_acc[...] = jnp.zeros_like(dk_acc)
            dv_acc[...] = jnp.zeros_like(dv_acc)
            gs_ref[...] = jnp.zeros_like(gs_ref)

        row = lax.broadcasted_iota(jnp.int32, (ATTN_BLOCK, 128), 0)
        for b in range(bps):
            n = i * bps + b
            rows = slice(b * ATTN_BLOCK, (b + 1) * ATTN_BLOCK)
            start = pl.multiple_of(n * ATTN_BLOCK, ATTN_BLOCK)
            kw, vw = kpad[pl.ds(start, WINDOW_KEYS), :], vpad[pl.ds(start, WINDOW_KEYS), :]
            qf = q_ref[rows, :].astype(F32)
            variant = _bias_variant(n, nb)
            z = z_ref[rows, :].astype(F32)
            sig = _sigmoid(z)
            dyb_t = d_ref[rows, :].astype(F32)
            d_attn = dyb_t * (z * sig)
            outs, dqs = [], []
            dk_w = jnp.zeros((WINDOW_KEYS, 128), F32)
            dv_w = jnp.zeros((WINDOW_KEYS, 128), F32)
            for g in range(2):
                e_bg, ds_bg = e_scr.at[2 * b + g], ds_scr.at[2 * b + g]
                qt = _heads_to_lanes(qf, g, row)
                inv, p_sink = _softmax_keys_major(
                    _dot(kw, (qt * ATTN_SCALE).astype(BF16)), bias, variant, _sink_row(sink_ref, g), e_bg)
                ot = _dot_tn(vw, e_bg[...]) * inv
                outs.append(ot)
                dot_ = _heads_to_lanes(d_attn, g, row)
                delta = jnp.sum(dot_ * ot, axis=0, keepdims=True)
                dpt = _dot(vw, dot_.astype(BF16))
                for k in range(WINDOW_KEYS // KEY_CHUNK):
                    rw = slice(k * KEY_CHUNK, (k + 1) * KEY_CHUNK)
                    ds_bg[rw, :] = (e_bg[rw, :].astype(F32) * (dpt[rw] - delta)).astype(BF16)
                sink_part = p_sink * delta
                for j in range(4):
                    h = 4 * g + j
                    gs_ref[h:h + 1, :] -= jnp.sum(sink_part[:, 128 * j:128 * j + 128])
                dqs.append(_dot_tn(kw, ds_bg[...]) * (inv * ATTN_SCALE))
                dk_w += _dot_nt(ds_bg[...], (qt * inv).astype(BF16)) * ATTN_SCALE
                dv_w += _dot_nt(e_bg[...], (dot_ * inv).astype(BF16))
            dk_acc[pl.ds(start, WINDOW_KEYS), :] += dk_w
            dv_acc[pl.ds(start, WINDOW_KEYS), :] += dv_w
            attn = _lanes_to_heads(outs[0], outs[1], row)
            dz_ref[rows, :] = (dyb_t * attn * (sig * (1.0 + z * (1.0 - sig)))).astype(BF16)
            dq = _lanes_to_heads(dqs[0], dqs[1], row)
            trows = pl.ds(start, ATTN_BLOCK)
            cs, s1, s2 = t_ref[0, trows, :], t_ref[1, trows, :], t_ref[2, trows, :]
            for blk in range(4):
                cols = slice(128 * blk, 128 * blk + 128)
                dq_ref[rows, cols] = _rope_t(dq[:, cols], cs, s1, s2).astype(BF16)

        @pl.when(i == nb // bps - 1)
        def _():
            dk = dk_acc[ATTN_BLOCK:ATTN_BLOCK + s, :]
            dkv_ref[:, 0:128] = _rope_t(dk, t_ref[0], t_ref[1], t_ref[2]).astype(BF16)
            dkv_ref[:, 128:256] = dv_acc[ATTN_BLOCK:ATTN_BLOCK + s, :].astype(BF16)

    tq = bps * ATTN_BLOCK
    tile = _rows(tq, 512)
    return pl.pallas_call(
        body, name="attn_bwd", grid=(s // tq,),
        out_shape=[jax.ShapeDtypeStruct((s, 512), BF16), jax.ShapeDtypeStruct((s, 512), BF16),
                   jax.ShapeDtypeStruct((s, 256), BF16), jax.ShapeDtypeStruct((8, 128), F32)],
        in_specs=[pl.BlockSpec(memory_space=pltpu.SMEM), _full(TOKEN), tile, tile, tile, _full((s, 256)),
                  _full((3, s, 128))],
        out_specs=[tile, tile, _full((s, 256)), _full((8, 128))],
        scratch_shapes=[pltpu.VMEM((s + 2 * ATTN_BLOCK, 128), BF16)] * 2
        + [pltpu.VMEM((s + 2 * ATTN_BLOCK, 128), F32)] * 2
        + [pltpu.VMEM((3, WINDOW_KEYS, STACKED), F32)]
        + [pltpu.VMEM((2 * bps, WINDOW_KEYS, STACKED), BF16)] * 2,
        compiler_params=_params(),
    )(sink, after, pq, pbz, dyb, pkv, tabs)


def _mem_attn_bwd(pmq, pmz, mkv, dym):
    s = pmq.shape[0]
    m = mkv.shape[0]
    tm = min(512, s)

    def body(q_ref, z_ref, d_ref, mk_ref, mv_ref, dq_ref, dz_ref, dmkv_ref):
        @pl.when(pl.program_id(0) == 0)
        def _():
            dmkv_ref[...] = jnp.zeros_like(dmkv_ref)

        z = z_ref[...].astype(F32)
        sig = _sigmoid(z)
        dym_t = d_ref[...].astype(F32)
        d_attn = dym_t * (z * sig)
        dsilu = sig * (1.0 + z * (1.0 - sig))
        for h in range(MEM_HEADS):
            cols = slice(128 * h, 128 * h + 128)
            q, mk, mv = q_ref[:, cols], mk_ref[:, cols], mv_ref[:, cols]
            pt = _mem_softmax_t(q, mk)
            pb = pt.astype(BF16)
            o = _dot_tn(pb, mv)
            dob = d_attn[:, cols].astype(BF16)
            dpt = _dot_nt(mv, dob)
            dst = (pt * (dpt - jnp.sum(pt * dpt, axis=0, keepdims=True))).astype(BF16)
            dq_ref[:, cols] = (_dot_tn(dst, mk) * MEM_SCALE).astype(BF16)
            dz_ref[:, cols] = (dym_t[:, cols] * o * dsilu[:, cols]).astype(BF16)
            dmkv_ref[:, cols] += _dot(dst, q) * MEM_SCALE
            dmkv_ref[:, 512 + 128 * h:512 + 128 * h + 128] += _dot(pb, dob)

    return pl.pallas_call(
        body, name="mem_attn_bwd", grid=(s // tm,),
        out_shape=[jax.ShapeDtypeStruct((s, 512), BF16), jax.ShapeDtypeStruct((s, 512), BF16),
                   jax.ShapeDtypeStruct((m, D_MODEL), F32)],
        in_specs=[_rows(tm, 512), _rows(tm, 512), _rows(tm, 512), pl.BlockSpec((m, 512), lambda i: (0, 0)),
                  pl.BlockSpec((m, 512), lambda i: (0, 1))],
        out_specs=[_rows(tm, 512), _rows(tm, 512), _full((m, D_MODEL))],
        compiler_params=_params(),
    )(pmq, pmz, dym, mkv, mkv)


def _mem_kv_bwd(mem, g_mem, mn, dmkv, w_mkv):
    m = mem.shape[0]

    def body(mem_ref, g_ref, mn_ref, d_ref, w_ref, gw_ref, gg_ref):
        db = d_ref[...].astype(BF16)
        gw_ref[...] = _dot_tn(mn_ref[...], db).astype(BF16)
        d_mn = _dot_nt(db, w_ref[...])
        xf = mem_ref[...]
        r = lax.rsqrt(jnp.mean(xf * xf, axis=-1, keepdims=True) + EPS)
        gg_ref[...] = jnp.sum(d_mn * (xf * r), axis=0, keepdims=True)

    return pl.pallas_call(
        body, name="mem_kv_bwd", grid=(1,),
        out_shape=[jax.ShapeDtypeStruct((D_MODEL, D_MODEL), BF16), jax.ShapeDtypeStruct((1, D_MODEL), F32)],
        in_specs=[_full((m, D_MODEL)), _full((1, D_MODEL)), _full((m, D_MODEL)), _full((m, D_MODEL)),
                  _full((D_MODEL, D_MODEL))],
        out_specs=[_full((D_MODEL, D_MODEL)), _full((1, D_MODEL))],
        compiler_params=_params(),
    )(mem, g_mem, mn, dmkv, w_mkv)


def _dh_bwd(after, dparts, x, dy, g_pre, w_int):
    s = x.shape[0]
    tm = min(256, s)

    def body(after_ref, *refs):
        d_refs = refs[:7]
        x_ref, dy_ref, g_ref, w_hbm, gx_ref, gg_ref, w_vm, sems = refs[7:]
        _load_once([(w_hbm, w_vm)], sems)

        @pl.when(pl.program_id(0) == 0)
        def _():
            gg_ref[...] = jnp.zeros_like(gg_ref)

        d_h = jnp.zeros((tm, D_MODEL), F32)
        for d_ref, (r0, width) in zip(d_refs, SEGS):
            for c0 in range(0, width, 512):
                cw = min(512, width - c0)
                d_h += _dot(d_ref[:, c0:c0 + cw], w_vm[r0 + c0:r0 + c0 + cw, :])
        xf = x_ref[...]
        r = lax.rsqrt(jnp.mean(xf * xf, axis=-1, keepdims=True) + EPS)
        xn = xf * r
        a = d_h * g_ref[...]
        gx_ref[...] = r * (a - xn * jnp.mean(a * xn, axis=-1, keepdims=True)) + dy_ref[...]
        gg_ref[...] += jnp.sum(d_h * xn, axis=0, keepdims=True)

    return pl.pallas_call(
        body, name="dh_bwd", grid=(s // tm,),
        out_shape=[jax.ShapeDtypeStruct((s, D_MODEL), F32), jax.ShapeDtypeStruct((1, D_MODEL), F32)],
        in_specs=[_full(TOKEN)] + [_rows(tm, w) for _, w in SEGS]
        + [_rows(tm, D_MODEL), _rows(tm, D_MODEL), _full((1, D_MODEL)), ANY],
        out_specs=[_rows(tm, D_MODEL), _full((1, D_MODEL))],
        scratch_shapes=[pltpu.VMEM((IN_WIDTH, D_MODEL), BF16), pltpu.SemaphoreType.DMA((1,))],
        compiler_params=_params(),
    )(after, *dparts, x, dy, g_pre, w_int)


def _gw_in(dparts, h):
    s = h.shape[0]
    tn = 256
    starts, counts = [], []
    for r0, width in SEGS:
        starts.append(r0 // tn)
        counts.append(width // tn)

    def body(*refs):
        d_refs = refs[:7]
        h_hbm, o_ref, h_vm, sems = refs[7:]
        _load_once([(h_hbm, h_vm)], sems)
        j = pl.program_id(0)
        for d_ref, st, cnt in zip(d_refs, starts, counts):
            @pl.when((j >= st) & (j < st + cnt))
            def _(d_ref=d_ref):
                o_ref[...] = _dot_tn(d_ref[...], h_vm[...]).astype(BF16)

    def seg_spec(st, cnt):
        return pl.BlockSpec((s, tn), lambda j: (0, jnp.clip(j - st, 0, cnt - 1)))

    return pl.pallas_call(
        body, name="gw_in", grid=(IN_WIDTH // tn,),
        out_shape=jax.ShapeDtypeStruct((IN_WIDTH, D_MODEL), BF16),
        in_specs=[seg_spec(st, cnt) for st, cnt in zip(starts, counts)] + [ANY],
        out_specs=pl.BlockSpec((tn, D_MODEL), lambda j: (j, 0)),
        scratch_shapes=[pltpu.VMEM((s, D_MODEL), BF16), pltpu.SemaphoreType.DMA((1,))],
        compiler_params=_params(),
    )(*dparts, h)


def _adamw_math(w, g, m, v):
    m2 = ADAM_B1 * m + (1.0 - ADAM_B1) * g
    v2 = ADAM_B2 * v + (1.0 - ADAM_B2) * (g * g)
    m_hat = m2 / (1.0 - ADAM_B1 ** ADAM_STEP)
    v_hat = v2 / (1.0 - ADAM_B2 ** ADAM_STEP)
    delta = -ADAM_LR * (m_hat / (jnp.sqrt(v_hat) + ADAM_EPS) + ADAM_WD * w)
    return delta, m2, v2


def _sum_adamw(after, own, land, chip, block, w, m, v, name, tiles=1):
    r, c = w.shape
    rt = r // tiles

    def body(c_ref, after_ref, own_ref, l1_ref, l2_ref, l3_ref, w_ref, m_ref, v_ref, g_ref, d_ref, m2_ref, v2_ref):
        g = own_ref[...].astype(F32)
        for l_ref in (l1_ref, l2_ref, l3_ref):
            g += l_ref[...].astype(F32)
        g_ref[...] = g
        d_ref[...], m2_ref[...], v2_ref[...] = _adamw_math(w_ref[...], g, m_ref[...], v_ref[...])

    def share(k):
        return pl.BlockSpec((None, rt, c), lambda i, c_ref: (jnp.bitwise_xor(c_ref[0], k), block * tiles + i, 0))

    spec = pl.BlockSpec((rt, c), lambda i, c_ref: (i, 0))
    grid_spec = pltpu.PrefetchScalarGridSpec(
        num_scalar_prefetch=1, grid=(tiles,),
        in_specs=[ANY, share(0), share(1), share(2), share(3)] + [spec] * 3, out_specs=[spec] * 4)
    return pl.pallas_call(
        body, name=name, grid_spec=grid_spec,
        out_shape=[jax.ShapeDtypeStruct((r, c), F32)] * 4,
        compiler_params=_params(),
    )(chip, after, own, land, land, land, w, m, v)


def _sum_adamw_group(after, items, chip, name):
    k = len(items)

    def body(c_ref, after_ref, *refs):
        shares, wmv, outs = refs[:4 * k], refs[4 * k:7 * k], refs[7 * k:]
        for j in range(k):
            g = shares[4 * j][...].astype(F32)
            for l_ref in shares[4 * j + 1:4 * j + 4]:
                g += l_ref[...].astype(F32)
            outs[4 * j][...] = g
            outs[4 * j + 1][...], outs[4 * j + 2][...], outs[4 * j + 3][...] = _adamw_math(
                wmv[3 * j][...], g, wmv[3 * j + 1][...], wmv[3 * j + 2][...])

    def share(shape, block, q):
        return pl.BlockSpec((None,) + shape, lambda i, c_ref: (jnp.bitwise_xor(c_ref[0], q), block, 0))

    in_specs, args = [ANY], [after]
    for own, land, block, w, m, v in items:
        in_specs += [share(w.shape, block, q) for q in range(4)]
        args += [own, land, land, land]
    for own, land, block, w, m, v in items:
        in_specs += [pl.BlockSpec(w.shape, lambda i, c_ref: (0, 0))] * 3
        args += [w, m, v]
    out_specs = [pl.BlockSpec(w.shape, lambda i, c_ref: (0, 0)) for _, _, _, w, _, _ in items for _ in range(4)]
    res = pl.pallas_call(
        body, name=name,
        grid_spec=pltpu.PrefetchScalarGridSpec(num_scalar_prefetch=1, grid=(1,), in_specs=in_specs,
                                               out_specs=out_specs),
        out_shape=[jax.ShapeDtypeStruct(w.shape, F32) for _, _, _, w, _, _ in items for _ in range(4)],
        compiler_params=_params(),
    )(chip, *args)
    return [res[4 * j:4 * j + 4] for j in range(k)]


def _small_pack(parts):
    def pack_body(gpre_ref, gconv_ref, gsink_ref, gmem_ref, gpost_ref, loss_ref, pack):
        lane = lax.broadcasted_iota(jnp.int32, (1, 128), 1)
        sink_row = jnp.zeros((1, 128), F32)
        for h in range(8):
            sink_row = jnp.where(lane == h, gsink_ref[h:h + 1, :], sink_row)
        pack[...] = jnp.zeros_like(pack)
        pack[0:1, :] = gpre_ref[...]
        pack[1:2, :] = gmem_ref[...]
        pack[2:3, :] = gpost_ref[...]
        pack[3:6, 0:512] = gconv_ref[0:3, :]
        pack[6:7, 0:128] = sink_row
        pack[7:8, 0:128] = loss_ref[0:1, :]

    return pl.pallas_call(
        pack_body, name="small_pack", grid=(1,),
        out_shape=jax.ShapeDtypeStruct((8, D_MODEL), F32),
        in_specs=[_full(p.shape) for p in parts], out_specs=_full((8, D_MODEL)),
    )(*parts)


def _small_apply(packs, ws, ms, vs):
    def apply(p_ref, *refs):
        w_refs, m_refs, v_refs = refs[0:5], refs[5:10], refs[10:15]
        loss_out = refs[15]
        g_outs, d_outs, m_outs, v_outs = refs[16:21], refs[21:26], refs[26:31], refs[31:36]
        x, y, c = _my_place()
        tot = p_ref[0]
        for d in range(1, N_DEV):
            tot = tot + p_ref[d]
        conv = pltpu.roll(tot[:, 0:512], (512 - 64 * (4 * x + 2 * y + c)) % 512, 1)[3:6, 0:64]
        grads = (tot[0:1, :], conv, tot[6:7, 0:8], tot[1:2, :], tot[2:3, :])
        loss_out[...] = tot[7:8, 0:128]
        for j in range(5):
            g_outs[j][...] = grads[j]
            d_outs[j][...], m_outs[j][...], v_outs[j][...] = _adamw_math(
                w_refs[j][...], grads[j], m_refs[j][...], v_refs[j][...])

    specs = [_full(w.shape) for w in ws]
    res = pl.pallas_call(
        apply, name="small_apply", grid=(1,),
        out_shape=[jax.ShapeDtypeStruct((1, 128), F32)] + [jax.ShapeDtypeStruct(w.shape, F32) for w in ws] * 4,
        in_specs=[_full((N_DEV, 8, D_MODEL))] + specs * 3,
        out_specs=[_full((1, 128))] + specs * 4,
    )(packs, *ws, *ms, *vs)
    return res[0], res[1:6], res[6:11], res[11:16], res[16:21]


def kernel(x, mem, g_pre, w_in, w_conv, attn_sink, g_mem, w_mem_kv, w_up_a, w_up_b, w_up_m, w_out, g_post, loss_target, m_g_pre, m_w_in, m_w_conv, m_attn_sink, m_g_mem, m_w_mem_kv, m_w_up_a, m_w_up_b, m_w_up_m, m_w_out, m_g_post, v_g_pre, v_w_in, v_w_conv, v_attn_sink, v_g_mem, v_w_mem_kv, v_w_up_a, v_w_up_b, v_w_up_m, v_w_out, v_g_post):
    s = x.shape[1]
    x2, mem2, tgt2 = x[0], mem[0], loss_target[0]
    me = 4 * lax.axis_index("x") + 2 * lax.axis_index("y") + lax.axis_index("c")

    w_conv_loc = jnp.zeros((8, 128), F32).at[:3, :64].set(w_conv[0])
    w_int_g, w_conv_g, tabs, w_mkv_loc, w_out_loc, w_up_loc = _all_gather(
        [w_in[0].T.astype(BF16), w_conv_loc], "gather_w_in",
        splits=[[(112 * k, 112) for k in range(7)] + [(784, 144)], [(0, 8)]],
        side=_gather_side(s, w_mem_kv[0], w_out[0], (w_up_a[0], w_up_b[0], w_up_m[0])))
    w_int = w_int_g.reshape(IN_WIDTH, D_MODEL)
    w_conv_f = w_conv_g[:, :3, :64].transpose(1, 0, 2).reshape(3, 512)
    late = _gather_start([w_mkv_loc, w_out_loc, w_up_loc], me, "gather_late_start")
    sink = attn_sink[0]

    h, pa, pq, pkv, pbz, pmq, pmz, pg = _proj_fwd(late[4], x2, g_pre, w_int, tabs)
    ya = _conv_fwd(pa, w_conv_f)
    yb = _attn_fwd(pq, pkv, pbz, sink)
    w_mkv_g, w_out_g, w_up_g = _gather_wait(*late[:4], yb, "gather_late_wait")
    w_mkv = w_mkv_g.reshape(D_MODEL, D_MODEL)
    w_out_f = w_out_g.reshape(D_MODEL, D_MODEL)
    mn, mkv = _mem_kv_fwd(mem2, g_mem, w_mkv)
    ym = _mem_attn_fwd(pmq, pmz, mkv)
    dg, dya, dyb, dym, dy, loss_p, gg_post, mb, dob, du = _mid(ya, yb, ym, pg, x2, tgt2, g_post, w_up_g, w_out_f)
    gw_out, gw_up = _gw_mid(mb, dob, (ya, yb, ym), du)

    core = lax.axis_index("c").astype(jnp.int32).reshape(1)
    chip = (2 * lax.axis_index("x") + lax.axis_index("y")).astype(jnp.int32).reshape(1)

    dmq, dmz, dmkv = _mem_attn_bwd(pmq, pmz, mkv, dym)
    gw_mkv, gg_mem = _mem_kv_bwd(mem2, g_mem, mn, dmkv, w_mkv)
    shares1 = [gw_mkv.reshape(N_DEV, 128, D_MODEL), gw_out.reshape(N_DEV, 128, D_MODEL), gw_up]
    sib = _split_start(_sibling_copies, N_CHIPS, shares1,
                       [lax.empty((N_CHIPS,) + a.shape[1:], a.dtype) for a in shares1], "grads_to_sibling_small_start")
    da, gw_conv = _conv_bwd(sib[4], pa, dya, w_conv_f)
    shares1, from_sibling = _split_wait(_sibling_copies, N_CHIPS, *sib[:4], da, "grads_to_sibling_small_wait")
    send1, recv1, srcs1, lands1, token1 = _chip_exchange_start(
        _pair_add(shares1, from_sibling, core, "grads_pair_add_small"), "grads_to_chips_start_small")
    dq, dbz, dkv, g_sink = _attn_bwd(token1, pq, pkv, pbz, dyb, sink, tabs)
    dparts = (da, dq, dkv, dbz, dmq, dmz, dg)
    gw_int = _gw_in(dparts, h)
    send2, recv2, srcs2, lands2, token2 = _chip_exchange_start(
        [_sibling_reduce(gw_int.reshape(N_DEV, SHARD_IN, D_MODEL), "grads_sibling_reduce_w_in")],
        "grads_to_chips_start_w_in")
    grad_x, gg_pre = _dh_bwd(token2, dparts, x2, dy, g_pre, w_int)
    (o_mkv, o_out, o_up, o_int), (l_mkv, l_out, l_up, l_int) = _chip_exchange_wait(
        send1 + send2, recv1 + recv2, srcs1 + srcs2, lands1 + lands2, grad_x, "grads_to_chips_wait")

    small = _gather_start([_small_pack((gg_pre, gw_conv, g_sink, gg_mem, gg_post, loss_p))], me,
                          "small_gather_start")

    w_in_t = _sum_adamw(small[4], o_int, l_int, chip, 0, w_in[0].T, m_w_in[0].T, v_w_in[0].T, "adamw_w_in", tiles=2)
    g_w_in, d_w_in, nm_w_in, nv_w_in = (t.T for t in w_in_t)
    (g_mkv, d_mkv, nm_mkv, nv_mkv), (g_out, d_out, nm_out, nv_out), *up = _sum_adamw_group(
        w_in_t[0],
        [(o_mkv, l_mkv, 0, w_mem_kv[0], m_w_mem_kv[0], v_w_mem_kv[0]),
         (o_out, l_out, 0, w_out[0], m_w_out[0], v_w_out[0]),
         (o_up, l_up, 0, w_up_a[0], m_w_up_a[0], v_w_up_a[0]),
         (o_up, l_up, 1, w_up_b[0], m_w_up_b[0], v_w_up_b[0]),
         (o_up, l_up, 2, w_up_m[0], m_w_up_m[0], v_w_up_m[0])], chip, "adamw_mid_weights")

    (packs,) = _gather_wait(*small[:4], g_out, "small_gather_wait")
    loss_row, small_g, sd, sm, sv = _small_apply(
        packs, [g_pre, w_conv[0], attn_sink, g_mem, g_post],
        [m_g_pre, m_w_conv[0], m_attn_sink, m_g_mem, m_g_post],
        [v_g_pre, v_w_conv[0], v_attn_sink, v_g_mem, v_g_post])
    loss = loss_row[0, 0]
    g_g_pre, g_conv, g_sink_tot, g_g_mem, g_g_post = small_g

    def lead(a):
        return a[None]

    grads = [g_g_pre, lead(g_w_in), lead(g_conv), g_sink_tot, g_g_mem, lead(g_mkv), lead(up[0][0]),
             lead(up[1][0]), lead(up[2][0]), lead(g_out), g_g_post]

    def assemble(small, big_in, big_mkv, big_up, big_out):
        return [small[0], lead(big_in), lead(small[1]), small[2], small[3], lead(big_mkv), lead(big_up[0]),
                lead(big_up[1]), lead(big_up[2]), lead(big_out), small[4]]

    deltas = assemble(sd, d_w_in, d_mkv, [u[1] for u in up], d_out)
    new_m = assemble(sm, nm_w_in, nm_mkv, [u[2] for u in up], nm_out)
    new_v = assemble(sv, nv_w_in, nv_mkv, [u[3] for u in up], nv_out)
    return (loss, grad_x[None], *grads, *deltas, *new_m, *new_v)
```

```python
import functools

import jax
import jax.numpy as jnp
from jax import lax
from jax.experimental import pallas as pl
from jax.experimental.pallas import tpu as pltpu

F32 = jnp.float32
BF16 = jnp.bfloat16
MESH = pl.DeviceIdType.MESH

N_DEV = 8
D_MODEL = 1024
EPS = 1e-6
ROPE_THETA = 500000.0
ROT_DIM = 16
HEAD_DIM = 64
ATTN_BLOCK = 128
MEM_HEADS = 4
MEM_HEAD_DIM = 128
ATTN_SCALE = HEAD_DIM ** -0.5
MEM_SCALE = MEM_HEAD_DIM ** -0.5

ADAM_LR = 0.001
ADAM_B1 = 0.9
ADAM_B2 = 0.999
ADAM_EPS = 1e-08
ADAM_WD = 0.01
ADAM_STEP = 10

SEG_A = (0, 2048)
SEG_BQ = (2048, 512)
SEG_BKV = (2560, 256)
SEG_BZ = (2816, 512)
SEG_MQ = (3328, 512)
SEG_MZ = (3840, 512)
SEG_G = (4352, 3072)
SEGS = (SEG_A, SEG_BQ, SEG_BKV, SEG_BZ, SEG_MQ, SEG_MZ, SEG_G)
IN_WIDTH = 7424
SHARD_IN = IN_WIDTH // N_DEV

V7X_VMEM_BYTES = 64 * 1024 * 1024
CALL_VMEM_MB = 57
ANY = pl.BlockSpec(memory_space=pl.ANY)


def _params():
    assert CALL_VMEM_MB * 1024 * 1024 < V7X_VMEM_BYTES
    return pltpu.CompilerParams(dimension_semantics=("arbitrary",), vmem_limit_bytes=CALL_VMEM_MB * 1024 * 1024)


def _full(shape):
    zeros = (0,) * len(shape)
    return pl.BlockSpec(shape, lambda i: zeros)


def _rows(tm, width):
    return pl.BlockSpec((tm, width), lambda i: (i, 0))


def _dot(a, b):
    return jnp.dot(a, b, preferred_element_type=F32)


def _dot_nt(a, b):
    return lax.dot_general(a, b, (((1,), (1,)), ((), ())), preferred_element_type=F32)


def _dot_tn(a, b):
    return lax.dot_general(a, b, (((0,), (0,)), ((), ())), preferred_element_type=F32)


def _sigmoid(z):
    return 1.0 / (1.0 + jnp.exp(-z))


def _rope(t, cs, s1, s2):
    return t * cs + pltpu.roll(t, 120, 1) * s1 + pltpu.roll(t, 8, 1) * s2


def _rope_t(d, cs, s1, s2):
    return d * cs + pltpu.roll(d * s1, 8, 1) + pltpu.roll(d * s2, 120, 1)


def _gather_side(s, w_mkv, w_out, w_ups):
    half = ROT_DIM // 2
    inv_freq = jnp.power(jnp.float32(ROPE_THETA), -jnp.arange(half, dtype=F32) * (2.0 / ROT_DIM))
    freq_row = jnp.tile(jnp.concatenate([inv_freq, inv_freq, jnp.zeros((HEAD_DIM - ROT_DIM,), F32)]), 2)[None, :]

    def fn(in_refs, out_refs):
        f_ref, mkv_ref, out_ref, *up_refs = in_refs
        t_ref, mkv_bf, out_bf, up_bf = out_refs
        mkv_bf[...] = mkv_ref[...].astype(BF16)
        out_bf[...] = out_ref[...].astype(BF16)
        for k, up_ref in enumerate(up_refs):
            up_bf[512 * k:512 * k + 512, :] = up_ref[...].astype(BF16)
        pos = lax.broadcasted_iota(jnp.int32, (s, 128), 0).astype(F32)
        d = lax.broadcasted_iota(jnp.int32, (s, 128), 1) & (HEAD_DIM - 1)
        ang = pos * f_ref[...]
        cos, sin = jnp.cos(ang), jnp.sin(ang)
        lo, hi = d < half, (d >= half) & (d < ROT_DIM)
        t_ref[0] = jnp.where(lo | hi, cos, 1.0)
        t_ref[1] = jnp.where(lo, -sin, 0.0)
        t_ref[2] = jnp.where(hi, sin, 0.0)

    return ([freq_row, w_mkv, w_out, *w_ups],
            [jax.ShapeDtypeStruct((3, s, 128), F32), jax.ShapeDtypeStruct(w_mkv.shape, BF16),
             jax.ShapeDtypeStruct(w_out.shape, BF16), jax.ShapeDtypeStruct((1536, 128), BF16)], fn)


def _load_once(pairs, sems):
    @pl.when(pl.program_id(0) == 0)
    def _():
        cps = [pltpu.make_async_copy(src, dst, sems.at[k]) for k, (src, dst) in enumerate(pairs)]
        for cp in cps:
            cp.start()
        for cp in cps:
            cp.wait()


def _my_place():
    x, y, c = lax.axis_index("x"), lax.axis_index("y"), lax.axis_index("c")
    return x, y, c


def _all_gather(arrs, name, splits=None, side=None):
    n = len(arrs)
    if splits is None:
        splits = [[(0, a.shape[0])] for a in arrs]
    pieces = [(a, r0, rn) for a in range(n) for r0, rn in splits[a]]
    n_p = len(pieces)
    side_in, side_out, side_fn = side if side is not None else ((), (), None)
    m, q = len(side_in), len(side_out)

    def body(*refs):
        ins, outs = refs[:n], refs[n + m:2 * n + m]
        send_sems, recv_sems, local_sems = refs[2 * n + m + q:]
        x, y, c = _my_place()
        me, sibling = (x, y, c), (x, y, 1 - c)

        def route(core):
            first = (jnp.bitwise_xor(x, 1 - core), jnp.bitwise_xor(y, core), core)
            second = (jnp.bitwise_xor(x, core), jnp.bitwise_xor(y, 1 - core), core)
            return first, second, (1 - x, 1 - y, core)

        def idx(px, py, pc):
            return 4 * px + 2 * py + pc

        def copy(p, k, block, to, own=False):
            a, r0, rn = pieces[p]
            dst = outs[a].at[idx(*block), pl.ds(r0, rn)]
            return pltpu.make_async_remote_copy(
                src_ref=ins[a].at[pl.ds(r0, rn)] if own else dst, dst_ref=dst,
                send_sem=send_sems.at[p * 7 + k], recv_sem=recv_sems.at[p * 7 + k],
                device_id=to, device_id_type=MESH)

        nbr1, nbr2, diag = route(c)
        mine = [pltpu.make_async_copy(ins[a], outs[a].at[idx(*me)], local_sems.at[a]) for a in range(n)]
        for cp in mine:
            cp.start()
        sent = []
        for p in range(n_p):
            for k, to in enumerate((sibling, nbr1, nbr2)):
                sent.append(copy(p, k, me, to, own=True))
        for cp in sent:
            cp.start()
        if side_fn is not None:
            side_fn(refs[n:n + m], refs[2 * n + m:2 * n + m + q])
        for k_in, block, onward in ((1, nbr1, ((3, nbr2), (4, sibling))), (2, nbr2, ((5, sibling),)),
                                    (3, diag, ((6, sibling),))):
            for p in range(n_p):
                copy(p, k_in, block, me).wait_recv()
                for k_out, to in onward:
                    cp = copy(p, k_out, block, to)
                    cp.start()
                    sent.append(cp)
        s1, s2, sd = route(1 - c)
        for k_in, block in ((0, sibling), (4, s1), (5, s2), (6, sd)):
            for p in range(n_p):
                copy(p, k_in, block, me).wait_recv()
        for cp in sent:
            cp.wait_send()
        for cp in mine:
            cp.wait()

    return pl.pallas_call(
        body, name=name,
        out_shape=[jax.ShapeDtypeStruct((N_DEV,) + a.shape, a.dtype) for a in arrs] + list(side_out),
        in_specs=[ANY] * n + [pl.BlockSpec(memory_space=pltpu.VMEM)] * m,
        out_specs=[ANY] * n + [pl.BlockSpec(memory_space=pltpu.VMEM)] * q,
        scratch_shapes=[pltpu.SemaphoreType.DMA((7 * n_p,)), pltpu.SemaphoreType.DMA((7 * n_p,)),
                        pltpu.SemaphoreType.DMA((n,))],
        compiler_params=pltpu.CompilerParams(vmem_limit_bytes=32 * 1024 * 1024),
    )(*arrs, *side_in)


N_CHIPS = 4


def _sibling_reduce(arr, name):
    _, r, c = arr.shape

    def body(in_ref, out_ref, land, a_buf, b_buf, o_buf, send_sems, recv_sems, local_sems):
        x, y, core = _my_place()
        cps = [pltpu.make_async_remote_copy(
            src_ref=in_ref.at[2 * j + (1 - core)], dst_ref=land.at[j], send_sem=send_sems.at[j],
            recv_sem=recv_sems.at[j], device_id=(x, y, 1 - core), device_id_type=MESH) for j in range(N_CHIPS)]
        for cp in cps:
            cp.start()
        store = None
        for j in range(N_CHIPS):
            mine = pltpu.make_async_copy(in_ref.at[2 * j + core], a_buf, local_sems.at[0])
            mine.start()
            cps[j].wait_recv()
            theirs = pltpu.make_async_copy(land.at[j], b_buf, local_sems.at[1])
            theirs.start()
            mine.wait()
            theirs.wait()
            if store is not None:
                store.wait()
            o_buf[...] = (a_buf[...].astype(F32) + b_buf[...].astype(F32)).astype(BF16)
            store = pltpu.make_async_copy(o_buf, out_ref.at[j], local_sems.at[2])
            store.start()
        store.wait()
        for cp in cps:
            cp.wait_send()

    return pl.pallas_call(
        body, name=name,
        out_shape=[jax.ShapeDtypeStruct((N_CHIPS, r, c), BF16)] * 2,
        in_specs=[ANY], out_specs=[ANY, ANY],
        scratch_shapes=[pltpu.VMEM((r, c), BF16)] * 3
        + [pltpu.SemaphoreType.DMA((N_CHIPS,)), pltpu.SemaphoreType.DMA((N_CHIPS,)), pltpu.SemaphoreType.DMA((3,))],
        compiler_params=pltpu.CompilerParams(vmem_limit_bytes=32 * 1024 * 1024),
    )(arr)[0]


def _sibling_copies(srcs, lands, send_sems, recv_sems):
    x, y, c = _my_place()
    cps = []
    for j in range(N_CHIPS):
        for a in range(len(srcs)):
            k = a * N_CHIPS + j
            cps.append(pltpu.make_async_remote_copy(
                src_ref=srcs[a].at[2 * j + (1 - c)], dst_ref=lands[a].at[j], send_sem=send_sems[k],
                recv_sem=recv_sems[k], device_id=(x, y, 1 - c), device_id_type=MESH))
    return cps


def _pair_add(mine, recv, core, name):
    n = len(mine)

    def body(c_ref, *refs):
        for a in range(n):
            refs[2 * n + a][...] = (refs[a][...].astype(F32) + refs[n + a][...].astype(F32)).astype(BF16)

    def blk(a):
        return (None,) + a.shape[1:]

    grid_spec = pltpu.PrefetchScalarGridSpec(
        num_scalar_prefetch=1, grid=(N_CHIPS,),
        in_specs=[pl.BlockSpec(blk(a), lambda j, c_ref: (2 * j + c_ref[0], 0, 0)) for a in mine]
        + [pl.BlockSpec(blk(a), lambda j, c_ref: (j, 0, 0)) for a in recv],
        out_specs=[pl.BlockSpec(blk(a), lambda j, c_ref: (j, 0, 0)) for a in recv])
    return pl.pallas_call(
        body, name=name, grid_spec=grid_spec,
        out_shape=[jax.ShapeDtypeStruct(a.shape, BF16) for a in recv],
        compiler_params=_params(),
    )(core, *mine, *recv)


HBM = pl.BlockSpec(memory_space=pltpu.HBM)
SEM = pl.BlockSpec(memory_space=pltpu.SEMAPHORE)
N_PEER_CHIPS = 3
TOKEN = (8, 128)


def _chip_copies(srcs, lands, send_sems, recv_sems):
    x, y, c = _my_place()
    my_chip = 2 * x + y
    peers = [(x, 1 - y), (1 - x, y), (1 - x, 1 - y)]
    cps = []
    for k, (px, py) in enumerate(peers):
        for a in range(len(srcs)):
            j = a * N_PEER_CHIPS + k
            cps.append(pltpu.make_async_remote_copy(
                src_ref=srcs[a].at[2 * px + py], dst_ref=lands[a].at[my_chip],
                send_sem=send_sems[j], recv_sem=recv_sems[j],
                device_id=(px, py, c), device_id_type=MESH))
    return cps


N_PEERS = N_DEV - 1


def _gather_copies(srcs, lands, send_sems, recv_sems):
    x, y, c = _my_place()
    me_idx = 4 * x + 2 * y + c
    flips = [(0, 0, 1), (0, 1, 0), (1, 0, 0), (0, 1, 1), (1, 0, 1), (1, 1, 0), (1, 1, 1)]
    cps = []
    for k, (fx, fy, fc) in enumerate(flips):
        peer = ((1 - x) if fx else x, (1 - y) if fy else y, (1 - c) if fc else c)
        for a in range(len(srcs)):
            j = a * N_PEERS + k
            cps.append(pltpu.make_async_remote_copy(
                src_ref=srcs[a], dst_ref=lands[a].at[me_idx], send_sem=send_sems[j], recv_sem=recv_sems[j],
                device_id=peer, device_id_type=MESH))
    return cps


def _split_start(copies, per_array, arrs, lands, name):
    arrs, lands = list(arrs), list(lands)
    n = len(arrs)
    k = n * per_array

    def body(*refs):
        srcs, land_refs = refs[:n], refs[n:2 * n]
        send_sems, recv_sems = refs[2 * n:2 * n + k], refs[2 * n + k:2 * n + 2 * k]
        token = refs[-1]
        for cp in copies(srcs, land_refs, send_sems, recv_sems):
            cp.start()
        token[...] = jnp.zeros_like(token)

    hbm_arrs = [pltpu.with_memory_space_constraint(a, pltpu.HBM) for a in arrs]
    lands = [pltpu.with_memory_space_constraint(a, pltpu.HBM) for a in lands]
    res = pl.pallas_call(
        body, name=name,
        out_shape=[pltpu.SemaphoreType.DMA(())] * (2 * k) + [pltpu.HBM(a.shape, a.dtype) for a in arrs + lands]
        + [jax.ShapeDtypeStruct(TOKEN, F32)],
        in_specs=[HBM] * (2 * n),
        out_specs=[SEM] * (2 * k) + [HBM] * (2 * n) + [pl.BlockSpec(memory_space=pltpu.VMEM)],
        input_output_aliases={a: 2 * k + a for a in range(2 * n)},
        compiler_params=pltpu.CompilerParams(has_side_effects=pltpu.SideEffectType.DATAFLOW_SIDE_EFFECTING),
    )(*hbm_arrs, *lands)
    return res[:k], res[k:2 * k], res[2 * k:2 * k + n], res[2 * k + n:2 * k + 2 * n], res[-1]


def _split_wait(copies, per_array, send_sems, recv_sems, srcs, lands, after, name):
    n = len(srcs)
    k = n * per_array

    def body(*refs):
        src_refs, land_refs = refs[:n], refs[n:2 * n]
        s_sems, r_sems = refs[2 * n:2 * n + k], refs[2 * n + k:2 * n + 2 * k]
        for cp in copies(src_refs, land_refs, s_sems, r_sems):
            cp.wait_send()
            cp.wait_recv()

    res = pl.pallas_call(
        body, name=name,
        out_shape=[pltpu.HBM(a.shape, a.dtype) for a in list(srcs) + list(lands)],
        in_specs=[HBM] * (2 * n) + [SEM] * (2 * k) + [ANY],
        out_specs=[HBM] * (2 * n),
        input_output_aliases={a: a for a in range(2 * n)},
        compiler_params=pltpu.CompilerParams(has_side_effects=pltpu.SideEffectType.DATAFLOW_SIDE_EFFECTING),
    )(*srcs, *lands, *send_sems, *recv_sems, after)
    return res[:n], res[n:]


def _chip_exchange_start(arrs, name):
    return _split_start(_chip_copies, N_PEER_CHIPS, arrs, [lax.empty(a.shape, a.dtype) for a in arrs], name)


def _chip_exchange_wait(send_sems, recv_sems, srcs, lands, after, name):
    return _split_wait(_chip_copies, N_PEER_CHIPS, send_sems, recv_sems, srcs, lands, after, name)


def _gather_start(arrs, me_idx, name):
    lands = [lax.dynamic_update_slice(lax.empty((N_DEV,) + a.shape, a.dtype), a[None], (me_idx, 0, 0)) for a in arrs]
    return _split_start(_gather_copies, N_PEERS, arrs, lands, name)


def _gather_wait(send_sems, recv_sems, srcs, lands, after, name):
    return _split_wait(_gather_copies, N_PEERS, send_sems, recv_sems, srcs, lands, after, name)[1]


def _proj_fwd(after, x, g_pre, w_int, tabs):
    s = x.shape[0]
    tm = min(512, s)

    def body(after_ref, x_ref, g_ref, t_ref, w_hbm,
             h_ref, pa_ref, pq_ref, pkv_ref, pbz_ref, pmq_ref, pmz_ref, pg_ref, w_vm, sems):
        _load_once([(w_hbm, w_vm)], sems)
        xf = x_ref[...]
        r = lax.rsqrt(jnp.mean(xf * xf, axis=-1, keepdims=True) + EPS)
        h = ((xf * r) * g_ref[...]).astype(BF16)
        h_ref[...] = h
        cs, s1, s2 = t_ref[0], t_ref[1], t_ref[2]

        def mm(seg, c0, width):
            return _dot_nt(h, w_vm[seg[0] + c0:seg[0] + c0 + width, :])

        for c0 in range(0, SEG_A[1], 512):
            pa_ref[:, c0:c0 + 512] = mm(SEG_A, c0, 512).astype(BF16)
        q = mm(SEG_BQ, 0, 512)
        for b in range(4):
            pq_ref[:, 128 * b:128 * b + 128] = _rope(q[:, 128 * b:128 * b + 128], cs, s1, s2).astype(BF16)
        kv = mm(SEG_BKV, 0, 256)
        pkv_ref[:, 0:128] = _rope(kv[:, 0:128], cs, s1, s2).astype(BF16)
        pkv_ref[:, 128:256] = kv[:, 128:256].astype(BF16)
        pbz_ref[...] = mm(SEG_BZ, 0, 512).astype(BF16)
        pmq_ref[...] = mm(SEG_MQ, 0, 512).astype(BF16)
        pmz_ref[...] = mm(SEG_MZ, 0, 512).astype(BF16)
        for c0 in range(0, SEG_G[1], 512):
            pg_ref[:, c0:c0 + 512] = mm(SEG_G, c0, 512).astype(BF16)

    widths = (D_MODEL, 2048, 512, 256, 512, 512, 512, 3072)
    return pl.pallas_call(
        body, name="proj_fwd", grid=(s // tm,),
        out_shape=[jax.ShapeDtypeStruct((s, w), BF16) for w in widths],
        in_specs=[_full(TOKEN), _rows(tm, D_MODEL), _full((1, D_MODEL)),
                  pl.BlockSpec((3, tm, 128), lambda i: (0, i, 0)), ANY],
        out_specs=[_rows(tm, w) for w in widths],
        scratch_shapes=[pltpu.VMEM((IN_WIDTH, D_MODEL), BF16), pltpu.SemaphoreType.DMA((1,))],
        compiler_params=_params(),
    )(after, x, g_pre, tabs, w_int)


def _mem_kv_fwd(mem, g_mem, w_mkv):
    m = mem.shape[0]

    def body(mem_ref, g_ref, w_ref, mn_ref, mkv_ref):
        xf = mem_ref[...]
        r = lax.rsqrt(jnp.mean(xf * xf, axis=-1, keepdims=True) + EPS)
        mn = ((xf * r) * g_ref[...]).astype(BF16)
        mn_ref[...] = mn
        mkv_ref[...] = _dot(mn, w_ref[...]).astype(BF16)

    return pl.pallas_call(
        body, name="mem_kv_fwd", grid=(1,),
        out_shape=[jax.ShapeDtypeStruct((m, D_MODEL), BF16)] * 2,
        in_specs=[_full((m, D_MODEL)), _full((1, D_MODEL)), _full((D_MODEL, D_MODEL))],
        out_specs=[_full((m, D_MODEL))] * 2,
        compiler_params=_params(),
    )(mem, g_mem, w_mkv)


def _halo_specs(s, tm, rows, width):
    nblk = s // rows
    prev = pl.BlockSpec((rows, width), lambda i: (jnp.maximum(i * (tm // rows) - 1, 0), 0))
    nxt = pl.BlockSpec((rows, width), lambda i: (jnp.minimum((i + 1) * (tm // rows), nblk - 1), 0))
    return prev, nxt


def _conv_common(pa, cu_prev, cu_next, w, tm):
    b, c, u, z = (pa[:, 512 * k:512 * k + 512] for k in range(4))
    cu = c * u
    row = lax.broadcasted_iota(jnp.int32, (tm, 512), 0)
    cu_m1 = jnp.where(row == 0, cu_prev, pltpu.roll(cu, 1, 0))
    cu_p1 = jnp.where(row == tm - 1, cu_next, pltpu.roll(cu, tm - 1, 0))
    y = cu_m1 * w[0:1] + cu * w[1:2] + cu_p1 * w[2:3]
    sig = _sigmoid(z)
    return b, c, u, z, cu, cu_m1, cu_p1, y, sig, row


def _conv_fwd(pa, w_conv):
    s = pa.shape[0]
    tm = min(512, s)
    nt = s // tm

    def body(pa_ref, pp_ref, pn_ref, w_ref, ya_ref):
        i = pl.program_id(0)
        prev_row = pp_ref[...].astype(F32)[15:16, :]
        next_row = pn_ref[...].astype(F32)[0:1, :]
        b, _, _, z, _, _, _, y, sig, _ = _conv_common(
            pa_ref[...].astype(F32),
            jnp.where(i == 0, 0.0, prev_row[:, 512:1024] * prev_row[:, 1024:1536]),
            jnp.where(i == nt - 1, 0.0, next_row[:, 512:1024] * next_row[:, 1024:1536]), w_ref[...], tm)
        ya_ref[...] = (b * y * (z * sig)).astype(BF16)

    prev, nxt = _halo_specs(s, tm, 16, 2048)
    return pl.pallas_call(
        body, name="conv_fwd", grid=(nt,),
        out_shape=jax.ShapeDtypeStruct((s, 512), BF16),
        in_specs=[_rows(tm, 2048), prev, nxt, _full((3, 512))],
        out_specs=_rows(tm, 512),
        compiler_params=_params(),
    )(pa, pa, pa, w_conv)


def _heads_to_lanes(a, g, row):
    low = row < HEAD_DIM
    parts = []
    for b in (2 * g, 2 * g + 1):
        t = jnp.transpose(a[:, 128 * b:128 * b + 128])
        swapped = pltpu.roll(t, HEAD_DIM, 0)
        if g == 0:
            parts += [jnp.where(low, t, 0.0), jnp.where(low, swapped, 0.0)]
        else:
            parts += [jnp.where(low, 0.0, swapped), jnp.where(low, 0.0, t)]
    return jnp.concatenate(parts, axis=1)


def _lanes_to_heads(t0, t1, row):
    low = row < HEAD_DIM
    blocks = []
    for b in range(4):
        g = b // 2
        tg = (t0, t1)[g]
        je = 2 * (b - 2 * g)
        even, odd = tg[:, 128 * je:128 * je + 128], tg[:, 128 * je + 128:128 * je + 256]
        if g == 0:
            t = jnp.where(low, even, pltpu.roll(odd, HEAD_DIM, 0))
        else:
            t = jnp.where(low, pltpu.roll(even, HEAD_DIM, 0), odd)
        blocks.append(jnp.transpose(t))
    return jnp.concatenate(blocks, axis=1)


WINDOW_KEYS = 3 * ATTN_BLOCK
STACKED = 4 * ATTN_BLOCK
KEY_CHUNK = 32
MAX_BLOCKS_IN_STEP = 8


def _fill_band_bias(bias, nb):
    assert nb >= 2
    c = lax.broadcasted_iota(jnp.int32, (WINDOW_KEYS, STACKED), 0)
    r = lax.broadcasted_iota(jnp.int32, (WINDOW_KEYS, STACKED), 1) & (ATTN_BLOCK - 1)
    band = (c >= r) & (c <= r + 2 * ATTN_BLOCK)
    for v, ok in enumerate((band, band & (c >= ATTN_BLOCK), band & (c < 2 * ATTN_BLOCK))):
        bias[v] = jnp.where(ok, 0.0, -jnp.inf)


def _bias_variant(n, nb):
    return jnp.where(n == 0, 1, jnp.where(n == nb - 1, 2, 0))


def _sink_row(sink_ref, g):
    return jnp.concatenate([jnp.full((1, ATTN_BLOCK), sink_ref[4 * g + j], F32) for j in range(4)], axis=1)


def _softmax_keys_major(sc, bias, variant, sink, e_scr):
    chunks = [pl.ds(k * KEY_CHUNK, KEY_CHUNK) for k in range(WINDOW_KEYS // KEY_CHUNK)]
    rows = [slice(k * KEY_CHUNK, (k + 1) * KEY_CHUNK) for k in range(WINDOW_KEYS // KEY_CHUNK)]
    m_run = jnp.full((KEY_CHUNK, STACKED), -jnp.inf, F32)
    for ck, rw in zip(chunks, rows):
        m_run = jnp.maximum(m_run, sc[rw] + bias[variant, ck, :])
    m = jnp.maximum(jnp.max(m_run, axis=0, keepdims=True), sink)
    l_run = jnp.zeros((KEY_CHUNK, STACKED), F32)
    for ck, rw in zip(chunks, rows):
        e = jnp.exp(sc[rw] + bias[variant, ck, :] - m)
        l_run += e
        e_scr[rw, :] = e.astype(BF16)
    es = jnp.exp(sink - m)
    inv = 1.0 / (jnp.sum(l_run, axis=0, keepdims=True) + es)
    return inv, es * inv


def _fill_padded(kv_ref, kpad, vpad, s):
    zero = jnp.zeros((ATTN_BLOCK, 128), BF16)
    kpad[0:ATTN_BLOCK, :] = zero
    vpad[0:ATTN_BLOCK, :] = zero
    kpad[ATTN_BLOCK + s:2 * ATTN_BLOCK + s, :] = zero
    vpad[ATTN_BLOCK + s:2 * ATTN_BLOCK + s, :] = zero
    kpad[ATTN_BLOCK:ATTN_BLOCK + s, :] = kv_ref[:, 0:128]
    vpad[ATTN_BLOCK:ATTN_BLOCK + s, :] = kv_ref[:, 128:256]


def _attn_fwd(pq, pkv, pbz, sink):
    s = pq.shape[0]
    nb = s // ATTN_BLOCK
    bps = min(MAX_BLOCKS_IN_STEP, nb)

    def body(sink_ref, q_ref, z_ref, kv_ref, yb_ref, kpad, vpad, bias, e_scr):
        i = pl.program_id(0)

        @pl.when(i == 0)
        def _():
            _fill_padded(kv_ref, kpad, vpad, s)
            _fill_band_bias(bias, nb)

        row = lax.broadcasted_iota(jnp.int32, (ATTN_BLOCK, 128), 0)
        for b in range(bps):
            n = i * bps + b
            rows = slice(b * ATTN_BLOCK, (b + 1) * ATTN_BLOCK)
            start = pl.multiple_of(n * ATTN_BLOCK, ATTN_BLOCK)
            kw, vw = kpad[pl.ds(start, WINDOW_KEYS), :], vpad[pl.ds(start, WINDOW_KEYS), :]
            qf = q_ref[rows, :].astype(F32)
            variant = _bias_variant(n, nb)
            outs = []
            for g in range(2):
                e_bg = e_scr.at[2 * b + g]
                qt = (_heads_to_lanes(qf, g, row) * ATTN_SCALE).astype(BF16)
                inv, _ = _softmax_keys_major(_dot(kw, qt), bias, variant, _sink_row(sink_ref, g), e_bg)
                outs.append(_dot_tn(vw, e_bg[...]) * inv)
            attn = _lanes_to_heads(outs[0], outs[1], row)
            z = z_ref[rows, :].astype(F32)
            yb_ref[rows, :] = (attn * (z * _sigmoid(z))).astype(BF16)

    tq = bps * ATTN_BLOCK
    return pl.pallas_call(
        body, name="attn_fwd", grid=(s // tq,),
        out_shape=jax.ShapeDtypeStruct((s, 512), BF16),
        in_specs=[pl.BlockSpec(memory_space=pltpu.SMEM), _rows(tq, 512), _rows(tq, 512), _full((s, 256))],
        out_specs=_rows(tq, 512),
        scratch_shapes=[pltpu.VMEM((s + 2 * ATTN_BLOCK, 128), BF16)] * 2
        + [pltpu.VMEM((3, WINDOW_KEYS, STACKED), F32),
           pltpu.VMEM((2 * bps, WINDOW_KEYS, STACKED), BF16)],
        compiler_params=_params(),
    )(sink, pq, pbz, pkv)


def _mem_softmax_t(q, mk):
    sc = _dot_nt(mk, q) * MEM_SCALE
    e = jnp.exp(sc - jnp.max(sc, axis=0, keepdims=True))
    return e * (1.0 / jnp.sum(e, axis=0, keepdims=True))


def _mem_attn_fwd(pmq, pmz, mkv):
    s = pmq.shape[0]
    m = mkv.shape[0]
    tm = min(512, s)

    def body(q_ref, z_ref, mk_ref, mv_ref, ym_ref):
        z = z_ref[...].astype(F32)
        sz = z * _sigmoid(z)
        for h in range(MEM_HEADS):
            cols = slice(128 * h, 128 * h + 128)
            pt = _mem_softmax_t(q_ref[:, cols], mk_ref[:, cols])
            o = _dot_tn(pt.astype(BF16), mv_ref[:, cols])
            ym_ref[:, cols] = (o * sz[:, cols]).astype(BF16)

    return pl.pallas_call(
        body, name="mem_attn_fwd", grid=(s // tm,),
        out_shape=jax.ShapeDtypeStruct((s, 512), BF16),
        in_specs=[_rows(tm, 512), _rows(tm, 512), pl.BlockSpec((m, 512), lambda i: (0, 0)),
                  pl.BlockSpec((m, 512), lambda i: (0, 1))],
        out_specs=_rows(tm, 512),
        compiler_params=_params(),
    )(pmq, pmz, mkv, mkv)


def _mid(ya, yb, ym, pg, x, target, g_post, w_up, w_out):
    s = x.shape[0]
    tm = min(256, s)
    nt = s // tm

    def body(ya_ref, yb_ref, ym_ref, pg_ref, x_ref, t_ref, gp_ref, wup_hbm, wout_hbm,
             dg_ref, dya_ref, dyb_ref, dym_ref, dy_ref, loss_ref, ggp_ref, mb_ref, dob_ref, du_ref,
             wup_vm, wout_vm, sems):
        i = pl.program_id(0)
        _load_once([(wup_hbm.at[d], wup_vm.at[:, pl.ds(128 * d, 128)]) for d in range(N_DEV)]
                   + [(wout_hbm, wout_vm)], sems)

        @pl.when(i == 0)
        def _():
            loss_ref[...] = jnp.zeros_like(loss_ref)
            ggp_ref[...] = jnp.zeros_like(ggp_ref)

        ys = (ya_ref[...], yb_ref[...], ym_ref[...])
        us = [_dot(ys[k], wup_vm[512 * k:512 * k + 512, :]) for k in range(3)]
        gates = [_sigmoid(pg_ref[:, 1024 * k:1024 * k + 1024].astype(F32)) for k in range(3)]
        merged = gates[0] * us[0] + gates[1] * us[1] + gates[2] * us[2]
        mb = merged.astype(BF16)
        mb_ref[...] = mb
        out = _dot(mb, wout_vm[...])
        r = lax.rsqrt(jnp.mean(out * out, axis=-1, keepdims=True) + EPS)
        on = out * r
        gp = gp_ref[...]
        err = (x_ref[...] + on * gp) - t_ref[...]
        loss_ref[...] += 0.5 * jnp.sum(err * err) * (1.0 / D_MODEL)
        dy = err * (1.0 / D_MODEL)
        dy_ref[...] = dy
        ggp_ref[...] += jnp.sum(dy * on, axis=0, keepdims=True)
        a = dy * gp
        d_out = r * (a - on * jnp.mean(a * on, axis=-1, keepdims=True))
        dob = d_out.astype(BF16)
        dob_ref[...] = dob
        d_merged = _dot_nt(dob, wout_vm[...])
        d_refs = (dya_ref, dyb_ref, dym_ref)
        for k in range(3):
            g = gates[k]
            du_f = d_merged * g
            dg_ref[:, 1024 * k:1024 * k + 1024] = (du_f * us[k] * (1.0 - g)).astype(BF16)
            du = du_f.astype(BF16)
            du_ref[k] = du
            d_refs[k][...] = _dot_nt(du, wup_vm[512 * k:512 * k + 512, :]).astype(BF16)

    return pl.pallas_call(
        body, name="mid", grid=(nt,),
        out_shape=[jax.ShapeDtypeStruct((s, 3072), BF16)] + [jax.ShapeDtypeStruct((s, 512), BF16)] * 3
        + [jax.ShapeDtypeStruct((s, D_MODEL), F32), jax.ShapeDtypeStruct((8, 128), F32),
           jax.ShapeDtypeStruct((1, D_MODEL), F32), jax.ShapeDtypeStruct((s, D_MODEL), BF16),
           jax.ShapeDtypeStruct((s, D_MODEL), BF16), jax.ShapeDtypeStruct((3, s, D_MODEL), BF16)],
        in_specs=[_rows(tm, 512)] * 3 + [_rows(tm, 3072), _rows(tm, D_MODEL), _rows(tm, D_MODEL),
                                         _full((1, D_MODEL)), ANY, ANY],
        out_specs=[_rows(tm, 3072)] + [_rows(tm, 512)] * 3
        + [_rows(tm, D_MODEL), _full((8, 128)), _full((1, D_MODEL)), _rows(tm, D_MODEL), _rows(tm, D_MODEL),
           pl.BlockSpec((3, tm, D_MODEL), lambda i: (0, i, 0))],
        scratch_shapes=[pltpu.VMEM((1536, D_MODEL), BF16), pltpu.VMEM((D_MODEL, D_MODEL), BF16),
                        pltpu.SemaphoreType.DMA((N_DEV + 1,))],
        compiler_params=_params(),
    )(ya, yb, ym, pg, x, target, g_post, w_up, w_out)


def _gw_mid(mb, dob, ys, du):
    s = mb.shape[0]
    tn = 256

    def out_body(mb_ref, dob_ref, o_ref):
        o_ref[...] = _dot_tn(mb_ref[...], dob_ref[...]).astype(BF16)

    gw_out = pl.pallas_call(
        out_body, name="gw_out", grid=(D_MODEL // tn,),
        out_shape=jax.ShapeDtypeStruct((D_MODEL, D_MODEL), BF16),
        in_specs=[pl.BlockSpec((s, tn), lambda j: (0, j)), _full((s, D_MODEL))],
        out_specs=pl.BlockSpec((tn, D_MODEL), lambda j: (j, 0)),
        compiler_params=_params(),
    )(mb, dob)

    per = 512 // tn

    def up_body(ya_ref, yb_ref, ym_ref, du_ref, o_ref):
        j = pl.program_id(0)
        for k, y_ref in enumerate((ya_ref, yb_ref, ym_ref)):
            @pl.when(j // per == k)
            def _(y_ref=y_ref):
                res = _dot_tn(y_ref[...], du_ref[...])
                for d in range(N_DEV):
                    o_ref[d] = res[:, 128 * d:128 * d + 128].astype(BF16)

    def y_spec(k):
        return pl.BlockSpec((s, tn), lambda j: (0, jnp.clip(j - per * k, 0, per - 1)))

    gw_up = pl.pallas_call(
        up_body, name="gw_up", grid=(3 * per,),
        out_shape=jax.ShapeDtypeStruct((N_DEV, 1536, 128), BF16),
        in_specs=[y_spec(0), y_spec(1), y_spec(2), pl.BlockSpec((None, s, D_MODEL), lambda j: (j // per, 0, 0))],
        out_specs=pl.BlockSpec((N_DEV, tn, 128), lambda j: (0, j, 0)),
        compiler_params=_params(),
    )(*ys, du)
    return gw_out, gw_up


def _conv_bwd(after, pa, dya, w_conv):
    s = pa.shape[0]
    tm = min(512, s)
    nt = s // tm

    def body(after_ref, pa_ref, pp_ref, pn_ref, d_ref, dp_ref, dn_ref, w_ref, da_ref, gw_ref):
        i = pl.program_id(0)
        first, last = i == 0, i == nt - 1

        @pl.when(first)
        def _():
            gw_ref[...] = jnp.zeros_like(gw_ref)

        w = w_ref[...]
        prev_row = pp_ref[...].astype(F32)[15:16, :]
        next_row = pn_ref[...].astype(F32)[0:1, :]
        b, c, u, z, cu, cu_m1, cu_p1, y, sig, row = _conv_common(
            pa_ref[...].astype(F32),
            jnp.where(first, 0.0, prev_row[:, 512:1024] * prev_row[:, 1024:1536]),
            jnp.where(last, 0.0, next_row[:, 512:1024] * next_row[:, 1024:1536]), w, tm)
        sz = z * sig
        dya_t = d_ref[...].astype(F32)
        d_y = dya_t * b * sz

        def halo_dy(p_row, d_row):
            zz = p_row[:, 1536:2048]
            return d_row * p_row[:, 0:512] * (zz * _sigmoid(zz))

        dy_prev = jnp.where(first, 0.0, halo_dy(prev_row, dp_ref[...].astype(F32)[15:16, :]))
        dy_next = jnp.where(last, 0.0, halo_dy(next_row, dn_ref[...].astype(F32)[0:1, :]))
        dy_m1 = jnp.where(row == 0, dy_prev, pltpu.roll(d_y, 1, 0))
        dy_p1 = jnp.where(row == tm - 1, dy_next, pltpu.roll(d_y, tm - 1, 0))
        d_cu = dy_p1 * w[0:1] + d_y * w[1:2] + dy_m1 * w[2:3]
        da_ref[:, 0:512] = (dya_t * y * sz).astype(BF16)
        da_ref[:, 512:1024] = (d_cu * u).astype(BF16)
        da_ref[:, 1024:1536] = (d_cu * c).astype(BF16)
        da_ref[:, 1536:2048] = (dya_t * b * y * (sig + sz * (1.0 - sig))).astype(BF16)
        gw_ref[0:1, :] += jnp.sum(d_y * cu_m1, axis=0, keepdims=True)
        gw_ref[1:2, :] += jnp.sum(d_y * cu, axis=0, keepdims=True)
        gw_ref[2:3, :] += jnp.sum(d_y * cu_p1, axis=0, keepdims=True)

    prev, nxt = _halo_specs(s, tm, 16, 2048)
    dprev, dnxt = _halo_specs(s, tm, 16, 512)
    return pl.pallas_call(
        body, name="conv_bwd", grid=(nt,),
        out_shape=[jax.ShapeDtypeStruct((s, 2048), BF16), jax.ShapeDtypeStruct((8, 512), F32)],
        in_specs=[_full(TOKEN), _rows(tm, 2048), prev, nxt, _rows(tm, 512), dprev, dnxt, _full((3, 512))],
        out_specs=[_rows(tm, 2048), _full((8, 512))],
        compiler_params=_params(),
    )(after, pa, pa, pa, dya, dya, dya, w_conv)


def _attn_bwd(after, pq, pkv, pbz, dyb, sink, tabs):
    s = pq.shape[0]
    nb = s // ATTN_BLOCK
    bps = min(MAX_BLOCKS_IN_STEP, nb)

    def body(sink_ref, after_ref, q_ref, z_ref, d_ref, kv_ref, t_ref,
             dq_ref, dz_ref, dkv_ref, gs_ref, kpad, vpad, dk_acc, dv_acc, bias, e_scr, ds_scr):
        i = pl.program_id(0)

        @pl.when(i == 0)
        def _():
            _fill_padded(kv_ref, kpad, vpad, s)
            _fill_band_bias(bias, nb)
            dk_acc[...] = jnp.zeros_like(dk_acc)
            dv_acc[...] = jnp.zeros_like(dv_acc)
            gs_ref[...] = jnp.zeros_like(gs_ref)

        row = lax.broadcasted_iota(jnp.int32, (ATTN_BLOCK, 128), 0)
        for b in range(bps):
            n = i * bps + b
            rows = slice(b * ATTN_BLOCK, (b + 1) * ATTN_BLOCK)
            start = pl.multiple_of(n * ATTN_BLOCK, ATTN_BLOCK)
            kw, vw = kpad[pl.ds(start, WINDOW_KEYS), :], vpad[pl.ds(start, WINDOW_KEYS), :]
            qf = q_ref[rows, :].astype(F32)
            variant = _bias_variant(n, nb)
            z = z_ref[rows, :].astype(F32)
            sig = _sigmoid(z)
            dyb_t = d_ref[rows, :].astype(F32)
            d_attn = dyb_t * (z * sig)
            outs, dqs = [], []
            dk_w = jnp.zeros((WINDOW_KEYS, 128), F32)
            dv_w = jnp.zeros((WINDOW_KEYS, 128), F32)
            for g in range(2):
                e_bg, ds_bg = e_scr.at[2 * b + g], ds_scr.at[2 * b + g]
                qt = _heads_to_lanes(qf, g, row)
                inv, p_sink = _softmax_keys_major(
                    _dot(kw, (qt * ATTN_SCALE).astype(BF16)), bias, variant, _sink_row(sink_ref, g), e_bg)
                ot = _dot_tn(vw, e_bg[...]) * inv
                outs.append(ot)
                dot_ = _heads_to_lanes(d_attn, g, row)
                delta = jnp.sum(dot_ * ot, axis=0, keepdims=True)
                dpt = _dot(vw, dot_.astype(BF16))
                for k in range(WINDOW_KEYS // KEY_CHUNK):
                    rw = slice(k * KEY_CHUNK, (k + 1) * KEY_CHUNK)
                    ds_bg[rw, :] = (e_bg[rw, :].astype(F32) * (dpt[rw] - delta)).astype(BF16)
                sink_part = p_sink * delta
                for j in range(4):
                    h = 4 * g + j
                    gs_ref[h:h + 1, :] -= jnp.sum(sink_part[:, 128 * j:128 * j + 128])
                dqs.append(_dot_tn(kw, ds_bg[...]) * (inv * ATTN_SCALE))
                dk_w += _dot_nt(ds_bg[...], (qt * inv).astype(BF16)) * ATTN_SCALE
                dv_w += _dot_nt(e_bg[...], (dot_ * inv).astype(BF16))
            dk_acc[pl.ds(start, WINDOW_KEYS), :] += dk_w
            dv_acc[pl.ds(start, WINDOW_KEYS), :] += dv_w
            attn = _lanes_to_heads(outs[0], outs[1], row)
            dz_ref[rows, :] = (dyb_t * attn * (sig * (1.0 + z * (1.0 - sig)))).astype(BF16)
            dq = _lanes_to_heads(dqs[0], dqs[1], row)
            trows = pl.ds(start, ATTN_BLOCK)
            cs, s1, s2 = t_ref[0, trows, :], t_ref[1, trows, :], t_ref[2, trows, :]
            for blk in range(4):
                cols = slice(128 * blk, 128 * blk + 128)
                dq_ref[rows, cols] = _rope_t(dq[:, cols], cs, s1, s2).astype(BF16)

        @pl.when(i == nb // bps - 1)
        def _():
            dk = dk_acc[ATTN_BLOCK:ATTN_BLOCK + s, :]
            dkv_ref[:, 0:128] = _rope_t(dk, t_ref[0], t_ref[1], t_ref[2]).astype(BF16)
            dkv_ref[:, 128:256] = dv_acc[ATTN_BLOCK:ATTN_BLOCK + s, :].astype(BF16)

    tq = bps * ATTN_BLOCK
    tile = _rows(tq, 512)
    return pl.pallas_call(
        body, name="attn_bwd", grid=(s // tq,),
        out_shape=[jax.ShapeDtypeStruct((s, 512), BF16), jax.ShapeDtypeStruct((s, 512), BF16),
                   jax.ShapeDtypeStruct((s, 256), BF16), jax.ShapeDtypeStruct((8, 128), F32)],
        in_specs=[pl.BlockSpec(memory_space=pltpu.SMEM), _full(TOKEN), tile, tile, tile, _full((s, 256)),
                  _full((3, s, 128))],
        out_specs=[tile, tile, _full((s, 256)), _full((8, 128))],
        scratch_shapes=[pltpu.VMEM((s + 2 * ATTN_BLOCK, 128), BF16)] * 2
        + [pltpu.VMEM((s + 2 * ATTN_BLOCK, 128), F32)] * 2
        + [pltpu.VMEM((3, WINDOW_KEYS, STACKED), F32)]
        + [pltpu.VMEM((2 * bps, WINDOW_KEYS, STACKED), BF16)] * 2,
        compiler_params=_params(),
    )(sink, after, pq, pbz, dyb, pkv, tabs)


def _mem_attn_bwd(pmq, pmz, mkv, dym):
    s = pmq.shape[0]
    m = mkv.shape[0]
    tm = min(512, s)

    def body(q_ref, z_ref, d_ref, mk_ref, mv_ref, dq_ref, dz_ref, dmkv_ref):
        @pl.when(pl.program_id(0) == 0)
        def _():
            dmkv_ref[...] = jnp.zeros_like(dmkv_ref)

        z = z_ref[...].astype(F32)
        sig = _sigmoid(z)
        dym_t = d_ref[...].astype(F32)
        d_attn = dym_t * (z * sig)
        dsilu = sig * (1.0 + z * (1.0 - sig))
        for h in range(MEM_HEADS):
            cols = slice(128 * h, 128 * h + 128)
            q, mk, mv = q_ref[:, cols], mk_ref[:, cols], mv_ref[:, cols]
            pt = _mem_softmax_t(q, mk)
            pb = pt.astype(BF16)
            o = _dot_tn(pb, mv)
            dob = d_attn[:, cols].astype(BF16)
            dpt = _dot_nt(mv, dob)
            dst = (pt * (dpt - jnp.sum(pt * dpt, axis=0, keepdims=True))).astype(BF16)
            dq_ref[:, cols] = (_dot_tn(dst, mk) * MEM_SCALE).astype(BF16)
            dz_ref[:, cols] = (dym_t[:, cols] * o * dsilu[:, cols]).astype(BF16)
            dmkv_ref[:, cols] += _dot(dst, q) * MEM_SCALE
            dmkv_ref[:, 512 + 128 * h:512 + 128 * h + 128] += _dot(pb, dob)

    return pl.pallas_call(
        body, name="mem_attn_bwd", grid=(s // tm,),
        out_shape=[jax.ShapeDtypeStruct((s, 512), BF16), jax.ShapeDtypeStruct((s, 512), BF16),
                   jax.ShapeDtypeStruct((m, D_MODEL), F32)],
        in_specs=[_rows(tm, 512), _rows(tm, 512), _rows(tm, 512), pl.BlockSpec((m, 512), lambda i: (0, 0)),
                  pl.BlockSpec((m, 512), lambda i: (0, 1))],
        out_specs=[_rows(tm, 512), _rows(tm, 512), _full((m, D_MODEL))],
        compiler_params=_params(),
    )(pmq, pmz, dym, mkv, mkv)


def _mem_kv_bwd(mem, g_mem, mn, dmkv, w_mkv):
    m = mem.shape[0]

    def body(mem_ref, g_ref, mn_ref, d_ref, w_ref, gw_ref, gg_ref):
        db = d_ref[...].astype(BF16)
        gw_ref[...] = _dot_tn(mn_ref[...], db).astype(BF16)
        d_mn = _dot_nt(db, w_ref[...])
        xf = mem_ref[...]
        r = lax.rsqrt(jnp.mean(xf * xf, axis=-1, keepdims=True) + EPS)
        gg_ref[...] = jnp.sum(d_mn * (xf * r), axis=0, keepdims=True)

    return pl.pallas_call(
        body, name="mem_kv_bwd", grid=(1,),
        out_shape=[jax.ShapeDtypeStruct((D_MODEL, D_MODEL), BF16), jax.ShapeDtypeStruct((1, D_MODEL), F32)],
        in_specs=[_full((m, D_MODEL)), _full((1, D_MODEL)), _full((m, D_MODEL)), _full((m, D_MODEL)),
                  _full((D_MODEL, D_MODEL))],
        out_specs=[_full((D_MODEL, D_MODEL)), _full((1, D_MODEL))],
        compiler_params=_params(),
    )(mem, g_mem, mn, dmkv, w_mkv)


def _dh_bwd(after, dparts, x, dy, g_pre, w_int):
    s = x.shape[0]
    tm = min(256, s)

    def body(after_ref, *refs):
        d_refs = refs[:7]
        x_ref, dy_ref, g_ref, w_hbm, gx_ref, gg_ref, w_vm, sems = refs[7:]
        _load_once([(w_hbm, w_vm)], sems)

        @pl.when(pl.program_id(0) == 0)
        def _():
            gg_ref[...] = jnp.zeros_like(gg_ref)

        d_h = jnp.zeros((tm, D_MODEL), F32)
        for d_ref, (r0, width) in zip(d_refs, SEGS):
            for c0 in range(0, width, 512):
                cw = min(512, width - c0)
                d_h += _dot(d_ref[:, c0:c0 + cw], w_vm[r0 + c0:r0 + c0 + cw, :])
        xf = x_ref[...]
        r = lax.rsqrt(jnp.mean(xf * xf, axis=-1, keepdims=True) + EPS)
        xn = xf * r
        a = d_h * g_ref[...]
        gx_ref[...] = r * (a - xn * jnp.mean(a * xn, axis=-1, keepdims=True)) + dy_ref[...]
        gg_ref[...] += jnp.sum(d_h * xn, axis=0, keepdims=True)

    return pl.pallas_call(
        body, name="dh_bwd", grid=(s // tm,),
        out_shape=[jax.ShapeDtypeStruct((s, D_MODEL), F32), jax.ShapeDtypeStruct((1, D_MODEL), F32)],
        in_specs=[_full(TOKEN)] + [_rows(tm, w) for _, w in SEGS]
        + [_rows(tm, D_MODEL), _rows(tm, D_MODEL), _full((1, D_MODEL)), ANY],
        out_specs=[_rows(tm, D_MODEL), _full((1, D_MODEL))],
        scratch_shapes=[pltpu.VMEM((IN_WIDTH, D_MODEL), BF16), pltpu.SemaphoreType.DMA((1,))],
        compiler_params=_params(),
    )(after, *dparts, x, dy, g_pre, w_int)


def _gw_in(dparts, h):
    s = h.shape[0]
    tn = 256
    tiles = [(a, c0) for a, (_, width) in enumerate(SEGS) for c0 in range(0, width, tn)]
    n_t = len(tiles)
    assert n_t * tn == IN_WIDTH

    def body(*refs):
        d_hbm = refs[:7]
        h_hbm, out_hbm, lhs, h_vm, res, in_sems, out_sems, h_sem = refs[7:]

        def load(a, c0, slot):
            return pltpu.make_async_copy(d_hbm[a].at[:, pl.ds(c0, tn)], lhs.at[slot], in_sems.at[slot])

        def store(t, slot):
            rows = pl.ds(pl.multiple_of(t * tn, tn), tn)
            return pltpu.make_async_copy(res.at[slot], out_hbm.at[rows], out_sems.at[slot])

        def start_load(t, slot):
            for k, (a, c0) in enumerate(tiles):
                @pl.when(t == k)
                def _(a=a, c0=c0):
                    load(a, c0, slot).start()

        h_copy = pltpu.make_async_copy(h_hbm, h_vm, h_sem)
        h_copy.start()
        load(*tiles[0], 0).start()
        h_copy.wait()

        def step(t, carry):
            slot = t & 1

            @pl.when(t + 1 < n_t)
            def _():
                start_load(t + 1, 1 - slot)

            load(*tiles[0], slot).wait()

            @pl.when(t >= 2)
            def _():
                store(t - 2, slot).wait()

            res[slot] = _dot_tn(lhs[slot], h_vm[...]).astype(BF16)
            store(t, slot).start()
            return carry

        lax.fori_loop(0, n_t, step, 0)
        store(n_t - 2, (n_t - 2) & 1).wait()
        store(n_t - 1, (n_t - 1) & 1).wait()

    return pl.pallas_call(
        body, name="gw_in",
        out_shape=jax.ShapeDtypeStruct((IN_WIDTH, D_MODEL), BF16),
        in_specs=[ANY] * 8, out_specs=ANY,
        scratch_shapes=[pltpu.VMEM((2, s, tn), BF16), pltpu.VMEM((s, D_MODEL), BF16),
                        pltpu.VMEM((2, tn, D_MODEL), BF16), pltpu.SemaphoreType.DMA((2,)),
                        pltpu.SemaphoreType.DMA((2,)), pltpu.SemaphoreType.DMA(())],
        compiler_params=pltpu.CompilerParams(vmem_limit_bytes=CALL_VMEM_MB * 1024 * 1024),
    )(*dparts, h)


def _adamw_math(w, g, m, v):
    m2 = ADAM_B1 * m + (1.0 - ADAM_B1) * g
    v2 = ADAM_B2 * v + (1.0 - ADAM_B2) * (g * g)
    m_hat = m2 / (1.0 - ADAM_B1 ** ADAM_STEP)
    v_hat = v2 / (1.0 - ADAM_B2 ** ADAM_STEP)
    delta = -ADAM_LR * (m_hat / (jnp.sqrt(v_hat) + ADAM_EPS) + ADAM_WD * w)
    return delta, m2, v2


def _sum_adamw(after, own, land, chip, block, w, m, v, name, tiles=1):
    r, c = w.shape
    rt = r // tiles

    def body(c_ref, after_ref, own_ref, l1_ref, l2_ref, l3_ref, w_ref, m_ref, v_ref, g_ref, d_ref, m2_ref, v2_ref):
        g = own_ref[...].astype(F32)
        for l_ref in (l1_ref, l2_ref, l3_ref):
            g += l_ref[...].astype(F32)
        g_ref[...] = g
        d_ref[...], m2_ref[...], v2_ref[...] = _adamw_math(w_ref[...], g, m_ref[...], v_ref[...])

    def share(k):
        return pl.BlockSpec((None, rt, c), lambda i, c_ref: (jnp.bitwise_xor(c_ref[0], k), block * tiles + i, 0))

    spec = pl.BlockSpec((rt, c), lambda i, c_ref: (i, 0))
    grid_spec = pltpu.PrefetchScalarGridSpec(
        num_scalar_prefetch=1, grid=(tiles,),
        in_specs=[ANY, share(0), share(1), share(2), share(3)] + [spec] * 3, out_specs=[spec] * 4)
    return pl.pallas_call(
        body, name=name, grid_spec=grid_spec,
        out_shape=[jax.ShapeDtypeStruct((r, c), F32)] * 4,
        compiler_params=_params(),
    )(chip, after, own, land, land, land, w, m, v)


def _sum_adamw_group(after, items, chip, name):
    k = len(items)

    def body(c_ref, after_ref, *refs):
        shares, wmv, outs = refs[:4 * k], refs[4 * k:7 * k], refs[7 * k:]
        for j in range(k):
            g = shares[4 * j][...].astype(F32)
            for l_ref in shares[4 * j + 1:4 * j + 4]:
                g += l_ref[...].astype(F32)
            outs[4 * j][...] = g
            outs[4 * j + 1][...], outs[4 * j + 2][...], outs[4 * j + 3][...] = _adamw_math(
                wmv[3 * j][...], g, wmv[3 * j + 1][...], wmv[3 * j + 2][...])

    def share(shape, block, q):
        return pl.BlockSpec((None,) + shape, lambda i, c_ref: (jnp.bitwise_xor(c_ref[0], q), block, 0))

    in_specs, args = [ANY], [after]
    for own, land, block, w, m, v in items:
        in_specs += [share(w.shape, block, q) for q in range(4)]
        args += [own, land, land, land]
    for own, land, block, w, m, v in items:
        in_specs += [pl.BlockSpec(w.shape, lambda i, c_ref: (0, 0))] * 3
        args += [w, m, v]
    out_specs = [pl.BlockSpec(w.shape, lambda i, c_ref: (0, 0)) for _, _, _, w, _, _ in items for _ in range(4)]
    res = pl.pallas_call(
        body, name=name,
        grid_spec=pltpu.PrefetchScalarGridSpec(num_scalar_prefetch=1, grid=(1,), in_specs=in_specs,
                                               out_specs=out_specs),
        out_shape=[jax.ShapeDtypeStruct(w.shape, F32) for _, _, _, w, _, _ in items for _ in range(4)],
        compiler_params=_params(),
    )(chip, *args)
    return [res[4 * j:4 * j + 4] for j in range(k)]


def _small_pack(parts):
    def pack_body(gpre_ref, gconv_ref, gsink_ref, gmem_ref, gpost_ref, loss_ref, pack):
        lane = lax.broadcasted_iota(jnp.int32, (1, 128), 1)
        sink_row = jnp.zeros((1, 128), F32)
        for h in range(8):
            sink_row = jnp.where(lane == h, gsink_ref[h:h + 1, :], sink_row)
        pack[...] = jnp.zeros_like(pack)
        pack[0:1, :] = gpre_ref[...]
        pack[1:2, :] = gmem_ref[...]
        pack[2:3, :] = gpost_ref[...]
        pack[3:6, 0:512] = gconv_ref[0:3, :]
        pack[6:7, 0:128] = sink_row
        pack[7:8, 0:128] = loss_ref[0:1, :]

    return pl.pallas_call(
        pack_body, name="small_pack", grid=(1,),
        out_shape=jax.ShapeDtypeStruct((8, D_MODEL), F32),
        in_specs=[_full(p.shape) for p in parts], out_specs=_full((8, D_MODEL)),
    )(*parts)


def _small_apply(packs, ws, ms, vs):
    def apply(p_ref, *refs):
        w_refs, m_refs, v_refs = refs[0:5], refs[5:10], refs[10:15]
        loss_out = refs[15]
        g_outs, d_outs, m_outs, v_outs = refs[16:21], refs[21:26], refs[26:31], refs[31:36]
        x, y, c = _my_place()
        tot = p_ref[0]
        for d in range(1, N_DEV):
            tot = tot + p_ref[d]
        conv = pltpu.roll(tot[:, 0:512], (512 - 64 * (4 * x + 2 * y + c)) % 512, 1)[3:6, 0:64]
        grads = (tot[0:1, :], conv, tot[6:7, 0:8], tot[1:2, :], tot[2:3, :])
        loss_out[...] = tot[7:8, 0:128]
        for j in range(5):
            g_outs[j][...] = grads[j]
            d_outs[j][...], m_outs[j][...], v_outs[j][...] = _adamw_math(
                w_refs[j][...], grads[j], m_refs[j][...], v_refs[j][...])

    specs = [_full(w.shape) for w in ws]
    res = pl.pallas_call(
        apply, name="small_apply", grid=(1,),
        out_shape=[jax.ShapeDtypeStruct((1, 128), F32)] + [jax.ShapeDtypeStruct(w.shape, F32) for w in ws] * 4,
        in_specs=[_full((N_DEV, 8, D_MODEL))] + specs * 3,
        out_specs=[_full((1, 128))] + specs * 4,
    )(packs, *ws, *ms, *vs)
    return res[0], res[1:6], res[6:11], res[11:16], res[16:21]


def kernel(x, mem, g_pre, w_in, w_conv, attn_sink, g_mem, w_mem_kv, w_up_a, w_up_b, w_up_m, w_out, g_post, loss_target, m_g_pre, m_w_in, m_w_conv, m_attn_sink, m_g_mem, m_w_mem_kv, m_w_up_a, m_w_up_b, m_w_up_m, m_w_out, m_g_post, v_g_pre, v_w_in, v_w_conv, v_attn_sink, v_g_mem, v_w_mem_kv, v_w_up_a, v_w_up_b, v_w_up_m, v_w_out, v_g_post):
    s = x.shape[1]
    x2, mem2, tgt2 = x[0], mem[0], loss_target[0]
    me = 4 * lax.axis_index("x") + 2 * lax.axis_index("y") + lax.axis_index("c")

    w_conv_loc = jnp.zeros((8, 128), F32).at[:3, :64].set(w_conv[0])
    w_int_g, w_conv_g, tabs, w_mkv_loc, w_out_loc, w_up_loc = _all_gather(
        [w_in[0].T.astype(BF16), w_conv_loc], "gather_w_in",
        splits=[[(112 * k, 112) for k in range(7)] + [(784, 144)], [(0, 8)]],
        side=_gather_side(s, w_mem_kv[0], w_out[0], (w_up_a[0], w_up_b[0], w_up_m[0])))
    w_int = w_int_g.reshape(IN_WIDTH, D_MODEL)
    w_conv_f = w_conv_g[:, :3, :64].transpose(1, 0, 2).reshape(3, 512)
    late = _gather_start([w_mkv_loc, w_out_loc, w_up_loc], me, "gather_late_start")
    sink = attn_sink[0]

    h, pa, pq, pkv, pbz, pmq, pmz, pg = _proj_fwd(late[4], x2, g_pre, w_int, tabs)
    ya = _conv_fwd(pa, w_conv_f)
    yb = _attn_fwd(pq, pkv, pbz, sink)
    w_mkv_g, w_out_g, w_up_g = _gather_wait(*late[:4], yb, "gather_late_wait")
    w_mkv = w_mkv_g.reshape(D_MODEL, D_MODEL)
    w_out_f = w_out_g.reshape(D_MODEL, D_MODEL)
    mn, mkv = _mem_kv_fwd(mem2, g_mem, w_mkv)
    ym = _mem_attn_fwd(pmq, pmz, mkv)
    dg, dya, dyb, dym, dy, loss_p, gg_post, mb, dob, du = _mid(ya, yb, ym, pg, x2, tgt2, g_post, w_up_g, w_out_f)
    gw_out, gw_up = _gw_mid(mb, dob, (ya, yb, ym), du)

    core = lax.axis_index("c").astype(jnp.int32).reshape(1)
    chip = (2 * lax.axis_index("x") + lax.axis_index("y")).astype(jnp.int32).reshape(1)

    dmq, dmz, dmkv = _mem_attn_bwd(pmq, pmz, mkv, dym)
    gw_mkv, gg_mem = _mem_kv_bwd(mem2, g_mem, mn, dmkv, w_mkv)
    shares1 = [gw_mkv.reshape(N_DEV, 128, D_MODEL), gw_out.reshape(N_DEV, 128, D_MODEL), gw_up]
    sib = _split_start(_sibling_copies, N_CHIPS, shares1,
                       [lax.empty((N_CHIPS,) + a.shape[1:], a.dtype) for a in shares1], "grads_to_sibling_small_start")
    da, gw_conv = _conv_bwd(sib[4], pa, dya, w_conv_f)
    shares1, from_sibling = _split_wait(_sibling_copies, N_CHIPS, *sib[:4], da, "grads_to_sibling_small_wait")
    send1, recv1, srcs1, lands1, token1 = _chip_exchange_start(
        _pair_add(shares1, from_sibling, core, "grads_pair_add_small"), "grads_to_chips_start_small")
    dq, dbz, dkv, g_sink = _attn_bwd(token1, pq, pkv, pbz, dyb, sink, tabs)
    dparts = (da, dq, dkv, dbz, dmq, dmz, dg)
    gw_int = _gw_in(dparts, h)
    send2, recv2, srcs2, lands2, token2 = _chip_exchange_start(
        [_sibling_reduce(gw_int.reshape(N_DEV, SHARD_IN, D_MODEL), "grads_sibling_reduce_w_in")],
        "grads_to_chips_start_w_in")
    grad_x, gg_pre = _dh_bwd(token2, dparts, x2, dy, g_pre, w_int)
    (o_mkv, o_out, o_up, o_int), (l_mkv, l_out, l_up, l_int) = _chip_exchange_wait(
        send1 + send2, recv1 + recv2, srcs1 + srcs2, lands1 + lands2, grad_x, "grads_to_chips_wait")

    small = _gather_start([_small_pack((gg_pre, gw_conv, g_sink, gg_mem, gg_post, loss_p))], me,
                          "small_gather_start")

    w_in_t = _sum_adamw(small[4], o_int, l_int, chip, 0, w_in[0].T, m_w_in[0].T, v_w_in[0].T, "adamw_w_in", tiles=2)
    g_w_in, d_w_in, nm_w_in, nv_w_in = (t.T for t in w_in_t)
    (g_mkv, d_mkv, nm_mkv, nv_mkv), (g_out, d_out, nm_out, nv_out), *up = _sum_adamw_group(
        w_in_t[0],
        [(o_mkv, l_mkv, 0, w_mem_kv[0], m_w_mem_kv[0], v_w_mem_kv[0]),
         (o_out, l_out, 0, w_out[0], m_w_out[0], v_w_out[0]),
         (o_up, l_up, 0, w_up_a[0], m_w_up_a[0], v_w_up_a[0]),
         (o_up, l_up, 1, w_up_b[0], m_w_up_b[0], v_w_up_b[0]),
         (o_up, l_up, 2, w_up_m[0], m_w_up_m[0], v_w_up_m[0])], chip, "adamw_mid_weights")

    (packs,) = _gather_wait(*small[:4], g_out, "small_gather_wait")
    loss_row, small_g, sd, sm, sv = _small_apply(
        packs, [g_pre, w_conv[0], attn_sink, g_mem, g_post],
        [m_g_pre, m_w_conv[0], m_attn_sink, m_g_mem, m_g_post],
        [v_g_pre, v_w_conv[0], v_attn_sink, v_g_mem, v_g_post])
    loss = loss_row[0, 0]
    g_g_pre, g_conv, g_sink_tot, g_g_mem, g_g_post = small_g

    def lead(a):
        return a[None]

    grads = [g_g_pre, lead(g_w_in), lead(g_conv), g_sink_tot, g_g_mem, lead(g_mkv), lead(up[0][0]),
             lead(up[1][0]), lead(up[2][0]), lead(g_out), g_g_post]

    def assemble(small, big_in, big_mkv, big_up, big_out):
        return [small[0], lead(big_in), lead(small[1]), small[2], small[3], lead(big_mkv), lead(big_up[0]),
                lead(big_up[1]), lead(big_up[2]), lead(big_out), small[4]]

    deltas = assemble(sd, d_w_in, d_mkv, [u[1] for u in up], d_out)
    new_m = assemble(sm, nm_w_in, nm_mkv, [u[2] for u in up], nm_out)
    new_v = assemble(sv, nv_w_in, nv_mkv, [u[3] for u in up], nv_out)
    return (loss, grad_x[None], *grads, *deltas, *new_m, *new_v)
```

```python
import functools

import jax
import jax.numpy as jnp
from jax import lax
from jax.experimental import pallas as pl
from jax.experimental.pallas import tpu as pltpu

F32 = jnp.float32
BF16 = jnp.bfloat16
MESH = pl.DeviceIdType.MESH

N_DEV = 8
D_MODEL = 1024
EPS = 1e-6
ROPE_THETA = 500000.0
ROT_DIM = 16
HEAD_DIM = 64
ATTN_BLOCK = 128
MEM_HEADS = 4
MEM_HEAD_DIM = 128
ATTN_SCALE = HEAD_DIM ** -0.5
MEM_SCALE = MEM_HEAD_DIM ** -0.5

ADAM_LR = 0.001
ADAM_B1 = 0.9
ADAM_B2 = 0.999
ADAM_EPS = 1e-08
ADAM_WD = 0.01
ADAM_STEP = 10

SEG_A = (0, 2048)
SEG_BQ = (2048, 512)
SEG_BKV = (2560, 256)
SEG_BZ = (2816, 512)
SEG_MQ = (3328, 512)
SEG_MZ = (3840, 512)
SEG_G = (4352, 3072)
SEGS = (SEG_A, SEG_BQ, SEG_BKV, SEG_BZ, SEG_MQ, SEG_MZ, SEG_G)
IN_WIDTH = 7424
SHARD_IN = IN_WIDTH // N_DEV

V7X_VMEM_BYTES = 64 * 1024 * 1024
CALL_VMEM_MB = 57
ANY = pl.BlockSpec(memory_space=pl.ANY)


def _params():
    assert CALL_VMEM_MB * 1024 * 1024 < V7X_VMEM_BYTES
    return pltpu.CompilerParams(dimension_semantics=("arbitrary",), vmem_limit_bytes=CALL_VMEM_MB * 1024 * 1024)


def _full(shape):
    zeros = (0,) * len(shape)
    return pl.BlockSpec(shape, lambda i: zeros)


def _rows(tm, width):
    return pl.BlockSpec((tm, width), lambda i: (i, 0))


def _dot(a, b):
    return jnp.dot(a, b, preferred_element_type=F32)


def _dot_nt(a, b):
    return lax.dot_general(a, b, (((1,), (1,)), ((), ())), preferred_element_type=F32)


def _dot_tn(a, b):
    return lax.dot_general(a, b, (((0,), (0,)), ((), ())), preferred_element_type=F32)


def _sigmoid(z):
    return 1.0 / (1.0 + jnp.exp(-z))


def _rope(t, cs, s1, s2):
    return t * cs + pltpu.roll(t, 120, 1) * s1 + pltpu.roll(t, 8, 1) * s2


def _rope_t(d, cs, s1, s2):
    return d * cs + pltpu.roll(d * s1, 8, 1) + pltpu.roll(d * s2, 120, 1)


def _gather_side(s, w_mkv, w_out, w_ups):
    half = ROT_DIM // 2
    inv_freq = jnp.power(jnp.float32(ROPE_THETA), -jnp.arange(half, dtype=F32) * (2.0 / ROT_DIM))
    freq_row = jnp.tile(jnp.concatenate([inv_freq, inv_freq, jnp.zeros((HEAD_DIM - ROT_DIM,), F32)]), 2)[None, :]

    def fn(in_refs, out_refs):
        f_ref, mkv_ref, out_ref, *up_refs = in_refs
        t_ref, mkv_bf, out_bf, up_bf = out_refs
        mkv_bf[...] = mkv_ref[...].astype(BF16)
        out_bf[...] = out_ref[...].astype(BF16)
        for k, up_ref in enumerate(up_refs):
            up_bf[512 * k:512 * k + 512, :] = up_ref[...].astype(BF16)
        pos = lax.broadcasted_iota(jnp.int32, (s, 128), 0).astype(F32)
        d = lax.broadcasted_iota(jnp.int32, (s, 128), 1) & (HEAD_DIM - 1)
        ang = pos * f_ref[...]
        cos, sin = jnp.cos(ang), jnp.sin(ang)
        lo, hi = d < half, (d >= half) & (d < ROT_DIM)
        t_ref[0] = jnp.where(lo | hi, cos, 1.0)
        t_ref[1] = jnp.where(lo, -sin, 0.0)
        t_ref[2] = jnp.where(hi, sin, 0.0)

    return ([freq_row, w_mkv, w_out, *w_ups],
            [jax.ShapeDtypeStruct((3, s, 128), F32), jax.ShapeDtypeStruct(w_mkv.shape, BF16),
             jax.ShapeDtypeStruct(w_out.shape, BF16), jax.ShapeDtypeStruct((1536, 128), BF16)], fn)


def _load_once(pairs, sems):
    @pl.when(pl.program_id(0) == 0)
    def _():
        cps = [pltpu.make_async_copy(src, dst, sems.at[k]) for k, (src, dst) in enumerate(pairs)]
        for cp in cps:
            cp.start()
        for cp in cps:
            cp.wait()


def _my_place():
    x, y, c = lax.axis_index("x"), lax.axis_index("y"), lax.axis_index("c")
    return x, y, c


def _all_gather(arrs, name, splits=None, side=None):
    n = len(arrs)
    if splits is None:
        splits = [[(0, a.shape[0])] for a in arrs]
    pieces = [(a, r0, rn) for a in range(n) for r0, rn in splits[a]]
    n_p = len(pieces)
    side_in, side_out, side_fn = side if side is not None else ((), (), None)
    m, q = len(side_in), len(side_out)

    def body(*refs):
        ins, outs = refs[:n], refs[n + m:2 * n + m]
        send_sems, recv_sems, local_sems = refs[2 * n + m + q:]
        x, y, c = _my_place()
        me, sibling = (x, y, c), (x, y, 1 - c)

        def route(core):
            first = (jnp.bitwise_xor(x, 1 - core), jnp.bitwise_xor(y, core), core)
            second = (jnp.bitwise_xor(x, core), jnp.bitwise_xor(y, 1 - core), core)
            return first, second, (1 - x, 1 - y, core)

        def idx(px, py, pc):
            return 4 * px + 2 * py + pc

        def copy(p, k, block, to, own=False):
            a, r0, rn = pieces[p]
            dst = outs[a].at[idx(*block), pl.ds(r0, rn)]
            return pltpu.make_async_remote_copy(
                src_ref=ins[a].at[pl.ds(r0, rn)] if own else dst, dst_ref=dst,
                send_sem=send_sems.at[p * 7 + k], recv_sem=recv_sems.at[p * 7 + k],
                device_id=to, device_id_type=MESH)

        nbr1, nbr2, diag = route(c)
        mine = [pltpu.make_async_copy(ins[a], outs[a].at[idx(*me)], local_sems.at[a]) for a in range(n)]
        for cp in mine:
            cp.start()
        sent = []
        for p in range(n_p):
            for k, to in enumerate((sibling, nbr1, nbr2)):
                sent.append(copy(p, k, me, to, own=True))
        for cp in sent:
            cp.start()
        if side_fn is not None:
            side_fn(refs[n:n + m], refs[2 * n + m:2 * n + m + q])
        for k_in, block, onward in ((1, nbr1, ((3, nbr2), (4, sibling))), (2, nbr2, ((5, sibling),)),
                                    (3, diag, ((6, sibling),))):
            for p in range(n_p):
                copy(p, k_in, block, me).wait_recv()
                for k_out, to in onward:
                    cp = copy(p, k_out, block, to)
                    cp.start()
                    sent.append(cp)
        s1, s2, sd = route(1 - c)
        for k_in, block in ((0, sibling), (4, s1), (5, s2), (6, sd)):
            for p in range(n_p):
                copy(p, k_in, block, me).wait_recv()
        for cp in sent:
            cp.wait_send()
        for cp in mine:
            cp.wait()

    return pl.pallas_call(
        body, name=name,
        out_shape=[jax.ShapeDtypeStruct((N_DEV,) + a.shape, a.dtype) for a in arrs] + list(side_out),
        in_specs=[ANY] * n + [pl.BlockSpec(memory_space=pltpu.VMEM)] * m,
        out_specs=[ANY] * n + [pl.BlockSpec(memory_space=pltpu.VMEM)] * q,
        scratch_shapes=[pltpu.SemaphoreType.DMA((7 * n_p,)), pltpu.SemaphoreType.DMA((7 * n_p,)),
                        pltpu.SemaphoreType.DMA((n,))],
        compiler_params=pltpu.CompilerParams(vmem_limit_bytes=32 * 1024 * 1024),
    )(*arrs, *side_in)


N_CHIPS = 4


def _sibling_reduce(arr, name):
    _, r, c = arr.shape

    def body(in_ref, out_ref, land, a_buf, b_buf, o_buf, send_sems, recv_sems, local_sems):
        x, y, core = _my_place()
        cps = [pltpu.make_async_remote_copy(
            src_ref=in_ref.at[2 * j + (1 - core)], dst_ref=land.at[j], send_sem=send_sems.at[j],
            recv_sem=recv_sems.at[j], device_id=(x, y, 1 - core), device_id_type=MESH) for j in range(N_CHIPS)]
        for cp in cps:
            cp.start()
        store = None
        for j in range(N_CHIPS):
            mine = pltpu.make_async_copy(in_ref.at[2 * j + core], a_buf, local_sems.at[0])
            mine.start()
            cps[j].wait_recv()
            theirs = pltpu.make_async_copy(land.at[j], b_buf, local_sems.at[1])
            theirs.start()
            mine.wait()
            theirs.wait()
            if store is not None:
                store.wait()
            o_buf[...] = (a_buf[...].astype(F32) + b_buf[...].astype(F32)).astype(BF16)
            store = pltpu.make_async_copy(o_buf, out_ref.at[j], local_sems.at[2])
            store.start()
        store.wait()
        for cp in cps:
            cp.wait_send()

    return pl.pallas_call(
        body, name=name,
        out_shape=[jax.ShapeDtypeStruct((N_CHIPS, r, c), BF16)] * 2,
        in_specs=[ANY], out_specs=[ANY, ANY],
        scratch_shapes=[pltpu.VMEM((r, c), BF16)] * 3
        + [pltpu.SemaphoreType.DMA((N_CHIPS,)), pltpu.SemaphoreType.DMA((N_CHIPS,)), pltpu.SemaphoreType.DMA((3,))],
        compiler_params=pltpu.CompilerParams(vmem_limit_bytes=32 * 1024 * 1024),
    )(arr)[0]


def _sibling_copies(srcs, lands, send_sems, recv_sems):
    x, y, c = _my_place()
    cps = []
    for j in range(N_CHIPS):
        for a in range(len(srcs)):
            k = a * N_CHIPS + j
            cps.append(pltpu.make_async_remote_copy(
                src_ref=srcs[a].at[2 * j + (1 - c)], dst_ref=lands[a].at[j], send_sem=send_sems[k],
                recv_sem=recv_sems[k], device_id=(x, y, 1 - c), device_id_type=MESH))
    return cps


def _pair_add(mine, recv, core, name):
    n = len(mine)

    def body(c_ref, *refs):
        for a in range(n):
            refs[2 * n + a][...] = (refs[a][...].astype(F32) + refs[n + a][...].astype(F32)).astype(BF16)

    def blk(a):
        return (None,) + a.shape[1:]

    grid_spec = pltpu.PrefetchScalarGridSpec(
        num_scalar_prefetch=1, grid=(N_CHIPS,),
        in_specs=[pl.BlockSpec(blk(a), lambda j, c_ref: (2 * j + c_ref[0], 0, 0)) for a in mine]
        + [pl.BlockSpec(blk(a), lambda j, c_ref: (j, 0, 0)) for a in recv],
        out_specs=[pl.BlockSpec(blk(a), lambda j, c_ref: (j, 0, 0)) for a in recv])
    return pl.pallas_call(
        body, name=name, grid_spec=grid_spec,
        out_shape=[jax.ShapeDtypeStruct(a.shape, BF16) for a in recv],
        compiler_params=_params(),
    )(core, *mine, *recv)


HBM = pl.BlockSpec(memory_space=pltpu.HBM)
SEM = pl.BlockSpec(memory_space=pltpu.SEMAPHORE)
N_PEER_CHIPS = 3
TOKEN = (8, 128)


def _chip_copies(srcs, lands, send_sems, recv_sems):
    x, y, c = _my_place()
    my_chip = 2 * x + y
    peers = [(x, 1 - y), (1 - x, y), (1 - x, 1 - y)]
    cps = []
    for k, (px, py) in enumerate(peers):
        for a in range(len(srcs)):
            j = a * N_PEER_CHIPS + k
            cps.append(pltpu.make_async_remote_copy(
                src_ref=srcs[a].at[2 * px + py], dst_ref=lands[a].at[my_chip],
                send_sem=send_sems[j], recv_sem=recv_sems[j],
                device_id=(px, py, c), device_id_type=MESH))
    return cps


N_PEERS = N_DEV - 1


def _gather_copies(srcs, lands, send_sems, recv_sems):
    x, y, c = _my_place()
    me_idx = 4 * x + 2 * y + c
    flips = [(0, 0, 1), (0, 1, 0), (1, 0, 0), (0, 1, 1), (1, 0, 1), (1, 1, 0), (1, 1, 1)]
    cps = []
    for k, (fx, fy, fc) in enumerate(flips):
        peer = ((1 - x) if fx else x, (1 - y) if fy else y, (1 - c) if fc else c)
        for a in range(len(srcs)):
            j = a * N_PEERS + k
            cps.append(pltpu.make_async_remote_copy(
                src_ref=srcs[a], dst_ref=lands[a].at[me_idx], send_sem=send_sems[j], recv_sem=recv_sems[j],
                device_id=peer, device_id_type=MESH))
    return cps


def _split_start(copies, per_array, arrs, lands, name):
    arrs, lands = list(arrs), list(lands)
    n = len(arrs)
    k = n * per_array

    def body(*refs):
        srcs, land_refs = refs[:n], refs[n:2 * n]
        send_sems, recv_sems = refs[2 * n:2 * n + k], refs[2 * n + k:2 * n + 2 * k]
        token = refs[-1]
        for cp in copies(srcs, land_refs, send_sems, recv_sems):
            cp.start()
        token[...] = jnp.zeros_like(token)

    hbm_arrs = [pltpu.with_memory_space_constraint(a, pltpu.HBM) for a in arrs]
    lands = [pltpu.with_memory_space_constraint(a, pltpu.HBM) for a in lands]
    res = pl.pallas_call(
        body, name=name,
        out_shape=[pltpu.SemaphoreType.DMA(())] * (2 * k) + [pltpu.HBM(a.shape, a.dtype) for a in arrs + lands]
        + [jax.ShapeDtypeStruct(TOKEN, F32)],
        in_specs=[HBM] * (2 * n),
        out_specs=[SEM] * (2 * k) + [HBM] * (2 * n) + [pl.BlockSpec(memory_space=pltpu.VMEM)],
        input_output_aliases={a: 2 * k + a for a in range(2 * n)},
        compiler_params=pltpu.CompilerParams(has_side_effects=pltpu.SideEffectType.DATAFLOW_SIDE_EFFECTING),
    )(*hbm_arrs, *lands)
    return res[:k], res[k:2 * k], res[2 * k:2 * k + n], res[2 * k + n:2 * k + 2 * n], res[-1]


def _split_wait(copies, per_array, send_sems, recv_sems, srcs, lands, after, name):
    n = len(srcs)
    k = n * per_array

    def body(*refs):
        src_refs, land_refs = refs[:n], refs[n:2 * n]
        s_sems, r_sems = refs[2 * n:2 * n + k], refs[2 * n + k:2 * n + 2 * k]
        for cp in copies(src_refs, land_refs, s_sems, r_sems):
            cp.wait_send()
            cp.wait_recv()

    res = pl.pallas_call(
        body, name=name,
        out_shape=[pltpu.HBM(a.shape, a.dtype) for a in list(srcs) + list(lands)],
        in_specs=[HBM] * (2 * n) + [SEM] * (2 * k) + [ANY],
        out_specs=[HBM] * (2 * n),
        input_output_aliases={a: a for a in range(2 * n)},
        compiler_params=pltpu.CompilerParams(has_side_effects=pltpu.SideEffectType.DATAFLOW_SIDE_EFFECTING),
    )(*srcs, *lands, *send_sems, *recv_sems, after)
    return res[:n], res[n:]


def _chip_exchange_start(arrs, name):
    return _split_start(_chip_copies, N_PEER_CHIPS, arrs, [lax.empty(a.shape, a.dtype) for a in arrs], name)


def _chip_exchange_wait(send_sems, recv_sems, srcs, lands, after, name):
    return _split_wait(_chip_copies, N_PEER_CHIPS, send_sems, recv_sems, srcs, lands, after, name)


def _gather_start(arrs, me_idx, name):
    lands = [lax.dynamic_update_slice(lax.empty((N_DEV,) + a.shape, a.dtype), a[None], (me_idx, 0, 0)) for a in arrs]
    return _split_start(_gather_copies, N_PEERS, arrs, lands, name)


def _gather_wait(send_sems, recv_sems, srcs, lands, after, name):
    return _split_wait(_gather_copies, N_PEERS, send_sems, recv_sems, srcs, lands, after, name)[1]


def _proj_fwd(after, x, g_pre, w_int, tabs):
    s = x.shape[0]
    tm = min(512, s)

    def body(after_ref, x_ref, g_ref, t_ref, w_hbm,
             h_ref, pa_ref, pq_ref, pkv_ref, pbz_ref, pmq_ref, pmz_ref, pg_ref, w_vm, sems):
        _load_once([(w_hbm, w_vm)], sems)
        xf = x_ref[...]
        r = lax.rsqrt(jnp.mean(xf * xf, axis=-1, keepdims=True) + EPS)
        h = ((xf * r) * g_ref[...]).astype(BF16)
        h_ref[...] = h
        cs, s1, s2 = t_ref[0], t_ref[1], t_ref[2]

        def mm(seg, c0, width):
            return _dot_nt(h, w_vm[seg[0] + c0:seg[0] + c0 + width, :])

        for c0 in range(0, SEG_A[1], 512):
            pa_ref[:, c0:c0 + 512] = mm(SEG_A, c0, 512).astype(BF16)
        q = mm(SEG_BQ, 0, 512)
        for b in range(4):
            pq_ref[:, 128 * b:128 * b + 128] = _rope(q[:, 128 * b:128 * b + 128], cs, s1, s2).astype(BF16)
        kv = mm(SEG_BKV, 0, 256)
        pkv_ref[:, 0:128] = _rope(kv[:, 0:128], cs, s1, s2).astype(BF16)
        pkv_ref[:, 128:256] = kv[:, 128:256].astype(BF16)
        pbz_ref[...] = mm(SEG_BZ, 0, 512).astype(BF16)
        pmq_ref[...] = mm(SEG_MQ, 0, 512).astype(BF16)
        pmz_ref[...] = mm(SEG_MZ, 0, 512).astype(BF16)
        for c0 in range(0, SEG_G[1], 512):
            pg_ref[:, c0:c0 + 512] = mm(SEG_G, c0, 512).astype(BF16)

    widths = (D_MODEL, 2048, 512, 256, 512, 512, 512, 3072)
    return pl.pallas_call(
        body, name="proj_fwd", grid=(s // tm,),
        out_shape=[jax.ShapeDtypeStruct((s, w), BF16) for w in widths],
        in_specs=[_full(TOKEN), _rows(tm, D_MODEL), _full((1, D_MODEL)),
                  pl.BlockSpec((3, tm, 128), lambda i: (0, i, 0)), ANY],
        out_specs=[_rows(tm, w) for w in widths],
        scratch_shapes=[pltpu.VMEM((IN_WIDTH, D_MODEL), BF16), pltpu.SemaphoreType.DMA((1,))],
        compiler_params=_params(),
    )(after, x, g_pre, tabs, w_int)


def _mem_kv_fwd(mem, g_mem, w_mkv):
    m = mem.shape[0]

    def body(mem_ref, g_ref, w_ref, mn_ref, mkv_ref):
        xf = mem_ref[...]
        r = lax.rsqrt(jnp.mean(xf * xf, axis=-1, keepdims=True) + EPS)
        mn = ((xf * r) * g_ref[...]).astype(BF16)
        mn_ref[...] = mn
        mkv_ref[...] = _dot(mn, w_ref[...]).astype(BF16)

    return pl.pallas_call(
        body, name="mem_kv_fwd", grid=(1,),
        out_shape=[jax.ShapeDtypeStruct((m, D_MODEL), BF16)] * 2,
        in_specs=[_full((m, D_MODEL)), _full((1, D_MODEL)), _full((D_MODEL, D_MODEL))],
        out_specs=[_full((m, D_MODEL))] * 2,
        compiler_params=_params(),
    )(mem, g_mem, w_mkv)


def _halo_specs(s, tm, rows, width):
    nblk = s // rows
    prev = pl.BlockSpec((rows, width), lambda i: (jnp.maximum(i * (tm // rows) - 1, 0), 0))
    nxt = pl.BlockSpec((rows, width), lambda i: (jnp.minimum((i + 1) * (tm // rows), nblk - 1), 0))
    return prev, nxt


def _conv_common(pa, cu_prev, cu_next, w, tm):
    b, c, u, z = (pa[:, 512 * k:512 * k + 512] for k in range(4))
    cu = c * u
    row = lax.broadcasted_iota(jnp.int32, (tm, 512), 0)
    cu_m1 = jnp.where(row == 0, cu_prev, pltpu.roll(cu, 1, 0))
    cu_p1 = jnp.where(row == tm - 1, cu_next, pltpu.roll(cu, tm - 1, 0))
    y = cu_m1 * w[0:1] + cu * w[1:2] + cu_p1 * w[2:3]
    sig = _sigmoid(z)
    return b, c, u, z, cu, cu_m1, cu_p1, y, sig, row


def _conv_fwd(pa, w_conv):
    s = pa.shape[0]
    tm = min(512, s)
    nt = s // tm

    def body(pa_ref, pp_ref, pn_ref, w_ref, ya_ref):
        i = pl.program_id(0)
        prev_row = pp_ref[...].astype(F32)[15:16, :]
        next_row = pn_ref[...].astype(F32)[0:1, :]
        b, _, _, z, _, _, _, y, sig, _ = _conv_common(
            pa_ref[...].astype(F32),
            jnp.where(i == 0, 0.0, prev_row[:, 512:1024] * prev_row[:, 1024:1536]),
            jnp.where(i == nt - 1, 0.0, next_row[:, 512:1024] * next_row[:, 1024:1536]), w_ref[...], tm)
        ya_ref[...] = (b * y * (z * sig)).astype(BF16)

    prev, nxt = _halo_specs(s, tm, 16, 2048)
    return pl.pallas_call(
        body, name="conv_fwd", grid=(nt,),
        out_shape=jax.ShapeDtypeStruct((s, 512), BF16),
        in_specs=[_rows(tm, 2048), prev, nxt, _full((3, 512))],
        out_specs=_rows(tm, 512),
        compiler_params=_params(),
    )(pa, pa, pa, w_conv)


def _heads_to_lanes(a, g, row):
    low = row < HEAD_DIM
    parts = []
    for b in (2 * g, 2 * g + 1):
        t = jnp.transpose(a[:, 128 * b:128 * b + 128])
        swapped = pltpu.roll(t, HEAD_DIM, 0)
        if g == 0:
            parts += [jnp.where(low, t, 0.0), jnp.where(low, swapped, 0.0)]
        else:
            parts += [jnp.where(low, 0.0, swapped), jnp.where(low, 0.0, t)]
    return jnp.concatenate(parts, axis=1)


def _lanes_to_heads(t0, t1, row):
    low = row < HEAD_DIM
    blocks = []
    for b in range(4):
        g = b // 2
        tg = (t0, t1)[g]
        je = 2 * (b - 2 * g)
        even, odd = tg[:, 128 * je:128 * je + 128], tg[:, 128 * je + 128:128 * je + 256]
        if g == 0:
            t = jnp.where(low, even, pltpu.roll(odd, HEAD_DIM, 0))
        else:
            t = jnp.where(low, pltpu.roll(even, HEAD_DIM, 0), odd)
        blocks.append(jnp.transpose(t))
    return jnp.concatenate(blocks, axis=1)


WINDOW_KEYS = 3 * ATTN_BLOCK
STACKED = 4 * ATTN_BLOCK
KEY_CHUNK = 32
MAX_BLOCKS_IN_STEP = 8


def _fill_band_bias(bias, nb):
    assert nb >= 2
    c = lax.broadcasted_iota(jnp.int32, (WINDOW_KEYS, STACKED), 0)
    r = lax.broadcasted_iota(jnp.int32, (WINDOW_KEYS, STACKED), 1) & (ATTN_BLOCK - 1)
    band = (c >= r) & (c <= r + 2 * ATTN_BLOCK)
    for v, ok in enumerate((band, band & (c >= ATTN_BLOCK), band & (c < 2 * ATTN_BLOCK))):
        bias[v] = jnp.where(ok, 0.0, -jnp.inf)


def _bias_variant(n, nb):
    return jnp.where(n == 0, 1, jnp.where(n == nb - 1, 2, 0))


def _sink_row(sink_ref, g):
    return jnp.concatenate([jnp.full((1, ATTN_BLOCK), sink_ref[4 * g + j], F32) for j in range(4)], axis=1)


def _softmax_keys_major(sc, bias, variant, sink, e_scr):
    chunks = [pl.ds(k * KEY_CHUNK, KEY_CHUNK) for k in range(WINDOW_KEYS // KEY_CHUNK)]
    rows = [slice(k * KEY_CHUNK, (k + 1) * KEY_CHUNK) for k in range(WINDOW_KEYS // KEY_CHUNK)]
    m_run = jnp.full((KEY_CHUNK, STACKED), -jnp.inf, F32)
    for ck, rw in zip(chunks, rows):
        m_run = jnp.maximum(m_run, sc[rw] + bias[variant, ck, :])
    m = jnp.maximum(jnp.max(m_run, axis=0, keepdims=True), sink)
    l_run = jnp.zeros((KEY_CHUNK, STACKED), F32)
    for ck, rw in zip(chunks, rows):
        e = jnp.exp(sc[rw] + bias[variant, ck, :] - m)
        l_run += e
        e_scr[rw, :] = e.astype(BF16)
    es = jnp.exp(sink - m)
    inv = 1.0 / (jnp.sum(l_run, axis=0, keepdims=True) + es)
    return inv, es * inv


def _fill_padded(kv_ref, kpad, vpad, s):
    zero = jnp.zeros((ATTN_BLOCK, 128), BF16)
    kpad[0:ATTN_BLOCK, :] = zero
    vpad[0:ATTN_BLOCK, :] = zero
    kpad[ATTN_BLOCK + s:2 * ATTN_BLOCK + s, :] = zero
    vpad[ATTN_BLOCK + s:2 * ATTN_BLOCK + s, :] = zero
    kpad[ATTN_BLOCK:ATTN_BLOCK + s, :] = kv_ref[:, 0:128]
    vpad[ATTN_BLOCK:ATTN_BLOCK + s, :] = kv_ref[:, 128:256]


def _attn_fwd(pq, pkv, pbz, sink):
    s = pq.shape[0]
    nb = s // ATTN_BLOCK
    bps = min(MAX_BLOCKS_IN_STEP, nb)

    def body(sink_ref, q_ref, z_ref, kv_ref, yb_ref, kpad, vpad, bias, e_scr):
        i = pl.program_id(0)

        @pl.when(i == 0)
        def _():
            _fill_padded(kv_ref, kpad, vpad, s)
            _fill_band_bias(bias, nb)

        row = lax.broadcasted_iota(jnp.int32, (ATTN_BLOCK, 128), 0)
        for b in range(bps):
            n = i * bps + b
            rows = slice(b * ATTN_BLOCK, (b + 1) * ATTN_BLOCK)
            start = pl.multiple_of(n * ATTN_BLOCK, ATTN_BLOCK)
            kw, vw = kpad[pl.ds(start, WINDOW_KEYS), :], vpad[pl.ds(start, WINDOW_KEYS), :]
            qf = q_ref[rows, :].astype(F32)
            variant = _bias_variant(n, nb)
            outs = []
            for g in range(2):
                e_bg = e_scr.at[2 * b + g]
                qt = (_heads_to_lanes(qf, g, row) * ATTN_SCALE).astype(BF16)
                inv, _ = _softmax_keys_major(_dot(kw, qt), bias, variant, _sink_row(sink_ref, g), e_bg)
                outs.append(_dot_tn(vw, e_bg[...]) * inv)
            attn = _lanes_to_heads(outs[0], outs[1], row)
            z = z_ref[rows, :].astype(F32)
            yb_ref[rows, :] = (attn * (z * _sigmoid(z))).astype(BF16)

    tq = bps * ATTN_BLOCK
    return pl.pallas_call(
        body, name="attn_fwd", grid=(s // tq,),
        out_shape=jax.ShapeDtypeStruct((s, 512), BF16),
        in_specs=[pl.BlockSpec(memory_space=pltpu.SMEM), _rows(tq, 512), _rows(tq, 512), _full((s, 256))],
        out_specs=_rows(tq, 512),
        scratch_shapes=[pltpu.VMEM((s + 2 * ATTN_BLOCK, 128), BF16)] * 2
        + [pltpu.VMEM((3, WINDOW_KEYS, STACKED), F32),
           pltpu.VMEM((2 * bps, WINDOW_KEYS, STACKED), BF16)],
        compiler_params=_params(),
    )(sink, pq, pbz, pkv)


def _mem_softmax_t(q, mk):
    sc = _dot_nt(mk, q) * MEM_SCALE
    e = jnp.exp(sc - jnp.max(sc, axis=0, keepdims=True))
    return e * (1.0 / jnp.sum(e, axis=0, keepdims=True))


def _mem_attn_fwd(pmq, pmz, mkv):
    s = pmq.shape[0]
    m = mkv.shape[0]
    tm = min(512, s)

    def body(q_ref, z_ref, mk_ref, mv_ref, ym_ref):
        z = z_ref[...].astype(F32)
        sz = z * _sigmoid(z)
        for h in range(MEM_HEADS):
            cols = slice(128 * h, 128 * h + 128)
            pt = _mem_softmax_t(q_ref[:, cols], mk_ref[:, cols])
            o = _dot_tn(pt.astype(BF16), mv_ref[:, cols])
            ym_ref[:, cols] = (o * sz[:, cols]).astype(BF16)

    return pl.pallas_call(
        body, name="mem_attn_fwd", grid=(s // tm,),
        out_shape=jax.ShapeDtypeStruct((s, 512), BF16),
        in_specs=[_rows(tm, 512), _rows(tm, 512), pl.BlockSpec((m, 512), lambda i: (0, 0)),
                  pl.BlockSpec((m, 512), lambda i: (0, 1))],
        out_specs=_rows(tm, 512),
        compiler_params=_params(),
    )(pmq, pmz, mkv, mkv)


def _mid(ya, yb, ym, pg, x, target, g_post, w_up, w_out):
    s = x.shape[0]
    tm = min(256, s)
    nt = s // tm

    def body(ya_ref, yb_ref, ym_ref, pg_ref, x_ref, t_ref, gp_ref, wup_hbm, wout_hbm,
             dg_ref, dya_ref, dyb_ref, dym_ref, dy_ref, loss_ref, ggp_ref, mb_ref, dob_ref, du_ref,
             wup_vm, wout_vm, sems):
        i = pl.program_id(0)
        _load_once([(wup_hbm.at[d], wup_vm.at[:, pl.ds(128 * d, 128)]) for d in range(N_DEV)]
                   + [(wout_hbm, wout_vm)], sems)

        @pl.when(i == 0)
        def _():
            loss_ref[...] = jnp.zeros_like(loss_ref)
            ggp_ref[...] = jnp.zeros_like(ggp_ref)

        ys = (ya_ref[...], yb_ref[...], ym_ref[...])
        us = [_dot(ys[k], wup_vm[512 * k:512 * k + 512, :]) for k in range(3)]
        gates = [_sigmoid(pg_ref[:, 1024 * k:1024 * k + 1024].astype(F32)) for k in range(3)]
        merged = gates[0] * us[0] + gates[1] * us[1] + gates[2] * us[2]
        mb = merged.astype(BF16)
        mb_ref[...] = mb
        out = _dot(mb, wout_vm[...])
        r = lax.rsqrt(jnp.mean(out * out, axis=-1, keepdims=True) + EPS)
        on = out * r
        gp = gp_ref[...]
        err = (x_ref[...] + on * gp) - t_ref[...]
        loss_ref[...] += 0.5 * jnp.sum(err * err) * (1.0 / D_MODEL)
        dy = err * (1.0 / D_MODEL)
        dy_ref[...] = dy
        ggp_ref[...] += jnp.sum(dy * on, axis=0, keepdims=True)
        a = dy * gp
        d_out = r * (a - on * jnp.mean(a * on, axis=-1, keepdims=True))
        dob = d_out.astype(BF16)
        dob_ref[...] = dob
        d_merged = _dot_nt(dob, wout_vm[...])
        d_refs = (dya_ref, dyb_ref, dym_ref)
        for k in range(3):
            g = gates[k]
            du_f = d_merged * g
            dg_ref[:, 1024 * k:1024 * k + 1024] = (du_f * us[k] * (1.0 - g)).astype(BF16)
            du = du_f.astype(BF16)
            du_ref[k] = du
            d_refs[k][...] = _dot_nt(du, wup_vm[512 * k:512 * k + 512, :]).astype(BF16)

    return pl.pallas_call(
        body, name="mid", grid=(nt,),
        out_shape=[jax.ShapeDtypeStruct((s, 3072), BF16)] + [jax.ShapeDtypeStruct((s, 512), BF16)] * 3
        + [jax.ShapeDtypeStruct((s, D_MODEL), F32), jax.ShapeDtypeStruct((8, 128), F32),
           jax.ShapeDtypeStruct((1, D_MODEL), F32), jax.ShapeDtypeStruct((s, D_MODEL), BF16),
           jax.ShapeDtypeStruct((s, D_MODEL), BF16), jax.ShapeDtypeStruct((3, s, D_MODEL), BF16)],
        in_specs=[_rows(tm, 512)] * 3 + [_rows(tm, 3072), _rows(tm, D_MODEL), _rows(tm, D_MODEL),
                                         _full((1, D_MODEL)), ANY, ANY],
        out_specs=[_rows(tm, 3072)] + [_rows(tm, 512)] * 3
        + [_rows(tm, D_MODEL), _full((8, 128)), _full((1, D_MODEL)), _rows(tm, D_MODEL), _rows(tm, D_MODEL),
           pl.BlockSpec((3, tm, D_MODEL), lambda i: (0, i, 0))],
        scratch_shapes=[pltpu.VMEM((1536, D_MODEL), BF16), pltpu.VMEM((D_MODEL, D_MODEL), BF16),
                        pltpu.SemaphoreType.DMA((N_DEV + 1,))],
        compiler_params=_params(),
    )(ya, yb, ym, pg, x, target, g_post, w_up, w_out)


def _gw_mid(mb, dob, ys, du):
    s = mb.shape[0]
    tn = 256
    n_out = D_MODEL // tn
    tiles = [(0, c0, 0) for c0 in range(0, D_MODEL, tn)]
    tiles += [(1 + k, c0, 1 + k) for k in range(3) for c0 in range(0, 512, tn)]
    n_t = len(tiles)

    def body(mb_hbm, ya_hbm, yb_hbm, ym_hbm, dob_hbm, du_hbm, out_hbm, up_hbm,
             lhs, rhs, res_out, res_up, in_sems, rhs_sems, out_sems):
        lhs_hbm = (mb_hbm, ya_hbm, yb_hbm, ym_hbm)

        def load(t):
            a, c0, _ = tiles[t]
            return pltpu.make_async_copy(lhs_hbm[a].at[:, pl.ds(c0, tn)], lhs.at[t & 1], in_sems.at[t & 1])

        def load_rhs(g):
            src = dob_hbm if g == 0 else du_hbm.at[g - 1]
            return pltpu.make_async_copy(src, rhs.at[g & 1], rhs_sems.at[g & 1])

        def store(t):
            if t < n_out:
                return pltpu.make_async_copy(res_out.at[t & 1], out_hbm.at[pl.ds(t * tn, tn)], out_sems.at[t & 1])
            rows = pl.ds((t - n_out) * tn, tn)
            return pltpu.make_async_copy(res_up.at[t & 1], up_hbm.at[:, rows, :], out_sems.at[t & 1])

        load_rhs(0).start()
        load(0).start()
        for t, (_, _, g) in enumerate(tiles):
            new_rhs = t == 0 or tiles[t - 1][2] != g
            if t + 1 < n_t:
                load(t + 1).start()
            if new_rhs and g < 3:
                load_rhs(g + 1).start()
            load(t).wait()
            if new_rhs:
                load_rhs(g).wait()
            if t >= 2:
                store(t - 2).wait()
            r = _dot_tn(lhs[t & 1], rhs[g & 1])
            if t < n_out:
                res_out[t & 1] = r.astype(BF16)
            else:
                for d in range(N_DEV):
                    res_up[t & 1, d] = r[:, 128 * d:128 * d + 128].astype(BF16)
            store(t).start()
        store(n_t - 2).wait()
        store(n_t - 1).wait()

    return pl.pallas_call(
        body, name="gw_mid",
        out_shape=[jax.ShapeDtypeStruct((D_MODEL, D_MODEL), BF16), jax.ShapeDtypeStruct((N_DEV, 1536, 128), BF16)],
        in_specs=[ANY] * 6, out_specs=[ANY, ANY],
        scratch_shapes=[pltpu.VMEM((2, s, tn), BF16), pltpu.VMEM((2, s, D_MODEL), BF16),
                        pltpu.VMEM((2, tn, D_MODEL), BF16), pltpu.VMEM((2, N_DEV, tn, 128), BF16),
                        pltpu.SemaphoreType.DMA((2,)), pltpu.SemaphoreType.DMA((2,)),
                        pltpu.SemaphoreType.DMA((2,))],
        compiler_params=pltpu.CompilerParams(vmem_limit_bytes=CALL_VMEM_MB * 1024 * 1024),
    )(mb, *ys, dob, du)


def _conv_bwd(after, pa, dya, w_conv):
    s = pa.shape[0]
    tm = min(512, s)
    nt = s // tm

    def body(after_ref, pa_ref, pp_ref, pn_ref, d_ref, dp_ref, dn_ref, w_ref, da_ref, gw_ref):
        i = pl.program_id(0)
        first, last = i == 0, i == nt - 1

        @pl.when(first)
        def _():
            gw_ref[...] = jnp.zeros_like(gw_ref)

        w = w_ref[...]
        prev_row = pp_ref[...].astype(F32)[15:16, :]
        next_row = pn_ref[...].astype(F32)[0:1, :]
        b, c, u, z, cu, cu_m1, cu_p1, y, sig, row = _conv_common(
            pa_ref[...].astype(F32),
            jnp.where(first, 0.0, prev_row[:, 512:1024] * prev_row[:, 1024:1536]),
            jnp.where(last, 0.0, next_row[:, 512:1024] * next_row[:, 1024:1536]), w, tm)
        sz = z * sig
        dya_t = d_ref[...].astype(F32)
        d_y = dya_t * b * sz

        def halo_dy(p_row, d_row):
            zz = p_row[:, 1536:2048]
            return d_row * p_row[:, 0:512] * (zz * _sigmoid(zz))

        dy_prev = jnp.where(first, 0.0, halo_dy(prev_row, dp_ref[...].astype(F32)[15:16, :]))
        dy_next = jnp.where(last, 0.0, halo_dy(next_row, dn_ref[...].astype(F32)[0:1, :]))
        dy_m1 = jnp.where(row == 0, dy_prev, pltpu.roll(d_y, 1, 0))
        dy_p1 = jnp.where(row == tm - 1, dy_next, pltpu.roll(d_y, tm - 1, 0))
        d_cu = dy_p1 * w[0:1] + d_y * w[1:2] + dy_m1 * w[2:3]
        da_ref[:, 0:512] = (dya_t * y * sz).astype(BF16)
        da_ref[:, 512:1024] = (d_cu * u).astype(BF16)
        da_ref[:, 1024:1536] = (d_cu * c).astype(BF16)
        da_ref[:, 1536:2048] = (dya_t * b * y * (sig + sz * (1.0 - sig))).astype(BF16)
        gw_ref[0:1, :] += jnp.sum(d_y * cu_m1, axis=0, keepdims=True)
        gw_ref[1:2, :] += jnp.sum(d_y * cu, axis=0, keepdims=True)
        gw_ref[2:3, :] += jnp.sum(d_y * cu_p1, axis=0, keepdims=True)

    prev, nxt = _halo_specs(s, tm, 16, 2048)
    dprev, dnxt = _halo_specs(s, tm, 16, 512)
    return pl.pallas_call(
        body, name="conv_bwd", grid=(nt,),
        out_shape=[jax.ShapeDtypeStruct((s, 2048), BF16), jax.ShapeDtypeStruct((8, 512), F32)],
        in_specs=[_full(TOKEN), _rows(tm, 2048), prev, nxt, _rows(tm, 512), dprev, dnxt, _full((3, 512))],
        out_specs=[_rows(tm, 2048), _full((8, 512))],
        compiler_params=_params(),
    )(after, pa, pa, pa, dya, dya, dya, w_conv)


def _attn_bwd(after, pq, pkv, pbz, dyb, sink, tabs):
    s = pq.shape[0]
    nb = s // ATTN_BLOCK
    bps = min(MAX_BLOCKS_IN_STEP, nb)

    def body(sink_ref, after_ref, q_ref, z_ref, d_ref, kv_ref, t_ref,
             dq_ref, dz_ref, dkv_ref, gs_ref, kpad, vpad, dk_acc, dv_acc, bias, e_scr, ds_scr):
        i = pl.program_id(0)

        @pl.when(i == 0)
        def _():
            _fill_padded(kv_ref, kpad, vpad, s)
            _fill_band_bias(bias, nb)
            dk_acc[...] = jnp.zeros_like(dk_acc)
            dv_acc[...] = jnp.zeros_like(dv_acc)
            gs_ref[...] = jnp.zeros_like(gs_ref)

        row = lax.broadcasted_iota(jnp.int32, (ATTN_BLOCK, 128), 0)
        for b in range(bps):
            n = i * bps + b
            rows = slice(b * ATTN_BLOCK, (b + 1) * ATTN_BLOCK)
            start = pl.multiple_of(n * ATTN_BLOCK, ATTN_BLOCK)
            kw, vw = kpad[pl.ds(start, WINDOW_KEYS), :], vpad[pl.ds(start, WINDOW_KEYS), :]
            qf = q_ref[rows, :].astype(F32)
            variant = _bias_variant(n, nb)
            z = z_ref[rows, :].astype(F32)
            sig = _sigmoid(z)
            dyb_t = d_ref[rows, :].astype(F32)
            d_attn = dyb_t * (z * sig)
            outs, dqs = [], []
            dk_w = jnp.zeros((WINDOW_KEYS, 128), F32)
            dv_w = jnp.zeros((WINDOW_KEYS, 128), F32)
            for g in range(2):
                e_bg, ds_bg = e_scr.at[2 * b + g], ds_scr.at[2 * b + g]
                qt = _heads_to_lanes(qf, g, row)
                inv, p_sink = _softmax_keys_major(
                    _dot(kw, (qt * ATTN_SCALE).astype(BF16)), bias, variant, _sink_row(sink_ref, g), e_bg)
                ot = _dot_tn(vw, e_bg[...]) * inv
                outs.append(ot)
                dot_ = _heads_to_lanes(d_attn, g, row)
                delta = jnp.sum(dot_ * ot, axis=0, keepdims=True)
                dpt = _dot(vw, dot_.astype(BF16))
                for k in range(WINDOW_KEYS // KEY_CHUNK):
                    rw = slice(k * KEY_CHUNK, (k + 1) * KEY_CHUNK)
                    ds_bg[rw, :] = (e_bg[rw, :].astype(F32) * (dpt[rw] - delta)).astype(BF16)
                sink_part = p_sink * delta
                for j in range(4):
                    h = 4 * g + j
                    gs_ref[h:h + 1, :] -= jnp.sum(sink_part[:, 128 * j:128 * j + 128])
                dqs.append(_dot_tn(kw, ds_bg[...]) * (inv * ATTN_SCALE))
                dk_w += _dot_nt(ds_bg[...], (qt * inv).astype(BF16)) * ATTN_SCALE
                dv_w += _dot_nt(e_bg[...], (dot_ * inv).astype(BF16))
            dk_acc[pl.ds(start, WINDOW_KEYS), :] += dk_w
            dv_acc[pl.ds(start, WINDOW_KEYS), :] += dv_w
            attn = _lanes_to_heads(outs[0], outs[1], row)
            dz_ref[rows, :] = (dyb_t * attn * (sig * (1.0 + z * (1.0 - sig)))).astype(BF16)
            dq = _lanes_to_heads(dqs[0], dqs[1], row)
            trows = pl.ds(start, ATTN_BLOCK)
            cs, s1, s2 = t_ref[0, trows, :], t_ref[1, trows, :], t_ref[2, trows, :]
            for blk in range(4):
                cols = slice(128 * blk, 128 * blk + 128)
                dq_ref[rows, cols] = _rope_t(dq[:, cols], cs, s1, s2).astype(BF16)

        @pl.when(i == nb // bps - 1)
        def _():
            dk = dk_acc[ATTN_BLOCK:ATTN_BLOCK + s, :]
            dkv_ref[:, 0:128] = _rope_t(dk, t_ref[0], t_ref[1], t_ref[2]).astype(BF16)
            dkv_ref[:, 128:256] = dv_acc[ATTN_BLOCK:ATTN_BLOCK + s, :].astype(BF16)

    tq = bps * ATTN_BLOCK
    tile = _rows(tq, 512)
    return pl.pallas_call(
        body, name="attn_bwd", grid=(s // tq,),
        out_shape=[jax.ShapeDtypeStruct((s, 512), BF16), jax.ShapeDtypeStruct((s, 512), BF16),
                   jax.ShapeDtypeStruct((s, 256), BF16), jax.ShapeDtypeStruct((8, 128), F32)],
        in_specs=[pl.BlockSpec(memory_space=pltpu.SMEM), _full(TOKEN), tile, tile, tile, _full((s, 256)),
                  _full((3, s, 128))],
        out_specs=[tile, tile, _full((s, 256)), _full((8, 128))],
        scratch_shapes=[pltpu.VMEM((s + 2 * ATTN_BLOCK, 128), BF16)] * 2
        + [pltpu.VMEM((s + 2 * ATTN_BLOCK, 128), F32)] * 2
        + [pltpu.VMEM((3, WINDOW_KEYS, STACKED), F32)]
        + [pltpu.VMEM((2 * bps, WINDOW_KEYS, STACKED), BF16)] * 2,
        compiler_params=_params(),
    )(sink, after, pq, pbz, dyb, pkv, tabs)


def _mem_attn_bwd(pmq, pmz, mkv, dym):
    s = pmq.shape[0]
    m = mkv.shape[0]
    tm = min(512, s)

    def body(q_ref, z_ref, d_ref, mk_ref, mv_ref, dq_ref, dz_ref, dmkv_ref):
        @pl.when(pl.program_id(0) == 0)
        def _():
            dmkv_ref[...] = jnp.zeros_like(dmkv_ref)

        z = z_ref[...].astype(F32)
        sig = _sigmoid(z)
        dym_t = d_ref[...].astype(F32)
        d_attn = dym_t * (z * sig)
        dsilu = sig * (1.0 + z * (1.0 - sig))
        for h in range(MEM_HEADS):
            cols = slice(128 * h, 128 * h + 128)
            q, mk, mv = q_ref[:, cols], mk_ref[:, cols], mv_ref[:, cols]
            pt = _mem_softmax_t(q, mk)
            pb = pt.astype(BF16)
            o = _dot_tn(pb, mv)
            dob = d_attn[:, cols].astype(BF16)
            dpt = _dot_nt(mv, dob)
            dst = (pt * (dpt - jnp.sum(pt * dpt, axis=0, keepdims=True))).astype(BF16)
            dq_ref[:, cols] = (_dot_tn(dst, mk) * MEM_SCALE).astype(BF16)
            dz_ref[:, cols] = (dym_t[:, cols] * o * dsilu[:, cols]).astype(BF16)
            dmkv_ref[:, cols] += _dot(dst, q) * MEM_SCALE
            dmkv_ref[:, 512 + 128 * h:512 + 128 * h + 128] += _dot(pb, dob)

    return pl.pallas_call(
        body, name="mem_attn_bwd", grid=(s // tm,),
        out_shape=[jax.ShapeDtypeStruct((s, 512), BF16), jax.ShapeDtypeStruct((s, 512), BF16),
                   jax.ShapeDtypeStruct((m, D_MODEL), F32)],
        in_specs=[_rows(tm, 512), _rows(tm, 512), _rows(tm, 512), pl.BlockSpec((m, 512), lambda i: (0, 0)),
                  pl.BlockSpec((m, 512), lambda i: (0, 1))],
        out_specs=[_rows(tm, 512), _rows(tm, 512), _full((m, D_MODEL))],
        compiler_params=_params(),
    )(pmq, pmz, dym, mkv, mkv)


def _mem_kv_bwd(mem, g_mem, mn, dmkv, w_mkv):
    m = mem.shape[0]

    def body(mem_ref, g_ref, mn_ref, d_ref, w_ref, gw_ref, gg_ref):
        db = d_ref[...].astype(BF16)
        gw_ref[...] = _dot_tn(mn_ref[...], db).astype(BF16)
        d_mn = _dot_nt(db, w_ref[...])
        xf = mem_ref[...]
        r = lax.rsqrt(jnp.mean(xf * xf, axis=-1, keepdims=True) + EPS)
        gg_ref[...] = jnp.sum(d_mn * (xf * r), axis=0, keepdims=True)

    return pl.pallas_call(
        body, name="mem_kv_bwd", grid=(1,),
        out_shape=[jax.ShapeDtypeStruct((D_MODEL, D_MODEL), BF16), jax.ShapeDtypeStruct((1, D_MODEL), F32)],
        in_specs=[_full((m, D_MODEL)), _full((1, D_MODEL)), _full((m, D_MODEL)), _full((m, D_MODEL)),
                  _full((D_MODEL, D_MODEL))],
        out_specs=[_full((D_MODEL, D_MODEL)), _full((1, D_MODEL))],
        compiler_params=_params(),
    )(mem, g_mem, mn, dmkv, w_mkv)


def _dh_bwd(after, dparts, x, dy, g_pre, w_int):
    s = x.shape[0]
    tm = min(256, s)

    def body(after_ref, *refs):
        d_refs = refs[:7]
        x_ref, dy_ref, g_ref, w_hbm, gx_ref, gg_ref, w_vm, sems = refs[7:]
        _load_once([(w_hbm, w_vm)], sems)

        @pl.when(pl.program_id(0) == 0)
        def _():
            gg_ref[...] = jnp.zeros_like(gg_ref)

        d_h = jnp.zeros((tm, D_MODEL), F32)
        for d_ref, (r0, width) in zip(d_refs, SEGS):
            for c0 in range(0, width, 512):
                cw = min(512, width - c0)
                d_h += _dot(d_ref[:, c0:c0 + cw], w_vm[r0 + c0:r0 + c0 + cw, :])
        xf = x_ref[...]
        r = lax.rsqrt(jnp.mean(xf * xf, axis=-1, keepdims=True) + EPS)
        xn = xf * r
        a = d_h * g_ref[...]
        gx_ref[...] = r * (a - xn * jnp.mean(a * xn, axis=-1, keepdims=True)) + dy_ref[...]
        gg_ref[...] += jnp.sum(d_h * xn, axis=0, keepdims=True)

    return pl.pallas_call(
        body, name="dh_bwd", grid=(s // tm,),
        out_shape=[jax.ShapeDtypeStruct((s, D_MODEL), F32), jax.ShapeDtypeStruct((1, D_MODEL), F32)],
        in_specs=[_full(TOKEN)] + [_rows(tm, w) for _, w in SEGS]
        + [_rows(tm, D_MODEL), _rows(tm, D_MODEL), _full((1, D_MODEL)), ANY],
        out_specs=[_rows(tm, D_MODEL), _full((1, D_MODEL))],
        scratch_shapes=[pltpu.VMEM((IN_WIDTH, D_MODEL), BF16), pltpu.SemaphoreType.DMA((1,))],
        compiler_params=_params(),
    )(after, *dparts, x, dy, g_pre, w_int)


def _gw_in(dparts, h):
    s = h.shape[0]
    tn = 256
    tiles = [(a, c0) for a, (_, width) in enumerate(SEGS) for c0 in range(0, width, tn)]
    n_t = len(tiles)
    assert n_t * tn == IN_WIDTH

    def body(*refs):
        d_hbm = refs[:7]
        h_hbm, out_hbm, lhs, h_vm, res, in_sems, out_sems, h_sem = refs[7:]

        def load(a, c0, slot):
            return pltpu.make_async_copy(d_hbm[a].at[:, pl.ds(c0, tn)], lhs.at[slot], in_sems.at[slot])

        def store(t, slot):
            rows = pl.ds(pl.multiple_of(t * tn, tn), tn)
            return pltpu.make_async_copy(res.at[slot], out_hbm.at[rows], out_sems.at[slot])

        def start_load(t, slot):
            for k, (a, c0) in enumerate(tiles):
                @pl.when(t == k)
                def _(a=a, c0=c0):
                    load(a, c0, slot).start()

        h_copy = pltpu.make_async_copy(h_hbm, h_vm, h_sem)
        h_copy.start()
        load(*tiles[0], 0).start()
        h_copy.wait()

        def step(t, carry):
            slot = t & 1

            @pl.when(t + 1 < n_t)
            def _():
                start_load(t + 1, 1 - slot)

            load(*tiles[0], slot).wait()

            @pl.when(t >= 2)
            def _():
                store(t - 2, slot).wait()

            res[slot] = _dot_tn(lhs[slot], h_vm[...]).astype(BF16)
            store(t, slot).start()
            return carry

        lax.fori_loop(0, n_t, step, 0)
        store(n_t - 2, (n_t - 2) & 1).wait()
        store(n_t - 1, (n_t - 1) & 1).wait()

    return pl.pallas_call(
        body, name="gw_in",
        out_shape=jax.ShapeDtypeStruct((IN_WIDTH, D_MODEL), BF16),
        in_specs=[ANY] * 8, out_specs=ANY,
        scratch_shapes=[pltpu.VMEM((2, s, tn), BF16), pltpu.VMEM((s, D_MODEL), BF16),
                        pltpu.VMEM((2, tn, D_MODEL), BF16), pltpu.SemaphoreType.DMA((2,)),
                        pltpu.SemaphoreType.DMA((2,)), pltpu.SemaphoreType.DMA(())],
        compiler_params=pltpu.CompilerParams(vmem_limit_bytes=CALL_VMEM_MB * 1024 * 1024),
    )(*dparts, h)


def _adamw_math(w, g, m, v):
    m2 = ADAM_B1 * m + (1.0 - ADAM_B1) * g
    v2 = ADAM_B2 * v + (1.0 - ADAM_B2) * (g * g)
    m_hat = m2 / (1.0 - ADAM_B1 ** ADAM_STEP)
    v_hat = v2 / (1.0 - ADAM_B2 ** ADAM_STEP)
    delta = -ADAM_LR * (m_hat / (jnp.sqrt(v_hat) + ADAM_EPS) + ADAM_WD * w)
    return delta, m2, v2


def _sum_adamw(after, own, land, chip, block, w, m, v, name, tiles=1):
    r, c = w.shape
    rt = r // tiles

    def body(c_ref, after_ref, own_ref, l1_ref, l2_ref, l3_ref, w_ref, m_ref, v_ref, g_ref, d_ref, m2_ref, v2_ref):
        g = own_ref[...].astype(F32)
        for l_ref in (l1_ref, l2_ref, l3_ref):
            g += l_ref[...].astype(F32)
        g_ref[...] = g
        d_ref[...], m2_ref[...], v2_ref[...] = _adamw_math(w_ref[...], g, m_ref[...], v_ref[...])

    def share(k):
        return pl.BlockSpec((None, rt, c), lambda i, c_ref: (jnp.bitwise_xor(c_ref[0], k), block * tiles + i, 0))

    spec = pl.BlockSpec((rt, c), lambda i, c_ref: (i, 0))
    grid_spec = pltpu.PrefetchScalarGridSpec(
        num_scalar_prefetch=1, grid=(tiles,),
        in_specs=[ANY, share(0), share(1), share(2), share(3)] + [spec] * 3, out_specs=[spec] * 4)
    return pl.pallas_call(
        body, name=name, grid_spec=grid_spec,
        out_shape=[jax.ShapeDtypeStruct((r, c), F32)] * 4,
        compiler_params=_params(),
    )(chip, after, own, land, land, land, w, m, v)


def _sum_adamw_group(after, items, chip, name):
    k = len(items)

    def body(c_ref, after_ref, *refs):
        shares, wmv, outs = refs[:4 * k], refs[4 * k:7 * k], refs[7 * k:]
        for j in range(k):
            g = shares[4 * j][...].astype(F32)
            for l_ref in shares[4 * j + 1:4 * j + 4]:
                g += l_ref[...].astype(F32)
            outs[4 * j][...] = g
            outs[4 * j + 1][...], outs[4 * j + 2][...], outs[4 * j + 3][...] = _adamw_math(
                wmv[3 * j][...], g, wmv[3 * j + 1][...], wmv[3 * j + 2][...])

    def share(shape, block, q):
        return pl.BlockSpec((None,) + shape, lambda i, c_ref: (jnp.bitwise_xor(c_ref[0], q), block, 0))

    in_specs, args = [ANY], [after]
    for own, land, block, w, m, v in items:
        in_specs += [share(w.shape, block, q) for q in range(4)]
        args += [own, land, land, land]
    for own, land, block, w, m, v in items:
        in_specs += [pl.BlockSpec(w.shape, lambda i, c_ref: (0, 0))] * 3
        args += [w, m, v]
    out_specs = [pl.BlockSpec(w.shape, lambda i, c_ref: (0, 0)) for _, _, _, w, _, _ in items for _ in range(4)]
    res = pl.pallas_call(
        body, name=name,
        grid_spec=pltpu.PrefetchScalarGridSpec(num_scalar_prefetch=1, grid=(1,), in_specs=in_specs,
                                               out_specs=out_specs),
        out_shape=[jax.ShapeDtypeStruct(w.shape, F32) for _, _, _, w, _, _ in items for _ in range(4)],
        compiler_params=_params(),
    )(chip, *args)
    return [res[4 * j:4 * j + 4] for j in range(k)]


def _small_pack(parts):
    def pack_body(gpre_ref, gconv_ref, gsink_ref, gmem_ref, gpost_ref, loss_ref, pack):
        lane = lax.broadcasted_iota(jnp.int32, (1, 128), 1)
        sink_row = jnp.zeros((1, 128), F32)
        for h in range(8):
            sink_row = jnp.where(lane == h, gsink_ref[h:h + 1, :], sink_row)
        pack[...] = jnp.zeros_like(pack)
        pack[0:1, :] = gpre_ref[...]
        pack[1:2, :] = gmem_ref[...]
        pack[2:3, :] = gpost_ref[...]
        pack[3:6, 0:512] = gconv_ref[0:3, :]
        pack[6:7, 0:128] = sink_row
        pack[7:8, 0:128] = loss_ref[0:1, :]

    return pl.pallas_call(
        pack_body, name="small_pack", grid=(1,),
        out_shape=jax.ShapeDtypeStruct((8, D_MODEL), F32),
        in_specs=[_full(p.shape) for p in parts], out_specs=_full((8, D_MODEL)),
    )(*parts)


def _small_apply(packs, ws, ms, vs):
    def apply(p_ref, *refs):
        w_refs, m_refs, v_refs = refs[0:5], refs[5:10], refs[10:15]
        loss_out = refs[15]
        g_outs, d_outs, m_outs, v_outs = refs[16:21], refs[21:26], refs[26:31], refs[31:36]
        x, y, c = _my_place()
        tot = p_ref[0]
        for d in range(1, N_DEV):
            tot = tot + p_ref[d]
        conv = pltpu.roll(tot[:, 0:512], (512 - 64 * (4 * x + 2 * y + c)) % 512, 1)[3:6, 0:64]
        grads = (tot[0:1, :], conv, tot[6:7, 0:8], tot[1:2, :], tot[2:3, :])
        loss_out[...] = tot[7:8, 0:128]
        for j in range(5):
            g_outs[j][...] = grads[j]
            d_outs[j][...], m_outs[j][...], v_outs[j][...] = _adamw_math(
                w_refs[j][...], grads[j], m_refs[j][...], v_refs[j][...])

    specs = [_full(w.shape) for w in ws]
    res = pl.pallas_call(
        apply, name="small_apply", grid=(1,),
        out_shape=[jax.ShapeDtypeStruct((1, 128), F32)] + [jax.ShapeDtypeStruct(w.shape, F32) for w in ws] * 4,
        in_specs=[_full((N_DEV, 8, D_MODEL))] + specs * 3,
        out_specs=[_full((1, 128))] + specs * 4,
    )(packs, *ws, *ms, *vs)
    return res[0], res[1:6], res[6:11], res[11:16], res[16:21]


def kernel(x, mem, g_pre, w_in, w_conv, attn_sink, g_mem, w_mem_kv, w_up_a, w_up_b, w_up_m, w_out, g_post, loss_target, m_g_pre, m_w_in, m_w_conv, m_attn_sink, m_g_mem, m_w_mem_kv, m_w_up_a, m_w_up_b, m_w_up_m, m_w_out, m_g_post, v_g_pre, v_w_in, v_w_conv, v_attn_sink, v_g_mem, v_w_mem_kv, v_w_up_a, v_w_up_b, v_w_up_m, v_w_out, v_g_post):
    s = x.shape[1]
    x2, mem2, tgt2 = x[0], mem[0], loss_target[0]
    me = 4 * lax.axis_index("x") + 2 * lax.axis_index("y") + lax.axis_index("c")

    w_conv_loc = jnp.zeros((8, 128), F32).at[:3, :64].set(w_conv[0])
    w_int_g, w_conv_g, tabs, w_mkv_loc, w_out_loc, w_up_loc = _all_gather(
        [w_in[0].T.astype(BF16), w_conv_loc], "gather_w_in",
        splits=[[(112 * k, 112) for k in range(7)] + [(784, 144)], [(0, 8)]],
        side=_gather_side(s, w_mem_kv[0], w_out[0], (w_up_a[0], w_up_b[0], w_up_m[0])))
    w_int = w_int_g.reshape(IN_WIDTH, D_MODEL)
    w_conv_f = w_conv_g[:, :3, :64].transpose(1, 0, 2).reshape(3, 512)
    late = _gather_start([w_mkv_loc, w_out_loc, w_up_loc], me, "gather_late_start")
    sink = attn_sink[0]

    h, pa, pq, pkv, pbz, pmq, pmz, pg = _proj_fwd(late[4], x2, g_pre, w_int, tabs)
    ya = _conv_fwd(pa, w_conv_f)
    yb = _attn_fwd(pq, pkv, pbz, sink)
    w_mkv_g, w_out_g, w_up_g = _gather_wait(*late[:4], yb, "gather_late_wait")
    w_mkv = w_mkv_g.reshape(D_MODEL, D_MODEL)
    w_out_f = w_out_g.reshape(D_MODEL, D_MODEL)
    mn, mkv = _mem_kv_fwd(mem2, g_mem, w_mkv)
    ym = _mem_attn_fwd(pmq, pmz, mkv)
    dg, dya, dyb, dym, dy, loss_p, gg_post, mb, dob, du = _mid(ya, yb, ym, pg, x2, tgt2, g_post, w_up_g, w_out_f)
    gw_out, gw_up = _gw_mid(mb, dob, (ya, yb, ym), du)

    core = lax.axis_index("c").astype(jnp.int32).reshape(1)
    chip = (2 * lax.axis_index("x") + lax.axis_index("y")).astype(jnp.int32).reshape(1)

    dmq, dmz, dmkv = _mem_attn_bwd(pmq, pmz, mkv, dym)
    gw_mkv, gg_mem = _mem_kv_bwd(mem2, g_mem, mn, dmkv, w_mkv)
    shares1 = [gw_mkv.reshape(N_DEV, 128, D_MODEL), gw_out.reshape(N_DEV, 128, D_MODEL), gw_up]
    sib = _split_start(_sibling_copies, N_CHIPS, shares1,
                       [lax.empty((N_CHIPS,) + a.shape[1:], a.dtype) for a in shares1], "grads_to_sibling_small_start")
    da, gw_conv = _conv_bwd(sib[4], pa, dya, w_conv_f)
    shares1, from_sibling = _split_wait(_sibling_copies, N_CHIPS, *sib[:4], da, "grads_to_sibling_small_wait")
    send1, recv1, srcs1, lands1, token1 = _chip_exchange_start(
        _pair_add(shares1, from_sibling, core, "grads_pair_add_small"), "grads_to_chips_start_small")
    dq, dbz, dkv, g_sink = _attn_bwd(token1, pq, pkv, pbz, dyb, sink, tabs)
    dparts = (da, dq, dkv, dbz, dmq, dmz, dg)
    gw_int = _gw_in(dparts, h)
    send2, recv2, srcs2, lands2, token2 = _chip_exchange_start(
        [_sibling_reduce(gw_int.reshape(N_DEV, SHARD_IN, D_MODEL), "grads_sibling_reduce_w_in")],
        "grads_to_chips_start_w_in")
    grad_x, gg_pre = _dh_bwd(token2, dparts, x2, dy, g_pre, w_int)
    (o_mkv, o_out, o_up, o_int), (l_mkv, l_out, l_up, l_int) = _chip_exchange_wait(
        send1 + send2, recv1 + recv2, srcs1 + srcs2, lands1 + lands2, grad_x, "grads_to_chips_wait")

    small = _gather_start([_small_pack((gg_pre, gw_conv, g_sink, gg_mem, gg_post, loss_p))], me,
                          "small_gather_start")

    w_in_t = _sum_adamw(small[4], o_int, l_int, chip, 0, w_in[0].T, m_w_in[0].T, v_w_in[0].T, "adamw_w_in", tiles=2)
    g_w_in, d_w_in, nm_w_in, nv_w_in = (t.T for t in w_in_t)
    (g_mkv, d_mkv, nm_mkv, nv_mkv), (g_out, d_out, nm_out, nv_out), *up = _sum_adamw_group(
        w_in_t[0],
        [(o_mkv, l_mkv, 0, w_mem_kv[0], m_w_mem_kv[0], v_w_mem_kv[0]),
         (o_out, l_out, 0, w_out[0], m_w_out[0], v_w_out[0]),
         (o_up, l_up, 0, w_up_a[0], m_w_up_a[0], v_w_up_a[0]),
         (o_up, l_up, 1, w_up_b[0], m_w_up_b[0], v_w_up_b[0]),
         (o_up, l_up, 2, w_up_m[0], m_w_up_m[0], v_w_up_m[0])], chip, "adamw_mid_weights")

    (packs,) = _gather_wait(*small[:4], g_out, "small_gather_wait")
    loss_row, small_g, sd, sm, sv = _small_apply(
        packs, [g_pre, w_conv[0], attn_sink, g_mem, g_post],
        [m_g_pre, m_w_conv[0], m_attn_sink, m_g_mem, m_g_post],
        [v_g_pre, v_w_conv[0], v_attn_sink, v_g_mem, v_g_post])
    loss = loss_row[0, 0]
    g_g_pre, g_conv, g_sink_tot, g_g_mem, g_g_post = small_g

    def lead(a):
        return a[None]

    grads = [g_g_pre, lead(g_w_in), lead(g_conv), g_sink_tot, g_g_mem, lead(g_mkv), lead(up[0][0]),
             lead(up[1][0]), lead(up[2][0]), lead(g_out), g_g_post]

    def assemble(small, big_in, big_mkv, big_up, big_out):
        return [small[0], lead(big_in), lead(small[1]), small[2], small[3], lead(big_mkv), lead(big_up[0]),
                lead(big_up[1]), lead(big_up[2]), lead(big_out), small[4]]

    deltas = assemble(sd, d_w_in, d_mkv, [u[1] for u in up], d_out)
    new_m = assemble(sm, nm_w_in, nm_mkv, [u[2] for u in up], nm_out)
    new_v = assemble(sv, nv_w_in, nv_mkv, [u[3] for u in up], nv_out)
    return (loss, grad_x[None], *grads, *deltas, *new_m, *new_v)
```

```python
import functools

import jax
import jax.numpy as jnp
from jax import lax
from jax.experimental import pallas as pl
from jax.experimental.pallas import tpu as pltpu

F32 = jnp.float32
BF16 = jnp.bfloat16
MESH = pl.DeviceIdType.MESH

N_DEV = 8
D_MODEL = 1024
EPS = 1e-6
ROPE_THETA = 500000.0
ROT_DIM = 16
HEAD_DIM = 64
ATTN_BLOCK = 128
MEM_HEADS = 4
MEM_HEAD_DIM = 128
ATTN_SCALE = HEAD_DIM ** -0.5
MEM_SCALE = MEM_HEAD_DIM ** -0.5

ADAM_LR = 0.001
ADAM_B1 = 0.9
ADAM_B2 = 0.999
ADAM_EPS = 1e-08
ADAM_WD = 0.01
ADAM_STEP = 10

SEG_A = (0, 2048)
SEG_BQ = (2048, 512)
SEG_BKV = (2560, 256)
SEG_BZ = (2816, 512)
SEG_MQ = (3328, 512)
SEG_MZ = (3840, 512)
SEG_G = (4352, 3072)
SEGS = (SEG_A, SEG_BQ, SEG_BKV, SEG_BZ, SEG_MQ, SEG_MZ, SEG_G)
IN_WIDTH = 7424
SHARD_IN = IN_WIDTH // N_DEV
W_CHUNKS = tuple((r0 + c0, min(512, width - c0)) for r0, width in SEGS for c0 in range(0, width, 512))
W_CHUNKS_IN_FLIGHT = 3

V7X_VMEM_BYTES = 64 * 1024 * 1024
CALL_VMEM_MB = 57
ANY = pl.BlockSpec(memory_space=pl.ANY)


def _params():
    assert CALL_VMEM_MB * 1024 * 1024 < V7X_VMEM_BYTES
    return pltpu.CompilerParams(dimension_semantics=("arbitrary",), vmem_limit_bytes=CALL_VMEM_MB * 1024 * 1024)


def _full(shape):
    zeros = (0,) * len(shape)
    return pl.BlockSpec(shape, lambda i: zeros)


def _rows(tm, width):
    return pl.BlockSpec((tm, width), lambda i: (i, 0))


def _dot(a, b):
    return jnp.dot(a, b, preferred_element_type=F32)


def _dot_nt(a, b):
    return lax.dot_general(a, b, (((1,), (1,)), ((), ())), preferred_element_type=F32)


def _dot_tn(a, b):
    return lax.dot_general(a, b, (((0,), (0,)), ((), ())), preferred_element_type=F32)


def _sigmoid(z):
    return 1.0 / (1.0 + jnp.exp(-z))


def _rope(t, cs, s1, s2):
    return t * cs + pltpu.roll(t, 120, 1) * s1 + pltpu.roll(t, 8, 1) * s2


def _rope_t(d, cs, s1, s2):
    return d * cs + pltpu.roll(d * s1, 8, 1) + pltpu.roll(d * s2, 120, 1)


def _gather_side(s, w_mkv, w_out, w_ups):
    half = ROT_DIM // 2
    inv_freq = jnp.power(jnp.float32(ROPE_THETA), -jnp.arange(half, dtype=F32) * (2.0 / ROT_DIM))
    freq_row = jnp.tile(jnp.concatenate([inv_freq, inv_freq, jnp.zeros((HEAD_DIM - ROT_DIM,), F32)]), 2)[None, :]

    def fn(in_refs, out_refs):
        f_ref, mkv_ref, out_ref, *up_refs = in_refs
        t_ref, mkv_bf, out_bf, up_bf = out_refs
        mkv_bf[...] = mkv_ref[...].astype(BF16)
        out_bf[...] = out_ref[...].astype(BF16)
        for k, up_ref in enumerate(up_refs):
            up_bf[512 * k:512 * k + 512, :] = up_ref[...].astype(BF16)
        pos = lax.broadcasted_iota(jnp.int32, (s, 128), 0).astype(F32)
        d = lax.broadcasted_iota(jnp.int32, (s, 128), 1) & (HEAD_DIM - 1)
        ang = pos * f_ref[...]
        cos, sin = jnp.cos(ang), jnp.sin(ang)
        lo, hi = d < half, (d >= half) & (d < ROT_DIM)
        t_ref[0] = jnp.where(lo | hi, cos, 1.0)
        t_ref[1] = jnp.where(lo, -sin, 0.0)
        t_ref[2] = jnp.where(hi, sin, 0.0)

    return ([freq_row, w_mkv, w_out, *w_ups],
            [jax.ShapeDtypeStruct((3, s, 128), F32), jax.ShapeDtypeStruct(w_mkv.shape, BF16),
             jax.ShapeDtypeStruct(w_out.shape, BF16), jax.ShapeDtypeStruct((1536, 128), BF16)], fn)


def _load_once(pairs, sems):
    @pl.when(pl.program_id(0) == 0)
    def _():
        cps = [pltpu.make_async_copy(src, dst, sems.at[k]) for k, (src, dst) in enumerate(pairs)]
        for cp in cps:
            cp.start()
        for cp in cps:
            cp.wait()


def _w_chunks_reader(w_hbm, w_vm, sems, loading):
    def copy(k):
        r0, n = W_CHUNKS[k]
        return pltpu.make_async_copy(w_hbm.at[pl.ds(r0, n)], w_vm.at[pl.ds(r0, n)], sems.at[k])

    def begin():
        if loading:
            for k in range(W_CHUNKS_IN_FLIGHT):
                copy(k).start()

    def rows(r0, n):
        if loading:
            k = W_CHUNKS.index((r0, n))
            copy(k).wait()
            if k + W_CHUNKS_IN_FLIGHT < len(W_CHUNKS):
                copy(k + W_CHUNKS_IN_FLIGHT).start()
        return w_vm[r0:r0 + n, :]

    return begin, rows


def _my_place():
    x, y, c = lax.axis_index("x"), lax.axis_index("y"), lax.axis_index("c")
    return x, y, c


def _all_gather(arrs, name, splits=None, side=None):
    n = len(arrs)
    if splits is None:
        splits = [[(0, a.shape[0])] for a in arrs]
    pieces = [(a, r0, rn) for a in range(n) for r0, rn in splits[a]]
    n_p = len(pieces)
    side_in, side_out, side_fn = side if side is not None else ((), (), None)
    m, q = len(side_in), len(side_out)

    def body(*refs):
        ins, outs = refs[:n], refs[n + m:2 * n + m]
        send_sems, recv_sems, local_sems = refs[2 * n + m + q:]
        x, y, c = _my_place()
        me, sibling = (x, y, c), (x, y, 1 - c)

        def route(core):
            first = (jnp.bitwise_xor(x, 1 - core), jnp.bitwise_xor(y, core), core)
            second = (jnp.bitwise_xor(x, core), jnp.bitwise_xor(y, 1 - core), core)
            return first, second, (1 - x, 1 - y, core)

        def idx(px, py, pc):
            return 4 * px + 2 * py + pc

        def copy(p, k, block, to, own=False):
            a, r0, rn = pieces[p]
            dst = outs[a].at[idx(*block), pl.ds(r0, rn)]
            return pltpu.make_async_remote_copy(
                src_ref=ins[a].at[pl.ds(r0, rn)] if own else dst, dst_ref=dst,
                send_sem=send_sems.at[p * 7 + k], recv_sem=recv_sems.at[p * 7 + k],
                device_id=to, device_id_type=MESH)

        nbr1, nbr2, diag = route(c)
        mine = [pltpu.make_async_copy(ins[a], outs[a].at[idx(*me)], local_sems.at[a]) for a in range(n)]
        for cp in mine:
            cp.start()
        sent = []
        for p in range(n_p):
            for k, to in enumerate((sibling, nbr1, nbr2)):
                sent.append(copy(p, k, me, to, own=True))
        for cp in sent:
            cp.start()
        if side_fn is not None:
            side_fn(refs[n:n + m], refs[2 * n + m:2 * n + m + q])
        for k_in, block, onward in ((1, nbr1, ((3, nbr2), (4, sibling))), (2, nbr2, ((5, sibling),)),
                                    (3, diag, ((6, sibling),))):
            for p in range(n_p):
                copy(p, k_in, block, me).wait_recv()
                for k_out, to in onward:
                    cp = copy(p, k_out, block, to)
                    cp.start()
                    sent.append(cp)
        s1, s2, sd = route(1 - c)
        for k_in, block in ((0, sibling), (4, s1), (5, s2), (6, sd)):
            for p in range(n_p):
                copy(p, k_in, block, me).wait_recv()
        for cp in sent:
            cp.wait_send()
        for cp in mine:
            cp.wait()

    return pl.pallas_call(
        body, name=name,
        out_shape=[jax.ShapeDtypeStruct((N_DEV,) + a.shape, a.dtype) for a in arrs] + list(side_out),
        in_specs=[ANY] * n + [pl.BlockSpec(memory_space=pltpu.VMEM)] * m,
        out_specs=[ANY] * n + [pl.BlockSpec(memory_space=pltpu.VMEM)] * q,
        scratch_shapes=[pltpu.SemaphoreType.DMA((7 * n_p,)), pltpu.SemaphoreType.DMA((7 * n_p,)),
                        pltpu.SemaphoreType.DMA((n,))],
        compiler_params=pltpu.CompilerParams(vmem_limit_bytes=32 * 1024 * 1024),
    )(*arrs, *side_in)


N_CHIPS = 4


def _sibling_reduce(arr, name):
    _, r, c = arr.shape

    def body(in_ref, out_ref, land, a_buf, b_buf, o_buf, send_sems, recv_sems, local_sems):
        x, y, core = _my_place()
        cps = [pltpu.make_async_remote_copy(
            src_ref=in_ref.at[2 * j + (1 - core)], dst_ref=land.at[j], send_sem=send_sems.at[j],
            recv_sem=recv_sems.at[j], device_id=(x, y, 1 - core), device_id_type=MESH) for j in range(N_CHIPS)]
        for cp in cps:
            cp.start()
        store = None
        for j in range(N_CHIPS):
            mine = pltpu.make_async_copy(in_ref.at[2 * j + core], a_buf, local_sems.at[0])
            mine.start()
            cps[j].wait_recv()
            theirs = pltpu.make_async_copy(land.at[j], b_buf, local_sems.at[1])
            theirs.start()
            mine.wait()
            theirs.wait()
            if store is not None:
                store.wait()
            o_buf[...] = (a_buf[...].astype(F32) + b_buf[...].astype(F32)).astype(BF16)
            store = pltpu.make_async_copy(o_buf, out_ref.at[j], local_sems.at[2])
            store.start()
        store.wait()
        for cp in cps:
            cp.wait_send()

    return pl.pallas_call(
        body, name=name,
        out_shape=[jax.ShapeDtypeStruct((N_CHIPS, r, c), BF16)] * 2,
        in_specs=[ANY], out_specs=[ANY, ANY],
        scratch_shapes=[pltpu.VMEM((r, c), BF16)] * 3
        + [pltpu.SemaphoreType.DMA((N_CHIPS,)), pltpu.SemaphoreType.DMA((N_CHIPS,)), pltpu.SemaphoreType.DMA((3,))],
        compiler_params=pltpu.CompilerParams(vmem_limit_bytes=32 * 1024 * 1024),
    )(arr)[0]


def _sibling_copies(srcs, lands, send_sems, recv_sems):
    x, y, c = _my_place()
    cps = []
    for j in range(N_CHIPS):
        for a in range(len(srcs)):
            k = a * N_CHIPS + j
            cps.append(pltpu.make_async_remote_copy(
                src_ref=srcs[a].at[2 * j + (1 - c)], dst_ref=lands[a].at[j], send_sem=send_sems[k],
                recv_sem=recv_sems[k], device_id=(x, y, 1 - c), device_id_type=MESH))
    return cps


def _pair_add(mine, recv, core, name):
    n = len(mine)

    def body(c_ref, *refs):
        for a in range(n):
            refs[2 * n + a][...] = (refs[a][...].astype(F32) + refs[n + a][...].astype(F32)).astype(BF16)

    def blk(a):
        return (None,) + a.shape[1:]

    grid_spec = pltpu.PrefetchScalarGridSpec(
        num_scalar_prefetch=1, grid=(N_CHIPS,),
        in_specs=[pl.BlockSpec(blk(a), lambda j, c_ref: (2 * j + c_ref[0], 0, 0)) for a in mine]
        + [pl.BlockSpec(blk(a), lambda j, c_ref: (j, 0, 0)) for a in recv],
        out_specs=[pl.BlockSpec(blk(a), lambda j, c_ref: (j, 0, 0)) for a in recv])
    return pl.pallas_call(
        body, name=name, grid_spec=grid_spec,
        out_shape=[jax.ShapeDtypeStruct(a.shape, BF16) for a in recv],
        compiler_params=_params(),
    )(core, *mine, *recv)


HBM = pl.BlockSpec(memory_space=pltpu.HBM)
SEM = pl.BlockSpec(memory_space=pltpu.SEMAPHORE)
N_PEER_CHIPS = 3
TOKEN = (8, 128)


def _chip_copies(srcs, lands, send_sems, recv_sems):
    x, y, c = _my_place()
    my_chip = 2 * x + y
    peers = [(x, 1 - y), (1 - x, y), (1 - x, 1 - y)]
    cps = []
    for k, (px, py) in enumerate(peers):
        for a in range(len(srcs)):
            j = a * N_PEER_CHIPS + k
            cps.append(pltpu.make_async_remote_copy(
                src_ref=srcs[a].at[2 * px + py], dst_ref=lands[a].at[my_chip],
                send_sem=send_sems[j], recv_sem=recv_sems[j],
                device_id=(px, py, c), device_id_type=MESH))
    return cps


N_PEERS = N_DEV - 1


def _gather_copies(srcs, lands, send_sems, recv_sems):
    x, y, c = _my_place()
    me_idx = 4 * x + 2 * y + c
    flips = [(0, 0, 1), (0, 1, 0), (1, 0, 0), (0, 1, 1), (1, 0, 1), (1, 1, 0), (1, 1, 1)]
    cps = []
    for k, (fx, fy, fc) in enumerate(flips):
        peer = ((1 - x) if fx else x, (1 - y) if fy else y, (1 - c) if fc else c)
        for a in range(len(srcs)):
            j = a * N_PEERS + k
            cps.append(pltpu.make_async_remote_copy(
                src_ref=srcs[a], dst_ref=lands[a].at[me_idx], send_sem=send_sems[j], recv_sem=recv_sems[j],
                device_id=peer, device_id_type=MESH))
    return cps


def _split_start(copies, per_array, arrs, lands, name):
    arrs, lands = list(arrs), list(lands)
    n = len(arrs)
    k = n * per_array

    def body(*refs):
        srcs, land_refs = refs[:n], refs[n:2 * n]
        send_sems, recv_sems = refs[2 * n:2 * n + k], refs[2 * n + k:2 * n + 2 * k]
        token = refs[-1]
        for cp in copies(srcs, land_refs, send_sems, recv_sems):
            cp.start()
        token[...] = jnp.zeros_like(token)

    hbm_arrs = [pltpu.with_memory_space_constraint(a, pltpu.HBM) for a in arrs]
    lands = [pltpu.with_memory_space_constraint(a, pltpu.HBM) for a in lands]
    res = pl.pallas_call(
        body, name=name,
        out_shape=[pltpu.SemaphoreType.DMA(())] * (2 * k) + [pltpu.HBM(a.shape, a.dtype) for a in arrs + lands]
        + [jax.ShapeDtypeStruct(TOKEN, F32)],
        in_specs=[HBM] * (2 * n),
        out_specs=[SEM] * (2 * k) + [HBM] * (2 * n) + [pl.BlockSpec(memory_space=pltpu.VMEM)],
        input_output_aliases={a: 2 * k + a for a in range(2 * n)},
        compiler_params=pltpu.CompilerParams(has_side_effects=pltpu.SideEffectType.DATAFLOW_SIDE_EFFECTING),
    )(*hbm_arrs, *lands)
    return res[:k], res[k:2 * k], res[2 * k:2 * k + n], res[2 * k + n:2 * k + 2 * n], res[-1]


def _split_wait(copies, per_array, send_sems, recv_sems, srcs, lands, after, name):
    n = len(srcs)
    k = n * per_array

    def body(*refs):
        src_refs, land_refs = refs[:n], refs[n:2 * n]
        s_sems, r_sems = refs[2 * n:2 * n + k], refs[2 * n + k:2 * n + 2 * k]
        for cp in copies(src_refs, land_refs, s_sems, r_sems):
            cp.wait_send()
            cp.wait_recv()

    res = pl.pallas_call(
        body, name=name,
        out_shape=[pltpu.HBM(a.shape, a.dtype) for a in list(srcs) + list(lands)],
        in_specs=[HBM] * (2 * n) + [SEM] * (2 * k) + [ANY],
        out_specs=[HBM] * (2 * n),
        input_output_aliases={a: a for a in range(2 * n)},
        compiler_params=pltpu.CompilerParams(has_side_effects=pltpu.SideEffectType.DATAFLOW_SIDE_EFFECTING),
    )(*srcs, *lands, *send_sems, *recv_sems, after)
    return res[:n], res[n:]


def _chip_exchange_start(arrs, name):
    return _split_start(_chip_copies, N_PEER_CHIPS, arrs, [lax.empty(a.shape, a.dtype) for a in arrs], name)


def _chip_exchange_wait(send_sems, recv_sems, srcs, lands, after, name):
    return _split_wait(_chip_copies, N_PEER_CHIPS, send_sems, recv_sems, srcs, lands, after, name)


def _gather_start(arrs, me_idx, name):
    lands = [lax.dynamic_update_slice(lax.empty((N_DEV,) + a.shape, a.dtype), a[None], (me_idx, 0, 0)) for a in arrs]
    return _split_start(_gather_copies, N_PEERS, arrs, lands, name)


def _gather_wait(send_sems, recv_sems, srcs, lands, after, name):
    return _split_wait(_gather_copies, N_PEERS, send_sems, recv_sems, srcs, lands, after, name)[1]


def _proj_fwd(after, x, g_pre, w_int, tabs):
    s = x.shape[0]
    tm = min(512, s)

    def body(after_ref, x_ref, g_ref, t_ref, w_hbm,
             h_ref, pa_ref, pq_ref, pkv_ref, pbz_ref, pmq_ref, pmz_ref, pg_ref, w_vm, sems):
        def project(loading):
            begin, w_rows = _w_chunks_reader(w_hbm, w_vm, sems, loading)
            begin()
            xf = x_ref[...]
            r = lax.rsqrt(jnp.mean(xf * xf, axis=-1, keepdims=True) + EPS)
            h = ((xf * r) * g_ref[...]).astype(BF16)
            h_ref[...] = h
            cs, s1, s2 = t_ref[0], t_ref[1], t_ref[2]

            def mm(seg, c0, width):
                return _dot_nt(h, w_rows(seg[0] + c0, width))

            for c0 in range(0, SEG_A[1], 512):
                pa_ref[:, c0:c0 + 512] = mm(SEG_A, c0, 512).astype(BF16)
            q = mm(SEG_BQ, 0, 512)
            for b in range(4):
                pq_ref[:, 128 * b:128 * b + 128] = _rope(q[:, 128 * b:128 * b + 128], cs, s1, s2).astype(BF16)
            kv = mm(SEG_BKV, 0, 256)
            pkv_ref[:, 0:128] = _rope(kv[:, 0:128], cs, s1, s2).astype(BF16)
            pkv_ref[:, 128:256] = kv[:, 128:256].astype(BF16)
            pbz_ref[...] = mm(SEG_BZ, 0, 512).astype(BF16)
            pmq_ref[...] = mm(SEG_MQ, 0, 512).astype(BF16)
            pmz_ref[...] = mm(SEG_MZ, 0, 512).astype(BF16)
            for c0 in range(0, SEG_G[1], 512):
                pg_ref[:, c0:c0 + 512] = mm(SEG_G, c0, 512).astype(BF16)

        pl.when(pl.program_id(0) == 0)(functools.partial(project, True))
        pl.when(pl.program_id(0) > 0)(functools.partial(project, False))

    widths = (D_MODEL, 2048, 512, 256, 512, 512, 512, 3072)
    return pl.pallas_call(
        body, name="proj_fwd", grid=(s // tm,),
        out_shape=[jax.ShapeDtypeStruct((s, w), BF16) for w in widths],
        in_specs=[_full(TOKEN), _rows(tm, D_MODEL), _full((1, D_MODEL)),
                  pl.BlockSpec((3, tm, 128), lambda i: (0, i, 0)), ANY],
        out_specs=[_rows(tm, w) for w in widths],
        scratch_shapes=[pltpu.VMEM((IN_WIDTH, D_MODEL), BF16), pltpu.SemaphoreType.DMA((len(W_CHUNKS),))],
        compiler_params=_params(),
    )(after, x, g_pre, tabs, w_int)


def _mem_kv_fwd(mem, g_mem, w_mkv):
    m = mem.shape[0]

    def body(mem_ref, g_ref, w_ref, mn_ref, mkv_ref):
        xf = mem_ref[...]
        r = lax.rsqrt(jnp.mean(xf * xf, axis=-1, keepdims=True) + EPS)
        mn = ((xf * r) * g_ref[...]).astype(BF16)
        mn_ref[...] = mn
        mkv_ref[...] = _dot(mn, w_ref[...]).astype(BF16)

    return pl.pallas_call(
        body, name="mem_kv_fwd", grid=(1,),
        out_shape=[jax.ShapeDtypeStruct((m, D_MODEL), BF16)] * 2,
        in_specs=[_full((m, D_MODEL)), _full((1, D_MODEL)), _full((D_MODEL, D_MODEL))],
        out_specs=[_full((m, D_MODEL))] * 2,
        compiler_params=_params(),
    )(mem, g_mem, w_mkv)


def _halo_specs(s, tm, rows, width):
    nblk = s // rows
    prev = pl.BlockSpec((rows, width), lambda i: (jnp.maximum(i * (tm // rows) - 1, 0), 0))
    nxt = pl.BlockSpec((rows, width), lambda i: (jnp.minimum((i + 1) * (tm // rows), nblk - 1), 0))
    return prev, nxt


def _conv_common(pa, cu_prev, cu_next, w, tm):
    b, c, u, z = (pa[:, 512 * k:512 * k + 512] for k in range(4))
    cu = c * u
    row = lax.broadcasted_iota(jnp.int32, (tm, 512), 0)
    cu_m1 = jnp.where(row == 0, cu_prev, pltpu.roll(cu, 1, 0))
    cu_p1 = jnp.where(row == tm - 1, cu_next, pltpu.roll(cu, tm - 1, 0))
    y = cu_m1 * w[0:1] + cu * w[1:2] + cu_p1 * w[2:3]
    sig = _sigmoid(z)
    return b, c, u, z, cu, cu_m1, cu_p1, y, sig, row


def _conv_fwd(pa, w_conv):
    s = pa.shape[0]
    tm = min(512, s)
    nt = s // tm

    def body(pa_ref, pp_ref, pn_ref, w_ref, ya_ref):
        i = pl.program_id(0)
        prev_row = pp_ref[...].astype(F32)[15:16, :]
        next_row = pn_ref[...].astype(F32)[0:1, :]
        b, _, _, z, _, _, _, y, sig, _ = _conv_common(
            pa_ref[...].astype(F32),
            jnp.where(i == 0, 0.0, prev_row[:, 512:1024] * prev_row[:, 1024:1536]),
            jnp.where(i == nt - 1, 0.0, next_row[:, 512:1024] * next_row[:, 1024:1536]), w_ref[...], tm)
        ya_ref[...] = (b * y * (z * sig)).astype(BF16)

    prev, nxt = _halo_specs(s, tm, 16, 2048)
    return pl.pallas_call(
        body, name="conv_fwd", grid=(nt,),
        out_shape=jax.ShapeDtypeStruct((s, 512), BF16),
        in_specs=[_rows(tm, 2048), prev, nxt, _full((3, 512))],
        out_specs=_rows(tm, 512),
        compiler_params=_params(),
    )(pa, pa, pa, w_conv)


def _heads_to_lanes(a, g, row):
    low = row < HEAD_DIM
    parts = []
    for b in (2 * g, 2 * g + 1):
        t = jnp.transpose(a[:, 128 * b:128 * b + 128])
        swapped = pltpu.roll(t, HEAD_DIM, 0)
        if g == 0:
            parts += [jnp.where(low, t, 0.0), jnp.where(low, swapped, 0.0)]
        else:
            parts += [jnp.where(low, 0.0, swapped), jnp.where(low, 0.0, t)]
    return jnp.concatenate(parts, axis=1)


def _lanes_to_heads(t0, t1, row):
    low = row < HEAD_DIM
    blocks = []
    for b in range(4):
        g = b // 2
        tg = (t0, t1)[g]
        je = 2 * (b - 2 * g)
        even, odd = tg[:, 128 * je:128 * je + 128], tg[:, 128 * je + 128:128 * je + 256]
        if g == 0:
            t = jnp.where(low, even, pltpu.roll(odd, HEAD_DIM, 0))
        else:
            t = jnp.where(low, pltpu.roll(even, HEAD_DIM, 0), odd)
        blocks.append(jnp.transpose(t))
    return jnp.concatenate(blocks, axis=1)


WINDOW_KEYS = 3 * ATTN_BLOCK
STACKED = 4 * ATTN_BLOCK
KEY_CHUNK = 32
MAX_BLOCKS_IN_STEP = 8


def _fill_band_bias(bias, nb):
    assert nb >= 2
    c = lax.broadcasted_iota(jnp.int32, (WINDOW_KEYS, STACKED), 0)
    r = lax.broadcasted_iota(jnp.int32, (WINDOW_KEYS, STACKED), 1) & (ATTN_BLOCK - 1)
    band = (c >= r) & (c <= r + 2 * ATTN_BLOCK)
    for v, ok in enumerate((band, band & (c >= ATTN_BLOCK), band & (c < 2 * ATTN_BLOCK))):
        bias[v] = jnp.where(ok, 0.0, -jnp.inf)


def _bias_variant(n, nb):
    return jnp.where(n == 0, 1, jnp.where(n == nb - 1, 2, 0))


def _sink_row(sink_ref, g):
    return jnp.concatenate([jnp.full((1, ATTN_BLOCK), sink_ref[4 * g + j], F32) for j in range(4)], axis=1)


def _softmax_keys_major(sc, bias, variant, sink, e_scr):
    chunks = [pl.ds(k * KEY_CHUNK, KEY_CHUNK) for k in range(WINDOW_KEYS // KEY_CHUNK)]
    rows = [slice(k * KEY_CHUNK, (k + 1) * KEY_CHUNK) for k in range(WINDOW_KEYS // KEY_CHUNK)]
    m_run = jnp.full((KEY_CHUNK, STACKED), -jnp.inf, F32)
    for ck, rw in zip(chunks, rows):
        m_run = jnp.maximum(m_run, sc[rw] + bias[variant, ck, :])
    m = jnp.maximum(jnp.max(m_run, axis=0, keepdims=True), sink)
    l_run = jnp.zeros((KEY_CHUNK, STACKED), F32)
    for ck, rw in zip(chunks, rows):
        e = jnp.exp(sc[rw] + bias[variant, ck, :] - m)
        l_run += e
        e_scr[rw, :] = e.astype(BF16)
    es = jnp.exp(sink - m)
    inv = 1.0 / (jnp.sum(l_run, axis=0, keepdims=True) + es)
    return inv, es * inv


def _fill_padded(kv_ref, kpad, vpad, s):
    zero = jnp.zeros((ATTN_BLOCK, 128), BF16)
    kpad[0:ATTN_BLOCK, :] = zero
    vpad[0:ATTN_BLOCK, :] = zero
    kpad[ATTN_BLOCK + s:2 * ATTN_BLOCK + s, :] = zero
    vpad[ATTN_BLOCK + s:2 * ATTN_BLOCK + s, :] = zero
    kpad[ATTN_BLOCK:ATTN_BLOCK + s, :] = kv_ref[:, 0:128]
    vpad[ATTN_BLOCK:ATTN_BLOCK + s, :] = kv_ref[:, 128:256]


def _attn_fwd(pq, pkv, pbz, sink):
    s = pq.shape[0]
    nb = s // ATTN_BLOCK
    bps = min(MAX_BLOCKS_IN_STEP, nb)

    def body(sink_ref, q_ref, z_ref, kv_ref, yb_ref, kpad, vpad, bias, e_scr):
        i = pl.program_id(0)

        @pl.when(i == 0)
        def _():
            _fill_padded(kv_ref, kpad, vpad, s)
            _fill_band_bias(bias, nb)

        row = lax.broadcasted_iota(jnp.int32, (ATTN_BLOCK, 128), 0)
        for b in range(bps):
            n = i * bps + b
            rows = slice(b * ATTN_BLOCK, (b + 1) * ATTN_BLOCK)
            start = pl.multiple_of(n * ATTN_BLOCK, ATTN_BLOCK)
            kw, vw = kpad[pl.ds(start, WINDOW_KEYS), :], vpad[pl.ds(start, WINDOW_KEYS), :]
            qf = q_ref[rows, :].astype(F32)
            variant = _bias_variant(n, nb)
            outs = []
            for g in range(2):
                e_bg = e_scr.at[2 * b + g]
                qt = (_heads_to_lanes(qf, g, row) * ATTN_SCALE).astype(BF16)
                inv, _ = _softmax_keys_major(_dot(kw, qt), bias, variant, _sink_row(sink_ref, g), e_bg)
                outs.append(_dot_tn(vw, e_bg[...]) * inv)
            attn = _lanes_to_heads(outs[0], outs[1], row)
            z = z_ref[rows, :].astype(F32)
            yb_ref[rows, :] = (attn * (z * _sigmoid(z))).astype(BF16)

    tq = bps * ATTN_BLOCK
    return pl.pallas_call(
        body, name="attn_fwd", grid=(s // tq,),
        out_shape=jax.ShapeDtypeStruct((s, 512), BF16),
        in_specs=[pl.BlockSpec(memory_space=pltpu.SMEM), _rows(tq, 512), _rows(tq, 512), _full((s, 256))],
        out_specs=_rows(tq, 512),
        scratch_shapes=[pltpu.VMEM((s + 2 * ATTN_BLOCK, 128), BF16)] * 2
        + [pltpu.VMEM((3, WINDOW_KEYS, STACKED), F32),
           pltpu.VMEM((2 * bps, WINDOW_KEYS, STACKED), BF16)],
        compiler_params=_params(),
    )(sink, pq, pbz, pkv)


def _mem_softmax_t(q, mk):
    sc = _dot_nt(mk, q) * MEM_SCALE
    e = jnp.exp(sc - jnp.max(sc, axis=0, keepdims=True))
    return e * (1.0 / jnp.sum(e, axis=0, keepdims=True))


def _mem_attn_fwd(pmq, pmz, mkv):
    s = pmq.shape[0]
    m = mkv.shape[0]
    tm = min(512, s)

    def body(q_ref, z_ref, mk_ref, mv_ref, ym_ref):
        z = z_ref[...].astype(F32)
        sz = z * _sigmoid(z)
        for h in range(MEM_HEADS):
            cols = slice(128 * h, 128 * h + 128)
            pt = _mem_softmax_t(q_ref[:, cols], mk_ref[:, cols])
            o = _dot_tn(pt.astype(BF16), mv_ref[:, cols])
            ym_ref[:, cols] = (o * sz[:, cols]).astype(BF16)

    return pl.pallas_call(
        body, name="mem_attn_fwd", grid=(s // tm,),
        out_shape=jax.ShapeDtypeStruct((s, 512), BF16),
        in_specs=[_rows(tm, 512), _rows(tm, 512), pl.BlockSpec((m, 512), lambda i: (0, 0)),
                  pl.BlockSpec((m, 512), lambda i: (0, 1))],
        out_specs=_rows(tm, 512),
        compiler_params=_params(),
    )(pmq, pmz, mkv, mkv)


def _mid(ya, yb, ym, pg, x, target, g_post, w_up, w_out):
    s = x.shape[0]
    tm = min(256, s)
    nt = s // tm

    def body(ya_ref, yb_ref, ym_ref, pg_ref, x_ref, t_ref, gp_ref, wup_hbm, wout_hbm,
             dg_ref, dya_ref, dyb_ref, dym_ref, dy_ref, loss_ref, ggp_ref, mb_ref, dob_ref, du_ref,
             wup_vm, wout_vm, sems):
        i = pl.program_id(0)
        _load_once([(wup_hbm.at[d], wup_vm.at[:, pl.ds(128 * d, 128)]) for d in range(N_DEV)]
                   + [(wout_hbm, wout_vm)], sems)

        @pl.when(i == 0)
        def _():
            loss_ref[...] = jnp.zeros_like(loss_ref)
            ggp_ref[...] = jnp.zeros_like(ggp_ref)

        ys = (ya_ref[...], yb_ref[...], ym_ref[...])
        us = [_dot(ys[k], wup_vm[512 * k:512 * k + 512, :]) for k in range(3)]
        gates = [_sigmoid(pg_ref[:, 1024 * k:1024 * k + 1024].astype(F32)) for k in range(3)]
        merged = gates[0] * us[0] + gates[1] * us[1] + gates[2] * us[2]
        mb = merged.astype(BF16)
        mb_ref[...] = mb
        out = _dot(mb, wout_vm[...])
        r = lax.rsqrt(jnp.mean(out * out, axis=-1, keepdims=True) + EPS)
        on = out * r
        gp = gp_ref[...]
        err = (x_ref[...] + on * gp) - t_ref[...]
        loss_ref[...] += 0.5 * jnp.sum(err * err) * (1.0 / D_MODEL)
        dy = err * (1.0 / D_MODEL)
        dy_ref[...] = dy
        ggp_ref[...] += jnp.sum(dy * on, axis=0, keepdims=True)
        a = dy * gp
        d_out = r * (a - on * jnp.mean(a * on, axis=-1, keepdims=True))
        dob = d_out.astype(BF16)
        dob_ref[...] = dob
        d_merged = _dot_nt(dob, wout_vm[...])
        d_refs = (dya_ref, dyb_ref, dym_ref)
        for k in range(3):
            g = gates[k]
            du_f = d_merged * g
            dg_ref[:, 1024 * k:1024 * k + 1024] = (du_f * us[k] * (1.0 - g)).astype(BF16)
            du = du_f.astype(BF16)
            du_ref[k] = du
            d_refs[k][...] = _dot_nt(du, wup_vm[512 * k:512 * k + 512, :]).astype(BF16)

    return pl.pallas_call(
        body, name="mid", grid=(nt,),
        out_shape=[jax.ShapeDtypeStruct((s, 3072), BF16)] + [jax.ShapeDtypeStruct((s, 512), BF16)] * 3
        + [jax.ShapeDtypeStruct((s, D_MODEL), F32), jax.ShapeDtypeStruct((8, 128), F32),
           jax.ShapeDtypeStruct((1, D_MODEL), F32), jax.ShapeDtypeStruct((s, D_MODEL), BF16),
           jax.ShapeDtypeStruct((s, D_MODEL), BF16), jax.ShapeDtypeStruct((3, s, D_MODEL), BF16)],
        in_specs=[_rows(tm, 512)] * 3 + [_rows(tm, 3072), _rows(tm, D_MODEL), _rows(tm, D_MODEL),
                                         _full((1, D_MODEL)), ANY, ANY],
        out_specs=[_rows(tm, 3072)] + [_rows(tm, 512)] * 3
        + [_rows(tm, D_MODEL), _full((8, 128)), _full((1, D_MODEL)), _rows(tm, D_MODEL), _rows(tm, D_MODEL),
           pl.BlockSpec((3, tm, D_MODEL), lambda i: (0, i, 0))],
        scratch_shapes=[pltpu.VMEM((1536, D_MODEL), BF16), pltpu.VMEM((D_MODEL, D_MODEL), BF16),
                        pltpu.SemaphoreType.DMA((N_DEV + 1,))],
        compiler_params=_params(),
    )(ya, yb, ym, pg, x, target, g_post, w_up, w_out)


def _gw_mid(mb, dob, ys, du):
    s = mb.shape[0]
    tn = 256
    n_out = D_MODEL // tn
    tiles = [(0, c0, 0) for c0 in range(0, D_MODEL, tn)]
    tiles += [(1 + k, c0, 1 + k) for k in range(3) for c0 in range(0, 512, tn)]
    n_t = len(tiles)

    def body(mb_hbm, ya_hbm, yb_hbm, ym_hbm, dob_hbm, du_hbm, out_hbm, up_hbm,
             lhs, rhs, res_out, res_up, in_sems, rhs_sems, out_sems):
        lhs_hbm = (mb_hbm, ya_hbm, yb_hbm, ym_hbm)

        def load(t):
            a, c0, _ = tiles[t]
            return pltpu.make_async_copy(lhs_hbm[a].at[:, pl.ds(c0, tn)], lhs.at[t & 1], in_sems.at[t & 1])

        def load_rhs(g):
            src = dob_hbm if g == 0 else du_hbm.at[g - 1]
            return pltpu.make_async_copy(src, rhs.at[g & 1], rhs_sems.at[g & 1])

        def store(t):
            if t < n_out:
                return pltpu.make_async_copy(res_out.at[t & 1], out_hbm.at[pl.ds(t * tn, tn)], out_sems.at[t & 1])
            rows = pl.ds((t - n_out) * tn, tn)
            return pltpu.make_async_copy(res_up.at[t & 1], up_hbm.at[:, rows, :], out_sems.at[t & 1])

        load_rhs(0).start()
        load(0).start()
        for t, (_, _, g) in enumerate(tiles):
            new_rhs = t == 0 or tiles[t - 1][2] != g
            if t + 1 < n_t:
                load(t + 1).start()
            if new_rhs and g < 3:
                load_rhs(g + 1).start()
            load(t).wait()
            if new_rhs:
                load_rhs(g).wait()
            if t >= 2:
                store(t - 2).wait()
            r = _dot_tn(lhs[t & 1], rhs[g & 1])
            if t < n_out:
                res_out[t & 1] = r.astype(BF16)
            else:
                for d in range(N_DEV):
                    res_up[t & 1, d] = r[:, 128 * d:128 * d + 128].astype(BF16)
            store(t).start()
        store(n_t - 2).wait()
        store(n_t - 1).wait()

    return pl.pallas_call(
        body, name="gw_mid",
        out_shape=[jax.ShapeDtypeStruct((D_MODEL, D_MODEL), BF16), jax.ShapeDtypeStruct((N_DEV, 1536, 128), BF16)],
        in_specs=[ANY] * 6, out_specs=[ANY, ANY],
        scratch_shapes=[pltpu.VMEM((2, s, tn), BF16), pltpu.VMEM((2, s, D_MODEL), BF16),
                        pltpu.VMEM((2, tn, D_MODEL), BF16), pltpu.VMEM((2, N_DEV, tn, 128), BF16),
                        pltpu.SemaphoreType.DMA((2,)), pltpu.SemaphoreType.DMA((2,)),
                        pltpu.SemaphoreType.DMA((2,))],
        compiler_params=pltpu.CompilerParams(vmem_limit_bytes=CALL_VMEM_MB * 1024 * 1024),
    )(mb, *ys, dob, du)


def _conv_bwd(after, pa, dya, w_conv):
    s = pa.shape[0]
    tm = min(512, s)
    nt = s // tm

    def body(after_ref, pa_ref, pp_ref, pn_ref, d_ref, dp_ref, dn_ref, w_ref, da_ref, gw_ref):
        i = pl.program_id(0)
        first, last = i == 0, i == nt - 1

        @pl.when(first)
        def _():
            gw_ref[...] = jnp.zeros_like(gw_ref)

        w = w_ref[...]
        prev_row = pp_ref[...].astype(F32)[15:16, :]
        next_row = pn_ref[...].astype(F32)[0:1, :]
        b, c, u, z, cu, cu_m1, cu_p1, y, sig, row = _conv_common(
            pa_ref[...].astype(F32),
            jnp.where(first, 0.0, prev_row[:, 512:1024] * prev_row[:, 1024:1536]),
            jnp.where(last, 0.0, next_row[:, 512:1024] * next_row[:, 1024:1536]), w, tm)
        sz = z * sig
        dya_t = d_ref[...].astype(F32)
        d_y = dya_t * b * sz

        def halo_dy(p_row, d_row):
            zz = p_row[:, 1536:2048]
            return d_row * p_row[:, 0:512] * (zz * _sigmoid(zz))

        dy_prev = jnp.where(first, 0.0, halo_dy(prev_row, dp_ref[...].astype(F32)[15:16, :]))
        dy_next = jnp.where(last, 0.0, halo_dy(next_row, dn_ref[...].astype(F32)[0:1, :]))
        dy_m1 = jnp.where(row == 0, dy_prev, pltpu.roll(d_y, 1, 0))
        dy_p1 = jnp.where(row == tm - 1, dy_next, pltpu.roll(d_y, tm - 1, 0))
        d_cu = dy_p1 * w[0:1] + d_y * w[1:2] + dy_m1 * w[2:3]
        da_ref[:, 0:512] = (dya_t * y * sz).astype(BF16)
        da_ref[:, 512:1024] = (d_cu * u).astype(BF16)
        da_ref[:, 1024:1536] = (d_cu * c).astype(BF16)
        da_ref[:, 1536:2048] = (dya_t * b * y * (sig + sz * (1.0 - sig))).astype(BF16)
        gw_ref[0:1, :] += jnp.sum(d_y * cu_m1, axis=0, keepdims=True)
        gw_ref[1:2, :] += jnp.sum(d_y * cu, axis=0, keepdims=True)
        gw_ref[2:3, :] += jnp.sum(d_y * cu_p1, axis=0, keepdims=True)

    prev, nxt = _halo_specs(s, tm, 16, 2048)
    dprev, dnxt = _halo_specs(s, tm, 16, 512)
    return pl.pallas_call(
        body, name="conv_bwd", grid=(nt,),
        out_shape=[jax.ShapeDtypeStruct((s, 2048), BF16), jax.ShapeDtypeStruct((8, 512), F32)],
        in_specs=[_full(TOKEN), _rows(tm, 2048), prev, nxt, _rows(tm, 512), dprev, dnxt, _full((3, 512))],
        out_specs=[_rows(tm, 2048), _full((8, 512))],
        compiler_params=_params(),
    )(after, pa, pa, pa, dya, dya, dya, w_conv)


def _attn_bwd(after, pq, pkv, pbz, dyb, sink, tabs):
    s = pq.shape[0]
    nb = s // ATTN_BLOCK
    bps = min(MAX_BLOCKS_IN_STEP, nb)

    def body(sink_ref, after_ref, q_ref, z_ref, d_ref, kv_ref, t_ref,
             dq_ref, dz_ref, dkv_ref, gs_ref, kpad, vpad, dk_acc, dv_acc, bias, e_scr, ds_scr):
        i = pl.program_id(0)

        @pl.when(i == 0)
        def _():
            _fill_padded(kv_ref, kpad, vpad, s)
            _fill_band_bias(bias, nb)
            dk_acc[...] = jnp.zeros_like(dk_acc)
            dv_acc[...] = jnp.zeros_like(dv_acc)
            gs_ref[...] = jnp.zeros_like(gs_ref)

        row = lax.broadcasted_iota(jnp.int32, (ATTN_BLOCK, 128), 0)
        for b in range(bps):
            n = i * bps + b
            rows = slice(b * ATTN_BLOCK, (b + 1) * ATTN_BLOCK)
            start = pl.multiple_of(n * ATTN_BLOCK, ATTN_BLOCK)
            kw, vw = kpad[pl.ds(start, WINDOW_KEYS), :], vpad[pl.ds(start, WINDOW_KEYS), :]
            qf = q_ref[rows, :].astype(F32)
            variant = _bias_variant(n, nb)
            z = z_ref[rows, :].astype(F32)
            sig = _sigmoid(z)
            dyb_t = d_ref[rows, :].astype(F32)
            d_attn = dyb_t * (z * sig)
            outs, dqs = [], []
            dk_w = jnp.zeros((WINDOW_KEYS, 128), F32)
            dv_w = jnp.zeros((WINDOW_KEYS, 128), F32)
            for g in range(2):
                e_bg, ds_bg = e_scr.at[2 * b + g], ds_scr.at[2 * b + g]
                qt = _heads_to_lanes(qf, g, row)
                inv, p_sink = _softmax_keys_major(
                    _dot(kw, (qt * ATTN_SCALE).astype(BF16)), bias, variant, _sink_row(sink_ref, g), e_bg)
                ot = _dot_tn(vw, e_bg[...]) * inv
                outs.append(ot)
                dot_ = _heads_to_lanes(d_attn, g, row)
                delta = jnp.sum(dot_ * ot, axis=0, keepdims=True)
                dpt = _dot(vw, dot_.astype(BF16))
                for k in range(WINDOW_KEYS // KEY_CHUNK):
                    rw = slice(k * KEY_CHUNK, (k + 1) * KEY_CHUNK)
                    ds_bg[rw, :] = (e_bg[rw, :].astype(F32) * (dpt[rw] - delta)).astype(BF16)
                sink_part = p_sink * delta
                for j in range(4):
                    h = 4 * g + j
                    gs_ref[h:h + 1, :] -= jnp.sum(sink_part[:, 128 * j:128 * j + 128])
                dqs.append(_dot_tn(kw, ds_bg[...]) * (inv * ATTN_SCALE))
                dk_w += _dot_nt(ds_bg[...], (qt * inv).astype(BF16)) * ATTN_SCALE
                dv_w += _dot_nt(e_bg[...], (dot_ * inv).astype(BF16))
            dk_acc[pl.ds(start, WINDOW_KEYS), :] += dk_w
            dv_acc[pl.ds(start, WINDOW_KEYS), :] += dv_w
            attn = _lanes_to_heads(outs[0], outs[1], row)
            dz_ref[rows, :] = (dyb_t * attn * (sig * (1.0 + z * (1.0 - sig)))).astype(BF16)
            dq = _lanes_to_heads(dqs[0], dqs[1], row)
            trows = pl.ds(start, ATTN_BLOCK)
            cs, s1, s2 = t_ref[0, trows, :], t_ref[1, trows, :], t_ref[2, trows, :]
            for blk in range(4):
                cols = slice(128 * blk, 128 * blk + 128)
                dq_ref[rows, cols] = _rope_t(dq[:, cols], cs, s1, s2).astype(BF16)

        @pl.when(i == nb // bps - 1)
        def _():
            dk = dk_acc[ATTN_BLOCK:ATTN_BLOCK + s, :]
            dkv_ref[:, 0:128] = _rope_t(dk, t_ref[0], t_ref[1], t_ref[2]).astype(BF16)
            dkv_ref[:, 128:256] = dv_acc[ATTN_BLOCK:ATTN_BLOCK + s, :].astype(BF16)

    tq = bps * ATTN_BLOCK
    tile = _rows(tq, 512)
    return pl.pallas_call(
        body, name="attn_bwd", grid=(s // tq,),
        out_shape=[jax.ShapeDtypeStruct((s, 512), BF16), jax.ShapeDtypeStruct((s, 512), BF16),
                   jax.ShapeDtypeStruct((s, 256), BF16), jax.ShapeDtypeStruct((8, 128), F32)],
        in_specs=[pl.BlockSpec(memory_space=pltpu.SMEM), _full(TOKEN), tile, tile, tile, _full((s, 256)),
                  _full((3, s, 128))],
        out_specs=[tile, tile, _full((s, 256)), _full((8, 128))],
        scratch_shapes=[pltpu.VMEM((s + 2 * ATTN_BLOCK, 128), BF16)] * 2
        + [pltpu.VMEM((s + 2 * ATTN_BLOCK, 128), F32)] * 2
        + [pltpu.VMEM((3, WINDOW_KEYS, STACKED), F32)]
        + [pltpu.VMEM((2 * bps, WINDOW_KEYS, STACKED), BF16)] * 2,
        compiler_params=_params(),
    )(sink, after, pq, pbz, dyb, pkv, tabs)


def _mem_attn_bwd(pmq, pmz, mkv, dym):
    s = pmq.shape[0]
    m = mkv.shape[0]
    tm = min(512, s)

    def body(q_ref, z_ref, d_ref, mk_ref, mv_ref, dq_ref, dz_ref, dmkv_ref):
        @pl.when(pl.program_id(0) == 0)
        def _():
            dmkv_ref[...] = jnp.zeros_like(dmkv_ref)

        z = z_ref[...].astype(F32)
        sig = _sigmoid(z)
        dym_t = d_ref[...].astype(F32)
        d_attn = dym_t * (z * sig)
        dsilu = sig * (1.0 + z * (1.0 - sig))
        for h in range(MEM_HEADS):
            cols = slice(128 * h, 128 * h + 128)
            q, mk, mv = q_ref[:, cols], mk_ref[:, cols], mv_ref[:, cols]
            pt = _mem_softmax_t(q, mk)
            pb = pt.astype(BF16)
            o = _dot_tn(pb, mv)
            dob = d_attn[:, cols].astype(BF16)
            dpt = _dot_nt(mv, dob)
            dst = (pt * (dpt - jnp.sum(pt * dpt, axis=0, keepdims=True))).astype(BF16)
            dq_ref[:, cols] = (_dot_tn(dst, mk) * MEM_SCALE).astype(BF16)
            dz_ref[:, cols] = (dym_t[:, cols] * o * dsilu[:, cols]).astype(BF16)
            dmkv_ref[:, cols] += _dot(dst, q) * MEM_SCALE
            dmkv_ref[:, 512 + 128 * h:512 + 128 * h + 128] += _dot(pb, dob)

    return pl.pallas_call(
        body, name="mem_attn_bwd", grid=(s // tm,),
        out_shape=[jax.ShapeDtypeStruct((s, 512), BF16), jax.ShapeDtypeStruct((s, 512), BF16),
                   jax.ShapeDtypeStruct((m, D_MODEL), F32)],
        in_specs=[_rows(tm, 512), _rows(tm, 512), _rows(tm, 512), pl.BlockSpec((m, 512), lambda i: (0, 0)),
                  pl.BlockSpec((m, 512), lambda i: (0, 1))],
        out_specs=[_rows(tm, 512), _rows(tm, 512), _full((m, D_MODEL))],
        compiler_params=_params(),
    )(pmq, pmz, dym, mkv, mkv)


def _mem_kv_bwd(mem, g_mem, mn, dmkv, w_mkv):
    m = mem.shape[0]

    def body(mem_ref, g_ref, mn_ref, d_ref, w_ref, gw_ref, gg_ref):
        db = d_ref[...].astype(BF16)
        gw_ref[...] = _dot_tn(mn_ref[...], db).astype(BF16)
        d_mn = _dot_nt(db, w_ref[...])
        xf = mem_ref[...]
        r = lax.rsqrt(jnp.mean(xf * xf, axis=-1, keepdims=True) + EPS)
        gg_ref[...] = jnp.sum(d_mn * (xf * r), axis=0, keepdims=True)

    return pl.pallas_call(
        body, name="mem_kv_bwd", grid=(1,),
        out_shape=[jax.ShapeDtypeStruct((D_MODEL, D_MODEL), BF16), jax.ShapeDtypeStruct((1, D_MODEL), F32)],
        in_specs=[_full((m, D_MODEL)), _full((1, D_MODEL)), _full((m, D_MODEL)), _full((m, D_MODEL)),
                  _full((D_MODEL, D_MODEL))],
        out_specs=[_full((D_MODEL, D_MODEL)), _full((1, D_MODEL))],
        compiler_params=_params(),
    )(mem, g_mem, mn, dmkv, w_mkv)


def _dh_bwd(after, dparts, x, dy, g_pre, w_int):
    s = x.shape[0]
    tm = min(256, s)

    def body(after_ref, *refs):
        d_refs = refs[:7]
        x_ref, dy_ref, g_ref, w_hbm, gx_ref, gg_ref, w_vm, sems = refs[7:]
        def backward(loading):
            begin, w_rows = _w_chunks_reader(w_hbm, w_vm, sems, loading)
            begin()
            d_h = jnp.zeros((tm, D_MODEL), F32)
            for d_ref, (r0, width) in zip(d_refs, SEGS):
                for c0 in range(0, width, 512):
                    cw = min(512, width - c0)
                    d_h += _dot(d_ref[:, c0:c0 + cw], w_rows(r0 + c0, cw))
            xf = x_ref[...]
            r = lax.rsqrt(jnp.mean(xf * xf, axis=-1, keepdims=True) + EPS)
            xn = xf * r
            a = d_h * g_ref[...]
            gx_ref[...] = r * (a - xn * jnp.mean(a * xn, axis=-1, keepdims=True)) + dy_ref[...]
            gg = jnp.sum(d_h * xn, axis=0, keepdims=True)
            gg_ref[...] = gg if loading else gg_ref[...] + gg

        pl.when(pl.program_id(0) == 0)(functools.partial(backward, True))
        pl.when(pl.program_id(0) > 0)(functools.partial(backward, False))

    return pl.pallas_call(
        body, name="dh_bwd", grid=(s // tm,),
        out_shape=[jax.ShapeDtypeStruct((s, D_MODEL), F32), jax.ShapeDtypeStruct((1, D_MODEL), F32)],
        in_specs=[_full(TOKEN)] + [_rows(tm, w) for _, w in SEGS]
        + [_rows(tm, D_MODEL), _rows(tm, D_MODEL), _full((1, D_MODEL)), ANY],
        out_specs=[_rows(tm, D_MODEL), _full((1, D_MODEL))],
        scratch_shapes=[pltpu.VMEM((IN_WIDTH, D_MODEL), BF16), pltpu.SemaphoreType.DMA((len(W_CHUNKS),))],
        compiler_params=_params(),
    )(after, *dparts, x, dy, g_pre, w_int)


def _gw_in(dparts, h):
    s = h.shape[0]
    tn = 256
    tiles = [(a, c0) for a, (_, width) in enumerate(SEGS) for c0 in range(0, width, tn)]
    n_t = len(tiles)
    assert n_t * tn == IN_WIDTH

    def body(*refs):
        d_hbm = refs[:7]
        h_hbm, out_hbm, lhs, h_vm, res, in_sems, out_sems, h_sem = refs[7:]

        def load(a, c0, slot):
            return pltpu.make_async_copy(d_hbm[a].at[:, pl.ds(c0, tn)], lhs.at[slot], in_sems.at[slot])

        def store(t, slot):
            rows = pl.ds(pl.multiple_of(t * tn, tn), tn)
            return pltpu.make_async_copy(res.at[slot], out_hbm.at[rows], out_sems.at[slot])

        def start_load(t, slot):
            for k, (a, c0) in enumerate(tiles):
                @pl.when(t == k)
                def _(a=a, c0=c0):
                    load(a, c0, slot).start()

        h_copy = pltpu.make_async_copy(h_hbm, h_vm, h_sem)
        h_copy.start()
        load(*tiles[0], 0).start()
        h_copy.wait()

        def step(t, carry):
            slot = t & 1

            @pl.when(t + 1 < n_t)
            def _():
                start_load(t + 1, 1 - slot)

            load(*tiles[0], slot).wait()

            @pl.when(t >= 2)
            def _():
                store(t - 2, slot).wait()

            res[slot] = _dot_tn(lhs[slot], h_vm[...]).astype(BF16)
            store(t, slot).start()
            return carry

        lax.fori_loop(0, n_t, step, 0)
        store(n_t - 2, (n_t - 2) & 1).wait()
        store(n_t - 1, (n_t - 1) & 1).wait()

    return pl.pallas_call(
        body, name="gw_in",
        out_shape=jax.ShapeDtypeStruct((IN_WIDTH, D_MODEL), BF16),
        in_specs=[ANY] * 8, out_specs=ANY,
        scratch_shapes=[pltpu.VMEM((2, s, tn), BF16), pltpu.VMEM((s, D_MODEL), BF16),
                        pltpu.VMEM((2, tn, D_MODEL), BF16), pltpu.SemaphoreType.DMA((2,)),
                        pltpu.SemaphoreType.DMA((2,)), pltpu.SemaphoreType.DMA(())],
        compiler_params=pltpu.CompilerParams(vmem_limit_bytes=CALL_VMEM_MB * 1024 * 1024),
    )(*dparts, h)


def _adamw_math(w, g, m, v):
    m2 = ADAM_B1 * m + (1.0 - ADAM_B1) * g
    v2 = ADAM_B2 * v + (1.0 - ADAM_B2) * (g * g)
    m_hat = m2 / (1.0 - ADAM_B1 ** ADAM_STEP)
    v_hat = v2 / (1.0 - ADAM_B2 ** ADAM_STEP)
    delta = -ADAM_LR * (m_hat / (jnp.sqrt(v_hat) + ADAM_EPS) + ADAM_WD * w)
    return delta, m2, v2


def _sum_adamw(after, own, land, chip, block, w, m, v, name, tiles=1):
    r, c = w.shape
    rt = r // tiles

    def body(c_ref, after_ref, own_ref, l1_ref, l2_ref, l3_ref, w_ref, m_ref, v_ref, g_ref, d_ref, m2_ref, v2_ref):
        g = own_ref[...].astype(F32)
        for l_ref in (l1_ref, l2_ref, l3_ref):
            g += l_ref[...].astype(F32)
        g_ref[...] = g
        d_ref[...], m2_ref[...], v2_ref[...] = _adamw_math(w_ref[...], g, m_ref[...], v_ref[...])

    def share(k):
        return pl.BlockSpec((None, rt, c), lambda i, c_ref: (jnp.bitwise_xor(c_ref[0], k), block * tiles + i, 0))

    spec = pl.BlockSpec((rt, c), lambda i, c_ref: (i, 0))
    grid_spec = pltpu.PrefetchScalarGridSpec(
        num_scalar_prefetch=1, grid=(tiles,),
        in_specs=[ANY, share(0), share(1), share(2), share(3)] + [spec] * 3, out_specs=[spec] * 4)
    return pl.pallas_call(
        body, name=name, grid_spec=grid_spec,
        out_shape=[jax.ShapeDtypeStruct((r, c), F32)] * 4,
        compiler_params=_params(),
    )(chip, after, own, land, land, land, w, m, v)


def _sum_adamw_group(after, items, chip, name):
    k = len(items)

    def body(c_ref, after_ref, *refs):
        shares, wmv, outs = refs[:4 * k], refs[4 * k:7 * k], refs[7 * k:]
        for j in range(k):
            g = shares[4 * j][...].astype(F32)
            for l_ref in shares[4 * j + 1:4 * j + 4]:
                g += l_ref[...].astype(F32)
            outs[4 * j][...] = g
            outs[4 * j + 1][...], outs[4 * j + 2][...], outs[4 * j + 3][...] = _adamw_math(
                wmv[3 * j][...], g, wmv[3 * j + 1][...], wmv[3 * j + 2][...])

    def share(shape, block, q):
        return pl.BlockSpec((None,) + shape, lambda i, c_ref: (jnp.bitwise_xor(c_ref[0], q), block, 0))

    in_specs, args = [ANY], [after]
    for own, land, block, w, m, v in items:
        in_specs += [share(w.shape, block, q) for q in range(4)]
        args += [own, land, land, land]
    for own, land, block, w, m, v in items:
        in_specs += [pl.BlockSpec(w.shape, lambda i, c_ref: (0, 0))] * 3
        args += [w, m, v]
    out_specs = [pl.BlockSpec(w.shape, lambda i, c_ref: (0, 0)) for _, _, _, w, _, _ in items for _ in range(4)]
    res = pl.pallas_call(
        body, name=name,
        grid_spec=pltpu.PrefetchScalarGridSpec(num_scalar_prefetch=1, grid=(1,), in_specs=in_specs,
                                               out_specs=out_specs),
        out_shape=[jax.ShapeDtypeStruct(w.shape, F32) for _, _, _, w, _, _ in items for _ in range(4)],
        compiler_params=_params(),
    )(chip, *args)
    return [res[4 * j:4 * j + 4] for j in range(k)]


def _small_pack(parts):
    def pack_body(gpre_ref, gconv_ref, gsink_ref, gmem_ref, gpost_ref, loss_ref, pack):
        lane = lax.broadcasted_iota(jnp.int32, (1, 128), 1)
        sink_row = jnp.zeros((1, 128), F32)
        for h in range(8):
            sink_row = jnp.where(lane == h, gsink_ref[h:h + 1, :], sink_row)
        pack[...] = jnp.zeros_like(pack)
        pack[0:1, :] = gpre_ref[...]
        pack[1:2, :] = gmem_ref[...]
        pack[2:3, :] = gpost_ref[...]
        pack[3:6, 0:512] = gconv_ref[0:3, :]
        pack[6:7, 0:128] = sink_row
        pack[7:8, 0:128] = loss_ref[0:1, :]

    return pl.pallas_call(
        pack_body, name="small_pack", grid=(1,),
        out_shape=jax.ShapeDtypeStruct((8, D_MODEL), F32),
        in_specs=[_full(p.shape) for p in parts], out_specs=_full((8, D_MODEL)),
    )(*parts)


def _small_apply(packs, ws, ms, vs):
    def apply(p_ref, *refs):
        w_refs, m_refs, v_refs = refs[0:5], refs[5:10], refs[10:15]
        loss_out = refs[15]
        g_outs, d_outs, m_outs, v_outs = refs[16:21], refs[21:26], refs[26:31], refs[31:36]
        x, y, c = _my_place()
        tot = p_ref[0]
        for d in range(1, N_DEV):
            tot = tot + p_ref[d]
        conv = pltpu.roll(tot[:, 0:512], (512 - 64 * (4 * x + 2 * y + c)) % 512, 1)[3:6, 0:64]
        grads = (tot[0:1, :], conv, tot[6:7, 0:8], tot[1:2, :], tot[2:3, :])
        loss_out[...] = tot[7:8, 0:128]
        for j in range(5):
            g_outs[j][...] = grads[j]
            d_outs[j][...], m_outs[j][...], v_outs[j][...] = _adamw_math(
                w_refs[j][...], grads[j], m_refs[j][...], v_refs[j][...])

    specs = [_full(w.shape) for w in ws]
    res = pl.pallas_call(
        apply, name="small_apply", grid=(1,),
        out_shape=[jax.ShapeDtypeStruct((1, 128), F32)] + [jax.ShapeDtypeStruct(w.shape, F32) for w in ws] * 4,
        in_specs=[_full((N_DEV, 8, D_MODEL))] + specs * 3,
        out_specs=[_full((1, 128))] + specs * 4,
    )(packs, *ws, *ms, *vs)
    return res[0], res[1:6], res[6:11], res[11:16], res[16:21]


def kernel(x, mem, g_pre, w_in, w_conv, attn_sink, g_mem, w_mem_kv, w_up_a, w_up_b, w_up_m, w_out, g_post, loss_target, m_g_pre, m_w_in, m_w_conv, m_attn_sink, m_g_mem, m_w_mem_kv, m_w_up_a, m_w_up_b, m_w_up_m, m_w_out, m_g_post, v_g_pre, v_w_in, v_w_conv, v_attn_sink, v_g_mem, v_w_mem_kv, v_w_up_a, v_w_up_b, v_w_up_m, v_w_out, v_g_post):
    s = x.shape[1]
    x2, mem2, tgt2 = x[0], mem[0], loss_target[0]
    me = 4 * lax.axis_index("x") + 2 * lax.axis_index("y") + lax.axis_index("c")

    w_conv_loc = jnp.zeros((8, 128), F32).at[:3, :64].set(w_conv[0])
    w_int_g, w_conv_g, tabs, w_mkv_loc, w_out_loc, w_up_loc = _all_gather(
        [w_in[0].T.astype(BF16), w_conv_loc], "gather_w_in",
        splits=[[(112 * k, 112) for k in range(7)] + [(784, 144)], [(0, 8)]],
        side=_gather_side(s, w_mem_kv[0], w_out[0], (w_up_a[0], w_up_b[0], w_up_m[0])))
    w_int = w_int_g.reshape(IN_WIDTH, D_MODEL)
    w_conv_f = w_conv_g[:, :3, :64].transpose(1, 0, 2).reshape(3, 512)
    late = _gather_start([w_mkv_loc, w_out_loc, w_up_loc], me, "gather_late_start")
    sink = attn_sink[0]

    h, pa, pq, pkv, pbz, pmq, pmz, pg = _proj_fwd(late[4], x2, g_pre, w_int, tabs)
    ya = _conv_fwd(pa, w_conv_f)
    yb = _attn_fwd(pq, pkv, pbz, sink)
    w_mkv_g, w_out_g, w_up_g = _gather_wait(*late[:4], yb, "gather_late_wait")
    w_mkv = w_mkv_g.reshape(D_MODEL, D_MODEL)
    w_out_f = w_out_g.reshape(D_MODEL, D_MODEL)
    mn, mkv = _mem_kv_fwd(mem2, g_mem, w_mkv)
    ym = _mem_attn_fwd(pmq, pmz, mkv)
    dg, dya, dyb, dym, dy, loss_p, gg_post, mb, dob, du = _mid(ya, yb, ym, pg, x2, tgt2, g_post, w_up_g, w_out_f)
    gw_out, gw_up = _gw_mid(mb, dob, (ya, yb, ym), du)

    core = lax.axis_index("c").astype(jnp.int32).reshape(1)
    chip = (2 * lax.axis_index("x") + lax.axis_index("y")).astype(jnp.int32).reshape(1)

    dmq, dmz, dmkv = _mem_attn_bwd(pmq, pmz, mkv, dym)
    gw_mkv, gg_mem = _mem_kv_bwd(mem2, g_mem, mn, dmkv, w_mkv)
    shares1 = [gw_mkv.reshape(N_DEV, 128, D_MODEL), gw_out.reshape(N_DEV, 128, D_MODEL), gw_up]
    sib = _split_start(_sibling_copies, N_CHIPS, shares1,
                       [lax.empty((N_CHIPS,) + a.shape[1:], a.dtype) for a in shares1], "grads_to_sibling_small_start")
    da, gw_conv = _conv_bwd(sib[4], pa, dya, w_conv_f)
    shares1, from_sibling = _split_wait(_sibling_copies, N_CHIPS, *sib[:4], da, "grads_to_sibling_small_wait")
    send1, recv1, srcs1, lands1, token1 = _chip_exchange_start(
        _pair_add(shares1, from_sibling, core, "grads_pair_add_small"), "grads_to_chips_start_small")
    dq, dbz, dkv, g_sink = _attn_bwd(token1, pq, pkv, pbz, dyb, sink, tabs)
    dparts = (da, dq, dkv, dbz, dmq, dmz, dg)
    gw_int = _gw_in(dparts, h)
    send2, recv2, srcs2, lands2, token2 = _chip_exchange_start(
        [_sibling_reduce(gw_int.reshape(N_DEV, SHARD_IN, D_MODEL), "grads_sibling_reduce_w_in")],
        "grads_to_chips_start_w_in")
    grad_x, gg_pre = _dh_bwd(token2, dparts, x2, dy, g_pre, w_int)
    (o_mkv, o_out, o_up, o_int), (l_mkv, l_out, l_up, l_int) = _chip_exchange_wait(
        send1 + send2, recv1 + recv2, srcs1 + srcs2, lands1 + lands2, grad_x, "grads_to_chips_wait")

    small = _gather_start([_small_pack((gg_pre, gw_conv, g_sink, gg_mem, gg_post, loss_p))], me,
                          "small_gather_start")

    w_in_t = _sum_adamw(small[4], o_int, l_int, chip, 0, w_in[0].T, m_w_in[0].T, v_w_in[0].T, "adamw_w_in", tiles=2)
    g_w_in, d_w_in, nm_w_in, nv_w_in = (t.T for t in w_in_t)
    (g_mkv, d_mkv, nm_mkv, nv_mkv), (g_out, d_out, nm_out, nv_out), *up = _sum_adamw_group(
        w_in_t[0],
        [(o_mkv, l_mkv, 0, w_mem_kv[0], m_w_mem_kv[0], v_w_mem_kv[0]),
         (o_out, l_out, 0, w_out[0], m_w_out[0], v_w_out[0]),
         (o_up, l_up, 0, w_up_a[0], m_w_up_a[0], v_w_up_a[0]),
         (o_up, l_up, 1, w_up_b[0], m_w_up_b[0], v_w_up_b[0]),
         (o_up, l_up, 2, w_up_m[0], m_w_up_m[0], v_w_up_m[0])], chip, "adamw_mid_weights")

    (packs,) = _gather_wait(*small[:4], g_out, "small_gather_wait")
    loss_row, small_g, sd, sm, sv = _small_apply(
        packs, [g_pre, w_conv[0], attn_sink, g_mem, g_post],
        [m_g_pre, m_w_conv[0], m_attn_sink, m_g_mem, m_g_post],
        [v_g_pre, v_w_conv[0], v_attn_sink, v_g_mem, v_g_post])
    loss = loss_row[0, 0]
    g_g_pre, g_conv, g_sink_tot, g_g_mem, g_g_post = small_g

    def lead(a):
        return a[None]

    grads = [g_g_pre, lead(g_w_in), lead(g_conv), g_sink_tot, g_g_mem, lead(g_mkv), lead(up[0][0]),
             lead(up[1][0]), lead(up[2][0]), lead(g_out), g_g_post]

    def assemble(small, big_in, big_mkv, big_up, big_out):
        return [small[0], lead(big_in), lead(small[1]), small[2], small[3], lead(big_mkv), lead(big_up[0]),
                lead(big_up[1]), lead(big_up[2]), lead(big_out), small[4]]

    deltas = assemble(sd, d_w_in, d_mkv, [u[1] for u in up], d_out)
    new_m = assemble(sm, nm_w_in, nm_mkv, [u[2] for u in up], nm_out)
    new_v = assemble(sv, nv_w_in, nv_mkv, [u[3] for u in up], nv_out)
    return (loss, grad_x[None], *grads, *deltas, *new_m, *new_v)
```

```python
import functools

import jax
import jax.numpy as jnp
from jax import lax
from jax.experimental import pallas as pl
from jax.experimental.pallas import tpu as pltpu

F32 = jnp.float32
BF16 = jnp.bfloat16
MESH = pl.DeviceIdType.MESH

N_DEV = 8
D_MODEL = 1024
EPS = 1e-6
ROPE_THETA = 500000.0
ROT_DIM = 16
HEAD_DIM = 64
ATTN_BLOCK = 128
MEM_HEADS = 4
MEM_HEAD_DIM = 128
ATTN_SCALE = HEAD_DIM ** -0.5
MEM_SCALE = MEM_HEAD_DIM ** -0.5

ADAM_LR = 0.001
ADAM_B1 = 0.9
ADAM_B2 = 0.999
ADAM_EPS = 1e-08
ADAM_WD = 0.01
ADAM_STEP = 10

SEG_A = (0, 2048)
SEG_BQ = (2048, 512)
SEG_BKV = (2560, 256)
SEG_BZ = (2816, 512)
SEG_MQ = (3328, 512)
SEG_MZ = (3840, 512)
SEG_G = (4352, 3072)
SEGS = (SEG_A, SEG_BQ, SEG_BKV, SEG_BZ, SEG_MQ, SEG_MZ, SEG_G)
IN_WIDTH = 7424
SHARD_IN = IN_WIDTH // N_DEV

V7X_VMEM_BYTES = 64 * 1024 * 1024
CALL_VMEM_MB = 57
ANY = pl.BlockSpec(memory_space=pl.ANY)


def _params():
    assert CALL_VMEM_MB * 1024 * 1024 < V7X_VMEM_BYTES
    return pltpu.CompilerParams(dimension_semantics=("arbitrary",), vmem_limit_bytes=CALL_VMEM_MB * 1024 * 1024)


def _full(shape):
    zeros = (0,) * len(shape)
    return pl.BlockSpec(shape, lambda i: zeros)


def _rows(tm, width):
    return pl.BlockSpec((tm, width), lambda i: (i, 0))


def _dot(a, b):
    return jnp.dot(a, b, preferred_element_type=F32)


def _dot_nt(a, b):
    return lax.dot_general(a, b, (((1,), (1,)), ((), ())), preferred_element_type=F32)


def _dot_tn(a, b):
    return lax.dot_general(a, b, (((0,), (0,)), ((), ())), preferred_element_type=F32)


def _sigmoid(z):
    return 1.0 / (1.0 + jnp.exp(-z))


def _rope(t, cs, s1, s2):
    return t * cs + pltpu.roll(t, 120, 1) * s1 + pltpu.roll(t, 8, 1) * s2


def _rope_t(d, cs, s1, s2):
    return d * cs + pltpu.roll(d * s1, 8, 1) + pltpu.roll(d * s2, 120, 1)


def _gather_side(s, w_mkv, w_out, w_ups):
    half = ROT_DIM // 2
    inv_freq = jnp.power(jnp.float32(ROPE_THETA), -jnp.arange(half, dtype=F32) * (2.0 / ROT_DIM))
    freq_row = jnp.tile(jnp.concatenate([inv_freq, inv_freq, jnp.zeros((HEAD_DIM - ROT_DIM,), F32)]), 2)[None, :]

    def fn(in_refs, out_refs):
        f_ref, mkv_ref, out_ref, *up_refs = in_refs
        t_ref, mkv_bf, out_bf, up_bf = out_refs
        mkv_bf[...] = mkv_ref[...].astype(BF16)
        out_bf[...] = out_ref[...].astype(BF16)
        for k, up_ref in enumerate(up_refs):
            up_bf[512 * k:512 * k + 512, :] = up_ref[...].astype(BF16)
        pos = lax.broadcasted_iota(jnp.int32, (s, 128), 0).astype(F32)
        d = lax.broadcasted_iota(jnp.int32, (s, 128), 1) & (HEAD_DIM - 1)
        ang = pos * f_ref[...]
        cos, sin = jnp.cos(ang), jnp.sin(ang)
        lo, hi = d < half, (d >= half) & (d < ROT_DIM)
        t_ref[0] = jnp.where(lo | hi, cos, 1.0)
        t_ref[1] = jnp.where(lo, -sin, 0.0)
        t_ref[2] = jnp.where(hi, sin, 0.0)

    return ([freq_row, w_mkv, w_out, *w_ups],
            [jax.ShapeDtypeStruct((3, s, 128), F32), jax.ShapeDtypeStruct(w_mkv.shape, BF16),
             jax.ShapeDtypeStruct(w_out.shape, BF16), jax.ShapeDtypeStruct((1536, 128), BF16)], fn)


def _load_once(pairs, sems):
    @pl.when(pl.program_id(0) == 0)
    def _():
        cps = [pltpu.make_async_copy(src, dst, sems.at[k]) for k, (src, dst) in enumerate(pairs)]
        for cp in cps:
            cp.start()
        for cp in cps:
            cp.wait()


def _my_place():
    x, y, c = lax.axis_index("x"), lax.axis_index("y"), lax.axis_index("c")
    return x, y, c


def _all_gather(arrs, name, splits=None, side=None):
    n = len(arrs)
    if splits is None:
        splits = [[(0, a.shape[0])] for a in arrs]
    pieces = [(a, r0, rn) for a in range(n) for r0, rn in splits[a]]
    n_p = len(pieces)
    side_in, side_out, side_fn = side if side is not None else ((), (), None)
    m, q = len(side_in), len(side_out)

    def body(*refs):
        ins, outs = refs[:n], refs[n + m:2 * n + m]
        send_sems, recv_sems, local_sems = refs[2 * n + m + q:]
        x, y, c = _my_place()
        me, sibling = (x, y, c), (x, y, 1 - c)

        def route(core):
            first = (jnp.bitwise_xor(x, 1 - core), jnp.bitwise_xor(y, core), core)
            second = (jnp.bitwise_xor(x, core), jnp.bitwise_xor(y, 1 - core), core)
            return first, second, (1 - x, 1 - y, core)

        def idx(px, py, pc):
            return 4 * px + 2 * py + pc

        def copy(p, k, block, to, own=False):
            a, r0, rn = pieces[p]
            dst = outs[a].at[idx(*block), pl.ds(r0, rn)]
            return pltpu.make_async_remote_copy(
                src_ref=ins[a].at[pl.ds(r0, rn)] if own else dst, dst_ref=dst,
                send_sem=send_sems.at[p * 7 + k], recv_sem=recv_sems.at[p * 7 + k],
                device_id=to, device_id_type=MESH)

        nbr1, nbr2, diag = route(c)
        mine = [pltpu.make_async_copy(ins[a], outs[a].at[idx(*me)], local_sems.at[a]) for a in range(n)]
        for cp in mine:
            cp.start()
        sent = []
        for p in range(n_p):
            for k, to in enumerate((sibling, nbr1, nbr2)):
                sent.append(copy(p, k, me, to, own=True))
        for cp in sent:
            cp.start()
        if side_fn is not None:
            side_fn(refs[n:n + m], refs[2 * n + m:2 * n + m + q])
        for k_in, block, onward in ((1, nbr1, ((3, nbr2), (4, sibling))), (2, nbr2, ((5, sibling),)),
                                    (3, diag, ((6, sibling),))):
            for p in range(n_p):
                copy(p, k_in, block, me).wait_recv()
                for k_out, to in onward:
                    cp = copy(p, k_out, block, to)
                    cp.start()
                    sent.append(cp)
        s1, s2, sd = route(1 - c)
        for k_in, block in ((0, sibling), (4, s1), (5, s2), (6, sd)):
            for p in range(n_p):
                copy(p, k_in, block, me).wait_recv()
        for cp in sent:
            cp.wait_send()
        for cp in mine:
            cp.wait()

    return pl.pallas_call(
        body, name=name,
        out_shape=[jax.ShapeDtypeStruct((N_DEV,) + a.shape, a.dtype) for a in arrs] + list(side_out),
        in_specs=[ANY] * n + [pl.BlockSpec(memory_space=pltpu.VMEM)] * m,
        out_specs=[ANY] * n + [pl.BlockSpec(memory_space=pltpu.VMEM)] * q,
        scratch_shapes=[pltpu.SemaphoreType.DMA((7 * n_p,)), pltpu.SemaphoreType.DMA((7 * n_p,)),
                        pltpu.SemaphoreType.DMA((n,))],
        compiler_params=pltpu.CompilerParams(vmem_limit_bytes=32 * 1024 * 1024),
    )(*arrs, *side_in)


N_CHIPS = 4


def _sibling_reduce(arr, name):
    _, r, c = arr.shape

    def body(in_ref, out_ref, land, a_buf, b_buf, o_buf, send_sems, recv_sems, local_sems):
        x, y, core = _my_place()
        cps = [pltpu.make_async_remote_copy(
            src_ref=in_ref.at[2 * j + (1 - core)], dst_ref=land.at[j], send_sem=send_sems.at[j],
            recv_sem=recv_sems.at[j], device_id=(x, y, 1 - core), device_id_type=MESH) for j in range(N_CHIPS)]
        for cp in cps:
            cp.start()
        store = None
        for j in range(N_CHIPS):
            mine = pltpu.make_async_copy(in_ref.at[2 * j + core], a_buf, local_sems.at[0])
            mine.start()
            cps[j].wait_recv()
            theirs = pltpu.make_async_copy(land.at[j], b_buf, local_sems.at[1])
            theirs.start()
            mine.wait()
            theirs.wait()
            if store is not None:
                store.wait()
            o_buf[...] = (a_buf[...].astype(F32) + b_buf[...].astype(F32)).astype(BF16)
            store = pltpu.make_async_copy(o_buf, out_ref.at[j], local_sems.at[2])
            store.start()
        store.wait()
        for cp in cps:
            cp.wait_send()

    return pl.pallas_call(
        body, name=name,
        out_shape=[jax.ShapeDtypeStruct((N_CHIPS, r, c), BF16)] * 2,
        in_specs=[ANY], out_specs=[ANY, ANY],
        scratch_shapes=[pltpu.VMEM((r, c), BF16)] * 3
        + [pltpu.SemaphoreType.DMA((N_CHIPS,)), pltpu.SemaphoreType.DMA((N_CHIPS,)), pltpu.SemaphoreType.DMA((3,))],
        compiler_params=pltpu.CompilerParams(vmem_limit_bytes=32 * 1024 * 1024),
    )(arr)[0]


def _sibling_copies(srcs, lands, send_sems, recv_sems):
    x, y, c = _my_place()
    cps = []
    for j in range(N_CHIPS):
        for a in range(len(srcs)):
            k = a * N_CHIPS + j
            cps.append(pltpu.make_async_remote_copy(
                src_ref=srcs[a].at[2 * j + (1 - c)], dst_ref=lands[a].at[j], send_sem=send_sems[k],
                recv_sem=recv_sems[k], device_id=(x, y, 1 - c), device_id_type=MESH))
    return cps


def _pair_add(mine, recv, core, name):
    n = len(mine)

    def body(c_ref, *refs):
        for a in range(n):
            refs[2 * n + a][...] = (refs[a][...].astype(F32) + refs[n + a][...].astype(F32)).astype(BF16)

    def blk(a):
        return (None,) + a.shape[1:]

    grid_spec = pltpu.PrefetchScalarGridSpec(
        num_scalar_prefetch=1, grid=(N_CHIPS,),
        in_specs=[pl.BlockSpec(blk(a), lambda j, c_ref: (2 * j + c_ref[0], 0, 0)) for a in mine]
        + [pl.BlockSpec(blk(a), lambda j, c_ref: (j, 0, 0)) for a in recv],
        out_specs=[pl.BlockSpec(blk(a), lambda j, c_ref: (j, 0, 0)) for a in recv])
    return pl.pallas_call(
        body, name=name, grid_spec=grid_spec,
        out_shape=[jax.ShapeDtypeStruct(a.shape, BF16) for a in recv],
        compiler_params=_params(),
    )(core, *mine, *recv)


HBM = pl.BlockSpec(memory_space=pltpu.HBM)
SEM = pl.BlockSpec(memory_space=pltpu.SEMAPHORE)
N_PEER_CHIPS = 3
TOKEN = (8, 128)


def _chip_copies(srcs, lands, send_sems, recv_sems):
    x, y, c = _my_place()
    my_chip = 2 * x + y
    peers = [(x, 1 - y), (1 - x, y), (1 - x, 1 - y)]
    cps = []
    for k, (px, py) in enumerate(peers):
        for a in range(len(srcs)):
            j = a * N_PEER_CHIPS + k
            cps.append(pltpu.make_async_remote_copy(
                src_ref=srcs[a].at[2 * px + py], dst_ref=lands[a].at[my_chip],
                send_sem=send_sems[j], recv_sem=recv_sems[j],
                device_id=(px, py, c), device_id_type=MESH))
    return cps


N_PEERS = N_DEV - 1


def _gather_copies(srcs, lands, send_sems, recv_sems):
    x, y, c = _my_place()
    me_idx = 4 * x + 2 * y + c
    flips = [(0, 0, 1), (0, 1, 0), (1, 0, 0), (0, 1, 1), (1, 0, 1), (1, 1, 0), (1, 1, 1)]
    cps = []
    for k, (fx, fy, fc) in enumerate(flips):
        peer = ((1 - x) if fx else x, (1 - y) if fy else y, (1 - c) if fc else c)
        for a in range(len(srcs)):
            j = a * N_PEERS + k
            cps.append(pltpu.make_async_remote_copy(
                src_ref=srcs[a], dst_ref=lands[a].at[me_idx], send_sem=send_sems[j], recv_sem=recv_sems[j],
                device_id=peer, device_id_type=MESH))
    return cps


def _split_start(copies, per_array, arrs, lands, name):
    arrs, lands = list(arrs), list(lands)
    n = len(arrs)
    k = n * per_array

    def body(*refs):
        srcs, land_refs = refs[:n], refs[n:2 * n]
        send_sems, recv_sems = refs[2 * n:2 * n + k], refs[2 * n + k:2 * n + 2 * k]
        token = refs[-1]
        for cp in copies(srcs, land_refs, send_sems, recv_sems):
            cp.start()
        token[...] = jnp.zeros_like(token)

    hbm_arrs = [pltpu.with_memory_space_constraint(a, pltpu.HBM) for a in arrs]
    lands = [pltpu.with_memory_space_constraint(a, pltpu.HBM) for a in lands]
    res = pl.pallas_call(
        body, name=name,
        out_shape=[pltpu.SemaphoreType.DMA(())] * (2 * k) + [pltpu.HBM(a.shape, a.dtype) for a in arrs + lands]
        + [jax.ShapeDtypeStruct(TOKEN, F32)],
        in_specs=[HBM] * (2 * n),
        out_specs=[SEM] * (2 * k) + [HBM] * (2 * n) + [pl.BlockSpec(memory_space=pltpu.VMEM)],
        input_output_aliases={a: 2 * k + a for a in range(2 * n)},
        compiler_params=pltpu.CompilerParams(has_side_effects=pltpu.SideEffectType.DATAFLOW_SIDE_EFFECTING),
    )(*hbm_arrs, *lands)
    return res[:k], res[k:2 * k], res[2 * k:2 * k + n], res[2 * k + n:2 * k + 2 * n], res[-1]


def _split_wait(copies, per_array, send_sems, recv_sems, srcs, lands, after, name):
    n = len(srcs)
    k = n * per_array

    def body(*refs):
        src_refs, land_refs = refs[:n], refs[n:2 * n]
        s_sems, r_sems = refs[2 * n:2 * n + k], refs[2 * n + k:2 * n + 2 * k]
        for cp in copies(src_refs, land_refs, s_sems, r_sems):
            cp.wait_send()
            cp.wait_recv()

    res = pl.pallas_call(
        body, name=name,
        out_shape=[pltpu.HBM(a.shape, a.dtype) for a in list(srcs) + list(lands)],
        in_specs=[HBM] * (2 * n) + [SEM] * (2 * k) + [ANY],
        out_specs=[HBM] * (2 * n),
        input_output_aliases={a: a for a in range(2 * n)},
        compiler_params=pltpu.CompilerParams(has_side_effects=pltpu.SideEffectType.DATAFLOW_SIDE_EFFECTING),
    )(*srcs, *lands, *send_sems, *recv_sems, after)
    return res[:n], res[n:]


def _chip_exchange_start(arrs, name):
    return _split_start(_chip_copies, N_PEER_CHIPS, arrs, [lax.empty(a.shape, a.dtype) for a in arrs], name)


def _chip_exchange_wait(send_sems, recv_sems, srcs, lands, after, name):
    return _split_wait(_chip_copies, N_PEER_CHIPS, send_sems, recv_sems, srcs, lands, after, name)


def _gather_start(arrs, me_idx, name):
    lands = [lax.dynamic_update_slice(lax.empty((N_DEV,) + a.shape, a.dtype), a[None], (me_idx, 0, 0)) for a in arrs]
    return _split_start(_gather_copies, N_PEERS, arrs, lands, name)


def _gather_wait(send_sems, recv_sems, srcs, lands, after, name):
    return _split_wait(_gather_copies, N_PEERS, send_sems, recv_sems, srcs, lands, after, name)[1]


def _proj_fwd(after, x, g_pre, w_int, tabs):
    s = x.shape[0]
    tm = min(512, s)

    def body(after_ref, x_ref, g_ref, t_ref, w_hbm,
             h_ref, pa_ref, pq_ref, pkv_ref, pbz_ref, pmq_ref, pmz_ref, pg_ref, w_vm, sems):
        _load_once([(w_hbm, w_vm)], sems)
        xf = x_ref[...]
        r = lax.rsqrt(jnp.mean(xf * xf, axis=-1, keepdims=True) + EPS)
        h = ((xf * r) * g_ref[...]).astype(BF16)
        h_ref[...] = h
        cs, s1, s2 = t_ref[0], t_ref[1], t_ref[2]

        def mm(seg, c0, width):
            return _dot_nt(h, w_vm[seg[0] + c0:seg[0] + c0 + width, :])

        for c0 in range(0, SEG_A[1], 512):
            pa_ref[:, c0:c0 + 512] = mm(SEG_A, c0, 512).astype(BF16)
        q = mm(SEG_BQ, 0, 512)
        for b in range(4):
            pq_ref[:, 128 * b:128 * b + 128] = _rope(q[:, 128 * b:128 * b + 128], cs, s1, s2).astype(BF16)
        kv = mm(SEG_BKV, 0, 256)
        pkv_ref[:, 0:128] = _rope(kv[:, 0:128], cs, s1, s2).astype(BF16)
        pkv_ref[:, 128:256] = kv[:, 128:256].astype(BF16)
        pbz_ref[...] = mm(SEG_BZ, 0, 512).astype(BF16)
        pmq_ref[...] = mm(SEG_MQ, 0, 512).astype(BF16)
        pmz_ref[...] = mm(SEG_MZ, 0, 512).astype(BF16)
        for c0 in range(0, SEG_G[1], 512):
            pg_ref[:, c0:c0 + 512] = mm(SEG_G, c0, 512).astype(BF16)

    widths = (D_MODEL, 2048, 512, 256, 512, 512, 512, 3072)
    return pl.pallas_call(
        body, name="proj_fwd", grid=(s // tm,),
        out_shape=[jax.ShapeDtypeStruct((s, w), BF16) for w in widths],
        in_specs=[_full(TOKEN), _rows(tm, D_MODEL), _full((1, D_MODEL)),
                  pl.BlockSpec((3, tm, 128), lambda i: (0, i, 0)), ANY],
        out_specs=[_rows(tm, w) for w in widths],
        scratch_shapes=[pltpu.VMEM((IN_WIDTH, D_MODEL), BF16), pltpu.SemaphoreType.DMA((1,))],
        compiler_params=_params(),
    )(after, x, g_pre, tabs, w_int)


def _mem_kv_fwd(mem, g_mem, w_mkv):
    m = mem.shape[0]

    def body(mem_ref, g_ref, w_ref, mn_ref, mkv_ref):
        xf = mem_ref[...]
        r = lax.rsqrt(jnp.mean(xf * xf, axis=-1, keepdims=True) + EPS)
        mn = ((xf * r) * g_ref[...]).astype(BF16)
        mn_ref[...] = mn
        mkv_ref[...] = _dot(mn, w_ref[...]).astype(BF16)

    return pl.pallas_call(
        body, name="mem_kv_fwd", grid=(1,),
        out_shape=[jax.ShapeDtypeStruct((m, D_MODEL), BF16)] * 2,
        in_specs=[_full((m, D_MODEL)), _full((1, D_MODEL)), _full((D_MODEL, D_MODEL))],
        out_specs=[_full((m, D_MODEL))] * 2,
        compiler_params=_params(),
    )(mem, g_mem, w_mkv)


def _halo_specs(s, tm, rows, width):
    nblk = s // rows
    prev = pl.BlockSpec((rows, width), lambda i: (jnp.maximum(i * (tm // rows) - 1, 0), 0))
    nxt = pl.BlockSpec((rows, width), lambda i: (jnp.minimum((i + 1) * (tm // rows), nblk - 1), 0))
    return prev, nxt


def _conv_common(pa, cu_prev, cu_next, w, tm):
    b, c, u, z = (pa[:, 512 * k:512 * k + 512] for k in range(4))
    cu = c * u
    row = lax.broadcasted_iota(jnp.int32, (tm, 512), 0)
    cu_m1 = jnp.where(row == 0, cu_prev, pltpu.roll(cu, 1, 0))
    cu_p1 = jnp.where(row == tm - 1, cu_next, pltpu.roll(cu, tm - 1, 0))
    y = cu_m1 * w[0:1] + cu * w[1:2] + cu_p1 * w[2:3]
    sig = _sigmoid(z)
    return b, c, u, z, cu, cu_m1, cu_p1, y, sig, row


def _conv_fwd(pa, w_conv):
    s = pa.shape[0]
    tm = min(512, s)
    nt = s // tm

    def body(pa_ref, pp_ref, pn_ref, w_ref, ya_ref):
        i = pl.program_id(0)
        prev_row = pp_ref[...].astype(F32)[15:16, :]
        next_row = pn_ref[...].astype(F32)[0:1, :]
        b, _, _, z, _, _, _, y, sig, _ = _conv_common(
            pa_ref[...].astype(F32),
            jnp.where(i == 0, 0.0, prev_row[:, 512:1024] * prev_row[:, 1024:1536]),
            jnp.where(i == nt - 1, 0.0, next_row[:, 512:1024] * next_row[:, 1024:1536]), w_ref[...], tm)
        ya_ref[...] = (b * y * (z * sig)).astype(BF16)

    prev, nxt = _halo_specs(s, tm, 16, 2048)
    return pl.pallas_call(
        body, name="conv_fwd", grid=(nt,),
        out_shape=jax.ShapeDtypeStruct((s, 512), BF16),
        in_specs=[_rows(tm, 2048), prev, nxt, _full((3, 512))],
        out_specs=_rows(tm, 512),
        compiler_params=_params(),
    )(pa, pa, pa, w_conv)


def _heads_to_lanes(a, g, row):
    low = row < HEAD_DIM
    parts = []
    for b in (2 * g, 2 * g + 1):
        t = jnp.transpose(a[:, 128 * b:128 * b + 128])
        swapped = pltpu.roll(t, HEAD_DIM, 0)
        if g == 0:
            parts += [jnp.where(low, t, 0.0), jnp.where(low, swapped, 0.0)]
        else:
            parts += [jnp.where(low, 0.0, swapped), jnp.where(low, 0.0, t)]
    return jnp.concatenate(parts, axis=1)


def _lanes_to_heads(t0, t1, row):
    low = row < HEAD_DIM
    blocks = []
    for b in range(4):
        g = b // 2
        tg = (t0, t1)[g]
        je = 2 * (b - 2 * g)
        even, odd = tg[:, 128 * je:128 * je + 128], tg[:, 128 * je + 128:128 * je + 256]
        if g == 0:
            t = jnp.where(low, even, pltpu.roll(odd, HEAD_DIM, 0))
        else:
            t = jnp.where(low, pltpu.roll(even, HEAD_DIM, 0), odd)
        blocks.append(jnp.transpose(t))
    return jnp.concatenate(blocks, axis=1)


WINDOW_KEYS = 3 * ATTN_BLOCK
STACKED = 4 * ATTN_BLOCK
KEY_CHUNK = 32
MAX_BLOCKS_IN_STEP = 8


def _fill_band_bias(bias, nb):
    assert nb >= 2
    c = lax.broadcasted_iota(jnp.int32, (WINDOW_KEYS, STACKED), 0)
    r = lax.broadcasted_iota(jnp.int32, (WINDOW_KEYS, STACKED), 1) & (ATTN_BLOCK - 1)
    band = (c >= r) & (c <= r + 2 * ATTN_BLOCK)
    for v, ok in enumerate((band, band & (c >= ATTN_BLOCK), band & (c < 2 * ATTN_BLOCK))):
        bias[v] = jnp.where(ok, 0.0, -jnp.inf)


def _bias_variant(n, nb):
    return jnp.where(n == 0, 1, jnp.where(n == nb - 1, 2, 0))


def _sink_row(sink_ref, g):
    return jnp.concatenate([jnp.full((1, ATTN_BLOCK), sink_ref[4 * g + j], F32) for j in range(4)], axis=1)


def _softmax_keys_major(sc, bias, variant, sink, e_scr):
    chunks = [pl.ds(k * KEY_CHUNK, KEY_CHUNK) for k in range(WINDOW_KEYS // KEY_CHUNK)]
    rows = [slice(k * KEY_CHUNK, (k + 1) * KEY_CHUNK) for k in range(WINDOW_KEYS // KEY_CHUNK)]
    m_run = jnp.full((KEY_CHUNK, STACKED), -jnp.inf, F32)
    for ck, rw in zip(chunks, rows):
        m_run = jnp.maximum(m_run, sc[rw] + bias[variant, ck, :])
    m = jnp.maximum(jnp.max(m_run, axis=0, keepdims=True), sink)
    l_run = jnp.zeros((KEY_CHUNK, STACKED), F32)
    for ck, rw in zip(chunks, rows):
        e = jnp.exp(sc[rw] + bias[variant, ck, :] - m)
        l_run += e
        e_scr[rw, :] = e.astype(BF16)
    es = jnp.exp(sink - m)
    inv = 1.0 / (jnp.sum(l_run, axis=0, keepdims=True) + es)
    return inv, es * inv


def _fill_padded(kv_ref, kpad, vpad, s):
    zero = jnp.zeros((ATTN_BLOCK, 128), BF16)
    kpad[0:ATTN_BLOCK, :] = zero
    vpad[0:ATTN_BLOCK, :] = zero
    kpad[ATTN_BLOCK + s:2 * ATTN_BLOCK + s, :] = zero
    vpad[ATTN_BLOCK + s:2 * ATTN_BLOCK + s, :] = zero
    kpad[ATTN_BLOCK:ATTN_BLOCK + s, :] = kv_ref[:, 0:128]
    vpad[ATTN_BLOCK:ATTN_BLOCK + s, :] = kv_ref[:, 128:256]


def _attn_fwd(pq, pkv, pbz, sink):
    s = pq.shape[0]
    nb = s // ATTN_BLOCK
    bps = min(MAX_BLOCKS_IN_STEP, nb)

    def body(sink_ref, q_ref, z_ref, kv_ref, yb_ref, kpad, vpad, bias, e_scr):
        i = pl.program_id(0)

        @pl.when(i == 0)
        def _():
            _fill_padded(kv_ref, kpad, vpad, s)
            _fill_band_bias(bias, nb)

        row = lax.broadcasted_iota(jnp.int32, (ATTN_BLOCK, 128), 0)
        for b in range(bps):
            n = i * bps + b
            rows = slice(b * ATTN_BLOCK, (b + 1) * ATTN_BLOCK)
            start = pl.multiple_of(n * ATTN_BLOCK, ATTN_BLOCK)
            kw, vw = kpad[pl.ds(start, WINDOW_KEYS), :], vpad[pl.ds(start, WINDOW_KEYS), :]
            qf = q_ref[rows, :].astype(F32)
            variant = _bias_variant(n, nb)
            outs = []
            for g in range(2):
                e_bg = e_scr.at[2 * b + g]
                qt = (_heads_to_lanes(qf, g, row) * ATTN_SCALE).astype(BF16)
                inv, _ = _softmax_keys_major(_dot(kw, qt), bias, variant, _sink_row(sink_ref, g), e_bg)
                outs.append(_dot_tn(vw, e_bg[...]) * inv)
            attn = _lanes_to_heads(outs[0], outs[1], row)
            z = z_ref[rows, :].astype(F32)
            yb_ref[rows, :] = (attn * (z * _sigmoid(z))).astype(BF16)

    tq = bps * ATTN_BLOCK
    return pl.pallas_call(
        body, name="attn_fwd", grid=(s // tq,),
        out_shape=jax.ShapeDtypeStruct((s, 512), BF16),
        in_specs=[pl.BlockSpec(memory_space=pltpu.SMEM), _rows(tq, 512), _rows(tq, 512), _full((s, 256))],
        out_specs=_rows(tq, 512),
        scratch_shapes=[pltpu.VMEM((s + 2 * ATTN_BLOCK, 128), BF16)] * 2
        + [pltpu.VMEM((3, WINDOW_KEYS, STACKED), F32),
           pltpu.VMEM((2 * bps, WINDOW_KEYS, STACKED), BF16)],
        compiler_params=_params(),
    )(sink, pq, pbz, pkv)


def _mem_softmax_t(q, mk):
    sc = _dot_nt(mk, q) * MEM_SCALE
    e = jnp.exp(sc - jnp.max(sc, axis=0, keepdims=True))
    return e * (1.0 / jnp.sum(e, axis=0, keepdims=True))


def _mem_attn_fwd(pmq, pmz, mkv):
    s = pmq.shape[0]
    m = mkv.shape[0]
    tm = min(512, s)

    def body(q_ref, z_ref, mk_ref, mv_ref, ym_ref):
        z = z_ref[...].astype(F32)
        sz = z * _sigmoid(z)
        for h in range(MEM_HEADS):
            cols = slice(128 * h, 128 * h + 128)
            pt = _mem_softmax_t(q_ref[:, cols], mk_ref[:, cols])
            o = _dot_tn(pt.astype(BF16), mv_ref[:, cols])
            ym_ref[:, cols] = (o * sz[:, cols]).astype(BF16)

    return pl.pallas_call(
        body, name="mem_attn_fwd", grid=(s // tm,),
        out_shape=jax.ShapeDtypeStruct((s, 512), BF16),
        in_specs=[_rows(tm, 512), _rows(tm, 512), pl.BlockSpec((m, 512), lambda i: (0, 0)),
                  pl.BlockSpec((m, 512), lambda i: (0, 1))],
        out_specs=_rows(tm, 512),
        compiler_params=_params(),
    )(pmq, pmz, mkv, mkv)


def _mid(ya, yb, ym, pg, x, target, g_post, w_up, w_out):
    s = x.shape[0]
    tm = min(256, s)
    nt = s // tm

    def body(ya_ref, yb_ref, ym_ref, pg_ref, x_ref, t_ref, gp_ref, wup_hbm, wout_hbm,
             dg_ref, dya_ref, dyb_ref, dym_ref, dy_ref, loss_ref, ggp_ref, mb_ref, dob_ref, du_ref,
             wup_vm, wout_vm, sems):
        i = pl.program_id(0)
        _load_once([(wup_hbm.at[d], wup_vm.at[:, pl.ds(128 * d, 128)]) for d in range(N_DEV)]
                   + [(wout_hbm, wout_vm)], sems)

        @pl.when(i == 0)
        def _():
            loss_ref[...] = jnp.zeros_like(loss_ref)
            ggp_ref[...] = jnp.zeros_like(ggp_ref)

        ys = (ya_ref[...], yb_ref[...], ym_ref[...])
        us = [_dot(ys[k], wup_vm[512 * k:512 * k + 512, :]) for k in range(3)]
        gates = [_sigmoid(pg_ref[:, 1024 * k:1024 * k + 1024].astype(F32)) for k in range(3)]
        merged = gates[0] * us[0] + gates[1] * us[1] + gates[2] * us[2]
        mb = merged.astype(BF16)
        mb_ref[...] = mb
        out = _dot(mb, wout_vm[...])
        r = lax.rsqrt(jnp.mean(out * out, axis=-1, keepdims=True) + EPS)
        on = out * r
        gp = gp_ref[...]
        err = (x_ref[...] + on * gp) - t_ref[...]
        loss_ref[...] += 0.5 * jnp.sum(err * err) * (1.0 / D_MODEL)
        dy = err * (1.0 / D_MODEL)
        dy_ref[...] = dy
        ggp_ref[...] += jnp.sum(dy * on, axis=0, keepdims=True)
        a = dy * gp
        d_out = r * (a - on * jnp.mean(a * on, axis=-1, keepdims=True))
        dob = d_out.astype(BF16)
        dob_ref[...] = dob
        d_merged = _dot_nt(dob, wout_vm[...])
        d_refs = (dya_ref, dyb_ref, dym_ref)
        for k in range(3):
            g = gates[k]
            du_f = d_merged * g
            dg_ref[:, 1024 * k:1024 * k + 1024] = (du_f * us[k] * (1.0 - g)).astype(BF16)
            du = du_f.astype(BF16)
            du_ref[k] = du
            d_refs[k][...] = _dot_nt(du, wup_vm[512 * k:512 * k + 512, :]).astype(BF16)

    return pl.pallas_call(
        body, name="mid", grid=(nt,),
        out_shape=[jax.ShapeDtypeStruct((s, 3072), BF16)] + [jax.ShapeDtypeStruct((s, 512), BF16)] * 3
        + [jax.ShapeDtypeStruct((s, D_MODEL), F32), jax.ShapeDtypeStruct((8, 128), F32),
           jax.ShapeDtypeStruct((1, D_MODEL), F32), jax.ShapeDtypeStruct((s, D_MODEL), BF16),
           jax.ShapeDtypeStruct((s, D_MODEL), BF16), jax.ShapeDtypeStruct((3, s, D_MODEL), BF16)],
        in_specs=[_rows(tm, 512)] * 3 + [_rows(tm, 3072), _rows(tm, D_MODEL), _rows(tm, D_MODEL),
                                         _full((1, D_MODEL)), ANY, ANY],
        out_specs=[_rows(tm, 3072)] + [_rows(tm, 512)] * 3
        + [_rows(tm, D_MODEL), _full((8, 128)), _full((1, D_MODEL)), _rows(tm, D_MODEL), _rows(tm, D_MODEL),
           pl.BlockSpec((3, tm, D_MODEL), lambda i: (0, i, 0))],
        scratch_shapes=[pltpu.VMEM((1536, D_MODEL), BF16), pltpu.VMEM((D_MODEL, D_MODEL), BF16),
                        pltpu.SemaphoreType.DMA((N_DEV + 1,))],
        compiler_params=_params(),
    )(ya, yb, ym, pg, x, target, g_post, w_up, w_out)


def _gw_mid(mb, dob, ys, du):
    s = mb.shape[0]
    tn = 256
    n_out = D_MODEL // tn
    tiles = [(0, c0, 0) for c0 in range(0, D_MODEL, tn)]
    tiles += [(1 + k, c0, 1 + k) for k in range(3) for c0 in range(0, 512, tn)]
    n_t = len(tiles)

    def body(mb_hbm, ya_hbm, yb_hbm, ym_hbm, dob_hbm, du_hbm, out_hbm, up_hbm,
             lhs, rhs, res_out, res_up, in_sems, rhs_sems, out_sems):
        lhs_hbm = (mb_hbm, ya_hbm, yb_hbm, ym_hbm)

        def load(t):
            a, c0, _ = tiles[t]
            return pltpu.make_async_copy(lhs_hbm[a].at[:, pl.ds(c0, tn)], lhs.at[t & 1], in_sems.at[t & 1])

        def load_rhs(g):
            src = dob_hbm if g == 0 else du_hbm.at[g - 1]
            return pltpu.make_async_copy(src, rhs.at[g & 1], rhs_sems.at[g & 1])

        def store(t):
            if t < n_out:
                return pltpu.make_async_copy(res_out.at[t & 1], out_hbm.at[pl.ds(t * tn, tn)], out_sems.at[t & 1])
            rows = pl.ds((t - n_out) * tn, tn)
            return pltpu.make_async_copy(res_up.at[t & 1], up_hbm.at[:, rows, :], out_sems.at[t & 1])

        load_rhs(0).start()
        load(0).start()
        for t, (_, _, g) in enumerate(tiles):
            new_rhs = t == 0 or tiles[t - 1][2] != g
            if t + 1 < n_t:
                load(t + 1).start()
            if new_rhs and g < 3:
                load_rhs(g + 1).start()
            load(t).wait()
            if new_rhs:
                load_rhs(g).wait()
            if t >= 2:
                store(t - 2).wait()
            r = _dot_tn(lhs[t & 1], rhs[g & 1])
            if t < n_out:
                res_out[t & 1] = r.astype(BF16)
            else:
                for d in range(N_DEV):
                    res_up[t & 1, d] = r[:, 128 * d:128 * d + 128].astype(BF16)
            store(t).start()
        store(n_t - 2).wait()
        store(n_t - 1).wait()

    return pl.pallas_call(
        body, name="gw_mid",
        out_shape=[jax.ShapeDtypeStruct((D_MODEL, D_MODEL), BF16), jax.ShapeDtypeStruct((N_DEV, 1536, 128), BF16)],
        in_specs=[ANY] * 6, out_specs=[ANY, ANY],
        scratch_shapes=[pltpu.VMEM((2, s, tn), BF16), pltpu.VMEM((2, s, D_MODEL), BF16),
                        pltpu.VMEM((2, tn, D_MODEL), BF16), pltpu.VMEM((2, N_DEV, tn, 128), BF16),
                        pltpu.SemaphoreType.DMA((2,)), pltpu.SemaphoreType.DMA((2,)),
                        pltpu.SemaphoreType.DMA((2,))],
        compiler_params=pltpu.CompilerParams(vmem_limit_bytes=CALL_VMEM_MB * 1024 * 1024),
    )(mb, *ys, dob, du)


def _conv_bwd(after, pa, dya, w_conv):
    s = pa.shape[0]
    tm = min(512, s)
    nt = s // tm

    def body(after_ref, pa_ref, pp_ref, pn_ref, d_ref, dp_ref, dn_ref, w_ref, da_ref, gw_ref):
        i = pl.program_id(0)
        first, last = i == 0, i == nt - 1

        @pl.when(first)
        def _():
            gw_ref[...] = jnp.zeros_like(gw_ref)

        w = w_ref[...]
        prev_row = pp_ref[...].astype(F32)[15:16, :]
        next_row = pn_ref[...].astype(F32)[0:1, :]
        b, c, u, z, cu, cu_m1, cu_p1, y, sig, row = _conv_common(
            pa_ref[...].astype(F32),
            jnp.where(first, 0.0, prev_row[:, 512:1024] * prev_row[:, 1024:1536]),
            jnp.where(last, 0.0, next_row[:, 512:1024] * next_row[:, 1024:1536]), w, tm)
        sz = z * sig
        dya_t = d_ref[...].astype(F32)
        d_y = dya_t * b * sz

        def halo_dy(p_row, d_row):
            zz = p_row[:, 1536:2048]
            return d_row * p_row[:, 0:512] * (zz * _sigmoid(zz))

        dy_prev = jnp.where(first, 0.0, halo_dy(prev_row, dp_ref[...].astype(F32)[15:16, :]))
        dy_next = jnp.where(last, 0.0, halo_dy(next_row, dn_ref[...].astype(F32)[0:1, :]))
        dy_m1 = jnp.where(row == 0, dy_prev, pltpu.roll(d_y, 1, 0))
        dy_p1 = jnp.where(row == tm - 1, dy_next, pltpu.roll(d_y, tm - 1, 0))
        d_cu = dy_p1 * w[0:1] + d_y * w[1:2] + dy_m1 * w[2:3]
        da_ref[:, 0:512] = (dya_t * y * sz).astype(BF16)
        da_ref[:, 512:1024] = (d_cu * u).astype(BF16)
        da_ref[:, 1024:1536] = (d_cu * c).astype(BF16)
        da_ref[:, 1536:2048] = (dya_t * b * y * (sig + sz * (1.0 - sig))).astype(BF16)
        gw_ref[0:1, :] += jnp.sum(d_y * cu_m1, axis=0, keepdims=True)
        gw_ref[1:2, :] += jnp.sum(d_y * cu, axis=0, keepdims=True)
        gw_ref[2:3, :] += jnp.sum(d_y * cu_p1, axis=0, keepdims=True)

    prev, nxt = _halo_specs(s, tm, 16, 2048)
    dprev, dnxt = _halo_specs(s, tm, 16, 512)
    return pl.pallas_call(
        body, name="conv_bwd", grid=(nt,),
        out_shape=[jax.ShapeDtypeStruct((s, 2048), BF16), jax.ShapeDtypeStruct((8, 512), F32)],
        in_specs=[_full(TOKEN), _rows(tm, 2048), prev, nxt, _rows(tm, 512), dprev, dnxt, _full((3, 512))],
        out_specs=[_rows(tm, 2048), _full((8, 512))],
        compiler_params=_params(),
    )(after, pa, pa, pa, dya, dya, dya, w_conv)


def _attn_bwd(after, pq, pkv, pbz, dyb, sink, tabs):
    s = pq.shape[0]
    nb = s // ATTN_BLOCK
    bps = min(MAX_BLOCKS_IN_STEP, nb)

    def body(sink_ref, after_ref, q_ref, z_ref, d_ref, kv_ref, t_ref,
             dq_ref, dz_ref, dkv_ref, gs_ref, kpad, vpad, dk_acc, dv_acc, bias, e_scr, ds_scr):
        i = pl.program_id(0)

        @pl.when(i == 0)
        def _():
            _fill_padded(kv_ref, kpad, vpad, s)
            _fill_band_bias(bias, nb)
            dk_acc[...] = jnp.zeros_like(dk_acc)
            dv_acc[...] = jnp.zeros_like(dv_acc)
            gs_ref[...] = jnp.zeros_like(gs_ref)

        row = lax.broadcasted_iota(jnp.int32, (ATTN_BLOCK, 128), 0)
        for b in range(bps):
            n = i * bps + b
            rows = slice(b * ATTN_BLOCK, (b + 1) * ATTN_BLOCK)
            start = pl.multiple_of(n * ATTN_BLOCK, ATTN_BLOCK)
            kw, vw = kpad[pl.ds(start, WINDOW_KEYS), :], vpad[pl.ds(start, WINDOW_KEYS), :]
            qf = q_ref[rows, :].astype(F32)
            variant = _bias_variant(n, nb)
            z = z_ref[rows, :].astype(F32)
            sig = _sigmoid(z)
            dyb_t = d_ref[rows, :].astype(F32)
            d_attn = dyb_t * (z * sig)
            outs, dqs = [], []
            dk_w = jnp.zeros((WINDOW_KEYS, 128), F32)
            dv_w = jnp.zeros((WINDOW_KEYS, 128), F32)
            for g in range(2):
                e_bg, ds_bg = e_scr.at[2 * b + g], ds_scr.at[2 * b + g]
                qt = _heads_to_lanes(qf, g, row)
                inv, p_sink = _softmax_keys_major(
                    _dot(kw, (qt * ATTN_SCALE).astype(BF16)), bias, variant, _sink_row(sink_ref, g), e_bg)
                ot = _dot_tn(vw, e_bg[...]) * inv
                outs.append(ot)
                dot_ = _heads_to_lanes(d_attn, g, row)
                delta = jnp.sum(dot_ * ot, axis=0, keepdims=True)
                dpt = _dot(vw, dot_.astype(BF16))
                for k in range(WINDOW_KEYS // KEY_CHUNK):
                    rw = slice(k * KEY_CHUNK, (k + 1) * KEY_CHUNK)
                    ds_bg[rw, :] = (e_bg[rw, :].astype(F32) * (dpt[rw] - delta)).astype(BF16)
                sink_part = p_sink * delta
                for j in range(4):
                    h = 4 * g + j
                    gs_ref[h:h + 1, :] -= jnp.sum(sink_part[:, 128 * j:128 * j + 128])
                dqs.append(_dot_tn(kw, ds_bg[...]) * (inv * ATTN_SCALE))
                dk_w += _dot_nt(ds_bg[...], (qt * inv).astype(BF16)) * ATTN_SCALE
                dv_w += _dot_nt(e_bg[...], (dot_ * inv).astype(BF16))
            dk_acc[pl.ds(start, WINDOW_KEYS), :] += dk_w
            dv_acc[pl.ds(start, WINDOW_KEYS), :] += dv_w
            attn = _lanes_to_heads(outs[0], outs[1], row)
            dz_ref[rows, :] = (dyb_t * attn * (sig * (1.0 + z * (1.0 - sig)))).astype(BF16)
            dq = _lanes_to_heads(dqs[0], dqs[1], row)
            trows = pl.ds(start, ATTN_BLOCK)
            cs, s1, s2 = t_ref[0, trows, :], t_ref[1, trows, :], t_ref[2, trows, :]
            for blk in range(4):
                cols = slice(128 * blk, 128 * blk + 128)
                dq_ref[rows, cols] = _rope_t(dq[:, cols], cs, s1, s2).astype(BF16)

        @pl.when(i == nb // bps - 1)
        def _():
            dk = dk_acc[ATTN_BLOCK:ATTN_BLOCK + s, :]
            dkv_ref[:, 0:128] = _rope_t(dk, t_ref[0], t_ref[1], t_ref[2]).astype(BF16)
            dkv_ref[:, 128:256] = dv_acc[ATTN_BLOCK:ATTN_BLOCK + s, :].astype(BF16)

    tq = bps * ATTN_BLOCK
    tile = _rows(tq, 512)
    return pl.pallas_call(
        body, name="attn_bwd", grid=(s // tq,),
        out_shape=[jax.ShapeDtypeStruct((s, 512), BF16), jax.ShapeDtypeStruct((s, 512), BF16),
                   jax.ShapeDtypeStruct((s, 256), BF16), jax.ShapeDtypeStruct((8, 128), F32)],
        in_specs=[pl.BlockSpec(memory_space=pltpu.SMEM), _full(TOKEN), tile, tile, tile, _full((s, 256)),
                  _full((3, s, 128))],
        out_specs=[tile, tile, _full((s, 256)), _full((8, 128))],
        scratch_shapes=[pltpu.VMEM((s + 2 * ATTN_BLOCK, 128), BF16)] * 2
        + [pltpu.VMEM((s + 2 * ATTN_BLOCK, 128), F32)] * 2
        + [pltpu.VMEM((3, WINDOW_KEYS, STACKED), F32)]
        + [pltpu.VMEM((2 * bps, WINDOW_KEYS, STACKED), BF16)] * 2,
        compiler_params=_params(),
    )(sink, after, pq, pbz, dyb, pkv, tabs)


def _mem_attn_bwd(pmq, pmz, mkv, dym):
    s = pmq.shape[0]
    m = mkv.shape[0]
    tm = min(512, s)

    def body(q_ref, z_ref, d_ref, mk_ref, mv_ref, dq_ref, dz_ref, dmkv_ref):
        @pl.when(pl.program_id(0) == 0)
        def _():
            dmkv_ref[...] = jnp.zeros_like(dmkv_ref)

        z = z_ref[...].astype(F32)
        sig = _sigmoid(z)
        dym_t = d_ref[...].astype(F32)
        d_attn = dym_t * (z * sig)
        dsilu = sig * (1.0 + z * (1.0 - sig))
        for h in range(MEM_HEADS):
            cols = slice(128 * h, 128 * h + 128)
            q, mk, mv = q_ref[:, cols], mk_ref[:, cols], mv_ref[:, cols]
            pt = _mem_softmax_t(q, mk)
            pb = pt.astype(BF16)
            o = _dot_tn(pb, mv)
            dob = d_attn[:, cols].astype(BF16)
            dpt = _dot_nt(mv, dob)
            dst = (pt * (dpt - jnp.sum(pt * dpt, axis=0, keepdims=True))).astype(BF16)
            dq_ref[:, cols] = (_dot_tn(dst, mk) * MEM_SCALE).astype(BF16)
            dz_ref[:, cols] = (dym_t[:, cols] * o * dsilu[:, cols]).astype(BF16)
            dmkv_ref[:, cols] += _dot(dst, q) * MEM_SCALE
            dmkv_ref[:, 512 + 128 * h:512 + 128 * h + 128] += _dot(pb, dob)

    return pl.pallas_call(
        body, name="mem_attn_bwd", grid=(s // tm,),
        out_shape=[jax.ShapeDtypeStruct((s, 512), BF16), jax.ShapeDtypeStruct((s, 512), BF16),
                   jax.ShapeDtypeStruct((m, D_MODEL), F32)],
        in_specs=[_rows(tm, 512), _rows(tm, 512), _rows(tm, 512), pl.BlockSpec((m, 512), lambda i: (0, 0)),
                  pl.BlockSpec((m, 512), lambda i: (0, 1))],
        out_specs=[_rows(tm, 512), _rows(tm, 512), _full((m, D_MODEL))],
        compiler_params=_params(),
    )(pmq, pmz, dym, mkv, mkv)


def _mem_kv_bwd(mem, g_mem, mn, dmkv, w_mkv):
    m = mem.shape[0]

    def body(mem_ref, g_ref, mn_ref, d_ref, w_ref, gw_ref, gg_ref):
        db = d_ref[...].astype(BF16)
        gw_ref[...] = _dot_tn(mn_ref[...], db).astype(BF16)
        d_mn = _dot_nt(db, w_ref[...])
        xf = mem_ref[...]
        r = lax.rsqrt(jnp.mean(xf * xf, axis=-1, keepdims=True) + EPS)
        gg_ref[...] = jnp.sum(d_mn * (xf * r), axis=0, keepdims=True)

    return pl.pallas_call(
        body, name="mem_kv_bwd", grid=(1,),
        out_shape=[jax.ShapeDtypeStruct((D_MODEL, D_MODEL), BF16), jax.ShapeDtypeStruct((1, D_MODEL), F32)],
        in_specs=[_full((m, D_MODEL)), _full((1, D_MODEL)), _full((m, D_MODEL)), _full((m, D_MODEL)),
                  _full((D_MODEL, D_MODEL))],
        out_specs=[_full((D_MODEL, D_MODEL)), _full((1, D_MODEL))],
        compiler_params=_params(),
    )(mem, g_mem, mn, dmkv, w_mkv)


def _dh_bwd(after, dparts, x, dy, g_pre, w_int):
    s = x.shape[0]
    tm = min(256, s)

    def body(after_ref, *refs):
        d_refs = refs[:7]
        x_ref, dy_ref, g_ref, w_hbm, gx_ref, gg_ref, w_vm, sems = refs[7:]
        _load_once([(w_hbm, w_vm)], sems)

        @pl.when(pl.program_id(0) == 0)
        def _():
            gg_ref[...] = jnp.zeros_like(gg_ref)

        d_h = jnp.zeros((tm, D_MODEL), F32)
        for d_ref, (r0, width) in zip(d_refs, SEGS):
            for c0 in range(0, width, 512):
                cw = min(512, width - c0)
                d_h += _dot(d_ref[:, c0:c0 + cw], w_vm[r0 + c0:r0 + c0 + cw, :])
        xf = x_ref[...]
        r = lax.rsqrt(jnp.mean(xf * xf, axis=-1, keepdims=True) + EPS)
        xn = xf * r
        a = d_h * g_ref[...]
        gx_ref[...] = r * (a - xn * jnp.mean(a * xn, axis=-1, keepdims=True)) + dy_ref[...]
        gg_ref[...] += jnp.sum(d_h * xn, axis=0, keepdims=True)

    return pl.pallas_call(
        body, name="dh_bwd", grid=(s // tm,),
        out_shape=[jax.ShapeDtypeStruct((s, D_MODEL), F32), jax.ShapeDtypeStruct((1, D_MODEL), F32)],
        in_specs=[_full(TOKEN)] + [_rows(tm, w) for _, w in SEGS]
        + [_rows(tm, D_MODEL), _rows(tm, D_MODEL), _full((1, D_MODEL)), ANY],
        out_specs=[_rows(tm, D_MODEL), _full((1, D_MODEL))],
        scratch_shapes=[pltpu.VMEM((IN_WIDTH, D_MODEL), BF16), pltpu.SemaphoreType.DMA((1,))],
        compiler_params=_params(),
    )(after, *dparts, x, dy, g_pre, w_int)


def _gw_in(dparts, h):
    s = h.shape[0]
    tn = 256
    tiles = [(a, c0) for a, (_, width) in enumerate(SEGS) for c0 in range(0, width, tn)]
    n_t = len(tiles)
    assert n_t * tn == IN_WIDTH

    def body(*refs):
        d_hbm = refs[:7]
        h_hbm, out_hbm, lhs, h_vm, res, in_sems, out_sems, h_sem = refs[7:]

        def load(a, c0, slot):
            return pltpu.make_async_copy(d_hbm[a].at[:, pl.ds(c0, tn)], lhs.at[slot], in_sems.at[slot])

        def store(t, slot):
            rows = pl.ds(pl.multiple_of(t * tn, tn), tn)
            return pltpu.make_async_copy(res.at[slot], out_hbm.at[rows], out_sems.at[slot])

        def start_load(t, slot):
            for k, (a, c0) in enumerate(tiles):
                @pl.when(t == k)
                def _(a=a, c0=c0):
                    load(a, c0, slot).start()

        h_copy = pltpu.make_async_copy(h_hbm, h_vm, h_sem)
        h_copy.start()
        load(*tiles[0], 0).start()
        h_copy.wait()

        def step(t, carry):
            slot = t & 1

            @pl.when(t + 1 < n_t)
            def _():
                start_load(t + 1, 1 - slot)

            load(*tiles[0], slot).wait()

            @pl.when(t >= 2)
            def _():
                store(t - 2, slot).wait()

            res[slot] = _dot_tn(lhs[slot], h_vm[...]).astype(BF16)
            store(t, slot).start()
            return carry

        lax.fori_loop(0, n_t, step, 0)
        store(n_t - 2, (n_t - 2) & 1).wait()
        store(n_t - 1, (n_t - 1) & 1).wait()

    return pl.pallas_call(
        body, name="gw_in",
        out_shape=jax.ShapeDtypeStruct((IN_WIDTH, D_MODEL), BF16),
        in_specs=[ANY] * 8, out_specs=ANY,
        scratch_shapes=[pltpu.VMEM((2, s, tn), BF16), pltpu.VMEM((s, D_MODEL), BF16),
                        pltpu.VMEM((2, tn, D_MODEL), BF16), pltpu.SemaphoreType.DMA((2,)),
                        pltpu.SemaphoreType.DMA((2,)), pltpu.SemaphoreType.DMA(())],
        compiler_params=pltpu.CompilerParams(vmem_limit_bytes=CALL_VMEM_MB * 1024 * 1024),
    )(*dparts, h)


def _adamw_math(w, g, m, v):
    m2 = ADAM_B1 * m + (1.0 - ADAM_B1) * g
    v2 = ADAM_B2 * v + (1.0 - ADAM_B2) * (g * g)
    m_hat = m2 / (1.0 - ADAM_B1 ** ADAM_STEP)
    v_hat = v2 / (1.0 - ADAM_B2 ** ADAM_STEP)
    delta = -ADAM_LR * (m_hat / (jnp.sqrt(v_hat) + ADAM_EPS) + ADAM_WD * w)
    return delta, m2, v2


def _sum_adamw(after, own, land, chip, block, w, m, v, name, tiles=1):
    r, c = w.shape
    rt = r // tiles

    def body(c_ref, after_ref, own_ref, l1_ref, l2_ref, l3_ref, w_ref, m_ref, v_ref, g_ref, d_ref, m2_ref, v2_ref):
        g = own_ref[...].astype(F32)
        for l_ref in (l1_ref, l2_ref, l3_ref):
            g += l_ref[...].astype(F32)
        g_ref[...] = g
        d_ref[...], m2_ref[...], v2_ref[...] = _adamw_math(w_ref[...], g, m_ref[...], v_ref[...])

    def share(k):
        return pl.BlockSpec((None, rt, c), lambda i, c_ref: (jnp.bitwise_xor(c_ref[0], k), block * tiles + i, 0))

    spec = pl.BlockSpec((rt, c), lambda i, c_ref: (i, 0))
    grid_spec = pltpu.PrefetchScalarGridSpec(
        num_scalar_prefetch=1, grid=(tiles,),
        in_specs=[ANY, share(0), share(1), share(2), share(3)] + [spec] * 3, out_specs=[spec] * 4)
    return pl.pallas_call(
        body, name=name, grid_spec=grid_spec,
        out_shape=[jax.ShapeDtypeStruct((r, c), F32)] * 4,
        compiler_params=_params(),
    )(chip, after, own, land, land, land, w, m, v)


def _sum_adamw_group(after, items, chip, name):
    k = len(items)

    def body(c_ref, after_ref, *refs):
        shares, wmv, outs = refs[:4 * k], refs[4 * k:7 * k], refs[7 * k:]
        for j in range(k):
            g = shares[4 * j][...].astype(F32)
            for l_ref in shares[4 * j + 1:4 * j + 4]:
                g += l_ref[...].astype(F32)
            outs[4 * j][...] = g
            outs[4 * j + 1][...], outs[4 * j + 2][...], outs[4 * j + 3][...] = _adamw_math(
                wmv[3 * j][...], g, wmv[3 * j + 1][...], wmv[3 * j + 2][...])

    def share(shape, block, q):
        return pl.BlockSpec((None,) + shape, lambda i, c_ref: (jnp.bitwise_xor(c_ref[0], q), block, 0))

    in_specs, args = [ANY], [after]
    for own, land, block, w, m, v in items:
        in_specs += [share(w.shape, block, q) for q in range(4)]
        args += [own, land, land, land]
    for own, land, block, w, m, v in items:
        in_specs += [pl.BlockSpec(w.shape, lambda i, c_ref: (0, 0))] * 3
        args += [w, m, v]
    out_specs = [pl.BlockSpec(w.shape, lambda i, c_ref: (0, 0)) for _, _, _, w, _, _ in items for _ in range(4)]
    res = pl.pallas_call(
        body, name=name,
        grid_spec=pltpu.PrefetchScalarGridSpec(num_scalar_prefetch=1, grid=(1,), in_specs=in_specs,
                                               out_specs=out_specs),
        out_shape=[jax.ShapeDtypeStruct(w.shape, F32) for _, _, _, w, _, _ in items for _ in range(4)],
        compiler_params=_params(),
    )(chip, *args)
    return [res[4 * j:4 * j + 4] for j in range(k)]


def _small_pack(parts):
    def pack_body(gpre_ref, gconv_ref, gsink_ref, gmem_ref, gpost_ref, loss_ref, pack):
        lane = lax.broadcasted_iota(jnp.int32, (1, 128), 1)
        sink_row = jnp.zeros((1, 128), F32)
        for h in range(8):
            sink_row = jnp.where(lane == h, gsink_ref[h:h + 1, :], sink_row)
        pack[...] = jnp.zeros_like(pack)
        pack[0:1, :] = gpre_ref[...]
        pack[1:2, :] = gmem_ref[...]
        pack[2:3, :] = gpost_ref[...]
        pack[3:6, 0:512] = gconv_ref[0:3, :]
        pack[6:7, 0:128] = sink_row
        pack[7:8, 0:128] = loss_ref[0:1, :]

    return pl.pallas_call(
        pack_body, name="small_pack", grid=(1,),
        out_shape=jax.ShapeDtypeStruct((8, D_MODEL), F32),
        in_specs=[_full(p.shape) for p in parts], out_specs=_full((8, D_MODEL)),
    )(*parts)


def _small_apply(packs, ws, ms, vs):
    def apply(p_ref, *refs):
        w_refs, m_refs, v_refs = refs[0:5], refs[5:10], refs[10:15]
        loss_out = refs[15]
        g_outs, d_outs, m_outs, v_outs = refs[16:21], refs[21:26], refs[26:31], refs[31:36]
        x, y, c = _my_place()
        tot = p_ref[0]
        for d in range(1, N_DEV):
            tot = tot + p_ref[d]
        conv = pltpu.roll(tot[:, 0:512], (512 - 64 * (4 * x + 2 * y + c)) % 512, 1)[3:6, 0:64]
        grads = (tot[0:1, :], conv, tot[6:7, 0:8], tot[1:2, :], tot[2:3, :])
        loss_out[...] = tot[7:8, 0:128]
        for j in range(5):
            if len(w_refs[j].shape) == 3:
                for k in range(w_refs[j].shape[0]):
                    g_outs[j][k] = grads[j][k:k + 1]
                    d_outs[j][k], m_outs[j][k], v_outs[j][k] = _adamw_math(
                        w_refs[j][k], grads[j][k:k + 1], m_refs[j][k], v_refs[j][k])
                continue
            g_outs[j][...] = grads[j]
            d_outs[j][...], m_outs[j][...], v_outs[j][...] = _adamw_math(
                w_refs[j][...], grads[j], m_refs[j][...], v_refs[j][...])

    specs = [_full(w.shape) for w in ws]
    res = pl.pallas_call(
        apply, name="small_apply", grid=(1,),
        out_shape=[jax.ShapeDtypeStruct((1, 128), F32)] + [jax.ShapeDtypeStruct(w.shape, F32) for w in ws] * 4,
        in_specs=[_full((N_DEV, 8, D_MODEL))] + specs * 3,
        out_specs=[_full((1, 128))] + specs * 4,
    )(packs, *ws, *ms, *vs)
    return res[0], res[1:6], res[6:11], res[11:16], res[16:21]


def kernel(x, mem, g_pre, w_in, w_conv, attn_sink, g_mem, w_mem_kv, w_up_a, w_up_b, w_up_m, w_out, g_post, loss_target, m_g_pre, m_w_in, m_w_conv, m_attn_sink, m_g_mem, m_w_mem_kv, m_w_up_a, m_w_up_b, m_w_up_m, m_w_out, m_g_post, v_g_pre, v_w_in, v_w_conv, v_attn_sink, v_g_mem, v_w_mem_kv, v_w_up_a, v_w_up_b, v_w_up_m, v_w_out, v_g_post):
    s = x.shape[1]
    x2, mem2, tgt2 = x[0], mem[0], loss_target[0]
    me = 4 * lax.axis_index("x") + 2 * lax.axis_index("y") + lax.axis_index("c")

    w_conv_loc = jnp.zeros((8, 128), F32).at[:3, :64].set(w_conv[0])
    w_int_g, w_conv_g, tabs, w_mkv_loc, w_out_loc, w_up_loc = _all_gather(
        [w_in[0].T.astype(BF16), w_conv_loc], "gather_w_in",
        splits=[[(112 * k, 112) for k in range(7)] + [(784, 144)], [(0, 8)]],
        side=_gather_side(s, w_mem_kv[0], w_out[0], (w_up_a[0], w_up_b[0], w_up_m[0])))
    w_int = w_int_g.reshape(IN_WIDTH, D_MODEL)
    w_conv_f = w_conv_g[:, :3, :64].transpose(1, 0, 2).reshape(3, 512)
    late = _gather_start([w_mkv_loc, w_out_loc, w_up_loc], me, "gather_late_start")
    sink = attn_sink[0]

    h, pa, pq, pkv, pbz, pmq, pmz, pg = _proj_fwd(late[4], x2, g_pre, w_int, tabs)
    ya = _conv_fwd(pa, w_conv_f)
    yb = _attn_fwd(pq, pkv, pbz, sink)
    w_mkv_g, w_out_g, w_up_g = _gather_wait(*late[:4], yb, "gather_late_wait")
    w_mkv = w_mkv_g.reshape(D_MODEL, D_MODEL)
    w_out_f = w_out_g.reshape(D_MODEL, D_MODEL)
    mn, mkv = _mem_kv_fwd(mem2, g_mem, w_mkv)
    ym = _mem_attn_fwd(pmq, pmz, mkv)
    dg, dya, dyb, dym, dy, loss_p, gg_post, mb, dob, du = _mid(ya, yb, ym, pg, x2, tgt2, g_post, w_up_g, w_out_f)
    gw_out, gw_up = _gw_mid(mb, dob, (ya, yb, ym), du)

    core = lax.axis_index("c").astype(jnp.int32).reshape(1)
    chip = (2 * lax.axis_index("x") + lax.axis_index("y")).astype(jnp.int32).reshape(1)

    dmq, dmz, dmkv = _mem_attn_bwd(pmq, pmz, mkv, dym)
    gw_mkv, gg_mem = _mem_kv_bwd(mem2, g_mem, mn, dmkv, w_mkv)
    shares1 = [gw_mkv.reshape(N_DEV, 128, D_MODEL), gw_out.reshape(N_DEV, 128, D_MODEL), gw_up]
    sib = _split_start(_sibling_copies, N_CHIPS, shares1,
                       [lax.empty((N_CHIPS,) + a.shape[1:], a.dtype) for a in shares1], "grads_to_sibling_small_start")
    da, gw_conv = _conv_bwd(sib[4], pa, dya, w_conv_f)
    shares1, from_sibling = _split_wait(_sibling_copies, N_CHIPS, *sib[:4], da, "grads_to_sibling_small_wait")
    send1, recv1, srcs1, lands1, token1 = _chip_exchange_start(
        _pair_add(shares1, from_sibling, core, "grads_pair_add_small"), "grads_to_chips_start_small")
    dq, dbz, dkv, g_sink = _attn_bwd(token1, pq, pkv, pbz, dyb, sink, tabs)
    dparts = (da, dq, dkv, dbz, dmq, dmz, dg)
    gw_int = _gw_in(dparts, h)
    send2, recv2, srcs2, lands2, token2 = _chip_exchange_start(
        [_sibling_reduce(gw_int.reshape(N_DEV, SHARD_IN, D_MODEL), "grads_sibling_reduce_w_in")],
        "grads_to_chips_start_w_in")
    grad_x, gg_pre = _dh_bwd(token2, dparts, x2, dy, g_pre, w_int)
    (o_mkv, o_out, o_up, o_int), (l_mkv, l_out, l_up, l_int) = _chip_exchange_wait(
        send1 + send2, recv1 + recv2, srcs1 + srcs2, lands1 + lands2, grad_x, "grads_to_chips_wait")

    small = _gather_start([_small_pack((gg_pre, gw_conv, g_sink, gg_mem, gg_post, loss_p))], me,
                          "small_gather_start")

    w_in_t = _sum_adamw(small[4], o_int, l_int, chip, 0, w_in[0].T, m_w_in[0].T, v_w_in[0].T, "adamw_w_in", tiles=2)
    g_w_in, d_w_in, nm_w_in, nv_w_in = (t.T for t in w_in_t)
    (g_mkv, d_mkv, nm_mkv, nv_mkv), (g_out, d_out, nm_out, nv_out), *up = _sum_adamw_group(
        w_in_t[0],
        [(o_mkv, l_mkv, 0, w_mem_kv[0], m_w_mem_kv[0], v_w_mem_kv[0]),
         (o_out, l_out, 0, w_out[0], m_w_out[0], v_w_out[0]),
         (o_up, l_up, 0, w_up_a[0], m_w_up_a[0], v_w_up_a[0]),
         (o_up, l_up, 1, w_up_b[0], m_w_up_b[0], v_w_up_b[0]),
         (o_up, l_up, 2, w_up_m[0], m_w_up_m[0], v_w_up_m[0])], chip, "adamw_mid_weights")

    (packs,) = _gather_wait(*small[:4], g_out, "small_gather_wait")

    def taps_first(a):
        return a.transpose(1, 0, 2)

    loss_row, small_g, sd, sm, sv = _small_apply(
        packs, [g_pre, taps_first(w_conv), attn_sink, g_mem, g_post],
        [m_g_pre, taps_first(m_w_conv), m_attn_sink, m_g_mem, m_g_post],
        [v_g_pre, taps_first(v_w_conv), v_attn_sink, v_g_mem, v_g_post])
    loss = loss_row[0, 0]
    g_g_pre, g_conv, g_sink_tot, g_g_mem, g_g_post = small_g

    def lead(a):
        return a[None]

    grads = [g_g_pre, lead(g_w_in), taps_first(g_conv), g_sink_tot, g_g_mem, lead(g_mkv), lead(up[0][0]),
             lead(up[1][0]), lead(up[2][0]), lead(g_out), g_g_post]

    def assemble(small, big_in, big_mkv, big_up, big_out):
        return [small[0], lead(big_in), taps_first(small[1]), small[2], small[3], lead(big_mkv), lead(big_up[0]),
                lead(big_up[1]), lead(big_up[2]), lead(big_out), small[4]]

    deltas = assemble(sd, d_w_in, d_mkv, [u[1] for u in up], d_out)
    new_m = assemble(sm, nm_w_in, nm_mkv, [u[2] for u in up], nm_out)
    new_v = assemble(sv, nv_w_in, nv_mkv, [u[3] for u in up], nv_out)
    return (loss, grad_x[None], *grads, *deltas, *new_m, *new_v)
```

```python
import functools

import jax
import jax.numpy as jnp
from jax import lax
from jax.experimental import pallas as pl
from jax.experimental.pallas import tpu as pltpu

F32 = jnp.float32
BF16 = jnp.bfloat16
MESH = pl.DeviceIdType.MESH

N_DEV = 8
D_MODEL = 1024
EPS = 1e-6
ROPE_THETA = 500000.0
ROT_DIM = 16
HEAD_DIM = 64
ATTN_BLOCK = 128
MEM_HEADS = 4
MEM_HEAD_DIM = 128
ATTN_SCALE = HEAD_DIM ** -0.5
MEM_SCALE = MEM_HEAD_DIM ** -0.5

ADAM_LR = 0.001
ADAM_B1 = 0.9
ADAM_B2 = 0.999
ADAM_EPS = 1e-08
ADAM_WD = 0.01
ADAM_STEP = 10

SEG_A = (0, 2048)
SEG_BQ = (2048, 512)
SEG_BKV = (2560, 256)
SEG_BZ = (2816, 512)
SEG_MQ = (3328, 512)
SEG_MZ = (3840, 512)
SEG_G = (4352, 3072)
SEGS = (SEG_A, SEG_BQ, SEG_BKV, SEG_BZ, SEG_MQ, SEG_MZ, SEG_G)
IN_WIDTH = 7424
SHARD_IN = IN_WIDTH // N_DEV

V7X_VMEM_BYTES = 64 * 1024 * 1024
CALL_VMEM_MB = 57
ANY = pl.BlockSpec(memory_space=pl.ANY)


def _params():
    assert CALL_VMEM_MB * 1024 * 1024 < V7X_VMEM_BYTES
    return pltpu.CompilerParams(dimension_semantics=("arbitrary",), vmem_limit_bytes=CALL_VMEM_MB * 1024 * 1024)


def _full(shape):
    zeros = (0,) * len(shape)
    return pl.BlockSpec(shape, lambda i: zeros)


def _rows(tm, width):
    return pl.BlockSpec((tm, width), lambda i: (i, 0))


def _dot(a, b):
    return jnp.dot(a, b, preferred_element_type=F32)


def _dot_nt(a, b):
    return lax.dot_general(a, b, (((1,), (1,)), ((), ())), preferred_element_type=F32)


def _dot_tn(a, b):
    return lax.dot_general(a, b, (((0,), (0,)), ((), ())), preferred_element_type=F32)


def _sigmoid(z):
    return 1.0 / (1.0 + jnp.exp(-z))


def _rope(t, cs, s1, s2):
    return t * cs + pltpu.roll(t, 120, 1) * s1 + pltpu.roll(t, 8, 1) * s2


def _rope_t(d, cs, s1, s2):
    return d * cs + pltpu.roll(d * s1, 8, 1) + pltpu.roll(d * s2, 120, 1)


def _gather_side(s, w_mkv, w_out, w_ups):
    half = ROT_DIM // 2
    inv_freq = jnp.power(jnp.float32(ROPE_THETA), -jnp.arange(half, dtype=F32) * (2.0 / ROT_DIM))
    freq_row = jnp.tile(jnp.concatenate([inv_freq, inv_freq, jnp.zeros((HEAD_DIM - ROT_DIM,), F32)]), 2)[None, :]

    def fn(in_refs, out_refs):
        f_ref, mkv_ref, out_ref, *up_refs = in_refs
        t_ref, mkv_bf, out_bf, up_bf = out_refs
        mkv_bf[...] = mkv_ref[...].astype(BF16)
        out_bf[...] = out_ref[...].astype(BF16)
        for k, up_ref in enumerate(up_refs):
            up_bf[512 * k:512 * k + 512, :] = up_ref[...].astype(BF16)
        pos = lax.broadcasted_iota(jnp.int32, (s, 128), 0).astype(F32)
        d = lax.broadcasted_iota(jnp.int32, (s, 128), 1) & (HEAD_DIM - 1)
        ang = pos * f_ref[...]
        cos, sin = jnp.cos(ang), jnp.sin(ang)
        lo, hi = d < half, (d >= half) & (d < ROT_DIM)
        t_ref[0] = jnp.where(lo | hi, cos, 1.0)
        t_ref[1] = jnp.where(lo, -sin, 0.0)
        t_ref[2] = jnp.where(hi, sin, 0.0)

    return ([freq_row, w_mkv, w_out, *w_ups],
            [jax.ShapeDtypeStruct((3, s, 128), F32), jax.ShapeDtypeStruct(w_mkv.shape, BF16),
             jax.ShapeDtypeStruct(w_out.shape, BF16), jax.ShapeDtypeStruct((1536, 128), BF16)], fn)


def _load_once(pairs, sems):
    @pl.when(pl.program_id(0) == 0)
    def _():
        cps = [pltpu.make_async_copy(src, dst, sems.at[k]) for k, (src, dst) in enumerate(pairs)]
        for cp in cps:
            cp.start()
        for cp in cps:
            cp.wait()


def _my_place():
    x, y, c = lax.axis_index("x"), lax.axis_index("y"), lax.axis_index("c")
    return x, y, c


def _all_gather(arrs, name, splits=None, side=None):
    n = len(arrs)
    if splits is None:
        splits = [[(0, a.shape[0])] for a in arrs]
    pieces = [(a, r0, rn) for a in range(n) for r0, rn in splits[a]]
    n_p = len(pieces)
    side_in, side_out, side_fn = side if side is not None else ((), (), None)
    m, q = len(side_in), len(side_out)

    def body(*refs):
        ins, outs = refs[:n], refs[n + m:2 * n + m]
        send_sems, recv_sems, local_sems, side_sems, *side_vm = refs[2 * n + m + q:]
        side_stores = [pltpu.make_async_copy(side_vm[k], refs[2 * n + m + k], side_sems.at[k]) for k in range(q)]
        x, y, c = _my_place()
        me, sibling = (x, y, c), (x, y, 1 - c)

        def route(core):
            first = (jnp.bitwise_xor(x, 1 - core), jnp.bitwise_xor(y, core), core)
            second = (jnp.bitwise_xor(x, core), jnp.bitwise_xor(y, 1 - core), core)
            return first, second, (1 - x, 1 - y, core)

        def idx(px, py, pc):
            return 4 * px + 2 * py + pc

        def copy(p, k, block, to, own=False):
            a, r0, rn = pieces[p]
            dst = outs[a].at[idx(*block), pl.ds(r0, rn)]
            return pltpu.make_async_remote_copy(
                src_ref=ins[a].at[pl.ds(r0, rn)] if own else dst, dst_ref=dst,
                send_sem=send_sems.at[p * 7 + k], recv_sem=recv_sems.at[p * 7 + k],
                device_id=to, device_id_type=MESH)

        nbr1, nbr2, diag = route(c)
        mine = [pltpu.make_async_copy(ins[a], outs[a].at[idx(*me)], local_sems.at[a]) for a in range(n)]
        for cp in mine:
            cp.start()
        sent = []
        for p in range(n_p):
            for k, to in enumerate((sibling, nbr1, nbr2)):
                sent.append(copy(p, k, me, to, own=True))
        for cp in sent:
            cp.start()
        if side_fn is not None:
            side_fn(refs[n:n + m], side_vm)
            for cp in side_stores:
                cp.start()
        for k_in, block, onward in ((1, nbr1, ((3, nbr2), (4, sibling))), (2, nbr2, ((5, sibling),)),
                                    (3, diag, ((6, sibling),))):
            for p in range(n_p):
                copy(p, k_in, block, me).wait_recv()
                for k_out, to in onward:
                    cp = copy(p, k_out, block, to)
                    cp.start()
                    sent.append(cp)
        s1, s2, sd = route(1 - c)
        for k_in, block in ((0, sibling), (4, s1), (5, s2), (6, sd)):
            for p in range(n_p):
                copy(p, k_in, block, me).wait_recv()
        for cp in sent:
            cp.wait_send()
        for cp in mine:
            cp.wait()
        for cp in side_stores:
            cp.wait()

    return pl.pallas_call(
        body, name=name,
        out_shape=[jax.ShapeDtypeStruct((N_DEV,) + a.shape, a.dtype) for a in arrs]
        + [pltpu.HBM(o.shape, o.dtype) for o in side_out],
        in_specs=[ANY] * n + [pl.BlockSpec(memory_space=pltpu.VMEM)] * m,
        out_specs=[ANY] * n + [HBM] * q,
        scratch_shapes=[pltpu.SemaphoreType.DMA((7 * n_p,)), pltpu.SemaphoreType.DMA((7 * n_p,)),
                        pltpu.SemaphoreType.DMA((n,)), pltpu.SemaphoreType.DMA((max(q, 1),))]
        + [pltpu.VMEM(o.shape, o.dtype) for o in side_out],
        compiler_params=pltpu.CompilerParams(vmem_limit_bytes=32 * 1024 * 1024),
    )(*arrs, *side_in)


N_CHIPS = 4


def _sibling_reduce(arr, name):
    _, r, c = arr.shape

    def body(in_ref, out_ref, land, a_buf, b_buf, o_buf, send_sems, recv_sems, local_sems):
        x, y, core = _my_place()
        cps = [pltpu.make_async_remote_copy(
            src_ref=in_ref.at[2 * j + (1 - core)], dst_ref=land.at[j], send_sem=send_sems.at[j],
            recv_sem=recv_sems.at[j], device_id=(x, y, 1 - core), device_id_type=MESH) for j in range(N_CHIPS)]
        for cp in cps:
            cp.start()
        store = None
        for j in range(N_CHIPS):
            mine = pltpu.make_async_copy(in_ref.at[2 * j + core], a_buf, local_sems.at[0])
            mine.start()
            cps[j].wait_recv()
            theirs = pltpu.make_async_copy(land.at[j], b_buf, local_sems.at[1])
            theirs.start()
            mine.wait()
            theirs.wait()
            if store is not None:
                store.wait()
            o_buf[...] = (a_buf[...].astype(F32) + b_buf[...].astype(F32)).astype(BF16)
            store = pltpu.make_async_copy(o_buf, out_ref.at[j], local_sems.at[2])
            store.start()
        store.wait()
        for cp in cps:
            cp.wait_send()

    return pl.pallas_call(
        body, name=name,
        out_shape=[pltpu.HBM((N_CHIPS, r, c), BF16), jax.ShapeDtypeStruct((N_CHIPS, r, c), BF16)],
        in_specs=[ANY], out_specs=[HBM, ANY],
        scratch_shapes=[pltpu.VMEM((r, c), BF16)] * 3
        + [pltpu.SemaphoreType.DMA((N_CHIPS,)), pltpu.SemaphoreType.DMA((N_CHIPS,)), pltpu.SemaphoreType.DMA((3,))],
        compiler_params=pltpu.CompilerParams(vmem_limit_bytes=32 * 1024 * 1024),
    )(arr)[0]


def _sibling_copies(srcs, lands, send_sems, recv_sems):
    x, y, c = _my_place()
    cps = []
    for j in range(N_CHIPS):
        for a in range(len(srcs)):
            k = a * N_CHIPS + j
            cps.append(pltpu.make_async_remote_copy(
                src_ref=srcs[a].at[2 * j + (1 - c)], dst_ref=lands[a].at[j], send_sem=send_sems[k],
                recv_sem=recv_sems[k], device_id=(x, y, 1 - c), device_id_type=MESH))
    return cps


def _pair_add(mine, recv, core, name):
    n = len(mine)

    def body(c_ref, *refs):
        for a in range(n):
            refs[2 * n + a][...] = (refs[a][...].astype(F32) + refs[n + a][...].astype(F32)).astype(BF16)

    def blk(a):
        return (None,) + a.shape[1:]

    grid_spec = pltpu.PrefetchScalarGridSpec(
        num_scalar_prefetch=1, grid=(N_CHIPS,),
        in_specs=[pl.BlockSpec(blk(a), lambda j, c_ref: (2 * j + c_ref[0], 0, 0)) for a in mine]
        + [pl.BlockSpec(blk(a), lambda j, c_ref: (j, 0, 0)) for a in recv],
        out_specs=[pl.BlockSpec(blk(a), lambda j, c_ref: (j, 0, 0)) for a in recv])
    return pl.pallas_call(
        body, name=name, grid_spec=grid_spec,
        out_shape=[jax.ShapeDtypeStruct(a.shape, BF16) for a in recv],
        compiler_params=_params(),
    )(core, *mine, *recv)


HBM = pl.BlockSpec(memory_space=pltpu.HBM)
SEM = pl.BlockSpec(memory_space=pltpu.SEMAPHORE)
N_PEER_CHIPS = 3
TOKEN = (8, 128)


def _chip_copies(srcs, lands, send_sems, recv_sems):
    x, y, c = _my_place()
    my_chip = 2 * x + y
    peers = [(x, 1 - y), (1 - x, y), (1 - x, 1 - y)]
    cps = []
    for k, (px, py) in enumerate(peers):
        for a in range(len(srcs)):
            j = a * N_PEER_CHIPS + k
            cps.append(pltpu.make_async_remote_copy(
                src_ref=srcs[a].at[2 * px + py], dst_ref=lands[a].at[my_chip],
                send_sem=send_sems[j], recv_sem=recv_sems[j],
                device_id=(px, py, c), device_id_type=MESH))
    return cps


N_PEERS = N_DEV - 1


def _gather_copies(srcs, lands, send_sems, recv_sems):
    x, y, c = _my_place()
    me_idx = 4 * x + 2 * y + c
    flips = [(0, 0, 1), (0, 1, 0), (1, 0, 0), (0, 1, 1), (1, 0, 1), (1, 1, 0), (1, 1, 1)]
    cps = []
    for k, (fx, fy, fc) in enumerate(flips):
        peer = ((1 - x) if fx else x, (1 - y) if fy else y, (1 - c) if fc else c)
        for a in range(len(srcs)):
            j = a * N_PEERS + k
            cps.append(pltpu.make_async_remote_copy(
                src_ref=srcs[a], dst_ref=lands[a].at[me_idx], send_sem=send_sems[j], recv_sem=recv_sems[j],
                device_id=peer, device_id_type=MESH))
    return cps


def _split_start(copies, per_array, arrs, lands, name):
    arrs, lands = list(arrs), list(lands)
    n = len(arrs)
    k = n * per_array

    def body(*refs):
        srcs, land_refs = refs[:n], refs[n:2 * n]
        send_sems, recv_sems = refs[2 * n:2 * n + k], refs[2 * n + k:2 * n + 2 * k]
        token = refs[-1]
        for cp in copies(srcs, land_refs, send_sems, recv_sems):
            cp.start()
        token[...] = jnp.zeros_like(token)

    hbm_arrs = [pltpu.with_memory_space_constraint(a, pltpu.HBM) for a in arrs]
    lands = [pltpu.with_memory_space_constraint(a, pltpu.HBM) for a in lands]
    res = pl.pallas_call(
        body, name=name,
        out_shape=[pltpu.SemaphoreType.DMA(())] * (2 * k) + [pltpu.HBM(a.shape, a.dtype) for a in arrs + lands]
        + [jax.ShapeDtypeStruct(TOKEN, F32)],
        in_specs=[HBM] * (2 * n),
        out_specs=[SEM] * (2 * k) + [HBM] * (2 * n) + [pl.BlockSpec(memory_space=pltpu.VMEM)],
        input_output_aliases={a: 2 * k + a for a in range(2 * n)},
        compiler_params=pltpu.CompilerParams(has_side_effects=pltpu.SideEffectType.DATAFLOW_SIDE_EFFECTING),
    )(*hbm_arrs, *lands)
    return res[:k], res[k:2 * k], res[2 * k:2 * k + n], res[2 * k + n:2 * k + 2 * n], res[-1]


def _split_wait(copies, per_array, send_sems, recv_sems, srcs, lands, after, name):
    n = len(srcs)
    k = n * per_array

    def body(*refs):
        src_refs, land_refs = refs[:n], refs[n:2 * n]
        s_sems, r_sems = refs[2 * n:2 * n + k], refs[2 * n + k:2 * n + 2 * k]
        for cp in copies(src_refs, land_refs, s_sems, r_sems):
            cp.wait_send()
            cp.wait_recv()

    res = pl.pallas_call(
        body, name=name,
        out_shape=[pltpu.HBM(a.shape, a.dtype) for a in list(srcs) + list(lands)],
        in_specs=[HBM] * (2 * n) + [SEM] * (2 * k) + [ANY],
        out_specs=[HBM] * (2 * n),
        input_output_aliases={a: a for a in range(2 * n)},
        compiler_params=pltpu.CompilerParams(has_side_effects=pltpu.SideEffectType.DATAFLOW_SIDE_EFFECTING),
    )(*srcs, *lands, *send_sems, *recv_sems, after)
    return res[:n], res[n:]


def _chip_exchange_start(arrs, name):
    return _split_start(_chip_copies, N_PEER_CHIPS, arrs, [lax.empty(a.shape, a.dtype) for a in arrs], name)


def _chip_exchange_wait(send_sems, recv_sems, srcs, lands, after, name):
    return _split_wait(_chip_copies, N_PEER_CHIPS, send_sems, recv_sems, srcs, lands, after, name)


def _gather_start(arrs, me_idx, name):
    lands = [lax.dynamic_update_slice(lax.empty((N_DEV,) + a.shape, a.dtype), a[None], (me_idx, 0, 0)) for a in arrs]
    return _split_start(_gather_copies, N_PEERS, arrs, lands, name)


def _gather_wait(send_sems, recv_sems, srcs, lands, after, name):
    return _split_wait(_gather_copies, N_PEERS, send_sems, recv_sems, srcs, lands, after, name)[1]


def _proj_fwd(after, x, g_pre, w_int, tabs):
    s = x.shape[0]
    tm = min(512, s)

    def body(after_ref, x_ref, g_ref, t_ref, w_hbm,
             h_ref, pa_ref, pq_ref, pkv_ref, pbz_ref, pmq_ref, pmz_ref, pg_ref, w_vm, sems):
        _load_once([(w_hbm, w_vm)], sems)
        xf = x_ref[...]
        r = lax.rsqrt(jnp.mean(xf * xf, axis=-1, keepdims=True) + EPS)
        h = ((xf * r) * g_ref[...]).astype(BF16)
        h_ref[...] = h
        cs, s1, s2 = t_ref[0], t_ref[1], t_ref[2]

        def mm(seg, c0, width):
            return _dot_nt(h, w_vm[seg[0] + c0:seg[0] + c0 + width, :])

        for c0 in range(0, SEG_A[1], 512):
            pa_ref[:, c0:c0 + 512] = mm(SEG_A, c0, 512).astype(BF16)
        q = mm(SEG_BQ, 0, 512)
        for b in range(4):
            pq_ref[:, 128 * b:128 * b + 128] = _rope(q[:, 128 * b:128 * b + 128], cs, s1, s2).astype(BF16)
        kv = mm(SEG_BKV, 0, 256)
        pkv_ref[:, 0:128] = _rope(kv[:, 0:128], cs, s1, s2).astype(BF16)
        pkv_ref[:, 128:256] = kv[:, 128:256].astype(BF16)
        pbz_ref[...] = mm(SEG_BZ, 0, 512).astype(BF16)
        pmq_ref[...] = mm(SEG_MQ, 0, 512).astype(BF16)
        pmz_ref[...] = mm(SEG_MZ, 0, 512).astype(BF16)
        for c0 in range(0, SEG_G[1], 512):
            pg_ref[:, c0:c0 + 512] = mm(SEG_G, c0, 512).astype(BF16)

    widths = (D_MODEL, 2048, 512, 256, 512, 512, 512, 3072)
    return pl.pallas_call(
        body, name="proj_fwd", grid=(s // tm,),
        out_shape=[jax.ShapeDtypeStruct((s, w), BF16) for w in widths],
        in_specs=[_full(TOKEN), _rows(tm, D_MODEL), _full((1, D_MODEL)),
                  pl.BlockSpec((3, tm, 128), lambda i: (0, i, 0)), ANY],
        out_specs=[_rows(tm, w) for w in widths],
        scratch_shapes=[pltpu.VMEM((IN_WIDTH, D_MODEL), BF16), pltpu.SemaphoreType.DMA((1,))],
        compiler_params=_params(),
    )(after, x, g_pre, tabs, w_int)


def _mem_kv_fwd(mem, g_mem, w_mkv):
    m = mem.shape[0]

    def body(mem_ref, g_ref, w_ref, mn_ref, mkv_ref):
        xf = mem_ref[...]
        r = lax.rsqrt(jnp.mean(xf * xf, axis=-1, keepdims=True) + EPS)
        mn = ((xf * r) * g_ref[...]).astype(BF16)
        mn_ref[...] = mn
        mkv_ref[...] = _dot(mn, w_ref[...]).astype(BF16)

    return pl.pallas_call(
        body, name="mem_kv_fwd", grid=(1,),
        out_shape=[jax.ShapeDtypeStruct((m, D_MODEL), BF16)] * 2,
        in_specs=[_full((m, D_MODEL)), _full((1, D_MODEL)), _full((D_MODEL, D_MODEL))],
        out_specs=[_full((m, D_MODEL))] * 2,
        compiler_params=_params(),
    )(mem, g_mem, w_mkv)


def _halo_specs(s, tm, rows, width):
    nblk = s // rows
    prev = pl.BlockSpec((rows, width), lambda i: (jnp.maximum(i * (tm // rows) - 1, 0), 0))
    nxt = pl.BlockSpec((rows, width), lambda i: (jnp.minimum((i + 1) * (tm // rows), nblk - 1), 0))
    return prev, nxt


def _conv_common(pa, cu_prev, cu_next, w, tm):
    b, c, u, z = (pa[:, 512 * k:512 * k + 512] for k in range(4))
    cu = c * u
    row = lax.broadcasted_iota(jnp.int32, (tm, 512), 0)
    cu_m1 = jnp.where(row == 0, cu_prev, pltpu.roll(cu, 1, 0))
    cu_p1 = jnp.where(row == tm - 1, cu_next, pltpu.roll(cu, tm - 1, 0))
    y = cu_m1 * w[0:1] + cu * w[1:2] + cu_p1 * w[2:3]
    sig = _sigmoid(z)
    return b, c, u, z, cu, cu_m1, cu_p1, y, sig, row


def _conv_fwd(pa, w_conv):
    s = pa.shape[0]
    tm = min(512, s)
    nt = s // tm

    def body(pa_ref, pp_ref, pn_ref, w_ref, ya_ref):
        i = pl.program_id(0)
        prev_row = pp_ref[...].astype(F32)[15:16, :]
        next_row = pn_ref[...].astype(F32)[0:1, :]
        b, _, _, z, _, _, _, y, sig, _ = _conv_common(
            pa_ref[...].astype(F32),
            jnp.where(i == 0, 0.0, prev_row[:, 512:1024] * prev_row[:, 1024:1536]),
            jnp.where(i == nt - 1, 0.0, next_row[:, 512:1024] * next_row[:, 1024:1536]), w_ref[...], tm)
        ya_ref[...] = (b * y * (z * sig)).astype(BF16)

    prev, nxt = _halo_specs(s, tm, 16, 2048)
    return pl.pallas_call(
        body, name="conv_fwd", grid=(nt,),
        out_shape=jax.ShapeDtypeStruct((s, 512), BF16),
        in_specs=[_rows(tm, 2048), prev, nxt, _full((3, 512))],
        out_specs=_rows(tm, 512),
        compiler_params=_params(),
    )(pa, pa, pa, w_conv)


def _heads_to_lanes(a, g, row):
    low = row < HEAD_DIM
    parts = []
    for b in (2 * g, 2 * g + 1):
        t = jnp.transpose(a[:, 128 * b:128 * b + 128])
        swapped = pltpu.roll(t, HEAD_DIM, 0)
        if g == 0:
            parts += [jnp.where(low, t, 0.0), jnp.where(low, swapped, 0.0)]
        else:
            parts += [jnp.where(low, 0.0, swapped), jnp.where(low, 0.0, t)]
    return jnp.concatenate(parts, axis=1)


def _lanes_to_heads(t0, t1, row):
    low = row < HEAD_DIM
    blocks = []
    for b in range(4):
        g = b // 2
        tg = (t0, t1)[g]
        je = 2 * (b - 2 * g)
        even, odd = tg[:, 128 * je:128 * je + 128], tg[:, 128 * je + 128:128 * je + 256]
        if g == 0:
            t = jnp.where(low, even, pltpu.roll(odd, HEAD_DIM, 0))
        else:
            t = jnp.where(low, pltpu.roll(even, HEAD_DIM, 0), odd)
        blocks.append(jnp.transpose(t))
    return jnp.concatenate(blocks, axis=1)


WINDOW_KEYS = 3 * ATTN_BLOCK
STACKED = 4 * ATTN_BLOCK
KEY_CHUNK = 32
MAX_BLOCKS_IN_STEP = 8


def _fill_band_bias(bias, nb):
    assert nb >= 2
    c = lax.broadcasted_iota(jnp.int32, (WINDOW_KEYS, STACKED), 0)
    r = lax.broadcasted_iota(jnp.int32, (WINDOW_KEYS, STACKED), 1) & (ATTN_BLOCK - 1)
    band = (c >= r) & (c <= r + 2 * ATTN_BLOCK)
    for v, ok in enumerate((band, band & (c >= ATTN_BLOCK), band & (c < 2 * ATTN_BLOCK))):
        bias[v] = jnp.where(ok, 0.0, -jnp.inf)


def _bias_variant(n, nb):
    return jnp.where(n == 0, 1, jnp.where(n == nb - 1, 2, 0))


def _sink_row(sink_ref, g):
    return jnp.concatenate([jnp.full((1, ATTN_BLOCK), sink_ref[4 * g + j], F32) for j in range(4)], axis=1)


def _softmax_keys_major(sc, bias, variant, sink, e_scr):
    chunks = [pl.ds(k * KEY_CHUNK, KEY_CHUNK) for k in range(WINDOW_KEYS // KEY_CHUNK)]
    rows = [slice(k * KEY_CHUNK, (k + 1) * KEY_CHUNK) for k in range(WINDOW_KEYS // KEY_CHUNK)]
    m_run = jnp.full((KEY_CHUNK, STACKED), -jnp.inf, F32)
    for ck, rw in zip(chunks, rows):
        m_run = jnp.maximum(m_run, sc[rw] + bias[variant, ck, :])
    m = jnp.maximum(jnp.max(m_run, axis=0, keepdims=True), sink)
    l_run = jnp.zeros((KEY_CHUNK, STACKED), F32)
    for ck, rw in zip(chunks, rows):
        e = jnp.exp(sc[rw] + bias[variant, ck, :] - m)
        l_run += e
        e_scr[rw, :] = e.astype(BF16)
    es = jnp.exp(sink - m)
    inv = 1.0 / (jnp.sum(l_run, axis=0, keepdims=True) + es)
    return inv, es * inv


def _fill_padded(kv_ref, kpad, vpad, s):
    zero = jnp.zeros((ATTN_BLOCK, 128), BF16)
    kpad[0:ATTN_BLOCK, :] = zero
    vpad[0:ATTN_BLOCK, :] = zero
    kpad[ATTN_BLOCK + s:2 * ATTN_BLOCK + s, :] = zero
    vpad[ATTN_BLOCK + s:2 * ATTN_BLOCK + s, :] = zero
    kpad[ATTN_BLOCK:ATTN_BLOCK + s, :] = kv_ref[:, 0:128]
    vpad[ATTN_BLOCK:ATTN_BLOCK + s, :] = kv_ref[:, 128:256]


def _attn_fwd(pq, pkv, pbz, sink):
    s = pq.shape[0]
    nb = s // ATTN_BLOCK
    bps = min(MAX_BLOCKS_IN_STEP, nb)

    def body(sink_ref, q_ref, z_ref, kv_ref, yb_ref, kpad, vpad, bias, e_scr):
        i = pl.program_id(0)

        @pl.when(i == 0)
        def _():
            _fill_padded(kv_ref, kpad, vpad, s)
            _fill_band_bias(bias, nb)

        row = lax.broadcasted_iota(jnp.int32, (ATTN_BLOCK, 128), 0)
        for b in range(bps):
            n = i * bps + b
            rows = slice(b * ATTN_BLOCK, (b + 1) * ATTN_BLOCK)
            start = pl.multiple_of(n * ATTN_BLOCK, ATTN_BLOCK)
            kw, vw = kpad[pl.ds(start, WINDOW_KEYS), :], vpad[pl.ds(start, WINDOW_KEYS), :]
            qf = q_ref[rows, :].astype(F32)
            variant = _bias_variant(n, nb)
            outs = []
            for g in range(2):
                e_bg = e_scr.at[2 * b + g]
                qt = (_heads_to_lanes(qf, g, row) * ATTN_SCALE).astype(BF16)
                inv, _ = _softmax_keys_major(_dot(kw, qt), bias, variant, _sink_row(sink_ref, g), e_bg)
                outs.append(_dot_tn(vw, e_bg[...]) * inv)
            attn = _lanes_to_heads(outs[0], outs[1], row)
            z = z_ref[rows, :].astype(F32)
            yb_ref[rows, :] = (attn * (z * _sigmoid(z))).astype(BF16)

    tq = bps * ATTN_BLOCK
    return pl.pallas_call(
        body, name="attn_fwd", grid=(s // tq,),
        out_shape=jax.ShapeDtypeStruct((s, 512), BF16),
        in_specs=[pl.BlockSpec(memory_space=pltpu.SMEM), _rows(tq, 512), _rows(tq, 512), _full((s, 256))],
        out_specs=_rows(tq, 512),
        scratch_shapes=[pltpu.VMEM((s + 2 * ATTN_BLOCK, 128), BF16)] * 2
        + [pltpu.VMEM((3, WINDOW_KEYS, STACKED), F32),
           pltpu.VMEM((2 * bps, WINDOW_KEYS, STACKED), BF16)],
        compiler_params=_params(),
    )(sink, pq, pbz, pkv)


def _mem_softmax_t(q, mk):
    sc = _dot_nt(mk, q) * MEM_SCALE
    e = jnp.exp(sc - jnp.max(sc, axis=0, keepdims=True))
    return e * (1.0 / jnp.sum(e, axis=0, keepdims=True))


def _mem_attn_fwd(pmq, pmz, mkv):
    s = pmq.shape[0]
    m = mkv.shape[0]
    tm = min(512, s)

    def body(q_ref, z_ref, mk_ref, mv_ref, ym_ref):
        z = z_ref[...].astype(F32)
        sz = z * _sigmoid(z)
        for h in range(MEM_HEADS):
            cols = slice(128 * h, 128 * h + 128)
            pt = _mem_softmax_t(q_ref[:, cols], mk_ref[:, cols])
            o = _dot_tn(pt.astype(BF16), mv_ref[:, cols])
            ym_ref[:, cols] = (o * sz[:, cols]).astype(BF16)

    return pl.pallas_call(
        body, name="mem_attn_fwd", grid=(s // tm,),
        out_shape=jax.ShapeDtypeStruct((s, 512), BF16),
        in_specs=[_rows(tm, 512), _rows(tm, 512), pl.BlockSpec((m, 512), lambda i: (0, 0)),
                  pl.BlockSpec((m, 512), lambda i: (0, 1))],
        out_specs=_rows(tm, 512),
        compiler_params=_params(),
    )(pmq, pmz, mkv, mkv)


def _mid(ya, yb, ym, pg, x, target, g_post, w_up, w_out):
    s = x.shape[0]
    tm = min(256, s)
    nt = s // tm

    def body(ya_ref, yb_ref, ym_ref, pg_ref, x_ref, t_ref, gp_ref, wup_hbm, wout_hbm,
             dg_ref, dya_ref, dyb_ref, dym_ref, dy_ref, loss_ref, ggp_ref, mb_ref, dob_ref, du_ref,
             wup_vm, wout_vm, sems):
        i = pl.program_id(0)
        _load_once([(wup_hbm.at[d], wup_vm.at[:, pl.ds(128 * d, 128)]) for d in range(N_DEV)]
                   + [(wout_hbm, wout_vm)], sems)

        @pl.when(i == 0)
        def _():
            loss_ref[...] = jnp.zeros_like(loss_ref)
            ggp_ref[...] = jnp.zeros_like(ggp_ref)

        ys = (ya_ref[...], yb_ref[...], ym_ref[...])
        us = [_dot(ys[k], wup_vm[512 * k:512 * k + 512, :]) for k in range(3)]
        gates = [_sigmoid(pg_ref[:, 1024 * k:1024 * k + 1024].astype(F32)) for k in range(3)]
        merged = gates[0] * us[0] + gates[1] * us[1] + gates[2] * us[2]
        mb = merged.astype(BF16)
        mb_ref[...] = mb
        out = _dot(mb, wout_vm[...])
        r = lax.rsqrt(jnp.mean(out * out, axis=-1, keepdims=True) + EPS)
        on = out * r
        gp = gp_ref[...]
        err = (x_ref[...] + on * gp) - t_ref[...]
        loss_ref[...] += 0.5 * jnp.sum(err * err) * (1.0 / D_MODEL)
        dy = err * (1.0 / D_MODEL)
        dy_ref[...] = dy
        ggp_ref[...] += jnp.sum(dy * on, axis=0, keepdims=True)
        a = dy * gp
        d_out = r * (a - on * jnp.mean(a * on, axis=-1, keepdims=True))
        dob = d_out.astype(BF16)
        dob_ref[...] = dob
        d_merged = _dot_nt(dob, wout_vm[...])
        d_refs = (dya_ref, dyb_ref, dym_ref)
        for k in range(3):
            g = gates[k]
            du_f = d_merged * g
            dg_ref[:, 1024 * k:1024 * k + 1024] = (du_f * us[k] * (1.0 - g)).astype(BF16)
            du = du_f.astype(BF16)
            du_ref[k] = du
            d_refs[k][...] = _dot_nt(du, wup_vm[512 * k:512 * k + 512, :]).astype(BF16)

    return pl.pallas_call(
        body, name="mid", grid=(nt,),
        out_shape=[jax.ShapeDtypeStruct((s, 3072), BF16)] + [jax.ShapeDtypeStruct((s, 512), BF16)] * 3
        + [jax.ShapeDtypeStruct((s, D_MODEL), F32), jax.ShapeDtypeStruct((8, 128), F32),
           jax.ShapeDtypeStruct((1, D_MODEL), F32), jax.ShapeDtypeStruct((s, D_MODEL), BF16),
           jax.ShapeDtypeStruct((s, D_MODEL), BF16), jax.ShapeDtypeStruct((3, s, D_MODEL), BF16)],
        in_specs=[_rows(tm, 512)] * 3 + [_rows(tm, 3072), _rows(tm, D_MODEL), _rows(tm, D_MODEL),
                                         _full((1, D_MODEL)), ANY, ANY],
        out_specs=[_rows(tm, 3072)] + [_rows(tm, 512)] * 3
        + [_rows(tm, D_MODEL), _full((8, 128)), _full((1, D_MODEL)), _rows(tm, D_MODEL), _rows(tm, D_MODEL),
           pl.BlockSpec((3, tm, D_MODEL), lambda i: (0, i, 0))],
        scratch_shapes=[pltpu.VMEM((1536, D_MODEL), BF16), pltpu.VMEM((D_MODEL, D_MODEL), BF16),
                        pltpu.SemaphoreType.DMA((N_DEV + 1,))],
        compiler_params=_params(),
    )(ya, yb, ym, pg, x, target, g_post, w_up, w_out)


def _gw_mid(mb, dob, ys, du):
    s = mb.shape[0]
    tn = 256
    n_out = D_MODEL // tn
    tiles = [(0, c0, 0) for c0 in range(0, D_MODEL, tn)]
    tiles += [(1 + k, c0, 1 + k) for k in range(3) for c0 in range(0, 512, tn)]
    n_t = len(tiles)

    def body(mb_hbm, ya_hbm, yb_hbm, ym_hbm, dob_hbm, du_hbm, out_hbm, up_hbm,
             lhs, rhs, res_out, res_up, in_sems, rhs_sems, out_sems):
        lhs_hbm = (mb_hbm, ya_hbm, yb_hbm, ym_hbm)

        def load(t):
            a, c0, _ = tiles[t]
            return pltpu.make_async_copy(lhs_hbm[a].at[:, pl.ds(c0, tn)], lhs.at[t & 1], in_sems.at[t & 1])

        def load_rhs(g):
            src = dob_hbm if g == 0 else du_hbm.at[g - 1]
            return pltpu.make_async_copy(src, rhs.at[g & 1], rhs_sems.at[g & 1])

        def store(t):
            if t < n_out:
                return pltpu.make_async_copy(res_out.at[t & 1], out_hbm.at[pl.ds(t * tn, tn)], out_sems.at[t & 1])
            rows = pl.ds((t - n_out) * tn, tn)
            return pltpu.make_async_copy(res_up.at[t & 1], up_hbm.at[:, rows, :], out_sems.at[t & 1])

        load_rhs(0).start()
        load(0).start()
        for t, (_, _, g) in enumerate(tiles):
            new_rhs = t == 0 or tiles[t - 1][2] != g
            if t + 1 < n_t:
                load(t + 1).start()
            if new_rhs and g < 3:
                load_rhs(g + 1).start()
            load(t).wait()
            if new_rhs:
                load_rhs(g).wait()
            if t >= 2:
                store(t - 2).wait()
            r = _dot_tn(lhs[t & 1], rhs[g & 1])
            if t < n_out:
                res_out[t & 1] = r.astype(BF16)
            else:
                for d in range(N_DEV):
                    res_up[t & 1, d] = r[:, 128 * d:128 * d + 128].astype(BF16)
            store(t).start()
        store(n_t - 2).wait()
        store(n_t - 1).wait()

    return pl.pallas_call(
        body, name="gw_mid",
        out_shape=[jax.ShapeDtypeStruct((D_MODEL, D_MODEL), BF16), jax.ShapeDtypeStruct((N_DEV, 1536, 128), BF16)],
        in_specs=[ANY] * 6, out_specs=[ANY, ANY],
        scratch_shapes=[pltpu.VMEM((2, s, tn), BF16), pltpu.VMEM((2, s, D_MODEL), BF16),
                        pltpu.VMEM((2, tn, D_MODEL), BF16), pltpu.VMEM((2, N_DEV, tn, 128), BF16),
                        pltpu.SemaphoreType.DMA((2,)), pltpu.SemaphoreType.DMA((2,)),
                        pltpu.SemaphoreType.DMA((2,))],
        compiler_params=pltpu.CompilerParams(vmem_limit_bytes=CALL_VMEM_MB * 1024 * 1024),
    )(mb, *ys, dob, du)


def _conv_bwd(after, pa, dya, w_conv):
    s = pa.shape[0]
    tm = min(512, s)
    nt = s // tm

    def body(after_ref, pa_ref, pp_ref, pn_ref, d_ref, dp_ref, dn_ref, w_ref, da_ref, gw_ref):
        i = pl.program_id(0)
        first, last = i == 0, i == nt - 1

        @pl.when(first)
        def _():
            gw_ref[...] = jnp.zeros_like(gw_ref)

        w = w_ref[...]
        prev_row = pp_ref[...].astype(F32)[15:16, :]
        next_row = pn_ref[...].astype(F32)[0:1, :]
        b, c, u, z, cu, cu_m1, cu_p1, y, sig, row = _conv_common(
            pa_ref[...].astype(F32),
            jnp.where(first, 0.0, prev_row[:, 512:1024] * prev_row[:, 1024:1536]),
            jnp.where(last, 0.0, next_row[:, 512:1024] * next_row[:, 1024:1536]), w, tm)
        sz = z * sig
        dya_t = d_ref[...].astype(F32)
        d_y = dya_t * b * sz

        def halo_dy(p_row, d_row):
            zz = p_row[:, 1536:2048]
            return d_row * p_row[:, 0:512] * (zz * _sigmoid(zz))

        dy_prev = jnp.where(first, 0.0, halo_dy(prev_row, dp_ref[...].astype(F32)[15:16, :]))
        dy_next = jnp.where(last, 0.0, halo_dy(next_row, dn_ref[...].astype(F32)[0:1, :]))
        dy_m1 = jnp.where(row == 0, dy_prev, pltpu.roll(d_y, 1, 0))
        dy_p1 = jnp.where(row == tm - 1, dy_next, pltpu.roll(d_y, tm - 1, 0))
        d_cu = dy_p1 * w[0:1] + d_y * w[1:2] + dy_m1 * w[2:3]
        da_ref[:, 0:512] = (dya_t * y * sz).astype(BF16)
        da_ref[:, 512:1024] = (d_cu * u).astype(BF16)
        da_ref[:, 1024:1536] = (d_cu * c).astype(BF16)
        da_ref[:, 1536:2048] = (dya_t * b * y * (sig + sz * (1.0 - sig))).astype(BF16)
        gw_ref[0:1, :] += jnp.sum(d_y * cu_m1, axis=0, keepdims=True)
        gw_ref[1:2, :] += jnp.sum(d_y * cu, axis=0, keepdims=True)
        gw_ref[2:3, :] += jnp.sum(d_y * cu_p1, axis=0, keepdims=True)

    prev, nxt = _halo_specs(s, tm, 16, 2048)
    dprev, dnxt = _halo_specs(s, tm, 16, 512)
    return pl.pallas_call(
        body, name="conv_bwd", grid=(nt,),
        out_shape=[jax.ShapeDtypeStruct((s, 2048), BF16), jax.ShapeDtypeStruct((8, 512), F32)],
        in_specs=[_full(TOKEN), _rows(tm, 2048), prev, nxt, _rows(tm, 512), dprev, dnxt, _full((3, 512))],
        out_specs=[_rows(tm, 2048), _full((8, 512))],
        compiler_params=_params(),
    )(after, pa, pa, pa, dya, dya, dya, w_conv)


def _attn_bwd(after, pq, pkv, pbz, dyb, sink, tabs):
    s = pq.shape[0]
    nb = s // ATTN_BLOCK
    bps = min(MAX_BLOCKS_IN_STEP, nb)

    def body(sink_ref, after_ref, q_ref, z_ref, d_ref, kv_ref, t_ref,
             dq_ref, dz_ref, dkv_ref, gs_ref, kpad, vpad, dk_acc, dv_acc, bias, e_scr, ds_scr):
        i = pl.program_id(0)

        @pl.when(i == 0)
        def _():
            _fill_padded(kv_ref, kpad, vpad, s)
            _fill_band_bias(bias, nb)
            dk_acc[...] = jnp.zeros_like(dk_acc)
            dv_acc[...] = jnp.zeros_like(dv_acc)
            gs_ref[...] = jnp.zeros_like(gs_ref)

        row = lax.broadcasted_iota(jnp.int32, (ATTN_BLOCK, 128), 0)
        for b in range(bps):
            n = i * bps + b
            rows = slice(b * ATTN_BLOCK, (b + 1) * ATTN_BLOCK)
            start = pl.multiple_of(n * ATTN_BLOCK, ATTN_BLOCK)
            kw, vw = kpad[pl.ds(start, WINDOW_KEYS), :], vpad[pl.ds(start, WINDOW_KEYS), :]
            qf = q_ref[rows, :].astype(F32)
            variant = _bias_variant(n, nb)
            z = z_ref[rows, :].astype(F32)
            sig = _sigmoid(z)
            dyb_t = d_ref[rows, :].astype(F32)
            d_attn = dyb_t * (z * sig)
            outs, dqs = [], []
            dk_w = jnp.zeros((WINDOW_KEYS, 128), F32)
            dv_w = jnp.zeros((WINDOW_KEYS, 128), F32)
            for g in range(2):
                e_bg, ds_bg = e_scr.at[2 * b + g], ds_scr.at[2 * b + g]
                qt = _heads_to_lanes(qf, g, row)
                inv, p_sink = _softmax_keys_major(
                    _dot(kw, (qt * ATTN_SCALE).astype(BF16)), bias, variant, _sink_row(sink_ref, g), e_bg)
                ot = _dot_tn(vw, e_bg[...]) * inv
                outs.append(ot)
                dot_ = _heads_to_lanes(d_attn, g, row)
                delta = jnp.sum(dot_ * ot, axis=0, keepdims=True)
                dpt = _dot(vw, dot_.astype(BF16))
                for k in range(WINDOW_KEYS // KEY_CHUNK):
                    rw = slice(k * KEY_CHUNK, (k + 1) * KEY_CHUNK)
                    ds_bg[rw, :] = (e_bg[rw, :].astype(F32) * (dpt[rw] - delta)).astype(BF16)
                sink_part = p_sink * delta
                for j in range(4):
                    h = 4 * g + j
                    gs_ref[h:h + 1, :] -= jnp.sum(sink_part[:, 128 * j:128 * j + 128])
                dqs.append(_dot_tn(kw, ds_bg[...]) * (inv * ATTN_SCALE))
                dk_w += _dot_nt(ds_bg[...], (qt * inv).astype(BF16)) * ATTN_SCALE
                dv_w += _dot_nt(e_bg[...], (dot_ * inv).astype(BF16))
            dk_acc[pl.ds(start, WINDOW_KEYS), :] += dk_w
            dv_acc[pl.ds(start, WINDOW_KEYS), :] += dv_w
            attn = _lanes_to_heads(outs[0], outs[1], row)
            dz_ref[rows, :] = (dyb_t * attn * (sig * (1.0 + z * (1.0 - sig)))).astype(BF16)
            dq = _lanes_to_heads(dqs[0], dqs[1], row)
            trows = pl.ds(start, ATTN_BLOCK)
            cs, s1, s2 = t_ref[0, trows, :], t_ref[1, trows, :], t_ref[2, trows, :]
            for blk in range(4):
                cols = slice(128 * blk, 128 * blk + 128)
                dq_ref[rows, cols] = _rope_t(dq[:, cols], cs, s1, s2).astype(BF16)

        @pl.when(i == nb // bps - 1)
        def _():
            dk = dk_acc[ATTN_BLOCK:ATTN_BLOCK + s, :]
            dkv_ref[:, 0:128] = _rope_t(dk, t_ref[0], t_ref[1], t_ref[2]).astype(BF16)
            dkv_ref[:, 128:256] = dv_acc[ATTN_BLOCK:ATTN_BLOCK + s, :].astype(BF16)

    tq = bps * ATTN_BLOCK
    tile = _rows(tq, 512)
    return pl.pallas_call(
        body, name="attn_bwd", grid=(s // tq,),
        out_shape=[jax.ShapeDtypeStruct((s, 512), BF16), jax.ShapeDtypeStruct((s, 512), BF16),
                   jax.ShapeDtypeStruct((s, 256), BF16), jax.ShapeDtypeStruct((8, 128), F32)],
        in_specs=[pl.BlockSpec(memory_space=pltpu.SMEM), _full(TOKEN), tile, tile, tile, _full((s, 256)),
                  _full((3, s, 128))],
        out_specs=[tile, tile, _full((s, 256)), _full((8, 128))],
        scratch_shapes=[pltpu.VMEM((s + 2 * ATTN_BLOCK, 128), BF16)] * 2
        + [pltpu.VMEM((s + 2 * ATTN_BLOCK, 128), F32)] * 2
        + [pltpu.VMEM((3, WINDOW_KEYS, STACKED), F32)]
        + [pltpu.VMEM((2 * bps, WINDOW_KEYS, STACKED), BF16)] * 2,
        compiler_params=_params(),
    )(sink, after, pq, pbz, dyb, pkv, tabs)


def _mem_attn_bwd(pmq, pmz, mkv, dym):
    s = pmq.shape[0]
    m = mkv.shape[0]
    tm = min(512, s)

    def body(q_ref, z_ref, d_ref, mk_ref, mv_ref, dq_ref, dz_ref, dmkv_ref):
        @pl.when(pl.program_id(0) == 0)
        def _():
            dmkv_ref[...] = jnp.zeros_like(dmkv_ref)

        z = z_ref[...].astype(F32)
        sig = _sigmoid(z)
        dym_t = d_ref[...].astype(F32)
        d_attn = dym_t * (z * sig)
        dsilu = sig * (1.0 + z * (1.0 - sig))
        for h in range(MEM_HEADS):
            cols = slice(128 * h, 128 * h + 128)
            q, mk, mv = q_ref[:, cols], mk_ref[:, cols], mv_ref[:, cols]
            pt = _mem_softmax_t(q, mk)
            pb = pt.astype(BF16)
            o = _dot_tn(pb, mv)
            dob = d_attn[:, cols].astype(BF16)
            dpt = _dot_nt(mv, dob)
            dst = (pt * (dpt - jnp.sum(pt * dpt, axis=0, keepdims=True))).astype(BF16)
            dq_ref[:, cols] = (_dot_tn(dst, mk) * MEM_SCALE).astype(BF16)
            dz_ref[:, cols] = (dym_t[:, cols] * o * dsilu[:, cols]).astype(BF16)
            dmkv_ref[:, cols] += _dot(dst, q) * MEM_SCALE
            dmkv_ref[:, 512 + 128 * h:512 + 128 * h + 128] += _dot(pb, dob)

    return pl.pallas_call(
        body, name="mem_attn_bwd", grid=(s // tm,),
        out_shape=[jax.ShapeDtypeStruct((s, 512), BF16), jax.ShapeDtypeStruct((s, 512), BF16),
                   jax.ShapeDtypeStruct((m, D_MODEL), F32)],
        in_specs=[_rows(tm, 512), _rows(tm, 512), _rows(tm, 512), pl.BlockSpec((m, 512), lambda i: (0, 0)),
                  pl.BlockSpec((m, 512), lambda i: (0, 1))],
        out_specs=[_rows(tm, 512), _rows(tm, 512), _full((m, D_MODEL))],
        compiler_params=_params(),
    )(pmq, pmz, dym, mkv, mkv)


def _mem_kv_bwd(mem, g_mem, mn, dmkv, w_mkv):
    m = mem.shape[0]

    def body(mem_ref, g_ref, mn_ref, d_ref, w_ref, gw_ref, gg_ref):
        db = d_ref[...].astype(BF16)
        gw_ref[...] = _dot_tn(mn_ref[...], db).astype(BF16)
        d_mn = _dot_nt(db, w_ref[...])
        xf = mem_ref[...]
        r = lax.rsqrt(jnp.mean(xf * xf, axis=-1, keepdims=True) + EPS)
        gg_ref[...] = jnp.sum(d_mn * (xf * r), axis=0, keepdims=True)

    return pl.pallas_call(
        body, name="mem_kv_bwd", grid=(1,),
        out_shape=[jax.ShapeDtypeStruct((D_MODEL, D_MODEL), BF16), jax.ShapeDtypeStruct((1, D_MODEL), F32)],
        in_specs=[_full((m, D_MODEL)), _full((1, D_MODEL)), _full((m, D_MODEL)), _full((m, D_MODEL)),
                  _full((D_MODEL, D_MODEL))],
        out_specs=[_full((D_MODEL, D_MODEL)), _full((1, D_MODEL))],
        compiler_params=_params(),
    )(mem, g_mem, mn, dmkv, w_mkv)


def _dh_bwd(after, dparts, x, dy, g_pre, w_int):
    s = x.shape[0]
    tm = min(256, s)

    def body(after_ref, *refs):
        d_refs = refs[:7]
        x_ref, dy_ref, g_ref, w_hbm, gx_ref, gg_ref, w_vm, sems = refs[7:]
        _load_once([(w_hbm, w_vm)], sems)

        @pl.when(pl.program_id(0) == 0)
        def _():
            gg_ref[...] = jnp.zeros_like(gg_ref)

        d_h = jnp.zeros((tm, D_MODEL), F32)
        for d_ref, (r0, width) in zip(d_refs, SEGS):
            for c0 in range(0, width, 512):
                cw = min(512, width - c0)
                d_h += _dot(d_ref[:, c0:c0 + cw], w_vm[r0 + c0:r0 + c0 + cw, :])
        xf = x_ref[...]
        r = lax.rsqrt(jnp.mean(xf * xf, axis=-1, keepdims=True) + EPS)
        xn = xf * r
        a = d_h * g_ref[...]
        gx_ref[...] = r * (a - xn * jnp.mean(a * xn, axis=-1, keepdims=True)) + dy_ref[...]
        gg_ref[...] += jnp.sum(d_h * xn, axis=0, keepdims=True)

    return pl.pallas_call(
        body, name="dh_bwd", grid=(s // tm,),
        out_shape=[jax.ShapeDtypeStruct((s, D_MODEL), F32), jax.ShapeDtypeStruct((1, D_MODEL), F32)],
        in_specs=[_full(TOKEN)] + [_rows(tm, w) for _, w in SEGS]
        + [_rows(tm, D_MODEL), _rows(tm, D_MODEL), _full((1, D_MODEL)), ANY],
        out_specs=[_rows(tm, D_MODEL), _full((1, D_MODEL))],
        scratch_shapes=[pltpu.VMEM((IN_WIDTH, D_MODEL), BF16), pltpu.SemaphoreType.DMA((1,))],
        compiler_params=_params(),
    )(after, *dparts, x, dy, g_pre, w_int)


def _gw_in(dparts, h):
    s = h.shape[0]
    tn = 256
    tiles = [(a, c0) for a, (_, width) in enumerate(SEGS) for c0 in range(0, width, tn)]
    n_t = len(tiles)
    assert n_t * tn == IN_WIDTH

    def body(*refs):
        d_hbm = refs[:7]
        h_hbm, out_hbm, lhs, h_vm, res, in_sems, out_sems, h_sem = refs[7:]

        def load(a, c0, slot):
            return pltpu.make_async_copy(d_hbm[a].at[:, pl.ds(c0, tn)], lhs.at[slot], in_sems.at[slot])

        def store(t, slot):
            rows = pl.ds(pl.multiple_of(t * tn, tn), tn)
            return pltpu.make_async_copy(res.at[slot], out_hbm.at[rows], out_sems.at[slot])

        def start_load(t, slot):
            for k, (a, c0) in enumerate(tiles):
                @pl.when(t == k)
                def _(a=a, c0=c0):
                    load(a, c0, slot).start()

        h_copy = pltpu.make_async_copy(h_hbm, h_vm, h_sem)
        h_copy.start()
        load(*tiles[0], 0).start()
        h_copy.wait()

        def step(t, carry):
            slot = t & 1

            @pl.when(t + 1 < n_t)
            def _():
                start_load(t + 1, 1 - slot)

            load(*tiles[0], slot).wait()

            @pl.when(t >= 2)
            def _():
                store(t - 2, slot).wait()

            res[slot] = _dot_tn(lhs[slot], h_vm[...]).astype(BF16)
            store(t, slot).start()
            return carry

        lax.fori_loop(0, n_t, step, 0)
        store(n_t - 2, (n_t - 2) & 1).wait()
        store(n_t - 1, (n_t - 1) & 1).wait()

    return pl.pallas_call(
        body, name="gw_in",
        out_shape=jax.ShapeDtypeStruct((IN_WIDTH, D_MODEL), BF16),
        in_specs=[ANY] * 8, out_specs=ANY,
        scratch_shapes=[pltpu.VMEM((2, s, tn), BF16), pltpu.VMEM((s, D_MODEL), BF16),
                        pltpu.VMEM((2, tn, D_MODEL), BF16), pltpu.SemaphoreType.DMA((2,)),
                        pltpu.SemaphoreType.DMA((2,)), pltpu.SemaphoreType.DMA(())],
        compiler_params=pltpu.CompilerParams(vmem_limit_bytes=CALL_VMEM_MB * 1024 * 1024),
    )(*dparts, h)


def _adamw_math(w, g, m, v):
    m2 = ADAM_B1 * m + (1.0 - ADAM_B1) * g
    v2 = ADAM_B2 * v + (1.0 - ADAM_B2) * (g * g)
    m_hat = m2 / (1.0 - ADAM_B1 ** ADAM_STEP)
    v_hat = v2 / (1.0 - ADAM_B2 ** ADAM_STEP)
    delta = -ADAM_LR * (m_hat / (jnp.sqrt(v_hat) + ADAM_EPS) + ADAM_WD * w)
    return delta, m2, v2


def _sum_adamw(after, own, land, chip, block, w, m, v, name, tiles=1):
    r, c = w.shape
    rt = r // tiles

    def body(c_ref, after_ref, own_ref, l1_ref, l2_ref, l3_ref, w_ref, m_ref, v_ref, g_ref, d_ref, m2_ref, v2_ref):
        g = own_ref[...].astype(F32)
        for l_ref in (l1_ref, l2_ref, l3_ref):
            g += l_ref[...].astype(F32)
        g_ref[...] = g
        d_ref[...], m2_ref[...], v2_ref[...] = _adamw_math(w_ref[...], g, m_ref[...], v_ref[...])

    def share(k):
        return pl.BlockSpec((None, rt, c), lambda i, c_ref: (jnp.bitwise_xor(c_ref[0], k), block * tiles + i, 0))

    spec = pl.BlockSpec((rt, c), lambda i, c_ref: (i, 0))
    grid_spec = pltpu.PrefetchScalarGridSpec(
        num_scalar_prefetch=1, grid=(tiles,),
        in_specs=[ANY, share(0), share(1), share(2), share(3)] + [spec] * 3, out_specs=[spec] * 4)
    return pl.pallas_call(
        body, name=name, grid_spec=grid_spec,
        out_shape=[jax.ShapeDtypeStruct((r, c), F32)] * 4,
        compiler_params=_params(),
    )(chip, after, own, land, land, land, w, m, v)


def _sum_adamw_group(after, items, chip, name):
    k = len(items)

    def body(c_ref, after_ref, *refs):
        shares, wmv, outs = refs[:4 * k], refs[4 * k:7 * k], refs[7 * k:]
        for j in range(k):
            g = shares[4 * j][...].astype(F32)
            for l_ref in shares[4 * j + 1:4 * j + 4]:
                g += l_ref[...].astype(F32)
            outs[4 * j][...] = g
            outs[4 * j + 1][...], outs[4 * j + 2][...], outs[4 * j + 3][...] = _adamw_math(
                wmv[3 * j][...], g, wmv[3 * j + 1][...], wmv[3 * j + 2][...])

    def share(shape, block, q):
        return pl.BlockSpec((None,) + shape, lambda i, c_ref: (jnp.bitwise_xor(c_ref[0], q), block, 0))

    in_specs, args = [ANY], [after]
    for own, land, block, w, m, v in items:
        in_specs += [share(w.shape, block, q) for q in range(4)]
        args += [own, land, land, land]
    for own, land, block, w, m, v in items:
        in_specs += [pl.BlockSpec(w.shape, lambda i, c_ref: (0, 0))] * 3
        args += [w, m, v]
    out_specs = [pl.BlockSpec(w.shape, lambda i, c_ref: (0, 0)) for _, _, _, w, _, _ in items for _ in range(4)]
    res = pl.pallas_call(
        body, name=name,
        grid_spec=pltpu.PrefetchScalarGridSpec(num_scalar_prefetch=1, grid=(1,), in_specs=in_specs,
                                               out_specs=out_specs),
        out_shape=[jax.ShapeDtypeStruct(w.shape, F32) for _, _, _, w, _, _ in items for _ in range(4)],
        compiler_params=_params(),
    )(chip, *args)
    return [res[4 * j:4 * j + 4] for j in range(k)]


def _small_pack(parts):
    def pack_body(gpre_ref, gconv_ref, gsink_ref, gmem_ref, gpost_ref, loss_ref, pack):
        lane = lax.broadcasted_iota(jnp.int32, (1, 128), 1)
        sink_row = jnp.zeros((1, 128), F32)
        for h in range(8):
            sink_row = jnp.where(lane == h, gsink_ref[h:h + 1, :], sink_row)
        pack[...] = jnp.zeros_like(pack)
        pack[0:1, :] = gpre_ref[...]
        pack[1:2, :] = gmem_ref[...]
        pack[2:3, :] = gpost_ref[...]
        pack[3:6, 0:512] = gconv_ref[0:3, :]
        pack[6:7, 0:128] = sink_row
        pack[7:8, 0:128] = loss_ref[0:1, :]

    return pl.pallas_call(
        pack_body, name="small_pack", grid=(1,),
        out_shape=jax.ShapeDtypeStruct((8, D_MODEL), F32),
        in_specs=[_full(p.shape) for p in parts], out_specs=_full((8, D_MODEL)),
    )(*parts)


def _small_apply(packs, ws, ms, vs):
    def apply(p_ref, *refs):
        w_refs, m_refs, v_refs = refs[0:5], refs[5:10], refs[10:15]
        loss_out = refs[15]
        g_outs, d_outs, m_outs, v_outs = refs[16:21], refs[21:26], refs[26:31], refs[31:36]
        x, y, c = _my_place()
        tot = p_ref[0]
        for d in range(1, N_DEV):
            tot = tot + p_ref[d]
        conv = pltpu.roll(tot[:, 0:512], (512 - 64 * (4 * x + 2 * y + c)) % 512, 1)[3:6, 0:64]
        grads = (tot[0:1, :], conv, tot[6:7, 0:8], tot[1:2, :], tot[2:3, :])
        loss_out[...] = tot[7:8, 0:128]
        for j in range(5):
            if len(w_refs[j].shape) == 3:
                for k in range(w_refs[j].shape[0]):
                    g_outs[j][k] = grads[j][k:k + 1]
                    d_outs[j][k], m_outs[j][k], v_outs[j][k] = _adamw_math(
                        w_refs[j][k], grads[j][k:k + 1], m_refs[j][k], v_refs[j][k])
                continue
            g_outs[j][...] = grads[j]
            d_outs[j][...], m_outs[j][...], v_outs[j][...] = _adamw_math(
                w_refs[j][...], grads[j], m_refs[j][...], v_refs[j][...])

    specs = [_full(w.shape) for w in ws]
    res = pl.pallas_call(
        apply, name="small_apply", grid=(1,),
        out_shape=[jax.ShapeDtypeStruct((1, 128), F32)] + [jax.ShapeDtypeStruct(w.shape, F32) for w in ws] * 4,
        in_specs=[_full((N_DEV, 8, D_MODEL))] + specs * 3,
        out_specs=[_full((1, 128))] + specs * 4,
    )(packs, *ws, *ms, *vs)
    return res[0], res[1:6], res[6:11], res[11:16], res[16:21]


def kernel(x, mem, g_pre, w_in, w_conv, attn_sink, g_mem, w_mem_kv, w_up_a, w_up_b, w_up_m, w_out, g_post, loss_target, m_g_pre, m_w_in, m_w_conv, m_attn_sink, m_g_mem, m_w_mem_kv, m_w_up_a, m_w_up_b, m_w_up_m, m_w_out, m_g_post, v_g_pre, v_w_in, v_w_conv, v_attn_sink, v_g_mem, v_w_mem_kv, v_w_up_a, v_w_up_b, v_w_up_m, v_w_out, v_g_post):
    s = x.shape[1]
    x2, mem2, tgt2 = x[0], mem[0], loss_target[0]
    me = 4 * lax.axis_index("x") + 2 * lax.axis_index("y") + lax.axis_index("c")

    w_conv_loc = jnp.zeros((8, 128), F32).at[:3, :64].set(w_conv[0])
    w_int_g, w_conv_g, tabs, w_mkv_loc, w_out_loc, w_up_loc = _all_gather(
        [w_in[0].T.astype(BF16), w_conv_loc], "gather_w_in",
        splits=[[(112 * k, 112) for k in range(7)] + [(784, 144)], [(0, 8)]],
        side=_gather_side(s, w_mem_kv[0], w_out[0], (w_up_a[0], w_up_b[0], w_up_m[0])))
    w_int = w_int_g.reshape(IN_WIDTH, D_MODEL)
    w_conv_f = w_conv_g[:, :3, :64].transpose(1, 0, 2).reshape(3, 512)
    late = _gather_start([w_mkv_loc, w_out_loc, w_up_loc], me, "gather_late_start")
    sink = attn_sink[0]

    h, pa, pq, pkv, pbz, pmq, pmz, pg = _proj_fwd(late[4], x2, g_pre, w_int, tabs)
    ya = _conv_fwd(pa, w_conv_f)
    yb = _attn_fwd(pq, pkv, pbz, sink)
    w_mkv_g, w_out_g, w_up_g = _gather_wait(*late[:4], yb, "gather_late_wait")
    w_mkv = w_mkv_g.reshape(D_MODEL, D_MODEL)
    w_out_f = w_out_g.reshape(D_MODEL, D_MODEL)
    mn, mkv = _mem_kv_fwd(mem2, g_mem, w_mkv)
    ym = _mem_attn_fwd(pmq, pmz, mkv)
    dg, dya, dyb, dym, dy, loss_p, gg_post, mb, dob, du = _mid(ya, yb, ym, pg, x2, tgt2, g_post, w_up_g, w_out_f)
    gw_out, gw_up = _gw_mid(mb, dob, (ya, yb, ym), du)

    core = lax.axis_index("c").astype(jnp.int32).reshape(1)
    chip = (2 * lax.axis_index("x") + lax.axis_index("y")).astype(jnp.int32).reshape(1)

    dmq, dmz, dmkv = _mem_attn_bwd(pmq, pmz, mkv, dym)
    gw_mkv, gg_mem = _mem_kv_bwd(mem2, g_mem, mn, dmkv, w_mkv)
    shares1 = [gw_mkv.reshape(N_DEV, 128, D_MODEL), gw_out.reshape(N_DEV, 128, D_MODEL), gw_up]
    sib = _split_start(_sibling_copies, N_CHIPS, shares1,
                       [lax.empty((N_CHIPS,) + a.shape[1:], a.dtype) for a in shares1], "grads_to_sibling_small_start")
    da, gw_conv = _conv_bwd(sib[4], pa, dya, w_conv_f)
    shares1, from_sibling = _split_wait(_sibling_copies, N_CHIPS, *sib[:4], da, "grads_to_sibling_small_wait")
    send1, recv1, srcs1, lands1, token1 = _chip_exchange_start(
        _pair_add(shares1, from_sibling, core, "grads_pair_add_small"), "grads_to_chips_start_small")
    dq, dbz, dkv, g_sink = _attn_bwd(token1, pq, pkv, pbz, dyb, sink, tabs)
    dparts = (da, dq, dkv, dbz, dmq, dmz, dg)
    gw_int = _gw_in(dparts, h)
    send2, recv2, srcs2, lands2, token2 = _chip_exchange_start(
        [_sibling_reduce(gw_int.reshape(N_DEV, SHARD_IN, D_MODEL), "grads_sibling_reduce_w_in")],
        "grads_to_chips_start_w_in")
    grad_x, gg_pre = _dh_bwd(token2, dparts, x2, dy, g_pre, w_int)
    (o_mkv, o_out, o_up, o_int), (l_mkv, l_out, l_up, l_int) = _chip_exchange_wait(
        send1 + send2, recv1 + recv2, srcs1 + srcs2, lands1 + lands2, grad_x, "grads_to_chips_wait")

    small = _gather_start([_small_pack((gg_pre, gw_conv, g_sink, gg_mem, gg_post, loss_p))], me,
                          "small_gather_start")

    w_in_t = _sum_adamw(small[4], o_int, l_int, chip, 0, w_in[0].T, m_w_in[0].T, v_w_in[0].T, "adamw_w_in", tiles=2)
    g_w_in, d_w_in, nm_w_in, nv_w_in = (t.T for t in w_in_t)
    (g_mkv, d_mkv, nm_mkv, nv_mkv), (g_out, d_out, nm_out, nv_out), *up = _sum_adamw_group(
        w_in_t[0],
        [(o_mkv, l_mkv, 0, w_mem_kv[0], m_w_mem_kv[0], v_w_mem_kv[0]),
         (o_out, l_out, 0, w_out[0], m_w_out[0], v_w_out[0]),
         (o_up, l_up, 0, w_up_a[0], m_w_up_a[0], v_w_up_a[0]),
         (o_up, l_up, 1, w_up_b[0], m_w_up_b[0], v_w_up_b[0]),
         (o_up, l_up, 2, w_up_m[0], m_w_up_m[0], v_w_up_m[0])], chip, "adamw_mid_weights")

    (packs,) = _gather_wait(*small[:4], g_out, "small_gather_wait")

    def taps_first(a):
        return a.transpose(1, 0, 2)

    loss_row, small_g, sd, sm, sv = _small_apply(
        packs, [g_pre, taps_first(w_conv), attn_sink, g_mem, g_post],
        [m_g_pre, taps_first(m_w_conv), m_attn_sink, m_g_mem, m_g_post],
        [v_g_pre, taps_first(v_w_conv), v_attn_sink, v_g_mem, v_g_post])
    loss = loss_row[0, 0]
    g_g_pre, g_conv, g_sink_tot, g_g_mem, g_g_post = small_g

    def lead(a):
        return a[None]

    grads = [g_g_pre, lead(g_w_in), taps_first(g_conv), g_sink_tot, g_g_mem, lead(g_mkv), lead(up[0][0]),
             lead(up[1][0]), lead(up[2][0]), lead(g_out), g_g_post]

    def assemble(small, big_in, big_mkv, big_up, big_out):
        return [small[0], lead(big_in), taps_first(small[1]), small[2], small[3], lead(big_mkv), lead(big_up[0]),
                lead(big_up[1]), lead(big_up[2]), lead(big_out), small[4]]

    deltas = assemble(sd, d_w_in, d_mkv, [u[1] for u in up], d_out)
    new_m = assemble(sm, nm_w_in, nm_mkv, [u[2] for u in up], nm_out)
    new_v = assemble(sv, nv_w_in, nv_mkv, [u[3] for u in up], nv_out)
    return (loss, grad_x[None], *grads, *deltas, *new_m, *new_v)
```

```python
import functools

import jax
import jax.numpy as jnp
from jax import lax
from jax.experimental import pallas as pl
from jax.experimental.pallas import tpu as pltpu

F32 = jnp.float32
BF16 = jnp.bfloat16
MESH = pl.DeviceIdType.MESH

N_DEV = 8
D_MODEL = 1024
EPS = 1e-6
ROPE_THETA = 500000.0
ROT_DIM = 16
HEAD_DIM = 64
ATTN_BLOCK = 128
MEM_HEADS = 4
MEM_HEAD_DIM = 128
ATTN_SCALE = HEAD_DIM ** -0.5
MEM_SCALE = MEM_HEAD_DIM ** -0.5

ADAM_LR = 0.001
ADAM_B1 = 0.9
ADAM_B2 = 0.999
ADAM_EPS = 1e-08
ADAM_WD = 0.01
ADAM_STEP = 10

SEG_A = (0, 2048)
SEG_BQ = (2048, 512)
SEG_BKV = (2560, 256)
SEG_BZ = (2816, 512)
SEG_MQ = (3328, 512)
SEG_MZ = (3840, 512)
SEG_G = (4352, 3072)
SEGS = (SEG_A, SEG_BQ, SEG_BKV, SEG_BZ, SEG_MQ, SEG_MZ, SEG_G)
IN_WIDTH = 7424
SHARD_IN = IN_WIDTH // N_DEV

V7X_VMEM_BYTES = 64 * 1024 * 1024
CALL_VMEM_MB = 57
ANY = pl.BlockSpec(memory_space=pl.ANY)


def _params():
    assert CALL_VMEM_MB * 1024 * 1024 < V7X_VMEM_BYTES
    return pltpu.CompilerParams(dimension_semantics=("arbitrary",), vmem_limit_bytes=CALL_VMEM_MB * 1024 * 1024)


def _full(shape):
    zeros = (0,) * len(shape)
    return pl.BlockSpec(shape, lambda i: zeros)


def _rows(tm, width):
    return pl.BlockSpec((tm, width), lambda i: (i, 0))


def _dot(a, b):
    return jnp.dot(a, b, preferred_element_type=F32)


def _dot_nt(a, b):
    return lax.dot_general(a, b, (((1,), (1,)), ((), ())), preferred_element_type=F32)


def _dot_tn(a, b):
    return lax.dot_general(a, b, (((0,), (0,)), ((), ())), preferred_element_type=F32)


def _sigmoid(z):
    return 1.0 / (1.0 + jnp.exp(-z))


def _rope(t, cs, s1, s2):
    return t * cs + pltpu.roll(t, 120, 1) * s1 + pltpu.roll(t, 8, 1) * s2


def _rope_t(d, cs, s1, s2):
    return d * cs + pltpu.roll(d * s1, 8, 1) + pltpu.roll(d * s2, 120, 1)


def _gather_side(s, w_mkv, w_out, w_ups):
    half = ROT_DIM // 2
    inv_freq = jnp.power(jnp.float32(ROPE_THETA), -jnp.arange(half, dtype=F32) * (2.0 / ROT_DIM))
    freq_row = jnp.tile(jnp.concatenate([inv_freq, inv_freq, jnp.zeros((HEAD_DIM - ROT_DIM,), F32)]), 2)[None, :]

    def fn(in_refs, out_refs):
        f_ref, mkv_ref, out_ref, *up_refs = in_refs
        t_ref, mkv_bf, out_bf, up_bf = out_refs
        mkv_bf[...] = mkv_ref[...].astype(BF16)
        out_bf[...] = out_ref[...].astype(BF16)
        for k, up_ref in enumerate(up_refs):
            up_bf[512 * k:512 * k + 512, :] = up_ref[...].astype(BF16)
        pos = lax.broadcasted_iota(jnp.int32, (s, 128), 0).astype(F32)
        d = lax.broadcasted_iota(jnp.int32, (s, 128), 1) & (HEAD_DIM - 1)
        ang = pos * f_ref[...]
        cos, sin = jnp.cos(ang), jnp.sin(ang)
        lo, hi = d < half, (d >= half) & (d < ROT_DIM)
        t_ref[0] = jnp.where(lo | hi, cos, 1.0)
        t_ref[1] = jnp.where(lo, -sin, 0.0)
        t_ref[2] = jnp.where(hi, sin, 0.0)

    return ([freq_row, w_mkv, w_out, *w_ups],
            [jax.ShapeDtypeStruct((3, s, 128), F32), jax.ShapeDtypeStruct(w_mkv.shape, BF16),
             jax.ShapeDtypeStruct(w_out.shape, BF16), jax.ShapeDtypeStruct((1536, 128), BF16)], fn)


def _load_once(pairs, sems):
    @pl.when(pl.program_id(0) == 0)
    def _():
        cps = [pltpu.make_async_copy(src, dst, sems.at[k]) for k, (src, dst) in enumerate(pairs)]
        for cp in cps:
            cp.start()
        for cp in cps:
            cp.wait()


def _my_place():
    x, y, c = lax.axis_index("x"), lax.axis_index("y"), lax.axis_index("c")
    return x, y, c


def _all_gather(arrs, name, splits=None, side=None):
    n = len(arrs)
    if splits is None:
        splits = [[(0, a.shape[0])] for a in arrs]
    pieces = [(a, r0, rn) for a in range(n) for r0, rn in splits[a]]
    n_p = len(pieces)
    side_in, side_out, side_fn = side if side is not None else ((), (), None)
    m, q = len(side_in), len(side_out)

    def body(*refs):
        ins, outs = refs[:n], refs[n + m:2 * n + m]
        send_sems, recv_sems, local_sems = refs[2 * n + m + q:]
        x, y, c = _my_place()
        me, sibling = (x, y, c), (x, y, 1 - c)

        def route(core):
            first = (jnp.bitwise_xor(x, 1 - core), jnp.bitwise_xor(y, core), core)
            second = (jnp.bitwise_xor(x, core), jnp.bitwise_xor(y, 1 - core), core)
            return first, second, (1 - x, 1 - y, core)

        def idx(px, py, pc):
            return 4 * px + 2 * py + pc

        def copy(p, k, block, to, own=False):
            a, r0, rn = pieces[p]
            dst = outs[a].at[idx(*block), pl.ds(r0, rn)]
            return pltpu.make_async_remote_copy(
                src_ref=ins[a].at[pl.ds(r0, rn)] if own else dst, dst_ref=dst,
                send_sem=send_sems.at[p * 7 + k], recv_sem=recv_sems.at[p * 7 + k],
                device_id=to, device_id_type=MESH)

        nbr1, nbr2, diag = route(c)
        mine = [pltpu.make_async_copy(ins[a], outs[a].at[idx(*me)], local_sems.at[a]) for a in range(n)]
        for cp in mine:
            cp.start()
        sent = []
        for p in range(n_p):
            for k, to in enumerate((sibling, nbr1, nbr2)):
                sent.append(copy(p, k, me, to, own=True))
        for cp in sent:
            cp.start()
        if side_fn is not None:
            side_fn(refs[n:n + m], refs[2 * n + m:2 * n + m + q])
        for k_in, block, onward in ((1, nbr1, ((3, nbr2), (4, sibling))), (2, nbr2, ((5, sibling),)),
                                    (3, diag, ((6, sibling),))):
            for p in range(n_p):
                copy(p, k_in, block, me).wait_recv()
                for k_out, to in onward:
                    cp = copy(p, k_out, block, to)
                    cp.start()
                    sent.append(cp)
        s1, s2, sd = route(1 - c)
        for k_in, block in ((0, sibling), (4, s1), (5, s2), (6, sd)):
            for p in range(n_p):
                copy(p, k_in, block, me).wait_recv()
        for cp in sent:
            cp.wait_send()
        for cp in mine:
            cp.wait()

    return pl.pallas_call(
        body, name=name,
        out_shape=[jax.ShapeDtypeStruct((N_DEV,) + a.shape, a.dtype) for a in arrs] + list(side_out),
        in_specs=[ANY] * n + [pl.BlockSpec(memory_space=pltpu.VMEM)] * m,
        out_specs=[ANY] * n + [pl.BlockSpec(memory_space=pltpu.VMEM)] * q,
        scratch_shapes=[pltpu.SemaphoreType.DMA((7 * n_p,)), pltpu.SemaphoreType.DMA((7 * n_p,)),
                        pltpu.SemaphoreType.DMA((n,))],
        compiler_params=pltpu.CompilerParams(vmem_limit_bytes=32 * 1024 * 1024),
    )(*arrs, *side_in)


N_CHIPS = 4


def _sibling_reduce(arr, name):
    _, r, c = arr.shape

    def body(in_ref, out_ref, land, a_buf, b_buf, o_buf, send_sems, recv_sems, local_sems):
        x, y, core = _my_place()
        cps = [pltpu.make_async_remote_copy(
            src_ref=in_ref.at[2 * j + (1 - core)], dst_ref=land.at[j], send_sem=send_sems.at[j],
            recv_sem=recv_sems.at[j], device_id=(x, y, 1 - core), device_id_type=MESH) for j in range(N_CHIPS)]
        for cp in cps:
            cp.start()
        store = None
        for j in range(N_CHIPS):
            mine = pltpu.make_async_copy(in_ref.at[2 * j + core], a_buf, local_sems.at[0])
            mine.start()
            cps[j].wait_recv()
            theirs = pltpu.make_async_copy(land.at[j], b_buf, local_sems.at[1])
            theirs.start()
            mine.wait()
            theirs.wait()
            if store is not None:
                store.wait()
            o_buf[...] = (a_buf[...].astype(F32) + b_buf[...].astype(F32)).astype(BF16)
            store = pltpu.make_async_copy(o_buf, out_ref.at[j], local_sems.at[2])
            store.start()
        store.wait()
        for cp in cps:
            cp.wait_send()

    return pl.pallas_call(
        body, name=name,
        out_shape=[jax.ShapeDtypeStruct((N_CHIPS, r, c), BF16)] * 2,
        in_specs=[ANY], out_specs=[ANY, ANY],
        scratch_shapes=[pltpu.VMEM((r, c), BF16)] * 3
        + [pltpu.SemaphoreType.DMA((N_CHIPS,)), pltpu.SemaphoreType.DMA((N_CHIPS,)), pltpu.SemaphoreType.DMA((3,))],
        compiler_params=pltpu.CompilerParams(vmem_limit_bytes=32 * 1024 * 1024),
    )(arr)[0]


def _sibling_copies(srcs, lands, send_sems, recv_sems):
    x, y, c = _my_place()
    cps = []
    for j in range(N_CHIPS):
        for a in range(len(srcs)):
            k = a * N_CHIPS + j
            cps.append(pltpu.make_async_remote_copy(
                src_ref=srcs[a].at[2 * j + (1 - c)], dst_ref=lands[a].at[j], send_sem=send_sems[k],
                recv_sem=recv_sems[k], device_id=(x, y, 1 - c), device_id_type=MESH))
    return cps


def _pair_add(mine, recv, core, name):
    n = len(mine)

    def body(c_ref, *refs):
        for a in range(n):
            refs[2 * n + a][...] = (refs[a][...].astype(F32) + refs[n + a][...].astype(F32)).astype(BF16)

    def blk(a):
        return (None,) + a.shape[1:]

    grid_spec = pltpu.PrefetchScalarGridSpec(
        num_scalar_prefetch=1, grid=(N_CHIPS,),
        in_specs=[pl.BlockSpec(blk(a), lambda j, c_ref: (2 * j + c_ref[0], 0, 0)) for a in mine]
        + [pl.BlockSpec(blk(a), lambda j, c_ref: (j, 0, 0)) for a in recv],
        out_specs=[pl.BlockSpec(blk(a), lambda j, c_ref: (j, 0, 0)) for a in recv])
    return pl.pallas_call(
        body, name=name, grid_spec=grid_spec,
        out_shape=[jax.ShapeDtypeStruct(a.shape, BF16) for a in recv],
        compiler_params=_params(),
    )(core, *mine, *recv)


HBM = pl.BlockSpec(memory_space=pltpu.HBM)
SEM = pl.BlockSpec(memory_space=pltpu.SEMAPHORE)
N_PEER_CHIPS = 3
TOKEN = (8, 128)


def _chip_copies(srcs, lands, send_sems, recv_sems):
    x, y, c = _my_place()
    my_chip = 2 * x + y
    peers = [(x, 1 - y), (1 - x, y), (1 - x, 1 - y)]
    cps = []
    for k, (px, py) in enumerate(peers):
        for a in range(len(srcs)):
            j = a * N_PEER_CHIPS + k
            cps.append(pltpu.make_async_remote_copy(
                src_ref=srcs[a].at[2 * px + py], dst_ref=lands[a].at[my_chip],
                send_sem=send_sems[j], recv_sem=recv_sems[j],
                device_id=(px, py, c), device_id_type=MESH))
    return cps


N_PEERS = N_DEV - 1


def _gather_copies(srcs, lands, send_sems, recv_sems):
    x, y, c = _my_place()
    me_idx = 4 * x + 2 * y + c
    flips = [(0, 0, 1), (0, 1, 0), (1, 0, 0), (0, 1, 1), (1, 0, 1), (1, 1, 0), (1, 1, 1)]
    cps = []
    for k, (fx, fy, fc) in enumerate(flips):
        peer = ((1 - x) if fx else x, (1 - y) if fy else y, (1 - c) if fc else c)
        for a in range(len(srcs)):
            j = a * N_PEERS + k
            cps.append(pltpu.make_async_remote_copy(
                src_ref=srcs[a], dst_ref=lands[a].at[me_idx], send_sem=send_sems[j], recv_sem=recv_sems[j],
                device_id=peer, device_id_type=MESH))
    return cps


def _split_start(copies, per_array, arrs, lands, name):
    arrs, lands = list(arrs), list(lands)
    n = len(arrs)
    k = n * per_array

    def body(*refs):
        srcs, land_refs = refs[:n], refs[n:2 * n]
        send_sems, recv_sems = refs[2 * n:2 * n + k], refs[2 * n + k:2 * n + 2 * k]
        token = refs[-1]
        for cp in copies(srcs, land_refs, send_sems, recv_sems):
            cp.start()
        token[...] = jnp.zeros_like(token)

    hbm_arrs = [pltpu.with_memory_space_constraint(a, pltpu.HBM) for a in arrs]
    lands = [pltpu.with_memory_space_constraint(a, pltpu.HBM) for a in lands]
    res = pl.pallas_call(
        body, name=name,
        out_shape=[pltpu.SemaphoreType.DMA(())] * (2 * k) + [pltpu.HBM(a.shape, a.dtype) for a in arrs + lands]
        + [jax.ShapeDtypeStruct(TOKEN, F32)],
        in_specs=[HBM] * (2 * n),
        out_specs=[SEM] * (2 * k) + [HBM] * (2 * n) + [pl.BlockSpec(memory_space=pltpu.VMEM)],
        input_output_aliases={a: 2 * k + a for a in range(2 * n)},
        compiler_params=pltpu.CompilerParams(has_side_effects=pltpu.SideEffectType.DATAFLOW_SIDE_EFFECTING),
    )(*hbm_arrs, *lands)
    return res[:k], res[k:2 * k], res[2 * k:2 * k + n], res[2 * k + n:2 * k + 2 * n], res[-1]


def _split_wait(copies, per_array, send_sems, recv_sems, srcs, lands, after, name):
    n = len(srcs)
    k = n * per_array

    def body(*refs):
        src_refs, land_refs = refs[:n], refs[n:2 * n]
        s_sems, r_sems = refs[2 * n:2 * n + k], refs[2 * n + k:2 * n + 2 * k]
        for cp in copies(src_refs, land_refs, s_sems, r_sems):
            cp.wait_send()
            cp.wait_recv()

    res = pl.pallas_call(
        body, name=name,
        out_shape=[pltpu.HBM(a.shape, a.dtype) for a in list(srcs) + list(lands)],
        in_specs=[HBM] * (2 * n) + [SEM] * (2 * k) + [ANY],
        out_specs=[HBM] * (2 * n),
        input_output_aliases={a: a for a in range(2 * n)},
        compiler_params=pltpu.CompilerParams(has_side_effects=pltpu.SideEffectType.DATAFLOW_SIDE_EFFECTING),
    )(*srcs, *lands, *send_sems, *recv_sems, after)
    return res[:n], res[n:]


def _chip_exchange_start(arrs, name):
    return _split_start(_chip_copies, N_PEER_CHIPS, arrs, [lax.empty(a.shape, a.dtype) for a in arrs], name)


def _chip_exchange_wait(send_sems, recv_sems, srcs, lands, after, name):
    return _split_wait(_chip_copies, N_PEER_CHIPS, send_sems, recv_sems, srcs, lands, after, name)


def _gather_start(arrs, me_idx, name):
    lands = [lax.dynamic_update_slice(lax.empty((N_DEV,) + a.shape, a.dtype), a[None], (me_idx, 0, 0)) for a in arrs]
    return _split_start(_gather_copies, N_PEERS, arrs, lands, name)


def _gather_wait(send_sems, recv_sems, srcs, lands, after, name):
    return _split_wait(_gather_copies, N_PEERS, send_sems, recv_sems, srcs, lands, after, name)[1]


def _proj_fwd(after, x, g_pre, w_int, tabs):
    s = x.shape[0]
    tm = min(512, s)

    def body(after_ref, x_ref, g_ref, t_ref, w_hbm,
             h_ref, pa_ref, pq_ref, pkv_ref, pbz_ref, pmq_ref, pmz_ref, pg_ref, w_vm, sems):
        _load_once([(w_hbm, w_vm)], sems)
        xf = x_ref[...]
        r = lax.rsqrt(jnp.mean(xf * xf, axis=-1, keepdims=True) + EPS)
        h = ((xf * r) * g_ref[...]).astype(BF16)
        h_ref[...] = h
        cs, s1, s2 = t_ref[0], t_ref[1], t_ref[2]

        def mm(seg, c0, width):
            return _dot_nt(h, w_vm[seg[0] + c0:seg[0] + c0 + width, :])

        for c0 in range(0, SEG_A[1], 512):
            pa_ref[:, c0:c0 + 512] = mm(SEG_A, c0, 512).astype(BF16)
        q = mm(SEG_BQ, 0, 512)
        for b in range(4):
            pq_ref[:, 128 * b:128 * b + 128] = _rope(q[:, 128 * b:128 * b + 128], cs, s1, s2).astype(BF16)
        kv = mm(SEG_BKV, 0, 256)
        pkv_ref[:, 0:128] = _rope(kv[:, 0:128], cs, s1, s2).astype(BF16)
        pkv_ref[:, 128:256] = kv[:, 128:256].astype(BF16)
        pbz_ref[...] = mm(SEG_BZ, 0, 512).astype(BF16)
        pmq_ref[...] = mm(SEG_MQ, 0, 512).astype(BF16)
        pmz_ref[...] = mm(SEG_MZ, 0, 512).astype(BF16)
        for c0 in range(0, SEG_G[1], 512):
            pg_ref[:, c0:c0 + 512] = mm(SEG_G, c0, 512).astype(BF16)

    widths = (D_MODEL, 2048, 512, 256, 512, 512, 512, 3072)
    return pl.pallas_call(
        body, name="proj_fwd", grid=(s // tm,),
        out_shape=[jax.ShapeDtypeStruct((s, w), BF16) for w in widths],
        in_specs=[_full(TOKEN), _rows(tm, D_MODEL), _full((1, D_MODEL)),
                  pl.BlockSpec((3, tm, 128), lambda i: (0, i, 0)), ANY],
        out_specs=[_rows(tm, w) for w in widths],
        scratch_shapes=[pltpu.VMEM((IN_WIDTH, D_MODEL), BF16), pltpu.SemaphoreType.DMA((1,))],
        compiler_params=_params(),
    )(after, x, g_pre, tabs, w_int)


def _mem_kv_fwd(mem, g_mem, w_mkv):
    m = mem.shape[0]

    def body(mem_ref, g_ref, w_ref, mn_ref, mkv_ref):
        xf = mem_ref[...]
        r = lax.rsqrt(jnp.mean(xf * xf, axis=-1, keepdims=True) + EPS)
        mn = ((xf * r) * g_ref[...]).astype(BF16)
        mn_ref[...] = mn
        mkv_ref[...] = _dot(mn, w_ref[...]).astype(BF16)

    return pl.pallas_call(
        body, name="mem_kv_fwd", grid=(1,),
        out_shape=[jax.ShapeDtypeStruct((m, D_MODEL), BF16)] * 2,
        in_specs=[_full((m, D_MODEL)), _full((1, D_MODEL)), _full((D_MODEL, D_MODEL))],
        out_specs=[_full((m, D_MODEL))] * 2,
        compiler_params=_params(),
    )(mem, g_mem, w_mkv)


def _halo_specs(s, tm, rows, width):
    nblk = s // rows
    prev = pl.BlockSpec((rows, width), lambda i: (jnp.maximum(i * (tm // rows) - 1, 0), 0))
    nxt = pl.BlockSpec((rows, width), lambda i: (jnp.minimum((i + 1) * (tm // rows), nblk - 1), 0))
    return prev, nxt


def _conv_common(pa, cu_prev, cu_next, w, tm):
    b, c, u, z = (pa[:, 512 * k:512 * k + 512] for k in range(4))
    cu = c * u
    row = lax.broadcasted_iota(jnp.int32, (tm, 512), 0)
    cu_m1 = jnp.where(row == 0, cu_prev, pltpu.roll(cu, 1, 0))
    cu_p1 = jnp.where(row == tm - 1, cu_next, pltpu.roll(cu, tm - 1, 0))
    y = cu_m1 * w[0:1] + cu * w[1:2] + cu_p1 * w[2:3]
    sig = _sigmoid(z)
    return b, c, u, z, cu, cu_m1, cu_p1, y, sig, row


def _conv_fwd(pa, w_conv):
    s = pa.shape[0]
    tm = min(512, s)
    nt = s // tm

    def body(pa_ref, pp_ref, pn_ref, w_ref, ya_ref):
        i = pl.program_id(0)
        prev_row = pp_ref[...].astype(F32)[15:16, :]
        next_row = pn_ref[...].astype(F32)[0:1, :]
        b, _, _, z, _, _, _, y, sig, _ = _conv_common(
            pa_ref[...].astype(F32),
            jnp.where(i == 0, 0.0, prev_row[:, 512:1024] * prev_row[:, 1024:1536]),
            jnp.where(i == nt - 1, 0.0, next_row[:, 512:1024] * next_row[:, 1024:1536]), w_ref[...], tm)
        ya_ref[...] = (b * y * (z * sig)).astype(BF16)

    prev, nxt = _halo_specs(s, tm, 16, 2048)
    return pl.pallas_call(
        body, name="conv_fwd", grid=(nt,),
        out_shape=jax.ShapeDtypeStruct((s, 512), BF16),
        in_specs=[_rows(tm, 2048), prev, nxt, _full((3, 512))],
        out_specs=_rows(tm, 512),
        compiler_params=_params(),
    )(pa, pa, pa, w_conv)


def _heads_to_lanes(a, g, row):
    low = row < HEAD_DIM
    parts = []
    for b in (2 * g, 2 * g + 1):
        t = jnp.transpose(a[:, 128 * b:128 * b + 128])
        swapped = pltpu.roll(t, HEAD_DIM, 0)
        if g == 0:
            parts += [jnp.where(low, t, 0.0), jnp.where(low, swapped, 0.0)]
        else:
            parts += [jnp.where(low, 0.0, swapped), jnp.where(low, 0.0, t)]
    return jnp.concatenate(parts, axis=1)


def _lanes_to_heads(t0, t1, row):
    low = row < HEAD_DIM
    blocks = []
    for b in range(4):
        g = b // 2
        tg = (t0, t1)[g]
        je = 2 * (b - 2 * g)
        even, odd = tg[:, 128 * je:128 * je + 128], tg[:, 128 * je + 128:128 * je + 256]
        if g == 0:
            t = jnp.where(low, even, pltpu.roll(odd, HEAD_DIM, 0))
        else:
            t = jnp.where(low, pltpu.roll(even, HEAD_DIM, 0), odd)
        blocks.append(jnp.transpose(t))
    return jnp.concatenate(blocks, axis=1)


WINDOW_KEYS = 3 * ATTN_BLOCK
STACKED = 4 * ATTN_BLOCK
KEY_CHUNK = 32
MAX_BLOCKS_IN_STEP = 8


def _fill_band_bias(bias, nb):
    assert nb >= 2
    c = lax.broadcasted_iota(jnp.int32, (WINDOW_KEYS, STACKED), 0)
    r = lax.broadcasted_iota(jnp.int32, (WINDOW_KEYS, STACKED), 1) & (ATTN_BLOCK - 1)
    band = (c >= r) & (c <= r + 2 * ATTN_BLOCK)
    for v, ok in enumerate((band, band & (c >= ATTN_BLOCK), band & (c < 2 * ATTN_BLOCK))):
        bias[v] = jnp.where(ok, 0.0, -jnp.inf)


def _bias_variant(n, nb):
    return jnp.where(n == 0, 1, jnp.where(n == nb - 1, 2, 0))


def _sink_row(sink_ref, g):
    return jnp.concatenate([jnp.full((1, ATTN_BLOCK), sink_ref[4 * g + j], F32) for j in range(4)], axis=1)


def _softmax_keys_major(sc, bias, variant, sink, e_scr):
    chunks = [pl.ds(k * KEY_CHUNK, KEY_CHUNK) for k in range(WINDOW_KEYS // KEY_CHUNK)]
    rows = [slice(k * KEY_CHUNK, (k + 1) * KEY_CHUNK) for k in range(WINDOW_KEYS // KEY_CHUNK)]
    m_run = jnp.full((KEY_CHUNK, STACKED), -jnp.inf, F32)
    for ck, rw in zip(chunks, rows):
        m_run = jnp.maximum(m_run, sc[rw] + bias[variant, ck, :])
    m = jnp.maximum(jnp.max(m_run, axis=0, keepdims=True), sink)
    l_run = jnp.zeros((KEY_CHUNK, STACKED), F32)
    for ck, rw in zip(chunks, rows):
        e = jnp.exp(sc[rw] + bias[variant, ck, :] - m)
        l_run += e
        e_scr[rw, :] = e.astype(BF16)
    es = jnp.exp(sink - m)
    inv = 1.0 / (jnp.sum(l_run, axis=0, keepdims=True) + es)
    return inv, es * inv


def _fill_padded(kv_ref, kpad, vpad, s):
    zero = jnp.zeros((ATTN_BLOCK, 128), BF16)
    kpad[0:ATTN_BLOCK, :] = zero
    vpad[0:ATTN_BLOCK, :] = zero
    kpad[ATTN_BLOCK + s:2 * ATTN_BLOCK + s, :] = zero
    vpad[ATTN_BLOCK + s:2 * ATTN_BLOCK + s, :] = zero
    kpad[ATTN_BLOCK:ATTN_BLOCK + s, :] = kv_ref[:, 0:128]
    vpad[ATTN_BLOCK:ATTN_BLOCK + s, :] = kv_ref[:, 128:256]


def _attn_fwd(pq, pkv, pbz, sink):
    s = pq.shape[0]
    nb = s // ATTN_BLOCK
    bps = min(MAX_BLOCKS_IN_STEP, nb)

    def body(sink_ref, q_ref, z_ref, kv_ref, yb_ref, kpad, vpad, bias, e_scr):
        i = pl.program_id(0)

        @pl.when(i == 0)
        def _():
            _fill_padded(kv_ref, kpad, vpad, s)
            _fill_band_bias(bias, nb)

        row = lax.broadcasted_iota(jnp.int32, (ATTN_BLOCK, 128), 0)
        for b in range(bps):
            n = i * bps + b
            rows = slice(b * ATTN_BLOCK, (b + 1) * ATTN_BLOCK)
            start = pl.multiple_of(n * ATTN_BLOCK, ATTN_BLOCK)
            kw, vw = kpad[pl.ds(start, WINDOW_KEYS), :], vpad[pl.ds(start, WINDOW_KEYS), :]
            qf = q_ref[rows, :].astype(F32)
            variant = _bias_variant(n, nb)
            outs = []
            for g in range(2):
                e_bg = e_scr.at[2 * b + g]
                qt = (_heads_to_lanes(qf, g, row) * ATTN_SCALE).astype(BF16)
                inv, _ = _softmax_keys_major(_dot(kw, qt), bias, variant, _sink_row(sink_ref, g), e_bg)
                outs.append(_dot_tn(vw, e_bg[...]) * inv)
            attn = _lanes_to_heads(outs[0], outs[1], row)
            z = z_ref[rows, :].astype(F32)
            yb_ref[rows, :] = (attn * (z * _sigmoid(z))).astype(BF16)

    tq = bps * ATTN_BLOCK
    return pl.pallas_call(
        body, name="attn_fwd", grid=(s // tq,),
        out_shape=jax.ShapeDtypeStruct((s, 512), BF16),
        in_specs=[pl.BlockSpec(memory_space=pltpu.SMEM), _rows(tq, 512), _rows(tq, 512), _full((s, 256))],
        out_specs=_rows(tq, 512),
        scratch_shapes=[pltpu.VMEM((s + 2 * ATTN_BLOCK, 128), BF16)] * 2
        + [pltpu.VMEM((3, WINDOW_KEYS, STACKED), F32),
           pltpu.VMEM((2 * bps, WINDOW_KEYS, STACKED), BF16)],
        compiler_params=_params(),
    )(sink, pq, pbz, pkv)


def _mem_softmax_t(q, mk):
    sc = _dot_nt(mk, q) * MEM_SCALE
    e = jnp.exp(sc - jnp.max(sc, axis=0, keepdims=True))
    return e * (1.0 / jnp.sum(e, axis=0, keepdims=True))


def _mem_attn_fwd(pmq, pmz, mkv):
    s = pmq.shape[0]
    m = mkv.shape[0]
    tm = min(512, s)

    def body(q_ref, z_ref, mk_ref, mv_ref, ym_ref):
        z = z_ref[...].astype(F32)
        sz = z * _sigmoid(z)
        for h in range(MEM_HEADS):
            cols = slice(128 * h, 128 * h + 128)
            pt = _mem_softmax_t(q_ref[:, cols], mk_ref[:, cols])
            o = _dot_tn(pt.astype(BF16), mv_ref[:, cols])
            ym_ref[:, cols] = (o * sz[:, cols]).astype(BF16)

    return pl.pallas_call(
        body, name="mem_attn_fwd", grid=(s // tm,),
        out_shape=jax.ShapeDtypeStruct((s, 512), BF16),
        in_specs=[_rows(tm, 512), _rows(tm, 512), pl.BlockSpec((m, 512), lambda i: (0, 0)),
                  pl.BlockSpec((m, 512), lambda i: (0, 1))],
        out_specs=_rows(tm, 512),
        compiler_params=_params(),
    )(pmq, pmz, mkv, mkv)


def _mid(ya, yb, ym, pg, x, target, g_post, w_up, w_out):
    s = x.shape[0]
    tm = min(256, s)
    nt = s // tm

    def body(ya_ref, yb_ref, ym_ref, pg_ref, x_ref, t_ref, gp_ref, wup_hbm, wout_hbm,
             dg_ref, dya_ref, dyb_ref, dym_ref, dy_ref, loss_ref, ggp_ref, mb_ref, dob_ref, du_ref,
             wup_vm, wout_vm, sems):
        i = pl.program_id(0)
        _load_once([(wup_hbm.at[d], wup_vm.at[:, pl.ds(128 * d, 128)]) for d in range(N_DEV)]
                   + [(wout_hbm, wout_vm)], sems)

        @pl.when(i == 0)
        def _():
            loss_ref[...] = jnp.zeros_like(loss_ref)
            ggp_ref[...] = jnp.zeros_like(ggp_ref)

        ys = (ya_ref[...], yb_ref[...], ym_ref[...])
        us = [_dot(ys[k], wup_vm[512 * k:512 * k + 512, :]) for k in range(3)]
        gates = [_sigmoid(pg_ref[:, 1024 * k:1024 * k + 1024].astype(F32)) for k in range(3)]
        merged = gates[0] * us[0] + gates[1] * us[1] + gates[2] * us[2]
        mb = merged.astype(BF16)
        mb_ref[...] = mb
        out = _dot(mb, wout_vm[...])
        r = lax.rsqrt(jnp.mean(out * out, axis=-1, keepdims=True) + EPS)
        on = out * r
        gp = gp_ref[...]
        err = (x_ref[...] + on * gp) - t_ref[...]
        loss_ref[...] += 0.5 * jnp.sum(err * err) * (1.0 / D_MODEL)
        dy = err * (1.0 / D_MODEL)
        dy_ref[...] = dy
        ggp_ref[...] += jnp.sum(dy * on, axis=0, keepdims=True)
        a = dy * gp
        d_out = r * (a - on * jnp.mean(a * on, axis=-1, keepdims=True))
        dob = d_out.astype(BF16)
        dob_ref[...] = dob
        d_merged = _dot_nt(dob, wout_vm[...])
        d_refs = (dya_ref, dyb_ref, dym_ref)
        for k in range(3):
            g = gates[k]
            du_f = d_merged * g
            dg_ref[:, 1024 * k:1024 * k + 1024] = (du_f * us[k] * (1.0 - g)).astype(BF16)
            du = du_f.astype(BF16)
            du_ref[k] = du
            d_refs[k][...] = _dot_nt(du, wup_vm[512 * k:512 * k + 512, :]).astype(BF16)

    return pl.pallas_call(
        body, name="mid", grid=(nt,),
        out_shape=[jax.ShapeDtypeStruct((s, 3072), BF16)] + [jax.ShapeDtypeStruct((s, 512), BF16)] * 3
        + [jax.ShapeDtypeStruct((s, D_MODEL), F32), jax.ShapeDtypeStruct((8, 128), F32),
           jax.ShapeDtypeStruct((1, D_MODEL), F32), jax.ShapeDtypeStruct((s, D_MODEL), BF16),
           jax.ShapeDtypeStruct((s, D_MODEL), BF16), jax.ShapeDtypeStruct((3, s, D_MODEL), BF16)],
        in_specs=[_rows(tm, 512)] * 3 + [_rows(tm, 3072), _rows(tm, D_MODEL), _rows(tm, D_MODEL),
                                         _full((1, D_MODEL)), ANY, ANY],
        out_specs=[_rows(tm, 3072)] + [_rows(tm, 512)] * 3
        + [_rows(tm, D_MODEL), _full((8, 128)), _full((1, D_MODEL)), _rows(tm, D_MODEL), _rows(tm, D_MODEL),
           pl.BlockSpec((3, tm, D_MODEL), lambda i: (0, i, 0))],
        scratch_shapes=[pltpu.VMEM((1536, D_MODEL), BF16), pltpu.VMEM((D_MODEL, D_MODEL), BF16),
                        pltpu.SemaphoreType.DMA((N_DEV + 1,))],
        compiler_params=_params(),
    )(ya, yb, ym, pg, x, target, g_post, w_up, w_out)


def _gw_mid(mb, dob, ys, du):
    s = mb.shape[0]
    tn = 256
    n_out = D_MODEL // tn
    tiles = [(0, c0, 0) for c0 in range(0, D_MODEL, tn)]
    tiles += [(1 + k, c0, 1 + k) for k in range(3) for c0 in range(0, 512, tn)]
    n_t = len(tiles)

    def body(mb_hbm, ya_hbm, yb_hbm, ym_hbm, dob_hbm, du_hbm, out_hbm, up_hbm,
             lhs, rhs, res_out, res_up, in_sems, rhs_sems, out_sems):
        lhs_hbm = (mb_hbm, ya_hbm, yb_hbm, ym_hbm)

        def load(t):
            a, c0, _ = tiles[t]
            return pltpu.make_async_copy(lhs_hbm[a].at[:, pl.ds(c0, tn)], lhs.at[t & 1], in_sems.at[t & 1])

        def load_rhs(g):
            src = dob_hbm if g == 0 else du_hbm.at[g - 1]
            return pltpu.make_async_copy(src, rhs.at[g & 1], rhs_sems.at[g & 1])

        def store(t):
            if t < n_out:
                return pltpu.make_async_copy(res_out.at[t & 1], out_hbm.at[pl.ds(t * tn, tn)], out_sems.at[t & 1])
            rows = pl.ds((t - n_out) * tn, tn)
            return pltpu.make_async_copy(res_up.at[t & 1], up_hbm.at[:, rows, :], out_sems.at[t & 1])

        load_rhs(0).start()
        load(0).start()
        for t, (_, _, g) in enumerate(tiles):
            new_rhs = t == 0 or tiles[t - 1][2] != g
            if t + 1 < n_t:
                load(t + 1).start()
            if new_rhs and g < 3:
                load_rhs(g + 1).start()
            load(t).wait()
            if new_rhs:
                load_rhs(g).wait()
            if t >= 2:
                store(t - 2).wait()
            r = _dot_tn(lhs[t & 1], rhs[g & 1])
            if t < n_out:
                res_out[t & 1] = r.astype(BF16)
            else:
                for d in range(N_DEV):
                    res_up[t & 1, d] = r[:, 128 * d:128 * d + 128].astype(BF16)
            store(t).start()
        store(n_t - 2).wait()
        store(n_t - 1).wait()

    return pl.pallas_call(
        body, name="gw_mid",
        out_shape=[jax.ShapeDtypeStruct((D_MODEL, D_MODEL), BF16), jax.ShapeDtypeStruct((N_DEV, 1536, 128), BF16)],
        in_specs=[ANY] * 6, out_specs=[ANY, ANY],
        scratch_shapes=[pltpu.VMEM((2, s, tn), BF16), pltpu.VMEM((2, s, D_MODEL), BF16),
                        pltpu.VMEM((2, tn, D_MODEL), BF16), pltpu.VMEM((2, N_DEV, tn, 128), BF16),
                        pltpu.SemaphoreType.DMA((2,)), pltpu.SemaphoreType.DMA((2,)),
                        pltpu.SemaphoreType.DMA((2,))],
        compiler_params=pltpu.CompilerParams(vmem_limit_bytes=CALL_VMEM_MB * 1024 * 1024),
    )(mb, *ys, dob, du)


def _conv_bwd(after, pa, dya, w_conv):
    s = pa.shape[0]
    tm = min(512, s)
    nt = s // tm

    def body(after_ref, pa_ref, pp_ref, pn_ref, d_ref, dp_ref, dn_ref, w_ref, da_ref, gw_ref):
        i = pl.program_id(0)
        first, last = i == 0, i == nt - 1

        @pl.when(first)
        def _():
            gw_ref[...] = jnp.zeros_like(gw_ref)

        w = w_ref[...]
        prev_row = pp_ref[...].astype(F32)[15:16, :]
        next_row = pn_ref[...].astype(F32)[0:1, :]
        b, c, u, z, cu, cu_m1, cu_p1, y, sig, row = _conv_common(
            pa_ref[...].astype(F32),
            jnp.where(first, 0.0, prev_row[:, 512:1024] * prev_row[:, 1024:1536]),
            jnp.where(last, 0.0, next_row[:, 512:1024] * next_row[:, 1024:1536]), w, tm)
        sz = z * sig
        dya_t = d_ref[...].astype(F32)
        d_y = dya_t * b * sz

        def halo_dy(p_row, d_row):
            zz = p_row[:, 1536:2048]
            return d_row * p_row[:, 0:512] * (zz * _sigmoid(zz))

        dy_prev = jnp.where(first, 0.0, halo_dy(prev_row, dp_ref[...].astype(F32)[15:16, :]))
        dy_next = jnp.where(last, 0.0, halo_dy(next_row, dn_ref[...].astype(F32)[0:1, :]))
        dy_m1 = jnp.where(row == 0, dy_prev, pltpu.roll(d_y, 1, 0))
        dy_p1 = jnp.where(row == tm - 1, dy_next, pltpu.roll(d_y, tm - 1, 0))
        d_cu = dy_p1 * w[0:1] + d_y * w[1:2] + dy_m1 * w[2:3]
        da_ref[:, 0:512] = (dya_t * y * sz).astype(BF16)
        da_ref[:, 512:1024] = (d_cu * u).astype(BF16)
        da_ref[:, 1024:1536] = (d_cu * c).astype(BF16)
        da_ref[:, 1536:2048] = (dya_t * b * y * (sig + sz * (1.0 - sig))).astype(BF16)
        gw_ref[0:1, :] += jnp.sum(d_y * cu_m1, axis=0, keepdims=True)
        gw_ref[1:2, :] += jnp.sum(d_y * cu, axis=0, keepdims=True)
        gw_ref[2:3, :] += jnp.sum(d_y * cu_p1, axis=0, keepdims=True)

    prev, nxt = _halo_specs(s, tm, 16, 2048)
    dprev, dnxt = _halo_specs(s, tm, 16, 512)
    return pl.pallas_call(
        body, name="conv_bwd", grid=(nt,),
        out_shape=[jax.ShapeDtypeStruct((s, 2048), BF16), jax.ShapeDtypeStruct((8, 512), F32)],
        in_specs=[_full(TOKEN), _rows(tm, 2048), prev, nxt, _rows(tm, 512), dprev, dnxt, _full((3, 512))],
        out_specs=[_rows(tm, 2048), _full((8, 512))],
        compiler_params=_params(),
    )(after, pa, pa, pa, dya, dya, dya, w_conv)


def _attn_bwd(after, pq, pkv, pbz, dyb, sink, tabs):
    s = pq.shape[0]
    nb = s // ATTN_BLOCK
    bps = min(MAX_BLOCKS_IN_STEP, nb)

    def body(sink_ref, after_ref, q_ref, z_ref, d_ref, kv_ref, t_ref,
             dq_ref, dz_ref, dkv_ref, gs_ref, kpad, vpad, dk_acc, dv_acc, bias, e_scr, ds_scr):
        i = pl.program_id(0)

        @pl.when(i == 0)
        def _():
            _fill_padded(kv_ref, kpad, vpad, s)
            _fill_band_bias(bias, nb)
            dk_acc[...] = jnp.zeros_like(dk_acc)
            dv_acc[...] = jnp.zeros_like(dv_acc)
            gs_ref[...] = jnp.zeros_like(gs_ref)

        row = lax.broadcasted_iota(jnp.int32, (ATTN_BLOCK, 128), 0)
        for b in range(bps):
            n = i * bps + b
            rows = slice(b * ATTN_BLOCK, (b + 1) * ATTN_BLOCK)
            start = pl.multiple_of(n * ATTN_BLOCK, ATTN_BLOCK)
            kw, vw = kpad[pl.ds(start, WINDOW_KEYS), :], vpad[pl.ds(start, WINDOW_KEYS), :]
            qf = q_ref[rows, :].astype(F32)
            variant = _bias_variant(n, nb)
            z = z_ref[rows, :].astype(F32)
            sig = _sigmoid(z)
            dyb_t = d_ref[rows, :].astype(F32)
            d_attn = dyb_t * (z * sig)
            outs, dqs = [], []
            dk_w = jnp.zeros((WINDOW_KEYS, 128), F32)
            dv_w = jnp.zeros((WINDOW_KEYS, 128), F32)
            for g in range(2):
                e_bg, ds_bg = e_scr.at[2 * b + g], ds_scr.at[2 * b + g]
                qt = _heads_to_lanes(qf, g, row)
                inv, p_sink = _softmax_keys_major(
                    _dot(kw, (qt * ATTN_SCALE).astype(BF16)), bias, variant, _sink_row(sink_ref, g), e_bg)
                ot = _dot_tn(vw, e_bg[...]) * inv
                outs.append(ot)
                dot_ = _heads_to_lanes(d_attn, g, row)
                delta = jnp.sum(dot_ * ot, axis=0, keepdims=True)
                dpt = _dot(vw, dot_.astype(BF16))
                for k in range(WINDOW_KEYS // KEY_CHUNK):
                    rw = slice(k * KEY_CHUNK, (k + 1) * KEY_CHUNK)
                    ds_bg[rw, :] = (e_bg[rw, :].astype(F32) * (dpt[rw] - delta)).astype(BF16)
                sink_part = p_sink * delta
                for j in range(4):
                    h = 4 * g + j
                    gs_ref[h:h + 1, :] -= jnp.sum(sink_part[:, 128 * j:128 * j + 128])
                dqs.append(_dot_tn(kw, ds_bg[...]) * (inv * ATTN_SCALE))
                dk_w += _dot_nt(ds_bg[...], (qt * inv).astype(BF16)) * ATTN_SCALE
                dv_w += _dot_nt(e_bg[...], (dot_ * inv).astype(BF16))
            dk_acc[pl.ds(start, WINDOW_KEYS), :] += dk_w
            dv_acc[pl.ds(start, WINDOW_KEYS), :] += dv_w
            attn = _lanes_to_heads(outs[0], outs[1], row)
            dz_ref[rows, :] = (dyb_t * attn * (sig * (1.0 + z * (1.0 - sig)))).astype(BF16)
            dq = _lanes_to_heads(dqs[0], dqs[1], row)
            trows = pl.ds(start, ATTN_BLOCK)
            cs, s1, s2 = t_ref[0, trows, :], t_ref[1, trows, :], t_ref[2, trows, :]
            for blk in range(4):
                cols = slice(128 * blk, 128 * blk + 128)
                dq_ref[rows, cols] = _rope_t(dq[:, cols], cs, s1, s2).astype(BF16)

        @pl.when(i == nb // bps - 1)
        def _():
            dk = dk_acc[ATTN_BLOCK:ATTN_BLOCK + s, :]
            dkv_ref[:, 0:128] = _rope_t(dk, t_ref[0], t_ref[1], t_ref[2]).astype(BF16)
            dkv_ref[:, 128:256] = dv_acc[ATTN_BLOCK:ATTN_BLOCK + s, :].astype(BF16)

    tq = bps * ATTN_BLOCK
    tile = _rows(tq, 512)
    return pl.pallas_call(
        body, name="attn_bwd", grid=(s // tq,),
        out_shape=[jax.ShapeDtypeStruct((s, 512), BF16), jax.ShapeDtypeStruct((s, 512), BF16),
                   jax.ShapeDtypeStruct((s, 256), BF16), jax.ShapeDtypeStruct((8, 128), F32)],
        in_specs=[pl.BlockSpec(memory_space=pltpu.SMEM), _full(TOKEN), tile, tile, tile, _full((s, 256)),
                  _full((3, s, 128))],
        out_specs=[tile, tile, _full((s, 256)), _full((8, 128))],
        scratch_shapes=[pltpu.VMEM((s + 2 * ATTN_BLOCK, 128), BF16)] * 2
        + [pltpu.VMEM((s + 2 * ATTN_BLOCK, 128), F32)] * 2
        + [pltpu.VMEM((3, WINDOW_KEYS, STACKED), F32)]
        + [pltpu.VMEM((2 * bps, WINDOW_KEYS, STACKED), BF16)] * 2,
        compiler_params=_params(),
    )(sink, after, pq, pbz, dyb, pkv, tabs)


def _mem_attn_bwd(pmq, pmz, mkv, dym):
    s = pmq.shape[0]
    m = mkv.shape[0]
    tm = min(512, s)

    def body(q_ref, z_ref, d_ref, mk_ref, mv_ref, dq_ref, dz_ref, dmkv_ref):
        @pl.when(pl.program_id(0) == 0)
        def _():
            dmkv_ref[...] = jnp.zeros_like(dmkv_ref)

        z = z_ref[...].astype(F32)
        sig = _sigmoid(z)
        dym_t = d_ref[...].astype(F32)
        d_attn = dym_t * (z * sig)
        dsilu = sig * (1.0 + z * (1.0 - sig))
        for h in range(MEM_HEADS):
            cols = slice(128 * h, 128 * h + 128)
            q, mk, mv = q_ref[:, cols], mk_ref[:, cols], mv_ref[:, cols]
            pt = _mem_softmax_t(q, mk)
            pb = pt.astype(BF16)
            o = _dot_tn(pb, mv)
            dob = d_attn[:, cols].astype(BF16)
            dpt = _dot_nt(mv, dob)
            dst = (pt * (dpt - jnp.sum(pt * dpt, axis=0, keepdims=True))).astype(BF16)
            dq_ref[:, cols] = (_dot_tn(dst, mk) * MEM_SCALE).astype(BF16)
            dz_ref[:, cols] = (dym_t[:, cols] * o * dsilu[:, cols]).astype(BF16)
            dmkv_ref[:, cols] += _dot(dst, q) * MEM_SCALE
            dmkv_ref[:, 512 + 128 * h:512 + 128 * h + 128] += _dot(pb, dob)

    return pl.pallas_call(
        body, name="mem_attn_bwd", grid=(s // tm,),
        out_shape=[jax.ShapeDtypeStruct((s, 512), BF16), jax.ShapeDtypeStruct((s, 512), BF16),
                   jax.ShapeDtypeStruct((m, D_MODEL), F32)],
        in_specs=[_rows(tm, 512), _rows(tm, 512), _rows(tm, 512), pl.BlockSpec((m, 512), lambda i: (0, 0)),
                  pl.BlockSpec((m, 512), lambda i: (0, 1))],
        out_specs=[_rows(tm, 512), _rows(tm, 512), _full((m, D_MODEL))],
        compiler_params=_params(),
    )(pmq, pmz, dym, mkv, mkv)


def _mem_kv_bwd(mem, g_mem, mn, dmkv, w_mkv):
    m = mem.shape[0]

    def body(mem_ref, g_ref, mn_ref, d_ref, w_ref, gw_ref, gg_ref):
        db = d_ref[...].astype(BF16)
        gw_ref[...] = _dot_tn(mn_ref[...], db).astype(BF16)
        d_mn = _dot_nt(db, w_ref[...])
        xf = mem_ref[...]
        r = lax.rsqrt(jnp.mean(xf * xf, axis=-1, keepdims=True) + EPS)
        gg_ref[...] = jnp.sum(d_mn * (xf * r), axis=0, keepdims=True)

    return pl.pallas_call(
        body, name="mem_kv_bwd", grid=(1,),
        out_shape=[jax.ShapeDtypeStruct((D_MODEL, D_MODEL), BF16), jax.ShapeDtypeStruct((1, D_MODEL), F32)],
        in_specs=[_full((m, D_MODEL)), _full((1, D_MODEL)), _full((m, D_MODEL)), _full((m, D_MODEL)),
                  _full((D_MODEL, D_MODEL))],
        out_specs=[_full((D_MODEL, D_MODEL)), _full((1, D_MODEL))],
        compiler_params=_params(),
    )(mem, g_mem, mn, dmkv, w_mkv)


def _dh_bwd(after, dparts, x, dy, g_pre, w_int):
    s = x.shape[0]
    tm = min(256, s)
    n_t = s // tm
    tiled = list(dparts) + [x, dy]
    n_in = len(tiled)

    def body(after_ref, *refs):
        in_hbm = refs[:n_in]
        g_ref, w_hbm, gx_hbm, gg_ref = refs[n_in:n_in + 4]
        bufs = refs[n_in + 4:2 * n_in + 4]
        w_vm, gx_buf, in_sems, out_sems, w_sem = refs[2 * n_in + 4:]

        def loads(t, slot):
            rows = pl.ds(pl.multiple_of(t * tm, tm), tm)
            return [pltpu.make_async_copy(in_hbm[k].at[rows], bufs[k].at[slot], in_sems.at[slot, k])
                    for k in range(n_in)]

        def store(t, slot):
            rows = pl.ds(pl.multiple_of(t * tm, tm), tm)
            return pltpu.make_async_copy(gx_buf.at[slot], gx_hbm.at[rows], out_sems.at[slot])

        w_copy = pltpu.make_async_copy(w_hbm, w_vm, w_sem)
        w_copy.start()
        for cp in loads(0, 0):
            cp.start()
        w_copy.wait()

        def step(t, gg):
            slot = t & 1

            @pl.when(t + 1 < n_t)
            def _():
                for cp in loads(t + 1, 1 - slot):
                    cp.start()

            for cp in loads(t, slot):
                cp.wait()

            @pl.when(t >= 2)
            def _():
                store(t - 2, slot).wait()

            d_h = jnp.zeros((tm, D_MODEL), F32)
            for k, (r0, width) in enumerate(SEGS):
                for c0 in range(0, width, 512):
                    cw = min(512, width - c0)
                    d_h += _dot(bufs[k].at[slot][:, c0:c0 + cw], w_vm[r0 + c0:r0 + c0 + cw, :])
            xf = bufs[n_in - 2][slot]
            r = lax.rsqrt(jnp.mean(xf * xf, axis=-1, keepdims=True) + EPS)
            xn = xf * r
            a = d_h * g_ref[...]
            gx_buf[slot] = r * (a - xn * jnp.mean(a * xn, axis=-1, keepdims=True)) + bufs[n_in - 1][slot]
            store(t, slot).start()
            return gg + jnp.sum(d_h * xn, axis=0, keepdims=True)

        gg_ref[...] = lax.fori_loop(0, n_t, step, jnp.zeros((1, D_MODEL), F32))
        store(n_t - 2, (n_t - 2) & 1).wait()
        store(n_t - 1, (n_t - 1) & 1).wait()

    vmem = pl.BlockSpec(memory_space=pltpu.VMEM)
    return pl.pallas_call(
        body, name="dh_bwd",
        out_shape=[jax.ShapeDtypeStruct((s, D_MODEL), F32), jax.ShapeDtypeStruct((1, D_MODEL), F32)],
        in_specs=[ANY] * (1 + n_in) + [vmem, ANY],
        out_specs=[ANY, vmem],
        scratch_shapes=[pltpu.VMEM((2, tm) + a.shape[1:], a.dtype) for a in tiled]
        + [pltpu.VMEM((IN_WIDTH, D_MODEL), BF16), pltpu.VMEM((2, tm, D_MODEL), F32),
           pltpu.SemaphoreType.DMA((2, n_in)), pltpu.SemaphoreType.DMA((2,)), pltpu.SemaphoreType.DMA(())],
        compiler_params=pltpu.CompilerParams(vmem_limit_bytes=CALL_VMEM_MB * 1024 * 1024),
    )(after, *dparts, x, dy, g_pre, w_int)


def _gw_in(dparts, h):
    s = h.shape[0]
    tn = 256
    tiles = [(a, c0) for a, (_, width) in enumerate(SEGS) for c0 in range(0, width, tn)]
    n_t = len(tiles)
    assert n_t * tn == IN_WIDTH

    def body(*refs):
        d_hbm = refs[:7]
        h_hbm, out_hbm, lhs, h_vm, res, in_sems, out_sems, h_sem = refs[7:]

        def load(a, c0, slot):
            return pltpu.make_async_copy(d_hbm[a].at[:, pl.ds(c0, tn)], lhs.at[slot], in_sems.at[slot])

        def store(t, slot):
            rows = pl.ds(pl.multiple_of(t * tn, tn), tn)
            return pltpu.make_async_copy(res.at[slot], out_hbm.at[rows], out_sems.at[slot])

        def start_load(t, slot):
            for k, (a, c0) in enumerate(tiles):
                @pl.when(t == k)
                def _(a=a, c0=c0):
                    load(a, c0, slot).start()

        h_copy = pltpu.make_async_copy(h_hbm, h_vm, h_sem)
        h_copy.start()
        load(*tiles[0], 0).start()
        h_copy.wait()

        def step(t, carry):
            slot = t & 1

            @pl.when(t + 1 < n_t)
            def _():
                start_load(t + 1, 1 - slot)

            load(*tiles[0], slot).wait()

            @pl.when(t >= 2)
            def _():
                store(t - 2, slot).wait()

            res[slot] = _dot_tn(lhs[slot], h_vm[...]).astype(BF16)
            store(t, slot).start()
            return carry

        lax.fori_loop(0, n_t, step, 0)
        store(n_t - 2, (n_t - 2) & 1).wait()
        store(n_t - 1, (n_t - 1) & 1).wait()

    return pl.pallas_call(
        body, name="gw_in",
        out_shape=jax.ShapeDtypeStruct((IN_WIDTH, D_MODEL), BF16),
        in_specs=[ANY] * 8, out_specs=ANY,
        scratch_shapes=[pltpu.VMEM((2, s, tn), BF16), pltpu.VMEM((s, D_MODEL), BF16),
                        pltpu.VMEM((2, tn, D_MODEL), BF16), pltpu.SemaphoreType.DMA((2,)),
                        pltpu.SemaphoreType.DMA((2,)), pltpu.SemaphoreType.DMA(())],
        compiler_params=pltpu.CompilerParams(vmem_limit_bytes=CALL_VMEM_MB * 1024 * 1024),
    )(*dparts, h)


def _adamw_math(w, g, m, v):
    m2 = ADAM_B1 * m + (1.0 - ADAM_B1) * g
    v2 = ADAM_B2 * v + (1.0 - ADAM_B2) * (g * g)
    m_hat = m2 / (1.0 - ADAM_B1 ** ADAM_STEP)
    v_hat = v2 / (1.0 - ADAM_B2 ** ADAM_STEP)
    delta = -ADAM_LR * (m_hat / (jnp.sqrt(v_hat) + ADAM_EPS) + ADAM_WD * w)
    return delta, m2, v2


def _sum_adamw(after, own, land, chip, block, w, m, v, name, tiles=1):
    r, c = w.shape
    rt = r // tiles

    def body(c_ref, after_ref, own_ref, l1_ref, l2_ref, l3_ref, w_ref, m_ref, v_ref, g_ref, d_ref, m2_ref, v2_ref):
        g = own_ref[...].astype(F32)
        for l_ref in (l1_ref, l2_ref, l3_ref):
            g += l_ref[...].astype(F32)
        g_ref[...] = g
        d_ref[...], m2_ref[...], v2_ref[...] = _adamw_math(w_ref[...], g, m_ref[...], v_ref[...])

    def share(k):
        return pl.BlockSpec((None, rt, c), lambda i, c_ref: (jnp.bitwise_xor(c_ref[0], k), block * tiles + i, 0))

    spec = pl.BlockSpec((rt, c), lambda i, c_ref: (i, 0))
    grid_spec = pltpu.PrefetchScalarGridSpec(
        num_scalar_prefetch=1, grid=(tiles,),
        in_specs=[ANY, share(0), share(1), share(2), share(3)] + [spec] * 3, out_specs=[spec] * 4)
    return pl.pallas_call(
        body, name=name, grid_spec=grid_spec,
        out_shape=[jax.ShapeDtypeStruct((r, c), F32)] * 4,
        compiler_params=_params(),
    )(chip, after, own, land, land, land, w, m, v)


def _sum_adamw_group(after, items, chip, name):
    k = len(items)

    def body(c_ref, after_ref, *refs):
        shares, wmv, outs = refs[:4 * k], refs[4 * k:7 * k], refs[7 * k:]
        for j in range(k):
            g = shares[4 * j][...].astype(F32)
            for l_ref in shares[4 * j + 1:4 * j + 4]:
                g += l_ref[...].astype(F32)
            outs[4 * j][...] = g
            outs[4 * j + 1][...], outs[4 * j + 2][...], outs[4 * j + 3][...] = _adamw_math(
                wmv[3 * j][...], g, wmv[3 * j + 1][...], wmv[3 * j + 2][...])

    def share(shape, block, q):
        return pl.BlockSpec((None,) + shape, lambda i, c_ref: (jnp.bitwise_xor(c_ref[0], q), block, 0))

    in_specs, args = [ANY], [after]
    for own, land, block, w, m, v in items:
        in_specs += [share(w.shape, block, q) for q in range(4)]
        args += [own, land, land, land]
    for own, land, block, w, m, v in items:
        in_specs += [pl.BlockSpec(w.shape, lambda i, c_ref: (0, 0))] * 3
        args += [w, m, v]
    out_specs = [pl.BlockSpec(w.shape, lambda i, c_ref: (0, 0)) for _, _, _, w, _, _ in items for _ in range(4)]
    res = pl.pallas_call(
        body, name=name,
        grid_spec=pltpu.PrefetchScalarGridSpec(num_scalar_prefetch=1, grid=(1,), in_specs=in_specs,
                                               out_specs=out_specs),
        out_shape=[jax.ShapeDtypeStruct(w.shape, F32) for _, _, _, w, _, _ in items for _ in range(4)],
        compiler_params=_params(),
    )(chip, *args)
    return [res[4 * j:4 * j + 4] for j in range(k)]


def _small_pack(parts):
    def pack_body(gpre_ref, gconv_ref, gsink_ref, gmem_ref, gpost_ref, loss_ref, pack):
        lane = lax.broadcasted_iota(jnp.int32, (1, 128), 1)
        sink_row = jnp.zeros((1, 128), F32)
        for h in range(8):
            sink_row = jnp.where(lane == h, gsink_ref[h:h + 1, :], sink_row)
        pack[...] = jnp.zeros_like(pack)
        pack[0:1, :] = gpre_ref[...]
        pack[1:2, :] = gmem_ref[...]
        pack[2:3, :] = gpost_ref[...]
        pack[3:6, 0:512] = gconv_ref[0:3, :]
        pack[6:7, 0:128] = sink_row
        pack[7:8, 0:128] = loss_ref[0:1, :]

    return pl.pallas_call(
        pack_body, name="small_pack", grid=(1,),
        out_shape=jax.ShapeDtypeStruct((8, D_MODEL), F32),
        in_specs=[_full(p.shape) for p in parts], out_specs=_full((8, D_MODEL)),
    )(*parts)


def _small_apply(packs, ws, ms, vs):
    def apply(p_ref, *refs):
        w_refs, m_refs, v_refs = refs[0:5], refs[5:10], refs[10:15]
        loss_out = refs[15]
        g_outs, d_outs, m_outs, v_outs = refs[16:21], refs[21:26], refs[26:31], refs[31:36]
        x, y, c = _my_place()
        tot = p_ref[0]
        for d in range(1, N_DEV):
            tot = tot + p_ref[d]
        conv = pltpu.roll(tot[:, 0:512], (512 - 64 * (4 * x + 2 * y + c)) % 512, 1)[3:6, 0:64]
        grads = (tot[0:1, :], conv, tot[6:7, 0:8], tot[1:2, :], tot[2:3, :])
        loss_out[...] = tot[7:8, 0:128]
        for j in range(5):
            if len(w_refs[j].shape) == 3:
                for k in range(w_refs[j].shape[0]):
                    g_outs[j][k] = grads[j][k:k + 1]
                    d_outs[j][k], m_outs[j][k], v_outs[j][k] = _adamw_math(
                        w_refs[j][k], grads[j][k:k + 1], m_refs[j][k], v_refs[j][k])
                continue
            g_outs[j][...] = grads[j]
            d_outs[j][...], m_outs[j][...], v_outs[j][...] = _adamw_math(
                w_refs[j][...], grads[j], m_refs[j][...], v_refs[j][...])

    specs = [_full(w.shape) for w in ws]
    res = pl.pallas_call(
        apply, name="small_apply", grid=(1,),
        out_shape=[jax.ShapeDtypeStruct((1, 128), F32)] + [jax.ShapeDtypeStruct(w.shape, F32) for w in ws] * 4,
        in_specs=[_full((N_DEV, 8, D_MODEL))] + specs * 3,
        out_specs=[_full((1, 128))] + specs * 4,
    )(packs, *ws, *ms, *vs)
    return res[0], res[1:6], res[6:11], res[11:16], res[16:21]


def kernel(x, mem, g_pre, w_in, w_conv, attn_sink, g_mem, w_mem_kv, w_up_a, w_up_b, w_up_m, w_out, g_post, loss_target, m_g_pre, m_w_in, m_w_conv, m_attn_sink, m_g_mem, m_w_mem_kv, m_w_up_a, m_w_up_b, m_w_up_m, m_w_out, m_g_post, v_g_pre, v_w_in, v_w_conv, v_attn_sink, v_g_mem, v_w_mem_kv, v_w_up_a, v_w_up_b, v_w_up_m, v_w_out, v_g_post):
    s = x.shape[1]
    x2, mem2, tgt2 = x[0], mem[0], loss_target[0]
    me = 4 * lax.axis_index("x") + 2 * lax.axis_index("y") + lax.axis_index("c")

    w_conv_loc = jnp.zeros((8, 128), F32).at[:3, :64].set(w_conv[0])
    w_int_g, w_conv_g, tabs, w_mkv_loc, w_out_loc, w_up_loc = _all_gather(
        [w_in[0].T.astype(BF16), w_conv_loc], "gather_w_in",
        splits=[[(112 * k, 112) for k in range(7)] + [(784, 144)], [(0, 8)]],
        side=_gather_side(s, w_mem_kv[0], w_out[0], (w_up_a[0], w_up_b[0], w_up_m[0])))
    w_int = w_int_g.reshape(IN_WIDTH, D_MODEL)
    w_conv_f = w_conv_g[:, :3, :64].transpose(1, 0, 2).reshape(3, 512)
    late = _gather_start([w_mkv_loc, w_out_loc, w_up_loc], me, "gather_late_start")
    sink = attn_sink[0]

    h, pa, pq, pkv, pbz, pmq, pmz, pg = _proj_fwd(late[4], x2, g_pre, w_int, tabs)
    ya = _conv_fwd(pa, w_conv_f)
    yb = _attn_fwd(pq, pkv, pbz, sink)
    w_mkv_g, w_out_g, w_up_g = _gather_wait(*late[:4], yb, "gather_late_wait")
    w_mkv = w_mkv_g.reshape(D_MODEL, D_MODEL)
    w_out_f = w_out_g.reshape(D_MODEL, D_MODEL)
    mn, mkv = _mem_kv_fwd(mem2, g_mem, w_mkv)
    ym = _mem_attn_fwd(pmq, pmz, mkv)
    dg, dya, dyb, dym, dy, loss_p, gg_post, mb, dob, du = _mid(ya, yb, ym, pg, x2, tgt2, g_post, w_up_g, w_out_f)
    gw_out, gw_up = _gw_mid(mb, dob, (ya, yb, ym), du)

    core = lax.axis_index("c").astype(jnp.int32).reshape(1)
    chip = (2 * lax.axis_index("x") + lax.axis_index("y")).astype(jnp.int32).reshape(1)

    dmq, dmz, dmkv = _mem_attn_bwd(pmq, pmz, mkv, dym)
    gw_mkv, gg_mem = _mem_kv_bwd(mem2, g_mem, mn, dmkv, w_mkv)
    shares1 = [gw_mkv.reshape(N_DEV, 128, D_MODEL), gw_out.reshape(N_DEV, 128, D_MODEL), gw_up]
    sib = _split_start(_sibling_copies, N_CHIPS, shares1,
                       [lax.empty((N_CHIPS,) + a.shape[1:], a.dtype) for a in shares1], "grads_to_sibling_small_start")
    da, gw_conv = _conv_bwd(sib[4], pa, dya, w_conv_f)
    shares1, from_sibling = _split_wait(_sibling_copies, N_CHIPS, *sib[:4], da, "grads_to_sibling_small_wait")
    send1, recv1, srcs1, lands1, token1 = _chip_exchange_start(
        _pair_add(shares1, from_sibling, core, "grads_pair_add_small"), "grads_to_chips_start_small")
    dq, dbz, dkv, g_sink = _attn_bwd(token1, pq, pkv, pbz, dyb, sink, tabs)
    dparts = (da, dq, dkv, dbz, dmq, dmz, dg)
    gw_int = _gw_in(dparts, h)
    send2, recv2, srcs2, lands2, token2 = _chip_exchange_start(
        [_sibling_reduce(gw_int.reshape(N_DEV, SHARD_IN, D_MODEL), "grads_sibling_reduce_w_in")],
        "grads_to_chips_start_w_in")
    grad_x, gg_pre = _dh_bwd(token2, dparts, x2, dy, g_pre, w_int)
    (o_mkv, o_out, o_up, o_int), (l_mkv, l_out, l_up, l_int) = _chip_exchange_wait(
        send1 + send2, recv1 + recv2, srcs1 + srcs2, lands1 + lands2, grad_x, "grads_to_chips_wait")

    small = _gather_start([_small_pack((gg_pre, gw_conv, g_sink, gg_mem, gg_post, loss_p))], me,
                          "small_gather_start")

    w_in_t = _sum_adamw(small[4], o_int, l_int, chip, 0, w_in[0].T, m_w_in[0].T, v_w_in[0].T, "adamw_w_in", tiles=2)
    g_w_in, d_w_in, nm_w_in, nv_w_in = (t.T for t in w_in_t)
    (g_mkv, d_mkv, nm_mkv, nv_mkv), (g_out, d_out, nm_out, nv_out), *up = _sum_adamw_group(
        w_in_t[0],
        [(o_mkv, l_mkv, 0, w_mem_kv[0], m_w_mem_kv[0], v_w_mem_kv[0]),
         (o_out, l_out, 0, w_out[0], m_w_out[0], v_w_out[0]),
         (o_up, l_up, 0, w_up_a[0], m_w_up_a[0], v_w_up_a[0]),
         (o_up, l_up, 1, w_up_b[0], m_w_up_b[0], v_w_up_b[0]),
         (o_up, l_up, 2, w_up_m[0], m_w_up_m[0], v_w_up_m[0])], chip, "adamw_mid_weights")

    (packs,) = _gather_wait(*small[:4], g_out, "small_gather_wait")

    def taps_first(a):
        return a.transpose(1, 0, 2)

    loss_row, small_g, sd, sm, sv = _small_apply(
        packs, [g_pre, taps_first(w_conv), attn_sink, g_mem, g_post],
        [m_g_pre, taps_first(m_w_conv), m_attn_sink, m_g_mem, m_g_post],
        [v_g_pre, taps_first(v_w_conv), v_attn_sink, v_g_mem, v_g_post])
    loss = loss_row[0, 0]
    g_g_pre, g_conv, g_sink_tot, g_g_mem, g_g_post = small_g

    def lead(a):
        return a[None]

    grads = [g_g_pre, lead(g_w_in), taps_first(g_conv), g_sink_tot, g_g_mem, lead(g_mkv), lead(up[0][0]),
             lead(up[1][0]), lead(up[2][0]), lead(g_out), g_g_post]

    def assemble(small, big_in, big_mkv, big_up, big_out):
        return [small[0], lead(big_in), taps_first(small[1]), small[2], small[3], lead(big_mkv), lead(big_up[0]),
                lead(big_up[1]), lead(big_up[2]), lead(big_out), small[4]]

    deltas = assemble(sd, d_w_in, d_mkv, [u[1] for u in up], d_out)
    new_m = assemble(sm, nm_w_in, nm_mkv, [u[2] for u in up], nm_out)
    new_v = assemble(sv, nv_w_in, nv_mkv, [u[3] for u in up], nv_out)
    return (loss, grad_x[None], *grads, *deltas, *new_m, *new_v)
```

```python
import functools

import jax
import jax.numpy as jnp
from jax import lax
from jax.experimental import pallas as pl
from jax.experimental.pallas import tpu as pltpu

F32 = jnp.float32
BF16 = jnp.bfloat16
MESH = pl.DeviceIdType.MESH

N_DEV = 8
D_MODEL = 1024
EPS = 1e-6
ROPE_THETA = 500000.0
ROT_DIM = 16
HEAD_DIM = 64
ATTN_BLOCK = 128
MEM_HEADS = 4
MEM_HEAD_DIM = 128
ATTN_SCALE = HEAD_DIM ** -0.5
MEM_SCALE = MEM_HEAD_DIM ** -0.5

ADAM_LR = 0.001
ADAM_B1 = 0.9
ADAM_B2 = 0.999
ADAM_EPS = 1e-08
ADAM_WD = 0.01
ADAM_STEP = 10

SEG_A = (0, 2048)
SEG_BQ = (2048, 512)
SEG_BKV = (2560, 256)
SEG_BZ = (2816, 512)
SEG_MQ = (3328, 512)
SEG_MZ = (3840, 512)
SEG_G = (4352, 3072)
SEGS = (SEG_A, SEG_BQ, SEG_BKV, SEG_BZ, SEG_MQ, SEG_MZ, SEG_G)
IN_WIDTH = 7424
SHARD_IN = IN_WIDTH // N_DEV

V7X_VMEM_BYTES = 64 * 1024 * 1024
CALL_VMEM_MB = 57
ANY = pl.BlockSpec(memory_space=pl.ANY)


def _params():
    assert CALL_VMEM_MB * 1024 * 1024 < V7X_VMEM_BYTES
    return pltpu.CompilerParams(dimension_semantics=("arbitrary",), vmem_limit_bytes=CALL_VMEM_MB * 1024 * 1024)


def _full(shape):
    zeros = (0,) * len(shape)
    return pl.BlockSpec(shape, lambda i: zeros)


def _rows(tm, width):
    return pl.BlockSpec((tm, width), lambda i: (i, 0))


def _dot(a, b):
    return jnp.dot(a, b, preferred_element_type=F32)


def _dot_nt(a, b):
    return lax.dot_general(a, b, (((1,), (1,)), ((), ())), preferred_element_type=F32)


def _dot_tn(a, b):
    return lax.dot_general(a, b, (((0,), (0,)), ((), ())), preferred_element_type=F32)


def _sigmoid(z):
    return 1.0 / (1.0 + jnp.exp(-z))


def _rope(t, cs, s1, s2):
    return t * cs + pltpu.roll(t, 120, 1) * s1 + pltpu.roll(t, 8, 1) * s2


def _rope_t(d, cs, s1, s2):
    return d * cs + pltpu.roll(d * s1, 8, 1) + pltpu.roll(d * s2, 120, 1)


def _gather_side(s, w_mkv, w_out, w_ups):
    half = ROT_DIM // 2
    inv_freq = jnp.power(jnp.float32(ROPE_THETA), -jnp.arange(half, dtype=F32) * (2.0 / ROT_DIM))
    freq_row = jnp.tile(jnp.concatenate([inv_freq, inv_freq, jnp.zeros((HEAD_DIM - ROT_DIM,), F32)]), 2)[None, :]

    def fn(in_refs, out_refs):
        f_ref, mkv_ref, out_ref, *up_refs = in_refs
        t_ref, mkv_bf, out_bf, up_bf = out_refs
        mkv_bf[...] = mkv_ref[...].astype(BF16)
        out_bf[...] = out_ref[...].astype(BF16)
        for k, up_ref in enumerate(up_refs):
            up_bf[512 * k:512 * k + 512, :] = up_ref[...].astype(BF16)
        pos = lax.broadcasted_iota(jnp.int32, (s, 128), 0).astype(F32)
        d = lax.broadcasted_iota(jnp.int32, (s, 128), 1) & (HEAD_DIM - 1)
        ang = pos * f_ref[...]
        cos, sin = jnp.cos(ang), jnp.sin(ang)
        lo, hi = d < half, (d >= half) & (d < ROT_DIM)
        t_ref[0] = jnp.where(lo | hi, cos, 1.0)
        t_ref[1] = jnp.where(lo, -sin, 0.0)
        t_ref[2] = jnp.where(hi, sin, 0.0)

    return ([freq_row, w_mkv, w_out, *w_ups],
            [jax.ShapeDtypeStruct((3, s, 128), F32), jax.ShapeDtypeStruct(w_mkv.shape, BF16),
             jax.ShapeDtypeStruct(w_out.shape, BF16), jax.ShapeDtypeStruct((1536, 128), BF16)], fn)


def _load_once(pairs, sems):
    @pl.when(pl.program_id(0) == 0)
    def _():
        cps = [pltpu.make_async_copy(src, dst, sems.at[k]) for k, (src, dst) in enumerate(pairs)]
        for cp in cps:
            cp.start()
        for cp in cps:
            cp.wait()


def _my_place():
    x, y, c = lax.axis_index("x"), lax.axis_index("y"), lax.axis_index("c")
    return x, y, c


def _all_gather(arrs, name, splits=None, side=None):
    n = len(arrs)
    if splits is None:
        splits = [[(0, a.shape[0])] for a in arrs]
    pieces = [(a, r0, rn) for a in range(n) for r0, rn in splits[a]]
    n_p = len(pieces)
    side_in, side_out, side_fn = side if side is not None else ((), (), None)
    m, q = len(side_in), len(side_out)

    def body(*refs):
        ins, outs = refs[:n], refs[n + m:2 * n + m]
        send_sems, recv_sems, local_sems = refs[2 * n + m + q:]
        x, y, c = _my_place()
        me, sibling = (x, y, c), (x, y, 1 - c)

        def route(core):
            first = (jnp.bitwise_xor(x, 1 - core), jnp.bitwise_xor(y, core), core)
            second = (jnp.bitwise_xor(x, core), jnp.bitwise_xor(y, 1 - core), core)
            return first, second, (1 - x, 1 - y, core)

        def idx(px, py, pc):
            return 4 * px + 2 * py + pc

        def copy(p, k, block, to, own=False):
            a, r0, rn = pieces[p]
            dst = outs[a].at[idx(*block), pl.ds(r0, rn)]
            return pltpu.make_async_remote_copy(
                src_ref=ins[a].at[pl.ds(r0, rn)] if own else dst, dst_ref=dst,
                send_sem=send_sems.at[p * 7 + k], recv_sem=recv_sems.at[p * 7 + k],
                device_id=to, device_id_type=MESH)

        nbr1, nbr2, diag = route(c)
        mine = [pltpu.make_async_copy(ins[a], outs[a].at[idx(*me)], local_sems.at[a]) for a in range(n)]
        for cp in mine:
            cp.start()
        sent = []
        for p in range(n_p):
            for k, to in enumerate((sibling, nbr1, nbr2)):
                sent.append(copy(p, k, me, to, own=True))
        for cp in sent:
            cp.start()
        if side_fn is not None:
            side_fn(refs[n:n + m], refs[2 * n + m:2 * n + m + q])
        for k_in, block, onward in ((1, nbr1, ((3, nbr2), (4, sibling))), (2, nbr2, ((5, sibling),)),
                                    (3, diag, ((6, sibling),))):
            for p in range(n_p):
                copy(p, k_in, block, me).wait_recv()
                for k_out, to in onward:
                    cp = copy(p, k_out, block, to)
                    cp.start()
                    sent.append(cp)
        s1, s2, sd = route(1 - c)
        for k_in, block in ((0, sibling), (4, s1), (5, s2), (6, sd)):
            for p in range(n_p):
                copy(p, k_in, block, me).wait_recv()
        for cp in sent:
            cp.wait_send()
        for cp in mine:
            cp.wait()

    return pl.pallas_call(
        body, name=name,
        out_shape=[jax.ShapeDtypeStruct((N_DEV,) + a.shape, a.dtype) for a in arrs] + list(side_out),
        in_specs=[ANY] * n + [pl.BlockSpec(memory_space=pltpu.VMEM)] * m,
        out_specs=[ANY] * n + [pl.BlockSpec(memory_space=pltpu.VMEM)] * q,
        scratch_shapes=[pltpu.SemaphoreType.DMA((7 * n_p,)), pltpu.SemaphoreType.DMA((7 * n_p,)),
                        pltpu.SemaphoreType.DMA((n,))],
        compiler_params=pltpu.CompilerParams(vmem_limit_bytes=32 * 1024 * 1024),
    )(*arrs, *side_in)


N_CHIPS = 4


def _sibling_reduce(arr, name):
    _, r, c = arr.shape

    def body(in_ref, out_ref, land, a_buf, b_buf, o_buf, send_sems, recv_sems, local_sems):
        x, y, core = _my_place()
        cps = [pltpu.make_async_remote_copy(
            src_ref=in_ref.at[2 * j + (1 - core)], dst_ref=land.at[j], send_sem=send_sems.at[j],
            recv_sem=recv_sems.at[j], device_id=(x, y, 1 - core), device_id_type=MESH) for j in range(N_CHIPS)]
        for cp in cps:
            cp.start()
        store = None
        for j in range(N_CHIPS):
            mine = pltpu.make_async_copy(in_ref.at[2 * j + core], a_buf, local_sems.at[0])
            mine.start()
            cps[j].wait_recv()
            theirs = pltpu.make_async_copy(land.at[j], b_buf, local_sems.at[1])
            theirs.start()
            mine.wait()
            theirs.wait()
            if store is not None:
                store.wait()
            o_buf[...] = (a_buf[...].astype(F32) + b_buf[...].astype(F32)).astype(BF16)
            store = pltpu.make_async_copy(o_buf, out_ref.at[j], local_sems.at[2])
            store.start()
        store.wait()
        for cp in cps:
            cp.wait_send()

    return pl.pallas_call(
        body, name=name,
        out_shape=[jax.ShapeDtypeStruct((N_CHIPS, r, c), BF16)] * 2,
        in_specs=[ANY], out_specs=[ANY, ANY],
        scratch_shapes=[pltpu.VMEM((r, c), BF16)] * 3
        + [pltpu.SemaphoreType.DMA((N_CHIPS,)), pltpu.SemaphoreType.DMA((N_CHIPS,)), pltpu.SemaphoreType.DMA((3,))],
        compiler_params=pltpu.CompilerParams(vmem_limit_bytes=32 * 1024 * 1024),
    )(arr)[0]


def _sibling_copies(srcs, lands, send_sems, recv_sems):
    x, y, c = _my_place()
    cps = []
    for j in range(N_CHIPS):
        for a in range(len(srcs)):
            k = a * N_CHIPS + j
            cps.append(pltpu.make_async_remote_copy(
                src_ref=srcs[a].at[2 * j + (1 - c)], dst_ref=lands[a].at[j], send_sem=send_sems[k],
                recv_sem=recv_sems[k], device_id=(x, y, 1 - c), device_id_type=MESH))
    return cps


def _pair_add(mine, recv, core, name):
    n = len(mine)

    def body(c_ref, *refs):
        for a in range(n):
            refs[2 * n + a][...] = (refs[a][...].astype(F32) + refs[n + a][...].astype(F32)).astype(BF16)

    def blk(a):
        return (None,) + a.shape[1:]

    grid_spec = pltpu.PrefetchScalarGridSpec(
        num_scalar_prefetch=1, grid=(N_CHIPS,),
        in_specs=[pl.BlockSpec(blk(a), lambda j, c_ref: (2 * j + c_ref[0], 0, 0)) for a in mine]
        + [pl.BlockSpec(blk(a), lambda j, c_ref: (j, 0, 0)) for a in recv],
        out_specs=[pl.BlockSpec(blk(a), lambda j, c_ref: (j, 0, 0)) for a in recv])
    return pl.pallas_call(
        body, name=name, grid_spec=grid_spec,
        out_shape=[jax.ShapeDtypeStruct(a.shape, BF16) for a in recv],
        compiler_params=_params(),
    )(core, *mine, *recv)


HBM = pl.BlockSpec(memory_space=pltpu.HBM)
SEM = pl.BlockSpec(memory_space=pltpu.SEMAPHORE)
N_PEER_CHIPS = 3
TOKEN = (8, 128)


def _chip_copies(srcs, lands, send_sems, recv_sems):
    x, y, c = _my_place()
    my_chip = 2 * x + y
    peers = [(x, 1 - y), (1 - x, y), (1 - x, 1 - y)]
    cps = []
    for k, (px, py) in enumerate(peers):
        for a in range(len(srcs)):
            j = a * N_PEER_CHIPS + k
            cps.append(pltpu.make_async_remote_copy(
                src_ref=srcs[a].at[2 * px + py], dst_ref=lands[a].at[my_chip],
                send_sem=send_sems[j], recv_sem=recv_sems[j],
                device_id=(px, py, c), device_id_type=MESH))
    return cps


N_PEERS = N_DEV - 1


def _gather_copies(srcs, lands, send_sems, recv_sems):
    x, y, c = _my_place()
    me_idx = 4 * x + 2 * y + c
    flips = [(0, 0, 1), (0, 1, 0), (1, 0, 0), (0, 1, 1), (1, 0, 1), (1, 1, 0), (1, 1, 1)]
    cps = []
    for k, (fx, fy, fc) in enumerate(flips):
        peer = ((1 - x) if fx else x, (1 - y) if fy else y, (1 - c) if fc else c)
        for a in range(len(srcs)):
            j = a * N_PEERS + k
            cps.append(pltpu.make_async_remote_copy(
                src_ref=srcs[a], dst_ref=lands[a].at[me_idx], send_sem=send_sems[j], recv_sem=recv_sems[j],
                device_id=peer, device_id_type=MESH))
    return cps


def _split_start(copies, per_array, arrs, lands, name):
    arrs, lands = list(arrs), list(lands)
    n = len(arrs)
    k = n * per_array

    def body(*refs):
        srcs, land_refs = refs[:n], refs[n:2 * n]
        send_sems, recv_sems = refs[2 * n:2 * n + k], refs[2 * n + k:2 * n + 2 * k]
        token = refs[-1]
        for cp in copies(srcs, land_refs, send_sems, recv_sems):
            cp.start()
        token[...] = jnp.zeros_like(token)

    hbm_arrs = [pltpu.with_memory_space_constraint(a, pltpu.HBM) for a in arrs]
    lands = [pltpu.with_memory_space_constraint(a, pltpu.HBM) for a in lands]
    res = pl.pallas_call(
        body, name=name,
        out_shape=[pltpu.SemaphoreType.DMA(())] * (2 * k) + [pltpu.HBM(a.shape, a.dtype) for a in arrs + lands]
        + [jax.ShapeDtypeStruct(TOKEN, F32)],
        in_specs=[HBM] * (2 * n),
        out_specs=[SEM] * (2 * k) + [HBM] * (2 * n) + [pl.BlockSpec(memory_space=pltpu.VMEM)],
        input_output_aliases={a: 2 * k + a for a in range(2 * n)},
        compiler_params=pltpu.CompilerParams(has_side_effects=pltpu.SideEffectType.DATAFLOW_SIDE_EFFECTING),
    )(*hbm_arrs, *lands)
    return res[:k], res[k:2 * k], res[2 * k:2 * k + n], res[2 * k + n:2 * k + 2 * n], res[-1]


def _split_wait(copies, per_array, send_sems, recv_sems, srcs, lands, after, name):
    n = len(srcs)
    k = n * per_array

    def body(*refs):
        src_refs, land_refs = refs[:n], refs[n:2 * n]
        s_sems, r_sems = refs[2 * n:2 * n + k], refs[2 * n + k:2 * n + 2 * k]
        for cp in copies(src_refs, land_refs, s_sems, r_sems):
            cp.wait_send()
            cp.wait_recv()

    res = pl.pallas_call(
        body, name=name,
        out_shape=[pltpu.HBM(a.shape, a.dtype) for a in list(srcs) + list(lands)],
        in_specs=[HBM] * (2 * n) + [SEM] * (2 * k) + [ANY],
        out_specs=[HBM] * (2 * n),
        input_output_aliases={a: a for a in range(2 * n)},
        compiler_params=pltpu.CompilerParams(has_side_effects=pltpu.SideEffectType.DATAFLOW_SIDE_EFFECTING),
    )(*srcs, *lands, *send_sems, *recv_sems, after)
    return res[:n], res[n:]


def _chip_exchange_start(arrs, name):
    return _split_start(_chip_copies, N_PEER_CHIPS, arrs, [lax.empty(a.shape, a.dtype) for a in arrs], name)


def _chip_exchange_wait(send_sems, recv_sems, srcs, lands, after, name):
    return _split_wait(_chip_copies, N_PEER_CHIPS, send_sems, recv_sems, srcs, lands, after, name)


def _gather_start(arrs, me_idx, name):
    lands = [lax.dynamic_update_slice(lax.empty((N_DEV,) + a.shape, a.dtype), a[None], (me_idx, 0, 0)) for a in arrs]
    return _split_start(_gather_copies, N_PEERS, arrs, lands, name)


def _gather_wait(send_sems, recv_sems, srcs, lands, after, name):
    return _split_wait(_gather_copies, N_PEERS, send_sems, recv_sems, srcs, lands, after, name)[1]


def _proj_fwd(after, x, g_pre, w_int, tabs):
    s = x.shape[0]
    tm = min(512, s)

    def body(after_ref, x_ref, g_ref, t_ref, w_hbm,
             h_ref, pa_ref, pq_ref, pkv_ref, pbz_ref, pmq_ref, pmz_ref, pg_ref, w_vm, sems):
        _load_once([(w_hbm, w_vm)], sems)
        xf = x_ref[...]
        r = lax.rsqrt(jnp.mean(xf * xf, axis=-1, keepdims=True) + EPS)
        h = ((xf * r) * g_ref[...]).astype(BF16)
        h_ref[...] = h
        cs, s1, s2 = t_ref[0], t_ref[1], t_ref[2]

        def mm(seg, c0, width):
            return _dot_nt(h, w_vm[seg[0] + c0:seg[0] + c0 + width, :])

        for c0 in range(0, SEG_A[1], 512):
            pa_ref[:, c0:c0 + 512] = mm(SEG_A, c0, 512).astype(BF16)
        q = mm(SEG_BQ, 0, 512)
        for b in range(4):
            pq_ref[:, 128 * b:128 * b + 128] = _rope(q[:, 128 * b:128 * b + 128], cs, s1, s2).astype(BF16)
        kv = mm(SEG_BKV, 0, 256)
        pkv_ref[:, 0:128] = _rope(kv[:, 0:128], cs, s1, s2).astype(BF16)
        pkv_ref[:, 128:256] = kv[:, 128:256].astype(BF16)
        pbz_ref[...] = mm(SEG_BZ, 0, 512).astype(BF16)
        pmq_ref[...] = mm(SEG_MQ, 0, 512).astype(BF16)
        pmz_ref[...] = mm(SEG_MZ, 0, 512).astype(BF16)
        for c0 in range(0, SEG_G[1], 512):
            pg_ref[:, c0:c0 + 512] = mm(SEG_G, c0, 512).astype(BF16)

    widths = (D_MODEL, 2048, 512, 256, 512, 512, 512, 3072)
    return pl.pallas_call(
        body, name="proj_fwd", grid=(s // tm,),
        out_shape=[jax.ShapeDtypeStruct((s, w), BF16) for w in widths],
        in_specs=[_full(TOKEN), _rows(tm, D_MODEL), _full((1, D_MODEL)),
                  pl.BlockSpec((3, tm, 128), lambda i: (0, i, 0)), ANY],
        out_specs=[_rows(tm, w) for w in widths],
        scratch_shapes=[pltpu.VMEM((IN_WIDTH, D_MODEL), BF16), pltpu.SemaphoreType.DMA((1,))],
        compiler_params=_params(),
    )(after, x, g_pre, tabs, w_int)


def _mem_kv_fwd(mem, g_mem, w_mkv):
    m = mem.shape[0]

    def body(mem_ref, g_ref, w_ref, mn_ref, mkv_ref):
        xf = mem_ref[...]
        r = lax.rsqrt(jnp.mean(xf * xf, axis=-1, keepdims=True) + EPS)
        mn = ((xf * r) * g_ref[...]).astype(BF16)
        mn_ref[...] = mn
        mkv_ref[...] = _dot(mn, w_ref[...]).astype(BF16)

    return pl.pallas_call(
        body, name="mem_kv_fwd", grid=(1,),
        out_shape=[jax.ShapeDtypeStruct((m, D_MODEL), BF16)] * 2,
        in_specs=[_full((m, D_MODEL)), _full((1, D_MODEL)), _full((D_MODEL, D_MODEL))],
        out_specs=[_full((m, D_MODEL))] * 2,
        compiler_params=_params(),
    )(mem, g_mem, w_mkv)


def _halo_specs(s, tm, rows, width):
    nblk = s // rows
    prev = pl.BlockSpec((rows, width), lambda i: (jnp.maximum(i * (tm // rows) - 1, 0), 0))
    nxt = pl.BlockSpec((rows, width), lambda i: (jnp.minimum((i + 1) * (tm // rows), nblk - 1), 0))
    return prev, nxt


def _conv_common(pa, cu_prev, cu_next, w, tm):
    b, c, u, z = (pa[:, 512 * k:512 * k + 512] for k in range(4))
    cu = c * u
    row = lax.broadcasted_iota(jnp.int32, (tm, 512), 0)
    cu_m1 = jnp.where(row == 0, cu_prev, pltpu.roll(cu, 1, 0))
    cu_p1 = jnp.where(row == tm - 1, cu_next, pltpu.roll(cu, tm - 1, 0))
    y = cu_m1 * w[0:1] + cu * w[1:2] + cu_p1 * w[2:3]
    sig = _sigmoid(z)
    return b, c, u, z, cu, cu_m1, cu_p1, y, sig, row


def _conv_fwd(pa, w_conv):
    s = pa.shape[0]
    tm = min(512, s)
    nt = s // tm

    def body(pa_ref, pp_ref, pn_ref, w_ref, ya_ref):
        i = pl.program_id(0)
        prev_row = pp_ref[...].astype(F32)[15:16, :]
        next_row = pn_ref[...].astype(F32)[0:1, :]
        b, _, _, z, _, _, _, y, sig, _ = _conv_common(
            pa_ref[...].astype(F32),
            jnp.where(i == 0, 0.0, prev_row[:, 512:1024] * prev_row[:, 1024:1536]),
            jnp.where(i == nt - 1, 0.0, next_row[:, 512:1024] * next_row[:, 1024:1536]), w_ref[...], tm)
        ya_ref[...] = (b * y * (z * sig)).astype(BF16)

    prev, nxt = _halo_specs(s, tm, 16, 2048)
    return pl.pallas_call(
        body, name="conv_fwd", grid=(nt,),
        out_shape=jax.ShapeDtypeStruct((s, 512), BF16),
        in_specs=[_rows(tm, 2048), prev, nxt, _full((3, 512))],
        out_specs=_rows(tm, 512),
        compiler_params=_params(),
    )(pa, pa, pa, w_conv)


def _heads_to_lanes(a, g, row):
    low = row < HEAD_DIM
    parts = []
    for b in (2 * g, 2 * g + 1):
        t = jnp.transpose(a[:, 128 * b:128 * b + 128])
        swapped = pltpu.roll(t, HEAD_DIM, 0)
        if g == 0:
            parts += [jnp.where(low, t, 0.0), jnp.where(low, swapped, 0.0)]
        else:
            parts += [jnp.where(low, 0.0, swapped), jnp.where(low, 0.0, t)]
    return jnp.concatenate(parts, axis=1)


def _lanes_to_heads(t0, t1, row):
    low = row < HEAD_DIM
    blocks = []
    for b in range(4):
        g = b // 2
        tg = (t0, t1)[g]
        je = 2 * (b - 2 * g)
        even, odd = tg[:, 128 * je:128 * je + 128], tg[:, 128 * je + 128:128 * je + 256]
        if g == 0:
            t = jnp.where(low, even, pltpu.roll(odd, HEAD_DIM, 0))
        else:
            t = jnp.where(low, pltpu.roll(even, HEAD_DIM, 0), odd)
        blocks.append(jnp.transpose(t))
    return jnp.concatenate(blocks, axis=1)


WINDOW_KEYS = 3 * ATTN_BLOCK
STACKED = 4 * ATTN_BLOCK
KEY_CHUNK = 32
MAX_BLOCKS_IN_STEP = 8


def _fill_band_bias(bias, nb):
    assert nb >= 2
    c = lax.broadcasted_iota(jnp.int32, (WINDOW_KEYS, STACKED), 0)
    r = lax.broadcasted_iota(jnp.int32, (WINDOW_KEYS, STACKED), 1) & (ATTN_BLOCK - 1)
    band = (c >= r) & (c <= r + 2 * ATTN_BLOCK)
    for v, ok in enumerate((band, band & (c >= ATTN_BLOCK), band & (c < 2 * ATTN_BLOCK))):
        bias[v] = jnp.where(ok, 0.0, -jnp.inf)


def _bias_variant(n, nb):
    return jnp.where(n == 0, 1, jnp.where(n == nb - 1, 2, 0))


def _sink_row(sink_ref, g):
    return jnp.concatenate([jnp.full((1, ATTN_BLOCK), sink_ref[4 * g + j], F32) for j in range(4)], axis=1)


def _softmax_keys_major(sc, bias, variant, sink, e_scr):
    chunks = [pl.ds(k * KEY_CHUNK, KEY_CHUNK) for k in range(WINDOW_KEYS // KEY_CHUNK)]
    rows = [slice(k * KEY_CHUNK, (k + 1) * KEY_CHUNK) for k in range(WINDOW_KEYS // KEY_CHUNK)]
    m_run = jnp.full((KEY_CHUNK, STACKED), -jnp.inf, F32)
    for ck, rw in zip(chunks, rows):
        m_run = jnp.maximum(m_run, sc[rw] + bias[variant, ck, :])
    m = jnp.maximum(jnp.max(m_run, axis=0, keepdims=True), sink)
    l_run = jnp.zeros((KEY_CHUNK, STACKED), F32)
    for ck, rw in zip(chunks, rows):
        e = jnp.exp(sc[rw] + bias[variant, ck, :] - m)
        l_run += e
        e_scr[rw, :] = e.astype(BF16)
    es = jnp.exp(sink - m)
    inv = 1.0 / (jnp.sum(l_run, axis=0, keepdims=True) + es)
    return inv, es * inv


def _fill_padded(kv_ref, kpad, vpad, s):
    zero = jnp.zeros((ATTN_BLOCK, 128), BF16)
    kpad[0:ATTN_BLOCK, :] = zero
    vpad[0:ATTN_BLOCK, :] = zero
    kpad[ATTN_BLOCK + s:2 * ATTN_BLOCK + s, :] = zero
    vpad[ATTN_BLOCK + s:2 * ATTN_BLOCK + s, :] = zero
    kpad[ATTN_BLOCK:ATTN_BLOCK + s, :] = kv_ref[:, 0:128]
    vpad[ATTN_BLOCK:ATTN_BLOCK + s, :] = kv_ref[:, 128:256]


def _attn_fwd(pq, pkv, pbz, sink):
    s = pq.shape[0]
    nb = s // ATTN_BLOCK
    bps = min(MAX_BLOCKS_IN_STEP, nb)

    def body(sink_ref, q_ref, z_ref, kv_ref, yb_ref, kpad, vpad, bias, e_scr):
        i = pl.program_id(0)

        @pl.when(i == 0)
        def _():
            _fill_padded(kv_ref, kpad, vpad, s)
            _fill_band_bias(bias, nb)

        row = lax.broadcasted_iota(jnp.int32, (ATTN_BLOCK, 128), 0)
        for b in range(bps):
            n = i * bps + b
            rows = slice(b * ATTN_BLOCK, (b + 1) * ATTN_BLOCK)
            start = pl.multiple_of(n * ATTN_BLOCK, ATTN_BLOCK)
            kw, vw = kpad[pl.ds(start, WINDOW_KEYS), :], vpad[pl.ds(start, WINDOW_KEYS), :]
            qf = q_ref[rows, :].astype(F32)
            variant = _bias_variant(n, nb)
            outs = []
            for g in range(2):
                e_bg = e_scr.at[2 * b + g]
                qt = (_heads_to_lanes(qf, g, row) * ATTN_SCALE).astype(BF16)
                inv, _ = _softmax_keys_major(_dot(kw, qt), bias, variant, _sink_row(sink_ref, g), e_bg)
                outs.append(_dot_tn(vw, e_bg[...]) * inv)
            attn = _lanes_to_heads(outs[0], outs[1], row)
            z = z_ref[rows, :].astype(F32)
            yb_ref[rows, :] = (attn * (z * _sigmoid(z))).astype(BF16)

    tq = bps * ATTN_BLOCK
    return pl.pallas_call(
        body, name="attn_fwd", grid=(s // tq,),
        out_shape=jax.ShapeDtypeStruct((s, 512), BF16),
        in_specs=[pl.BlockSpec(memory_space=pltpu.SMEM), _rows(tq, 512), _rows(tq, 512), _full((s, 256))],
        out_specs=_rows(tq, 512),
        scratch_shapes=[pltpu.VMEM((s + 2 * ATTN_BLOCK, 128), BF16)] * 2
        + [pltpu.VMEM((3, WINDOW_KEYS, STACKED), F32),
           pltpu.VMEM((2 * bps, WINDOW_KEYS, STACKED), BF16)],
        compiler_params=_params(),
    )(sink, pq, pbz, pkv)


def _mem_softmax_t(q, mk):
    sc = _dot_nt(mk, q) * MEM_SCALE
    e = jnp.exp(sc - jnp.max(sc, axis=0, keepdims=True))
    return e * (1.0 / jnp.sum(e, axis=0, keepdims=True))


def _mem_attn_fwd(pmq, pmz, mkv):
    s = pmq.shape[0]
    m = mkv.shape[0]
    tm = min(512, s)

    def body(q_ref, z_ref, mk_ref, mv_ref, ym_ref):
        z = z_ref[...].astype(F32)
        sz = z * _sigmoid(z)
        for h in range(MEM_HEADS):
            cols = slice(128 * h, 128 * h + 128)
            pt = _mem_softmax_t(q_ref[:, cols], mk_ref[:, cols])
            o = _dot_tn(pt.astype(BF16), mv_ref[:, cols])
            ym_ref[:, cols] = (o * sz[:, cols]).astype(BF16)

    return pl.pallas_call(
        body, name="mem_attn_fwd", grid=(s // tm,),
        out_shape=jax.ShapeDtypeStruct((s, 512), BF16),
        in_specs=[_rows(tm, 512), _rows(tm, 512), pl.BlockSpec((m, 512), lambda i: (0, 0)),
                  pl.BlockSpec((m, 512), lambda i: (0, 1))],
        out_specs=_rows(tm, 512),
        compiler_params=_params(),
    )(pmq, pmz, mkv, mkv)


def _mid(ya, yb, ym, pg, x, target, g_post, w_up, w_out):
    s = x.shape[0]
    tm = min(256, s)
    nt = s // tm

    def body(ya_ref, yb_ref, ym_ref, pg_ref, x_ref, t_ref, gp_ref, wup_hbm, wout_hbm,
             dg_ref, dya_ref, dyb_ref, dym_ref, dy_ref, loss_ref, ggp_ref, mb_ref, dob_ref, du_ref,
             wup_vm, wout_vm, sems):
        i = pl.program_id(0)
        _load_once([(wup_hbm.at[d], wup_vm.at[:, pl.ds(128 * d, 128)]) for d in range(N_DEV)]
                   + [(wout_hbm, wout_vm)], sems)

        @pl.when(i == 0)
        def _():
            loss_ref[...] = jnp.zeros_like(loss_ref)
            ggp_ref[...] = jnp.zeros_like(ggp_ref)

        ys = (ya_ref[...], yb_ref[...], ym_ref[...])
        us = [_dot(ys[k], wup_vm[512 * k:512 * k + 512, :]) for k in range(3)]
        gates = [_sigmoid(pg_ref[:, 1024 * k:1024 * k + 1024].astype(F32)) for k in range(3)]
        merged = gates[0] * us[0] + gates[1] * us[1] + gates[2] * us[2]
        mb = merged.astype(BF16)
        mb_ref[...] = mb
        out = _dot(mb, wout_vm[...])
        r = lax.rsqrt(jnp.mean(out * out, axis=-1, keepdims=True) + EPS)
        on = out * r
        gp = gp_ref[...]
        err = (x_ref[...] + on * gp) - t_ref[...]
        loss_ref[...] += 0.5 * jnp.sum(err * err) * (1.0 / D_MODEL)
        dy = err * (1.0 / D_MODEL)
        dy_ref[...] = dy
        ggp_ref[...] += jnp.sum(dy * on, axis=0, keepdims=True)
        a = dy * gp
        d_out = r * (a - on * jnp.mean(a * on, axis=-1, keepdims=True))
        dob = d_out.astype(BF16)
        dob_ref[...] = dob
        d_merged = _dot_nt(dob, wout_vm[...])
        d_refs = (dya_ref, dyb_ref, dym_ref)
        for k in range(3):
            g = gates[k]
            du_f = d_merged * g
            dg_ref[:, 1024 * k:1024 * k + 1024] = (du_f * us[k] * (1.0 - g)).astype(BF16)
            du = du_f.astype(BF16)
            du_ref[k] = du
            d_refs[k][...] = _dot_nt(du, wup_vm[512 * k:512 * k + 512, :]).astype(BF16)

    return pl.pallas_call(
        body, name="mid", grid=(nt,),
        out_shape=[jax.ShapeDtypeStruct((s, 3072), BF16)] + [jax.ShapeDtypeStruct((s, 512), BF16)] * 3
        + [jax.ShapeDtypeStruct((s, D_MODEL), F32), jax.ShapeDtypeStruct((8, 128), F32),
           jax.ShapeDtypeStruct((1, D_MODEL), F32), jax.ShapeDtypeStruct((s, D_MODEL), BF16),
           jax.ShapeDtypeStruct((s, D_MODEL), BF16), jax.ShapeDtypeStruct((3, s, D_MODEL), BF16)],
        in_specs=[_rows(tm, 512)] * 3 + [_rows(tm, 3072), _rows(tm, D_MODEL), _rows(tm, D_MODEL),
                                         _full((1, D_MODEL)), ANY, ANY],
        out_specs=[_rows(tm, 3072)] + [_rows(tm, 512)] * 3
        + [_rows(tm, D_MODEL), _full((8, 128)), _full((1, D_MODEL)), _rows(tm, D_MODEL), _rows(tm, D_MODEL),
           pl.BlockSpec((3, tm, D_MODEL), lambda i: (0, i, 0))],
        scratch_shapes=[pltpu.VMEM((1536, D_MODEL), BF16), pltpu.VMEM((D_MODEL, D_MODEL), BF16),
                        pltpu.SemaphoreType.DMA((N_DEV + 1,))],
        compiler_params=_params(),
    )(ya, yb, ym, pg, x, target, g_post, w_up, w_out)


def _gw_mid(mb, dob, ys, du):
    s = mb.shape[0]
    tn = 256
    n_out = D_MODEL // tn
    tiles = [(0, c0, 0) for c0 in range(0, D_MODEL, tn)]
    tiles += [(1 + k, c0, 1 + k) for k in range(3) for c0 in range(0, 512, tn)]
    n_t = len(tiles)

    def body(mb_hbm, ya_hbm, yb_hbm, ym_hbm, dob_hbm, du_hbm, out_hbm, up_hbm,
             lhs, rhs, res_out, res_up, in_sems, rhs_sems, out_sems):
        lhs_hbm = (mb_hbm, ya_hbm, yb_hbm, ym_hbm)

        def load(t):
            a, c0, _ = tiles[t]
            return pltpu.make_async_copy(lhs_hbm[a].at[:, pl.ds(c0, tn)], lhs.at[t & 1], in_sems.at[t & 1])

        def load_rhs(g):
            src = dob_hbm if g == 0 else du_hbm.at[g - 1]
            return pltpu.make_async_copy(src, rhs.at[g & 1], rhs_sems.at[g & 1])

        def store(t):
            if t < n_out:
                return pltpu.make_async_copy(res_out.at[t & 1], out_hbm.at[pl.ds(t * tn, tn)], out_sems.at[t & 1])
            rows = pl.ds((t - n_out) * tn, tn)
            return pltpu.make_async_copy(res_up.at[t & 1], up_hbm.at[:, rows, :], out_sems.at[t & 1])

        load_rhs(0).start()
        load(0).start()
        for t, (_, _, g) in enumerate(tiles):
            new_rhs = t == 0 or tiles[t - 1][2] != g
            if t + 1 < n_t:
                load(t + 1).start()
            if new_rhs and g < 3:
                load_rhs(g + 1).start()
            load(t).wait()
            if new_rhs:
                load_rhs(g).wait()
            if t >= 2:
                store(t - 2).wait()
            r = _dot_tn(lhs[t & 1], rhs[g & 1])
            if t < n_out:
                res_out[t & 1] = r.astype(BF16)
            else:
                for d in range(N_DEV):
                    res_up[t & 1, d] = r[:, 128 * d:128 * d + 128].astype(BF16)
            store(t).start()
        store(n_t - 2).wait()
        store(n_t - 1).wait()

    return pl.pallas_call(
        body, name="gw_mid",
        out_shape=[jax.ShapeDtypeStruct((D_MODEL, D_MODEL), BF16), jax.ShapeDtypeStruct((N_DEV, 1536, 128), BF16)],
        in_specs=[ANY] * 6, out_specs=[ANY, ANY],
        scratch_shapes=[pltpu.VMEM((2, s, tn), BF16), pltpu.VMEM((2, s, D_MODEL), BF16),
                        pltpu.VMEM((2, tn, D_MODEL), BF16), pltpu.VMEM((2, N_DEV, tn, 128), BF16),
                        pltpu.SemaphoreType.DMA((2,)), pltpu.SemaphoreType.DMA((2,)),
                        pltpu.SemaphoreType.DMA((2,))],
        compiler_params=pltpu.CompilerParams(vmem_limit_bytes=CALL_VMEM_MB * 1024 * 1024),
    )(mb, *ys, dob, du)


def _conv_bwd(after, pa, dya, w_conv):
    s = pa.shape[0]
    tm = min(512, s)
    nt = s // tm

    def body(after_ref, pa_ref, pp_ref, pn_ref, d_ref, dp_ref, dn_ref, w_ref, da_ref, gw_ref):
        i = pl.program_id(0)
        first, last = i == 0, i == nt - 1

        @pl.when(first)
        def _():
            gw_ref[...] = jnp.zeros_like(gw_ref)

        w = w_ref[...]
        prev_row = pp_ref[...].astype(F32)[15:16, :]
        next_row = pn_ref[...].astype(F32)[0:1, :]
        b, c, u, z, cu, cu_m1, cu_p1, y, sig, row = _conv_common(
            pa_ref[...].astype(F32),
            jnp.where(first, 0.0, prev_row[:, 512:1024] * prev_row[:, 1024:1536]),
            jnp.where(last, 0.0, next_row[:, 512:1024] * next_row[:, 1024:1536]), w, tm)
        sz = z * sig
        dya_t = d_ref[...].astype(F32)
        d_y = dya_t * b * sz

        def halo_dy(p_row, d_row):
            zz = p_row[:, 1536:2048]
            return d_row * p_row[:, 0:512] * (zz * _sigmoid(zz))

        dy_prev = jnp.where(first, 0.0, halo_dy(prev_row, dp_ref[...].astype(F32)[15:16, :]))
        dy_next = jnp.where(last, 0.0, halo_dy(next_row, dn_ref[...].astype(F32)[0:1, :]))
        dy_m1 = jnp.where(row == 0, dy_prev, pltpu.roll(d_y, 1, 0))
        dy_p1 = jnp.where(row == tm - 1, dy_next, pltpu.roll(d_y, tm - 1, 0))
        d_cu = dy_p1 * w[0:1] + d_y * w[1:2] + dy_m1 * w[2:3]
        da_ref[:, 0:512] = (dya_t * y * sz).astype(BF16)
        da_ref[:, 512:1024] = (d_cu * u).astype(BF16)
        da_ref[:, 1024:1536] = (d_cu * c).astype(BF16)
        da_ref[:, 1536:2048] = (dya_t * b * y * (sig + sz * (1.0 - sig))).astype(BF16)
        gw_ref[0:1, :] += jnp.sum(d_y * cu_m1, axis=0, keepdims=True)
        gw_ref[1:2, :] += jnp.sum(d_y * cu, axis=0, keepdims=True)
        gw_ref[2:3, :] += jnp.sum(d_y * cu_p1, axis=0, keepdims=True)

    prev, nxt = _halo_specs(s, tm, 16, 2048)
    dprev, dnxt = _halo_specs(s, tm, 16, 512)
    return pl.pallas_call(
        body, name="conv_bwd", grid=(nt,),
        out_shape=[jax.ShapeDtypeStruct((s, 2048), BF16), jax.ShapeDtypeStruct((8, 512), F32)],
        in_specs=[_full(TOKEN), _rows(tm, 2048), prev, nxt, _rows(tm, 512), dprev, dnxt, _full((3, 512))],
        out_specs=[_rows(tm, 2048), _full((8, 512))],
        compiler_params=_params(),
    )(after, pa, pa, pa, dya, dya, dya, w_conv)


def _attn_bwd(after, pq, pkv, pbz, dyb, sink, tabs):
    s = pq.shape[0]
    nb = s // ATTN_BLOCK
    bps = min(MAX_BLOCKS_IN_STEP, nb)

    def body(sink_ref, after_ref, q_ref, z_ref, d_ref, kv_ref, t_ref,
             dq_ref, dz_ref, dkv_ref, gs_ref, kpad, vpad, dk_acc, dv_acc, bias, e_scr, ds_scr):
        i = pl.program_id(0)

        @pl.when(i == 0)
        def _():
            _fill_padded(kv_ref, kpad, vpad, s)
            _fill_band_bias(bias, nb)
            dk_acc[...] = jnp.zeros_like(dk_acc)
            dv_acc[...] = jnp.zeros_like(dv_acc)
            gs_ref[...] = jnp.zeros_like(gs_ref)

        row = lax.broadcasted_iota(jnp.int32, (ATTN_BLOCK, 128), 0)
        for b in range(bps):
            n = i * bps + b
            rows = slice(b * ATTN_BLOCK, (b + 1) * ATTN_BLOCK)
            start = pl.multiple_of(n * ATTN_BLOCK, ATTN_BLOCK)
            kw, vw = kpad[pl.ds(start, WINDOW_KEYS), :], vpad[pl.ds(start, WINDOW_KEYS), :]
            qf = q_ref[rows, :].astype(F32)
            variant = _bias_variant(n, nb)
            z = z_ref[rows, :].astype(F32)
            sig = _sigmoid(z)
            dyb_t = d_ref[rows, :].astype(F32)
            d_attn = dyb_t * (z * sig)
            outs, dqs = [], []
            dk_w = jnp.zeros((WINDOW_KEYS, 128), F32)
            dv_w = jnp.zeros((WINDOW_KEYS, 128), F32)
            for g in range(2):
                e_bg, ds_bg = e_scr.at[2 * b + g], ds_scr.at[2 * b + g]
                qt = _heads_to_lanes(qf, g, row)
                inv, p_sink = _softmax_keys_major(
                    _dot(kw, (qt * ATTN_SCALE).astype(BF16)), bias, variant, _sink_row(sink_ref, g), e_bg)
                ot = _dot_tn(vw, e_bg[...]) * inv
                outs.append(ot)
                dot_ = _heads_to_lanes(d_attn, g, row)
                delta = jnp.sum(dot_ * ot, axis=0, keepdims=True)
                dpt = _dot(vw, dot_.astype(BF16))
                for k in range(WINDOW_KEYS // KEY_CHUNK):
                    rw = slice(k * KEY_CHUNK, (k + 1) * KEY_CHUNK)
                    ds_bg[rw, :] = (e_bg[rw, :].astype(F32) * (dpt[rw] - delta)).astype(BF16)
                sink_part = p_sink * delta
                for j in range(4):
                    h = 4 * g + j
                    gs_ref[h:h + 1, :] -= jnp.sum(sink_part[:, 128 * j:128 * j + 128])
                dqs.append(_dot_tn(kw, ds_bg[...]) * (inv * ATTN_SCALE))
                dk_w += _dot_nt(ds_bg[...], (qt * inv).astype(BF16)) * ATTN_SCALE
                dv_w += _dot_nt(e_bg[...], (dot_ * inv).astype(BF16))
            dk_acc[pl.ds(start, WINDOW_KEYS), :] += dk_w
            dv_acc[pl.ds(start, WINDOW_KEYS), :] += dv_w
            attn = _lanes_to_heads(outs[0], outs[1], row)
            dz_ref[rows, :] = (dyb_t * attn * (sig * (1.0 + z * (1.0 - sig)))).astype(BF16)
            dq = _lanes_to_heads(dqs[0], dqs[1], row)
            trows = pl.ds(start, ATTN_BLOCK)
            cs, s1, s2 = t_ref[0, trows, :], t_ref[1, trows, :], t_ref[2, trows, :]
            for blk in range(4):
                cols = slice(128 * blk, 128 * blk + 128)
                dq_ref[rows, cols] = _rope_t(dq[:, cols], cs, s1, s2).astype(BF16)

        @pl.when(i == nb // bps - 1)
        def _():
            dk = dk_acc[ATTN_BLOCK:ATTN_BLOCK + s, :]
            dkv_ref[:, 0:128] = _rope_t(dk, t_ref[0], t_ref[1], t_ref[2]).astype(BF16)
            dkv_ref[:, 128:256] = dv_acc[ATTN_BLOCK:ATTN_BLOCK + s, :].astype(BF16)

    tq = bps * ATTN_BLOCK
    tile = _rows(tq, 512)
    return pl.pallas_call(
        body, name="attn_bwd", grid=(s // tq,),
        out_shape=[jax.ShapeDtypeStruct((s, 512), BF16), jax.ShapeDtypeStruct((s, 512), BF16),
                   jax.ShapeDtypeStruct((s, 256), BF16), jax.ShapeDtypeStruct((8, 128), F32)],
        in_specs=[pl.BlockSpec(memory_space=pltpu.SMEM), _full(TOKEN), tile, tile, tile, _full((s, 256)),
                  _full((3, s, 128))],
        out_specs=[tile, tile, _full((s, 256)), _full((8, 128))],
        scratch_shapes=[pltpu.VMEM((s + 2 * ATTN_BLOCK, 128), BF16)] * 2
        + [pltpu.VMEM((s + 2 * ATTN_BLOCK, 128), F32)] * 2
        + [pltpu.VMEM((3, WINDOW_KEYS, STACKED), F32)]
        + [pltpu.VMEM((2 * bps, WINDOW_KEYS, STACKED), BF16)] * 2,
        compiler_params=_params(),
    )(sink, after, pq, pbz, dyb, pkv, tabs)


def _mem_attn_bwd(pmq, pmz, mkv, dym):
    s = pmq.shape[0]
    m = mkv.shape[0]
    tm = min(512, s)

    def body(q_ref, z_ref, d_ref, mk_ref, mv_ref, dq_ref, dz_ref, dmkv_ref):
        @pl.when(pl.program_id(0) == 0)
        def _():
            dmkv_ref[...] = jnp.zeros_like(dmkv_ref)

        z = z_ref[...].astype(F32)
        sig = _sigmoid(z)
        dym_t = d_ref[...].astype(F32)
        d_attn = dym_t * (z * sig)
        dsilu = sig * (1.0 + z * (1.0 - sig))
        for h in range(MEM_HEADS):
            cols = slice(128 * h, 128 * h + 128)
            q, mk, mv = q_ref[:, cols], mk_ref[:, cols], mv_ref[:, cols]
            pt = _mem_softmax_t(q, mk)
            pb = pt.astype(BF16)
            o = _dot_tn(pb, mv)
            dob = d_attn[:, cols].astype(BF16)
            dpt = _dot_nt(mv, dob)
            dst = (pt * (dpt - jnp.sum(pt * dpt, axis=0, keepdims=True))).astype(BF16)
            dq_ref[:, cols] = (_dot_tn(dst, mk) * MEM_SCALE).astype(BF16)
            dz_ref[:, cols] = (dym_t[:, cols] * o * dsilu[:, cols]).astype(BF16)
            dmkv_ref[:, cols] += _dot(dst, q) * MEM_SCALE
            dmkv_ref[:, 512 + 128 * h:512 + 128 * h + 128] += _dot(pb, dob)

    return pl.pallas_call(
        body, name="mem_attn_bwd", grid=(s // tm,),
        out_shape=[jax.ShapeDtypeStruct((s, 512), BF16), jax.ShapeDtypeStruct((s, 512), BF16),
                   jax.ShapeDtypeStruct((m, D_MODEL), F32)],
        in_specs=[_rows(tm, 512), _rows(tm, 512), _rows(tm, 512), pl.BlockSpec((m, 512), lambda i: (0, 0)),
                  pl.BlockSpec((m, 512), lambda i: (0, 1))],
        out_specs=[_rows(tm, 512), _rows(tm, 512), _full((m, D_MODEL))],
        compiler_params=_params(),
    )(pmq, pmz, dym, mkv, mkv)


def _mem_kv_bwd(mem, g_mem, mn, dmkv, w_mkv):
    m = mem.shape[0]

    def body(mem_ref, g_ref, mn_ref, d_ref, w_ref, gw_ref, gg_ref):
        db = d_ref[...].astype(BF16)
        gw_ref[...] = _dot_tn(mn_ref[...], db).astype(BF16)
        d_mn = _dot_nt(db, w_ref[...])
        xf = mem_ref[...]
        r = lax.rsqrt(jnp.mean(xf * xf, axis=-1, keepdims=True) + EPS)
        gg_ref[...] = jnp.sum(d_mn * (xf * r), axis=0, keepdims=True)

    return pl.pallas_call(
        body, name="mem_kv_bwd", grid=(1,),
        out_shape=[jax.ShapeDtypeStruct((D_MODEL, D_MODEL), BF16), jax.ShapeDtypeStruct((1, D_MODEL), F32)],
        in_specs=[_full((m, D_MODEL)), _full((1, D_MODEL)), _full((m, D_MODEL)), _full((m, D_MODEL)),
                  _full((D_MODEL, D_MODEL))],
        out_specs=[_full((D_MODEL, D_MODEL)), _full((1, D_MODEL))],
        compiler_params=_params(),
    )(mem, g_mem, mn, dmkv, w_mkv)


def _dh_bwd(after, dparts, x, dy, g_pre, w_int):
    s = x.shape[0]
    tm = min(512, s)
    n_t = s // tm
    tiled = list(dparts) + [x, dy]
    n_in = len(tiled)

    def body(after_ref, *refs):
        in_hbm = refs[:n_in]
        g_ref, w_hbm, gx_hbm, gg_ref = refs[n_in:n_in + 4]
        bufs = refs[n_in + 4:2 * n_in + 4]
        w_vm, gx_buf, in_sems, out_sems, w_sem = refs[2 * n_in + 4:]

        def loads(t, slot):
            rows = pl.ds(pl.multiple_of(t * tm, tm), tm)
            return [pltpu.make_async_copy(in_hbm[k].at[rows], bufs[k].at[slot], in_sems.at[slot, k])
                    for k in range(n_in)]

        def store(t, slot):
            rows = pl.ds(pl.multiple_of(t * tm, tm), tm)
            return pltpu.make_async_copy(gx_buf.at[slot], gx_hbm.at[rows], out_sems.at[slot])

        w_copy = pltpu.make_async_copy(w_hbm, w_vm, w_sem)
        w_copy.start()
        for cp in loads(0, 0):
            cp.start()
        w_copy.wait()

        def step(t, gg):
            slot = t & 1

            @pl.when(t + 1 < n_t)
            def _():
                for cp in loads(t + 1, 1 - slot):
                    cp.start()

            for cp in loads(t, slot):
                cp.wait()

            @pl.when(t >= 2)
            def _():
                store(t - 2, slot).wait()

            d_h = jnp.zeros((tm, D_MODEL), F32)
            for k, (r0, width) in enumerate(SEGS):
                for c0 in range(0, width, 512):
                    cw = min(512, width - c0)
                    d_h += _dot(bufs[k].at[slot][:, c0:c0 + cw], w_vm[r0 + c0:r0 + c0 + cw, :])
            xf = bufs[n_in - 2][slot]
            r = lax.rsqrt(jnp.mean(xf * xf, axis=-1, keepdims=True) + EPS)
            xn = xf * r
            a = d_h * g_ref[...]
            gx_buf[slot] = r * (a - xn * jnp.mean(a * xn, axis=-1, keepdims=True)) + bufs[n_in - 1][slot]
            store(t, slot).start()
            return gg + jnp.sum(d_h * xn, axis=0, keepdims=True)

        gg_ref[...] = lax.fori_loop(0, n_t, step, jnp.zeros((1, D_MODEL), F32))
        for t in range(max(n_t - 2, 0), n_t):
            store(t, t & 1).wait()

    vmem = pl.BlockSpec(memory_space=pltpu.VMEM)
    return pl.pallas_call(
        body, name="dh_bwd",
        out_shape=[jax.ShapeDtypeStruct((s, D_MODEL), F32), jax.ShapeDtypeStruct((1, D_MODEL), F32)],
        in_specs=[ANY] * (1 + n_in) + [vmem, ANY],
        out_specs=[ANY, vmem],
        scratch_shapes=[pltpu.VMEM((2, tm) + a.shape[1:], a.dtype) for a in tiled]
        + [pltpu.VMEM((IN_WIDTH, D_MODEL), BF16), pltpu.VMEM((2, tm, D_MODEL), F32),
           pltpu.SemaphoreType.DMA((2, n_in)), pltpu.SemaphoreType.DMA((2,)), pltpu.SemaphoreType.DMA(())],
        compiler_params=pltpu.CompilerParams(vmem_limit_bytes=CALL_VMEM_MB * 1024 * 1024),
    )(after, *dparts, x, dy, g_pre, w_int)


def _gw_in(dparts, h):
    s = h.shape[0]
    tn = 256
    tiles = [(a, c0) for a, (_, width) in enumerate(SEGS) for c0 in range(0, width, tn)]
    n_t = len(tiles)
    assert n_t * tn == IN_WIDTH

    def body(*refs):
        d_hbm = refs[:7]
        h_hbm, out_hbm, lhs, h_vm, res, in_sems, out_sems, h_sem = refs[7:]

        def load(a, c0, slot):
            return pltpu.make_async_copy(d_hbm[a].at[:, pl.ds(c0, tn)], lhs.at[slot], in_sems.at[slot])

        def store(t, slot):
            rows = pl.ds(pl.multiple_of(t * tn, tn), tn)
            return pltpu.make_async_copy(res.at[slot], out_hbm.at[rows], out_sems.at[slot])

        def start_load(t, slot):
            for k, (a, c0) in enumerate(tiles):
                @pl.when(t == k)
                def _(a=a, c0=c0):
                    load(a, c0, slot).start()

        h_copy = pltpu.make_async_copy(h_hbm, h_vm, h_sem)
        h_copy.start()
        load(*tiles[0], 0).start()
        h_copy.wait()

        def step(t, carry):
            slot = t & 1

            @pl.when(t + 1 < n_t)
            def _():
                start_load(t + 1, 1 - slot)

            load(*tiles[0], slot).wait()

            @pl.when(t >= 2)
            def _():
                store(t - 2, slot).wait()

            res[slot] = _dot_tn(lhs[slot], h_vm[...]).astype(BF16)
            store(t, slot).start()
            return carry

        lax.fori_loop(0, n_t, step, 0)
        store(n_t - 2, (n_t - 2) & 1).wait()
        store(n_t - 1, (n_t - 1) & 1).wait()

    return pl.pallas_call(
        body, name="gw_in",
        out_shape=jax.ShapeDtypeStruct((IN_WIDTH, D_MODEL), BF16),
        in_specs=[ANY] * 8, out_specs=ANY,
        scratch_shapes=[pltpu.VMEM((2, s, tn), BF16), pltpu.VMEM((s, D_MODEL), BF16),
                        pltpu.VMEM((2, tn, D_MODEL), BF16), pltpu.SemaphoreType.DMA((2,)),
                        pltpu.SemaphoreType.DMA((2,)), pltpu.SemaphoreType.DMA(())],
        compiler_params=pltpu.CompilerParams(vmem_limit_bytes=CALL_VMEM_MB * 1024 * 1024),
    )(*dparts, h)


def _adamw_math(w, g, m, v):
    m2 = ADAM_B1 * m + (1.0 - ADAM_B1) * g
    v2 = ADAM_B2 * v + (1.0 - ADAM_B2) * (g * g)
    m_hat = m2 / (1.0 - ADAM_B1 ** ADAM_STEP)
    v_hat = v2 / (1.0 - ADAM_B2 ** ADAM_STEP)
    delta = -ADAM_LR * (m_hat / (jnp.sqrt(v_hat) + ADAM_EPS) + ADAM_WD * w)
    return delta, m2, v2


def _sum_adamw(after, own, land, chip, block, w, m, v, name, tiles=1):
    r, c = w.shape
    rt = r // tiles

    def body(c_ref, after_ref, own_ref, l1_ref, l2_ref, l3_ref, w_ref, m_ref, v_ref, g_ref, d_ref, m2_ref, v2_ref):
        g = own_ref[...].astype(F32)
        for l_ref in (l1_ref, l2_ref, l3_ref):
            g += l_ref[...].astype(F32)
        g_ref[...] = g
        d_ref[...], m2_ref[...], v2_ref[...] = _adamw_math(w_ref[...], g, m_ref[...], v_ref[...])

    def share(k):
        return pl.BlockSpec((None, rt, c), lambda i, c_ref: (jnp.bitwise_xor(c_ref[0], k), block * tiles + i, 0))

    spec = pl.BlockSpec((rt, c), lambda i, c_ref: (i, 0))
    grid_spec = pltpu.PrefetchScalarGridSpec(
        num_scalar_prefetch=1, grid=(tiles,),
        in_specs=[ANY, share(0), share(1), share(2), share(3)] + [spec] * 3, out_specs=[spec] * 4)
    return pl.pallas_call(
        body, name=name, grid_spec=grid_spec,
        out_shape=[jax.ShapeDtypeStruct((r, c), F32)] * 4,
        compiler_params=_params(),
    )(chip, after, own, land, land, land, w, m, v)


def _sum_adamw_group(after, items, chip, name):
    k = len(items)

    def body(c_ref, after_ref, *refs):
        shares, wmv, outs = refs[:4 * k], refs[4 * k:7 * k], refs[7 * k:]
        for j in range(k):
            g = shares[4 * j][...].astype(F32)
            for l_ref in shares[4 * j + 1:4 * j + 4]:
                g += l_ref[...].astype(F32)
            outs[4 * j][...] = g
            outs[4 * j + 1][...], outs[4 * j + 2][...], outs[4 * j + 3][...] = _adamw_math(
                wmv[3 * j][...], g, wmv[3 * j + 1][...], wmv[3 * j + 2][...])

    def share(shape, block, q):
        return pl.BlockSpec((None,) + shape, lambda i, c_ref: (jnp.bitwise_xor(c_ref[0], q), block, 0))

    in_specs, args = [ANY], [after]
    for own, land, block, w, m, v in items:
        in_specs += [share(w.shape, block, q) for q in range(4)]
        args += [own, land, land, land]
    for own, land, block, w, m, v in items:
        in_specs += [pl.BlockSpec(w.shape, lambda i, c_ref: (0, 0))] * 3
        args += [w, m, v]
    out_specs = [pl.BlockSpec(w.shape, lambda i, c_ref: (0, 0)) for _, _, _, w, _, _ in items for _ in range(4)]
    res = pl.pallas_call(
        body, name=name,
        grid_spec=pltpu.PrefetchScalarGridSpec(num_scalar_prefetch=1, grid=(1,), in_specs=in_specs,
                                               out_specs=out_specs),
        out_shape=[jax.ShapeDtypeStruct(w.shape, F32) for _, _, _, w, _, _ in items for _ in range(4)],
        compiler_params=_params(),
    )(chip, *args)
    return [res[4 * j:4 * j + 4] for j in range(k)]


def _small_pack(parts):
    def pack_body(gpre_ref, gconv_ref, gsink_ref, gmem_ref, gpost_ref, loss_ref, pack):
        lane = lax.broadcasted_iota(jnp.int32, (1, 128), 1)
        sink_row = jnp.zeros((1, 128), F32)
        for h in range(8):
            sink_row = jnp.where(lane == h, gsink_ref[h:h + 1, :], sink_row)
        pack[...] = jnp.zeros_like(pack)
        pack[0:1, :] = gpre_ref[...]
        pack[1:2, :] = gmem_ref[...]
        pack[2:3, :] = gpost_ref[...]
        pack[3:6, 0:512] = gconv_ref[0:3, :]
        pack[6:7, 0:128] = sink_row
        pack[7:8, 0:128] = loss_ref[0:1, :]

    return pl.pallas_call(
        pack_body, name="small_pack", grid=(1,),
        out_shape=jax.ShapeDtypeStruct((8, D_MODEL), F32),
        in_specs=[_full(p.shape) for p in parts], out_specs=_full((8, D_MODEL)),
    )(*parts)


def _small_apply(packs, ws, ms, vs):
    def apply(p_ref, *refs):
        w_refs, m_refs, v_refs = refs[0:5], refs[5:10], refs[10:15]
        loss_out = refs[15]
        g_outs, d_outs, m_outs, v_outs = refs[16:21], refs[21:26], refs[26:31], refs[31:36]
        x, y, c = _my_place()
        tot = p_ref[0]
        for d in range(1, N_DEV):
            tot = tot + p_ref[d]
        conv = pltpu.roll(tot[:, 0:512], (512 - 64 * (4 * x + 2 * y + c)) % 512, 1)[3:6, 0:64]
        grads = (tot[0:1, :], conv, tot[6:7, 0:8], tot[1:2, :], tot[2:3, :])
        loss_out[...] = tot[7:8, 0:128]
        for j in range(5):
            if len(w_refs[j].shape) == 3:
                for k in range(w_refs[j].shape[0]):
                    g_outs[j][k] = grads[j][k:k + 1]
                    d_outs[j][k], m_outs[j][k], v_outs[j][k] = _adamw_math(
                        w_refs[j][k], grads[j][k:k + 1], m_refs[j][k], v_refs[j][k])
                continue
            g_outs[j][...] = grads[j]
            d_outs[j][...], m_outs[j][...], v_outs[j][...] = _adamw_math(
                w_refs[j][...], grads[j], m_refs[j][...], v_refs[j][...])

    specs = [_full(w.shape) for w in ws]
    res = pl.pallas_call(
        apply, name="small_apply", grid=(1,),
        out_shape=[jax.ShapeDtypeStruct((1, 128), F32)] + [jax.ShapeDtypeStruct(w.shape, F32) for w in ws] * 4,
        in_specs=[_full((N_DEV, 8, D_MODEL))] + specs * 3,
        out_specs=[_full((1, 128))] + specs * 4,
    )(packs, *ws, *ms, *vs)
    return res[0], res[1:6], res[6:11], res[11:16], res[16:21]


def kernel(x, mem, g_pre, w_in, w_conv, attn_sink, g_mem, w_mem_kv, w_up_a, w_up_b, w_up_m, w_out, g_post, loss_target, m_g_pre, m_w_in, m_w_conv, m_attn_sink, m_g_mem, m_w_mem_kv, m_w_up_a, m_w_up_b, m_w_up_m, m_w_out, m_g_post, v_g_pre, v_w_in, v_w_conv, v_attn_sink, v_g_mem, v_w_mem_kv, v_w_up_a, v_w_up_b, v_w_up_m, v_w_out, v_g_post):
    s = x.shape[1]
    x2, mem2, tgt2 = x[0], mem[0], loss_target[0]
    me = 4 * lax.axis_index("x") + 2 * lax.axis_index("y") + lax.axis_index("c")

    w_conv_loc = jnp.zeros((8, 128), F32).at[:3, :64].set(w_conv[0])
    w_int_g, w_conv_g, tabs, w_mkv_loc, w_out_loc, w_up_loc = _all_gather(
        [w_in[0].T.astype(BF16), w_conv_loc], "gather_w_in",
        splits=[[(112 * k, 112) for k in range(7)] + [(784, 144)], [(0, 8)]],
        side=_gather_side(s, w_mem_kv[0], w_out[0], (w_up_a[0], w_up_b[0], w_up_m[0])))
    w_int = w_int_g.reshape(IN_WIDTH, D_MODEL)
    w_conv_f = w_conv_g[:, :3, :64].transpose(1, 0, 2).reshape(3, 512)
    late = _gather_start([w_mkv_loc, w_out_loc, w_up_loc], me, "gather_late_start")
    sink = attn_sink[0]

    h, pa, pq, pkv, pbz, pmq, pmz, pg = _proj_fwd(late[4], x2, g_pre, w_int, tabs)
    ya = _conv_fwd(pa, w_conv_f)
    yb = _attn_fwd(pq, pkv, pbz, sink)
    w_mkv_g, w_out_g, w_up_g = _gather_wait(*late[:4], yb, "gather_late_wait")
    w_mkv = w_mkv_g.reshape(D_MODEL, D_MODEL)
    w_out_f = w_out_g.reshape(D_MODEL, D_MODEL)
    mn, mkv = _mem_kv_fwd(mem2, g_mem, w_mkv)
    ym = _mem_attn_fwd(pmq, pmz, mkv)
    dg, dya, dyb, dym, dy, loss_p, gg_post, mb, dob, du = _mid(ya, yb, ym, pg, x2, tgt2, g_post, w_up_g, w_out_f)
    gw_out, gw_up = _gw_mid(mb, dob, (ya, yb, ym), du)

    core = lax.axis_index("c").astype(jnp.int32).reshape(1)
    chip = (2 * lax.axis_index("x") + lax.axis_index("y")).astype(jnp.int32).reshape(1)

    dmq, dmz, dmkv = _mem_attn_bwd(pmq, pmz, mkv, dym)
    gw_mkv, gg_mem = _mem_kv_bwd(mem2, g_mem, mn, dmkv, w_mkv)
    shares1 = [gw_mkv.reshape(N_DEV, 128, D_MODEL), gw_out.reshape(N_DEV, 128, D_MODEL), gw_up]
    sib = _split_start(_sibling_copies, N_CHIPS, shares1,
                       [lax.empty((N_CHIPS,) + a.shape[1:], a.dtype) for a in shares1], "grads_to_sibling_small_start")
    da, gw_conv = _conv_bwd(sib[4], pa, dya, w_conv_f)
    shares1, from_sibling = _split_wait(_sibling_copies, N_CHIPS, *sib[:4], da, "grads_to_sibling_small_wait")
    send1, recv1, srcs1, lands1, token1 = _chip_exchange_start(
        _pair_add(shares1, from_sibling, core, "grads_pair_add_small"), "grads_to_chips_start_small")
    dq, dbz, dkv, g_sink = _attn_bwd(token1, pq, pkv, pbz, dyb, sink, tabs)
    dparts = (da, dq, dkv, dbz, dmq, dmz, dg)
    gw_int = _gw_in(dparts, h)
    send2, recv2, srcs2, lands2, token2 = _chip_exchange_start(
        [_sibling_reduce(gw_int.reshape(N_DEV, SHARD_IN, D_MODEL), "grads_sibling_reduce_w_in")],
        "grads_to_chips_start_w_in")
    grad_x, gg_pre = _dh_bwd(token2, dparts, x2, dy, g_pre, w_int)
    (o_mkv, o_out, o_up, o_int), (l_mkv, l_out, l_up, l_int) = _chip_exchange_wait(
        send1 + send2, recv1 + recv2, srcs1 + srcs2, lands1 + lands2, grad_x, "grads_to_chips_wait")

    small = _gather_start([_small_pack((gg_pre, gw_conv, g_sink, gg_mem, gg_post, loss_p))], me,
                          "small_gather_start")

    w_in_t = _sum_adamw(small[4], o_int, l_int, chip, 0, w_in[0].T, m_w_in[0].T, v_w_in[0].T, "adamw_w_in", tiles=2)
    g_w_in, d_w_in, nm_w_in, nv_w_in = (t.T for t in w_in_t)
    (g_mkv, d_mkv, nm_mkv, nv_mkv), (g_out, d_out, nm_out, nv_out), *up = _sum_adamw_group(
        w_in_t[0],
        [(o_mkv, l_mkv, 0, w_mem_kv[0], m_w_mem_kv[0], v_w_mem_kv[0]),
         (o_out, l_out, 0, w_out[0], m_w_out[0], v_w_out[0]),
         (o_up, l_up, 0, w_up_a[0], m_w_up_a[0], v_w_up_a[0]),
         (o_up, l_up, 1, w_up_b[0], m_w_up_b[0], v_w_up_b[0]),
         (o_up, l_up, 2, w_up_m[0], m_w_up_m[0], v_w_up_m[0])], chip, "adamw_mid_weights")

    (packs,) = _gather_wait(*small[:4], g_out, "small_gather_wait")

    def taps_first(a):
        return a.transpose(1, 0, 2)

    loss_row, small_g, sd, sm, sv = _small_apply(
        packs, [g_pre, taps_first(w_conv), attn_sink, g_mem, g_post],
        [m_g_pre, taps_first(m_w_conv), m_attn_sink, m_g_mem, m_g_post],
        [v_g_pre, taps_first(v_w_conv), v_attn_sink, v_g_mem, v_g_post])
    loss = loss_row[0, 0]
    g_g_pre, g_conv, g_sink_tot, g_g_mem, g_g_post = small_g

    def lead(a):
        return a[None]

    grads = [g_g_pre, lead(g_w_in), taps_first(g_conv), g_sink_tot, g_g_mem, lead(g_mkv), lead(up[0][0]),
             lead(up[1][0]), lead(up[2][0]), lead(g_out), g_g_post]

    def assemble(small, big_in, big_mkv, big_up, big_out):
        return [small[0], lead(big_in), taps_first(small[1]), small[2], small[3], lead(big_mkv), lead(big_up[0]),
                lead(big_up[1]), lead(big_up[2]), lead(big_out), small[4]]

    deltas = assemble(sd, d_w_in, d_mkv, [u[1] for u in up], d_out)
    new_m = assemble(sm, nm_w_in, nm_mkv, [u[2] for u in up], nm_out)
    new_v = assemble(sv, nv_w_in, nv_mkv, [u[3] for u in up], nv_out)
    return (loss, grad_x[None], *grads, *deltas, *new_m, *new_v)
```

```python
import functools

import jax
import jax.numpy as jnp
from jax import lax
from jax.experimental import pallas as pl
from jax.experimental.pallas import tpu as pltpu

F32 = jnp.float32
BF16 = jnp.bfloat16
MESH = pl.DeviceIdType.MESH

N_DEV = 8
D_MODEL = 1024
EPS = 1e-6
ROPE_THETA = 500000.0
ROT_DIM = 16
HEAD_DIM = 64
ATTN_BLOCK = 128
MEM_HEADS = 4
MEM_HEAD_DIM = 128
ATTN_SCALE = HEAD_DIM ** -0.5
MEM_SCALE = MEM_HEAD_DIM ** -0.5

ADAM_LR = 0.001
ADAM_B1 = 0.9
ADAM_B2 = 0.999
ADAM_EPS = 1e-08
ADAM_WD = 0.01
ADAM_STEP = 10

SEG_A = (0, 2048)
SEG_BQ = (2048, 512)
SEG_BKV = (2560, 256)
SEG_BZ = (2816, 512)
SEG_MQ = (3328, 512)
SEG_MZ = (3840, 512)
SEG_G = (4352, 3072)
SEGS = (SEG_A, SEG_BQ, SEG_BKV, SEG_BZ, SEG_MQ, SEG_MZ, SEG_G)
IN_WIDTH = 7424
SHARD_IN = IN_WIDTH // N_DEV

V7X_VMEM_BYTES = 64 * 1024 * 1024
CALL_VMEM_MB = 57
ANY = pl.BlockSpec(memory_space=pl.ANY)


def _params():
    assert CALL_VMEM_MB * 1024 * 1024 < V7X_VMEM_BYTES
    return pltpu.CompilerParams(dimension_semantics=("arbitrary",), vmem_limit_bytes=CALL_VMEM_MB * 1024 * 1024)


def _full(shape):
    zeros = (0,) * len(shape)
    return pl.BlockSpec(shape, lambda i: zeros)


def _rows(tm, width):
    return pl.BlockSpec((tm, width), lambda i: (i, 0))


def _dot(a, b):
    return jnp.dot(a, b, preferred_element_type=F32)


def _dot_nt(a, b):
    return lax.dot_general(a, b, (((1,), (1,)), ((), ())), preferred_element_type=F32)


def _dot_tn(a, b):
    return lax.dot_general(a, b, (((0,), (0,)), ((), ())), preferred_element_type=F32)


def _sigmoid(z):
    return 1.0 / (1.0 + jnp.exp(-z))


def _rope(t, cs, s1, s2):
    return t * cs + pltpu.roll(t, 120, 1) * s1 + pltpu.roll(t, 8, 1) * s2


def _rope_t(d, cs, s1, s2):
    return d * cs + pltpu.roll(d * s1, 8, 1) + pltpu.roll(d * s2, 120, 1)


def _gather_side(s, w_mkv, w_out, w_ups):
    half = ROT_DIM // 2
    inv_freq = jnp.power(jnp.float32(ROPE_THETA), -jnp.arange(half, dtype=F32) * (2.0 / ROT_DIM))
    freq_row = jnp.tile(jnp.concatenate([inv_freq, inv_freq, jnp.zeros((HEAD_DIM - ROT_DIM,), F32)]), 2)[None, :]

    def fn(in_refs, out_refs):
        f_ref, mkv_ref, out_ref, *up_refs = in_refs
        t_ref, mkv_bf, out_bf, up_bf = out_refs
        mkv_bf[...] = mkv_ref[...].astype(BF16)
        out_bf[...] = out_ref[...].astype(BF16)
        for k, up_ref in enumerate(up_refs):
            up_bf[512 * k:512 * k + 512, :] = up_ref[...].astype(BF16)
        pos = lax.broadcasted_iota(jnp.int32, (s, 128), 0).astype(F32)
        d = lax.broadcasted_iota(jnp.int32, (s, 128), 1) & (HEAD_DIM - 1)
        ang = pos * f_ref[...]
        cos, sin = jnp.cos(ang), jnp.sin(ang)
        lo, hi = d < half, (d >= half) & (d < ROT_DIM)
        t_ref[0] = jnp.where(lo | hi, cos, 1.0)
        t_ref[1] = jnp.where(lo, -sin, 0.0)
        t_ref[2] = jnp.where(hi, sin, 0.0)

    return ([freq_row, w_mkv, w_out, *w_ups],
            [jax.ShapeDtypeStruct((3, s, 128), F32), jax.ShapeDtypeStruct(w_mkv.shape, BF16),
             jax.ShapeDtypeStruct(w_out.shape, BF16), jax.ShapeDtypeStruct((1536, 128), BF16)], fn)


def _load_once(pairs, sems):
    @pl.when(pl.program_id(0) == 0)
    def _():
        cps = [pltpu.make_async_copy(src, dst, sems.at[k]) for k, (src, dst) in enumerate(pairs)]
        for cp in cps:
            cp.start()
        for cp in cps:
            cp.wait()


def _my_place():
    x, y, c = lax.axis_index("x"), lax.axis_index("y"), lax.axis_index("c")
    return x, y, c


def _all_gather(arrs, name, splits=None, side=None):
    n = len(arrs)
    if splits is None:
        splits = [[(0, a.shape[0])] for a in arrs]
    pieces = [(a, r0, rn) for a in range(n) for r0, rn in splits[a]]
    n_p = len(pieces)
    side_in, side_out, side_fn = side if side is not None else ((), (), None)
    m, q = len(side_in), len(side_out)

    def body(*refs):
        ins, outs = refs[:n], refs[n + m:2 * n + m]
        send_sems, recv_sems, local_sems = refs[2 * n + m + q:]
        x, y, c = _my_place()
        me, sibling = (x, y, c), (x, y, 1 - c)

        def route(core):
            first = (jnp.bitwise_xor(x, 1 - core), jnp.bitwise_xor(y, core), core)
            second = (jnp.bitwise_xor(x, core), jnp.bitwise_xor(y, 1 - core), core)
            return first, second, (1 - x, 1 - y, core)

        def idx(px, py, pc):
            return 4 * px + 2 * py + pc

        def copy(p, k, block, to, own=False):
            a, r0, rn = pieces[p]
            dst = outs[a].at[idx(*block), pl.ds(r0, rn)]
            return pltpu.make_async_remote_copy(
                src_ref=ins[a].at[pl.ds(r0, rn)] if own else dst, dst_ref=dst,
                send_sem=send_sems.at[p * 7 + k], recv_sem=recv_sems.at[p * 7 + k],
                device_id=to, device_id_type=MESH)

        nbr1, nbr2, diag = route(c)
        mine = [pltpu.make_async_copy(ins[a], outs[a].at[idx(*me)], local_sems.at[a]) for a in range(n)]
        for cp in mine:
            cp.start()
        sent = []
        for p in range(n_p):
            for k, to in enumerate((sibling, nbr1, nbr2)):
                sent.append(copy(p, k, me, to, own=True))
        for cp in sent:
            cp.start()
        if side_fn is not None:
            side_fn(refs[n:n + m], refs[2 * n + m:2 * n + m + q])
        for k_in, block, onward in ((1, nbr1, ((3, nbr2), (4, sibling))), (2, nbr2, ((5, sibling),)),
                                    (3, diag, ((6, sibling),))):
            for p in range(n_p):
                copy(p, k_in, block, me).wait_recv()
                for k_out, to in onward:
                    cp = copy(p, k_out, block, to)
                    cp.start()
                    sent.append(cp)
        s1, s2, sd = route(1 - c)
        for k_in, block in ((0, sibling), (4, s1), (5, s2), (6, sd)):
            for p in range(n_p):
                copy(p, k_in, block, me).wait_recv()
        for cp in sent:
            cp.wait_send()
        for cp in mine:
            cp.wait()

    return pl.pallas_call(
        body, name=name,
        out_shape=[jax.ShapeDtypeStruct((N_DEV,) + a.shape, a.dtype) for a in arrs] + list(side_out),
        in_specs=[ANY] * n + [pl.BlockSpec(memory_space=pltpu.VMEM)] * m,
        out_specs=[ANY] * n + [pl.BlockSpec(memory_space=pltpu.VMEM)] * q,
        scratch_shapes=[pltpu.SemaphoreType.DMA((7 * n_p,)), pltpu.SemaphoreType.DMA((7 * n_p,)),
                        pltpu.SemaphoreType.DMA((n,))],
        compiler_params=pltpu.CompilerParams(vmem_limit_bytes=32 * 1024 * 1024),
    )(*arrs, *side_in)


N_CHIPS = 4


def _sibling_reduce(arr, name):
    _, r, c = arr.shape

    def body(in_ref, out_ref, land, a_buf, b_buf, o_buf, send_sems, recv_sems, local_sems):
        x, y, core = _my_place()
        cps = [pltpu.make_async_remote_copy(
            src_ref=in_ref.at[2 * j + (1 - core)], dst_ref=land.at[j], send_sem=send_sems.at[j],
            recv_sem=recv_sems.at[j], device_id=(x, y, 1 - core), device_id_type=MESH) for j in range(N_CHIPS)]
        for cp in cps:
            cp.start()
        store = None
        for j in range(N_CHIPS):
            mine = pltpu.make_async_copy(in_ref.at[2 * j + core], a_buf, local_sems.at[0])
            mine.start()
            cps[j].wait_recv()
            theirs = pltpu.make_async_copy(land.at[j], b_buf, local_sems.at[1])
            theirs.start()
            mine.wait()
            theirs.wait()
            if store is not None:
                store.wait()
            o_buf[...] = (a_buf[...].astype(F32) + b_buf[...].astype(F32)).astype(BF16)
            store = pltpu.make_async_copy(o_buf, out_ref.at[j], local_sems.at[2])
            store.start()
        store.wait()
        for cp in cps:
            cp.wait_send()

    return pl.pallas_call(
        body, name=name,
        out_shape=[jax.ShapeDtypeStruct((N_CHIPS, r, c), BF16)] * 2,
        in_specs=[ANY], out_specs=[ANY, ANY],
        scratch_shapes=[pltpu.VMEM((r, c), BF16)] * 3
        + [pltpu.SemaphoreType.DMA((N_CHIPS,)), pltpu.SemaphoreType.DMA((N_CHIPS,)), pltpu.SemaphoreType.DMA((3,))],
        compiler_params=pltpu.CompilerParams(vmem_limit_bytes=32 * 1024 * 1024),
    )(arr)[0]


def _sibling_copies(srcs, lands, send_sems, recv_sems):
    x, y, c = _my_place()
    cps = []
    for j in range(N_CHIPS):
        for a in range(len(srcs)):
            k = a * N_CHIPS + j
            cps.append(pltpu.make_async_remote_copy(
                src_ref=srcs[a].at[2 * j + (1 - c)], dst_ref=lands[a].at[j], send_sem=send_sems[k],
                recv_sem=recv_sems[k], device_id=(x, y, 1 - c), device_id_type=MESH))
    return cps


def _pair_add(mine, recv, core, name):
    n = len(mine)

    def body(c_ref, *refs):
        for a in range(n):
            refs[2 * n + a][...] = (refs[a][...].astype(F32) + refs[n + a][...].astype(F32)).astype(BF16)

    def blk(a):
        return (None,) + a.shape[1:]

    grid_spec = pltpu.PrefetchScalarGridSpec(
        num_scalar_prefetch=1, grid=(N_CHIPS,),
        in_specs=[pl.BlockSpec(blk(a), lambda j, c_ref: (2 * j + c_ref[0], 0, 0)) for a in mine]
        + [pl.BlockSpec(blk(a), lambda j, c_ref: (j, 0, 0)) for a in recv],
        out_specs=[pl.BlockSpec(blk(a), lambda j, c_ref: (j, 0, 0)) for a in recv])
    return pl.pallas_call(
        body, name=name, grid_spec=grid_spec,
        out_shape=[jax.ShapeDtypeStruct(a.shape, BF16) for a in recv],
        compiler_params=_params(),
    )(core, *mine, *recv)


HBM = pl.BlockSpec(memory_space=pltpu.HBM)
SEM = pl.BlockSpec(memory_space=pltpu.SEMAPHORE)
N_PEER_CHIPS = 3
TOKEN = (8, 128)


def _chip_copies(srcs, lands, send_sems, recv_sems):
    x, y, c = _my_place()
    my_chip = 2 * x + y
    peers = [(x, 1 - y), (1 - x, y), (1 - x, 1 - y)]
    cps = []
    for k, (px, py) in enumerate(peers):
        for a in range(len(srcs)):
            j = a * N_PEER_CHIPS + k
            cps.append(pltpu.make_async_remote_copy(
                src_ref=srcs[a].at[2 * px + py], dst_ref=lands[a].at[my_chip],
                send_sem=send_sems[j], recv_sem=recv_sems[j],
                device_id=(px, py, c), device_id_type=MESH))
    return cps


N_PEERS = N_DEV - 1


def _gather_copies(srcs, lands, send_sems, recv_sems):
    x, y, c = _my_place()
    me_idx = 4 * x + 2 * y + c
    flips = [(0, 0, 1), (0, 1, 0), (1, 0, 0), (0, 1, 1), (1, 0, 1), (1, 1, 0), (1, 1, 1)]
    cps = []
    for k, (fx, fy, fc) in enumerate(flips):
        peer = ((1 - x) if fx else x, (1 - y) if fy else y, (1 - c) if fc else c)
        for a in range(len(srcs)):
            j = a * N_PEERS + k
            cps.append(pltpu.make_async_remote_copy(
                src_ref=srcs[a], dst_ref=lands[a].at[me_idx], send_sem=send_sems[j], recv_sem=recv_sems[j],
                device_id=peer, device_id_type=MESH))
    return cps


def _split_start(copies, per_array, arrs, lands, name, own_slot=False):
    arrs, lands = list(arrs), list(lands)
    n = len(arrs)
    k = n * per_array

    def body(*refs):
        srcs, land_refs = refs[:n], refs[n:2 * n]
        send_sems, recv_sems = refs[2 * n:2 * n + k], refs[2 * n + k:2 * n + 2 * k]
        token, local_sems = refs[4 * n + 2 * k], refs[4 * n + 2 * k + 1]
        x, y, c = _my_place()
        mine = [pltpu.make_async_copy(srcs[a], land_refs[a].at[4 * x + 2 * y + c], local_sems.at[a])
                for a in range(n)] if own_slot else []
        for cp in mine:
            cp.start()
        for cp in copies(srcs, land_refs, send_sems, recv_sems):
            cp.start()
        token[...] = jnp.zeros_like(token)
        for cp in mine:
            cp.wait()

    hbm_arrs = [pltpu.with_memory_space_constraint(a, pltpu.HBM) for a in arrs]
    lands = [pltpu.with_memory_space_constraint(a, pltpu.HBM) for a in lands]
    res = pl.pallas_call(
        body, name=name,
        out_shape=[pltpu.SemaphoreType.DMA(())] * (2 * k) + [pltpu.HBM(a.shape, a.dtype) for a in arrs + lands]
        + [jax.ShapeDtypeStruct(TOKEN, F32)],
        in_specs=[HBM] * (2 * n),
        out_specs=[SEM] * (2 * k) + [HBM] * (2 * n) + [pl.BlockSpec(memory_space=pltpu.VMEM)],
        input_output_aliases={a: 2 * k + a for a in range(2 * n)},
        scratch_shapes=[pltpu.SemaphoreType.DMA((n,))],
        compiler_params=pltpu.CompilerParams(has_side_effects=pltpu.SideEffectType.DATAFLOW_SIDE_EFFECTING),
    )(*hbm_arrs, *lands)
    return res[:k], res[k:2 * k], res[2 * k:2 * k + n], res[2 * k + n:2 * k + 2 * n], res[-1]


def _split_wait(copies, per_array, send_sems, recv_sems, srcs, lands, after, name):
    n = len(srcs)
    k = n * per_array

    def body(*refs):
        src_refs, land_refs = refs[:n], refs[n:2 * n]
        s_sems, r_sems = refs[2 * n:2 * n + k], refs[2 * n + k:2 * n + 2 * k]
        for cp in copies(src_refs, land_refs, s_sems, r_sems):
            cp.wait_send()
            cp.wait_recv()

    res = pl.pallas_call(
        body, name=name,
        out_shape=[pltpu.HBM(a.shape, a.dtype) for a in list(srcs) + list(lands)],
        in_specs=[HBM] * (2 * n) + [SEM] * (2 * k) + [ANY],
        out_specs=[HBM] * (2 * n),
        input_output_aliases={a: a for a in range(2 * n)},
        compiler_params=pltpu.CompilerParams(has_side_effects=pltpu.SideEffectType.DATAFLOW_SIDE_EFFECTING),
    )(*srcs, *lands, *send_sems, *recv_sems, after)
    return res[:n], res[n:]


def _chip_exchange_start(arrs, name):
    return _split_start(_chip_copies, N_PEER_CHIPS, arrs, [lax.empty(a.shape, a.dtype) for a in arrs], name)


def _chip_exchange_wait(send_sems, recv_sems, srcs, lands, after, name):
    return _split_wait(_chip_copies, N_PEER_CHIPS, send_sems, recv_sems, srcs, lands, after, name)


def _gather_start(arrs, name):
    lands = [lax.empty((N_DEV,) + a.shape, a.dtype) for a in arrs]
    return _split_start(_gather_copies, N_PEERS, arrs, lands, name, own_slot=True)


def _gather_wait(send_sems, recv_sems, srcs, lands, after, name):
    return _split_wait(_gather_copies, N_PEERS, send_sems, recv_sems, srcs, lands, after, name)[1]


def _proj_fwd(after, x, g_pre, w_int, tabs):
    s = x.shape[0]
    tm = min(512, s)

    def body(after_ref, x_ref, g_ref, t_ref, w_hbm,
             h_ref, pa_ref, pq_ref, pkv_ref, pbz_ref, pmq_ref, pmz_ref, pg_ref, w_vm, sems):
        _load_once([(w_hbm, w_vm)], sems)
        xf = x_ref[...]
        r = lax.rsqrt(jnp.mean(xf * xf, axis=-1, keepdims=True) + EPS)
        h = ((xf * r) * g_ref[...]).astype(BF16)
        h_ref[...] = h
        cs, s1, s2 = t_ref[0], t_ref[1], t_ref[2]

        def mm(seg, c0, width):
            return _dot_nt(h, w_vm[seg[0] + c0:seg[0] + c0 + width, :])

        for c0 in range(0, SEG_A[1], 512):
            pa_ref[:, c0:c0 + 512] = mm(SEG_A, c0, 512).astype(BF16)
        q = mm(SEG_BQ, 0, 512)
        for b in range(4):
            pq_ref[:, 128 * b:128 * b + 128] = _rope(q[:, 128 * b:128 * b + 128], cs, s1, s2).astype(BF16)
        kv = mm(SEG_BKV, 0, 256)
        pkv_ref[:, 0:128] = _rope(kv[:, 0:128], cs, s1, s2).astype(BF16)
        pkv_ref[:, 128:256] = kv[:, 128:256].astype(BF16)
        pbz_ref[...] = mm(SEG_BZ, 0, 512).astype(BF16)
        pmq_ref[...] = mm(SEG_MQ, 0, 512).astype(BF16)
        pmz_ref[...] = mm(SEG_MZ, 0, 512).astype(BF16)
        for c0 in range(0, SEG_G[1], 512):
            pg_ref[:, c0:c0 + 512] = mm(SEG_G, c0, 512).astype(BF16)

    widths = (D_MODEL, 2048, 512, 256, 512, 512, 512, 3072)
    return pl.pallas_call(
        body, name="proj_fwd", grid=(s // tm,),
        out_shape=[jax.ShapeDtypeStruct((s, w), BF16) for w in widths],
        in_specs=[_full(TOKEN), _rows(tm, D_MODEL), _full((1, D_MODEL)),
                  pl.BlockSpec((3, tm, 128), lambda i: (0, i, 0)), ANY],
        out_specs=[_rows(tm, w) for w in widths],
        scratch_shapes=[pltpu.VMEM((IN_WIDTH, D_MODEL), BF16), pltpu.SemaphoreType.DMA((1,))],
        compiler_params=_params(),
    )(after, x, g_pre, tabs, w_int)


def _mem_kv_fwd(mem, g_mem, w_mkv):
    m = mem.shape[0]

    def body(mem_ref, g_ref, w_ref, mn_ref, mkv_ref):
        xf = mem_ref[...]
        r = lax.rsqrt(jnp.mean(xf * xf, axis=-1, keepdims=True) + EPS)
        mn = ((xf * r) * g_ref[...]).astype(BF16)
        mn_ref[...] = mn
        mkv_ref[...] = _dot(mn, w_ref[...]).astype(BF16)

    return pl.pallas_call(
        body, name="mem_kv_fwd", grid=(1,),
        out_shape=[jax.ShapeDtypeStruct((m, D_MODEL), BF16)] * 2,
        in_specs=[_full((m, D_MODEL)), _full((1, D_MODEL)), _full((D_MODEL, D_MODEL))],
        out_specs=[_full((m, D_MODEL))] * 2,
        compiler_params=_params(),
    )(mem, g_mem, w_mkv)


def _halo_specs(s, tm, rows, width):
    nblk = s // rows
    prev = pl.BlockSpec((rows, width), lambda i: (jnp.maximum(i * (tm // rows) - 1, 0), 0))
    nxt = pl.BlockSpec((rows, width), lambda i: (jnp.minimum((i + 1) * (tm // rows), nblk - 1), 0))
    return prev, nxt


def _conv_common(pa, cu_prev, cu_next, w, tm):
    b, c, u, z = (pa[:, 512 * k:512 * k + 512] for k in range(4))
    cu = c * u
    row = lax.broadcasted_iota(jnp.int32, (tm, 512), 0)
    cu_m1 = jnp.where(row == 0, cu_prev, pltpu.roll(cu, 1, 0))
    cu_p1 = jnp.where(row == tm - 1, cu_next, pltpu.roll(cu, tm - 1, 0))
    y = cu_m1 * w[0:1] + cu * w[1:2] + cu_p1 * w[2:3]
    sig = _sigmoid(z)
    return b, c, u, z, cu, cu_m1, cu_p1, y, sig, row


def _conv_fwd(pa, w_conv):
    s = pa.shape[0]
    tm = min(512, s)
    nt = s // tm

    def body(pa_ref, pp_ref, pn_ref, w_ref, ya_ref):
        i = pl.program_id(0)
        prev_row = pp_ref[...].astype(F32)[15:16, :]
        next_row = pn_ref[...].astype(F32)[0:1, :]
        b, _, _, z, _, _, _, y, sig, _ = _conv_common(
            pa_ref[...].astype(F32),
            jnp.where(i == 0, 0.0, prev_row[:, 512:1024] * prev_row[:, 1024:1536]),
            jnp.where(i == nt - 1, 0.0, next_row[:, 512:1024] * next_row[:, 1024:1536]), w_ref[...], tm)
        ya_ref[...] = (b * y * (z * sig)).astype(BF16)

    prev, nxt = _halo_specs(s, tm, 16, 2048)
    return pl.pallas_call(
        body, name="conv_fwd", grid=(nt,),
        out_shape=jax.ShapeDtypeStruct((s, 512), BF16),
        in_specs=[_rows(tm, 2048), prev, nxt, _full((3, 512))],
        out_specs=_rows(tm, 512),
        compiler_params=_params(),
    )(pa, pa, pa, w_conv)


def _heads_to_lanes(a, g, row):
    low = row < HEAD_DIM
    parts = []
    for b in (2 * g, 2 * g + 1):
        t = jnp.transpose(a[:, 128 * b:128 * b + 128])
        swapped = pltpu.roll(t, HEAD_DIM, 0)
        if g == 0:
            parts += [jnp.where(low, t, 0.0), jnp.where(low, swapped, 0.0)]
        else:
            parts += [jnp.where(low, 0.0, swapped), jnp.where(low, 0.0, t)]
    return jnp.concatenate(parts, axis=1)


def _lanes_to_heads(t0, t1, row):
    low = row < HEAD_DIM
    blocks = []
    for b in range(4):
        g = b // 2
        tg = (t0, t1)[g]
        je = 2 * (b - 2 * g)
        even, odd = tg[:, 128 * je:128 * je + 128], tg[:, 128 * je + 128:128 * je + 256]
        if g == 0:
            t = jnp.where(low, even, pltpu.roll(odd, HEAD_DIM, 0))
        else:
            t = jnp.where(low, pltpu.roll(even, HEAD_DIM, 0), odd)
        blocks.append(jnp.transpose(t))
    return jnp.concatenate(blocks, axis=1)


WINDOW_KEYS = 3 * ATTN_BLOCK
STACKED = 4 * ATTN_BLOCK
KEY_CHUNK = 32
MAX_BLOCKS_IN_STEP = 8


def _fill_band_bias(bias, nb):
    assert nb >= 2
    c = lax.broadcasted_iota(jnp.int32, (WINDOW_KEYS, STACKED), 0)
    r = lax.broadcasted_iota(jnp.int32, (WINDOW_KEYS, STACKED), 1) & (ATTN_BLOCK - 1)
    band = (c >= r) & (c <= r + 2 * ATTN_BLOCK)
    for v, ok in enumerate((band, band & (c >= ATTN_BLOCK), band & (c < 2 * ATTN_BLOCK))):
        bias[v] = jnp.where(ok, 0.0, -jnp.inf)


def _bias_variant(n, nb):
    return jnp.where(n == 0, 1, jnp.where(n == nb - 1, 2, 0))


def _sink_row(sink_ref, g):
    return jnp.concatenate([jnp.full((1, ATTN_BLOCK), sink_ref[4 * g + j], F32) for j in range(4)], axis=1)


def _softmax_keys_major(sc, bias, variant, sink, e_scr):
    chunks = [pl.ds(k * KEY_CHUNK, KEY_CHUNK) for k in range(WINDOW_KEYS // KEY_CHUNK)]
    rows = [slice(k * KEY_CHUNK, (k + 1) * KEY_CHUNK) for k in range(WINDOW_KEYS // KEY_CHUNK)]
    m_run = jnp.full((KEY_CHUNK, STACKED), -jnp.inf, F32)
    for ck, rw in zip(chunks, rows):
        m_run = jnp.maximum(m_run, sc[rw] + bias[variant, ck, :])
    m = jnp.maximum(jnp.max(m_run, axis=0, keepdims=True), sink)
    l_run = jnp.zeros((KEY_CHUNK, STACKED), F32)
    for ck, rw in zip(chunks, rows):
        e = jnp.exp(sc[rw] + bias[variant, ck, :] - m)
        l_run += e
        e_scr[rw, :] = e.astype(BF16)
    es = jnp.exp(sink - m)
    inv = 1.0 / (jnp.sum(l_run, axis=0, keepdims=True) + es)
    return inv, es * inv


def _fill_padded(kv_ref, kpad, vpad, s):
    zero = jnp.zeros((ATTN_BLOCK, 128), BF16)
    kpad[0:ATTN_BLOCK, :] = zero
    vpad[0:ATTN_BLOCK, :] = zero
    kpad[ATTN_BLOCK + s:2 * ATTN_BLOCK + s, :] = zero
    vpad[ATTN_BLOCK + s:2 * ATTN_BLOCK + s, :] = zero
    kpad[ATTN_BLOCK:ATTN_BLOCK + s, :] = kv_ref[:, 0:128]
    vpad[ATTN_BLOCK:ATTN_BLOCK + s, :] = kv_ref[:, 128:256]


def _attn_fwd(pq, pkv, pbz, sink):
    s = pq.shape[0]
    nb = s // ATTN_BLOCK
    bps = min(MAX_BLOCKS_IN_STEP, nb)

    def body(sink_ref, q_ref, z_ref, kv_ref, yb_ref, kpad, vpad, bias, e_scr):
        i = pl.program_id(0)

        @pl.when(i == 0)
        def _():
            _fill_padded(kv_ref, kpad, vpad, s)
            _fill_band_bias(bias, nb)

        row = lax.broadcasted_iota(jnp.int32, (ATTN_BLOCK, 128), 0)
        for b in range(bps):
            n = i * bps + b
            rows = slice(b * ATTN_BLOCK, (b + 1) * ATTN_BLOCK)
            start = pl.multiple_of(n * ATTN_BLOCK, ATTN_BLOCK)
            kw, vw = kpad[pl.ds(start, WINDOW_KEYS), :], vpad[pl.ds(start, WINDOW_KEYS), :]
            qf = q_ref[rows, :].astype(F32)
            variant = _bias_variant(n, nb)
            outs = []
            for g in range(2):
                e_bg = e_scr.at[2 * b + g]
                qt = (_heads_to_lanes(qf, g, row) * ATTN_SCALE).astype(BF16)
                inv, _ = _softmax_keys_major(_dot(kw, qt), bias, variant, _sink_row(sink_ref, g), e_bg)
                outs.append(_dot_tn(vw, e_bg[...]) * inv)
            attn = _lanes_to_heads(outs[0], outs[1], row)
            z = z_ref[rows, :].astype(F32)
            yb_ref[rows, :] = (attn * (z * _sigmoid(z))).astype(BF16)

    tq = bps * ATTN_BLOCK
    return pl.pallas_call(
        body, name="attn_fwd", grid=(s // tq,),
        out_shape=jax.ShapeDtypeStruct((s, 512), BF16),
        in_specs=[pl.BlockSpec(memory_space=pltpu.SMEM), _rows(tq, 512), _rows(tq, 512), _full((s, 256))],
        out_specs=_rows(tq, 512),
        scratch_shapes=[pltpu.VMEM((s + 2 * ATTN_BLOCK, 128), BF16)] * 2
        + [pltpu.VMEM((3, WINDOW_KEYS, STACKED), F32),
           pltpu.VMEM((2 * bps, WINDOW_KEYS, STACKED), BF16)],
        compiler_params=_params(),
    )(sink, pq, pbz, pkv)


def _mem_softmax_t(q, mk):
    sc = _dot_nt(mk, q) * MEM_SCALE
    e = jnp.exp(sc - jnp.max(sc, axis=0, keepdims=True))
    return e * (1.0 / jnp.sum(e, axis=0, keepdims=True))


def _mem_attn_fwd(pmq, pmz, mkv):
    s = pmq.shape[0]
    m = mkv.shape[0]
    tm = min(512, s)

    def body(q_ref, z_ref, mk_ref, mv_ref, ym_ref):
        z = z_ref[...].astype(F32)
        sz = z * _sigmoid(z)
        for h in range(MEM_HEADS):
            cols = slice(128 * h, 128 * h + 128)
            pt = _mem_softmax_t(q_ref[:, cols], mk_ref[:, cols])
            o = _dot_tn(pt.astype(BF16), mv_ref[:, cols])
            ym_ref[:, cols] = (o * sz[:, cols]).astype(BF16)

    return pl.pallas_call(
        body, name="mem_attn_fwd", grid=(s // tm,),
        out_shape=jax.ShapeDtypeStruct((s, 512), BF16),
        in_specs=[_rows(tm, 512), _rows(tm, 512), pl.BlockSpec((m, 512), lambda i: (0, 0)),
                  pl.BlockSpec((m, 512), lambda i: (0, 1))],
        out_specs=_rows(tm, 512),
        compiler_params=_params(),
    )(pmq, pmz, mkv, mkv)


def _mid(ya, yb, ym, pg, x, target, g_post, w_up, w_out):
    s = x.shape[0]
    tm = min(256, s)
    nt = s // tm

    def body(ya_ref, yb_ref, ym_ref, pg_ref, x_ref, t_ref, gp_ref, wup_hbm, wout_hbm,
             dg_ref, dya_ref, dyb_ref, dym_ref, dy_ref, loss_ref, ggp_ref, mb_ref, dob_ref, du_ref,
             wup_vm, wout_vm, sems):
        i = pl.program_id(0)
        _load_once([(wup_hbm.at[d], wup_vm.at[:, pl.ds(128 * d, 128)]) for d in range(N_DEV)]
                   + [(wout_hbm, wout_vm)], sems)

        @pl.when(i == 0)
        def _():
            loss_ref[...] = jnp.zeros_like(loss_ref)
            ggp_ref[...] = jnp.zeros_like(ggp_ref)

        ys = (ya_ref[...], yb_ref[...], ym_ref[...])
        us = [_dot(ys[k], wup_vm[512 * k:512 * k + 512, :]) for k in range(3)]
        gates = [_sigmoid(pg_ref[:, 1024 * k:1024 * k + 1024].astype(F32)) for k in range(3)]
        merged = gates[0] * us[0] + gates[1] * us[1] + gates[2] * us[2]
        mb = merged.astype(BF16)
        mb_ref[...] = mb
        out = _dot(mb, wout_vm[...])
        r = lax.rsqrt(jnp.mean(out * out, axis=-1, keepdims=True) + EPS)
        on = out * r
        gp = gp_ref[...]
        err = (x_ref[...] + on * gp) - t_ref[...]
        loss_ref[...] += 0.5 * jnp.sum(err * err) * (1.0 / D_MODEL)
        dy = err * (1.0 / D_MODEL)
        dy_ref[...] = dy
        ggp_ref[...] += jnp.sum(dy * on, axis=0, keepdims=True)
        a = dy * gp
        d_out = r * (a - on * jnp.mean(a * on, axis=-1, keepdims=True))
        dob = d_out.astype(BF16)
        dob_ref[...] = dob
        d_merged = _dot_nt(dob, wout_vm[...])
        d_refs = (dya_ref, dyb_ref, dym_ref)
        for k in range(3):
            g = gates[k]
            du_f = d_merged * g
            dg_ref[:, 1024 * k:1024 * k + 1024] = (du_f * us[k] * (1.0 - g)).astype(BF16)
            du = du_f.astype(BF16)
            du_ref[k] = du
            d_refs[k][...] = _dot_nt(du, wup_vm[512 * k:512 * k + 512, :]).astype(BF16)

    return pl.pallas_call(
        body, name="mid", grid=(nt,),
        out_shape=[jax.ShapeDtypeStruct((s, 3072), BF16)] + [jax.ShapeDtypeStruct((s, 512), BF16)] * 3
        + [jax.ShapeDtypeStruct((s, D_MODEL), F32), jax.ShapeDtypeStruct((8, 128), F32),
           jax.ShapeDtypeStruct((1, D_MODEL), F32), jax.ShapeDtypeStruct((s, D_MODEL), BF16),
           jax.ShapeDtypeStruct((s, D_MODEL), BF16), jax.ShapeDtypeStruct((3, s, D_MODEL), BF16)],
        in_specs=[_rows(tm, 512)] * 3 + [_rows(tm, 3072), _rows(tm, D_MODEL), _rows(tm, D_MODEL),
                                         _full((1, D_MODEL)), ANY, ANY],
        out_specs=[_rows(tm, 3072)] + [_rows(tm, 512)] * 3
        + [_rows(tm, D_MODEL), _full((8, 128)), _full((1, D_MODEL)), _rows(tm, D_MODEL), _rows(tm, D_MODEL),
           pl.BlockSpec((3, tm, D_MODEL), lambda i: (0, i, 0))],
        scratch_shapes=[pltpu.VMEM((1536, D_MODEL), BF16), pltpu.VMEM((D_MODEL, D_MODEL), BF16),
                        pltpu.SemaphoreType.DMA((N_DEV + 1,))],
        compiler_params=_params(),
    )(ya, yb, ym, pg, x, target, g_post, w_up, w_out)


def _gw_mid(mb, dob, ys, du):
    s = mb.shape[0]
    tn = 256
    n_out = D_MODEL // tn
    tiles = [(0, c0, 0) for c0 in range(0, D_MODEL, tn)]
    tiles += [(1 + k, c0, 1 + k) for k in range(3) for c0 in range(0, 512, tn)]
    n_t = len(tiles)

    def body(mb_hbm, ya_hbm, yb_hbm, ym_hbm, dob_hbm, du_hbm, out_hbm, up_hbm,
             lhs, rhs, res_out, res_up, in_sems, rhs_sems, out_sems):
        lhs_hbm = (mb_hbm, ya_hbm, yb_hbm, ym_hbm)

        def load(t):
            a, c0, _ = tiles[t]
            return pltpu.make_async_copy(lhs_hbm[a].at[:, pl.ds(c0, tn)], lhs.at[t & 1], in_sems.at[t & 1])

        def load_rhs(g):
            src = dob_hbm if g == 0 else du_hbm.at[g - 1]
            return pltpu.make_async_copy(src, rhs.at[g & 1], rhs_sems.at[g & 1])

        def store(t):
            if t < n_out:
                return pltpu.make_async_copy(res_out.at[t & 1], out_hbm.at[pl.ds(t * tn, tn)], out_sems.at[t & 1])
            rows = pl.ds((t - n_out) * tn, tn)
            return pltpu.make_async_copy(res_up.at[t & 1], up_hbm.at[:, rows, :], out_sems.at[t & 1])

        load_rhs(0).start()
        load(0).start()
        for t, (_, _, g) in enumerate(tiles):
            new_rhs = t == 0 or tiles[t - 1][2] != g
            if t + 1 < n_t:
                load(t + 1).start()
            if new_rhs and g < 3:
                load_rhs(g + 1).start()
            load(t).wait()
            if new_rhs:
                load_rhs(g).wait()
            if t >= 2:
                store(t - 2).wait()
            r = _dot_tn(lhs[t & 1], rhs[g & 1])
            if t < n_out:
                res_out[t & 1] = r.astype(BF16)
            else:
                for d in range(N_DEV):
                    res_up[t & 1, d] = r[:, 128 * d:128 * d + 128].astype(BF16)
            store(t).start()
        store(n_t - 2).wait()
        store(n_t - 1).wait()

    return pl.pallas_call(
        body, name="gw_mid",
        out_shape=[jax.ShapeDtypeStruct((D_MODEL, D_MODEL), BF16), jax.ShapeDtypeStruct((N_DEV, 1536, 128), BF16)],
        in_specs=[ANY] * 6, out_specs=[ANY, ANY],
        scratch_shapes=[pltpu.VMEM((2, s, tn), BF16), pltpu.VMEM((2, s, D_MODEL), BF16),
                        pltpu.VMEM((2, tn, D_MODEL), BF16), pltpu.VMEM((2, N_DEV, tn, 128), BF16),
                        pltpu.SemaphoreType.DMA((2,)), pltpu.SemaphoreType.DMA((2,)),
                        pltpu.SemaphoreType.DMA((2,))],
        compiler_params=pltpu.CompilerParams(vmem_limit_bytes=CALL_VMEM_MB * 1024 * 1024),
    )(mb, *ys, dob, du)


def _conv_bwd(after, pa, dya, w_conv):
    s = pa.shape[0]
    tm = min(512, s)
    nt = s // tm

    def body(after_ref, pa_ref, pp_ref, pn_ref, d_ref, dp_ref, dn_ref, w_ref, da_ref, gw_ref):
        i = pl.program_id(0)
        first, last = i == 0, i == nt - 1

        @pl.when(first)
        def _():
            gw_ref[...] = jnp.zeros_like(gw_ref)

        w = w_ref[...]
        prev_row = pp_ref[...].astype(F32)[15:16, :]
        next_row = pn_ref[...].astype(F32)[0:1, :]
        b, c, u, z, cu, cu_m1, cu_p1, y, sig, row = _conv_common(
            pa_ref[...].astype(F32),
            jnp.where(first, 0.0, prev_row[:, 512:1024] * prev_row[:, 1024:1536]),
            jnp.where(last, 0.0, next_row[:, 512:1024] * next_row[:, 1024:1536]), w, tm)
        sz = z * sig
        dya_t = d_ref[...].astype(F32)
        d_y = dya_t * b * sz

        def halo_dy(p_row, d_row):
            zz = p_row[:, 1536:2048]
            return d_row * p_row[:, 0:512] * (zz * _sigmoid(zz))

        dy_prev = jnp.where(first, 0.0, halo_dy(prev_row, dp_ref[...].astype(F32)[15:16, :]))
        dy_next = jnp.where(last, 0.0, halo_dy(next_row, dn_ref[...].astype(F32)[0:1, :]))
        dy_m1 = jnp.where(row == 0, dy_prev, pltpu.roll(d_y, 1, 0))
        dy_p1 = jnp.where(row == tm - 1, dy_next, pltpu.roll(d_y, tm - 1, 0))
        d_cu = dy_p1 * w[0:1] + d_y * w[1:2] + dy_m1 * w[2:3]
        da_ref[:, 0:512] = (dya_t * y * sz).astype(BF16)
        da_ref[:, 512:1024] = (d_cu * u).astype(BF16)
        da_ref[:, 1024:1536] = (d_cu * c).astype(BF16)
        da_ref[:, 1536:2048] = (dya_t * b * y * (sig + sz * (1.0 - sig))).astype(BF16)
        gw_ref[0:1, :] += jnp.sum(d_y * cu_m1, axis=0, keepdims=True)
        gw_ref[1:2, :] += jnp.sum(d_y * cu, axis=0, keepdims=True)
        gw_ref[2:3, :] += jnp.sum(d_y * cu_p1, axis=0, keepdims=True)

    prev, nxt = _halo_specs(s, tm, 16, 2048)
    dprev, dnxt = _halo_specs(s, tm, 16, 512)
    return pl.pallas_call(
        body, name="conv_bwd", grid=(nt,),
        out_shape=[jax.ShapeDtypeStruct((s, 2048), BF16), jax.ShapeDtypeStruct((8, 512), F32)],
        in_specs=[_full(TOKEN), _rows(tm, 2048), prev, nxt, _rows(tm, 512), dprev, dnxt, _full((3, 512))],
        out_specs=[_rows(tm, 2048), _full((8, 512))],
        compiler_params=_params(),
    )(after, pa, pa, pa, dya, dya, dya, w_conv)


def _attn_bwd(after, pq, pkv, pbz, dyb, sink, tabs):
    s = pq.shape[0]
    nb = s // ATTN_BLOCK
    bps = min(MAX_BLOCKS_IN_STEP, nb)

    def body(sink_ref, after_ref, q_ref, z_ref, d_ref, kv_ref, t_ref,
             dq_ref, dz_ref, dkv_ref, gs_ref, kpad, vpad, dk_acc, dv_acc, bias, e_scr, ds_scr):
        i = pl.program_id(0)

        @pl.when(i == 0)
        def _():
            _fill_padded(kv_ref, kpad, vpad, s)
            _fill_band_bias(bias, nb)
            dk_acc[...] = jnp.zeros_like(dk_acc)
            dv_acc[...] = jnp.zeros_like(dv_acc)
            gs_ref[...] = jnp.zeros_like(gs_ref)

        row = lax.broadcasted_iota(jnp.int32, (ATTN_BLOCK, 128), 0)
        for b in range(bps):
            n = i * bps + b
            rows = slice(b * ATTN_BLOCK, (b + 1) * ATTN_BLOCK)
            start = pl.multiple_of(n * ATTN_BLOCK, ATTN_BLOCK)
            kw, vw = kpad[pl.ds(start, WINDOW_KEYS), :], vpad[pl.ds(start, WINDOW_KEYS), :]
            qf = q_ref[rows, :].astype(F32)
            variant = _bias_variant(n, nb)
            z = z_ref[rows, :].astype(F32)
            sig = _sigmoid(z)
            dyb_t = d_ref[rows, :].astype(F32)
            d_attn = dyb_t * (z * sig)
            outs, dqs = [], []
            dk_w = jnp.zeros((WINDOW_KEYS, 128), F32)
            dv_w = jnp.zeros((WINDOW_KEYS, 128), F32)
            for g in range(2):
                e_bg, ds_bg = e_scr.at[2 * b + g], ds_scr.at[2 * b + g]
                qt = _heads_to_lanes(qf, g, row)
                inv, p_sink = _softmax_keys_major(
                    _dot(kw, (qt * ATTN_SCALE).astype(BF16)), bias, variant, _sink_row(sink_ref, g), e_bg)
                ot = _dot_tn(vw, e_bg[...]) * inv
                outs.append(ot)
                dot_ = _heads_to_lanes(d_attn, g, row)
                delta = jnp.sum(dot_ * ot, axis=0, keepdims=True)
                dpt = _dot(vw, dot_.astype(BF16))
                for k in range(WINDOW_KEYS // KEY_CHUNK):
                    rw = slice(k * KEY_CHUNK, (k + 1) * KEY_CHUNK)
                    ds_bg[rw, :] = (e_bg[rw, :].astype(F32) * (dpt[rw] - delta)).astype(BF16)
                sink_part = p_sink * delta
                for j in range(4):
                    h = 4 * g + j
                    gs_ref[h:h + 1, :] -= jnp.sum(sink_part[:, 128 * j:128 * j + 128])
                dqs.append(_dot_tn(kw, ds_bg[...]) * (inv * ATTN_SCALE))
                dk_w += _dot_nt(ds_bg[...], (qt * inv).astype(BF16)) * ATTN_SCALE
                dv_w += _dot_nt(e_bg[...], (dot_ * inv).astype(BF16))
            dk_acc[pl.ds(start, WINDOW_KEYS), :] += dk_w
            dv_acc[pl.ds(start, WINDOW_KEYS), :] += dv_w
            attn = _lanes_to_heads(outs[0], outs[1], row)
            dz_ref[rows, :] = (dyb_t * attn * (sig * (1.0 + z * (1.0 - sig)))).astype(BF16)
            dq = _lanes_to_heads(dqs[0], dqs[1], row)
            trows = pl.ds(start, ATTN_BLOCK)
            cs, s1, s2 = t_ref[0, trows, :], t_ref[1, trows, :], t_ref[2, trows, :]
            for blk in range(4):
                cols = slice(128 * blk, 128 * blk + 128)
                dq_ref[rows, cols] = _rope_t(dq[:, cols], cs, s1, s2).astype(BF16)

        @pl.when(i == nb // bps - 1)
        def _():
            dk = dk_acc[ATTN_BLOCK:ATTN_BLOCK + s, :]
            dkv_ref[:, 0:128] = _rope_t(dk, t_ref[0], t_ref[1], t_ref[2]).astype(BF16)
            dkv_ref[:, 128:256] = dv_acc[ATTN_BLOCK:ATTN_BLOCK + s, :].astype(BF16)

    tq = bps * ATTN_BLOCK
    tile = _rows(tq, 512)
    return pl.pallas_call(
        body, name="attn_bwd", grid=(s // tq,),
        out_shape=[jax.ShapeDtypeStruct((s, 512), BF16), jax.ShapeDtypeStruct((s, 512), BF16),
                   jax.ShapeDtypeStruct((s, 256), BF16), jax.ShapeDtypeStruct((8, 128), F32)],
        in_specs=[pl.BlockSpec(memory_space=pltpu.SMEM), _full(TOKEN), tile, tile, tile, _full((s, 256)),
                  _full((3, s, 128))],
        out_specs=[tile, tile, _full((s, 256)), _full((8, 128))],
        scratch_shapes=[pltpu.VMEM((s + 2 * ATTN_BLOCK, 128), BF16)] * 2
        + [pltpu.VMEM((s + 2 * ATTN_BLOCK, 128), F32)] * 2
        + [pltpu.VMEM((3, WINDOW_KEYS, STACKED), F32)]
        + [pltpu.VMEM((2 * bps, WINDOW_KEYS, STACKED), BF16)] * 2,
        compiler_params=_params(),
    )(sink, after, pq, pbz, dyb, pkv, tabs)


def _mem_attn_bwd(pmq, pmz, mkv, dym):
    s = pmq.shape[0]
    m = mkv.shape[0]
    tm = min(512, s)

    def body(q_ref, z_ref, d_ref, mk_ref, mv_ref, dq_ref, dz_ref, dmkv_ref):
        @pl.when(pl.program_id(0) == 0)
        def _():
            dmkv_ref[...] = jnp.zeros_like(dmkv_ref)

        z = z_ref[...].astype(F32)
        sig = _sigmoid(z)
        dym_t = d_ref[...].astype(F32)
        d_attn = dym_t * (z * sig)
        dsilu = sig * (1.0 + z * (1.0 - sig))
        for h in range(MEM_HEADS):
            cols = slice(128 * h, 128 * h + 128)
            q, mk, mv = q_ref[:, cols], mk_ref[:, cols], mv_ref[:, cols]
            pt = _mem_softmax_t(q, mk)
            pb = pt.astype(BF16)
            o = _dot_tn(pb, mv)
            dob = d_attn[:, cols].astype(BF16)
            dpt = _dot_nt(mv, dob)
            dst = (pt * (dpt - jnp.sum(pt * dpt, axis=0, keepdims=True))).astype(BF16)
            dq_ref[:, cols] = (_dot_tn(dst, mk) * MEM_SCALE).astype(BF16)
            dz_ref[:, cols] = (dym_t[:, cols] * o * dsilu[:, cols]).astype(BF16)
            dmkv_ref[:, cols] += _dot(dst, q) * MEM_SCALE
            dmkv_ref[:, 512 + 128 * h:512 + 128 * h + 128] += _dot(pb, dob)

    return pl.pallas_call(
        body, name="mem_attn_bwd", grid=(s // tm,),
        out_shape=[jax.ShapeDtypeStruct((s, 512), BF16), jax.ShapeDtypeStruct((s, 512), BF16),
                   jax.ShapeDtypeStruct((m, D_MODEL), F32)],
        in_specs=[_rows(tm, 512), _rows(tm, 512), _rows(tm, 512), pl.BlockSpec((m, 512), lambda i: (0, 0)),
                  pl.BlockSpec((m, 512), lambda i: (0, 1))],
        out_specs=[_rows(tm, 512), _rows(tm, 512), _full((m, D_MODEL))],
        compiler_params=_params(),
    )(pmq, pmz, dym, mkv, mkv)


def _mem_kv_bwd(mem, g_mem, mn, dmkv, w_mkv):
    m = mem.shape[0]

    def body(mem_ref, g_ref, mn_ref, d_ref, w_ref, gw_ref, gg_ref):
        db = d_ref[...].astype(BF16)
        gw_ref[...] = _dot_tn(mn_ref[...], db).astype(BF16)
        d_mn = _dot_nt(db, w_ref[...])
        xf = mem_ref[...]
        r = lax.rsqrt(jnp.mean(xf * xf, axis=-1, keepdims=True) + EPS)
        gg_ref[...] = jnp.sum(d_mn * (xf * r), axis=0, keepdims=True)

    return pl.pallas_call(
        body, name="mem_kv_bwd", grid=(1,),
        out_shape=[jax.ShapeDtypeStruct((D_MODEL, D_MODEL), BF16), jax.ShapeDtypeStruct((1, D_MODEL), F32)],
        in_specs=[_full((m, D_MODEL)), _full((1, D_MODEL)), _full((m, D_MODEL)), _full((m, D_MODEL)),
                  _full((D_MODEL, D_MODEL))],
        out_specs=[_full((D_MODEL, D_MODEL)), _full((1, D_MODEL))],
        compiler_params=_params(),
    )(mem, g_mem, mn, dmkv, w_mkv)


def _dh_bwd(after, dparts, x, dy, g_pre, w_int):
    s = x.shape[0]
    tm = min(256, s)
    n_t = s // tm
    tiled = list(dparts) + [x, dy]
    n_in = len(tiled)

    def body(after_ref, *refs):
        in_hbm = refs[:n_in]
        g_ref, w_hbm, gx_hbm, gg_ref = refs[n_in:n_in + 4]
        bufs = refs[n_in + 4:2 * n_in + 4]
        w_vm, gx_buf, in_sems, out_sems, w_sem = refs[2 * n_in + 4:]

        def loads(t, slot):
            rows = pl.ds(pl.multiple_of(t * tm, tm), tm)
            return [pltpu.make_async_copy(in_hbm[k].at[rows], bufs[k].at[slot], in_sems.at[slot, k])
                    for k in range(n_in)]

        def store(t, slot):
            rows = pl.ds(pl.multiple_of(t * tm, tm), tm)
            return pltpu.make_async_copy(gx_buf.at[slot], gx_hbm.at[rows], out_sems.at[slot])

        w_copy = pltpu.make_async_copy(w_hbm, w_vm, w_sem)
        w_copy.start()
        for cp in loads(0, 0):
            cp.start()
        w_copy.wait()

        def step(t, gg):
            slot = t & 1

            @pl.when(t + 1 < n_t)
            def _():
                for cp in loads(t + 1, 1 - slot):
                    cp.start()

            for cp in loads(t, slot):
                cp.wait()

            @pl.when(t >= 2)
            def _():
                store(t - 2, slot).wait()

            d_h = jnp.zeros((tm, D_MODEL), F32)
            for k, (r0, width) in enumerate(SEGS):
                for c0 in range(0, width, 512):
                    cw = min(512, width - c0)
                    d_h += _dot(bufs[k].at[slot][:, c0:c0 + cw], w_vm[r0 + c0:r0 + c0 + cw, :])
            xf = bufs[n_in - 2][slot]
            r = lax.rsqrt(jnp.mean(xf * xf, axis=-1, keepdims=True) + EPS)
            xn = xf * r
            a = d_h * g_ref[...]
            gx_buf[slot] = r * (a - xn * jnp.mean(a * xn, axis=-1, keepdims=True)) + bufs[n_in - 1][slot]
            store(t, slot).start()
            return gg + jnp.sum(d_h * xn, axis=0, keepdims=True)

        gg_ref[...] = lax.fori_loop(0, n_t, step, jnp.zeros((1, D_MODEL), F32))
        store(n_t - 2, (n_t - 2) & 1).wait()
        store(n_t - 1, (n_t - 1) & 1).wait()

    vmem = pl.BlockSpec(memory_space=pltpu.VMEM)
    return pl.pallas_call(
        body, name="dh_bwd",
        out_shape=[jax.ShapeDtypeStruct((s, D_MODEL), F32), jax.ShapeDtypeStruct((1, D_MODEL), F32)],
        in_specs=[ANY] * (1 + n_in) + [vmem, ANY],
        out_specs=[ANY, vmem],
        scratch_shapes=[pltpu.VMEM((2, tm) + a.shape[1:], a.dtype) for a in tiled]
        + [pltpu.VMEM((IN_WIDTH, D_MODEL), BF16), pltpu.VMEM((2, tm, D_MODEL), F32),
           pltpu.SemaphoreType.DMA((2, n_in)), pltpu.SemaphoreType.DMA((2,)), pltpu.SemaphoreType.DMA(())],
        compiler_params=pltpu.CompilerParams(vmem_limit_bytes=CALL_VMEM_MB * 1024 * 1024),
    )(after, *dparts, x, dy, g_pre, w_int)


def _gw_in(dparts, h):
    s = h.shape[0]
    tn = 256
    tiles = [(a, c0) for a, (_, width) in enumerate(SEGS) for c0 in range(0, width, tn)]
    n_t = len(tiles)
    assert n_t * tn == IN_WIDTH

    def body(*refs):
        d_hbm = refs[:7]
        h_hbm, out_hbm, lhs, h_vm, res, in_sems, out_sems, h_sem = refs[7:]

        def load(a, c0, slot):
            return pltpu.make_async_copy(d_hbm[a].at[:, pl.ds(c0, tn)], lhs.at[slot], in_sems.at[slot])

        def store(t, slot):
            rows = pl.ds(pl.multiple_of(t * tn, tn), tn)
            return pltpu.make_async_copy(res.at[slot], out_hbm.at[rows], out_sems.at[slot])

        def start_load(t, slot):
            for k, (a, c0) in enumerate(tiles):
                @pl.when(t == k)
                def _(a=a, c0=c0):
                    load(a, c0, slot).start()

        h_copy = pltpu.make_async_copy(h_hbm, h_vm, h_sem)
        h_copy.start()
        load(*tiles[0], 0).start()
        h_copy.wait()

        def step(t, carry):
            slot = t & 1

            @pl.when(t + 1 < n_t)
            def _():
                start_load(t + 1, 1 - slot)

            load(*tiles[0], slot).wait()

            @pl.when(t >= 2)
            def _():
                store(t - 2, slot).wait()

            res[slot] = _dot_tn(lhs[slot], h_vm[...]).astype(BF16)
            store(t, slot).start()
            return carry

        lax.fori_loop(0, n_t, step, 0)
        store(n_t - 2, (n_t - 2) & 1).wait()
        store(n_t - 1, (n_t - 1) & 1).wait()

    return pl.pallas_call(
        body, name="gw_in",
        out_shape=jax.ShapeDtypeStruct((IN_WIDTH, D_MODEL), BF16),
        in_specs=[ANY] * 8, out_specs=ANY,
        scratch_shapes=[pltpu.VMEM((2, s, tn), BF16), pltpu.VMEM((s, D_MODEL), BF16),
                        pltpu.VMEM((2, tn, D_MODEL), BF16), pltpu.SemaphoreType.DMA((2,)),
                        pltpu.SemaphoreType.DMA((2,)), pltpu.SemaphoreType.DMA(())],
        compiler_params=pltpu.CompilerParams(vmem_limit_bytes=CALL_VMEM_MB * 1024 * 1024),
    )(*dparts, h)


def _adamw_math(w, g, m, v):
    m2 = ADAM_B1 * m + (1.0 - ADAM_B1) * g
    v2 = ADAM_B2 * v + (1.0 - ADAM_B2) * (g * g)
    m_hat = m2 / (1.0 - ADAM_B1 ** ADAM_STEP)
    v_hat = v2 / (1.0 - ADAM_B2 ** ADAM_STEP)
    delta = -ADAM_LR * (m_hat / (jnp.sqrt(v_hat) + ADAM_EPS) + ADAM_WD * w)
    return delta, m2, v2


def _sum_adamw(after, own, land, chip, block, w, m, v, name, tiles=1):
    r, c = w.shape
    rt = r // tiles

    def body(c_ref, after_ref, own_ref, l1_ref, l2_ref, l3_ref, w_ref, m_ref, v_ref, g_ref, d_ref, m2_ref, v2_ref):
        g = own_ref[...].astype(F32)
        for l_ref in (l1_ref, l2_ref, l3_ref):
            g += l_ref[...].astype(F32)
        g_ref[...] = g
        d_ref[...], m2_ref[...], v2_ref[...] = _adamw_math(w_ref[...], g, m_ref[...], v_ref[...])

    def share(k):
        return pl.BlockSpec((None, rt, c), lambda i, c_ref: (jnp.bitwise_xor(c_ref[0], k), block * tiles + i, 0))

    spec = pl.BlockSpec((rt, c), lambda i, c_ref: (i, 0))
    grid_spec = pltpu.PrefetchScalarGridSpec(
        num_scalar_prefetch=1, grid=(tiles,),
        in_specs=[ANY, share(0), share(1), share(2), share(3)] + [spec] * 3, out_specs=[spec] * 4)
    return pl.pallas_call(
        body, name=name, grid_spec=grid_spec,
        out_shape=[jax.ShapeDtypeStruct((r, c), F32)] * 4,
        compiler_params=_params(),
    )(chip, after, own, land, land, land, w, m, v)


def _sum_adamw_group(after, items, chip, name):
    k = len(items)

    def body(c_ref, after_ref, *refs):
        shares, wmv, outs = refs[:4 * k], refs[4 * k:7 * k], refs[7 * k:]
        for j in range(k):
            g = shares[4 * j][...].astype(F32)
            for l_ref in shares[4 * j + 1:4 * j + 4]:
                g += l_ref[...].astype(F32)
            outs[4 * j][...] = g
            outs[4 * j + 1][...], outs[4 * j + 2][...], outs[4 * j + 3][...] = _adamw_math(
                wmv[3 * j][...], g, wmv[3 * j + 1][...], wmv[3 * j + 2][...])

    def share(shape, block, q):
        return pl.BlockSpec((None,) + shape, lambda i, c_ref: (jnp.bitwise_xor(c_ref[0], q), block, 0))

    in_specs, args = [ANY], [after]
    for own, land, block, w, m, v in items:
        in_specs += [share(w.shape, block, q) for q in range(4)]
        args += [own, land, land, land]
    for own, land, block, w, m, v in items:
        in_specs += [pl.BlockSpec(w.shape, lambda i, c_ref: (0, 0))] * 3
        args += [w, m, v]
    out_specs = [pl.BlockSpec(w.shape, lambda i, c_ref: (0, 0)) for _, _, _, w, _, _ in items for _ in range(4)]
    res = pl.pallas_call(
        body, name=name,
        grid_spec=pltpu.PrefetchScalarGridSpec(num_scalar_prefetch=1, grid=(1,), in_specs=in_specs,
                                               out_specs=out_specs),
        out_shape=[jax.ShapeDtypeStruct(w.shape, F32) for _, _, _, w, _, _ in items for _ in range(4)],
        compiler_params=_params(),
    )(chip, *args)
    return [res[4 * j:4 * j + 4] for j in range(k)]


def _small_pack(parts):
    def pack_body(gpre_ref, gconv_ref, gsink_ref, gmem_ref, gpost_ref, loss_ref, pack):
        lane = lax.broadcasted_iota(jnp.int32, (1, 128), 1)
        sink_row = jnp.zeros((1, 128), F32)
        for h in range(8):
            sink_row = jnp.where(lane == h, gsink_ref[h:h + 1, :], sink_row)
        pack[...] = jnp.zeros_like(pack)
        pack[0:1, :] = gpre_ref[...]
        pack[1:2, :] = gmem_ref[...]
        pack[2:3, :] = gpost_ref[...]
        pack[3:6, 0:512] = gconv_ref[0:3, :]
        pack[6:7, 0:128] = sink_row
        pack[7:8, 0:128] = loss_ref[0:1, :]

    return pl.pallas_call(
        pack_body, name="small_pack", grid=(1,),
        out_shape=jax.ShapeDtypeStruct((8, D_MODEL), F32),
        in_specs=[_full(p.shape) for p in parts], out_specs=_full((8, D_MODEL)),
    )(*parts)


def _small_apply(packs, ws, ms, vs):
    def apply(p_ref, *refs):
        w_refs, m_refs, v_refs = refs[0:5], refs[5:10], refs[10:15]
        loss_out = refs[15]
        g_outs, d_outs, m_outs, v_outs = refs[16:21], refs[21:26], refs[26:31], refs[31:36]
        x, y, c = _my_place()
        tot = p_ref[0]
        for d in range(1, N_DEV):
            tot = tot + p_ref[d]
        conv = pltpu.roll(tot[:, 0:512], (512 - 64 * (4 * x + 2 * y + c)) % 512, 1)[3:6, 0:64]
        grads = (tot[0:1, :], conv, tot[6:7, 0:8], tot[1:2, :], tot[2:3, :])
        loss_out[...] = tot[7:8, 0:128]
        for j in range(5):
            if len(w_refs[j].shape) == 3:
                for k in range(w_refs[j].shape[0]):
                    g_outs[j][k] = grads[j][k:k + 1]
                    d_outs[j][k], m_outs[j][k], v_outs[j][k] = _adamw_math(
                        w_refs[j][k], grads[j][k:k + 1], m_refs[j][k], v_refs[j][k])
                continue
            g_outs[j][...] = grads[j]
            d_outs[j][...], m_outs[j][...], v_outs[j][...] = _adamw_math(
                w_refs[j][...], grads[j], m_refs[j][...], v_refs[j][...])

    specs = [_full(w.shape) for w in ws]
    res = pl.pallas_call(
        apply, name="small_apply", grid=(1,),
        out_shape=[jax.ShapeDtypeStruct((1, 128), F32)] + [jax.ShapeDtypeStruct(w.shape, F32) for w in ws] * 4,
        in_specs=[_full((N_DEV, 8, D_MODEL))] + specs * 3,
        out_specs=[_full((1, 128))] + specs * 4,
    )(packs, *ws, *ms, *vs)
    return res[0], res[1:6], res[6:11], res[11:16], res[16:21]


def kernel(x, mem, g_pre, w_in, w_conv, attn_sink, g_mem, w_mem_kv, w_up_a, w_up_b, w_up_m, w_out, g_post, loss_target, m_g_pre, m_w_in, m_w_conv, m_attn_sink, m_g_mem, m_w_mem_kv, m_w_up_a, m_w_up_b, m_w_up_m, m_w_out, m_g_post, v_g_pre, v_w_in, v_w_conv, v_attn_sink, v_g_mem, v_w_mem_kv, v_w_up_a, v_w_up_b, v_w_up_m, v_w_out, v_g_post):
    s = x.shape[1]
    x2, mem2, tgt2 = x[0], mem[0], loss_target[0]

    w_conv_loc = jnp.zeros((8, 128), F32).at[:3, :64].set(w_conv[0])
    w_int_g, w_conv_g, tabs, w_mkv_loc, w_out_loc, w_up_loc = _all_gather(
        [w_in[0].T.astype(BF16), w_conv_loc], "gather_w_in",
        splits=[[(112 * k, 112) for k in range(7)] + [(784, 144)], [(0, 8)]],
        side=_gather_side(s, w_mem_kv[0], w_out[0], (w_up_a[0], w_up_b[0], w_up_m[0])))
    w_int = w_int_g.reshape(IN_WIDTH, D_MODEL)
    w_conv_f = w_conv_g[:, :3, :64].transpose(1, 0, 2).reshape(3, 512)
    late = _gather_start([w_mkv_loc, w_out_loc, w_up_loc], "gather_late_start")
    sink = attn_sink[0]

    h, pa, pq, pkv, pbz, pmq, pmz, pg = _proj_fwd(late[4], x2, g_pre, w_int, tabs)
    ya = _conv_fwd(pa, w_conv_f)
    yb = _attn_fwd(pq, pkv, pbz, sink)
    w_mkv_g, w_out_g, w_up_g = _gather_wait(*late[:4], yb, "gather_late_wait")
    w_mkv = w_mkv_g.reshape(D_MODEL, D_MODEL)
    w_out_f = w_out_g.reshape(D_MODEL, D_MODEL)
    mn, mkv = _mem_kv_fwd(mem2, g_mem, w_mkv)
    ym = _mem_attn_fwd(pmq, pmz, mkv)
    dg, dya, dyb, dym, dy, loss_p, gg_post, mb, dob, du = _mid(ya, yb, ym, pg, x2, tgt2, g_post, w_up_g, w_out_f)
    gw_out, gw_up = _gw_mid(mb, dob, (ya, yb, ym), du)

    core = lax.axis_index("c").astype(jnp.int32).reshape(1)
    chip = (2 * lax.axis_index("x") + lax.axis_index("y")).astype(jnp.int32).reshape(1)

    dmq, dmz, dmkv = _mem_attn_bwd(pmq, pmz, mkv, dym)
    gw_mkv, gg_mem = _mem_kv_bwd(mem2, g_mem, mn, dmkv, w_mkv)
    shares1 = [gw_mkv.reshape(N_DEV, 128, D_MODEL), gw_out.reshape(N_DEV, 128, D_MODEL), gw_up]
    sib = _split_start(_sibling_copies, N_CHIPS, shares1,
                       [lax.empty((N_CHIPS,) + a.shape[1:], a.dtype) for a in shares1], "grads_to_sibling_small_start")
    da, gw_conv = _conv_bwd(sib[4], pa, dya, w_conv_f)
    shares1, from_sibling = _split_wait(_sibling_copies, N_CHIPS, *sib[:4], da, "grads_to_sibling_small_wait")
    send1, recv1, srcs1, lands1, token1 = _chip_exchange_start(
        _pair_add(shares1, from_sibling, core, "grads_pair_add_small"), "grads_to_chips_start_small")
    dq, dbz, dkv, g_sink = _attn_bwd(token1, pq, pkv, pbz, dyb, sink, tabs)
    dparts = (da, dq, dkv, dbz, dmq, dmz, dg)
    gw_int = _gw_in(dparts, h)
    send2, recv2, srcs2, lands2, token2 = _chip_exchange_start(
        [_sibling_reduce(gw_int.reshape(N_DEV, SHARD_IN, D_MODEL), "grads_sibling_reduce_w_in")],
        "grads_to_chips_start_w_in")
    grad_x, gg_pre = _dh_bwd(token2, dparts, x2, dy, g_pre, w_int)
    (o_mkv, o_out, o_up, o_int), (l_mkv, l_out, l_up, l_int) = _chip_exchange_wait(
        send1 + send2, recv1 + recv2, srcs1 + srcs2, lands1 + lands2, grad_x, "grads_to_chips_wait")

    small = _gather_start([_small_pack((gg_pre, gw_conv, g_sink, gg_mem, gg_post, loss_p))], "small_gather_start")

    w_in_t = _sum_adamw(small[4], o_int, l_int, chip, 0, w_in[0].T, m_w_in[0].T, v_w_in[0].T, "adamw_w_in", tiles=2)
    g_w_in, d_w_in, nm_w_in, nv_w_in = (t.T for t in w_in_t)
    (g_mkv, d_mkv, nm_mkv, nv_mkv), (g_out, d_out, nm_out, nv_out), *up = _sum_adamw_group(
        w_in_t[0],
        [(o_mkv, l_mkv, 0, w_mem_kv[0], m_w_mem_kv[0], v_w_mem_kv[0]),
         (o_out, l_out, 0, w_out[0], m_w_out[0], v_w_out[0]),
         (o_up, l_up, 0, w_up_a[0], m_w_up_a[0], v_w_up_a[0]),
         (o_up, l_up, 1, w_up_b[0], m_w_up_b[0], v_w_up_b[0]),
         (o_up, l_up, 2, w_up_m[0], m_w_up_m[0], v_w_up_m[0])], chip, "adamw_mid_weights")

    (packs,) = _gather_wait(*small[:4], g_out, "small_gather_wait")

    def taps_first(a):
        return a.transpose(1, 0, 2)

    loss_row, small_g, sd, sm, sv = _small_apply(
        packs, [g_pre, taps_first(w_conv), attn_sink, g_mem, g_post],
        [m_g_pre, taps_first(m_w_conv), m_attn_sink, m_g_mem, m_g_post],
        [v_g_pre, taps_first(v_w_conv), v_attn_sink, v_g_mem, v_g_post])
    loss = loss_row[0, 0]
    g_g_pre, g_conv, g_sink_tot, g_g_mem, g_g_post = small_g

    def lead(a):
        return a[None]

    grads = [g_g_pre, lead(g_w_in), taps_first(g_conv), g_sink_tot, g_g_mem, lead(g_mkv), lead(up[0][0]),
             lead(up[1][0]), lead(up[2][0]), lead(g_out), g_g_post]

    def assemble(small, big_in, big_mkv, big_up, big_out):
        return [small[0], lead(big_in), taps_first(small[1]), small[2], small[3], lead(big_mkv), lead(big_up[0]),
                lead(big_up[1]), lead(big_up[2]), lead(big_out), small[4]]

    deltas = assemble(sd, d_w_in, d_mkv, [u[1] for u in up], d_out)
    new_m = assemble(sm, nm_w_in, nm_mkv, [u[2] for u in up], nm_out)
    new_v = assemble(sv, nv_w_in, nv_mkv, [u[3] for u in up], nv_out)
    return (loss, grad_x[None], *grads, *deltas, *new_m, *new_v)
```

```python
import functools

import jax
import jax.numpy as jnp
from jax import lax
from jax.experimental import pallas as pl
from jax.experimental.pallas import tpu as pltpu

F32 = jnp.float32
BF16 = jnp.bfloat16
MESH = pl.DeviceIdType.MESH

N_DEV = 8
D_MODEL = 1024
EPS = 1e-6
ROPE_THETA = 500000.0
ROT_DIM = 16
HEAD_DIM = 64
ATTN_BLOCK = 128
MEM_HEADS = 4
MEM_HEAD_DIM = 128
ATTN_SCALE = HEAD_DIM ** -0.5
MEM_SCALE = MEM_HEAD_DIM ** -0.5

ADAM_LR = 0.001
ADAM_B1 = 0.9
ADAM_B2 = 0.999
ADAM_EPS = 1e-08
ADAM_WD = 0.01
ADAM_STEP = 10

SEG_A = (0, 2048)
SEG_BQ = (2048, 512)
SEG_BKV = (2560, 256)
SEG_BZ = (2816, 512)
SEG_MQ = (3328, 512)
SEG_MZ = (3840, 512)
SEG_G = (4352, 3072)
SEGS = (SEG_A, SEG_BQ, SEG_BKV, SEG_BZ, SEG_MQ, SEG_MZ, SEG_G)
IN_WIDTH = 7424
SHARD_IN = IN_WIDTH // N_DEV

V7X_VMEM_BYTES = 64 * 1024 * 1024
CALL_VMEM_MB = 57
ANY = pl.BlockSpec(memory_space=pl.ANY)


def _params():
    assert CALL_VMEM_MB * 1024 * 1024 < V7X_VMEM_BYTES
    return pltpu.CompilerParams(dimension_semantics=("arbitrary",), vmem_limit_bytes=CALL_VMEM_MB * 1024 * 1024)


def _full(shape):
    zeros = (0,) * len(shape)
    return pl.BlockSpec(shape, lambda i: zeros)


def _rows(tm, width):
    return pl.BlockSpec((tm, width), lambda i: (i, 0))


def _dot(a, b):
    return jnp.dot(a, b, preferred_element_type=F32)


def _dot_nt(a, b):
    return lax.dot_general(a, b, (((1,), (1,)), ((), ())), preferred_element_type=F32)


def _dot_tn(a, b):
    return lax.dot_general(a, b, (((0,), (0,)), ((), ())), preferred_element_type=F32)


def _sigmoid(z):
    return 1.0 / (1.0 + jnp.exp(-z))


def _rope(t, cs, s1, s2):
    return t * cs + pltpu.roll(t, 120, 1) * s1 + pltpu.roll(t, 8, 1) * s2


def _rope_t(d, cs, s1, s2):
    return d * cs + pltpu.roll(d * s1, 8, 1) + pltpu.roll(d * s2, 120, 1)


def _gather_side(s, w_mkv, w_out, w_ups):
    half = ROT_DIM // 2
    inv_freq = jnp.power(jnp.float32(ROPE_THETA), -jnp.arange(half, dtype=F32) * (2.0 / ROT_DIM))
    freq_row = jnp.tile(jnp.concatenate([inv_freq, inv_freq, jnp.zeros((HEAD_DIM - ROT_DIM,), F32)]), 2)[None, :]

    def fn(in_refs, out_refs):
        f_ref, mkv_ref, out_ref, *up_refs = in_refs
        t_ref, mkv_bf, out_bf, up_bf = out_refs
        mkv_bf[...] = mkv_ref[...].astype(BF16)
        out_bf[...] = out_ref[...].astype(BF16)
        for k, up_ref in enumerate(up_refs):
            up_bf[512 * k:512 * k + 512, :] = up_ref[...].astype(BF16)
        pos = lax.broadcasted_iota(jnp.int32, (s, 128), 0).astype(F32)
        d = lax.broadcasted_iota(jnp.int32, (s, 128), 1) & (HEAD_DIM - 1)
        ang = pos * f_ref[...]
        cos, sin = jnp.cos(ang), jnp.sin(ang)
        lo, hi = d < half, (d >= half) & (d < ROT_DIM)
        t_ref[0] = jnp.where(lo | hi, cos, 1.0)
        t_ref[1] = jnp.where(lo, -sin, 0.0)
        t_ref[2] = jnp.where(hi, sin, 0.0)

    return ([freq_row, w_mkv, w_out, *w_ups],
            [jax.ShapeDtypeStruct((3, s, 128), F32), jax.ShapeDtypeStruct(w_mkv.shape, BF16),
             jax.ShapeDtypeStruct(w_out.shape, BF16), jax.ShapeDtypeStruct((1536, 128), BF16)], fn)


def _load_once(pairs, sems):
    @pl.when(pl.program_id(0) == 0)
    def _():
        cps = [pltpu.make_async_copy(src, dst, sems.at[k]) for k, (src, dst) in enumerate(pairs)]
        for cp in cps:
            cp.start()
        for cp in cps:
            cp.wait()


def _my_place():
    x, y, c = lax.axis_index("x"), lax.axis_index("y"), lax.axis_index("c")
    return x, y, c


def _all_gather(arrs, name, splits=None, side=None):
    n = len(arrs)
    if splits is None:
        splits = [[(0, a.shape[0])] for a in arrs]
    pieces = [(a, r0, rn) for a in range(n) for r0, rn in splits[a]]
    n_p = len(pieces)
    side_in, side_out, side_fn = side if side is not None else ((), (), None)
    m, q = len(side_in), len(side_out)

    def body(*refs):
        ins, outs = refs[:n], refs[n + m:2 * n + m]
        send_sems, recv_sems, local_sems = refs[2 * n + m + q:]
        x, y, c = _my_place()
        me, sibling = (x, y, c), (x, y, 1 - c)

        def route(core):
            first = (jnp.bitwise_xor(x, 1 - core), jnp.bitwise_xor(y, core), core)
            second = (jnp.bitwise_xor(x, core), jnp.bitwise_xor(y, 1 - core), core)
            return first, second, (1 - x, 1 - y, core)

        def idx(px, py, pc):
            return 4 * px + 2 * py + pc

        def copy(p, k, block, to, own=False):
            a, r0, rn = pieces[p]
            dst = outs[a].at[idx(*block), pl.ds(r0, rn)]
            return pltpu.make_async_remote_copy(
                src_ref=ins[a].at[pl.ds(r0, rn)] if own else dst, dst_ref=dst,
                send_sem=send_sems.at[p * 7 + k], recv_sem=recv_sems.at[p * 7 + k],
                device_id=to, device_id_type=MESH)

        nbr1, nbr2, diag = route(c)
        mine = [pltpu.make_async_copy(ins[a], outs[a].at[idx(*me)], local_sems.at[a]) for a in range(n)]
        for cp in mine:
            cp.start()
        sent = []
        for p in range(n_p):
            for k, to in enumerate((sibling, nbr1, nbr2)):
                sent.append(copy(p, k, me, to, own=True))
        for cp in sent:
            cp.start()
        if side_fn is not None:
            side_fn(refs[n:n + m], refs[2 * n + m:2 * n + m + q])
        for k_in, block, onward in ((1, nbr1, ((3, nbr2), (4, sibling))), (2, nbr2, ((5, sibling),)),
                                    (3, diag, ((6, sibling),))):
            for p in range(n_p):
                copy(p, k_in, block, me).wait_recv()
                for k_out, to in onward:
                    cp = copy(p, k_out, block, to)
                    cp.start()
                    sent.append(cp)
        s1, s2, sd = route(1 - c)
        for k_in, block in ((0, sibling), (4, s1), (5, s2), (6, sd)):
            for p in range(n_p):
                copy(p, k_in, block, me).wait_recv()
        for cp in sent:
            cp.wait_send()
        for cp in mine:
            cp.wait()

    return pl.pallas_call(
        body, name=name,
        out_shape=[jax.ShapeDtypeStruct((N_DEV,) + a.shape, a.dtype) for a in arrs] + list(side_out),
        in_specs=[ANY] * n + [pl.BlockSpec(memory_space=pltpu.VMEM)] * m,
        out_specs=[ANY] * n + [pl.BlockSpec(memory_space=pltpu.VMEM)] * q,
        scratch_shapes=[pltpu.SemaphoreType.DMA((7 * n_p,)), pltpu.SemaphoreType.DMA((7 * n_p,)),
                        pltpu.SemaphoreType.DMA((n,))],
        compiler_params=pltpu.CompilerParams(vmem_limit_bytes=32 * 1024 * 1024),
    )(*arrs, *side_in)


N_CHIPS = 4


def _sibling_reduce(arr, name):
    _, r, c = arr.shape

    def body(in_ref, out_ref, land, a_buf, b_buf, o_buf, send_sems, recv_sems, local_sems):
        x, y, core = _my_place()
        cps = [pltpu.make_async_remote_copy(
            src_ref=in_ref.at[2 * j + (1 - core)], dst_ref=land.at[j], send_sem=send_sems.at[j],
            recv_sem=recv_sems.at[j], device_id=(x, y, 1 - core), device_id_type=MESH) for j in range(N_CHIPS)]
        for cp in cps:
            cp.start()
        store = None
        for j in range(N_CHIPS):
            mine = pltpu.make_async_copy(in_ref.at[2 * j + core], a_buf, local_sems.at[0])
            mine.start()
            cps[j].wait_recv()
            theirs = pltpu.make_async_copy(land.at[j], b_buf, local_sems.at[1])
            theirs.start()
            mine.wait()
            theirs.wait()
            if store is not None:
                store.wait()
            o_buf[...] = (a_buf[...].astype(F32) + b_buf[...].astype(F32)).astype(BF16)
            store = pltpu.make_async_copy(o_buf, out_ref.at[j], local_sems.at[2])
            store.start()
        store.wait()
        for cp in cps:
            cp.wait_send()

    return pl.pallas_call(
        body, name=name,
        out_shape=[jax.ShapeDtypeStruct((N_CHIPS, r, c), BF16)] * 2,
        in_specs=[ANY], out_specs=[ANY, ANY],
        scratch_shapes=[pltpu.VMEM((r, c), BF16)] * 3
        + [pltpu.SemaphoreType.DMA((N_CHIPS,)), pltpu.SemaphoreType.DMA((N_CHIPS,)), pltpu.SemaphoreType.DMA((3,))],
        compiler_params=pltpu.CompilerParams(vmem_limit_bytes=32 * 1024 * 1024),
    )(arr)[0]


def _sibling_copies(srcs, lands, send_sems, recv_sems):
    x, y, c = _my_place()
    cps = []
    for j in range(N_CHIPS):
        for a in range(len(srcs)):
            k = a * N_CHIPS + j
            cps.append(pltpu.make_async_remote_copy(
                src_ref=srcs[a].at[2 * j + (1 - c)], dst_ref=lands[a].at[j], send_sem=send_sems[k],
                recv_sem=recv_sems[k], device_id=(x, y, 1 - c), device_id_type=MESH))
    return cps


def _pair_add(mine, recv, core, name):
    n = len(mine)

    def body(c_ref, *refs):
        for a in range(n):
            refs[2 * n + a][...] = (refs[a][...].astype(F32) + refs[n + a][...].astype(F32)).astype(BF16)

    def blk(a):
        return (None,) + a.shape[1:]

    grid_spec = pltpu.PrefetchScalarGridSpec(
        num_scalar_prefetch=1, grid=(N_CHIPS,),
        in_specs=[pl.BlockSpec(blk(a), lambda j, c_ref: (2 * j + c_ref[0], 0, 0)) for a in mine]
        + [pl.BlockSpec(blk(a), lambda j, c_ref: (j, 0, 0)) for a in recv],
        out_specs=[pl.BlockSpec(blk(a), lambda j, c_ref: (j, 0, 0)) for a in recv])
    return pl.pallas_call(
        body, name=name, grid_spec=grid_spec,
        out_shape=[jax.ShapeDtypeStruct(a.shape, BF16) for a in recv],
        compiler_params=_params(),
    )(core, *mine, *recv)


HBM = pl.BlockSpec(memory_space=pltpu.HBM)
SEM = pl.BlockSpec(memory_space=pltpu.SEMAPHORE)
N_PEER_CHIPS = 3
TOKEN = (8, 128)


def _chip_copies(srcs, lands, send_sems, recv_sems):
    x, y, c = _my_place()
    my_chip = 2 * x + y
    peers = [(x, 1 - y), (1 - x, y), (1 - x, 1 - y)]
    cps = []
    for k, (px, py) in enumerate(peers):
        for a in range(len(srcs)):
            j = a * N_PEER_CHIPS + k
            cps.append(pltpu.make_async_remote_copy(
                src_ref=srcs[a].at[2 * px + py], dst_ref=lands[a].at[my_chip],
                send_sem=send_sems[j], recv_sem=recv_sems[j],
                device_id=(px, py, c), device_id_type=MESH))
    return cps


N_PEERS = N_DEV - 1


def _gather_copies(srcs, lands, send_sems, recv_sems):
    x, y, c = _my_place()
    me_idx = 4 * x + 2 * y + c
    flips = [(0, 0, 1), (0, 1, 0), (1, 0, 0), (0, 1, 1), (1, 0, 1), (1, 1, 0), (1, 1, 1)]
    cps = []
    for k, (fx, fy, fc) in enumerate(flips):
        peer = ((1 - x) if fx else x, (1 - y) if fy else y, (1 - c) if fc else c)
        for a in range(len(srcs)):
            j = a * N_PEERS + k
            cps.append(pltpu.make_async_remote_copy(
                src_ref=srcs[a], dst_ref=lands[a].at[me_idx], send_sem=send_sems[j], recv_sem=recv_sems[j],
                device_id=peer, device_id_type=MESH))
    return cps


def _split_start(copies, per_array, arrs, lands, name):
    arrs, lands = list(arrs), list(lands)
    n = len(arrs)
    k = n * per_array

    def body(*refs):
        srcs, land_refs = refs[:n], refs[n:2 * n]
        send_sems, recv_sems = refs[2 * n:2 * n + k], refs[2 * n + k:2 * n + 2 * k]
        token = refs[-1]
        for cp in copies(srcs, land_refs, send_sems, recv_sems):
            cp.start()
        token[...] = jnp.zeros_like(token)

    hbm_arrs = [pltpu.with_memory_space_constraint(a, pltpu.HBM) for a in arrs]
    lands = [pltpu.with_memory_space_constraint(a, pltpu.HBM) for a in lands]
    res = pl.pallas_call(
        body, name=name,
        out_shape=[pltpu.SemaphoreType.DMA(())] * (2 * k) + [pltpu.HBM(a.shape, a.dtype) for a in arrs + lands]
        + [jax.ShapeDtypeStruct(TOKEN, F32)],
        in_specs=[HBM] * (2 * n),
        out_specs=[SEM] * (2 * k) + [HBM] * (2 * n) + [pl.BlockSpec(memory_space=pltpu.VMEM)],
        input_output_aliases={a: 2 * k + a for a in range(2 * n)},
        compiler_params=pltpu.CompilerParams(has_side_effects=pltpu.SideEffectType.DATAFLOW_SIDE_EFFECTING),
    )(*hbm_arrs, *lands)
    return res[:k], res[k:2 * k], res[2 * k:2 * k + n], res[2 * k + n:2 * k + 2 * n], res[-1]


def _split_wait(copies, per_array, send_sems, recv_sems, srcs, lands, after, name):
    n = len(srcs)
    k = n * per_array

    def body(*refs):
        src_refs, land_refs = refs[:n], refs[n:2 * n]
        s_sems, r_sems = refs[2 * n:2 * n + k], refs[2 * n + k:2 * n + 2 * k]
        for cp in copies(src_refs, land_refs, s_sems, r_sems):
            cp.wait_send()
            cp.wait_recv()

    res = pl.pallas_call(
        body, name=name,
        out_shape=[pltpu.HBM(a.shape, a.dtype) for a in list(srcs) + list(lands)],
        in_specs=[HBM] * (2 * n) + [SEM] * (2 * k) + [ANY],
        out_specs=[HBM] * (2 * n),
        input_output_aliases={a: a for a in range(2 * n)},
        compiler_params=pltpu.CompilerParams(has_side_effects=pltpu.SideEffectType.DATAFLOW_SIDE_EFFECTING),
    )(*srcs, *lands, *send_sems, *recv_sems, after)
    return res[:n], res[n:]


def _chip_exchange_start(arrs, name):
    return _split_start(_chip_copies, N_PEER_CHIPS, arrs, [lax.empty(a.shape, a.dtype) for a in arrs], name)


def _chip_exchange_wait(send_sems, recv_sems, srcs, lands, after, name):
    return _split_wait(_chip_copies, N_PEER_CHIPS, send_sems, recv_sems, srcs, lands, after, name)


def _gather_start(arrs, me_idx, name):
    lands = [lax.dynamic_update_slice(lax.empty((N_DEV,) + a.shape, a.dtype), a[None], (me_idx, 0, 0)) for a in arrs]
    return _split_start(_gather_copies, N_PEERS, arrs, lands, name)


def _gather_wait(send_sems, recv_sems, srcs, lands, after, name):
    return _split_wait(_gather_copies, N_PEERS, send_sems, recv_sems, srcs, lands, after, name)[1]


def _proj_fwd(after, x, g_pre, w_int, tabs):
    s = x.shape[0]
    tm = min(512, s)

    def tile(g_ref, w_vm, x_ref, t_ref, h_ref, pa_ref, pq_ref, pkv_ref, pbz_ref, pmq_ref, pmz_ref, pg_ref):
        xf = x_ref[...]
        r = lax.rsqrt(jnp.mean(xf * xf, axis=-1, keepdims=True) + EPS)
        h = ((xf * r) * g_ref[...]).astype(BF16)
        h_ref[...] = h
        cs, s1, s2 = t_ref[0], t_ref[1], t_ref[2]

        def mm(seg, c0, width):
            return _dot_nt(h, w_vm[seg[0] + c0:seg[0] + c0 + width, :])

        for c0 in range(0, SEG_A[1], 512):
            pa_ref[:, c0:c0 + 512] = mm(SEG_A, c0, 512).astype(BF16)
        q = mm(SEG_BQ, 0, 512)
        for b in range(4):
            pq_ref[:, 128 * b:128 * b + 128] = _rope(q[:, 128 * b:128 * b + 128], cs, s1, s2).astype(BF16)
        kv = mm(SEG_BKV, 0, 256)
        pkv_ref[:, 0:128] = _rope(kv[:, 0:128], cs, s1, s2).astype(BF16)
        pkv_ref[:, 128:256] = kv[:, 128:256].astype(BF16)
        pbz_ref[...] = mm(SEG_BZ, 0, 512).astype(BF16)
        pmq_ref[...] = mm(SEG_MQ, 0, 512).astype(BF16)
        pmz_ref[...] = mm(SEG_MZ, 0, 512).astype(BF16)
        for c0 in range(0, SEG_G[1], 512):
            pg_ref[:, c0:c0 + 512] = mm(SEG_G, c0, 512).astype(BF16)

    widths = (D_MODEL, 2048, 512, 256, 512, 512, 512, 3072)

    def body(after_ref, x_hbm, g_ref, t_hbm, w_hbm, *refs):
        outs, w_vm, sem = refs[:len(widths)], refs[len(widths)], refs[len(widths) + 1]
        w_copy = pltpu.make_async_copy(w_hbm, w_vm, sem)
        w_copy.start()
        w_copy.wait()
        pltpu.emit_pipeline(
            functools.partial(tile, g_ref, w_vm), grid=(s // tm,),
            in_specs=[_rows(tm, D_MODEL), pl.BlockSpec((3, tm, 128), lambda i: (0, i, 0))],
            out_specs=[_rows(tm, w) for w in widths],
        )(x_hbm, t_hbm, *outs)

    return pl.pallas_call(
        body, name="proj_fwd",
        out_shape=[jax.ShapeDtypeStruct((s, w), BF16) for w in widths],
        in_specs=[ANY, ANY, pl.BlockSpec(memory_space=pltpu.VMEM), ANY, ANY],
        out_specs=[ANY] * len(widths),
        scratch_shapes=[pltpu.VMEM((IN_WIDTH, D_MODEL), BF16), pltpu.SemaphoreType.DMA(())],
        compiler_params=pltpu.CompilerParams(vmem_limit_bytes=CALL_VMEM_MB * 1024 * 1024),
    )(after, x, g_pre, tabs, w_int)


def _mem_kv_fwd(mem, g_mem, w_mkv):
    m = mem.shape[0]

    def body(mem_ref, g_ref, w_ref, mn_ref, mkv_ref):
        xf = mem_ref[...]
        r = lax.rsqrt(jnp.mean(xf * xf, axis=-1, keepdims=True) + EPS)
        mn = ((xf * r) * g_ref[...]).astype(BF16)
        mn_ref[...] = mn
        mkv_ref[...] = _dot(mn, w_ref[...]).astype(BF16)

    return pl.pallas_call(
        body, name="mem_kv_fwd", grid=(1,),
        out_shape=[jax.ShapeDtypeStruct((m, D_MODEL), BF16)] * 2,
        in_specs=[_full((m, D_MODEL)), _full((1, D_MODEL)), _full((D_MODEL, D_MODEL))],
        out_specs=[_full((m, D_MODEL))] * 2,
        compiler_params=_params(),
    )(mem, g_mem, w_mkv)


def _halo_specs(s, tm, rows, width):
    nblk = s // rows
    prev = pl.BlockSpec((rows, width), lambda i: (jnp.maximum(i * (tm // rows) - 1, 0), 0))
    nxt = pl.BlockSpec((rows, width), lambda i: (jnp.minimum((i + 1) * (tm // rows), nblk - 1), 0))
    return prev, nxt


def _conv_common(pa, cu_prev, cu_next, w, tm):
    b, c, u, z = (pa[:, 512 * k:512 * k + 512] for k in range(4))
    cu = c * u
    row = lax.broadcasted_iota(jnp.int32, (tm, 512), 0)
    cu_m1 = jnp.where(row == 0, cu_prev, pltpu.roll(cu, 1, 0))
    cu_p1 = jnp.where(row == tm - 1, cu_next, pltpu.roll(cu, tm - 1, 0))
    y = cu_m1 * w[0:1] + cu * w[1:2] + cu_p1 * w[2:3]
    sig = _sigmoid(z)
    return b, c, u, z, cu, cu_m1, cu_p1, y, sig, row


def _conv_fwd(pa, w_conv):
    s = pa.shape[0]
    tm = min(512, s)
    nt = s // tm

    def body(pa_ref, pp_ref, pn_ref, w_ref, ya_ref):
        i = pl.program_id(0)
        prev_row = pp_ref[...].astype(F32)[15:16, :]
        next_row = pn_ref[...].astype(F32)[0:1, :]
        b, _, _, z, _, _, _, y, sig, _ = _conv_common(
            pa_ref[...].astype(F32),
            jnp.where(i == 0, 0.0, prev_row[:, 512:1024] * prev_row[:, 1024:1536]),
            jnp.where(i == nt - 1, 0.0, next_row[:, 512:1024] * next_row[:, 1024:1536]), w_ref[...], tm)
        ya_ref[...] = (b * y * (z * sig)).astype(BF16)

    prev, nxt = _halo_specs(s, tm, 16, 2048)
    return pl.pallas_call(
        body, name="conv_fwd", grid=(nt,),
        out_shape=jax.ShapeDtypeStruct((s, 512), BF16),
        in_specs=[_rows(tm, 2048), prev, nxt, _full((3, 512))],
        out_specs=_rows(tm, 512),
        compiler_params=_params(),
    )(pa, pa, pa, w_conv)


def _heads_to_lanes(a, g, row):
    low = row < HEAD_DIM
    parts = []
    for b in (2 * g, 2 * g + 1):
        t = jnp.transpose(a[:, 128 * b:128 * b + 128])
        swapped = pltpu.roll(t, HEAD_DIM, 0)
        if g == 0:
            parts += [jnp.where(low, t, 0.0), jnp.where(low, swapped, 0.0)]
        else:
            parts += [jnp.where(low, 0.0, swapped), jnp.where(low, 0.0, t)]
    return jnp.concatenate(parts, axis=1)


def _lanes_to_heads(t0, t1, row):
    low = row < HEAD_DIM
    blocks = []
    for b in range(4):
        g = b // 2
        tg = (t0, t1)[g]
        je = 2 * (b - 2 * g)
        even, odd = tg[:, 128 * je:128 * je + 128], tg[:, 128 * je + 128:128 * je + 256]
        if g == 0:
            t = jnp.where(low, even, pltpu.roll(odd, HEAD_DIM, 0))
        else:
            t = jnp.where(low, pltpu.roll(even, HEAD_DIM, 0), odd)
        blocks.append(jnp.transpose(t))
    return jnp.concatenate(blocks, axis=1)


WINDOW_KEYS = 3 * ATTN_BLOCK
STACKED = 4 * ATTN_BLOCK
KEY_CHUNK = 32
MAX_BLOCKS_IN_STEP = 8


def _fill_band_bias(bias, nb):
    assert nb >= 2
    c = lax.broadcasted_iota(jnp.int32, (WINDOW_KEYS, STACKED), 0)
    r = lax.broadcasted_iota(jnp.int32, (WINDOW_KEYS, STACKED), 1) & (ATTN_BLOCK - 1)
    band = (c >= r) & (c <= r + 2 * ATTN_BLOCK)
    for v, ok in enumerate((band, band & (c >= ATTN_BLOCK), band & (c < 2 * ATTN_BLOCK))):
        bias[v] = jnp.where(ok, 0.0, -jnp.inf)


def _bias_variant(n, nb):
    return jnp.where(n == 0, 1, jnp.where(n == nb - 1, 2, 0))


def _sink_row(sink_ref, g):
    return jnp.concatenate([jnp.full((1, ATTN_BLOCK), sink_ref[4 * g + j], F32) for j in range(4)], axis=1)


def _softmax_keys_major(sc, bias, variant, sink, e_scr):
    chunks = [pl.ds(k * KEY_CHUNK, KEY_CHUNK) for k in range(WINDOW_KEYS // KEY_CHUNK)]
    rows = [slice(k * KEY_CHUNK, (k + 1) * KEY_CHUNK) for k in range(WINDOW_KEYS // KEY_CHUNK)]
    m_run = jnp.full((KEY_CHUNK, STACKED), -jnp.inf, F32)
    for ck, rw in zip(chunks, rows):
        m_run = jnp.maximum(m_run, sc[rw] + bias[variant, ck, :])
    m = jnp.maximum(jnp.max(m_run, axis=0, keepdims=True), sink)
    l_run = jnp.zeros((KEY_CHUNK, STACKED), F32)
    for ck, rw in zip(chunks, rows):
        e = jnp.exp(sc[rw] + bias[variant, ck, :] - m)
        l_run += e
        e_scr[rw, :] = e.astype(BF16)
    es = jnp.exp(sink - m)
    inv = 1.0 / (jnp.sum(l_run, axis=0, keepdims=True) + es)
    return inv, es * inv


def _fill_padded(kv_ref, kpad, vpad, s):
    zero = jnp.zeros((ATTN_BLOCK, 128), BF16)
    kpad[0:ATTN_BLOCK, :] = zero
    vpad[0:ATTN_BLOCK, :] = zero
    kpad[ATTN_BLOCK + s:2 * ATTN_BLOCK + s, :] = zero
    vpad[ATTN_BLOCK + s:2 * ATTN_BLOCK + s, :] = zero
    kpad[ATTN_BLOCK:ATTN_BLOCK + s, :] = kv_ref[:, 0:128]
    vpad[ATTN_BLOCK:ATTN_BLOCK + s, :] = kv_ref[:, 128:256]


def _attn_fwd(pq, pkv, pbz, sink):
    s = pq.shape[0]
    nb = s // ATTN_BLOCK
    bps = min(MAX_BLOCKS_IN_STEP, nb)

    def body(sink_ref, q_ref, z_ref, kv_ref, yb_ref, kpad, vpad, bias, e_scr):
        i = pl.program_id(0)

        @pl.when(i == 0)
        def _():
            _fill_padded(kv_ref, kpad, vpad, s)
            _fill_band_bias(bias, nb)

        row = lax.broadcasted_iota(jnp.int32, (ATTN_BLOCK, 128), 0)
        for b in range(bps):
            n = i * bps + b
            rows = slice(b * ATTN_BLOCK, (b + 1) * ATTN_BLOCK)
            start = pl.multiple_of(n * ATTN_BLOCK, ATTN_BLOCK)
            kw, vw = kpad[pl.ds(start, WINDOW_KEYS), :], vpad[pl.ds(start, WINDOW_KEYS), :]
            qf = q_ref[rows, :].astype(F32)
            variant = _bias_variant(n, nb)
            outs = []
            for g in range(2):
                e_bg = e_scr.at[2 * b + g]
                qt = (_heads_to_lanes(qf, g, row) * ATTN_SCALE).astype(BF16)
                inv, _ = _softmax_keys_major(_dot(kw, qt), bias, variant, _sink_row(sink_ref, g), e_bg)
                outs.append(_dot_tn(vw, e_bg[...]) * inv)
            attn = _lanes_to_heads(outs[0], outs[1], row)
            z = z_ref[rows, :].astype(F32)
            yb_ref[rows, :] = (attn * (z * _sigmoid(z))).astype(BF16)

    tq = bps * ATTN_BLOCK
    return pl.pallas_call(
        body, name="attn_fwd", grid=(s // tq,),
        out_shape=jax.ShapeDtypeStruct((s, 512), BF16),
        in_specs=[pl.BlockSpec(memory_space=pltpu.SMEM), _rows(tq, 512), _rows(tq, 512), _full((s, 256))],
        out_specs=_rows(tq, 512),
        scratch_shapes=[pltpu.VMEM((s + 2 * ATTN_BLOCK, 128), BF16)] * 2
        + [pltpu.VMEM((3, WINDOW_KEYS, STACKED), F32),
           pltpu.VMEM((2 * bps, WINDOW_KEYS, STACKED), BF16)],
        compiler_params=_params(),
    )(sink, pq, pbz, pkv)


def _mem_softmax_t(q, mk):
    sc = _dot_nt(mk, q) * MEM_SCALE
    e = jnp.exp(sc - jnp.max(sc, axis=0, keepdims=True))
    return e * (1.0 / jnp.sum(e, axis=0, keepdims=True))


def _mem_attn_fwd(pmq, pmz, mkv):
    s = pmq.shape[0]
    m = mkv.shape[0]
    tm = min(512, s)

    def body(q_ref, z_ref, mk_ref, mv_ref, ym_ref):
        z = z_ref[...].astype(F32)
        sz = z * _sigmoid(z)
        for h in range(MEM_HEADS):
            cols = slice(128 * h, 128 * h + 128)
            pt = _mem_softmax_t(q_ref[:, cols], mk_ref[:, cols])
            o = _dot_tn(pt.astype(BF16), mv_ref[:, cols])
            ym_ref[:, cols] = (o * sz[:, cols]).astype(BF16)

    return pl.pallas_call(
        body, name="mem_attn_fwd", grid=(s // tm,),
        out_shape=jax.ShapeDtypeStruct((s, 512), BF16),
        in_specs=[_rows(tm, 512), _rows(tm, 512), pl.BlockSpec((m, 512), lambda i: (0, 0)),
                  pl.BlockSpec((m, 512), lambda i: (0, 1))],
        out_specs=_rows(tm, 512),
        compiler_params=_params(),
    )(pmq, pmz, mkv, mkv)


def _mid(ya, yb, ym, pg, x, target, g_post, w_up, w_out):
    s = x.shape[0]
    tm = min(256, s)
    nt = s // tm

    def body(ya_ref, yb_ref, ym_ref, pg_ref, x_ref, t_ref, gp_ref, wup_hbm, wout_hbm,
             dg_ref, dya_ref, dyb_ref, dym_ref, dy_ref, loss_ref, ggp_ref, mb_ref, dob_ref, du_ref,
             wup_vm, wout_vm, sems):
        i = pl.program_id(0)
        _load_once([(wup_hbm.at[d], wup_vm.at[:, pl.ds(128 * d, 128)]) for d in range(N_DEV)]
                   + [(wout_hbm, wout_vm)], sems)

        @pl.when(i == 0)
        def _():
            loss_ref[...] = jnp.zeros_like(loss_ref)
            ggp_ref[...] = jnp.zeros_like(ggp_ref)

        ys = (ya_ref[...], yb_ref[...], ym_ref[...])
        us = [_dot(ys[k], wup_vm[512 * k:512 * k + 512, :]) for k in range(3)]
        gates = [_sigmoid(pg_ref[:, 1024 * k:1024 * k + 1024].astype(F32)) for k in range(3)]
        merged = gates[0] * us[0] + gates[1] * us[1] + gates[2] * us[2]
        mb = merged.astype(BF16)
        mb_ref[...] = mb
        out = _dot(mb, wout_vm[...])
        r = lax.rsqrt(jnp.mean(out * out, axis=-1, keepdims=True) + EPS)
        on = out * r
        gp = gp_ref[...]
        err = (x_ref[...] + on * gp) - t_ref[...]
        loss_ref[...] += 0.5 * jnp.sum(err * err) * (1.0 / D_MODEL)
        dy = err * (1.0 / D_MODEL)
        dy_ref[...] = dy
        ggp_ref[...] += jnp.sum(dy * on, axis=0, keepdims=True)
        a = dy * gp
        d_out = r * (a - on * jnp.mean(a * on, axis=-1, keepdims=True))
        dob = d_out.astype(BF16)
        dob_ref[...] = dob
        d_merged = _dot_nt(dob, wout_vm[...])
        d_refs = (dya_ref, dyb_ref, dym_ref)
        for k in range(3):
            g = gates[k]
            du_f = d_merged * g
            dg_ref[:, 1024 * k:1024 * k + 1024] = (du_f * us[k] * (1.0 - g)).astype(BF16)
            du = du_f.astype(BF16)
            du_ref[k] = du
            d_refs[k][...] = _dot_nt(du, wup_vm[512 * k:512 * k + 512, :]).astype(BF16)

    return pl.pallas_call(
        body, name="mid", grid=(nt,),
        out_shape=[jax.ShapeDtypeStruct((s, 3072), BF16)] + [jax.ShapeDtypeStruct((s, 512), BF16)] * 3
        + [jax.ShapeDtypeStruct((s, D_MODEL), F32), jax.ShapeDtypeStruct((8, 128), F32),
           jax.ShapeDtypeStruct((1, D_MODEL), F32), jax.ShapeDtypeStruct((s, D_MODEL), BF16),
           jax.ShapeDtypeStruct((s, D_MODEL), BF16), jax.ShapeDtypeStruct((3, s, D_MODEL), BF16)],
        in_specs=[_rows(tm, 512)] * 3 + [_rows(tm, 3072), _rows(tm, D_MODEL), _rows(tm, D_MODEL),
                                         _full((1, D_MODEL)), ANY, ANY],
        out_specs=[_rows(tm, 3072)] + [_rows(tm, 512)] * 3
        + [_rows(tm, D_MODEL), _full((8, 128)), _full((1, D_MODEL)), _rows(tm, D_MODEL), _rows(tm, D_MODEL),
           pl.BlockSpec((3, tm, D_MODEL), lambda i: (0, i, 0))],
        scratch_shapes=[pltpu.VMEM((1536, D_MODEL), BF16), pltpu.VMEM((D_MODEL, D_MODEL), BF16),
                        pltpu.SemaphoreType.DMA((N_DEV + 1,))],
        compiler_params=_params(),
    )(ya, yb, ym, pg, x, target, g_post, w_up, w_out)


def _gw_mid(mb, dob, ys, du):
    s = mb.shape[0]
    tn = 256
    n_out = D_MODEL // tn
    tiles = [(0, c0, 0) for c0 in range(0, D_MODEL, tn)]
    tiles += [(1 + k, c0, 1 + k) for k in range(3) for c0 in range(0, 512, tn)]
    n_t = len(tiles)

    def body(mb_hbm, ya_hbm, yb_hbm, ym_hbm, dob_hbm, du_hbm, out_hbm, up_hbm,
             lhs, rhs, res_out, res_up, in_sems, rhs_sems, out_sems):
        lhs_hbm = (mb_hbm, ya_hbm, yb_hbm, ym_hbm)

        def load(t):
            a, c0, _ = tiles[t]
            return pltpu.make_async_copy(lhs_hbm[a].at[:, pl.ds(c0, tn)], lhs.at[t & 1], in_sems.at[t & 1])

        def load_rhs(g):
            src = dob_hbm if g == 0 else du_hbm.at[g - 1]
            return pltpu.make_async_copy(src, rhs.at[g & 1], rhs_sems.at[g & 1])

        def store(t):
            if t < n_out:
                return pltpu.make_async_copy(res_out.at[t & 1], out_hbm.at[pl.ds(t * tn, tn)], out_sems.at[t & 1])
            rows = pl.ds((t - n_out) * tn, tn)
            return pltpu.make_async_copy(res_up.at[t & 1], up_hbm.at[:, rows, :], out_sems.at[t & 1])

        load_rhs(0).start()
        load(0).start()
        for t, (_, _, g) in enumerate(tiles):
            new_rhs = t == 0 or tiles[t - 1][2] != g
            if t + 1 < n_t:
                load(t + 1).start()
            if new_rhs and g < 3:
                load_rhs(g + 1).start()
            load(t).wait()
            if new_rhs:
                load_rhs(g).wait()
            if t >= 2:
                store(t - 2).wait()
            r = _dot_tn(lhs[t & 1], rhs[g & 1])
            if t < n_out:
                res_out[t & 1] = r.astype(BF16)
            else:
                for d in range(N_DEV):
                    res_up[t & 1, d] = r[:, 128 * d:128 * d + 128].astype(BF16)
            store(t).start()
        store(n_t - 2).wait()
        store(n_t - 1).wait()

    return pl.pallas_call(
        body, name="gw_mid",
        out_shape=[jax.ShapeDtypeStruct((D_MODEL, D_MODEL), BF16), jax.ShapeDtypeStruct((N_DEV, 1536, 128), BF16)],
        in_specs=[ANY] * 6, out_specs=[ANY, ANY],
        scratch_shapes=[pltpu.VMEM((2, s, tn), BF16), pltpu.VMEM((2, s, D_MODEL), BF16),
                        pltpu.VMEM((2, tn, D_MODEL), BF16), pltpu.VMEM((2, N_DEV, tn, 128), BF16),
                        pltpu.SemaphoreType.DMA((2,)), pltpu.SemaphoreType.DMA((2,)),
                        pltpu.SemaphoreType.DMA((2,))],
        compiler_params=pltpu.CompilerParams(vmem_limit_bytes=CALL_VMEM_MB * 1024 * 1024),
    )(mb, *ys, dob, du)


def _conv_bwd(after, pa, dya, w_conv):
    s = pa.shape[0]
    tm = min(512, s)
    nt = s // tm

    def body(after_ref, pa_ref, pp_ref, pn_ref, d_ref, dp_ref, dn_ref, w_ref, da_ref, gw_ref):
        i = pl.program_id(0)
        first, last = i == 0, i == nt - 1

        @pl.when(first)
        def _():
            gw_ref[...] = jnp.zeros_like(gw_ref)

        w = w_ref[...]
        prev_row = pp_ref[...].astype(F32)[15:16, :]
        next_row = pn_ref[...].astype(F32)[0:1, :]
        b, c, u, z, cu, cu_m1, cu_p1, y, sig, row = _conv_common(
            pa_ref[...].astype(F32),
            jnp.where(first, 0.0, prev_row[:, 512:1024] * prev_row[:, 1024:1536]),
            jnp.where(last, 0.0, next_row[:, 512:1024] * next_row[:, 1024:1536]), w, tm)
        sz = z * sig
        dya_t = d_ref[...].astype(F32)
        d_y = dya_t * b * sz

        def halo_dy(p_row, d_row):
            zz = p_row[:, 1536:2048]
            return d_row * p_row[:, 0:512] * (zz * _sigmoid(zz))

        dy_prev = jnp.where(first, 0.0, halo_dy(prev_row, dp_ref[...].astype(F32)[15:16, :]))
        dy_next = jnp.where(last, 0.0, halo_dy(next_row, dn_ref[...].astype(F32)[0:1, :]))
        dy_m1 = jnp.where(row == 0, dy_prev, pltpu.roll(d_y, 1, 0))
        dy_p1 = jnp.where(row == tm - 1, dy_next, pltpu.roll(d_y, tm - 1, 0))
        d_cu = dy_p1 * w[0:1] + d_y * w[1:2] + dy_m1 * w[2:3]
        da_ref[:, 0:512] = (dya_t * y * sz).astype(BF16)
        da_ref[:, 512:1024] = (d_cu * u).astype(BF16)
        da_ref[:, 1024:1536] = (d_cu * c).astype(BF16)
        da_ref[:, 1536:2048] = (dya_t * b * y * (sig + sz * (1.0 - sig))).astype(BF16)
        gw_ref[0:1, :] += jnp.sum(d_y * cu_m1, axis=0, keepdims=True)
        gw_ref[1:2, :] += jnp.sum(d_y * cu, axis=0, keepdims=True)
        gw_ref[2:3, :] += jnp.sum(d_y * cu_p1, axis=0, keepdims=True)

    prev, nxt = _halo_specs(s, tm, 16, 2048)
    dprev, dnxt = _halo_specs(s, tm, 16, 512)
    return pl.pallas_call(
        body, name="conv_bwd", grid=(nt,),
        out_shape=[jax.ShapeDtypeStruct((s, 2048), BF16), jax.ShapeDtypeStruct((8, 512), F32)],
        in_specs=[_full(TOKEN), _rows(tm, 2048), prev, nxt, _rows(tm, 512), dprev, dnxt, _full((3, 512))],
        out_specs=[_rows(tm, 2048), _full((8, 512))],
        compiler_params=_params(),
    )(after, pa, pa, pa, dya, dya, dya, w_conv)


def _attn_bwd(after, pq, pkv, pbz, dyb, sink, tabs):
    s = pq.shape[0]
    nb = s // ATTN_BLOCK
    bps = min(MAX_BLOCKS_IN_STEP, nb)

    def body(sink_ref, after_ref, q_ref, z_ref, d_ref, kv_ref, t_ref,
             dq_ref, dz_ref, dkv_ref, gs_ref, kpad, vpad, dk_acc, dv_acc, bias, e_scr, ds_scr):
        i = pl.program_id(0)

        @pl.when(i == 0)
        def _():
            _fill_padded(kv_ref, kpad, vpad, s)
            _fill_band_bias(bias, nb)
            dk_acc[...] = jnp.zeros_like(dk_acc)
            dv_acc[...] = jnp.zeros_like(dv_acc)
            gs_ref[...] = jnp.zeros_like(gs_ref)

        row = lax.broadcasted_iota(jnp.int32, (ATTN_BLOCK, 128), 0)
        for b in range(bps):
            n = i * bps + b
            rows = slice(b * ATTN_BLOCK, (b + 1) * ATTN_BLOCK)
            start = pl.multiple_of(n * ATTN_BLOCK, ATTN_BLOCK)
            kw, vw = kpad[pl.ds(start, WINDOW_KEYS), :], vpad[pl.ds(start, WINDOW_KEYS), :]
            qf = q_ref[rows, :].astype(F32)
            variant = _bias_variant(n, nb)
            z = z_ref[rows, :].astype(F32)
            sig = _sigmoid(z)
            dyb_t = d_ref[rows, :].astype(F32)
            d_attn = dyb_t * (z * sig)
            outs, dqs = [], []
            dk_w = jnp.zeros((WINDOW_KEYS, 128), F32)
            dv_w = jnp.zeros((WINDOW_KEYS, 128), F32)
            for g in range(2):
                e_bg, ds_bg = e_scr.at[2 * b + g], ds_scr.at[2 * b + g]
                qt = _heads_to_lanes(qf, g, row)
                inv, p_sink = _softmax_keys_major(
                    _dot(kw, (qt * ATTN_SCALE).astype(BF16)), bias, variant, _sink_row(sink_ref, g), e_bg)
                ot = _dot_tn(vw, e_bg[...]) * inv
                outs.append(ot)
                dot_ = _heads_to_lanes(d_attn, g, row)
                delta = jnp.sum(dot_ * ot, axis=0, keepdims=True)
                dpt = _dot(vw, dot_.astype(BF16))
                for k in range(WINDOW_KEYS // KEY_CHUNK):
                    rw = slice(k * KEY_CHUNK, (k + 1) * KEY_CHUNK)
                    ds_bg[rw, :] = (e_bg[rw, :].astype(F32) * (dpt[rw] - delta)).astype(BF16)
                sink_part = p_sink * delta
                for j in range(4):
                    h = 4 * g + j
                    gs_ref[h:h + 1, :] -= jnp.sum(sink_part[:, 128 * j:128 * j + 128])
                dqs.append(_dot_tn(kw, ds_bg[...]) * (inv * ATTN_SCALE))
                dk_w += _dot_nt(ds_bg[...], (qt * inv).astype(BF16)) * ATTN_SCALE
                dv_w += _dot_nt(e_bg[...], (dot_ * inv).astype(BF16))
            dk_acc[pl.ds(start, WINDOW_KEYS), :] += dk_w
            dv_acc[pl.ds(start, WINDOW_KEYS), :] += dv_w
            attn = _lanes_to_heads(outs[0], outs[1], row)
            dz_ref[rows, :] = (dyb_t * attn * (sig * (1.0 + z * (1.0 - sig)))).astype(BF16)
            dq = _lanes_to_heads(dqs[0], dqs[1], row)
            trows = pl.ds(start, ATTN_BLOCK)
            cs, s1, s2 = t_ref[0, trows, :], t_ref[1, trows, :], t_ref[2, trows, :]
            for blk in range(4):
                cols = slice(128 * blk, 128 * blk + 128)
                dq_ref[rows, cols] = _rope_t(dq[:, cols], cs, s1, s2).astype(BF16)

        @pl.when(i == nb // bps - 1)
        def _():
            dk = dk_acc[ATTN_BLOCK:ATTN_BLOCK + s, :]
            dkv_ref[:, 0:128] = _rope_t(dk, t_ref[0], t_ref[1], t_ref[2]).astype(BF16)
            dkv_ref[:, 128:256] = dv_acc[ATTN_BLOCK:ATTN_BLOCK + s, :].astype(BF16)

    tq = bps * ATTN_BLOCK
    tile = _rows(tq, 512)
    return pl.pallas_call(
        body, name="attn_bwd", grid=(s // tq,),
        out_shape=[jax.ShapeDtypeStruct((s, 512), BF16), jax.ShapeDtypeStruct((s, 512), BF16),
                   jax.ShapeDtypeStruct((s, 256), BF16), jax.ShapeDtypeStruct((8, 128), F32)],
        in_specs=[pl.BlockSpec(memory_space=pltpu.SMEM), _full(TOKEN), tile, tile, tile, _full((s, 256)),
                  _full((3, s, 128))],
        out_specs=[tile, tile, _full((s, 256)), _full((8, 128))],
        scratch_shapes=[pltpu.VMEM((s + 2 * ATTN_BLOCK, 128), BF16)] * 2
        + [pltpu.VMEM((s + 2 * ATTN_BLOCK, 128), F32)] * 2
        + [pltpu.VMEM((3, WINDOW_KEYS, STACKED), F32)]
        + [pltpu.VMEM((2 * bps, WINDOW_KEYS, STACKED), BF16)] * 2,
        compiler_params=_params(),
    )(sink, after, pq, pbz, dyb, pkv, tabs)


def _mem_attn_bwd(pmq, pmz, mkv, dym):
    s = pmq.shape[0]
    m = mkv.shape[0]
    tm = min(512, s)

    def body(q_ref, z_ref, d_ref, mk_ref, mv_ref, dq_ref, dz_ref, dmkv_ref):
        @pl.when(pl.program_id(0) == 0)
        def _():
            dmkv_ref[...] = jnp.zeros_like(dmkv_ref)

        z = z_ref[...].astype(F32)
        sig = _sigmoid(z)
        dym_t = d_ref[...].astype(F32)
        d_attn = dym_t * (z * sig)
        dsilu = sig * (1.0 + z * (1.0 - sig))
        for h in range(MEM_HEADS):
            cols = slice(128 * h, 128 * h + 128)
            q, mk, mv = q_ref[:, cols], mk_ref[:, cols], mv_ref[:, cols]
            pt = _mem_softmax_t(q, mk)
            pb = pt.astype(BF16)
            o = _dot_tn(pb, mv)
            dob = d_attn[:, cols].astype(BF16)
            dpt = _dot_nt(mv, dob)
            dst = (pt * (dpt - jnp.sum(pt * dpt, axis=0, keepdims=True))).astype(BF16)
            dq_ref[:, cols] = (_dot_tn(dst, mk) * MEM_SCALE).astype(BF16)
            dz_ref[:, cols] = (dym_t[:, cols] * o * dsilu[:, cols]).astype(BF16)
            dmkv_ref[:, cols] += _dot(dst, q) * MEM_SCALE
            dmkv_ref[:, 512 + 128 * h:512 + 128 * h + 128] += _dot(pb, dob)

    return pl.pallas_call(
        body, name="mem_attn_bwd", grid=(s // tm,),
        out_shape=[jax.ShapeDtypeStruct((s, 512), BF16), jax.ShapeDtypeStruct((s, 512), BF16),
                   jax.ShapeDtypeStruct((m, D_MODEL), F32)],
        in_specs=[_rows(tm, 512), _rows(tm, 512), _rows(tm, 512), pl.BlockSpec((m, 512), lambda i: (0, 0)),
                  pl.BlockSpec((m, 512), lambda i: (0, 1))],
        out_specs=[_rows(tm, 512), _rows(tm, 512), _full((m, D_MODEL))],
        compiler_params=_params(),
    )(pmq, pmz, dym, mkv, mkv)


def _mem_kv_bwd(mem, g_mem, mn, dmkv, w_mkv):
    m = mem.shape[0]

    def body(mem_ref, g_ref, mn_ref, d_ref, w_ref, gw_ref, gg_ref):
        db = d_ref[...].astype(BF16)
        gw_ref[...] = _dot_tn(mn_ref[...], db).astype(BF16)
        d_mn = _dot_nt(db, w_ref[...])
        xf = mem_ref[...]
        r = lax.rsqrt(jnp.mean(xf * xf, axis=-1, keepdims=True) + EPS)
        gg_ref[...] = jnp.sum(d_mn * (xf * r), axis=0, keepdims=True)

    return pl.pallas_call(
        body, name="mem_kv_bwd", grid=(1,),
        out_shape=[jax.ShapeDtypeStruct((D_MODEL, D_MODEL), BF16), jax.ShapeDtypeStruct((1, D_MODEL), F32)],
        in_specs=[_full((m, D_MODEL)), _full((1, D_MODEL)), _full((m, D_MODEL)), _full((m, D_MODEL)),
                  _full((D_MODEL, D_MODEL))],
        out_specs=[_full((D_MODEL, D_MODEL)), _full((1, D_MODEL))],
        compiler_params=_params(),
    )(mem, g_mem, mn, dmkv, w_mkv)


def _dh_bwd(after, dparts, x, dy, g_pre, w_int):
    s = x.shape[0]
    tm = min(256, s)
    n_t = s // tm
    tiled = list(dparts) + [x, dy]
    n_in = len(tiled)

    def body(after_ref, *refs):
        in_hbm = refs[:n_in]
        g_ref, w_hbm, gx_hbm, gg_ref = refs[n_in:n_in + 4]
        bufs = refs[n_in + 4:2 * n_in + 4]
        w_vm, gx_buf, in_sems, out_sems, w_sem = refs[2 * n_in + 4:]

        def loads(t, slot):
            rows = pl.ds(pl.multiple_of(t * tm, tm), tm)
            return [pltpu.make_async_copy(in_hbm[k].at[rows], bufs[k].at[slot], in_sems.at[slot, k])
                    for k in range(n_in)]

        def store(t, slot):
            rows = pl.ds(pl.multiple_of(t * tm, tm), tm)
            return pltpu.make_async_copy(gx_buf.at[slot], gx_hbm.at[rows], out_sems.at[slot])

        w_copy = pltpu.make_async_copy(w_hbm, w_vm, w_sem)
        w_copy.start()
        for cp in loads(0, 0):
            cp.start()
        w_copy.wait()

        def step(t, gg):
            slot = t & 1

            @pl.when(t + 1 < n_t)
            def _():
                for cp in loads(t + 1, 1 - slot):
                    cp.start()

            for cp in loads(t, slot):
                cp.wait()

            @pl.when(t >= 2)
            def _():
                store(t - 2, slot).wait()

            d_h = jnp.zeros((tm, D_MODEL), F32)
            for k, (r0, width) in enumerate(SEGS):
                for c0 in range(0, width, 512):
                    cw = min(512, width - c0)
                    d_h += _dot(bufs[k].at[slot][:, c0:c0 + cw], w_vm[r0 + c0:r0 + c0 + cw, :])
            xf = bufs[n_in - 2][slot]
            r = lax.rsqrt(jnp.mean(xf * xf, axis=-1, keepdims=True) + EPS)
            xn = xf * r
            a = d_h * g_ref[...]
            gx_buf[slot] = r * (a - xn * jnp.mean(a * xn, axis=-1, keepdims=True)) + bufs[n_in - 1][slot]
            store(t, slot).start()
            return gg + jnp.sum(d_h * xn, axis=0, keepdims=True)

        gg_ref[...] = lax.fori_loop(0, n_t, step, jnp.zeros((1, D_MODEL), F32))
        store(n_t - 2, (n_t - 2) & 1).wait()
        store(n_t - 1, (n_t - 1) & 1).wait()

    vmem = pl.BlockSpec(memory_space=pltpu.VMEM)
    return pl.pallas_call(
        body, name="dh_bwd",
        out_shape=[jax.ShapeDtypeStruct((s, D_MODEL), F32), jax.ShapeDtypeStruct((1, D_MODEL), F32)],
        in_specs=[ANY] * (1 + n_in) + [vmem, ANY],
        out_specs=[ANY, vmem],
        scratch_shapes=[pltpu.VMEM((2, tm) + a.shape[1:], a.dtype) for a in tiled]
        + [pltpu.VMEM((IN_WIDTH, D_MODEL), BF16), pltpu.VMEM((2, tm, D_MODEL), F32),
           pltpu.SemaphoreType.DMA((2, n_in)), pltpu.SemaphoreType.DMA((2,)), pltpu.SemaphoreType.DMA(())],
        compiler_params=pltpu.CompilerParams(vmem_limit_bytes=CALL_VMEM_MB * 1024 * 1024),
    )(after, *dparts, x, dy, g_pre, w_int)


def _gw_in(dparts, h):
    s = h.shape[0]
    tn = 256
    tiles = [(a, c0) for a, (_, width) in enumerate(SEGS) for c0 in range(0, width, tn)]
    n_t = len(tiles)
    assert n_t * tn == IN_WIDTH

    def body(*refs):
        d_hbm = refs[:7]
        h_hbm, out_hbm, lhs, h_vm, res, in_sems, out_sems, h_sem = refs[7:]

        def load(a, c0, slot):
            return pltpu.make_async_copy(d_hbm[a].at[:, pl.ds(c0, tn)], lhs.at[slot], in_sems.at[slot])

        def store(t, slot):
            rows = pl.ds(pl.multiple_of(t * tn, tn), tn)
            return pltpu.make_async_copy(res.at[slot], out_hbm.at[rows], out_sems.at[slot])

        def start_load(t, slot):
            for k, (a, c0) in enumerate(tiles):
                @pl.when(t == k)
                def _(a=a, c0=c0):
                    load(a, c0, slot).start()

        h_copy = pltpu.make_async_copy(h_hbm, h_vm, h_sem)
        h_copy.start()
        load(*tiles[0], 0).start()
        h_copy.wait()

        def step(t, carry):
            slot = t & 1

            @pl.when(t + 1 < n_t)
            def _():
                start_load(t + 1, 1 - slot)

            load(*tiles[0], slot).wait()

            @pl.when(t >= 2)
            def _():
                store(t - 2, slot).wait()

            res[slot] = _dot_tn(lhs[slot], h_vm[...]).astype(BF16)
            store(t, slot).start()
            return carry

        lax.fori_loop(0, n_t, step, 0)
        store(n_t - 2, (n_t - 2) & 1).wait()
        store(n_t - 1, (n_t - 1) & 1).wait()

    return pl.pallas_call(
        body, name="gw_in",
        out_shape=jax.ShapeDtypeStruct((IN_WIDTH, D_MODEL), BF16),
        in_specs=[ANY] * 8, out_specs=ANY,
        scratch_shapes=[pltpu.VMEM((2, s, tn), BF16), pltpu.VMEM((s, D_MODEL), BF16),
                        pltpu.VMEM((2, tn, D_MODEL), BF16), pltpu.SemaphoreType.DMA((2,)),
                        pltpu.SemaphoreType.DMA((2,)), pltpu.SemaphoreType.DMA(())],
        compiler_params=pltpu.CompilerParams(vmem_limit_bytes=CALL_VMEM_MB * 1024 * 1024),
    )(*dparts, h)


def _adamw_math(w, g, m, v):
    m2 = ADAM_B1 * m + (1.0 - ADAM_B1) * g
    v2 = ADAM_B2 * v + (1.0 - ADAM_B2) * (g * g)
    m_hat = m2 / (1.0 - ADAM_B1 ** ADAM_STEP)
    v_hat = v2 / (1.0 - ADAM_B2 ** ADAM_STEP)
    delta = -ADAM_LR * (m_hat / (jnp.sqrt(v_hat) + ADAM_EPS) + ADAM_WD * w)
    return delta, m2, v2


def _sum_adamw(after, own, land, chip, block, w, m, v, name, tiles=1):
    r, c = w.shape
    rt = r // tiles

    def body(c_ref, after_ref, own_ref, l1_ref, l2_ref, l3_ref, w_ref, m_ref, v_ref, g_ref, d_ref, m2_ref, v2_ref):
        g = own_ref[...].astype(F32)
        for l_ref in (l1_ref, l2_ref, l3_ref):
            g += l_ref[...].astype(F32)
        g_ref[...] = g
        d_ref[...], m2_ref[...], v2_ref[...] = _adamw_math(w_ref[...], g, m_ref[...], v_ref[...])

    def share(k):
        return pl.BlockSpec((None, rt, c), lambda i, c_ref: (jnp.bitwise_xor(c_ref[0], k), block * tiles + i, 0))

    spec = pl.BlockSpec((rt, c), lambda i, c_ref: (i, 0))
    grid_spec = pltpu.PrefetchScalarGridSpec(
        num_scalar_prefetch=1, grid=(tiles,),
        in_specs=[ANY, share(0), share(1), share(2), share(3)] + [spec] * 3, out_specs=[spec] * 4)
    return pl.pallas_call(
        body, name=name, grid_spec=grid_spec,
        out_shape=[jax.ShapeDtypeStruct((r, c), F32)] * 4,
        compiler_params=_params(),
    )(chip, after, own, land, land, land, w, m, v)


def _sum_adamw_group(after, items, chip, name):
    k = len(items)

    def body(c_ref, after_ref, *refs):
        shares, wmv, outs = refs[:4 * k], refs[4 * k:7 * k], refs[7 * k:]
        for j in range(k):
            g = shares[4 * j][...].astype(F32)
            for l_ref in shares[4 * j + 1:4 * j + 4]:
                g += l_ref[...].astype(F32)
            outs[4 * j][...] = g
            outs[4 * j + 1][...], outs[4 * j + 2][...], outs[4 * j + 3][...] = _adamw_math(
                wmv[3 * j][...], g, wmv[3 * j + 1][...], wmv[3 * j + 2][...])

    def share(shape, block, q):
        return pl.BlockSpec((None,) + shape, lambda i, c_ref: (jnp.bitwise_xor(c_ref[0], q), block, 0))

    in_specs, args = [ANY], [after]
    for own, land, block, w, m, v in items:
        in_specs += [share(w.shape, block, q) for q in range(4)]
        args += [own, land, land, land]
    for own, land, block, w, m, v in items:
        in_specs += [pl.BlockSpec(w.shape, lambda i, c_ref: (0, 0))] * 3
        args += [w, m, v]
    out_specs = [pl.BlockSpec(w.shape, lambda i, c_ref: (0, 0)) for _, _, _, w, _, _ in items for _ in range(4)]
    res = pl.pallas_call(
        body, name=name,
        grid_spec=pltpu.PrefetchScalarGridSpec(num_scalar_prefetch=1, grid=(1,), in_specs=in_specs,
                                               out_specs=out_specs),
        out_shape=[jax.ShapeDtypeStruct(w.shape, F32) for _, _, _, w, _, _ in items for _ in range(4)],
        compiler_params=_params(),
    )(chip, *args)
    return [res[4 * j:4 * j + 4] for j in range(k)]


def _small_pack(parts):
    def pack_body(gpre_ref, gconv_ref, gsink_ref, gmem_ref, gpost_ref, loss_ref, pack):
        lane = lax.broadcasted_iota(jnp.int32, (1, 128), 1)
        sink_row = jnp.zeros((1, 128), F32)
        for h in range(8):
            sink_row = jnp.where(lane == h, gsink_ref[h:h + 1, :], sink_row)
        pack[...] = jnp.zeros_like(pack)
        pack[0:1, :] = gpre_ref[...]
        pack[1:2, :] = gmem_ref[...]
        pack[2:3, :] = gpost_ref[...]
        pack[3:6, 0:512] = gconv_ref[0:3, :]
        pack[6:7, 0:128] = sink_row
        pack[7:8, 0:128] = loss_ref[0:1, :]

    return pl.pallas_call(
        pack_body, name="small_pack", grid=(1,),
        out_shape=jax.ShapeDtypeStruct((8, D_MODEL), F32),
        in_specs=[_full(p.shape) for p in parts], out_specs=_full((8, D_MODEL)),
    )(*parts)


def _small_apply(packs, ws, ms, vs):
    def apply(p_ref, *refs):
        w_refs, m_refs, v_refs = refs[0:5], refs[5:10], refs[10:15]
        loss_out = refs[15]
        g_outs, d_outs, m_outs, v_outs = refs[16:21], refs[21:26], refs[26:31], refs[31:36]
        x, y, c = _my_place()
        tot = p_ref[0]
        for d in range(1, N_DEV):
            tot = tot + p_ref[d]
        conv = pltpu.roll(tot[:, 0:512], (512 - 64 * (4 * x + 2 * y + c)) % 512, 1)[3:6, 0:64]
        grads = (tot[0:1, :], conv, tot[6:7, 0:8], tot[1:2, :], tot[2:3, :])
        loss_out[...] = tot[7:8, 0:128]
        for j in range(5):
            if len(w_refs[j].shape) == 3:
                for k in range(w_refs[j].shape[0]):
                    g_outs[j][k] = grads[j][k:k + 1]
                    d_outs[j][k], m_outs[j][k], v_outs[j][k] = _adamw_math(
                        w_refs[j][k], grads[j][k:k + 1], m_refs[j][k], v_refs[j][k])
                continue
            g_outs[j][...] = grads[j]
            d_outs[j][...], m_outs[j][...], v_outs[j][...] = _adamw_math(
                w_refs[j][...], grads[j], m_refs[j][...], v_refs[j][...])

    specs = [_full(w.shape) for w in ws]
    res = pl.pallas_call(
        apply, name="small_apply", grid=(1,),
        out_shape=[jax.ShapeDtypeStruct((1, 128), F32)] + [jax.ShapeDtypeStruct(w.shape, F32) for w in ws] * 4,
        in_specs=[_full((N_DEV, 8, D_MODEL))] + specs * 3,
        out_specs=[_full((1, 128))] + specs * 4,
    )(packs, *ws, *ms, *vs)
    return res[0], res[1:6], res[6:11], res[11:16], res[16:21]


def kernel(x, mem, g_pre, w_in, w_conv, attn_sink, g_mem, w_mem_kv, w_up_a, w_up_b, w_up_m, w_out, g_post, loss_target, m_g_pre, m_w_in, m_w_conv, m_attn_sink, m_g_mem, m_w_mem_kv, m_w_up_a, m_w_up_b, m_w_up_m, m_w_out, m_g_post, v_g_pre, v_w_in, v_w_conv, v_attn_sink, v_g_mem, v_w_mem_kv, v_w_up_a, v_w_up_b, v_w_up_m, v_w_out, v_g_post):
    s = x.shape[1]
    x2, mem2, tgt2 = x[0], mem[0], loss_target[0]
    me = 4 * lax.axis_index("x") + 2 * lax.axis_index("y") + lax.axis_index("c")

    w_conv_loc = jnp.zeros((8, 128), F32).at[:3, :64].set(w_conv[0])
    w_int_g, w_conv_g, tabs, w_mkv_loc, w_out_loc, w_up_loc = _all_gather(
        [w_in[0].T.astype(BF16), w_conv_loc], "gather_w_in",
        splits=[[(112 * k, 112) for k in range(7)] + [(784, 144)], [(0, 8)]],
        side=_gather_side(s, w_mem_kv[0], w_out[0], (w_up_a[0], w_up_b[0], w_up_m[0])))
    w_int = w_int_g.reshape(IN_WIDTH, D_MODEL)
    w_conv_f = w_conv_g[:, :3, :64].transpose(1, 0, 2).reshape(3, 512)
    late = _gather_start([w_mkv_loc, w_out_loc, w_up_loc], me, "gather_late_start")
    sink = attn_sink[0]

    h, pa, pq, pkv, pbz, pmq, pmz, pg = _proj_fwd(late[4], x2, g_pre, w_int, tabs)
    ya = _conv_fwd(pa, w_conv_f)
    yb = _attn_fwd(pq, pkv, pbz, sink)
    w_mkv_g, w_out_g, w_up_g = _gather_wait(*late[:4], yb, "gather_late_wait")
    w_mkv = w_mkv_g.reshape(D_MODEL, D_MODEL)
    w_out_f = w_out_g.reshape(D_MODEL, D_MODEL)
    mn, mkv = _mem_kv_fwd(mem2, g_mem, w_mkv)
    ym = _mem_attn_fwd(pmq, pmz, mkv)
    dg, dya, dyb, dym, dy, loss_p, gg_post, mb, dob, du = _mid(ya, yb, ym, pg, x2, tgt2, g_post, w_up_g, w_out_f)
    gw_out, gw_up = _gw_mid(mb, dob, (ya, yb, ym), du)

    core = lax.axis_index("c").astype(jnp.int32).reshape(1)
    chip = (2 * lax.axis_index("x") + lax.axis_index("y")).astype(jnp.int32).reshape(1)

    dmq, dmz, dmkv = _mem_attn_bwd(pmq, pmz, mkv, dym)
    gw_mkv, gg_mem = _mem_kv_bwd(mem2, g_mem, mn, dmkv, w_mkv)
    shares1 = [gw_mkv.reshape(N_DEV, 128, D_MODEL), gw_out.reshape(N_DEV, 128, D_MODEL), gw_up]
    sib = _split_start(_sibling_copies, N_CHIPS, shares1,
                       [lax.empty((N_CHIPS,) + a.shape[1:], a.dtype) for a in shares1], "grads_to_sibling_small_start")
    da, gw_conv = _conv_bwd(sib[4], pa, dya, w_conv_f)
    shares1, from_sibling = _split_wait(_sibling_copies, N_CHIPS, *sib[:4], da, "grads_to_sibling_small_wait")
    send1, recv1, srcs1, lands1, token1 = _chip_exchange_start(
        _pair_add(shares1, from_sibling, core, "grads_pair_add_small"), "grads_to_chips_start_small")
    dq, dbz, dkv, g_sink = _attn_bwd(token1, pq, pkv, pbz, dyb, sink, tabs)
    dparts = (da, dq, dkv, dbz, dmq, dmz, dg)
    gw_int = _gw_in(dparts, h)
    send2, recv2, srcs2, lands2, token2 = _chip_exchange_start(
        [_sibling_reduce(gw_int.reshape(N_DEV, SHARD_IN, D_MODEL), "grads_sibling_reduce_w_in")],
        "grads_to_chips_start_w_in")
    grad_x, gg_pre = _dh_bwd(token2, dparts, x2, dy, g_pre, w_int)
    (o_mkv, o_out, o_up, o_int), (l_mkv, l_out, l_up, l_int) = _chip_exchange_wait(
        send1 + send2, recv1 + recv2, srcs1 + srcs2, lands1 + lands2, grad_x, "grads_to_chips_wait")

    small = _gather_start([_small_pack((gg_pre, gw_conv, g_sink, gg_mem, gg_post, loss_p))], me,
                          "small_gather_start")

    w_in_t = _sum_adamw(small[4], o_int, l_int, chip, 0, w_in[0].T, m_w_in[0].T, v_w_in[0].T, "adamw_w_in", tiles=2)
    g_w_in, d_w_in, nm_w_in, nv_w_in = (t.T for t in w_in_t)
    (g_mkv, d_mkv, nm_mkv, nv_mkv), (g_out, d_out, nm_out, nv_out), *up = _sum_adamw_group(
        w_in_t[0],
        [(o_mkv, l_mkv, 0, w_mem_kv[0], m_w_mem_kv[0], v_w_mem_kv[0]),
         (o_out, l_out, 0, w_out[0], m_w_out[0], v_w_out[0]),
         (o_up, l_up, 0, w_up_a[0], m_w_up_a[0], v_w_up_a[0]),
         (o_up, l_up, 1, w_up_b[0], m_w_up_b[0], v_w_up_b[0]),
         (o_up, l_up, 2, w_up_m[0], m_w_up_m[0], v_w_up_m[0])], chip, "adamw_mid_weights")

    (packs,) = _gather_wait(*small[:4], g_out, "small_gather_wait")

    def taps_first(a):
        return a.transpose(1, 0, 2)

    loss_row, small_g, sd, sm, sv = _small_apply(
        packs, [g_pre, taps_first(w_conv), attn_sink, g_mem, g_post],
        [m_g_pre, taps_first(m_w_conv), m_attn_sink, m_g_mem, m_g_post],
        [v_g_pre, taps_first(v_w_conv), v_attn_sink, v_g_mem, v_g_post])
    loss = loss_row[0, 0]
    g_g_pre, g_conv, g_sink_tot, g_g_mem, g_g_post = small_g

    def lead(a):
        return a[None]

    grads = [g_g_pre, lead(g_w_in), taps_first(g_conv), g_sink_tot, g_g_mem, lead(g_mkv), lead(up[0][0]),
             lead(up[1][0]), lead(up[2][0]), lead(g_out), g_g_post]

    def assemble(small, big_in, big_mkv, big_up, big_out):
        return [small[0], lead(big_in), taps_first(small[1]), small[2], small[3], lead(big_mkv), lead(big_up[0]),
                lead(big_up[1]), lead(big_up[2]), lead(big_out), small[4]]

    deltas = assemble(sd, d_w_in, d_mkv, [u[1] for u in up], d_out)
    new_m = assemble(sm, nm_w_in, nm_mkv, [u[2] for u in up], nm_out)
    new_v = assemble(sv, nv_w_in, nv_mkv, [u[3] for u in up], nv_out)
    return (loss, grad_x[None], *grads, *deltas, *new_m, *new_v)
```

```python
import functools

import jax
import jax.numpy as jnp
from jax import lax
from jax.experimental import pallas as pl
from jax.experimental.pallas import tpu as pltpu

F32 = jnp.float32
BF16 = jnp.bfloat16
MESH = pl.DeviceIdType.MESH

N_DEV = 8
D_MODEL = 1024
EPS = 1e-6
ROPE_THETA = 500000.0
ROT_DIM = 16
HEAD_DIM = 64
ATTN_BLOCK = 128
MEM_HEADS = 4
MEM_HEAD_DIM = 128
ATTN_SCALE = HEAD_DIM ** -0.5
MEM_SCALE = MEM_HEAD_DIM ** -0.5

ADAM_LR = 0.001
ADAM_B1 = 0.9
ADAM_B2 = 0.999
ADAM_EPS = 1e-08
ADAM_WD = 0.01
ADAM_STEP = 10

SEG_A = (0, 2048)
SEG_BQ = (2048, 512)
SEG_BKV = (2560, 256)
SEG_BZ = (2816, 512)
SEG_MQ = (3328, 512)
SEG_MZ = (3840, 512)
SEG_G = (4352, 3072)
SEGS = (SEG_A, SEG_BQ, SEG_BKV, SEG_BZ, SEG_MQ, SEG_MZ, SEG_G)
IN_WIDTH = 7424
SHARD_IN = IN_WIDTH // N_DEV

V7X_VMEM_BYTES = 64 * 1024 * 1024
CALL_VMEM_MB = 57
ANY = pl.BlockSpec(memory_space=pl.ANY)


def _params():
    assert CALL_VMEM_MB * 1024 * 1024 < V7X_VMEM_BYTES
    return pltpu.CompilerParams(dimension_semantics=("arbitrary",), vmem_limit_bytes=CALL_VMEM_MB * 1024 * 1024)


def _full(shape):
    zeros = (0,) * len(shape)
    return pl.BlockSpec(shape, lambda i: zeros)


def _rows(tm, width):
    return pl.BlockSpec((tm, width), lambda i: (i, 0))


def _dot(a, b):
    return jnp.dot(a, b, preferred_element_type=F32)


def _dot_nt(a, b):
    return lax.dot_general(a, b, (((1,), (1,)), ((), ())), preferred_element_type=F32)


def _dot_tn(a, b):
    return lax.dot_general(a, b, (((0,), (0,)), ((), ())), preferred_element_type=F32)


def _sigmoid(z):
    return 1.0 / (1.0 + jnp.exp(-z))


def _rope(t, cs, s1, s2):
    return t * cs + pltpu.roll(t, 120, 1) * s1 + pltpu.roll(t, 8, 1) * s2


def _rope_t(d, cs, s1, s2):
    return d * cs + pltpu.roll(d * s1, 8, 1) + pltpu.roll(d * s2, 120, 1)


def _gather_side(s, w_mkv, w_out, w_ups):
    half = ROT_DIM // 2
    inv_freq = jnp.power(jnp.float32(ROPE_THETA), -jnp.arange(half, dtype=F32) * (2.0 / ROT_DIM))
    freq_row = jnp.tile(jnp.concatenate([inv_freq, inv_freq, jnp.zeros((HEAD_DIM - ROT_DIM,), F32)]), 2)[None, :]

    def fn(in_refs, out_refs):
        f_ref, mkv_ref, out_ref, *up_refs = in_refs
        t_ref, mkv_bf, out_bf, up_bf = out_refs
        mkv_bf[...] = mkv_ref[...].astype(BF16)
        out_bf[...] = out_ref[...].astype(BF16)
        for k, up_ref in enumerate(up_refs):
            up_bf[512 * k:512 * k + 512, :] = up_ref[...].astype(BF16)
        pos = lax.broadcasted_iota(jnp.int32, (s, 128), 0).astype(F32)
        d = lax.broadcasted_iota(jnp.int32, (s, 128), 1) & (HEAD_DIM - 1)
        ang = pos * f_ref[...]
        cos, sin = jnp.cos(ang), jnp.sin(ang)
        lo, hi = d < half, (d >= half) & (d < ROT_DIM)
        t_ref[0] = jnp.where(lo | hi, cos, 1.0)
        t_ref[1] = jnp.where(lo, -sin, 0.0)
        t_ref[2] = jnp.where(hi, sin, 0.0)

    return ([freq_row, w_mkv, w_out, *w_ups],
            [jax.ShapeDtypeStruct((3, s, 128), F32), jax.ShapeDtypeStruct(w_mkv.shape, BF16),
             jax.ShapeDtypeStruct(w_out.shape, BF16), jax.ShapeDtypeStruct((1536, 128), BF16)], fn)


def _load_once(pairs, sems):
    @pl.when(pl.program_id(0) == 0)
    def _():
        cps = [pltpu.make_async_copy(src, dst, sems.at[k]) for k, (src, dst) in enumerate(pairs)]
        for cp in cps:
            cp.start()
        for cp in cps:
            cp.wait()


def _my_place():
    x, y, c = lax.axis_index("x"), lax.axis_index("y"), lax.axis_index("c")
    return x, y, c


def _all_gather(arrs, name, splits=None, side=None):
    n = len(arrs)
    if splits is None:
        splits = [[(0, a.shape[0])] for a in arrs]
    pieces = [(a, r0, rn) for a in range(n) for r0, rn in splits[a]]
    n_p = len(pieces)
    side_in, side_out, side_fn = side if side is not None else ((), (), None)
    m, q = len(side_in), len(side_out)

    def body(*refs):
        ins, outs = refs[:n], refs[n + m:2 * n + m]
        send_sems, recv_sems, local_sems = refs[2 * n + m + q:]
        x, y, c = _my_place()
        me, sibling = (x, y, c), (x, y, 1 - c)

        def route(core):
            first = (jnp.bitwise_xor(x, 1 - core), jnp.bitwise_xor(y, core), core)
            second = (jnp.bitwise_xor(x, core), jnp.bitwise_xor(y, 1 - core), core)
            return first, second, (1 - x, 1 - y, core)

        def idx(px, py, pc):
            return 4 * px + 2 * py + pc

        def copy(p, k, block, to, own=False):
            a, r0, rn = pieces[p]
            dst = outs[a].at[idx(*block), pl.ds(r0, rn)]
            return pltpu.make_async_remote_copy(
                src_ref=ins[a].at[pl.ds(r0, rn)] if own else dst, dst_ref=dst,
                send_sem=send_sems.at[p * 7 + k], recv_sem=recv_sems.at[p * 7 + k],
                device_id=to, device_id_type=MESH)

        nbr1, nbr2, diag = route(c)
        mine = [pltpu.make_async_copy(ins[a], outs[a].at[idx(*me)], local_sems.at[a]) for a in range(n)]
        for cp in mine:
            cp.start()
        sent = []
        for p in range(n_p):
            for k, to in enumerate((sibling, nbr1, nbr2)):
                sent.append(copy(p, k, me, to, own=True))
        for cp in sent:
            cp.start()
        if side_fn is not None:
            side_fn(refs[n:n + m], refs[2 * n + m:2 * n + m + q])
        for k_in, block, onward in ((1, nbr1, ((3, nbr2), (4, sibling))), (2, nbr2, ((5, sibling),)),
                                    (3, diag, ((6, sibling),))):
            for p in range(n_p):
                copy(p, k_in, block, me).wait_recv()
                for k_out, to in onward:
                    cp = copy(p, k_out, block, to)
                    cp.start()
                    sent.append(cp)
        s1, s2, sd = route(1 - c)
        for k_in, block in ((0, sibling), (4, s1), (5, s2), (6, sd)):
            for p in range(n_p):
                copy(p, k_in, block, me).wait_recv()
        for cp in sent:
            cp.wait_send()
        for cp in mine:
            cp.wait()

    return pl.pallas_call(
        body, name=name,
        out_shape=[jax.ShapeDtypeStruct((N_DEV,) + a.shape, a.dtype) for a in arrs] + list(side_out),
        in_specs=[ANY] * n + [pl.BlockSpec(memory_space=pltpu.VMEM)] * m,
        out_specs=[ANY] * n + [pl.BlockSpec(memory_space=pltpu.VMEM)] * q,
        scratch_shapes=[pltpu.SemaphoreType.DMA((7 * n_p,)), pltpu.SemaphoreType.DMA((7 * n_p,)),
                        pltpu.SemaphoreType.DMA((n,))],
        compiler_params=pltpu.CompilerParams(vmem_limit_bytes=32 * 1024 * 1024),
    )(*arrs, *side_in)


N_CHIPS = 4


def _sibling_reduce(arr, name):
    _, r, c = arr.shape

    def body(in_ref, out_ref, land, a_buf, b_buf, o_buf, send_sems, recv_sems, local_sems):
        x, y, core = _my_place()
        cps = [pltpu.make_async_remote_copy(
            src_ref=in_ref.at[2 * j + (1 - core)], dst_ref=land.at[j], send_sem=send_sems.at[j],
            recv_sem=recv_sems.at[j], device_id=(x, y, 1 - core), device_id_type=MESH) for j in range(N_CHIPS)]
        for cp in cps:
            cp.start()
        store = None
        for j in range(N_CHIPS):
            mine = pltpu.make_async_copy(in_ref.at[2 * j + core], a_buf, local_sems.at[0])
            mine.start()
            cps[j].wait_recv()
            theirs = pltpu.make_async_copy(land.at[j], b_buf, local_sems.at[1])
            theirs.start()
            mine.wait()
            theirs.wait()
            if store is not None:
                store.wait()
            o_buf[...] = (a_buf[...].astype(F32) + b_buf[...].astype(F32)).astype(BF16)
            store = pltpu.make_async_copy(o_buf, out_ref.at[j], local_sems.at[2])
            store.start()
        store.wait()
        for cp in cps:
            cp.wait_send()

    return pl.pallas_call(
        body, name=name,
        out_shape=[jax.ShapeDtypeStruct((N_CHIPS, r, c), BF16)] * 2,
        in_specs=[ANY], out_specs=[ANY, ANY],
        scratch_shapes=[pltpu.VMEM((r, c), BF16)] * 3
        + [pltpu.SemaphoreType.DMA((N_CHIPS,)), pltpu.SemaphoreType.DMA((N_CHIPS,)), pltpu.SemaphoreType.DMA((3,))],
        compiler_params=pltpu.CompilerParams(vmem_limit_bytes=32 * 1024 * 1024),
    )(arr)[0]


def _sibling_copies(srcs, lands, send_sems, recv_sems):
    x, y, c = _my_place()
    cps = []
    for j in range(N_CHIPS):
        for a in range(len(srcs)):
            k = a * N_CHIPS + j
            cps.append(pltpu.make_async_remote_copy(
                src_ref=srcs[a].at[2 * j + (1 - c)], dst_ref=lands[a].at[j], send_sem=send_sems[k],
                recv_sem=recv_sems[k], device_id=(x, y, 1 - c), device_id_type=MESH))
    return cps


def _pair_add(mine, recv, core, name):
    n = len(mine)

    def body(c_ref, *refs):
        for a in range(n):
            refs[2 * n + a][...] = (refs[a][...].astype(F32) + refs[n + a][...].astype(F32)).astype(BF16)

    def blk(a):
        return (None,) + a.shape[1:]

    grid_spec = pltpu.PrefetchScalarGridSpec(
        num_scalar_prefetch=1, grid=(N_CHIPS,),
        in_specs=[pl.BlockSpec(blk(a), lambda j, c_ref: (2 * j + c_ref[0], 0, 0)) for a in mine]
        + [pl.BlockSpec(blk(a), lambda j, c_ref: (j, 0, 0)) for a in recv],
        out_specs=[pl.BlockSpec(blk(a), lambda j, c_ref: (j, 0, 0)) for a in recv])
    return pl.pallas_call(
        body, name=name, grid_spec=grid_spec,
        out_shape=[jax.ShapeDtypeStruct(a.shape, BF16) for a in recv],
        compiler_params=_params(),
    )(core, *mine, *recv)


HBM = pl.BlockSpec(memory_space=pltpu.HBM)
SEM = pl.BlockSpec(memory_space=pltpu.SEMAPHORE)
N_PEER_CHIPS = 3
TOKEN = (8, 128)


def _chip_copies(srcs, lands, send_sems, recv_sems):
    x, y, c = _my_place()
    my_chip = 2 * x + y
    peers = [(x, 1 - y), (1 - x, y), (1 - x, 1 - y)]
    cps = []
    for k, (px, py) in enumerate(peers):
        for a in range(len(srcs)):
            j = a * N_PEER_CHIPS + k
            cps.append(pltpu.make_async_remote_copy(
                src_ref=srcs[a].at[2 * px + py], dst_ref=lands[a].at[my_chip],
                send_sem=send_sems[j], recv_sem=recv_sems[j],
                device_id=(px, py, c), device_id_type=MESH))
    return cps


N_PEERS = N_DEV - 1


def _gather_copies(srcs, lands, send_sems, recv_sems):
    x, y, c = _my_place()
    me_idx = 4 * x + 2 * y + c
    flips = [(0, 0, 1), (0, 1, 0), (1, 0, 0), (0, 1, 1), (1, 0, 1), (1, 1, 0), (1, 1, 1)]
    cps = []
    for k, (fx, fy, fc) in enumerate(flips):
        peer = ((1 - x) if fx else x, (1 - y) if fy else y, (1 - c) if fc else c)
        for a in range(len(srcs)):
            j = a * N_PEERS + k
            cps.append(pltpu.make_async_remote_copy(
                src_ref=srcs[a], dst_ref=lands[a].at[me_idx], send_sem=send_sems[j], recv_sem=recv_sems[j],
                device_id=peer, device_id_type=MESH))
    return cps


def _split_start(copies, per_array, arrs, lands, name):
    arrs, lands = list(arrs), list(lands)
    n = len(arrs)
    k = n * per_array

    def body(*refs):
        srcs, land_refs = refs[:n], refs[n:2 * n]
        send_sems, recv_sems = refs[2 * n:2 * n + k], refs[2 * n + k:2 * n + 2 * k]
        token = refs[-1]
        for cp in copies(srcs, land_refs, send_sems, recv_sems):
            cp.start()
        token[...] = jnp.zeros_like(token)

    hbm_arrs = [pltpu.with_memory_space_constraint(a, pltpu.HBM) for a in arrs]
    lands = [pltpu.with_memory_space_constraint(a, pltpu.HBM) for a in lands]
    res = pl.pallas_call(
        body, name=name,
        out_shape=[pltpu.SemaphoreType.DMA(())] * (2 * k) + [pltpu.HBM(a.shape, a.dtype) for a in arrs + lands]
        + [jax.ShapeDtypeStruct(TOKEN, F32)],
        in_specs=[HBM] * (2 * n),
        out_specs=[SEM] * (2 * k) + [HBM] * (2 * n) + [pl.BlockSpec(memory_space=pltpu.VMEM)],
        input_output_aliases={a: 2 * k + a for a in range(2 * n)},
        compiler_params=pltpu.CompilerParams(has_side_effects=pltpu.SideEffectType.DATAFLOW_SIDE_EFFECTING),
    )(*hbm_arrs, *lands)
    return res[:k], res[k:2 * k], res[2 * k:2 * k + n], res[2 * k + n:2 * k + 2 * n], res[-1]


def _split_wait(copies, per_array, send_sems, recv_sems, srcs, lands, after, name):
    n = len(srcs)
    k = n * per_array

    def body(*refs):
        src_refs, land_refs = refs[:n], refs[n:2 * n]
        s_sems, r_sems = refs[2 * n:2 * n + k], refs[2 * n + k:2 * n + 2 * k]
        for cp in copies(src_refs, land_refs, s_sems, r_sems):
            cp.wait_send()
            cp.wait_recv()

    res = pl.pallas_call(
        body, name=name,
        out_shape=[pltpu.HBM(a.shape, a.dtype) for a in list(srcs) + list(lands)],
        in_specs=[HBM] * (2 * n) + [SEM] * (2 * k) + [ANY],
        out_specs=[HBM] * (2 * n),
        input_output_aliases={a: a for a in range(2 * n)},
        compiler_params=pltpu.CompilerParams(has_side_effects=pltpu.SideEffectType.DATAFLOW_SIDE_EFFECTING),
    )(*srcs, *lands, *send_sems, *recv_sems, after)
    return res[:n], res[n:]


def _chip_exchange_start(arrs, name):
    return _split_start(_chip_copies, N_PEER_CHIPS, arrs, [lax.empty(a.shape, a.dtype) for a in arrs], name)


def _chip_exchange_wait(send_sems, recv_sems, srcs, lands, after, name):
    return _split_wait(_chip_copies, N_PEER_CHIPS, send_sems, recv_sems, srcs, lands, after, name)


def _gather_start(arrs, me_idx, name):
    lands = [lax.dynamic_update_slice(lax.empty((N_DEV,) + a.shape, a.dtype), a[None], (me_idx, 0, 0)) for a in arrs]
    return _split_start(_gather_copies, N_PEERS, arrs, lands, name)


def _gather_wait(send_sems, recv_sems, srcs, lands, after, name):
    return _split_wait(_gather_copies, N_PEERS, send_sems, recv_sems, srcs, lands, after, name)[1]


def _proj_fwd(after, x, g_pre, w_int, tabs):
    s = x.shape[0]
    tm = min(512, s)

    def body(after_ref, x_ref, g_ref, t_ref, w_hbm,
             h_ref, pa_ref, pq_ref, pkv_ref, pbz_ref, pmq_ref, pmz_ref, pg_ref, w_vm, sems):
        _load_once([(w_hbm, w_vm)], sems)
        xf = x_ref[...]
        r = lax.rsqrt(jnp.mean(xf * xf, axis=-1, keepdims=True) + EPS)
        h = ((xf * r) * g_ref[...]).astype(BF16)
        h_ref[...] = h
        cs, s1, s2 = t_ref[0], t_ref[1], t_ref[2]

        def mm(seg, c0, width):
            return _dot_nt(h, w_vm[seg[0] + c0:seg[0] + c0 + width, :])

        for c0 in range(0, SEG_A[1], 512):
            pa_ref[:, c0:c0 + 512] = mm(SEG_A, c0, 512).astype(BF16)
        q = mm(SEG_BQ, 0, 512)
        for b in range(4):
            pq_ref[:, 128 * b:128 * b + 128] = _rope(q[:, 128 * b:128 * b + 128], cs, s1, s2).astype(BF16)
        kv = mm(SEG_BKV, 0, 256)
        pkv_ref[:, 0:128] = _rope(kv[:, 0:128], cs, s1, s2).astype(BF16)
        pkv_ref[:, 128:256] = kv[:, 128:256].astype(BF16)
        pbz_ref[...] = mm(SEG_BZ, 0, 512).astype(BF16)
        pmq_ref[...] = mm(SEG_MQ, 0, 512).astype(BF16)
        pmz_ref[...] = mm(SEG_MZ, 0, 512).astype(BF16)
        for c0 in range(0, SEG_G[1], 512):
            pg_ref[:, c0:c0 + 512] = mm(SEG_G, c0, 512).astype(BF16)

    widths = (D_MODEL, 2048, 512, 256, 512, 512, 512, 3072)
    return pl.pallas_call(
        body, name="proj_fwd", grid=(s // tm,),
        out_shape=[jax.ShapeDtypeStruct((s, w), BF16) for w in widths],
        in_specs=[_full(TOKEN), _rows(tm, D_MODEL), _full((1, D_MODEL)),
                  pl.BlockSpec((3, tm, 128), lambda i: (0, i, 0)), ANY],
        out_specs=[_rows(tm, w) for w in widths],
        scratch_shapes=[pltpu.VMEM((IN_WIDTH, D_MODEL), BF16), pltpu.SemaphoreType.DMA((1,))],
        compiler_params=_params(),
    )(after, x, g_pre, tabs, w_int)


def _mem_kv_fwd(mem, g_mem, w_mkv):
    m = mem.shape[0]

    def body(mem_ref, g_ref, w_ref, mn_ref, mkv_ref):
        xf = mem_ref[...]
        r = lax.rsqrt(jnp.mean(xf * xf, axis=-1, keepdims=True) + EPS)
        mn = ((xf * r) * g_ref[...]).astype(BF16)
        mn_ref[...] = mn
        mkv_ref[...] = _dot(mn, w_ref[...]).astype(BF16)

    return pl.pallas_call(
        body, name="mem_kv_fwd", grid=(1,),
        out_shape=[jax.ShapeDtypeStruct((m, D_MODEL), BF16)] * 2,
        in_specs=[_full((m, D_MODEL)), _full((1, D_MODEL)), _full((D_MODEL, D_MODEL))],
        out_specs=[_full((m, D_MODEL))] * 2,
        compiler_params=_params(),
    )(mem, g_mem, w_mkv)


def _halo_specs(s, tm, rows, width):
    nblk = s // rows
    prev = pl.BlockSpec((rows, width), lambda i: (jnp.maximum(i * (tm // rows) - 1, 0), 0))
    nxt = pl.BlockSpec((rows, width), lambda i: (jnp.minimum((i + 1) * (tm // rows), nblk - 1), 0))
    return prev, nxt


def _conv_common(pa, cu_prev, cu_next, w, tm):
    b, c, u, z = (pa[:, 512 * k:512 * k + 512] for k in range(4))
    cu = c * u
    row = lax.broadcasted_iota(jnp.int32, (tm, 512), 0)
    cu_m1 = jnp.where(row == 0, cu_prev, pltpu.roll(cu, 1, 0))
    cu_p1 = jnp.where(row == tm - 1, cu_next, pltpu.roll(cu, tm - 1, 0))
    y = cu_m1 * w[0:1] + cu * w[1:2] + cu_p1 * w[2:3]
    sig = _sigmoid(z)
    return b, c, u, z, cu, cu_m1, cu_p1, y, sig, row


def _conv_fwd(pa, w_conv):
    s = pa.shape[0]
    tm = min(512, s)
    nt = s // tm

    def body(pa_ref, pp_ref, pn_ref, w_ref, ya_ref):
        i = pl.program_id(0)
        prev_row = pp_ref[...].astype(F32)[15:16, :]
        next_row = pn_ref[...].astype(F32)[0:1, :]
        b, _, _, z, _, _, _, y, sig, _ = _conv_common(
            pa_ref[...].astype(F32),
            jnp.where(i == 0, 0.0, prev_row[:, 512:1024] * prev_row[:, 1024:1536]),
            jnp.where(i == nt - 1, 0.0, next_row[:, 512:1024] * next_row[:, 1024:1536]), w_ref[...], tm)
        ya_ref[...] = (b * y * (z * sig)).astype(BF16)

    prev, nxt = _halo_specs(s, tm, 16, 2048)
    return pl.pallas_call(
        body, name="conv_fwd", grid=(nt,),
        out_shape=jax.ShapeDtypeStruct((s, 512), BF16),
        in_specs=[_rows(tm, 2048), prev, nxt, _full((3, 512))],
        out_specs=_rows(tm, 512),
        compiler_params=_params(),
    )(pa, pa, pa, w_conv)


def _heads_to_lanes(a, g, row):
    low = row < HEAD_DIM
    parts = []
    for b in (2 * g, 2 * g + 1):
        t = jnp.transpose(a[:, 128 * b:128 * b + 128])
        swapped = pltpu.roll(t, HEAD_DIM, 0)
        if g == 0:
            parts += [jnp.where(low, t, 0.0), jnp.where(low, swapped, 0.0)]
        else:
            parts += [jnp.where(low, 0.0, swapped), jnp.where(low, 0.0, t)]
    return jnp.concatenate(parts, axis=1)


def _lanes_to_heads(t0, t1, row):
    low = row < HEAD_DIM
    blocks = []
    for b in range(4):
        g = b // 2
        tg = (t0, t1)[g]
        je = 2 * (b - 2 * g)
        even, odd = tg[:, 128 * je:128 * je + 128], tg[:, 128 * je + 128:128 * je + 256]
        if g == 0:
            t = jnp.where(low, even, pltpu.roll(odd, HEAD_DIM, 0))
        else:
            t = jnp.where(low, pltpu.roll(even, HEAD_DIM, 0), odd)
        blocks.append(jnp.transpose(t))
    return jnp.concatenate(blocks, axis=1)


WINDOW_KEYS = 3 * ATTN_BLOCK
STACKED = 4 * ATTN_BLOCK
KEY_CHUNK = 32
MAX_BLOCKS_IN_STEP = 8


def _fill_band_bias(bias, nb):
    assert nb >= 2
    c = lax.broadcasted_iota(jnp.int32, (WINDOW_KEYS, STACKED), 0)
    r = lax.broadcasted_iota(jnp.int32, (WINDOW_KEYS, STACKED), 1) & (ATTN_BLOCK - 1)
    band = (c >= r) & (c <= r + 2 * ATTN_BLOCK)
    for v, ok in enumerate((band, band & (c >= ATTN_BLOCK), band & (c < 2 * ATTN_BLOCK))):
        bias[v] = jnp.where(ok, 0.0, -jnp.inf)


def _bias_variant(n, nb):
    return jnp.where(n == 0, 1, jnp.where(n == nb - 1, 2, 0))


def _sink_row(sink_ref, g):
    return jnp.concatenate([jnp.full((1, ATTN_BLOCK), sink_ref[4 * g + j], F32) for j in range(4)], axis=1)


def _softmax_keys_major(sc, bias, variant, sink, e_scr):
    chunks = [pl.ds(k * KEY_CHUNK, KEY_CHUNK) for k in range(WINDOW_KEYS // KEY_CHUNK)]
    rows = [slice(k * KEY_CHUNK, (k + 1) * KEY_CHUNK) for k in range(WINDOW_KEYS // KEY_CHUNK)]
    m_run = jnp.full((KEY_CHUNK, STACKED), -jnp.inf, F32)
    for ck, rw in zip(chunks, rows):
        m_run = jnp.maximum(m_run, sc[rw] + bias[variant, ck, :])
    m = jnp.maximum(jnp.max(m_run, axis=0, keepdims=True), sink)
    l_run = jnp.zeros((KEY_CHUNK, STACKED), F32)
    for ck, rw in zip(chunks, rows):
        e = jnp.exp(sc[rw] + bias[variant, ck, :] - m)
        l_run += e
        e_scr[rw, :] = e.astype(BF16)
    es = jnp.exp(sink - m)
    inv = 1.0 / (jnp.sum(l_run, axis=0, keepdims=True) + es)
    return inv, es * inv


def _fill_padded(kv_ref, kpad, vpad, s):
    zero = jnp.zeros((ATTN_BLOCK, 128), BF16)
    kpad[0:ATTN_BLOCK, :] = zero
    vpad[0:ATTN_BLOCK, :] = zero
    kpad[ATTN_BLOCK + s:2 * ATTN_BLOCK + s, :] = zero
    vpad[ATTN_BLOCK + s:2 * ATTN_BLOCK + s, :] = zero
    kpad[ATTN_BLOCK:ATTN_BLOCK + s, :] = kv_ref[:, 0:128]
    vpad[ATTN_BLOCK:ATTN_BLOCK + s, :] = kv_ref[:, 128:256]


def _attn_fwd(pq, pkv, pbz, sink):
    s = pq.shape[0]
    nb = s // ATTN_BLOCK
    bps = min(MAX_BLOCKS_IN_STEP, nb)

    def body(sink_ref, q_ref, z_ref, kv_ref, yb_ref, kpad, vpad, bias, e_scr):
        i = pl.program_id(0)

        @pl.when(i == 0)
        def _():
            _fill_padded(kv_ref, kpad, vpad, s)
            _fill_band_bias(bias, nb)

        row = lax.broadcasted_iota(jnp.int32, (ATTN_BLOCK, 128), 0)
        for b in range(bps):
            n = i * bps + b
            rows = slice(b * ATTN_BLOCK, (b + 1) * ATTN_BLOCK)
            start = pl.multiple_of(n * ATTN_BLOCK, ATTN_BLOCK)
            kw, vw = kpad[pl.ds(start, WINDOW_KEYS), :], vpad[pl.ds(start, WINDOW_KEYS), :]
            qf = q_ref[rows, :].astype(F32)
            variant = _bias_variant(n, nb)
            outs = []
            for g in range(2):
                e_bg = e_scr.at[2 * b + g]
                qt = (_heads_to_lanes(qf, g, row) * ATTN_SCALE).astype(BF16)
                inv, _ = _softmax_keys_major(_dot(kw, qt), bias, variant, _sink_row(sink_ref, g), e_bg)
                outs.append(_dot_tn(vw, e_bg[...]) * inv)
            attn = _lanes_to_heads(outs[0], outs[1], row)
            z = z_ref[rows, :].astype(F32)
            yb_ref[rows, :] = (attn * (z * _sigmoid(z))).astype(BF16)

    tq = bps * ATTN_BLOCK
    return pl.pallas_call(
        body, name="attn_fwd", grid=(s // tq,),
        out_shape=jax.ShapeDtypeStruct((s, 512), BF16),
        in_specs=[pl.BlockSpec(memory_space=pltpu.SMEM), _rows(tq, 512), _rows(tq, 512), _full((s, 256))],
        out_specs=_rows(tq, 512),
        scratch_shapes=[pltpu.VMEM((s + 2 * ATTN_BLOCK, 128), BF16)] * 2
        + [pltpu.VMEM((3, WINDOW_KEYS, STACKED), F32),
           pltpu.VMEM((2 * bps, WINDOW_KEYS, STACKED), BF16)],
        compiler_params=_params(),
    )(sink, pq, pbz, pkv)


def _mem_softmax_t(q, mk):
    sc = _dot_nt(mk, q) * MEM_SCALE
    e = jnp.exp(sc - jnp.max(sc, axis=0, keepdims=True))
    return e * (1.0 / jnp.sum(e, axis=0, keepdims=True))


def _mem_attn_fwd(pmq, pmz, mkv):
    s = pmq.shape[0]
    m = mkv.shape[0]
    tm = min(512, s)

    def body(q_ref, z_ref, mk_ref, mv_ref, ym_ref):
        z = z_ref[...].astype(F32)
        sz = z * _sigmoid(z)
        for h in range(MEM_HEADS):
            cols = slice(128 * h, 128 * h + 128)
            pt = _mem_softmax_t(q_ref[:, cols], mk_ref[:, cols])
            o = _dot_tn(pt.astype(BF16), mv_ref[:, cols])
            ym_ref[:, cols] = (o * sz[:, cols]).astype(BF16)

    return pl.pallas_call(
        body, name="mem_attn_fwd", grid=(s // tm,),
        out_shape=jax.ShapeDtypeStruct((s, 512), BF16),
        in_specs=[_rows(tm, 512), _rows(tm, 512), pl.BlockSpec((m, 512), lambda i: (0, 0)),
                  pl.BlockSpec((m, 512), lambda i: (0, 1))],
        out_specs=_rows(tm, 512),
        compiler_params=_params(),
    )(pmq, pmz, mkv, mkv)


def _mid(ya, yb, ym, pg, x, target, g_post, w_up, w_out):
    s = x.shape[0]
    tm = min(256, s)
    nt = s // tm

    def body(ya_ref, yb_ref, ym_ref, pg_ref, x_ref, t_ref, gp_ref, wup_hbm, wout_hbm,
             dg_ref, dya_ref, dyb_ref, dym_ref, dy_ref, loss_ref, ggp_ref, mb_ref, dob_ref, du_ref,
             wup_vm, wout_vm, sems):
        i = pl.program_id(0)
        _load_once([(wup_hbm.at[d], wup_vm.at[:, pl.ds(128 * d, 128)]) for d in range(N_DEV)]
                   + [(wout_hbm, wout_vm)], sems)

        @pl.when(i == 0)
        def _():
            loss_ref[...] = jnp.zeros_like(loss_ref)
            ggp_ref[...] = jnp.zeros_like(ggp_ref)

        ys = (ya_ref[...], yb_ref[...], ym_ref[...])
        us = [_dot(ys[k], wup_vm[512 * k:512 * k + 512, :]) for k in range(3)]
        gates = [_sigmoid(pg_ref[:, 1024 * k:1024 * k + 1024].astype(F32)) for k in range(3)]
        merged = gates[0] * us[0] + gates[1] * us[1] + gates[2] * us[2]
        mb = merged.astype(BF16)
        mb_ref[...] = mb
        out = _dot(mb, wout_vm[...])
        r = lax.rsqrt(jnp.mean(out * out, axis=-1, keepdims=True) + EPS)
        on = out * r
        gp = gp_ref[...]
        err = (x_ref[...] + on * gp) - t_ref[...]
        loss_ref[...] += 0.5 * jnp.sum(err * err) * (1.0 / D_MODEL)
        dy = err * (1.0 / D_MODEL)
        dy_ref[...] = dy
        ggp_ref[...] += jnp.sum(dy * on, axis=0, keepdims=True)
        a = dy * gp
        d_out = r * (a - on * jnp.mean(a * on, axis=-1, keepdims=True))
        dob = d_out.astype(BF16)
        dob_ref[...] = dob
        d_merged = _dot_nt(dob, wout_vm[...])
        d_refs = (dya_ref, dyb_ref, dym_ref)
        for k in range(3):
            g = gates[k]
            du_f = d_merged * g
            dg_ref[:, 1024 * k:1024 * k + 1024] = (du_f * us[k] * (1.0 - g)).astype(BF16)
            du = du_f.astype(BF16)
            du_ref[k] = du
            d_refs[k][...] = _dot_nt(du, wup_vm[512 * k:512 * k + 512, :]).astype(BF16)

    return pl.pallas_call(
        body, name="mid", grid=(nt,),
        out_shape=[jax.ShapeDtypeStruct((s, 3072), BF16)] + [jax.ShapeDtypeStruct((s, 512), BF16)] * 3
        + [jax.ShapeDtypeStruct((s, D_MODEL), F32), jax.ShapeDtypeStruct((8, 128), F32),
           jax.ShapeDtypeStruct((1, D_MODEL), F32), jax.ShapeDtypeStruct((s, D_MODEL), BF16),
           jax.ShapeDtypeStruct((s, D_MODEL), BF16), jax.ShapeDtypeStruct((3, s, D_MODEL), BF16)],
        in_specs=[_rows(tm, 512)] * 3 + [_rows(tm, 3072), _rows(tm, D_MODEL), _rows(tm, D_MODEL),
                                         _full((1, D_MODEL)), ANY, ANY],
        out_specs=[_rows(tm, 3072)] + [_rows(tm, 512)] * 3
        + [_rows(tm, D_MODEL), _full((8, 128)), _full((1, D_MODEL)), _rows(tm, D_MODEL), _rows(tm, D_MODEL),
           pl.BlockSpec((3, tm, D_MODEL), lambda i: (0, i, 0))],
        scratch_shapes=[pltpu.VMEM((1536, D_MODEL), BF16), pltpu.VMEM((D_MODEL, D_MODEL), BF16),
                        pltpu.SemaphoreType.DMA((N_DEV + 1,))],
        compiler_params=_params(),
    )(ya, yb, ym, pg, x, target, g_post, w_up, w_out)


def _gw_mid(mb, dob, ys, du):
    s = mb.shape[0]
    tn = 256
    n_out = D_MODEL // tn
    tiles = [(0, c0, 0) for c0 in range(0, D_MODEL, tn)]
    tiles += [(1 + k, c0, 1 + k) for k in range(3) for c0 in range(0, 512, tn)]
    n_t = len(tiles)

    def body(mb_hbm, ya_hbm, yb_hbm, ym_hbm, dob_hbm, du_hbm, out_hbm, up_hbm,
             lhs, rhs, res_out, res_up, in_sems, rhs_sems, out_sems):
        lhs_hbm = (mb_hbm, ya_hbm, yb_hbm, ym_hbm)

        def load(t):
            a, c0, _ = tiles[t]
            return pltpu.make_async_copy(lhs_hbm[a].at[:, pl.ds(c0, tn)], lhs.at[t & 1], in_sems.at[t & 1])

        def load_rhs(g):
            src = dob_hbm if g == 0 else du_hbm.at[g - 1]
            return pltpu.make_async_copy(src, rhs.at[g & 1], rhs_sems.at[g & 1])

        def store(t):
            if t < n_out:
                return pltpu.make_async_copy(res_out.at[t & 1], out_hbm.at[pl.ds(t * tn, tn)], out_sems.at[t & 1])
            rows = pl.ds((t - n_out) * tn, tn)
            return pltpu.make_async_copy(res_up.at[t & 1], up_hbm.at[:, rows, :], out_sems.at[t & 1])

        load_rhs(0).start()
        load(0).start()
        for t, (_, _, g) in enumerate(tiles):
            new_rhs = t == 0 or tiles[t - 1][2] != g
            if t + 1 < n_t:
                load(t + 1).start()
            if new_rhs and g < 3:
                load_rhs(g + 1).start()
            load(t).wait()
            if new_rhs:
                load_rhs(g).wait()
            if t >= 2:
                store(t - 2).wait()
            r = _dot_tn(lhs[t & 1], rhs[g & 1])
            if t < n_out:
                res_out[t & 1] = r.astype(BF16)
            else:
                for d in range(N_DEV):
                    res_up[t & 1, d] = r[:, 128 * d:128 * d + 128].astype(BF16)
            store(t).start()
        store(n_t - 2).wait()
        store(n_t - 1).wait()

    return pl.pallas_call(
        body, name="gw_mid",
        out_shape=[jax.ShapeDtypeStruct((D_MODEL, D_MODEL), BF16), jax.ShapeDtypeStruct((N_DEV, 1536, 128), BF16)],
        in_specs=[ANY] * 6, out_specs=[ANY, ANY],
        scratch_shapes=[pltpu.VMEM((2, s, tn), BF16), pltpu.VMEM((2, s, D_MODEL), BF16),
                        pltpu.VMEM((2, tn, D_MODEL), BF16), pltpu.VMEM((2, N_DEV, tn, 128), BF16),
                        pltpu.SemaphoreType.DMA((2,)), pltpu.SemaphoreType.DMA((2,)),
                        pltpu.SemaphoreType.DMA((2,))],
        compiler_params=pltpu.CompilerParams(vmem_limit_bytes=CALL_VMEM_MB * 1024 * 1024),
    )(mb, *ys, dob, du)


def _conv_bwd(after, pa, dya, w_conv):
    s = pa.shape[0]
    tm = min(512, s)
    nt = s // tm

    def body(after_ref, pa_ref, pp_ref, pn_ref, d_ref, dp_ref, dn_ref, w_ref, da_ref, gw_ref):
        i = pl.program_id(0)
        first, last = i == 0, i == nt - 1

        @pl.when(first)
        def _():
            gw_ref[...] = jnp.zeros_like(gw_ref)

        w = w_ref[...]
        prev_row = pp_ref[...].astype(F32)[15:16, :]
        next_row = pn_ref[...].astype(F32)[0:1, :]
        b, c, u, z, cu, cu_m1, cu_p1, y, sig, row = _conv_common(
            pa_ref[...].astype(F32),
            jnp.where(first, 0.0, prev_row[:, 512:1024] * prev_row[:, 1024:1536]),
            jnp.where(last, 0.0, next_row[:, 512:1024] * next_row[:, 1024:1536]), w, tm)
        sz = z * sig
        dya_t = d_ref[...].astype(F32)
        d_y = dya_t * b * sz

        def halo_dy(p_row, d_row):
            zz = p_row[:, 1536:2048]
            return d_row * p_row[:, 0:512] * (zz * _sigmoid(zz))

        dy_prev = jnp.where(first, 0.0, halo_dy(prev_row, dp_ref[...].astype(F32)[15:16, :]))
        dy_next = jnp.where(last, 0.0, halo_dy(next_row, dn_ref[...].astype(F32)[0:1, :]))
        dy_m1 = jnp.where(row == 0, dy_prev, pltpu.roll(d_y, 1, 0))
        dy_p1 = jnp.where(row == tm - 1, dy_next, pltpu.roll(d_y, tm - 1, 0))
        d_cu = dy_p1 * w[0:1] + d_y * w[1:2] + dy_m1 * w[2:3]
        da_ref[:, 0:512] = (dya_t * y * sz).astype(BF16)
        da_ref[:, 512:1024] = (d_cu * u).astype(BF16)
        da_ref[:, 1024:1536] = (d_cu * c).astype(BF16)
        da_ref[:, 1536:2048] = (dya_t * b * y * (sig + sz * (1.0 - sig))).astype(BF16)
        gw_ref[0:1, :] += jnp.sum(d_y * cu_m1, axis=0, keepdims=True)
        gw_ref[1:2, :] += jnp.sum(d_y * cu, axis=0, keepdims=True)
        gw_ref[2:3, :] += jnp.sum(d_y * cu_p1, axis=0, keepdims=True)

    prev, nxt = _halo_specs(s, tm, 16, 2048)
    dprev, dnxt = _halo_specs(s, tm, 16, 512)
    return pl.pallas_call(
        body, name="conv_bwd", grid=(nt,),
        out_shape=[jax.ShapeDtypeStruct((s, 2048), BF16), jax.ShapeDtypeStruct((8, 512), F32)],
        in_specs=[_full(TOKEN), _rows(tm, 2048), prev, nxt, _rows(tm, 512), dprev, dnxt, _full((3, 512))],
        out_specs=[_rows(tm, 2048), _full((8, 512))],
        compiler_params=_params(),
    )(after, pa, pa, pa, dya, dya, dya, w_conv)


def _attn_bwd(after, pq, pkv, pbz, dyb, sink, tabs):
    s = pq.shape[0]
    nb = s // ATTN_BLOCK
    bps = min(MAX_BLOCKS_IN_STEP, nb)

    def body(sink_ref, after_ref, q_ref, z_ref, d_ref, kv_ref, t_ref,
             dq_ref, dz_ref, dkv_ref, gs_ref, kpad, vpad, dk_acc, dv_acc, bias, e_scr, ds_scr):
        i = pl.program_id(0)

        @pl.when(i == 0)
        def _():
            _fill_padded(kv_ref, kpad, vpad, s)
            _fill_band_bias(bias, nb)
            dk_acc[...] = jnp.zeros_like(dk_acc)
            dv_acc[...] = jnp.zeros_like(dv_acc)
            gs_ref[...] = jnp.zeros_like(gs_ref)

        row = lax.broadcasted_iota(jnp.int32, (ATTN_BLOCK, 128), 0)
        for b in range(bps):
            n = i * bps + b
            rows = slice(b * ATTN_BLOCK, (b + 1) * ATTN_BLOCK)
            start = pl.multiple_of(n * ATTN_BLOCK, ATTN_BLOCK)
            kw, vw = kpad[pl.ds(start, WINDOW_KEYS), :], vpad[pl.ds(start, WINDOW_KEYS), :]
            qf = q_ref[rows, :].astype(F32)
            variant = _bias_variant(n, nb)
            z = z_ref[rows, :].astype(F32)
            sig = _sigmoid(z)
            dyb_t = d_ref[rows, :].astype(F32)
            d_attn = dyb_t * (z * sig)
            outs, dqs = [], []
            dk_w = jnp.zeros((WINDOW_KEYS, 128), F32)
            dv_w = jnp.zeros((WINDOW_KEYS, 128), F32)
            for g in range(2):
                e_bg, ds_bg = e_scr.at[2 * b + g], ds_scr.at[2 * b + g]
                qt = _heads_to_lanes(qf, g, row)
                inv, p_sink = _softmax_keys_major(
                    _dot(kw, (qt * ATTN_SCALE).astype(BF16)), bias, variant, _sink_row(sink_ref, g), e_bg)
                ot = _dot_tn(vw, e_bg[...]) * inv
                outs.append(ot)
                dot_ = _heads_to_lanes(d_attn, g, row)
                delta = jnp.sum(dot_ * ot, axis=0, keepdims=True)
                dpt = _dot(vw, dot_.astype(BF16))
                for k in range(WINDOW_KEYS // KEY_CHUNK):
                    rw = slice(k * KEY_CHUNK, (k + 1) * KEY_CHUNK)
                    ds_bg[rw, :] = (e_bg[rw, :].astype(F32) * (dpt[rw] - delta)).astype(BF16)
                sink_part = p_sink * delta
                for j in range(4):
                    h = 4 * g + j
                    gs_ref[h:h + 1, :] -= jnp.sum(sink_part[:, 128 * j:128 * j + 128])
                dqs.append(_dot_tn(kw, ds_bg[...]) * (inv * ATTN_SCALE))
                dk_w += _dot_nt(ds_bg[...], (qt * inv).astype(BF16)) * ATTN_SCALE
                dv_w += _dot_nt(e_bg[...], (dot_ * inv).astype(BF16))
            dk_acc[pl.ds(start, WINDOW_KEYS), :] += dk_w
            dv_acc[pl.ds(start, WINDOW_KEYS), :] += dv_w
            attn = _lanes_to_heads(outs[0], outs[1], row)
            dz_ref[rows, :] = (dyb_t * attn * (sig * (1.0 + z * (1.0 - sig)))).astype(BF16)
            dq = _lanes_to_heads(dqs[0], dqs[1], row)
            trows = pl.ds(start, ATTN_BLOCK)
            cs, s1, s2 = t_ref[0, trows, :], t_ref[1, trows, :], t_ref[2, trows, :]
            for blk in range(4):
                cols = slice(128 * blk, 128 * blk + 128)
                dq_ref[rows, cols] = _rope_t(dq[:, cols], cs, s1, s2).astype(BF16)

        @pl.when(i == nb // bps - 1)
        def _():
            dk = dk_acc[ATTN_BLOCK:ATTN_BLOCK + s, :]
            dkv_ref[:, 0:128] = _rope_t(dk, t_ref[0], t_ref[1], t_ref[2]).astype(BF16)
            dkv_ref[:, 128:256] = dv_acc[ATTN_BLOCK:ATTN_BLOCK + s, :].astype(BF16)

    tq = bps * ATTN_BLOCK
    tile = _rows(tq, 512)
    return pl.pallas_call(
        body, name="attn_bwd", grid=(s // tq,),
        out_shape=[jax.ShapeDtypeStruct((s, 512), BF16), jax.ShapeDtypeStruct((s, 512), BF16),
                   jax.ShapeDtypeStruct((s, 256), BF16), jax.ShapeDtypeStruct((8, 128), F32)],
        in_specs=[pl.BlockSpec(memory_space=pltpu.SMEM), _full(TOKEN), tile, tile, tile, _full((s, 256)),
                  _full((3, s, 128))],
        out_specs=[tile, tile, _full((s, 256)), _full((8, 128))],
        scratch_shapes=[pltpu.VMEM((s + 2 * ATTN_BLOCK, 128), BF16)] * 2
        + [pltpu.VMEM((s + 2 * ATTN_BLOCK, 128), F32)] * 2
        + [pltpu.VMEM((3, WINDOW_KEYS, STACKED), F32)]
        + [pltpu.VMEM((2 * bps, WINDOW_KEYS, STACKED), BF16)] * 2,
        compiler_params=_params(),
    )(sink, after, pq, pbz, dyb, pkv, tabs)


def _mem_attn_bwd(pmq, pmz, mkv, dym):
    s = pmq.shape[0]
    m = mkv.shape[0]
    tm = min(512, s)

    def body(q_ref, z_ref, d_ref, mk_ref, mv_ref, dq_ref, dz_ref, dmkv_ref):
        @pl.when(pl.program_id(0) == 0)
        def _():
            dmkv_ref[...] = jnp.zeros_like(dmkv_ref)

        z = z_ref[...].astype(F32)
        sig = _sigmoid(z)
        dym_t = d_ref[...].astype(F32)
        d_attn = dym_t * (z * sig)
        dsilu = sig * (1.0 + z * (1.0 - sig))
        for h in range(MEM_HEADS):
            cols = slice(128 * h, 128 * h + 128)
            q, mk, mv = q_ref[:, cols], mk_ref[:, cols], mv_ref[:, cols]
            pt = _mem_softmax_t(q, mk)
            pb = pt.astype(BF16)
            o = _dot_tn(pb, mv)
            dob = d_attn[:, cols].astype(BF16)
            dpt = _dot_nt(mv, dob)
            dst = (pt * (dpt - jnp.sum(pt * dpt, axis=0, keepdims=True))).astype(BF16)
            dq_ref[:, cols] = (_dot_tn(dst, mk) * MEM_SCALE).astype(BF16)
            dz_ref[:, cols] = (dym_t[:, cols] * o * dsilu[:, cols]).astype(BF16)
            dmkv_ref[:, cols] += _dot(dst, q) * MEM_SCALE
            dmkv_ref[:, 512 + 128 * h:512 + 128 * h + 128] += _dot(pb, dob)

    return pl.pallas_call(
        body, name="mem_attn_bwd", grid=(s // tm,),
        out_shape=[jax.ShapeDtypeStruct((s, 512), BF16), jax.ShapeDtypeStruct((s, 512), BF16),
                   jax.ShapeDtypeStruct((m, D_MODEL), F32)],
        in_specs=[_rows(tm, 512), _rows(tm, 512), _rows(tm, 512), pl.BlockSpec((m, 512), lambda i: (0, 0)),
                  pl.BlockSpec((m, 512), lambda i: (0, 1))],
        out_specs=[_rows(tm, 512), _rows(tm, 512), _full((m, D_MODEL))],
        compiler_params=_params(),
    )(pmq, pmz, dym, mkv, mkv)


def _mem_kv_bwd(mem, g_mem, mn, dmkv, w_mkv):
    m = mem.shape[0]

    def body(mem_ref, g_ref, mn_ref, d_ref, w_ref, gw_ref, gg_ref):
        db = d_ref[...].astype(BF16)
        gw_ref[...] = _dot_tn(mn_ref[...], db).astype(BF16)
        d_mn = _dot_nt(db, w_ref[...])
        xf = mem_ref[...]
        r = lax.rsqrt(jnp.mean(xf * xf, axis=-1, keepdims=True) + EPS)
        gg_ref[...] = jnp.sum(d_mn * (xf * r), axis=0, keepdims=True)

    return pl.pallas_call(
        body, name="mem_kv_bwd", grid=(1,),
        out_shape=[jax.ShapeDtypeStruct((D_MODEL, D_MODEL), BF16), jax.ShapeDtypeStruct((1, D_MODEL), F32)],
        in_specs=[_full((m, D_MODEL)), _full((1, D_MODEL)), _full((m, D_MODEL)), _full((m, D_MODEL)),
                  _full((D_MODEL, D_MODEL))],
        out_specs=[_full((D_MODEL, D_MODEL)), _full((1, D_MODEL))],
        compiler_params=_params(),
    )(mem, g_mem, mn, dmkv, w_mkv)


def _dh_bwd(after, dparts, x, dy, g_pre, w_int):
    s = x.shape[0]
    tm = min(256, s)
    n_t = s // tm
    tiled = list(dparts) + [x, dy]
    n_in = len(tiled)

    def body(after_ref, *refs):
        in_hbm = refs[:n_in]
        g_ref, w_hbm, gx_hbm, gg_ref = refs[n_in:n_in + 4]
        bufs = refs[n_in + 4:2 * n_in + 4]
        w_vm, gx_buf, in_sems, out_sems, w_sem = refs[2 * n_in + 4:]

        def loads(t, slot):
            rows = pl.ds(pl.multiple_of(t * tm, tm), tm)
            return [pltpu.make_async_copy(in_hbm[k].at[rows], bufs[k].at[slot], in_sems.at[slot, k])
                    for k in range(n_in)]

        def store(t, slot):
            rows = pl.ds(pl.multiple_of(t * tm, tm), tm)
            return pltpu.make_async_copy(gx_buf.at[slot], gx_hbm.at[rows], out_sems.at[slot])

        w_copy = pltpu.make_async_copy(w_hbm, w_vm, w_sem)
        w_copy.start()
        for cp in loads(0, 0):
            cp.start()
        w_copy.wait()

        def step(t, gg):
            slot = t & 1

            @pl.when(t + 1 < n_t)
            def _():
                for cp in loads(t + 1, 1 - slot):
                    cp.start()

            for cp in loads(t, slot):
                cp.wait()

            @pl.when(t >= 2)
            def _():
                store(t - 2, slot).wait()

            d_h = jnp.zeros((tm, D_MODEL), F32)
            for k, (r0, width) in enumerate(SEGS):
                for c0 in range(0, width, 512):
                    cw = min(512, width - c0)
                    d_h += _dot(bufs[k].at[slot][:, c0:c0 + cw], w_vm[r0 + c0:r0 + c0 + cw, :])
            xf = bufs[n_in - 2][slot]
            r = lax.rsqrt(jnp.mean(xf * xf, axis=-1, keepdims=True) + EPS)
            xn = xf * r
            a = d_h * g_ref[...]
            gx_buf[slot] = r * (a - xn * jnp.mean(a * xn, axis=-1, keepdims=True)) + bufs[n_in - 1][slot]
            store(t, slot).start()
            return gg + jnp.sum(d_h * xn, axis=0, keepdims=True)

        gg_ref[...] = lax.fori_loop(0, n_t, step, jnp.zeros((1, D_MODEL), F32))
        store(n_t - 2, (n_t - 2) & 1).wait()
        store(n_t - 1, (n_t - 1) & 1).wait()

    vmem = pl.BlockSpec(memory_space=pltpu.VMEM)
    return pl.pallas_call(
        body, name="dh_bwd",
        out_shape=[jax.ShapeDtypeStruct((s, D_MODEL), F32), jax.ShapeDtypeStruct((1, D_MODEL), F32)],
        in_specs=[ANY] * (1 + n_in) + [vmem, ANY],
        out_specs=[ANY, vmem],
        scratch_shapes=[pltpu.VMEM((2, tm) + a.shape[1:], a.dtype) for a in tiled]
        + [pltpu.VMEM((IN_WIDTH, D_MODEL), BF16), pltpu.VMEM((2, tm, D_MODEL), F32),
           pltpu.SemaphoreType.DMA((2, n_in)), pltpu.SemaphoreType.DMA((2,)), pltpu.SemaphoreType.DMA(())],
        compiler_params=pltpu.CompilerParams(vmem_limit_bytes=CALL_VMEM_MB * 1024 * 1024),
    )(after, *dparts, x, dy, g_pre, w_int)


def _gw_in(dparts, h):
    s = h.shape[0]
    tn = 256
    tiles = [(a, c0) for a, (_, width) in enumerate(SEGS) for c0 in range(0, width, tn)]
    n_t = len(tiles)
    assert n_t * tn == IN_WIDTH

    def body(*refs):
        d_hbm = refs[:7]
        h_hbm, out_hbm, lhs, h_vm, res, in_sems, out_sems, h_sem = refs[7:]

        def load(a, c0, slot):
            return pltpu.make_async_copy(d_hbm[a].at[:, pl.ds(c0, tn)], lhs.at[slot], in_sems.at[slot])

        def store(t, slot):
            rows = pl.ds(pl.multiple_of(t * tn, tn), tn)
            return pltpu.make_async_copy(res.at[slot], out_hbm.at[rows], out_sems.at[slot])

        def start_load(t, slot):
            for k, (a, c0) in enumerate(tiles):
                @pl.when(t == k)
                def _(a=a, c0=c0):
                    load(a, c0, slot).start()

        h_copy = pltpu.make_async_copy(h_hbm, h_vm, h_sem)
        h_copy.start()
        load(*tiles[0], 0).start()
        load(*tiles[1], 1).start()
        h_copy.wait()

        def step(t, carry):
            slot = t & 1
            lslot = lax.rem(t, 3)

            @pl.when(t + 2 < n_t)
            def _():
                start_load(t + 2, lax.rem(t + 2, 3))

            load(*tiles[0], lslot).wait()

            @pl.when(t >= 2)
            def _():
                store(t - 2, slot).wait()

            res[slot] = _dot_tn(lhs[lslot], h_vm[...]).astype(BF16)
            store(t, slot).start()
            return carry

        lax.fori_loop(0, n_t, step, 0)
        store(n_t - 2, (n_t - 2) & 1).wait()
        store(n_t - 1, (n_t - 1) & 1).wait()

    return pl.pallas_call(
        body, name="gw_in",
        out_shape=jax.ShapeDtypeStruct((IN_WIDTH, D_MODEL), BF16),
        in_specs=[ANY] * 8, out_specs=ANY,
        scratch_shapes=[pltpu.VMEM((3, s, tn), BF16), pltpu.VMEM((s, D_MODEL), BF16),
                        pltpu.VMEM((2, tn, D_MODEL), BF16), pltpu.SemaphoreType.DMA((3,)),
                        pltpu.SemaphoreType.DMA((2,)), pltpu.SemaphoreType.DMA(())],
        compiler_params=pltpu.CompilerParams(vmem_limit_bytes=CALL_VMEM_MB * 1024 * 1024),
    )(*dparts, h)


def _adamw_math(w, g, m, v):
    m2 = ADAM_B1 * m + (1.0 - ADAM_B1) * g
    v2 = ADAM_B2 * v + (1.0 - ADAM_B2) * (g * g)
    m_hat = m2 / (1.0 - ADAM_B1 ** ADAM_STEP)
    v_hat = v2 / (1.0 - ADAM_B2 ** ADAM_STEP)
    delta = -ADAM_LR * (m_hat / (jnp.sqrt(v_hat) + ADAM_EPS) + ADAM_WD * w)
    return delta, m2, v2


def _sum_adamw(after, own, land, chip, block, w, m, v, name, tiles=1):
    r, c = w.shape
    rt = r // tiles

    def body(c_ref, after_ref, own_ref, l1_ref, l2_ref, l3_ref, w_ref, m_ref, v_ref, g_ref, d_ref, m2_ref, v2_ref):
        g = own_ref[...].astype(F32)
        for l_ref in (l1_ref, l2_ref, l3_ref):
            g += l_ref[...].astype(F32)
        g_ref[...] = g
        d_ref[...], m2_ref[...], v2_ref[...] = _adamw_math(w_ref[...], g, m_ref[...], v_ref[...])

    def share(k):
        return pl.BlockSpec((None, rt, c), lambda i, c_ref: (jnp.bitwise_xor(c_ref[0], k), block * tiles + i, 0))

    spec = pl.BlockSpec((rt, c), lambda i, c_ref: (i, 0))
    grid_spec = pltpu.PrefetchScalarGridSpec(
        num_scalar_prefetch=1, grid=(tiles,),
        in_specs=[ANY, share(0), share(1), share(2), share(3)] + [spec] * 3, out_specs=[spec] * 4)
    return pl.pallas_call(
        body, name=name, grid_spec=grid_spec,
        out_shape=[jax.ShapeDtypeStruct((r, c), F32)] * 4,
        compiler_params=_params(),
    )(chip, after, own, land, land, land, w, m, v)


def _sum_adamw_group(after, items, chip, name):
    k = len(items)

    def body(c_ref, after_ref, *refs):
        shares, wmv, outs = refs[:4 * k], refs[4 * k:7 * k], refs[7 * k:]
        for j in range(k):
            g = shares[4 * j][...].astype(F32)
            for l_ref in shares[4 * j + 1:4 * j + 4]:
                g += l_ref[...].astype(F32)
            outs[4 * j][...] = g
            outs[4 * j + 1][...], outs[4 * j + 2][...], outs[4 * j + 3][...] = _adamw_math(
                wmv[3 * j][...], g, wmv[3 * j + 1][...], wmv[3 * j + 2][...])

    def share(shape, block, q):
        return pl.BlockSpec((None,) + shape, lambda i, c_ref: (jnp.bitwise_xor(c_ref[0], q), block, 0))

    in_specs, args = [ANY], [after]
    for own, land, block, w, m, v in items:
        in_specs += [share(w.shape, block, q) for q in range(4)]
        args += [own, land, land, land]
    for own, land, block, w, m, v in items:
        in_specs += [pl.BlockSpec(w.shape, lambda i, c_ref: (0, 0))] * 3
        args += [w, m, v]
    out_specs = [pl.BlockSpec(w.shape, lambda i, c_ref: (0, 0)) for _, _, _, w, _, _ in items for _ in range(4)]
    res = pl.pallas_call(
        body, name=name,
        grid_spec=pltpu.PrefetchScalarGridSpec(num_scalar_prefetch=1, grid=(1,), in_specs=in_specs,
                                               out_specs=out_specs),
        out_shape=[jax.ShapeDtypeStruct(w.shape, F32) for _, _, _, w, _, _ in items for _ in range(4)],
        compiler_params=_params(),
    )(chip, *args)
    return [res[4 * j:4 * j + 4] for j in range(k)]


def _small_pack(parts):
    def pack_body(gpre_ref, gconv_ref, gsink_ref, gmem_ref, gpost_ref, loss_ref, pack):
        lane = lax.broadcasted_iota(jnp.int32, (1, 128), 1)
        sink_row = jnp.zeros((1, 128), F32)
        for h in range(8):
            sink_row = jnp.where(lane == h, gsink_ref[h:h + 1, :], sink_row)
        pack[...] = jnp.zeros_like(pack)
        pack[0:1, :] = gpre_ref[...]
        pack[1:2, :] = gmem_ref[...]
        pack[2:3, :] = gpost_ref[...]
        pack[3:6, 0:512] = gconv_ref[0:3, :]
        pack[6:7, 0:128] = sink_row
        pack[7:8, 0:128] = loss_ref[0:1, :]

    return pl.pallas_call(
        pack_body, name="small_pack", grid=(1,),
        out_shape=jax.ShapeDtypeStruct((8, D_MODEL), F32),
        in_specs=[_full(p.shape) for p in parts], out_specs=_full((8, D_MODEL)),
    )(*parts)


def _small_apply(packs, ws, ms, vs):
    def apply(p_ref, *refs):
        w_refs, m_refs, v_refs = refs[0:5], refs[5:10], refs[10:15]
        loss_out = refs[15]
        g_outs, d_outs, m_outs, v_outs = refs[16:21], refs[21:26], refs[26:31], refs[31:36]
        x, y, c = _my_place()
        tot = p_ref[0]
        for d in range(1, N_DEV):
            tot = tot + p_ref[d]
        conv = pltpu.roll(tot[:, 0:512], (512 - 64 * (4 * x + 2 * y + c)) % 512, 1)[3:6, 0:64]
        grads = (tot[0:1, :], conv, tot[6:7, 0:8], tot[1:2, :], tot[2:3, :])
        loss_out[...] = tot[7:8, 0:128]
        for j in range(5):
            if len(w_refs[j].shape) == 3:
                for k in range(w_refs[j].shape[0]):
                    g_outs[j][k] = grads[j][k:k + 1]
                    d_outs[j][k], m_outs[j][k], v_outs[j][k] = _adamw_math(
                        w_refs[j][k], grads[j][k:k + 1], m_refs[j][k], v_refs[j][k])
                continue
            g_outs[j][...] = grads[j]
            d_outs[j][...], m_outs[j][...], v_outs[j][...] = _adamw_math(
                w_refs[j][...], grads[j], m_refs[j][...], v_refs[j][...])

    specs = [_full(w.shape) for w in ws]
    res = pl.pallas_call(
        apply, name="small_apply", grid=(1,),
        out_shape=[jax.ShapeDtypeStruct((1, 128), F32)] + [jax.ShapeDtypeStruct(w.shape, F32) for w in ws] * 4,
        in_specs=[_full((N_DEV, 8, D_MODEL))] + specs * 3,
        out_specs=[_full((1, 128))] + specs * 4,
    )(packs, *ws, *ms, *vs)
    return res[0], res[1:6], res[6:11], res[11:16], res[16:21]


def kernel(x, mem, g_pre, w_in, w_conv, attn_sink, g_mem, w_mem_kv, w_up_a, w_up_b, w_up_m, w_out, g_post, loss_target, m_g_pre, m_w_in, m_w_conv, m_attn_sink, m_g_mem, m_w_mem_kv, m_w_up_a, m_w_up_b, m_w_up_m, m_w_out, m_g_post, v_g_pre, v_w_in, v_w_conv, v_attn_sink, v_g_mem, v_w_mem_kv, v_w_up_a, v_w_up_b, v_w_up_m, v_w_out, v_g_post):
    s = x.shape[1]
    x2, mem2, tgt2 = x[0], mem[0], loss_target[0]
    me = 4 * lax.axis_index("x") + 2 * lax.axis_index("y") + lax.axis_index("c")

    w_conv_loc = jnp.zeros((8, 128), F32).at[:3, :64].set(w_conv[0])
    w_int_g, w_conv_g, tabs, w_mkv_loc, w_out_loc, w_up_loc = _all_gather(
        [w_in[0].T.astype(BF16), w_conv_loc], "gather_w_in",
        splits=[[(112 * k, 112) for k in range(7)] + [(784, 144)], [(0, 8)]],
        side=_gather_side(s, w_mem_kv[0], w_out[0], (w_up_a[0], w_up_b[0], w_up_m[0])))
    w_int = w_int_g.reshape(IN_WIDTH, D_MODEL)
    w_conv_f = w_conv_g[:, :3, :64].transpose(1, 0, 2).reshape(3, 512)
    late = _gather_start([w_mkv_loc, w_out_loc, w_up_loc], me, "gather_late_start")
    sink = attn_sink[0]

    h, pa, pq, pkv, pbz, pmq, pmz, pg = _proj_fwd(late[4], x2, g_pre, w_int, tabs)
    ya = _conv_fwd(pa, w_conv_f)
    yb = _attn_fwd(pq, pkv, pbz, sink)
    w_mkv_g, w_out_g, w_up_g = _gather_wait(*late[:4], yb, "gather_late_wait")
    w_mkv = w_mkv_g.reshape(D_MODEL, D_MODEL)
    w_out_f = w_out_g.reshape(D_MODEL, D_MODEL)
    mn, mkv = _mem_kv_fwd(mem2, g_mem, w_mkv)
    ym = _mem_attn_fwd(pmq, pmz, mkv)
    dg, dya, dyb, dym, dy, loss_p, gg_post, mb, dob, du = _mid(ya, yb, ym, pg, x2, tgt2, g_post, w_up_g, w_out_f)
    gw_out, gw_up = _gw_mid(mb, dob, (ya, yb, ym), du)

    core = lax.axis_index("c").astype(jnp.int32).reshape(1)
    chip = (2 * lax.axis_index("x") + lax.axis_index("y")).astype(jnp.int32).reshape(1)

    dmq, dmz, dmkv = _mem_attn_bwd(pmq, pmz, mkv, dym)
    gw_mkv, gg_mem = _mem_kv_bwd(mem2, g_mem, mn, dmkv, w_mkv)
    shares1 = [gw_mkv.reshape(N_DEV, 128, D_MODEL), gw_out.reshape(N_DEV, 128, D_MODEL), gw_up]
    sib = _split_start(_sibling_copies, N_CHIPS, shares1,
                       [lax.empty((N_CHIPS,) + a.shape[1:], a.dtype) for a in shares1], "grads_to_sibling_small_start")
    da, gw_conv = _conv_bwd(sib[4], pa, dya, w_conv_f)
    shares1, from_sibling = _split_wait(_sibling_copies, N_CHIPS, *sib[:4], da, "grads_to_sibling_small_wait")
    send1, recv1, srcs1, lands1, token1 = _chip_exchange_start(
        _pair_add(shares1, from_sibling, core, "grads_pair_add_small"), "grads_to_chips_start_small")
    dq, dbz, dkv, g_sink = _attn_bwd(token1, pq, pkv, pbz, dyb, sink, tabs)
    dparts = (da, dq, dkv, dbz, dmq, dmz, dg)
    gw_int = _gw_in(dparts, h)
    send2, recv2, srcs2, lands2, token2 = _chip_exchange_start(
        [_sibling_reduce(gw_int.reshape(N_DEV, SHARD_IN, D_MODEL), "grads_sibling_reduce_w_in")],
        "grads_to_chips_start_w_in")
    grad_x, gg_pre = _dh_bwd(token2, dparts, x2, dy, g_pre, w_int)
    (o_mkv, o_out, o_up, o_int), (l_mkv, l_out, l_up, l_int) = _chip_exchange_wait(
        send1 + send2, recv1 + recv2, srcs1 + srcs2, lands1 + lands2, grad_x, "grads_to_chips_wait")

    small = _gather_start([_small_pack((gg_pre, gw_conv, g_sink, gg_mem, gg_post, loss_p))], me,
                          "small_gather_start")

    w_in_t = _sum_adamw(small[4], o_int, l_int, chip, 0, w_in[0].T, m_w_in[0].T, v_w_in[0].T, "adamw_w_in", tiles=2)
    g_w_in, d_w_in, nm_w_in, nv_w_in = (t.T for t in w_in_t)
    (g_mkv, d_mkv, nm_mkv, nv_mkv), (g_out, d_out, nm_out, nv_out), *up = _sum_adamw_group(
        w_in_t[0],
        [(o_mkv, l_mkv, 0, w_mem_kv[0], m_w_mem_kv[0], v_w_mem_kv[0]),
         (o_out, l_out, 0, w_out[0], m_w_out[0], v_w_out[0]),
         (o_up, l_up, 0, w_up_a[0], m_w_up_a[0], v_w_up_a[0]),
         (o_up, l_up, 1, w_up_b[0], m_w_up_b[0], v_w_up_b[0]),
         (o_up, l_up, 2, w_up_m[0], m_w_up_m[0], v_w_up_m[0])], chip, "adamw_mid_weights")

    (packs,) = _gather_wait(*small[:4], g_out, "small_gather_wait")

    def taps_first(a):
        return a.transpose(1, 0, 2)

    loss_row, small_g, sd, sm, sv = _small_apply(
        packs, [g_pre, taps_first(w_conv), attn_sink, g_mem, g_post],
        [m_g_pre, taps_first(m_w_conv), m_attn_sink, m_g_mem, m_g_post],
        [v_g_pre, taps_first(v_w_conv), v_attn_sink, v_g_mem, v_g_post])
    loss = loss_row[0, 0]
    g_g_pre, g_conv, g_sink_tot, g_g_mem, g_g_post = small_g

    def lead(a):
        return a[None]

    grads = [g_g_pre, lead(g_w_in), taps_first(g_conv), g_sink_tot, g_g_mem, lead(g_mkv), lead(up[0][0]),
             lead(up[1][0]), lead(up[2][0]), lead(g_out), g_g_post]

    def assemble(small, big_in, big_mkv, big_up, big_out):
        return [small[0], lead(big_in), taps_first(small[1]), small[2], small[3], lead(big_mkv), lead(big_up[0]),
                lead(big_up[1]), lead(big_up[2]), lead(big_out), small[4]]

    deltas = assemble(sd, d_w_in, d_mkv, [u[1] for u in up], d_out)
    new_m = assemble(sm, nm_w_in, nm_mkv, [u[2] for u in up], nm_out)
    new_v = assemble(sv, nv_w_in, nv_mkv, [u[3] for u in up], nv_out)
    return (loss, grad_x[None], *grads, *deltas, *new_m, *new_v)
```
